```python
import jax, jax.numpy as jnp
from jax import lax
import numpy as np

D_MODEL = 2048
BATCH = 8
SEQ = 2048
DEPTH = 1

HEAD_DIM = 64
N_Q_HEADS = (D_MODEL // 2) // HEAD_DIM
N_KV_HEADS = N_Q_HEADS // 4
GROUP = N_Q_HEADS // N_KV_HEADS
ATTN_WIDTH = N_Q_HEADS * HEAD_DIM
KV_WIDTH = N_KV_HEADS * HEAD_DIM
CONV_WIDTH = D_MODEL // 2
CONV_K = 3
WINDOW = 128
BLOCK = 128
ROPE_THETA = 500000.0
ROT_DIM = HEAD_DIM // 4
D_FF = ((8 * D_MODEL // 3 + 255) // 256) * 256
RMS_EPS = 1e-6
ATTN_SCALE = HEAD_DIM ** -0.5
NEG_INF = -1e30
IN_WIDTHS = (CONV_WIDTH, CONV_WIDTH, CONV_WIDTH, ATTN_WIDTH, KV_WIDTH, KV_WIDTH, D_MODEL, D_MODEL)
IN_SPLITS = tuple(int(s) for s in np.cumsum(IN_WIDTHS)[:-1])
IN_TOTAL = int(sum(IN_WIDTHS))

kernel_name = "hybrid_macaron_conv_swa_gated"


def rms_norm(x, g):
    xf = x.astype(jnp.float32)
    y = xf * lax.rsqrt(jnp.mean(xf * xf, axis=-1, keepdims=True) + RMS_EPS)
    return (y * g.astype(jnp.float32)).astype(x.dtype)


def swiglu(h, w_gu, w_down):
    g, u = jnp.split(h @ w_gu, 2, axis=-1)
    return (jax.nn.silu(g) * u) @ w_down


def rope_tables(seq_len):
    inv_freq = 1.0 / (ROPE_THETA ** (jnp.arange(0, ROT_DIM, 2, dtype=jnp.float32) / ROT_DIM))
    ang = jnp.arange(seq_len, dtype=jnp.float32)[:, None] * inv_freq[None, :]
    return jnp.cos(ang)[None, :, None, :], jnp.sin(ang)[None, :, None, :]


def partial_rope(x, cos, sin):
    half = ROT_DIM // 2
    xf = x.astype(jnp.float32)
    x1, x2, xp = xf[..., :half], xf[..., half:ROT_DIM], xf[..., ROT_DIM:]
    out = jnp.concatenate([x1 * cos - x2 * sin, x2 * cos + x1 * sin, xp], axis=-1)
    return out.astype(x.dtype)


def causal_short_conv(u, w):
    S = u.shape[1]
    up = jnp.pad(u, ((0, 0), (CONV_K - 1, 0), (0, 0)))
    y = up[:, 0:S] * w[0]
    for j in range(1, CONV_K):
        y = y + up[:, j:j + S] * w[j]
    return y


def banded(t, nb):
    B = t.shape[0]
    tp = jnp.pad(t, ((0, 0), (BLOCK, 0), (0, 0), (0, 0)))
    tb = tp.reshape(B, nb + 1, BLOCK, t.shape[2], t.shape[3])
    return jnp.concatenate([tb[:, :-1], tb[:, 1:]], axis=2)


def sliding_window_gqa_sinks(q, k, v, sinks):
    B, S = q.shape[0], q.shape[1]
    nb = S // BLOCK
    qb = q.reshape(B, nb, BLOCK, N_KV_HEADS, GROUP, HEAD_DIM)
    kb, vb = banded(k, nb), banded(v, nb)
    s = jnp.einsum('bnqhgd,bnkhd->bnhgqk', qb, kb).astype(jnp.float32) * ATTN_SCALE
    qi = jnp.arange(BLOCK)[:, None] + BLOCK
    ki = jnp.arange(2 * BLOCK)[None, :]
    diff = qi - ki
    in_window = (diff >= 0) & (diff < WINDOW)
    key_pos = jnp.arange(nb)[:, None] * BLOCK + jnp.arange(2 * BLOCK)[None, :] - BLOCK
    valid = in_window[None] & (key_pos >= 0)[:, None, :]
    s = jnp.where(valid[None, :, None, None], s, NEG_INF)
    sink = sinks.astype(jnp.float32).reshape(1, 1, N_KV_HEADS, GROUP, 1, 1)
    m = jnp.maximum(jnp.max(s, axis=-1, keepdims=True), sink)
    p = jnp.exp(s - m)
    denom = jnp.sum(p, axis=-1, keepdims=True) + jnp.exp(sink - m)
    o = jnp.einsum('bnhgqk,bnkhd->bnqhgd', p / denom, vb.astype(jnp.float32))
    return o.reshape(B, S, ATTN_WIDTH).astype(q.dtype)


def hybrid_mixer(h, w_in, conv_w, q_norm_g, k_norm_g, sinks, w_out_conv, w_out_attn, w_o, cos, sin):
    B, S, _ = h.shape
    xc, bg, cg, q, k, v, ga, gb = jnp.split(h @ w_in, IN_SPLITS, axis=-1)
    ya = (bg * causal_short_conv(cg * xc, conv_w)) @ w_out_conv
    q = rms_norm(q.reshape(B, S, N_Q_HEADS, HEAD_DIM), q_norm_g)
    k = rms_norm(k.reshape(B, S, N_KV_HEADS, HEAD_DIM), k_norm_g)
    v = v.reshape(B, S, N_KV_HEADS, HEAD_DIM)
    q = partial_rope(q, cos, sin)
    k = partial_rope(k, cos, sin)
    yb = sliding_window_gqa_sinks(q, k, v, sinks) @ w_out_attn
    merged = jax.nn.sigmoid(ga) * ya + jax.nn.sigmoid(gb) * yb
    return merged @ w_o


def _fwd_setup_inputs(seed: int = 0) -> dict:
    key = jax.random.key(seed)
    ks = jax.random.split(key, 18)
    f32 = jnp.float32

    def w(k, shape, fan_in):
        return jax.random.normal(k, shape, f32) * (fan_in ** -0.5)

    def gain(k, n):
        return 1.0 + 0.02 * jax.random.normal(k, (DEPTH, n), f32)

    return {
        "x": jax.random.normal(ks[0], (BATCH, SEQ, D_MODEL), f32),
        "g_ffn1": gain(ks[1], D_MODEL),
        "w_gu1": w(ks[2], (DEPTH, D_MODEL, 2 * D_FF), D_MODEL),
        "w_down1": w(ks[3], (DEPTH, D_FF, D_MODEL), D_FF),
        "g_mix": gain(ks[4], D_MODEL),
        "w_in": w(ks[5], (DEPTH, D_MODEL, IN_TOTAL), D_MODEL),
        "conv_w": w(ks[6], (DEPTH, CONV_K, CONV_WIDTH), CONV_K),
        "q_norm_g": gain(ks[7], HEAD_DIM),
        "k_norm_g": gain(ks[8], HEAD_DIM),
        "sinks": 0.5 * jax.random.normal(ks[9], (DEPTH, N_Q_HEADS), f32),
        "w_out_conv": w(ks[10], (DEPTH, CONV_WIDTH, D_MODEL), CONV_WIDTH),
        "w_out_attn": w(ks[11], (DEPTH, ATTN_WIDTH, D_MODEL), ATTN_WIDTH),
        "w_o": w(ks[12], (DEPTH, D_MODEL, D_MODEL), D_MODEL),
        "g_ffn2": gain(ks[13], D_MODEL),
        "w_gu2": w(ks[14], (DEPTH, D_MODEL, 2 * D_FF), D_MODEL),
        "w_down2": w(ks[15], (DEPTH, D_FF, D_MODEL), D_FF),
    }


def _fwd_reference(x, g_ffn1, w_gu1, w_down1, g_mix, w_in, conv_w, q_norm_g, k_norm_g, sinks,
              w_out_conv, w_out_attn, w_o, g_ffn2, w_gu2, w_down2):
    cos, sin = rope_tables(x.shape[1])
    for l in range(DEPTH):
        x = x + 0.5 * swiglu(rms_norm(x, g_ffn1[l]), w_gu1[l], w_down1[l])
        x = x + hybrid_mixer(rms_norm(x, g_mix[l]), w_in[l], conv_w[l], q_norm_g[l], k_norm_g[l],
                             sinks[l], w_out_conv[l], w_out_attn[l], w_o[l], cos, sin)
        x = x + 0.5 * swiglu(rms_norm(x, g_ffn2[l]), w_gu2[l], w_down2[l])
    return x


import jax as _jax
import jax.numpy as _jnp

TWIN_FORMAT = 'train_step'
FWD_PARAMS = ['x', 'g_ffn1', 'w_gu1', 'w_down1', 'g_mix', 'w_in', 'conv_w', 'q_norm_g', 'k_norm_g', 'sinks', 'w_out_conv', 'w_out_attn', 'w_o', 'g_ffn2', 'w_gu2', 'w_down2']
TWIN_WEIGHTS = ['g_ffn1', 'w_gu1', 'w_down1', 'g_mix', 'w_in', 'conv_w', 'q_norm_g', 'k_norm_g', 'sinks', 'w_out_conv', 'w_out_attn', 'w_o', 'g_ffn2', 'w_gu2', 'w_down2']
TWIN_DIFF_INPUT = 'x'
TWIN_INPUTS = ['x', 'g_ffn1', 'w_gu1', 'w_down1', 'g_mix', 'w_in', 'conv_w', 'q_norm_g', 'k_norm_g', 'sinks', 'w_out_conv', 'w_out_attn', 'w_o', 'g_ffn2', 'w_gu2', 'w_down2', 'loss_target', 'm_g_ffn1', 'm_w_gu1', 'm_w_down1', 'm_g_mix', 'm_w_in', 'm_conv_w', 'm_q_norm_g', 'm_k_norm_g', 'm_sinks', 'm_w_out_conv', 'm_w_out_attn', 'm_w_o', 'm_g_ffn2', 'm_w_gu2', 'm_w_down2', 'v_g_ffn1', 'v_w_gu1', 'v_w_down1', 'v_g_mix', 'v_w_in', 'v_conv_w', 'v_q_norm_g', 'v_k_norm_g', 'v_sinks', 'v_w_out_conv', 'v_w_out_attn', 'v_w_o', 'v_g_ffn2', 'v_w_gu2', 'v_w_down2']
TWIN_OUTPUTS = ['loss', 'grad_x', 'grad_g_ffn1', 'grad_w_gu1', 'grad_w_down1', 'grad_g_mix', 'grad_w_in', 'grad_conv_w', 'grad_q_norm_g', 'grad_k_norm_g', 'grad_sinks', 'grad_w_out_conv', 'grad_w_out_attn', 'grad_w_o', 'grad_g_ffn2', 'grad_w_gu2', 'grad_w_down2', 'delta_g_ffn1', 'delta_w_gu1', 'delta_w_down1', 'delta_g_mix', 'delta_w_in', 'delta_conv_w', 'delta_q_norm_g', 'delta_k_norm_g', 'delta_sinks', 'delta_w_out_conv', 'delta_w_out_attn', 'delta_w_o', 'delta_g_ffn2', 'delta_w_gu2', 'delta_w_down2', 'new_m_g_ffn1', 'new_m_w_gu1', 'new_m_w_down1', 'new_m_g_mix', 'new_m_w_in', 'new_m_conv_w', 'new_m_q_norm_g', 'new_m_k_norm_g', 'new_m_sinks', 'new_m_w_out_conv', 'new_m_w_out_attn', 'new_m_w_o', 'new_m_g_ffn2', 'new_m_w_gu2', 'new_m_w_down2', 'new_v_g_ffn1', 'new_v_w_gu1', 'new_v_w_down1', 'new_v_g_mix', 'new_v_w_in', 'new_v_conv_w', 'new_v_q_norm_g', 'new_v_k_norm_g', 'new_v_sinks', 'new_v_w_out_conv', 'new_v_w_out_attn', 'new_v_w_o', 'new_v_g_ffn2', 'new_v_w_gu2', 'new_v_w_down2']
TWIN_LEAF_KINDS = {'loss': 'loss', 'grad_x': 'grad_x', 'grad_g_ffn1': 'grad_w', 'grad_w_gu1': 'grad_w', 'grad_w_down1': 'grad_w', 'grad_g_mix': 'grad_w', 'grad_w_in': 'grad_w', 'grad_conv_w': 'grad_w', 'grad_q_norm_g': 'grad_w', 'grad_k_norm_g': 'grad_w', 'grad_sinks': 'grad_w', 'grad_w_out_conv': 'grad_w', 'grad_w_out_attn': 'grad_w', 'grad_w_o': 'grad_w', 'grad_g_ffn2': 'grad_w', 'grad_w_gu2': 'grad_w', 'grad_w_down2': 'grad_w', 'delta_g_ffn1': 'delta_w', 'delta_w_gu1': 'delta_w', 'delta_w_down1': 'delta_w', 'delta_g_mix': 'delta_w', 'delta_w_in': 'delta_w', 'delta_conv_w': 'delta_w', 'delta_q_norm_g': 'delta_w', 'delta_k_norm_g': 'delta_w', 'delta_sinks': 'delta_w', 'delta_w_out_conv': 'delta_w', 'delta_w_out_attn': 'delta_w', 'delta_w_o': 'delta_w', 'delta_g_ffn2': 'delta_w', 'delta_w_gu2': 'delta_w', 'delta_w_down2': 'delta_w', 'new_m_g_ffn1': 'new_m', 'new_m_w_gu1': 'new_m', 'new_m_w_down1': 'new_m', 'new_m_g_mix': 'new_m', 'new_m_w_in': 'new_m', 'new_m_conv_w': 'new_m', 'new_m_q_norm_g': 'new_m', 'new_m_k_norm_g': 'new_m', 'new_m_sinks': 'new_m', 'new_m_w_out_conv': 'new_m', 'new_m_w_out_attn': 'new_m', 'new_m_w_o': 'new_m', 'new_m_g_ffn2': 'new_m', 'new_m_w_gu2': 'new_m', 'new_m_w_down2': 'new_m', 'new_v_g_ffn1': 'new_v', 'new_v_w_gu1': 'new_v', 'new_v_w_down1': 'new_v', 'new_v_g_mix': 'new_v', 'new_v_w_in': 'new_v', 'new_v_conv_w': 'new_v', 'new_v_q_norm_g': 'new_v', 'new_v_k_norm_g': 'new_v', 'new_v_sinks': 'new_v', 'new_v_w_out_conv': 'new_v', 'new_v_w_out_attn': 'new_v', 'new_v_w_o': 'new_v', 'new_v_g_ffn2': 'new_v', 'new_v_w_gu2': 'new_v', 'new_v_w_down2': 'new_v'}


def _forward(args):
    return _fwd_reference(*[args[k] for k in FWD_PARAMS])


def _output_shape():
    out = _jax.eval_shape(lambda: _forward(_fwd_setup_inputs(0)))
    return out.shape, out.dtype

N_MICROBATCH = 1
ADAM_LR = 0.001
ADAM_B1 = 0.9
ADAM_B2 = 0.999
ADAM_EPS = 1e-08
ADAM_WD = 0.01
ADAM_STEP = 10
PER_EXAMPLE_BATCH_AXIS = {'x': 0, 'loss_target': 0}
SHARED_INPUTS = []
_WEIGHT_DTYPES = {'g_ffn1': _jnp.float32, 'w_gu1': _jnp.float32, 'w_down1': _jnp.float32, 'g_mix': _jnp.float32, 'w_in': _jnp.float32, 'conv_w': _jnp.float32, 'q_norm_g': _jnp.float32, 'k_norm_g': _jnp.float32, 'sinks': _jnp.float32, 'w_out_conv': _jnp.float32, 'w_out_attn': _jnp.float32, 'w_o': _jnp.float32, 'g_ffn2': _jnp.float32, 'w_gu2': _jnp.float32, 'w_down2': _jnp.float32}
MOMENT_SCALE = {'g_ffn1': 1.520576e+00, 'w_gu1': 3.943362e-02, 'w_down1': 6.487249e-02, 'g_mix': 7.239171e+00, 'w_in': 1.052690e-01, 'conv_w': 2.672583e+00, 'q_norm_g': 1.006711e+00, 'k_norm_g': 1.007793e+00, 'sinks': 1.556668e-01, 'w_out_conv': 1.090339e-01, 'w_out_attn': 1.289611e-02, 'w_o': 8.904032e-02, 'g_ffn2': 1.538359e+00, 'w_gu2': 2.051380e-02, 'w_down2': 3.391442e-02}


def _to_microbatches(a, axis):
    t = _jnp.moveaxis(a, axis, 0)
    t = t.reshape((N_MICROBATCH, t.shape[0] // N_MICROBATCH) + t.shape[1:])
    return _jnp.moveaxis(t, 1, axis + 1)


def setup_inputs(seed: int = 0) -> dict:
    inp = _fwd_setup_inputs(seed)
    key = _jax.random.fold_in(_jax.random.key(seed), 7919)
    shape, _ = _output_shape()
    out = dict(inp)
    out["loss_target"] = _jax.random.normal(_jax.random.fold_in(key, 0), shape, _jnp.float32)
    for i, name in enumerate(TWIN_WEIGHTS):
        w = inp[name].astype(_jnp.float32)
        if MOMENT_SCALE is None:
            s = _jnp.sqrt(_jnp.mean(_jnp.square(w)) + 1e-30)
        else:
            s = MOMENT_SCALE[name]
        km, kv = _jax.random.split(_jax.random.fold_in(key, i + 1))
        out[name] = w
        out["m_" + name] = s * _jax.random.normal(km, w.shape, _jnp.float32)
        out["v_" + name] = (s * s) * _jax.random.uniform(kv, w.shape, _jnp.float32, 0.5, 1.5)
    if N_MICROBATCH > 1:
        for name, axis in PER_EXAMPLE_BATCH_AXIS.items():
            out[name] = _to_microbatches(out[name], axis)
    return {'x': out['x'], 'g_ffn1': out['g_ffn1'], 'w_gu1': out['w_gu1'], 'w_down1': out['w_down1'], 'g_mix': out['g_mix'], 'w_in': out['w_in'], 'conv_w': out['conv_w'], 'q_norm_g': out['q_norm_g'], 'k_norm_g': out['k_norm_g'], 'sinks': out['sinks'], 'w_out_conv': out['w_out_conv'], 'w_out_attn': out['w_out_attn'], 'w_o': out['w_o'], 'g_ffn2': out['g_ffn2'], 'w_gu2': out['w_gu2'], 'w_down2': out['w_down2'], 'loss_target': out['loss_target'], 'm_g_ffn1': out['m_g_ffn1'], 'm_w_gu1': out['m_w_gu1'], 'm_w_down1': out['m_w_down1'], 'm_g_mix': out['m_g_mix'], 'm_w_in': out['m_w_in'], 'm_conv_w': out['m_conv_w'], 'm_q_norm_g': out['m_q_norm_g'], 'm_k_norm_g': out['m_k_norm_g'], 'm_sinks': out['m_sinks'], 'm_w_out_conv': out['m_w_out_conv'], 'm_w_out_attn': out['m_w_out_attn'], 'm_w_o': out['m_w_o'], 'm_g_ffn2': out['m_g_ffn2'], 'm_w_gu2': out['m_w_gu2'], 'm_w_down2': out['m_w_down2'], 'v_g_ffn1': out['v_g_ffn1'], 'v_w_gu1': out['v_w_gu1'], 'v_w_down1': out['v_w_down1'], 'v_g_mix': out['v_g_mix'], 'v_w_in': out['v_w_in'], 'v_conv_w': out['v_conv_w'], 'v_q_norm_g': out['v_q_norm_g'], 'v_k_norm_g': out['v_k_norm_g'], 'v_sinks': out['v_sinks'], 'v_w_out_conv': out['v_w_out_conv'], 'v_w_out_attn': out['v_w_out_attn'], 'v_w_o': out['v_w_o'], 'v_g_ffn2': out['v_g_ffn2'], 'v_w_gu2': out['v_w_gu2'], 'v_w_down2': out['v_w_down2']}


def _loss(weights, diff, rest, loss_target):
    with _jax.named_scope("forward"):
        args = {**rest, TWIN_DIFF_INPUT: diff, **{k: w.astype(_WEIGHT_DTYPES[k]) for k, w in weights.items()}}
        y = _forward(args)
    with _jax.named_scope("loss_head"):
        err = _jnp.square(y.astype(_jnp.float32) - loss_target)
        return 0.5 * _jnp.sum(_jnp.mean(err, axis=-1)) if err.ndim else 0.5 * err


def _adamw(w, g, m, v):
    m = ADAM_B1 * m + (1.0 - ADAM_B1) * g
    v = ADAM_B2 * v + (1.0 - ADAM_B2) * _jnp.square(g)
    m_hat = m / (1.0 - ADAM_B1 ** ADAM_STEP)
    v_hat = v / (1.0 - ADAM_B2 ** ADAM_STEP)
    delta = -ADAM_LR * (m_hat / (_jnp.sqrt(v_hat) + ADAM_EPS) + ADAM_WD * w)
    return delta, m, v


def reference(x, g_ffn1, w_gu1, w_down1, g_mix, w_in, conv_w, q_norm_g, k_norm_g, sinks, w_out_conv, w_out_attn, w_o, g_ffn2, w_gu2, w_down2, loss_target, m_g_ffn1, m_w_gu1, m_w_down1, m_g_mix, m_w_in, m_conv_w, m_q_norm_g, m_k_norm_g, m_sinks, m_w_out_conv, m_w_out_attn, m_w_o, m_g_ffn2, m_w_gu2, m_w_down2, v_g_ffn1, v_w_gu1, v_w_down1, v_g_mix, v_w_in, v_conv_w, v_q_norm_g, v_k_norm_g, v_sinks, v_w_out_conv, v_w_out_attn, v_w_o, v_g_ffn2, v_w_gu2, v_w_down2):
    given = dict(x=x, g_ffn1=g_ffn1, w_gu1=w_gu1, w_down1=w_down1, g_mix=g_mix, w_in=w_in, conv_w=conv_w, q_norm_g=q_norm_g, k_norm_g=k_norm_g, sinks=sinks, w_out_conv=w_out_conv, w_out_attn=w_out_attn, w_o=w_o, g_ffn2=g_ffn2, w_gu2=w_gu2, w_down2=w_down2, loss_target=loss_target, m_g_ffn1=m_g_ffn1, m_w_gu1=m_w_gu1, m_w_down1=m_w_down1, m_g_mix=m_g_mix, m_w_in=m_w_in, m_conv_w=m_conv_w, m_q_norm_g=m_q_norm_g, m_k_norm_g=m_k_norm_g, m_sinks=m_sinks, m_w_out_conv=m_w_out_conv, m_w_out_attn=m_w_out_attn, m_w_o=m_w_o, m_g_ffn2=m_g_ffn2, m_w_gu2=m_w_gu2, m_w_down2=m_w_down2, v_g_ffn1=v_g_ffn1, v_w_gu1=v_w_gu1, v_w_down1=v_w_down1, v_g_mix=v_g_mix, v_w_in=v_w_in, v_conv_w=v_conv_w, v_q_norm_g=v_q_norm_g, v_k_norm_g=v_k_norm_g, v_sinks=v_sinks, v_w_out_conv=v_w_out_conv, v_w_out_attn=v_w_out_attn, v_w_o=v_w_o, v_g_ffn2=v_g_ffn2, v_w_gu2=v_w_gu2, v_w_down2=v_w_down2)
    weights = {n: given[n] for n in TWIN_WEIGHTS}
    shared = {n: given[n] for n in SHARED_INPUTS}
    per_example = {n: given[n] for n in ['x']}
    grad_fn = _jax.value_and_grad(_loss, argnums=(0, 1))

    def one_microbatch(ex, loss_target):
        ex = dict(ex)
        diff = ex.pop(TWIN_DIFF_INPUT)
        return grad_fn(weights, diff, {**shared, **ex}, loss_target)

    if N_MICROBATCH == 1:
        loss, (grad_w, grad_x) = one_microbatch(per_example, given["loss_target"])
    else:
        def body(carry, xs):
            loss_sum, grad_sum = carry
            l_k, (gw_k, gx_k) = one_microbatch(xs[0], xs[1])
            with _jax.named_scope("update"):
                return (loss_sum + l_k, _jax.tree.map(_jnp.add, grad_sum, gw_k)), gx_k

        init = (_jnp.zeros((), _jnp.float32), _jax.tree.map(_jnp.zeros_like, weights))
        (loss, grad_w), grad_x = _jax.lax.scan(body, init, (per_example, given["loss_target"]))
    with _jax.named_scope("update"):
        delta_w, new_m, new_v = {}, {}, {}
        for n in TWIN_WEIGHTS:
            delta_w[n], new_m[n], new_v[n] = _adamw(weights[n], grad_w[n], given["m_" + n], given["v_" + n])
    return (loss, grad_x, *[grad_w[n] for n in TWIN_WEIGHTS], *[delta_w[n] for n in TWIN_WEIGHTS],
            *[new_m[n] for n in TWIN_WEIGHTS], *[new_v[n] for n in TWIN_WEIGHTS])
```

```python
import functools

import jax
import jax.numpy as jnp
from jax import lax
from jax.experimental import pallas as pl
from jax.experimental.pallas import tpu as pltpu

F32 = jnp.float32
BF16 = jnp.bfloat16

N_DEV = 8
HEAD_DIM = 64
GROUP = 4
BLOCK = 128
ROT_DIM = 16
ROPE_THETA = 500000.0
RMS_EPS = 1e-6
NEG_INF = -1e30
ATTN_SCALE = HEAD_DIM ** -0.5
CONV_K = 3
LANES = 128
VMEM_BYTES_V7X = 64 * 1024 * 1024
VMEM_CAP = VMEM_BYTES_V7X - 6 * 1024 * 1024

ADAM_LR = 0.001
ADAM_B1 = 0.9
ADAM_B2 = 0.999
ADAM_EPS = 1e-08
ADAM_WD = 0.01
ADAM_STEP = 10

NN = (((1,), (0,)), ((), ()))
NT = (((1,), (1,)), ((), ()))
TN = (((0,), (0,)), ((), ()))

MESH = pl.DeviceIdType.MESH


def _nbytes(shape, dtype):
    n = 1
    for s in shape:
        if s is not None:
            n *= s
    return n * jnp.dtype(dtype).itemsize


def _params(semantics, block_bytes, temp_bytes):
    est = 2 * block_bytes + temp_bytes + (4 << 20)
    return pltpu.CompilerParams(dimension_semantics=semantics, vmem_limit_bytes=int(min(max(est, 16 << 20), VMEM_CAP)))


def _fused(name, grid, ins, outs, dots, epilogue, *, nk=1, acc_shape=None, temp_bytes=0,
           semantics=("parallel", "parallel", "arbitrary")):
    n_in, n_out = len(ins), len(outs)

    def body(*refs):
        in_refs, out_refs, scratch = refs[:n_in], refs[n_in:n_in + n_out], refs[n_in + n_out:]

        def products():
            total = None
            for ai, bi, contract in dots:
                a, b = in_refs[ai][...], in_refs[bi][...]
                a = a if a.dtype == BF16 else a.astype(BF16)
                b = b if b.dtype == BF16 else b.astype(BF16)
                p = lax.dot_general(a, b, contract, preferred_element_type=F32)
                total = p if total is None else total + p
            return total

        if nk == 1:
            epilogue(products() if dots else None, in_refs, out_refs)
        else:
            acc = scratch[0]
            k = pl.program_id(2)

            @pl.when(k == 0)
            def _():
                acc[...] = jnp.zeros_like(acc)

            acc[...] += products()

            @pl.when(k == nk - 1)
            def _():
                epilogue(acc[...], in_refs, out_refs)

    block_bytes = sum(_nbytes(spec.block_shape, a.dtype) for a, spec in ins)
    block_bytes += sum(_nbytes(spec.block_shape, s.dtype) for s, spec in outs)
    scratch_shapes = []
    if nk > 1:
        scratch_shapes.append(pltpu.VMEM(acc_shape, F32))
        temp_bytes += _nbytes(acc_shape, F32)
    res = pl.pallas_call(
        body, name=name, grid=grid,
        in_specs=[spec for _, spec in ins],
        out_specs=[spec for _, spec in outs],
        out_shape=[s for s, _ in outs],
        scratch_shapes=scratch_shapes,
        compiler_params=_params(semantics, block_bytes, temp_bytes),
    )(*[a for a, _ in ins])
    return res


def _sds(shape, dtype):
    return jax.ShapeDtypeStruct(shape, dtype)


def _sigmoid(x):
    return jax.nn.sigmoid(x)


def _exchange(name, arrays, gather):
    n = len(arrays)
    out_shapes = [((N_DEV,) + a.shape) if gather else a.shape for a in arrays]

    def body(*refs):
        srcs, dsts = refs[:n], refs[n:2 * n]
        send_sems, recv_sems, local_sems = refs[2 * n:]
        x, y, c = lax.axis_index("x"), lax.axis_index("y"), lax.axis_index("c")
        me = 4 * x + 2 * y + c
        copies = []
        for w in range(n):
            own = srcs[w] if gather else srcs[w].at[me]
            local = pltpu.make_async_copy(own, dsts[w].at[me], local_sems.at[w])
            local.start()
            copies.append(local)
            for k in range(1, N_DEV):
                px = (1 - x) if (k & 4) else x
                py = (1 - y) if (k & 2) else y
                pc = (1 - c) if (k & 1) else c
                peer = 4 * px + 2 * py + pc
                cp = pltpu.make_async_remote_copy(
                    src_ref=srcs[w] if gather else srcs[w].at[peer],
                    dst_ref=dsts[w].at[me],
                    send_sem=send_sems.at[w * (N_DEV - 1) + k - 1],
                    recv_sem=recv_sems.at[w * (N_DEV - 1) + k - 1],
                    device_id=(px, py, pc), device_id_type=MESH)
                cp.start()
                copies.append(cp)
        for cp in copies:
            cp.wait()

    hbm = pl.BlockSpec(memory_space=pltpu.HBM)
    return pl.pallas_call(
        body, name=name,
        in_specs=[hbm] * n, out_specs=[hbm] * n,
        out_shape=[_sds(s, a.dtype) for s, a in zip(out_shapes, arrays)],
        scratch_shapes=[pltpu.SemaphoreType.DMA((n * (N_DEV - 1),)),
                        pltpu.SemaphoreType.DMA((n * (N_DEV - 1),)),
                        pltpu.SemaphoreType.DMA((n,))],
    )(*arrays)


def _row_tile(t):
    return min(t, 256)


def _rms_fwd(name, x, g):
    t, d = x.shape
    tm = _row_tile(t)

    def epilogue(_, ins, outs):
        xv = ins[0][...]
        r = lax.rsqrt(jnp.mean(xv * xv, axis=-1, keepdims=True) + RMS_EPS)
        outs[0][...] = (xv * r * ins[1][...]).astype(BF16)

    row = pl.BlockSpec((tm, d), lambda i, j, k: (i, 0))
    vec = pl.BlockSpec((1, d), lambda i, j, k: (0, 0))
    return _fused(name, (t // tm, 1, 1), [(x, row), (g, vec)], [(_sds((t, d), BF16), row)], [], epilogue,
                  temp_bytes=4 * tm * d * 4)[0]


def _rms_bwd(name, x, g, dh, resid):
    t, d = x.shape
    tm = _row_tile(t)

    def epilogue(_, ins, outs):
        xv, gv, dhv = ins[0][...], ins[1][...], ins[2][...]
        r = lax.rsqrt(jnp.mean(xv * xv, axis=-1, keepdims=True) + RMS_EPS)
        xh = xv * r
        u = dhv * gv
        dot = jnp.mean(u * xh, axis=-1, keepdims=True)
        outs[0][...] = ins[3][...] + r * (u - xh * dot)

        @pl.when(pl.program_id(0) == 0)
        def _():
            outs[1][...] = jnp.zeros_like(outs[1])

        outs[1][0:1, :] += jnp.sum(dhv * xh, axis=0, keepdims=True)

    row = pl.BlockSpec((tm, d), lambda i, j, k: (i, 0))
    vec = pl.BlockSpec((1, d), lambda i, j, k: (0, 0))
    acc = pl.BlockSpec((8, d), lambda i, j, k: (0, 0))
    return _fused(name, (t // tm, 1, 1), [(x, row), (g, vec), (dh, row), (resid, row)],
                  [(_sds((t, d), F32), row), (_sds((8, d), F32), acc)], [], epilogue,
                  temp_bytes=6 * tm * d * 4, semantics=("arbitrary", "arbitrary", "arbitrary"))


def _loss_dy(y, target):
    t, d = y.shape
    tm = _row_tile(t)

    def epilogue(_, ins, outs):
        e = ins[0][...] - ins[1][...]
        outs[0][...] = e * (1.0 / d)

        @pl.when(pl.program_id(0) == 0)
        def _():
            outs[1][...] = jnp.zeros_like(outs[1])

        part = jnp.sum(jnp.sum(e * e, axis=1, keepdims=True), axis=0, keepdims=True)
        outs[1][...] += jnp.broadcast_to(part, outs[1].shape)

    row = pl.BlockSpec((tm, d), lambda i, j, k: (i, 0))
    acc = pl.BlockSpec((8, LANES), lambda i, j, k: (0, 0))
    return _fused("loss_dy", (t // tm, 1, 1), [(y, row), (target, row)],
                  [(_sds((t, d), F32), row), (_sds((8, LANES), F32), acc)], [], epilogue,
                  temp_bytes=3 * tm * d * 4, semantics=("arbitrary", "arbitrary", "arbitrary"))


def _ffn_up(name, h, wgu):
    t, d = h.shape
    nb = wgu.shape[2]
    f = 4 * nb
    tm = _row_tile(t)

    def body(h_ref, wg_ref, wu_ref, gu_ref, a_ref):
        hv = h_ref[...]
        g = jnp.dot(hv, wg_ref[...], preferred_element_type=F32)
        u = jnp.dot(hv, wu_ref[...], preferred_element_type=F32)
        gu_ref[0] = g.astype(BF16)
        gu_ref[1] = u.astype(BF16)
        a_ref[...] = (g * _sigmoid(g) * u).astype(BF16)

    blocks = tm * d * 2 + 2 * d * nb * 2 + 3 * tm * nb * 2
    return pl.pallas_call(
        body, name=name, grid=(4, t // tm),
        in_specs=[pl.BlockSpec((tm, d), lambda j, i: (i, 0)),
                  pl.BlockSpec((None, d, nb), lambda j, i: (j, 0, 0)),
                  pl.BlockSpec((None, d, nb), lambda j, i: (j + 4, 0, 0))],
        out_specs=[pl.BlockSpec((2, tm, nb), lambda j, i: (0, i, j)),
                   pl.BlockSpec((tm, nb), lambda j, i: (i, j))],
        out_shape=[_sds((2, t, f), BF16), _sds((t, f), BF16)],
        compiler_params=_params(("parallel", "parallel"), blocks, 5 * tm * nb * 4),
    )(h, wgu, wgu)


def _ffn_down(name, a, wd, x):
    t, f = a.shape
    d = wd.shape[1]
    tm = min(t, 512)
    tk = f // 4

    def epilogue(acc, ins, outs):
        outs[0][...] = ins[2][...] + 0.5 * acc

    return _fused(name, (t // tm, 1, 4),
                  [(a, pl.BlockSpec((tm, tk), lambda i, j, k: (i, k))),
                   (wd, pl.BlockSpec((tk, d), lambda i, j, k: (k, 0))),
                   (x, pl.BlockSpec((tm, d), lambda i, j, k: (i, 0)))],
                  [(_sds((t, d), F32), pl.BlockSpec((tm, d), lambda i, j, k: (i, 0)))],
                  [(0, 1, NN)], epilogue, nk=4, acc_shape=(tm, d), temp_bytes=2 * tm * d * 4)[0]


def _ffn_bwd_act(name, dy, wd, gu):
    t, d = dy.shape
    f = wd.shape[0]
    nb = f // 4
    tm = _row_tile(t)

    def body(dy_ref, wd_ref, gu_ref, dgu_ref):
        da = 0.5 * lax.dot_general(dy_ref[...].astype(BF16), wd_ref[...], NT, preferred_element_type=F32)
        g = gu_ref[0].astype(F32)
        u = gu_ref[1].astype(F32)
        s = _sigmoid(g)
        dgu_ref[0] = (da * u * (s * (1.0 + g * (1.0 - s)))).astype(BF16)
        dgu_ref[1] = (da * (g * s)).astype(BF16)

    blocks = tm * d * 4 + nb * d * 2 + 4 * tm * nb * 2
    return pl.pallas_call(
        body, name=name, grid=(4, t // tm),
        in_specs=[pl.BlockSpec((tm, d), lambda j, i: (i, 0)),
                  pl.BlockSpec((nb, d), lambda j, i: (j, 0)),
                  pl.BlockSpec((2, tm, nb), lambda j, i: (0, i, j))],
        out_specs=pl.BlockSpec((2, tm, nb), lambda j, i: (0, i, j)),
        out_shape=_sds((2, t, f), BF16),
        compiler_params=_params(("parallel", "parallel"), blocks, 6 * tm * nb * 4),
    )(dy, wd, gu)


def _ffn_bwd_dwd(name, a, dy):
    t, f = a.shape
    d = dy.shape[1]
    tm = f // 4
    tk = min(t, 512)

    def epilogue(acc, ins, outs):
        outs[0][...] = (0.5 * acc).astype(BF16)

    return _fused(name, (4, 1, t // tk),
                  [(a, pl.BlockSpec((tk, tm), lambda i, j, k: (k, i))),
                   (dy, pl.BlockSpec((tk, d), lambda i, j, k: (k, 0)))],
                  [(_sds((f, d), BF16), pl.BlockSpec((tm, d), lambda i, j, k: (i, 0)))],
                  [(0, 1, TN)], epilogue, nk=t // tk, acc_shape=(tm, d), temp_bytes=2 * tm * d * 4)[0]


def _ffn_bwd_dh(name, dgu, wgu):
    _, t, f = dgu.shape
    d, nb = wgu.shape[1], wgu.shape[2]
    tm = min(t, 512)

    def epilogue(acc, ins, outs):
        outs[0][...] = acc

    return _fused(name, (t // tm, 1, N_DEV),
                  [(dgu, pl.BlockSpec((None, tm, nb), lambda i, j, k: (k // 4, i, k % 4))),
                   (wgu, pl.BlockSpec((None, d, nb), lambda i, j, k: (k, 0, 0)))],
                  [(_sds((t, d), F32), pl.BlockSpec((tm, d), lambda i, j, k: (i, 0)))],
                  [(0, 1, NT)], epilogue, nk=N_DEV, acc_shape=(tm, d), temp_bytes=tm * d * 4)[0]


def _ffn_bwd_dwgu(name, h, dgu):
    t, d = h.shape
    nb = dgu.shape[2] // 4
    tk = min(t, 512)

    def epilogue(acc, ins, outs):
        outs[0][...] = acc.astype(BF16)

    return _fused(name, (N_DEV, 1, t // tk),
                  [(h, pl.BlockSpec((tk, d), lambda i, j, k: (k, 0))),
                   (dgu, pl.BlockSpec((None, tk, nb), lambda i, j, k: (i // 4, k, i % 4)))],
                  [(_sds((N_DEV, d, nb), BF16), pl.BlockSpec((None, d, nb), lambda i, j, k: (i, 0, 0)))],
                  [(0, 1, TN)], epilogue, nk=t // tk, acc_shape=(d, nb), temp_bytes=d * nb * 4)[0]


def _ffn_backward(tag, dy, x_in, g, h, gu, a, wgu, wd):
    dgu = _ffn_bwd_act(tag + "_bwd_act", dy, wd, gu)
    dwd = _ffn_bwd_dwd(tag + "_bwd_dwd", a, dy)
    dh = _ffn_bwd_dh(tag + "_bwd_dh", dgu, wgu)
    dwgu = _ffn_bwd_dwgu(tag + "_bwd_dwgu", h, dgu)
    dx, dg = _rms_bwd(tag + "_bwd_rms", x_in, g, dh, dy)
    return dx, dg, dwgu, dwd


def _proj(h, w_in):
    t, d = h.shape
    n = w_in.shape[1]
    tn = n // 4
    tm = min(t, 512)

    def epilogue(acc, ins, outs):
        outs[0][...] = acc

    return _fused("mix_proj", (4, t // tm, 1),
                  [(h, pl.BlockSpec((tm, d), lambda j, i, k: (i, 0))),
                   (w_in, pl.BlockSpec((d, tn), lambda j, i, k: (0, j)))],
                  [(_sds((t, n), F32), pl.BlockSpec((tm, tn), lambda j, i, k: (i, j)))],
                  [(0, 1, NN)], epilogue, temp_bytes=2 * tm * tn * 4)[0]


def _shift_rows(u, k):
    t = u.shape[0]
    rolled = pltpu.roll(u, k % t, axis=0)
    row = lax.broadcasted_iota(jnp.int32, u.shape, 0)
    keep = (row >= k) if k > 0 else (row < t + k)
    return jnp.where(keep, rolled, 0.0)


def _conv_fwd(proj, conv_w):
    t = proj.shape[0]
    cw = conv_w.shape[1]
    tc = min(cw, 256)
    nc = cw // tc

    def epilogue(_, ins, outs):
        u = ins[2][...] * ins[0][...]
        w = ins[3][...]
        y = u * w[2:3, :] + _shift_rows(u, 1) * w[1:2, :] + _shift_rows(u, 2) * w[0:1, :]
        outs[0][...] = (ins[1][...] * y).astype(BF16)

    def col(seg):
        return pl.BlockSpec((t, tc), lambda i, j, k: (0, seg * nc + i))

    return _fused("conv_fwd", (nc, 1, 1),
                  [(proj, col(0)), (proj, col(1)), (proj, col(2)),
                   (conv_w, pl.BlockSpec((8, tc), lambda i, j, k: (0, i)))],
                  [(_sds((t, cw), BF16), pl.BlockSpec((t, tc), lambda i, j, k: (0, i)))],
                  [], epilogue, temp_bytes=6 * t * tc * 4)[0]


def _conv_bwd(proj, conv_w, dca):
    t = proj.shape[0]
    cw = conv_w.shape[1]
    tc = min(cw, 256)
    nc = cw // tc

    def epilogue(_, ins, outs):
        xc, bg, cg, w, dc = ins[0][...], ins[1][...], ins[2][...], ins[3][...], ins[4][...]
        u = cg * xc
        u1, u2 = _shift_rows(u, 1), _shift_rows(u, 2)
        y = u * w[2:3, :] + u1 * w[1:2, :] + u2 * w[0:1, :]
        dconv = dc * bg
        du = dconv * w[2:3, :] + _shift_rows(dconv, -1) * w[1:2, :] + _shift_rows(dconv, -2) * w[0:1, :]
        outs[0][0] = (du * cg).astype(BF16)
        outs[0][1] = (dc * y).astype(BF16)
        outs[0][2] = (du * xc).astype(BF16)
        outs[1][...] = jnp.zeros_like(outs[1])
        outs[1][0:1, :] = jnp.sum(dconv * u2, axis=0, keepdims=True)
        outs[1][1:2, :] = jnp.sum(dconv * u1, axis=0, keepdims=True)
        outs[1][2:3, :] = jnp.sum(dconv * u, axis=0, keepdims=True)

    def col(seg):
        return pl.BlockSpec((t, tc), lambda i, j, k: (0, seg * nc + i))

    own = pl.BlockSpec((t, tc), lambda i, j, k: (0, i))
    wspec = pl.BlockSpec((8, tc), lambda i, j, k: (0, i))
    return _fused("conv_bwd", (nc, 1, 1),
                  [(proj, col(0)), (proj, col(1)), (proj, col(2)), (conv_w, wspec), (dca, own)],
                  [(_sds((3, t, cw), BF16), pl.BlockSpec((3, t, tc), lambda i, j, k: (0, 0, i))),
                   (_sds((8, cw), F32), wspec)],
                  [], epilogue, temp_bytes=10 * t * tc * 4)


def _split3(x):
    hi = x.astype(BF16)
    r1 = x - hi.astype(F32)
    mid = r1.astype(BF16)
    lo = (r1 - mid.astype(F32)).astype(BF16)
    return hi, mid, lo


def _head_selector(width):
    r = lax.broadcasted_iota(jnp.int32, (width, LANES), 0)
    c = lax.broadcasted_iota(jnp.int32, (width, LANES), 1)
    return (lax.shift_right_logical(r, 6) == c).astype(BF16)


def _head_sum(x, sel):
    return sum(jnp.dot(p, sel, preferred_element_type=F32) for p in _split3(x))


def _head_bcast(r, sel):
    return sum(lax.dot_general(p, sel, NT, preferred_element_type=F32) for p in _split3(r))


def _rope(x, c, sa, sb):
    n = x.shape[1]
    return x * c + pltpu.roll(x, n - ROT_DIM // 2, axis=1) * sa + pltpu.roll(x, ROT_DIM // 2, axis=1) * sb


def _rope_t(d, c, sa, sb):
    n = d.shape[1]
    return d * c + pltpu.roll(d * sa, ROT_DIM // 2, axis=1) + pltpu.roll(d * sb, n - ROT_DIM // 2, axis=1)


def _tile_lanes(tab, width):
    return tab if width == tab.shape[1] else jnp.tile(tab, (1, width // tab.shape[1]))


def _qk_prep(proj, gq, gk, rope_tabs, cw, kw):
    t = proj.shape[0]
    tm = _row_tile(t)

    def epilogue(_, ins, outs):
        c, sa, sb = ins[5][...], ins[6][...], ins[7][...]
        for src, gain, dst, width in ((0, 3, 0, cw), (1, 4, 1, kw)):
            xv = ins[src][...]
            sel = _head_selector(width)
            r = lax.rsqrt(_head_sum(xv * xv, sel) * (1.0 / HEAD_DIM) + RMS_EPS)
            xn = xv * _head_bcast(r, sel) * ins[gain][...]
            outs[dst][...] = _rope(xn, _tile_lanes(c, width), _tile_lanes(sa, width), _tile_lanes(sb, width)).astype(BF16)
        outs[2][...] = ins[2][...].astype(BF16)

    kblk = cw // kw
    tab = pl.BlockSpec((tm, LANES), lambda i, j, k: (i, 0))
    kspec = pl.BlockSpec((tm, kw), lambda i, j, k: (i, 0))
    return _fused("qk_prep", (t // tm, 1, 1),
                  [(proj, pl.BlockSpec((tm, cw), lambda i, j, k: (i, 3))),
                   (proj, pl.BlockSpec((tm, kw), lambda i, j, k: (i, 4 * kblk))),
                   (proj, pl.BlockSpec((tm, kw), lambda i, j, k: (i, 4 * kblk + 1))),
                   (gq, pl.BlockSpec((1, cw), lambda i, j, k: (0, 0))),
                   (gk, pl.BlockSpec((1, kw), lambda i, j, k: (0, 0))),
                   (rope_tabs[0], tab), (rope_tabs[1], tab), (rope_tabs[2], tab)],
                  [(_sds((t, cw), BF16), pl.BlockSpec((tm, cw), lambda i, j, k: (i, 0))),
                   (_sds((t, kw), BF16), kspec), (_sds((t, kw), BF16), kspec)],
                  [], epilogue, temp_bytes=12 * tm * cw * 4)


def _qk_prep_bwd(proj, gq, gk, rope_tabs, dq, dkc, dkp, dvc, dvp, cw, kw):
    t = proj.shape[0]
    tm = BLOCK
    nblk = t // tm

    def epilogue(_, ins, outs):
        c, sa, sb = ins[5][...], ins[6][...], ins[7][...]
        has_next = (pl.program_id(0) < nblk - 1).astype(F32)
        dk = ins[9][...] + has_next * ins[10][...]
        dv = ins[11][...] + has_next * ins[12][...]
        pieces = []
        for src, gain, dval, dst, width in ((0, 3, ins[8][...], 1, cw), (1, 4, dk, 2, kw)):
            xv, gv = ins[src][...], ins[gain][...]
            sel = _head_selector(width)
            r = _head_bcast(lax.rsqrt(_head_sum(xv * xv, sel) * (1.0 / HEAD_DIM) + RMS_EPS), sel)
            xh = xv * r
            dxn = _rope_t(dval, _tile_lanes(c, width), _tile_lanes(sa, width), _tile_lanes(sb, width))
            u = dxn * gv
            dot = _head_bcast(_head_sum(u * xh, sel), sel) * (1.0 / HEAD_DIM)
            pieces.append((r * (u - xh * dot)).astype(BF16))
            ri = lax.broadcasted_iota(jnp.int32, (width, LANES), 0)
            ci = lax.broadcasted_iota(jnp.int32, (width, LANES), 1)
            fold = (lax.bitwise_and(ri, HEAD_DIM - 1) == ci).astype(BF16)
            colsum = jnp.broadcast_to(jnp.sum(dxn * xh, axis=0, keepdims=True), (8, width))
            part = sum(jnp.dot(p, fold, preferred_element_type=F32) for p in _split3(colsum))

            @pl.when(pl.program_id(0) == 0)
            def _():
                outs[dst][...] = jnp.zeros_like(outs[dst])

            outs[dst][0:1, :] += part[0:1, :]
        outs[0][:, 0:cw] = pieces[0]
        outs[0][:, cw:cw + kw] = pieces[1]
        outs[0][:, cw + kw:cw + 2 * kw] = dv.astype(BF16)

    kblk = cw // kw
    tab = pl.BlockSpec((tm, LANES), lambda i, j, k: (i, 0))
    kcur = pl.BlockSpec((tm, kw), lambda i, j, k: (i, 0))
    knext = pl.BlockSpec((tm, kw), lambda i, j, k: (jnp.minimum(i + 1, nblk - 1), 0))
    acc = pl.BlockSpec((8, LANES), lambda i, j, k: (0, 0))
    return _fused("qk_prep_bwd", (nblk, 1, 1),
                  [(proj, pl.BlockSpec((tm, cw), lambda i, j, k: (i, 3))),
                   (proj, pl.BlockSpec((tm, kw), lambda i, j, k: (i, 4 * kblk))),
                   (proj, pl.BlockSpec((tm, kw), lambda i, j, k: (i, 4 * kblk + 1))),
                   (gq, pl.BlockSpec((1, cw), lambda i, j, k: (0, 0))),
                   (gk, pl.BlockSpec((1, kw), lambda i, j, k: (0, 0))),
                   (rope_tabs[0], tab), (rope_tabs[1], tab), (rope_tabs[2], tab),
                   (dq, pl.BlockSpec((tm, cw), lambda i, j, k: (i, 0))),
                   (dkc, kcur), (dkp, knext), (dvc, kcur), (dvp, knext)],
                  [(_sds((t, cw + 2 * kw), BF16), pl.BlockSpec((tm, cw + 2 * kw), lambda i, j, k: (i, 0))),
                   (_sds((8, LANES), F32), acc), (_sds((8, LANES), F32), acc)],
                  [], epilogue, temp_bytes=16 * tm * cw * 4, semantics=("arbitrary", "arbitrary", "arbitrary"))


def _attn_mask(n):
    row = lax.broadcasted_iota(jnp.int32, (BLOCK, 2 * BLOCK), 0)
    col = lax.broadcasted_iota(jnp.int32, (BLOCK, 2 * BLOCK), 1)
    return (col > row) & (col <= row + BLOCK) & ((col >= BLOCK) | (n > 0))


def _softmax_with_sink(q, k2, sink, valid):
    s = lax.dot_general(q, k2, NT, preferred_element_type=F32) * ATTN_SCALE
    s = jnp.where(valid, s, NEG_INF)
    m = jnp.maximum(jnp.max(s, axis=-1, keepdims=True), sink)
    p = jnp.exp(s - m)
    es = jnp.exp(sink - m)
    denom = jnp.sum(p, axis=-1, keepdims=True) + es
    return p / denom, es / denom


def _attn_fwd(qn, kn, vb, sink_rows):
    t, cw = qn.shape
    kw = kn.shape[1]
    nkv = kw // HEAD_DIM

    def body(q_ref, kp_ref, kc_ref, vp_ref, vc_ref, sink_ref, o_ref):
        valid = _attn_mask(pl.program_id(0))
        qv = q_ref[...]
        kp, kc, vp, vc = kp_ref[...], kc_ref[...], vp_ref[...], vc_ref[...]
        outs = []
        for h in range(nkv):
            hs = slice(h * HEAD_DIM, (h + 1) * HEAD_DIM)
            k2 = jnp.concatenate([kp[:, hs], kc[:, hs]], axis=0)
            v2 = jnp.concatenate([vp[:, hs], vc[:, hs]], axis=0)
            for g in range(GROUP):
                hq = h * GROUP + g
                pn, _ = _softmax_with_sink(qv[:, hq * HEAD_DIM:(hq + 1) * HEAD_DIM], k2, sink_ref[hq:hq + 1, 0:1], valid)
                outs.append(jnp.dot(pn.astype(BF16), v2, preferred_element_type=F32))
        o_ref[...] = jnp.concatenate(outs, axis=-1).astype(BF16)

    cur = lambda n: (n, 0)
    prev = lambda n: (jnp.maximum(n - 1, 0), 0)
    return pl.pallas_call(
        body, name="attn_fwd", grid=(t // BLOCK,),
        in_specs=[pl.BlockSpec((BLOCK, cw), cur),
                  pl.BlockSpec((BLOCK, kw), prev), pl.BlockSpec((BLOCK, kw), cur),
                  pl.BlockSpec((BLOCK, kw), prev), pl.BlockSpec((BLOCK, kw), cur),
                  pl.BlockSpec(sink_rows.shape, lambda n: (0, 0))],
        out_specs=pl.BlockSpec((BLOCK, cw), cur),
        out_shape=_sds((t, cw), BF16),
        compiler_params=_params(("parallel",), BLOCK * (cw + 4 * kw) * 2 + BLOCK * cw * 2, 8 << 20),
    )(qn, kn, kn, vb, vb, sink_rows)


def _attn_bwd(qn, kn, vb, sink_rows, do):
    t, cw = qn.shape
    kw = kn.shape[1]
    nkv = kw // HEAD_DIM
    nq = nkv * GROUP

    def body(q_ref, kp_ref, kc_ref, vp_ref, vc_ref, sink_ref, do_ref,
             dq_ref, dkc_ref, dkp_ref, dvc_ref, dvp_ref, dsink_ref):
        n = pl.program_id(0)
        valid = _attn_mask(n)
        qv, dov = q_ref[...], do_ref[...]
        kp, kc, vp, vc = kp_ref[...], kc_ref[...], vp_ref[...], vc_ref[...]
        dqs, dks, dvs, dsinks = [], [], [], []
        for h in range(nkv):
            hs = slice(h * HEAD_DIM, (h + 1) * HEAD_DIM)
            k2 = jnp.concatenate([kp[:, hs], kc[:, hs]], axis=0)
            v2 = jnp.concatenate([vp[:, hs], vc[:, hs]], axis=0)
            dk2 = jnp.zeros((2 * BLOCK, HEAD_DIM), F32)
            dv2 = jnp.zeros((2 * BLOCK, HEAD_DIM), F32)
            for g in range(GROUP):
                hq = h * GROUP + g
                qs = slice(hq * HEAD_DIM, (hq + 1) * HEAD_DIM)
                q = qv[:, qs]
                pn, psink = _softmax_with_sink(q, k2, sink_ref[hq:hq + 1, 0:1], valid)
                pb = pn.astype(BF16)
                dob = dov[:, qs].astype(BF16)
                dpn = lax.dot_general(dob, v2, NT, preferred_element_type=F32)
                dv2 = dv2 + lax.dot_general(pb, dob, TN, preferred_element_type=F32)
                delta = jnp.sum(pn * dpn, axis=-1, keepdims=True)
                ds = (pn * (dpn - delta) * ATTN_SCALE).astype(BF16)
                dqs.append(jnp.dot(ds, k2, preferred_element_type=F32))
                dk2 = dk2 + lax.dot_general(ds, q, TN, preferred_element_type=F32)
                dsinks.append(jnp.broadcast_to(jnp.sum(-psink * delta, axis=0, keepdims=True), (1, LANES)))
            dks.append(dk2)
            dvs.append(dv2)
        dq_ref[...] = jnp.concatenate(dqs, axis=-1)
        dkp_ref[...] = jnp.concatenate([d[:BLOCK] for d in dks], axis=-1)
        dkc_ref[...] = jnp.concatenate([d[BLOCK:] for d in dks], axis=-1)
        dvp_ref[...] = jnp.concatenate([d[:BLOCK] for d in dvs], axis=-1)
        dvc_ref[...] = jnp.concatenate([d[BLOCK:] for d in dvs], axis=-1)

        @pl.when(n == 0)
        def _():
            dsink_ref[...] = jnp.zeros_like(dsink_ref)

        dsink_ref[...] += jnp.concatenate(dsinks, axis=0)

    cur = lambda n: (n, 0)
    prev = lambda n: (jnp.maximum(n - 1, 0), 0)
    kspec = pl.BlockSpec((BLOCK, kw), cur)
    return pl.pallas_call(
        body, name="attn_bwd", grid=(t // BLOCK,),
        in_specs=[pl.BlockSpec((BLOCK, cw), cur),
                  pl.BlockSpec((BLOCK, kw), prev), kspec,
                  pl.BlockSpec((BLOCK, kw), prev), kspec,
                  pl.BlockSpec(sink_rows.shape, lambda n: (0, 0)),
                  pl.BlockSpec((BLOCK, cw), cur)],
        out_specs=[pl.BlockSpec((BLOCK, cw), cur), kspec, kspec, kspec, kspec,
                   pl.BlockSpec((nq, LANES), lambda n: (0, 0))],
        out_shape=[_sds((t, cw), F32)] + [_sds((t, kw), F32)] * 4 + [_sds((nq, LANES), F32)],
        compiler_params=_params(("arbitrary",), BLOCK * (cw + 4 * kw) * 2 + 2 * BLOCK * cw * 4 + 4 * BLOCK * kw * 4, 12 << 20),
    )(qn, kn, kn, vb, vb, sink_rows, do)


def _mix_out(ca, o, woc, woa, proj):
    t, cw = ca.shape
    nb = woc.shape[2]
    d = N_DEV * nb
    tm = min(t, 1024)
    ga0 = (3 * cw + cw + 2 * (cw // 4)) // nb

    def body(ca_ref, o_ref, woc_ref, woa_ref, ga_ref, gb_ref, m_ref, ya_ref, yb_ref):
        ya = jnp.dot(ca_ref[...], woc_ref[...], preferred_element_type=F32)
        yb = jnp.dot(o_ref[...], woa_ref[...], preferred_element_type=F32)
        ya_ref[...] = ya.astype(BF16)
        yb_ref[...] = yb.astype(BF16)
        m_ref[...] = (_sigmoid(ga_ref[...]) * ya + _sigmoid(gb_ref[...]) * yb).astype(BF16)

    act = pl.BlockSpec((tm, cw), lambda i, j: (i, 0))
    wsp = pl.BlockSpec((None, cw, nb), lambda i, j: (j, 0, 0))
    osp = pl.BlockSpec((tm, nb), lambda i, j: (i, j))
    blocks = 2 * tm * cw * 2 + 2 * cw * nb * 2 + 2 * tm * nb * 4 + 3 * tm * nb * 2
    return pl.pallas_call(
        body, name="mix_out", grid=(t // tm, N_DEV),
        in_specs=[act, act, wsp, wsp,
                  pl.BlockSpec((tm, nb), lambda i, j: (i, ga0 + j)),
                  pl.BlockSpec((tm, nb), lambda i, j: (i, ga0 + N_DEV + j))],
        out_specs=[osp, osp, osp],
        out_shape=[_sds((t, d), BF16)] * 3,
        compiler_params=_params(("parallel", "parallel"), blocks, 6 * tm * nb * 4),
    )(ca, o, woc, woa, proj, proj)


def _mix_residual(merged, wo, x):
    t, d = x.shape
    tm = min(t, 512)

    def epilogue(acc, ins, outs):
        outs[0][...] = ins[2][...] + acc

    row = pl.BlockSpec((tm, d), lambda i, j, k: (i, 0))
    return _fused("mix_residual", (t // tm, 1, 1),
                  [(merged, row), (wo, pl.BlockSpec((d, d), lambda i, j, k: (0, 0))), (x, row)],
                  [(_sds((t, d), F32), row)], [(0, 1, NN)], epilogue, temp_bytes=2 * tm * d * 4)[0]


def _mix_bwd_gates(dx, wo, ya, yb, proj, cw):
    t, d = dx.shape
    tm = min(t, 512)
    tn = min(d, 512)
    ga0 = (4 * cw + 2 * (cw // 4)) // tn

    def epilogue(acc, ins, outs):
        sa, sb = _sigmoid(ins[4][...]), _sigmoid(ins[5][...])
        outs[0][...] = (acc * sa).astype(BF16)
        outs[1][...] = (acc * sb).astype(BF16)
        outs[2][0] = (acc * ins[2][...].astype(F32) * sa * (1.0 - sa)).astype(BF16)
        outs[2][1] = (acc * ins[3][...].astype(F32) * sb * (1.0 - sb)).astype(BF16)

    blk = pl.BlockSpec((tm, tn), lambda i, j, k: (i, j))
    return _fused("mix_bwd_gates", (t // tm, d // tn, 1),
                  [(dx, pl.BlockSpec((tm, d), lambda i, j, k: (i, 0))),
                   (wo, pl.BlockSpec((tn, d), lambda i, j, k: (j, 0))),
                   (ya, blk), (yb, blk),
                   (proj, pl.BlockSpec((tm, tn), lambda i, j, k: (i, ga0 + j))),
                   (proj, pl.BlockSpec((tm, tn), lambda i, j, k: (i, ga0 + d // tn + j)))],
                  [(_sds((t, d), BF16), blk), (_sds((t, d), BF16), blk),
                   (_sds((2, t, d), BF16), pl.BlockSpec((2, tm, tn), lambda i, j, k: (0, i, j)))],
                  [(0, 1, NT)], epilogue, temp_bytes=8 * tm * tn * 4)


def _tn_matmul(name, a, b, tm, out_dtype=BF16):
    t, m = a.shape
    n = b.shape[1]
    tk = min(t, 512)

    def epilogue(acc, ins, outs):
        outs[0][...] = acc.astype(out_dtype)

    return _fused(name, (m // tm, 1, t // tk),
                  [(a, pl.BlockSpec((tk, tm), lambda i, j, k: (k, i))),
                   (b, pl.BlockSpec((tk, n), lambda i, j, k: (k, 0)))],
                  [(_sds((m, n), out_dtype), pl.BlockSpec((tm, n), lambda i, j, k: (i, 0)))],
                  [(0, 1, TN)], epilogue, nk=t // tk, acc_shape=(tm, n), temp_bytes=tm * n * 4)[0]


def _out_proj_bwd_act(name, dy, w):
    t, d = dy.shape
    kdim, nb = w.shape[1], w.shape[2]
    tm = min(t, 1024)

    def epilogue(acc, ins, outs):
        outs[0][...] = acc

    return _fused(name, (t // tm, 1, N_DEV),
                  [(dy, pl.BlockSpec((tm, nb), lambda i, j, k: (i, k))),
                   (w, pl.BlockSpec((None, kdim, nb), lambda i, j, k: (k, 0, 0)))],
                  [(_sds((t, kdim), F32), pl.BlockSpec((tm, kdim), lambda i, j, k: (i, 0)))],
                  [(0, 1, NT)], epilogue, nk=N_DEV, acc_shape=(tm, kdim), temp_bytes=tm * kdim * 4)[0]


def _out_proj_bwd_w(name, act, dy, nb):
    t, kdim = act.shape
    tk = min(t, 1024)

    def epilogue(acc, ins, outs):
        outs[0][...] = acc.astype(BF16)

    return _fused(name, (N_DEV, 1, t // tk),
                  [(act, pl.BlockSpec((tk, kdim), lambda i, j, k: (k, 0))),
                   (dy, pl.BlockSpec((tk, nb), lambda i, j, k: (k, i)))],
                  [(_sds((N_DEV, kdim, nb), BF16), pl.BlockSpec((None, kdim, nb), lambda i, j, k: (i, 0, 0)))],
                  [(0, 1, TN)], epilogue, nk=t // tk, acc_shape=(kdim, nb), temp_bytes=kdim * nb * 4)[0]


def _proj_bwd_act(dproj, w_in):
    t, n = dproj.shape
    d = w_in.shape[0]
    tm = min(t, 512)
    tk = n // 4

    def epilogue(acc, ins, outs):
        outs[0][...] = acc

    return _fused("mix_bwd_dh", (t // tm, 1, 4),
                  [(dproj, pl.BlockSpec((tm, tk), lambda i, j, k: (i, k))),
                   (w_in, pl.BlockSpec((d, tk), lambda i, j, k: (0, k)))],
                  [(_sds((t, d), F32), pl.BlockSpec((tm, d), lambda i, j, k: (i, 0)))],
                  [(0, 1, NT)], epilogue, nk=4, acc_shape=(tm, d), temp_bytes=tm * d * 4)[0]


def _proj_bwd_w(h, dproj):
    t, d = h.shape
    n = dproj.shape[1]
    tn = n // 4
    tm = min(d, 1024)
    tk = min(t, 512)

    def epilogue(acc, ins, outs):
        outs[0][...] = acc.astype(BF16)

    return _fused("mix_bwd_dwin", (d // tm, 4, t // tk),
                  [(h, pl.BlockSpec((tk, tm), lambda i, j, k: (k, i))),
                   (dproj, pl.BlockSpec((tk, tn), lambda i, j, k: (k, j)))],
                  [(_sds((d, n), BF16), pl.BlockSpec((tm, tn), lambda i, j, k: (i, j)))],
                  [(0, 1, TN)], epilogue, nk=t // tk, acc_shape=(tm, tn), temp_bytes=tm * tn * 4)[0]


def _adamw_math(w, g, m, v):
    m = ADAM_B1 * m + (1.0 - ADAM_B1) * g
    v = ADAM_B2 * v + (1.0 - ADAM_B2) * (g * g)
    m_hat = m / (1.0 - ADAM_B1 ** ADAM_STEP)
    v_hat = v / (1.0 - ADAM_B2 ** ADAM_STEP)
    delta = -ADAM_LR * (m_hat / (jnp.sqrt(v_hat) + ADAM_EPS) + ADAM_WD * w)
    return delta, m, v


def _adamw(name, parts, w, m, v, tr):
    r, c = w.shape

    def body(p_ref, w_ref, m_ref, v_ref, g_out, d_out, m_out, v_out):
        g = p_ref[0].astype(F32)
        for s in range(1, N_DEV):
            g = g + p_ref[s].astype(F32)
        delta, mn, vn = _adamw_math(w_ref[...], g, m_ref[...], v_ref[...])
        g_out[...] = g
        d_out[...] = delta
        m_out[...] = mn
        v_out[...] = vn

    blk = pl.BlockSpec((tr, c), lambda i: (i, 0))
    blocks = N_DEV * tr * c * parts.dtype.itemsize + 7 * tr * c * 4
    return pl.pallas_call(
        body, name=name, grid=(r // tr,),
        in_specs=[pl.BlockSpec((N_DEV, tr, c), lambda i: (0, i, 0)), blk, blk, blk],
        out_specs=[blk] * 4, out_shape=[_sds((r, c), F32)] * 4,
        compiler_params=_params(("parallel",), blocks, 6 * tr * c * 4),
    )(parts, w, m, v)


def _rope_tables(t):
    half = ROT_DIM // 2
    inv_freq = 1.0 / (ROPE_THETA ** (jnp.arange(0, ROT_DIM, 2, dtype=F32) / ROT_DIM))
    ang = jnp.arange(t, dtype=F32)[:, None] * inv_freq[None, :]
    cos, sin = jnp.cos(ang), jnp.sin(ang)
    ones = jnp.ones((t, HEAD_DIM - ROT_DIM), F32)
    zeros = jnp.zeros((t, HEAD_DIM - half), F32)
    c = jnp.concatenate([cos, cos, ones], axis=1)
    sa = jnp.concatenate([-sin, zeros], axis=1)
    sb = jnp.concatenate([jnp.zeros((t, half), F32), sin, jnp.zeros((t, HEAD_DIM - ROT_DIM), F32)], axis=1)
    return tuple(jnp.tile(a, (1, LANES // HEAD_DIM)) for a in (c, sa, sb))


def _pad_rows(a, rows=8):
    return jnp.pad(a, ((0, rows - a.shape[0]), (0, 0)))


def kernel(x, g_ffn1, w_gu1, w_down1, g_mix, w_in, conv_w, q_norm_g, k_norm_g, sinks, w_out_conv, w_out_attn, w_o, g_ffn2, w_gu2, w_down2, loss_target, m_g_ffn1, m_w_gu1, m_w_down1, m_g_mix, m_w_in, m_conv_w, m_q_norm_g, m_k_norm_g, m_sinks, m_w_out_conv, m_w_out_attn, m_w_o, m_g_ffn2, m_w_gu2, m_w_down2, v_g_ffn1, v_w_gu1, v_w_down1, v_g_mix, v_w_in, v_conv_w, v_q_norm_g, v_k_norm_g, v_sinks, v_w_out_conv, v_w_out_attn, v_w_o, v_g_ffn2, v_w_gu2, v_w_down2):
    t, d = x.shape[1], x.shape[2]
    cw = d // 2
    kw = cw // GROUP
    nq = cw // HEAD_DIM
    xs, target = x.reshape(t, d), loss_target.reshape(t, d)
    me = 4 * lax.axis_index("x") + 2 * lax.axis_index("y") + lax.axis_index("c")

    big = {"w_gu1": w_gu1, "w_down1": w_down1, "w_in": w_in, "w_out_conv": w_out_conv,
           "w_out_attn": w_out_attn, "w_o": w_o, "w_gu2": w_gu2, "w_down2": w_down2}
    big_m = {"w_gu1": m_w_gu1, "w_down1": m_w_down1, "w_in": m_w_in, "w_out_conv": m_w_out_conv,
             "w_out_attn": m_w_out_attn, "w_o": m_w_o, "w_gu2": m_w_gu2, "w_down2": m_w_down2}
    big_v = {"w_gu1": v_w_gu1, "w_down1": v_w_down1, "w_in": v_w_in, "w_out_conv": v_w_out_conv,
             "w_out_attn": v_w_out_attn, "w_o": v_w_o, "w_gu2": v_w_gu2, "w_down2": v_w_down2}
    names = list(big)

    shards = [big[n][0].astype(BF16) for n in names] + [_pad_rows(conv_w[0])]
    gathered = _exchange("gather_weights", shards, gather=True)
    wts = dict(zip(names, gathered[:-1]))
    conv_full = jnp.transpose(gathered[-1], (1, 0, 2)).reshape(8, cw)
    wd1 = wts["w_down1"].reshape(-1, d)
    wd2 = wts["w_down2"].reshape(-1, d)
    wo = wts["w_o"].reshape(d, d)
    w_in_full = jnp.transpose(wts["w_in"], (1, 0, 2)).reshape(d, -1)
    rope_tabs = _rope_tables(t)
    gq = jnp.tile(q_norm_g, (1, nq))
    gk = jnp.tile(k_norm_g, (1, nq // GROUP))
    sink_rows = jnp.broadcast_to(sinks[0][:, None], (nq, LANES))

    h1 = _rms_fwd("ffn1_norm", xs, g_ffn1)
    gu1, a1 = _ffn_up("ffn1_up", h1, wts["w_gu1"])
    x1 = _ffn_down("ffn1_down", a1, wd1, xs)
    h2 = _rms_fwd("mix_norm", x1, g_mix)
    proj = _proj(h2, w_in_full)
    ca = _conv_fwd(proj, conv_full)
    qn, kn, vb = _qk_prep(proj, gq, gk, rope_tabs, cw, kw)
    o = _attn_fwd(qn, kn, vb, sink_rows)
    merged, ya, yb = _mix_out(ca, o, wts["w_out_conv"], wts["w_out_attn"], proj)
    x2 = _mix_residual(merged, wo, x1)
    h3 = _rms_fwd("ffn2_norm", x2, g_ffn2)
    gu2, a2 = _ffn_up("ffn2_up", h3, wts["w_gu2"])
    y = _ffn_down("ffn2_down", a2, wd2, x2)
    dy, sq = _loss_dy(y, target)
    loss = lax.psum(sq[0, 0] * (0.5 / d), ("x", "y", "c"))

    grads = {}
    dx2, dg_ffn2, grads["w_gu2"], grads["w_down2"] = _ffn_backward("ffn2", dy, x2, g_ffn2, h3, gu2, a2, wts["w_gu2"], wd2)
    dya, dyb, dgates = _mix_bwd_gates(dx2, wo, ya, yb, proj, cw)
    grads["w_o"] = _tn_matmul("mix_bwd_dwo", merged, dx2, min(d, 1024))
    dca = _out_proj_bwd_act("mix_bwd_dca", dya, wts["w_out_conv"])
    do = _out_proj_bwd_act("mix_bwd_do", dyb, wts["w_out_attn"])
    grads["w_out_conv"] = _out_proj_bwd_w("mix_bwd_dwoc", ca, dya, d // N_DEV)
    grads["w_out_attn"] = _out_proj_bwd_w("mix_bwd_dwoa", o, dyb, d // N_DEV)
    d3, dconv_w = _conv_bwd(proj, conv_full, dca)
    dq, dkc, dkp, dvc, dvp, dsink = _attn_bwd(qn, kn, vb, sink_rows, do)
    dqkv, dgq, dgk = _qk_prep_bwd(proj, gq, gk, rope_tabs, dq, dkc, dkp, dvc, dvp, cw, kw)
    dproj = jnp.concatenate([d3[0], d3[1], d3[2], dqkv, dgates[0], dgates[1]], axis=1)
    dh2 = _proj_bwd_act(dproj, w_in_full)
    dw_in = _proj_bwd_w(h2, dproj)
    grads["w_in"] = jnp.transpose(dw_in.reshape(d, N_DEV, -1), (1, 0, 2))
    dx1, dg_mix = _rms_bwd("mix_bwd_rms", x1, g_mix, dh2, dx2)
    grad_x, dg_ffn1, grads["w_gu1"], grads["w_down1"] = _ffn_backward("ffn1", dx1, xs, g_ffn1, h1, gu1, a1, wts["w_gu1"], wd1)

    stacks = [grads[n].reshape((N_DEV,) + big[n].shape[1:]) for n in names]
    landed = _exchange("scatter_grads", stacks, gather=False)
    tiles = {"w_gu1": 256, "w_gu2": 256, "w_in": 256, "w_down1": 176, "w_down2": 176,
             "w_out_conv": 1024, "w_out_attn": 1024, "w_o": 128}
    big_out = {}
    for n, parts in zip(names, landed):
        r = big[n].shape[1]
        tr = tiles[n] if r % tiles[n] == 0 else r
        res = _adamw("adamw_" + n, parts, big[n][0], big_m[n][0], big_v[n][0], tr)
        big_out[n] = [a[None] for a in res]

    small = {"g_ffn1": dg_ffn1[0:1], "g_mix": dg_mix[0:1], "g_ffn2": dg_ffn2[0:1],
             "q_norm_g": dgq[0:1, :HEAD_DIM], "k_norm_g": dgk[0:1, :HEAD_DIM], "sinks": dsink[:, 0][None],
             "conv_w": dconv_w[0:CONV_K].reshape(1, -1)}
    small_w = {"g_ffn1": g_ffn1, "g_mix": g_mix, "g_ffn2": g_ffn2, "q_norm_g": q_norm_g, "k_norm_g": k_norm_g,
               "sinks": sinks, "conv_w": None}
    small_m = {"g_ffn1": m_g_ffn1, "g_mix": m_g_mix, "g_ffn2": m_g_ffn2, "q_norm_g": m_q_norm_g,
               "k_norm_g": m_k_norm_g, "sinks": m_sinks, "conv_w": m_conv_w}
    small_v = {"g_ffn1": v_g_ffn1, "g_mix": v_g_mix, "g_ffn2": v_g_ffn2, "q_norm_g": v_q_norm_g,
               "k_norm_g": v_k_norm_g, "sinks": v_sinks, "conv_w": v_conv_w}
    snames = list(small)
    widths = [small[n].shape[1] for n in snames]
    total = sum(widths)
    rows = -(-total // LANES)
    rows = -(-rows // 8) * 8

    def pack(vals):
        flat = jnp.concatenate([v.reshape(1, -1) for v in vals], axis=1)
        return jnp.pad(flat, ((0, 0), (0, rows * LANES - total))).reshape(rows, LANES)

    csh = cw // N_DEV

    def place_conv(local, fill):
        full = jnp.full((CONV_K, cw), fill, F32)
        return lax.dynamic_update_slice(full, local, (0, me * csh)).reshape(1, -1)

    pw = pack([small_w[n] if n != "conv_w" else place_conv(conv_w[0], 0.0) for n in snames])
    pm = pack([small_m[n] if n != "conv_w" else place_conv(m_conv_w[0], 0.0) for n in snames])
    pv = pack([small_v[n] if n != "conv_w" else place_conv(v_conv_w[0], 1.0) for n in snames])
    parts = _exchange("gather_small_grads", [pack([small[n] for n in snames])], gather=True)[0]
    sg, sd, sm, sv = [a.reshape(1, -1) for a in _adamw("adamw_small", parts, pw, pm, pv, rows)]

    def unpack(flat, n):
        off = sum(widths[:snames.index(n)])
        piece = flat[:, off:off + widths[snames.index(n)]]
        if n == "conv_w":
            piece = lax.dynamic_slice(piece.reshape(CONV_K, cw), (0, me * csh), (CONV_K, csh))[None]
        return piece

    order = ["g_ffn1", "w_gu1", "w_down1", "g_mix", "w_in", "conv_w", "q_norm_g", "k_norm_g", "sinks",
             "w_out_conv", "w_out_attn", "w_o", "g_ffn2", "w_gu2", "w_down2"]
    outs = [loss, grad_x[None]]
    for idx, flat in enumerate((sg, sd, sm, sv)):
        for n in order:
            outs.append(big_out[n][idx] if n in big_out else unpack(flat, n))
    return tuple(outs)
```

```python
import functools

import jax
import jax.numpy as jnp
from jax import lax
from jax.experimental import pallas as pl
from jax.experimental.pallas import tpu as pltpu

F32 = jnp.float32
BF16 = jnp.bfloat16

N_DEV = 8
HEAD_DIM = 64
GROUP = 4
BLOCK = 128
ROT_DIM = 16
ROPE_THETA = 500000.0
RMS_EPS = 1e-6
NEG_INF = -1e30
ATTN_SCALE = HEAD_DIM ** -0.5
CONV_K = 3
LANES = 128
VMEM_BYTES_V7X = 64 * 1024 * 1024
VMEM_CAP = VMEM_BYTES_V7X - 6 * 1024 * 1024

ADAM_LR = 0.001
ADAM_B1 = 0.9
ADAM_B2 = 0.999
ADAM_EPS = 1e-08
ADAM_WD = 0.01
ADAM_STEP = 10

NN = (((1,), (0,)), ((), ()))
NT = (((1,), (1,)), ((), ()))
TN = (((0,), (0,)), ((), ()))

MESH = pl.DeviceIdType.MESH


def _nbytes(shape, dtype):
    n = 1
    for s in shape:
        if s is not None:
            n *= s
    return n * jnp.dtype(dtype).itemsize


def _params(semantics, block_bytes, temp_bytes):
    est = 2 * block_bytes + temp_bytes + (4 << 20)
    return pltpu.CompilerParams(dimension_semantics=semantics, vmem_limit_bytes=int(min(max(est, 16 << 20), VMEM_CAP)))


def _fused(name, grid, ins, outs, dots, epilogue, *, nk=1, acc_shape=None, temp_bytes=0,
           semantics=("parallel", "parallel", "arbitrary"), deps=()):
    n_in, n_out = len(ins), len(outs)
    n_dep = len(deps)

    def body(*refs):
        in_refs, out_refs = refs[:n_in], refs[n_in + n_dep:n_in + n_dep + n_out]
        scratch = refs[n_in + n_dep + n_out:]

        def products():
            total = None
            for ai, bi, contract in dots:
                a, b = in_refs[ai][...], in_refs[bi][...]
                a = a if a.dtype == BF16 else a.astype(BF16)
                b = b if b.dtype == BF16 else b.astype(BF16)
                p = lax.dot_general(a, b, contract, preferred_element_type=F32)
                total = p if total is None else total + p
            return total

        if nk == 1:
            epilogue(products() if dots else None, in_refs, out_refs)
        else:
            acc = scratch[0]
            k = pl.program_id(2)

            @pl.when(k == 0)
            def _():
                acc[...] = jnp.zeros_like(acc)

            acc[...] += products()

            @pl.when(k == nk - 1)
            def _():
                epilogue(acc[...], in_refs, out_refs)

    block_bytes = sum(_nbytes(spec.block_shape, a.dtype) for a, spec in ins)
    block_bytes += sum(_nbytes(spec.block_shape, s.dtype) for s, spec in outs)
    scratch_shapes = []
    if nk > 1:
        scratch_shapes.append(pltpu.VMEM(acc_shape, F32))
        temp_bytes += _nbytes(acc_shape, F32)
    res = pl.pallas_call(
        body, name=name, grid=grid,
        in_specs=[spec for _, spec in ins] + [pl.BlockSpec(memory_space=pl.ANY)] * n_dep,
        out_specs=[spec for _, spec in outs],
        out_shape=[s for s, _ in outs],
        scratch_shapes=scratch_shapes,
        compiler_params=_params(semantics, block_bytes, temp_bytes),
    )(*[a for a, _ in ins], *deps)
    return res


def _sds(shape, dtype):
    return jax.ShapeDtypeStruct(shape, dtype)


def _sigmoid(x):
    return jax.nn.sigmoid(x)


def _exchange(name, arrays, gather):
    n = len(arrays)
    out_shapes = [((N_DEV,) + a.shape) if gather else a.shape for a in arrays]

    def body(*refs):
        srcs, dsts = refs[:n], refs[n:2 * n]
        send_sems, recv_sems, local_sems = refs[2 * n:]
        x, y, c = lax.axis_index("x"), lax.axis_index("y"), lax.axis_index("c")
        me = 4 * x + 2 * y + c
        copies = []
        for w in range(n):
            own = srcs[w] if gather else srcs[w].at[me]
            local = pltpu.make_async_copy(own, dsts[w].at[me], local_sems.at[w])
            local.start()
            copies.append(local)
            for k in range(1, N_DEV):
                px = (1 - x) if (k & 4) else x
                py = (1 - y) if (k & 2) else y
                pc = (1 - c) if (k & 1) else c
                peer = 4 * px + 2 * py + pc
                cp = pltpu.make_async_remote_copy(
                    src_ref=srcs[w] if gather else srcs[w].at[peer],
                    dst_ref=dsts[w].at[me],
                    send_sem=send_sems.at[w * (N_DEV - 1) + k - 1],
                    recv_sem=recv_sems.at[w * (N_DEV - 1) + k - 1],
                    device_id=(px, py, pc), device_id_type=MESH)
                cp.start()
                copies.append(cp)
        for cp in copies:
            cp.wait()

    hbm = pl.BlockSpec(memory_space=pltpu.HBM)
    return pl.pallas_call(
        body, name=name,
        in_specs=[hbm] * n, out_specs=[hbm] * n,
        out_shape=[_sds(s, a.dtype) for s, a in zip(out_shapes, arrays)],
        scratch_shapes=[pltpu.SemaphoreType.DMA((n * (N_DEV - 1),)),
                        pltpu.SemaphoreType.DMA((n * (N_DEV - 1),)),
                        pltpu.SemaphoreType.DMA((n,))],
    )(*arrays)


_HBM = pl.BlockSpec(memory_space=pltpu.HBM)
_SEM = pl.BlockSpec(memory_space=pltpu.SEMAPHORE)
_ANY = pl.BlockSpec(memory_space=pl.ANY)
_EFFECT = pltpu.SideEffectType.DATAFLOW_SIDE_EFFECTING
N_TARGETS = 4


def _mesh_pos():
    return lax.axis_index("x"), lax.axis_index("y"), lax.axis_index("c")


def _chip_peers(x, y, c):
    return [(1 - x, y, c), (x, 1 - y, c), (1 - x, 1 - y, c)]


def _dev_index(pos):
    return 4 * pos[0] + 2 * pos[1] + pos[2]


def _hbm_like(a):
    return pltpu.HBM(a.shape, a.dtype)


def _gather_start(shards, lands):
    n = len(shards)

    def body(*refs):
        srcs, dsts = refs[:n], refs[n:2 * n]
        send, recv = refs[2 * n], refs[2 * n + 1]
        x, y, c = _mesh_pos()
        me = _dev_index((x, y, c))
        targets = [(x, y, 1 - c)] + _chip_peers(x, y, c)
        for w in range(n):
            for k, to in enumerate(targets):
                pltpu.make_async_remote_copy(
                    src_ref=srcs[w], dst_ref=dsts[w].at[me],
                    send_sem=send.at[N_TARGETS * w + k], recv_sem=recv.at[N_TARGETS * w + k],
                    device_id=to, device_id_type=MESH).start()

    sems = pltpu.SemaphoreType.DMA((N_TARGETS * n,))
    outs = pl.pallas_call(
        body, name="gather_start",
        in_specs=[_HBM] * (2 * n), out_specs=[_SEM, _SEM] + [_HBM] * (2 * n),
        out_shape=[sems, sems] + [_hbm_like(a) for a in shards] + [_hbm_like(a) for a in lands],
        input_output_aliases={i: 2 + i for i in range(2 * n)},
        compiler_params=pltpu.CompilerParams(has_side_effects=_EFFECT),
    )(*shards, *lands)
    return outs[0], outs[1], list(outs[2:2 + n]), list(outs[2 + n:])


def _gather_wait(name, idxs, send, recv, shards, lands, after):
    m = len(idxs)

    def body(*refs):
        srcs, dsts = refs[:m], refs[m:2 * m]
        send_sems, recv_sems = refs[2 * m], refs[2 * m + 1]
        x, y, c = _mesh_pos()
        sources = [(x, y, 1 - c)] + _chip_peers(x, y, c)
        for j, w in enumerate(idxs):
            for k, frm in enumerate(sources):
                cp = pltpu.make_async_remote_copy(
                    src_ref=srcs[j], dst_ref=dsts[j].at[_dev_index(frm)],
                    send_sem=send_sems.at[N_TARGETS * w + k], recv_sem=recv_sems.at[N_TARGETS * w + k],
                    device_id=frm, device_id_type=MESH)
                cp.wait_send()
                cp.wait_recv()

    outs = pl.pallas_call(
        body, name=name,
        in_specs=[_HBM] * (2 * m) + [_SEM, _SEM, _ANY], out_specs=[_HBM] * (2 * m),
        out_shape=[_hbm_like(a) for a in shards] + [_hbm_like(a) for a in lands],
        input_output_aliases={i: i for i in range(2 * m)},
        compiler_params=pltpu.CompilerParams(has_side_effects=_EFFECT),
    )(*shards, *lands, send, recv, after)
    return list(outs[m:])


def _forward_to_sibling(name, lands):
    m = len(lands)

    def body(*refs):
        bufs = refs[m:2 * m]
        send_sems, recv_sems = refs[2 * m], refs[2 * m + 1]
        x, y, c = _mesh_pos()
        copies = []
        for j in range(m):
            for k, chip in enumerate(_chip_peers(x, y, c)):
                block = bufs[j].at[_dev_index(chip)]
                cp = pltpu.make_async_remote_copy(
                    src_ref=block, dst_ref=block,
                    send_sem=send_sems.at[3 * j + k], recv_sem=recv_sems.at[3 * j + k],
                    device_id=(x, y, 1 - c), device_id_type=MESH)
                cp.start()
                copies.append(cp)
        for cp in copies:
            cp.wait()

    outs = pl.pallas_call(
        body, name=name,
        in_specs=[_HBM] * m, out_specs=[_HBM] * m,
        out_shape=[_sds(a.shape, a.dtype) for a in lands],
        input_output_aliases={i: i for i in range(m)},
        scratch_shapes=[pltpu.SemaphoreType.DMA((3 * m,)), pltpu.SemaphoreType.DMA((3 * m,))],
    )(*lands)
    return list(outs)


def _token_spec():
    return pl.BlockSpec(memory_space=pltpu.VMEM)


def _pair_start(name, stacks, lands):
    n = len(stacks)

    def body(*refs):
        srcs, dsts = refs[:n], refs[n:2 * n]
        send, recv = refs[2 * n], refs[2 * n + 1]
        token = refs[-1]
        x, y, c = _mesh_pos()
        for w in range(n):
            for chip in range(4):
                pltpu.make_async_remote_copy(
                    src_ref=srcs[w].at[chip, 1 - c], dst_ref=dsts[w].at[chip],
                    send_sem=send.at[4 * w + chip], recv_sem=recv.at[4 * w + chip],
                    device_id=(x, y, 1 - c), device_id_type=MESH).start()
        token[...] = jnp.zeros_like(token)

    sems = pltpu.SemaphoreType.DMA((4 * n,))
    outs = pl.pallas_call(
        body, name=name,
        in_specs=[_HBM] * (2 * n), out_specs=[_SEM, _SEM] + [_HBM] * (2 * n) + [_token_spec()],
        out_shape=[sems, sems] + [_hbm_like(a) for a in stacks] + [_hbm_like(a) for a in lands] + [_sds((8, LANES), F32)],
        input_output_aliases={i: 2 + i for i in range(2 * n)},
        compiler_params=pltpu.CompilerParams(has_side_effects=_EFFECT),
    )(*stacks, *lands)
    return outs[0], outs[1], list(outs[2:2 + n]), list(outs[2 + n:2 + 2 * n]), outs[-1]


def _pair_wait(name, send, recv, stacks, lands, after):
    n = len(stacks)

    def body(*refs):
        srcs, dsts = refs[:n], refs[n:2 * n]
        send_sems, recv_sems = refs[2 * n], refs[2 * n + 1]
        x, y, c = _mesh_pos()
        for w in range(n):
            for chip in range(4):
                cp = pltpu.make_async_remote_copy(
                    src_ref=srcs[w].at[chip, 1 - c], dst_ref=dsts[w].at[chip],
                    send_sem=send_sems.at[4 * w + chip], recv_sem=recv_sems.at[4 * w + chip],
                    device_id=(x, y, 1 - c), device_id_type=MESH)
                cp.wait_send()
                cp.wait_recv()

    outs = pl.pallas_call(
        body, name=name,
        in_specs=[_HBM] * (2 * n) + [_SEM, _SEM, _ANY], out_specs=[_HBM] * (2 * n),
        out_shape=[_hbm_like(a) for a in stacks] + [_hbm_like(a) for a in lands],
        input_output_aliases={i: i for i in range(2 * n)},
        compiler_params=pltpu.CompilerParams(has_side_effects=_EFFECT),
    )(*stacks, *lands, send, recv, after)
    return list(outs[:n]), list(outs[n:])


def _pair_add(name, stack, land, core, tr):
    _, _, r, c = stack.shape

    def body(core_ref, a_ref, b_ref, o_ref):
        o_ref[...] = (a_ref[...].astype(F32) + b_ref[...].astype(F32)).astype(BF16)

    grid_spec = pltpu.PrefetchScalarGridSpec(
        num_scalar_prefetch=1, grid=(4, r // tr),
        in_specs=[pl.BlockSpec((None, None, tr, c), lambda k, i, core_ref: (k, core_ref[0], i, 0)),
                  pl.BlockSpec((None, tr, c), lambda k, i, core_ref: (k, i, 0))],
        out_specs=pl.BlockSpec((None, tr, c), lambda k, i, core_ref: (k, i, 0)))
    return pl.pallas_call(
        body, name=name, grid_spec=grid_spec, out_shape=_sds((4, r, c), BF16),
        compiler_params=_params(("parallel", "parallel"), 3 * tr * c * 2, 3 * tr * c * 4),
    )(core, stack, land)


def _chip_start(name, parts, lands):
    n = len(parts)

    def body(*refs):
        srcs, dsts = refs[:n], refs[n:2 * n]
        send, recv = refs[2 * n], refs[2 * n + 1]
        token = refs[-1]
        x, y, c = _mesh_pos()
        for w in range(n):
            for k, to in enumerate(_chip_peers(x, y, c)):
                pltpu.make_async_remote_copy(
                    src_ref=srcs[w].at[2 * to[0] + to[1]], dst_ref=dsts[w].at[2 * x + y],
                    send_sem=send.at[3 * w + k], recv_sem=recv.at[3 * w + k],
                    device_id=to, device_id_type=MESH).start()
        token[...] = jnp.zeros_like(token)

    sems = pltpu.SemaphoreType.DMA((3 * n,))
    outs = pl.pallas_call(
        body, name=name,
        in_specs=[_HBM] * (2 * n), out_specs=[_SEM, _SEM] + [_HBM] * (2 * n) + [_token_spec()],
        out_shape=[sems, sems] + [_hbm_like(a) for a in parts] + [_hbm_like(a) for a in lands] + [_sds((8, LANES), F32)],
        input_output_aliases={i: 2 + i for i in range(2 * n)},
        compiler_params=pltpu.CompilerParams(has_side_effects=_EFFECT),
    )(*parts, *lands)
    return outs[0], outs[1], list(outs[2:2 + n]), list(outs[2 + n:2 + 2 * n]), outs[-1]


def _chip_wait(name, send, recv, parts, lands, after):
    n = len(parts)

    def body(*refs):
        srcs, dsts = refs[:n], refs[n:2 * n]
        send_sems, recv_sems = refs[2 * n], refs[2 * n + 1]
        x, y, c = _mesh_pos()
        for w in range(n):
            for k, frm in enumerate(_chip_peers(x, y, c)):
                chip = 2 * frm[0] + frm[1]
                cp = pltpu.make_async_remote_copy(
                    src_ref=srcs[w].at[chip], dst_ref=dsts[w].at[chip],
                    send_sem=send_sems.at[3 * w + k], recv_sem=recv_sems.at[3 * w + k],
                    device_id=frm, device_id_type=MESH)
                cp.wait_send()
                cp.wait_recv()

    outs = pl.pallas_call(
        body, name=name,
        in_specs=[_HBM] * (2 * n) + [_SEM, _SEM, _ANY], out_specs=[_HBM] * (2 * n),
        out_shape=[_hbm_like(a) for a in parts] + [_hbm_like(a) for a in lands],
        input_output_aliases={i: i for i in range(2 * n)},
        compiler_params=pltpu.CompilerParams(has_side_effects=_EFFECT),
    )(*parts, *lands, send, recv, after)
    return list(outs[:n]), list(outs[n:])


def _row_tile(t):
    return min(t, 256)


def _rms_fwd(name, x, g):
    t, d = x.shape
    tm = _row_tile(t)

    def epilogue(_, ins, outs):
        xv = ins[0][...]
        r = lax.rsqrt(jnp.mean(xv * xv, axis=-1, keepdims=True) + RMS_EPS)
        outs[0][...] = (xv * r * ins[1][...]).astype(BF16)

    row = pl.BlockSpec((tm, d), lambda i, j, k: (i, 0))
    vec = pl.BlockSpec((1, d), lambda i, j, k: (0, 0))
    return _fused(name, (t // tm, 1, 1), [(x, row), (g, vec)], [(_sds((t, d), BF16), row)], [], epilogue,
                  temp_bytes=4 * tm * d * 4)[0]


def _rms_bwd(name, x, g, dh, resid, deps=()):
    t, d = x.shape
    tm = _row_tile(t)

    def epilogue(_, ins, outs):
        xv, gv, dhv = ins[0][...], ins[1][...], ins[2][...]
        r = lax.rsqrt(jnp.mean(xv * xv, axis=-1, keepdims=True) + RMS_EPS)
        xh = xv * r
        u = dhv * gv
        dot = jnp.mean(u * xh, axis=-1, keepdims=True)
        outs[0][...] = ins[3][...] + r * (u - xh * dot)

        @pl.when(pl.program_id(0) == 0)
        def _():
            outs[1][...] = jnp.zeros_like(outs[1])

        outs[1][0:1, :] += jnp.sum(dhv * xh, axis=0, keepdims=True)

    row = pl.BlockSpec((tm, d), lambda i, j, k: (i, 0))
    vec = pl.BlockSpec((1, d), lambda i, j, k: (0, 0))
    acc = pl.BlockSpec((8, d), lambda i, j, k: (0, 0))
    return _fused(name, (t // tm, 1, 1), [(x, row), (g, vec), (dh, row), (resid, row)],
                  [(_sds((t, d), F32), row), (_sds((8, d), F32), acc)], [], epilogue,
                  temp_bytes=6 * tm * d * 4, semantics=("arbitrary", "arbitrary", "arbitrary"), deps=deps)


def _loss_dy(y, target):
    t, d = y.shape
    tm = _row_tile(t)

    def epilogue(_, ins, outs):
        e = ins[0][...] - ins[1][...]
        outs[0][...] = e * (1.0 / d)

        @pl.when(pl.program_id(0) == 0)
        def _():
            outs[1][...] = jnp.zeros_like(outs[1])

        part = jnp.sum(jnp.sum(e * e, axis=1, keepdims=True), axis=0, keepdims=True)
        outs[1][...] += jnp.broadcast_to(part, outs[1].shape)

    row = pl.BlockSpec((tm, d), lambda i, j, k: (i, 0))
    acc = pl.BlockSpec((8, LANES), lambda i, j, k: (0, 0))
    return _fused("loss_dy", (t // tm, 1, 1), [(y, row), (target, row)],
                  [(_sds((t, d), F32), row), (_sds((8, LANES), F32), acc)], [], epilogue,
                  temp_bytes=3 * tm * d * 4, semantics=("arbitrary", "arbitrary", "arbitrary"))


def _ffn_up(name, h, wgu):
    t, d = h.shape
    nb = wgu.shape[2]
    f = 4 * nb
    tm = _row_tile(t)

    def body(h_ref, wg_ref, wu_ref, gu_ref, a_ref):
        hv = h_ref[...]
        g = jnp.dot(hv, wg_ref[...], preferred_element_type=F32)
        u = jnp.dot(hv, wu_ref[...], preferred_element_type=F32)
        gu_ref[0] = g.astype(BF16)
        gu_ref[1] = u.astype(BF16)
        a_ref[...] = (g * _sigmoid(g) * u).astype(BF16)

    blocks = tm * d * 2 + 2 * d * nb * 2 + 3 * tm * nb * 2
    return pl.pallas_call(
        body, name=name, grid=(4, t // tm),
        in_specs=[pl.BlockSpec((tm, d), lambda j, i: (i, 0)),
                  pl.BlockSpec((None, d, nb), lambda j, i: (j, 0, 0)),
                  pl.BlockSpec((None, d, nb), lambda j, i: (j + 4, 0, 0))],
        out_specs=[pl.BlockSpec((2, tm, nb), lambda j, i: (0, i, j)),
                   pl.BlockSpec((tm, nb), lambda j, i: (i, j))],
        out_shape=[_sds((2, t, f), BF16), _sds((t, f), BF16)],
        compiler_params=_params(("parallel", "parallel"), blocks, 5 * tm * nb * 4),
    )(h, wgu, wgu)


def _ffn_down(name, a, wd, x):
    t, f = a.shape
    d = wd.shape[1]
    tm = min(t, 512)
    tk = f // 4

    def epilogue(acc, ins, outs):
        outs[0][...] = ins[2][...] + 0.5 * acc

    return _fused(name, (t // tm, 1, 4),
                  [(a, pl.BlockSpec((tm, tk), lambda i, j, k: (i, k))),
                   (wd, pl.BlockSpec((tk, d), lambda i, j, k: (k, 0))),
                   (x, pl.BlockSpec((tm, d), lambda i, j, k: (i, 0)))],
                  [(_sds((t, d), F32), pl.BlockSpec((tm, d), lambda i, j, k: (i, 0)))],
                  [(0, 1, NN)], epilogue, nk=4, acc_shape=(tm, d), temp_bytes=2 * tm * d * 4)[0]


def _ffn_bwd_act(name, dy, wd, gu):
    t, d = dy.shape
    f = wd.shape[0]
    nb = f // 4
    tm = _row_tile(t)

    def body(dy_ref, wd_ref, gu_ref, dgu_ref):
        da = 0.5 * lax.dot_general(dy_ref[...].astype(BF16), wd_ref[...], NT, preferred_element_type=F32)
        g = gu_ref[0].astype(F32)
        u = gu_ref[1].astype(F32)
        s = _sigmoid(g)
        dgu_ref[0] = (da * u * (s * (1.0 + g * (1.0 - s)))).astype(BF16)
        dgu_ref[1] = (da * (g * s)).astype(BF16)

    blocks = tm * d * 4 + nb * d * 2 + 4 * tm * nb * 2
    return pl.pallas_call(
        body, name=name, grid=(4, t // tm),
        in_specs=[pl.BlockSpec((tm, d), lambda j, i: (i, 0)),
                  pl.BlockSpec((nb, d), lambda j, i: (j, 0)),
                  pl.BlockSpec((2, tm, nb), lambda j, i: (0, i, j))],
        out_specs=pl.BlockSpec((2, tm, nb), lambda j, i: (0, i, j)),
        out_shape=_sds((2, t, f), BF16),
        compiler_params=_params(("parallel", "parallel"), blocks, 6 * tm * nb * 4),
    )(dy, wd, gu)


def _ffn_bwd_dwd(name, a, dy):
    t, f = a.shape
    d = dy.shape[1]
    tm = f // 4
    tk = min(t, 512)

    def epilogue(acc, ins, outs):
        outs[0][...] = (0.5 * acc).astype(BF16)

    return _fused(name, (4, 1, t // tk),
                  [(a, pl.BlockSpec((tk, tm), lambda i, j, k: (k, i))),
                   (dy, pl.BlockSpec((tk, d), lambda i, j, k: (k, 0)))],
                  [(_sds((f, d), BF16), pl.BlockSpec((tm, d), lambda i, j, k: (i, 0)))],
                  [(0, 1, TN)], epilogue, nk=t // tk, acc_shape=(tm, d), temp_bytes=2 * tm * d * 4)[0]


def _ffn_bwd_dh(name, dgu, wgu, deps=()):
    _, t, f = dgu.shape
    d, nb = wgu.shape[1], wgu.shape[2]
    tm = min(t, 512)

    def epilogue(acc, ins, outs):
        outs[0][...] = acc

    return _fused(name, (t // tm, 1, N_DEV),
                  [(dgu, pl.BlockSpec((None, tm, nb), lambda i, j, k: (k // 4, i, k % 4))),
                   (wgu, pl.BlockSpec((None, d, nb), lambda i, j, k: (k, 0, 0)))],
                  [(_sds((t, d), F32), pl.BlockSpec((tm, d), lambda i, j, k: (i, 0)))],
                  [(0, 1, NT)], epilogue, nk=N_DEV, acc_shape=(tm, d), temp_bytes=tm * d * 4, deps=deps)[0]


def _ffn_bwd_dwgu(name, h, dgu):
    t, d = h.shape
    nb = dgu.shape[2] // 4
    tk = min(t, 512)

    def epilogue(acc, ins, outs):
        outs[0][...] = acc.astype(BF16)

    return _fused(name, (N_DEV, 1, t // tk),
                  [(h, pl.BlockSpec((tk, d), lambda i, j, k: (k, 0))),
                   (dgu, pl.BlockSpec((None, tk, nb), lambda i, j, k: (i // 4, k, i % 4)))],
                  [(_sds((N_DEV, d, nb), BF16), pl.BlockSpec((None, d, nb), lambda i, j, k: (i, 0, 0)))],
                  [(0, 1, TN)], epilogue, nk=t // tk, acc_shape=(d, nb), temp_bytes=d * nb * 4)[0]


def _proj(h, w_in):
    t, d = h.shape
    n = w_in.shape[1]
    tn = n // 4
    tm = min(t, 512)

    def epilogue(acc, ins, outs):
        outs[0][...] = acc

    return _fused("mix_proj", (4, t // tm, 1),
                  [(h, pl.BlockSpec((tm, d), lambda j, i, k: (i, 0))),
                   (w_in, pl.BlockSpec((d, tn), lambda j, i, k: (0, j)))],
                  [(_sds((t, n), F32), pl.BlockSpec((tm, tn), lambda j, i, k: (i, j)))],
                  [(0, 1, NN)], epilogue, temp_bytes=2 * tm * tn * 4)[0]


def _shift_rows(u, k):
    t = u.shape[0]
    rolled = pltpu.roll(u, k % t, axis=0)
    row = lax.broadcasted_iota(jnp.int32, u.shape, 0)
    keep = (row >= k) if k > 0 else (row < t + k)
    return jnp.where(keep, rolled, 0.0)


def _conv_fwd(proj, conv_w):
    t = proj.shape[0]
    cw = conv_w.shape[1]
    tc = min(cw, 256)
    nc = cw // tc

    def epilogue(_, ins, outs):
        u = ins[2][...] * ins[0][...]
        w = ins[3][...]
        y = u * w[2:3, :] + _shift_rows(u, 1) * w[1:2, :] + _shift_rows(u, 2) * w[0:1, :]
        outs[0][...] = (ins[1][...] * y).astype(BF16)

    def col(seg):
        return pl.BlockSpec((t, tc), lambda i, j, k: (0, seg * nc + i))

    return _fused("conv_fwd", (nc, 1, 1),
                  [(proj, col(0)), (proj, col(1)), (proj, col(2)),
                   (conv_w, pl.BlockSpec((8, tc), lambda i, j, k: (0, i)))],
                  [(_sds((t, cw), BF16), pl.BlockSpec((t, tc), lambda i, j, k: (0, i)))],
                  [], epilogue, temp_bytes=6 * t * tc * 4)[0]


def _conv_bwd(proj, conv_w, dca, deps=()):
    t = proj.shape[0]
    cw = conv_w.shape[1]
    tc = min(cw, 256)
    nc = cw // tc

    def epilogue(_, ins, outs):
        xc, bg, cg, w, dc = ins[0][...], ins[1][...], ins[2][...], ins[3][...], ins[4][...]
        u = cg * xc
        u1, u2 = _shift_rows(u, 1), _shift_rows(u, 2)
        y = u * w[2:3, :] + u1 * w[1:2, :] + u2 * w[0:1, :]
        dconv = dc * bg
        du = dconv * w[2:3, :] + _shift_rows(dconv, -1) * w[1:2, :] + _shift_rows(dconv, -2) * w[0:1, :]
        outs[0][0] = (du * cg).astype(BF16)
        outs[0][1] = (dc * y).astype(BF16)
        outs[0][2] = (du * xc).astype(BF16)
        outs[1][...] = jnp.zeros_like(outs[1])
        outs[1][0:1, :] = jnp.sum(dconv * u2, axis=0, keepdims=True)
        outs[1][1:2, :] = jnp.sum(dconv * u1, axis=0, keepdims=True)
        outs[1][2:3, :] = jnp.sum(dconv * u, axis=0, keepdims=True)

    def col(seg):
        return pl.BlockSpec((t, tc), lambda i, j, k: (0, seg * nc + i))

    own = pl.BlockSpec((t, tc), lambda i, j, k: (0, i))
    wspec = pl.BlockSpec((8, tc), lambda i, j, k: (0, i))
    return _fused("conv_bwd", (nc, 1, 1),
                  [(proj, col(0)), (proj, col(1)), (proj, col(2)), (conv_w, wspec), (dca, own)],
                  [(_sds((3, t, cw), BF16), pl.BlockSpec((3, t, tc), lambda i, j, k: (0, 0, i))),
                   (_sds((8, cw), F32), wspec)],
                  [], epilogue, temp_bytes=10 * t * tc * 4, deps=deps)


def _split3(x):
    hi = x.astype(BF16)
    r1 = x - hi.astype(F32)
    mid = r1.astype(BF16)
    lo = (r1 - mid.astype(F32)).astype(BF16)
    return hi, mid, lo


def _head_selector(width):
    r = lax.broadcasted_iota(jnp.int32, (width, LANES), 0)
    c = lax.broadcasted_iota(jnp.int32, (width, LANES), 1)
    return (lax.shift_right_logical(r, 6) == c).astype(BF16)


def _head_sum(x, sel):
    return sum(jnp.dot(p, sel, preferred_element_type=F32) for p in _split3(x))


def _head_bcast(r, sel):
    return sum(lax.dot_general(p, sel, NT, preferred_element_type=F32) for p in _split3(r))


def _rope(x, c, sa, sb):
    n = x.shape[1]
    return x * c + pltpu.roll(x, n - ROT_DIM // 2, axis=1) * sa + pltpu.roll(x, ROT_DIM // 2, axis=1) * sb


def _rope_t(d, c, sa, sb):
    n = d.shape[1]
    return d * c + pltpu.roll(d * sa, ROT_DIM // 2, axis=1) + pltpu.roll(d * sb, n - ROT_DIM // 2, axis=1)


def _tile_lanes(tab, width):
    return tab if width == tab.shape[1] else jnp.tile(tab, (1, width // tab.shape[1]))


def _qk_prep(proj, gq, gk, rope_tabs, cw, kw):
    t = proj.shape[0]
    tm = _row_tile(t)

    def epilogue(_, ins, outs):
        c, sa, sb = ins[5][...], ins[6][...], ins[7][...]
        for src, gain, dst, width in ((0, 3, 0, cw), (1, 4, 1, kw)):
            xv = ins[src][...]
            sel = _head_selector(width)
            r = lax.rsqrt(_head_sum(xv * xv, sel) * (1.0 / HEAD_DIM) + RMS_EPS)
            xn = xv * _head_bcast(r, sel) * ins[gain][...]
            outs[dst][...] = _rope(xn, _tile_lanes(c, width), _tile_lanes(sa, width), _tile_lanes(sb, width)).astype(BF16)
        outs[2][...] = ins[2][...].astype(BF16)

    kblk = cw // kw
    tab = pl.BlockSpec((tm, LANES), lambda i, j, k: (i, 0))
    kspec = pl.BlockSpec((tm, kw), lambda i, j, k: (i, 0))
    return _fused("qk_prep", (t // tm, 1, 1),
                  [(proj, pl.BlockSpec((tm, cw), lambda i, j, k: (i, 3))),
                   (proj, pl.BlockSpec((tm, kw), lambda i, j, k: (i, 4 * kblk))),
                   (proj, pl.BlockSpec((tm, kw), lambda i, j, k: (i, 4 * kblk + 1))),
                   (gq, pl.BlockSpec((1, cw), lambda i, j, k: (0, 0))),
                   (gk, pl.BlockSpec((1, kw), lambda i, j, k: (0, 0))),
                   (rope_tabs[0], tab), (rope_tabs[1], tab), (rope_tabs[2], tab)],
                  [(_sds((t, cw), BF16), pl.BlockSpec((tm, cw), lambda i, j, k: (i, 0))),
                   (_sds((t, kw), BF16), kspec), (_sds((t, kw), BF16), kspec)],
                  [], epilogue, temp_bytes=12 * tm * cw * 4)


def _qk_prep_bwd(proj, gq, gk, rope_tabs, dq, dkc, dkp, dvc, dvp, cw, kw):
    t = proj.shape[0]
    tm = BLOCK
    nblk = t // tm

    def epilogue(_, ins, outs):
        c, sa, sb = ins[5][...], ins[6][...], ins[7][...]
        has_next = (pl.program_id(0) < nblk - 1).astype(F32)
        dk = ins[9][...] + has_next * ins[10][...]
        dv = ins[11][...] + has_next * ins[12][...]
        pieces = []
        for src, gain, dval, dst, width in ((0, 3, ins[8][...], 1, cw), (1, 4, dk, 2, kw)):
            xv, gv = ins[src][...], ins[gain][...]
            sel = _head_selector(width)
            r = _head_bcast(lax.rsqrt(_head_sum(xv * xv, sel) * (1.0 / HEAD_DIM) + RMS_EPS), sel)
            xh = xv * r
            dxn = _rope_t(dval, _tile_lanes(c, width), _tile_lanes(sa, width), _tile_lanes(sb, width))
            u = dxn * gv
            dot = _head_bcast(_head_sum(u * xh, sel), sel) * (1.0 / HEAD_DIM)
            pieces.append((r * (u - xh * dot)).astype(BF16))
            ri = lax.broadcasted_iota(jnp.int32, (width, LANES), 0)
            ci = lax.broadcasted_iota(jnp.int32, (width, LANES), 1)
            fold = (lax.bitwise_and(ri, HEAD_DIM - 1) == ci).astype(BF16)
            colsum = jnp.broadcast_to(jnp.sum(dxn * xh, axis=0, keepdims=True), (8, width))
            part = sum(jnp.dot(p, fold, preferred_element_type=F32) for p in _split3(colsum))

            @pl.when(pl.program_id(0) == 0)
            def _():
                outs[dst][...] = jnp.zeros_like(outs[dst])

            outs[dst][0:1, :] += part[0:1, :]
        outs[0][:, 0:cw] = pieces[0]
        outs[0][:, cw:cw + kw] = pieces[1]
        outs[0][:, cw + kw:cw + 2 * kw] = dv.astype(BF16)

    kblk = cw // kw
    tab = pl.BlockSpec((tm, LANES), lambda i, j, k: (i, 0))
    kcur = pl.BlockSpec((tm, kw), lambda i, j, k: (i, 0))
    knext = pl.BlockSpec((tm, kw), lambda i, j, k: (jnp.minimum(i + 1, nblk - 1), 0))
    acc = pl.BlockSpec((8, LANES), lambda i, j, k: (0, 0))
    return _fused("qk_prep_bwd", (nblk, 1, 1),
                  [(proj, pl.BlockSpec((tm, cw), lambda i, j, k: (i, 3))),
                   (proj, pl.BlockSpec((tm, kw), lambda i, j, k: (i, 4 * kblk))),
                   (proj, pl.BlockSpec((tm, kw), lambda i, j, k: (i, 4 * kblk + 1))),
                   (gq, pl.BlockSpec((1, cw), lambda i, j, k: (0, 0))),
                   (gk, pl.BlockSpec((1, kw), lambda i, j, k: (0, 0))),
                   (rope_tabs[0], tab), (rope_tabs[1], tab), (rope_tabs[2], tab),
                   (dq, pl.BlockSpec((tm, cw), lambda i, j, k: (i, 0))),
                   (dkc, kcur), (dkp, knext), (dvc, kcur), (dvp, knext)],
                  [(_sds((t, cw + 2 * kw), BF16), pl.BlockSpec((tm, cw + 2 * kw), lambda i, j, k: (i, 0))),
                   (_sds((8, LANES), F32), acc), (_sds((8, LANES), F32), acc)],
                  [], epilogue, temp_bytes=16 * tm * cw * 4, semantics=("arbitrary", "arbitrary", "arbitrary"))


def _attn_mask(n):
    row = lax.broadcasted_iota(jnp.int32, (BLOCK, 2 * BLOCK), 0)
    col = lax.broadcasted_iota(jnp.int32, (BLOCK, 2 * BLOCK), 1)
    return (col > row) & (col <= row + BLOCK) & ((col >= BLOCK) | (n > 0))


def _softmax_with_sink(q, k2, sink, valid):
    s = lax.dot_general(q, k2, NT, preferred_element_type=F32) * ATTN_SCALE
    s = jnp.where(valid, s, NEG_INF)
    m = jnp.maximum(jnp.max(s, axis=-1, keepdims=True), sink)
    p = jnp.exp(s - m)
    es = jnp.exp(sink - m)
    denom = jnp.sum(p, axis=-1, keepdims=True) + es
    return p / denom, es / denom


def _attn_fwd(qn, kn, vb, sink_rows):
    t, cw = qn.shape
    kw = kn.shape[1]
    nkv = kw // HEAD_DIM

    def body(q_ref, kp_ref, kc_ref, vp_ref, vc_ref, sink_ref, o_ref):
        valid = _attn_mask(pl.program_id(0))
        qv = q_ref[...]
        kp, kc, vp, vc = kp_ref[...], kc_ref[...], vp_ref[...], vc_ref[...]
        outs = []
        for h in range(nkv):
            hs = slice(h * HEAD_DIM, (h + 1) * HEAD_DIM)
            k2 = jnp.concatenate([kp[:, hs], kc[:, hs]], axis=0)
            v2 = jnp.concatenate([vp[:, hs], vc[:, hs]], axis=0)
            for g in range(GROUP):
                hq = h * GROUP + g
                pn, _ = _softmax_with_sink(qv[:, hq * HEAD_DIM:(hq + 1) * HEAD_DIM], k2, sink_ref[hq:hq + 1, 0:1], valid)
                outs.append(jnp.dot(pn.astype(BF16), v2, preferred_element_type=F32))
        o_ref[...] = jnp.concatenate(outs, axis=-1).astype(BF16)

    cur = lambda n: (n, 0)
    prev = lambda n: (jnp.maximum(n - 1, 0), 0)
    return pl.pallas_call(
        body, name="attn_fwd", grid=(t // BLOCK,),
        in_specs=[pl.BlockSpec((BLOCK, cw), cur),
                  pl.BlockSpec((BLOCK, kw), prev), pl.BlockSpec((BLOCK, kw), cur),
                  pl.BlockSpec((BLOCK, kw), prev), pl.BlockSpec((BLOCK, kw), cur),
                  pl.BlockSpec(sink_rows.shape, lambda n: (0, 0))],
        out_specs=pl.BlockSpec((BLOCK, cw), cur),
        out_shape=_sds((t, cw), BF16),
        compiler_params=_params(("parallel",), BLOCK * (cw + 4 * kw) * 2 + BLOCK * cw * 2, 8 << 20),
    )(qn, kn, kn, vb, vb, sink_rows)


def _attn_bwd(qn, kn, vb, sink_rows, do):
    t, cw = qn.shape
    kw = kn.shape[1]
    nkv = kw // HEAD_DIM
    nq = nkv * GROUP

    def body(q_ref, kp_ref, kc_ref, vp_ref, vc_ref, sink_ref, do_ref,
             dq_ref, dkc_ref, dkp_ref, dvc_ref, dvp_ref, dsink_ref):
        n = pl.program_id(0)
        valid = _attn_mask(n)
        qv, dov = q_ref[...], do_ref[...]
        kp, kc, vp, vc = kp_ref[...], kc_ref[...], vp_ref[...], vc_ref[...]
        dqs, dks, dvs, dsinks = [], [], [], []
        for h in range(nkv):
            hs = slice(h * HEAD_DIM, (h + 1) * HEAD_DIM)
            k2 = jnp.concatenate([kp[:, hs], kc[:, hs]], axis=0)
            v2 = jnp.concatenate([vp[:, hs], vc[:, hs]], axis=0)
            dk2 = jnp.zeros((2 * BLOCK, HEAD_DIM), F32)
            dv2 = jnp.zeros((2 * BLOCK, HEAD_DIM), F32)
            for g in range(GROUP):
                hq = h * GROUP + g
                qs = slice(hq * HEAD_DIM, (hq + 1) * HEAD_DIM)
                q = qv[:, qs]
                pn, psink = _softmax_with_sink(q, k2, sink_ref[hq:hq + 1, 0:1], valid)
                pb = pn.astype(BF16)
                dob = dov[:, qs].astype(BF16)
                dpn = lax.dot_general(dob, v2, NT, preferred_element_type=F32)
                dv2 = dv2 + lax.dot_general(pb, dob, TN, preferred_element_type=F32)
                delta = jnp.sum(pn * dpn, axis=-1, keepdims=True)
                ds = (pn * (dpn - delta) * ATTN_SCALE).astype(BF16)
                dqs.append(jnp.dot(ds, k2, preferred_element_type=F32))
                dk2 = dk2 + lax.dot_general(ds, q, TN, preferred_element_type=F32)
                dsinks.append(jnp.broadcast_to(jnp.sum(-psink * delta, axis=0, keepdims=True), (1, LANES)))
            dks.append(dk2)
            dvs.append(dv2)
        dq_ref[...] = jnp.concatenate(dqs, axis=-1)
        dkp_ref[...] = jnp.concatenate([d[:BLOCK] for d in dks], axis=-1)
        dkc_ref[...] = jnp.concatenate([d[BLOCK:] for d in dks], axis=-1)
        dvp_ref[...] = jnp.concatenate([d[:BLOCK] for d in dvs], axis=-1)
        dvc_ref[...] = jnp.concatenate([d[BLOCK:] for d in dvs], axis=-1)

        @pl.when(n == 0)
        def _():
            dsink_ref[...] = jnp.zeros_like(dsink_ref)

        dsink_ref[...] += jnp.concatenate(dsinks, axis=0)

    cur = lambda n: (n, 0)
    prev = lambda n: (jnp.maximum(n - 1, 0), 0)
    kspec = pl.BlockSpec((BLOCK, kw), cur)
    return pl.pallas_call(
        body, name="attn_bwd", grid=(t // BLOCK,),
        in_specs=[pl.BlockSpec((BLOCK, cw), cur),
                  pl.BlockSpec((BLOCK, kw), prev), kspec,
                  pl.BlockSpec((BLOCK, kw), prev), kspec,
                  pl.BlockSpec(sink_rows.shape, lambda n: (0, 0)),
                  pl.BlockSpec((BLOCK, cw), cur)],
        out_specs=[pl.BlockSpec((BLOCK, cw), cur), kspec, kspec, kspec, kspec,
                   pl.BlockSpec((nq, LANES), lambda n: (0, 0))],
        out_shape=[_sds((t, cw), F32)] + [_sds((t, kw), F32)] * 4 + [_sds((nq, LANES), F32)],
        compiler_params=_params(("arbitrary",), BLOCK * (cw + 4 * kw) * 2 + 2 * BLOCK * cw * 4 + 4 * BLOCK * kw * 4, 12 << 20),
    )(qn, kn, kn, vb, vb, sink_rows, do)


def _mix_out(ca, o, woc, woa, proj):
    t, cw = ca.shape
    nb = woc.shape[2]
    d = N_DEV * nb
    tm = min(t, 1024)
    ga0 = (3 * cw + cw + 2 * (cw // 4)) // nb

    def body(ca_ref, o_ref, woc_ref, woa_ref, ga_ref, gb_ref, m_ref, ya_ref, yb_ref):
        ya = jnp.dot(ca_ref[...], woc_ref[...], preferred_element_type=F32)
        yb = jnp.dot(o_ref[...], woa_ref[...], preferred_element_type=F32)
        ya_ref[...] = ya.astype(BF16)
        yb_ref[...] = yb.astype(BF16)
        m_ref[...] = (_sigmoid(ga_ref[...]) * ya + _sigmoid(gb_ref[...]) * yb).astype(BF16)

    act = pl.BlockSpec((tm, cw), lambda i, j: (i, 0))
    wsp = pl.BlockSpec((None, cw, nb), lambda i, j: (j, 0, 0))
    osp = pl.BlockSpec((tm, nb), lambda i, j: (i, j))
    blocks = 2 * tm * cw * 2 + 2 * cw * nb * 2 + 2 * tm * nb * 4 + 3 * tm * nb * 2
    return pl.pallas_call(
        body, name="mix_out", grid=(t // tm, N_DEV),
        in_specs=[act, act, wsp, wsp,
                  pl.BlockSpec((tm, nb), lambda i, j: (i, ga0 + j)),
                  pl.BlockSpec((tm, nb), lambda i, j: (i, ga0 + N_DEV + j))],
        out_specs=[osp, osp, osp],
        out_shape=[_sds((t, d), BF16)] * 3,
        compiler_params=_params(("parallel", "parallel"), blocks, 6 * tm * nb * 4),
    )(ca, o, woc, woa, proj, proj)


def _mix_residual(merged, wo, x):
    t, d = x.shape
    tm = min(t, 512)

    def epilogue(acc, ins, outs):
        outs[0][...] = ins[2][...] + acc

    row = pl.BlockSpec((tm, d), lambda i, j, k: (i, 0))
    return _fused("mix_residual", (t // tm, 1, 1),
                  [(merged, row), (wo, pl.BlockSpec((d, d), lambda i, j, k: (0, 0))), (x, row)],
                  [(_sds((t, d), F32), row)], [(0, 1, NN)], epilogue, temp_bytes=2 * tm * d * 4)[0]


def _mix_bwd_gates(dx, wo, ya, yb, proj, cw):
    t, d = dx.shape
    tm = min(t, 512)
    tn = min(d, 512)
    ga0 = (4 * cw + 2 * (cw // 4)) // tn

    def epilogue(acc, ins, outs):
        sa, sb = _sigmoid(ins[4][...]), _sigmoid(ins[5][...])
        outs[0][...] = (acc * sa).astype(BF16)
        outs[1][...] = (acc * sb).astype(BF16)
        outs[2][0] = (acc * ins[2][...].astype(F32) * sa * (1.0 - sa)).astype(BF16)
        outs[2][1] = (acc * ins[3][...].astype(F32) * sb * (1.0 - sb)).astype(BF16)

    blk = pl.BlockSpec((tm, tn), lambda i, j, k: (i, j))
    return _fused("mix_bwd_gates", (t // tm, d // tn, 1),
                  [(dx, pl.BlockSpec((tm, d), lambda i, j, k: (i, 0))),
                   (wo, pl.BlockSpec((tn, d), lambda i, j, k: (j, 0))),
                   (ya, blk), (yb, blk),
                   (proj, pl.BlockSpec((tm, tn), lambda i, j, k: (i, ga0 + j))),
                   (proj, pl.BlockSpec((tm, tn), lambda i, j, k: (i, ga0 + d // tn + j)))],
                  [(_sds((t, d), BF16), blk), (_sds((t, d), BF16), blk),
                   (_sds((2, t, d), BF16), pl.BlockSpec((2, tm, tn), lambda i, j, k: (0, i, j)))],
                  [(0, 1, NT)], epilogue, temp_bytes=8 * tm * tn * 4)


def _tn_matmul(name, a, b, tm, out_dtype=BF16):
    t, m = a.shape
    n = b.shape[1]
    tk = min(t, 512)

    def epilogue(acc, ins, outs):
        outs[0][...] = acc.astype(out_dtype)

    return _fused(name, (m // tm, 1, t // tk),
                  [(a, pl.BlockSpec((tk, tm), lambda i, j, k: (k, i))),
                   (b, pl.BlockSpec((tk, n), lambda i, j, k: (k, 0)))],
                  [(_sds((m, n), out_dtype), pl.BlockSpec((tm, n), lambda i, j, k: (i, 0)))],
                  [(0, 1, TN)], epilogue, nk=t // tk, acc_shape=(tm, n), temp_bytes=tm * n * 4)[0]


def _out_proj_bwd_act(name, dy, w, deps=()):
    t, d = dy.shape
    kdim, nb = w.shape[1], w.shape[2]
    tm = min(t, 1024)

    def epilogue(acc, ins, outs):
        outs[0][...] = acc

    return _fused(name, (t // tm, 1, N_DEV),
                  [(dy, pl.BlockSpec((tm, nb), lambda i, j, k: (i, k))),
                   (w, pl.BlockSpec((None, kdim, nb), lambda i, j, k: (k, 0, 0)))],
                  [(_sds((t, kdim), F32), pl.BlockSpec((tm, kdim), lambda i, j, k: (i, 0)))],
                  [(0, 1, NT)], epilogue, nk=N_DEV, acc_shape=(tm, kdim), temp_bytes=tm * kdim * 4, deps=deps)[0]


def _out_proj_bwd_w(name, act, dy, nb):
    t, kdim = act.shape
    tk = min(t, 1024)

    def epilogue(acc, ins, outs):
        outs[0][...] = acc.astype(BF16)

    return _fused(name, (N_DEV, 1, t // tk),
                  [(act, pl.BlockSpec((tk, kdim), lambda i, j, k: (k, 0))),
                   (dy, pl.BlockSpec((tk, nb), lambda i, j, k: (k, i)))],
                  [(_sds((N_DEV, kdim, nb), BF16), pl.BlockSpec((None, kdim, nb), lambda i, j, k: (i, 0, 0)))],
                  [(0, 1, TN)], epilogue, nk=t // tk, acc_shape=(kdim, nb), temp_bytes=kdim * nb * 4)[0]


def _proj_bwd_act(dproj, w_in, deps=()):
    t, n = dproj.shape
    d = w_in.shape[0]
    tm = min(t, 512)
    tk = n // 4

    def epilogue(acc, ins, outs):
        outs[0][...] = acc

    return _fused("mix_bwd_dh", (t // tm, 1, 4),
                  [(dproj, pl.BlockSpec((tm, tk), lambda i, j, k: (i, k))),
                   (w_in, pl.BlockSpec((d, tk), lambda i, j, k: (0, k)))],
                  [(_sds((t, d), F32), pl.BlockSpec((tm, d), lambda i, j, k: (i, 0)))],
                  [(0, 1, NT)], epilogue, nk=4, acc_shape=(tm, d), temp_bytes=tm * d * 4, deps=deps)[0]


def _proj_bwd_w(h, dproj):
    t, d = h.shape
    n = dproj.shape[1]
    tn = n // 4
    tm = min(d, 1024)
    tk = min(t, 512)

    def epilogue(acc, ins, outs):
        outs[0][...] = acc.astype(BF16)

    return _fused("mix_bwd_dwin", (d // tm, 4, t // tk),
                  [(h, pl.BlockSpec((tk, tm), lambda i, j, k: (k, i))),
                   (dproj, pl.BlockSpec((tk, tn), lambda i, j, k: (k, j)))],
                  [(_sds((d, n), BF16), pl.BlockSpec((tm, tn), lambda i, j, k: (i, j)))],
                  [(0, 1, TN)], epilogue, nk=t // tk, acc_shape=(tm, tn), temp_bytes=tm * tn * 4)[0]


def _adamw_math(w, g, m, v):
    m = ADAM_B1 * m + (1.0 - ADAM_B1) * g
    v = ADAM_B2 * v + (1.0 - ADAM_B2) * (g * g)
    m_hat = m / (1.0 - ADAM_B1 ** ADAM_STEP)
    v_hat = v / (1.0 - ADAM_B2 ** ADAM_STEP)
    delta = -ADAM_LR * (m_hat / (jnp.sqrt(v_hat) + ADAM_EPS) + ADAM_WD * w)
    return delta, m, v


def _adamw(name, parts, w, m, v, tr):
    r, c = w.shape

    def body(p_ref, w_ref, m_ref, v_ref, g_out, d_out, m_out, v_out):
        g = p_ref[0].astype(F32)
        for s in range(1, N_DEV):
            g = g + p_ref[s].astype(F32)
        delta, mn, vn = _adamw_math(w_ref[...], g, m_ref[...], v_ref[...])
        g_out[...] = g
        d_out[...] = delta
        m_out[...] = mn
        v_out[...] = vn

    blk = pl.BlockSpec((tr, c), lambda i: (i, 0))
    blocks = N_DEV * tr * c * parts.dtype.itemsize + 7 * tr * c * 4
    return pl.pallas_call(
        body, name=name, grid=(r // tr,),
        in_specs=[pl.BlockSpec((N_DEV, tr, c), lambda i: (0, i, 0)), blk, blk, blk],
        out_specs=[blk] * 4, out_shape=[_sds((r, c), F32)] * 4,
        compiler_params=_params(("parallel",), blocks, 6 * tr * c * 4),
    )(parts, w, m, v)


def _adamw_chips(name, chip, own, landed, w, m, v, tr):
    r, c = w.shape

    def body(chip_ref, own_ref, land_ref, w_ref, m_ref, v_ref, g_out, d_out, m_out, v_out):
        mine = own_ref[...].astype(F32)
        g = jnp.zeros((tr, c), F32)
        for k in range(4):
            g = g + jnp.where(chip_ref[0] == k, mine, land_ref[k].astype(F32))
        delta, mn, vn = _adamw_math(w_ref[...], g, m_ref[...], v_ref[...])
        g_out[...] = g
        d_out[...] = delta
        m_out[...] = mn
        v_out[...] = vn

    blk = pl.BlockSpec((tr, c), lambda i, chip_ref: (i, 0))
    grid_spec = pltpu.PrefetchScalarGridSpec(
        num_scalar_prefetch=1, grid=(r // tr,),
        in_specs=[pl.BlockSpec((None, tr, c), lambda i, chip_ref: (chip_ref[0], i, 0)),
                  pl.BlockSpec((4, tr, c), lambda i, chip_ref: (0, i, 0)), blk, blk, blk],
        out_specs=[blk] * 4)
    blocks = 5 * tr * c * 2 + 7 * tr * c * 4
    return pl.pallas_call(
        body, name=name, grid_spec=grid_spec, out_shape=[_sds((r, c), F32)] * 4,
        compiler_params=_params(("parallel",), blocks, 6 * tr * c * 4),
    )(chip, own, landed, w, m, v)


def _rope_tables(t):
    half = ROT_DIM // 2
    inv_freq = 1.0 / (ROPE_THETA ** (jnp.arange(0, ROT_DIM, 2, dtype=F32) / ROT_DIM))
    ang = jnp.arange(t, dtype=F32)[:, None] * inv_freq[None, :]
    cos, sin = jnp.cos(ang), jnp.sin(ang)
    ones = jnp.ones((t, HEAD_DIM - ROT_DIM), F32)
    zeros = jnp.zeros((t, HEAD_DIM - half), F32)
    c = jnp.concatenate([cos, cos, ones], axis=1)
    sa = jnp.concatenate([-sin, zeros], axis=1)
    sb = jnp.concatenate([jnp.zeros((t, half), F32), sin, jnp.zeros((t, HEAD_DIM - ROT_DIM), F32)], axis=1)
    return tuple(jnp.tile(a, (1, LANES // HEAD_DIM)) for a in (c, sa, sb))


def _pad_rows(a, rows=8):
    return jnp.pad(a, ((0, rows - a.shape[0]), (0, 0)))


def kernel(x, g_ffn1, w_gu1, w_down1, g_mix, w_in, conv_w, q_norm_g, k_norm_g, sinks, w_out_conv, w_out_attn, w_o, g_ffn2, w_gu2, w_down2, loss_target, m_g_ffn1, m_w_gu1, m_w_down1, m_g_mix, m_w_in, m_conv_w, m_q_norm_g, m_k_norm_g, m_sinks, m_w_out_conv, m_w_out_attn, m_w_o, m_g_ffn2, m_w_gu2, m_w_down2, v_g_ffn1, v_w_gu1, v_w_down1, v_g_mix, v_w_in, v_conv_w, v_q_norm_g, v_k_norm_g, v_sinks, v_w_out_conv, v_w_out_attn, v_w_o, v_g_ffn2, v_w_gu2, v_w_down2):
    t, d = x.shape[1], x.shape[2]
    cw = d // 2
    kw = cw // GROUP
    nq = cw // HEAD_DIM
    xs, target = x.reshape(t, d), loss_target.reshape(t, d)
    me = 4 * lax.axis_index("x") + 2 * lax.axis_index("y") + lax.axis_index("c")

    big = {"w_gu1": w_gu1, "w_down1": w_down1, "w_in": w_in, "w_out_conv": w_out_conv,
           "w_out_attn": w_out_attn, "w_o": w_o, "w_gu2": w_gu2, "w_down2": w_down2}
    big_m = {"w_gu1": m_w_gu1, "w_down1": m_w_down1, "w_in": m_w_in, "w_out_conv": m_w_out_conv,
             "w_out_attn": m_w_out_attn, "w_o": m_w_o, "w_gu2": m_w_gu2, "w_down2": m_w_down2}
    big_v = {"w_gu1": v_w_gu1, "w_down1": v_w_down1, "w_in": v_w_in, "w_out_conv": v_w_out_conv,
             "w_out_attn": v_w_out_attn, "w_o": v_w_o, "w_gu2": v_w_gu2, "w_down2": v_w_down2}
    names = list(big)

    shards = [big[n][0].astype(BF16) for n in names] + [_pad_rows(conv_w[0])]
    lands = [lax.dynamic_update_slice(lax.empty((N_DEV,) + s.shape, s.dtype), s[None], (me,) + (0,) * s.ndim)
             for s in shards]
    send, recv, shards, lands = _gather_start(shards, lands)

    def fetch(tag, idxs, after):
        got = _gather_wait("gather_wait_" + tag, idxs, send, recv, [shards[i] for i in idxs],
                           [lands[i] for i in idxs], after)
        return _forward_to_sibling("gather_forward_" + tag, got)

    rope_tabs = _rope_tables(t)
    gq = jnp.tile(q_norm_g, (1, nq))
    gk = jnp.tile(k_norm_g, (1, nq // GROUP))
    sink_rows = jnp.broadcast_to(sinks[0][:, None], (nq, LANES))

    wts = {}
    h1 = _rms_fwd("ffn1_norm", xs, g_ffn1)
    wts["w_gu1"], = fetch("gu1", [0], h1)
    gu1, a1 = _ffn_up("ffn1_up", h1, wts["w_gu1"])
    wts["w_down1"], = fetch("down1", [1], a1)
    wd1 = wts["w_down1"].reshape(-1, d)
    x1 = _ffn_down("ffn1_down", a1, wd1, xs)
    h2 = _rms_fwd("mix_norm", x1, g_mix)
    wts["w_in"], conv_land = fetch("in", [2, 8], h2)
    w_in_full = jnp.transpose(wts["w_in"], (1, 0, 2)).reshape(d, -1)
    conv_full = jnp.transpose(conv_land, (1, 0, 2)).reshape(8, cw)
    proj = _proj(h2, w_in_full)
    ca = _conv_fwd(proj, conv_full)
    qn, kn, vb = _qk_prep(proj, gq, gk, rope_tabs, cw, kw)
    o = _attn_fwd(qn, kn, vb, sink_rows)
    wts["w_out_conv"], wts["w_out_attn"] = fetch("out", [3, 4], o)
    merged, ya, yb = _mix_out(ca, o, wts["w_out_conv"], wts["w_out_attn"], proj)
    wts["w_o"], = fetch("o", [5], merged)
    wo = wts["w_o"].reshape(d, d)
    x2 = _mix_residual(merged, wo, x1)
    h3 = _rms_fwd("ffn2_norm", x2, g_ffn2)
    wts["w_gu2"], = fetch("gu2", [6], h3)
    gu2, a2 = _ffn_up("ffn2_up", h3, wts["w_gu2"])
    wts["w_down2"], = fetch("down2", [7], a2)
    wd2 = wts["w_down2"].reshape(-1, d)
    y = _ffn_down("ffn2_down", a2, wd2, x2)
    dy, sq = _loss_dy(y, target)
    loss = lax.psum(sq[0, 0] * (0.5 / d), ("x", "y", "c"))

    core = lax.axis_index("c").astype(jnp.int32).reshape(1)
    chip = (2 * lax.axis_index("x") + lax.axis_index("y")).astype(jnp.int32).reshape(1)
    tiles = {"w_gu1": 256, "w_gu2": 256, "w_in": 256, "w_down1": 176, "w_down2": 176,
             "w_out_conv": 1024, "w_out_attn": 1024, "w_o": 128}

    def row_tile(n):
        r = big[n].shape[1]
        return tiles[n] if r % tiles[n] == 0 else r

    def pair_start(tag, group, grads):
        stacks = [grads[n].reshape((4, 2) + big[n].shape[1:]) for n in group]
        lands = [lax.empty((4,) + big[n].shape[1:], BF16) for n in group]
        return _pair_start("rs_pair_start_" + tag, stacks, lands)

    def chip_start(tag, group, pending, after):
        send, recv, stacks, lands, _ = pending
        stacks, lands = _pair_wait("rs_pair_wait_" + tag, send, recv, stacks, lands, after)
        parts = [_pair_add("rs_pair_add_" + n, st, ld, core, row_tile(n)) for n, st, ld in zip(group, stacks, lands)]
        lands2 = [lax.empty((4,) + big[n].shape[1:], BF16) for n in group]
        return _chip_start("rs_chip_start_" + tag, parts, lands2)

    group_a, group_b, group_c, group_d = ["w_down2", "w_gu2"], ["w_o", "w_out_conv", "w_out_attn"], ["w_in"], ["w_down1", "w_gu1"]
    g = {}
    dgu2 = _ffn_bwd_act("ffn2_bwd_act", dy, wd2, gu2)
    g["w_down2"] = _ffn_bwd_dwd("ffn2_bwd_dwd", a2, dy)
    g["w_gu2"] = _ffn_bwd_dwgu("ffn2_bwd_dwgu", h3, dgu2)
    pend_a = pair_start("a", group_a, g)
    dh3 = _ffn_bwd_dh("ffn2_bwd_dh", dgu2, wts["w_gu2"], deps=(pend_a[4],))
    ring_a = chip_start("a", group_a, pend_a, dh3)
    dx2, dg_ffn2 = _rms_bwd("ffn2_bwd_rms", x2, g_ffn2, dh3, dy, deps=(ring_a[4],))

    dya, dyb, dgates = _mix_bwd_gates(dx2, wo, ya, yb, proj, cw)
    g["w_o"] = _tn_matmul("mix_bwd_dwo", merged, dx2, min(d, 1024))
    g["w_out_conv"] = _out_proj_bwd_w("mix_bwd_dwoc", ca, dya, d // N_DEV)
    g["w_out_attn"] = _out_proj_bwd_w("mix_bwd_dwoa", o, dyb, d // N_DEV)
    pend_b = pair_start("b", group_b, g)
    dca = _out_proj_bwd_act("mix_bwd_dca", dya, wts["w_out_conv"], deps=(pend_b[4],))
    do = _out_proj_bwd_act("mix_bwd_do", dyb, wts["w_out_attn"])
    ring_b = chip_start("b", group_b, pend_b, do)
    d3, dconv_w = _conv_bwd(proj, conv_full, dca, deps=(ring_b[4],))
    dq, dkc, dkp, dvc, dvp, dsink = _attn_bwd(qn, kn, vb, sink_rows, do)
    dqkv, dgq, dgk = _qk_prep_bwd(proj, gq, gk, rope_tabs, dq, dkc, dkp, dvc, dvp, cw, kw)
    dproj = jnp.concatenate([d3[0], d3[1], d3[2], dqkv, dgates[0], dgates[1]], axis=1)
    dw_in = _proj_bwd_w(h2, dproj)
    g["w_in"] = jnp.transpose(dw_in.reshape(d, N_DEV, -1), (1, 0, 2))
    pend_c = pair_start("c", group_c, g)
    dh2 = _proj_bwd_act(dproj, w_in_full, deps=(pend_c[4],))
    ring_c = chip_start("c", group_c, pend_c, dh2)
    dx1, dg_mix = _rms_bwd("mix_bwd_rms", x1, g_mix, dh2, dx2, deps=(ring_c[4],))

    dgu1 = _ffn_bwd_act("ffn1_bwd_act", dx1, wd1, gu1)
    g["w_down1"] = _ffn_bwd_dwd("ffn1_bwd_dwd", a1, dx1)
    g["w_gu1"] = _ffn_bwd_dwgu("ffn1_bwd_dwgu", h1, dgu1)
    pend_d = pair_start("d", group_d, g)
    dh1 = _ffn_bwd_dh("ffn1_bwd_dh", dgu1, wts["w_gu1"], deps=(pend_d[4],))
    ring_d = chip_start("d", group_d, pend_d, dh1)
    grad_x, dg_ffn1 = _rms_bwd("ffn1_bwd_rms", xs, g_ffn1, dh1, dx1, deps=(ring_d[4],))

    big_out = {}
    after = grad_x
    for tag, group, ring in (("a", group_a, ring_a), ("b", group_b, ring_b), ("c", group_c, ring_c), ("d", group_d, ring_d)):
        send, recv, parts, lands2, _ = ring
        parts, lands2 = _chip_wait("rs_chip_wait_" + tag, send, recv, parts, lands2, after)
        for n, own, landed in zip(group, parts, lands2):
            res = _adamw_chips("adamw_" + n, chip, own, landed, big[n][0], big_m[n][0], big_v[n][0], row_tile(n))
            big_out[n] = [a[None] for a in res]
            after = res[0]

    small = {"g_ffn1": dg_ffn1[0:1], "g_mix": dg_mix[0:1], "g_ffn2": dg_ffn2[0:1],
             "q_norm_g": dgq[0:1, :HEAD_DIM], "k_norm_g": dgk[0:1, :HEAD_DIM], "sinks": dsink[:, 0][None],
             "conv_w": dconv_w[0:CONV_K].reshape(1, -1)}
    small_w = {"g_ffn1": g_ffn1, "g_mix": g_mix, "g_ffn2": g_ffn2, "q_norm_g": q_norm_g, "k_norm_g": k_norm_g,
               "sinks": sinks, "conv_w": None}
    small_m = {"g_ffn1": m_g_ffn1, "g_mix": m_g_mix, "g_ffn2": m_g_ffn2, "q_norm_g": m_q_norm_g,
               "k_norm_g": m_k_norm_g, "sinks": m_sinks, "conv_w": m_conv_w}
    small_v = {"g_ffn1": v_g_ffn1, "g_mix": v_g_mix, "g_ffn2": v_g_ffn2, "q_norm_g": v_q_norm_g,
               "k_norm_g": v_k_norm_g, "sinks": v_sinks, "conv_w": v_conv_w}
    snames = list(small)
    widths = [small[n].shape[1] for n in snames]
    total = sum(widths)
    rows = -(-total // LANES)
    rows = -(-rows // 8) * 8

    def pack(vals):
        flat = jnp.concatenate([v.reshape(1, -1) for v in vals], axis=1)
        return jnp.pad(flat, ((0, 0), (0, rows * LANES - total))).reshape(rows, LANES)

    csh = cw // N_DEV

    def place_conv(local, fill):
        full = jnp.full((CONV_K, cw), fill, F32)
        return lax.dynamic_update_slice(full, local, (0, me * csh)).reshape(1, -1)

    pw = pack([small_w[n] if n != "conv_w" else place_conv(conv_w[0], 0.0) for n in snames])
    pm = pack([small_m[n] if n != "conv_w" else place_conv(m_conv_w[0], 0.0) for n in snames])
    pv = pack([small_v[n] if n != "conv_w" else place_conv(v_conv_w[0], 1.0) for n in snames])
    parts = _exchange("gather_small_grads", [pack([small[n] for n in snames])], gather=True)[0]
    sg, sd, sm, sv = [a.reshape(1, -1) for a in _adamw("adamw_small", parts, pw, pm, pv, rows)]

    def unpack(flat, n):
        off = sum(widths[:snames.index(n)])
        piece = flat[:, off:off + widths[snames.index(n)]]
        if n == "conv_w":
            piece = lax.dynamic_slice(piece.reshape(CONV_K, cw), (0, me * csh), (CONV_K, csh))[None]
        return piece

    order = ["g_ffn1", "w_gu1", "w_down1", "g_mix", "w_in", "conv_w", "q_norm_g", "k_norm_g", "sinks",
             "w_out_conv", "w_out_attn", "w_o", "g_ffn2", "w_gu2", "w_down2"]
    outs = [loss, grad_x[None]]
    for idx, flat in enumerate((sg, sd, sm, sv)):
        for n in order:
            outs.append(big_out[n][idx] if n in big_out else unpack(flat, n))
    return tuple(outs)
```

```python
import functools

import jax
import jax.numpy as jnp
from jax import lax
from jax.experimental import pallas as pl
from jax.experimental.pallas import tpu as pltpu

F32 = jnp.float32
BF16 = jnp.bfloat16

N_DEV = 8
HEAD_DIM = 64
GROUP = 4
BLOCK = 128
ROT_DIM = 16
ROPE_THETA = 500000.0
RMS_EPS = 1e-6
NEG_INF = -1e30
ATTN_SCALE = HEAD_DIM ** -0.5
CONV_K = 3
LANES = 128
VMEM_BYTES_V7X = 64 * 1024 * 1024
VMEM_CAP = VMEM_BYTES_V7X - 6 * 1024 * 1024

ADAM_LR = 0.001
ADAM_B1 = 0.9
ADAM_B2 = 0.999
ADAM_EPS = 1e-08
ADAM_WD = 0.01
ADAM_STEP = 10

NN = (((1,), (0,)), ((), ()))
NT = (((1,), (1,)), ((), ()))
TN = (((0,), (0,)), ((), ()))

MESH = pl.DeviceIdType.MESH


def _nbytes(shape, dtype):
    n = 1
    for s in shape:
        if s is not None:
            n *= s
    return n * jnp.dtype(dtype).itemsize


def _params(semantics, block_bytes, temp_bytes):
    est = 2 * block_bytes + temp_bytes + (4 << 20)
    return pltpu.CompilerParams(dimension_semantics=semantics, vmem_limit_bytes=int(min(max(est, 16 << 20), VMEM_CAP)))


def _fused(name, grid, ins, outs, dots, epilogue, *, nk=1, acc_shape=None, temp_bytes=0,
           semantics=("parallel", "parallel", "arbitrary"), deps=()):
    n_in, n_out = len(ins), len(outs)
    n_dep = len(deps)

    def body(*refs):
        in_refs, out_refs = refs[:n_in], refs[n_in + n_dep:n_in + n_dep + n_out]
        scratch = refs[n_in + n_dep + n_out:]

        def products():
            total = None
            for ai, bi, contract in dots:
                a, b = in_refs[ai][...], in_refs[bi][...]
                a = a if a.dtype == BF16 else a.astype(BF16)
                b = b if b.dtype == BF16 else b.astype(BF16)
                p = lax.dot_general(a, b, contract, preferred_element_type=F32)
                total = p if total is None else total + p
            return total

        if nk == 1:
            epilogue(products() if dots else None, in_refs, out_refs)
        else:
            acc = scratch[0]
            k = pl.program_id(2)

            @pl.when(k == 0)
            def _():
                acc[...] = jnp.zeros_like(acc)

            acc[...] += products()

            @pl.when(k == nk - 1)
            def _():
                epilogue(acc[...], in_refs, out_refs)

    block_bytes = sum(_nbytes(spec.block_shape, a.dtype) for a, spec in ins)
    block_bytes += sum(_nbytes(spec.block_shape, s.dtype) for s, spec in outs)
    scratch_shapes = []
    if nk > 1:
        scratch_shapes.append(pltpu.VMEM(acc_shape, F32))
        temp_bytes += _nbytes(acc_shape, F32)
    res = pl.pallas_call(
        body, name=name, grid=grid,
        in_specs=[spec for _, spec in ins] + [pl.BlockSpec(memory_space=pl.ANY)] * n_dep,
        out_specs=[spec for _, spec in outs],
        out_shape=[s for s, _ in outs],
        scratch_shapes=scratch_shapes,
        compiler_params=_params(semantics, block_bytes, temp_bytes),
    )(*[a for a, _ in ins], *deps)
    return res


def _sds(shape, dtype):
    return jax.ShapeDtypeStruct(shape, dtype)


def _sigmoid(x):
    return jax.nn.sigmoid(x)


def _exchange(name, arrays, gather, deps=()):
    n = len(arrays)
    out_shapes = [((N_DEV,) + a.shape) if gather else a.shape for a in arrays]

    def body(*refs):
        srcs, dsts = refs[:n], refs[n + len(deps):2 * n + len(deps)]
        send_sems, recv_sems, local_sems = refs[2 * n + len(deps):]
        x, y, c = lax.axis_index("x"), lax.axis_index("y"), lax.axis_index("c")
        me = 4 * x + 2 * y + c
        copies = []
        for w in range(n):
            own = srcs[w] if gather else srcs[w].at[me]
            local = pltpu.make_async_copy(own, dsts[w].at[me], local_sems.at[w])
            local.start()
            copies.append(local)
            for k in range(1, N_DEV):
                px = (1 - x) if (k & 4) else x
                py = (1 - y) if (k & 2) else y
                pc = (1 - c) if (k & 1) else c
                peer = 4 * px + 2 * py + pc
                cp = pltpu.make_async_remote_copy(
                    src_ref=srcs[w] if gather else srcs[w].at[peer],
                    dst_ref=dsts[w].at[me],
                    send_sem=send_sems.at[w * (N_DEV - 1) + k - 1],
                    recv_sem=recv_sems.at[w * (N_DEV - 1) + k - 1],
                    device_id=(px, py, pc), device_id_type=MESH)
                cp.start()
                copies.append(cp)
        for cp in copies:
            cp.wait()

    hbm = pl.BlockSpec(memory_space=pltpu.HBM)
    return pl.pallas_call(
        body, name=name,
        in_specs=[hbm] * n + [pl.BlockSpec(memory_space=pl.ANY)] * len(deps), out_specs=[hbm] * n,
        out_shape=[_sds(s, a.dtype) for s, a in zip(out_shapes, arrays)],
        scratch_shapes=[pltpu.SemaphoreType.DMA((n * (N_DEV - 1),)),
                        pltpu.SemaphoreType.DMA((n * (N_DEV - 1),)),
                        pltpu.SemaphoreType.DMA((n,))],
    )(*arrays, *deps)


_HBM = pl.BlockSpec(memory_space=pltpu.HBM)
_SEM = pl.BlockSpec(memory_space=pltpu.SEMAPHORE)
_ANY = pl.BlockSpec(memory_space=pl.ANY)
_EFFECT = pltpu.SideEffectType.DATAFLOW_SIDE_EFFECTING
N_TARGETS = 4


def _mesh_pos():
    return lax.axis_index("x"), lax.axis_index("y"), lax.axis_index("c")


def _chip_peers(x, y, c):
    return [(1 - x, y, c), (x, 1 - y, c), (1 - x, 1 - y, c)]


def _dev_index(pos):
    return 4 * pos[0] + 2 * pos[1] + pos[2]


def _hbm_like(a):
    return pltpu.HBM(a.shape, a.dtype)


def _place_shard(name, w, out_dtype, me, tr, deps=()):
    r, c = w.shape
    n_dep = len(deps)

    def body(me_ref, w_ref, *rest):
        rest[n_dep][...] = w_ref[...].astype(out_dtype)

    grid_spec = pltpu.PrefetchScalarGridSpec(
        num_scalar_prefetch=1, grid=(r // tr,),
        in_specs=[pl.BlockSpec((tr, c), lambda i, me_ref: (i, 0))] + [_ANY] * n_dep,
        out_specs=pl.BlockSpec((None, tr, c), lambda i, me_ref: (me_ref[0], i, 0)))
    return pl.pallas_call(
        body, name=name, grid_spec=grid_spec, out_shape=_sds((N_DEV, r, c), out_dtype),
        compiler_params=_params(("parallel",), tr * c * 6, tr * c * 4),
    )(me, w, *deps)


def _gather_start(name, lands):
    n = len(lands)

    def body(*refs):
        bufs = refs[:n]
        send, recv = refs[n], refs[n + 1]
        token = refs[-1]
        x, y, c = _mesh_pos()
        me = _dev_index((x, y, c))
        targets = [(x, y, 1 - c)] + _chip_peers(x, y, c)
        for w in range(n):
            for k, to in enumerate(targets):
                pltpu.make_async_remote_copy(
                    src_ref=bufs[w].at[me], dst_ref=bufs[w].at[me],
                    send_sem=send.at[N_TARGETS * w + k], recv_sem=recv.at[N_TARGETS * w + k],
                    device_id=to, device_id_type=MESH).start()
        token[...] = jnp.zeros_like(token)

    sems = pltpu.SemaphoreType.DMA((N_TARGETS * n,))
    outs = pl.pallas_call(
        body, name=name,
        in_specs=[_HBM] * n, out_specs=[_SEM, _SEM] + [_HBM] * n + [_token_spec()],
        out_shape=[sems, sems] + [_hbm_like(a) for a in lands] + [_sds((8, LANES), F32)],
        input_output_aliases={i: 2 + i for i in range(n)},
        compiler_params=pltpu.CompilerParams(has_side_effects=_EFFECT),
    )(*lands)
    return outs[0], outs[1], list(outs[2:2 + n]), outs[-1]


def _gather_wait(name, positions, send, recv, lands, after):
    m = len(positions)

    def body(*refs):
        bufs = refs[:m]
        send_sems, recv_sems = refs[m], refs[m + 1]
        x, y, c = _mesh_pos()
        me = _dev_index((x, y, c))
        sources = [(x, y, 1 - c)] + _chip_peers(x, y, c)
        for j, w in enumerate(positions):
            for k, frm in enumerate(sources):
                cp = pltpu.make_async_remote_copy(
                    src_ref=bufs[j].at[me], dst_ref=bufs[j].at[_dev_index(frm)],
                    send_sem=send_sems.at[N_TARGETS * w + k], recv_sem=recv_sems.at[N_TARGETS * w + k],
                    device_id=frm, device_id_type=MESH)
                cp.wait_send()
                cp.wait_recv()

    outs = pl.pallas_call(
        body, name=name,
        in_specs=[_HBM] * m + [_SEM, _SEM, _ANY], out_specs=[_HBM] * m,
        out_shape=[_hbm_like(a) for a in lands],
        input_output_aliases={i: i for i in range(m)},
        compiler_params=pltpu.CompilerParams(has_side_effects=_EFFECT),
    )(*lands, send, recv, after)
    return list(outs)


def _forward_to_sibling(name, lands):
    m = len(lands)

    def body(*refs):
        bufs = refs[m:2 * m]
        send_sems, recv_sems = refs[2 * m], refs[2 * m + 1]
        x, y, c = _mesh_pos()
        copies = []
        for j in range(m):
            for k, chip in enumerate(_chip_peers(x, y, c)):
                block = bufs[j].at[_dev_index(chip)]
                cp = pltpu.make_async_remote_copy(
                    src_ref=block, dst_ref=block,
                    send_sem=send_sems.at[3 * j + k], recv_sem=recv_sems.at[3 * j + k],
                    device_id=(x, y, 1 - c), device_id_type=MESH)
                cp.start()
                copies.append(cp)
        for cp in copies:
            cp.wait()

    outs = pl.pallas_call(
        body, name=name,
        in_specs=[_HBM] * m, out_specs=[_HBM] * m,
        out_shape=[_sds(a.shape, a.dtype) for a in lands],
        input_output_aliases={i: i for i in range(m)},
        scratch_shapes=[pltpu.SemaphoreType.DMA((3 * m,)), pltpu.SemaphoreType.DMA((3 * m,))],
    )(*lands)
    return list(outs)


def _token_spec():
    return pl.BlockSpec(memory_space=pltpu.VMEM)


def _pair_start(name, stacks, lands):
    n = len(stacks)

    def body(*refs):
        srcs, dsts = refs[:n], refs[n:2 * n]
        send, recv = refs[2 * n], refs[2 * n + 1]
        token = refs[-1]
        x, y, c = _mesh_pos()
        for w in range(n):
            for chip in range(4):
                pltpu.make_async_remote_copy(
                    src_ref=srcs[w].at[chip, 1 - c], dst_ref=dsts[w].at[chip],
                    send_sem=send.at[4 * w + chip], recv_sem=recv.at[4 * w + chip],
                    device_id=(x, y, 1 - c), device_id_type=MESH).start()
        token[...] = jnp.zeros_like(token)

    sems = pltpu.SemaphoreType.DMA((4 * n,))
    outs = pl.pallas_call(
        body, name=name,
        in_specs=[_HBM] * (2 * n), out_specs=[_SEM, _SEM] + [_HBM] * (2 * n) + [_token_spec()],
        out_shape=[sems, sems] + [_hbm_like(a) for a in stacks] + [_hbm_like(a) for a in lands] + [_sds((8, LANES), F32)],
        input_output_aliases={i: 2 + i for i in range(2 * n)},
        compiler_params=pltpu.CompilerParams(has_side_effects=_EFFECT),
    )(*stacks, *lands)
    return outs[0], outs[1], list(outs[2:2 + n]), list(outs[2 + n:2 + 2 * n]), outs[-1]


def _pair_wait(name, send, recv, stacks, lands, after):
    n = len(stacks)

    def body(*refs):
        srcs, dsts = refs[:n], refs[n:2 * n]
        send_sems, recv_sems = refs[2 * n], refs[2 * n + 1]
        x, y, c = _mesh_pos()
        for w in range(n):
            for chip in range(4):
                cp = pltpu.make_async_remote_copy(
                    src_ref=srcs[w].at[chip, 1 - c], dst_ref=dsts[w].at[chip],
                    send_sem=send_sems.at[4 * w + chip], recv_sem=recv_sems.at[4 * w + chip],
                    device_id=(x, y, 1 - c), device_id_type=MESH)
                cp.wait_send()
                cp.wait_recv()

    outs = pl.pallas_call(
        body, name=name,
        in_specs=[_HBM] * (2 * n) + [_SEM, _SEM, _ANY], out_specs=[_HBM] * (2 * n),
        out_shape=[_hbm_like(a) for a in stacks] + [_hbm_like(a) for a in lands],
        input_output_aliases={i: i for i in range(2 * n)},
        compiler_params=pltpu.CompilerParams(has_side_effects=_EFFECT),
    )(*stacks, *lands, send, recv, after)
    return list(outs[:n]), list(outs[n:])


def _pair_add(name, stack, land, core, tr):
    _, _, r, c = stack.shape

    def body(core_ref, a_ref, b_ref, o_ref):
        o_ref[...] = (a_ref[...].astype(F32) + b_ref[...].astype(F32)).astype(BF16)

    grid_spec = pltpu.PrefetchScalarGridSpec(
        num_scalar_prefetch=1, grid=(4, r // tr),
        in_specs=[pl.BlockSpec((None, None, tr, c), lambda k, i, core_ref: (k, core_ref[0], i, 0)),
                  pl.BlockSpec((None, tr, c), lambda k, i, core_ref: (k, i, 0))],
        out_specs=pl.BlockSpec((None, tr, c), lambda k, i, core_ref: (k, i, 0)))
    return pl.pallas_call(
        body, name=name, grid_spec=grid_spec, out_shape=_sds((4, r, c), BF16),
        compiler_params=_params(("parallel", "parallel"), 3 * tr * c * 2, 3 * tr * c * 4),
    )(core, stack, land)


def _chip_start(name, parts, lands):
    n = len(parts)

    def body(*refs):
        srcs, dsts = refs[:n], refs[n:2 * n]
        send, recv = refs[2 * n], refs[2 * n + 1]
        token = refs[-1]
        x, y, c = _mesh_pos()
        for w in range(n):
            for k, to in enumerate(_chip_peers(x, y, c)):
                pltpu.make_async_remote_copy(
                    src_ref=srcs[w].at[2 * to[0] + to[1]], dst_ref=dsts[w].at[2 * x + y],
                    send_sem=send.at[3 * w + k], recv_sem=recv.at[3 * w + k],
                    device_id=to, device_id_type=MESH).start()
        token[...] = jnp.zeros_like(token)

    sems = pltpu.SemaphoreType.DMA((3 * n,))
    outs = pl.pallas_call(
        body, name=name,
        in_specs=[_HBM] * (2 * n), out_specs=[_SEM, _SEM] + [_HBM] * (2 * n) + [_token_spec()],
        out_shape=[sems, sems] + [_hbm_like(a) for a in parts] + [_hbm_like(a) for a in lands] + [_sds((8, LANES), F32)],
        input_output_aliases={i: 2 + i for i in range(2 * n)},
        compiler_params=pltpu.CompilerParams(has_side_effects=_EFFECT),
    )(*parts, *lands)
    return outs[0], outs[1], list(outs[2:2 + n]), list(outs[2 + n:2 + 2 * n]), outs[-1]


def _chip_wait(name, send, recv, parts, lands, after):
    n = len(parts)

    def body(*refs):
        srcs, dsts = refs[:n], refs[n:2 * n]
        send_sems, recv_sems = refs[2 * n], refs[2 * n + 1]
        x, y, c = _mesh_pos()
        for w in range(n):
            for k, frm in enumerate(_chip_peers(x, y, c)):
                chip = 2 * frm[0] + frm[1]
                cp = pltpu.make_async_remote_copy(
                    src_ref=srcs[w].at[chip], dst_ref=dsts[w].at[chip],
                    send_sem=send_sems.at[3 * w + k], recv_sem=recv_sems.at[3 * w + k],
                    device_id=frm, device_id_type=MESH)
                cp.wait_send()
                cp.wait_recv()

    outs = pl.pallas_call(
        body, name=name,
        in_specs=[_HBM] * (2 * n) + [_SEM, _SEM, _ANY], out_specs=[_HBM] * (2 * n),
        out_shape=[_hbm_like(a) for a in parts] + [_hbm_like(a) for a in lands],
        input_output_aliases={i: i for i in range(2 * n)},
        compiler_params=pltpu.CompilerParams(has_side_effects=_EFFECT),
    )(*parts, *lands, send, recv, after)
    return list(outs[:n]), list(outs[n:])


def _row_tile(t):
    return min(t, 256)


def _rms_fwd(name, x, g):
    t, d = x.shape
    tm = _row_tile(t)

    def epilogue(_, ins, outs):
        xv = ins[0][...]
        r = lax.rsqrt(jnp.mean(xv * xv, axis=-1, keepdims=True) + RMS_EPS)
        outs[0][...] = (xv * r * ins[1][...]).astype(BF16)

    row = pl.BlockSpec((tm, d), lambda i, j, k: (i, 0))
    vec = pl.BlockSpec((1, d), lambda i, j, k: (0, 0))
    return _fused(name, (t // tm, 1, 1), [(x, row), (g, vec)], [(_sds((t, d), BF16), row)], [], epilogue,
                  temp_bytes=4 * tm * d * 4)[0]


def _rms_bwd(name, x, g, dh, resid, deps=()):
    t, d = x.shape
    tm = _row_tile(t)

    def epilogue(_, ins, outs):
        xv, gv, dhv = ins[0][...], ins[1][...], ins[2][...]
        r = lax.rsqrt(jnp.mean(xv * xv, axis=-1, keepdims=True) + RMS_EPS)
        xh = xv * r
        u = dhv * gv
        dot = jnp.mean(u * xh, axis=-1, keepdims=True)
        outs[0][...] = ins[3][...] + r * (u - xh * dot)

        @pl.when(pl.program_id(0) == 0)
        def _():
            outs[1][...] = jnp.zeros_like(outs[1])

        outs[1][0:1, :] += jnp.sum(dhv * xh, axis=0, keepdims=True)

    row = pl.BlockSpec((tm, d), lambda i, j, k: (i, 0))
    vec = pl.BlockSpec((1, d), lambda i, j, k: (0, 0))
    acc = pl.BlockSpec((8, d), lambda i, j, k: (0, 0))
    return _fused(name, (t // tm, 1, 1), [(x, row), (g, vec), (dh, row), (resid, row)],
                  [(_sds((t, d), F32), row), (_sds((8, d), F32), acc)], [], epilogue,
                  temp_bytes=6 * tm * d * 4, semantics=("arbitrary", "arbitrary", "arbitrary"), deps=deps)


def _loss_dy(y, target):
    t, d = y.shape
    tm = _row_tile(t)

    def epilogue(_, ins, outs):
        e = ins[0][...] - ins[1][...]
        outs[0][...] = e * (1.0 / d)

        @pl.when(pl.program_id(0) == 0)
        def _():
            outs[1][...] = jnp.zeros_like(outs[1])

        part = jnp.sum(jnp.sum(e * e, axis=1, keepdims=True), axis=0, keepdims=True)
        outs[1][...] += jnp.broadcast_to(part, outs[1].shape)

    row = pl.BlockSpec((tm, d), lambda i, j, k: (i, 0))
    acc = pl.BlockSpec((8, LANES), lambda i, j, k: (0, 0))
    return _fused("loss_dy", (t // tm, 1, 1), [(y, row), (target, row)],
                  [(_sds((t, d), F32), row), (_sds((8, LANES), F32), acc)], [], epilogue,
                  temp_bytes=3 * tm * d * 4, semantics=("arbitrary", "arbitrary", "arbitrary"))


def _ffn_up(name, h, wgu):
    t, d = h.shape
    nb = wgu.shape[2]
    f = 4 * nb
    tm = _row_tile(t)

    def body(h_ref, wg_ref, wu_ref, gu_ref, a_ref):
        hv = h_ref[...]
        g = jnp.dot(hv, wg_ref[...], preferred_element_type=F32)
        u = jnp.dot(hv, wu_ref[...], preferred_element_type=F32)
        gu_ref[0] = g.astype(BF16)
        gu_ref[1] = u.astype(BF16)
        a_ref[...] = (g * _sigmoid(g) * u).astype(BF16)

    blocks = tm * d * 2 + 2 * d * nb * 2 + 3 * tm * nb * 2
    return pl.pallas_call(
        body, name=name, grid=(4, t // tm),
        in_specs=[pl.BlockSpec((tm, d), lambda j, i: (i, 0)),
                  pl.BlockSpec((None, d, nb), lambda j, i: (j, 0, 0)),
                  pl.BlockSpec((None, d, nb), lambda j, i: (j + 4, 0, 0))],
        out_specs=[pl.BlockSpec((2, tm, nb), lambda j, i: (0, i, j)),
                   pl.BlockSpec((tm, nb), lambda j, i: (i, j))],
        out_shape=[_sds((2, t, f), BF16), _sds((t, f), BF16)],
        compiler_params=_params(("parallel", "parallel"), blocks, 5 * tm * nb * 4),
    )(h, wgu, wgu)


def _ffn_down(name, a, wd, x):
    t, f = a.shape
    d = wd.shape[1]
    tm = min(t, 512)
    tk = f // 4

    def epilogue(acc, ins, outs):
        outs[0][...] = ins[2][...] + 0.5 * acc

    return _fused(name, (t // tm, 1, 4),
                  [(a, pl.BlockSpec((tm, tk), lambda i, j, k: (i, k))),
                   (wd, pl.BlockSpec((tk, d), lambda i, j, k: (k, 0))),
                   (x, pl.BlockSpec((tm, d), lambda i, j, k: (i, 0)))],
                  [(_sds((t, d), F32), pl.BlockSpec((tm, d), lambda i, j, k: (i, 0)))],
                  [(0, 1, NN)], epilogue, nk=4, acc_shape=(tm, d), temp_bytes=2 * tm * d * 4)[0]


def _ffn_bwd_act(name, dy, wd, gu, deps=()):
    t, d = dy.shape
    f = wd.shape[0]
    nb = f // 4
    tm = _row_tile(t)

    def body(dy_ref, wd_ref, gu_ref, *rest):
        dgu_ref = rest[-1]
        da = 0.5 * lax.dot_general(dy_ref[...].astype(BF16), wd_ref[...], NT, preferred_element_type=F32)
        g = gu_ref[0].astype(F32)
        u = gu_ref[1].astype(F32)
        s = _sigmoid(g)
        dgu_ref[0] = (da * u * (s * (1.0 + g * (1.0 - s)))).astype(BF16)
        dgu_ref[1] = (da * (g * s)).astype(BF16)

    blocks = tm * d * 4 + nb * d * 2 + 4 * tm * nb * 2
    return pl.pallas_call(
        body, name=name, grid=(4, t // tm),
        in_specs=[pl.BlockSpec((tm, d), lambda j, i: (i, 0)),
                  pl.BlockSpec((nb, d), lambda j, i: (j, 0)),
                  pl.BlockSpec((2, tm, nb), lambda j, i: (0, i, j))] + [_ANY] * len(deps),
        out_specs=pl.BlockSpec((2, tm, nb), lambda j, i: (0, i, j)),
        out_shape=_sds((2, t, f), BF16),
        compiler_params=_params(("parallel", "parallel"), blocks, 6 * tm * nb * 4),
    )(dy, wd, gu, *deps)


def _ffn_bwd_dwd(name, a, dy):
    t, f = a.shape
    d = dy.shape[1]
    tm = f // 4
    tk = min(t, 512)

    def epilogue(acc, ins, outs):
        outs[0][...] = (0.5 * acc).astype(BF16)

    return _fused(name, (4, 1, t // tk),
                  [(a, pl.BlockSpec((tk, tm), lambda i, j, k: (k, i))),
                   (dy, pl.BlockSpec((tk, d), lambda i, j, k: (k, 0)))],
                  [(_sds((f, d), BF16), pl.BlockSpec((tm, d), lambda i, j, k: (i, 0)))],
                  [(0, 1, TN)], epilogue, nk=t // tk, acc_shape=(tm, d), temp_bytes=2 * tm * d * 4)[0]


def _ffn_bwd_dh(name, dgu, wgu, deps=()):
    _, t, f = dgu.shape
    d, nb = wgu.shape[1], wgu.shape[2]
    tm = min(t, 512)

    def epilogue(acc, ins, outs):
        outs[0][...] = acc

    return _fused(name, (t // tm, 1, N_DEV),
                  [(dgu, pl.BlockSpec((None, tm, nb), lambda i, j, k: (k // 4, i, k % 4))),
                   (wgu, pl.BlockSpec((None, d, nb), lambda i, j, k: (k, 0, 0)))],
                  [(_sds((t, d), F32), pl.BlockSpec((tm, d), lambda i, j, k: (i, 0)))],
                  [(0, 1, NT)], epilogue, nk=N_DEV, acc_shape=(tm, d), temp_bytes=tm * d * 4, deps=deps)[0]


def _ffn_bwd_dwgu(name, h, dgu, deps=()):
    t, d = h.shape
    nb = dgu.shape[2] // 4
    tk = min(t, 512)

    def epilogue(acc, ins, outs):
        outs[0][...] = acc.astype(BF16)

    return _fused(name, (N_DEV, 1, t // tk),
                  [(h, pl.BlockSpec((tk, d), lambda i, j, k: (k, 0))),
                   (dgu, pl.BlockSpec((None, tk, nb), lambda i, j, k: (i // 4, k, i % 4)))],
                  [(_sds((N_DEV, d, nb), BF16), pl.BlockSpec((None, d, nb), lambda i, j, k: (i, 0, 0)))],
                  [(0, 1, TN)], epilogue, nk=t // tk, acc_shape=(d, nb), temp_bytes=d * nb * 4, deps=deps)[0]


def _proj(h, w_in):
    t, d = h.shape
    n = w_in.shape[1]
    tn = n // 4
    tm = min(t, 512)

    def epilogue(acc, ins, outs):
        outs[0][...] = acc

    return _fused("mix_proj", (4, t // tm, 1),
                  [(h, pl.BlockSpec((tm, d), lambda j, i, k: (i, 0))),
                   (w_in, pl.BlockSpec((d, tn), lambda j, i, k: (0, j)))],
                  [(_sds((t, n), F32), pl.BlockSpec((tm, tn), lambda j, i, k: (i, j)))],
                  [(0, 1, NN)], epilogue, temp_bytes=2 * tm * tn * 4)[0]


def _shift_rows(u, k):
    t = u.shape[0]
    rolled = pltpu.roll(u, k % t, axis=0)
    row = lax.broadcasted_iota(jnp.int32, u.shape, 0)
    keep = (row >= k) if k > 0 else (row < t + k)
    return jnp.where(keep, rolled, 0.0)


def _conv_fwd(proj, conv_w):
    t = proj.shape[0]
    cw = conv_w.shape[1]
    tc = min(cw, 256)
    nc = cw // tc

    def epilogue(_, ins, outs):
        u = ins[2][...] * ins[0][...]
        w = ins[3][...]
        y = u * w[2:3, :] + _shift_rows(u, 1) * w[1:2, :] + _shift_rows(u, 2) * w[0:1, :]
        outs[0][...] = (ins[1][...] * y).astype(BF16)

    def col(seg):
        return pl.BlockSpec((t, tc), lambda i, j, k: (0, seg * nc + i))

    return _fused("conv_fwd", (nc, 1, 1),
                  [(proj, col(0)), (proj, col(1)), (proj, col(2)),
                   (conv_w, pl.BlockSpec((8, tc), lambda i, j, k: (0, i)))],
                  [(_sds((t, cw), BF16), pl.BlockSpec((t, tc), lambda i, j, k: (0, i)))],
                  [], epilogue, temp_bytes=6 * t * tc * 4)[0]


def _conv_bwd(proj, conv_w, dca, deps=()):
    t = proj.shape[0]
    cw = conv_w.shape[1]
    tc = min(cw, 256)
    nc = cw // tc

    def epilogue(_, ins, outs):
        xc, bg, cg, w, dc = ins[0][...], ins[1][...], ins[2][...], ins[3][...], ins[4][...]
        u = cg * xc
        u1, u2 = _shift_rows(u, 1), _shift_rows(u, 2)
        y = u * w[2:3, :] + u1 * w[1:2, :] + u2 * w[0:1, :]
        dconv = dc * bg
        du = dconv * w[2:3, :] + _shift_rows(dconv, -1) * w[1:2, :] + _shift_rows(dconv, -2) * w[0:1, :]
        outs[0][0] = (du * cg).astype(BF16)
        outs[0][1] = (dc * y).astype(BF16)
        outs[0][2] = (du * xc).astype(BF16)
        outs[1][...] = jnp.zeros_like(outs[1])
        outs[1][0:1, :] = jnp.sum(dconv * u2, axis=0, keepdims=True)
        outs[1][1:2, :] = jnp.sum(dconv * u1, axis=0, keepdims=True)
        outs[1][2:3, :] = jnp.sum(dconv * u, axis=0, keepdims=True)

    def col(seg):
        return pl.BlockSpec((t, tc), lambda i, j, k: (0, seg * nc + i))

    own = pl.BlockSpec((t, tc), lambda i, j, k: (0, i))
    wspec = pl.BlockSpec((8, tc), lambda i, j, k: (0, i))
    return _fused("conv_bwd", (nc, 1, 1),
                  [(proj, col(0)), (proj, col(1)), (proj, col(2)), (conv_w, wspec), (dca, own)],
                  [(_sds((3, t, cw), BF16), pl.BlockSpec((3, t, tc), lambda i, j, k: (0, 0, i))),
                   (_sds((8, cw), F32), wspec)],
                  [], epilogue, temp_bytes=10 * t * tc * 4, deps=deps)


def _split3(x):
    hi = x.astype(BF16)
    r1 = x - hi.astype(F32)
    mid = r1.astype(BF16)
    lo = (r1 - mid.astype(F32)).astype(BF16)
    return hi, mid, lo


def _head_selector(width):
    r = lax.broadcasted_iota(jnp.int32, (width, LANES), 0)
    c = lax.broadcasted_iota(jnp.int32, (width, LANES), 1)
    return (lax.shift_right_logical(r, 6) == c).astype(BF16)


def _head_sum(x, sel):
    return sum(jnp.dot(p, sel, preferred_element_type=F32) for p in _split3(x))


def _head_bcast(r, sel):
    return sum(lax.dot_general(p, sel, NT, preferred_element_type=F32) for p in _split3(r))


def _rope(x, c, sa, sb):
    n = x.shape[1]
    return x * c + pltpu.roll(x, n - ROT_DIM // 2, axis=1) * sa + pltpu.roll(x, ROT_DIM // 2, axis=1) * sb


def _rope_t(d, c, sa, sb):
    n = d.shape[1]
    return d * c + pltpu.roll(d * sa, ROT_DIM // 2, axis=1) + pltpu.roll(d * sb, n - ROT_DIM // 2, axis=1)


def _tile_lanes(tab, width):
    return tab if width == tab.shape[1] else jnp.tile(tab, (1, width // tab.shape[1]))


def _qk_prep(proj, gq, gk, rope_tabs, cw, kw):
    t = proj.shape[0]
    tm = _row_tile(t)

    def epilogue(_, ins, outs):
        c, sa, sb = ins[5][...], ins[6][...], ins[7][...]
        for src, gain, dst, width in ((0, 3, 0, cw), (1, 4, 1, kw)):
            xv = ins[src][...]
            sel = _head_selector(width)
            r = lax.rsqrt(_head_sum(xv * xv, sel) * (1.0 / HEAD_DIM) + RMS_EPS)
            xn = xv * _head_bcast(r, sel) * ins[gain][...]
            outs[dst][...] = _rope(xn, _tile_lanes(c, width), _tile_lanes(sa, width), _tile_lanes(sb, width)).astype(BF16)
        outs[2][...] = ins[2][...].astype(BF16)

    kblk = cw // kw
    tab = pl.BlockSpec((tm, LANES), lambda i, j, k: (i, 0))
    kspec = pl.BlockSpec((tm, kw), lambda i, j, k: (i, 0))
    return _fused("qk_prep", (t // tm, 1, 1),
                  [(proj, pl.BlockSpec((tm, cw), lambda i, j, k: (i, 3))),
                   (proj, pl.BlockSpec((tm, kw), lambda i, j, k: (i, 4 * kblk))),
                   (proj, pl.BlockSpec((tm, kw), lambda i, j, k: (i, 4 * kblk + 1))),
                   (gq, pl.BlockSpec((1, cw), lambda i, j, k: (0, 0))),
                   (gk, pl.BlockSpec((1, kw), lambda i, j, k: (0, 0))),
                   (rope_tabs[0], tab), (rope_tabs[1], tab), (rope_tabs[2], tab)],
                  [(_sds((t, cw), BF16), pl.BlockSpec((tm, cw), lambda i, j, k: (i, 0))),
                   (_sds((t, kw), BF16), kspec), (_sds((t, kw), BF16), kspec)],
                  [], epilogue, temp_bytes=12 * tm * cw * 4)


def _qk_prep_bwd(proj, gq, gk, rope_tabs, dq, dkc, dkp, dvc, dvp, cw, kw):
    t = proj.shape[0]
    tm = BLOCK
    nblk = t // tm

    def epilogue(_, ins, outs):
        c, sa, sb = ins[5][...], ins[6][...], ins[7][...]
        has_next = (pl.program_id(0) < nblk - 1).astype(F32)
        dk = ins[9][...] + has_next * ins[10][...]
        dv = ins[11][...] + has_next * ins[12][...]
        pieces = []
        for src, gain, dval, dst, width in ((0, 3, ins[8][...], 1, cw), (1, 4, dk, 2, kw)):
            xv, gv = ins[src][...], ins[gain][...]
            sel = _head_selector(width)
            r = _head_bcast(lax.rsqrt(_head_sum(xv * xv, sel) * (1.0 / HEAD_DIM) + RMS_EPS), sel)
            xh = xv * r
            dxn = _rope_t(dval, _tile_lanes(c, width), _tile_lanes(sa, width), _tile_lanes(sb, width))
            u = dxn * gv
            dot = _head_bcast(_head_sum(u * xh, sel), sel) * (1.0 / HEAD_DIM)
            pieces.append((r * (u - xh * dot)).astype(BF16))
            ri = lax.broadcasted_iota(jnp.int32, (width, LANES), 0)
            ci = lax.broadcasted_iota(jnp.int32, (width, LANES), 1)
            fold = (lax.bitwise_and(ri, HEAD_DIM - 1) == ci).astype(BF16)
            colsum = jnp.broadcast_to(jnp.sum(dxn * xh, axis=0, keepdims=True), (8, width))
            part = sum(jnp.dot(p, fold, preferred_element_type=F32) for p in _split3(colsum))

            @pl.when(pl.program_id(0) == 0)
            def _():
                outs[dst][...] = jnp.zeros_like(outs[dst])

            outs[dst][0:1, :] += part[0:1, :]
        outs[0][:, 0:cw] = pieces[0]
        outs[0][:, cw:cw + kw] = pieces[1]
        outs[0][:, cw + kw:cw + 2 * kw] = dv.astype(BF16)

    kblk = cw // kw
    tab = pl.BlockSpec((tm, LANES), lambda i, j, k: (i, 0))
    kcur = pl.BlockSpec((tm, kw), lambda i, j, k: (i, 0))
    knext = pl.BlockSpec((tm, kw), lambda i, j, k: (jnp.minimum(i + 1, nblk - 1), 0))
    acc = pl.BlockSpec((8, LANES), lambda i, j, k: (0, 0))
    return _fused("qk_prep_bwd", (nblk, 1, 1),
                  [(proj, pl.BlockSpec((tm, cw), lambda i, j, k: (i, 3))),
                   (proj, pl.BlockSpec((tm, kw), lambda i, j, k: (i, 4 * kblk))),
                   (proj, pl.BlockSpec((tm, kw), lambda i, j, k: (i, 4 * kblk + 1))),
                   (gq, pl.BlockSpec((1, cw), lambda i, j, k: (0, 0))),
                   (gk, pl.BlockSpec((1, kw), lambda i, j, k: (0, 0))),
                   (rope_tabs[0], tab), (rope_tabs[1], tab), (rope_tabs[2], tab),
                   (dq, pl.BlockSpec((tm, cw), lambda i, j, k: (i, 0))),
                   (dkc, kcur), (dkp, knext), (dvc, kcur), (dvp, knext)],
                  [(_sds((t, cw + 2 * kw), BF16), pl.BlockSpec((tm, cw + 2 * kw), lambda i, j, k: (i, 0))),
                   (_sds((8, LANES), F32), acc), (_sds((8, LANES), F32), acc)],
                  [], epilogue, temp_bytes=16 * tm * cw * 4, semantics=("arbitrary", "arbitrary", "arbitrary"))


def _attn_mask(n):
    row = lax.broadcasted_iota(jnp.int32, (BLOCK, 2 * BLOCK), 0)
    col = lax.broadcasted_iota(jnp.int32, (BLOCK, 2 * BLOCK), 1)
    return (col > row) & (col <= row + BLOCK) & ((col >= BLOCK) | (n > 0))


def _softmax_with_sink(q, k2, sink, valid):
    s = lax.dot_general(q, k2, NT, preferred_element_type=F32) * ATTN_SCALE
    s = jnp.where(valid, s, NEG_INF)
    m = jnp.maximum(jnp.max(s, axis=-1, keepdims=True), sink)
    p = jnp.exp(s - m)
    es = jnp.exp(sink - m)
    denom = jnp.sum(p, axis=-1, keepdims=True) + es
    return p / denom, es / denom


def _attn_fwd(qn, kn, vb, sink_rows):
    t, cw = qn.shape
    kw = kn.shape[1]
    nkv = kw // HEAD_DIM

    def body(q_ref, kp_ref, kc_ref, vp_ref, vc_ref, sink_ref, o_ref):
        valid = _attn_mask(pl.program_id(0))
        qv = q_ref[...]
        kp, kc, vp, vc = kp_ref[...], kc_ref[...], vp_ref[...], vc_ref[...]
        outs = []
        for h in range(nkv):
            hs = slice(h * HEAD_DIM, (h + 1) * HEAD_DIM)
            k2 = jnp.concatenate([kp[:, hs], kc[:, hs]], axis=0)
            v2 = jnp.concatenate([vp[:, hs], vc[:, hs]], axis=0)
            for g in range(GROUP):
                hq = h * GROUP + g
                pn, _ = _softmax_with_sink(qv[:, hq * HEAD_DIM:(hq + 1) * HEAD_DIM], k2, sink_ref[hq:hq + 1, 0:1], valid)
                outs.append(jnp.dot(pn.astype(BF16), v2, preferred_element_type=F32))
        o_ref[...] = jnp.concatenate(outs, axis=-1).astype(BF16)

    cur = lambda n: (n, 0)
    prev = lambda n: (jnp.maximum(n - 1, 0), 0)
    return pl.pallas_call(
        body, name="attn_fwd", grid=(t // BLOCK,),
        in_specs=[pl.BlockSpec((BLOCK, cw), cur),
                  pl.BlockSpec((BLOCK, kw), prev), pl.BlockSpec((BLOCK, kw), cur),
                  pl.BlockSpec((BLOCK, kw), prev), pl.BlockSpec((BLOCK, kw), cur),
                  pl.BlockSpec(sink_rows.shape, lambda n: (0, 0))],
        out_specs=pl.BlockSpec((BLOCK, cw), cur),
        out_shape=_sds((t, cw), BF16),
        compiler_params=_params(("parallel",), BLOCK * (cw + 4 * kw) * 2 + BLOCK * cw * 2, 8 << 20),
    )(qn, kn, kn, vb, vb, sink_rows)


def _attn_bwd(qn, kn, vb, sink_rows, do):
    t, cw = qn.shape
    kw = kn.shape[1]
    nkv = kw // HEAD_DIM
    nq = nkv * GROUP

    def body(q_ref, kp_ref, kc_ref, vp_ref, vc_ref, sink_ref, do_ref,
             dq_ref, dkc_ref, dkp_ref, dvc_ref, dvp_ref, dsink_ref):
        n = pl.program_id(0)
        valid = _attn_mask(n)
        qv, dov = q_ref[...], do_ref[...]
        kp, kc, vp, vc = kp_ref[...], kc_ref[...], vp_ref[...], vc_ref[...]
        dqs, dks, dvs, dsinks = [], [], [], []
        for h in range(nkv):
            hs = slice(h * HEAD_DIM, (h + 1) * HEAD_DIM)
            k2 = jnp.concatenate([kp[:, hs], kc[:, hs]], axis=0)
            v2 = jnp.concatenate([vp[:, hs], vc[:, hs]], axis=0)
            dk2 = jnp.zeros((2 * BLOCK, HEAD_DIM), F32)
            dv2 = jnp.zeros((2 * BLOCK, HEAD_DIM), F32)
            for g in range(GROUP):
                hq = h * GROUP + g
                qs = slice(hq * HEAD_DIM, (hq + 1) * HEAD_DIM)
                q = qv[:, qs]
                pn, psink = _softmax_with_sink(q, k2, sink_ref[hq:hq + 1, 0:1], valid)
                pb = pn.astype(BF16)
                dob = dov[:, qs].astype(BF16)
                dpn = lax.dot_general(dob, v2, NT, preferred_element_type=F32)
                dv2 = dv2 + lax.dot_general(pb, dob, TN, preferred_element_type=F32)
                delta = jnp.sum(pn * dpn, axis=-1, keepdims=True)
                ds = (pn * (dpn - delta) * ATTN_SCALE).astype(BF16)
                dqs.append(jnp.dot(ds, k2, preferred_element_type=F32))
                dk2 = dk2 + lax.dot_general(ds, q, TN, preferred_element_type=F32)
                dsinks.append(jnp.broadcast_to(jnp.sum(-psink * delta, axis=0, keepdims=True), (1, LANES)))
            dks.append(dk2)
            dvs.append(dv2)
        dq_ref[...] = jnp.concatenate(dqs, axis=-1)
        dkp_ref[...] = jnp.concatenate([d[:BLOCK] for d in dks], axis=-1)
        dkc_ref[...] = jnp.concatenate([d[BLOCK:] for d in dks], axis=-1)
        dvp_ref[...] = jnp.concatenate([d[:BLOCK] for d in dvs], axis=-1)
        dvc_ref[...] = jnp.concatenate([d[BLOCK:] for d in dvs], axis=-1)

        @pl.when(n == 0)
        def _():
            dsink_ref[...] = jnp.zeros_like(dsink_ref)

        dsink_ref[...] += jnp.concatenate(dsinks, axis=0)

    cur = lambda n: (n, 0)
    prev = lambda n: (jnp.maximum(n - 1, 0), 0)
    kspec = pl.BlockSpec((BLOCK, kw), cur)
    return pl.pallas_call(
        body, name="attn_bwd", grid=(t // BLOCK,),
        in_specs=[pl.BlockSpec((BLOCK, cw), cur),
                  pl.BlockSpec((BLOCK, kw), prev), kspec,
                  pl.BlockSpec((BLOCK, kw), prev), kspec,
                  pl.BlockSpec(sink_rows.shape, lambda n: (0, 0)),
                  pl.BlockSpec((BLOCK, cw), cur)],
        out_specs=[pl.BlockSpec((BLOCK, cw), cur), kspec, kspec, kspec, kspec,
                   pl.BlockSpec((nq, LANES), lambda n: (0, 0))],
        out_shape=[_sds((t, cw), F32)] + [_sds((t, kw), F32)] * 4 + [_sds((nq, LANES), F32)],
        compiler_params=_params(("arbitrary",), BLOCK * (cw + 4 * kw) * 2 + 2 * BLOCK * cw * 4 + 4 * BLOCK * kw * 4, 12 << 20),
    )(qn, kn, kn, vb, vb, sink_rows, do)


def _mix_out(ca, o, woc, woa, proj):
    t, cw = ca.shape
    nb = woc.shape[2]
    d = N_DEV * nb
    tm = min(t, 1024)
    ga0 = (3 * cw + cw + 2 * (cw // 4)) // nb

    def body(ca_ref, o_ref, woc_ref, woa_ref, ga_ref, gb_ref, m_ref, ya_ref, yb_ref):
        ya = jnp.dot(ca_ref[...], woc_ref[...], preferred_element_type=F32)
        yb = jnp.dot(o_ref[...], woa_ref[...], preferred_element_type=F32)
        ya_ref[...] = ya.astype(BF16)
        yb_ref[...] = yb.astype(BF16)
        m_ref[...] = (_sigmoid(ga_ref[...]) * ya + _sigmoid(gb_ref[...]) * yb).astype(BF16)

    act = pl.BlockSpec((tm, cw), lambda i, j: (i, 0))
    wsp = pl.BlockSpec((None, cw, nb), lambda i, j: (j, 0, 0))
    osp = pl.BlockSpec((tm, nb), lambda i, j: (i, j))
    blocks = 2 * tm * cw * 2 + 2 * cw * nb * 2 + 2 * tm * nb * 4 + 3 * tm * nb * 2
    return pl.pallas_call(
        body, name="mix_out", grid=(t // tm, N_DEV),
        in_specs=[act, act, wsp, wsp,
                  pl.BlockSpec((tm, nb), lambda i, j: (i, ga0 + j)),
                  pl.BlockSpec((tm, nb), lambda i, j: (i, ga0 + N_DEV + j))],
        out_specs=[osp, osp, osp],
        out_shape=[_sds((t, d), BF16)] * 3,
        compiler_params=_params(("parallel", "parallel"), blocks, 6 * tm * nb * 4),
    )(ca, o, woc, woa, proj, proj)


def _mix_residual(merged, wo, x):
    t, d = x.shape
    tm = min(t, 512)

    def epilogue(acc, ins, outs):
        outs[0][...] = ins[2][...] + acc

    row = pl.BlockSpec((tm, d), lambda i, j, k: (i, 0))
    return _fused("mix_residual", (t // tm, 1, 1),
                  [(merged, row), (wo, pl.BlockSpec((d, d), lambda i, j, k: (0, 0))), (x, row)],
                  [(_sds((t, d), F32), row)], [(0, 1, NN)], epilogue, temp_bytes=2 * tm * d * 4)[0]


def _mix_bwd_gates(dx, wo, ya, yb, proj, cw):
    t, d = dx.shape
    tm = min(t, 512)
    tn = min(d, 512)
    ga0 = (4 * cw + 2 * (cw // 4)) // tn

    def epilogue(acc, ins, outs):
        sa, sb = _sigmoid(ins[4][...]), _sigmoid(ins[5][...])
        outs[0][...] = (acc * sa).astype(BF16)
        outs[1][...] = (acc * sb).astype(BF16)
        outs[2][0] = (acc * ins[2][...].astype(F32) * sa * (1.0 - sa)).astype(BF16)
        outs[2][1] = (acc * ins[3][...].astype(F32) * sb * (1.0 - sb)).astype(BF16)

    blk = pl.BlockSpec((tm, tn), lambda i, j, k: (i, j))
    return _fused("mix_bwd_gates", (t // tm, d // tn, 1),
                  [(dx, pl.BlockSpec((tm, d), lambda i, j, k: (i, 0))),
                   (wo, pl.BlockSpec((tn, d), lambda i, j, k: (j, 0))),
                   (ya, blk), (yb, blk),
                   (proj, pl.BlockSpec((tm, tn), lambda i, j, k: (i, ga0 + j))),
                   (proj, pl.BlockSpec((tm, tn), lambda i, j, k: (i, ga0 + d // tn + j)))],
                  [(_sds((t, d), BF16), blk), (_sds((t, d), BF16), blk),
                   (_sds((2, t, d), BF16), pl.BlockSpec((2, tm, tn), lambda i, j, k: (0, i, j)))],
                  [(0, 1, NT)], epilogue, temp_bytes=8 * tm * tn * 4)


def _tn_matmul(name, a, b, tm, out_dtype=BF16):
    t, m = a.shape
    n = b.shape[1]
    tk = min(t, 512)

    def epilogue(acc, ins, outs):
        outs[0][...] = acc.astype(out_dtype)

    return _fused(name, (m // tm, 1, t // tk),
                  [(a, pl.BlockSpec((tk, tm), lambda i, j, k: (k, i))),
                   (b, pl.BlockSpec((tk, n), lambda i, j, k: (k, 0)))],
                  [(_sds((m, n), out_dtype), pl.BlockSpec((tm, n), lambda i, j, k: (i, 0)))],
                  [(0, 1, TN)], epilogue, nk=t // tk, acc_shape=(tm, n), temp_bytes=tm * n * 4)[0]


def _out_proj_bwd_act(name, dy, w, deps=()):
    t, d = dy.shape
    kdim, nb = w.shape[1], w.shape[2]
    tm = min(t, 1024)

    def epilogue(acc, ins, outs):
        outs[0][...] = acc

    return _fused(name, (t // tm, 1, N_DEV),
                  [(dy, pl.BlockSpec((tm, nb), lambda i, j, k: (i, k))),
                   (w, pl.BlockSpec((None, kdim, nb), lambda i, j, k: (k, 0, 0)))],
                  [(_sds((t, kdim), F32), pl.BlockSpec((tm, kdim), lambda i, j, k: (i, 0)))],
                  [(0, 1, NT)], epilogue, nk=N_DEV, acc_shape=(tm, kdim), temp_bytes=tm * kdim * 4, deps=deps)[0]


def _out_proj_bwd_w(name, act, dy, nb):
    t, kdim = act.shape
    tk = min(t, 1024)

    def epilogue(acc, ins, outs):
        outs[0][...] = acc.astype(BF16)

    return _fused(name, (N_DEV, 1, t // tk),
                  [(act, pl.BlockSpec((tk, kdim), lambda i, j, k: (k, 0))),
                   (dy, pl.BlockSpec((tk, nb), lambda i, j, k: (k, i)))],
                  [(_sds((N_DEV, kdim, nb), BF16), pl.BlockSpec((None, kdim, nb), lambda i, j, k: (i, 0, 0)))],
                  [(0, 1, TN)], epilogue, nk=t // tk, acc_shape=(kdim, nb), temp_bytes=kdim * nb * 4)[0]


def _proj_bwd_act(dproj, w_in, deps=()):
    t, n = dproj.shape
    d = w_in.shape[0]
    tm = min(t, 512)
    tk = n // 4

    def epilogue(acc, ins, outs):
        outs[0][...] = acc

    return _fused("mix_bwd_dh", (t // tm, 1, 4),
                  [(dproj, pl.BlockSpec((tm, tk), lambda i, j, k: (i, k))),
                   (w_in, pl.BlockSpec((d, tk), lambda i, j, k: (0, k)))],
                  [(_sds((t, d), F32), pl.BlockSpec((tm, d), lambda i, j, k: (i, 0)))],
                  [(0, 1, NT)], epilogue, nk=4, acc_shape=(tm, d), temp_bytes=tm * d * 4, deps=deps)[0]


def _proj_bwd_w(h, dproj):
    t, d = h.shape
    n = dproj.shape[1]
    tn = n // 4
    tm = min(d, 1024)
    tk = min(t, 512)

    def epilogue(acc, ins, outs):
        outs[0][...] = acc.astype(BF16)

    return _fused("mix_bwd_dwin", (d // tm, 4, t // tk),
                  [(h, pl.BlockSpec((tk, tm), lambda i, j, k: (k, i))),
                   (dproj, pl.BlockSpec((tk, tn), lambda i, j, k: (k, j)))],
                  [(_sds((d, n), BF16), pl.BlockSpec((tm, tn), lambda i, j, k: (i, j)))],
                  [(0, 1, TN)], epilogue, nk=t // tk, acc_shape=(tm, tn), temp_bytes=tm * tn * 4)[0]


def _adamw_math(w, g, m, v):
    m = ADAM_B1 * m + (1.0 - ADAM_B1) * g
    v = ADAM_B2 * v + (1.0 - ADAM_B2) * (g * g)
    m_hat = m / (1.0 - ADAM_B1 ** ADAM_STEP)
    v_hat = v / (1.0 - ADAM_B2 ** ADAM_STEP)
    delta = -ADAM_LR * (m_hat / (jnp.sqrt(v_hat) + ADAM_EPS) + ADAM_WD * w)
    return delta, m, v


def _adamw(name, parts, w, m, v, tr):
    r, c = w.shape

    def body(p_ref, w_ref, m_ref, v_ref, g_out, d_out, m_out, v_out):
        g = p_ref[0].astype(F32)
        for s in range(1, N_DEV):
            g = g + p_ref[s].astype(F32)
        delta, mn, vn = _adamw_math(w_ref[...], g, m_ref[...], v_ref[...])
        g_out[...] = g
        d_out[...] = delta
        m_out[...] = mn
        v_out[...] = vn

    blk = pl.BlockSpec((tr, c), lambda i: (i, 0))
    blocks = N_DEV * tr * c * parts.dtype.itemsize + 7 * tr * c * 4
    return pl.pallas_call(
        body, name=name, grid=(r // tr,),
        in_specs=[pl.BlockSpec((N_DEV, tr, c), lambda i: (0, i, 0)), blk, blk, blk],
        out_specs=[blk] * 4, out_shape=[_sds((r, c), F32)] * 4,
        compiler_params=_params(("parallel",), blocks, 6 * tr * c * 4),
    )(parts, w, m, v)


def _adamw_chips(name, chip, own, landed, w, m, v, tr):
    r, c = w.shape

    def body(chip_ref, own_ref, land_ref, w_ref, m_ref, v_ref, g_out, d_out, m_out, v_out):
        mine = own_ref[...].astype(F32)
        g = jnp.zeros((tr, c), F32)
        for k in range(4):
            g = g + jnp.where(chip_ref[0] == k, mine, land_ref[k].astype(F32))
        delta, mn, vn = _adamw_math(w_ref[...], g, m_ref[...], v_ref[...])
        g_out[...] = g
        d_out[...] = delta
        m_out[...] = mn
        v_out[...] = vn

    blk = pl.BlockSpec((tr, c), lambda i, chip_ref: (i, 0))
    grid_spec = pltpu.PrefetchScalarGridSpec(
        num_scalar_prefetch=1, grid=(r // tr,),
        in_specs=[pl.BlockSpec((None, tr, c), lambda i, chip_ref: (chip_ref[0], i, 0)),
                  pl.BlockSpec((4, tr, c), lambda i, chip_ref: (0, i, 0)), blk, blk, blk],
        out_specs=[blk] * 4)
    blocks = 5 * tr * c * 2 + 7 * tr * c * 4
    return pl.pallas_call(
        body, name=name, grid_spec=grid_spec, out_shape=[_sds((r, c), F32)] * 4,
        compiler_params=_params(("parallel",), blocks, 6 * tr * c * 4),
    )(chip, own, landed, w, m, v)


def _rope_tables(t):
    half = ROT_DIM // 2
    inv_freq = 1.0 / (ROPE_THETA ** (jnp.arange(0, ROT_DIM, 2, dtype=F32) / ROT_DIM))
    ang = jnp.arange(t, dtype=F32)[:, None] * inv_freq[None, :]
    cos, sin = jnp.cos(ang), jnp.sin(ang)
    ones = jnp.ones((t, HEAD_DIM - ROT_DIM), F32)
    zeros = jnp.zeros((t, HEAD_DIM - half), F32)
    c = jnp.concatenate([cos, cos, ones], axis=1)
    sa = jnp.concatenate([-sin, zeros], axis=1)
    sb = jnp.concatenate([jnp.zeros((t, half), F32), sin, jnp.zeros((t, HEAD_DIM - ROT_DIM), F32)], axis=1)
    return tuple(jnp.tile(a, (1, LANES // HEAD_DIM)) for a in (c, sa, sb))


def _pad_rows(a, rows=8):
    return jnp.pad(a, ((0, rows - a.shape[0]), (0, 0)))


def kernel(x, g_ffn1, w_gu1, w_down1, g_mix, w_in, conv_w, q_norm_g, k_norm_g, sinks, w_out_conv, w_out_attn, w_o, g_ffn2, w_gu2, w_down2, loss_target, m_g_ffn1, m_w_gu1, m_w_down1, m_g_mix, m_w_in, m_conv_w, m_q_norm_g, m_k_norm_g, m_sinks, m_w_out_conv, m_w_out_attn, m_w_o, m_g_ffn2, m_w_gu2, m_w_down2, v_g_ffn1, v_w_gu1, v_w_down1, v_g_mix, v_w_in, v_conv_w, v_q_norm_g, v_k_norm_g, v_sinks, v_w_out_conv, v_w_out_attn, v_w_o, v_g_ffn2, v_w_gu2, v_w_down2):
    t, d = x.shape[1], x.shape[2]
    cw = d // 2
    kw = cw // GROUP
    nq = cw // HEAD_DIM
    xs, target = x.reshape(t, d), loss_target.reshape(t, d)
    me = 4 * lax.axis_index("x") + 2 * lax.axis_index("y") + lax.axis_index("c")

    big = {"w_gu1": w_gu1, "w_down1": w_down1, "w_in": w_in, "w_out_conv": w_out_conv,
           "w_out_attn": w_out_attn, "w_o": w_o, "w_gu2": w_gu2, "w_down2": w_down2}
    big_m = {"w_gu1": m_w_gu1, "w_down1": m_w_down1, "w_in": m_w_in, "w_out_conv": m_w_out_conv,
             "w_out_attn": m_w_out_attn, "w_o": m_w_o, "w_gu2": m_w_gu2, "w_down2": m_w_down2}
    big_v = {"w_gu1": v_w_gu1, "w_down1": v_w_down1, "w_in": v_w_in, "w_out_conv": v_w_out_conv,
             "w_out_attn": v_w_out_attn, "w_o": v_w_o, "w_gu2": v_w_gu2, "w_down2": v_w_down2}
    names = list(big)

    tiles = {"w_gu1": 256, "w_gu2": 256, "w_in": 256, "w_down1": 176, "w_down2": 176,
             "w_out_conv": 1024, "w_out_attn": 1024, "w_o": 128}

    def row_tile(n):
        r = big[n].shape[1]
        return tiles[n] if r % tiles[n] == 0 else r

    me_arr = me.astype(jnp.int32).reshape(1)
    sources = [(n, big[n][0], BF16, row_tile(n)) for n in names] + [("conv_w", _pad_rows(conv_w[0]), F32, 8)]
    issue_order = [0, 1, 2, 8, 3, 4, 5, 6, 7]
    first = _place_shard("place_" + names[0], sources[0][1], BF16, me_arr, sources[0][3])
    started = [_gather_start("gather_start_first", [first])]
    rest = [_place_shard("place_" + sources[i][0], sources[i][1], sources[i][2], me_arr, sources[i][3],
                         deps=(started[0][3],)) for i in issue_order[1:]]
    started.append(_gather_start("gather_start_rest", rest))
    where = {0: (0, 0)}
    where.update({i: (1, p) for p, i in enumerate(issue_order[1:])})

    def fetch(tag, idxs, after):
        call = where[idxs[0]][0]
        send, recv, stacks, _ = started[call]
        positions = [where[i][1] for i in idxs]
        got = _gather_wait("gather_wait_" + tag, positions, send, recv, [stacks[p] for p in positions], after)
        return _forward_to_sibling("gather_forward_" + tag, got)

    rope_tabs = _rope_tables(t)
    gq = jnp.tile(q_norm_g, (1, nq))
    gk = jnp.tile(k_norm_g, (1, nq // GROUP))
    sink_rows = jnp.broadcast_to(sinks[0][:, None], (nq, LANES))

    wts = {}
    h1 = _rms_fwd("ffn1_norm", xs, g_ffn1)
    wts["w_gu1"], = fetch("gu1", [0], h1)
    gu1, a1 = _ffn_up("ffn1_up", h1, wts["w_gu1"])
    wts["w_down1"], = fetch("down1", [1], a1)
    wd1 = wts["w_down1"].reshape(-1, d)
    x1 = _ffn_down("ffn1_down", a1, wd1, xs)
    h2 = _rms_fwd("mix_norm", x1, g_mix)
    wts["w_in"], conv_land = fetch("in", [2, 8], h2)
    w_in_full = jnp.transpose(wts["w_in"], (1, 0, 2)).reshape(d, -1)
    conv_full = jnp.transpose(conv_land, (1, 0, 2)).reshape(8, cw)
    proj = _proj(h2, w_in_full)
    ca = _conv_fwd(proj, conv_full)
    qn, kn, vb = _qk_prep(proj, gq, gk, rope_tabs, cw, kw)
    o = _attn_fwd(qn, kn, vb, sink_rows)
    wts["w_out_conv"], wts["w_out_attn"] = fetch("out", [3, 4], o)
    merged, ya, yb = _mix_out(ca, o, wts["w_out_conv"], wts["w_out_attn"], proj)
    wts["w_o"], = fetch("o", [5], merged)
    wo = wts["w_o"].reshape(d, d)
    x2 = _mix_residual(merged, wo, x1)
    h3 = _rms_fwd("ffn2_norm", x2, g_ffn2)
    wts["w_gu2"], = fetch("gu2", [6], h3)
    gu2, a2 = _ffn_up("ffn2_up", h3, wts["w_gu2"])
    wts["w_down2"], = fetch("down2", [7], a2)
    wd2 = wts["w_down2"].reshape(-1, d)
    y = _ffn_down("ffn2_down", a2, wd2, x2)
    dy, sq = _loss_dy(y, target)
    loss = lax.psum(sq[0, 0] * (0.5 / d), ("x", "y", "c"))

    core = lax.axis_index("c").astype(jnp.int32).reshape(1)
    chip = (2 * lax.axis_index("x") + lax.axis_index("y")).astype(jnp.int32).reshape(1)
    def pair_start(tag, group, grads):
        stacks = [grads[n].reshape((4, 2) + big[n].shape[1:]) for n in group]
        lands = [lax.empty((4,) + big[n].shape[1:], BF16) for n in group]
        return _pair_start("rs_pair_start_" + tag, stacks, lands)

    def chip_start(tag, group, pending, after):
        send, recv, stacks, lands, _ = pending
        stacks, lands = _pair_wait("rs_pair_wait_" + tag, send, recv, stacks, lands, after)
        parts = [_pair_add("rs_pair_add_" + n, st, ld, core, row_tile(n)) for n, st, ld in zip(group, stacks, lands)]
        lands2 = [lax.empty((4,) + big[n].shape[1:], BF16) for n in group]
        return _chip_start("rs_chip_start_" + tag, parts, lands2)

    group_a, group_b, group_c = ["w_down2", "w_gu2"], ["w_o", "w_out_conv", "w_out_attn"], ["w_in"]
    group_d, group_e = ["w_down1"], ["w_gu1"]
    g = {}
    dgu2 = _ffn_bwd_act("ffn2_bwd_act", dy, wd2, gu2)
    g["w_down2"] = _ffn_bwd_dwd("ffn2_bwd_dwd", a2, dy)
    g["w_gu2"] = _ffn_bwd_dwgu("ffn2_bwd_dwgu", h3, dgu2)
    pend_a = pair_start("a", group_a, g)
    dh3 = _ffn_bwd_dh("ffn2_bwd_dh", dgu2, wts["w_gu2"], deps=(pend_a[4],))
    ring_a = chip_start("a", group_a, pend_a, dh3)
    dx2, dg_ffn2 = _rms_bwd("ffn2_bwd_rms", x2, g_ffn2, dh3, dy, deps=(ring_a[4],))

    dya, dyb, dgates = _mix_bwd_gates(dx2, wo, ya, yb, proj, cw)
    g["w_o"] = _tn_matmul("mix_bwd_dwo", merged, dx2, min(d, 1024))
    g["w_out_conv"] = _out_proj_bwd_w("mix_bwd_dwoc", ca, dya, d // N_DEV)
    g["w_out_attn"] = _out_proj_bwd_w("mix_bwd_dwoa", o, dyb, d // N_DEV)
    pend_b = pair_start("b", group_b, g)
    dca = _out_proj_bwd_act("mix_bwd_dca", dya, wts["w_out_conv"], deps=(pend_b[4],))
    do = _out_proj_bwd_act("mix_bwd_do", dyb, wts["w_out_attn"])
    ring_b = chip_start("b", group_b, pend_b, do)
    d3, dconv_w = _conv_bwd(proj, conv_full, dca, deps=(ring_b[4],))
    dq, dkc, dkp, dvc, dvp, dsink = _attn_bwd(qn, kn, vb, sink_rows, do)
    dqkv, dgq, dgk = _qk_prep_bwd(proj, gq, gk, rope_tabs, dq, dkc, dkp, dvc, dvp, cw, kw)
    dproj = jnp.concatenate([d3[0], d3[1], d3[2], dqkv, dgates[0], dgates[1]], axis=1)
    dw_in = _proj_bwd_w(h2, dproj)
    g["w_in"] = jnp.transpose(dw_in.reshape(d, N_DEV, -1), (1, 0, 2))
    pend_c = pair_start("c", group_c, g)
    dh2 = _proj_bwd_act(dproj, w_in_full, deps=(pend_c[4],))
    ring_c = chip_start("c", group_c, pend_c, dh2)
    dx1, dg_mix = _rms_bwd("mix_bwd_rms", x1, g_mix, dh2, dx2, deps=(ring_c[4],))

    g["w_down1"] = _ffn_bwd_dwd("ffn1_bwd_dwd", a1, dx1)
    pend_d = pair_start("d", group_d, g)
    dgu1 = _ffn_bwd_act("ffn1_bwd_act", dx1, wd1, gu1, deps=(pend_d[4],))
    ring_d = chip_start("d", group_d, pend_d, dgu1)
    g["w_gu1"] = _ffn_bwd_dwgu("ffn1_bwd_dwgu", h1, dgu1, deps=(ring_d[4],))
    pend_e = pair_start("e", group_e, g)
    dh1 = _ffn_bwd_dh("ffn1_bwd_dh", dgu1, wts["w_gu1"], deps=(pend_e[4],))
    ring_e = chip_start("e", group_e, pend_e, dh1)
    grad_x, dg_ffn1 = _rms_bwd("ffn1_bwd_rms", xs, g_ffn1, dh1, dx1, deps=(ring_e[4],))

    big_out = {}
    after = grad_x
    for tag, group, ring in (("a", group_a, ring_a), ("b", group_b, ring_b), ("c", group_c, ring_c),
                             ("d", group_d, ring_d), ("e", group_e, ring_e)):
        send, recv, parts, lands2, _ = ring
        parts, lands2 = _chip_wait("rs_chip_wait_" + tag, send, recv, parts, lands2, after)
        for n, own, landed in zip(group, parts, lands2):
            res = _adamw_chips("adamw_" + n, chip, own, landed, big[n][0], big_m[n][0], big_v[n][0], row_tile(n))
            big_out[n] = [a[None] for a in res]
            after = res[0]

    small = {"g_ffn1": dg_ffn1[0:1], "g_mix": dg_mix[0:1], "g_ffn2": dg_ffn2[0:1],
             "q_norm_g": dgq[0:1, :HEAD_DIM], "k_norm_g": dgk[0:1, :HEAD_DIM], "sinks": dsink[:, 0][None],
             "conv_w": dconv_w[0:CONV_K].reshape(1, -1)}
    small_w = {"g_ffn1": g_ffn1, "g_mix": g_mix, "g_ffn2": g_ffn2, "q_norm_g": q_norm_g, "k_norm_g": k_norm_g,
               "sinks": sinks, "conv_w": None}
    small_m = {"g_ffn1": m_g_ffn1, "g_mix": m_g_mix, "g_ffn2": m_g_ffn2, "q_norm_g": m_q_norm_g,
               "k_norm_g": m_k_norm_g, "sinks": m_sinks, "conv_w": m_conv_w}
    small_v = {"g_ffn1": v_g_ffn1, "g_mix": v_g_mix, "g_ffn2": v_g_ffn2, "q_norm_g": v_q_norm_g,
               "k_norm_g": v_k_norm_g, "sinks": v_sinks, "conv_w": v_conv_w}
    snames = list(small)
    widths = [small[n].shape[1] for n in snames]
    total = sum(widths)
    rows = -(-total // LANES)
    rows = -(-rows // 8) * 8

    def pack(vals):
        flat = jnp.concatenate([v.reshape(1, -1) for v in vals], axis=1)
        return jnp.pad(flat, ((0, 0), (0, rows * LANES - total))).reshape(rows, LANES)

    csh = cw // N_DEV

    def place_conv(local, fill):
        full = jnp.full((CONV_K, cw), fill, F32)
        return lax.dynamic_update_slice(full, local, (0, me * csh)).reshape(1, -1)

    pw = pack([small_w[n] if n != "conv_w" else place_conv(conv_w[0], 0.0) for n in snames])
    pm = pack([small_m[n] if n != "conv_w" else place_conv(m_conv_w[0], 0.0) for n in snames])
    pv = pack([small_v[n] if n != "conv_w" else place_conv(v_conv_w[0], 1.0) for n in snames])
    parts = _exchange("gather_small_grads", [pack([small[n] for n in snames])], gather=True, deps=(after,))[0]
    sg, sd, sm, sv = [a.reshape(1, -1) for a in _adamw("adamw_small", parts, pw, pm, pv, rows)]

    def unpack(flat, n):
        off = sum(widths[:snames.index(n)])
        piece = flat[:, off:off + widths[snames.index(n)]]
        if n == "conv_w":
            piece = lax.dynamic_slice(piece.reshape(CONV_K, cw), (0, me * csh), (CONV_K, csh))[None]
        return piece

    order = ["g_ffn1", "w_gu1", "w_down1", "g_mix", "w_in", "conv_w", "q_norm_g", "k_norm_g", "sinks",
             "w_out_conv", "w_out_attn", "w_o", "g_ffn2", "w_gu2", "w_down2"]
    outs = [loss, grad_x[None]]
    for idx, flat in enumerate((sg, sd, sm, sv)):
        for n in order:
            outs.append(big_out[n][idx] if n in big_out else unpack(flat, n))
    return tuple(outs)
```

```python
import functools

import jax
import jax.numpy as jnp
from jax import lax
from jax.experimental import pallas as pl
from jax.experimental.pallas import tpu as pltpu

F32 = jnp.float32
BF16 = jnp.bfloat16

N_DEV = 8
HEAD_DIM = 64
GROUP = 4
BLOCK = 128
ROT_DIM = 16
ROPE_THETA = 500000.0
RMS_EPS = 1e-6
NEG_INF = -1e30
ATTN_SCALE = HEAD_DIM ** -0.5
CONV_K = 3
LANES = 128
VMEM_BYTES_V7X = 64 * 1024 * 1024
VMEM_CAP = VMEM_BYTES_V7X - 6 * 1024 * 1024

ADAM_LR = 0.001
ADAM_B1 = 0.9
ADAM_B2 = 0.999
ADAM_EPS = 1e-08
ADAM_WD = 0.01
ADAM_STEP = 10

NN = (((1,), (0,)), ((), ()))
NT = (((1,), (1,)), ((), ()))
TN = (((0,), (0,)), ((), ()))

MESH = pl.DeviceIdType.MESH


def _nbytes(shape, dtype):
    n = 1
    for s in shape:
        if s is not None:
            n *= s
    return n * jnp.dtype(dtype).itemsize


def _params(semantics, block_bytes, temp_bytes):
    assert 2 * block_bytes + temp_bytes <= VMEM_CAP, (block_bytes, temp_bytes)
    return pltpu.CompilerParams(dimension_semantics=semantics, vmem_limit_bytes=VMEM_CAP)


def _fused(name, grid, ins, outs, dots, epilogue, *, nk=1, acc_shape=None, temp_bytes=0,
           semantics=("parallel", "parallel", "arbitrary"), deps=()):
    n_in, n_out = len(ins), len(outs)
    n_dep = len(deps)

    def body(*refs):
        in_refs, out_refs = refs[:n_in], refs[n_in + n_dep:n_in + n_dep + n_out]
        scratch = refs[n_in + n_dep + n_out:]

        def products():
            if callable(dots):
                return dots(in_refs)
            total = None
            for ai, bi, contract in dots:
                a, b = in_refs[ai][...], in_refs[bi][...]
                a = a if a.dtype == BF16 else a.astype(BF16)
                b = b if b.dtype == BF16 else b.astype(BF16)
                p = lax.dot_general(a, b, contract, preferred_element_type=F32)
                total = p if total is None else total + p
            return total

        if nk == 1:
            epilogue(products() if dots else None, in_refs, out_refs)
        else:
            acc = scratch[0]
            k = pl.program_id(2)

            @pl.when(k == 0)
            def _():
                acc[...] = jnp.zeros_like(acc)

            acc[...] += products()

            @pl.when(k == nk - 1)
            def _():
                epilogue(acc[...], in_refs, out_refs)

    block_bytes = sum(_nbytes(spec.block_shape, a.dtype) for a, spec in ins)
    block_bytes += sum(_nbytes(spec.block_shape, s.dtype) for s, spec in outs)
    scratch_shapes = []
    if nk > 1:
        scratch_shapes.append(pltpu.VMEM(acc_shape, F32))
        temp_bytes += _nbytes(acc_shape, F32)
    res = pl.pallas_call(
        body, name=name, grid=grid,
        in_specs=[spec for _, spec in ins] + [pl.BlockSpec(memory_space=pl.ANY)] * n_dep,
        out_specs=[spec for _, spec in outs],
        out_shape=[s for s, _ in outs],
        scratch_shapes=scratch_shapes,
        compiler_params=_params(semantics, block_bytes, temp_bytes),
    )(*[a for a, _ in ins], *deps)
    return res


def _sds(shape, dtype):
    return jax.ShapeDtypeStruct(shape, dtype)


def _sigmoid(x):
    return jax.nn.sigmoid(x)


def _exchange(name, arrays, gather, deps=()):
    n = len(arrays)
    out_shapes = [((N_DEV,) + a.shape) if gather else a.shape for a in arrays]

    def body(*refs):
        srcs, dsts = refs[:n], refs[n + len(deps):2 * n + len(deps)]
        send_sems, recv_sems, local_sems = refs[2 * n + len(deps):]
        x, y, c = lax.axis_index("x"), lax.axis_index("y"), lax.axis_index("c")
        me = 4 * x + 2 * y + c
        copies = []
        for w in range(n):
            own = srcs[w] if gather else srcs[w].at[me]
            local = pltpu.make_async_copy(own, dsts[w].at[me], local_sems.at[w])
            local.start()
            copies.append(local)
            for k in range(1, N_DEV):
                px = (1 - x) if (k & 4) else x
                py = (1 - y) if (k & 2) else y
                pc = (1 - c) if (k & 1) else c
                peer = 4 * px + 2 * py + pc
                cp = pltpu.make_async_remote_copy(
                    src_ref=srcs[w] if gather else srcs[w].at[peer],
                    dst_ref=dsts[w].at[me],
                    send_sem=send_sems.at[w * (N_DEV - 1) + k - 1],
                    recv_sem=recv_sems.at[w * (N_DEV - 1) + k - 1],
                    device_id=(px, py, pc), device_id_type=MESH)
                cp.start()
                copies.append(cp)
        for cp in copies:
            cp.wait()

    hbm = pl.BlockSpec(memory_space=pltpu.HBM)
    return pl.pallas_call(
        body, name=name,
        in_specs=[hbm] * n + [pl.BlockSpec(memory_space=pl.ANY)] * len(deps), out_specs=[hbm] * n,
        out_shape=[_sds(s, a.dtype) for s, a in zip(out_shapes, arrays)],
        scratch_shapes=[pltpu.SemaphoreType.DMA((n * (N_DEV - 1),)),
                        pltpu.SemaphoreType.DMA((n * (N_DEV - 1),)),
                        pltpu.SemaphoreType.DMA((n,))],
    )(*arrays, *deps)


_HBM = pl.BlockSpec(memory_space=pltpu.HBM)
_SEM = pl.BlockSpec(memory_space=pltpu.SEMAPHORE)
_ANY = pl.BlockSpec(memory_space=pl.ANY)
_EFFECT = pltpu.SideEffectType.DATAFLOW_SIDE_EFFECTING
N_TARGETS = 4


def _mesh_pos():
    return lax.axis_index("x"), lax.axis_index("y"), lax.axis_index("c")


def _chip_peers(x, y, c):
    return [(1 - x, y, c), (x, 1 - y, c), (1 - x, 1 - y, c)]


def _dev_index(pos):
    return 4 * pos[0] + 2 * pos[1] + pos[2]


def _hbm_like(a):
    return pltpu.HBM(a.shape, a.dtype)


def _place_shard(name, w, out_dtype, me, tr, deps=()):
    r, c = w.shape
    n_dep = len(deps)

    def body(me_ref, w_ref, *rest):
        rest[n_dep][...] = w_ref[...].astype(out_dtype)

    grid_spec = pltpu.PrefetchScalarGridSpec(
        num_scalar_prefetch=1, grid=(r // tr,),
        in_specs=[pl.BlockSpec((tr, c), lambda i, me_ref: (i, 0))] + [_ANY] * n_dep,
        out_specs=pl.BlockSpec((None, tr, c), lambda i, me_ref: (me_ref[0], i, 0)))
    return pl.pallas_call(
        body, name=name, grid_spec=grid_spec, out_shape=_sds((N_DEV, r, c), out_dtype),
        compiler_params=_params(("parallel",), tr * c * 6, tr * c * 4),
    )(me, w, *deps)


def _gather_start(name, lands):
    n = len(lands)

    def body(*refs):
        bufs = refs[:n]
        send, recv = refs[n], refs[n + 1]
        token = refs[-1]
        x, y, c = _mesh_pos()
        me = _dev_index((x, y, c))
        targets = [(x, y, 1 - c)] + _chip_peers(x, y, c)
        for w in range(n):
            for k, to in enumerate(targets):
                pltpu.make_async_remote_copy(
                    src_ref=bufs[w].at[me], dst_ref=bufs[w].at[me],
                    send_sem=send.at[N_TARGETS * w + k], recv_sem=recv.at[N_TARGETS * w + k],
                    device_id=to, device_id_type=MESH).start()
        token[...] = jnp.zeros_like(token)

    sems = pltpu.SemaphoreType.DMA((N_TARGETS * n,))
    outs = pl.pallas_call(
        body, name=name,
        in_specs=[_HBM] * n, out_specs=[_SEM, _SEM] + [_HBM] * n + [_token_spec()],
        out_shape=[sems, sems] + [_hbm_like(a) for a in lands] + [_sds((8, LANES), F32)],
        input_output_aliases={i: 2 + i for i in range(n)},
        compiler_params=pltpu.CompilerParams(has_side_effects=_EFFECT),
    )(*lands)
    return outs[0], outs[1], list(outs[2:2 + n]), outs[-1]


def _gather_wait(name, positions, send, recv, lands, after):
    m = len(positions)

    def body(*refs):
        bufs = refs[:m]
        send_sems, recv_sems = refs[m], refs[m + 1]
        x, y, c = _mesh_pos()
        me = _dev_index((x, y, c))
        sources = [(x, y, 1 - c)] + _chip_peers(x, y, c)
        for j, w in enumerate(positions):
            for k, frm in enumerate(sources):
                cp = pltpu.make_async_remote_copy(
                    src_ref=bufs[j].at[me], dst_ref=bufs[j].at[_dev_index(frm)],
                    send_sem=send_sems.at[N_TARGETS * w + k], recv_sem=recv_sems.at[N_TARGETS * w + k],
                    device_id=frm, device_id_type=MESH)
                cp.wait_send()
                cp.wait_recv()

    outs = pl.pallas_call(
        body, name=name,
        in_specs=[_HBM] * m + [_SEM, _SEM, _ANY], out_specs=[_HBM] * m,
        out_shape=[_hbm_like(a) for a in lands],
        input_output_aliases={i: i for i in range(m)},
        compiler_params=pltpu.CompilerParams(has_side_effects=_EFFECT),
    )(*lands, send, recv, after)
    return list(outs)


def _forward_to_sibling(name, lands):
    m = len(lands)

    def body(*refs):
        bufs = refs[m:2 * m]
        send_sems, recv_sems = refs[2 * m], refs[2 * m + 1]
        x, y, c = _mesh_pos()
        copies = []
        for j in range(m):
            for k, chip in enumerate(_chip_peers(x, y, c)):
                block = bufs[j].at[_dev_index(chip)]
                cp = pltpu.make_async_remote_copy(
                    src_ref=block, dst_ref=block,
                    send_sem=send_sems.at[3 * j + k], recv_sem=recv_sems.at[3 * j + k],
                    device_id=(x, y, 1 - c), device_id_type=MESH)
                cp.start()
                copies.append(cp)
        for cp in copies:
            cp.wait()

    outs = pl.pallas_call(
        body, name=name,
        in_specs=[_HBM] * m, out_specs=[_HBM] * m,
        out_shape=[_sds(a.shape, a.dtype) for a in lands],
        input_output_aliases={i: i for i in range(m)},
        scratch_shapes=[pltpu.SemaphoreType.DMA((3 * m,)), pltpu.SemaphoreType.DMA((3 * m,))],
    )(*lands)
    return list(outs)


def _token_spec():
    return pl.BlockSpec(memory_space=pltpu.VMEM)


def _pair_start(name, stacks, lands):
    n = len(stacks)

    def body(*refs):
        srcs, dsts = refs[:n], refs[n:2 * n]
        send, recv = refs[2 * n], refs[2 * n + 1]
        token = refs[-1]
        x, y, c = _mesh_pos()
        for w in range(n):
            for chip in range(4):
                pltpu.make_async_remote_copy(
                    src_ref=srcs[w].at[chip, 1 - c], dst_ref=dsts[w].at[chip],
                    send_sem=send.at[4 * w + chip], recv_sem=recv.at[4 * w + chip],
                    device_id=(x, y, 1 - c), device_id_type=MESH).start()
        token[...] = jnp.zeros_like(token)

    sems = pltpu.SemaphoreType.DMA((4 * n,))
    outs = pl.pallas_call(
        body, name=name,
        in_specs=[_HBM] * (2 * n), out_specs=[_SEM, _SEM] + [_HBM] * (2 * n) + [_token_spec()],
        out_shape=[sems, sems] + [_hbm_like(a) for a in stacks] + [_hbm_like(a) for a in lands] + [_sds((8, LANES), F32)],
        input_output_aliases={i: 2 + i for i in range(2 * n)},
        compiler_params=pltpu.CompilerParams(has_side_effects=_EFFECT),
    )(*stacks, *lands)
    return outs[0], outs[1], list(outs[2:2 + n]), list(outs[2 + n:2 + 2 * n]), outs[-1]


def _pair_wait(name, send, recv, stacks, lands, after):
    n = len(stacks)

    def body(*refs):
        srcs, dsts = refs[:n], refs[n:2 * n]
        send_sems, recv_sems = refs[2 * n], refs[2 * n + 1]
        x, y, c = _mesh_pos()
        for w in range(n):
            for chip in range(4):
                cp = pltpu.make_async_remote_copy(
                    src_ref=srcs[w].at[chip, 1 - c], dst_ref=dsts[w].at[chip],
                    send_sem=send_sems.at[4 * w + chip], recv_sem=recv_sems.at[4 * w + chip],
                    device_id=(x, y, 1 - c), device_id_type=MESH)
                cp.wait_send()
                cp.wait_recv()

    outs = pl.pallas_call(
        body, name=name,
        in_specs=[_HBM] * (2 * n) + [_SEM, _SEM, _ANY], out_specs=[_HBM] * (2 * n),
        out_shape=[_hbm_like(a) for a in stacks] + [_hbm_like(a) for a in lands],
        input_output_aliases={i: i for i in range(2 * n)},
        compiler_params=pltpu.CompilerParams(has_side_effects=_EFFECT),
    )(*stacks, *lands, send, recv, after)
    return list(outs[:n]), list(outs[n:])


def _pair_add(name, stack, land, core, tr):
    _, _, r, c = stack.shape

    def body(core_ref, a_ref, b_ref, o_ref):
        o_ref[...] = (a_ref[...].astype(F32) + b_ref[...].astype(F32)).astype(BF16)

    grid_spec = pltpu.PrefetchScalarGridSpec(
        num_scalar_prefetch=1, grid=(4, r // tr),
        in_specs=[pl.BlockSpec((None, None, tr, c), lambda k, i, core_ref: (k, core_ref[0], i, 0)),
                  pl.BlockSpec((None, tr, c), lambda k, i, core_ref: (k, i, 0))],
        out_specs=pl.BlockSpec((None, tr, c), lambda k, i, core_ref: (k, i, 0)))
    return pl.pallas_call(
        body, name=name, grid_spec=grid_spec, out_shape=_sds((4, r, c), BF16),
        compiler_params=_params(("parallel", "parallel"), 3 * tr * c * 2, 3 * tr * c * 4),
    )(core, stack, land)


def _chip_start(name, parts, lands):
    n = len(parts)

    def body(*refs):
        srcs, dsts = refs[:n], refs[n:2 * n]
        send, recv = refs[2 * n], refs[2 * n + 1]
        token = refs[-1]
        x, y, c = _mesh_pos()
        for w in range(n):
            for k, to in enumerate(_chip_peers(x, y, c)):
                pltpu.make_async_remote_copy(
                    src_ref=srcs[w].at[2 * to[0] + to[1]], dst_ref=dsts[w].at[2 * x + y],
                    send_sem=send.at[3 * w + k], recv_sem=recv.at[3 * w + k],
                    device_id=to, device_id_type=MESH).start()
        token[...] = jnp.zeros_like(token)

    sems = pltpu.SemaphoreType.DMA((3 * n,))
    outs = pl.pallas_call(
        body, name=name,
        in_specs=[_HBM] * (2 * n), out_specs=[_SEM, _SEM] + [_HBM] * (2 * n) + [_token_spec()],
        out_shape=[sems, sems] + [_hbm_like(a) for a in parts] + [_hbm_like(a) for a in lands] + [_sds((8, LANES), F32)],
        input_output_aliases={i: 2 + i for i in range(2 * n)},
        compiler_params=pltpu.CompilerParams(has_side_effects=_EFFECT),
    )(*parts, *lands)
    return outs[0], outs[1], list(outs[2:2 + n]), list(outs[2 + n:2 + 2 * n]), outs[-1]


def _chip_wait(name, send, recv, parts, lands, after):
    n = len(parts)

    def body(*refs):
        srcs, dsts = refs[:n], refs[n:2 * n]
        send_sems, recv_sems = refs[2 * n], refs[2 * n + 1]
        x, y, c = _mesh_pos()
        for w in range(n):
            for k, frm in enumerate(_chip_peers(x, y, c)):
                chip = 2 * frm[0] + frm[1]
                cp = pltpu.make_async_remote_copy(
                    src_ref=srcs[w].at[chip], dst_ref=dsts[w].at[chip],
                    send_sem=send_sems.at[3 * w + k], recv_sem=recv_sems.at[3 * w + k],
                    device_id=frm, device_id_type=MESH)
                cp.wait_send()
                cp.wait_recv()

    outs = pl.pallas_call(
        body, name=name,
        in_specs=[_HBM] * (2 * n) + [_SEM, _SEM, _ANY], out_specs=[_HBM] * (2 * n),
        out_shape=[_hbm_like(a) for a in parts] + [_hbm_like(a) for a in lands],
        input_output_aliases={i: i for i in range(2 * n)},
        compiler_params=pltpu.CompilerParams(has_side_effects=_EFFECT),
    )(*parts, *lands, send, recv, after)
    return list(outs[:n]), list(outs[n:])


def _row_tile(t):
    return min(t, 256)


def _rms_fwd(name, x, g):
    t, d = x.shape
    tm = _row_tile(t)

    def epilogue(_, ins, outs):
        xv = ins[0][...]
        r = lax.rsqrt(jnp.mean(xv * xv, axis=-1, keepdims=True) + RMS_EPS)
        outs[0][...] = (xv * r * ins[1][...]).astype(BF16)

    row = pl.BlockSpec((tm, d), lambda i, j, k: (i, 0))
    vec = pl.BlockSpec((1, d), lambda i, j, k: (0, 0))
    return _fused(name, (t // tm, 1, 1), [(x, row), (g, vec)], [(_sds((t, d), BF16), row)], [], epilogue,
                  temp_bytes=4 * tm * d * 4)[0]


def _rms_bwd(name, x, g, dh, resid, deps=()):
    t, d = x.shape
    tm = _row_tile(t)

    def epilogue(_, ins, outs):
        xv, gv, dhv = ins[0][...], ins[1][...], ins[2][...]
        r = lax.rsqrt(jnp.mean(xv * xv, axis=-1, keepdims=True) + RMS_EPS)
        xh = xv * r
        u = dhv * gv
        dot = jnp.mean(u * xh, axis=-1, keepdims=True)
        outs[0][...] = ins[3][...] + r * (u - xh * dot)

        @pl.when(pl.program_id(0) == 0)
        def _():
            outs[1][...] = jnp.zeros_like(outs[1])

        outs[1][0:1, :] += jnp.sum(dhv * xh, axis=0, keepdims=True)

    row = pl.BlockSpec((tm, d), lambda i, j, k: (i, 0))
    vec = pl.BlockSpec((1, d), lambda i, j, k: (0, 0))
    acc = pl.BlockSpec((8, d), lambda i, j, k: (0, 0))
    return _fused(name, (t // tm, 1, 1), [(x, row), (g, vec), (dh, row), (resid, row)],
                  [(_sds((t, d), F32), row), (_sds((8, d), F32), acc)], [], epilogue,
                  temp_bytes=6 * tm * d * 4, semantics=("arbitrary", "arbitrary", "arbitrary"), deps=deps)


def _loss_dy(y, target):
    t, d = y.shape
    tm = _row_tile(t)

    def epilogue(_, ins, outs):
        e = ins[0][...] - ins[1][...]
        outs[0][...] = e * (1.0 / d)

        @pl.when(pl.program_id(0) == 0)
        def _():
            outs[1][...] = jnp.zeros_like(outs[1])

        part = jnp.sum(jnp.sum(e * e, axis=1, keepdims=True), axis=0, keepdims=True)
        outs[1][...] += jnp.broadcast_to(part, outs[1].shape)

    row = pl.BlockSpec((tm, d), lambda i, j, k: (i, 0))
    acc = pl.BlockSpec((8, LANES), lambda i, j, k: (0, 0))
    return _fused("loss_dy", (t // tm, 1, 1), [(y, row), (target, row)],
                  [(_sds((t, d), F32), row), (_sds((8, LANES), F32), acc)], [], epilogue,
                  temp_bytes=3 * tm * d * 4, semantics=("arbitrary", "arbitrary", "arbitrary"))


def _ffn_up(name, h, wgu):
    t, d = h.shape
    nb = wgu.shape[2]
    f = 4 * nb
    tm = _row_tile(t)

    def body(h_ref, wg_ref, wu_ref, gu_ref, a_ref):
        hv = h_ref[...]
        g = jnp.dot(hv, wg_ref[...], preferred_element_type=F32)
        u = jnp.dot(hv, wu_ref[...], preferred_element_type=F32)
        gu_ref[0] = g.astype(BF16)
        gu_ref[1] = u.astype(BF16)
        a_ref[...] = (g * _sigmoid(g) * u).astype(BF16)

    blocks = tm * d * 2 + 2 * d * nb * 2 + 3 * tm * nb * 2
    return pl.pallas_call(
        body, name=name, grid=(4, t // tm),
        in_specs=[pl.BlockSpec((tm, d), lambda j, i: (i, 0)),
                  pl.BlockSpec((None, d, nb), lambda j, i: (j, 0, 0)),
                  pl.BlockSpec((None, d, nb), lambda j, i: (j + 4, 0, 0))],
        out_specs=[pl.BlockSpec((2, tm, nb), lambda j, i: (0, i, j)),
                   pl.BlockSpec((tm, nb), lambda j, i: (i, j))],
        out_shape=[_sds((2, t, f), BF16), _sds((t, f), BF16)],
        compiler_params=_params(("parallel", "parallel"), blocks, 5 * tm * nb * 4),
    )(h, wgu, wgu)


def _ffn_down(name, a, wd, x):
    t, f = a.shape
    d = wd.shape[1]
    tm = min(t, 512)
    tk = f // 4

    def epilogue(acc, ins, outs):
        outs[0][...] = ins[2][...] + 0.5 * acc

    return _fused(name, (t // tm, 1, 4),
                  [(a, pl.BlockSpec((tm, tk), lambda i, j, k: (i, k))),
                   (wd, pl.BlockSpec((tk, d), lambda i, j, k: (k, 0))),
                   (x, pl.BlockSpec((tm, d), lambda i, j, k: (i, 0)))],
                  [(_sds((t, d), F32), pl.BlockSpec((tm, d), lambda i, j, k: (i, 0)))],
                  [(0, 1, NN)], epilogue, nk=4, acc_shape=(tm, d), temp_bytes=2 * tm * d * 4)[0]


def _ffn_bwd_act(name, dy, wd, gu, deps=()):
    t, d = dy.shape
    f = wd.shape[0]
    nb = f // 4
    tm = _row_tile(t)

    def body(dy_ref, wd_ref, gu_ref, *rest):
        dgu_ref = rest[-1]
        da = 0.5 * lax.dot_general(dy_ref[...].astype(BF16), wd_ref[...], NT, preferred_element_type=F32)
        g = gu_ref[0].astype(F32)
        u = gu_ref[1].astype(F32)
        s = _sigmoid(g)
        dgu_ref[0] = (da * u * (s * (1.0 + g * (1.0 - s)))).astype(BF16)
        dgu_ref[1] = (da * (g * s)).astype(BF16)

    blocks = tm * d * 4 + nb * d * 2 + 4 * tm * nb * 2
    return pl.pallas_call(
        body, name=name, grid=(4, t // tm),
        in_specs=[pl.BlockSpec((tm, d), lambda j, i: (i, 0)),
                  pl.BlockSpec((nb, d), lambda j, i: (j, 0)),
                  pl.BlockSpec((2, tm, nb), lambda j, i: (0, i, j))] + [_ANY] * len(deps),
        out_specs=pl.BlockSpec((2, tm, nb), lambda j, i: (0, i, j)),
        out_shape=_sds((2, t, f), BF16),
        compiler_params=_params(("parallel", "parallel"), blocks, 6 * tm * nb * 4),
    )(dy, wd, gu, *deps)


def _ffn_bwd_dwd(name, gu, dy):
    _, t, f = gu.shape
    d = dy.shape[1]
    tm = f // 4
    tk = min(t, 512)

    def products(ins):
        g = ins[0][0].astype(F32)
        a = (g * _sigmoid(g) * ins[0][1].astype(F32)).astype(BF16)
        return lax.dot_general(a, ins[1][...].astype(BF16), TN, preferred_element_type=F32)

    def epilogue(acc, ins, outs):
        outs[0][...] = (0.5 * acc).astype(BF16)

    return _fused(name, (4, 1, t // tk),
                  [(gu, pl.BlockSpec((2, tk, tm), lambda i, j, k: (0, k, i))),
                   (dy, pl.BlockSpec((tk, d), lambda i, j, k: (k, 0)))],
                  [(_sds((f, d), BF16), pl.BlockSpec((tm, d), lambda i, j, k: (i, 0)))],
                  products, epilogue, nk=t // tk, acc_shape=(tm, d), temp_bytes=tm * d * 4 + 3 * tk * tm * 4)[0]


def _ffn_bwd_dh(name, dgu, wgu, deps=()):
    _, t, f = dgu.shape
    d, nb = wgu.shape[1], wgu.shape[2]
    tm = min(t, 512)

    def epilogue(acc, ins, outs):
        outs[0][...] = acc

    return _fused(name, (t // tm, 1, N_DEV),
                  [(dgu, pl.BlockSpec((None, tm, nb), lambda i, j, k: (k // 4, i, k % 4))),
                   (wgu, pl.BlockSpec((None, d, nb), lambda i, j, k: (k, 0, 0)))],
                  [(_sds((t, d), F32), pl.BlockSpec((tm, d), lambda i, j, k: (i, 0)))],
                  [(0, 1, NT)], epilogue, nk=N_DEV, acc_shape=(tm, d), temp_bytes=tm * d * 4, deps=deps)[0]


def _ffn_bwd_dwgu(name, h, dgu, deps=()):
    t, d = h.shape
    nb = dgu.shape[2] // 4
    tk = min(t, 512)

    def epilogue(acc, ins, outs):
        outs[0][...] = acc.astype(BF16)

    return _fused(name, (N_DEV, 1, t // tk),
                  [(h, pl.BlockSpec((tk, d), lambda i, j, k: (k, 0))),
                   (dgu, pl.BlockSpec((None, tk, nb), lambda i, j, k: (i // 4, k, i % 4)))],
                  [(_sds((N_DEV, d, nb), BF16), pl.BlockSpec((None, d, nb), lambda i, j, k: (i, 0, 0)))],
                  [(0, 1, TN)], epilogue, nk=t // tk, acc_shape=(d, nb), temp_bytes=d * nb * 4, deps=deps)[0]


def _proj(h, w_in):
    t, d = h.shape
    nb = w_in.shape[3]
    tm = min(t, 512)

    def body(h_ref, w_ref, o_ref):
        hv = h_ref[...]
        o_ref[:, 0:nb] = jnp.dot(hv, w_ref[0], preferred_element_type=F32)
        o_ref[:, nb:2 * nb] = jnp.dot(hv, w_ref[1], preferred_element_type=F32)

    blocks = tm * d * 2 + 2 * d * nb * 2 + tm * 2 * nb * 4
    return pl.pallas_call(
        body, name="mix_proj", grid=(4, t // tm),
        in_specs=[pl.BlockSpec((tm, d), lambda j, i: (i, 0)),
                  pl.BlockSpec((None, 2, d, nb), lambda j, i: (j, 0, 0, 0))],
        out_specs=pl.BlockSpec((tm, 2 * nb), lambda j, i: (i, j)),
        out_shape=_sds((t, N_DEV * nb), F32),
        compiler_params=_params(("parallel", "parallel"), blocks, 2 * tm * nb * 4),
    )(h, w_in)


def _shift_rows(u, k):
    t = u.shape[0]
    rolled = pltpu.roll(u, k % t, axis=0)
    row = lax.broadcasted_iota(jnp.int32, u.shape, 0)
    keep = (row >= k) if k > 0 else (row < t + k)
    return jnp.where(keep, rolled, 0.0)


def _conv_fwd(proj, conv_w):
    t = proj.shape[0]
    cw = conv_w.shape[1]
    tc = min(cw, 256)
    nc = cw // tc

    def epilogue(_, ins, outs):
        u = ins[2][...] * ins[0][...]
        w = ins[3][...]
        y = u * w[2:3, :] + _shift_rows(u, 1) * w[1:2, :] + _shift_rows(u, 2) * w[0:1, :]
        outs[0][...] = (ins[1][...] * y).astype(BF16)

    def col(seg):
        return pl.BlockSpec((t, tc), lambda i, j, k: (0, seg * nc + i))

    return _fused("conv_fwd", (nc, 1, 1),
                  [(proj, col(0)), (proj, col(1)), (proj, col(2)),
                   (conv_w, pl.BlockSpec((8, tc), lambda i, j, k: (0, i)))],
                  [(_sds((t, cw), BF16), pl.BlockSpec((t, tc), lambda i, j, k: (0, i)))],
                  [], epilogue, temp_bytes=6 * t * tc * 4)[0]


def _conv_bwd(proj, conv_w, dca, deps=()):
    t = proj.shape[0]
    cw = conv_w.shape[1]
    tc = min(cw, 256)
    nc = cw // tc

    def epilogue(_, ins, outs):
        xc, bg, cg, w, dc = ins[0][...], ins[1][...], ins[2][...], ins[3][...], ins[4][...]
        u = cg * xc
        u1, u2 = _shift_rows(u, 1), _shift_rows(u, 2)
        y = u * w[2:3, :] + u1 * w[1:2, :] + u2 * w[0:1, :]
        dconv = dc * bg
        du = dconv * w[2:3, :] + _shift_rows(dconv, -1) * w[1:2, :] + _shift_rows(dconv, -2) * w[0:1, :]
        outs[0][0] = (du * cg).astype(BF16)
        outs[0][1] = (dc * y).astype(BF16)
        outs[0][2] = (du * xc).astype(BF16)
        outs[1][...] = jnp.zeros_like(outs[1])
        outs[1][0:1, :] = jnp.sum(dconv * u2, axis=0, keepdims=True)
        outs[1][1:2, :] = jnp.sum(dconv * u1, axis=0, keepdims=True)
        outs[1][2:3, :] = jnp.sum(dconv * u, axis=0, keepdims=True)

    def col(seg):
        return pl.BlockSpec((t, tc), lambda i, j, k: (0, seg * nc + i))

    own = pl.BlockSpec((t, tc), lambda i, j, k: (0, i))
    wspec = pl.BlockSpec((8, tc), lambda i, j, k: (0, i))
    return _fused("conv_bwd", (nc, 1, 1),
                  [(proj, col(0)), (proj, col(1)), (proj, col(2)), (conv_w, wspec), (dca, own)],
                  [(_sds((3, t, cw), BF16), pl.BlockSpec((3, t, tc), lambda i, j, k: (0, 0, i))),
                   (_sds((8, cw), F32), wspec)],
                  [], epilogue, temp_bytes=10 * t * tc * 4, deps=deps)


def _split3(x):
    hi = x.astype(BF16)
    r1 = x - hi.astype(F32)
    mid = r1.astype(BF16)
    lo = (r1 - mid.astype(F32)).astype(BF16)
    return hi, mid, lo


def _head_selector(width):
    r = lax.broadcasted_iota(jnp.int32, (width, LANES), 0)
    c = lax.broadcasted_iota(jnp.int32, (width, LANES), 1)
    return (lax.shift_right_logical(r, 6) == c).astype(BF16)


def _head_sum(x, sel):
    return sum(jnp.dot(p, sel, preferred_element_type=F32) for p in _split3(x))


def _head_bcast(r, sel):
    return sum(lax.dot_general(p, sel, NT, preferred_element_type=F32) for p in _split3(r))


def _rope(x, c, sa, sb):
    n = x.shape[1]
    return x * c + pltpu.roll(x, n - ROT_DIM // 2, axis=1) * sa + pltpu.roll(x, ROT_DIM // 2, axis=1) * sb


def _rope_t(d, c, sa, sb):
    n = d.shape[1]
    return d * c + pltpu.roll(d * sa, ROT_DIM // 2, axis=1) + pltpu.roll(d * sb, n - ROT_DIM // 2, axis=1)


def _tile_lanes(tab, width):
    return tab if width == tab.shape[1] else jnp.tile(tab, (1, width // tab.shape[1]))


def _qk_prep(proj, gq, gk, rope_tabs, cw, kw):
    t = proj.shape[0]
    tm = _row_tile(t)

    def epilogue(_, ins, outs):
        c, sa, sb = ins[5][...], ins[6][...], ins[7][...]
        for src, gain, dst, width in ((0, 3, 0, cw), (1, 4, 1, kw)):
            xv = ins[src][...]
            sel = _head_selector(width)
            r = lax.rsqrt(_head_sum(xv * xv, sel) * (1.0 / HEAD_DIM) + RMS_EPS)
            xn = xv * _head_bcast(r, sel) * ins[gain][...]
            outs[dst][...] = _rope(xn, _tile_lanes(c, width), _tile_lanes(sa, width), _tile_lanes(sb, width)).astype(BF16)
        outs[2][...] = ins[2][...].astype(BF16)

    kblk = cw // kw
    tab = pl.BlockSpec((tm, LANES), lambda i, j, k: (i, 0))
    kspec = pl.BlockSpec((tm, kw), lambda i, j, k: (i, 0))
    return _fused("qk_prep", (t // tm, 1, 1),
                  [(proj, pl.BlockSpec((tm, cw), lambda i, j, k: (i, 3))),
                   (proj, pl.BlockSpec((tm, kw), lambda i, j, k: (i, 4 * kblk))),
                   (proj, pl.BlockSpec((tm, kw), lambda i, j, k: (i, 4 * kblk + 1))),
                   (gq, pl.BlockSpec((1, cw), lambda i, j, k: (0, 0))),
                   (gk, pl.BlockSpec((1, kw), lambda i, j, k: (0, 0))),
                   (rope_tabs[0], tab), (rope_tabs[1], tab), (rope_tabs[2], tab)],
                  [(_sds((t, cw), BF16), pl.BlockSpec((tm, cw), lambda i, j, k: (i, 0))),
                   (_sds((t, kw), BF16), kspec), (_sds((t, kw), BF16), kspec)],
                  [], epilogue, temp_bytes=12 * tm * cw * 4)


def _qk_prep_bwd(proj, gq, gk, rope_tabs, dq, dkc, dkp, dvc, dvp, cw, kw):
    t = proj.shape[0]
    tm = BLOCK
    nblk = t // tm

    def epilogue(_, ins, outs):
        c, sa, sb = ins[5][...], ins[6][...], ins[7][...]
        has_next = (pl.program_id(0) < nblk - 1).astype(F32)
        dk = ins[9][...] + has_next * ins[10][...]
        dv = ins[11][...] + has_next * ins[12][...]
        pieces = []
        for src, gain, dval, dst, width in ((0, 3, ins[8][...], 1, cw), (1, 4, dk, 2, kw)):
            xv, gv = ins[src][...], ins[gain][...]
            sel = _head_selector(width)
            r = _head_bcast(lax.rsqrt(_head_sum(xv * xv, sel) * (1.0 / HEAD_DIM) + RMS_EPS), sel)
            xh = xv * r
            dxn = _rope_t(dval, _tile_lanes(c, width), _tile_lanes(sa, width), _tile_lanes(sb, width))
            u = dxn * gv
            dot = _head_bcast(_head_sum(u * xh, sel), sel) * (1.0 / HEAD_DIM)
            pieces.append((r * (u - xh * dot)).astype(BF16))
            ri = lax.broadcasted_iota(jnp.int32, (width, LANES), 0)
            ci = lax.broadcasted_iota(jnp.int32, (width, LANES), 1)
            fold = (lax.bitwise_and(ri, HEAD_DIM - 1) == ci).astype(BF16)
            colsum = jnp.broadcast_to(jnp.sum(dxn * xh, axis=0, keepdims=True), (8, width))
            part = sum(jnp.dot(p, fold, preferred_element_type=F32) for p in _split3(colsum))

            @pl.when(pl.program_id(0) == 0)
            def _():
                outs[dst][...] = jnp.zeros_like(outs[dst])

            outs[dst][0:1, :] += part[0:1, :]
        outs[0][:, 0:cw] = pieces[0]
        outs[0][:, cw:cw + kw] = pieces[1]
        outs[0][:, cw + kw:cw + 2 * kw] = dv.astype(BF16)

    kblk = cw // kw
    tab = pl.BlockSpec((tm, LANES), lambda i, j, k: (i, 0))
    kcur = pl.BlockSpec((tm, kw), lambda i, j, k: (i, 0))
    knext = pl.BlockSpec((tm, kw), lambda i, j, k: (jnp.minimum(i + 1, nblk - 1), 0))
    acc = pl.BlockSpec((8, LANES), lambda i, j, k: (0, 0))
    return _fused("qk_prep_bwd", (nblk, 1, 1),
                  [(proj, pl.BlockSpec((tm, cw), lambda i, j, k: (i, 3))),
                   (proj, pl.BlockSpec((tm, kw), lambda i, j, k: (i, 4 * kblk))),
                   (proj, pl.BlockSpec((tm, kw), lambda i, j, k: (i, 4 * kblk + 1))),
                   (gq, pl.BlockSpec((1, cw), lambda i, j, k: (0, 0))),
                   (gk, pl.BlockSpec((1, kw), lambda i, j, k: (0, 0))),
                   (rope_tabs[0], tab), (rope_tabs[1], tab), (rope_tabs[2], tab),
                   (dq, pl.BlockSpec((tm, cw), lambda i, j, k: (i, 0))),
                   (dkc, kcur), (dkp, knext), (dvc, kcur), (dvp, knext)],
                  [(_sds((t, cw + 2 * kw), BF16), pl.BlockSpec((tm, cw + 2 * kw), lambda i, j, k: (i, 0))),
                   (_sds((8, LANES), F32), acc), (_sds((8, LANES), F32), acc)],
                  [], epilogue, temp_bytes=16 * tm * cw * 4, semantics=("arbitrary", "arbitrary", "arbitrary"))


def _attn_mask(n):
    row = lax.broadcasted_iota(jnp.int32, (BLOCK, 2 * BLOCK), 0)
    col = lax.broadcasted_iota(jnp.int32, (BLOCK, 2 * BLOCK), 1)
    return (col > row) & (col <= row + BLOCK) & ((col >= BLOCK) | (n > 0))


def _softmax_with_sink(q, k2, sink, valid):
    s = lax.dot_general(q, k2, NT, preferred_element_type=F32) * ATTN_SCALE
    s = jnp.where(valid, s, NEG_INF)
    m = jnp.maximum(jnp.max(s, axis=-1, keepdims=True), sink)
    p = jnp.exp(s - m)
    es = jnp.exp(sink - m)
    denom = jnp.sum(p, axis=-1, keepdims=True) + es
    return p / denom, es / denom


def _attn_fwd(qn, kn, vb, sink_rows):
    t, cw = qn.shape
    kw = kn.shape[1]
    nkv = kw // HEAD_DIM

    def body(q_ref, kp_ref, kc_ref, vp_ref, vc_ref, sink_ref, o_ref):
        valid = _attn_mask(pl.program_id(0))
        qv = q_ref[...]
        kp, kc, vp, vc = kp_ref[...], kc_ref[...], vp_ref[...], vc_ref[...]
        outs = []
        for h in range(nkv):
            hs = slice(h * HEAD_DIM, (h + 1) * HEAD_DIM)
            k2 = jnp.concatenate([kp[:, hs], kc[:, hs]], axis=0)
            v2 = jnp.concatenate([vp[:, hs], vc[:, hs]], axis=0)
            for g in range(GROUP):
                hq = h * GROUP + g
                pn, _ = _softmax_with_sink(qv[:, hq * HEAD_DIM:(hq + 1) * HEAD_DIM], k2, sink_ref[hq:hq + 1, 0:1], valid)
                outs.append(jnp.dot(pn.astype(BF16), v2, preferred_element_type=F32))
        o_ref[...] = jnp.concatenate(outs, axis=-1).astype(BF16)

    cur = lambda n: (n, 0)
    prev = lambda n: (jnp.maximum(n - 1, 0), 0)
    return pl.pallas_call(
        body, name="attn_fwd", grid=(t // BLOCK,),
        in_specs=[pl.BlockSpec((BLOCK, cw), cur),
                  pl.BlockSpec((BLOCK, kw), prev), pl.BlockSpec((BLOCK, kw), cur),
                  pl.BlockSpec((BLOCK, kw), prev), pl.BlockSpec((BLOCK, kw), cur),
                  pl.BlockSpec(sink_rows.shape, lambda n: (0, 0))],
        out_specs=pl.BlockSpec((BLOCK, cw), cur),
        out_shape=_sds((t, cw), BF16),
        compiler_params=_params(("parallel",), BLOCK * (cw + 4 * kw) * 2 + BLOCK * cw * 2, 8 << 20),
    )(qn, kn, kn, vb, vb, sink_rows)


def _attn_bwd(qn, kn, vb, sink_rows, do):
    t, cw = qn.shape
    kw = kn.shape[1]
    nkv = kw // HEAD_DIM
    nq = nkv * GROUP

    def body(q_ref, kp_ref, kc_ref, vp_ref, vc_ref, sink_ref, do_ref,
             dq_ref, dkc_ref, dkp_ref, dvc_ref, dvp_ref, dsink_ref):
        n = pl.program_id(0)
        valid = _attn_mask(n)
        qv, dov = q_ref[...], do_ref[...]
        kp, kc, vp, vc = kp_ref[...], kc_ref[...], vp_ref[...], vc_ref[...]
        dqs, dks, dvs, dsinks = [], [], [], []
        for h in range(nkv):
            hs = slice(h * HEAD_DIM, (h + 1) * HEAD_DIM)
            k2 = jnp.concatenate([kp[:, hs], kc[:, hs]], axis=0)
            v2 = jnp.concatenate([vp[:, hs], vc[:, hs]], axis=0)
            dk2 = jnp.zeros((2 * BLOCK, HEAD_DIM), F32)
            dv2 = jnp.zeros((2 * BLOCK, HEAD_DIM), F32)
            for g in range(GROUP):
                hq = h * GROUP + g
                qs = slice(hq * HEAD_DIM, (hq + 1) * HEAD_DIM)
                q = qv[:, qs]
                pn, psink = _softmax_with_sink(q, k2, sink_ref[hq:hq + 1, 0:1], valid)
                pb = pn.astype(BF16)
                dob = dov[:, qs].astype(BF16)
                dpn = lax.dot_general(dob, v2, NT, preferred_element_type=F32)
                dv2 = dv2 + lax.dot_general(pb, dob, TN, preferred_element_type=F32)
                delta = jnp.sum(pn * dpn, axis=-1, keepdims=True)
                ds = (pn * (dpn - delta) * ATTN_SCALE).astype(BF16)
                dqs.append(jnp.dot(ds, k2, preferred_element_type=F32))
                dk2 = dk2 + lax.dot_general(ds, q, TN, preferred_element_type=F32)
                dsinks.append(jnp.broadcast_to(jnp.sum(-psink * delta, axis=0, keepdims=True), (1, LANES)))
            dks.append(dk2)
            dvs.append(dv2)
        dq_ref[...] = jnp.concatenate(dqs, axis=-1)
        dkp_ref[...] = jnp.concatenate([d[:BLOCK] for d in dks], axis=-1)
        dkc_ref[...] = jnp.concatenate([d[BLOCK:] for d in dks], axis=-1)
        dvp_ref[...] = jnp.concatenate([d[:BLOCK] for d in dvs], axis=-1)
        dvc_ref[...] = jnp.concatenate([d[BLOCK:] for d in dvs], axis=-1)

        @pl.when(n == 0)
        def _():
            dsink_ref[...] = jnp.zeros_like(dsink_ref)

        dsink_ref[...] += jnp.concatenate(dsinks, axis=0)

    cur = lambda n: (n, 0)
    prev = lambda n: (jnp.maximum(n - 1, 0), 0)
    kspec = pl.BlockSpec((BLOCK, kw), cur)
    return pl.pallas_call(
        body, name="attn_bwd", grid=(t // BLOCK,),
        in_specs=[pl.BlockSpec((BLOCK, cw), cur),
                  pl.BlockSpec((BLOCK, kw), prev), kspec,
                  pl.BlockSpec((BLOCK, kw), prev), kspec,
                  pl.BlockSpec(sink_rows.shape, lambda n: (0, 0)),
                  pl.BlockSpec((BLOCK, cw), cur)],
        out_specs=[pl.BlockSpec((BLOCK, cw), cur), kspec, kspec, kspec, kspec,
                   pl.BlockSpec((nq, LANES), lambda n: (0, 0))],
        out_shape=[_sds((t, cw), F32)] + [_sds((t, kw), F32)] * 4 + [_sds((nq, LANES), F32)],
        compiler_params=_params(("arbitrary",), BLOCK * (cw + 4 * kw) * 2 + 2 * BLOCK * cw * 4 + 4 * BLOCK * kw * 4, 12 << 20),
    )(qn, kn, kn, vb, vb, sink_rows, do)


def _mix_out(ca, o, woc, woa, proj):
    t, cw = ca.shape
    nb = woc.shape[2]
    d = N_DEV * nb
    tm = min(t, 1024)
    ga0 = (3 * cw + cw + 2 * (cw // 4)) // nb

    def body(ca_ref, o_ref, woc_ref, woa_ref, ga_ref, gb_ref, m_ref, ya_ref, yb_ref):
        ya = jnp.dot(ca_ref[...], woc_ref[...], preferred_element_type=F32)
        yb = jnp.dot(o_ref[...], woa_ref[...], preferred_element_type=F32)
        ya_ref[...] = ya.astype(BF16)
        yb_ref[...] = yb.astype(BF16)
        m_ref[...] = (_sigmoid(ga_ref[...]) * ya + _sigmoid(gb_ref[...]) * yb).astype(BF16)

    act = pl.BlockSpec((tm, cw), lambda i, j: (i, 0))
    wsp = pl.BlockSpec((None, cw, nb), lambda i, j: (j, 0, 0))
    osp = pl.BlockSpec((tm, nb), lambda i, j: (i, j))
    blocks = 2 * tm * cw * 2 + 2 * cw * nb * 2 + 2 * tm * nb * 4 + 3 * tm * nb * 2
    return pl.pallas_call(
        body, name="mix_out", grid=(t // tm, N_DEV),
        in_specs=[act, act, wsp, wsp,
                  pl.BlockSpec((tm, nb), lambda i, j: (i, ga0 + j)),
                  pl.BlockSpec((tm, nb), lambda i, j: (i, ga0 + N_DEV + j))],
        out_specs=[osp, osp, osp],
        out_shape=[_sds((t, d), BF16)] * 3,
        compiler_params=_params(("parallel", "parallel"), blocks, 6 * tm * nb * 4),
    )(ca, o, woc, woa, proj, proj)


def _mix_residual(merged, wo, x):
    t, d = x.shape
    tm = min(t, 512)

    def epilogue(acc, ins, outs):
        outs[0][...] = ins[2][...] + acc

    row = pl.BlockSpec((tm, d), lambda i, j, k: (i, 0))
    return _fused("mix_residual", (t // tm, 1, 1),
                  [(merged, row), (wo, pl.BlockSpec((d, d), lambda i, j, k: (0, 0))), (x, row)],
                  [(_sds((t, d), F32), row)], [(0, 1, NN)], epilogue, temp_bytes=2 * tm * d * 4)[0]


def _mix_bwd_gates(dx, wo, ya, yb, proj, cw):
    t, d = dx.shape
    tm = min(t, 512)
    tn = min(d, 512)
    ga0 = (4 * cw + 2 * (cw // 4)) // tn

    def epilogue(acc, ins, outs):
        sa, sb = _sigmoid(ins[4][...]), _sigmoid(ins[5][...])
        outs[0][...] = (acc * sa).astype(BF16)
        outs[1][...] = (acc * sb).astype(BF16)
        outs[2][0] = (acc * ins[2][...].astype(F32) * sa * (1.0 - sa)).astype(BF16)
        outs[2][1] = (acc * ins[3][...].astype(F32) * sb * (1.0 - sb)).astype(BF16)

    blk = pl.BlockSpec((tm, tn), lambda i, j, k: (i, j))
    return _fused("mix_bwd_gates", (t // tm, d // tn, 1),
                  [(dx, pl.BlockSpec((tm, d), lambda i, j, k: (i, 0))),
                   (wo, pl.BlockSpec((tn, d), lambda i, j, k: (j, 0))),
                   (ya, blk), (yb, blk),
                   (proj, pl.BlockSpec((tm, tn), lambda i, j, k: (i, ga0 + j))),
                   (proj, pl.BlockSpec((tm, tn), lambda i, j, k: (i, ga0 + d // tn + j)))],
                  [(_sds((t, d), BF16), blk), (_sds((t, d), BF16), blk),
                   (_sds((2, t, d), BF16), pl.BlockSpec((2, tm, tn), lambda i, j, k: (0, i, j)))],
                  [(0, 1, NT)], epilogue, temp_bytes=8 * tm * tn * 4)


def _tn_matmul(name, a, b, tm, out_dtype=BF16):
    t, m = a.shape
    n = b.shape[1]
    tk = min(t, 512)

    def epilogue(acc, ins, outs):
        outs[0][...] = acc.astype(out_dtype)

    return _fused(name, (m // tm, 1, t // tk),
                  [(a, pl.BlockSpec((tk, tm), lambda i, j, k: (k, i))),
                   (b, pl.BlockSpec((tk, n), lambda i, j, k: (k, 0)))],
                  [(_sds((m, n), out_dtype), pl.BlockSpec((tm, n), lambda i, j, k: (i, 0)))],
                  [(0, 1, TN)], epilogue, nk=t // tk, acc_shape=(tm, n), temp_bytes=tm * n * 4)[0]


def _out_proj_bwd_act(name, dy, w, deps=()):
    t, d = dy.shape
    kdim, nb = w.shape[1], w.shape[2]
    tm = min(t, 1024)

    def epilogue(acc, ins, outs):
        outs[0][...] = acc

    return _fused(name, (t // tm, 1, N_DEV),
                  [(dy, pl.BlockSpec((tm, nb), lambda i, j, k: (i, k))),
                   (w, pl.BlockSpec((None, kdim, nb), lambda i, j, k: (k, 0, 0)))],
                  [(_sds((t, kdim), F32), pl.BlockSpec((tm, kdim), lambda i, j, k: (i, 0)))],
                  [(0, 1, NT)], epilogue, nk=N_DEV, acc_shape=(tm, kdim), temp_bytes=tm * kdim * 4, deps=deps)[0]


def _out_proj_bwd_w(name, act, dy, nb):
    t, kdim = act.shape
    tk = min(t, 1024)

    def epilogue(acc, ins, outs):
        outs[0][...] = acc.astype(BF16)

    return _fused(name, (N_DEV, 1, t // tk),
                  [(act, pl.BlockSpec((tk, kdim), lambda i, j, k: (k, 0))),
                   (dy, pl.BlockSpec((tk, nb), lambda i, j, k: (k, i)))],
                  [(_sds((N_DEV, kdim, nb), BF16), pl.BlockSpec((None, kdim, nb), lambda i, j, k: (i, 0, 0)))],
                  [(0, 1, TN)], epilogue, nk=t // tk, acc_shape=(kdim, nb), temp_bytes=kdim * nb * 4)[0]


def _proj_bwd_act(dproj, w_in, deps=()):
    t, n = dproj.shape
    d, nb = w_in.shape[2], w_in.shape[3]
    tm = min(t, 512)

    def epilogue(acc, ins, outs):
        outs[0][...] = acc

    def products(ins):
        return (lax.dot_general(ins[0][:, 0:nb], ins[1][0], NT, preferred_element_type=F32)
                + lax.dot_general(ins[0][:, nb:2 * nb], ins[1][1], NT, preferred_element_type=F32))

    return _fused("mix_bwd_dh", (t // tm, 1, 4),
                  [(dproj, pl.BlockSpec((tm, 2 * nb), lambda i, j, k: (i, k))),
                   (w_in, pl.BlockSpec((None, 2, d, nb), lambda i, j, k: (k, 0, 0, 0)))],
                  [(_sds((t, d), F32), pl.BlockSpec((tm, d), lambda i, j, k: (i, 0)))],
                  products, epilogue, nk=4, acc_shape=(tm, d), temp_bytes=tm * d * 4, deps=deps)[0]


def _proj_bwd_w(h, dproj):
    t, d = h.shape
    nb = dproj.shape[1] // N_DEV
    tk = min(t, 512)
    nk = t // tk

    def body(h_ref, dp_ref, o_ref, acc0, acc1):
        k = pl.program_id(1)

        @pl.when(k == 0)
        def _():
            acc0[...] = jnp.zeros_like(acc0)
            acc1[...] = jnp.zeros_like(acc1)

        hv = h_ref[...]
        acc0[...] += lax.dot_general(hv, dp_ref[:, 0:nb], TN, preferred_element_type=F32)
        acc1[...] += lax.dot_general(hv, dp_ref[:, nb:2 * nb], TN, preferred_element_type=F32)

        @pl.when(k == nk - 1)
        def _():
            o_ref[0] = acc0[...].astype(BF16)
            o_ref[1] = acc1[...].astype(BF16)

    blocks = tk * d * 2 + tk * 2 * nb * 2 + 2 * d * nb * 2
    return pl.pallas_call(
        body, name="mix_bwd_dwin", grid=(4, nk),
        in_specs=[pl.BlockSpec((tk, d), lambda j, k: (k, 0)),
                  pl.BlockSpec((tk, 2 * nb), lambda j, k: (k, j))],
        out_specs=pl.BlockSpec((None, 2, d, nb), lambda j, k: (j, 0, 0, 0)),
        out_shape=_sds((4, 2, d, nb), BF16),
        scratch_shapes=[pltpu.VMEM((d, nb), F32), pltpu.VMEM((d, nb), F32)],
        compiler_params=_params(("parallel", "arbitrary"), blocks, 2 * d * nb * 4),
    )(h, dproj)


def _adamw_math(w, g, m, v):
    m = ADAM_B1 * m + (1.0 - ADAM_B1) * g
    v = ADAM_B2 * v + (1.0 - ADAM_B2) * (g * g)
    m_hat = m / (1.0 - ADAM_B1 ** ADAM_STEP)
    v_hat = v / (1.0 - ADAM_B2 ** ADAM_STEP)
    delta = -ADAM_LR * (m_hat / (jnp.sqrt(v_hat) + ADAM_EPS) + ADAM_WD * w)
    return delta, m, v


def _adamw(name, parts, w, m, v, tr):
    r, c = w.shape

    def body(p_ref, w_ref, m_ref, v_ref, g_out, d_out, m_out, v_out):
        g = p_ref[0].astype(F32)
        for s in range(1, N_DEV):
            g = g + p_ref[s].astype(F32)
        delta, mn, vn = _adamw_math(w_ref[...], g, m_ref[...], v_ref[...])
        g_out[...] = g
        d_out[...] = delta
        m_out[...] = mn
        v_out[...] = vn

    blk = pl.BlockSpec((tr, c), lambda i: (i, 0))
    blocks = N_DEV * tr * c * parts.dtype.itemsize + 7 * tr * c * 4
    return pl.pallas_call(
        body, name=name, grid=(r // tr,),
        in_specs=[pl.BlockSpec((N_DEV, tr, c), lambda i: (0, i, 0)), blk, blk, blk],
        out_specs=[blk] * 4, out_shape=[_sds((r, c), F32)] * 4,
        compiler_params=_params(("parallel",), blocks, 6 * tr * c * 4),
    )(parts, w, m, v)


def _adamw_chips(name, chip, own, landed, w, m, v, tr):
    r, c = w.shape

    def body(chip_ref, own_ref, land_ref, w_ref, m_ref, v_ref, g_out, d_out, m_out, v_out):
        mine = own_ref[...].astype(F32)
        g = jnp.zeros((tr, c), F32)
        for k in range(4):
            g = g + jnp.where(chip_ref[0] == k, mine, land_ref[k].astype(F32))
        delta, mn, vn = _adamw_math(w_ref[...], g, m_ref[...], v_ref[...])
        g_out[...] = g
        d_out[...] = delta
        m_out[...] = mn
        v_out[...] = vn

    blk = pl.BlockSpec((tr, c), lambda i, chip_ref: (i, 0))
    grid_spec = pltpu.PrefetchScalarGridSpec(
        num_scalar_prefetch=1, grid=(r // tr,),
        in_specs=[pl.BlockSpec((None, tr, c), lambda i, chip_ref: (chip_ref[0], i, 0)),
                  pl.BlockSpec((4, tr, c), lambda i, chip_ref: (0, i, 0)), blk, blk, blk],
        out_specs=[blk] * 4)
    blocks = 5 * tr * c * 2 + 7 * tr * c * 4
    return pl.pallas_call(
        body, name=name, grid_spec=grid_spec, out_shape=[_sds((r, c), F32)] * 4,
        compiler_params=_params(("parallel",), blocks, 6 * tr * c * 4),
    )(chip, own, landed, w, m, v)


def _rope_tables(t):
    half = ROT_DIM // 2
    inv_freq = 1.0 / (ROPE_THETA ** (jnp.arange(0, ROT_DIM, 2, dtype=F32) / ROT_DIM))
    ang = jnp.arange(t, dtype=F32)[:, None] * inv_freq[None, :]
    cos, sin = jnp.cos(ang), jnp.sin(ang)
    ones = jnp.ones((t, HEAD_DIM - ROT_DIM), F32)
    zeros = jnp.zeros((t, HEAD_DIM - half), F32)
    c = jnp.concatenate([cos, cos, ones], axis=1)
    sa = jnp.concatenate([-sin, zeros], axis=1)
    sb = jnp.concatenate([jnp.zeros((t, half), F32), sin, jnp.zeros((t, HEAD_DIM - ROT_DIM), F32)], axis=1)
    return tuple(jnp.tile(a, (1, LANES // HEAD_DIM)) for a in (c, sa, sb))


def _pad_rows(a, rows=8):
    return jnp.pad(a, ((0, rows - a.shape[0]), (0, 0)))


def kernel(x, g_ffn1, w_gu1, w_down1, g_mix, w_in, conv_w, q_norm_g, k_norm_g, sinks, w_out_conv, w_out_attn, w_o, g_ffn2, w_gu2, w_down2, loss_target, m_g_ffn1, m_w_gu1, m_w_down1, m_g_mix, m_w_in, m_conv_w, m_q_norm_g, m_k_norm_g, m_sinks, m_w_out_conv, m_w_out_attn, m_w_o, m_g_ffn2, m_w_gu2, m_w_down2, v_g_ffn1, v_w_gu1, v_w_down1, v_g_mix, v_w_in, v_conv_w, v_q_norm_g, v_k_norm_g, v_sinks, v_w_out_conv, v_w_out_attn, v_w_o, v_g_ffn2, v_w_gu2, v_w_down2):
    t, d = x.shape[1], x.shape[2]
    cw = d // 2
    kw = cw // GROUP
    nq = cw // HEAD_DIM
    xs, target = x.reshape(t, d), loss_target.reshape(t, d)
    me = 4 * lax.axis_index("x") + 2 * lax.axis_index("y") + lax.axis_index("c")

    big = {"w_gu1": w_gu1, "w_down1": w_down1, "w_in": w_in, "w_out_conv": w_out_conv,
           "w_out_attn": w_out_attn, "w_o": w_o, "w_gu2": w_gu2, "w_down2": w_down2}
    big_m = {"w_gu1": m_w_gu1, "w_down1": m_w_down1, "w_in": m_w_in, "w_out_conv": m_w_out_conv,
             "w_out_attn": m_w_out_attn, "w_o": m_w_o, "w_gu2": m_w_gu2, "w_down2": m_w_down2}
    big_v = {"w_gu1": v_w_gu1, "w_down1": v_w_down1, "w_in": v_w_in, "w_out_conv": v_w_out_conv,
             "w_out_attn": v_w_out_attn, "w_o": v_w_o, "w_gu2": v_w_gu2, "w_down2": v_w_down2}
    names = list(big)

    tiles = {"w_gu1": 256, "w_gu2": 256, "w_in": 256, "w_down1": 176, "w_down2": 176,
             "w_out_conv": 1024, "w_out_attn": 1024, "w_o": 128}

    def row_tile(n):
        r = big[n].shape[1]
        return tiles[n] if r % tiles[n] == 0 else r

    me_arr = me.astype(jnp.int32).reshape(1)
    sources = [(n, big[n][0], BF16, row_tile(n)) for n in names] + [("conv_w", _pad_rows(conv_w[0]), F32, 8)]
    issue_order = [0, 1, 2, 8, 3, 4, 5, 6, 7]
    first = _place_shard("place_" + names[0], sources[0][1], BF16, me_arr, sources[0][3])
    started = [_gather_start("gather_start_first", [first])]
    rest = [_place_shard("place_" + sources[i][0], sources[i][1], sources[i][2], me_arr, sources[i][3],
                         deps=(started[0][3],)) for i in issue_order[1:]]
    started.append(_gather_start("gather_start_rest", rest))
    where = {0: (0, 0)}
    where.update({i: (1, p) for p, i in enumerate(issue_order[1:])})

    def fetch(tag, idxs, after):
        call = where[idxs[0]][0]
        send, recv, stacks, _ = started[call]
        positions = [where[i][1] for i in idxs]
        got = _gather_wait("gather_wait_" + tag, positions, send, recv, [stacks[p] for p in positions], after)
        return _forward_to_sibling("gather_forward_" + tag, got)

    rope_tabs = _rope_tables(t)
    gq = jnp.tile(q_norm_g, (1, nq))
    gk = jnp.tile(k_norm_g, (1, nq // GROUP))
    sink_rows = jnp.broadcast_to(sinks[0][:, None], (nq, LANES))

    wts = {}
    h1 = _rms_fwd("ffn1_norm", xs, g_ffn1)
    wts["w_gu1"], = fetch("gu1", [0], started[1][3])
    gu1, a1 = _ffn_up("ffn1_up", h1, wts["w_gu1"])
    wts["w_down1"], = fetch("down1", [1], a1)
    wd1 = wts["w_down1"].reshape(-1, d)
    x1 = _ffn_down("ffn1_down", a1, wd1, xs)
    h2 = _rms_fwd("mix_norm", x1, g_mix)
    wts["w_in"], conv_land = fetch("in", [2, 8], h2)
    w_in_full = wts["w_in"].reshape(4, 2, d, -1)
    conv_full = jnp.transpose(conv_land, (1, 0, 2)).reshape(8, cw)
    proj = _proj(h2, w_in_full)
    ca = _conv_fwd(proj, conv_full)
    qn, kn, vb = _qk_prep(proj, gq, gk, rope_tabs, cw, kw)
    o = _attn_fwd(qn, kn, vb, sink_rows)
    wts["w_out_conv"], wts["w_out_attn"] = fetch("out", [3, 4], o)
    merged, ya, yb = _mix_out(ca, o, wts["w_out_conv"], wts["w_out_attn"], proj)
    wts["w_o"], = fetch("o", [5], merged)
    wo = wts["w_o"].reshape(d, d)
    x2 = _mix_residual(merged, wo, x1)
    h3 = _rms_fwd("ffn2_norm", x2, g_ffn2)
    wts["w_gu2"], = fetch("gu2", [6], h3)
    gu2, a2 = _ffn_up("ffn2_up", h3, wts["w_gu2"])
    wts["w_down2"], = fetch("down2", [7], a2)
    wd2 = wts["w_down2"].reshape(-1, d)
    y = _ffn_down("ffn2_down", a2, wd2, x2)
    dy, sq = _loss_dy(y, target)
    loss = lax.psum(sq[0, 0] * (0.5 / d), ("x", "y", "c"))

    core = lax.axis_index("c").astype(jnp.int32).reshape(1)
    chip = (2 * lax.axis_index("x") + lax.axis_index("y")).astype(jnp.int32).reshape(1)
    def pair_start(tag, group, grads):
        stacks = [grads[n].reshape((4, 2) + big[n].shape[1:]) for n in group]
        lands = [lax.empty((4,) + big[n].shape[1:], BF16) for n in group]
        return _pair_start("rs_pair_start_" + tag, stacks, lands)

    def chip_start(tag, group, pending, after):
        send, recv, stacks, lands, _ = pending
        stacks, lands = _pair_wait("rs_pair_wait_" + tag, send, recv, stacks, lands, after)
        parts = [_pair_add("rs_pair_add_" + n, st, ld, core, row_tile(n)) for n, st, ld in zip(group, stacks, lands)]
        lands2 = [lax.empty((4,) + big[n].shape[1:], BF16) for n in group]
        return _chip_start("rs_chip_start_" + tag, parts, lands2)

    group_a, group_b, group_c = ["w_down2", "w_gu2"], ["w_o", "w_out_conv", "w_out_attn"], ["w_in"]
    group_d, group_e = ["w_down1"], ["w_gu1"]
    g = {}
    dgu2 = _ffn_bwd_act("ffn2_bwd_act", dy, wd2, gu2)
    g["w_down2"] = _ffn_bwd_dwd("ffn2_bwd_dwd", gu2, dy)
    g["w_gu2"] = _ffn_bwd_dwgu("ffn2_bwd_dwgu", h3, dgu2)
    pend_a = pair_start("a", group_a, g)
    dh3 = _ffn_bwd_dh("ffn2_bwd_dh", dgu2, wts["w_gu2"], deps=(pend_a[4],))
    ring_a = chip_start("a", group_a, pend_a, dh3)
    dx2, dg_ffn2 = _rms_bwd("ffn2_bwd_rms", x2, g_ffn2, dh3, dy, deps=(ring_a[4],))

    dya, dyb, dgates = _mix_bwd_gates(dx2, wo, ya, yb, proj, cw)
    g["w_o"] = _tn_matmul("mix_bwd_dwo", merged, dx2, min(d, 1024))
    g["w_out_conv"] = _out_proj_bwd_w("mix_bwd_dwoc", ca, dya, d // N_DEV)
    g["w_out_attn"] = _out_proj_bwd_w("mix_bwd_dwoa", o, dyb, d // N_DEV)
    pend_b = pair_start("b", group_b, g)
    dca = _out_proj_bwd_act("mix_bwd_dca", dya, wts["w_out_conv"], deps=(pend_b[4],))
    do = _out_proj_bwd_act("mix_bwd_do", dyb, wts["w_out_attn"])
    ring_b = chip_start("b", group_b, pend_b, do)
    d3, dconv_w = _conv_bwd(proj, conv_full, dca, deps=(ring_b[4],))
    dq, dkc, dkp, dvc, dvp, dsink = _attn_bwd(qn, kn, vb, sink_rows, do)
    dqkv, dgq, dgk = _qk_prep_bwd(proj, gq, gk, rope_tabs, dq, dkc, dkp, dvc, dvp, cw, kw)
    dproj = jnp.concatenate([d3[0], d3[1], d3[2], dqkv, dgates[0], dgates[1]], axis=1)
    g["w_in"] = _proj_bwd_w(h2, dproj)
    pend_c = pair_start("c", group_c, g)
    dh2 = _proj_bwd_act(dproj, w_in_full, deps=(pend_c[4],))
    ring_c = chip_start("c", group_c, pend_c, dh2)
    dx1, dg_mix = _rms_bwd("mix_bwd_rms", x1, g_mix, dh2, dx2, deps=(ring_c[4],))

    g["w_down1"] = _ffn_bwd_dwd("ffn1_bwd_dwd", gu1, dx1)
    pend_d = pair_start("d", group_d, g)
    dgu1 = _ffn_bwd_act("ffn1_bwd_act", dx1, wd1, gu1, deps=(pend_d[4],))
    ring_d = chip_start("d", group_d, pend_d, dgu1)
    g["w_gu1"] = _ffn_bwd_dwgu("ffn1_bwd_dwgu", h1, dgu1, deps=(ring_d[4],))
    pend_e = pair_start("e", group_e, g)

    big_out = {}

    def finish(tag, group, ring, after):
        send, recv, parts, lands2, _ = ring
        parts, lands2 = _chip_wait("rs_chip_wait_" + tag, send, recv, parts, lands2, after)
        for n, own, landed in zip(group, parts, lands2):
            res = _adamw_chips("adamw_" + n, chip, own, landed, big[n][0], big_m[n][0], big_v[n][0], row_tile(n))
            big_out[n] = [a[None] for a in res]
            after = res[0]
        return after

    after = finish("a", group_a, ring_a, pend_e[4])
    ring_e = chip_start("e", group_e, pend_e, after)
    dh1 = _ffn_bwd_dh("ffn1_bwd_dh", dgu1, wts["w_gu1"], deps=(ring_e[4],))
    grad_x, dg_ffn1 = _rms_bwd("ffn1_bwd_rms", xs, g_ffn1, dh1, dx1)
    after = grad_x
    for tag, group, ring in (("b", group_b, ring_b), ("c", group_c, ring_c), ("d", group_d, ring_d), ("e", group_e, ring_e)):
        after = finish(tag, group, ring, after)

    small = {"g_ffn1": dg_ffn1[0:1], "g_mix": dg_mix[0:1], "g_ffn2": dg_ffn2[0:1],
             "q_norm_g": dgq[0:1, :HEAD_DIM], "k_norm_g": dgk[0:1, :HEAD_DIM], "sinks": dsink[:, 0][None],
             "conv_w": dconv_w[0:CONV_K].reshape(1, -1)}
    small_w = {"g_ffn1": g_ffn1, "g_mix": g_mix, "g_ffn2": g_ffn2, "q_norm_g": q_norm_g, "k_norm_g": k_norm_g,
               "sinks": sinks, "conv_w": None}
    small_m = {"g_ffn1": m_g_ffn1, "g_mix": m_g_mix, "g_ffn2": m_g_ffn2, "q_norm_g": m_q_norm_g,
               "k_norm_g": m_k_norm_g, "sinks": m_sinks, "conv_w": m_conv_w}
    small_v = {"g_ffn1": v_g_ffn1, "g_mix": v_g_mix, "g_ffn2": v_g_ffn2, "q_norm_g": v_q_norm_g,
               "k_norm_g": v_k_norm_g, "sinks": v_sinks, "conv_w": v_conv_w}
    snames = list(small)
    widths = [small[n].shape[1] for n in snames]
    total = sum(widths)
    rows = -(-total // LANES)
    rows = -(-rows // 8) * 8

    def pack(vals):
        flat = jnp.concatenate([v.reshape(1, -1) for v in vals], axis=1)
        return jnp.pad(flat, ((0, 0), (0, rows * LANES - total))).reshape(rows, LANES)

    csh = cw // N_DEV

    def place_conv(local, fill):
        full = jnp.full((CONV_K, cw), fill, F32)
        return lax.dynamic_update_slice(full, local, (0, me * csh)).reshape(1, -1)

    pw = pack([small_w[n] if n != "conv_w" else place_conv(conv_w[0], 0.0) for n in snames])
    pm = pack([small_m[n] if n != "conv_w" else place_conv(m_conv_w[0], 0.0) for n in snames])
    pv = pack([small_v[n] if n != "conv_w" else place_conv(v_conv_w[0], 1.0) for n in snames])
    parts = _exchange("gather_small_grads", [pack([small[n] for n in snames])], gather=True, deps=(after,))[0]
    sg, sd, sm, sv = [a.reshape(1, -1) for a in _adamw("adamw_small", parts, pw, pm, pv, rows)]

    def unpack(flat, n):
        off = sum(widths[:snames.index(n)])
        piece = flat[:, off:off + widths[snames.index(n)]]
        if n == "conv_w":
            piece = lax.dynamic_slice(piece.reshape(CONV_K, cw), (0, me * csh), (CONV_K, csh))[None]
        return piece

    order = ["g_ffn1", "w_gu1", "w_down1", "g_mix", "w_in", "conv_w", "q_norm_g", "k_norm_g", "sinks",
             "w_out_conv", "w_out_attn", "w_o", "g_ffn2", "w_gu2", "w_down2"]
    outs = [loss, grad_x[None]]
    for idx, flat in enumerate((sg, sd, sm, sv)):
        for n in order:
            outs.append(big_out[n][idx] if n in big_out else unpack(flat, n))
    return tuple(outs)
```

```python
import functools

import jax
import jax.numpy as jnp
from jax import lax
from jax.experimental import pallas as pl
from jax.experimental.pallas import tpu as pltpu

F32 = jnp.float32
BF16 = jnp.bfloat16

N_DEV = 8
HEAD_DIM = 64
GROUP = 4
BLOCK = 128
ROT_DIM = 16
ROPE_THETA = 500000.0
RMS_EPS = 1e-6
NEG_INF = -1e30
ATTN_SCALE = HEAD_DIM ** -0.5
CONV_K = 3
LANES = 128
MXU_COLS = 256
VMEM_BYTES_V7X = 64 * 1024 * 1024
VMEM_CAP = VMEM_BYTES_V7X - 6 * 1024 * 1024

ADAM_LR = 0.001
ADAM_B1 = 0.9
ADAM_B2 = 0.999
ADAM_EPS = 1e-08
ADAM_WD = 0.01
ADAM_STEP = 10

NN = (((1,), (0,)), ((), ()))
NT = (((1,), (1,)), ((), ()))
TN = (((0,), (0,)), ((), ()))

MESH = pl.DeviceIdType.MESH


def _nbytes(shape, dtype):
    n = 1
    for s in shape:
        if s is not None:
            n *= s
    return n * jnp.dtype(dtype).itemsize


def _params(semantics, block_bytes, temp_bytes):
    assert 2 * block_bytes + temp_bytes <= VMEM_CAP, (block_bytes, temp_bytes)
    return pltpu.CompilerParams(dimension_semantics=semantics, vmem_limit_bytes=VMEM_CAP)


def _fused(name, grid, ins, outs, dots, epilogue, *, nk=1, acc_shape=None, temp_bytes=0,
           semantics=("parallel", "parallel", "arbitrary"), deps=()):
    n_in, n_out = len(ins), len(outs)
    n_dep = len(deps)

    def body(*refs):
        in_refs, out_refs = refs[:n_in], refs[n_in + n_dep:n_in + n_dep + n_out]
        scratch = refs[n_in + n_dep + n_out:]

        def products():
            if callable(dots):
                return dots(in_refs)
            total = None
            for ai, bi, contract in dots:
                a, b = in_refs[ai][...], in_refs[bi][...]
                a = a if a.dtype == BF16 else a.astype(BF16)
                b = b if b.dtype == BF16 else b.astype(BF16)
                p = lax.dot_general(a, b, contract, preferred_element_type=F32)
                total = p if total is None else total + p
            return total

        if nk == 1:
            epilogue(products() if dots else None, in_refs, out_refs)
        else:
            acc = scratch[0]
            k = pl.program_id(2)

            @pl.when(k == 0)
            def _():
                acc[...] = jnp.zeros_like(acc)

            acc[...] += products()

            @pl.when(k == nk - 1)
            def _():
                epilogue(acc[...], in_refs, out_refs)

    block_bytes = sum(_nbytes(spec.block_shape, a.dtype) for a, spec in ins)
    block_bytes += sum(_nbytes(spec.block_shape, s.dtype) for s, spec in outs)
    scratch_shapes = []
    if nk > 1:
        scratch_shapes.append(pltpu.VMEM(acc_shape, F32))
        temp_bytes += _nbytes(acc_shape, F32)
    res = pl.pallas_call(
        body, name=name, grid=grid,
        in_specs=[spec for _, spec in ins] + [pl.BlockSpec(memory_space=pl.ANY)] * n_dep,
        out_specs=[spec for _, spec in outs],
        out_shape=[s for s, _ in outs],
        scratch_shapes=scratch_shapes,
        compiler_params=_params(semantics, block_bytes, temp_bytes),
    )(*[a for a, _ in ins], *deps)
    return res


def _sds(shape, dtype):
    return jax.ShapeDtypeStruct(shape, dtype)


def _sigmoid(x):
    return jax.nn.sigmoid(x)


def _exchange(name, arrays, gather, deps=()):
    n = len(arrays)
    out_shapes = [((N_DEV,) + a.shape) if gather else a.shape for a in arrays]

    def body(*refs):
        srcs, dsts = refs[:n], refs[n + len(deps):2 * n + len(deps)]
        send_sems, recv_sems, local_sems = refs[2 * n + len(deps):]
        x, y, c = lax.axis_index("x"), lax.axis_index("y"), lax.axis_index("c")
        me = 4 * x + 2 * y + c
        copies = []
        for w in range(n):
            own = srcs[w] if gather else srcs[w].at[me]
            local = pltpu.make_async_copy(own, dsts[w].at[me], local_sems.at[w])
            local.start()
            copies.append(local)
            for k in range(1, N_DEV):
                px = (1 - x) if (k & 4) else x
                py = (1 - y) if (k & 2) else y
                pc = (1 - c) if (k & 1) else c
                peer = 4 * px + 2 * py + pc
                cp = pltpu.make_async_remote_copy(
                    src_ref=srcs[w] if gather else srcs[w].at[peer],
                    dst_ref=dsts[w].at[me],
                    send_sem=send_sems.at[w * (N_DEV - 1) + k - 1],
                    recv_sem=recv_sems.at[w * (N_DEV - 1) + k - 1],
                    device_id=(px, py, pc), device_id_type=MESH)
                cp.start()
                copies.append(cp)
        for cp in copies:
            cp.wait()

    hbm = pl.BlockSpec(memory_space=pltpu.HBM)
    return pl.pallas_call(
        body, name=name,
        in_specs=[hbm] * n + [pl.BlockSpec(memory_space=pl.ANY)] * len(deps), out_specs=[hbm] * n,
        out_shape=[_sds(s, a.dtype) for s, a in zip(out_shapes, arrays)],
        scratch_shapes=[pltpu.SemaphoreType.DMA((n * (N_DEV - 1),)),
                        pltpu.SemaphoreType.DMA((n * (N_DEV - 1),)),
                        pltpu.SemaphoreType.DMA((n,))],
    )(*arrays, *deps)


_HBM = pl.BlockSpec(memory_space=pltpu.HBM)
_SEM = pl.BlockSpec(memory_space=pltpu.SEMAPHORE)
_ANY = pl.BlockSpec(memory_space=pl.ANY)
_EFFECT = pltpu.SideEffectType.DATAFLOW_SIDE_EFFECTING
N_TARGETS = 4


def _mesh_pos():
    return lax.axis_index("x"), lax.axis_index("y"), lax.axis_index("c")


def _chip_peers(x, y, c):
    return [(1 - x, y, c), (x, 1 - y, c), (1 - x, 1 - y, c)]


def _dev_index(pos):
    return 4 * pos[0] + 2 * pos[1] + pos[2]


def _hbm_like(a):
    return pltpu.HBM(a.shape, a.dtype)


def _place_shard(name, w, out_dtype, me, tr, deps=()):
    r, c = w.shape
    n_dep = len(deps)

    def body(me_ref, w_ref, *rest):
        rest[n_dep][...] = w_ref[...].astype(out_dtype)

    grid_spec = pltpu.PrefetchScalarGridSpec(
        num_scalar_prefetch=1, grid=(r // tr,),
        in_specs=[pl.BlockSpec((tr, c), lambda i, me_ref: (i, 0))] + [_ANY] * n_dep,
        out_specs=pl.BlockSpec((None, tr, c), lambda i, me_ref: (me_ref[0], i, 0)))
    return pl.pallas_call(
        body, name=name, grid_spec=grid_spec, out_shape=_sds((N_DEV, r, c), out_dtype),
        compiler_params=_params(("parallel",), tr * c * 6, tr * c * 4),
    )(me, w, *deps)


def _gather_start(name, lands):
    n = len(lands)

    def body(*refs):
        bufs = refs[:n]
        send, recv = refs[n], refs[n + 1]
        token = refs[-1]
        x, y, c = _mesh_pos()
        me = _dev_index((x, y, c))
        targets = [(x, y, 1 - c)] + _chip_peers(x, y, c)
        for w in range(n):
            for k, to in enumerate(targets):
                pltpu.make_async_remote_copy(
                    src_ref=bufs[w].at[me], dst_ref=bufs[w].at[me],
                    send_sem=send.at[N_TARGETS * w + k], recv_sem=recv.at[N_TARGETS * w + k],
                    device_id=to, device_id_type=MESH).start()
        token[...] = jnp.zeros_like(token)

    sems = pltpu.SemaphoreType.DMA((N_TARGETS * n,))
    outs = pl.pallas_call(
        body, name=name,
        in_specs=[_HBM] * n, out_specs=[_SEM, _SEM] + [_HBM] * n + [_token_spec()],
        out_shape=[sems, sems] + [_hbm_like(a) for a in lands] + [_sds((8, LANES), F32)],
        input_output_aliases={i: 2 + i for i in range(n)},
        compiler_params=pltpu.CompilerParams(has_side_effects=_EFFECT),
    )(*lands)
    return outs[0], outs[1], list(outs[2:2 + n]), outs[-1]


def _gather_wait(name, positions, send, recv, lands, after):
    m = len(positions)

    def body(*refs):
        bufs = refs[:m]
        send_sems, recv_sems = refs[m], refs[m + 1]
        x, y, c = _mesh_pos()
        me = _dev_index((x, y, c))
        sources = [(x, y, 1 - c)] + _chip_peers(x, y, c)
        for j, w in enumerate(positions):
            for k, frm in enumerate(sources):
                cp = pltpu.make_async_remote_copy(
                    src_ref=bufs[j].at[me], dst_ref=bufs[j].at[_dev_index(frm)],
                    send_sem=send_sems.at[N_TARGETS * w + k], recv_sem=recv_sems.at[N_TARGETS * w + k],
                    device_id=frm, device_id_type=MESH)
                cp.wait_send()
                cp.wait_recv()

    outs = pl.pallas_call(
        body, name=name,
        in_specs=[_HBM] * m + [_SEM, _SEM, _ANY], out_specs=[_HBM] * m,
        out_shape=[_hbm_like(a) for a in lands],
        input_output_aliases={i: i for i in range(m)},
        compiler_params=pltpu.CompilerParams(has_side_effects=_EFFECT),
    )(*lands, send, recv, after)
    return list(outs)


def _forward_to_sibling(name, lands):
    m = len(lands)

    def body(*refs):
        bufs = refs[m:2 * m]
        send_sems, recv_sems = refs[2 * m], refs[2 * m + 1]
        x, y, c = _mesh_pos()
        copies = []
        for j in range(m):
            for k, chip in enumerate(_chip_peers(x, y, c)):
                block = bufs[j].at[_dev_index(chip)]
                cp = pltpu.make_async_remote_copy(
                    src_ref=block, dst_ref=block,
                    send_sem=send_sems.at[3 * j + k], recv_sem=recv_sems.at[3 * j + k],
                    device_id=(x, y, 1 - c), device_id_type=MESH)
                cp.start()
                copies.append(cp)
        for cp in copies:
            cp.wait()

    outs = pl.pallas_call(
        body, name=name,
        in_specs=[_HBM] * m, out_specs=[_HBM] * m,
        out_shape=[_sds(a.shape, a.dtype) for a in lands],
        input_output_aliases={i: i for i in range(m)},
        scratch_shapes=[pltpu.SemaphoreType.DMA((3 * m,)), pltpu.SemaphoreType.DMA((3 * m,))],
    )(*lands)
    return list(outs)


def _token_spec():
    return pl.BlockSpec(memory_space=pltpu.VMEM)


def _pair_start(name, stacks, lands):
    n = len(stacks)

    def body(*refs):
        srcs, dsts = refs[:n], refs[n:2 * n]
        send, recv = refs[2 * n], refs[2 * n + 1]
        token = refs[-1]
        x, y, c = _mesh_pos()
        for w in range(n):
            for chip in range(4):
                pltpu.make_async_remote_copy(
                    src_ref=srcs[w].at[chip, 1 - c], dst_ref=dsts[w].at[chip],
                    send_sem=send.at[4 * w + chip], recv_sem=recv.at[4 * w + chip],
                    device_id=(x, y, 1 - c), device_id_type=MESH).start()
        token[...] = jnp.zeros_like(token)

    sems = pltpu.SemaphoreType.DMA((4 * n,))
    outs = pl.pallas_call(
        body, name=name,
        in_specs=[_HBM] * (2 * n), out_specs=[_SEM, _SEM] + [_HBM] * (2 * n) + [_token_spec()],
        out_shape=[sems, sems] + [_hbm_like(a) for a in stacks] + [_hbm_like(a) for a in lands] + [_sds((8, LANES), F32)],
        input_output_aliases={i: 2 + i for i in range(2 * n)},
        compiler_params=pltpu.CompilerParams(has_side_effects=_EFFECT),
    )(*stacks, *lands)
    return outs[0], outs[1], list(outs[2:2 + n]), list(outs[2 + n:2 + 2 * n]), outs[-1]


def _pair_wait(name, send, recv, stacks, lands, after):
    n = len(stacks)

    def body(*refs):
        srcs, dsts = refs[:n], refs[n:2 * n]
        send_sems, recv_sems = refs[2 * n], refs[2 * n + 1]
        x, y, c = _mesh_pos()
        for w in range(n):
            for chip in range(4):
                cp = pltpu.make_async_remote_copy(
                    src_ref=srcs[w].at[chip, 1 - c], dst_ref=dsts[w].at[chip],
                    send_sem=send_sems.at[4 * w + chip], recv_sem=recv_sems.at[4 * w + chip],
                    device_id=(x, y, 1 - c), device_id_type=MESH)
                cp.wait_send()
                cp.wait_recv()

    outs = pl.pallas_call(
        body, name=name,
        in_specs=[_HBM] * (2 * n) + [_SEM, _SEM, _ANY], out_specs=[_HBM] * (2 * n),
        out_shape=[_hbm_like(a) for a in stacks] + [_hbm_like(a) for a in lands],
        input_output_aliases={i: i for i in range(2 * n)},
        compiler_params=pltpu.CompilerParams(has_side_effects=_EFFECT),
    )(*stacks, *lands, send, recv, after)
    return list(outs[:n]), list(outs[n:])


def _pair_add(name, stack, land, core, tr):
    _, _, r, c = stack.shape

    def body(core_ref, a_ref, b_ref, o_ref):
        o_ref[...] = (a_ref[...].astype(F32) + b_ref[...].astype(F32)).astype(BF16)

    grid_spec = pltpu.PrefetchScalarGridSpec(
        num_scalar_prefetch=1, grid=(4, r // tr),
        in_specs=[pl.BlockSpec((None, None, tr, c), lambda k, i, core_ref: (k, core_ref[0], i, 0)),
                  pl.BlockSpec((None, tr, c), lambda k, i, core_ref: (k, i, 0))],
        out_specs=pl.BlockSpec((None, tr, c), lambda k, i, core_ref: (k, i, 0)))
    return pl.pallas_call(
        body, name=name, grid_spec=grid_spec, out_shape=_sds((4, r, c), BF16),
        compiler_params=_params(("parallel", "parallel"), 3 * tr * c * 2, 3 * tr * c * 4),
    )(core, stack, land)


def _chip_start(name, parts, lands):
    n = len(parts)

    def body(*refs):
        srcs, dsts = refs[:n], refs[n:2 * n]
        send, recv = refs[2 * n], refs[2 * n + 1]
        token = refs[-1]
        x, y, c = _mesh_pos()
        for w in range(n):
            for k, to in enumerate(_chip_peers(x, y, c)):
                pltpu.make_async_remote_copy(
                    src_ref=srcs[w].at[2 * to[0] + to[1]], dst_ref=dsts[w].at[2 * x + y],
                    send_sem=send.at[3 * w + k], recv_sem=recv.at[3 * w + k],
                    device_id=to, device_id_type=MESH).start()
        token[...] = jnp.zeros_like(token)

    sems = pltpu.SemaphoreType.DMA((3 * n,))
    outs = pl.pallas_call(
        body, name=name,
        in_specs=[_HBM] * (2 * n), out_specs=[_SEM, _SEM] + [_HBM] * (2 * n) + [_token_spec()],
        out_shape=[sems, sems] + [_hbm_like(a) for a in parts] + [_hbm_like(a) for a in lands] + [_sds((8, LANES), F32)],
        input_output_aliases={i: 2 + i for i in range(2 * n)},
        compiler_params=pltpu.CompilerParams(has_side_effects=_EFFECT),
    )(*parts, *lands)
    return outs[0], outs[1], list(outs[2:2 + n]), list(outs[2 + n:2 + 2 * n]), outs[-1]


def _chip_wait(name, send, recv, parts, lands, after):
    n = len(parts)

    def body(*refs):
        srcs, dsts = refs[:n], refs[n:2 * n]
        send_sems, recv_sems = refs[2 * n], refs[2 * n + 1]
        x, y, c = _mesh_pos()
        for w in range(n):
            for k, frm in enumerate(_chip_peers(x, y, c)):
                chip = 2 * frm[0] + frm[1]
                cp = pltpu.make_async_remote_copy(
                    src_ref=srcs[w].at[chip], dst_ref=dsts[w].at[chip],
                    send_sem=send_sems.at[3 * w + k], recv_sem=recv_sems.at[3 * w + k],
                    device_id=frm, device_id_type=MESH)
                cp.wait_send()
                cp.wait_recv()

    outs = pl.pallas_call(
        body, name=name,
        in_specs=[_HBM] * (2 * n) + [_SEM, _SEM, _ANY], out_specs=[_HBM] * (2 * n),
        out_shape=[_hbm_like(a) for a in parts] + [_hbm_like(a) for a in lands],
        input_output_aliases={i: i for i in range(2 * n)},
        compiler_params=pltpu.CompilerParams(has_side_effects=_EFFECT),
    )(*parts, *lands, send, recv, after)
    return list(outs[:n]), list(outs[n:])


def _row_tile(t):
    return min(t, 256)


def _rms_fwd(name, x, g):
    t, d = x.shape
    tm = _row_tile(t)

    def epilogue(_, ins, outs):
        xv = ins[0][...]
        r = lax.rsqrt(jnp.mean(xv * xv, axis=-1, keepdims=True) + RMS_EPS)
        outs[0][...] = (xv * r * ins[1][...]).astype(BF16)

    row = pl.BlockSpec((tm, d), lambda i, j, k: (i, 0))
    vec = pl.BlockSpec((1, d), lambda i, j, k: (0, 0))
    return _fused(name, (t // tm, 1, 1), [(x, row), (g, vec)], [(_sds((t, d), BF16), row)], [], epilogue,
                  temp_bytes=4 * tm * d * 4)[0]


def _rms_bwd(name, x, g, dh, resid, deps=()):
    t, d = x.shape
    tm = _row_tile(t)

    def epilogue(_, ins, outs):
        xv, gv, dhv = ins[0][...], ins[1][...], ins[2][...]
        r = lax.rsqrt(jnp.mean(xv * xv, axis=-1, keepdims=True) + RMS_EPS)
        xh = xv * r
        u = dhv * gv
        dot = jnp.mean(u * xh, axis=-1, keepdims=True)
        outs[0][...] = ins[3][...] + r * (u - xh * dot)

        @pl.when(pl.program_id(0) == 0)
        def _():
            outs[1][...] = jnp.zeros_like(outs[1])

        outs[1][0:1, :] += jnp.sum(dhv * xh, axis=0, keepdims=True)

    row = pl.BlockSpec((tm, d), lambda i, j, k: (i, 0))
    vec = pl.BlockSpec((1, d), lambda i, j, k: (0, 0))
    acc = pl.BlockSpec((8, d), lambda i, j, k: (0, 0))
    return _fused(name, (t // tm, 1, 1), [(x, row), (g, vec), (dh, row), (resid, row)],
                  [(_sds((t, d), F32), row), (_sds((8, d), F32), acc)], [], epilogue,
                  temp_bytes=6 * tm * d * 4, semantics=("arbitrary", "arbitrary", "arbitrary"), deps=deps)


def _loss_dy(y, target):
    t, d = y.shape
    tm = _row_tile(t)

    def epilogue(_, ins, outs):
        e = ins[0][...] - ins[1][...]
        outs[0][...] = e * (1.0 / d)

        @pl.when(pl.program_id(0) == 0)
        def _():
            outs[1][...] = jnp.zeros_like(outs[1])

        part = jnp.sum(jnp.sum(e * e, axis=1, keepdims=True), axis=0, keepdims=True)
        outs[1][...] += jnp.broadcast_to(part, outs[1].shape)

    row = pl.BlockSpec((tm, d), lambda i, j, k: (i, 0))
    acc = pl.BlockSpec((8, LANES), lambda i, j, k: (0, 0))
    return _fused("loss_dy", (t // tm, 1, 1), [(y, row), (target, row)],
                  [(_sds((t, d), F32), row), (_sds((8, LANES), F32), acc)], [], epilogue,
                  temp_bytes=3 * tm * d * 4, semantics=("arbitrary", "arbitrary", "arbitrary"))


def _ffn_up(name, h, wgu):
    t, d = h.shape
    nb = wgu.shape[2]
    f = 4 * nb
    tm = min(t, 512)

    def body(h_ref, wg_ref, wu_ref, gu_ref, a_ref):
        hv = h_ref[...]
        for c0 in range(0, nb, MXU_COLS):
            cs = slice(c0, min(c0 + MXU_COLS, nb))
            g = jnp.dot(hv, wg_ref[:, cs], preferred_element_type=F32)
            u = jnp.dot(hv, wu_ref[:, cs], preferred_element_type=F32)
            gu_ref[0, :, cs] = g.astype(BF16)
            gu_ref[1, :, cs] = u.astype(BF16)
            a_ref[:, cs] = (g * _sigmoid(g) * u).astype(BF16)

    blocks = tm * d * 2 + 2 * d * nb * 2 + 3 * tm * nb * 2
    return pl.pallas_call(
        body, name=name, grid=(4, t // tm),
        in_specs=[pl.BlockSpec((tm, d), lambda j, i: (i, 0)),
                  pl.BlockSpec((None, d, nb), lambda j, i: (j, 0, 0)),
                  pl.BlockSpec((None, d, nb), lambda j, i: (j + 4, 0, 0))],
        out_specs=[pl.BlockSpec((2, tm, nb), lambda j, i: (0, i, j)),
                   pl.BlockSpec((tm, nb), lambda j, i: (i, j))],
        out_shape=[_sds((2, t, f), BF16), _sds((t, f), BF16)],
        compiler_params=_params(("parallel", "parallel"), blocks, 8 * tm * MXU_COLS * 4),
    )(h, wgu, wgu)


def _ffn_down(name, a, wd, x):
    t, f = a.shape
    d = wd.shape[1]
    tm = min(t, 512)
    tk = f // 4

    def epilogue(acc, ins, outs):
        outs[0][...] = ins[2][...] + 0.5 * acc

    return _fused(name, (t // tm, 1, 4),
                  [(a, pl.BlockSpec((tm, tk), lambda i, j, k: (i, k))),
                   (wd, pl.BlockSpec((tk, d), lambda i, j, k: (k, 0))),
                   (x, pl.BlockSpec((tm, d), lambda i, j, k: (i, 0)))],
                  [(_sds((t, d), F32), pl.BlockSpec((tm, d), lambda i, j, k: (i, 0)))],
                  [(0, 1, NN)], epilogue, nk=4, acc_shape=(tm, d), temp_bytes=2 * tm * d * 4)[0]


def _ffn_bwd_act(name, dy, wd, gu, deps=()):
    t, d = dy.shape
    f = wd.shape[0]
    nb = f // 4
    tm = min(t, 512)

    def body(dy_ref, wd_ref, gu_ref, *rest):
        dgu_ref = rest[-1]
        dyv = dy_ref[...].astype(BF16)
        for c0 in range(0, nb, MXU_COLS):
            cs = slice(c0, min(c0 + MXU_COLS, nb))
            da = 0.5 * lax.dot_general(dyv, wd_ref[cs, :], NT, preferred_element_type=F32)
            g = gu_ref[0, :, cs].astype(F32)
            u = gu_ref[1, :, cs].astype(F32)
            s = _sigmoid(g)
            dgu_ref[0, :, cs] = (da * u * (s * (1.0 + g * (1.0 - s)))).astype(BF16)
            dgu_ref[1, :, cs] = (da * (g * s)).astype(BF16)

    blocks = tm * d * 4 + nb * d * 2 + 4 * tm * nb * 2
    return pl.pallas_call(
        body, name=name, grid=(4, t // tm),
        in_specs=[pl.BlockSpec((tm, d), lambda j, i: (i, 0)),
                  pl.BlockSpec((nb, d), lambda j, i: (j, 0)),
                  pl.BlockSpec((2, tm, nb), lambda j, i: (0, i, j))] + [_ANY] * len(deps),
        out_specs=pl.BlockSpec((2, tm, nb), lambda j, i: (0, i, j)),
        out_shape=_sds((2, t, f), BF16),
        compiler_params=_params(("parallel", "parallel"), blocks, tm * d * 2 + 8 * tm * MXU_COLS * 4),
    )(dy, wd, gu, *deps)


def _ffn_bwd_dwd(name, gu, dy):
    _, t, f = gu.shape
    d = dy.shape[1]
    tm = f // 4
    tk = min(t, 512)

    def products(ins):
        g = ins[0][0].astype(F32)
        a = (g * _sigmoid(g) * ins[0][1].astype(F32)).astype(BF16)
        return lax.dot_general(a, ins[1][...].astype(BF16), TN, preferred_element_type=F32)

    def epilogue(acc, ins, outs):
        outs[0][...] = (0.5 * acc).astype(BF16)

    return _fused(name, (4, 1, t // tk),
                  [(gu, pl.BlockSpec((2, tk, tm), lambda i, j, k: (0, k, i))),
                   (dy, pl.BlockSpec((tk, d), lambda i, j, k: (k, 0)))],
                  [(_sds((f, d), BF16), pl.BlockSpec((tm, d), lambda i, j, k: (i, 0)))],
                  products, epilogue, nk=t // tk, acc_shape=(tm, d), temp_bytes=tm * d * 4 + 3 * tk * tm * 4)[0]


def _ffn_bwd_dh(name, dgu, wgu, deps=()):
    _, t, f = dgu.shape
    d, nb = wgu.shape[1], wgu.shape[2]
    tm = min(t, 512)

    def epilogue(acc, ins, outs):
        outs[0][...] = acc

    return _fused(name, (t // tm, 1, N_DEV),
                  [(dgu, pl.BlockSpec((None, tm, nb), lambda i, j, k: (k // 4, i, k % 4))),
                   (wgu, pl.BlockSpec((None, d, nb), lambda i, j, k: (k, 0, 0)))],
                  [(_sds((t, d), F32), pl.BlockSpec((tm, d), lambda i, j, k: (i, 0)))],
                  [(0, 1, NT)], epilogue, nk=N_DEV, acc_shape=(tm, d), temp_bytes=tm * d * 4, deps=deps)[0]


def _ffn_bwd_dwgu(name, h, dgu, deps=()):
    t, d = h.shape
    nb = dgu.shape[2] // 4
    tk = min(t, 512)

    def epilogue(acc, ins, outs):
        outs[0][...] = acc.astype(BF16)

    return _fused(name, (N_DEV, 1, t // tk),
                  [(h, pl.BlockSpec((tk, d), lambda i, j, k: (k, 0))),
                   (dgu, pl.BlockSpec((None, tk, nb), lambda i, j, k: (i // 4, k, i % 4)))],
                  [(_sds((N_DEV, d, nb), BF16), pl.BlockSpec((None, d, nb), lambda i, j, k: (i, 0, 0)))],
                  [(0, 1, TN)], epilogue, nk=t // tk, acc_shape=(d, nb), temp_bytes=d * nb * 4, deps=deps)[0]


def _proj(h, w_in):
    t, d = h.shape
    nb = w_in.shape[3]
    tm = min(t, 512)

    def body(h_ref, w_ref, o_ref):
        hv = h_ref[...]
        o_ref[:, 0:nb] = jnp.dot(hv, w_ref[0], preferred_element_type=F32)
        o_ref[:, nb:2 * nb] = jnp.dot(hv, w_ref[1], preferred_element_type=F32)

    blocks = tm * d * 2 + 2 * d * nb * 2 + tm * 2 * nb * 4
    return pl.pallas_call(
        body, name="mix_proj", grid=(4, t // tm),
        in_specs=[pl.BlockSpec((tm, d), lambda j, i: (i, 0)),
                  pl.BlockSpec((None, 2, d, nb), lambda j, i: (j, 0, 0, 0))],
        out_specs=pl.BlockSpec((tm, 2 * nb), lambda j, i: (i, j)),
        out_shape=_sds((t, N_DEV * nb), F32),
        compiler_params=_params(("parallel", "parallel"), blocks, 2 * tm * nb * 4),
    )(h, w_in)


def _shift_rows(u, k):
    t = u.shape[0]
    rolled = pltpu.roll(u, k % t, axis=0)
    row = lax.broadcasted_iota(jnp.int32, u.shape, 0)
    keep = (row >= k) if k > 0 else (row < t + k)
    return jnp.where(keep, rolled, 0.0)


def _conv_fwd(proj, conv_w):
    t = proj.shape[0]
    cw = conv_w.shape[1]
    tc = min(cw, 256)
    nc = cw // tc

    def epilogue(_, ins, outs):
        u = ins[2][...] * ins[0][...]
        w = ins[3][...]
        y = u * w[2:3, :] + _shift_rows(u, 1) * w[1:2, :] + _shift_rows(u, 2) * w[0:1, :]
        outs[0][...] = (ins[1][...] * y).astype(BF16)

    def col(seg):
        return pl.BlockSpec((t, tc), lambda i, j, k: (0, seg * nc + i))

    return _fused("conv_fwd", (nc, 1, 1),
                  [(proj, col(0)), (proj, col(1)), (proj, col(2)),
                   (conv_w, pl.BlockSpec((8, tc), lambda i, j, k: (0, i)))],
                  [(_sds((t, cw), BF16), pl.BlockSpec((t, tc), lambda i, j, k: (0, i)))],
                  [], epilogue, temp_bytes=6 * t * tc * 4)[0]


def _conv_bwd(proj, conv_w, dca, deps=()):
    t = proj.shape[0]
    cw = conv_w.shape[1]
    tc = min(cw, 256)
    nc = cw // tc

    def epilogue(_, ins, outs):
        xc, bg, cg, w, dc = ins[0][...], ins[1][...], ins[2][...], ins[3][...], ins[4][...]
        u = cg * xc
        u1, u2 = _shift_rows(u, 1), _shift_rows(u, 2)
        y = u * w[2:3, :] + u1 * w[1:2, :] + u2 * w[0:1, :]
        dconv = dc * bg
        du = dconv * w[2:3, :] + _shift_rows(dconv, -1) * w[1:2, :] + _shift_rows(dconv, -2) * w[0:1, :]
        outs[0][0] = (du * cg).astype(BF16)
        outs[0][1] = (dc * y).astype(BF16)
        outs[0][2] = (du * xc).astype(BF16)
        outs[1][...] = jnp.zeros_like(outs[1])
        outs[1][0:1, :] = jnp.sum(dconv * u2, axis=0, keepdims=True)
        outs[1][1:2, :] = jnp.sum(dconv * u1, axis=0, keepdims=True)
        outs[1][2:3, :] = jnp.sum(dconv * u, axis=0, keepdims=True)

    def col(seg):
        return pl.BlockSpec((t, tc), lambda i, j, k: (0, seg * nc + i))

    own = pl.BlockSpec((t, tc), lambda i, j, k: (0, i))
    wspec = pl.BlockSpec((8, tc), lambda i, j, k: (0, i))
    return _fused("conv_bwd", (nc, 1, 1),
                  [(proj, col(0)), (proj, col(1)), (proj, col(2)), (conv_w, wspec), (dca, own)],
                  [(_sds((3, t, cw), BF16), pl.BlockSpec((3, t, tc), lambda i, j, k: (0, 0, i))),
                   (_sds((8, cw), F32), wspec)],
                  [], epilogue, temp_bytes=10 * t * tc * 4, deps=deps)


def _split3(x):
    hi = x.astype(BF16)
    r1 = x - hi.astype(F32)
    mid = r1.astype(BF16)
    lo = (r1 - mid.astype(F32)).astype(BF16)
    return hi, mid, lo


def _head_selector(width):
    r = lax.broadcasted_iota(jnp.int32, (width, LANES), 0)
    c = lax.broadcasted_iota(jnp.int32, (width, LANES), 1)
    return (lax.shift_right_logical(r, 6) == c).astype(BF16)


def _head_sum(x, sel):
    return sum(jnp.dot(p, sel, preferred_element_type=F32) for p in _split3(x))


def _head_bcast(r, sel):
    return sum(lax.dot_general(p, sel, NT, preferred_element_type=F32) for p in _split3(r))


def _rope(x, c, sa, sb):
    n = x.shape[1]
    return x * c + pltpu.roll(x, n - ROT_DIM // 2, axis=1) * sa + pltpu.roll(x, ROT_DIM // 2, axis=1) * sb


def _rope_t(d, c, sa, sb):
    n = d.shape[1]
    return d * c + pltpu.roll(d * sa, ROT_DIM // 2, axis=1) + pltpu.roll(d * sb, n - ROT_DIM // 2, axis=1)


def _tile_lanes(tab, width):
    return tab if width == tab.shape[1] else jnp.tile(tab, (1, width // tab.shape[1]))


def _qk_prep(proj, gq, gk, rope_tabs, cw, kw):
    t = proj.shape[0]
    tm = _row_tile(t)

    def epilogue(_, ins, outs):
        c, sa, sb = ins[5][...], ins[6][...], ins[7][...]
        for src, gain, dst, width in ((0, 3, 0, cw), (1, 4, 1, kw)):
            xv = ins[src][...]
            sel = _head_selector(width)
            r = lax.rsqrt(_head_sum(xv * xv, sel) * (1.0 / HEAD_DIM) + RMS_EPS)
            xn = xv * _head_bcast(r, sel) * ins[gain][...]
            outs[dst][...] = _rope(xn, _tile_lanes(c, width), _tile_lanes(sa, width), _tile_lanes(sb, width)).astype(BF16)
        outs[2][...] = ins[2][...].astype(BF16)

    kblk = cw // kw
    tab = pl.BlockSpec((tm, LANES), lambda i, j, k: (i, 0))
    kspec = pl.BlockSpec((tm, kw), lambda i, j, k: (i, 0))
    return _fused("qk_prep", (t // tm, 1, 1),
                  [(proj, pl.BlockSpec((tm, cw), lambda i, j, k: (i, 3))),
                   (proj, pl.BlockSpec((tm, kw), lambda i, j, k: (i, 4 * kblk))),
                   (proj, pl.BlockSpec((tm, kw), lambda i, j, k: (i, 4 * kblk + 1))),
                   (gq, pl.BlockSpec((1, cw), lambda i, j, k: (0, 0))),
                   (gk, pl.BlockSpec((1, kw), lambda i, j, k: (0, 0))),
                   (rope_tabs[0], tab), (rope_tabs[1], tab), (rope_tabs[2], tab)],
                  [(_sds((t, cw), BF16), pl.BlockSpec((tm, cw), lambda i, j, k: (i, 0))),
                   (_sds((t, kw), BF16), kspec), (_sds((t, kw), BF16), kspec)],
                  [], epilogue, temp_bytes=12 * tm * cw * 4)


def _qk_prep_bwd(proj, gq, gk, rope_tabs, dq, dkc, dkp, dvc, dvp, cw, kw):
    t = proj.shape[0]
    tm = BLOCK
    nblk = t // tm

    def epilogue(_, ins, outs):
        c, sa, sb = ins[5][...], ins[6][...], ins[7][...]
        has_next = (pl.program_id(0) < nblk - 1).astype(F32)
        dk = ins[9][...] + has_next * ins[10][...]
        dv = ins[11][...] + has_next * ins[12][...]
        pieces = []
        for src, gain, dval, dst, width in ((0, 3, ins[8][...], 1, cw), (1, 4, dk, 2, kw)):
            xv, gv = ins[src][...], ins[gain][...]
            sel = _head_selector(width)
            r = _head_bcast(lax.rsqrt(_head_sum(xv * xv, sel) * (1.0 / HEAD_DIM) + RMS_EPS), sel)
            xh = xv * r
            dxn = _rope_t(dval, _tile_lanes(c, width), _tile_lanes(sa, width), _tile_lanes(sb, width))
            u = dxn * gv
            dot = _head_bcast(_head_sum(u * xh, sel), sel) * (1.0 / HEAD_DIM)
            pieces.append((r * (u - xh * dot)).astype(BF16))
            ri = lax.broadcasted_iota(jnp.int32, (width, LANES), 0)
            ci = lax.broadcasted_iota(jnp.int32, (width, LANES), 1)
            fold = (lax.bitwise_and(ri, HEAD_DIM - 1) == ci).astype(BF16)
            colsum = jnp.broadcast_to(jnp.sum(dxn * xh, axis=0, keepdims=True), (8, width))
            part = sum(jnp.dot(p, fold, preferred_element_type=F32) for p in _split3(colsum))

            @pl.when(pl.program_id(0) == 0)
            def _():
                outs[dst][...] = jnp.zeros_like(outs[dst])

            outs[dst][0:1, :] += part[0:1, :]
        outs[0][:, 0:cw] = pieces[0]
        outs[0][:, cw:cw + kw] = pieces[1]
        outs[0][:, cw + kw:cw + 2 * kw] = dv.astype(BF16)

    kblk = cw // kw
    tab = pl.BlockSpec((tm, LANES), lambda i, j, k: (i, 0))
    kcur = pl.BlockSpec((tm, kw), lambda i, j, k: (i, 0))
    knext = pl.BlockSpec((tm, kw), lambda i, j, k: (jnp.minimum(i + 1, nblk - 1), 0))
    acc = pl.BlockSpec((8, LANES), lambda i, j, k: (0, 0))
    return _fused("qk_prep_bwd", (nblk, 1, 1),
                  [(proj, pl.BlockSpec((tm, cw), lambda i, j, k: (i, 3))),
                   (proj, pl.BlockSpec((tm, kw), lambda i, j, k: (i, 4 * kblk))),
                   (proj, pl.BlockSpec((tm, kw), lambda i, j, k: (i, 4 * kblk + 1))),
                   (gq, pl.BlockSpec((1, cw), lambda i, j, k: (0, 0))),
                   (gk, pl.BlockSpec((1, kw), lambda i, j, k: (0, 0))),
                   (rope_tabs[0], tab), (rope_tabs[1], tab), (rope_tabs[2], tab),
                   (dq, pl.BlockSpec((tm, cw), lambda i, j, k: (i, 0))),
                   (dkc, kcur), (dkp, knext), (dvc, kcur), (dvp, knext)],
                  [(_sds((t, cw + 2 * kw), BF16), pl.BlockSpec((tm, cw + 2 * kw), lambda i, j, k: (i, 0))),
                   (_sds((8, LANES), F32), acc), (_sds((8, LANES), F32), acc)],
                  [], epilogue, temp_bytes=16 * tm * cw * 4, semantics=("arbitrary", "arbitrary", "arbitrary"))


def _attn_mask(n):
    row = lax.broadcasted_iota(jnp.int32, (GROUP * BLOCK, 2 * BLOCK), 0)
    row = lax.bitwise_and(row, BLOCK - 1)
    col = lax.broadcasted_iota(jnp.int32, (GROUP * BLOCK, 2 * BLOCK), 1)
    return (col > row) & (col <= row + BLOCK) & ((col >= BLOCK) | (n > 0))


def _stack_heads(x, h):
    return jnp.concatenate([x[:, (h * GROUP + g) * HEAD_DIM:(h * GROUP + g + 1) * HEAD_DIM] for g in range(GROUP)], axis=0)


def _stack_sinks(sink_ref, h):
    return jnp.concatenate([jnp.broadcast_to(sink_ref[h * GROUP + g:h * GROUP + g + 1, 0:1], (BLOCK, 1))
                            for g in range(GROUP)], axis=0)


def _softmax_with_sink(q, k2, sink, valid):
    s = lax.dot_general(q, k2, NT, preferred_element_type=F32) * ATTN_SCALE
    s = jnp.where(valid, s, NEG_INF)
    m = jnp.maximum(jnp.max(s, axis=-1, keepdims=True), sink)
    p = jnp.exp(s - m)
    es = jnp.exp(sink - m)
    denom = jnp.sum(p, axis=-1, keepdims=True) + es
    return p / denom, es / denom


def _attn_fwd(qn, kn, vb, sink_rows):
    t, cw = qn.shape
    kw = kn.shape[1]
    nkv = kw // HEAD_DIM

    def body(q_ref, kp_ref, kc_ref, vp_ref, vc_ref, sink_ref, o_ref):
        valid = _attn_mask(pl.program_id(0))
        qv = q_ref[...]
        kp, kc, vp, vc = kp_ref[...], kc_ref[...], vp_ref[...], vc_ref[...]
        outs = []
        for h in range(nkv):
            hs = slice(h * HEAD_DIM, (h + 1) * HEAD_DIM)
            k2 = jnp.concatenate([kp[:, hs], kc[:, hs]], axis=0)
            v2 = jnp.concatenate([vp[:, hs], vc[:, hs]], axis=0)
            pn, _ = _softmax_with_sink(_stack_heads(qv, h), k2, _stack_sinks(sink_ref, h), valid)
            o4 = jnp.dot(pn.astype(BF16), v2, preferred_element_type=F32)
            outs += [o4[g * BLOCK:(g + 1) * BLOCK] for g in range(GROUP)]
        o_ref[...] = jnp.concatenate(outs, axis=-1).astype(BF16)

    cur = lambda n: (n, 0)
    prev = lambda n: (jnp.maximum(n - 1, 0), 0)
    return pl.pallas_call(
        body, name="attn_fwd", grid=(t // BLOCK,),
        in_specs=[pl.BlockSpec((BLOCK, cw), cur),
                  pl.BlockSpec((BLOCK, kw), prev), pl.BlockSpec((BLOCK, kw), cur),
                  pl.BlockSpec((BLOCK, kw), prev), pl.BlockSpec((BLOCK, kw), cur),
                  pl.BlockSpec(sink_rows.shape, lambda n: (0, 0))],
        out_specs=pl.BlockSpec((BLOCK, cw), cur),
        out_shape=_sds((t, cw), BF16),
        compiler_params=_params(("parallel",), BLOCK * (cw + 4 * kw) * 2 + BLOCK * cw * 2, 8 << 20),
    )(qn, kn, kn, vb, vb, sink_rows)


def _attn_bwd(qn, kn, vb, sink_rows, do):
    t, cw = qn.shape
    kw = kn.shape[1]
    nkv = kw // HEAD_DIM
    nq = nkv * GROUP

    def body(q_ref, kp_ref, kc_ref, vp_ref, vc_ref, sink_ref, do_ref,
             dq_ref, dkc_ref, dkp_ref, dvc_ref, dvp_ref, dsink_ref):
        n = pl.program_id(0)
        valid = _attn_mask(n)
        qv, dov = q_ref[...], do_ref[...]
        kp, kc, vp, vc = kp_ref[...], kc_ref[...], vp_ref[...], vc_ref[...]
        dqs, dks, dvs, dsinks = [], [], [], []
        for h in range(nkv):
            hs = slice(h * HEAD_DIM, (h + 1) * HEAD_DIM)
            k2 = jnp.concatenate([kp[:, hs], kc[:, hs]], axis=0)
            v2 = jnp.concatenate([vp[:, hs], vc[:, hs]], axis=0)
            q4 = _stack_heads(qv, h)
            pn, psink = _softmax_with_sink(q4, k2, _stack_sinks(sink_ref, h), valid)
            dob = _stack_heads(dov, h).astype(BF16)
            dpn = lax.dot_general(dob, v2, NT, preferred_element_type=F32)
            dvs.append(lax.dot_general(pn.astype(BF16), dob, TN, preferred_element_type=F32))
            delta = jnp.sum(pn * dpn, axis=-1, keepdims=True)
            ds = (pn * (dpn - delta) * ATTN_SCALE).astype(BF16)
            dq4 = jnp.dot(ds, k2, preferred_element_type=F32)
            dks.append(lax.dot_general(ds, q4, TN, preferred_element_type=F32))
            dsink4 = -psink * delta
            for g in range(GROUP):
                rows = slice(g * BLOCK, (g + 1) * BLOCK)
                dqs.append(dq4[rows])
                dsinks.append(jnp.broadcast_to(jnp.sum(dsink4[rows], axis=0, keepdims=True), (1, LANES)))
        dq_ref[...] = jnp.concatenate(dqs, axis=-1)
        dkp_ref[...] = jnp.concatenate([d[:BLOCK] for d in dks], axis=-1)
        dkc_ref[...] = jnp.concatenate([d[BLOCK:] for d in dks], axis=-1)
        dvp_ref[...] = jnp.concatenate([d[:BLOCK] for d in dvs], axis=-1)
        dvc_ref[...] = jnp.concatenate([d[BLOCK:] for d in dvs], axis=-1)

        @pl.when(n == 0)
        def _():
            dsink_ref[...] = jnp.zeros_like(dsink_ref)

        dsink_ref[...] += jnp.concatenate(dsinks, axis=0)

    cur = lambda n: (n, 0)
    prev = lambda n: (jnp.maximum(n - 1, 0), 0)
    kspec = pl.BlockSpec((BLOCK, kw), cur)
    return pl.pallas_call(
        body, name="attn_bwd", grid=(t // BLOCK,),
        in_specs=[pl.BlockSpec((BLOCK, cw), cur),
                  pl.BlockSpec((BLOCK, kw), prev), kspec,
                  pl.BlockSpec((BLOCK, kw), prev), kspec,
                  pl.BlockSpec(sink_rows.shape, lambda n: (0, 0)),
                  pl.BlockSpec((BLOCK, cw), cur)],
        out_specs=[pl.BlockSpec((BLOCK, cw), cur), kspec, kspec, kspec, kspec,
                   pl.BlockSpec((nq, LANES), lambda n: (0, 0))],
        out_shape=[_sds((t, cw), F32)] + [_sds((t, kw), F32)] * 4 + [_sds((nq, LANES), F32)],
        compiler_params=_params(("arbitrary",), BLOCK * (cw + 4 * kw) * 2 + 2 * BLOCK * cw * 4 + 4 * BLOCK * kw * 4, 12 << 20),
    )(qn, kn, kn, vb, vb, sink_rows, do)


def _mix_out(ca, o, woc, woa, proj):
    t, cw = ca.shape
    nb = woc.shape[2]
    d = N_DEV * nb
    tm = min(t, 1024)
    ga0 = (3 * cw + cw + 2 * (cw // 4)) // nb

    def body(ca_ref, o_ref, woc_ref, woa_ref, ga_ref, gb_ref, m_ref, ya_ref, yb_ref):
        ya = jnp.dot(ca_ref[...], woc_ref[...], preferred_element_type=F32)
        yb = jnp.dot(o_ref[...], woa_ref[...], preferred_element_type=F32)
        ya_ref[...] = ya.astype(BF16)
        yb_ref[...] = yb.astype(BF16)
        m_ref[...] = (_sigmoid(ga_ref[...]) * ya + _sigmoid(gb_ref[...]) * yb).astype(BF16)

    act = pl.BlockSpec((tm, cw), lambda i, j: (i, 0))
    wsp = pl.BlockSpec((None, cw, nb), lambda i, j: (j, 0, 0))
    osp = pl.BlockSpec((tm, nb), lambda i, j: (i, j))
    blocks = 2 * tm * cw * 2 + 2 * cw * nb * 2 + 2 * tm * nb * 4 + 3 * tm * nb * 2
    return pl.pallas_call(
        body, name="mix_out", grid=(t // tm, N_DEV),
        in_specs=[act, act, wsp, wsp,
                  pl.BlockSpec((tm, nb), lambda i, j: (i, ga0 + j)),
                  pl.BlockSpec((tm, nb), lambda i, j: (i, ga0 + N_DEV + j))],
        out_specs=[osp, osp, osp],
        out_shape=[_sds((t, d), BF16)] * 3,
        compiler_params=_params(("parallel", "parallel"), blocks, 6 * tm * nb * 4),
    )(ca, o, woc, woa, proj, proj)


def _mix_residual(merged, wo, x):
    t, d = x.shape
    tm = min(t, 512)

    def epilogue(acc, ins, outs):
        outs[0][...] = ins[2][...] + acc

    row = pl.BlockSpec((tm, d), lambda i, j, k: (i, 0))
    return _fused("mix_residual", (t // tm, 1, 1),
                  [(merged, row), (wo, pl.BlockSpec((d, d), lambda i, j, k: (0, 0))), (x, row)],
                  [(_sds((t, d), F32), row)], [(0, 1, NN)], epilogue, temp_bytes=2 * tm * d * 4)[0]


def _mix_bwd_gates(dx, wo, ya, yb, proj, cw):
    t, d = dx.shape
    tm = min(t, 512)
    tn = min(d, 512)
    ga0 = (4 * cw + 2 * (cw // 4)) // tn

    def epilogue(acc, ins, outs):
        sa, sb = _sigmoid(ins[4][...]), _sigmoid(ins[5][...])
        outs[0][...] = (acc * sa).astype(BF16)
        outs[1][...] = (acc * sb).astype(BF16)
        outs[2][0] = (acc * ins[2][...].astype(F32) * sa * (1.0 - sa)).astype(BF16)
        outs[2][1] = (acc * ins[3][...].astype(F32) * sb * (1.0 - sb)).astype(BF16)

    blk = pl.BlockSpec((tm, tn), lambda i, j, k: (i, j))
    return _fused("mix_bwd_gates", (t // tm, d // tn, 1),
                  [(dx, pl.BlockSpec((tm, d), lambda i, j, k: (i, 0))),
                   (wo, pl.BlockSpec((tn, d), lambda i, j, k: (j, 0))),
                   (ya, blk), (yb, blk),
                   (proj, pl.BlockSpec((tm, tn), lambda i, j, k: (i, ga0 + j))),
                   (proj, pl.BlockSpec((tm, tn), lambda i, j, k: (i, ga0 + d // tn + j)))],
                  [(_sds((t, d), BF16), blk), (_sds((t, d), BF16), blk),
                   (_sds((2, t, d), BF16), pl.BlockSpec((2, tm, tn), lambda i, j, k: (0, i, j)))],
                  [(0, 1, NT)], epilogue, temp_bytes=8 * tm * tn * 4)


def _tn_matmul(name, a, b, tm, out_dtype=BF16):
    t, m = a.shape
    n = b.shape[1]
    tk = min(t, 512)

    def epilogue(acc, ins, outs):
        outs[0][...] = acc.astype(out_dtype)

    return _fused(name, (m // tm, 1, t // tk),
                  [(a, pl.BlockSpec((tk, tm), lambda i, j, k: (k, i))),
                   (b, pl.BlockSpec((tk, n), lambda i, j, k: (k, 0)))],
                  [(_sds((m, n), out_dtype), pl.BlockSpec((tm, n), lambda i, j, k: (i, 0)))],
                  [(0, 1, TN)], epilogue, nk=t // tk, acc_shape=(tm, n), temp_bytes=tm * n * 4)[0]


def _out_proj_bwd_act(name, dy, w, deps=()):
    t, d = dy.shape
    kdim, nb = w.shape[1], w.shape[2]
    tm = min(t, 1024)

    def epilogue(acc, ins, outs):
        outs[0][...] = acc

    return _fused(name, (t // tm, 1, N_DEV),
                  [(dy, pl.BlockSpec((tm, nb), lambda i, j, k: (i, k))),
                   (w, pl.BlockSpec((None, kdim, nb), lambda i, j, k: (k, 0, 0)))],
                  [(_sds((t, kdim), F32), pl.BlockSpec((tm, kdim), lambda i, j, k: (i, 0)))],
                  [(0, 1, NT)], epilogue, nk=N_DEV, acc_shape=(tm, kdim), temp_bytes=tm * kdim * 4, deps=deps)[0]


def _out_proj_bwd_w(name, act, dy, nb):
    t, kdim = act.shape
    tk = min(t, 1024)

    def epilogue(acc, ins, outs):
        outs[0][...] = acc.astype(BF16)

    return _fused(name, (N_DEV, 1, t // tk),
                  [(act, pl.BlockSpec((tk, kdim), lambda i, j, k: (k, 0))),
                   (dy, pl.BlockSpec((tk, nb), lambda i, j, k: (k, i)))],
                  [(_sds((N_DEV, kdim, nb), BF16), pl.BlockSpec((None, kdim, nb), lambda i, j, k: (i, 0, 0)))],
                  [(0, 1, TN)], epilogue, nk=t // tk, acc_shape=(kdim, nb), temp_bytes=kdim * nb * 4)[0]


def _proj_bwd_act(dproj, w_in, deps=()):
    t, n = dproj.shape
    d, nb = w_in.shape[2], w_in.shape[3]
    tm = min(t, 512)

    def epilogue(acc, ins, outs):
        outs[0][...] = acc

    def products(ins):
        return (lax.dot_general(ins[0][:, 0:nb], ins[1][0], NT, preferred_element_type=F32)
                + lax.dot_general(ins[0][:, nb:2 * nb], ins[1][1], NT, preferred_element_type=F32))

    return _fused("mix_bwd_dh", (t // tm, 1, 4),
                  [(dproj, pl.BlockSpec((tm, 2 * nb), lambda i, j, k: (i, k))),
                   (w_in, pl.BlockSpec((None, 2, d, nb), lambda i, j, k: (k, 0, 0, 0)))],
                  [(_sds((t, d), F32), pl.BlockSpec((tm, d), lambda i, j, k: (i, 0)))],
                  products, epilogue, nk=4, acc_shape=(tm, d), temp_bytes=tm * d * 4, deps=deps)[0]


def _proj_bwd_w(h, dproj):
    t, d = h.shape
    nb = dproj.shape[1] // N_DEV
    tk = min(t, 512)
    nk = t // tk

    def body(h_ref, dp_ref, o_ref, acc0, acc1):
        k = pl.program_id(1)

        @pl.when(k == 0)
        def _():
            acc0[...] = jnp.zeros_like(acc0)
            acc1[...] = jnp.zeros_like(acc1)

        hv = h_ref[...]
        acc0[...] += lax.dot_general(hv, dp_ref[:, 0:nb], TN, preferred_element_type=F32)
        acc1[...] += lax.dot_general(hv, dp_ref[:, nb:2 * nb], TN, preferred_element_type=F32)

        @pl.when(k == nk - 1)
        def _():
            o_ref[0] = acc0[...].astype(BF16)
            o_ref[1] = acc1[...].astype(BF16)

    blocks = tk * d * 2 + tk * 2 * nb * 2 + 2 * d * nb * 2
    return pl.pallas_call(
        body, name="mix_bwd_dwin", grid=(4, nk),
        in_specs=[pl.BlockSpec((tk, d), lambda j, k: (k, 0)),
                  pl.BlockSpec((tk, 2 * nb), lambda j, k: (k, j))],
        out_specs=pl.BlockSpec((None, 2, d, nb), lambda j, k: (j, 0, 0, 0)),
        out_shape=_sds((4, 2, d, nb), BF16),
        scratch_shapes=[pltpu.VMEM((d, nb), F32), pltpu.VMEM((d, nb), F32)],
        compiler_params=_params(("parallel", "arbitrary"), blocks, 2 * d * nb * 4),
    )(h, dproj)


def _adamw_math(w, g, m, v):
    m = ADAM_B1 * m + (1.0 - ADAM_B1) * g
    v = ADAM_B2 * v + (1.0 - ADAM_B2) * (g * g)
    m_hat = m / (1.0 - ADAM_B1 ** ADAM_STEP)
    v_hat = v / (1.0 - ADAM_B2 ** ADAM_STEP)
    delta = -ADAM_LR * (m_hat / (jnp.sqrt(v_hat) + ADAM_EPS) + ADAM_WD * w)
    return delta, m, v


def _adamw(name, parts, w, m, v, tr):
    r, c = w.shape

    def body(p_ref, w_ref, m_ref, v_ref, g_out, d_out, m_out, v_out):
        g = p_ref[0].astype(F32)
        for s in range(1, N_DEV):
            g = g + p_ref[s].astype(F32)
        delta, mn, vn = _adamw_math(w_ref[...], g, m_ref[...], v_ref[...])
        g_out[...] = g
        d_out[...] = delta
        m_out[...] = mn
        v_out[...] = vn

    blk = pl.BlockSpec((tr, c), lambda i: (i, 0))
    blocks = N_DEV * tr * c * parts.dtype.itemsize + 7 * tr * c * 4
    return pl.pallas_call(
        body, name=name, grid=(r // tr,),
        in_specs=[pl.BlockSpec((N_DEV, tr, c), lambda i: (0, i, 0)), blk, blk, blk],
        out_specs=[blk] * 4, out_shape=[_sds((r, c), F32)] * 4,
        compiler_params=_params(("parallel",), blocks, 6 * tr * c * 4),
    )(parts, w, m, v)


def _adamw_chips(name, chip, own, landed, w, m, v, tr):
    r, c = w.shape

    def body(chip_ref, own_ref, land_ref, w_ref, m_ref, v_ref, g_out, d_out, m_out, v_out):
        mine = own_ref[...].astype(F32)
        g = jnp.zeros((tr, c), F32)
        for k in range(4):
            g = g + jnp.where(chip_ref[0] == k, mine, land_ref[k].astype(F32))
        delta, mn, vn = _adamw_math(w_ref[...], g, m_ref[...], v_ref[...])
        g_out[...] = g
        d_out[...] = delta
        m_out[...] = mn
        v_out[...] = vn

    blk = pl.BlockSpec((tr, c), lambda i, chip_ref: (i, 0))
    grid_spec = pltpu.PrefetchScalarGridSpec(
        num_scalar_prefetch=1, grid=(r // tr,),
        in_specs=[pl.BlockSpec((None, tr, c), lambda i, chip_ref: (chip_ref[0], i, 0)),
                  pl.BlockSpec((4, tr, c), lambda i, chip_ref: (0, i, 0)), blk, blk, blk],
        out_specs=[blk] * 4)
    blocks = 5 * tr * c * 2 + 7 * tr * c * 4
    return pl.pallas_call(
        body, name=name, grid_spec=grid_spec, out_shape=[_sds((r, c), F32)] * 4,
        compiler_params=_params(("parallel",), blocks, 6 * tr * c * 4),
    )(chip, own, landed, w, m, v)


def _rope_tables(t):
    half = ROT_DIM // 2
    inv_freq = 1.0 / (ROPE_THETA ** (jnp.arange(0, ROT_DIM, 2, dtype=F32) / ROT_DIM))
    ang = jnp.arange(t, dtype=F32)[:, None] * inv_freq[None, :]
    cos, sin = jnp.cos(ang), jnp.sin(ang)
    ones = jnp.ones((t, HEAD_DIM - ROT_DIM), F32)
    zeros = jnp.zeros((t, HEAD_DIM - half), F32)
    c = jnp.concatenate([cos, cos, ones], axis=1)
    sa = jnp.concatenate([-sin, zeros], axis=1)
    sb = jnp.concatenate([jnp.zeros((t, half), F32), sin, jnp.zeros((t, HEAD_DIM - ROT_DIM), F32)], axis=1)
    return tuple(jnp.tile(a, (1, LANES // HEAD_DIM)) for a in (c, sa, sb))


def _pad_rows(a, rows=8):
    return jnp.pad(a, ((0, rows - a.shape[0]), (0, 0)))


def kernel(x, g_ffn1, w_gu1, w_down1, g_mix, w_in, conv_w, q_norm_g, k_norm_g, sinks, w_out_conv, w_out_attn, w_o, g_ffn2, w_gu2, w_down2, loss_target, m_g_ffn1, m_w_gu1, m_w_down1, m_g_mix, m_w_in, m_conv_w, m_q_norm_g, m_k_norm_g, m_sinks, m_w_out_conv, m_w_out_attn, m_w_o, m_g_ffn2, m_w_gu2, m_w_down2, v_g_ffn1, v_w_gu1, v_w_down1, v_g_mix, v_w_in, v_conv_w, v_q_norm_g, v_k_norm_g, v_sinks, v_w_out_conv, v_w_out_attn, v_w_o, v_g_ffn2, v_w_gu2, v_w_down2):
    t, d = x.shape[1], x.shape[2]
    cw = d // 2
    kw = cw // GROUP
    nq = cw // HEAD_DIM
    xs, target = x.reshape(t, d), loss_target.reshape(t, d)
    me = 4 * lax.axis_index("x") + 2 * lax.axis_index("y") + lax.axis_index("c")

    big = {"w_gu1": w_gu1, "w_down1": w_down1, "w_in": w_in, "w_out_conv": w_out_conv,
           "w_out_attn": w_out_attn, "w_o": w_o, "w_gu2": w_gu2, "w_down2": w_down2}
    big_m = {"w_gu1": m_w_gu1, "w_down1": m_w_down1, "w_in": m_w_in, "w_out_conv": m_w_out_conv,
             "w_out_attn": m_w_out_attn, "w_o": m_w_o, "w_gu2": m_w_gu2, "w_down2": m_w_down2}
    big_v = {"w_gu1": v_w_gu1, "w_down1": v_w_down1, "w_in": v_w_in, "w_out_conv": v_w_out_conv,
             "w_out_attn": v_w_out_attn, "w_o": v_w_o, "w_gu2": v_w_gu2, "w_down2": v_w_down2}
    names = list(big)

    tiles = {"w_gu1": 256, "w_gu2": 256, "w_in": 256, "w_down1": 176, "w_down2": 176,
             "w_out_conv": 1024, "w_out_attn": 1024, "w_o": 128}

    def row_tile(n):
        r = big[n].shape[1]
        return tiles[n] if r % tiles[n] == 0 else r

    me_arr = me.astype(jnp.int32).reshape(1)
    sources = [(n, big[n][0], BF16, row_tile(n)) for n in names] + [("conv_w", _pad_rows(conv_w[0]), F32, 8)]
    issue_order = [0, 1, 2, 8, 3, 4, 5, 6, 7]
    first = _place_shard("place_" + names[0], sources[0][1], BF16, me_arr, sources[0][3])
    started = [_gather_start("gather_start_first", [first])]
    rest = [_place_shard("place_" + sources[i][0], sources[i][1], sources[i][2], me_arr, sources[i][3],
                         deps=(started[0][3],)) for i in issue_order[1:]]
    started.append(_gather_start("gather_start_rest", rest))
    where = {0: (0, 0)}
    where.update({i: (1, p) for p, i in enumerate(issue_order[1:])})

    def fetch(tag, idxs, after):
        call = where[idxs[0]][0]
        send, recv, stacks, _ = started[call]
        positions = [where[i][1] for i in idxs]
        got = _gather_wait("gather_wait_" + tag, positions, send, recv, [stacks[p] for p in positions], after)
        return _forward_to_sibling("gather_forward_" + tag, got)

    rope_tabs = _rope_tables(t)
    gq = jnp.tile(q_norm_g, (1, nq))
    gk = jnp.tile(k_norm_g, (1, nq // GROUP))
    sink_rows = jnp.broadcast_to(sinks[0][:, None], (nq, LANES))

    wts = {}
    h1 = _rms_fwd("ffn1_norm", xs, g_ffn1)
    wts["w_gu1"], = fetch("gu1", [0], started[1][3])
    gu1, a1 = _ffn_up("ffn1_up", h1, wts["w_gu1"])
    wts["w_down1"], = fetch("down1", [1], a1)
    wd1 = wts["w_down1"].reshape(-1, d)
    x1 = _ffn_down("ffn1_down", a1, wd1, xs)
    h2 = _rms_fwd("mix_norm", x1, g_mix)
    wts["w_in"], conv_land = fetch("in", [2, 8], h2)
    w_in_full = wts["w_in"].reshape(4, 2, d, -1)
    conv_full = jnp.transpose(conv_land, (1, 0, 2)).reshape(8, cw)
    proj = _proj(h2, w_in_full)
    ca = _conv_fwd(proj, conv_full)
    qn, kn, vb = _qk_prep(proj, gq, gk, rope_tabs, cw, kw)
    o = _attn_fwd(qn, kn, vb, sink_rows)
    wts["w_out_conv"], wts["w_out_attn"] = fetch("out", [3, 4], o)
    merged, ya, yb = _mix_out(ca, o, wts["w_out_conv"], wts["w_out_attn"], proj)
    wts["w_o"], = fetch("o", [5], merged)
    wo = wts["w_o"].reshape(d, d)
    x2 = _mix_residual(merged, wo, x1)
    h3 = _rms_fwd("ffn2_norm", x2, g_ffn2)
    wts["w_gu2"], = fetch("gu2", [6], h3)
    gu2, a2 = _ffn_up("ffn2_up", h3, wts["w_gu2"])
    wts["w_down2"], = fetch("down2", [7], a2)
    wd2 = wts["w_down2"].reshape(-1, d)
    y = _ffn_down("ffn2_down", a2, wd2, x2)
    dy, sq = _loss_dy(y, target)
    loss = lax.psum(sq[0, 0] * (0.5 / d), ("x", "y", "c"))

    core = lax.axis_index("c").astype(jnp.int32).reshape(1)
    chip = (2 * lax.axis_index("x") + lax.axis_index("y")).astype(jnp.int32).reshape(1)
    def pair_start(tag, group, grads):
        stacks = [grads[n].reshape((4, 2) + big[n].shape[1:]) for n in group]
        lands = [lax.empty((4,) + big[n].shape[1:], BF16) for n in group]
        return _pair_start("rs_pair_start_" + tag, stacks, lands)

    def chip_start(tag, group, pending, after):
        send, recv, stacks, lands, _ = pending
        stacks, lands = _pair_wait("rs_pair_wait_" + tag, send, recv, stacks, lands, after)
        parts = [_pair_add("rs_pair_add_" + n, st, ld, core, row_tile(n)) for n, st, ld in zip(group, stacks, lands)]
        lands2 = [lax.empty((4,) + big[n].shape[1:], BF16) for n in group]
        return _chip_start("rs_chip_start_" + tag, parts, lands2)

    group_a, group_b, group_c = ["w_down2", "w_gu2"], ["w_o", "w_out_conv", "w_out_attn"], ["w_in"]
    group_d, group_e = ["w_down1"], ["w_gu1"]
    g = {}
    dgu2 = _ffn_bwd_act("ffn2_bwd_act", dy, wd2, gu2)
    g["w_down2"] = _ffn_bwd_dwd("ffn2_bwd_dwd", gu2, dy)
    g["w_gu2"] = _ffn_bwd_dwgu("ffn2_bwd_dwgu", h3, dgu2)
    pend_a = pair_start("a", group_a, g)
    dh3 = _ffn_bwd_dh("ffn2_bwd_dh", dgu2, wts["w_gu2"], deps=(pend_a[4],))
    ring_a = chip_start("a", group_a, pend_a, dh3)
    dx2, dg_ffn2 = _rms_bwd("ffn2_bwd_rms", x2, g_ffn2, dh3, dy, deps=(ring_a[4],))

    dya, dyb, dgates = _mix_bwd_gates(dx2, wo, ya, yb, proj, cw)
    g["w_o"] = _tn_matmul("mix_bwd_dwo", merged, dx2, min(d, 1024))
    g["w_out_conv"] = _out_proj_bwd_w("mix_bwd_dwoc", ca, dya, d // N_DEV)
    g["w_out_attn"] = _out_proj_bwd_w("mix_bwd_dwoa", o, dyb, d // N_DEV)
    pend_b = pair_start("b", group_b, g)
    dca = _out_proj_bwd_act("mix_bwd_dca", dya, wts["w_out_conv"], deps=(pend_b[4],))
    do = _out_proj_bwd_act("mix_bwd_do", dyb, wts["w_out_attn"])
    ring_b = chip_start("b", group_b, pend_b, do)
    d3, dconv_w = _conv_bwd(proj, conv_full, dca, deps=(ring_b[4],))
    dq, dkc, dkp, dvc, dvp, dsink = _attn_bwd(qn, kn, vb, sink_rows, do)
    dqkv, dgq, dgk = _qk_prep_bwd(proj, gq, gk, rope_tabs, dq, dkc, dkp, dvc, dvp, cw, kw)
    dproj = jnp.concatenate([d3[0], d3[1], d3[2], dqkv, dgates[0], dgates[1]], axis=1)
    g["w_in"] = _proj_bwd_w(h2, dproj)
    pend_c = pair_start("c", group_c, g)
    dh2 = _proj_bwd_act(dproj, w_in_full, deps=(pend_c[4],))
    ring_c = chip_start("c", group_c, pend_c, dh2)
    dx1, dg_mix = _rms_bwd("mix_bwd_rms", x1, g_mix, dh2, dx2, deps=(ring_c[4],))

    g["w_down1"] = _ffn_bwd_dwd("ffn1_bwd_dwd", gu1, dx1)
    pend_d = pair_start("d", group_d, g)
    dgu1 = _ffn_bwd_act("ffn1_bwd_act", dx1, wd1, gu1, deps=(pend_d[4],))
    ring_d = chip_start("d", group_d, pend_d, dgu1)
    g["w_gu1"] = _ffn_bwd_dwgu("ffn1_bwd_dwgu", h1, dgu1, deps=(ring_d[4],))
    pend_e = pair_start("e", group_e, g)

    big_out = {}

    def finish(tag, group, ring, after):
        send, recv, parts, lands2, _ = ring
        parts, lands2 = _chip_wait("rs_chip_wait_" + tag, send, recv, parts, lands2, after)
        for n, own, landed in zip(group, parts, lands2):
            res = _adamw_chips("adamw_" + n, chip, own, landed, big[n][0], big_m[n][0], big_v[n][0], row_tile(n))
            big_out[n] = [a[None] for a in res]
            after = res[0]
        return after

    after = finish("a", group_a, ring_a, pend_e[4])
    ring_e = chip_start("e", group_e, pend_e, after)
    dh1 = _ffn_bwd_dh("ffn1_bwd_dh", dgu1, wts["w_gu1"], deps=(ring_e[4],))
    grad_x, dg_ffn1 = _rms_bwd("ffn1_bwd_rms", xs, g_ffn1, dh1, dx1)
    after = grad_x
    for tag, group, ring in (("b", group_b, ring_b), ("c", group_c, ring_c), ("d", group_d, ring_d), ("e", group_e, ring_e)):
        after = finish(tag, group, ring, after)

    small = {"g_ffn1": dg_ffn1[0:1], "g_mix": dg_mix[0:1], "g_ffn2": dg_ffn2[0:1],
             "q_norm_g": dgq[0:1, :HEAD_DIM], "k_norm_g": dgk[0:1, :HEAD_DIM], "sinks": dsink[:, 0][None],
             "conv_w": dconv_w[0:CONV_K].reshape(1, -1)}
    small_w = {"g_ffn1": g_ffn1, "g_mix": g_mix, "g_ffn2": g_ffn2, "q_norm_g": q_norm_g, "k_norm_g": k_norm_g,
               "sinks": sinks, "conv_w": None}
    small_m = {"g_ffn1": m_g_ffn1, "g_mix": m_g_mix, "g_ffn2": m_g_ffn2, "q_norm_g": m_q_norm_g,
               "k_norm_g": m_k_norm_g, "sinks": m_sinks, "conv_w": m_conv_w}
    small_v = {"g_ffn1": v_g_ffn1, "g_mix": v_g_mix, "g_ffn2": v_g_ffn2, "q_norm_g": v_q_norm_g,
               "k_norm_g": v_k_norm_g, "sinks": v_sinks, "conv_w": v_conv_w}
    snames = list(small)
    widths = [small[n].shape[1] for n in snames]
    total = sum(widths)
    rows = -(-total // LANES)
    rows = -(-rows // 8) * 8

    def pack(vals):
        flat = jnp.concatenate([v.reshape(1, -1) for v in vals], axis=1)
        return jnp.pad(flat, ((0, 0), (0, rows * LANES - total))).reshape(rows, LANES)

    csh = cw // N_DEV

    def place_conv(local, fill):
        full = jnp.full((CONV_K, cw), fill, F32)
        return lax.dynamic_update_slice(full, local, (0, me * csh)).reshape(1, -1)

    pw = pack([small_w[n] if n != "conv_w" else place_conv(conv_w[0], 0.0) for n in snames])
    pm = pack([small_m[n] if n != "conv_w" else place_conv(m_conv_w[0], 0.0) for n in snames])
    pv = pack([small_v[n] if n != "conv_w" else place_conv(v_conv_w[0], 1.0) for n in snames])
    parts = _exchange("gather_small_grads", [pack([small[n] for n in snames])], gather=True, deps=(after,))[0]
    sg, sd, sm, sv = [a.reshape(1, -1) for a in _adamw("adamw_small", parts, pw, pm, pv, rows)]

    def unpack(flat, n):
        off = sum(widths[:snames.index(n)])
        piece = flat[:, off:off + widths[snames.index(n)]]
        if n == "conv_w":
            piece = lax.dynamic_slice(piece.reshape(CONV_K, cw), (0, me * csh), (CONV_K, csh))[None]
        return piece

    order = ["g_ffn1", "w_gu1", "w_down1", "g_mix", "w_in", "conv_w", "q_norm_g", "k_norm_g", "sinks",
             "w_out_conv", "w_out_attn", "w_o", "g_ffn2", "w_gu2", "w_down2"]
    outs = [loss, grad_x[None]]
    for idx, flat in enumerate((sg, sd, sm, sv)):
        for n in order:
            outs.append(big_out[n][idx] if n in big_out else unpack(flat, n))
    return tuple(outs)
```

```python
import functools

import jax
import jax.numpy as jnp
from jax import lax
from jax.experimental import pallas as pl
from jax.experimental.pallas import tpu as pltpu

F32 = jnp.float32
BF16 = jnp.bfloat16

N_DEV = 8
HEAD_DIM = 64
GROUP = 4
BLOCK = 128
ROT_DIM = 16
ROPE_THETA = 500000.0
RMS_EPS = 1e-6
NEG_INF = -1e30
ATTN_SCALE = HEAD_DIM ** -0.5
CONV_K = 3
LANES = 128
MXU_COLS = 256
VMEM_BYTES_V7X = 64 * 1024 * 1024
VMEM_CAP = VMEM_BYTES_V7X - 6 * 1024 * 1024

ADAM_LR = 0.001
ADAM_B1 = 0.9
ADAM_B2 = 0.999
ADAM_EPS = 1e-08
ADAM_WD = 0.01
ADAM_STEP = 10

NN = (((1,), (0,)), ((), ()))
NT = (((1,), (1,)), ((), ()))
TN = (((0,), (0,)), ((), ()))

MESH = pl.DeviceIdType.MESH


def _nbytes(shape, dtype):
    n = 1
    for s in shape:
        if s is not None:
            n *= s
    return n * jnp.dtype(dtype).itemsize


def _params(semantics, block_bytes, temp_bytes):
    assert 2 * block_bytes + temp_bytes <= VMEM_CAP, (block_bytes, temp_bytes)
    return pltpu.CompilerParams(dimension_semantics=semantics, vmem_limit_bytes=VMEM_CAP)


def _fused(name, grid, ins, outs, dots, epilogue, *, nk=1, acc_shape=None, temp_bytes=0,
           semantics=("parallel", "parallel", "arbitrary"), deps=()):
    n_in, n_out = len(ins), len(outs)
    n_dep = len(deps)

    def body(*refs):
        in_refs, out_refs = refs[:n_in], refs[n_in + n_dep:n_in + n_dep + n_out]
        scratch = refs[n_in + n_dep + n_out:]

        def products():
            if callable(dots):
                return dots(in_refs)
            total = None
            for ai, bi, contract in dots:
                a, b = in_refs[ai][...], in_refs[bi][...]
                a = a if a.dtype == BF16 else a.astype(BF16)
                b = b if b.dtype == BF16 else b.astype(BF16)
                p = lax.dot_general(a, b, contract, preferred_element_type=F32)
                total = p if total is None else total + p
            return total

        if nk == 1:
            epilogue(products() if dots else None, in_refs, out_refs)
        else:
            acc = scratch[0]
            k = pl.program_id(2)

            @pl.when(k == 0)
            def _():
                acc[...] = jnp.zeros_like(acc)

            acc[...] += products()

            @pl.when(k == nk - 1)
            def _():
                epilogue(acc[...], in_refs, out_refs)

    block_bytes = sum(_nbytes(spec.block_shape, a.dtype) for a, spec in ins)
    block_bytes += sum(_nbytes(spec.block_shape, s.dtype) for s, spec in outs)
    scratch_shapes = []
    if nk > 1:
        scratch_shapes.append(pltpu.VMEM(acc_shape, F32))
        temp_bytes += _nbytes(acc_shape, F32)
    res = pl.pallas_call(
        body, name=name, grid=grid,
        in_specs=[spec for _, spec in ins] + [pl.BlockSpec(memory_space=pl.ANY)] * n_dep,
        out_specs=[spec for _, spec in outs],
        out_shape=[s for s, _ in outs],
        scratch_shapes=scratch_shapes,
        compiler_params=_params(semantics, block_bytes, temp_bytes),
    )(*[a for a, _ in ins], *deps)
    return res


def _sds(shape, dtype):
    return jax.ShapeDtypeStruct(shape, dtype)


def _sigmoid(x):
    return jax.nn.sigmoid(x)


def _exchange(name, arrays, gather, deps=()):
    n = len(arrays)
    out_shapes = [((N_DEV,) + a.shape) if gather else a.shape for a in arrays]

    def body(*refs):
        srcs, dsts = refs[:n], refs[n + len(deps):2 * n + len(deps)]
        send_sems, recv_sems, local_sems = refs[2 * n + len(deps):]
        x, y, c = lax.axis_index("x"), lax.axis_index("y"), lax.axis_index("c")
        me = 4 * x + 2 * y + c
        copies = []
        for w in range(n):
            own = srcs[w] if gather else srcs[w].at[me]
            local = pltpu.make_async_copy(own, dsts[w].at[me], local_sems.at[w])
            local.start()
            copies.append(local)
            for k in range(1, N_DEV):
                px = (1 - x) if (k & 4) else x
                py = (1 - y) if (k & 2) else y
                pc = (1 - c) if (k & 1) else c
                peer = 4 * px + 2 * py + pc
                cp = pltpu.make_async_remote_copy(
                    src_ref=srcs[w] if gather else srcs[w].at[peer],
                    dst_ref=dsts[w].at[me],
                    send_sem=send_sems.at[w * (N_DEV - 1) + k - 1],
                    recv_sem=recv_sems.at[w * (N_DEV - 1) + k - 1],
                    device_id=(px, py, pc), device_id_type=MESH)
                cp.start()
                copies.append(cp)
        for cp in copies:
            cp.wait()

    hbm = pl.BlockSpec(memory_space=pltpu.HBM)
    return pl.pallas_call(
        body, name=name,
        in_specs=[hbm] * n + [pl.BlockSpec(memory_space=pl.ANY)] * len(deps), out_specs=[hbm] * n,
        out_shape=[_sds(s, a.dtype) for s, a in zip(out_shapes, arrays)],
        scratch_shapes=[pltpu.SemaphoreType.DMA((n * (N_DEV - 1),)),
                        pltpu.SemaphoreType.DMA((n * (N_DEV - 1),)),
                        pltpu.SemaphoreType.DMA((n,))],
    )(*arrays, *deps)


_HBM = pl.BlockSpec(memory_space=pltpu.HBM)
_SEM = pl.BlockSpec(memory_space=pltpu.SEMAPHORE)
_ANY = pl.BlockSpec(memory_space=pl.ANY)
_EFFECT = pltpu.SideEffectType.DATAFLOW_SIDE_EFFECTING
N_TARGETS = 4


def _mesh_pos():
    return lax.axis_index("x"), lax.axis_index("y"), lax.axis_index("c")


def _chip_peers(x, y, c):
    return [(1 - x, y, c), (x, 1 - y, c), (1 - x, 1 - y, c)]


def _dev_index(pos):
    return 4 * pos[0] + 2 * pos[1] + pos[2]


def _hbm_like(a):
    return pltpu.HBM(a.shape, a.dtype)


def _place_shard(name, w, out_dtype, me, tr, deps=()):
    r, c = w.shape
    n_dep = len(deps)

    def body(me_ref, w_ref, *rest):
        rest[n_dep][...] = w_ref[...].astype(out_dtype)

    grid_spec = pltpu.PrefetchScalarGridSpec(
        num_scalar_prefetch=1, grid=(r // tr,),
        in_specs=[pl.BlockSpec((tr, c), lambda i, me_ref: (i, 0))] + [_ANY] * n_dep,
        out_specs=pl.BlockSpec((None, tr, c), lambda i, me_ref: (me_ref[0], i, 0)))
    return pl.pallas_call(
        body, name=name, grid_spec=grid_spec, out_shape=_sds((N_DEV, r, c), out_dtype),
        compiler_params=_params(("parallel",), tr * c * 6, tr * c * 4),
    )(me, w, *deps)


def _gather_start(name, lands):
    n = len(lands)

    def body(*refs):
        bufs = refs[:n]
        send, recv = refs[n], refs[n + 1]
        token = refs[-1]
        x, y, c = _mesh_pos()
        me = _dev_index((x, y, c))
        targets = [(x, y, 1 - c)] + _chip_peers(x, y, c)
        for w in range(n):
            for k, to in enumerate(targets):
                pltpu.make_async_remote_copy(
                    src_ref=bufs[w].at[me], dst_ref=bufs[w].at[me],
                    send_sem=send.at[N_TARGETS * w + k], recv_sem=recv.at[N_TARGETS * w + k],
                    device_id=to, device_id_type=MESH).start()
        token[...] = jnp.zeros_like(token)

    sems = pltpu.SemaphoreType.DMA((N_TARGETS * n,))
    outs = pl.pallas_call(
        body, name=name,
        in_specs=[_HBM] * n, out_specs=[_SEM, _SEM] + [_HBM] * n + [_token_spec()],
        out_shape=[sems, sems] + [_hbm_like(a) for a in lands] + [_sds((8, LANES), F32)],
        input_output_aliases={i: 2 + i for i in range(n)},
        compiler_params=pltpu.CompilerParams(has_side_effects=_EFFECT),
    )(*lands)
    return outs[0], outs[1], list(outs[2:2 + n]), outs[-1]


def _gather_wait(name, positions, send, recv, lands, after):
    m = len(positions)

    def body(*refs):
        bufs = refs[:m]
        send_sems, recv_sems = refs[m], refs[m + 1]
        x, y, c = _mesh_pos()
        me = _dev_index((x, y, c))
        sources = [(x, y, 1 - c)] + _chip_peers(x, y, c)
        for j, w in enumerate(positions):
            for k, frm in enumerate(sources):
                cp = pltpu.make_async_remote_copy(
                    src_ref=bufs[j].at[me], dst_ref=bufs[j].at[_dev_index(frm)],
                    send_sem=send_sems.at[N_TARGETS * w + k], recv_sem=recv_sems.at[N_TARGETS * w + k],
                    device_id=frm, device_id_type=MESH)
                cp.wait_send()
                cp.wait_recv()

    outs = pl.pallas_call(
        body, name=name,
        in_specs=[_HBM] * m + [_SEM, _SEM, _ANY], out_specs=[_HBM] * m,
        out_shape=[_hbm_like(a) for a in lands],
        input_output_aliases={i: i for i in range(m)},
        compiler_params=pltpu.CompilerParams(has_side_effects=_EFFECT),
    )(*lands, send, recv, after)
    return list(outs)


def _forward_to_sibling(name, lands):
    m = len(lands)

    def body(*refs):
        bufs = refs[m:2 * m]
        send_sems, recv_sems = refs[2 * m], refs[2 * m + 1]
        x, y, c = _mesh_pos()
        copies = []
        for j in range(m):
            for k, chip in enumerate(_chip_peers(x, y, c)):
                block = bufs[j].at[_dev_index(chip)]
                cp = pltpu.make_async_remote_copy(
                    src_ref=block, dst_ref=block,
                    send_sem=send_sems.at[3 * j + k], recv_sem=recv_sems.at[3 * j + k],
                    device_id=(x, y, 1 - c), device_id_type=MESH)
                cp.start()
                copies.append(cp)
        for cp in copies:
            cp.wait()

    outs = pl.pallas_call(
        body, name=name,
        in_specs=[_HBM] * m, out_specs=[_HBM] * m,
        out_shape=[_sds(a.shape, a.dtype) for a in lands],
        input_output_aliases={i: i for i in range(m)},
        scratch_shapes=[pltpu.SemaphoreType.DMA((3 * m,)), pltpu.SemaphoreType.DMA((3 * m,))],
    )(*lands)
    return list(outs)


def _token_spec():
    return pl.BlockSpec(memory_space=pltpu.VMEM)


def _pair_start(name, stacks, lands):
    n = len(stacks)

    def body(*refs):
        srcs, dsts = refs[:n], refs[n:2 * n]
        send, recv = refs[2 * n], refs[2 * n + 1]
        token = refs[-1]
        x, y, c = _mesh_pos()
        for w in range(n):
            for chip in range(4):
                pltpu.make_async_remote_copy(
                    src_ref=srcs[w].at[chip, 1 - c], dst_ref=dsts[w].at[chip],
                    send_sem=send.at[4 * w + chip], recv_sem=recv.at[4 * w + chip],
                    device_id=(x, y, 1 - c), device_id_type=MESH).start()
        token[...] = jnp.zeros_like(token)

    sems = pltpu.SemaphoreType.DMA((4 * n,))
    outs = pl.pallas_call(
        body, name=name,
        in_specs=[_HBM] * (2 * n), out_specs=[_SEM, _SEM] + [_HBM] * (2 * n) + [_token_spec()],
        out_shape=[sems, sems] + [_hbm_like(a) for a in stacks] + [_hbm_like(a) for a in lands] + [_sds((8, LANES), F32)],
        input_output_aliases={i: 2 + i for i in range(2 * n)},
        compiler_params=pltpu.CompilerParams(has_side_effects=_EFFECT),
    )(*stacks, *lands)
    return outs[0], outs[1], list(outs[2:2 + n]), list(outs[2 + n:2 + 2 * n]), outs[-1]


def _pair_wait(name, send, recv, stacks, lands, after):
    n = len(stacks)

    def body(*refs):
        srcs, dsts = refs[:n], refs[n:2 * n]
        send_sems, recv_sems = refs[2 * n], refs[2 * n + 1]
        x, y, c = _mesh_pos()
        for w in range(n):
            for chip in range(4):
                cp = pltpu.make_async_remote_copy(
                    src_ref=srcs[w].at[chip, 1 - c], dst_ref=dsts[w].at[chip],
                    send_sem=send_sems.at[4 * w + chip], recv_sem=recv_sems.at[4 * w + chip],
                    device_id=(x, y, 1 - c), device_id_type=MESH)
                cp.wait_send()
                cp.wait_recv()

    outs = pl.pallas_call(
        body, name=name,
        in_specs=[_HBM] * (2 * n) + [_SEM, _SEM, _ANY], out_specs=[_HBM] * (2 * n),
        out_shape=[_hbm_like(a) for a in stacks] + [_hbm_like(a) for a in lands],
        input_output_aliases={i: i for i in range(2 * n)},
        compiler_params=pltpu.CompilerParams(has_side_effects=_EFFECT),
    )(*stacks, *lands, send, recv, after)
    return list(outs[:n]), list(outs[n:])


def _pair_add(name, stack, land, core, tr):
    _, _, r, c = stack.shape

    def body(core_ref, a_ref, b_ref, o_ref):
        o_ref[...] = (a_ref[...].astype(F32) + b_ref[...].astype(F32)).astype(BF16)

    grid_spec = pltpu.PrefetchScalarGridSpec(
        num_scalar_prefetch=1, grid=(4, r // tr),
        in_specs=[pl.BlockSpec((None, None, tr, c), lambda k, i, core_ref: (k, core_ref[0], i, 0)),
                  pl.BlockSpec((None, tr, c), lambda k, i, core_ref: (k, i, 0))],
        out_specs=pl.BlockSpec((None, tr, c), lambda k, i, core_ref: (k, i, 0)))
    return pl.pallas_call(
        body, name=name, grid_spec=grid_spec, out_shape=_sds((4, r, c), BF16),
        compiler_params=_params(("parallel", "parallel"), 3 * tr * c * 2, 3 * tr * c * 4),
    )(core, stack, land)


def _chip_start(name, parts, lands):
    n = len(parts)

    def body(*refs):
        srcs, dsts = refs[:n], refs[n:2 * n]
        send, recv = refs[2 * n], refs[2 * n + 1]
        token = refs[-1]
        x, y, c = _mesh_pos()
        for w in range(n):
            for k, to in enumerate(_chip_peers(x, y, c)):
                pltpu.make_async_remote_copy(
                    src_ref=srcs[w].at[2 * to[0] + to[1]], dst_ref=dsts[w].at[2 * x + y],
                    send_sem=send.at[3 * w + k], recv_sem=recv.at[3 * w + k],
                    device_id=to, device_id_type=MESH).start()
        token[...] = jnp.zeros_like(token)

    sems = pltpu.SemaphoreType.DMA((3 * n,))
    outs = pl.pallas_call(
        body, name=name,
        in_specs=[_HBM] * (2 * n), out_specs=[_SEM, _SEM] + [_HBM] * (2 * n) + [_token_spec()],
        out_shape=[sems, sems] + [_hbm_like(a) for a in parts] + [_hbm_like(a) for a in lands] + [_sds((8, LANES), F32)],
        input_output_aliases={i: 2 + i for i in range(2 * n)},
        compiler_params=pltpu.CompilerParams(has_side_effects=_EFFECT),
    )(*parts, *lands)
    return outs[0], outs[1], list(outs[2:2 + n]), list(outs[2 + n:2 + 2 * n]), outs[-1]


def _chip_wait(name, send, recv, parts, lands, after):
    n = len(parts)

    def body(*refs):
        srcs, dsts = refs[:n], refs[n:2 * n]
        send_sems, recv_sems = refs[2 * n], refs[2 * n + 1]
        x, y, c = _mesh_pos()
        for w in range(n):
            for k, frm in enumerate(_chip_peers(x, y, c)):
                chip = 2 * frm[0] + frm[1]
                cp = pltpu.make_async_remote_copy(
                    src_ref=srcs[w].at[chip], dst_ref=dsts[w].at[chip],
                    send_sem=send_sems.at[3 * w + k], recv_sem=recv_sems.at[3 * w + k],
                    device_id=frm, device_id_type=MESH)
                cp.wait_send()
                cp.wait_recv()

    outs = pl.pallas_call(
        body, name=name,
        in_specs=[_HBM] * (2 * n) + [_SEM, _SEM, _ANY], out_specs=[_HBM] * (2 * n),
        out_shape=[_hbm_like(a) for a in parts] + [_hbm_like(a) for a in lands],
        input_output_aliases={i: i for i in range(2 * n)},
        compiler_params=pltpu.CompilerParams(has_side_effects=_EFFECT),
    )(*parts, *lands, send, recv, after)
    return list(outs[:n]), list(outs[n:])


def _row_tile(t):
    return min(t, 256)


def _rms_fwd(name, x, g):
    t, d = x.shape
    tm = _row_tile(t)

    def epilogue(_, ins, outs):
        xv = ins[0][...]
        r = lax.rsqrt(jnp.mean(xv * xv, axis=-1, keepdims=True) + RMS_EPS)
        outs[0][...] = (xv * r * ins[1][...]).astype(BF16)

    row = pl.BlockSpec((tm, d), lambda i, j, k: (i, 0))
    vec = pl.BlockSpec((1, d), lambda i, j, k: (0, 0))
    return _fused(name, (t // tm, 1, 1), [(x, row), (g, vec)], [(_sds((t, d), BF16), row)], [], epilogue,
                  temp_bytes=4 * tm * d * 4)[0]


def _rms_bwd(name, x, g, dh, resid, deps=()):
    t, d = x.shape
    tm = _row_tile(t)

    def epilogue(_, ins, outs):
        xv, gv, dhv = ins[0][...], ins[1][...], ins[2][...]
        r = lax.rsqrt(jnp.mean(xv * xv, axis=-1, keepdims=True) + RMS_EPS)
        xh = xv * r
        u = dhv * gv
        dot = jnp.mean(u * xh, axis=-1, keepdims=True)
        outs[0][...] = ins[3][...] + r * (u - xh * dot)

        @pl.when(pl.program_id(0) == 0)
        def _():
            outs[1][...] = jnp.zeros_like(outs[1])

        outs[1][0:1, :] += jnp.sum(dhv * xh, axis=0, keepdims=True)

    row = pl.BlockSpec((tm, d), lambda i, j, k: (i, 0))
    vec = pl.BlockSpec((1, d), lambda i, j, k: (0, 0))
    acc = pl.BlockSpec((8, d), lambda i, j, k: (0, 0))
    return _fused(name, (t // tm, 1, 1), [(x, row), (g, vec), (dh, row), (resid, row)],
                  [(_sds((t, d), F32), row), (_sds((8, d), F32), acc)], [], epilogue,
                  temp_bytes=6 * tm * d * 4, semantics=("arbitrary", "arbitrary", "arbitrary"), deps=deps)


def _loss_dy(y, target):
    t, d = y.shape
    tm = _row_tile(t)

    def epilogue(_, ins, outs):
        e = ins[0][...] - ins[1][...]
        outs[0][...] = e * (1.0 / d)

        @pl.when(pl.program_id(0) == 0)
        def _():
            outs[1][...] = jnp.zeros_like(outs[1])

        part = jnp.sum(jnp.sum(e * e, axis=1, keepdims=True), axis=0, keepdims=True)
        outs[1][...] += jnp.broadcast_to(part, outs[1].shape)

    row = pl.BlockSpec((tm, d), lambda i, j, k: (i, 0))
    acc = pl.BlockSpec((8, LANES), lambda i, j, k: (0, 0))
    return _fused("loss_dy", (t // tm, 1, 1), [(y, row), (target, row)],
                  [(_sds((t, d), F32), row), (_sds((8, LANES), F32), acc)], [], epilogue,
                  temp_bytes=3 * tm * d * 4, semantics=("arbitrary", "arbitrary", "arbitrary"))


def _ffn_up(name, h, wgu):
    t, d = h.shape
    nb = wgu.shape[2]
    f = 4 * nb
    tm = min(t, 512)

    def body(h_ref, wg_ref, wu_ref, gu_ref, a_ref):
        hv = h_ref[...]
        for c0 in range(0, nb, MXU_COLS):
            cs = slice(c0, min(c0 + MXU_COLS, nb))
            g = jnp.dot(hv, wg_ref[:, cs], preferred_element_type=F32)
            u = jnp.dot(hv, wu_ref[:, cs], preferred_element_type=F32)
            gu_ref[0, :, cs] = g.astype(BF16)
            gu_ref[1, :, cs] = u.astype(BF16)
            a_ref[:, cs] = (g * _sigmoid(g) * u).astype(BF16)

    blocks = tm * d * 2 + 2 * d * nb * 2 + 3 * tm * nb * 2
    return pl.pallas_call(
        body, name=name, grid=(4, t // tm),
        in_specs=[pl.BlockSpec((tm, d), lambda j, i: (i, 0)),
                  pl.BlockSpec((None, d, nb), lambda j, i: (j, 0, 0)),
                  pl.BlockSpec((None, d, nb), lambda j, i: (j + 4, 0, 0))],
        out_specs=[pl.BlockSpec((2, tm, nb), lambda j, i: (0, i, j)),
                   pl.BlockSpec((tm, nb), lambda j, i: (i, j))],
        out_shape=[_sds((2, t, f), BF16), _sds((t, f), BF16)],
        compiler_params=_params(("parallel", "parallel"), blocks, 8 * tm * MXU_COLS * 4),
    )(h, wgu, wgu)


def _ffn_down(name, a, wd, x):
    t, f = a.shape
    d = wd.shape[1]
    tm = min(t, 512)
    tn = min(d, 1024)

    def epilogue(acc, ins, outs):
        outs[0][...] = ins[2][...] + 0.5 * acc

    blk = pl.BlockSpec((tm, tn), lambda j, i, k: (i, j))
    return _fused(name, (d // tn, t // tm, 1),
                  [(a, pl.BlockSpec((tm, f), lambda j, i, k: (i, 0))),
                   (wd, pl.BlockSpec((f, tn), lambda j, i, k: (0, j))),
                   (x, blk)],
                  [(_sds((t, d), F32), blk)],
                  [(0, 1, NN)], epilogue, temp_bytes=2 * tm * tn * 4)[0]


def _ffn_bwd_act(name, dy, wd, gu, deps=()):
    t, d = dy.shape
    f = wd.shape[0]
    nb = f // 4
    tm = min(t, 512)

    def body(dy_ref, wd_ref, gu_ref, *rest):
        dgu_ref = rest[-1]
        dyv = dy_ref[...].astype(BF16)
        for c0 in range(0, nb, MXU_COLS):
            cs = slice(c0, min(c0 + MXU_COLS, nb))
            da = 0.5 * lax.dot_general(dyv, wd_ref[cs, :], NT, preferred_element_type=F32)
            g = gu_ref[0, :, cs].astype(F32)
            u = gu_ref[1, :, cs].astype(F32)
            s = _sigmoid(g)
            dgu_ref[0, :, cs] = (da * u * (s * (1.0 + g * (1.0 - s)))).astype(BF16)
            dgu_ref[1, :, cs] = (da * (g * s)).astype(BF16)

    blocks = tm * d * 4 + nb * d * 2 + 4 * tm * nb * 2
    return pl.pallas_call(
        body, name=name, grid=(4, t // tm),
        in_specs=[pl.BlockSpec((tm, d), lambda j, i: (i, 0)),
                  pl.BlockSpec((nb, d), lambda j, i: (j, 0)),
                  pl.BlockSpec((2, tm, nb), lambda j, i: (0, i, j))] + [_ANY] * len(deps),
        out_specs=pl.BlockSpec((2, tm, nb), lambda j, i: (0, i, j)),
        out_shape=_sds((2, t, f), BF16),
        compiler_params=_params(("parallel", "parallel"), blocks, tm * d * 2 + 8 * tm * MXU_COLS * 4),
    )(dy, wd, gu, *deps)


def _ffn_bwd_dwd(name, gu, dy):
    _, t, f = gu.shape
    d = dy.shape[1]
    tm = f // 4
    tn = min(d, 512)
    tc = min(t, 512)

    def body(gu_ref, dy_ref, o_ref, a_ref):
        @pl.when(pl.program_id(1) == 0)
        def _():
            for r0 in range(0, t, tc):
                rs = slice(r0, r0 + tc)
                g = gu_ref[0, rs, :].astype(F32)
                a_ref[rs, :] = (g * _sigmoid(g) * gu_ref[1, rs, :].astype(F32)).astype(BF16)

        acc = lax.dot_general(a_ref[...], dy_ref[...].astype(BF16), TN, preferred_element_type=F32)
        o_ref[...] = (0.5 * acc).astype(BF16)

    blocks = 2 * t * tm * 2 + t * tn * 4 + tm * tn * 2
    return pl.pallas_call(
        body, name=name, grid=(4, d // tn),
        in_specs=[pl.BlockSpec((2, t, tm), lambda i, j: (0, 0, i)),
                  pl.BlockSpec((t, tn), lambda i, j: (0, j))],
        out_specs=pl.BlockSpec((tm, tn), lambda i, j: (i, j)),
        out_shape=_sds((f, d), BF16),
        scratch_shapes=[pltpu.VMEM((t, tm), BF16)],
        compiler_params=_params(("parallel", "arbitrary"), blocks, t * tm * 2 + t * tn * 2 + 2 * tm * tn * 4 + 4 * tc * tm * 4),
    )(gu, dy)


def _ffn_bwd_dh(name, dgu, wgu, deps=()):
    _, t, f = dgu.shape
    d, nb = wgu.shape[1], wgu.shape[2]
    tm = min(t, 512)
    tn = min(d, MXU_COLS)

    def products(ins):
        total = None
        for j in range(N_DEV):
            part = lax.dot_general(ins[0][j // 4, :, (j % 4) * nb:(j % 4 + 1) * nb], ins[1][j], NT,
                                   preferred_element_type=F32)
            total = part if total is None else total + part
        return total

    def epilogue(acc, ins, outs):
        outs[0][...] = acc

    return _fused(name, (t // tm, d // tn, 1),
                  [(dgu, pl.BlockSpec((2, tm, f), lambda i, j, k: (0, i, 0))),
                   (wgu, pl.BlockSpec((N_DEV, tn, nb), lambda i, j, k: (0, j, 0)))],
                  [(_sds((t, d), F32), pl.BlockSpec((tm, tn), lambda i, j, k: (i, j)))],
                  products, epilogue, temp_bytes=4 * tm * tn * 4, deps=deps)[0]


def _ffn_bwd_dwgu(name, h, dgu, deps=()):
    t, d = h.shape
    nb = dgu.shape[2] // 4
    tm = min(d, 512)

    def epilogue(acc, ins, outs):
        outs[0][...] = acc.astype(BF16)

    return _fused(name, (N_DEV, d // tm, 1),
                  [(h, pl.BlockSpec((t, tm), lambda i, j, k: (0, j))),
                   (dgu, pl.BlockSpec((None, t, nb), lambda i, j, k: (i // 4, 0, i % 4)))],
                  [(_sds((N_DEV, d, nb), BF16), pl.BlockSpec((None, tm, nb), lambda i, j, k: (i, j, 0)))],
                  [(0, 1, TN)], epilogue, temp_bytes=2 * tm * nb * 4, deps=deps)[0]


def _proj(h, w_in):
    t, d = h.shape
    nb = w_in.shape[3]
    tm = min(t, 512)

    def body(h_ref, w_ref, o_ref):
        hv = h_ref[...]
        o_ref[:, 0:nb] = jnp.dot(hv, w_ref[0], preferred_element_type=F32)
        o_ref[:, nb:2 * nb] = jnp.dot(hv, w_ref[1], preferred_element_type=F32)

    blocks = tm * d * 2 + 2 * d * nb * 2 + tm * 2 * nb * 4
    return pl.pallas_call(
        body, name="mix_proj", grid=(4, t // tm),
        in_specs=[pl.BlockSpec((tm, d), lambda j, i: (i, 0)),
                  pl.BlockSpec((None, 2, d, nb), lambda j, i: (j, 0, 0, 0))],
        out_specs=pl.BlockSpec((tm, 2 * nb), lambda j, i: (i, j)),
        out_shape=_sds((t, N_DEV * nb), F32),
        compiler_params=_params(("parallel", "parallel"), blocks, 2 * tm * nb * 4),
    )(h, w_in)


def _shift_rows(u, k):
    t = u.shape[0]
    rolled = pltpu.roll(u, k % t, axis=0)
    row = lax.broadcasted_iota(jnp.int32, u.shape, 0)
    keep = (row >= k) if k > 0 else (row < t + k)
    return jnp.where(keep, rolled, 0.0)


def _conv_fwd(proj, conv_w):
    t = proj.shape[0]
    cw = conv_w.shape[1]
    tc = min(cw, 256)
    nc = cw // tc

    def epilogue(_, ins, outs):
        u = ins[2][...] * ins[0][...]
        w = ins[3][...]
        y = u * w[2:3, :] + _shift_rows(u, 1) * w[1:2, :] + _shift_rows(u, 2) * w[0:1, :]
        outs[0][...] = (ins[1][...] * y).astype(BF16)

    def col(seg):
        return pl.BlockSpec((t, tc), lambda i, j, k: (0, seg * nc + i))

    return _fused("conv_fwd", (nc, 1, 1),
                  [(proj, col(0)), (proj, col(1)), (proj, col(2)),
                   (conv_w, pl.BlockSpec((8, tc), lambda i, j, k: (0, i)))],
                  [(_sds((t, cw), BF16), pl.BlockSpec((t, tc), lambda i, j, k: (0, i)))],
                  [], epilogue, temp_bytes=6 * t * tc * 4)[0]


def _conv_bwd(proj, conv_w, dca, deps=()):
    t = proj.shape[0]
    cw = conv_w.shape[1]
    tc = min(cw, 256)
    nc = cw // tc

    def epilogue(_, ins, outs):
        xc, bg, cg, w, dc = ins[0][...], ins[1][...], ins[2][...], ins[3][...], ins[4][...]
        u = cg * xc
        u1, u2 = _shift_rows(u, 1), _shift_rows(u, 2)
        y = u * w[2:3, :] + u1 * w[1:2, :] + u2 * w[0:1, :]
        dconv = dc * bg
        du = dconv * w[2:3, :] + _shift_rows(dconv, -1) * w[1:2, :] + _shift_rows(dconv, -2) * w[0:1, :]
        outs[0][0] = (du * cg).astype(BF16)
        outs[0][1] = (dc * y).astype(BF16)
        outs[0][2] = (du * xc).astype(BF16)
        outs[1][...] = jnp.zeros_like(outs[1])
        outs[1][0:1, :] = jnp.sum(dconv * u2, axis=0, keepdims=True)
        outs[1][1:2, :] = jnp.sum(dconv * u1, axis=0, keepdims=True)
        outs[1][2:3, :] = jnp.sum(dconv * u, axis=0, keepdims=True)

    def col(seg):
        return pl.BlockSpec((t, tc), lambda i, j, k: (0, seg * nc + i))

    own = pl.BlockSpec((t, tc), lambda i, j, k: (0, i))
    wspec = pl.BlockSpec((8, tc), lambda i, j, k: (0, i))
    return _fused("conv_bwd", (nc, 1, 1),
                  [(proj, col(0)), (proj, col(1)), (proj, col(2)), (conv_w, wspec), (dca, own)],
                  [(_sds((3, t, cw), BF16), pl.BlockSpec((3, t, tc), lambda i, j, k: (0, 0, i))),
                   (_sds((8, cw), F32), wspec)],
                  [], epilogue, temp_bytes=10 * t * tc * 4, deps=deps)


def _split3(x):
    hi = x.astype(BF16)
    r1 = x - hi.astype(F32)
    mid = r1.astype(BF16)
    lo = (r1 - mid.astype(F32)).astype(BF16)
    return hi, mid, lo


def _head_selector(width):
    r = lax.broadcasted_iota(jnp.int32, (width, LANES), 0)
    c = lax.broadcasted_iota(jnp.int32, (width, LANES), 1)
    return (lax.shift_right_logical(r, 6) == c).astype(BF16)


def _head_sum(x, sel):
    return sum(jnp.dot(p, sel, preferred_element_type=F32) for p in _split3(x))


def _head_bcast(r, sel):
    return sum(lax.dot_general(p, sel, NT, preferred_element_type=F32) for p in _split3(r))


def _rope(x, c, sa, sb):
    n = x.shape[1]
    return x * c + pltpu.roll(x, n - ROT_DIM // 2, axis=1) * sa + pltpu.roll(x, ROT_DIM // 2, axis=1) * sb


def _rope_t(d, c, sa, sb):
    n = d.shape[1]
    return d * c + pltpu.roll(d * sa, ROT_DIM // 2, axis=1) + pltpu.roll(d * sb, n - ROT_DIM // 2, axis=1)


def _tile_lanes(tab, width):
    return tab if width == tab.shape[1] else jnp.tile(tab, (1, width // tab.shape[1]))


def _qk_prep(proj, gq, gk, rope_tabs, cw, kw):
    t = proj.shape[0]
    tm = _row_tile(t)

    def epilogue(_, ins, outs):
        c, sa, sb = ins[5][...], ins[6][...], ins[7][...]
        for src, gain, dst, width in ((0, 3, 0, cw), (1, 4, 1, kw)):
            xv = ins[src][...]
            sel = _head_selector(width)
            r = lax.rsqrt(_head_sum(xv * xv, sel) * (1.0 / HEAD_DIM) + RMS_EPS)
            xn = xv * _head_bcast(r, sel) * ins[gain][...]
            outs[dst][...] = _rope(xn, _tile_lanes(c, width), _tile_lanes(sa, width), _tile_lanes(sb, width)).astype(BF16)
        outs[2][...] = ins[2][...].astype(BF16)

    kblk = cw // kw
    tab = pl.BlockSpec((tm, LANES), lambda i, j, k: (i, 0))
    kspec = pl.BlockSpec((tm, kw), lambda i, j, k: (i, 0))
    return _fused("qk_prep", (t // tm, 1, 1),
                  [(proj, pl.BlockSpec((tm, cw), lambda i, j, k: (i, 3))),
                   (proj, pl.BlockSpec((tm, kw), lambda i, j, k: (i, 4 * kblk))),
                   (proj, pl.BlockSpec((tm, kw), lambda i, j, k: (i, 4 * kblk + 1))),
                   (gq, pl.BlockSpec((1, cw), lambda i, j, k: (0, 0))),
                   (gk, pl.BlockSpec((1, kw), lambda i, j, k: (0, 0))),
                   (rope_tabs[0], tab), (rope_tabs[1], tab), (rope_tabs[2], tab)],
                  [(_sds((t, cw), BF16), pl.BlockSpec((tm, cw), lambda i, j, k: (i, 0))),
                   (_sds((t, kw), BF16), kspec), (_sds((t, kw), BF16), kspec)],
                  [], epilogue, temp_bytes=12 * tm * cw * 4)


def _qk_prep_bwd(proj, gq, gk, rope_tabs, dq, dkc, dkp, dvc, dvp, cw, kw):
    t = proj.shape[0]
    tm = BLOCK
    nblk = t // tm

    def epilogue(_, ins, outs):
        c, sa, sb = ins[5][...], ins[6][...], ins[7][...]
        has_next = (pl.program_id(0) < nblk - 1).astype(F32)
        dk = ins[9][...] + has_next * ins[10][...]
        dv = ins[11][...] + has_next * ins[12][...]
        pieces = []
        for src, gain, dval, dst, width in ((0, 3, ins[8][...], 1, cw), (1, 4, dk, 2, kw)):
            xv, gv = ins[src][...], ins[gain][...]
            sel = _head_selector(width)
            r = _head_bcast(lax.rsqrt(_head_sum(xv * xv, sel) * (1.0 / HEAD_DIM) + RMS_EPS), sel)
            xh = xv * r
            dxn = _rope_t(dval, _tile_lanes(c, width), _tile_lanes(sa, width), _tile_lanes(sb, width))
            u = dxn * gv
            dot = _head_bcast(_head_sum(u * xh, sel), sel) * (1.0 / HEAD_DIM)
            pieces.append((r * (u - xh * dot)).astype(BF16))
            ri = lax.broadcasted_iota(jnp.int32, (width, LANES), 0)
            ci = lax.broadcasted_iota(jnp.int32, (width, LANES), 1)
            fold = (lax.bitwise_and(ri, HEAD_DIM - 1) == ci).astype(BF16)
            colsum = jnp.broadcast_to(jnp.sum(dxn * xh, axis=0, keepdims=True), (8, width))
            part = sum(jnp.dot(p, fold, preferred_element_type=F32) for p in _split3(colsum))

            @pl.when(pl.program_id(0) == 0)
            def _():
                outs[dst][...] = jnp.zeros_like(outs[dst])

            outs[dst][0:1, :] += part[0:1, :]
        outs[0][:, 0:cw] = pieces[0]
        outs[0][:, cw:cw + kw] = pieces[1]
        outs[0][:, cw + kw:cw + 2 * kw] = dv.astype(BF16)

    kblk = cw // kw
    tab = pl.BlockSpec((tm, LANES), lambda i, j, k: (i, 0))
    kcur = pl.BlockSpec((tm, kw), lambda i, j, k: (i, 0))
    knext = pl.BlockSpec((tm, kw), lambda i, j, k: (jnp.minimum(i + 1, nblk - 1), 0))
    acc = pl.BlockSpec((8, LANES), lambda i, j, k: (0, 0))
    return _fused("qk_prep_bwd", (nblk, 1, 1),
                  [(proj, pl.BlockSpec((tm, cw), lambda i, j, k: (i, 3))),
                   (proj, pl.BlockSpec((tm, kw), lambda i, j, k: (i, 4 * kblk))),
                   (proj, pl.BlockSpec((tm, kw), lambda i, j, k: (i, 4 * kblk + 1))),
                   (gq, pl.BlockSpec((1, cw), lambda i, j, k: (0, 0))),
                   (gk, pl.BlockSpec((1, kw), lambda i, j, k: (0, 0))),
                   (rope_tabs[0], tab), (rope_tabs[1], tab), (rope_tabs[2], tab),
                   (dq, pl.BlockSpec((tm, cw), lambda i, j, k: (i, 0))),
                   (dkc, kcur), (dkp, knext), (dvc, kcur), (dvp, knext)],
                  [(_sds((t, cw + 2 * kw), BF16), pl.BlockSpec((tm, cw + 2 * kw), lambda i, j, k: (i, 0))),
                   (_sds((8, LANES), F32), acc), (_sds((8, LANES), F32), acc)],
                  [], epilogue, temp_bytes=16 * tm * cw * 4, semantics=("arbitrary", "arbitrary", "arbitrary"))


def _attn_mask(n):
    row = lax.broadcasted_iota(jnp.int32, (GROUP * BLOCK, 2 * BLOCK), 0)
    row = lax.bitwise_and(row, BLOCK - 1)
    col = lax.broadcasted_iota(jnp.int32, (GROUP * BLOCK, 2 * BLOCK), 1)
    return (col > row) & (col <= row + BLOCK) & ((col >= BLOCK) | (n > 0))


def _stack_heads(x, h):
    return jnp.concatenate([x[:, (h * GROUP + g) * HEAD_DIM:(h * GROUP + g + 1) * HEAD_DIM] for g in range(GROUP)], axis=0)


def _stack_sinks(sink_ref, h):
    return jnp.concatenate([jnp.broadcast_to(sink_ref[h * GROUP + g:h * GROUP + g + 1, 0:1], (BLOCK, 1))
                            for g in range(GROUP)], axis=0)


def _softmax_with_sink(q, k2, sink, valid):
    s = lax.dot_general(q, k2, NT, preferred_element_type=F32) * ATTN_SCALE
    s = jnp.where(valid, s, NEG_INF)
    m = jnp.maximum(jnp.max(s, axis=-1, keepdims=True), sink)
    p = jnp.exp(s - m)
    es = jnp.exp(sink - m)
    denom = jnp.sum(p, axis=-1, keepdims=True) + es
    return p / denom, es / denom


def _attn_fwd(qn, kn, vb, sink_rows):
    t, cw = qn.shape
    kw = kn.shape[1]
    nkv = kw // HEAD_DIM

    def body(q_ref, kp_ref, kc_ref, vp_ref, vc_ref, sink_ref, o_ref):
        valid = _attn_mask(pl.program_id(0))
        qv = q_ref[...]
        kp, kc, vp, vc = kp_ref[...], kc_ref[...], vp_ref[...], vc_ref[...]
        outs = []
        for h in range(nkv):
            hs = slice(h * HEAD_DIM, (h + 1) * HEAD_DIM)
            k2 = jnp.concatenate([kp[:, hs], kc[:, hs]], axis=0)
            v2 = jnp.concatenate([vp[:, hs], vc[:, hs]], axis=0)
            pn, _ = _softmax_with_sink(_stack_heads(qv, h), k2, _stack_sinks(sink_ref, h), valid)
            o4 = jnp.dot(pn.astype(BF16), v2, preferred_element_type=F32)
            outs += [o4[g * BLOCK:(g + 1) * BLOCK] for g in range(GROUP)]
        o_ref[...] = jnp.concatenate(outs, axis=-1).astype(BF16)

    cur = lambda n: (n, 0)
    prev = lambda n: (jnp.maximum(n - 1, 0), 0)
    return pl.pallas_call(
        body, name="attn_fwd", grid=(t // BLOCK,),
        in_specs=[pl.BlockSpec((BLOCK, cw), cur),
                  pl.BlockSpec((BLOCK, kw), prev), pl.BlockSpec((BLOCK, kw), cur),
                  pl.BlockSpec((BLOCK, kw), prev), pl.BlockSpec((BLOCK, kw), cur),
                  pl.BlockSpec(sink_rows.shape, lambda n: (0, 0))],
        out_specs=pl.BlockSpec((BLOCK, cw), cur),
        out_shape=_sds((t, cw), BF16),
        compiler_params=_params(("parallel",), BLOCK * (cw + 4 * kw) * 2 + BLOCK * cw * 2, 8 << 20),
    )(qn, kn, kn, vb, vb, sink_rows)


def _attn_bwd(qn, kn, vb, sink_rows, do):
    t, cw = qn.shape
    kw = kn.shape[1]
    nkv = kw // HEAD_DIM
    nq = nkv * GROUP

    def body(q_ref, kp_ref, kc_ref, vp_ref, vc_ref, sink_ref, do_ref,
             dq_ref, dkc_ref, dkp_ref, dvc_ref, dvp_ref, dsink_ref):
        n = pl.program_id(0)
        valid = _attn_mask(n)
        qv, dov = q_ref[...], do_ref[...]
        kp, kc, vp, vc = kp_ref[...], kc_ref[...], vp_ref[...], vc_ref[...]
        dqs, dks, dvs, dsinks = [], [], [], []
        for h in range(nkv):
            hs = slice(h * HEAD_DIM, (h + 1) * HEAD_DIM)
            k2 = jnp.concatenate([kp[:, hs], kc[:, hs]], axis=0)
            v2 = jnp.concatenate([vp[:, hs], vc[:, hs]], axis=0)
            q4 = _stack_heads(qv, h)
            pn, psink = _softmax_with_sink(q4, k2, _stack_sinks(sink_ref, h), valid)
            dob = _stack_heads(dov, h).astype(BF16)
            dpn = lax.dot_general(dob, v2, NT, preferred_element_type=F32)
            dvs.append(lax.dot_general(pn.astype(BF16), dob, TN, preferred_element_type=F32))
            delta = jnp.sum(pn * dpn, axis=-1, keepdims=True)
            ds = (pn * (dpn - delta) * ATTN_SCALE).astype(BF16)
            dq4 = jnp.dot(ds, k2, preferred_element_type=F32)
            dks.append(lax.dot_general(ds, q4, TN, preferred_element_type=F32))
            dsink4 = -psink * delta
            for g in range(GROUP):
                rows = slice(g * BLOCK, (g + 1) * BLOCK)
                dqs.append(dq4[rows])
                dsinks.append(jnp.broadcast_to(jnp.sum(dsink4[rows], axis=0, keepdims=True), (1, LANES)))
        dq_ref[...] = jnp.concatenate(dqs, axis=-1)
        dkp_ref[...] = jnp.concatenate([d[:BLOCK] for d in dks], axis=-1)
        dkc_ref[...] = jnp.concatenate([d[BLOCK:] for d in dks], axis=-1)
        dvp_ref[...] = jnp.concatenate([d[:BLOCK] for d in dvs], axis=-1)
        dvc_ref[...] = jnp.concatenate([d[BLOCK:] for d in dvs], axis=-1)

        @pl.when(n == 0)
        def _():
            dsink_ref[...] = jnp.zeros_like(dsink_ref)

        dsink_ref[...] += jnp.concatenate(dsinks, axis=0)

    cur = lambda n: (n, 0)
    prev = lambda n: (jnp.maximum(n - 1, 0), 0)
    kspec = pl.BlockSpec((BLOCK, kw), cur)
    return pl.pallas_call(
        body, name="attn_bwd", grid=(t // BLOCK,),
        in_specs=[pl.BlockSpec((BLOCK, cw), cur),
                  pl.BlockSpec((BLOCK, kw), prev), kspec,
                  pl.BlockSpec((BLOCK, kw), prev), kspec,
                  pl.BlockSpec(sink_rows.shape, lambda n: (0, 0)),
                  pl.BlockSpec((BLOCK, cw), cur)],
        out_specs=[pl.BlockSpec((BLOCK, cw), cur), kspec, kspec, kspec, kspec,
                   pl.BlockSpec((nq, LANES), lambda n: (0, 0))],
        out_shape=[_sds((t, cw), F32)] + [_sds((t, kw), F32)] * 4 + [_sds((nq, LANES), F32)],
        compiler_params=_params(("arbitrary",), BLOCK * (cw + 4 * kw) * 2 + 2 * BLOCK * cw * 4 + 4 * BLOCK * kw * 4, 12 << 20),
    )(qn, kn, kn, vb, vb, sink_rows, do)


def _mix_out(ca, o, woc, woa, proj):
    t, cw = ca.shape
    nb = woc.shape[2]
    d = N_DEV * nb
    tm = min(t, 1024)
    ga0 = (3 * cw + cw + 2 * (cw // 4)) // nb

    def body(ca_ref, o_ref, woc_ref, woa_ref, ga_ref, gb_ref, m_ref, ya_ref, yb_ref):
        ya = jnp.dot(ca_ref[...], woc_ref[...], preferred_element_type=F32)
        yb = jnp.dot(o_ref[...], woa_ref[...], preferred_element_type=F32)
        ya_ref[...] = ya.astype(BF16)
        yb_ref[...] = yb.astype(BF16)
        m_ref[...] = (_sigmoid(ga_ref[...]) * ya + _sigmoid(gb_ref[...]) * yb).astype(BF16)

    act = pl.BlockSpec((tm, cw), lambda i, j: (i, 0))
    wsp = pl.BlockSpec((None, cw, nb), lambda i, j: (j, 0, 0))
    osp = pl.BlockSpec((tm, nb), lambda i, j: (i, j))
    blocks = 2 * tm * cw * 2 + 2 * cw * nb * 2 + 2 * tm * nb * 4 + 3 * tm * nb * 2
    return pl.pallas_call(
        body, name="mix_out", grid=(t // tm, N_DEV),
        in_specs=[act, act, wsp, wsp,
                  pl.BlockSpec((tm, nb), lambda i, j: (i, ga0 + j)),
                  pl.BlockSpec((tm, nb), lambda i, j: (i, ga0 + N_DEV + j))],
        out_specs=[osp, osp, osp],
        out_shape=[_sds((t, d), BF16)] * 3,
        compiler_params=_params(("parallel", "parallel"), blocks, 6 * tm * nb * 4),
    )(ca, o, woc, woa, proj, proj)


def _mix_residual(merged, wo, x):
    t, d = x.shape
    tm = min(t, 512)

    def epilogue(acc, ins, outs):
        outs[0][...] = ins[2][...] + acc

    row = pl.BlockSpec((tm, d), lambda i, j, k: (i, 0))
    return _fused("mix_residual", (t // tm, 1, 1),
                  [(merged, row), (wo, pl.BlockSpec((d, d), lambda i, j, k: (0, 0))), (x, row)],
                  [(_sds((t, d), F32), row)], [(0, 1, NN)], epilogue, temp_bytes=2 * tm * d * 4)[0]


def _mix_bwd_gates(dx, wo, ya, yb, proj, cw):
    t, d = dx.shape
    tm = min(t, 512)
    tn = min(d, 512)
    ga0 = (4 * cw + 2 * (cw // 4)) // tn

    def epilogue(acc, ins, outs):
        sa, sb = _sigmoid(ins[4][...]), _sigmoid(ins[5][...])
        outs[0][...] = (acc * sa).astype(BF16)
        outs[1][...] = (acc * sb).astype(BF16)
        outs[2][0] = (acc * ins[2][...].astype(F32) * sa * (1.0 - sa)).astype(BF16)
        outs[2][1] = (acc * ins[3][...].astype(F32) * sb * (1.0 - sb)).astype(BF16)

    blk = pl.BlockSpec((tm, tn), lambda i, j, k: (i, j))
    return _fused("mix_bwd_gates", (t // tm, d // tn, 1),
                  [(dx, pl.BlockSpec((tm, d), lambda i, j, k: (i, 0))),
                   (wo, pl.BlockSpec((tn, d), lambda i, j, k: (j, 0))),
                   (ya, blk), (yb, blk),
                   (proj, pl.BlockSpec((tm, tn), lambda i, j, k: (i, ga0 + j))),
                   (proj, pl.BlockSpec((tm, tn), lambda i, j, k: (i, ga0 + d // tn + j)))],
                  [(_sds((t, d), BF16), blk), (_sds((t, d), BF16), blk),
                   (_sds((2, t, d), BF16), pl.BlockSpec((2, tm, tn), lambda i, j, k: (0, i, j)))],
                  [(0, 1, NT)], epilogue, temp_bytes=8 * tm * tn * 4)


def _tn_matmul(name, a, b, tm, out_dtype=BF16):
    t, m = a.shape
    n = b.shape[1]
    tk = min(t, 512)

    def epilogue(acc, ins, outs):
        outs[0][...] = acc.astype(out_dtype)

    return _fused(name, (m // tm, 1, t // tk),
                  [(a, pl.BlockSpec((tk, tm), lambda i, j, k: (k, i))),
                   (b, pl.BlockSpec((tk, n), lambda i, j, k: (k, 0)))],
                  [(_sds((m, n), out_dtype), pl.BlockSpec((tm, n), lambda i, j, k: (i, 0)))],
                  [(0, 1, TN)], epilogue, nk=t // tk, acc_shape=(tm, n), temp_bytes=tm * n * 4)[0]


def _out_proj_bwd_act(name, dy, w, deps=()):
    t, d = dy.shape
    kdim, nb = w.shape[1], w.shape[2]
    tm = min(t, 1024)

    def epilogue(acc, ins, outs):
        outs[0][...] = acc

    return _fused(name, (t // tm, 1, N_DEV),
                  [(dy, pl.BlockSpec((tm, nb), lambda i, j, k: (i, k))),
                   (w, pl.BlockSpec((None, kdim, nb), lambda i, j, k: (k, 0, 0)))],
                  [(_sds((t, kdim), F32), pl.BlockSpec((tm, kdim), lambda i, j, k: (i, 0)))],
                  [(0, 1, NT)], epilogue, nk=N_DEV, acc_shape=(tm, kdim), temp_bytes=tm * kdim * 4, deps=deps)[0]


def _out_proj_bwd_w(name, act, dy, nb):
    t, kdim = act.shape
    tk = min(t, 1024)

    def epilogue(acc, ins, outs):
        outs[0][...] = acc.astype(BF16)

    return _fused(name, (N_DEV, 1, t // tk),
                  [(act, pl.BlockSpec((tk, kdim), lambda i, j, k: (k, 0))),
                   (dy, pl.BlockSpec((tk, nb), lambda i, j, k: (k, i)))],
                  [(_sds((N_DEV, kdim, nb), BF16), pl.BlockSpec((None, kdim, nb), lambda i, j, k: (i, 0, 0)))],
                  [(0, 1, TN)], epilogue, nk=t // tk, acc_shape=(kdim, nb), temp_bytes=kdim * nb * 4)[0]


def _proj_bwd_act(dproj, w_in, deps=()):
    t, n = dproj.shape
    d, nb = w_in.shape[2], w_in.shape[3]
    tm = min(t, 512)

    def epilogue(acc, ins, outs):
        outs[0][...] = acc

    def products(ins):
        return (lax.dot_general(ins[0][:, 0:nb], ins[1][0], NT, preferred_element_type=F32)
                + lax.dot_general(ins[0][:, nb:2 * nb], ins[1][1], NT, preferred_element_type=F32))

    return _fused("mix_bwd_dh", (t // tm, 1, 4),
                  [(dproj, pl.BlockSpec((tm, 2 * nb), lambda i, j, k: (i, k))),
                   (w_in, pl.BlockSpec((None, 2, d, nb), lambda i, j, k: (k, 0, 0, 0)))],
                  [(_sds((t, d), F32), pl.BlockSpec((tm, d), lambda i, j, k: (i, 0)))],
                  products, epilogue, nk=4, acc_shape=(tm, d), temp_bytes=tm * d * 4, deps=deps)[0]


def _proj_bwd_w(h, dproj):
    t, d = h.shape
    nb = dproj.shape[1] // N_DEV
    tk = min(t, 512)
    nk = t // tk

    def body(h_ref, dp_ref, o_ref, acc0, acc1):
        k = pl.program_id(1)

        @pl.when(k == 0)
        def _():
            acc0[...] = jnp.zeros_like(acc0)
            acc1[...] = jnp.zeros_like(acc1)

        hv = h_ref[...]
        acc0[...] += lax.dot_general(hv, dp_ref[:, 0:nb], TN, preferred_element_type=F32)
        acc1[...] += lax.dot_general(hv, dp_ref[:, nb:2 * nb], TN, preferred_element_type=F32)

        @pl.when(k == nk - 1)
        def _():
            o_ref[0] = acc0[...].astype(BF16)
            o_ref[1] = acc1[...].astype(BF16)

    blocks = tk * d * 2 + tk * 2 * nb * 2 + 2 * d * nb * 2
    return pl.pallas_call(
        body, name="mix_bwd_dwin", grid=(4, nk),
        in_specs=[pl.BlockSpec((tk, d), lambda j, k: (k, 0)),
                  pl.BlockSpec((tk, 2 * nb), lambda j, k: (k, j))],
        out_specs=pl.BlockSpec((None, 2, d, nb), lambda j, k: (j, 0, 0, 0)),
        out_shape=_sds((4, 2, d, nb), BF16),
        scratch_shapes=[pltpu.VMEM((d, nb), F32), pltpu.VMEM((d, nb), F32)],
        compiler_params=_params(("parallel", "arbitrary"), blocks, 2 * d * nb * 4),
    )(h, dproj)


def _adamw_math(w, g, m, v):
    m = ADAM_B1 * m + (1.0 - ADAM_B1) * g
    v = ADAM_B2 * v + (1.0 - ADAM_B2) * (g * g)
    m_hat = m / (1.0 - ADAM_B1 ** ADAM_STEP)
    v_hat = v / (1.0 - ADAM_B2 ** ADAM_STEP)
    delta = -ADAM_LR * (m_hat / (jnp.sqrt(v_hat) + ADAM_EPS) + ADAM_WD * w)
    return delta, m, v


def _adamw(name, parts, w, m, v, tr):
    r, c = w.shape

    def body(p_ref, w_ref, m_ref, v_ref, g_out, d_out, m_out, v_out):
        g = p_ref[0].astype(F32)
        for s in range(1, N_DEV):
            g = g + p_ref[s].astype(F32)
        delta, mn, vn = _adamw_math(w_ref[...], g, m_ref[...], v_ref[...])
        g_out[...] = g
        d_out[...] = delta
        m_out[...] = mn
        v_out[...] = vn

    blk = pl.BlockSpec((tr, c), lambda i: (i, 0))
    blocks = N_DEV * tr * c * parts.dtype.itemsize + 7 * tr * c * 4
    return pl.pallas_call(
        body, name=name, grid=(r // tr,),
        in_specs=[pl.BlockSpec((N_DEV, tr, c), lambda i: (0, i, 0)), blk, blk, blk],
        out_specs=[blk] * 4, out_shape=[_sds((r, c), F32)] * 4,
        compiler_params=_params(("parallel",), blocks, 6 * tr * c * 4),
    )(parts, w, m, v)


def _adamw_chips(name, chip, own, landed, w, m, v, tr):
    r, c = w.shape

    def body(chip_ref, own_ref, land_ref, w_ref, m_ref, v_ref, g_out, d_out, m_out, v_out):
        mine = own_ref[...].astype(F32)
        g = jnp.zeros((tr, c), F32)
        for k in range(4):
            g = g + jnp.where(chip_ref[0] == k, mine, land_ref[k].astype(F32))
        delta, mn, vn = _adamw_math(w_ref[...], g, m_ref[...], v_ref[...])
        g_out[...] = g
        d_out[...] = delta
        m_out[...] = mn
        v_out[...] = vn

    blk = pl.BlockSpec((tr, c), lambda i, chip_ref: (i, 0))
    grid_spec = pltpu.PrefetchScalarGridSpec(
        num_scalar_prefetch=1, grid=(r // tr,),
        in_specs=[pl.BlockSpec((None, tr, c), lambda i, chip_ref: (chip_ref[0], i, 0)),
                  pl.BlockSpec((4, tr, c), lambda i, chip_ref: (0, i, 0)), blk, blk, blk],
        out_specs=[blk] * 4)
    blocks = 5 * tr * c * 2 + 7 * tr * c * 4
    return pl.pallas_call(
        body, name=name, grid_spec=grid_spec, out_shape=[_sds((r, c), F32)] * 4,
        compiler_params=_params(("parallel",), blocks, 6 * tr * c * 4),
    )(chip, own, landed, w, m, v)


def _rope_tables(t):
    half = ROT_DIM // 2
    inv_freq = 1.0 / (ROPE_THETA ** (jnp.arange(0, ROT_DIM, 2, dtype=F32) / ROT_DIM))
    ang = jnp.arange(t, dtype=F32)[:, None] * inv_freq[None, :]
    cos, sin = jnp.cos(ang), jnp.sin(ang)
    ones = jnp.ones((t, HEAD_DIM - ROT_DIM), F32)
    zeros = jnp.zeros((t, HEAD_DIM - half), F32)
    c = jnp.concatenate([cos, cos, ones], axis=1)
    sa = jnp.concatenate([-sin, zeros], axis=1)
    sb = jnp.concatenate([jnp.zeros((t, half), F32), sin, jnp.zeros((t, HEAD_DIM - ROT_DIM), F32)], axis=1)
    return tuple(jnp.tile(a, (1, LANES // HEAD_DIM)) for a in (c, sa, sb))


def _pad_rows(a, rows=8):
    return jnp.pad(a, ((0, rows - a.shape[0]), (0, 0)))


def kernel(x, g_ffn1, w_gu1, w_down1, g_mix, w_in, conv_w, q_norm_g, k_norm_g, sinks, w_out_conv, w_out_attn, w_o, g_ffn2, w_gu2, w_down2, loss_target, m_g_ffn1, m_w_gu1, m_w_down1, m_g_mix, m_w_in, m_conv_w, m_q_norm_g, m_k_norm_g, m_sinks, m_w_out_conv, m_w_out_attn, m_w_o, m_g_ffn2, m_w_gu2, m_w_down2, v_g_ffn1, v_w_gu1, v_w_down1, v_g_mix, v_w_in, v_conv_w, v_q_norm_g, v_k_norm_g, v_sinks, v_w_out_conv, v_w_out_attn, v_w_o, v_g_ffn2, v_w_gu2, v_w_down2):
    t, d = x.shape[1], x.shape[2]
    cw = d // 2
    kw = cw // GROUP
    nq = cw // HEAD_DIM
    xs, target = x.reshape(t, d), loss_target.reshape(t, d)
    me = 4 * lax.axis_index("x") + 2 * lax.axis_index("y") + lax.axis_index("c")

    big = {"w_gu1": w_gu1, "w_down1": w_down1, "w_in": w_in, "w_out_conv": w_out_conv,
           "w_out_attn": w_out_attn, "w_o": w_o, "w_gu2": w_gu2, "w_down2": w_down2}
    big_m = {"w_gu1": m_w_gu1, "w_down1": m_w_down1, "w_in": m_w_in, "w_out_conv": m_w_out_conv,
             "w_out_attn": m_w_out_attn, "w_o": m_w_o, "w_gu2": m_w_gu2, "w_down2": m_w_down2}
    big_v = {"w_gu1": v_w_gu1, "w_down1": v_w_down1, "w_in": v_w_in, "w_out_conv": v_w_out_conv,
             "w_out_attn": v_w_out_attn, "w_o": v_w_o, "w_gu2": v_w_gu2, "w_down2": v_w_down2}
    names = list(big)

    tiles = {"w_gu1": 256, "w_gu2": 256, "w_in": 256, "w_down1": 176, "w_down2": 176,
             "w_out_conv": 1024, "w_out_attn": 1024, "w_o": 128}

    def row_tile(n):
        r = big[n].shape[1]
        return tiles[n] if r % tiles[n] == 0 else r

    def add_tile(n):
        r, c = big[n].shape[1], big[n].shape[2]
        while r * c * 2 > (3 << 20) and r % 32 == 0:
            r //= 2
        return r

    me_arr = me.astype(jnp.int32).reshape(1)
    sources = [(n, big[n][0], BF16, row_tile(n)) for n in names] + [("conv_w", _pad_rows(conv_w[0]), F32, 8)]
    issue_order = [0, 1, 2, 8, 3, 4, 5, 6, 7]
    first = _place_shard("place_" + names[0], sources[0][1], BF16, me_arr, sources[0][3])
    started = [_gather_start("gather_start_first", [first])]
    rest = [_place_shard("place_" + sources[i][0], sources[i][1], sources[i][2], me_arr, sources[i][3],
                         deps=(started[0][3],)) for i in issue_order[1:]]
    started.append(_gather_start("gather_start_rest", rest))
    where = {0: (0, 0)}
    where.update({i: (1, p) for p, i in enumerate(issue_order[1:])})

    def fetch(tag, idxs, after):
        call = where[idxs[0]][0]
        send, recv, stacks, _ = started[call]
        positions = [where[i][1] for i in idxs]
        got = _gather_wait("gather_wait_" + tag, positions, send, recv, [stacks[p] for p in positions], after)
        return _forward_to_sibling("gather_forward_" + tag, got)

    rope_tabs = _rope_tables(t)
    gq = jnp.tile(q_norm_g, (1, nq))
    gk = jnp.tile(k_norm_g, (1, nq // GROUP))
    sink_rows = jnp.broadcast_to(sinks[0][:, None], (nq, LANES))

    wts = {}
    h1 = _rms_fwd("ffn1_norm", xs, g_ffn1)
    wts["w_gu1"], = fetch("gu1", [0], started[1][3])
    gu1, a1 = _ffn_up("ffn1_up", h1, wts["w_gu1"])
    wts["w_down1"], = fetch("down1", [1], a1)
    wd1 = wts["w_down1"].reshape(-1, d)
    x1 = _ffn_down("ffn1_down", a1, wd1, xs)
    h2 = _rms_fwd("mix_norm", x1, g_mix)
    wts["w_in"], conv_land = fetch("in", [2, 8], h2)
    w_in_full = wts["w_in"].reshape(4, 2, d, -1)
    conv_full = jnp.transpose(conv_land, (1, 0, 2)).reshape(8, cw)
    proj = _proj(h2, w_in_full)
    ca = _conv_fwd(proj, conv_full)
    qn, kn, vb = _qk_prep(proj, gq, gk, rope_tabs, cw, kw)
    o = _attn_fwd(qn, kn, vb, sink_rows)
    wts["w_out_conv"], wts["w_out_attn"] = fetch("out", [3, 4], o)
    merged, ya, yb = _mix_out(ca, o, wts["w_out_conv"], wts["w_out_attn"], proj)
    wts["w_o"], = fetch("o", [5], merged)
    wo = wts["w_o"].reshape(d, d)
    x2 = _mix_residual(merged, wo, x1)
    h3 = _rms_fwd("ffn2_norm", x2, g_ffn2)
    wts["w_gu2"], = fetch("gu2", [6], h3)
    gu2, a2 = _ffn_up("ffn2_up", h3, wts["w_gu2"])
    wts["w_down2"], = fetch("down2", [7], a2)
    wd2 = wts["w_down2"].reshape(-1, d)
    y = _ffn_down("ffn2_down", a2, wd2, x2)
    dy, sq = _loss_dy(y, target)
    loss = lax.psum(sq[0, 0] * (0.5 / d), ("x", "y", "c"))

    core = lax.axis_index("c").astype(jnp.int32).reshape(1)
    chip = (2 * lax.axis_index("x") + lax.axis_index("y")).astype(jnp.int32).reshape(1)
    def pair_start(tag, group, grads):
        stacks = [grads[n].reshape((4, 2) + big[n].shape[1:]) for n in group]
        lands = [lax.empty((4,) + big[n].shape[1:], BF16) for n in group]
        return _pair_start("rs_pair_start_" + tag, stacks, lands)

    def chip_start(tag, group, pending, after):
        send, recv, stacks, lands, _ = pending
        stacks, lands = _pair_wait("rs_pair_wait_" + tag, send, recv, stacks, lands, after)
        parts = [_pair_add("rs_pair_add_" + n, st, ld, core, add_tile(n)) for n, st, ld in zip(group, stacks, lands)]
        lands2 = [lax.empty((4,) + big[n].shape[1:], BF16) for n in group]
        return _chip_start("rs_chip_start_" + tag, parts, lands2)

    group_a, group_b, group_c = ["w_down2", "w_gu2"], ["w_o", "w_out_conv", "w_out_attn"], ["w_in"]
    group_d, group_e = ["w_down1"], ["w_gu1"]
    g = {}
    dgu2 = _ffn_bwd_act("ffn2_bwd_act", dy, wd2, gu2)
    g["w_down2"] = _ffn_bwd_dwd("ffn2_bwd_dwd", gu2, dy)
    g["w_gu2"] = _ffn_bwd_dwgu("ffn2_bwd_dwgu", h3, dgu2)
    pend_a = pair_start("a", group_a, g)
    dh3 = _ffn_bwd_dh("ffn2_bwd_dh", dgu2, wts["w_gu2"], deps=(pend_a[4],))
    ring_a = chip_start("a", group_a, pend_a, dh3)
    dx2, dg_ffn2 = _rms_bwd("ffn2_bwd_rms", x2, g_ffn2, dh3, dy, deps=(ring_a[4],))

    dya, dyb, dgates = _mix_bwd_gates(dx2, wo, ya, yb, proj, cw)
    g["w_o"] = _tn_matmul("mix_bwd_dwo", merged, dx2, min(d, 1024))
    g["w_out_conv"] = _out_proj_bwd_w("mix_bwd_dwoc", ca, dya, d // N_DEV)
    g["w_out_attn"] = _out_proj_bwd_w("mix_bwd_dwoa", o, dyb, d // N_DEV)
    pend_b = pair_start("b", group_b, g)
    dca = _out_proj_bwd_act("mix_bwd_dca", dya, wts["w_out_conv"], deps=(pend_b[4],))
    do = _out_proj_bwd_act("mix_bwd_do", dyb, wts["w_out_attn"])
    ring_b = chip_start("b", group_b, pend_b, do)
    d3, dconv_w = _conv_bwd(proj, conv_full, dca, deps=(ring_b[4],))
    dq, dkc, dkp, dvc, dvp, dsink = _attn_bwd(qn, kn, vb, sink_rows, do)
    dqkv, dgq, dgk = _qk_prep_bwd(proj, gq, gk, rope_tabs, dq, dkc, dkp, dvc, dvp, cw, kw)
    dproj = jnp.concatenate([d3[0], d3[1], d3[2], dqkv, dgates[0], dgates[1]], axis=1)
    g["w_in"] = _proj_bwd_w(h2, dproj)
    pend_c = pair_start("c", group_c, g)
    dh2 = _proj_bwd_act(dproj, w_in_full, deps=(pend_c[4],))
    ring_c = chip_start("c", group_c, pend_c, dh2)
    dx1, dg_mix = _rms_bwd("mix_bwd_rms", x1, g_mix, dh2, dx2, deps=(ring_c[4],))

    g["w_down1"] = _ffn_bwd_dwd("ffn1_bwd_dwd", gu1, dx1)
    pend_d = pair_start("d", group_d, g)
    dgu1 = _ffn_bwd_act("ffn1_bwd_act", dx1, wd1, gu1, deps=(pend_d[4],))
    ring_d = chip_start("d", group_d, pend_d, dgu1)
    g["w_gu1"] = _ffn_bwd_dwgu("ffn1_bwd_dwgu", h1, dgu1, deps=(ring_d[4],))
    pend_e = pair_start("e", group_e, g)

    big_out = {}

    def finish(tag, group, ring, after):
        send, recv, parts, lands2, _ = ring
        parts, lands2 = _chip_wait("rs_chip_wait_" + tag, send, recv, parts, lands2, after)
        for n, own, landed in zip(group, parts, lands2):
            res = _adamw_chips("adamw_" + n, chip, own, landed, big[n][0], big_m[n][0], big_v[n][0], row_tile(n))
            big_out[n] = [a[None] for a in res]
            after = res[0]
        return after

    after = finish("a", group_a, ring_a, pend_e[4])
    ring_e = chip_start("e", group_e, pend_e, after)
    dh1 = _ffn_bwd_dh("ffn1_bwd_dh", dgu1, wts["w_gu1"], deps=(ring_e[4],))
    grad_x, dg_ffn1 = _rms_bwd("ffn1_bwd_rms", xs, g_ffn1, dh1, dx1)
    after = grad_x
    for tag, group, ring in (("b", group_b, ring_b), ("c", group_c, ring_c), ("d", group_d, ring_d), ("e", group_e, ring_e)):
        after = finish(tag, group, ring, after)

    small = {"g_ffn1": dg_ffn1[0:1], "g_mix": dg_mix[0:1], "g_ffn2": dg_ffn2[0:1],
             "q_norm_g": dgq[0:1, :HEAD_DIM], "k_norm_g": dgk[0:1, :HEAD_DIM], "sinks": dsink[:, 0][None],
             "conv_w": dconv_w[0:CONV_K].reshape(1, -1)}
    small_w = {"g_ffn1": g_ffn1, "g_mix": g_mix, "g_ffn2": g_ffn2, "q_norm_g": q_norm_g, "k_norm_g": k_norm_g,
               "sinks": sinks, "conv_w": None}
    small_m = {"g_ffn1": m_g_ffn1, "g_mix": m_g_mix, "g_ffn2": m_g_ffn2, "q_norm_g": m_q_norm_g,
               "k_norm_g": m_k_norm_g, "sinks": m_sinks, "conv_w": m_conv_w}
    small_v = {"g_ffn1": v_g_ffn1, "g_mix": v_g_mix, "g_ffn2": v_g_ffn2, "q_norm_g": v_q_norm_g,
               "k_norm_g": v_k_norm_g, "sinks": v_sinks, "conv_w": v_conv_w}
    snames = list(small)
    widths = [small[n].shape[1] for n in snames]
    total = sum(widths)
    rows = -(-total // LANES)
    rows = -(-rows // 8) * 8

    def pack(vals):
        flat = jnp.concatenate([v.reshape(1, -1) for v in vals], axis=1)
        return jnp.pad(flat, ((0, 0), (0, rows * LANES - total))).reshape(rows, LANES)

    csh = cw // N_DEV

    def place_conv(local, fill):
        full = jnp.full((CONV_K, cw), fill, F32)
        return lax.dynamic_update_slice(full, local, (0, me * csh)).reshape(1, -1)

    pw = pack([small_w[n] if n != "conv_w" else place_conv(conv_w[0], 0.0) for n in snames])
    pm = pack([small_m[n] if n != "conv_w" else place_conv(m_conv_w[0], 0.0) for n in snames])
    pv = pack([small_v[n] if n != "conv_w" else place_conv(v_conv_w[0], 1.0) for n in snames])
    parts = _exchange("gather_small_grads", [pack([small[n] for n in snames])], gather=True, deps=(after,))[0]
    sg, sd, sm, sv = [a.reshape(1, -1) for a in _adamw("adamw_small", parts, pw, pm, pv, rows)]

    def unpack(flat, n):
        off = sum(widths[:snames.index(n)])
        piece = flat[:, off:off + widths[snames.index(n)]]
        if n == "conv_w":
            piece = lax.dynamic_slice(piece.reshape(CONV_K, cw), (0, me * csh), (CONV_K, csh))[None]
        return piece

    order = ["g_ffn1", "w_gu1", "w_down1", "g_mix", "w_in", "conv_w", "q_norm_g", "k_norm_g", "sinks",
             "w_out_conv", "w_out_attn", "w_o", "g_ffn2", "w_gu2", "w_down2"]
    outs = [loss, grad_x[None]]
    for idx, flat in enumerate((sg, sd, sm, sv)):
        for n in order:
            outs.append(big_out[n][idx] if n in big_out else unpack(flat, n))
    return tuple(outs)
```

```python
import functools

import jax
import jax.numpy as jnp
from jax import lax
from jax.experimental import pallas as pl
from jax.experimental.pallas import tpu as pltpu

F32 = jnp.float32
BF16 = jnp.bfloat16

N_DEV = 8
HEAD_DIM = 64
GROUP = 4
BLOCK = 128
ROT_DIM = 16
ROPE_THETA = 500000.0
RMS_EPS = 1e-6
NEG_INF = -1e30
ATTN_SCALE = HEAD_DIM ** -0.5
CONV_K = 3
LANES = 128
MXU_COLS = 256
VMEM_BYTES_V7X = 64 * 1024 * 1024
VMEM_CAP = VMEM_BYTES_V7X - 6 * 1024 * 1024

ADAM_LR = 0.001
ADAM_B1 = 0.9
ADAM_B2 = 0.999
ADAM_EPS = 1e-08
ADAM_WD = 0.01
ADAM_STEP = 10

NN = (((1,), (0,)), ((), ()))
NT = (((1,), (1,)), ((), ()))
TN = (((0,), (0,)), ((), ()))

MESH = pl.DeviceIdType.MESH


def _nbytes(shape, dtype):
    n = 1
    for s in shape:
        if s is not None:
            n *= s
    return n * jnp.dtype(dtype).itemsize


def _params(semantics, block_bytes, temp_bytes):
    assert 2 * block_bytes + temp_bytes <= VMEM_CAP, (block_bytes, temp_bytes)
    return pltpu.CompilerParams(dimension_semantics=semantics, vmem_limit_bytes=VMEM_CAP)


def _fused(name, grid, ins, outs, dots, epilogue, *, nk=1, acc_shape=None, temp_bytes=0,
           semantics=("parallel", "parallel", "arbitrary"), deps=()):
    n_in, n_out = len(ins), len(outs)
    n_dep = len(deps)

    def body(*refs):
        in_refs, out_refs = refs[:n_in], refs[n_in + n_dep:n_in + n_dep + n_out]
        scratch = refs[n_in + n_dep + n_out:]

        def products():
            if callable(dots):
                return dots(in_refs)
            total = None
            for ai, bi, contract in dots:
                a, b = in_refs[ai][...], in_refs[bi][...]
                a = a if a.dtype == BF16 else a.astype(BF16)
                b = b if b.dtype == BF16 else b.astype(BF16)
                p = lax.dot_general(a, b, contract, preferred_element_type=F32)
                total = p if total is None else total + p
            return total

        if nk == 1:
            epilogue(products() if dots else None, in_refs, out_refs)
        else:
            acc = scratch[0]
            k = pl.program_id(2)

            @pl.when(k == 0)
            def _():
                acc[...] = jnp.zeros_like(acc)

            acc[...] += products()

            @pl.when(k == nk - 1)
            def _():
                epilogue(acc[...], in_refs, out_refs)

    block_bytes = sum(_nbytes(spec.block_shape, a.dtype) for a, spec in ins)
    block_bytes += sum(_nbytes(spec.block_shape, s.dtype) for s, spec in outs)
    scratch_shapes = []
    if nk > 1:
        scratch_shapes.append(pltpu.VMEM(acc_shape, F32))
        temp_bytes += _nbytes(acc_shape, F32)
    res = pl.pallas_call(
        body, name=name, grid=grid,
        in_specs=[spec for _, spec in ins] + [pl.BlockSpec(memory_space=pl.ANY)] * n_dep,
        out_specs=[spec for _, spec in outs],
        out_shape=[s for s, _ in outs],
        scratch_shapes=scratch_shapes,
        compiler_params=_params(semantics, block_bytes, temp_bytes),
    )(*[a for a, _ in ins], *deps)
    return res


def _sds(shape, dtype):
    return jax.ShapeDtypeStruct(shape, dtype)


def _sigmoid(x):
    return jax.nn.sigmoid(x)


def _exchange(name, arrays, gather, deps=()):
    n = len(arrays)
    out_shapes = [((N_DEV,) + a.shape) if gather else a.shape for a in arrays]

    def body(*refs):
        srcs, dsts = refs[:n], refs[n + len(deps):2 * n + len(deps)]
        send_sems, recv_sems, local_sems = refs[2 * n + len(deps):]
        x, y, c = lax.axis_index("x"), lax.axis_index("y"), lax.axis_index("c")
        me = 4 * x + 2 * y + c
        copies = []
        for w in range(n):
            own = srcs[w] if gather else srcs[w].at[me]
            local = pltpu.make_async_copy(own, dsts[w].at[me], local_sems.at[w])
            local.start()
            copies.append(local)
            for k in range(1, N_DEV):
                px = (1 - x) if (k & 4) else x
                py = (1 - y) if (k & 2) else y
                pc = (1 - c) if (k & 1) else c
                peer = 4 * px + 2 * py + pc
                cp = pltpu.make_async_remote_copy(
                    src_ref=srcs[w] if gather else srcs[w].at[peer],
                    dst_ref=dsts[w].at[me],
                    send_sem=send_sems.at[w * (N_DEV - 1) + k - 1],
                    recv_sem=recv_sems.at[w * (N_DEV - 1) + k - 1],
                    device_id=(px, py, pc), device_id_type=MESH)
                cp.start()
                copies.append(cp)
        for cp in copies:
            cp.wait()

    hbm = pl.BlockSpec(memory_space=pltpu.HBM)
    return pl.pallas_call(
        body, name=name,
        in_specs=[hbm] * n + [pl.BlockSpec(memory_space=pl.ANY)] * len(deps), out_specs=[hbm] * n,
        out_shape=[_sds(s, a.dtype) for s, a in zip(out_shapes, arrays)],
        scratch_shapes=[pltpu.SemaphoreType.DMA((n * (N_DEV - 1),)),
                        pltpu.SemaphoreType.DMA((n * (N_DEV - 1),)),
                        pltpu.SemaphoreType.DMA((n,))],
    )(*arrays, *deps)


_HBM = pl.BlockSpec(memory_space=pltpu.HBM)
_SEM = pl.BlockSpec(memory_space=pltpu.SEMAPHORE)
_ANY = pl.BlockSpec(memory_space=pl.ANY)
_EFFECT = pltpu.SideEffectType.DATAFLOW_SIDE_EFFECTING
N_TARGETS = 4


def _mesh_pos():
    return lax.axis_index("x"), lax.axis_index("y"), lax.axis_index("c")


def _chip_peers(x, y, c):
    return [(1 - x, y, c), (x, 1 - y, c), (1 - x, 1 - y, c)]


def _dev_index(pos):
    return 4 * pos[0] + 2 * pos[1] + pos[2]


def _hbm_like(a):
    return pltpu.HBM(a.shape, a.dtype)


def _place_shard(name, w, out_dtype, me, tr, deps=()):
    r, c = w.shape
    n_dep = len(deps)

    def body(me_ref, w_ref, *rest):
        rest[n_dep][...] = w_ref[...].astype(out_dtype)

    grid_spec = pltpu.PrefetchScalarGridSpec(
        num_scalar_prefetch=1, grid=(r // tr,),
        in_specs=[pl.BlockSpec((tr, c), lambda i, me_ref: (i, 0))] + [_ANY] * n_dep,
        out_specs=pl.BlockSpec((None, tr, c), lambda i, me_ref: (me_ref[0], i, 0)))
    return pl.pallas_call(
        body, name=name, grid_spec=grid_spec, out_shape=_sds((N_DEV, r, c), out_dtype),
        compiler_params=_params(("parallel",), tr * c * 6, tr * c * 4),
    )(me, w, *deps)


def _gather_start(name, lands):
    n = len(lands)

    def body(*refs):
        bufs = refs[:n]
        send, recv = refs[n], refs[n + 1]
        token = refs[-1]
        x, y, c = _mesh_pos()
        me = _dev_index((x, y, c))
        targets = [(x, y, 1 - c)] + _chip_peers(x, y, c)
        for w in range(n):
            for k, to in enumerate(targets):
                pltpu.make_async_remote_copy(
                    src_ref=bufs[w].at[me], dst_ref=bufs[w].at[me],
                    send_sem=send.at[N_TARGETS * w + k], recv_sem=recv.at[N_TARGETS * w + k],
                    device_id=to, device_id_type=MESH).start()
        token[...] = jnp.zeros_like(token)

    sems = pltpu.SemaphoreType.DMA((N_TARGETS * n,))
    outs = pl.pallas_call(
        body, name=name,
        in_specs=[_HBM] * n, out_specs=[_SEM, _SEM] + [_HBM] * n + [_token_spec()],
        out_shape=[sems, sems] + [_hbm_like(a) for a in lands] + [_sds((8, LANES), F32)],
        input_output_aliases={i: 2 + i for i in range(n)},
        compiler_params=pltpu.CompilerParams(has_side_effects=_EFFECT),
    )(*lands)
    return outs[0], outs[1], list(outs[2:2 + n]), outs[-1]


def _gather_wait(name, positions, send, recv, lands, after):
    m = len(positions)

    def body(*refs):
        bufs = refs[:m]
        send_sems, recv_sems = refs[m], refs[m + 1]
        x, y, c = _mesh_pos()
        me = _dev_index((x, y, c))
        sources = [(x, y, 1 - c)] + _chip_peers(x, y, c)
        for j, w in enumerate(positions):
            for k, frm in enumerate(sources):
                cp = pltpu.make_async_remote_copy(
                    src_ref=bufs[j].at[me], dst_ref=bufs[j].at[_dev_index(frm)],
                    send_sem=send_sems.at[N_TARGETS * w + k], recv_sem=recv_sems.at[N_TARGETS * w + k],
                    device_id=frm, device_id_type=MESH)
                cp.wait_send()
                cp.wait_recv()

    outs = pl.pallas_call(
        body, name=name,
        in_specs=[_HBM] * m + [_SEM, _SEM, _ANY], out_specs=[_HBM] * m,
        out_shape=[_hbm_like(a) for a in lands],
        input_output_aliases={i: i for i in range(m)},
        compiler_params=pltpu.CompilerParams(has_side_effects=_EFFECT),
    )(*lands, send, recv, after)
    return list(outs)


def _forward_to_sibling(name, lands):
    m = len(lands)

    def body(*refs):
        bufs = refs[m:2 * m]
        send_sems, recv_sems = refs[2 * m], refs[2 * m + 1]
        x, y, c = _mesh_pos()
        copies = []
        for j in range(m):
            for k, chip in enumerate(_chip_peers(x, y, c)):
                block = bufs[j].at[_dev_index(chip)]
                cp = pltpu.make_async_remote_copy(
                    src_ref=block, dst_ref=block,
                    send_sem=send_sems.at[3 * j + k], recv_sem=recv_sems.at[3 * j + k],
                    device_id=(x, y, 1 - c), device_id_type=MESH)
                cp.start()
                copies.append(cp)
        for cp in copies:
            cp.wait()

    outs = pl.pallas_call(
        body, name=name,
        in_specs=[_HBM] * m, out_specs=[_HBM] * m,
        out_shape=[_sds(a.shape, a.dtype) for a in lands],
        input_output_aliases={i: i for i in range(m)},
        scratch_shapes=[pltpu.SemaphoreType.DMA((3 * m,)), pltpu.SemaphoreType.DMA((3 * m,))],
    )(*lands)
    return list(outs)


def _token_spec():
    return pl.BlockSpec(memory_space=pltpu.VMEM)


def _pair_start(name, stacks, lands):
    n = len(stacks)

    def body(*refs):
        srcs, dsts = refs[:n], refs[n:2 * n]
        send, recv = refs[2 * n], refs[2 * n + 1]
        token = refs[-1]
        x, y, c = _mesh_pos()
        for w in range(n):
            for chip in range(4):
                pltpu.make_async_remote_copy(
                    src_ref=srcs[w].at[chip, 1 - c], dst_ref=dsts[w].at[chip],
                    send_sem=send.at[4 * w + chip], recv_sem=recv.at[4 * w + chip],
                    device_id=(x, y, 1 - c), device_id_type=MESH).start()
        token[...] = jnp.zeros_like(token)

    sems = pltpu.SemaphoreType.DMA((4 * n,))
    outs = pl.pallas_call(
        body, name=name,
        in_specs=[_HBM] * (2 * n), out_specs=[_SEM, _SEM] + [_HBM] * (2 * n) + [_token_spec()],
        out_shape=[sems, sems] + [_hbm_like(a) for a in stacks] + [_hbm_like(a) for a in lands] + [_sds((8, LANES), F32)],
        input_output_aliases={i: 2 + i for i in range(2 * n)},
        compiler_params=pltpu.CompilerParams(has_side_effects=_EFFECT),
    )(*stacks, *lands)
    return outs[0], outs[1], list(outs[2:2 + n]), list(outs[2 + n:2 + 2 * n]), outs[-1]


def _pair_wait(name, send, recv, stacks, lands, after):
    n = len(stacks)

    def body(*refs):
        srcs, dsts = refs[:n], refs[n:2 * n]
        send_sems, recv_sems = refs[2 * n], refs[2 * n + 1]
        x, y, c = _mesh_pos()
        for w in range(n):
            for chip in range(4):
                cp = pltpu.make_async_remote_copy(
                    src_ref=srcs[w].at[chip, 1 - c], dst_ref=dsts[w].at[chip],
                    send_sem=send_sems.at[4 * w + chip], recv_sem=recv_sems.at[4 * w + chip],
                    device_id=(x, y, 1 - c), device_id_type=MESH)
                cp.wait_send()
                cp.wait_recv()

    outs = pl.pallas_call(
        body, name=name,
        in_specs=[_HBM] * (2 * n) + [_SEM, _SEM, _ANY], out_specs=[_HBM] * (2 * n),
        out_shape=[_hbm_like(a) for a in stacks] + [_hbm_like(a) for a in lands],
        input_output_aliases={i: i for i in range(2 * n)},
        compiler_params=pltpu.CompilerParams(has_side_effects=_EFFECT),
    )(*stacks, *lands, send, recv, after)
    return list(outs[:n]), list(outs[n:])


def _pair_add(name, stack, land, core, tr):
    _, _, r, c = stack.shape

    def body(core_ref, a_ref, b_ref, o_ref):
        o_ref[...] = (a_ref[...].astype(F32) + b_ref[...].astype(F32)).astype(BF16)

    grid_spec = pltpu.PrefetchScalarGridSpec(
        num_scalar_prefetch=1, grid=(4, r // tr),
        in_specs=[pl.BlockSpec((None, None, tr, c), lambda k, i, core_ref: (k, core_ref[0], i, 0)),
                  pl.BlockSpec((None, tr, c), lambda k, i, core_ref: (k, i, 0))],
        out_specs=pl.BlockSpec((None, tr, c), lambda k, i, core_ref: (k, i, 0)))
    return pl.pallas_call(
        body, name=name, grid_spec=grid_spec, out_shape=_sds((4, r, c), BF16),
        compiler_params=_params(("parallel", "parallel"), 3 * tr * c * 2, 3 * tr * c * 4),
    )(core, stack, land)


def _chip_start(name, parts, lands):
    n = len(parts)

    def body(*refs):
        srcs, dsts = refs[:n], refs[n:2 * n]
        send, recv = refs[2 * n], refs[2 * n + 1]
        token = refs[-1]
        x, y, c = _mesh_pos()
        for w in range(n):
            for k, to in enumerate(_chip_peers(x, y, c)):
                pltpu.make_async_remote_copy(
                    src_ref=srcs[w].at[2 * to[0] + to[1]], dst_ref=dsts[w].at[2 * x + y],
                    send_sem=send.at[3 * w + k], recv_sem=recv.at[3 * w + k],
                    device_id=to, device_id_type=MESH).start()
        token[...] = jnp.zeros_like(token)

    sems = pltpu.SemaphoreType.DMA((3 * n,))
    outs = pl.pallas_call(
        body, name=name,
        in_specs=[_HBM] * (2 * n), out_specs=[_SEM, _SEM] + [_HBM] * (2 * n) + [_token_spec()],
        out_shape=[sems, sems] + [_hbm_like(a) for a in parts] + [_hbm_like(a) for a in lands] + [_sds((8, LANES), F32)],
        input_output_aliases={i: 2 + i for i in range(2 * n)},
        compiler_params=pltpu.CompilerParams(has_side_effects=_EFFECT),
    )(*parts, *lands)
    return outs[0], outs[1], list(outs[2:2 + n]), list(outs[2 + n:2 + 2 * n]), outs[-1]


def _chip_wait(name, send, recv, parts, lands, after):
    n = len(parts)

    def body(*refs):
        srcs, dsts = refs[:n], refs[n:2 * n]
        send_sems, recv_sems = refs[2 * n], refs[2 * n + 1]
        x, y, c = _mesh_pos()
        for w in range(n):
            for k, frm in enumerate(_chip_peers(x, y, c)):
                chip = 2 * frm[0] + frm[1]
                cp = pltpu.make_async_remote_copy(
                    src_ref=srcs[w].at[chip], dst_ref=dsts[w].at[chip],
                    send_sem=send_sems.at[3 * w + k], recv_sem=recv_sems.at[3 * w + k],
                    device_id=frm, device_id_type=MESH)
                cp.wait_send()
                cp.wait_recv()

    outs = pl.pallas_call(
        body, name=name,
        in_specs=[_HBM] * (2 * n) + [_SEM, _SEM, _ANY], out_specs=[_HBM] * (2 * n),
        out_shape=[_hbm_like(a) for a in parts] + [_hbm_like(a) for a in lands],
        input_output_aliases={i: i for i in range(2 * n)},
        compiler_params=pltpu.CompilerParams(has_side_effects=_EFFECT),
    )(*parts, *lands, send, recv, after)
    return list(outs[:n]), list(outs[n:])


def _row_tile(t):
    return min(t, 256)


def _rms_fwd(name, x, g):
    t, d = x.shape
    tm = _row_tile(t)

    def epilogue(_, ins, outs):
        xv = ins[0][...]
        r = lax.rsqrt(jnp.mean(xv * xv, axis=-1, keepdims=True) + RMS_EPS)
        outs[0][...] = (xv * r * ins[1][...]).astype(BF16)

    row = pl.BlockSpec((tm, d), lambda i, j, k: (i, 0))
    vec = pl.BlockSpec((1, d), lambda i, j, k: (0, 0))
    return _fused(name, (t // tm, 1, 1), [(x, row), (g, vec)], [(_sds((t, d), BF16), row)], [], epilogue,
                  temp_bytes=4 * tm * d * 4)[0]


def _rms_bwd(name, x, g, dh, resid, deps=()):
    t, d = x.shape
    tm = _row_tile(t)

    def epilogue(_, ins, outs):
        xv, gv, dhv = ins[0][...], ins[1][...], ins[2][...]
        r = lax.rsqrt(jnp.mean(xv * xv, axis=-1, keepdims=True) + RMS_EPS)
        xh = xv * r
        u = dhv * gv
        dot = jnp.mean(u * xh, axis=-1, keepdims=True)
        outs[0][...] = ins[3][...] + r * (u - xh * dot)

        @pl.when(pl.program_id(0) == 0)
        def _():
            outs[1][...] = jnp.zeros_like(outs[1])

        outs[1][0:1, :] += jnp.sum(dhv * xh, axis=0, keepdims=True)

    row = pl.BlockSpec((tm, d), lambda i, j, k: (i, 0))
    vec = pl.BlockSpec((1, d), lambda i, j, k: (0, 0))
    acc = pl.BlockSpec((8, d), lambda i, j, k: (0, 0))
    return _fused(name, (t // tm, 1, 1), [(x, row), (g, vec), (dh, row), (resid, row)],
                  [(_sds((t, d), F32), row), (_sds((8, d), F32), acc)], [], epilogue,
                  temp_bytes=6 * tm * d * 4, semantics=("arbitrary", "arbitrary", "arbitrary"), deps=deps)


def _loss_dy(y, target):
    t, d = y.shape
    tm = _row_tile(t)

    def epilogue(_, ins, outs):
        e = ins[0][...] - ins[1][...]
        outs[0][...] = e * (1.0 / d)

        @pl.when(pl.program_id(0) == 0)
        def _():
            outs[1][...] = jnp.zeros_like(outs[1])

        part = jnp.sum(jnp.sum(e * e, axis=1, keepdims=True), axis=0, keepdims=True)
        outs[1][...] += jnp.broadcast_to(part, outs[1].shape)

    row = pl.BlockSpec((tm, d), lambda i, j, k: (i, 0))
    acc = pl.BlockSpec((8, LANES), lambda i, j, k: (0, 0))
    return _fused("loss_dy", (t // tm, 1, 1), [(y, row), (target, row)],
                  [(_sds((t, d), F32), row), (_sds((8, LANES), F32), acc)], [], epilogue,
                  temp_bytes=3 * tm * d * 4, semantics=("arbitrary", "arbitrary", "arbitrary"))


def _ffn_up(name, h, wgu):
    t, d = h.shape
    nb = wgu.shape[2]
    f = 4 * nb
    tm = min(t, 512)

    def body(h_ref, wg_ref, wu_ref, gu_ref, a_ref):
        hv = h_ref[...]
        for c0 in range(0, nb, MXU_COLS):
            cs = slice(c0, min(c0 + MXU_COLS, nb))
            g = jnp.dot(hv, wg_ref[:, cs], preferred_element_type=F32)
            u = jnp.dot(hv, wu_ref[:, cs], preferred_element_type=F32)
            gu_ref[0, :, cs] = g.astype(BF16)
            gu_ref[1, :, cs] = u.astype(BF16)
            a_ref[:, cs] = (g * _sigmoid(g) * u).astype(BF16)

    blocks = tm * d * 2 + 2 * d * nb * 2 + 3 * tm * nb * 2
    return pl.pallas_call(
        body, name=name, grid=(4, t // tm),
        in_specs=[pl.BlockSpec((tm, d), lambda j, i: (i, 0)),
                  pl.BlockSpec((None, d, nb), lambda j, i: (j, 0, 0)),
                  pl.BlockSpec((None, d, nb), lambda j, i: (j + 4, 0, 0))],
        out_specs=[pl.BlockSpec((2, tm, nb), lambda j, i: (0, i, j)),
                   pl.BlockSpec((tm, nb), lambda j, i: (i, j))],
        out_shape=[_sds((2, t, f), BF16), _sds((t, f), BF16)],
        compiler_params=_params(("parallel", "parallel"), blocks, 8 * tm * MXU_COLS * 4),
    )(h, wgu, wgu)


def _ffn_down(name, a, wd, x):
    t, f = a.shape
    d = wd.shape[1]
    tm = min(t, 512)
    tn = min(d, 1024)

    def epilogue(acc, ins, outs):
        outs[0][...] = ins[2][...] + 0.5 * acc

    blk = pl.BlockSpec((tm, tn), lambda j, i, k: (i, j))
    return _fused(name, (d // tn, t // tm, 1),
                  [(a, pl.BlockSpec((tm, f), lambda j, i, k: (i, 0))),
                   (wd, pl.BlockSpec((f, tn), lambda j, i, k: (0, j))),
                   (x, blk)],
                  [(_sds((t, d), F32), blk)],
                  [(0, 1, NN)], epilogue, temp_bytes=2 * tm * tn * 4)[0]


def _ffn_bwd_act(name, dy, wd, gu, deps=()):
    t, d = dy.shape
    f = wd.shape[0]
    nb = f // 4
    tm = min(t, 512)

    def body(dy_ref, wd_ref, gu_ref, *rest):
        dgu_ref, a_ref = rest[-2], rest[-1]
        dyv = dy_ref[...].astype(BF16)
        for c0 in range(0, nb, MXU_COLS):
            cs = slice(c0, min(c0 + MXU_COLS, nb))
            da = 0.5 * lax.dot_general(dyv, wd_ref[cs, :], NT, preferred_element_type=F32)
            g = gu_ref[0, :, cs].astype(F32)
            u = gu_ref[1, :, cs].astype(F32)
            s = _sigmoid(g)
            silu = g * s
            dgu_ref[0, :, cs] = (da * u * (s * (1.0 + g * (1.0 - s)))).astype(BF16)
            dgu_ref[1, :, cs] = (da * silu).astype(BF16)
            a_ref[:, cs] = (silu * u).astype(BF16)

    blocks = tm * d * 4 + nb * d * 2 + 5 * tm * nb * 2
    return pl.pallas_call(
        body, name=name, grid=(4, t // tm),
        in_specs=[pl.BlockSpec((tm, d), lambda j, i: (i, 0)),
                  pl.BlockSpec((nb, d), lambda j, i: (j, 0)),
                  pl.BlockSpec((2, tm, nb), lambda j, i: (0, i, j))] + [_ANY] * len(deps),
        out_specs=[pl.BlockSpec((2, tm, nb), lambda j, i: (0, i, j)), pl.BlockSpec((tm, nb), lambda j, i: (i, j))],
        out_shape=[_sds((2, t, f), BF16), _sds((t, f), BF16)],
        compiler_params=_params(("parallel", "parallel"), blocks, tm * d * 2 + 8 * tm * MXU_COLS * 4),
    )(dy, wd, gu, *deps)


def _ffn_bwd_dwd(name, a, dy, deps=()):
    t, f = a.shape
    d = dy.shape[1]
    tm = f // 4
    tn = min(d, 512)

    def epilogue(acc, ins, outs):
        outs[0][...] = (0.5 * acc).astype(BF16)

    return _fused(name, (4, d // tn, 1),
                  [(a, pl.BlockSpec((t, tm), lambda i, j, k: (0, i))),
                   (dy, pl.BlockSpec((t, tn), lambda i, j, k: (0, j)))],
                  [(_sds((f, d), BF16), pl.BlockSpec((tm, tn), lambda i, j, k: (i, j)))],
                  [(0, 1, TN)], epilogue, temp_bytes=t * tn * 2 + 2 * tm * tn * 4, deps=deps)[0]


def _ffn_bwd_dh(name, dgu, wgu, deps=()):
    _, t, f = dgu.shape
    d, nb = wgu.shape[1], wgu.shape[2]
    tm = min(t, 512)

    def products(ins):
        return (lax.dot_general(ins[0][:, 0:nb], ins[1][0], NT, preferred_element_type=F32)
                + lax.dot_general(ins[0][:, nb:2 * nb], ins[1][1], NT, preferred_element_type=F32))

    def epilogue(acc, ins, outs):
        outs[0][...] = acc

    return _fused(name, (t // tm, 1, 4),
                  [(dgu, pl.BlockSpec((None, tm, 2 * nb), lambda i, j, k: (k // 2, i, k % 2))),
                   (wgu, pl.BlockSpec((2, d, nb), lambda i, j, k: (k, 0, 0)))],
                  [(_sds((t, d), F32), pl.BlockSpec((tm, d), lambda i, j, k: (i, 0)))],
                  products, epilogue, nk=4, acc_shape=(tm, d), temp_bytes=tm * d * 4, deps=deps)[0]


def _ffn_bwd_dwgu(name, h, dgu, deps=()):
    t, d = h.shape
    nb = dgu.shape[2] // 4
    tm = min(d, 512)

    def epilogue(acc, ins, outs):
        outs[0][...] = acc.astype(BF16)

    return _fused(name, (N_DEV, d // tm, 1),
                  [(h, pl.BlockSpec((t, tm), lambda i, j, k: (0, j))),
                   (dgu, pl.BlockSpec((None, t, nb), lambda i, j, k: (i // 4, 0, i % 4)))],
                  [(_sds((N_DEV, d, nb), BF16), pl.BlockSpec((None, tm, nb), lambda i, j, k: (i, j, 0)))],
                  [(0, 1, TN)], epilogue, temp_bytes=2 * tm * nb * 4, deps=deps)[0]


def _proj(h, w_in):
    t, d = h.shape
    nb = w_in.shape[3]
    tm = min(t, 512)

    def body(h_ref, w_ref, o_ref):
        hv = h_ref[...]
        o_ref[:, 0:nb] = jnp.dot(hv, w_ref[0], preferred_element_type=F32).astype(BF16)
        o_ref[:, nb:2 * nb] = jnp.dot(hv, w_ref[1], preferred_element_type=F32).astype(BF16)

    blocks = tm * d * 2 + 2 * d * nb * 2 + tm * 2 * nb * 4
    return pl.pallas_call(
        body, name="mix_proj", grid=(4, t // tm),
        in_specs=[pl.BlockSpec((tm, d), lambda j, i: (i, 0)),
                  pl.BlockSpec((None, 2, d, nb), lambda j, i: (j, 0, 0, 0))],
        out_specs=pl.BlockSpec((tm, 2 * nb), lambda j, i: (i, j)),
        out_shape=_sds((t, N_DEV * nb), BF16),
        compiler_params=_params(("parallel", "parallel"), blocks, 2 * tm * nb * 4),
    )(h, w_in)


def _shift_rows(u, k):
    t = u.shape[0]
    rolled = pltpu.roll(u, k % t, axis=0)
    row = lax.broadcasted_iota(jnp.int32, u.shape, 0)
    keep = (row >= k) if k > 0 else (row < t + k)
    return jnp.where(keep, rolled, 0.0)


def _conv_fwd(proj, conv_w):
    t = proj.shape[0]
    cw = conv_w.shape[1]
    tc = min(cw, 256)
    nc = cw // tc

    def epilogue(_, ins, outs):
        u = ins[2][...].astype(F32) * ins[0][...].astype(F32)
        w = ins[3][...]
        y = u * w[2:3, :] + _shift_rows(u, 1) * w[1:2, :] + _shift_rows(u, 2) * w[0:1, :]
        outs[0][...] = (ins[1][...].astype(F32) * y).astype(BF16)

    def col(seg):
        return pl.BlockSpec((t, tc), lambda i, j, k: (0, seg * nc + i))

    return _fused("conv_fwd", (nc, 1, 1),
                  [(proj, col(0)), (proj, col(1)), (proj, col(2)),
                   (conv_w, pl.BlockSpec((8, tc), lambda i, j, k: (0, i)))],
                  [(_sds((t, cw), BF16), pl.BlockSpec((t, tc), lambda i, j, k: (0, i)))],
                  [], epilogue, temp_bytes=6 * t * tc * 4)[0]


def _conv_bwd(proj, conv_w, dca, deps=()):
    t = proj.shape[0]
    cw = conv_w.shape[1]
    tc = min(cw, 256)
    nc = cw // tc

    def epilogue(_, ins, outs):
        xc, bg, cg = ins[0][...].astype(F32), ins[1][...].astype(F32), ins[2][...].astype(F32)
        w, dc = ins[3][...], ins[4][...]
        u = cg * xc
        u1, u2 = _shift_rows(u, 1), _shift_rows(u, 2)
        y = u * w[2:3, :] + u1 * w[1:2, :] + u2 * w[0:1, :]
        dconv = dc * bg
        du = dconv * w[2:3, :] + _shift_rows(dconv, -1) * w[1:2, :] + _shift_rows(dconv, -2) * w[0:1, :]
        outs[0][0] = (du * cg).astype(BF16)
        outs[0][1] = (dc * y).astype(BF16)
        outs[0][2] = (du * xc).astype(BF16)
        outs[1][...] = jnp.zeros_like(outs[1])
        outs[1][0:1, :] = jnp.sum(dconv * u2, axis=0, keepdims=True)
        outs[1][1:2, :] = jnp.sum(dconv * u1, axis=0, keepdims=True)
        outs[1][2:3, :] = jnp.sum(dconv * u, axis=0, keepdims=True)

    def col(seg):
        return pl.BlockSpec((t, tc), lambda i, j, k: (0, seg * nc + i))

    own = pl.BlockSpec((t, tc), lambda i, j, k: (0, i))
    wspec = pl.BlockSpec((8, tc), lambda i, j, k: (0, i))
    return _fused("conv_bwd", (nc, 1, 1),
                  [(proj, col(0)), (proj, col(1)), (proj, col(2)), (conv_w, wspec), (dca, own)],
                  [(_sds((3, t, cw), BF16), pl.BlockSpec((3, t, tc), lambda i, j, k: (0, 0, i))),
                   (_sds((8, cw), F32), wspec)],
                  [], epilogue, temp_bytes=10 * t * tc * 4, deps=deps)


def _split3(x):
    hi = x.astype(BF16)
    r1 = x - hi.astype(F32)
    mid = r1.astype(BF16)
    lo = (r1 - mid.astype(F32)).astype(BF16)
    return hi, mid, lo


def _head_selector(width):
    r = lax.broadcasted_iota(jnp.int32, (width, LANES), 0)
    c = lax.broadcasted_iota(jnp.int32, (width, LANES), 1)
    return (lax.shift_right_logical(r, 6) == c).astype(BF16)


def _head_sum(x, sel):
    return sum(jnp.dot(p, sel, preferred_element_type=F32) for p in _split3(x))


def _head_bcast(r, sel):
    return sum(lax.dot_general(p, sel, NT, preferred_element_type=F32) for p in _split3(r))


def _rope(x, c, sa, sb):
    n = x.shape[1]
    return x * c + pltpu.roll(x, n - ROT_DIM // 2, axis=1) * sa + pltpu.roll(x, ROT_DIM // 2, axis=1) * sb


def _rope_t(d, c, sa, sb):
    n = d.shape[1]
    return d * c + pltpu.roll(d * sa, ROT_DIM // 2, axis=1) + pltpu.roll(d * sb, n - ROT_DIM // 2, axis=1)


def _tile_lanes(tab, width):
    return tab if width == tab.shape[1] else jnp.tile(tab, (1, width // tab.shape[1]))


def _qk_prep(proj, gq, gk, rope_tabs, cw, kw):
    t = proj.shape[0]
    tm = _row_tile(t)

    def epilogue(_, ins, outs):
        c, sa, sb = ins[5][...], ins[6][...], ins[7][...]
        for src, gain, dst, width in ((0, 3, 0, cw), (1, 4, 1, kw)):
            xv = ins[src][...].astype(F32)
            sel = _head_selector(width)
            r = lax.rsqrt(_head_sum(xv * xv, sel) * (1.0 / HEAD_DIM) + RMS_EPS)
            xn = xv * _head_bcast(r, sel) * ins[gain][...]
            outs[dst][...] = _rope(xn, _tile_lanes(c, width), _tile_lanes(sa, width), _tile_lanes(sb, width)).astype(BF16)
        outs[2][...] = ins[2][...].astype(BF16)

    kblk = cw // kw
    tab = pl.BlockSpec((tm, LANES), lambda i, j, k: (i, 0))
    kspec = pl.BlockSpec((tm, kw), lambda i, j, k: (i, 0))
    return _fused("qk_prep", (t // tm, 1, 1),
                  [(proj, pl.BlockSpec((tm, cw), lambda i, j, k: (i, 3))),
                   (proj, pl.BlockSpec((tm, kw), lambda i, j, k: (i, 4 * kblk))),
                   (proj, pl.BlockSpec((tm, kw), lambda i, j, k: (i, 4 * kblk + 1))),
                   (gq, pl.BlockSpec((1, cw), lambda i, j, k: (0, 0))),
                   (gk, pl.BlockSpec((1, kw), lambda i, j, k: (0, 0))),
                   (rope_tabs[0], tab), (rope_tabs[1], tab), (rope_tabs[2], tab)],
                  [(_sds((t, cw), BF16), pl.BlockSpec((tm, cw), lambda i, j, k: (i, 0))),
                   (_sds((t, kw), BF16), kspec), (_sds((t, kw), BF16), kspec)],
                  [], epilogue, temp_bytes=12 * tm * cw * 4)


def _qk_prep_bwd(proj, gq, gk, rope_tabs, dq, dkc, dkp, dvc, dvp, cw, kw):
    t = proj.shape[0]
    tm = BLOCK
    nblk = t // tm

    def epilogue(_, ins, outs):
        c, sa, sb = ins[5][...], ins[6][...], ins[7][...]
        has_next = (pl.program_id(0) < nblk - 1).astype(F32)
        dk = ins[9][...] + has_next * ins[10][...]
        dv = ins[11][...] + has_next * ins[12][...]
        pieces = []
        for src, gain, dval, dst, width in ((0, 3, ins[8][...], 1, cw), (1, 4, dk, 2, kw)):
            xv, gv = ins[src][...].astype(F32), ins[gain][...]
            sel = _head_selector(width)
            r = _head_bcast(lax.rsqrt(_head_sum(xv * xv, sel) * (1.0 / HEAD_DIM) + RMS_EPS), sel)
            xh = xv * r
            dxn = _rope_t(dval, _tile_lanes(c, width), _tile_lanes(sa, width), _tile_lanes(sb, width))
            u = dxn * gv
            dot = _head_bcast(_head_sum(u * xh, sel), sel) * (1.0 / HEAD_DIM)
            pieces.append((r * (u - xh * dot)).astype(BF16))
            ri = lax.broadcasted_iota(jnp.int32, (width, LANES), 0)
            ci = lax.broadcasted_iota(jnp.int32, (width, LANES), 1)
            fold = (lax.bitwise_and(ri, HEAD_DIM - 1) == ci).astype(BF16)
            colsum = jnp.broadcast_to(jnp.sum(dxn * xh, axis=0, keepdims=True), (8, width))
            part = sum(jnp.dot(p, fold, preferred_element_type=F32) for p in _split3(colsum))

            @pl.when(pl.program_id(0) == 0)
            def _():
                outs[dst][...] = jnp.zeros_like(outs[dst])

            outs[dst][0:1, :] += part[0:1, :]
        outs[0][:, 0:cw] = pieces[0]
        outs[0][:, cw:cw + kw] = pieces[1]
        outs[0][:, cw + kw:cw + 2 * kw] = dv.astype(BF16)

    kblk = cw // kw
    tab = pl.BlockSpec((tm, LANES), lambda i, j, k: (i, 0))
    kcur = pl.BlockSpec((tm, kw), lambda i, j, k: (i, 0))
    knext = pl.BlockSpec((tm, kw), lambda i, j, k: (jnp.minimum(i + 1, nblk - 1), 0))
    acc = pl.BlockSpec((8, LANES), lambda i, j, k: (0, 0))
    return _fused("qk_prep_bwd", (nblk, 1, 1),
                  [(proj, pl.BlockSpec((tm, cw), lambda i, j, k: (i, 3))),
                   (proj, pl.BlockSpec((tm, kw), lambda i, j, k: (i, 4 * kblk))),
                   (proj, pl.BlockSpec((tm, kw), lambda i, j, k: (i, 4 * kblk + 1))),
                   (gq, pl.BlockSpec((1, cw), lambda i, j, k: (0, 0))),
                   (gk, pl.BlockSpec((1, kw), lambda i, j, k: (0, 0))),
                   (rope_tabs[0], tab), (rope_tabs[1], tab), (rope_tabs[2], tab),
                   (dq, pl.BlockSpec((tm, cw), lambda i, j, k: (i, 0))),
                   (dkc, kcur), (dkp, knext), (dvc, kcur), (dvp, knext)],
                  [(_sds((t, cw + 2 * kw), BF16), pl.BlockSpec((tm, cw + 2 * kw), lambda i, j, k: (i, 0))),
                   (_sds((8, LANES), F32), acc), (_sds((8, LANES), F32), acc)],
                  [], epilogue, temp_bytes=16 * tm * cw * 4, semantics=("arbitrary", "arbitrary", "arbitrary"))


def _attn_mask(n):
    key = lax.broadcasted_iota(jnp.int32, (2 * BLOCK, GROUP * BLOCK), 0)
    qry = lax.bitwise_and(lax.broadcasted_iota(jnp.int32, (2 * BLOCK, GROUP * BLOCK), 1), BLOCK - 1)
    return (key > qry) & (key <= qry + BLOCK) & ((key >= BLOCK) | (n > 0))


def _stack_heads(x, h):
    return jnp.concatenate([x[:, (h * GROUP + g) * HEAD_DIM:(h * GROUP + g + 1) * HEAD_DIM] for g in range(GROUP)], axis=0)


def _softmax_with_sink(q4, k2, sink_ref, h, valid):
    sink = jnp.concatenate([sink_ref[h * GROUP + g:h * GROUP + g + 1, :] for g in range(GROUP)], axis=1)
    s = lax.dot_general(k2, q4, NT, preferred_element_type=F32) * ATTN_SCALE
    s = jnp.where(valid, s, NEG_INF)
    m = jnp.maximum(jnp.max(s, axis=0, keepdims=True), sink)
    p = jnp.exp(s - m)
    es = jnp.exp(sink - m)
    inv = 1.0 / (jnp.sum(p, axis=0, keepdims=True) + es)
    return p * inv, es * inv


def _attn_fwd(qn, kn, vb, sink_rows):
    t, cw = qn.shape
    kw = kn.shape[1]
    nkv = kw // HEAD_DIM

    def body(q_ref, kp_ref, kc_ref, vp_ref, vc_ref, sink_ref, o_ref):
        valid = _attn_mask(pl.program_id(0))
        qv = q_ref[...]
        kp, kc, vp, vc = kp_ref[...], kc_ref[...], vp_ref[...], vc_ref[...]
        outs = []
        for h in range(nkv):
            hs = slice(h * HEAD_DIM, (h + 1) * HEAD_DIM)
            k2 = jnp.concatenate([kp[:, hs], kc[:, hs]], axis=0)
            v2 = jnp.concatenate([vp[:, hs], vc[:, hs]], axis=0)
            pn, _ = _softmax_with_sink(_stack_heads(qv, h), k2, sink_ref, h, valid)
            o4 = lax.dot_general(pn.astype(BF16), v2, TN, preferred_element_type=F32)
            outs += [o4[g * BLOCK:(g + 1) * BLOCK] for g in range(GROUP)]
        o_ref[...] = jnp.concatenate(outs, axis=-1).astype(BF16)

    cur = lambda n: (n, 0)
    prev = lambda n: (jnp.maximum(n - 1, 0), 0)
    return pl.pallas_call(
        body, name="attn_fwd", grid=(t // BLOCK,),
        in_specs=[pl.BlockSpec((BLOCK, cw), cur),
                  pl.BlockSpec((BLOCK, kw), prev), pl.BlockSpec((BLOCK, kw), cur),
                  pl.BlockSpec((BLOCK, kw), prev), pl.BlockSpec((BLOCK, kw), cur),
                  pl.BlockSpec(sink_rows.shape, lambda n: (0, 0))],
        out_specs=pl.BlockSpec((BLOCK, cw), cur),
        out_shape=_sds((t, cw), BF16),
        compiler_params=_params(("parallel",), BLOCK * (cw + 4 * kw) * 2 + BLOCK * cw * 2, 8 << 20),
    )(qn, kn, kn, vb, vb, sink_rows)


def _attn_bwd(qn, kn, vb, sink_rows, do):
    t, cw = qn.shape
    kw = kn.shape[1]
    nkv = kw // HEAD_DIM
    nq = nkv * GROUP

    def body(q_ref, kp_ref, kc_ref, vp_ref, vc_ref, sink_ref, do_ref,
             dq_ref, dkc_ref, dkp_ref, dvc_ref, dvp_ref, dsink_ref):
        n = pl.program_id(0)
        valid = _attn_mask(n)
        qv, dov = q_ref[...], do_ref[...]
        kp, kc, vp, vc = kp_ref[...], kc_ref[...], vp_ref[...], vc_ref[...]
        dqs, dks, dvs, dsinks = [], [], [], []
        for h in range(nkv):
            hs = slice(h * HEAD_DIM, (h + 1) * HEAD_DIM)
            k2 = jnp.concatenate([kp[:, hs], kc[:, hs]], axis=0)
            v2 = jnp.concatenate([vp[:, hs], vc[:, hs]], axis=0)
            q4 = _stack_heads(qv, h)
            dob = _stack_heads(dov, h).astype(BF16)
            pn, psink = _softmax_with_sink(q4, k2, sink_ref, h, valid)
            dpn = lax.dot_general(v2, dob, NT, preferred_element_type=F32)
            dvs.append(jnp.dot(pn.astype(BF16), dob, preferred_element_type=F32))
            delta = jnp.sum(pn * dpn, axis=0, keepdims=True)
            ds = (pn * (dpn - delta) * ATTN_SCALE).astype(BF16)
            dks.append(jnp.dot(ds, q4, preferred_element_type=F32))
            dq4 = lax.dot_general(ds, k2, TN, preferred_element_type=F32)
            dsink4 = -psink * delta
            for g in range(GROUP):
                dqs.append(dq4[g * BLOCK:(g + 1) * BLOCK])
                dsinks.append(jnp.broadcast_to(jnp.sum(dsink4[:, g * BLOCK:(g + 1) * BLOCK], axis=1, keepdims=True), (1, LANES)))
        dq_ref[...] = jnp.concatenate(dqs, axis=-1)
        dkp_ref[...] = jnp.concatenate([d[:BLOCK] for d in dks], axis=-1)
        dkc_ref[...] = jnp.concatenate([d[BLOCK:] for d in dks], axis=-1)
        dvp_ref[...] = jnp.concatenate([d[:BLOCK] for d in dvs], axis=-1)
        dvc_ref[...] = jnp.concatenate([d[BLOCK:] for d in dvs], axis=-1)

        @pl.when(n == 0)
        def _():
            dsink_ref[...] = jnp.zeros_like(dsink_ref)

        dsink_ref[...] += jnp.concatenate(dsinks, axis=0)

    cur = lambda n: (n, 0)
    prev = lambda n: (jnp.maximum(n - 1, 0), 0)
    kspec = pl.BlockSpec((BLOCK, kw), cur)
    return pl.pallas_call(
        body, name="attn_bwd", grid=(t // BLOCK,),
        in_specs=[pl.BlockSpec((BLOCK, cw), cur),
                  pl.BlockSpec((BLOCK, kw), prev), kspec,
                  pl.BlockSpec((BLOCK, kw), prev), kspec,
                  pl.BlockSpec(sink_rows.shape, lambda n: (0, 0)),
                  pl.BlockSpec((BLOCK, cw), cur)],
        out_specs=[pl.BlockSpec((BLOCK, cw), cur), kspec, kspec, kspec, kspec,
                   pl.BlockSpec((nq, LANES), lambda n: (0, 0))],
        out_shape=[_sds((t, cw), F32)] + [_sds((t, kw), F32)] * 4 + [_sds((nq, LANES), F32)],
        compiler_params=_params(("arbitrary",), BLOCK * (cw + 4 * kw) * 2 + 2 * BLOCK * cw * 4 + 4 * BLOCK * kw * 4, 12 << 20),
    )(qn, kn, kn, vb, vb, sink_rows, do)


def _mix_out(ca, o, woc, woa, proj):
    t, cw = ca.shape
    nb = woc.shape[2]
    d = N_DEV * nb
    tm = min(t, 1024)
    ga0 = (3 * cw + cw + 2 * (cw // 4)) // nb

    def body(ca_ref, o_ref, woc_ref, woa_ref, ga_ref, gb_ref, m_ref, ya_ref, yb_ref):
        ya = jnp.dot(ca_ref[...], woc_ref[...], preferred_element_type=F32)
        yb = jnp.dot(o_ref[...], woa_ref[...], preferred_element_type=F32)
        ya_ref[...] = ya.astype(BF16)
        yb_ref[...] = yb.astype(BF16)
        m_ref[...] = (_sigmoid(ga_ref[...].astype(F32)) * ya + _sigmoid(gb_ref[...].astype(F32)) * yb).astype(BF16)

    act = pl.BlockSpec((tm, cw), lambda i, j: (i, 0))
    wsp = pl.BlockSpec((None, cw, nb), lambda i, j: (j, 0, 0))
    osp = pl.BlockSpec((tm, nb), lambda i, j: (i, j))
    blocks = 2 * tm * cw * 2 + 2 * cw * nb * 2 + 2 * tm * nb * 4 + 3 * tm * nb * 2
    return pl.pallas_call(
        body, name="mix_out", grid=(t // tm, N_DEV),
        in_specs=[act, act, wsp, wsp,
                  pl.BlockSpec((tm, nb), lambda i, j: (i, ga0 + j)),
                  pl.BlockSpec((tm, nb), lambda i, j: (i, ga0 + N_DEV + j))],
        out_specs=[osp, osp, osp],
        out_shape=[_sds((t, d), BF16)] * 3,
        compiler_params=_params(("parallel", "parallel"), blocks, 6 * tm * nb * 4),
    )(ca, o, woc, woa, proj, proj)


def _mix_residual(merged, wo, x):
    t, d = x.shape
    tm = min(t, 512)

    def epilogue(acc, ins, outs):
        outs[0][...] = ins[2][...] + acc

    row = pl.BlockSpec((tm, d), lambda i, j, k: (i, 0))
    return _fused("mix_residual", (t // tm, 1, 1),
                  [(merged, row), (wo, pl.BlockSpec((d, d), lambda i, j, k: (0, 0))), (x, row)],
                  [(_sds((t, d), F32), row)], [(0, 1, NN)], epilogue, temp_bytes=2 * tm * d * 4)[0]


def _mix_bwd_gates(dx, wo, ya, yb, proj, cw):
    t, d = dx.shape
    tm = min(t, 512)
    tn = min(d, 512)
    ga0 = (4 * cw + 2 * (cw // 4)) // tn

    def epilogue(acc, ins, outs):
        sa, sb = _sigmoid(ins[4][...].astype(F32)), _sigmoid(ins[5][...].astype(F32))
        outs[0][...] = (acc * sa).astype(BF16)
        outs[1][...] = (acc * sb).astype(BF16)
        outs[2][0] = (acc * ins[2][...].astype(F32) * sa * (1.0 - sa)).astype(BF16)
        outs[2][1] = (acc * ins[3][...].astype(F32) * sb * (1.0 - sb)).astype(BF16)

    blk = pl.BlockSpec((tm, tn), lambda i, j, k: (i, j))
    return _fused("mix_bwd_gates", (t // tm, d // tn, 1),
                  [(dx, pl.BlockSpec((tm, d), lambda i, j, k: (i, 0))),
                   (wo, pl.BlockSpec((tn, d), lambda i, j, k: (j, 0))),
                   (ya, blk), (yb, blk),
                   (proj, pl.BlockSpec((tm, tn), lambda i, j, k: (i, ga0 + j))),
                   (proj, pl.BlockSpec((tm, tn), lambda i, j, k: (i, ga0 + d // tn + j)))],
                  [(_sds((t, d), BF16), blk), (_sds((t, d), BF16), blk),
                   (_sds((2, t, d), BF16), pl.BlockSpec((2, tm, tn), lambda i, j, k: (0, i, j)))],
                  [(0, 1, NT)], epilogue, temp_bytes=8 * tm * tn * 4)


def _tn_matmul(name, a, b, tm, out_dtype=BF16):
    t, m = a.shape
    n = b.shape[1]
    tk = min(t, 512)

    def epilogue(acc, ins, outs):
        outs[0][...] = acc.astype(out_dtype)

    return _fused(name, (m // tm, 1, t // tk),
                  [(a, pl.BlockSpec((tk, tm), lambda i, j, k: (k, i))),
                   (b, pl.BlockSpec((tk, n), lambda i, j, k: (k, 0)))],
                  [(_sds((m, n), out_dtype), pl.BlockSpec((tm, n), lambda i, j, k: (i, 0)))],
                  [(0, 1, TN)], epilogue, nk=t // tk, acc_shape=(tm, n), temp_bytes=tm * n * 4)[0]


def _out_proj_bwd_act(dya, dyb, woc, woa, deps=()):
    t, d = dya.shape
    kdim, nb = woc.shape[1], woc.shape[2]
    tm = min(t, 512)

    def body(dya_ref, dyb_ref, woc_ref, woa_ref, *rest):
        for dy_ref, w_ref, o_ref in ((dya_ref, woc_ref, rest[-2]), (dyb_ref, woa_ref, rest[-1])):
            total = None
            for j in range(N_DEV):
                part = lax.dot_general(dy_ref[:, j * nb:(j + 1) * nb], w_ref[j], NT, preferred_element_type=F32)
                total = part if total is None else total + part
            o_ref[...] = total

    row = pl.BlockSpec((tm, d), lambda i: (i, 0))
    wsp = pl.BlockSpec((N_DEV, kdim, nb), lambda i: (0, 0, 0))
    osp = pl.BlockSpec((tm, kdim), lambda i: (i, 0))
    blocks = 2 * tm * d * 2 + 2 * N_DEV * kdim * nb * 2 + 2 * tm * kdim * 4
    return pl.pallas_call(
        body, name="mix_bwd_dca_do", grid=(t // tm,),
        in_specs=[row, row, wsp, wsp] + [_ANY] * len(deps), out_specs=[osp, osp],
        out_shape=[_sds((t, kdim), F32)] * 2,
        compiler_params=_params(("parallel",), blocks, 4 * tm * kdim * 4),
    )(dya, dyb, woc, woa, *deps)


def _out_proj_bwd_w(ca, o, dya, dyb, nb):
    t, kdim = ca.shape

    def body(ca_ref, o_ref, dya_ref, dyb_ref, dwoc_ref, dwoa_ref):
        dwoc_ref[...] = lax.dot_general(ca_ref[...], dya_ref[...], TN, preferred_element_type=F32).astype(BF16)
        dwoa_ref[...] = lax.dot_general(o_ref[...], dyb_ref[...], TN, preferred_element_type=F32).astype(BF16)

    act = pl.BlockSpec((t, kdim), lambda j: (0, 0))
    col = pl.BlockSpec((t, nb), lambda j: (0, j))
    osp = pl.BlockSpec((None, kdim, nb), lambda j: (j, 0, 0))
    blocks = 2 * t * kdim * 2 + 2 * t * nb * 2 + 2 * kdim * nb * 2
    return pl.pallas_call(
        body, name="mix_bwd_dwoc_dwoa", grid=(N_DEV,),
        in_specs=[act, act, col, col], out_specs=[osp, osp],
        out_shape=[_sds((N_DEV, kdim, nb), BF16)] * 2,
        compiler_params=_params(("parallel",), blocks, 4 * kdim * nb * 4),
    )(ca, o, dya, dyb)


def _proj_bwd_act(dproj, w_in, deps=()):
    t, n = dproj.shape
    d, nb = w_in.shape[2], w_in.shape[3]
    tm = min(t, 512)

    def epilogue(acc, ins, outs):
        outs[0][...] = acc

    def products(ins):
        return (lax.dot_general(ins[0][:, 0:nb], ins[1][0], NT, preferred_element_type=F32)
                + lax.dot_general(ins[0][:, nb:2 * nb], ins[1][1], NT, preferred_element_type=F32))

    return _fused("mix_bwd_dh", (t // tm, 1, 4),
                  [(dproj, pl.BlockSpec((tm, 2 * nb), lambda i, j, k: (i, k))),
                   (w_in, pl.BlockSpec((None, 2, d, nb), lambda i, j, k: (k, 0, 0, 0)))],
                  [(_sds((t, d), F32), pl.BlockSpec((tm, d), lambda i, j, k: (i, 0)))],
                  products, epilogue, nk=4, acc_shape=(tm, d), temp_bytes=tm * d * 4, deps=deps)[0]


def _proj_bwd_w(h, dproj):
    t, d = h.shape
    nb = dproj.shape[1] // N_DEV
    tk = min(t, 512)
    nk = t // tk

    def body(h_ref, dp_ref, o_ref, acc0, acc1):
        k = pl.program_id(1)

        @pl.when(k == 0)
        def _():
            acc0[...] = jnp.zeros_like(acc0)
            acc1[...] = jnp.zeros_like(acc1)

        hv = h_ref[...]
        acc0[...] += lax.dot_general(hv, dp_ref[:, 0:nb], TN, preferred_element_type=F32)
        acc1[...] += lax.dot_general(hv, dp_ref[:, nb:2 * nb], TN, preferred_element_type=F32)

        @pl.when(k == nk - 1)
        def _():
            o_ref[0] = acc0[...].astype(BF16)
            o_ref[1] = acc1[...].astype(BF16)

    blocks = tk * d * 2 + tk * 2 * nb * 2 + 2 * d * nb * 2
    return pl.pallas_call(
        body, name="mix_bwd_dwin", grid=(4, nk),
        in_specs=[pl.BlockSpec((tk, d), lambda j, k: (k, 0)),
                  pl.BlockSpec((tk, 2 * nb), lambda j, k: (k, j))],
        out_specs=pl.BlockSpec((None, 2, d, nb), lambda j, k: (j, 0, 0, 0)),
        out_shape=_sds((4, 2, d, nb), BF16),
        scratch_shapes=[pltpu.VMEM((d, nb), F32), pltpu.VMEM((d, nb), F32)],
        compiler_params=_params(("parallel", "arbitrary"), blocks, 2 * d * nb * 4),
    )(h, dproj)


def _adamw_math(w, g, m, v):
    m = ADAM_B1 * m + (1.0 - ADAM_B1) * g
    v = ADAM_B2 * v + (1.0 - ADAM_B2) * (g * g)
    m_hat = m / (1.0 - ADAM_B1 ** ADAM_STEP)
    v_hat = v / (1.0 - ADAM_B2 ** ADAM_STEP)
    delta = -ADAM_LR * (m_hat / (jnp.sqrt(v_hat) + ADAM_EPS) + ADAM_WD * w)
    return delta, m, v


def _adamw(name, parts, w, m, v, tr):
    r, c = w.shape

    def body(p_ref, w_ref, m_ref, v_ref, g_out, d_out, m_out, v_out):
        g = p_ref[0].astype(F32)
        for s in range(1, N_DEV):
            g = g + p_ref[s].astype(F32)
        delta, mn, vn = _adamw_math(w_ref[...], g, m_ref[...], v_ref[...])
        g_out[...] = g
        d_out[...] = delta
        m_out[...] = mn
        v_out[...] = vn

    blk = pl.BlockSpec((tr, c), lambda i: (i, 0))
    blocks = N_DEV * tr * c * parts.dtype.itemsize + 7 * tr * c * 4
    return pl.pallas_call(
        body, name=name, grid=(r // tr,),
        in_specs=[pl.BlockSpec((N_DEV, tr, c), lambda i: (0, i, 0)), blk, blk, blk],
        out_specs=[blk] * 4, out_shape=[_sds((r, c), F32)] * 4,
        compiler_params=_params(("parallel",), blocks, 6 * tr * c * 4),
    )(parts, w, m, v)


def _adamw_chips(name, chip, own, landed, w, m, v, tr):
    r, c = w.shape

    def body(chip_ref, own_ref, land_ref, w_ref, m_ref, v_ref, g_out, d_out, m_out, v_out):
        mine = own_ref[...].astype(F32)
        g = jnp.zeros((tr, c), F32)
        for k in range(4):
            g = g + jnp.where(chip_ref[0] == k, mine, land_ref[k].astype(F32))
        delta, mn, vn = _adamw_math(w_ref[...], g, m_ref[...], v_ref[...])
        g_out[...] = g
        d_out[...] = delta
        m_out[...] = mn
        v_out[...] = vn

    blk = pl.BlockSpec((tr, c), lambda i, chip_ref: (i, 0))
    grid_spec = pltpu.PrefetchScalarGridSpec(
        num_scalar_prefetch=1, grid=(r // tr,),
        in_specs=[pl.BlockSpec((None, tr, c), lambda i, chip_ref: (chip_ref[0], i, 0)),
                  pl.BlockSpec((4, tr, c), lambda i, chip_ref: (0, i, 0)), blk, blk, blk],
        out_specs=[blk] * 4)
    blocks = 5 * tr * c * 2 + 7 * tr * c * 4
    return pl.pallas_call(
        body, name=name, grid_spec=grid_spec, out_shape=[_sds((r, c), F32)] * 4,
        compiler_params=_params(("parallel",), blocks, 6 * tr * c * 4),
    )(chip, own, landed, w, m, v)


def _rope_tables(t):
    half = ROT_DIM // 2
    inv_freq = 1.0 / (ROPE_THETA ** (jnp.arange(0, ROT_DIM, 2, dtype=F32) / ROT_DIM))
    ang = jnp.arange(t, dtype=F32)[:, None] * inv_freq[None, :]
    cos, sin = jnp.cos(ang), jnp.sin(ang)
    ones = jnp.ones((t, HEAD_DIM - ROT_DIM), F32)
    zeros = jnp.zeros((t, HEAD_DIM - half), F32)
    c = jnp.concatenate([cos, cos, ones], axis=1)
    sa = jnp.concatenate([-sin, zeros], axis=1)
    sb = jnp.concatenate([jnp.zeros((t, half), F32), sin, jnp.zeros((t, HEAD_DIM - ROT_DIM), F32)], axis=1)
    return tuple(jnp.tile(a, (1, LANES // HEAD_DIM)) for a in (c, sa, sb))


def _pad_rows(a, rows=8):
    return jnp.pad(a, ((0, rows - a.shape[0]), (0, 0)))


def kernel(x, g_ffn1, w_gu1, w_down1, g_mix, w_in, conv_w, q_norm_g, k_norm_g, sinks, w_out_conv, w_out_attn, w_o, g_ffn2, w_gu2, w_down2, loss_target, m_g_ffn1, m_w_gu1, m_w_down1, m_g_mix, m_w_in, m_conv_w, m_q_norm_g, m_k_norm_g, m_sinks, m_w_out_conv, m_w_out_attn, m_w_o, m_g_ffn2, m_w_gu2, m_w_down2, v_g_ffn1, v_w_gu1, v_w_down1, v_g_mix, v_w_in, v_conv_w, v_q_norm_g, v_k_norm_g, v_sinks, v_w_out_conv, v_w_out_attn, v_w_o, v_g_ffn2, v_w_gu2, v_w_down2):
    t, d = x.shape[1], x.shape[2]
    cw = d // 2
    kw = cw // GROUP
    nq = cw // HEAD_DIM
    xs, target = x.reshape(t, d), loss_target.reshape(t, d)
    me = 4 * lax.axis_index("x") + 2 * lax.axis_index("y") + lax.axis_index("c")

    big = {"w_gu1": w_gu1, "w_down1": w_down1, "w_in": w_in, "w_out_conv": w_out_conv,
           "w_out_attn": w_out_attn, "w_o": w_o, "w_gu2": w_gu2, "w_down2": w_down2}
    big_m = {"w_gu1": m_w_gu1, "w_down1": m_w_down1, "w_in": m_w_in, "w_out_conv": m_w_out_conv,
             "w_out_attn": m_w_out_attn, "w_o": m_w_o, "w_gu2": m_w_gu2, "w_down2": m_w_down2}
    big_v = {"w_gu1": v_w_gu1, "w_down1": v_w_down1, "w_in": v_w_in, "w_out_conv": v_w_out_conv,
             "w_out_attn": v_w_out_attn, "w_o": v_w_o, "w_gu2": v_w_gu2, "w_down2": v_w_down2}
    names = list(big)

    tiles = {"w_gu1": 256, "w_gu2": 256, "w_in": 256, "w_down1": 176, "w_down2": 176,
             "w_out_conv": 1024, "w_out_attn": 1024, "w_o": 128}

    def row_tile(n):
        r = big[n].shape[1]
        return tiles[n] if r % tiles[n] == 0 else r

    def add_tile(n):
        r, c = big[n].shape[1], big[n].shape[2]
        while r * c * 2 > (3 << 20) and r % 32 == 0:
            r //= 2
        return r

    me_arr = me.astype(jnp.int32).reshape(1)
    sources = [(n, big[n][0], BF16, row_tile(n)) for n in names] + [("conv_w", _pad_rows(conv_w[0]), F32, 8)]
    issue_order = [0, 1, 2, 8, 3, 4, 5, 6, 7]
    first = _place_shard("place_" + names[0], sources[0][1], BF16, me_arr, sources[0][3])
    started = [_gather_start("gather_start_first", [first])]
    rest = [_place_shard("place_" + sources[i][0], sources[i][1], sources[i][2], me_arr, sources[i][3],
                         deps=(started[0][3],)) for i in issue_order[1:]]
    started.append(_gather_start("gather_start_rest", rest))
    where = {0: (0, 0)}
    where.update({i: (1, p) for p, i in enumerate(issue_order[1:])})

    def fetch(tag, idxs, after):
        call = where[idxs[0]][0]
        send, recv, stacks, _ = started[call]
        positions = [where[i][1] for i in idxs]
        got = _gather_wait("gather_wait_" + tag, positions, send, recv, [stacks[p] for p in positions], after)
        return _forward_to_sibling("gather_forward_" + tag, got)

    rope_tabs = _rope_tables(t)
    gq = jnp.tile(q_norm_g, (1, nq))
    gk = jnp.tile(k_norm_g, (1, nq // GROUP))
    sink_rows = jnp.broadcast_to(sinks[0][:, None], (nq, LANES))

    wts = {}
    h1 = _rms_fwd("ffn1_norm", xs, g_ffn1)
    wts["w_gu1"], = fetch("gu1", [0], started[1][3])
    gu1, a1 = _ffn_up("ffn1_up", h1, wts["w_gu1"])
    wts["w_down1"], = fetch("down1", [1], a1)
    wd1 = wts["w_down1"].reshape(-1, d)
    x1 = _ffn_down("ffn1_down", a1, wd1, xs)
    h2 = _rms_fwd("mix_norm", x1, g_mix)
    wts["w_in"], conv_land = fetch("in", [2, 8], h2)
    w_in_full = wts["w_in"].reshape(4, 2, d, -1)
    conv_full = jnp.transpose(conv_land, (1, 0, 2)).reshape(8, cw)
    proj = _proj(h2, w_in_full)
    ca = _conv_fwd(proj, conv_full)
    qn, kn, vb = _qk_prep(proj, gq, gk, rope_tabs, cw, kw)
    o = _attn_fwd(qn, kn, vb, sink_rows)
    wts["w_out_conv"], wts["w_out_attn"] = fetch("out", [3, 4], o)
    merged, ya, yb = _mix_out(ca, o, wts["w_out_conv"], wts["w_out_attn"], proj)
    wts["w_o"], = fetch("o", [5], merged)
    wo = wts["w_o"].reshape(d, d)
    x2 = _mix_residual(merged, wo, x1)
    h3 = _rms_fwd("ffn2_norm", x2, g_ffn2)
    wts["w_gu2"], = fetch("gu2", [6], h3)
    gu2, a2 = _ffn_up("ffn2_up", h3, wts["w_gu2"])
    wts["w_down2"], = fetch("down2", [7], a2)
    wd2 = wts["w_down2"].reshape(-1, d)
    y = _ffn_down("ffn2_down", a2, wd2, x2)
    dy, sq = _loss_dy(y, target)
    loss = lax.psum(sq[0, 0] * (0.5 / d), ("x", "y", "c"))

    core = lax.axis_index("c").astype(jnp.int32).reshape(1)
    chip = (2 * lax.axis_index("x") + lax.axis_index("y")).astype(jnp.int32).reshape(1)
    def pair_start(tag, group, grads):
        stacks = [grads[n].reshape((4, 2) + big[n].shape[1:]) for n in group]
        lands = [lax.empty((4,) + big[n].shape[1:], BF16) for n in group]
        return _pair_start("rs_pair_start_" + tag, stacks, lands)

    def chip_start(tag, group, pending, after):
        send, recv, stacks, lands, _ = pending
        stacks, lands = _pair_wait("rs_pair_wait_" + tag, send, recv, stacks, lands, after)
        parts = [_pair_add("rs_pair_add_" + n, st, ld, core, add_tile(n)) for n, st, ld in zip(group, stacks, lands)]
        lands2 = [lax.empty((4,) + big[n].shape[1:], BF16) for n in group]
        return _chip_start("rs_chip_start_" + tag, parts, lands2)

    group_a, group_b, group_c = ["w_down2", "w_gu2"], ["w_o", "w_out_conv", "w_out_attn"], ["w_in"]
    group_d, group_e = ["w_down1"], ["w_gu1"]
    g = {}
    dgu2, a2 = _ffn_bwd_act("ffn2_bwd_act", dy, wd2, gu2)
    g["w_down2"] = _ffn_bwd_dwd("ffn2_bwd_dwd", a2, dy)
    g["w_gu2"] = _ffn_bwd_dwgu("ffn2_bwd_dwgu", h3, dgu2)
    pend_a = pair_start("a", group_a, g)
    dh3 = _ffn_bwd_dh("ffn2_bwd_dh", dgu2, wts["w_gu2"], deps=(pend_a[4],))
    ring_a = chip_start("a", group_a, pend_a, dh3)
    dx2, dg_ffn2 = _rms_bwd("ffn2_bwd_rms", x2, g_ffn2, dh3, dy, deps=(ring_a[4],))

    dya, dyb, dgates = _mix_bwd_gates(dx2, wo, ya, yb, proj, cw)
    g["w_o"] = _tn_matmul("mix_bwd_dwo", merged, dx2, min(d, 1024))
    g["w_out_conv"], g["w_out_attn"] = _out_proj_bwd_w(ca, o, dya, dyb, d // N_DEV)
    pend_b = pair_start("b", group_b, g)
    dca, do = _out_proj_bwd_act(dya, dyb, wts["w_out_conv"], wts["w_out_attn"], deps=(pend_b[4],))
    ring_b = chip_start("b", group_b, pend_b, do)
    d3, dconv_w = _conv_bwd(proj, conv_full, dca, deps=(ring_b[4],))
    dq, dkc, dkp, dvc, dvp, dsink = _attn_bwd(qn, kn, vb, sink_rows, do)
    dqkv, dgq, dgk = _qk_prep_bwd(proj, gq, gk, rope_tabs, dq, dkc, dkp, dvc, dvp, cw, kw)
    dproj = jnp.concatenate([d3[0], d3[1], d3[2], dqkv, dgates[0], dgates[1]], axis=1)
    g["w_in"] = _proj_bwd_w(h2, dproj)
    pend_c = pair_start("c", group_c, g)
    dh2 = _proj_bwd_act(dproj, w_in_full, deps=(pend_c[4],))
    ring_c = chip_start("c", group_c, pend_c, dh2)
    dx1, dg_mix = _rms_bwd("mix_bwd_rms", x1, g_mix, dh2, dx2, deps=(ring_c[4],))

    dgu1, a1 = _ffn_bwd_act("ffn1_bwd_act", dx1, wd1, gu1)
    g["w_down1"] = _ffn_bwd_dwd("ffn1_bwd_dwd", a1, dx1)
    pend_d = pair_start("d", group_d, g)
    g["w_gu1"] = _ffn_bwd_dwgu("ffn1_bwd_dwgu", h1, dgu1, deps=(pend_d[4],))
    ring_d = chip_start("d", group_d, pend_d, g["w_gu1"])
    pend_e = pair_start("e", group_e, g)

    big_out = {}

    def finish(tag, group, ring, after):
        send, recv, parts, lands2, _ = ring
        parts, lands2 = _chip_wait("rs_chip_wait_" + tag, send, recv, parts, lands2, after)
        for n, own, landed in zip(group, parts, lands2):
            res = _adamw_chips("adamw_" + n, chip, own, landed, big[n][0], big_m[n][0], big_v[n][0], row_tile(n))
            big_out[n] = [a[None] for a in res]
            after = res[0]
        return after

    after = finish("a", group_a, ring_a, pend_e[4])
    ring_e = chip_start("e", group_e, pend_e, after)
    dh1 = _ffn_bwd_dh("ffn1_bwd_dh", dgu1, wts["w_gu1"], deps=(ring_e[4],))
    grad_x, dg_ffn1 = _rms_bwd("ffn1_bwd_rms", xs, g_ffn1, dh1, dx1)
    after = grad_x
    for tag, group, ring in (("b", group_b, ring_b), ("c", group_c, ring_c), ("d", group_d, ring_d), ("e", group_e, ring_e)):
        after = finish(tag, group, ring, after)

    small = {"g_ffn1": dg_ffn1[0:1], "g_mix": dg_mix[0:1], "g_ffn2": dg_ffn2[0:1],
             "q_norm_g": dgq[0:1, :HEAD_DIM], "k_norm_g": dgk[0:1, :HEAD_DIM], "sinks": dsink[:, 0][None],
             "conv_w": dconv_w[0:CONV_K].reshape(1, -1)}
    small_w = {"g_ffn1": g_ffn1, "g_mix": g_mix, "g_ffn2": g_ffn2, "q_norm_g": q_norm_g, "k_norm_g": k_norm_g,
               "sinks": sinks, "conv_w": None}
    small_m = {"g_ffn1": m_g_ffn1, "g_mix": m_g_mix, "g_ffn2": m_g_ffn2, "q_norm_g": m_q_norm_g,
               "k_norm_g": m_k_norm_g, "sinks": m_sinks, "conv_w": m_conv_w}
    small_v = {"g_ffn1": v_g_ffn1, "g_mix": v_g_mix, "g_ffn2": v_g_ffn2, "q_norm_g": v_q_norm_g,
               "k_norm_g": v_k_norm_g, "sinks": v_sinks, "conv_w": v_conv_w}
    snames = list(small)
    widths = [small[n].shape[1] for n in snames]
    total = sum(widths)
    rows = -(-total // LANES)
    rows = -(-rows // 8) * 8

    def pack(vals):
        flat = jnp.concatenate([v.reshape(1, -1) for v in vals], axis=1)
        return jnp.pad(flat, ((0, 0), (0, rows * LANES - total))).reshape(rows, LANES)

    csh = cw // N_DEV

    def place_conv(local, fill):
        full = jnp.full((CONV_K, cw), fill, F32)
        return lax.dynamic_update_slice(full, local, (0, me * csh)).reshape(1, -1)

    pw = pack([small_w[n] if n != "conv_w" else place_conv(conv_w[0], 0.0) for n in snames])
    pm = pack([small_m[n] if n != "conv_w" else place_conv(m_conv_w[0], 0.0) for n in snames])
    pv = pack([small_v[n] if n != "conv_w" else place_conv(v_conv_w[0], 1.0) for n in snames])
    parts = _exchange("gather_small_grads", [pack([small[n] for n in snames])], gather=True, deps=(after,))[0]
    sg, sd, sm, sv = [a.reshape(1, -1) for a in _adamw("adamw_small", parts, pw, pm, pv, rows)]

    def unpack(flat, n):
        off = sum(widths[:snames.index(n)])
        piece = flat[:, off:off + widths[snames.index(n)]]
        if n == "conv_w":
            piece = lax.dynamic_slice(piece.reshape(CONV_K, cw), (0, me * csh), (CONV_K, csh))[None]
        return piece

    order = ["g_ffn1", "w_gu1", "w_down1", "g_mix", "w_in", "conv_w", "q_norm_g", "k_norm_g", "sinks",
             "w_out_conv", "w_out_attn", "w_o", "g_ffn2", "w_gu2", "w_down2"]
    outs = [loss, grad_x[None]]
    for idx, flat in enumerate((sg, sd, sm, sv)):
        for n in order:
            outs.append(big_out[n][idx] if n in big_out else unpack(flat, n))
    return tuple(outs)
```

```python
import functools

import jax
import jax.numpy as jnp
from jax import lax
from jax.experimental import pallas as pl
from jax.experimental.pallas import tpu as pltpu

F32 = jnp.float32
BF16 = jnp.bfloat16

N_DEV = 8
HEAD_DIM = 64
GROUP = 4
BLOCK = 128
ROT_DIM = 16
ROPE_THETA = 500000.0
RMS_EPS = 1e-6
NEG_INF = -1e30
ATTN_SCALE = HEAD_DIM ** -0.5
CONV_K = 3
LANES = 128
MXU_COLS = 256
VMEM_BYTES_V7X = 64 * 1024 * 1024
VMEM_CAP = VMEM_BYTES_V7X - 6 * 1024 * 1024

ADAM_LR = 0.001
ADAM_B1 = 0.9
ADAM_B2 = 0.999
ADAM_EPS = 1e-08
ADAM_WD = 0.01
ADAM_STEP = 10

NN = (((1,), (0,)), ((), ()))
NT = (((1,), (1,)), ((), ()))
TN = (((0,), (0,)), ((), ()))

MESH = pl.DeviceIdType.MESH


def _nbytes(shape, dtype):
    n = 1
    for s in shape:
        if s is not None:
            n *= s
    return n * jnp.dtype(dtype).itemsize


def _params(semantics, block_bytes, temp_bytes):
    assert 2 * block_bytes + temp_bytes <= VMEM_CAP, (block_bytes, temp_bytes)
    return pltpu.CompilerParams(dimension_semantics=semantics, vmem_limit_bytes=VMEM_CAP)


def _fused(name, grid, ins, outs, dots, epilogue, *, nk=1, acc_shape=None, temp_bytes=0,
           semantics=("parallel", "parallel", "arbitrary"), deps=()):
    n_in, n_out = len(ins), len(outs)
    n_dep = len(deps)

    def body(*refs):
        in_refs, out_refs = refs[:n_in], refs[n_in + n_dep:n_in + n_dep + n_out]
        scratch = refs[n_in + n_dep + n_out:]

        def products():
            if callable(dots):
                return dots(in_refs)
            total = None
            for ai, bi, contract in dots:
                a, b = in_refs[ai][...], in_refs[bi][...]
                a = a if a.dtype == BF16 else a.astype(BF16)
                b = b if b.dtype == BF16 else b.astype(BF16)
                p = lax.dot_general(a, b, contract, preferred_element_type=F32)
                total = p if total is None else total + p
            return total

        if nk == 1:
            epilogue(products() if dots else None, in_refs, out_refs)
        else:
            acc = scratch[0]
            k = pl.program_id(2)

            @pl.when(k == 0)
            def _():
                acc[...] = jnp.zeros_like(acc)

            acc[...] += products()

            @pl.when(k == nk - 1)
            def _():
                epilogue(acc[...], in_refs, out_refs)

    block_bytes = sum(_nbytes(spec.block_shape, a.dtype) for a, spec in ins)
    block_bytes += sum(_nbytes(spec.block_shape, s.dtype) for s, spec in outs)
    scratch_shapes = []
    if nk > 1:
        scratch_shapes.append(pltpu.VMEM(acc_shape, F32))
        temp_bytes += _nbytes(acc_shape, F32)
    res = pl.pallas_call(
        body, name=name, grid=grid,
        in_specs=[spec for _, spec in ins] + [pl.BlockSpec(memory_space=pl.ANY)] * n_dep,
        out_specs=[spec for _, spec in outs],
        out_shape=[s for s, _ in outs],
        scratch_shapes=scratch_shapes,
        compiler_params=_params(semantics, block_bytes, temp_bytes),
    )(*[a for a, _ in ins], *deps)
    return res


def _sds(shape, dtype):
    return jax.ShapeDtypeStruct(shape, dtype)


def _sigmoid(x):
    return jax.nn.sigmoid(x)


def _exchange(name, arrays, gather, deps=()):
    n = len(arrays)
    out_shapes = [((N_DEV,) + a.shape) if gather else a.shape for a in arrays]

    def body(*refs):
        srcs, dsts = refs[:n], refs[n + len(deps):2 * n + len(deps)]
        send_sems, recv_sems, local_sems = refs[2 * n + len(deps):]
        x, y, c = lax.axis_index("x"), lax.axis_index("y"), lax.axis_index("c")
        me = 4 * x + 2 * y + c
        copies = []
        for w in range(n):
            own = srcs[w] if gather else srcs[w].at[me]
            local = pltpu.make_async_copy(own, dsts[w].at[me], local_sems.at[w])
            local.start()
            copies.append(local)
            for k in range(1, N_DEV):
                px = (1 - x) if (k & 4) else x
                py = (1 - y) if (k & 2) else y
                pc = (1 - c) if (k & 1) else c
                peer = 4 * px + 2 * py + pc
                cp = pltpu.make_async_remote_copy(
                    src_ref=srcs[w] if gather else srcs[w].at[peer],
                    dst_ref=dsts[w].at[me],
                    send_sem=send_sems.at[w * (N_DEV - 1) + k - 1],
                    recv_sem=recv_sems.at[w * (N_DEV - 1) + k - 1],
                    device_id=(px, py, pc), device_id_type=MESH)
                cp.start()
                copies.append(cp)
        for cp in copies:
            cp.wait()

    hbm = pl.BlockSpec(memory_space=pltpu.HBM)
    return pl.pallas_call(
        body, name=name,
        in_specs=[hbm] * n + [pl.BlockSpec(memory_space=pl.ANY)] * len(deps), out_specs=[hbm] * n,
        out_shape=[_sds(s, a.dtype) for s, a in zip(out_shapes, arrays)],
        scratch_shapes=[pltpu.SemaphoreType.DMA((n * (N_DEV - 1),)),
                        pltpu.SemaphoreType.DMA((n * (N_DEV - 1),)),
                        pltpu.SemaphoreType.DMA((n,))],
    )(*arrays, *deps)


_HBM = pl.BlockSpec(memory_space=pltpu.HBM)
_SEM = pl.BlockSpec(memory_space=pltpu.SEMAPHORE)
_ANY = pl.BlockSpec(memory_space=pl.ANY)
_EFFECT = pltpu.SideEffectType.DATAFLOW_SIDE_EFFECTING
N_TARGETS = 4


def _mesh_pos():
    return lax.axis_index("x"), lax.axis_index("y"), lax.axis_index("c")


def _chip_peers(x, y, c):
    return [(1 - x, y, c), (x, 1 - y, c), (1 - x, 1 - y, c)]


def _dev_index(pos):
    return 4 * pos[0] + 2 * pos[1] + pos[2]


def _hbm_like(a):
    return pltpu.HBM(a.shape, a.dtype)


def _place_shard(name, w, out_dtype, me, tr, deps=()):
    r, c = w.shape
    n_dep = len(deps)

    def body(me_ref, w_ref, *rest):
        rest[n_dep][...] = w_ref[...].astype(out_dtype)

    grid_spec = pltpu.PrefetchScalarGridSpec(
        num_scalar_prefetch=1, grid=(r // tr,),
        in_specs=[pl.BlockSpec((tr, c), lambda i, me_ref: (i, 0))] + [_ANY] * n_dep,
        out_specs=pl.BlockSpec((None, tr, c), lambda i, me_ref: (me_ref[0], i, 0)))
    return pl.pallas_call(
        body, name=name, grid_spec=grid_spec, out_shape=_sds((N_DEV, r, c), out_dtype),
        compiler_params=_params(("parallel",), tr * c * 6, tr * c * 4),
    )(me, w, *deps)


def _gather_start(name, lands):
    n = len(lands)

    def body(*refs):
        bufs = refs[:n]
        send, recv = refs[n], refs[n + 1]
        token = refs[-1]
        x, y, c = _mesh_pos()
        me = _dev_index((x, y, c))
        targets = [(x, y, 1 - c)] + _chip_peers(x, y, c)
        for w in range(n):
            for k, to in enumerate(targets):
                pltpu.make_async_remote_copy(
                    src_ref=bufs[w].at[me], dst_ref=bufs[w].at[me],
                    send_sem=send.at[N_TARGETS * w + k], recv_sem=recv.at[N_TARGETS * w + k],
                    device_id=to, device_id_type=MESH).start()
        token[...] = jnp.zeros_like(token)

    sems = pltpu.SemaphoreType.DMA((N_TARGETS * n,))
    outs = pl.pallas_call(
        body, name=name,
        in_specs=[_HBM] * n, out_specs=[_SEM, _SEM] + [_HBM] * n + [_token_spec()],
        out_shape=[sems, sems] + [_hbm_like(a) for a in lands] + [_sds((8, LANES), F32)],
        input_output_aliases={i: 2 + i for i in range(n)},
        compiler_params=pltpu.CompilerParams(has_side_effects=_EFFECT),
    )(*lands)
    return outs[0], outs[1], list(outs[2:2 + n]), outs[-1]


def _gather_wait(name, positions, send, recv, lands, after):
    m = len(positions)

    def body(*refs):
        bufs = refs[:m]
        send_sems, recv_sems = refs[m], refs[m + 1]
        x, y, c = _mesh_pos()
        me = _dev_index((x, y, c))
        sources = [(x, y, 1 - c)] + _chip_peers(x, y, c)
        for j, w in enumerate(positions):
            for k, frm in enumerate(sources):
                cp = pltpu.make_async_remote_copy(
                    src_ref=bufs[j].at[me], dst_ref=bufs[j].at[_dev_index(frm)],
                    send_sem=send_sems.at[N_TARGETS * w + k], recv_sem=recv_sems.at[N_TARGETS * w + k],
                    device_id=frm, device_id_type=MESH)
                cp.wait_send()
                cp.wait_recv()

    outs = pl.pallas_call(
        body, name=name,
        in_specs=[_HBM] * m + [_SEM, _SEM, _ANY], out_specs=[_HBM] * m,
        out_shape=[_hbm_like(a) for a in lands],
        input_output_aliases={i: i for i in range(m)},
        compiler_params=pltpu.CompilerParams(has_side_effects=_EFFECT),
    )(*lands, send, recv, after)
    return list(outs)


def _forward_to_sibling(name, lands):
    m = len(lands)

    def body(*refs):
        bufs = refs[m:2 * m]
        send_sems, recv_sems = refs[2 * m], refs[2 * m + 1]
        x, y, c = _mesh_pos()
        copies = []
        for j in range(m):
            for k, chip in enumerate(_chip_peers(x, y, c)):
                block = bufs[j].at[_dev_index(chip)]
                cp = pltpu.make_async_remote_copy(
                    src_ref=block, dst_ref=block,
                    send_sem=send_sems.at[3 * j + k], recv_sem=recv_sems.at[3 * j + k],
                    device_id=(x, y, 1 - c), device_id_type=MESH)
                cp.start()
                copies.append(cp)
        for cp in copies:
            cp.wait()

    outs = pl.pallas_call(
        body, name=name,
        in_specs=[_HBM] * m, out_specs=[_HBM] * m,
        out_shape=[_sds(a.shape, a.dtype) for a in lands],
        input_output_aliases={i: i for i in range(m)},
        scratch_shapes=[pltpu.SemaphoreType.DMA((3 * m,)), pltpu.SemaphoreType.DMA((3 * m,))],
    )(*lands)
    return list(outs)


def _token_spec():
    return pl.BlockSpec(memory_space=pltpu.VMEM)


def _pair_start(name, stacks, lands, deps=()):
    n = len(stacks)
    n_dep = len(deps)

    def body(*refs):
        srcs, dsts = refs[:n], refs[n:2 * n]
        send, recv = refs[2 * n + n_dep], refs[2 * n + n_dep + 1]
        token = refs[-1]
        x, y, c = _mesh_pos()
        for w in range(n):
            for chip in range(4):
                pltpu.make_async_remote_copy(
                    src_ref=srcs[w].at[chip, 1 - c], dst_ref=dsts[w].at[chip],
                    send_sem=send.at[4 * w + chip], recv_sem=recv.at[4 * w + chip],
                    device_id=(x, y, 1 - c), device_id_type=MESH).start()
        token[...] = jnp.zeros_like(token)

    sems = pltpu.SemaphoreType.DMA((4 * n,))
    outs = pl.pallas_call(
        body, name=name,
        in_specs=[_HBM] * (2 * n) + [_ANY] * n_dep, out_specs=[_SEM, _SEM] + [_HBM] * (2 * n) + [_token_spec()],
        out_shape=[sems, sems] + [_hbm_like(a) for a in stacks] + [_hbm_like(a) for a in lands] + [_sds((8, LANES), F32)],
        input_output_aliases={i: 2 + i for i in range(2 * n)},
        compiler_params=pltpu.CompilerParams(has_side_effects=_EFFECT),
    )(*stacks, *lands, *deps)
    return outs[0], outs[1], list(outs[2:2 + n]), list(outs[2 + n:2 + 2 * n]), outs[-1]


def _pair_wait(name, send, recv, stacks, lands, after):
    n = len(stacks)

    def body(*refs):
        srcs, dsts = refs[:n], refs[n:2 * n]
        send_sems, recv_sems = refs[2 * n], refs[2 * n + 1]
        x, y, c = _mesh_pos()
        for w in range(n):
            for chip in range(4):
                cp = pltpu.make_async_remote_copy(
                    src_ref=srcs[w].at[chip, 1 - c], dst_ref=dsts[w].at[chip],
                    send_sem=send_sems.at[4 * w + chip], recv_sem=recv_sems.at[4 * w + chip],
                    device_id=(x, y, 1 - c), device_id_type=MESH)
                cp.wait_send()
                cp.wait_recv()

    outs = pl.pallas_call(
        body, name=name,
        in_specs=[_HBM] * (2 * n) + [_SEM, _SEM, _ANY], out_specs=[_HBM] * (2 * n),
        out_shape=[_hbm_like(a) for a in stacks] + [_hbm_like(a) for a in lands],
        input_output_aliases={i: i for i in range(2 * n)},
        compiler_params=pltpu.CompilerParams(has_side_effects=_EFFECT),
    )(*stacks, *lands, send, recv, after)
    return list(outs[:n]), list(outs[n:])


def _pair_add(name, stack, land, core, tr):
    _, _, r, c = stack.shape

    def body(core_ref, a_ref, b_ref, o_ref):
        o_ref[...] = (a_ref[...].astype(F32) + b_ref[...].astype(F32)).astype(BF16)

    grid_spec = pltpu.PrefetchScalarGridSpec(
        num_scalar_prefetch=1, grid=(4, r // tr),
        in_specs=[pl.BlockSpec((None, None, tr, c), lambda k, i, core_ref: (k, core_ref[0], i, 0)),
                  pl.BlockSpec((None, tr, c), lambda k, i, core_ref: (k, i, 0))],
        out_specs=pl.BlockSpec((None, tr, c), lambda k, i, core_ref: (k, i, 0)))
    return pl.pallas_call(
        body, name=name, grid_spec=grid_spec, out_shape=_sds((4, r, c), BF16),
        compiler_params=_params(("parallel", "parallel"), 3 * tr * c * 2, 3 * tr * c * 4),
    )(core, stack, land)


def _chip_start(name, parts, lands):
    n = len(parts)

    def body(*refs):
        srcs, dsts = refs[:n], refs[n:2 * n]
        send, recv = refs[2 * n], refs[2 * n + 1]
        token = refs[-1]
        x, y, c = _mesh_pos()
        for w in range(n):
            for k, to in enumerate(_chip_peers(x, y, c)):
                pltpu.make_async_remote_copy(
                    src_ref=srcs[w].at[2 * to[0] + to[1]], dst_ref=dsts[w].at[2 * x + y],
                    send_sem=send.at[3 * w + k], recv_sem=recv.at[3 * w + k],
                    device_id=to, device_id_type=MESH).start()
        token[...] = jnp.zeros_like(token)

    sems = pltpu.SemaphoreType.DMA((3 * n,))
    outs = pl.pallas_call(
        body, name=name,
        in_specs=[_HBM] * (2 * n), out_specs=[_SEM, _SEM] + [_HBM] * (2 * n) + [_token_spec()],
        out_shape=[sems, sems] + [_hbm_like(a) for a in parts] + [_hbm_like(a) for a in lands] + [_sds((8, LANES), F32)],
        input_output_aliases={i: 2 + i for i in range(2 * n)},
        compiler_params=pltpu.CompilerParams(has_side_effects=_EFFECT),
    )(*parts, *lands)
    return outs[0], outs[1], list(outs[2:2 + n]), list(outs[2 + n:2 + 2 * n]), outs[-1]


def _chip_wait(name, send, recv, parts, lands, after):
    n = len(parts)

    def body(*refs):
        srcs, dsts = refs[:n], refs[n:2 * n]
        send_sems, recv_sems = refs[2 * n], refs[2 * n + 1]
        x, y, c = _mesh_pos()
        for w in range(n):
            for k, frm in enumerate(_chip_peers(x, y, c)):
                chip = 2 * frm[0] + frm[1]
                cp = pltpu.make_async_remote_copy(
                    src_ref=srcs[w].at[chip], dst_ref=dsts[w].at[chip],
                    send_sem=send_sems.at[3 * w + k], recv_sem=recv_sems.at[3 * w + k],
                    device_id=frm, device_id_type=MESH)
                cp.wait_send()
                cp.wait_recv()

    outs = pl.pallas_call(
        body, name=name,
        in_specs=[_HBM] * (2 * n) + [_SEM, _SEM, _ANY], out_specs=[_HBM] * (2 * n),
        out_shape=[_hbm_like(a) for a in parts] + [_hbm_like(a) for a in lands],
        input_output_aliases={i: i for i in range(2 * n)},
        compiler_params=pltpu.CompilerParams(has_side_effects=_EFFECT),
    )(*parts, *lands, send, recv, after)
    return list(outs[:n]), list(outs[n:])


def _row_tile(t):
    return min(t, 256)


def _rms_fwd(name, x, g):
    t, d = x.shape
    tm = _row_tile(t)

    def epilogue(_, ins, outs):
        xv = ins[0][...]
        r = lax.rsqrt(jnp.mean(xv * xv, axis=-1, keepdims=True) + RMS_EPS)
        outs[0][...] = (xv * r * ins[1][...]).astype(BF16)

    row = pl.BlockSpec((tm, d), lambda i, j, k: (i, 0))
    vec = pl.BlockSpec((1, d), lambda i, j, k: (0, 0))
    return _fused(name, (t // tm, 1, 1), [(x, row), (g, vec)], [(_sds((t, d), BF16), row)], [], epilogue,
                  temp_bytes=4 * tm * d * 4)[0]


def _rms_bwd(name, x, g, dh, resid, deps=()):
    t, d = x.shape
    tm = _row_tile(t)

    def epilogue(_, ins, outs):
        xv, gv, dhv = ins[0][...], ins[1][...], ins[2][...]
        r = lax.rsqrt(jnp.mean(xv * xv, axis=-1, keepdims=True) + RMS_EPS)
        xh = xv * r
        u = dhv * gv
        dot = jnp.mean(u * xh, axis=-1, keepdims=True)
        outs[0][...] = ins[3][...] + r * (u - xh * dot)

        @pl.when(pl.program_id(0) == 0)
        def _():
            outs[1][...] = jnp.zeros_like(outs[1])

        outs[1][0:1, :] += jnp.sum(dhv * xh, axis=0, keepdims=True)

    row = pl.BlockSpec((tm, d), lambda i, j, k: (i, 0))
    vec = pl.BlockSpec((1, d), lambda i, j, k: (0, 0))
    acc = pl.BlockSpec((8, d), lambda i, j, k: (0, 0))
    return _fused(name, (t // tm, 1, 1), [(x, row), (g, vec), (dh, row), (resid, row)],
                  [(_sds((t, d), F32), row), (_sds((8, d), F32), acc)], [], epilogue,
                  temp_bytes=6 * tm * d * 4, semantics=("arbitrary", "arbitrary", "arbitrary"), deps=deps)


def _loss_dy(y, target):
    t, d = y.shape
    tm = _row_tile(t)

    def epilogue(_, ins, outs):
        e = ins[0][...] - ins[1][...]
        outs[0][...] = e * (1.0 / d)

        @pl.when(pl.program_id(0) == 0)
        def _():
            outs[1][...] = jnp.zeros_like(outs[1])

        part = jnp.sum(jnp.sum(e * e, axis=1, keepdims=True), axis=0, keepdims=True)
        outs[1][...] += jnp.broadcast_to(part, outs[1].shape)

    row = pl.BlockSpec((tm, d), lambda i, j, k: (i, 0))
    acc = pl.BlockSpec((8, LANES), lambda i, j, k: (0, 0))
    return _fused("loss_dy", (t // tm, 1, 1), [(y, row), (target, row)],
                  [(_sds((t, d), F32), row), (_sds((8, LANES), F32), acc)], [], epilogue,
                  temp_bytes=3 * tm * d * 4, semantics=("arbitrary", "arbitrary", "arbitrary"))


def _ffn_up(name, h, wgu):
    t, d = h.shape
    nb = wgu.shape[2]
    f = 4 * nb
    tm = min(t, 512)

    def body(h_ref, wg_ref, wu_ref, gu_ref, a_ref):
        hv = h_ref[...]
        for c0 in range(0, nb, MXU_COLS):
            cs = slice(c0, min(c0 + MXU_COLS, nb))
            g = jnp.dot(hv, wg_ref[:, cs], preferred_element_type=F32)
            u = jnp.dot(hv, wu_ref[:, cs], preferred_element_type=F32)
            gu_ref[0, :, cs] = g.astype(BF16)
            gu_ref[1, :, cs] = u.astype(BF16)
            a_ref[:, cs] = (g * _sigmoid(g) * u).astype(BF16)

    blocks = tm * d * 2 + 2 * d * nb * 2 + 3 * tm * nb * 2
    return pl.pallas_call(
        body, name=name, grid=(4, t // tm),
        in_specs=[pl.BlockSpec((tm, d), lambda j, i: (i, 0)),
                  pl.BlockSpec((None, d, nb), lambda j, i: (j, 0, 0)),
                  pl.BlockSpec((None, d, nb), lambda j, i: (j + 4, 0, 0))],
        out_specs=[pl.BlockSpec((2, tm, nb), lambda j, i: (0, i, j)),
                   pl.BlockSpec((tm, nb), lambda j, i: (i, j))],
        out_shape=[_sds((2, t, f), BF16), _sds((t, f), BF16)],
        compiler_params=_params(("parallel", "parallel"), blocks, 8 * tm * MXU_COLS * 4),
    )(h, wgu, wgu)


def _ffn_down(name, a, wd, x):
    t, f = a.shape
    d = wd.shape[1]
    tm = min(t, 512)
    tn = min(d, 1024)

    def epilogue(acc, ins, outs):
        outs[0][...] = ins[2][...] + 0.5 * acc

    blk = pl.BlockSpec((tm, tn), lambda j, i, k: (i, j))
    return _fused(name, (d // tn, t // tm, 1),
                  [(a, pl.BlockSpec((tm, f), lambda j, i, k: (i, 0))),
                   (wd, pl.BlockSpec((f, tn), lambda j, i, k: (0, j))),
                   (x, blk)],
                  [(_sds((t, d), F32), blk)],
                  [(0, 1, NN)], epilogue, temp_bytes=2 * tm * tn * 4)[0]


def _ffn_bwd_act(name, dy, wd, gu, deps=()):
    t, d = dy.shape
    f = wd.shape[0]
    nb = f // 4
    tm = min(t, 512)

    def body(dy_ref, wd_ref, gu_ref, *rest):
        dgu_ref, a_ref = rest[-2], rest[-1]
        dyv = dy_ref[...].astype(BF16)
        for c0 in range(0, nb, MXU_COLS):
            cs = slice(c0, min(c0 + MXU_COLS, nb))
            da = 0.5 * lax.dot_general(dyv, wd_ref[cs, :], NT, preferred_element_type=F32)
            g = gu_ref[0, :, cs].astype(F32)
            u = gu_ref[1, :, cs].astype(F32)
            s = _sigmoid(g)
            silu = g * s
            dgu_ref[0, :, cs] = (da * u * (s * (1.0 + g * (1.0 - s)))).astype(BF16)
            dgu_ref[1, :, cs] = (da * silu).astype(BF16)
            a_ref[:, cs] = (silu * u).astype(BF16)

    blocks = tm * d * 4 + nb * d * 2 + 5 * tm * nb * 2
    return pl.pallas_call(
        body, name=name, grid=(4, t // tm),
        in_specs=[pl.BlockSpec((tm, d), lambda j, i: (i, 0)),
                  pl.BlockSpec((nb, d), lambda j, i: (j, 0)),
                  pl.BlockSpec((2, tm, nb), lambda j, i: (0, i, j))] + [_ANY] * len(deps),
        out_specs=[pl.BlockSpec((2, tm, nb), lambda j, i: (0, i, j)), pl.BlockSpec((tm, nb), lambda j, i: (i, j))],
        out_shape=[_sds((2, t, f), BF16), _sds((t, f), BF16)],
        compiler_params=_params(("parallel", "parallel"), blocks, tm * d * 2 + 8 * tm * MXU_COLS * 4),
    )(dy, wd, gu, *deps)


def _ffn_bwd_dwd(name, a, dy, deps=()):
    t, f = a.shape
    d = dy.shape[1]
    tm = f // 4
    tn = min(d, 512)

    def epilogue(acc, ins, outs):
        outs[0][...] = (0.5 * acc).astype(BF16)

    return _fused(name, (4, d // tn, 1),
                  [(a, pl.BlockSpec((t, tm), lambda i, j, k: (0, i))),
                   (dy, pl.BlockSpec((t, tn), lambda i, j, k: (0, j)))],
                  [(_sds((f, d), BF16), pl.BlockSpec((tm, tn), lambda i, j, k: (i, j)))],
                  [(0, 1, TN)], epilogue, temp_bytes=t * tn * 2 + 2 * tm * tn * 4, deps=deps)[0]


def _ffn_bwd_dh(name, dgu, wgu, deps=()):
    _, t, f = dgu.shape
    d, nb = wgu.shape[1], wgu.shape[2]
    tm = min(t, 512)

    def products(ins):
        return (lax.dot_general(ins[0][:, 0:nb], ins[1][0], NT, preferred_element_type=F32)
                + lax.dot_general(ins[0][:, nb:2 * nb], ins[1][1], NT, preferred_element_type=F32))

    def epilogue(acc, ins, outs):
        outs[0][...] = acc

    return _fused(name, (t // tm, 1, 4),
                  [(dgu, pl.BlockSpec((None, tm, 2 * nb), lambda i, j, k: (k // 2, i, k % 2))),
                   (wgu, pl.BlockSpec((2, d, nb), lambda i, j, k: (k, 0, 0)))],
                  [(_sds((t, d), F32), pl.BlockSpec((tm, d), lambda i, j, k: (i, 0)))],
                  products, epilogue, nk=4, acc_shape=(tm, d), temp_bytes=tm * d * 4, deps=deps)[0]


def _ffn_bwd_dwgu(name, h, dgu, deps=()):
    t, d = h.shape
    nb = dgu.shape[2] // 4
    tm = min(d, 512)

    def epilogue(acc, ins, outs):
        outs[0][...] = acc.astype(BF16)

    return _fused(name, (N_DEV, d // tm, 1),
                  [(h, pl.BlockSpec((t, tm), lambda i, j, k: (0, j))),
                   (dgu, pl.BlockSpec((None, t, nb), lambda i, j, k: (i // 4, 0, i % 4)))],
                  [(_sds((N_DEV, d, nb), BF16), pl.BlockSpec((None, tm, nb), lambda i, j, k: (i, j, 0)))],
                  [(0, 1, TN)], epilogue, temp_bytes=2 * tm * nb * 4, deps=deps)[0]


def _proj(h, w_in):
    t, d = h.shape
    nb = w_in.shape[3]
    tm = min(t, 512)

    def body(h_ref, w_ref, o_ref):
        hv = h_ref[...]
        o_ref[:, 0:nb] = jnp.dot(hv, w_ref[0], preferred_element_type=F32).astype(BF16)
        o_ref[:, nb:2 * nb] = jnp.dot(hv, w_ref[1], preferred_element_type=F32).astype(BF16)

    blocks = tm * d * 2 + 2 * d * nb * 2 + tm * 2 * nb * 4
    return pl.pallas_call(
        body, name="mix_proj", grid=(4, t // tm),
        in_specs=[pl.BlockSpec((tm, d), lambda j, i: (i, 0)),
                  pl.BlockSpec((None, 2, d, nb), lambda j, i: (j, 0, 0, 0))],
        out_specs=pl.BlockSpec((tm, 2 * nb), lambda j, i: (i, j)),
        out_shape=_sds((t, N_DEV * nb), BF16),
        compiler_params=_params(("parallel", "parallel"), blocks, 2 * tm * nb * 4),
    )(h, w_in)


def _shift_rows(u, k):
    t = u.shape[0]
    rolled = pltpu.roll(u, k % t, axis=0)
    row = lax.broadcasted_iota(jnp.int32, u.shape, 0)
    keep = (row >= k) if k > 0 else (row < t + k)
    return jnp.where(keep, rolled, 0.0)


def _conv_fwd(proj, conv_w):
    t = proj.shape[0]
    cw = conv_w.shape[1]
    tc = min(cw, 256)
    nc = cw // tc

    def epilogue(_, ins, outs):
        u = ins[2][...].astype(F32) * ins[0][...].astype(F32)
        w = ins[3][...]
        y = u * w[2:3, :] + _shift_rows(u, 1) * w[1:2, :] + _shift_rows(u, 2) * w[0:1, :]
        outs[0][...] = (ins[1][...].astype(F32) * y).astype(BF16)

    def col(seg):
        return pl.BlockSpec((t, tc), lambda i, j, k: (0, seg * nc + i))

    return _fused("conv_fwd", (nc, 1, 1),
                  [(proj, col(0)), (proj, col(1)), (proj, col(2)),
                   (conv_w, pl.BlockSpec((8, tc), lambda i, j, k: (0, i)))],
                  [(_sds((t, cw), BF16), pl.BlockSpec((t, tc), lambda i, j, k: (0, i)))],
                  [], epilogue, temp_bytes=6 * t * tc * 4)[0]


def _conv_bwd(proj, conv_w, dca, deps=()):
    t = proj.shape[0]
    cw = conv_w.shape[1]
    tc = min(cw, 256)
    nc = cw // tc

    def epilogue(_, ins, outs):
        xc, bg, cg = ins[0][...].astype(F32), ins[1][...].astype(F32), ins[2][...].astype(F32)
        w, dc = ins[3][...], ins[4][...]
        u = cg * xc
        u1, u2 = _shift_rows(u, 1), _shift_rows(u, 2)
        y = u * w[2:3, :] + u1 * w[1:2, :] + u2 * w[0:1, :]
        dconv = dc * bg
        du = dconv * w[2:3, :] + _shift_rows(dconv, -1) * w[1:2, :] + _shift_rows(dconv, -2) * w[0:1, :]
        outs[0][0] = (du * cg).astype(BF16)
        outs[0][1] = (dc * y).astype(BF16)
        outs[0][2] = (du * xc).astype(BF16)
        outs[1][...] = jnp.zeros_like(outs[1])
        outs[1][0:1, :] = jnp.sum(dconv * u2, axis=0, keepdims=True)
        outs[1][1:2, :] = jnp.sum(dconv * u1, axis=0, keepdims=True)
        outs[1][2:3, :] = jnp.sum(dconv * u, axis=0, keepdims=True)

    def col(seg):
        return pl.BlockSpec((t, tc), lambda i, j, k: (0, seg * nc + i))

    own = pl.BlockSpec((t, tc), lambda i, j, k: (0, i))
    wspec = pl.BlockSpec((8, tc), lambda i, j, k: (0, i))
    return _fused("conv_bwd", (nc, 1, 1),
                  [(proj, col(0)), (proj, col(1)), (proj, col(2)), (conv_w, wspec), (dca, own)],
                  [(_sds((3, t, cw), BF16), pl.BlockSpec((3, t, tc), lambda i, j, k: (0, 0, i))),
                   (_sds((8, cw), F32), wspec)],
                  [], epilogue, temp_bytes=10 * t * tc * 4, deps=deps)


def _split3(x):
    hi = x.astype(BF16)
    r1 = x - hi.astype(F32)
    mid = r1.astype(BF16)
    lo = (r1 - mid.astype(F32)).astype(BF16)
    return hi, mid, lo


def _head_selector(width):
    r = lax.broadcasted_iota(jnp.int32, (width, LANES), 0)
    c = lax.broadcasted_iota(jnp.int32, (width, LANES), 1)
    return (lax.shift_right_logical(r, 6) == c).astype(BF16)


def _head_sum(x, sel):
    return sum(jnp.dot(p, sel, preferred_element_type=F32) for p in _split3(x))


def _head_bcast(r, sel):
    return sum(lax.dot_general(p, sel, NT, preferred_element_type=F32) for p in _split3(r))


def _rope(x, c, sa, sb):
    n = x.shape[1]
    return x * c + pltpu.roll(x, n - ROT_DIM // 2, axis=1) * sa + pltpu.roll(x, ROT_DIM // 2, axis=1) * sb


def _rope_t(d, c, sa, sb):
    n = d.shape[1]
    return d * c + pltpu.roll(d * sa, ROT_DIM // 2, axis=1) + pltpu.roll(d * sb, n - ROT_DIM // 2, axis=1)


def _tile_lanes(tab, width):
    return tab if width == tab.shape[1] else jnp.tile(tab, (1, width // tab.shape[1]))


def _qk_prep(proj, gq, gk, rope_tabs, cw, kw):
    t = proj.shape[0]
    tm = _row_tile(t)

    def epilogue(_, ins, outs):
        c, sa, sb = ins[5][...], ins[6][...], ins[7][...]
        for src, gain, dst, width in ((0, 3, 0, cw), (1, 4, 1, kw)):
            xv = ins[src][...].astype(F32)
            sel = _head_selector(width)
            r = lax.rsqrt(_head_sum(xv * xv, sel) * (1.0 / HEAD_DIM) + RMS_EPS)
            xn = xv * _head_bcast(r, sel) * ins[gain][...]
            outs[dst][...] = _rope(xn, _tile_lanes(c, width), _tile_lanes(sa, width), _tile_lanes(sb, width)).astype(BF16)
        outs[2][...] = ins[2][...].astype(BF16)

    kblk = cw // kw
    tab = pl.BlockSpec((tm, LANES), lambda i, j, k: (i, 0))
    kspec = pl.BlockSpec((tm, kw), lambda i, j, k: (i, 0))
    return _fused("qk_prep", (t // tm, 1, 1),
                  [(proj, pl.BlockSpec((tm, cw), lambda i, j, k: (i, 3))),
                   (proj, pl.BlockSpec((tm, kw), lambda i, j, k: (i, 4 * kblk))),
                   (proj, pl.BlockSpec((tm, kw), lambda i, j, k: (i, 4 * kblk + 1))),
                   (gq, pl.BlockSpec((1, cw), lambda i, j, k: (0, 0))),
                   (gk, pl.BlockSpec((1, kw), lambda i, j, k: (0, 0))),
                   (rope_tabs[0], tab), (rope_tabs[1], tab), (rope_tabs[2], tab)],
                  [(_sds((t, cw), BF16), pl.BlockSpec((tm, cw), lambda i, j, k: (i, 0))),
                   (_sds((t, kw), BF16), kspec), (_sds((t, kw), BF16), kspec)],
                  [], epilogue, temp_bytes=12 * tm * cw * 4)


def _qk_prep_bwd(proj, gq, gk, rope_tabs, dq, dkc, dkp, dvc, dvp, cw, kw):
    t = proj.shape[0]
    tm = BLOCK
    nblk = t // tm

    def epilogue(_, ins, outs):
        c, sa, sb = ins[5][...], ins[6][...], ins[7][...]
        has_next = (pl.program_id(0) < nblk - 1).astype(F32)
        dk = ins[9][...] + has_next * ins[10][...]
        dv = ins[11][...] + has_next * ins[12][...]
        pieces = []
        for src, gain, dval, dst, width in ((0, 3, ins[8][...], 1, cw), (1, 4, dk, 2, kw)):
            xv, gv = ins[src][...].astype(F32), ins[gain][...]
            sel = _head_selector(width)
            r = _head_bcast(lax.rsqrt(_head_sum(xv * xv, sel) * (1.0 / HEAD_DIM) + RMS_EPS), sel)
            xh = xv * r
            dxn = _rope_t(dval, _tile_lanes(c, width), _tile_lanes(sa, width), _tile_lanes(sb, width))
            u = dxn * gv
            dot = _head_bcast(_head_sum(u * xh, sel), sel) * (1.0 / HEAD_DIM)
            pieces.append((r * (u - xh * dot)).astype(BF16))
            ri = lax.broadcasted_iota(jnp.int32, (width, LANES), 0)
            ci = lax.broadcasted_iota(jnp.int32, (width, LANES), 1)
            fold = (lax.bitwise_and(ri, HEAD_DIM - 1) == ci).astype(BF16)
            colsum = jnp.broadcast_to(jnp.sum(dxn * xh, axis=0, keepdims=True), (8, width))
            part = sum(jnp.dot(p, fold, preferred_element_type=F32) for p in _split3(colsum))

            @pl.when(pl.program_id(0) == 0)
            def _():
                outs[dst][...] = jnp.zeros_like(outs[dst])

            outs[dst][0:1, :] += part[0:1, :]
        outs[0][:, 0:cw] = pieces[0]
        outs[0][:, cw:cw + kw] = pieces[1]
        outs[0][:, cw + kw:cw + 2 * kw] = dv.astype(BF16)

    kblk = cw // kw
    tab = pl.BlockSpec((tm, LANES), lambda i, j, k: (i, 0))
    kcur = pl.BlockSpec((tm, kw), lambda i, j, k: (i, 0))
    knext = pl.BlockSpec((tm, kw), lambda i, j, k: (jnp.minimum(i + 1, nblk - 1), 0))
    acc = pl.BlockSpec((8, LANES), lambda i, j, k: (0, 0))
    return _fused("qk_prep_bwd", (nblk, 1, 1),
                  [(proj, pl.BlockSpec((tm, cw), lambda i, j, k: (i, 3))),
                   (proj, pl.BlockSpec((tm, kw), lambda i, j, k: (i, 4 * kblk))),
                   (proj, pl.BlockSpec((tm, kw), lambda i, j, k: (i, 4 * kblk + 1))),
                   (gq, pl.BlockSpec((1, cw), lambda i, j, k: (0, 0))),
                   (gk, pl.BlockSpec((1, kw), lambda i, j, k: (0, 0))),
                   (rope_tabs[0], tab), (rope_tabs[1], tab), (rope_tabs[2], tab),
                   (dq, pl.BlockSpec((tm, cw), lambda i, j, k: (i, 0))),
                   (dkc, kcur), (dkp, knext), (dvc, kcur), (dvp, knext)],
                  [(_sds((t, cw + 2 * kw), BF16), pl.BlockSpec((tm, cw + 2 * kw), lambda i, j, k: (i, 0))),
                   (_sds((8, LANES), F32), acc), (_sds((8, LANES), F32), acc)],
                  [], epilogue, temp_bytes=16 * tm * cw * 4, semantics=("arbitrary", "arbitrary", "arbitrary"))


def _attn_mask(n):
    key = lax.broadcasted_iota(jnp.int32, (2 * BLOCK, GROUP * BLOCK), 0)
    qry = lax.bitwise_and(lax.broadcasted_iota(jnp.int32, (2 * BLOCK, GROUP * BLOCK), 1), BLOCK - 1)
    return (key > qry) & (key <= qry + BLOCK) & ((key >= BLOCK) | (n > 0))


def _stack_heads(x, h):
    return jnp.concatenate([x[:, (h * GROUP + g) * HEAD_DIM:(h * GROUP + g + 1) * HEAD_DIM] for g in range(GROUP)], axis=0)


def _softmax_with_sink(q4, k2, sink_ref, h, valid):
    sink = jnp.concatenate([sink_ref[h * GROUP + g:h * GROUP + g + 1, :] for g in range(GROUP)], axis=1)
    s = lax.dot_general(k2, q4, NT, preferred_element_type=F32) * ATTN_SCALE
    s = jnp.where(valid, s, NEG_INF)
    m = jnp.maximum(jnp.max(s, axis=0, keepdims=True), sink)
    p = jnp.exp(s - m)
    es = jnp.exp(sink - m)
    inv = 1.0 / (jnp.sum(p, axis=0, keepdims=True) + es)
    return p * inv, es * inv


def _attn_fwd(qn, kn, vb, sink_rows):
    t, cw = qn.shape
    kw = kn.shape[1]
    nkv = kw // HEAD_DIM

    def body(q_ref, kp_ref, kc_ref, vp_ref, vc_ref, sink_ref, o_ref):
        valid = _attn_mask(pl.program_id(0))
        qv = q_ref[...]
        kp, kc, vp, vc = kp_ref[...], kc_ref[...], vp_ref[...], vc_ref[...]
        outs = []
        for h in range(nkv):
            hs = slice(h * HEAD_DIM, (h + 1) * HEAD_DIM)
            k2 = jnp.concatenate([kp[:, hs], kc[:, hs]], axis=0)
            v2 = jnp.concatenate([vp[:, hs], vc[:, hs]], axis=0)
            pn, _ = _softmax_with_sink(_stack_heads(qv, h), k2, sink_ref, h, valid)
            o4 = lax.dot_general(pn.astype(BF16), v2, TN, preferred_element_type=F32)
            outs += [o4[g * BLOCK:(g + 1) * BLOCK] for g in range(GROUP)]
        o_ref[...] = jnp.concatenate(outs, axis=-1).astype(BF16)

    cur = lambda n: (n, 0)
    prev = lambda n: (jnp.maximum(n - 1, 0), 0)
    return pl.pallas_call(
        body, name="attn_fwd", grid=(t // BLOCK,),
        in_specs=[pl.BlockSpec((BLOCK, cw), cur),
                  pl.BlockSpec((BLOCK, kw), prev), pl.BlockSpec((BLOCK, kw), cur),
                  pl.BlockSpec((BLOCK, kw), prev), pl.BlockSpec((BLOCK, kw), cur),
                  pl.BlockSpec(sink_rows.shape, lambda n: (0, 0))],
        out_specs=pl.BlockSpec((BLOCK, cw), cur),
        out_shape=_sds((t, cw), BF16),
        compiler_params=_params(("parallel",), BLOCK * (cw + 4 * kw) * 2 + BLOCK * cw * 2, 8 << 20),
    )(qn, kn, kn, vb, vb, sink_rows)


def _attn_bwd(qn, kn, vb, sink_rows, do):
    t, cw = qn.shape
    kw = kn.shape[1]
    nkv = kw // HEAD_DIM
    nq = nkv * GROUP

    def body(q_ref, kp_ref, kc_ref, vp_ref, vc_ref, sink_ref, do_ref,
             dq_ref, dkc_ref, dkp_ref, dvc_ref, dvp_ref, dsink_ref):
        n = pl.program_id(0)
        valid = _attn_mask(n)
        qv, dov = q_ref[...], do_ref[...]
        kp, kc, vp, vc = kp_ref[...], kc_ref[...], vp_ref[...], vc_ref[...]
        dqs, dks, dvs, dsinks = [], [], [], []
        for h in range(nkv):
            hs = slice(h * HEAD_DIM, (h + 1) * HEAD_DIM)
            k2 = jnp.concatenate([kp[:, hs], kc[:, hs]], axis=0)
            v2 = jnp.concatenate([vp[:, hs], vc[:, hs]], axis=0)
            q4 = _stack_heads(qv, h)
            dob = _stack_heads(dov, h).astype(BF16)
            pn, psink = _softmax_with_sink(q4, k2, sink_ref, h, valid)
            dpn = lax.dot_general(v2, dob, NT, preferred_element_type=F32)
            dvs.append(jnp.dot(pn.astype(BF16), dob, preferred_element_type=F32))
            delta = jnp.sum(pn * dpn, axis=0, keepdims=True)
            ds = (pn * (dpn - delta) * ATTN_SCALE).astype(BF16)
            dks.append(jnp.dot(ds, q4, preferred_element_type=F32))
            dq4 = lax.dot_general(ds, k2, TN, preferred_element_type=F32)
            dsink4 = -psink * delta
            for g in range(GROUP):
                dqs.append(dq4[g * BLOCK:(g + 1) * BLOCK])
                dsinks.append(jnp.broadcast_to(jnp.sum(dsink4[:, g * BLOCK:(g + 1) * BLOCK], axis=1, keepdims=True), (1, LANES)))
        dq_ref[...] = jnp.concatenate(dqs, axis=-1)
        dkp_ref[...] = jnp.concatenate([d[:BLOCK] for d in dks], axis=-1)
        dkc_ref[...] = jnp.concatenate([d[BLOCK:] for d in dks], axis=-1)
        dvp_ref[...] = jnp.concatenate([d[:BLOCK] for d in dvs], axis=-1)
        dvc_ref[...] = jnp.concatenate([d[BLOCK:] for d in dvs], axis=-1)

        @pl.when(n == 0)
        def _():
            dsink_ref[...] = jnp.zeros_like(dsink_ref)

        dsink_ref[...] += jnp.concatenate(dsinks, axis=0)

    cur = lambda n: (n, 0)
    prev = lambda n: (jnp.maximum(n - 1, 0), 0)
    kspec = pl.BlockSpec((BLOCK, kw), cur)
    return pl.pallas_call(
        body, name="attn_bwd", grid=(t // BLOCK,),
        in_specs=[pl.BlockSpec((BLOCK, cw), cur),
                  pl.BlockSpec((BLOCK, kw), prev), kspec,
                  pl.BlockSpec((BLOCK, kw), prev), kspec,
                  pl.BlockSpec(sink_rows.shape, lambda n: (0, 0)),
                  pl.BlockSpec((BLOCK, cw), cur)],
        out_specs=[pl.BlockSpec((BLOCK, cw), cur), kspec, kspec, kspec, kspec,
                   pl.BlockSpec((nq, LANES), lambda n: (0, 0))],
        out_shape=[_sds((t, cw), F32)] + [_sds((t, kw), F32)] * 4 + [_sds((nq, LANES), F32)],
        compiler_params=_params(("arbitrary",), BLOCK * (cw + 4 * kw) * 2 + 2 * BLOCK * cw * 4 + 4 * BLOCK * kw * 4, 12 << 20),
    )(qn, kn, kn, vb, vb, sink_rows, do)


def _mix_out(ca, o, woc, woa, proj):
    t, cw = ca.shape
    nb = woc.shape[2]
    d = N_DEV * nb
    tm = min(t, 1024)
    ga0 = (3 * cw + cw + 2 * (cw // 4)) // nb

    def body(ca_ref, o_ref, woc_ref, woa_ref, ga_ref, gb_ref, m_ref, ya_ref, yb_ref):
        ya = jnp.dot(ca_ref[...], woc_ref[...], preferred_element_type=F32)
        yb = jnp.dot(o_ref[...], woa_ref[...], preferred_element_type=F32)
        ya_ref[...] = ya.astype(BF16)
        yb_ref[...] = yb.astype(BF16)
        m_ref[...] = (_sigmoid(ga_ref[...].astype(F32)) * ya + _sigmoid(gb_ref[...].astype(F32)) * yb).astype(BF16)

    act = pl.BlockSpec((tm, cw), lambda i, j: (i, 0))
    wsp = pl.BlockSpec((None, cw, nb), lambda i, j: (j, 0, 0))
    osp = pl.BlockSpec((tm, nb), lambda i, j: (i, j))
    blocks = 2 * tm * cw * 2 + 2 * cw * nb * 2 + 2 * tm * nb * 4 + 3 * tm * nb * 2
    return pl.pallas_call(
        body, name="mix_out", grid=(t // tm, N_DEV),
        in_specs=[act, act, wsp, wsp,
                  pl.BlockSpec((tm, nb), lambda i, j: (i, ga0 + j)),
                  pl.BlockSpec((tm, nb), lambda i, j: (i, ga0 + N_DEV + j))],
        out_specs=[osp, osp, osp],
        out_shape=[_sds((t, d), BF16)] * 3,
        compiler_params=_params(("parallel", "parallel"), blocks, 6 * tm * nb * 4),
    )(ca, o, woc, woa, proj, proj)


def _mix_residual(merged, wo, x):
    t, d = x.shape
    tm = min(t, 512)

    def epilogue(acc, ins, outs):
        outs[0][...] = ins[2][...] + acc

    row = pl.BlockSpec((tm, d), lambda i, j, k: (i, 0))
    return _fused("mix_residual", (t // tm, 1, 1),
                  [(merged, row), (wo, pl.BlockSpec((d, d), lambda i, j, k: (0, 0))), (x, row)],
                  [(_sds((t, d), F32), row)], [(0, 1, NN)], epilogue, temp_bytes=2 * tm * d * 4)[0]


def _mix_bwd_gates(dx, wo, ya, yb, proj, cw):
    t, d = dx.shape
    tm = min(t, 512)
    tn = min(d, 512)
    ga0 = (4 * cw + 2 * (cw // 4)) // tn

    def epilogue(acc, ins, outs):
        sa, sb = _sigmoid(ins[4][...].astype(F32)), _sigmoid(ins[5][...].astype(F32))
        outs[0][...] = (acc * sa).astype(BF16)
        outs[1][...] = (acc * sb).astype(BF16)
        outs[2][0] = (acc * ins[2][...].astype(F32) * sa * (1.0 - sa)).astype(BF16)
        outs[2][1] = (acc * ins[3][...].astype(F32) * sb * (1.0 - sb)).astype(BF16)

    blk = pl.BlockSpec((tm, tn), lambda i, j, k: (i, j))
    return _fused("mix_bwd_gates", (t // tm, d // tn, 1),
                  [(dx, pl.BlockSpec((tm, d), lambda i, j, k: (i, 0))),
                   (wo, pl.BlockSpec((tn, d), lambda i, j, k: (j, 0))),
                   (ya, blk), (yb, blk),
                   (proj, pl.BlockSpec((tm, tn), lambda i, j, k: (i, ga0 + j))),
                   (proj, pl.BlockSpec((tm, tn), lambda i, j, k: (i, ga0 + d // tn + j)))],
                  [(_sds((t, d), BF16), blk), (_sds((t, d), BF16), blk),
                   (_sds((2, t, d), BF16), pl.BlockSpec((2, tm, tn), lambda i, j, k: (0, i, j)))],
                  [(0, 1, NT)], epilogue, temp_bytes=8 * tm * tn * 4)


def _tn_matmul(name, a, b, tm, out_dtype=BF16):
    t, m = a.shape
    n = b.shape[1]
    tk = min(t, 512)

    def epilogue(acc, ins, outs):
        outs[0][...] = acc.astype(out_dtype)

    return _fused(name, (m // tm, 1, t // tk),
                  [(a, pl.BlockSpec((tk, tm), lambda i, j, k: (k, i))),
                   (b, pl.BlockSpec((tk, n), lambda i, j, k: (k, 0)))],
                  [(_sds((m, n), out_dtype), pl.BlockSpec((tm, n), lambda i, j, k: (i, 0)))],
                  [(0, 1, TN)], epilogue, nk=t // tk, acc_shape=(tm, n), temp_bytes=tm * n * 4)[0]


def _out_proj_bwd_act(dya, dyb, woc, woa, deps=()):
    t, d = dya.shape
    kdim, nb = woc.shape[1], woc.shape[2]
    tm = min(t, 512)

    def body(dya_ref, dyb_ref, woc_ref, woa_ref, *rest):
        for dy_ref, w_ref, o_ref in ((dya_ref, woc_ref, rest[-2]), (dyb_ref, woa_ref, rest[-1])):
            total = None
            for j in range(N_DEV):
                part = lax.dot_general(dy_ref[:, j * nb:(j + 1) * nb], w_ref[j], NT, preferred_element_type=F32)
                total = part if total is None else total + part
            o_ref[...] = total

    row = pl.BlockSpec((tm, d), lambda i: (i, 0))
    wsp = pl.BlockSpec((N_DEV, kdim, nb), lambda i: (0, 0, 0))
    osp = pl.BlockSpec((tm, kdim), lambda i: (i, 0))
    blocks = 2 * tm * d * 2 + 2 * N_DEV * kdim * nb * 2 + 2 * tm * kdim * 4
    return pl.pallas_call(
        body, name="mix_bwd_dca_do", grid=(t // tm,),
        in_specs=[row, row, wsp, wsp] + [_ANY] * len(deps), out_specs=[osp, osp],
        out_shape=[_sds((t, kdim), F32)] * 2,
        compiler_params=_params(("parallel",), blocks, 4 * tm * kdim * 4),
    )(dya, dyb, woc, woa, *deps)


def _out_proj_bwd_w(ca, o, dya, dyb, nb):
    t, kdim = ca.shape

    def body(ca_ref, o_ref, dya_ref, dyb_ref, dwoc_ref, dwoa_ref):
        dwoc_ref[...] = lax.dot_general(ca_ref[...], dya_ref[...], TN, preferred_element_type=F32).astype(BF16)
        dwoa_ref[...] = lax.dot_general(o_ref[...], dyb_ref[...], TN, preferred_element_type=F32).astype(BF16)

    act = pl.BlockSpec((t, kdim), lambda j: (0, 0))
    col = pl.BlockSpec((t, nb), lambda j: (0, j))
    osp = pl.BlockSpec((None, kdim, nb), lambda j: (j, 0, 0))
    blocks = 2 * t * kdim * 2 + 2 * t * nb * 2 + 2 * kdim * nb * 2
    return pl.pallas_call(
        body, name="mix_bwd_dwoc_dwoa", grid=(N_DEV,),
        in_specs=[act, act, col, col], out_specs=[osp, osp],
        out_shape=[_sds((N_DEV, kdim, nb), BF16)] * 2,
        compiler_params=_params(("parallel",), blocks, 4 * kdim * nb * 4),
    )(ca, o, dya, dyb)


def _proj_bwd_act(dproj, w_in, deps=()):
    t, n = dproj.shape
    d, nb = w_in.shape[2], w_in.shape[3]
    tm = min(t, 512)

    def epilogue(acc, ins, outs):
        outs[0][...] = acc

    def products(ins):
        return (lax.dot_general(ins[0][:, 0:nb], ins[1][0], NT, preferred_element_type=F32)
                + lax.dot_general(ins[0][:, nb:2 * nb], ins[1][1], NT, preferred_element_type=F32))

    return _fused("mix_bwd_dh", (t // tm, 1, 4),
                  [(dproj, pl.BlockSpec((tm, 2 * nb), lambda i, j, k: (i, k))),
                   (w_in, pl.BlockSpec((None, 2, d, nb), lambda i, j, k: (k, 0, 0, 0)))],
                  [(_sds((t, d), F32), pl.BlockSpec((tm, d), lambda i, j, k: (i, 0)))],
                  products, epilogue, nk=4, acc_shape=(tm, d), temp_bytes=tm * d * 4, deps=deps)[0]


def _proj_bwd_w(h, dproj):
    t, d = h.shape
    nb = dproj.shape[1] // N_DEV
    tm = min(d, 512)

    def body(h_ref, dp_ref, o_ref):
        hv = h_ref[...]
        o_ref[0] = lax.dot_general(hv, dp_ref[:, 0:nb], TN, preferred_element_type=F32).astype(BF16)
        o_ref[1] = lax.dot_general(hv, dp_ref[:, nb:2 * nb], TN, preferred_element_type=F32).astype(BF16)

    blocks = t * tm * 2 + t * 2 * nb * 2 + 2 * tm * nb * 2
    return pl.pallas_call(
        body, name="mix_bwd_dwin", grid=(4, d // tm),
        in_specs=[pl.BlockSpec((t, tm), lambda j, i: (0, i)),
                  pl.BlockSpec((t, 2 * nb), lambda j, i: (0, j))],
        out_specs=pl.BlockSpec((None, 2, tm, nb), lambda j, i: (j, 0, i, 0)),
        out_shape=_sds((4, 2, d, nb), BF16),
        compiler_params=_params(("parallel", "parallel"), blocks, 4 * tm * nb * 4),
    )(h, dproj)


def _adamw_math(w, g, m, v):
    m = ADAM_B1 * m + (1.0 - ADAM_B1) * g
    v = ADAM_B2 * v + (1.0 - ADAM_B2) * (g * g)
    m_hat = m / (1.0 - ADAM_B1 ** ADAM_STEP)
    v_hat = v / (1.0 - ADAM_B2 ** ADAM_STEP)
    delta = -ADAM_LR * (m_hat / (jnp.sqrt(v_hat) + ADAM_EPS) + ADAM_WD * w)
    return delta, m, v


def _adamw(name, parts, w, m, v, tr):
    r, c = w.shape

    def body(p_ref, w_ref, m_ref, v_ref, g_out, d_out, m_out, v_out):
        g = p_ref[0].astype(F32)
        for s in range(1, N_DEV):
            g = g + p_ref[s].astype(F32)
        delta, mn, vn = _adamw_math(w_ref[...], g, m_ref[...], v_ref[...])
        g_out[...] = g
        d_out[...] = delta
        m_out[...] = mn
        v_out[...] = vn

    blk = pl.BlockSpec((tr, c), lambda i: (i, 0))
    blocks = N_DEV * tr * c * parts.dtype.itemsize + 7 * tr * c * 4
    return pl.pallas_call(
        body, name=name, grid=(r // tr,),
        in_specs=[pl.BlockSpec((N_DEV, tr, c), lambda i: (0, i, 0)), blk, blk, blk],
        out_specs=[blk] * 4, out_shape=[_sds((r, c), F32)] * 4,
        compiler_params=_params(("parallel",), blocks, 6 * tr * c * 4),
    )(parts, w, m, v)


def _adamw_chips(name, chip, own, landed, w, m, v, tr):
    r, c = w.shape

    def body(chip_ref, own_ref, land_ref, w_ref, m_ref, v_ref, g_out, d_out, m_out, v_out):
        mine = own_ref[...].astype(F32)
        g = jnp.zeros((tr, c), F32)
        for k in range(4):
            g = g + jnp.where(chip_ref[0] == k, mine, land_ref[k].astype(F32))
        delta, mn, vn = _adamw_math(w_ref[...], g, m_ref[...], v_ref[...])
        g_out[...] = g
        d_out[...] = delta
        m_out[...] = mn
        v_out[...] = vn

    blk = pl.BlockSpec((tr, c), lambda i, chip_ref: (i, 0))
    grid_spec = pltpu.PrefetchScalarGridSpec(
        num_scalar_prefetch=1, grid=(r // tr,),
        in_specs=[pl.BlockSpec((None, tr, c), lambda i, chip_ref: (chip_ref[0], i, 0)),
                  pl.BlockSpec((4, tr, c), lambda i, chip_ref: (0, i, 0)), blk, blk, blk],
        out_specs=[blk] * 4)
    blocks = 5 * tr * c * 2 + 7 * tr * c * 4
    return pl.pallas_call(
        body, name=name, grid_spec=grid_spec, out_shape=[_sds((r, c), F32)] * 4,
        compiler_params=_params(("parallel",), blocks, 6 * tr * c * 4),
    )(chip, own, landed, w, m, v)


def _rope_tables(t):
    half = ROT_DIM // 2
    inv_freq = 1.0 / (ROPE_THETA ** (jnp.arange(0, ROT_DIM, 2, dtype=F32) / ROT_DIM))
    ang = jnp.arange(t, dtype=F32)[:, None] * inv_freq[None, :]
    cos, sin = jnp.cos(ang), jnp.sin(ang)
    ones = jnp.ones((t, HEAD_DIM - ROT_DIM), F32)
    zeros = jnp.zeros((t, HEAD_DIM - half), F32)
    c = jnp.concatenate([cos, cos, ones], axis=1)
    sa = jnp.concatenate([-sin, zeros], axis=1)
    sb = jnp.concatenate([jnp.zeros((t, half), F32), sin, jnp.zeros((t, HEAD_DIM - ROT_DIM), F32)], axis=1)
    return tuple(jnp.tile(a, (1, LANES // HEAD_DIM)) for a in (c, sa, sb))


def _pad_rows(a, rows=8):
    return jnp.pad(a, ((0, rows - a.shape[0]), (0, 0)))


def kernel(x, g_ffn1, w_gu1, w_down1, g_mix, w_in, conv_w, q_norm_g, k_norm_g, sinks, w_out_conv, w_out_attn, w_o, g_ffn2, w_gu2, w_down2, loss_target, m_g_ffn1, m_w_gu1, m_w_down1, m_g_mix, m_w_in, m_conv_w, m_q_norm_g, m_k_norm_g, m_sinks, m_w_out_conv, m_w_out_attn, m_w_o, m_g_ffn2, m_w_gu2, m_w_down2, v_g_ffn1, v_w_gu1, v_w_down1, v_g_mix, v_w_in, v_conv_w, v_q_norm_g, v_k_norm_g, v_sinks, v_w_out_conv, v_w_out_attn, v_w_o, v_g_ffn2, v_w_gu2, v_w_down2):
    t, d = x.shape[1], x.shape[2]
    cw = d // 2
    kw = cw // GROUP
    nq = cw // HEAD_DIM
    xs, target = x.reshape(t, d), loss_target.reshape(t, d)
    me = 4 * lax.axis_index("x") + 2 * lax.axis_index("y") + lax.axis_index("c")

    big = {"w_gu1": w_gu1, "w_down1": w_down1, "w_in": w_in, "w_out_conv": w_out_conv,
           "w_out_attn": w_out_attn, "w_o": w_o, "w_gu2": w_gu2, "w_down2": w_down2}
    big_m = {"w_gu1": m_w_gu1, "w_down1": m_w_down1, "w_in": m_w_in, "w_out_conv": m_w_out_conv,
             "w_out_attn": m_w_out_attn, "w_o": m_w_o, "w_gu2": m_w_gu2, "w_down2": m_w_down2}
    big_v = {"w_gu1": v_w_gu1, "w_down1": v_w_down1, "w_in": v_w_in, "w_out_conv": v_w_out_conv,
             "w_out_attn": v_w_out_attn, "w_o": v_w_o, "w_gu2": v_w_gu2, "w_down2": v_w_down2}
    names = list(big)

    tiles = {"w_gu1": 256, "w_gu2": 256, "w_in": 256, "w_down1": 176, "w_down2": 176,
             "w_out_conv": 1024, "w_out_attn": 1024, "w_o": 128}

    def row_tile(n):
        r = big[n].shape[1]
        return tiles[n] if r % tiles[n] == 0 else r

    def add_tile(n):
        r, c = big[n].shape[1], big[n].shape[2]
        while r * c * 2 > (3 << 20) and r % 32 == 0:
            r //= 2
        return r

    me_arr = me.astype(jnp.int32).reshape(1)
    sources = [(n, big[n][0], BF16, row_tile(n)) for n in names] + [("conv_w", _pad_rows(conv_w[0]), F32, 8)]
    issue_order = [0, 1, 2, 8, 3, 4, 5, 6, 7]
    first = _place_shard("place_" + names[0], sources[0][1], BF16, me_arr, sources[0][3])
    started = [_gather_start("gather_start_first", [first])]
    rest = [_place_shard("place_" + sources[i][0], sources[i][1], sources[i][2], me_arr, sources[i][3],
                         deps=(started[0][3],)) for i in issue_order[1:]]
    started.append(_gather_start("gather_start_rest", rest))
    where = {0: (0, 0)}
    where.update({i: (1, p) for p, i in enumerate(issue_order[1:])})

    def fetch(tag, idxs, after):
        call = where[idxs[0]][0]
        send, recv, stacks, _ = started[call]
        positions = [where[i][1] for i in idxs]
        got = _gather_wait("gather_wait_" + tag, positions, send, recv, [stacks[p] for p in positions], after)
        return _forward_to_sibling("gather_forward_" + tag, got)

    rope_tabs = _rope_tables(t)
    gq = jnp.tile(q_norm_g, (1, nq))
    gk = jnp.tile(k_norm_g, (1, nq // GROUP))
    sink_rows = jnp.broadcast_to(sinks[0][:, None], (nq, LANES))

    wts = {}
    h1 = _rms_fwd("ffn1_norm", xs, g_ffn1)
    wts["w_gu1"], = fetch("gu1", [0], started[1][3])
    gu1, a1 = _ffn_up("ffn1_up", h1, wts["w_gu1"])
    wts["w_down1"], = fetch("down1", [1], a1)
    wd1 = wts["w_down1"].reshape(-1, d)
    x1 = _ffn_down("ffn1_down", a1, wd1, xs)
    h2 = _rms_fwd("mix_norm", x1, g_mix)
    wts["w_in"], conv_land = fetch("in", [2, 8], h2)
    w_in_full = wts["w_in"].reshape(4, 2, d, -1)
    conv_full = jnp.transpose(conv_land, (1, 0, 2)).reshape(8, cw)
    proj = _proj(h2, w_in_full)
    ca = _conv_fwd(proj, conv_full)
    qn, kn, vb = _qk_prep(proj, gq, gk, rope_tabs, cw, kw)
    o = _attn_fwd(qn, kn, vb, sink_rows)
    wts["w_out_conv"], wts["w_out_attn"] = fetch("out", [3, 4], o)
    merged, ya, yb = _mix_out(ca, o, wts["w_out_conv"], wts["w_out_attn"], proj)
    wts["w_o"], = fetch("o", [5], merged)
    wo = wts["w_o"].reshape(d, d)
    x2 = _mix_residual(merged, wo, x1)
    h3 = _rms_fwd("ffn2_norm", x2, g_ffn2)
    wts["w_gu2"], = fetch("gu2", [6], h3)
    gu2, a2 = _ffn_up("ffn2_up", h3, wts["w_gu2"])
    wts["w_down2"], = fetch("down2", [7], a2)
    wd2 = wts["w_down2"].reshape(-1, d)
    y = _ffn_down("ffn2_down", a2, wd2, x2)
    dy, sq = _loss_dy(y, target)
    loss = lax.psum(sq[0, 0] * (0.5 / d), ("x", "y", "c"))

    core = lax.axis_index("c").astype(jnp.int32).reshape(1)
    chip = (2 * lax.axis_index("x") + lax.axis_index("y")).astype(jnp.int32).reshape(1)
    def pair_start(tag, group, grads, deps=()):
        stacks = [grads[n].reshape((4, 2) + big[n].shape[1:]) for n in group]
        lands = [lax.empty((4,) + big[n].shape[1:], BF16) for n in group]
        return _pair_start("rs_pair_start_" + tag, stacks, lands, deps)

    def chip_start(tag, group, pending, after):
        send, recv, stacks, lands, _ = pending
        stacks, lands = _pair_wait("rs_pair_wait_" + tag, send, recv, stacks, lands, after)
        parts = [_pair_add("rs_pair_add_" + n, st, ld, core, add_tile(n)) for n, st, ld in zip(group, stacks, lands)]
        lands2 = [lax.empty((4,) + big[n].shape[1:], BF16) for n in group]
        return _chip_start("rs_chip_start_" + tag, parts, lands2)

    group_a, group_b, group_c = ["w_down2", "w_gu2"], ["w_o", "w_out_conv", "w_out_attn"], ["w_in"]
    group_d, group_e = ["w_down1"], ["w_gu1"]
    g = {}
    dgu2, a2 = _ffn_bwd_act("ffn2_bwd_act", dy, wd2, gu2)
    g["w_down2"] = _ffn_bwd_dwd("ffn2_bwd_dwd", a2, dy)
    g["w_gu2"] = _ffn_bwd_dwgu("ffn2_bwd_dwgu", h3, dgu2)
    pend_a = pair_start("a", group_a, g)
    dh3 = _ffn_bwd_dh("ffn2_bwd_dh", dgu2, wts["w_gu2"], deps=(pend_a[4],))
    ring_a = chip_start("a", group_a, pend_a, dh3)
    dx2, dg_ffn2 = _rms_bwd("ffn2_bwd_rms", x2, g_ffn2, dh3, dy, deps=(ring_a[4],))

    dya, dyb, dgates = _mix_bwd_gates(dx2, wo, ya, yb, proj, cw)
    g["w_o"] = _tn_matmul("mix_bwd_dwo", merged, dx2, min(d, 1024))
    g["w_out_conv"], g["w_out_attn"] = _out_proj_bwd_w(ca, o, dya, dyb, d // N_DEV)
    pend_b = pair_start("b", group_b, g)
    dca, do = _out_proj_bwd_act(dya, dyb, wts["w_out_conv"], wts["w_out_attn"], deps=(pend_b[4],))
    ring_b = chip_start("b", group_b, pend_b, do)
    d3, dconv_w = _conv_bwd(proj, conv_full, dca, deps=(ring_b[4],))
    dq, dkc, dkp, dvc, dvp, dsink = _attn_bwd(qn, kn, vb, sink_rows, do)
    dqkv, dgq, dgk = _qk_prep_bwd(proj, gq, gk, rope_tabs, dq, dkc, dkp, dvc, dvp, cw, kw)
    dproj = jnp.concatenate([d3[0], d3[1], d3[2], dqkv, dgates[0], dgates[1]], axis=1)
    g["w_in"] = _proj_bwd_w(h2, dproj)
    pend_c = pair_start("c", group_c, g)
    dh2 = _proj_bwd_act(dproj, w_in_full, deps=(pend_c[4],))
    ring_c = chip_start("c", group_c, pend_c, dh2)
    dx1, dg_mix = _rms_bwd("mix_bwd_rms", x1, g_mix, dh2, dx2, deps=(ring_c[4],))

    dgu1, a1 = _ffn_bwd_act("ffn1_bwd_act", dx1, wd1, gu1)
    g["w_down1"] = _ffn_bwd_dwd("ffn1_bwd_dwd", a1, dx1)
    pend_d = pair_start("d", group_d, g)
    g["w_gu1"] = _ffn_bwd_dwgu("ffn1_bwd_dwgu", h1, dgu1, deps=(pend_d[4],))
    ring_d = chip_start("d", group_d, pend_d, g["w_gu1"])
    pend_e = pair_start("e", group_e, g, deps=(ring_d[4],))

    big_out = {}

    def finish(tag, group, ring, after):
        send, recv, parts, lands2, _ = ring
        parts, lands2 = _chip_wait("rs_chip_wait_" + tag, send, recv, parts, lands2, after)
        for n, own, landed in zip(group, parts, lands2):
            res = _adamw_chips("adamw_" + n, chip, own, landed, big[n][0], big_m[n][0], big_v[n][0], row_tile(n))
            big_out[n] = [a[None] for a in res]
            after = res[0]
        return after

    after = finish("a", group_a, ring_a, pend_e[4])
    ring_e = chip_start("e", group_e, pend_e, after)
    dh1 = _ffn_bwd_dh("ffn1_bwd_dh", dgu1, wts["w_gu1"], deps=(ring_e[4],))
    grad_x, dg_ffn1 = _rms_bwd("ffn1_bwd_rms", xs, g_ffn1, dh1, dx1)
    after = grad_x
    for tag, group, ring in (("b", group_b, ring_b), ("c", group_c, ring_c), ("d", group_d, ring_d), ("e", group_e, ring_e)):
        after = finish(tag, group, ring, after)

    small = {"g_ffn1": dg_ffn1[0:1], "g_mix": dg_mix[0:1], "g_ffn2": dg_ffn2[0:1],
             "q_norm_g": dgq[0:1, :HEAD_DIM], "k_norm_g": dgk[0:1, :HEAD_DIM], "sinks": dsink[:, 0][None],
             "conv_w": dconv_w[0:CONV_K].reshape(1, -1)}
    small_w = {"g_ffn1": g_ffn1, "g_mix": g_mix, "g_ffn2": g_ffn2, "q_norm_g": q_norm_g, "k_norm_g": k_norm_g,
               "sinks": sinks, "conv_w": None}
    small_m = {"g_ffn1": m_g_ffn1, "g_mix": m_g_mix, "g_ffn2": m_g_ffn2, "q_norm_g": m_q_norm_g,
               "k_norm_g": m_k_norm_g, "sinks": m_sinks, "conv_w": m_conv_w}
    small_v = {"g_ffn1": v_g_ffn1, "g_mix": v_g_mix, "g_ffn2": v_g_ffn2, "q_norm_g": v_q_norm_g,
               "k_norm_g": v_k_norm_g, "sinks": v_sinks, "conv_w": v_conv_w}
    snames = list(small)
    widths = [small[n].shape[1] for n in snames]
    total = sum(widths)
    rows = -(-total // LANES)
    rows = -(-rows // 8) * 8

    def pack(vals):
        flat = jnp.concatenate([v.reshape(1, -1) for v in vals], axis=1)
        return jnp.pad(flat, ((0, 0), (0, rows * LANES - total))).reshape(rows, LANES)

    csh = cw // N_DEV

    def place_conv(local, fill):
        full = jnp.full((CONV_K, cw), fill, F32)
        return lax.dynamic_update_slice(full, local, (0, me * csh)).reshape(1, -1)

    pw = pack([small_w[n] if n != "conv_w" else place_conv(conv_w[0], 0.0) for n in snames])
    pm = pack([small_m[n] if n != "conv_w" else place_conv(m_conv_w[0], 0.0) for n in snames])
    pv = pack([small_v[n] if n != "conv_w" else place_conv(v_conv_w[0], 1.0) for n in snames])
    parts = _exchange("gather_small_grads", [pack([small[n] for n in snames])], gather=True, deps=(after,))[0]
    sg, sd, sm, sv = [a.reshape(1, -1) for a in _adamw("adamw_small", parts, pw, pm, pv, rows)]

    def unpack(flat, n):
        off = sum(widths[:snames.index(n)])
        piece = flat[:, off:off + widths[snames.index(n)]]
        if n == "conv_w":
            piece = lax.dynamic_slice(piece.reshape(CONV_K, cw), (0, me * csh), (CONV_K, csh))[None]
        return piece

    order = ["g_ffn1", "w_gu1", "w_down1", "g_mix", "w_in", "conv_w", "q_norm_g", "k_norm_g", "sinks",
             "w_out_conv", "w_out_attn", "w_o", "g_ffn2", "w_gu2", "w_down2"]
    outs = [loss, grad_x[None]]
    for idx, flat in enumerate((sg, sd, sm, sv)):
        for n in order:
            outs.append(big_out[n][idx] if n in big_out else unpack(flat, n))
    return tuple(outs)
```

```python
import functools

import jax
import jax.numpy as jnp
from jax import lax
from jax.experimental import pallas as pl
from jax.experimental.pallas import tpu as pltpu

F32 = jnp.float32
BF16 = jnp.bfloat16

N_DEV = 8
HEAD_DIM = 64
GROUP = 4
BLOCK = 128
ROT_DIM = 16
ROPE_THETA = 500000.0
RMS_EPS = 1e-6
NEG_INF = -1e30
ATTN_SCALE = HEAD_DIM ** -0.5
CONV_K = 3
LANES = 128
MXU_COLS = 256
VMEM_BYTES_V7X = 64 * 1024 * 1024
VMEM_CAP = VMEM_BYTES_V7X - 6 * 1024 * 1024

ADAM_LR = 0.001
ADAM_B1 = 0.9
ADAM_B2 = 0.999
ADAM_EPS = 1e-08
ADAM_WD = 0.01
ADAM_STEP = 10

NN = (((1,), (0,)), ((), ()))
NT = (((1,), (1,)), ((), ()))
TN = (((0,), (0,)), ((), ()))

MESH = pl.DeviceIdType.MESH


def _nbytes(shape, dtype):
    n = 1
    for s in shape:
        if s is not None:
            n *= s
    return n * jnp.dtype(dtype).itemsize


def _params(semantics, block_bytes, temp_bytes):
    assert 2 * block_bytes + temp_bytes <= VMEM_CAP, (block_bytes, temp_bytes)
    return pltpu.CompilerParams(dimension_semantics=semantics, vmem_limit_bytes=VMEM_CAP)


def _fused(name, grid, ins, outs, dots, epilogue, *, nk=1, acc_shape=None, temp_bytes=0,
           semantics=("parallel", "parallel", "arbitrary"), deps=()):
    n_in, n_out = len(ins), len(outs)
    n_dep = len(deps)

    def body(*refs):
        in_refs, out_refs = refs[:n_in], refs[n_in + n_dep:n_in + n_dep + n_out]
        scratch = refs[n_in + n_dep + n_out:]

        def products():
            if callable(dots):
                return dots(in_refs)
            total = None
            for ai, bi, contract in dots:
                a, b = in_refs[ai][...], in_refs[bi][...]
                a = a if a.dtype == BF16 else a.astype(BF16)
                b = b if b.dtype == BF16 else b.astype(BF16)
                p = lax.dot_general(a, b, contract, preferred_element_type=F32)
                total = p if total is None else total + p
            return total

        if nk == 1:
            epilogue(products() if dots else None, in_refs, out_refs)
        else:
            acc = scratch[0]
            k = pl.program_id(2)

            @pl.when(k == 0)
            def _():
                acc[...] = jnp.zeros_like(acc)

            acc[...] += products()

            @pl.when(k == nk - 1)
            def _():
                epilogue(acc[...], in_refs, out_refs)

    block_bytes = sum(_nbytes(spec.block_shape, a.dtype) for a, spec in ins)
    block_bytes += sum(_nbytes(spec.block_shape, s.dtype) for s, spec in outs)
    scratch_shapes = []
    if nk > 1:
        scratch_shapes.append(pltpu.VMEM(acc_shape, F32))
        temp_bytes += _nbytes(acc_shape, F32)
    res = pl.pallas_call(
        body, name=name, grid=grid,
        in_specs=[spec for _, spec in ins] + [pl.BlockSpec(memory_space=pl.ANY)] * n_dep,
        out_specs=[spec for _, spec in outs],
        out_shape=[s for s, _ in outs],
        scratch_shapes=scratch_shapes,
        compiler_params=_params(semantics, block_bytes, temp_bytes),
    )(*[a for a, _ in ins], *deps)
    return res


def _sds(shape, dtype):
    return jax.ShapeDtypeStruct(shape, dtype)


def _sigmoid(x):
    return jax.nn.sigmoid(x)


def _exchange(name, arrays, gather, deps=()):
    n = len(arrays)
    out_shapes = [((N_DEV,) + a.shape) if gather else a.shape for a in arrays]

    def body(*refs):
        srcs, dsts = refs[:n], refs[n + len(deps):2 * n + len(deps)]
        send_sems, recv_sems, local_sems = refs[2 * n + len(deps):]
        x, y, c = lax.axis_index("x"), lax.axis_index("y"), lax.axis_index("c")
        me = 4 * x + 2 * y + c
        copies = []
        for w in range(n):
            own = srcs[w] if gather else srcs[w].at[me]
            local = pltpu.make_async_copy(own, dsts[w].at[me], local_sems.at[w])
            local.start()
            copies.append(local)
            for k in range(1, N_DEV):
                px = (1 - x) if (k & 4) else x
                py = (1 - y) if (k & 2) else y
                pc = (1 - c) if (k & 1) else c
                peer = 4 * px + 2 * py + pc
                cp = pltpu.make_async_remote_copy(
                    src_ref=srcs[w] if gather else srcs[w].at[peer],
                    dst_ref=dsts[w].at[me],
                    send_sem=send_sems.at[w * (N_DEV - 1) + k - 1],
                    recv_sem=recv_sems.at[w * (N_DEV - 1) + k - 1],
                    device_id=(px, py, pc), device_id_type=MESH)
                cp.start()
                copies.append(cp)
        for cp in copies:
            cp.wait()

    hbm = pl.BlockSpec(memory_space=pltpu.HBM)
    return pl.pallas_call(
        body, name=name,
        in_specs=[hbm] * n + [pl.BlockSpec(memory_space=pl.ANY)] * len(deps), out_specs=[hbm] * n,
        out_shape=[_sds(s, a.dtype) for s, a in zip(out_shapes, arrays)],
        scratch_shapes=[pltpu.SemaphoreType.DMA((n * (N_DEV - 1),)),
                        pltpu.SemaphoreType.DMA((n * (N_DEV - 1),)),
                        pltpu.SemaphoreType.DMA((n,))],
    )(*arrays, *deps)


_HBM = pl.BlockSpec(memory_space=pltpu.HBM)
_SEM = pl.BlockSpec(memory_space=pltpu.SEMAPHORE)
_ANY = pl.BlockSpec(memory_space=pl.ANY)
_EFFECT = pltpu.SideEffectType.DATAFLOW_SIDE_EFFECTING
N_TARGETS = 4


def _mesh_pos():
    return lax.axis_index("x"), lax.axis_index("y"), lax.axis_index("c")


def _chip_peers(x, y, c):
    return [(1 - x, y, c), (x, 1 - y, c), (1 - x, 1 - y, c)]


def _dev_index(pos):
    return 4 * pos[0] + 2 * pos[1] + pos[2]


def _hbm_like(a):
    return pltpu.HBM(a.shape, a.dtype)


def _place_shard(name, w, out_dtype, me, tr, deps=()):
    r, c = w.shape
    n_dep = len(deps)

    def body(me_ref, w_ref, *rest):
        rest[n_dep][...] = w_ref[...].astype(out_dtype)

    grid_spec = pltpu.PrefetchScalarGridSpec(
        num_scalar_prefetch=1, grid=(r // tr,),
        in_specs=[pl.BlockSpec((tr, c), lambda i, me_ref: (i, 0))] + [_ANY] * n_dep,
        out_specs=pl.BlockSpec((None, tr, c), lambda i, me_ref: (me_ref[0], i, 0)))
    return pl.pallas_call(
        body, name=name, grid_spec=grid_spec, out_shape=_sds((N_DEV, r, c), out_dtype),
        compiler_params=_params(("parallel",), tr * c * 6, tr * c * 4),
    )(me, w, *deps)


def _gather_start(name, lands):
    n = len(lands)

    def body(*refs):
        bufs = refs[:n]
        send, recv = refs[n], refs[n + 1]
        token = refs[-1]
        x, y, c = _mesh_pos()
        me = _dev_index((x, y, c))
        targets = [(x, y, 1 - c)] + _chip_peers(x, y, c)
        for w in range(n):
            for k, to in enumerate(targets):
                pltpu.make_async_remote_copy(
                    src_ref=bufs[w].at[me], dst_ref=bufs[w].at[me],
                    send_sem=send.at[N_TARGETS * w + k], recv_sem=recv.at[N_TARGETS * w + k],
                    device_id=to, device_id_type=MESH).start()
        token[...] = jnp.zeros_like(token)

    sems = pltpu.SemaphoreType.DMA((N_TARGETS * n,))
    outs = pl.pallas_call(
        body, name=name,
        in_specs=[_HBM] * n, out_specs=[_SEM, _SEM] + [_HBM] * n + [_token_spec()],
        out_shape=[sems, sems] + [_hbm_like(a) for a in lands] + [_sds((8, LANES), F32)],
        input_output_aliases={i: 2 + i for i in range(n)},
        compiler_params=pltpu.CompilerParams(has_side_effects=_EFFECT),
    )(*lands)
    return outs[0], outs[1], list(outs[2:2 + n]), outs[-1]


def _gather_wait(name, positions, send, recv, lands, after):
    m = len(positions)

    def body(*refs):
        bufs = refs[:m]
        send_sems, recv_sems = refs[m], refs[m + 1]
        x, y, c = _mesh_pos()
        me = _dev_index((x, y, c))
        sources = [(x, y, 1 - c)] + _chip_peers(x, y, c)
        for j, w in enumerate(positions):
            for k, frm in enumerate(sources):
                cp = pltpu.make_async_remote_copy(
                    src_ref=bufs[j].at[me], dst_ref=bufs[j].at[_dev_index(frm)],
                    send_sem=send_sems.at[N_TARGETS * w + k], recv_sem=recv_sems.at[N_TARGETS * w + k],
                    device_id=frm, device_id_type=MESH)
                cp.wait_send()
                cp.wait_recv()

    outs = pl.pallas_call(
        body, name=name,
        in_specs=[_HBM] * m + [_SEM, _SEM, _ANY], out_specs=[_HBM] * m,
        out_shape=[_hbm_like(a) for a in lands],
        input_output_aliases={i: i for i in range(m)},
        compiler_params=pltpu.CompilerParams(has_side_effects=_EFFECT),
    )(*lands, send, recv, after)
    return list(outs)


def _forward_to_sibling(name, lands):
    m = len(lands)

    def body(*refs):
        bufs = refs[m:2 * m]
        send_sems, recv_sems = refs[2 * m], refs[2 * m + 1]
        x, y, c = _mesh_pos()
        copies = []
        for j in range(m):
            for k, chip in enumerate(_chip_peers(x, y, c)):
                block = bufs[j].at[_dev_index(chip)]
                cp = pltpu.make_async_remote_copy(
                    src_ref=block, dst_ref=block,
                    send_sem=send_sems.at[3 * j + k], recv_sem=recv_sems.at[3 * j + k],
                    device_id=(x, y, 1 - c), device_id_type=MESH)
                cp.start()
                copies.append(cp)
        for cp in copies:
            cp.wait()

    outs = pl.pallas_call(
        body, name=name,
        in_specs=[_HBM] * m, out_specs=[_HBM] * m,
        out_shape=[_sds(a.shape, a.dtype) for a in lands],
        input_output_aliases={i: i for i in range(m)},
        scratch_shapes=[pltpu.SemaphoreType.DMA((3 * m,)), pltpu.SemaphoreType.DMA((3 * m,))],
    )(*lands)
    return list(outs)


def _token_spec():
    return pl.BlockSpec(memory_space=pltpu.VMEM)


def _pair_start(name, stacks, lands, deps=()):
    n = len(stacks)
    n_dep = len(deps)

    def body(*refs):
        srcs, dsts = refs[:n], refs[n:2 * n]
        send, recv = refs[2 * n + n_dep], refs[2 * n + n_dep + 1]
        token = refs[-1]
        x, y, c = _mesh_pos()
        for w in range(n):
            for chip in range(4):
                pltpu.make_async_remote_copy(
                    src_ref=srcs[w].at[chip, 1 - c], dst_ref=dsts[w].at[chip],
                    send_sem=send.at[4 * w + chip], recv_sem=recv.at[4 * w + chip],
                    device_id=(x, y, 1 - c), device_id_type=MESH).start()
        token[...] = jnp.zeros_like(token)

    sems = pltpu.SemaphoreType.DMA((4 * n,))
    outs = pl.pallas_call(
        body, name=name,
        in_specs=[_HBM] * (2 * n) + [_ANY] * n_dep, out_specs=[_SEM, _SEM] + [_HBM] * (2 * n) + [_token_spec()],
        out_shape=[sems, sems] + [_hbm_like(a) for a in stacks] + [_hbm_like(a) for a in lands] + [_sds((8, LANES), F32)],
        input_output_aliases={i: 2 + i for i in range(2 * n)},
        compiler_params=pltpu.CompilerParams(has_side_effects=_EFFECT),
    )(*stacks, *lands, *deps)
    return outs[0], outs[1], list(outs[2:2 + n]), list(outs[2 + n:2 + 2 * n]), outs[-1]


def _pair_wait(name, send, recv, stacks, lands, after):
    n = len(stacks)

    def body(*refs):
        srcs, dsts = refs[:n], refs[n:2 * n]
        send_sems, recv_sems = refs[2 * n], refs[2 * n + 1]
        x, y, c = _mesh_pos()
        for w in range(n):
            for chip in range(4):
                cp = pltpu.make_async_remote_copy(
                    src_ref=srcs[w].at[chip, 1 - c], dst_ref=dsts[w].at[chip],
                    send_sem=send_sems.at[4 * w + chip], recv_sem=recv_sems.at[4 * w + chip],
                    device_id=(x, y, 1 - c), device_id_type=MESH)
                cp.wait_send()
                cp.wait_recv()

    outs = pl.pallas_call(
        body, name=name,
        in_specs=[_HBM] * (2 * n) + [_SEM, _SEM, _ANY], out_specs=[_HBM] * (2 * n),
        out_shape=[_hbm_like(a) for a in stacks] + [_hbm_like(a) for a in lands],
        input_output_aliases={i: i for i in range(2 * n)},
        compiler_params=pltpu.CompilerParams(has_side_effects=_EFFECT),
    )(*stacks, *lands, send, recv, after)
    return list(outs[:n]), list(outs[n:])


def _pair_add(name, stack, land, core, tr):
    _, _, r, c = stack.shape

    def body(core_ref, a_ref, b_ref, o_ref):
        o_ref[...] = (a_ref[...].astype(F32) + b_ref[...].astype(F32)).astype(BF16)

    grid_spec = pltpu.PrefetchScalarGridSpec(
        num_scalar_prefetch=1, grid=(4, r // tr),
        in_specs=[pl.BlockSpec((None, None, tr, c), lambda k, i, core_ref: (k, core_ref[0], i, 0)),
                  pl.BlockSpec((None, tr, c), lambda k, i, core_ref: (k, i, 0))],
        out_specs=pl.BlockSpec((None, tr, c), lambda k, i, core_ref: (k, i, 0)))
    return pl.pallas_call(
        body, name=name, grid_spec=grid_spec, out_shape=_sds((4, r, c), BF16),
        compiler_params=_params(("parallel", "parallel"), 3 * tr * c * 2, 3 * tr * c * 4),
    )(core, stack, land)


def _chip_start(name, parts, lands):
    n = len(parts)

    def body(*refs):
        srcs, dsts = refs[:n], refs[n:2 * n]
        send, recv = refs[2 * n], refs[2 * n + 1]
        token = refs[-1]
        x, y, c = _mesh_pos()
        for w in range(n):
            for k, to in enumerate(_chip_peers(x, y, c)):
                pltpu.make_async_remote_copy(
                    src_ref=srcs[w].at[2 * to[0] + to[1]], dst_ref=dsts[w].at[2 * x + y],
                    send_sem=send.at[3 * w + k], recv_sem=recv.at[3 * w + k],
                    device_id=to, device_id_type=MESH).start()
        token[...] = jnp.zeros_like(token)

    sems = pltpu.SemaphoreType.DMA((3 * n,))
    outs = pl.pallas_call(
        body, name=name,
        in_specs=[_HBM] * (2 * n), out_specs=[_SEM, _SEM] + [_HBM] * (2 * n) + [_token_spec()],
        out_shape=[sems, sems] + [_hbm_like(a) for a in parts] + [_hbm_like(a) for a in lands] + [_sds((8, LANES), F32)],
        input_output_aliases={i: 2 + i for i in range(2 * n)},
        compiler_params=pltpu.CompilerParams(has_side_effects=_EFFECT),
    )(*parts, *lands)
    return outs[0], outs[1], list(outs[2:2 + n]), list(outs[2 + n:2 + 2 * n]), outs[-1]


def _chip_wait(name, send, recv, parts, lands, after):
    n = len(parts)

    def body(*refs):
        srcs, dsts = refs[:n], refs[n:2 * n]
        send_sems, recv_sems = refs[2 * n], refs[2 * n + 1]
        x, y, c = _mesh_pos()
        for w in range(n):
            for k, frm in enumerate(_chip_peers(x, y, c)):
                chip = 2 * frm[0] + frm[1]
                cp = pltpu.make_async_remote_copy(
                    src_ref=srcs[w].at[chip], dst_ref=dsts[w].at[chip],
                    send_sem=send_sems.at[3 * w + k], recv_sem=recv_sems.at[3 * w + k],
                    device_id=frm, device_id_type=MESH)
                cp.wait_send()
                cp.wait_recv()

    outs = pl.pallas_call(
        body, name=name,
        in_specs=[_HBM] * (2 * n) + [_SEM, _SEM, _ANY], out_specs=[_HBM] * (2 * n),
        out_shape=[_hbm_like(a) for a in parts] + [_hbm_like(a) for a in lands],
        input_output_aliases={i: i for i in range(2 * n)},
        compiler_params=pltpu.CompilerParams(has_side_effects=_EFFECT),
    )(*parts, *lands, send, recv, after)
    return list(outs[:n]), list(outs[n:])


def _row_tile(t):
    return min(t, 256)


def _rms_fwd(name, x, g):
    t, d = x.shape
    tm = _row_tile(t)

    def epilogue(_, ins, outs):
        xv = ins[0][...]
        r = lax.rsqrt(jnp.mean(xv * xv, axis=-1, keepdims=True) + RMS_EPS)
        outs[0][...] = (xv * r * ins[1][...]).astype(BF16)

    row = pl.BlockSpec((tm, d), lambda i, j, k: (i, 0))
    vec = pl.BlockSpec((1, d), lambda i, j, k: (0, 0))
    return _fused(name, (t // tm, 1, 1), [(x, row), (g, vec)], [(_sds((t, d), BF16), row)], [], epilogue,
                  temp_bytes=4 * tm * d * 4)[0]


def _rms_bwd(name, x, g, dh, resid, deps=()):
    t, d = x.shape
    tm = _row_tile(t)

    def epilogue(_, ins, outs):
        xv, gv, dhv = ins[0][...], ins[1][...], ins[2][...]
        r = lax.rsqrt(jnp.mean(xv * xv, axis=-1, keepdims=True) + RMS_EPS)
        xh = xv * r
        u = dhv * gv
        dot = jnp.mean(u * xh, axis=-1, keepdims=True)
        outs[0][...] = ins[3][...] + r * (u - xh * dot)

        @pl.when(pl.program_id(0) == 0)
        def _():
            outs[1][...] = jnp.zeros_like(outs[1])

        outs[1][0:1, :] += jnp.sum(dhv * xh, axis=0, keepdims=True)

    row = pl.BlockSpec((tm, d), lambda i, j, k: (i, 0))
    vec = pl.BlockSpec((1, d), lambda i, j, k: (0, 0))
    acc = pl.BlockSpec((8, d), lambda i, j, k: (0, 0))
    return _fused(name, (t // tm, 1, 1), [(x, row), (g, vec), (dh, row), (resid, row)],
                  [(_sds((t, d), F32), row), (_sds((8, d), F32), acc)], [], epilogue,
                  temp_bytes=6 * tm * d * 4, semantics=("arbitrary", "arbitrary", "arbitrary"), deps=deps)


def _loss_dy(y, target):
    t, d = y.shape
    tm = _row_tile(t)

    def epilogue(_, ins, outs):
        e = ins[0][...] - ins[1][...]
        outs[0][...] = e * (1.0 / d)

        @pl.when(pl.program_id(0) == 0)
        def _():
            outs[1][...] = jnp.zeros_like(outs[1])

        part = jnp.sum(jnp.sum(e * e, axis=1, keepdims=True), axis=0, keepdims=True)
        outs[1][...] += jnp.broadcast_to(part, outs[1].shape)

    row = pl.BlockSpec((tm, d), lambda i, j, k: (i, 0))
    acc = pl.BlockSpec((8, LANES), lambda i, j, k: (0, 0))
    return _fused("loss_dy", (t // tm, 1, 1), [(y, row), (target, row)],
                  [(_sds((t, d), F32), row), (_sds((8, LANES), F32), acc)], [], epilogue,
                  temp_bytes=3 * tm * d * 4, semantics=("arbitrary", "arbitrary", "arbitrary"))


def _ffn_up(name, h, wgu):
    t, d = h.shape
    nb = wgu.shape[2]
    f = 4 * nb
    tm = min(t, 512)

    def body(h_ref, wg_ref, wu_ref, gu_ref, a_ref):
        hv = h_ref[...]
        for c0 in range(0, nb, MXU_COLS):
            cs = slice(c0, min(c0 + MXU_COLS, nb))
            g = jnp.dot(hv, wg_ref[:, cs], preferred_element_type=F32)
            u = jnp.dot(hv, wu_ref[:, cs], preferred_element_type=F32)
            gu_ref[0, :, cs] = g.astype(BF16)
            gu_ref[1, :, cs] = u.astype(BF16)
            a_ref[:, cs] = (g * _sigmoid(g) * u).astype(BF16)

    blocks = tm * d * 2 + 2 * d * nb * 2 + 3 * tm * nb * 2
    return pl.pallas_call(
        body, name=name, grid=(4, t // tm),
        in_specs=[pl.BlockSpec((tm, d), lambda j, i: (i, 0)),
                  pl.BlockSpec((None, d, nb), lambda j, i: (j, 0, 0)),
                  pl.BlockSpec((None, d, nb), lambda j, i: (j + 4, 0, 0))],
        out_specs=[pl.BlockSpec((2, tm, nb), lambda j, i: (0, i, j)),
                   pl.BlockSpec((tm, nb), lambda j, i: (i, j))],
        out_shape=[_sds((2, t, f), BF16), _sds((t, f), BF16)],
        compiler_params=_params(("parallel", "parallel"), blocks, 8 * tm * MXU_COLS * 4),
    )(h, wgu, wgu)


def _ffn_down(name, a, wd, x):
    t, f = a.shape
    d = wd.shape[1]
    tm = min(t, 512)
    tn = min(d, 1024)

    def epilogue(acc, ins, outs):
        outs[0][...] = ins[2][...] + 0.5 * acc

    blk = pl.BlockSpec((tm, tn), lambda j, i, k: (i, j))
    return _fused(name, (d // tn, t // tm, 1),
                  [(a, pl.BlockSpec((tm, f), lambda j, i, k: (i, 0))),
                   (wd, pl.BlockSpec((f, tn), lambda j, i, k: (0, j))),
                   (x, blk)],
                  [(_sds((t, d), F32), blk)],
                  [(0, 1, NN)], epilogue, temp_bytes=2 * tm * tn * 4)[0]


def _ffn_bwd_act(name, dy, wd, gu, deps=()):
    t, d = dy.shape
    f = wd.shape[0]
    nb = f // 4
    tm = min(t, 512)

    def body(dy_ref, wd_ref, gu_ref, *rest):
        dgu_ref, a_ref = rest[-2], rest[-1]
        dyv = dy_ref[...].astype(BF16)
        for c0 in range(0, nb, MXU_COLS):
            cs = slice(c0, min(c0 + MXU_COLS, nb))
            da = 0.5 * lax.dot_general(dyv, wd_ref[cs, :], NT, preferred_element_type=F32)
            g = gu_ref[0, :, cs].astype(F32)
            u = gu_ref[1, :, cs].astype(F32)
            s = _sigmoid(g)
            silu = g * s
            dgu_ref[0, :, cs] = (da * u * (s * (1.0 + g * (1.0 - s)))).astype(BF16)
            dgu_ref[1, :, cs] = (da * silu).astype(BF16)
            a_ref[:, cs] = (silu * u).astype(BF16)

    blocks = tm * d * 4 + nb * d * 2 + 5 * tm * nb * 2
    return pl.pallas_call(
        body, name=name, grid=(4, t // tm),
        in_specs=[pl.BlockSpec((tm, d), lambda j, i: (i, 0)),
                  pl.BlockSpec((nb, d), lambda j, i: (j, 0)),
                  pl.BlockSpec((2, tm, nb), lambda j, i: (0, i, j))] + [_ANY] * len(deps),
        out_specs=[pl.BlockSpec((2, tm, nb), lambda j, i: (0, i, j)), pl.BlockSpec((tm, nb), lambda j, i: (i, j))],
        out_shape=[_sds((2, t, f), BF16), _sds((t, f), BF16)],
        compiler_params=_params(("parallel", "parallel"), blocks, tm * d * 2 + 8 * tm * MXU_COLS * 4),
    )(dy, wd, gu, *deps)


def _ffn_bwd_dwd(name, a, dy, deps=()):
    t, f = a.shape
    d = dy.shape[1]
    tm = f // 4
    tn = min(d, 512)

    def epilogue(acc, ins, outs):
        outs[0][...] = (0.5 * acc).astype(BF16)

    return _fused(name, (4, d // tn, 1),
                  [(a, pl.BlockSpec((t, tm), lambda i, j, k: (0, i))),
                   (dy, pl.BlockSpec((t, tn), lambda i, j, k: (0, j)))],
                  [(_sds((f, d), BF16), pl.BlockSpec((tm, tn), lambda i, j, k: (i, j)))],
                  [(0, 1, TN)], epilogue, temp_bytes=t * tn * 2 + 2 * tm * tn * 4, deps=deps)[0]


def _ffn_bwd_dh(name, dgu, wgu, deps=()):
    _, t, f = dgu.shape
    d, nb = wgu.shape[1], wgu.shape[2]
    tm = min(t, 512)

    def products(ins):
        return (lax.dot_general(ins[0][:, 0:nb], ins[1][0], NT, preferred_element_type=F32)
                + lax.dot_general(ins[0][:, nb:2 * nb], ins[1][1], NT, preferred_element_type=F32))

    def epilogue(acc, ins, outs):
        outs[0][...] = acc

    return _fused(name, (t // tm, 1, 4),
                  [(dgu, pl.BlockSpec((None, tm, 2 * nb), lambda i, j, k: (k // 2, i, k % 2))),
                   (wgu, pl.BlockSpec((2, d, nb), lambda i, j, k: (k, 0, 0)))],
                  [(_sds((t, d), F32), pl.BlockSpec((tm, d), lambda i, j, k: (i, 0)))],
                  products, epilogue, nk=4, acc_shape=(tm, d), temp_bytes=tm * d * 4, deps=deps)[0]


def _ffn_bwd_dwgu(name, h, dgu, deps=()):
    t, d = h.shape
    nb = dgu.shape[2] // 4
    tm = min(d, 512)

    def epilogue(acc, ins, outs):
        outs[0][...] = acc.astype(BF16)

    return _fused(name, (N_DEV, d // tm, 1),
                  [(h, pl.BlockSpec((t, tm), lambda i, j, k: (0, j))),
                   (dgu, pl.BlockSpec((None, t, nb), lambda i, j, k: (i // 4, 0, i % 4)))],
                  [(_sds((N_DEV, d, nb), BF16), pl.BlockSpec((None, tm, nb), lambda i, j, k: (i, j, 0)))],
                  [(0, 1, TN)], epilogue, temp_bytes=2 * tm * nb * 4, deps=deps)[0]


def _proj(h, w_in):
    t, d = h.shape
    nb = w_in.shape[3]
    tm = min(t, 512)

    def body(h_ref, w_ref, o_ref):
        hv = h_ref[...]
        o_ref[:, 0:nb] = jnp.dot(hv, w_ref[0], preferred_element_type=F32).astype(BF16)
        o_ref[:, nb:2 * nb] = jnp.dot(hv, w_ref[1], preferred_element_type=F32).astype(BF16)

    blocks = tm * d * 2 + 2 * d * nb * 2 + tm * 2 * nb * 4
    return pl.pallas_call(
        body, name="mix_proj", grid=(4, t // tm),
        in_specs=[pl.BlockSpec((tm, d), lambda j, i: (i, 0)),
                  pl.BlockSpec((None, 2, d, nb), lambda j, i: (j, 0, 0, 0))],
        out_specs=pl.BlockSpec((tm, 2 * nb), lambda j, i: (i, j)),
        out_shape=_sds((t, N_DEV * nb), BF16),
        compiler_params=_params(("parallel", "parallel"), blocks, 2 * tm * nb * 4),
    )(h, w_in)


def _shift_rows(u, k):
    t = u.shape[0]
    rolled = pltpu.roll(u, k % t, axis=0)
    row = lax.broadcasted_iota(jnp.int32, u.shape, 0)
    keep = (row >= k) if k > 0 else (row < t + k)
    return jnp.where(keep, rolled, 0.0)


def _conv_fwd(proj, conv_w):
    t = proj.shape[0]
    cw = conv_w.shape[1]
    tc = min(cw, 256)
    nc = cw // tc

    def epilogue(_, ins, outs):
        u = ins[2][...].astype(F32) * ins[0][...].astype(F32)
        w = ins[3][...]
        y = u * w[2:3, :] + _shift_rows(u, 1) * w[1:2, :] + _shift_rows(u, 2) * w[0:1, :]
        outs[0][...] = (ins[1][...].astype(F32) * y).astype(BF16)

    def col(seg):
        return pl.BlockSpec((t, tc), lambda i, j, k: (0, seg * nc + i))

    return _fused("conv_fwd", (nc, 1, 1),
                  [(proj, col(0)), (proj, col(1)), (proj, col(2)),
                   (conv_w, pl.BlockSpec((8, tc), lambda i, j, k: (0, i)))],
                  [(_sds((t, cw), BF16), pl.BlockSpec((t, tc), lambda i, j, k: (0, i)))],
                  [], epilogue, temp_bytes=6 * t * tc * 4)[0]


def _conv_bwd(proj, conv_w, dca, deps=()):
    t = proj.shape[0]
    cw = conv_w.shape[1]
    tc = min(cw, 256)
    nc = cw // tc

    def epilogue(_, ins, outs):
        xc, bg, cg = ins[0][...].astype(F32), ins[1][...].astype(F32), ins[2][...].astype(F32)
        w, dc = ins[3][...], ins[4][...]
        u = cg * xc
        u1, u2 = _shift_rows(u, 1), _shift_rows(u, 2)
        y = u * w[2:3, :] + u1 * w[1:2, :] + u2 * w[0:1, :]
        dconv = dc * bg
        du = dconv * w[2:3, :] + _shift_rows(dconv, -1) * w[1:2, :] + _shift_rows(dconv, -2) * w[0:1, :]
        outs[0][0] = (du * cg).astype(BF16)
        outs[0][1] = (dc * y).astype(BF16)
        outs[0][2] = (du * xc).astype(BF16)
        outs[1][...] = jnp.zeros_like(outs[1])
        outs[1][0:1, :] = jnp.sum(dconv * u2, axis=0, keepdims=True)
        outs[1][1:2, :] = jnp.sum(dconv * u1, axis=0, keepdims=True)
        outs[1][2:3, :] = jnp.sum(dconv * u, axis=0, keepdims=True)

    def col(seg):
        return pl.BlockSpec((t, tc), lambda i, j, k: (0, seg * nc + i))

    own = pl.BlockSpec((t, tc), lambda i, j, k: (0, i))
    wspec = pl.BlockSpec((8, tc), lambda i, j, k: (0, i))
    return _fused("conv_bwd", (nc, 1, 1),
                  [(proj, col(0)), (proj, col(1)), (proj, col(2)), (conv_w, wspec), (dca, own)],
                  [(_sds((3, t, cw), BF16), pl.BlockSpec((3, t, tc), lambda i, j, k: (0, 0, i))),
                   (_sds((8, cw), F32), wspec)],
                  [], epilogue, temp_bytes=10 * t * tc * 4, deps=deps)


def _split3(x):
    hi = x.astype(BF16)
    r1 = x - hi.astype(F32)
    mid = r1.astype(BF16)
    lo = (r1 - mid.astype(F32)).astype(BF16)
    return hi, mid, lo


def _head_selector(width):
    r = lax.broadcasted_iota(jnp.int32, (width, LANES), 0)
    c = lax.broadcasted_iota(jnp.int32, (width, LANES), 1)
    return (lax.shift_right_logical(r, 6) == c).astype(BF16)


def _head_sum(x, sel):
    return sum(jnp.dot(p, sel, preferred_element_type=F32) for p in _split3(x))


def _head_bcast(r, sel):
    return sum(lax.dot_general(p, sel, NT, preferred_element_type=F32) for p in _split3(r))


def _rope(x, c, sa, sb):
    n = x.shape[1]
    return x * c + pltpu.roll(x, n - ROT_DIM // 2, axis=1) * sa + pltpu.roll(x, ROT_DIM // 2, axis=1) * sb


def _rope_t(d, c, sa, sb):
    n = d.shape[1]
    return d * c + pltpu.roll(d * sa, ROT_DIM // 2, axis=1) + pltpu.roll(d * sb, n - ROT_DIM // 2, axis=1)


def _tile_lanes(tab, width):
    return tab if width == tab.shape[1] else jnp.tile(tab, (1, width // tab.shape[1]))


def _qk_prep(proj, gq, gk, rope_tabs, cw, kw):
    t = proj.shape[0]
    tm = _row_tile(t)

    def epilogue(_, ins, outs):
        c, sa, sb = ins[5][...], ins[6][...], ins[7][...]
        for src, gain, dst, width in ((0, 3, 0, cw), (1, 4, 1, kw)):
            xv = ins[src][...].astype(F32)
            sel = _head_selector(width)
            r = lax.rsqrt(_head_sum(xv * xv, sel) * (1.0 / HEAD_DIM) + RMS_EPS)
            xn = xv * _head_bcast(r, sel) * ins[gain][...]
            outs[dst][...] = _rope(xn, _tile_lanes(c, width), _tile_lanes(sa, width), _tile_lanes(sb, width)).astype(BF16)
        outs[2][...] = ins[2][...].astype(BF16)

    kblk = cw // kw
    tab = pl.BlockSpec((tm, LANES), lambda i, j, k: (i, 0))
    kspec = pl.BlockSpec((tm, kw), lambda i, j, k: (i, 0))
    return _fused("qk_prep", (t // tm, 1, 1),
                  [(proj, pl.BlockSpec((tm, cw), lambda i, j, k: (i, 3))),
                   (proj, pl.BlockSpec((tm, kw), lambda i, j, k: (i, 4 * kblk))),
                   (proj, pl.BlockSpec((tm, kw), lambda i, j, k: (i, 4 * kblk + 1))),
                   (gq, pl.BlockSpec((1, cw), lambda i, j, k: (0, 0))),
                   (gk, pl.BlockSpec((1, kw), lambda i, j, k: (0, 0))),
                   (rope_tabs[0], tab), (rope_tabs[1], tab), (rope_tabs[2], tab)],
                  [(_sds((t, cw), BF16), pl.BlockSpec((tm, cw), lambda i, j, k: (i, 0))),
                   (_sds((t, kw), BF16), kspec), (_sds((t, kw), BF16), kspec)],
                  [], epilogue, temp_bytes=12 * tm * cw * 4)


def _qk_prep_bwd(proj, gq, gk, rope_tabs, dq, dkc, dkp, dvc, dvp, cw, kw):
    t = proj.shape[0]
    tm = BLOCK
    nblk = t // tm

    def epilogue(_, ins, outs):
        c, sa, sb = ins[5][...], ins[6][...], ins[7][...]
        has_next = (pl.program_id(0) < nblk - 1).astype(F32)
        dk = ins[9][...] + has_next * ins[10][...]
        dv = ins[11][...] + has_next * ins[12][...]
        pieces = []
        for src, gain, dval, dst, width in ((0, 3, ins[8][...], 1, cw), (1, 4, dk, 2, kw)):
            xv, gv = ins[src][...].astype(F32), ins[gain][...]
            sel = _head_selector(width)
            r = _head_bcast(lax.rsqrt(_head_sum(xv * xv, sel) * (1.0 / HEAD_DIM) + RMS_EPS), sel)
            xh = xv * r
            dxn = _rope_t(dval, _tile_lanes(c, width), _tile_lanes(sa, width), _tile_lanes(sb, width))
            u = dxn * gv
            dot = _head_bcast(_head_sum(u * xh, sel), sel) * (1.0 / HEAD_DIM)
            pieces.append((r * (u - xh * dot)).astype(BF16))
            ri = lax.broadcasted_iota(jnp.int32, (width, LANES), 0)
            ci = lax.broadcasted_iota(jnp.int32, (width, LANES), 1)
            fold = (lax.bitwise_and(ri, HEAD_DIM - 1) == ci).astype(BF16)
            colsum = jnp.broadcast_to(jnp.sum(dxn * xh, axis=0, keepdims=True), (8, width))
            part = sum(jnp.dot(p, fold, preferred_element_type=F32) for p in _split3(colsum))

            @pl.when(pl.program_id(0) == 0)
            def _():
                outs[dst][...] = jnp.zeros_like(outs[dst])

            outs[dst][0:1, :] += part[0:1, :]
        outs[0][:, 0:cw] = pieces[0]
        outs[0][:, cw:cw + kw] = pieces[1]
        outs[0][:, cw + kw:cw + 2 * kw] = dv.astype(BF16)

    kblk = cw // kw
    tab = pl.BlockSpec((tm, LANES), lambda i, j, k: (i, 0))
    kcur = pl.BlockSpec((tm, kw), lambda i, j, k: (i, 0))
    knext = pl.BlockSpec((tm, kw), lambda i, j, k: (jnp.minimum(i + 1, nblk - 1), 0))
    acc = pl.BlockSpec((8, LANES), lambda i, j, k: (0, 0))
    return _fused("qk_prep_bwd", (nblk, 1, 1),
                  [(proj, pl.BlockSpec((tm, cw), lambda i, j, k: (i, 3))),
                   (proj, pl.BlockSpec((tm, kw), lambda i, j, k: (i, 4 * kblk))),
                   (proj, pl.BlockSpec((tm, kw), lambda i, j, k: (i, 4 * kblk + 1))),
                   (gq, pl.BlockSpec((1, cw), lambda i, j, k: (0, 0))),
                   (gk, pl.BlockSpec((1, kw), lambda i, j, k: (0, 0))),
                   (rope_tabs[0], tab), (rope_tabs[1], tab), (rope_tabs[2], tab),
                   (dq, pl.BlockSpec((tm, cw), lambda i, j, k: (i, 0))),
                   (dkc, kcur), (dkp, knext), (dvc, kcur), (dvp, knext)],
                  [(_sds((t, cw + 2 * kw), BF16), pl.BlockSpec((tm, cw + 2 * kw), lambda i, j, k: (i, 0))),
                   (_sds((8, LANES), F32), acc), (_sds((8, LANES), F32), acc)],
                  [], epilogue, temp_bytes=16 * tm * cw * 4, semantics=("arbitrary", "arbitrary", "arbitrary"))


def _attn_mask(n):
    key = lax.broadcasted_iota(jnp.int32, (2 * BLOCK, GROUP * BLOCK), 0)
    qry = lax.bitwise_and(lax.broadcasted_iota(jnp.int32, (2 * BLOCK, GROUP * BLOCK), 1), BLOCK - 1)
    return (key > qry) & (key <= qry + BLOCK) & ((key >= BLOCK) | (n > 0))


def _stack_heads(x, h):
    return jnp.concatenate([x[:, (h * GROUP + g) * HEAD_DIM:(h * GROUP + g + 1) * HEAD_DIM] for g in range(GROUP)], axis=0)


def _softmax_with_sink(q4, k2, sink_ref, h, valid):
    sink = jnp.concatenate([sink_ref[h * GROUP + g:h * GROUP + g + 1, :] for g in range(GROUP)], axis=1)
    s = lax.dot_general(k2, q4, NT, preferred_element_type=F32) * ATTN_SCALE
    s = jnp.where(valid, s, NEG_INF)
    m = jnp.maximum(jnp.max(s, axis=0, keepdims=True), sink)
    p = jnp.exp(s - m)
    es = jnp.exp(sink - m)
    inv = 1.0 / (jnp.sum(p, axis=0, keepdims=True) + es)
    return p * inv, es * inv


def _attn_fwd(qn, kn, vb, sink_rows):
    t, cw = qn.shape
    kw = kn.shape[1]
    nkv = kw // HEAD_DIM

    def body(q_ref, kp_ref, kc_ref, vp_ref, vc_ref, sink_ref, o_ref):
        valid = _attn_mask(pl.program_id(0))
        qv = q_ref[...]
        kp, kc, vp, vc = kp_ref[...], kc_ref[...], vp_ref[...], vc_ref[...]
        outs = []
        for h in range(nkv):
            hs = slice(h * HEAD_DIM, (h + 1) * HEAD_DIM)
            k2 = jnp.concatenate([kp[:, hs], kc[:, hs]], axis=0)
            v2 = jnp.concatenate([vp[:, hs], vc[:, hs]], axis=0)
            pn, _ = _softmax_with_sink(_stack_heads(qv, h), k2, sink_ref, h, valid)
            o4 = lax.dot_general(pn.astype(BF16), v2, TN, preferred_element_type=F32)
            outs += [o4[g * BLOCK:(g + 1) * BLOCK] for g in range(GROUP)]
        o_ref[...] = jnp.concatenate(outs, axis=-1).astype(BF16)

    cur = lambda n: (n, 0)
    prev = lambda n: (jnp.maximum(n - 1, 0), 0)
    return pl.pallas_call(
        body, name="attn_fwd", grid=(t // BLOCK,),
        in_specs=[pl.BlockSpec((BLOCK, cw), cur),
                  pl.BlockSpec((BLOCK, kw), prev), pl.BlockSpec((BLOCK, kw), cur),
                  pl.BlockSpec((BLOCK, kw), prev), pl.BlockSpec((BLOCK, kw), cur),
                  pl.BlockSpec(sink_rows.shape, lambda n: (0, 0))],
        out_specs=pl.BlockSpec((BLOCK, cw), cur),
        out_shape=_sds((t, cw), BF16),
        compiler_params=_params(("parallel",), BLOCK * (cw + 4 * kw) * 2 + BLOCK * cw * 2, 8 << 20),
    )(qn, kn, kn, vb, vb, sink_rows)


def _attn_bwd(qn, kn, vb, sink_rows, do):
    t, cw = qn.shape
    kw = kn.shape[1]
    nkv = kw // HEAD_DIM
    nq = nkv * GROUP

    def body(q_ref, kp_ref, kc_ref, vp_ref, vc_ref, sink_ref, do_ref,
             dq_ref, dkc_ref, dkp_ref, dvc_ref, dvp_ref, dsink_ref):
        n = pl.program_id(0)
        valid = _attn_mask(n)
        qv, dov = q_ref[...], do_ref[...]
        kp, kc, vp, vc = kp_ref[...], kc_ref[...], vp_ref[...], vc_ref[...]
        dqs, dks, dvs, dsinks = [], [], [], []
        for h in range(nkv):
            hs = slice(h * HEAD_DIM, (h + 1) * HEAD_DIM)
            k2 = jnp.concatenate([kp[:, hs], kc[:, hs]], axis=0)
            v2 = jnp.concatenate([vp[:, hs], vc[:, hs]], axis=0)
            q4 = _stack_heads(qv, h)
            dob = _stack_heads(dov, h).astype(BF16)
            pn, psink = _softmax_with_sink(q4, k2, sink_ref, h, valid)
            dpn = lax.dot_general(v2, dob, NT, preferred_element_type=F32)
            dvs.append(jnp.dot(pn.astype(BF16), dob, preferred_element_type=F32))
            delta = jnp.sum(pn * dpn, axis=0, keepdims=True)
            ds = (pn * (dpn - delta) * ATTN_SCALE).astype(BF16)
            dks.append(jnp.dot(ds, q4, preferred_element_type=F32))
            dq4 = lax.dot_general(ds, k2, TN, preferred_element_type=F32)
            dsink4 = -psink * delta
            for g in range(GROUP):
                dqs.append(dq4[g * BLOCK:(g + 1) * BLOCK])
                dsinks.append(jnp.broadcast_to(jnp.sum(dsink4[:, g * BLOCK:(g + 1) * BLOCK], axis=1, keepdims=True), (1, LANES)))
        dq_ref[...] = jnp.concatenate(dqs, axis=-1)
        dkp_ref[...] = jnp.concatenate([d[:BLOCK] for d in dks], axis=-1)
        dkc_ref[...] = jnp.concatenate([d[BLOCK:] for d in dks], axis=-1)
        dvp_ref[...] = jnp.concatenate([d[:BLOCK] for d in dvs], axis=-1)
        dvc_ref[...] = jnp.concatenate([d[BLOCK:] for d in dvs], axis=-1)

        @pl.when(n == 0)
        def _():
            dsink_ref[...] = jnp.zeros_like(dsink_ref)

        dsink_ref[...] += jnp.concatenate(dsinks, axis=0)

    cur = lambda n: (n, 0)
    prev = lambda n: (jnp.maximum(n - 1, 0), 0)
    kspec = pl.BlockSpec((BLOCK, kw), cur)
    return pl.pallas_call(
        body, name="attn_bwd", grid=(t // BLOCK,),
        in_specs=[pl.BlockSpec((BLOCK, cw), cur),
                  pl.BlockSpec((BLOCK, kw), prev), kspec,
                  pl.BlockSpec((BLOCK, kw), prev), kspec,
                  pl.BlockSpec(sink_rows.shape, lambda n: (0, 0)),
                  pl.BlockSpec((BLOCK, cw), cur)],
        out_specs=[pl.BlockSpec((BLOCK, cw), cur), kspec, kspec, kspec, kspec,
                   pl.BlockSpec((nq, LANES), lambda n: (0, 0))],
        out_shape=[_sds((t, cw), F32)] + [_sds((t, kw), F32)] * 4 + [_sds((nq, LANES), F32)],
        compiler_params=_params(("arbitrary",), BLOCK * (cw + 4 * kw) * 2 + 2 * BLOCK * cw * 4 + 4 * BLOCK * kw * 4, 12 << 20),
    )(qn, kn, kn, vb, vb, sink_rows, do)


def _mix_out(ca, o, woc, woa, proj):
    t, cw = ca.shape
    nb = woc.shape[2]
    d = N_DEV * nb
    tm = min(t, 1024)
    ga0 = (3 * cw + cw + 2 * (cw // 4)) // nb

    def body(ca_ref, o_ref, woc_ref, woa_ref, ga_ref, gb_ref, m_ref, ya_ref, yb_ref):
        ya = jnp.dot(ca_ref[...], woc_ref[...], preferred_element_type=F32)
        yb = jnp.dot(o_ref[...], woa_ref[...], preferred_element_type=F32)
        ya_ref[...] = ya.astype(BF16)
        yb_ref[...] = yb.astype(BF16)
        m_ref[...] = (_sigmoid(ga_ref[...].astype(F32)) * ya + _sigmoid(gb_ref[...].astype(F32)) * yb).astype(BF16)

    act = pl.BlockSpec((tm, cw), lambda i, j: (i, 0))
    wsp = pl.BlockSpec((None, cw, nb), lambda i, j: (j, 0, 0))
    osp = pl.BlockSpec((tm, nb), lambda i, j: (i, j))
    blocks = 2 * tm * cw * 2 + 2 * cw * nb * 2 + 2 * tm * nb * 4 + 3 * tm * nb * 2
    return pl.pallas_call(
        body, name="mix_out", grid=(t // tm, N_DEV),
        in_specs=[act, act, wsp, wsp,
                  pl.BlockSpec((tm, nb), lambda i, j: (i, ga0 + j)),
                  pl.BlockSpec((tm, nb), lambda i, j: (i, ga0 + N_DEV + j))],
        out_specs=[osp, osp, osp],
        out_shape=[_sds((t, d), BF16)] * 3,
        compiler_params=_params(("parallel", "parallel"), blocks, 6 * tm * nb * 4),
    )(ca, o, woc, woa, proj, proj)


def _mix_residual(merged, wo, x):
    t, d = x.shape
    tm = min(t, 512)

    def epilogue(acc, ins, outs):
        outs[0][...] = ins[2][...] + acc

    row = pl.BlockSpec((tm, d), lambda i, j, k: (i, 0))
    return _fused("mix_residual", (t // tm, 1, 1),
                  [(merged, row), (wo, pl.BlockSpec((d, d), lambda i, j, k: (0, 0))), (x, row)],
                  [(_sds((t, d), F32), row)], [(0, 1, NN)], epilogue, temp_bytes=2 * tm * d * 4)[0]


def _mix_bwd_gates(dx, wo, ya, yb, proj, cw):
    t, d = dx.shape
    tm = min(t, 1024)
    tn = min(d, 512)
    ga0 = (4 * cw + 2 * (cw // 4)) // tn

    def epilogue(acc, ins, outs):
        sa, sb = _sigmoid(ins[4][...].astype(F32)), _sigmoid(ins[5][...].astype(F32))
        outs[0][...] = (acc * sa).astype(BF16)
        outs[1][...] = (acc * sb).astype(BF16)
        outs[2][0] = (acc * ins[2][...].astype(F32) * sa * (1.0 - sa)).astype(BF16)
        outs[2][1] = (acc * ins[3][...].astype(F32) * sb * (1.0 - sb)).astype(BF16)

    blk = pl.BlockSpec((tm, tn), lambda i, j, k: (i, j))
    return _fused("mix_bwd_gates", (t // tm, d // tn, 1),
                  [(dx, pl.BlockSpec((tm, d), lambda i, j, k: (i, 0))),
                   (wo, pl.BlockSpec((tn, d), lambda i, j, k: (j, 0))),
                   (ya, blk), (yb, blk),
                   (proj, pl.BlockSpec((tm, tn), lambda i, j, k: (i, ga0 + j))),
                   (proj, pl.BlockSpec((tm, tn), lambda i, j, k: (i, ga0 + d // tn + j)))],
                  [(_sds((t, d), BF16), blk), (_sds((t, d), BF16), blk),
                   (_sds((2, t, d), BF16), pl.BlockSpec((2, tm, tn), lambda i, j, k: (0, i, j)))],
                  [(0, 1, NT)], epilogue, temp_bytes=8 * tm * tn * 4)


def _tn_matmul(name, a, b, tm, out_dtype=BF16):
    t, m = a.shape
    n = b.shape[1]
    tk = min(t, 512)

    def epilogue(acc, ins, outs):
        outs[0][...] = acc.astype(out_dtype)

    return _fused(name, (m // tm, 1, t // tk),
                  [(a, pl.BlockSpec((tk, tm), lambda i, j, k: (k, i))),
                   (b, pl.BlockSpec((tk, n), lambda i, j, k: (k, 0)))],
                  [(_sds((m, n), out_dtype), pl.BlockSpec((tm, n), lambda i, j, k: (i, 0)))],
                  [(0, 1, TN)], epilogue, nk=t // tk, acc_shape=(tm, n), temp_bytes=tm * n * 4)[0]


def _out_proj_bwd_act(dya, dyb, woc, woa, deps=()):
    t, d = dya.shape
    kdim, nb = woc.shape[1], woc.shape[2]
    tm = min(t, 512)

    def body(dya_ref, dyb_ref, woc_ref, woa_ref, *rest):
        for dy_ref, w_ref, o_ref in ((dya_ref, woc_ref, rest[-2]), (dyb_ref, woa_ref, rest[-1])):
            total = None
            for j in range(N_DEV):
                part = lax.dot_general(dy_ref[:, j * nb:(j + 1) * nb], w_ref[j], NT, preferred_element_type=F32)
                total = part if total is None else total + part
            o_ref[...] = total

    row = pl.BlockSpec((tm, d), lambda i: (i, 0))
    wsp = pl.BlockSpec((N_DEV, kdim, nb), lambda i: (0, 0, 0))
    osp = pl.BlockSpec((tm, kdim), lambda i: (i, 0))
    blocks = 2 * tm * d * 2 + 2 * N_DEV * kdim * nb * 2 + 2 * tm * kdim * 4
    return pl.pallas_call(
        body, name="mix_bwd_dca_do", grid=(t // tm,),
        in_specs=[row, row, wsp, wsp] + [_ANY] * len(deps), out_specs=[osp, osp],
        out_shape=[_sds((t, kdim), F32)] * 2,
        compiler_params=_params(("parallel",), blocks, 4 * tm * kdim * 4),
    )(dya, dyb, woc, woa, *deps)


def _out_proj_bwd_w(ca, o, dya, dyb, nb):
    t, kdim = ca.shape

    def body(ca_ref, o_ref, dya_ref, dyb_ref, dwoc_ref, dwoa_ref):
        dwoc_ref[...] = lax.dot_general(ca_ref[...], dya_ref[...], TN, preferred_element_type=F32).astype(BF16)
        dwoa_ref[...] = lax.dot_general(o_ref[...], dyb_ref[...], TN, preferred_element_type=F32).astype(BF16)

    act = pl.BlockSpec((t, kdim), lambda j: (0, 0))
    col = pl.BlockSpec((t, nb), lambda j: (0, j))
    osp = pl.BlockSpec((None, kdim, nb), lambda j: (j, 0, 0))
    blocks = 2 * t * kdim * 2 + 2 * t * nb * 2 + 2 * kdim * nb * 2
    return pl.pallas_call(
        body, name="mix_bwd_dwoc_dwoa", grid=(N_DEV,),
        in_specs=[act, act, col, col], out_specs=[osp, osp],
        out_shape=[_sds((N_DEV, kdim, nb), BF16)] * 2,
        compiler_params=_params(("parallel",), blocks, 4 * kdim * nb * 4),
    )(ca, o, dya, dyb)


def _proj_bwd_act(dproj, w_in, deps=()):
    t, n = dproj.shape
    d, nb = w_in.shape[2], w_in.shape[3]
    tm = min(t, 512)

    def epilogue(acc, ins, outs):
        outs[0][...] = acc

    def products(ins):
        return (lax.dot_general(ins[0][:, 0:nb], ins[1][0], NT, preferred_element_type=F32)
                + lax.dot_general(ins[0][:, nb:2 * nb], ins[1][1], NT, preferred_element_type=F32))

    return _fused("mix_bwd_dh", (t // tm, 1, 4),
                  [(dproj, pl.BlockSpec((tm, 2 * nb), lambda i, j, k: (i, k))),
                   (w_in, pl.BlockSpec((None, 2, d, nb), lambda i, j, k: (k, 0, 0, 0)))],
                  [(_sds((t, d), F32), pl.BlockSpec((tm, d), lambda i, j, k: (i, 0)))],
                  products, epilogue, nk=4, acc_shape=(tm, d), temp_bytes=tm * d * 4, deps=deps)[0]


def _proj_bwd_w(h, dproj):
    t, d = h.shape
    nb = dproj.shape[1] // N_DEV
    tm = min(d, 512)

    def body(h_ref, dp_ref, o_ref):
        hv = h_ref[...]
        o_ref[0] = lax.dot_general(hv, dp_ref[:, 0:nb], TN, preferred_element_type=F32).astype(BF16)
        o_ref[1] = lax.dot_general(hv, dp_ref[:, nb:2 * nb], TN, preferred_element_type=F32).astype(BF16)

    blocks = t * tm * 2 + t * 2 * nb * 2 + 2 * tm * nb * 2
    return pl.pallas_call(
        body, name="mix_bwd_dwin", grid=(4, d // tm),
        in_specs=[pl.BlockSpec((t, tm), lambda j, i: (0, i)),
                  pl.BlockSpec((t, 2 * nb), lambda j, i: (0, j))],
        out_specs=pl.BlockSpec((None, 2, tm, nb), lambda j, i: (j, 0, i, 0)),
        out_shape=_sds((4, 2, d, nb), BF16),
        compiler_params=_params(("parallel", "parallel"), blocks, 4 * tm * nb * 4),
    )(h, dproj)


def _adamw_math(w, g, m, v):
    m = ADAM_B1 * m + (1.0 - ADAM_B1) * g
    v = ADAM_B2 * v + (1.0 - ADAM_B2) * (g * g)
    m_hat = m / (1.0 - ADAM_B1 ** ADAM_STEP)
    v_hat = v / (1.0 - ADAM_B2 ** ADAM_STEP)
    delta = -ADAM_LR * (m_hat / (jnp.sqrt(v_hat) + ADAM_EPS) + ADAM_WD * w)
    return delta, m, v


def _adamw(name, parts, w, m, v, tr):
    r, c = w.shape

    def body(p_ref, w_ref, m_ref, v_ref, g_out, d_out, m_out, v_out):
        g = p_ref[0].astype(F32)
        for s in range(1, N_DEV):
            g = g + p_ref[s].astype(F32)
        delta, mn, vn = _adamw_math(w_ref[...], g, m_ref[...], v_ref[...])
        g_out[...] = g
        d_out[...] = delta
        m_out[...] = mn
        v_out[...] = vn

    blk = pl.BlockSpec((tr, c), lambda i: (i, 0))
    blocks = N_DEV * tr * c * parts.dtype.itemsize + 7 * tr * c * 4
    return pl.pallas_call(
        body, name=name, grid=(r // tr,),
        in_specs=[pl.BlockSpec((N_DEV, tr, c), lambda i: (0, i, 0)), blk, blk, blk],
        out_specs=[blk] * 4, out_shape=[_sds((r, c), F32)] * 4,
        compiler_params=_params(("parallel",), blocks, 6 * tr * c * 4),
    )(parts, w, m, v)


def _adamw_chips(name, chip, own, landed, w, m, v, tr, deps=()):
    r, c = w.shape

    def body(chip_ref, own_ref, land_ref, w_ref, m_ref, v_ref, *rest):
        g_out, d_out, m_out, v_out = rest[len(deps):]
        mine = own_ref[...].astype(F32)
        g = jnp.zeros((tr, c), F32)
        for k in range(4):
            g = g + jnp.where(chip_ref[0] == k, mine, land_ref[k].astype(F32))
        delta, mn, vn = _adamw_math(w_ref[...], g, m_ref[...], v_ref[...])
        g_out[...] = g
        d_out[...] = delta
        m_out[...] = mn
        v_out[...] = vn

    blk = pl.BlockSpec((tr, c), lambda i, chip_ref: (i, 0))
    grid_spec = pltpu.PrefetchScalarGridSpec(
        num_scalar_prefetch=1, grid=(r // tr,),
        in_specs=[pl.BlockSpec((None, tr, c), lambda i, chip_ref: (chip_ref[0], i, 0)),
                  pl.BlockSpec((4, tr, c), lambda i, chip_ref: (0, i, 0)), blk, blk, blk] + [_ANY] * len(deps),
        out_specs=[blk] * 4)
    blocks = 5 * tr * c * 2 + 7 * tr * c * 4
    return pl.pallas_call(
        body, name=name, grid_spec=grid_spec, out_shape=[_sds((r, c), F32)] * 4,
        compiler_params=_params(("parallel",), blocks, 6 * tr * c * 4),
    )(chip, own, landed, w, m, v, *deps)


def _rope_tables(t):
    half = ROT_DIM // 2
    inv_freq = 1.0 / (ROPE_THETA ** (jnp.arange(0, ROT_DIM, 2, dtype=F32) / ROT_DIM))
    ang = jnp.arange(t, dtype=F32)[:, None] * inv_freq[None, :]
    cos, sin = jnp.cos(ang), jnp.sin(ang)
    ones = jnp.ones((t, HEAD_DIM - ROT_DIM), F32)
    zeros = jnp.zeros((t, HEAD_DIM - half), F32)
    c = jnp.concatenate([cos, cos, ones], axis=1)
    sa = jnp.concatenate([-sin, zeros], axis=1)
    sb = jnp.concatenate([jnp.zeros((t, half), F32), sin, jnp.zeros((t, HEAD_DIM - ROT_DIM), F32)], axis=1)
    return tuple(jnp.tile(a, (1, LANES // HEAD_DIM)) for a in (c, sa, sb))


def _pad_rows(a, rows=8):
    return jnp.pad(a, ((0, rows - a.shape[0]), (0, 0)))


def kernel(x, g_ffn1, w_gu1, w_down1, g_mix, w_in, conv_w, q_norm_g, k_norm_g, sinks, w_out_conv, w_out_attn, w_o, g_ffn2, w_gu2, w_down2, loss_target, m_g_ffn1, m_w_gu1, m_w_down1, m_g_mix, m_w_in, m_conv_w, m_q_norm_g, m_k_norm_g, m_sinks, m_w_out_conv, m_w_out_attn, m_w_o, m_g_ffn2, m_w_gu2, m_w_down2, v_g_ffn1, v_w_gu1, v_w_down1, v_g_mix, v_w_in, v_conv_w, v_q_norm_g, v_k_norm_g, v_sinks, v_w_out_conv, v_w_out_attn, v_w_o, v_g_ffn2, v_w_gu2, v_w_down2):
    t, d = x.shape[1], x.shape[2]
    cw = d // 2
    kw = cw // GROUP
    nq = cw // HEAD_DIM
    xs, target = x.reshape(t, d), loss_target.reshape(t, d)
    me = 4 * lax.axis_index("x") + 2 * lax.axis_index("y") + lax.axis_index("c")

    big = {"w_gu1": w_gu1, "w_down1": w_down1, "w_in": w_in, "w_out_conv": w_out_conv,
           "w_out_attn": w_out_attn, "w_o": w_o, "w_gu2": w_gu2, "w_down2": w_down2}
    big_m = {"w_gu1": m_w_gu1, "w_down1": m_w_down1, "w_in": m_w_in, "w_out_conv": m_w_out_conv,
             "w_out_attn": m_w_out_attn, "w_o": m_w_o, "w_gu2": m_w_gu2, "w_down2": m_w_down2}
    big_v = {"w_gu1": v_w_gu1, "w_down1": v_w_down1, "w_in": v_w_in, "w_out_conv": v_w_out_conv,
             "w_out_attn": v_w_out_attn, "w_o": v_w_o, "w_gu2": v_w_gu2, "w_down2": v_w_down2}
    names = list(big)

    tiles = {"w_gu1": 256, "w_gu2": 256, "w_in": 256, "w_down1": 176, "w_down2": 176,
             "w_out_conv": 1024, "w_out_attn": 1024, "w_o": 128}

    def row_tile(n):
        r = big[n].shape[1]
        return tiles[n] if r % tiles[n] == 0 else r

    def add_tile(n):
        r, c = big[n].shape[1], big[n].shape[2]
        while r * c * 2 > (3 << 20) and r % 32 == 0:
            r //= 2
        return r

    me_arr = me.astype(jnp.int32).reshape(1)
    sources = [(n, big[n][0], BF16, row_tile(n)) for n in names] + [("conv_w", _pad_rows(conv_w[0]), F32, 8)]
    issue_order = [0, 1, 2, 8, 3, 4, 5, 6, 7]
    first = _place_shard("place_" + names[0], sources[0][1], BF16, me_arr, sources[0][3])
    started = [_gather_start("gather_start_first", [first])]
    rest = [_place_shard("place_" + sources[i][0], sources[i][1], sources[i][2], me_arr, sources[i][3],
                         deps=(started[0][3],)) for i in issue_order[1:]]
    started.append(_gather_start("gather_start_rest", rest))
    where = {0: (0, 0)}
    where.update({i: (1, p) for p, i in enumerate(issue_order[1:])})

    def fetch(tag, idxs, after):
        call = where[idxs[0]][0]
        send, recv, stacks, _ = started[call]
        positions = [where[i][1] for i in idxs]
        got = _gather_wait("gather_wait_" + tag, positions, send, recv, [stacks[p] for p in positions], after)
        return _forward_to_sibling("gather_forward_" + tag, got)

    rope_tabs = _rope_tables(t)
    gq = jnp.tile(q_norm_g, (1, nq))
    gk = jnp.tile(k_norm_g, (1, nq // GROUP))
    sink_rows = jnp.broadcast_to(sinks[0][:, None], (nq, LANES))

    wts = {}
    h1 = _rms_fwd("ffn1_norm", xs, g_ffn1)
    wts["w_gu1"], = fetch("gu1", [0], started[1][3])
    gu1, a1 = _ffn_up("ffn1_up", h1, wts["w_gu1"])
    wts["w_down1"], = fetch("down1", [1], a1)
    wd1 = wts["w_down1"].reshape(-1, d)
    x1 = _ffn_down("ffn1_down", a1, wd1, xs)
    h2 = _rms_fwd("mix_norm", x1, g_mix)
    wts["w_in"], conv_land = fetch("in", [2, 8], h2)
    w_in_full = wts["w_in"].reshape(4, 2, d, -1)
    conv_full = jnp.transpose(conv_land, (1, 0, 2)).reshape(8, cw)
    proj = _proj(h2, w_in_full)
    ca = _conv_fwd(proj, conv_full)
    qn, kn, vb = _qk_prep(proj, gq, gk, rope_tabs, cw, kw)
    o = _attn_fwd(qn, kn, vb, sink_rows)
    wts["w_out_conv"], wts["w_out_attn"] = fetch("out", [3, 4], o)
    merged, ya, yb = _mix_out(ca, o, wts["w_out_conv"], wts["w_out_attn"], proj)
    wts["w_o"], = fetch("o", [5], merged)
    wo = wts["w_o"].reshape(d, d)
    x2 = _mix_residual(merged, wo, x1)
    h3 = _rms_fwd("ffn2_norm", x2, g_ffn2)
    wts["w_gu2"], = fetch("gu2", [6], h3)
    gu2, a2 = _ffn_up("ffn2_up", h3, wts["w_gu2"])
    wts["w_down2"], = fetch("down2", [7], a2)
    wd2 = wts["w_down2"].reshape(-1, d)
    y = _ffn_down("ffn2_down", a2, wd2, x2)
    dy, sq = _loss_dy(y, target)
    loss = lax.psum(sq[0, 0] * (0.5 / d), ("x", "y", "c"))

    core = lax.axis_index("c").astype(jnp.int32).reshape(1)
    chip = (2 * lax.axis_index("x") + lax.axis_index("y")).astype(jnp.int32).reshape(1)
    def pair_start(tag, group, grads, deps=()):
        stacks = [grads[n].reshape((4, 2) + big[n].shape[1:]) for n in group]
        lands = [lax.empty((4,) + big[n].shape[1:], BF16) for n in group]
        return _pair_start("rs_pair_start_" + tag, stacks, lands, deps)

    def chip_start(tag, group, pending, after):
        send, recv, stacks, lands, _ = pending
        stacks, lands = _pair_wait("rs_pair_wait_" + tag, send, recv, stacks, lands, after)
        parts = [_pair_add("rs_pair_add_" + n, st, ld, core, add_tile(n)) for n, st, ld in zip(group, stacks, lands)]
        lands2 = [lax.empty((4,) + big[n].shape[1:], BF16) for n in group]
        return _chip_start("rs_chip_start_" + tag, parts, lands2)

    group_a, group_b, group_c = ["w_down2", "w_gu2"], ["w_o", "w_out_conv", "w_out_attn"], ["w_in"]
    group_d, group_e = ["w_down1"], ["w_gu1"]
    g = {}
    dgu2, a2 = _ffn_bwd_act("ffn2_bwd_act", dy, wd2, gu2)
    g["w_down2"] = _ffn_bwd_dwd("ffn2_bwd_dwd", a2, dy)
    g["w_gu2"] = _ffn_bwd_dwgu("ffn2_bwd_dwgu", h3, dgu2)
    pend_a = pair_start("a", group_a, g)
    dh3 = _ffn_bwd_dh("ffn2_bwd_dh", dgu2, wts["w_gu2"], deps=(pend_a[4],))
    ring_a = chip_start("a", group_a, pend_a, dh3)
    dx2, dg_ffn2 = _rms_bwd("ffn2_bwd_rms", x2, g_ffn2, dh3, dy, deps=(ring_a[4],))

    dya, dyb, dgates = _mix_bwd_gates(dx2, wo, ya, yb, proj, cw)
    g["w_o"] = _tn_matmul("mix_bwd_dwo", merged, dx2, min(d, 1024))
    g["w_out_conv"], g["w_out_attn"] = _out_proj_bwd_w(ca, o, dya, dyb, d // N_DEV)
    pend_b = pair_start("b", group_b, g)
    dca, do = _out_proj_bwd_act(dya, dyb, wts["w_out_conv"], wts["w_out_attn"], deps=(pend_b[4],))
    ring_b = chip_start("b", group_b, pend_b, do)
    d3, dconv_w = _conv_bwd(proj, conv_full, dca, deps=(ring_b[4],))
    dq, dkc, dkp, dvc, dvp, dsink = _attn_bwd(qn, kn, vb, sink_rows, do)
    dqkv, dgq, dgk = _qk_prep_bwd(proj, gq, gk, rope_tabs, dq, dkc, dkp, dvc, dvp, cw, kw)
    dproj = jnp.concatenate([d3[0], d3[1], d3[2], dqkv, dgates[0], dgates[1]], axis=1)
    g["w_in"] = _proj_bwd_w(h2, dproj)
    pend_c = pair_start("c", group_c, g)
    dh2 = _proj_bwd_act(dproj, w_in_full, deps=(pend_c[4],))
    ring_c = chip_start("c", group_c, pend_c, dh2)
    dx1, dg_mix = _rms_bwd("mix_bwd_rms", x1, g_mix, dh2, dx2, deps=(ring_c[4],))

    dgu1, a1 = _ffn_bwd_act("ffn1_bwd_act", dx1, wd1, gu1)
    g["w_down1"] = _ffn_bwd_dwd("ffn1_bwd_dwd", a1, dx1)
    pend_d = pair_start("d", group_d, g)
    g["w_gu1"] = _ffn_bwd_dwgu("ffn1_bwd_dwgu", h1, dgu1, deps=(pend_d[4],))
    ring_d = chip_start("d", group_d, pend_d, g["w_gu1"])
    pend_e = pair_start("e", group_e, g, deps=(ring_d[4],))

    big_out = {}

    arrived = {}

    def wait_group(tag, group, ring, after):
        send, recv, parts, lands2, _ = ring
        parts, lands2 = _chip_wait("rs_chip_wait_" + tag, send, recv, parts, lands2, after)
        arrived.update({n: (own, landed) for n, own, landed in zip(group, parts, lands2)})

    def update(n, after):
        own, landed = arrived[n]
        res = _adamw_chips("adamw_" + n, chip, own, landed, big[n][0], big_m[n][0], big_v[n][0], row_tile(n), deps=(after,))
        big_out[n] = [a[None] for a in res]
        return res[0]

    wait_group("a", group_a, ring_a, pend_e[4])
    after = update("w_gu2", pend_e[4])
    ring_e = chip_start("e", group_e, pend_e, after)
    dh1 = _ffn_bwd_dh("ffn1_bwd_dh", dgu1, wts["w_gu1"], deps=(ring_e[4],))
    grad_x, dg_ffn1 = _rms_bwd("ffn1_bwd_rms", xs, g_ffn1, dh1, dx1)
    after = update("w_down2", grad_x)
    for tag, group, ring in (("b", group_b, ring_b), ("c", group_c, ring_c), ("d", group_d, ring_d), ("e", group_e, ring_e)):
        wait_group(tag, group, ring, after)
        for n in group:
            after = update(n, after)

    small = {"g_ffn1": dg_ffn1[0:1], "g_mix": dg_mix[0:1], "g_ffn2": dg_ffn2[0:1],
             "q_norm_g": dgq[0:1, :HEAD_DIM], "k_norm_g": dgk[0:1, :HEAD_DIM], "sinks": dsink[:, 0][None],
             "conv_w": dconv_w[0:CONV_K].reshape(1, -1)}
    small_w = {"g_ffn1": g_ffn1, "g_mix": g_mix, "g_ffn2": g_ffn2, "q_norm_g": q_norm_g, "k_norm_g": k_norm_g,
               "sinks": sinks, "conv_w": None}
    small_m = {"g_ffn1": m_g_ffn1, "g_mix": m_g_mix, "g_ffn2": m_g_ffn2, "q_norm_g": m_q_norm_g,
               "k_norm_g": m_k_norm_g, "sinks": m_sinks, "conv_w": m_conv_w}
    small_v = {"g_ffn1": v_g_ffn1, "g_mix": v_g_mix, "g_ffn2": v_g_ffn2, "q_norm_g": v_q_norm_g,
               "k_norm_g": v_k_norm_g, "sinks": v_sinks, "conv_w": v_conv_w}
    snames = list(small)
    widths = [small[n].shape[1] for n in snames]
    total = sum(widths)
    rows = -(-total // LANES)
    rows = -(-rows // 8) * 8

    def pack(vals):
        flat = jnp.concatenate([v.reshape(1, -1) for v in vals], axis=1)
        return jnp.pad(flat, ((0, 0), (0, rows * LANES - total))).reshape(rows, LANES)

    csh = cw // N_DEV

    def place_conv(local, fill):
        full = jnp.full((CONV_K, cw), fill, F32)
        return lax.dynamic_update_slice(full, local, (0, me * csh)).reshape(1, -1)

    pw = pack([small_w[n] if n != "conv_w" else place_conv(conv_w[0], 0.0) for n in snames])
    pm = pack([small_m[n] if n != "conv_w" else place_conv(m_conv_w[0], 0.0) for n in snames])
    pv = pack([small_v[n] if n != "conv_w" else place_conv(v_conv_w[0], 1.0) for n in snames])
    parts = _exchange("gather_small_grads", [pack([small[n] for n in snames])], gather=True, deps=(after,))[0]
    sg, sd, sm, sv = [a.reshape(1, -1) for a in _adamw("adamw_small", parts, pw, pm, pv, rows)]

    def unpack(flat, n):
        off = sum(widths[:snames.index(n)])
        piece = flat[:, off:off + widths[snames.index(n)]]
        if n == "conv_w":
            piece = lax.dynamic_slice(piece.reshape(CONV_K, cw), (0, me * csh), (CONV_K, csh))[None]
        return piece

    order = ["g_ffn1", "w_gu1", "w_down1", "g_mix", "w_in", "conv_w", "q_norm_g", "k_norm_g", "sinks",
             "w_out_conv", "w_out_attn", "w_o", "g_ffn2", "w_gu2", "w_down2"]
    outs = [loss, grad_x[None]]
    for idx, flat in enumerate((sg, sd, sm, sv)):
        for n in order:
            outs.append(big_out[n][idx] if n in big_out else unpack(flat, n))
    return tuple(outs)
```

```python
import functools

import jax
import jax.numpy as jnp
from jax import lax
from jax.experimental import pallas as pl
from jax.experimental.pallas import tpu as pltpu

F32 = jnp.float32
BF16 = jnp.bfloat16

N_DEV = 8
HEAD_DIM = 64
GROUP = 4
BLOCK = 128
ROT_DIM = 16
ROPE_THETA = 500000.0
RMS_EPS = 1e-6
NEG_INF = -1e30
ATTN_SCALE = HEAD_DIM ** -0.5
CONV_K = 3
LANES = 128
MXU_COLS = 256
VMEM_BYTES_V7X = 64 * 1024 * 1024
VMEM_CAP = VMEM_BYTES_V7X - 6 * 1024 * 1024

ADAM_LR = 0.001
ADAM_B1 = 0.9
ADAM_B2 = 0.999
ADAM_EPS = 1e-08
ADAM_WD = 0.01
ADAM_STEP = 10

NN = (((1,), (0,)), ((), ()))
NT = (((1,), (1,)), ((), ()))
TN = (((0,), (0,)), ((), ()))

MESH = pl.DeviceIdType.MESH


def _nbytes(shape, dtype):
    n = 1
    for s in shape:
        if s is not None:
            n *= s
    return n * jnp.dtype(dtype).itemsize


def _params(semantics, block_bytes, temp_bytes):
    assert 2 * block_bytes + temp_bytes <= VMEM_CAP, (block_bytes, temp_bytes)
    return pltpu.CompilerParams(dimension_semantics=semantics, vmem_limit_bytes=VMEM_CAP)


def _fused(name, grid, ins, outs, dots, epilogue, *, nk=1, acc_shape=None, temp_bytes=0,
           semantics=("parallel", "parallel", "arbitrary"), deps=(), side=None):
    n_main_in, n_main_out = len(ins), len(outs)
    if side is not None:
        ins, outs = list(ins) + list(side[0]), list(outs) + list(side[1])
    n_in, n_out = len(ins), len(outs)
    n_dep = len(deps)

    def body(*refs):
        in_refs, out_refs = refs[:n_in], refs[n_in + n_dep:n_in + n_dep + n_out]
        scratch = refs[n_in + n_dep + n_out:]
        if side is not None:
            side[2](in_refs[n_main_in:], out_refs[n_main_out:])

        def products():
            if callable(dots):
                return dots(in_refs)
            total = None
            for ai, bi, contract in dots:
                a, b = in_refs[ai][...], in_refs[bi][...]
                a = a if a.dtype == BF16 else a.astype(BF16)
                b = b if b.dtype == BF16 else b.astype(BF16)
                p = lax.dot_general(a, b, contract, preferred_element_type=F32)
                total = p if total is None else total + p
            return total

        if nk == 1:
            epilogue(products() if dots else None, in_refs, out_refs)
        else:
            acc = scratch[0]
            k = pl.program_id(2)

            @pl.when(k == 0)
            def _():
                acc[...] = jnp.zeros_like(acc)

            acc[...] += products()

            @pl.when(k == nk - 1)
            def _():
                epilogue(acc[...], in_refs, out_refs)

    block_bytes = sum(_nbytes(spec.block_shape, a.dtype) for a, spec in ins)
    block_bytes += sum(_nbytes(spec.block_shape, s.dtype) for s, spec in outs)
    scratch_shapes = []
    if nk > 1:
        scratch_shapes.append(pltpu.VMEM(acc_shape, F32))
        temp_bytes += _nbytes(acc_shape, F32)
    res = pl.pallas_call(
        body, name=name, grid=grid,
        in_specs=[spec for _, spec in ins] + [pl.BlockSpec(memory_space=pl.ANY)] * n_dep,
        out_specs=[spec for _, spec in outs],
        out_shape=[s for s, _ in outs],
        scratch_shapes=scratch_shapes,
        compiler_params=_params(semantics, block_bytes, temp_bytes),
    )(*[a for a, _ in ins], *deps)
    return res


def _sds(shape, dtype):
    return jax.ShapeDtypeStruct(shape, dtype)


def _sigmoid(x):
    return jax.nn.sigmoid(x)


def _exchange(name, arrays, gather, deps=()):
    n = len(arrays)
    out_shapes = [((N_DEV,) + a.shape) if gather else a.shape for a in arrays]

    def body(*refs):
        srcs, dsts = refs[:n], refs[n + len(deps):2 * n + len(deps)]
        send_sems, recv_sems, local_sems = refs[2 * n + len(deps):]
        x, y, c = lax.axis_index("x"), lax.axis_index("y"), lax.axis_index("c")
        me = 4 * x + 2 * y + c
        copies = []
        for w in range(n):
            own = srcs[w] if gather else srcs[w].at[me]
            local = pltpu.make_async_copy(own, dsts[w].at[me], local_sems.at[w])
            local.start()
            copies.append(local)
            for k in range(1, N_DEV):
                px = (1 - x) if (k & 4) else x
                py = (1 - y) if (k & 2) else y
                pc = (1 - c) if (k & 1) else c
                peer = 4 * px + 2 * py + pc
                cp = pltpu.make_async_remote_copy(
                    src_ref=srcs[w] if gather else srcs[w].at[peer],
                    dst_ref=dsts[w].at[me],
                    send_sem=send_sems.at[w * (N_DEV - 1) + k - 1],
                    recv_sem=recv_sems.at[w * (N_DEV - 1) + k - 1],
                    device_id=(px, py, pc), device_id_type=MESH)
                cp.start()
                copies.append(cp)
        for cp in copies:
            cp.wait()

    hbm = pl.BlockSpec(memory_space=pltpu.HBM)
    return pl.pallas_call(
        body, name=name,
        in_specs=[hbm] * n + [pl.BlockSpec(memory_space=pl.ANY)] * len(deps), out_specs=[hbm] * n,
        out_shape=[_sds(s, a.dtype) for s, a in zip(out_shapes, arrays)],
        scratch_shapes=[pltpu.SemaphoreType.DMA((n * (N_DEV - 1),)),
                        pltpu.SemaphoreType.DMA((n * (N_DEV - 1),)),
                        pltpu.SemaphoreType.DMA((n,))],
    )(*arrays, *deps)


_HBM = pl.BlockSpec(memory_space=pltpu.HBM)
_SEM = pl.BlockSpec(memory_space=pltpu.SEMAPHORE)
_ANY = pl.BlockSpec(memory_space=pl.ANY)
_EFFECT = pltpu.SideEffectType.DATAFLOW_SIDE_EFFECTING
N_TARGETS = 4


def _mesh_pos():
    return lax.axis_index("x"), lax.axis_index("y"), lax.axis_index("c")


def _chip_peers(x, y, c):
    return [(1 - x, y, c), (x, 1 - y, c), (1 - x, 1 - y, c)]


def _dev_index(pos):
    return 4 * pos[0] + 2 * pos[1] + pos[2]


def _hbm_like(a):
    return pltpu.HBM(a.shape, a.dtype)


def _place_shard(name, w, out_dtype, me, tr, deps=()):
    r, c = w.shape
    n_dep = len(deps)

    def body(me_ref, w_ref, *rest):
        rest[n_dep][...] = w_ref[...].astype(out_dtype)

    grid_spec = pltpu.PrefetchScalarGridSpec(
        num_scalar_prefetch=1, grid=(r // tr,),
        in_specs=[pl.BlockSpec((tr, c), lambda i, me_ref: (i, 0))] + [_ANY] * n_dep,
        out_specs=pl.BlockSpec((None, tr, c), lambda i, me_ref: (me_ref[0], i, 0)))
    return pl.pallas_call(
        body, name=name, grid_spec=grid_spec, out_shape=_sds((N_DEV, r, c), out_dtype),
        compiler_params=_params(("parallel",), tr * c * 6, tr * c * 4),
    )(me, w, *deps)


def _gather_start(name, lands):
    n = len(lands)

    def body(*refs):
        bufs = refs[:n]
        send, recv = refs[n], refs[n + 1]
        token = refs[-1]
        x, y, c = _mesh_pos()
        me = _dev_index((x, y, c))
        targets = [(x, y, 1 - c)] + _chip_peers(x, y, c)
        for w in range(n):
            for k, to in enumerate(targets):
                pltpu.make_async_remote_copy(
                    src_ref=bufs[w].at[me], dst_ref=bufs[w].at[me],
                    send_sem=send.at[N_TARGETS * w + k], recv_sem=recv.at[N_TARGETS * w + k],
                    device_id=to, device_id_type=MESH).start()
        token[...] = jnp.zeros_like(token)

    sems = pltpu.SemaphoreType.DMA((N_TARGETS * n,))
    outs = pl.pallas_call(
        body, name=name,
        in_specs=[_HBM] * n, out_specs=[_SEM, _SEM] + [_HBM] * n + [_token_spec()],
        out_shape=[sems, sems] + [_hbm_like(a) for a in lands] + [_sds((8, LANES), F32)],
        input_output_aliases={i: 2 + i for i in range(n)},
        compiler_params=pltpu.CompilerParams(has_side_effects=_EFFECT),
    )(*lands)
    return outs[0], outs[1], list(outs[2:2 + n]), outs[-1]


def _gather_wait(name, positions, send, recv, lands, after):
    m = len(positions)

    def body(*refs):
        bufs = refs[:m]
        send_sems, recv_sems = refs[m], refs[m + 1]
        x, y, c = _mesh_pos()
        me = _dev_index((x, y, c))
        sources = [(x, y, 1 - c)] + _chip_peers(x, y, c)
        for j, w in enumerate(positions):
            for k, frm in enumerate(sources):
                cp = pltpu.make_async_remote_copy(
                    src_ref=bufs[j].at[me], dst_ref=bufs[j].at[_dev_index(frm)],
                    send_sem=send_sems.at[N_TARGETS * w + k], recv_sem=recv_sems.at[N_TARGETS * w + k],
                    device_id=frm, device_id_type=MESH)
                cp.wait_send()
                cp.wait_recv()

    outs = pl.pallas_call(
        body, name=name,
        in_specs=[_HBM] * m + [_SEM, _SEM, _ANY], out_specs=[_HBM] * m,
        out_shape=[_hbm_like(a) for a in lands],
        input_output_aliases={i: i for i in range(m)},
        compiler_params=pltpu.CompilerParams(has_side_effects=_EFFECT),
    )(*lands, send, recv, after)
    return list(outs)


def _forward_to_sibling(name, lands):
    m = len(lands)

    def body(*refs):
        bufs = refs[m:2 * m]
        send_sems, recv_sems = refs[2 * m], refs[2 * m + 1]
        x, y, c = _mesh_pos()
        copies = []
        for j in range(m):
            for k, chip in enumerate(_chip_peers(x, y, c)):
                block = bufs[j].at[_dev_index(chip)]
                cp = pltpu.make_async_remote_copy(
                    src_ref=block, dst_ref=block,
                    send_sem=send_sems.at[3 * j + k], recv_sem=recv_sems.at[3 * j + k],
                    device_id=(x, y, 1 - c), device_id_type=MESH)
                cp.start()
                copies.append(cp)
        for cp in copies:
            cp.wait()

    outs = pl.pallas_call(
        body, name=name,
        in_specs=[_HBM] * m, out_specs=[_HBM] * m,
        out_shape=[_sds(a.shape, a.dtype) for a in lands],
        input_output_aliases={i: i for i in range(m)},
        scratch_shapes=[pltpu.SemaphoreType.DMA((3 * m,)), pltpu.SemaphoreType.DMA((3 * m,))],
    )(*lands)
    return list(outs)


def _token_spec():
    return pl.BlockSpec(memory_space=pltpu.VMEM)


def _pair_start(name, stacks, lands, deps=()):
    n = len(stacks)
    n_dep = len(deps)

    def body(*refs):
        srcs, dsts = refs[:n], refs[n:2 * n]
        send, recv = refs[2 * n + n_dep], refs[2 * n + n_dep + 1]
        token = refs[-1]
        x, y, c = _mesh_pos()
        for w in range(n):
            for chip in range(4):
                pltpu.make_async_remote_copy(
                    src_ref=srcs[w].at[chip, 1 - c], dst_ref=dsts[w].at[chip],
                    send_sem=send.at[4 * w + chip], recv_sem=recv.at[4 * w + chip],
                    device_id=(x, y, 1 - c), device_id_type=MESH).start()
        token[...] = jnp.zeros_like(token)

    sems = pltpu.SemaphoreType.DMA((4 * n,))
    outs = pl.pallas_call(
        body, name=name,
        in_specs=[_HBM] * (2 * n) + [_ANY] * n_dep, out_specs=[_SEM, _SEM] + [_HBM] * (2 * n) + [_token_spec()],
        out_shape=[sems, sems] + [_hbm_like(a) for a in stacks] + [_hbm_like(a) for a in lands] + [_sds((8, LANES), F32)],
        input_output_aliases={i: 2 + i for i in range(2 * n)},
        compiler_params=pltpu.CompilerParams(has_side_effects=_EFFECT),
    )(*stacks, *lands, *deps)
    return outs[0], outs[1], list(outs[2:2 + n]), list(outs[2 + n:2 + 2 * n]), outs[-1]


def _pair_wait(name, send, recv, stacks, lands, after):
    n = len(stacks)

    def body(*refs):
        srcs, dsts = refs[:n], refs[n:2 * n]
        send_sems, recv_sems = refs[2 * n], refs[2 * n + 1]
        x, y, c = _mesh_pos()
        for w in range(n):
            for chip in range(4):
                cp = pltpu.make_async_remote_copy(
                    src_ref=srcs[w].at[chip, 1 - c], dst_ref=dsts[w].at[chip],
                    send_sem=send_sems.at[4 * w + chip], recv_sem=recv_sems.at[4 * w + chip],
                    device_id=(x, y, 1 - c), device_id_type=MESH)
                cp.wait_send()
                cp.wait_recv()

    outs = pl.pallas_call(
        body, name=name,
        in_specs=[_HBM] * (2 * n) + [_SEM, _SEM, _ANY], out_specs=[_HBM] * (2 * n),
        out_shape=[_hbm_like(a) for a in stacks] + [_hbm_like(a) for a in lands],
        input_output_aliases={i: i for i in range(2 * n)},
        compiler_params=pltpu.CompilerParams(has_side_effects=_EFFECT),
    )(*stacks, *lands, send, recv, after)
    return list(outs[:n]), list(outs[n:])


def _pair_add(name, stack, land, place, tr):
    _, _, r, c = stack.shape

    def body(place_ref, a_ref, b_ref, sums_ref, slots_ref):
        total = (a_ref[...].astype(F32) + b_ref[...].astype(F32)).astype(BF16)
        sums_ref[...] = total

        @pl.when(pl.program_id(1) == place_ref[1])
        def _():
            slots_ref[...] = total

    grid_spec = pltpu.PrefetchScalarGridSpec(
        num_scalar_prefetch=1, grid=(r // tr, 4),
        in_specs=[pl.BlockSpec((None, None, tr, c), lambda i, k, place_ref: (k, place_ref[0], i, 0)),
                  pl.BlockSpec((None, tr, c), lambda i, k, place_ref: (k, i, 0))],
        out_specs=[pl.BlockSpec((None, tr, c), lambda i, k, place_ref: (k, i, 0)),
                   pl.BlockSpec((None, tr, c), lambda i, k, place_ref: (place_ref[1], i, 0))])
    return pl.pallas_call(
        body, name=name, grid_spec=grid_spec, out_shape=[_sds((4, r, c), BF16)] * 2,
        compiler_params=_params(("parallel", "arbitrary"), 4 * tr * c * 2, 3 * tr * c * 4),
    )(place, stack, land)


def _chip_start(name, parts, lands):
    n = len(parts)

    def body(*refs):
        srcs, dsts = refs[:n], refs[n:2 * n]
        send, recv = refs[2 * n], refs[2 * n + 1]
        token = refs[-1]
        x, y, c = _mesh_pos()
        for w in range(n):
            for k, to in enumerate(_chip_peers(x, y, c)):
                pltpu.make_async_remote_copy(
                    src_ref=srcs[w].at[2 * to[0] + to[1]], dst_ref=dsts[w].at[2 * x + y],
                    send_sem=send.at[3 * w + k], recv_sem=recv.at[3 * w + k],
                    device_id=to, device_id_type=MESH).start()
        token[...] = jnp.zeros_like(token)

    sems = pltpu.SemaphoreType.DMA((3 * n,))
    outs = pl.pallas_call(
        body, name=name,
        in_specs=[_HBM] * (2 * n), out_specs=[_SEM, _SEM] + [_HBM] * (2 * n) + [_token_spec()],
        out_shape=[sems, sems] + [_hbm_like(a) for a in parts] + [_hbm_like(a) for a in lands] + [_sds((8, LANES), F32)],
        input_output_aliases={i: 2 + i for i in range(2 * n)},
        compiler_params=pltpu.CompilerParams(has_side_effects=_EFFECT),
    )(*parts, *lands)
    return outs[0], outs[1], list(outs[2:2 + n]), list(outs[2 + n:2 + 2 * n]), outs[-1]


def _chip_wait(name, send, recv, parts, lands, after):
    n = len(parts)

    def body(*refs):
        srcs, dsts = refs[:n], refs[n:2 * n]
        send_sems, recv_sems = refs[2 * n], refs[2 * n + 1]
        x, y, c = _mesh_pos()
        for w in range(n):
            for k, frm in enumerate(_chip_peers(x, y, c)):
                chip = 2 * frm[0] + frm[1]
                cp = pltpu.make_async_remote_copy(
                    src_ref=srcs[w].at[chip], dst_ref=dsts[w].at[chip],
                    send_sem=send_sems.at[3 * w + k], recv_sem=recv_sems.at[3 * w + k],
                    device_id=frm, device_id_type=MESH)
                cp.wait_send()
                cp.wait_recv()

    outs = pl.pallas_call(
        body, name=name,
        in_specs=[_HBM] * (2 * n) + [_SEM, _SEM, _ANY], out_specs=[_HBM] * (2 * n),
        out_shape=[_hbm_like(a) for a in parts] + [_hbm_like(a) for a in lands],
        input_output_aliases={i: i for i in range(2 * n)},
        compiler_params=pltpu.CompilerParams(has_side_effects=_EFFECT),
    )(*parts, *lands, send, recv, after)
    return list(outs[:n]), list(outs[n:])


def _row_tile(t):
    return min(t, 256)


def _rms_fwd(name, x, g):
    t, d = x.shape
    tm = _row_tile(t)

    def epilogue(_, ins, outs):
        xv = ins[0][...]
        r = lax.rsqrt(jnp.mean(xv * xv, axis=-1, keepdims=True) + RMS_EPS)
        outs[0][...] = (xv * r * ins[1][...]).astype(BF16)

    row = pl.BlockSpec((tm, d), lambda i, j, k: (i, 0))
    vec = pl.BlockSpec((1, d), lambda i, j, k: (0, 0))
    return _fused(name, (t // tm, 1, 1), [(x, row), (g, vec)], [(_sds((t, d), BF16), row)], [], epilogue,
                  temp_bytes=4 * tm * d * 4)[0]


def _rms_bwd(name, x, g, dh, resid, deps=()):
    t, d = x.shape
    tm = _row_tile(t)

    def epilogue(_, ins, outs):
        xv, gv, dhv = ins[0][...], ins[1][...], ins[2][...]
        r = lax.rsqrt(jnp.mean(xv * xv, axis=-1, keepdims=True) + RMS_EPS)
        xh = xv * r
        u = dhv * gv
        dot = jnp.mean(u * xh, axis=-1, keepdims=True)
        outs[0][...] = ins[3][...] + r * (u - xh * dot)

        @pl.when(pl.program_id(0) == 0)
        def _():
            outs[1][...] = jnp.zeros_like(outs[1])

        outs[1][0:1, :] += jnp.sum(dhv * xh, axis=0, keepdims=True)

    row = pl.BlockSpec((tm, d), lambda i, j, k: (i, 0))
    vec = pl.BlockSpec((1, d), lambda i, j, k: (0, 0))
    acc = pl.BlockSpec((8, d), lambda i, j, k: (0, 0))
    return _fused(name, (t // tm, 1, 1), [(x, row), (g, vec), (dh, row), (resid, row)],
                  [(_sds((t, d), F32), row), (_sds((8, d), F32), acc)], [], epilogue,
                  temp_bytes=6 * tm * d * 4, semantics=("arbitrary", "arbitrary", "arbitrary"), deps=deps)


def _loss_dy(y, target):
    t, d = y.shape
    tm = _row_tile(t)

    def epilogue(_, ins, outs):
        e = ins[0][...] - ins[1][...]
        outs[0][...] = e * (1.0 / d)

        @pl.when(pl.program_id(0) == 0)
        def _():
            outs[1][...] = jnp.zeros_like(outs[1])

        part = jnp.sum(jnp.sum(e * e, axis=1, keepdims=True), axis=0, keepdims=True)
        outs[1][...] += jnp.broadcast_to(part, outs[1].shape)

    row = pl.BlockSpec((tm, d), lambda i, j, k: (i, 0))
    acc = pl.BlockSpec((8, LANES), lambda i, j, k: (0, 0))
    return _fused("loss_dy", (t // tm, 1, 1), [(y, row), (target, row)],
                  [(_sds((t, d), F32), row), (_sds((8, LANES), F32), acc)], [], epilogue,
                  temp_bytes=3 * tm * d * 4, semantics=("arbitrary", "arbitrary", "arbitrary"))


def _ffn_up(name, h, wgu):
    t, d = h.shape
    nb = wgu.shape[2]
    f = 4 * nb
    tm = min(t, 512)

    def body(h_ref, wg_ref, wu_ref, gu_ref, a_ref):
        hv = h_ref[...]
        for c0 in range(0, nb, MXU_COLS):
            cs = slice(c0, min(c0 + MXU_COLS, nb))
            g = jnp.dot(hv, wg_ref[:, cs], preferred_element_type=F32)
            u = jnp.dot(hv, wu_ref[:, cs], preferred_element_type=F32)
            gu_ref[0, :, cs] = g.astype(BF16)
            gu_ref[1, :, cs] = u.astype(BF16)
            a_ref[:, cs] = (g * _sigmoid(g) * u).astype(BF16)

    blocks = tm * d * 2 + 2 * d * nb * 2 + 3 * tm * nb * 2
    return pl.pallas_call(
        body, name=name, grid=(4, t // tm),
        in_specs=[pl.BlockSpec((tm, d), lambda j, i: (i, 0)),
                  pl.BlockSpec((None, d, nb), lambda j, i: (j, 0, 0)),
                  pl.BlockSpec((None, d, nb), lambda j, i: (j + 4, 0, 0))],
        out_specs=[pl.BlockSpec((2, tm, nb), lambda j, i: (0, i, j)),
                   pl.BlockSpec((tm, nb), lambda j, i: (i, j))],
        out_shape=[_sds((2, t, f), BF16), _sds((t, f), BF16)],
        compiler_params=_params(("parallel", "parallel"), blocks, 8 * tm * MXU_COLS * 4),
    )(h, wgu, wgu)


def _ffn_down(name, a, wd, x):
    t, f = a.shape
    d = wd.shape[1]
    tm = min(t, 512)
    tn = min(d, 1024)

    def epilogue(acc, ins, outs):
        outs[0][...] = ins[2][...] + 0.5 * acc

    blk = pl.BlockSpec((tm, tn), lambda j, i, k: (i, j))
    return _fused(name, (d // tn, t // tm, 1),
                  [(a, pl.BlockSpec((tm, f), lambda j, i, k: (i, 0))),
                   (wd, pl.BlockSpec((f, tn), lambda j, i, k: (0, j))),
                   (x, blk)],
                  [(_sds((t, d), F32), blk)],
                  [(0, 1, NN)], epilogue, temp_bytes=2 * tm * tn * 4)[0]


def _ffn_bwd_act(name, dy, wd, gu, deps=()):
    t, d = dy.shape
    f = wd.shape[0]
    nb = f // 4
    tm = min(t, 512)

    def body(dy_ref, wd_ref, gu_ref, *rest):
        dgu_ref, a_ref = rest[-2], rest[-1]
        dyv = dy_ref[...].astype(BF16)
        for c0 in range(0, nb, MXU_COLS):
            cs = slice(c0, min(c0 + MXU_COLS, nb))
            da = 0.5 * lax.dot_general(dyv, wd_ref[cs, :], NT, preferred_element_type=F32)
            g = gu_ref[0, :, cs].astype(F32)
            u = gu_ref[1, :, cs].astype(F32)
            s = _sigmoid(g)
            silu = g * s
            dgu_ref[0, :, cs] = (da * u * (s * (1.0 + g * (1.0 - s)))).astype(BF16)
            dgu_ref[1, :, cs] = (da * silu).astype(BF16)
            a_ref[:, cs] = (silu * u).astype(BF16)

    blocks = tm * d * 4 + nb * d * 2 + 5 * tm * nb * 2
    return pl.pallas_call(
        body, name=name, grid=(4, t // tm),
        in_specs=[pl.BlockSpec((tm, d), lambda j, i: (i, 0)),
                  pl.BlockSpec((nb, d), lambda j, i: (j, 0)),
                  pl.BlockSpec((2, tm, nb), lambda j, i: (0, i, j))] + [_ANY] * len(deps),
        out_specs=[pl.BlockSpec((2, tm, nb), lambda j, i: (0, i, j)), pl.BlockSpec((tm, nb), lambda j, i: (i, j))],
        out_shape=[_sds((2, t, f), BF16), _sds((t, f), BF16)],
        compiler_params=_params(("parallel", "parallel"), blocks, tm * d * 2 + 8 * tm * MXU_COLS * 4),
    )(dy, wd, gu, *deps)


def _ffn_bwd_dwd(name, a, dy, deps=(), side=None):
    t, f = a.shape
    d = dy.shape[1]
    tm = f // 4
    tn = min(d, 512)

    def epilogue(acc, ins, outs):
        outs[0][...] = (0.5 * acc).astype(BF16)

    return _fused(name, (4, d // tn, 1),
                  [(a, pl.BlockSpec((t, tm), lambda i, j, k: (0, i))),
                   (dy, pl.BlockSpec((t, tn), lambda i, j, k: (0, j)))],
                  [(_sds((f, d), BF16), pl.BlockSpec((tm, tn), lambda i, j, k: (i, j)))],
                  [(0, 1, TN)], epilogue, temp_bytes=t * tn * 2 + 2 * tm * tn * 4, deps=deps, side=side)


def _ffn_bwd_dh(name, dgu, wgu, deps=(), side=None):
    _, t, f = dgu.shape
    d, nb = wgu.shape[1], wgu.shape[2]
    tm = min(t, 512)

    def products(ins):
        return (lax.dot_general(ins[0][:, 0:nb], ins[1][0], NT, preferred_element_type=F32)
                + lax.dot_general(ins[0][:, nb:2 * nb], ins[1][1], NT, preferred_element_type=F32))

    def epilogue(acc, ins, outs):
        outs[0][...] = acc

    return _fused(name, (t // tm, 1, 4),
                  [(dgu, pl.BlockSpec((None, tm, 2 * nb), lambda i, j, k: (k // 2, i, k % 2))),
                   (wgu, pl.BlockSpec((2, d, nb), lambda i, j, k: (k, 0, 0)))],
                  [(_sds((t, d), F32), pl.BlockSpec((tm, d), lambda i, j, k: (i, 0)))],
                  products, epilogue, nk=4, acc_shape=(tm, d), temp_bytes=tm * d * 4, deps=deps, side=side)


def _ffn_bwd_dwgu(name, h, dgu, deps=(), side=None):
    t, d = h.shape
    nb = dgu.shape[2] // 4
    tm = min(d, 512)

    def epilogue(acc, ins, outs):
        outs[0][...] = acc.astype(BF16)

    return _fused(name, (N_DEV, d // tm, 1),
                  [(h, pl.BlockSpec((t, tm), lambda i, j, k: (0, j))),
                   (dgu, pl.BlockSpec((None, t, nb), lambda i, j, k: (i // 4, 0, i % 4)))],
                  [(_sds((N_DEV, d, nb), BF16), pl.BlockSpec((None, tm, nb), lambda i, j, k: (i, j, 0)))],
                  [(0, 1, TN)], epilogue, temp_bytes=2 * tm * nb * 4, deps=deps, side=side)


def _proj(h, w_in):
    t, d = h.shape
    nb = w_in.shape[3]
    tm = min(t, 512)

    def body(h_ref, w_ref, o_ref):
        hv = h_ref[...]
        o_ref[:, 0:nb] = jnp.dot(hv, w_ref[0], preferred_element_type=F32).astype(BF16)
        o_ref[:, nb:2 * nb] = jnp.dot(hv, w_ref[1], preferred_element_type=F32).astype(BF16)

    blocks = tm * d * 2 + 2 * d * nb * 2 + tm * 2 * nb * 4
    return pl.pallas_call(
        body, name="mix_proj", grid=(4, t // tm),
        in_specs=[pl.BlockSpec((tm, d), lambda j, i: (i, 0)),
                  pl.BlockSpec((None, 2, d, nb), lambda j, i: (j, 0, 0, 0))],
        out_specs=pl.BlockSpec((tm, 2 * nb), lambda j, i: (i, j)),
        out_shape=_sds((t, N_DEV * nb), BF16),
        compiler_params=_params(("parallel", "parallel"), blocks, 2 * tm * nb * 4),
    )(h, w_in)


def _shift_rows(u, k):
    t = u.shape[0]
    rolled = pltpu.roll(u, k % t, axis=0)
    row = lax.broadcasted_iota(jnp.int32, u.shape, 0)
    keep = (row >= k) if k > 0 else (row < t + k)
    return jnp.where(keep, rolled, 0.0)


def _conv_fwd(proj, conv_w):
    t = proj.shape[0]
    cw = conv_w.shape[1]
    tc = min(cw, 256)
    nc = cw // tc

    def epilogue(_, ins, outs):
        u = ins[2][...].astype(F32) * ins[0][...].astype(F32)
        w = ins[3][...]
        y = u * w[2:3, :] + _shift_rows(u, 1) * w[1:2, :] + _shift_rows(u, 2) * w[0:1, :]
        outs[0][...] = (ins[1][...].astype(F32) * y).astype(BF16)

    def col(seg):
        return pl.BlockSpec((t, tc), lambda i, j, k: (0, seg * nc + i))

    return _fused("conv_fwd", (nc, 1, 1),
                  [(proj, col(0)), (proj, col(1)), (proj, col(2)),
                   (conv_w, pl.BlockSpec((8, tc), lambda i, j, k: (0, i)))],
                  [(_sds((t, cw), BF16), pl.BlockSpec((t, tc), lambda i, j, k: (0, i)))],
                  [], epilogue, temp_bytes=6 * t * tc * 4)[0]


def _conv_bwd(proj, conv_w, dca, deps=()):
    t = proj.shape[0]
    cw = conv_w.shape[1]
    tc = min(cw, 256)
    nc = cw // tc

    def epilogue(_, ins, outs):
        xc, bg, cg = ins[0][...].astype(F32), ins[1][...].astype(F32), ins[2][...].astype(F32)
        w, dc = ins[3][...], ins[4][...]
        u = cg * xc
        u1, u2 = _shift_rows(u, 1), _shift_rows(u, 2)
        y = u * w[2:3, :] + u1 * w[1:2, :] + u2 * w[0:1, :]
        dconv = dc * bg
        du = dconv * w[2:3, :] + _shift_rows(dconv, -1) * w[1:2, :] + _shift_rows(dconv, -2) * w[0:1, :]
        outs[0][0] = (du * cg).astype(BF16)
        outs[0][1] = (dc * y).astype(BF16)
        outs[0][2] = (du * xc).astype(BF16)
        outs[1][...] = jnp.zeros_like(outs[1])
        outs[1][0:1, :] = jnp.sum(dconv * u2, axis=0, keepdims=True)
        outs[1][1:2, :] = jnp.sum(dconv * u1, axis=0, keepdims=True)
        outs[1][2:3, :] = jnp.sum(dconv * u, axis=0, keepdims=True)

    def col(seg):
        return pl.BlockSpec((t, tc), lambda i, j, k: (0, seg * nc + i))

    own = pl.BlockSpec((t, tc), lambda i, j, k: (0, i))
    wspec = pl.BlockSpec((8, tc), lambda i, j, k: (0, i))
    return _fused("conv_bwd", (nc, 1, 1),
                  [(proj, col(0)), (proj, col(1)), (proj, col(2)), (conv_w, wspec), (dca, own)],
                  [(_sds((3, t, cw), BF16), pl.BlockSpec((3, t, tc), lambda i, j, k: (0, 0, i))),
                   (_sds((8, cw), F32), wspec)],
                  [], epilogue, temp_bytes=10 * t * tc * 4, deps=deps)


def _split3(x):
    hi = x.astype(BF16)
    r1 = x - hi.astype(F32)
    mid = r1.astype(BF16)
    lo = (r1 - mid.astype(F32)).astype(BF16)
    return hi, mid, lo


def _head_selector(width):
    r = lax.broadcasted_iota(jnp.int32, (width, LANES), 0)
    c = lax.broadcasted_iota(jnp.int32, (width, LANES), 1)
    return (lax.shift_right_logical(r, 6) == c).astype(BF16)


def _head_sum(x, sel):
    return sum(jnp.dot(p, sel, preferred_element_type=F32) for p in _split3(x))


def _head_bcast(r, sel):
    return sum(lax.dot_general(p, sel, NT, preferred_element_type=F32) for p in _split3(r))


def _rope(x, c, sa, sb):
    n = x.shape[1]
    return x * c + pltpu.roll(x, n - ROT_DIM // 2, axis=1) * sa + pltpu.roll(x, ROT_DIM // 2, axis=1) * sb


def _rope_t(d, c, sa, sb):
    n = d.shape[1]
    return d * c + pltpu.roll(d * sa, ROT_DIM // 2, axis=1) + pltpu.roll(d * sb, n - ROT_DIM // 2, axis=1)


def _tile_lanes(tab, width):
    return tab if width == tab.shape[1] else jnp.tile(tab, (1, width // tab.shape[1]))


def _qk_prep(proj, gq, gk, rope_tabs, cw, kw):
    t = proj.shape[0]
    tm = _row_tile(t)

    def epilogue(_, ins, outs):
        c, sa, sb = ins[5][...], ins[6][...], ins[7][...]
        for src, gain, dst, width in ((0, 3, 0, cw), (1, 4, 1, kw)):
            xv = ins[src][...].astype(F32)
            sel = _head_selector(width)
            r = lax.rsqrt(_head_sum(xv * xv, sel) * (1.0 / HEAD_DIM) + RMS_EPS)
            xn = xv * _head_bcast(r, sel) * ins[gain][...]
            outs[dst][...] = _rope(xn, _tile_lanes(c, width), _tile_lanes(sa, width), _tile_lanes(sb, width)).astype(BF16)
        outs[2][...] = ins[2][...].astype(BF16)

    kblk = cw // kw
    tab = pl.BlockSpec((tm, LANES), lambda i, j, k: (i, 0))
    kspec = pl.BlockSpec((tm, kw), lambda i, j, k: (i, 0))
    return _fused("qk_prep", (t // tm, 1, 1),
                  [(proj, pl.BlockSpec((tm, cw), lambda i, j, k: (i, 3))),
                   (proj, pl.BlockSpec((tm, kw), lambda i, j, k: (i, 4 * kblk))),
                   (proj, pl.BlockSpec((tm, kw), lambda i, j, k: (i, 4 * kblk + 1))),
                   (gq, pl.BlockSpec((1, cw), lambda i, j, k: (0, 0))),
                   (gk, pl.BlockSpec((1, kw), lambda i, j, k: (0, 0))),
                   (rope_tabs[0], tab), (rope_tabs[1], tab), (rope_tabs[2], tab)],
                  [(_sds((t, cw), BF16), pl.BlockSpec((tm, cw), lambda i, j, k: (i, 0))),
                   (_sds((t, kw), BF16), kspec), (_sds((t, kw), BF16), kspec)],
                  [], epilogue, temp_bytes=12 * tm * cw * 4)


def _qk_prep_bwd(proj, gq, gk, rope_tabs, dq, dkc, dkp, dvc, dvp, cw, kw):
    t = proj.shape[0]
    tm = BLOCK
    nblk = t // tm

    def epilogue(_, ins, outs):
        c, sa, sb = ins[5][...], ins[6][...], ins[7][...]
        has_next = (pl.program_id(0) < nblk - 1).astype(F32)
        dk = ins[9][...] + has_next * ins[10][...]
        dv = ins[11][...] + has_next * ins[12][...]
        pieces = []
        for src, gain, dval, dst, width in ((0, 3, ins[8][...], 1, cw), (1, 4, dk, 2, kw)):
            xv, gv = ins[src][...].astype(F32), ins[gain][...]
            sel = _head_selector(width)
            r = _head_bcast(lax.rsqrt(_head_sum(xv * xv, sel) * (1.0 / HEAD_DIM) + RMS_EPS), sel)
            xh = xv * r
            dxn = _rope_t(dval, _tile_lanes(c, width), _tile_lanes(sa, width), _tile_lanes(sb, width))
            u = dxn * gv
            dot = _head_bcast(_head_sum(u * xh, sel), sel) * (1.0 / HEAD_DIM)
            pieces.append((r * (u - xh * dot)).astype(BF16))
            ri = lax.broadcasted_iota(jnp.int32, (width, LANES), 0)
            ci = lax.broadcasted_iota(jnp.int32, (width, LANES), 1)
            fold = (lax.bitwise_and(ri, HEAD_DIM - 1) == ci).astype(BF16)
            colsum = jnp.broadcast_to(jnp.sum(dxn * xh, axis=0, keepdims=True), (8, width))
            part = sum(jnp.dot(p, fold, preferred_element_type=F32) for p in _split3(colsum))

            @pl.when(pl.program_id(0) == 0)
            def _():
                outs[dst][...] = jnp.zeros_like(outs[dst])

            outs[dst][0:1, :] += part[0:1, :]
        outs[0][:, 0:cw] = pieces[0]
        outs[0][:, cw:cw + kw] = pieces[1]
        outs[0][:, cw + kw:cw + 2 * kw] = dv.astype(BF16)

    kblk = cw // kw
    tab = pl.BlockSpec((tm, LANES), lambda i, j, k: (i, 0))
    kcur = pl.BlockSpec((tm, kw), lambda i, j, k: (i, 0))
    knext = pl.BlockSpec((tm, kw), lambda i, j, k: (jnp.minimum(i + 1, nblk - 1), 0))
    acc = pl.BlockSpec((8, LANES), lambda i, j, k: (0, 0))
    return _fused("qk_prep_bwd", (nblk, 1, 1),
                  [(proj, pl.BlockSpec((tm, cw), lambda i, j, k: (i, 3))),
                   (proj, pl.BlockSpec((tm, kw), lambda i, j, k: (i, 4 * kblk))),
                   (proj, pl.BlockSpec((tm, kw), lambda i, j, k: (i, 4 * kblk + 1))),
                   (gq, pl.BlockSpec((1, cw), lambda i, j, k: (0, 0))),
                   (gk, pl.BlockSpec((1, kw), lambda i, j, k: (0, 0))),
                   (rope_tabs[0], tab), (rope_tabs[1], tab), (rope_tabs[2], tab),
                   (dq, pl.BlockSpec((tm, cw), lambda i, j, k: (i, 0))),
                   (dkc, kcur), (dkp, knext), (dvc, kcur), (dvp, knext)],
                  [(_sds((t, cw + 2 * kw), BF16), pl.BlockSpec((tm, cw + 2 * kw), lambda i, j, k: (i, 0))),
                   (_sds((8, LANES), F32), acc), (_sds((8, LANES), F32), acc)],
                  [], epilogue, temp_bytes=16 * tm * cw * 4, semantics=("arbitrary", "arbitrary", "arbitrary"))


def _attn_mask(n):
    key = lax.broadcasted_iota(jnp.int32, (2 * BLOCK, GROUP * BLOCK), 0)
    qry = lax.bitwise_and(lax.broadcasted_iota(jnp.int32, (2 * BLOCK, GROUP * BLOCK), 1), BLOCK - 1)
    return (key > qry) & (key <= qry + BLOCK) & ((key >= BLOCK) | (n > 0))


def _stack_heads(x, h):
    return jnp.concatenate([x[:, (h * GROUP + g) * HEAD_DIM:(h * GROUP + g + 1) * HEAD_DIM] for g in range(GROUP)], axis=0)


def _softmax_with_sink(q4, k2, sink_ref, h, valid):
    sink = jnp.concatenate([sink_ref[h * GROUP + g:h * GROUP + g + 1, :] for g in range(GROUP)], axis=1)
    s = lax.dot_general(k2, q4, NT, preferred_element_type=F32) * ATTN_SCALE
    s = jnp.where(valid, s, NEG_INF)
    m = jnp.maximum(jnp.max(s, axis=0, keepdims=True), sink)
    p = jnp.exp(s - m)
    es = jnp.exp(sink - m)
    inv = 1.0 / (jnp.sum(p, axis=0, keepdims=True) + es)
    return p * inv, es * inv


def _attn_fwd(qn, kn, vb, sink_rows):
    t, cw = qn.shape
    kw = kn.shape[1]
    nkv = kw // HEAD_DIM

    def body(q_ref, kp_ref, kc_ref, vp_ref, vc_ref, sink_ref, o_ref):
        valid = _attn_mask(pl.program_id(0))
        qv = q_ref[...]
        kp, kc, vp, vc = kp_ref[...], kc_ref[...], vp_ref[...], vc_ref[...]
        outs = []
        for h in range(nkv):
            hs = slice(h * HEAD_DIM, (h + 1) * HEAD_DIM)
            k2 = jnp.concatenate([kp[:, hs], kc[:, hs]], axis=0)
            v2 = jnp.concatenate([vp[:, hs], vc[:, hs]], axis=0)
            pn, _ = _softmax_with_sink(_stack_heads(qv, h), k2, sink_ref, h, valid)
            o4 = lax.dot_general(pn.astype(BF16), v2, TN, preferred_element_type=F32)
            outs += [o4[g * BLOCK:(g + 1) * BLOCK] for g in range(GROUP)]
        o_ref[...] = jnp.concatenate(outs, axis=-1).astype(BF16)

    cur = lambda n: (n, 0)
    prev = lambda n: (jnp.maximum(n - 1, 0), 0)
    return pl.pallas_call(
        body, name="attn_fwd", grid=(t // BLOCK,),
        in_specs=[pl.BlockSpec((BLOCK, cw), cur),
                  pl.BlockSpec((BLOCK, kw), prev), pl.BlockSpec((BLOCK, kw), cur),
                  pl.BlockSpec((BLOCK, kw), prev), pl.BlockSpec((BLOCK, kw), cur),
                  pl.BlockSpec(sink_rows.shape, lambda n: (0, 0))],
        out_specs=pl.BlockSpec((BLOCK, cw), cur),
        out_shape=_sds((t, cw), BF16),
        compiler_params=_params(("parallel",), BLOCK * (cw + 4 * kw) * 2 + BLOCK * cw * 2, 8 << 20),
    )(qn, kn, kn, vb, vb, sink_rows)


def _attn_bwd(qn, kn, vb, sink_rows, do):
    t, cw = qn.shape
    kw = kn.shape[1]
    nkv = kw // HEAD_DIM
    nq = nkv * GROUP

    def body(q_ref, kp_ref, kc_ref, vp_ref, vc_ref, sink_ref, do_ref,
             dq_ref, dkc_ref, dkp_ref, dvc_ref, dvp_ref, dsink_ref):
        n = pl.program_id(0)
        valid = _attn_mask(n)
        qv, dov = q_ref[...], do_ref[...]
        kp, kc, vp, vc = kp_ref[...], kc_ref[...], vp_ref[...], vc_ref[...]
        dqs, dks, dvs, dsinks = [], [], [], []
        for h in range(nkv):
            hs = slice(h * HEAD_DIM, (h + 1) * HEAD_DIM)
            k2 = jnp.concatenate([kp[:, hs], kc[:, hs]], axis=0)
            v2 = jnp.concatenate([vp[:, hs], vc[:, hs]], axis=0)
            q4 = _stack_heads(qv, h)
            dob = _stack_heads(dov, h).astype(BF16)
            pn, psink = _softmax_with_sink(q4, k2, sink_ref, h, valid)
            dpn = lax.dot_general(v2, dob, NT, preferred_element_type=F32)
            dvs.append(jnp.dot(pn.astype(BF16), dob, preferred_element_type=F32))
            delta = jnp.sum(pn * dpn, axis=0, keepdims=True)
            ds = (pn * (dpn - delta) * ATTN_SCALE).astype(BF16)
            dks.append(jnp.dot(ds, q4, preferred_element_type=F32))
            dq4 = lax.dot_general(ds, k2, TN, preferred_element_type=F32)
            dsink4 = -psink * delta
            for g in range(GROUP):
                dqs.append(dq4[g * BLOCK:(g + 1) * BLOCK])
                dsinks.append(jnp.broadcast_to(jnp.sum(dsink4[:, g * BLOCK:(g + 1) * BLOCK], axis=1, keepdims=True), (1, LANES)))
        dq_ref[...] = jnp.concatenate(dqs, axis=-1)
        dkp_ref[...] = jnp.concatenate([d[:BLOCK] for d in dks], axis=-1)
        dkc_ref[...] = jnp.concatenate([d[BLOCK:] for d in dks], axis=-1)
        dvp_ref[...] = jnp.concatenate([d[:BLOCK] for d in dvs], axis=-1)
        dvc_ref[...] = jnp.concatenate([d[BLOCK:] for d in dvs], axis=-1)

        @pl.when(n == 0)
        def _():
            dsink_ref[...] = jnp.zeros_like(dsink_ref)

        dsink_ref[...] += jnp.concatenate(dsinks, axis=0)

    cur = lambda n: (n, 0)
    prev = lambda n: (jnp.maximum(n - 1, 0), 0)
    kspec = pl.BlockSpec((BLOCK, kw), cur)
    return pl.pallas_call(
        body, name="attn_bwd", grid=(t // BLOCK,),
        in_specs=[pl.BlockSpec((BLOCK, cw), cur),
                  pl.BlockSpec((BLOCK, kw), prev), kspec,
                  pl.BlockSpec((BLOCK, kw), prev), kspec,
                  pl.BlockSpec(sink_rows.shape, lambda n: (0, 0)),
                  pl.BlockSpec((BLOCK, cw), cur)],
        out_specs=[pl.BlockSpec((BLOCK, cw), cur), kspec, kspec, kspec, kspec,
                   pl.BlockSpec((nq, LANES), lambda n: (0, 0))],
        out_shape=[_sds((t, cw), F32)] + [_sds((t, kw), F32)] * 4 + [_sds((nq, LANES), F32)],
        compiler_params=_params(("arbitrary",), BLOCK * (cw + 4 * kw) * 2 + 2 * BLOCK * cw * 4 + 4 * BLOCK * kw * 4, 12 << 20),
    )(qn, kn, kn, vb, vb, sink_rows, do)


def _mix_out(ca, o, woc, woa, proj):
    t, cw = ca.shape
    nb = woc.shape[2]
    d = N_DEV * nb
    tm = min(t, 1024)
    ga0 = (3 * cw + cw + 2 * (cw // 4)) // nb

    def body(ca_ref, o_ref, woc_ref, woa_ref, ga_ref, gb_ref, m_ref, ya_ref, yb_ref):
        ya = jnp.dot(ca_ref[...], woc_ref[...], preferred_element_type=F32)
        yb = jnp.dot(o_ref[...], woa_ref[...], preferred_element_type=F32)
        ya_ref[...] = ya.astype(BF16)
        yb_ref[...] = yb.astype(BF16)
        m_ref[...] = (_sigmoid(ga_ref[...].astype(F32)) * ya + _sigmoid(gb_ref[...].astype(F32)) * yb).astype(BF16)

    act = pl.BlockSpec((tm, cw), lambda i, j: (i, 0))
    wsp = pl.BlockSpec((None, cw, nb), lambda i, j: (j, 0, 0))
    osp = pl.BlockSpec((tm, nb), lambda i, j: (i, j))
    blocks = 2 * tm * cw * 2 + 2 * cw * nb * 2 + 2 * tm * nb * 4 + 3 * tm * nb * 2
    return pl.pallas_call(
        body, name="mix_out", grid=(t // tm, N_DEV),
        in_specs=[act, act, wsp, wsp,
                  pl.BlockSpec((tm, nb), lambda i, j: (i, ga0 + j)),
                  pl.BlockSpec((tm, nb), lambda i, j: (i, ga0 + N_DEV + j))],
        out_specs=[osp, osp, osp],
        out_shape=[_sds((t, d), BF16)] * 3,
        compiler_params=_params(("parallel", "parallel"), blocks, 6 * tm * nb * 4),
    )(ca, o, woc, woa, proj, proj)


def _mix_residual(merged, wo, x):
    t, d = x.shape
    tm = min(t, 512)

    def epilogue(acc, ins, outs):
        outs[0][...] = ins[2][...] + acc

    row = pl.BlockSpec((tm, d), lambda i, j, k: (i, 0))
    return _fused("mix_residual", (t // tm, 1, 1),
                  [(merged, row), (wo, pl.BlockSpec((d, d), lambda i, j, k: (0, 0))), (x, row)],
                  [(_sds((t, d), F32), row)], [(0, 1, NN)], epilogue, temp_bytes=2 * tm * d * 4)[0]


def _mix_bwd_gates(dx, wo, ya, yb, proj, cw):
    t, d = dx.shape
    tm = min(t, 1024)
    tn = min(d, 512)
    ga0 = (4 * cw + 2 * (cw // 4)) // tn

    def epilogue(acc, ins, outs):
        sa, sb = _sigmoid(ins[4][...].astype(F32)), _sigmoid(ins[5][...].astype(F32))
        outs[0][...] = (acc * sa).astype(BF16)
        outs[1][...] = (acc * sb).astype(BF16)
        outs[2][0] = (acc * ins[2][...].astype(F32) * sa * (1.0 - sa)).astype(BF16)
        outs[2][1] = (acc * ins[3][...].astype(F32) * sb * (1.0 - sb)).astype(BF16)

    blk = pl.BlockSpec((tm, tn), lambda i, j, k: (i, j))
    return _fused("mix_bwd_gates", (t // tm, d // tn, 1),
                  [(dx, pl.BlockSpec((tm, d), lambda i, j, k: (i, 0))),
                   (wo, pl.BlockSpec((tn, d), lambda i, j, k: (j, 0))),
                   (ya, blk), (yb, blk),
                   (proj, pl.BlockSpec((tm, tn), lambda i, j, k: (i, ga0 + j))),
                   (proj, pl.BlockSpec((tm, tn), lambda i, j, k: (i, ga0 + d // tn + j)))],
                  [(_sds((t, d), BF16), blk), (_sds((t, d), BF16), blk),
                   (_sds((2, t, d), BF16), pl.BlockSpec((2, tm, tn), lambda i, j, k: (0, i, j)))],
                  [(0, 1, NT)], epilogue, temp_bytes=8 * tm * tn * 4)


def _tn_matmul(name, a, b, tm, out_dtype=BF16):
    t, m = a.shape
    n = b.shape[1]
    tk = min(t, 512)

    def epilogue(acc, ins, outs):
        outs[0][...] = acc.astype(out_dtype)

    return _fused(name, (m // tm, 1, t // tk),
                  [(a, pl.BlockSpec((tk, tm), lambda i, j, k: (k, i))),
                   (b, pl.BlockSpec((tk, n), lambda i, j, k: (k, 0)))],
                  [(_sds((m, n), out_dtype), pl.BlockSpec((tm, n), lambda i, j, k: (i, 0)))],
                  [(0, 1, TN)], epilogue, nk=t // tk, acc_shape=(tm, n), temp_bytes=tm * n * 4)[0]


def _out_proj_bwd_act(dya, dyb, woc, woa, deps=()):
    t, d = dya.shape
    kdim, nb = woc.shape[1], woc.shape[2]
    tm = min(t, 512)

    def body(dya_ref, dyb_ref, woc_ref, woa_ref, *rest):
        for dy_ref, w_ref, o_ref in ((dya_ref, woc_ref, rest[-2]), (dyb_ref, woa_ref, rest[-1])):
            total = None
            for j in range(N_DEV):
                part = lax.dot_general(dy_ref[:, j * nb:(j + 1) * nb], w_ref[j], NT, preferred_element_type=F32)
                total = part if total is None else total + part
            o_ref[...] = total

    row = pl.BlockSpec((tm, d), lambda i: (i, 0))
    wsp = pl.BlockSpec((N_DEV, kdim, nb), lambda i: (0, 0, 0))
    osp = pl.BlockSpec((tm, kdim), lambda i: (i, 0))
    blocks = 2 * tm * d * 2 + 2 * N_DEV * kdim * nb * 2 + 2 * tm * kdim * 4
    return pl.pallas_call(
        body, name="mix_bwd_dca_do", grid=(t // tm,),
        in_specs=[row, row, wsp, wsp] + [_ANY] * len(deps), out_specs=[osp, osp],
        out_shape=[_sds((t, kdim), F32)] * 2,
        compiler_params=_params(("parallel",), blocks, 4 * tm * kdim * 4),
    )(dya, dyb, woc, woa, *deps)


def _out_proj_bwd_w(ca, o, dya, dyb, nb):
    t, kdim = ca.shape

    def body(ca_ref, o_ref, dya_ref, dyb_ref, dwoc_ref, dwoa_ref):
        dwoc_ref[...] = lax.dot_general(ca_ref[...], dya_ref[...], TN, preferred_element_type=F32).astype(BF16)
        dwoa_ref[...] = lax.dot_general(o_ref[...], dyb_ref[...], TN, preferred_element_type=F32).astype(BF16)

    act = pl.BlockSpec((t, kdim), lambda j: (0, 0))
    col = pl.BlockSpec((t, nb), lambda j: (0, j))
    osp = pl.BlockSpec((None, kdim, nb), lambda j: (j, 0, 0))
    blocks = 2 * t * kdim * 2 + 2 * t * nb * 2 + 2 * kdim * nb * 2
    return pl.pallas_call(
        body, name="mix_bwd_dwoc_dwoa", grid=(N_DEV,),
        in_specs=[act, act, col, col], out_specs=[osp, osp],
        out_shape=[_sds((N_DEV, kdim, nb), BF16)] * 2,
        compiler_params=_params(("parallel",), blocks, 4 * kdim * nb * 4),
    )(ca, o, dya, dyb)


def _proj_bwd_act(dproj, w_in, deps=()):
    t, n = dproj.shape
    d, nb = w_in.shape[2], w_in.shape[3]
    tm = min(t, 512)

    def epilogue(acc, ins, outs):
        outs[0][...] = acc

    def products(ins):
        return (lax.dot_general(ins[0][:, 0:nb], ins[1][0], NT, preferred_element_type=F32)
                + lax.dot_general(ins[0][:, nb:2 * nb], ins[1][1], NT, preferred_element_type=F32))

    return _fused("mix_bwd_dh", (t // tm, 1, 4),
                  [(dproj, pl.BlockSpec((tm, 2 * nb), lambda i, j, k: (i, k))),
                   (w_in, pl.BlockSpec((None, 2, d, nb), lambda i, j, k: (k, 0, 0, 0)))],
                  [(_sds((t, d), F32), pl.BlockSpec((tm, d), lambda i, j, k: (i, 0)))],
                  products, epilogue, nk=4, acc_shape=(tm, d), temp_bytes=tm * d * 4, deps=deps)[0]


def _proj_bwd_w(h, dproj):
    t, d = h.shape
    nb = dproj.shape[1] // N_DEV
    tm = min(d, 512)

    def body(h_ref, dp_ref, o_ref):
        hv = h_ref[...]
        o_ref[0] = lax.dot_general(hv, dp_ref[:, 0:nb], TN, preferred_element_type=F32).astype(BF16)
        o_ref[1] = lax.dot_general(hv, dp_ref[:, nb:2 * nb], TN, preferred_element_type=F32).astype(BF16)

    blocks = t * tm * 2 + t * 2 * nb * 2 + 2 * tm * nb * 2
    return pl.pallas_call(
        body, name="mix_bwd_dwin", grid=(4, d // tm),
        in_specs=[pl.BlockSpec((t, tm), lambda j, i: (0, i)),
                  pl.BlockSpec((t, 2 * nb), lambda j, i: (0, j))],
        out_specs=pl.BlockSpec((None, 2, tm, nb), lambda j, i: (j, 0, i, 0)),
        out_shape=_sds((4, 2, d, nb), BF16),
        compiler_params=_params(("parallel", "parallel"), blocks, 4 * tm * nb * 4),
    )(h, dproj)


def _adamw_math(w, g, m, v):
    m = ADAM_B1 * m + (1.0 - ADAM_B1) * g
    v = ADAM_B2 * v + (1.0 - ADAM_B2) * (g * g)
    m_hat = m / (1.0 - ADAM_B1 ** ADAM_STEP)
    v_hat = v / (1.0 - ADAM_B2 ** ADAM_STEP)
    delta = -ADAM_LR * (m_hat / (jnp.sqrt(v_hat) + ADAM_EPS) + ADAM_WD * w)
    return delta, m, v


def _adamw(name, parts, w, m, v, tr):
    r, c = w.shape

    def body(p_ref, w_ref, m_ref, v_ref, g_out, d_out, m_out, v_out):
        g = p_ref[0].astype(F32)
        for s in range(1, N_DEV):
            g = g + p_ref[s].astype(F32)
        delta, mn, vn = _adamw_math(w_ref[...], g, m_ref[...], v_ref[...])
        g_out[...] = g
        d_out[...] = delta
        m_out[...] = mn
        v_out[...] = vn

    blk = pl.BlockSpec((tr, c), lambda i: (i, 0))
    blocks = N_DEV * tr * c * parts.dtype.itemsize + 7 * tr * c * 4
    return pl.pallas_call(
        body, name=name, grid=(r // tr,),
        in_specs=[pl.BlockSpec((N_DEV, tr, c), lambda i: (0, i, 0)), blk, blk, blk],
        out_specs=[blk] * 4, out_shape=[_sds((r, c), F32)] * 4,
        compiler_params=_params(("parallel",), blocks, 6 * tr * c * 4),
    )(parts, w, m, v)


def _chip_sum(sums_ref):
    g = sums_ref[0].astype(F32)
    for k in range(1, 4):
        g = g + sums_ref[k].astype(F32)
    return g


def _adamw_chips(name, sums, w, m, v, tr, deps=()):
    r, c = w.shape

    def body(sums_ref, w_ref, m_ref, v_ref, *rest):
        g_out, d_out, m_out, v_out = rest[len(deps):]
        g = _chip_sum(sums_ref)
        delta, mn, vn = _adamw_math(w_ref[...], g, m_ref[...], v_ref[...])
        g_out[...] = g
        d_out[...] = delta
        m_out[...] = mn
        v_out[...] = vn

    blk = pl.BlockSpec((tr, c), lambda i: (i, 0))
    blocks = 4 * tr * c * 2 + 7 * tr * c * 4
    return pl.pallas_call(
        body, name=name, grid=(r // tr,),
        in_specs=[pl.BlockSpec((4, tr, c), lambda i: (0, i, 0)), blk, blk, blk] + [_ANY] * len(deps),
        out_specs=[blk] * 4, out_shape=[_sds((r, c), F32)] * 4,
        compiler_params=_params(("parallel",), blocks, 6 * tr * c * 4),
    )(sums, w, m, v, *deps)


def _adamw_side(contrib, w, m, v, n_tiles, step_of):
    r, c = w.shape
    tr = r // n_tiles
    assert tr * n_tiles == r and tr % 16 == 0, (r, n_tiles)

    def tile(i, j, k):
        return jnp.minimum(step_of(i, j, k), n_tiles - 1)

    blk = pl.BlockSpec((tr, c), lambda i, j, k: (tile(i, j, k), 0))
    ins = [(contrib, pl.BlockSpec((4, tr, c), lambda i, j, k: (0, tile(i, j, k), 0))), (w, blk), (m, blk), (v, blk)]
    outs = [(_sds((r, c), F32), blk)] * 4

    def fn(in_refs, out_refs):
        @pl.when(step_of(pl.program_id(0), pl.program_id(1), pl.program_id(2)) < n_tiles)
        def _():
            g = _chip_sum(in_refs[0])
            delta, mn, vn = _adamw_math(in_refs[1][...], g, in_refs[2][...], in_refs[3][...])
            out_refs[0][...] = g
            out_refs[1][...] = delta
            out_refs[2][...] = mn
            out_refs[3][...] = vn

    return ins, outs, fn


def _rope_tables(t):
    half = ROT_DIM // 2
    inv_freq = 1.0 / (ROPE_THETA ** (jnp.arange(0, ROT_DIM, 2, dtype=F32) / ROT_DIM))
    ang = jnp.arange(t, dtype=F32)[:, None] * inv_freq[None, :]
    cos, sin = jnp.cos(ang), jnp.sin(ang)
    ones = jnp.ones((t, HEAD_DIM - ROT_DIM), F32)
    zeros = jnp.zeros((t, HEAD_DIM - half), F32)
    c = jnp.concatenate([cos, cos, ones], axis=1)
    sa = jnp.concatenate([-sin, zeros], axis=1)
    sb = jnp.concatenate([jnp.zeros((t, half), F32), sin, jnp.zeros((t, HEAD_DIM - ROT_DIM), F32)], axis=1)
    return tuple(jnp.tile(a, (1, LANES // HEAD_DIM)) for a in (c, sa, sb))


def _pad_rows(a, rows=8):
    return jnp.pad(a, ((0, rows - a.shape[0]), (0, 0)))


def kernel(x, g_ffn1, w_gu1, w_down1, g_mix, w_in, conv_w, q_norm_g, k_norm_g, sinks, w_out_conv, w_out_attn, w_o, g_ffn2, w_gu2, w_down2, loss_target, m_g_ffn1, m_w_gu1, m_w_down1, m_g_mix, m_w_in, m_conv_w, m_q_norm_g, m_k_norm_g, m_sinks, m_w_out_conv, m_w_out_attn, m_w_o, m_g_ffn2, m_w_gu2, m_w_down2, v_g_ffn1, v_w_gu1, v_w_down1, v_g_mix, v_w_in, v_conv_w, v_q_norm_g, v_k_norm_g, v_sinks, v_w_out_conv, v_w_out_attn, v_w_o, v_g_ffn2, v_w_gu2, v_w_down2):
    t, d = x.shape[1], x.shape[2]
    cw = d // 2
    kw = cw // GROUP
    nq = cw // HEAD_DIM
    xs, target = x.reshape(t, d), loss_target.reshape(t, d)
    me = 4 * lax.axis_index("x") + 2 * lax.axis_index("y") + lax.axis_index("c")

    big = {"w_gu1": w_gu1, "w_down1": w_down1, "w_in": w_in, "w_out_conv": w_out_conv,
           "w_out_attn": w_out_attn, "w_o": w_o, "w_gu2": w_gu2, "w_down2": w_down2}
    big_m = {"w_gu1": m_w_gu1, "w_down1": m_w_down1, "w_in": m_w_in, "w_out_conv": m_w_out_conv,
             "w_out_attn": m_w_out_attn, "w_o": m_w_o, "w_gu2": m_w_gu2, "w_down2": m_w_down2}
    big_v = {"w_gu1": v_w_gu1, "w_down1": v_w_down1, "w_in": v_w_in, "w_out_conv": v_w_out_conv,
             "w_out_attn": v_w_out_attn, "w_o": v_w_o, "w_gu2": v_w_gu2, "w_down2": v_w_down2}
    names = list(big)

    tiles = {"w_gu1": 256, "w_gu2": 256, "w_in": 256, "w_down1": 176, "w_down2": 176,
             "w_out_conv": 1024, "w_out_attn": 1024, "w_o": 128}

    def row_tile(n):
        r = big[n].shape[1]
        return tiles[n] if r % tiles[n] == 0 else r

    def add_tile(n):
        r, c = big[n].shape[1], big[n].shape[2]
        while r * c * 2 > (3 << 20) and r % 32 == 0:
            r //= 2
        return r

    me_arr = me.astype(jnp.int32).reshape(1)
    sources = [(n, big[n][0], BF16, row_tile(n)) for n in names] + [("conv_w", _pad_rows(conv_w[0]), F32, 8)]
    issue_order = [0, 1, 2, 8, 3, 4, 5, 6, 7]
    first = _place_shard("place_" + names[0], sources[0][1], BF16, me_arr, sources[0][3])
    started = [_gather_start("gather_start_first", [first])]
    early = {2: (big_m["w_in"][0], big_v["w_in"][0])}
    rest = [_place_shard("place_" + sources[i][0], sources[i][1], sources[i][2], me_arr, sources[i][3],
                         deps=(started[0][3],) + early.get(i, ())) for i in issue_order[1:]]
    started.append(_gather_start("gather_start_rest", rest))
    where = {0: (0, 0)}
    where.update({i: (1, p) for p, i in enumerate(issue_order[1:])})

    def fetch(tag, idxs, after):
        call = where[idxs[0]][0]
        send, recv, stacks, _ = started[call]
        positions = [where[i][1] for i in idxs]
        got = _gather_wait("gather_wait_" + tag, positions, send, recv, [stacks[p] for p in positions], after)
        return _forward_to_sibling("gather_forward_" + tag, got)

    rope_tabs = _rope_tables(t)
    gq = jnp.tile(q_norm_g, (1, nq))
    gk = jnp.tile(k_norm_g, (1, nq // GROUP))
    sink_rows = jnp.broadcast_to(sinks[0][:, None], (nq, LANES))

    wts = {}
    h1 = _rms_fwd("ffn1_norm", xs, g_ffn1)
    wts["w_gu1"], = fetch("gu1", [0], started[1][3])
    gu1, a1 = _ffn_up("ffn1_up", h1, wts["w_gu1"])
    wts["w_down1"], = fetch("down1", [1], a1)
    wd1 = wts["w_down1"].reshape(-1, d)
    x1 = _ffn_down("ffn1_down", a1, wd1, xs)
    h2 = _rms_fwd("mix_norm", x1, g_mix)
    wts["w_in"], conv_land = fetch("in", [2, 8], h2)
    w_in_full = wts["w_in"].reshape(4, 2, d, -1)
    conv_full = jnp.transpose(conv_land, (1, 0, 2)).reshape(8, cw)
    proj = _proj(h2, w_in_full)
    ca = _conv_fwd(proj, conv_full)
    qn, kn, vb = _qk_prep(proj, gq, gk, rope_tabs, cw, kw)
    o = _attn_fwd(qn, kn, vb, sink_rows)
    wts["w_out_conv"], wts["w_out_attn"] = fetch("out", [3, 4], o)
    merged, ya, yb = _mix_out(ca, o, wts["w_out_conv"], wts["w_out_attn"], proj)
    wts["w_o"], = fetch("o", [5], merged)
    wo = wts["w_o"].reshape(d, d)
    x2 = _mix_residual(merged, wo, x1)
    h3 = _rms_fwd("ffn2_norm", x2, g_ffn2)
    wts["w_gu2"], = fetch("gu2", [6], h3)
    gu2, a2 = _ffn_up("ffn2_up", h3, wts["w_gu2"])
    wts["w_down2"], = fetch("down2", [7], a2)
    wd2 = wts["w_down2"].reshape(-1, d)
    y = _ffn_down("ffn2_down", a2, wd2, x2)
    dy, sq = _loss_dy(y, target)
    loss = lax.psum(sq[0, 0] * (0.5 / d), ("x", "y", "c"))

    place = jnp.stack([lax.axis_index("c"), 2 * lax.axis_index("x") + lax.axis_index("y")]).astype(jnp.int32)
    def pair_start(tag, group, grads, deps=()):
        stacks = [grads[n].reshape((4, 2) + big[n].shape[1:]) for n in group]
        lands = [lax.empty((4,) + big[n].shape[1:], BF16) for n in group]
        return _pair_start("rs_pair_start_" + tag, stacks, lands, deps)

    def chip_start(tag, group, pending, after):
        send, recv, stacks, lands, _ = pending
        stacks, lands = _pair_wait("rs_pair_wait_" + tag, send, recv, stacks, lands, after)
        added = [_pair_add("rs_pair_add_" + n, st, ld, place, add_tile(n)) for n, st, ld in zip(group, stacks, lands)]
        return _chip_start("rs_chip_start_" + tag, [a[0] for a in added], [a[1] for a in added])

    group_a, group_b, group_c = ["w_down2", "w_gu2"], ["w_o", "w_out_conv", "w_out_attn"], ["w_in"]
    group_d, group_e = ["w_down1"], ["w_gu1"]
    g = {}
    dgu2, a2 = _ffn_bwd_act("ffn2_bwd_act", dy, wd2, gu2)
    g["w_down2"], = _ffn_bwd_dwd("ffn2_bwd_dwd", a2, dy)
    g["w_gu2"], = _ffn_bwd_dwgu("ffn2_bwd_dwgu", h3, dgu2)
    pend_a = pair_start("a", group_a, g)
    dh3, = _ffn_bwd_dh("ffn2_bwd_dh", dgu2, wts["w_gu2"], deps=(pend_a[4],))
    ring_a = chip_start("a", group_a, pend_a, dh3)
    dx2, dg_ffn2 = _rms_bwd("ffn2_bwd_rms", x2, g_ffn2, dh3, dy, deps=(ring_a[4],))

    dya, dyb, dgates = _mix_bwd_gates(dx2, wo, ya, yb, proj, cw)
    g["w_o"] = _tn_matmul("mix_bwd_dwo", merged, dx2, min(d, 1024))
    g["w_out_conv"], g["w_out_attn"] = _out_proj_bwd_w(ca, o, dya, dyb, d // N_DEV)
    pend_b = pair_start("b", group_b, g)
    dca, do = _out_proj_bwd_act(dya, dyb, wts["w_out_conv"], wts["w_out_attn"], deps=(pend_b[4],))
    ring_b = chip_start("b", group_b, pend_b, do)
    d3, dconv_w = _conv_bwd(proj, conv_full, dca, deps=(ring_b[4],))
    dq, dkc, dkp, dvc, dvp, dsink = _attn_bwd(qn, kn, vb, sink_rows, do)
    dqkv, dgq, dgk = _qk_prep_bwd(proj, gq, gk, rope_tabs, dq, dkc, dkp, dvc, dvp, cw, kw)
    dproj = jnp.concatenate([d3[0], d3[1], d3[2], dqkv, dgates[0], dgates[1]], axis=1)
    g["w_in"] = _proj_bwd_w(h2, dproj)
    pend_c = pair_start("c", group_c, g)
    dh2 = _proj_bwd_act(dproj, w_in_full, deps=(pend_c[4],))
    ring_c = chip_start("c", group_c, pend_c, dh2)
    dx1, dg_mix = _rms_bwd("mix_bwd_rms", x1, g_mix, dh2, dx2, deps=(ring_c[4],))

    big_out = {}
    arrived = {}

    def wait_group(tag, group, ring, after):
        send, recv, parts, lands2, _ = ring
        parts, lands2 = _chip_wait("rs_chip_wait_" + tag, send, recv, parts, lands2, after)
        arrived.update(dict(zip(group, lands2)))

    def update(n, after):
        res = _adamw_chips("adamw_" + n, arrived[n], big[n][0], big_m[n][0], big_v[n][0], row_tile(n), deps=(after,))
        big_out[n] = [a[None] for a in res]
        return res[0]

    def update_beside(n, n_tiles, step_of):
        return _adamw_side(arrived[n], big[n][0], big_m[n][0], big_v[n][0], n_tiles, step_of)

    def keep(n, res):
        big_out[n] = [a[None] for a in res]

    dgu1, a1 = _ffn_bwd_act("ffn1_bwd_act", dx1, wd1, gu1)
    wait_group("a", group_a, ring_a, a1)
    g["w_down1"], *res = _ffn_bwd_dwd("ffn1_bwd_dwd", a1, dx1,
                                       side=update_beside("w_down2", 11, lambda i, j, k: i * 4 + j))
    keep("w_down2", res)
    pend_d = pair_start("d", group_d, g)
    g["w_gu1"], *res = _ffn_bwd_dwgu("ffn1_bwd_dwgu", h1, dgu1, deps=(pend_d[4],),
                                      side=update_beside("w_gu2", 32, lambda i, j, k: i * 4 + j))
    keep("w_gu2", res)
    ring_d = chip_start("d", group_d, pend_d, g["w_gu1"])
    pend_e = pair_start("e", group_e, g, deps=(ring_d[4],))
    wait_group("b", group_b, ring_b, pend_e[4])
    after = pend_e[4]
    for n in group_b:
        after = update(n, after)
    ring_e = chip_start("e", group_e, pend_e, after)
    wait_group("c", group_c, ring_c, ring_e[4])
    dh1, *res = _ffn_bwd_dh("ffn1_bwd_dh", dgu1, wts["w_gu1"],
                             side=update_beside("w_in", 16, lambda i, j, k: i * 4 + k))
    keep("w_in", res)
    grad_x, dg_ffn1 = _rms_bwd("ffn1_bwd_rms", xs, g_ffn1, dh1, dx1)
    after = grad_x
    for tag, group, ring in (("d", group_d, ring_d), ("e", group_e, ring_e)):
        wait_group(tag, group, ring, after)
        for n in group:
            after = update(n, after)

    small = {"g_ffn1": dg_ffn1[0:1], "g_mix": dg_mix[0:1], "g_ffn2": dg_ffn2[0:1],
             "q_norm_g": dgq[0:1, :HEAD_DIM], "k_norm_g": dgk[0:1, :HEAD_DIM], "sinks": dsink[:, 0][None],
             "conv_w": dconv_w[0:CONV_K].reshape(1, -1)}
    small_w = {"g_ffn1": g_ffn1, "g_mix": g_mix, "g_ffn2": g_ffn2, "q_norm_g": q_norm_g, "k_norm_g": k_norm_g,
               "sinks": sinks, "conv_w": None}
    small_m = {"g_ffn1": m_g_ffn1, "g_mix": m_g_mix, "g_ffn2": m_g_ffn2, "q_norm_g": m_q_norm_g,
               "k_norm_g": m_k_norm_g, "sinks": m_sinks, "conv_w": m_conv_w}
    small_v = {"g_ffn1": v_g_ffn1, "g_mix": v_g_mix, "g_ffn2": v_g_ffn2, "q_norm_g": v_q_norm_g,
               "k_norm_g": v_k_norm_g, "sinks": v_sinks, "conv_w": v_conv_w}
    snames = list(small)
    widths = [small[n].shape[1] for n in snames]
    total = sum(widths)
    rows = -(-total // LANES)
    rows = -(-rows // 8) * 8

    def pack(vals):
        flat = jnp.concatenate([v.reshape(1, -1) for v in vals], axis=1)
        return jnp.pad(flat, ((0, 0), (0, rows * LANES - total))).reshape(rows, LANES)

    csh = cw // N_DEV

    def place_conv(local, fill):
        full = jnp.full((CONV_K, cw), fill, F32)
        return lax.dynamic_update_slice(full, local, (0, me * csh)).reshape(1, -1)

    pw = pack([small_w[n] if n != "conv_w" else place_conv(conv_w[0], 0.0) for n in snames])
    pm = pack([small_m[n] if n != "conv_w" else place_conv(m_conv_w[0], 0.0) for n in snames])
    pv = pack([small_v[n] if n != "conv_w" else place_conv(v_conv_w[0], 1.0) for n in snames])
    parts = _exchange("gather_small_grads", [pack([small[n] for n in snames])], gather=True, deps=(after,))[0]
    sg, sd, sm, sv = [a.reshape(1, -1) for a in _adamw("adamw_small", parts, pw, pm, pv, rows)]

    def unpack(flat, n):
        off = sum(widths[:snames.index(n)])
        piece = flat[:, off:off + widths[snames.index(n)]]
        if n == "conv_w":
            piece = lax.dynamic_slice(piece.reshape(CONV_K, cw), (0, me * csh), (CONV_K, csh))[None]
        return piece

    order = ["g_ffn1", "w_gu1", "w_down1", "g_mix", "w_in", "conv_w", "q_norm_g", "k_norm_g", "sinks",
             "w_out_conv", "w_out_attn", "w_o", "g_ffn2", "w_gu2", "w_down2"]
    outs = [loss, grad_x[None]]
    for idx, flat in enumerate((sg, sd, sm, sv)):
        for n in order:
            outs.append(big_out[n][idx] if n in big_out else unpack(flat, n))
    return tuple(outs)
```

```python
import functools

import jax
import jax.numpy as jnp
from jax import lax
from jax.experimental import pallas as pl
from jax.experimental.pallas import tpu as pltpu

F32 = jnp.float32
BF16 = jnp.bfloat16

N_DEV = 8
HEAD_DIM = 64
GROUP = 4
BLOCK = 128
ROT_DIM = 16
ROPE_THETA = 500000.0
RMS_EPS = 1e-6
NEG_INF = -1e30
ATTN_SCALE = HEAD_DIM ** -0.5
CONV_K = 3
LANES = 128
MXU_COLS = 256
VMEM_BYTES_V7X = 64 * 1024 * 1024
VMEM_CAP = VMEM_BYTES_V7X - 6 * 1024 * 1024

ADAM_LR = 0.001
ADAM_B1 = 0.9
ADAM_B2 = 0.999
ADAM_EPS = 1e-08
ADAM_WD = 0.01
ADAM_STEP = 10

NN = (((1,), (0,)), ((), ()))
NT = (((1,), (1,)), ((), ()))
TN = (((0,), (0,)), ((), ()))

MESH = pl.DeviceIdType.MESH


def _nbytes(shape, dtype):
    n = 1
    for s in shape:
        if s is not None:
            n *= s
    return n * jnp.dtype(dtype).itemsize


def _params(semantics, block_bytes, temp_bytes):
    assert 2 * block_bytes + temp_bytes <= VMEM_CAP, (block_bytes, temp_bytes)
    return pltpu.CompilerParams(dimension_semantics=semantics, vmem_limit_bytes=VMEM_CAP)


def _fused(name, grid, ins, outs, dots, epilogue, *, nk=1, acc_shape=None, temp_bytes=0,
           semantics=("parallel", "parallel", "arbitrary"), deps=(), side=None):
    n_main_in, n_main_out = len(ins), len(outs)
    if side is not None:
        ins, outs = list(ins) + list(side[0]), list(outs) + list(side[1])
    n_in, n_out = len(ins), len(outs)
    n_dep = len(deps)

    def body(*refs):
        in_refs, out_refs = refs[:n_in], refs[n_in + n_dep:n_in + n_dep + n_out]
        scratch = refs[n_in + n_dep + n_out:]
        if side is not None:
            side[2](in_refs[n_main_in:], out_refs[n_main_out:])

        def products():
            if callable(dots):
                return dots(in_refs)
            total = None
            for ai, bi, contract in dots:
                a, b = in_refs[ai][...], in_refs[bi][...]
                a = a if a.dtype == BF16 else a.astype(BF16)
                b = b if b.dtype == BF16 else b.astype(BF16)
                p = lax.dot_general(a, b, contract, preferred_element_type=F32)
                total = p if total is None else total + p
            return total

        if nk == 1:
            epilogue(products() if dots else None, in_refs, out_refs)
        else:
            acc = scratch[0]
            k = pl.program_id(2)

            @pl.when(k == 0)
            def _():
                acc[...] = jnp.zeros_like(acc)

            acc[...] += products()

            @pl.when(k == nk - 1)
            def _():
                epilogue(acc[...], in_refs, out_refs)

    block_bytes = sum(_nbytes(spec.block_shape, a.dtype) for a, spec in ins)
    block_bytes += sum(_nbytes(spec.block_shape, s.dtype) for s, spec in outs)
    scratch_shapes = []
    if nk > 1:
        scratch_shapes.append(pltpu.VMEM(acc_shape, F32))
        temp_bytes += _nbytes(acc_shape, F32)
    res = pl.pallas_call(
        body, name=name, grid=grid,
        in_specs=[spec for _, spec in ins] + [pl.BlockSpec(memory_space=pl.ANY)] * n_dep,
        out_specs=[spec for _, spec in outs],
        out_shape=[s for s, _ in outs],
        scratch_shapes=scratch_shapes,
        compiler_params=_params(semantics, block_bytes, temp_bytes),
    )(*[a for a, _ in ins], *deps)
    return res


def _sds(shape, dtype):
    return jax.ShapeDtypeStruct(shape, dtype)


def _sigmoid(x):
    return jax.nn.sigmoid(x)


def _exchange(name, arrays, gather, deps=()):
    n = len(arrays)
    out_shapes = [((N_DEV,) + a.shape) if gather else a.shape for a in arrays]

    def body(*refs):
        srcs, dsts = refs[:n], refs[n + len(deps):2 * n + len(deps)]
        send_sems, recv_sems, local_sems = refs[2 * n + len(deps):]
        x, y, c = lax.axis_index("x"), lax.axis_index("y"), lax.axis_index("c")
        me = 4 * x + 2 * y + c
        copies = []
        for w in range(n):
            own = srcs[w] if gather else srcs[w].at[me]
            local = pltpu.make_async_copy(own, dsts[w].at[me], local_sems.at[w])
            local.start()
            copies.append(local)
            for k in range(1, N_DEV):
                px = (1 - x) if (k & 4) else x
                py = (1 - y) if (k & 2) else y
                pc = (1 - c) if (k & 1) else c
                peer = 4 * px + 2 * py + pc
                cp = pltpu.make_async_remote_copy(
                    src_ref=srcs[w] if gather else srcs[w].at[peer],
                    dst_ref=dsts[w].at[me],
                    send_sem=send_sems.at[w * (N_DEV - 1) + k - 1],
                    recv_sem=recv_sems.at[w * (N_DEV - 1) + k - 1],
                    device_id=(px, py, pc), device_id_type=MESH)
                cp.start()
                copies.append(cp)
        for cp in copies:
            cp.wait()

    hbm = pl.BlockSpec(memory_space=pltpu.HBM)
    return pl.pallas_call(
        body, name=name,
        in_specs=[hbm] * n + [pl.BlockSpec(memory_space=pl.ANY)] * len(deps), out_specs=[hbm] * n,
        out_shape=[_sds(s, a.dtype) for s, a in zip(out_shapes, arrays)],
        scratch_shapes=[pltpu.SemaphoreType.DMA((n * (N_DEV - 1),)),
                        pltpu.SemaphoreType.DMA((n * (N_DEV - 1),)),
                        pltpu.SemaphoreType.DMA((n,))],
    )(*arrays, *deps)


_HBM = pl.BlockSpec(memory_space=pltpu.HBM)
_SEM = pl.BlockSpec(memory_space=pltpu.SEMAPHORE)
_ANY = pl.BlockSpec(memory_space=pl.ANY)
_EFFECT = pltpu.SideEffectType.DATAFLOW_SIDE_EFFECTING
N_TARGETS = 4


def _mesh_pos():
    return lax.axis_index("x"), lax.axis_index("y"), lax.axis_index("c")


def _chip_peers(x, y, c):
    return [(1 - x, y, c), (x, 1 - y, c), (1 - x, 1 - y, c)]


def _dev_index(pos):
    return 4 * pos[0] + 2 * pos[1] + pos[2]


def _hbm_like(a):
    return pltpu.HBM(a.shape, a.dtype)


def _place_shard(name, w, out_dtype, me, tr, deps=()):
    r, c = w.shape
    n_dep = len(deps)

    def body(me_ref, w_ref, *rest):
        rest[n_dep][...] = w_ref[...].astype(out_dtype)

    grid_spec = pltpu.PrefetchScalarGridSpec(
        num_scalar_prefetch=1, grid=(r // tr,),
        in_specs=[pl.BlockSpec((tr, c), lambda i, me_ref: (i, 0))] + [_ANY] * n_dep,
        out_specs=pl.BlockSpec((None, tr, c), lambda i, me_ref: (me_ref[0], i, 0)))
    return pl.pallas_call(
        body, name=name, grid_spec=grid_spec, out_shape=_sds((N_DEV, r, c), out_dtype),
        compiler_params=_params(("parallel",), tr * c * 6, tr * c * 4),
    )(me, w, *deps)


def _gather_start(name, lands):
    n = len(lands)

    def body(*refs):
        bufs = refs[:n]
        send, recv = refs[n], refs[n + 1]
        token = refs[-1]
        x, y, c = _mesh_pos()
        me = _dev_index((x, y, c))
        targets = [(x, y, 1 - c)] + _chip_peers(x, y, c)
        for w in range(n):
            for k, to in enumerate(targets):
                pltpu.make_async_remote_copy(
                    src_ref=bufs[w].at[me], dst_ref=bufs[w].at[me],
                    send_sem=send.at[N_TARGETS * w + k], recv_sem=recv.at[N_TARGETS * w + k],
                    device_id=to, device_id_type=MESH).start()
        token[...] = jnp.zeros_like(token)

    sems = pltpu.SemaphoreType.DMA((N_TARGETS * n,))
    outs = pl.pallas_call(
        body, name=name,
        in_specs=[_HBM] * n, out_specs=[_SEM, _SEM] + [_HBM] * n + [_token_spec()],
        out_shape=[sems, sems] + [_hbm_like(a) for a in lands] + [_sds((8, LANES), F32)],
        input_output_aliases={i: 2 + i for i in range(n)},
        compiler_params=pltpu.CompilerParams(has_side_effects=_EFFECT),
    )(*lands)
    return outs[0], outs[1], list(outs[2:2 + n]), outs[-1]


def _gather_wait(name, positions, send, recv, lands, after):
    m = len(positions)

    def body(*refs):
        bufs = refs[:m]
        send_sems, recv_sems = refs[m], refs[m + 1]
        x, y, c = _mesh_pos()
        me = _dev_index((x, y, c))
        sources = [(x, y, 1 - c)] + _chip_peers(x, y, c)
        for j, w in enumerate(positions):
            for k, frm in enumerate(sources):
                cp = pltpu.make_async_remote_copy(
                    src_ref=bufs[j].at[me], dst_ref=bufs[j].at[_dev_index(frm)],
                    send_sem=send_sems.at[N_TARGETS * w + k], recv_sem=recv_sems.at[N_TARGETS * w + k],
                    device_id=frm, device_id_type=MESH)
                cp.wait_send()
                cp.wait_recv()

    outs = pl.pallas_call(
        body, name=name,
        in_specs=[_HBM] * m + [_SEM, _SEM, _ANY], out_specs=[_HBM] * m,
        out_shape=[_hbm_like(a) for a in lands],
        input_output_aliases={i: i for i in range(m)},
        compiler_params=pltpu.CompilerParams(has_side_effects=_EFFECT),
    )(*lands, send, recv, after)
    return list(outs)


def _forward_to_sibling(name, lands):
    m = len(lands)

    def body(*refs):
        bufs = refs[m:2 * m]
        send_sems, recv_sems = refs[2 * m], refs[2 * m + 1]
        x, y, c = _mesh_pos()
        copies = []
        for j in range(m):
            for k, chip in enumerate(_chip_peers(x, y, c)):
                block = bufs[j].at[_dev_index(chip)]
                cp = pltpu.make_async_remote_copy(
                    src_ref=block, dst_ref=block,
                    send_sem=send_sems.at[3 * j + k], recv_sem=recv_sems.at[3 * j + k],
                    device_id=(x, y, 1 - c), device_id_type=MESH)
                cp.start()
                copies.append(cp)
        for cp in copies:
            cp.wait()

    outs = pl.pallas_call(
        body, name=name,
        in_specs=[_HBM] * m, out_specs=[_HBM] * m,
        out_shape=[_sds(a.shape, a.dtype) for a in lands],
        input_output_aliases={i: i for i in range(m)},
        scratch_shapes=[pltpu.SemaphoreType.DMA((3 * m,)), pltpu.SemaphoreType.DMA((3 * m,))],
    )(*lands)
    return list(outs)


def _token_spec():
    return pl.BlockSpec(memory_space=pltpu.VMEM)


def _pair_start(name, stacks, lands, deps=()):
    n = len(stacks)
    n_dep = len(deps)

    def body(*refs):
        srcs, dsts = refs[:n], refs[n:2 * n]
        send, recv = refs[2 * n + n_dep], refs[2 * n + n_dep + 1]
        token = refs[-1]
        x, y, c = _mesh_pos()
        for w in range(n):
            for chip in range(4):
                pltpu.make_async_remote_copy(
                    src_ref=srcs[w].at[chip, 1 - c], dst_ref=dsts[w].at[chip],
                    send_sem=send.at[4 * w + chip], recv_sem=recv.at[4 * w + chip],
                    device_id=(x, y, 1 - c), device_id_type=MESH).start()
        token[...] = jnp.zeros_like(token)

    sems = pltpu.SemaphoreType.DMA((4 * n,))
    outs = pl.pallas_call(
        body, name=name,
        in_specs=[_HBM] * (2 * n) + [_ANY] * n_dep, out_specs=[_SEM, _SEM] + [_HBM] * (2 * n) + [_token_spec()],
        out_shape=[sems, sems] + [_hbm_like(a) for a in stacks] + [_hbm_like(a) for a in lands] + [_sds((8, LANES), F32)],
        input_output_aliases={i: 2 + i for i in range(2 * n)},
        compiler_params=pltpu.CompilerParams(has_side_effects=_EFFECT),
    )(*stacks, *lands, *deps)
    return outs[0], outs[1], list(outs[2:2 + n]), list(outs[2 + n:2 + 2 * n]), outs[-1]


def _pair_wait(name, send, recv, stacks, lands, after):
    n = len(stacks)

    def body(*refs):
        srcs, dsts = refs[:n], refs[n:2 * n]
        send_sems, recv_sems = refs[2 * n], refs[2 * n + 1]
        x, y, c = _mesh_pos()
        for w in range(n):
            for chip in range(4):
                cp = pltpu.make_async_remote_copy(
                    src_ref=srcs[w].at[chip, 1 - c], dst_ref=dsts[w].at[chip],
                    send_sem=send_sems.at[4 * w + chip], recv_sem=recv_sems.at[4 * w + chip],
                    device_id=(x, y, 1 - c), device_id_type=MESH)
                cp.wait_send()
                cp.wait_recv()

    outs = pl.pallas_call(
        body, name=name,
        in_specs=[_HBM] * (2 * n) + [_SEM, _SEM, _ANY], out_specs=[_HBM] * (2 * n),
        out_shape=[_hbm_like(a) for a in stacks] + [_hbm_like(a) for a in lands],
        input_output_aliases={i: i for i in range(2 * n)},
        compiler_params=pltpu.CompilerParams(has_side_effects=_EFFECT),
    )(*stacks, *lands, send, recv, after)
    return list(outs[:n]), list(outs[n:])


def _pair_add(name, stack, land, place, tr):
    _, _, r, c = stack.shape

    def body(place_ref, a_ref, b_ref, sums_ref, slots_ref):
        total = (a_ref[...].astype(F32) + b_ref[...].astype(F32)).astype(BF16)
        sums_ref[...] = total

        @pl.when(pl.program_id(1) == place_ref[1])
        def _():
            slots_ref[...] = total

    grid_spec = pltpu.PrefetchScalarGridSpec(
        num_scalar_prefetch=1, grid=(r // tr, 4),
        in_specs=[pl.BlockSpec((None, None, tr, c), lambda i, k, place_ref: (k, place_ref[0], i, 0)),
                  pl.BlockSpec((None, tr, c), lambda i, k, place_ref: (k, i, 0))],
        out_specs=[pl.BlockSpec((None, tr, c), lambda i, k, place_ref: (k, i, 0)),
                   pl.BlockSpec((None, tr, c), lambda i, k, place_ref: (place_ref[1], i, 0))])
    return pl.pallas_call(
        body, name=name, grid_spec=grid_spec, out_shape=[_sds((4, r, c), BF16)] * 2,
        compiler_params=_params(("parallel", "arbitrary"), 4 * tr * c * 2, 3 * tr * c * 4),
    )(place, stack, land)


def _chip_start(name, parts, lands):
    n = len(parts)

    def body(*refs):
        srcs, dsts = refs[:n], refs[n:2 * n]
        send, recv = refs[2 * n], refs[2 * n + 1]
        token = refs[-1]
        x, y, c = _mesh_pos()
        for w in range(n):
            for k, to in enumerate(_chip_peers(x, y, c)):
                pltpu.make_async_remote_copy(
                    src_ref=srcs[w].at[2 * to[0] + to[1]], dst_ref=dsts[w].at[2 * x + y],
                    send_sem=send.at[3 * w + k], recv_sem=recv.at[3 * w + k],
                    device_id=to, device_id_type=MESH).start()
        token[...] = jnp.zeros_like(token)

    sems = pltpu.SemaphoreType.DMA((3 * n,))
    outs = pl.pallas_call(
        body, name=name,
        in_specs=[_HBM] * (2 * n), out_specs=[_SEM, _SEM] + [_HBM] * (2 * n) + [_token_spec()],
        out_shape=[sems, sems] + [_hbm_like(a) for a in parts] + [_hbm_like(a) for a in lands] + [_sds((8, LANES), F32)],
        input_output_aliases={i: 2 + i for i in range(2 * n)},
        compiler_params=pltpu.CompilerParams(has_side_effects=_EFFECT),
    )(*parts, *lands)
    return outs[0], outs[1], list(outs[2:2 + n]), list(outs[2 + n:2 + 2 * n]), outs[-1]


def _chip_wait(name, send, recv, parts, lands, after):
    n = len(parts)

    def body(*refs):
        srcs, dsts = refs[:n], refs[n:2 * n]
        send_sems, recv_sems = refs[2 * n], refs[2 * n + 1]
        x, y, c = _mesh_pos()
        for w in range(n):
            for k, frm in enumerate(_chip_peers(x, y, c)):
                chip = 2 * frm[0] + frm[1]
                cp = pltpu.make_async_remote_copy(
                    src_ref=srcs[w].at[chip], dst_ref=dsts[w].at[chip],
                    send_sem=send_sems.at[3 * w + k], recv_sem=recv_sems.at[3 * w + k],
                    device_id=frm, device_id_type=MESH)
                cp.wait_send()
                cp.wait_recv()

    outs = pl.pallas_call(
        body, name=name,
        in_specs=[_HBM] * (2 * n) + [_SEM, _SEM, _ANY], out_specs=[_HBM] * (2 * n),
        out_shape=[_hbm_like(a) for a in parts] + [_hbm_like(a) for a in lands],
        input_output_aliases={i: i for i in range(2 * n)},
        compiler_params=pltpu.CompilerParams(has_side_effects=_EFFECT),
    )(*parts, *lands, send, recv, after)
    return list(outs[:n]), list(outs[n:])


def _row_tile(t):
    return min(t, 256)


def _rms_fwd(name, x, g):
    t, d = x.shape
    tm = _row_tile(t)

    def epilogue(_, ins, outs):
        xv = ins[0][...]
        r = lax.rsqrt(jnp.mean(xv * xv, axis=-1, keepdims=True) + RMS_EPS)
        outs[0][...] = (xv * r * ins[1][...]).astype(BF16)

    row = pl.BlockSpec((tm, d), lambda i, j, k: (i, 0))
    vec = pl.BlockSpec((1, d), lambda i, j, k: (0, 0))
    return _fused(name, (t // tm, 1, 1), [(x, row), (g, vec)], [(_sds((t, d), BF16), row)], [], epilogue,
                  temp_bytes=4 * tm * d * 4)[0]


def _rms_bwd(name, x, g, dh, resid, deps=()):
    t, d = x.shape
    tm = _row_tile(t)

    def epilogue(_, ins, outs):
        xv, gv, dhv = ins[0][...], ins[1][...], ins[2][...]
        r = lax.rsqrt(jnp.mean(xv * xv, axis=-1, keepdims=True) + RMS_EPS)
        xh = xv * r
        u = dhv * gv
        dot = jnp.mean(u * xh, axis=-1, keepdims=True)
        outs[0][...] = ins[3][...] + r * (u - xh * dot)

        @pl.when(pl.program_id(0) == 0)
        def _():
            outs[1][...] = jnp.zeros_like(outs[1])

        outs[1][0:1, :] += jnp.sum(dhv * xh, axis=0, keepdims=True)

    row = pl.BlockSpec((tm, d), lambda i, j, k: (i, 0))
    vec = pl.BlockSpec((1, d), lambda i, j, k: (0, 0))
    acc = pl.BlockSpec((8, d), lambda i, j, k: (0, 0))
    return _fused(name, (t // tm, 1, 1), [(x, row), (g, vec), (dh, row), (resid, row)],
                  [(_sds((t, d), F32), row), (_sds((8, d), F32), acc)], [], epilogue,
                  temp_bytes=6 * tm * d * 4, semantics=("arbitrary", "arbitrary", "arbitrary"), deps=deps)


def _loss_dy(y, target):
    t, d = y.shape
    tm = _row_tile(t)

    def epilogue(_, ins, outs):
        e = ins[0][...] - ins[1][...]
        outs[0][...] = e * (1.0 / d)

        @pl.when(pl.program_id(0) == 0)
        def _():
            outs[1][...] = jnp.zeros_like(outs[1])

        part = jnp.sum(jnp.sum(e * e, axis=1, keepdims=True), axis=0, keepdims=True)
        outs[1][...] += jnp.broadcast_to(part, outs[1].shape)

    row = pl.BlockSpec((tm, d), lambda i, j, k: (i, 0))
    acc = pl.BlockSpec((8, LANES), lambda i, j, k: (0, 0))
    return _fused("loss_dy", (t // tm, 1, 1), [(y, row), (target, row)],
                  [(_sds((t, d), F32), row), (_sds((8, LANES), F32), acc)], [], epilogue,
                  temp_bytes=3 * tm * d * 4, semantics=("arbitrary", "arbitrary", "arbitrary"))


def _ffn_up(name, h, wgu):
    t, d = h.shape
    nb = wgu.shape[2]
    f = 4 * nb
    tm = min(t, 512)

    def body(h_ref, wg_ref, wu_ref, gu_ref, a_ref):
        hv = h_ref[...]
        for c0 in range(0, nb, MXU_COLS):
            cs = slice(c0, min(c0 + MXU_COLS, nb))
            g = jnp.dot(hv, wg_ref[:, cs], preferred_element_type=F32)
            u = jnp.dot(hv, wu_ref[:, cs], preferred_element_type=F32)
            gu_ref[0, :, cs] = g.astype(BF16)
            gu_ref[1, :, cs] = u.astype(BF16)
            a_ref[:, cs] = (g * _sigmoid(g) * u).astype(BF16)

    blocks = tm * d * 2 + 2 * d * nb * 2 + 3 * tm * nb * 2
    return pl.pallas_call(
        body, name=name, grid=(4, t // tm),
        in_specs=[pl.BlockSpec((tm, d), lambda j, i: (i, 0)),
                  pl.BlockSpec((None, d, nb), lambda j, i: (j, 0, 0)),
                  pl.BlockSpec((None, d, nb), lambda j, i: (j + 4, 0, 0))],
        out_specs=[pl.BlockSpec((2, tm, nb), lambda j, i: (0, i, j)),
                   pl.BlockSpec((tm, nb), lambda j, i: (i, j))],
        out_shape=[_sds((2, t, f), BF16), _sds((t, f), BF16)],
        compiler_params=_params(("parallel", "parallel"), blocks, 8 * tm * MXU_COLS * 4),
    )(h, wgu, wgu)


def _ffn_down(name, a, wd, x):
    t, f = a.shape
    d = wd.shape[1]
    tm = min(t, 512)
    tn = min(d, 1024)

    def epilogue(acc, ins, outs):
        outs[0][...] = ins[2][...] + 0.5 * acc

    blk = pl.BlockSpec((tm, tn), lambda j, i, k: (i, j))
    return _fused(name, (d // tn, t // tm, 1),
                  [(a, pl.BlockSpec((tm, f), lambda j, i, k: (i, 0))),
                   (wd, pl.BlockSpec((f, tn), lambda j, i, k: (0, j))),
                   (x, blk)],
                  [(_sds((t, d), F32), blk)],
                  [(0, 1, NN)], epilogue, temp_bytes=2 * tm * tn * 4)[0]


def _ffn_bwd_act(name, dy, wd, gu, deps=()):
    t, d = dy.shape
    f = wd.shape[0]
    nb = f // 4
    tm = min(t, 512)

    def body(dy_ref, wd_ref, gu_ref, *rest):
        dgu_ref, a_ref = rest[-2], rest[-1]
        dyv = dy_ref[...].astype(BF16)
        for c0 in range(0, nb, MXU_COLS):
            cs = slice(c0, min(c0 + MXU_COLS, nb))
            da = 0.5 * lax.dot_general(dyv, wd_ref[cs, :], NT, preferred_element_type=F32)
            g = gu_ref[0, :, cs].astype(F32)
            u = gu_ref[1, :, cs].astype(F32)
            s = _sigmoid(g)
            silu = g * s
            dgu_ref[0, :, cs] = (da * u * (s * (1.0 + g * (1.0 - s)))).astype(BF16)
            dgu_ref[1, :, cs] = (da * silu).astype(BF16)
            a_ref[:, cs] = (silu * u).astype(BF16)

    blocks = tm * d * 4 + nb * d * 2 + 5 * tm * nb * 2
    return pl.pallas_call(
        body, name=name, grid=(4, t // tm),
        in_specs=[pl.BlockSpec((tm, d), lambda j, i: (i, 0)),
                  pl.BlockSpec((nb, d), lambda j, i: (j, 0)),
                  pl.BlockSpec((2, tm, nb), lambda j, i: (0, i, j))] + [_ANY] * len(deps),
        out_specs=[pl.BlockSpec((2, tm, nb), lambda j, i: (0, i, j)), pl.BlockSpec((tm, nb), lambda j, i: (i, j))],
        out_shape=[_sds((2, t, f), BF16), _sds((t, f), BF16)],
        compiler_params=_params(("parallel", "parallel"), blocks, tm * d * 2 + 8 * tm * MXU_COLS * 4),
    )(dy, wd, gu, *deps)


def _ffn_bwd_dwd(name, a, dy, deps=(), side=None):
    t, f = a.shape
    d = dy.shape[1]
    tm = f // 4
    tn = min(d, 512)

    def epilogue(acc, ins, outs):
        outs[0][...] = (0.5 * acc).astype(BF16)

    return _fused(name, (4, d // tn, 1),
                  [(a, pl.BlockSpec((t, tm), lambda i, j, k: (0, i))),
                   (dy, pl.BlockSpec((t, tn), lambda i, j, k: (0, j)))],
                  [(_sds((f, d), BF16), pl.BlockSpec((tm, tn), lambda i, j, k: (i, j)))],
                  [(0, 1, TN)], epilogue, temp_bytes=t * tn * 2 + 2 * tm * tn * 4, deps=deps, side=side)


def _ffn_bwd_dh(name, dgu, wgu, deps=(), side=None):
    _, t, f = dgu.shape
    d, nb = wgu.shape[1], wgu.shape[2]
    tm = min(t, 512)

    def products(ins):
        return (lax.dot_general(ins[0][:, 0:nb], ins[1][0], NT, preferred_element_type=F32)
                + lax.dot_general(ins[0][:, nb:2 * nb], ins[1][1], NT, preferred_element_type=F32))

    def epilogue(acc, ins, outs):
        outs[0][...] = acc

    return _fused(name, (t // tm, 1, 4),
                  [(dgu, pl.BlockSpec((None, tm, 2 * nb), lambda i, j, k: (k // 2, i, k % 2))),
                   (wgu, pl.BlockSpec((2, d, nb), lambda i, j, k: (k, 0, 0)))],
                  [(_sds((t, d), F32), pl.BlockSpec((tm, d), lambda i, j, k: (i, 0)))],
                  products, epilogue, nk=4, acc_shape=(tm, d), temp_bytes=tm * d * 4, deps=deps, side=side)


def _ffn_bwd_dwgu(name, h, dgu, deps=(), side=None):
    t, d = h.shape
    nb = dgu.shape[2] // 4
    tm = min(d, 512)

    def epilogue(acc, ins, outs):
        outs[0][...] = acc.astype(BF16)

    return _fused(name, (N_DEV, d // tm, 1),
                  [(h, pl.BlockSpec((t, tm), lambda i, j, k: (0, j))),
                   (dgu, pl.BlockSpec((None, t, nb), lambda i, j, k: (i // 4, 0, i % 4)))],
                  [(_sds((N_DEV, d, nb), BF16), pl.BlockSpec((None, tm, nb), lambda i, j, k: (i, j, 0)))],
                  [(0, 1, TN)], epilogue, temp_bytes=2 * tm * nb * 4, deps=deps, side=side)


def _proj(h, w_in):
    t, d = h.shape
    nb = w_in.shape[3]
    tm = min(t, 512)

    def body(h_ref, w_ref, o_ref):
        hv = h_ref[...]
        o_ref[:, 0:nb] = jnp.dot(hv, w_ref[0], preferred_element_type=F32).astype(BF16)
        o_ref[:, nb:2 * nb] = jnp.dot(hv, w_ref[1], preferred_element_type=F32).astype(BF16)

    blocks = tm * d * 2 + 2 * d * nb * 2 + tm * 2 * nb * 4
    return pl.pallas_call(
        body, name="mix_proj", grid=(4, t // tm),
        in_specs=[pl.BlockSpec((tm, d), lambda j, i: (i, 0)),
                  pl.BlockSpec((None, 2, d, nb), lambda j, i: (j, 0, 0, 0))],
        out_specs=pl.BlockSpec((tm, 2 * nb), lambda j, i: (i, j)),
        out_shape=_sds((t, N_DEV * nb), BF16),
        compiler_params=_params(("parallel", "parallel"), blocks, 2 * tm * nb * 4),
    )(h, w_in)


def _shift_rows(u, k):
    t = u.shape[0]
    rolled = pltpu.roll(u, k % t, axis=0)
    row = lax.broadcasted_iota(jnp.int32, u.shape, 0)
    keep = (row >= k) if k > 0 else (row < t + k)
    return jnp.where(keep, rolled, 0.0)


def _conv_fwd(proj, conv_w):
    t = proj.shape[0]
    cw = conv_w.shape[1]
    tc = min(cw, 256)
    nc = cw // tc

    def epilogue(_, ins, outs):
        u = ins[2][...].astype(F32) * ins[0][...].astype(F32)
        w = ins[3][...]
        y = u * w[2:3, :] + _shift_rows(u, 1) * w[1:2, :] + _shift_rows(u, 2) * w[0:1, :]
        outs[0][...] = (ins[1][...].astype(F32) * y).astype(BF16)

    def col(seg):
        return pl.BlockSpec((t, tc), lambda i, j, k: (0, seg * nc + i))

    return _fused("conv_fwd", (nc, 1, 1),
                  [(proj, col(0)), (proj, col(1)), (proj, col(2)),
                   (conv_w, pl.BlockSpec((8, tc), lambda i, j, k: (0, i)))],
                  [(_sds((t, cw), BF16), pl.BlockSpec((t, tc), lambda i, j, k: (0, i)))],
                  [], epilogue, temp_bytes=6 * t * tc * 4)[0]


def _conv_bwd(proj, conv_w, dca, deps=()):
    t = proj.shape[0]
    cw = conv_w.shape[1]
    tc = min(cw, 256)
    nc = cw // tc

    def epilogue(_, ins, outs):
        xc, bg, cg = ins[0][...].astype(F32), ins[1][...].astype(F32), ins[2][...].astype(F32)
        w, dc = ins[3][...], ins[4][...]
        u = cg * xc
        u1, u2 = _shift_rows(u, 1), _shift_rows(u, 2)
        y = u * w[2:3, :] + u1 * w[1:2, :] + u2 * w[0:1, :]
        dconv = dc * bg
        du = dconv * w[2:3, :] + _shift_rows(dconv, -1) * w[1:2, :] + _shift_rows(dconv, -2) * w[0:1, :]
        outs[0][0] = (du * cg).astype(BF16)
        outs[0][1] = (dc * y).astype(BF16)
        outs[0][2] = (du * xc).astype(BF16)
        outs[1][...] = jnp.zeros_like(outs[1])
        outs[1][0:1, :] = jnp.sum(dconv * u2, axis=0, keepdims=True)
        outs[1][1:2, :] = jnp.sum(dconv * u1, axis=0, keepdims=True)
        outs[1][2:3, :] = jnp.sum(dconv * u, axis=0, keepdims=True)

    def col(seg):
        return pl.BlockSpec((t, tc), lambda i, j, k: (0, seg * nc + i))

    own = pl.BlockSpec((t, tc), lambda i, j, k: (0, i))
    wspec = pl.BlockSpec((8, tc), lambda i, j, k: (0, i))
    return _fused("conv_bwd", (nc, 1, 1),
                  [(proj, col(0)), (proj, col(1)), (proj, col(2)), (conv_w, wspec), (dca, own)],
                  [(_sds((3, t, cw), BF16), pl.BlockSpec((3, t, tc), lambda i, j, k: (0, 0, i))),
                   (_sds((8, cw), F32), wspec)],
                  [], epilogue, temp_bytes=10 * t * tc * 4, deps=deps)


def _split3(x):
    hi = x.astype(BF16)
    r1 = x - hi.astype(F32)
    mid = r1.astype(BF16)
    lo = (r1 - mid.astype(F32)).astype(BF16)
    return hi, mid, lo


def _head_selector(width):
    r = lax.broadcasted_iota(jnp.int32, (width, LANES), 0)
    c = lax.broadcasted_iota(jnp.int32, (width, LANES), 1)
    return (lax.shift_right_logical(r, 6) == c).astype(BF16)


def _head_sum(x, sel):
    return sum(jnp.dot(p, sel, preferred_element_type=F32) for p in _split3(x))


def _head_bcast(r, sel):
    return sum(lax.dot_general(p, sel, NT, preferred_element_type=F32) for p in _split3(r))


def _rope(x, c, sa, sb):
    n = x.shape[1]
    return x * c + pltpu.roll(x, n - ROT_DIM // 2, axis=1) * sa + pltpu.roll(x, ROT_DIM // 2, axis=1) * sb


def _rope_t(d, c, sa, sb):
    n = d.shape[1]
    return d * c + pltpu.roll(d * sa, ROT_DIM // 2, axis=1) + pltpu.roll(d * sb, n - ROT_DIM // 2, axis=1)


def _tile_lanes(tab, width):
    return tab if width == tab.shape[1] else jnp.tile(tab, (1, width // tab.shape[1]))


def _qk_prep(proj, gq, gk, rope_tabs, cw, kw):
    t = proj.shape[0]
    tm = _row_tile(t)

    def epilogue(_, ins, outs):
        c, sa, sb = ins[5][...], ins[6][...], ins[7][...]
        for src, gain, dst, width in ((0, 3, 0, cw), (1, 4, 1, kw)):
            xv = ins[src][...].astype(F32)
            sel = _head_selector(width)
            r = lax.rsqrt(_head_sum(xv * xv, sel) * (1.0 / HEAD_DIM) + RMS_EPS)
            xn = xv * _head_bcast(r, sel) * ins[gain][...]
            outs[dst][...] = _rope(xn, _tile_lanes(c, width), _tile_lanes(sa, width), _tile_lanes(sb, width)).astype(BF16)
        outs[2][...] = ins[2][...].astype(BF16)

    kblk = cw // kw
    tab = pl.BlockSpec((tm, LANES), lambda i, j, k: (i, 0))
    kspec = pl.BlockSpec((tm, kw), lambda i, j, k: (i, 0))
    return _fused("qk_prep", (t // tm, 1, 1),
                  [(proj, pl.BlockSpec((tm, cw), lambda i, j, k: (i, 3))),
                   (proj, pl.BlockSpec((tm, kw), lambda i, j, k: (i, 4 * kblk))),
                   (proj, pl.BlockSpec((tm, kw), lambda i, j, k: (i, 4 * kblk + 1))),
                   (gq, pl.BlockSpec((1, cw), lambda i, j, k: (0, 0))),
                   (gk, pl.BlockSpec((1, kw), lambda i, j, k: (0, 0))),
                   (rope_tabs[0], tab), (rope_tabs[1], tab), (rope_tabs[2], tab)],
                  [(_sds((t, cw), BF16), pl.BlockSpec((tm, cw), lambda i, j, k: (i, 0))),
                   (_sds((t, kw), BF16), kspec), (_sds((t, kw), BF16), kspec)],
                  [], epilogue, temp_bytes=12 * tm * cw * 4)


def _qk_prep_bwd(proj, gq, gk, rope_tabs, dq, dkc, dkp, dvc, dvp, cw, kw):
    t = proj.shape[0]
    tm = BLOCK
    nblk = t // tm

    def epilogue(_, ins, outs):
        c, sa, sb = ins[5][...], ins[6][...], ins[7][...]
        has_next = (pl.program_id(0) < nblk - 1).astype(F32)
        dk = ins[9][...] + has_next * ins[10][...]
        dv = ins[11][...] + has_next * ins[12][...]
        pieces = []
        for src, gain, dval, dst, width in ((0, 3, ins[8][...], 1, cw), (1, 4, dk, 2, kw)):
            xv, gv = ins[src][...].astype(F32), ins[gain][...]
            sel = _head_selector(width)
            r = _head_bcast(lax.rsqrt(_head_sum(xv * xv, sel) * (1.0 / HEAD_DIM) + RMS_EPS), sel)
            xh = xv * r
            dxn = _rope_t(dval, _tile_lanes(c, width), _tile_lanes(sa, width), _tile_lanes(sb, width))
            u = dxn * gv
            dot = _head_bcast(_head_sum(u * xh, sel), sel) * (1.0 / HEAD_DIM)
            pieces.append((r * (u - xh * dot)).astype(BF16))
            ri = lax.broadcasted_iota(jnp.int32, (width, LANES), 0)
            ci = lax.broadcasted_iota(jnp.int32, (width, LANES), 1)
            fold = (lax.bitwise_and(ri, HEAD_DIM - 1) == ci).astype(BF16)
            colsum = jnp.broadcast_to(jnp.sum(dxn * xh, axis=0, keepdims=True), (8, width))
            part = sum(jnp.dot(p, fold, preferred_element_type=F32) for p in _split3(colsum))

            @pl.when(pl.program_id(0) == 0)
            def _():
                outs[dst][...] = jnp.zeros_like(outs[dst])

            outs[dst][0:1, :] += part[0:1, :]
        outs[0][:, 0:cw] = pieces[0]
        outs[0][:, cw:cw + kw] = pieces[1]
        outs[0][:, cw + kw:cw + 2 * kw] = dv.astype(BF16)

    kblk = cw // kw
    tab = pl.BlockSpec((tm, LANES), lambda i, j, k: (i, 0))
    kcur = pl.BlockSpec((tm, kw), lambda i, j, k: (i, 0))
    knext = pl.BlockSpec((tm, kw), lambda i, j, k: (jnp.minimum(i + 1, nblk - 1), 0))
    acc = pl.BlockSpec((8, LANES), lambda i, j, k: (0, 0))
    return _fused("qk_prep_bwd", (nblk, 1, 1),
                  [(proj, pl.BlockSpec((tm, cw), lambda i, j, k: (i, 3))),
                   (proj, pl.BlockSpec((tm, kw), lambda i, j, k: (i, 4 * kblk))),
                   (proj, pl.BlockSpec((tm, kw), lambda i, j, k: (i, 4 * kblk + 1))),
                   (gq, pl.BlockSpec((1, cw), lambda i, j, k: (0, 0))),
                   (gk, pl.BlockSpec((1, kw), lambda i, j, k: (0, 0))),
                   (rope_tabs[0], tab), (rope_tabs[1], tab), (rope_tabs[2], tab),
                   (dq, pl.BlockSpec((tm, cw), lambda i, j, k: (i, 0))),
                   (dkc, kcur), (dkp, knext), (dvc, kcur), (dvp, knext)],
                  [(_sds((t, cw + 2 * kw), BF16), pl.BlockSpec((tm, cw + 2 * kw), lambda i, j, k: (i, 0))),
                   (_sds((8, LANES), F32), acc), (_sds((8, LANES), F32), acc)],
                  [], epilogue, temp_bytes=16 * tm * cw * 4, semantics=("arbitrary", "arbitrary", "arbitrary"))


def _attn_mask(n):
    key = lax.broadcasted_iota(jnp.int32, (2 * BLOCK, GROUP * BLOCK), 0)
    qry = lax.bitwise_and(lax.broadcasted_iota(jnp.int32, (2 * BLOCK, GROUP * BLOCK), 1), BLOCK - 1)
    return (key > qry) & (key <= qry + BLOCK) & ((key >= BLOCK) | (n > 0))


def _stack_heads(x, h):
    return jnp.concatenate([x[:, (h * GROUP + g) * HEAD_DIM:(h * GROUP + g + 1) * HEAD_DIM] for g in range(GROUP)], axis=0)


def _softmax_with_sink(q4, k2, sink_ref, h, valid):
    sink = jnp.concatenate([sink_ref[h * GROUP + g:h * GROUP + g + 1, :] for g in range(GROUP)], axis=1)
    s = lax.dot_general(k2, q4, NT, preferred_element_type=F32) * ATTN_SCALE
    s = jnp.where(valid, s, NEG_INF)
    m = jnp.maximum(jnp.max(s, axis=0, keepdims=True), sink)
    p = jnp.exp(s - m)
    es = jnp.exp(sink - m)
    inv = 1.0 / (jnp.sum(p, axis=0, keepdims=True) + es)
    return p * inv, es * inv


def _attn_fwd(qn, kn, vb, sink_rows):
    t, cw = qn.shape
    kw = kn.shape[1]
    nkv = kw // HEAD_DIM

    def body(q_ref, kp_ref, kc_ref, vp_ref, vc_ref, sink_ref, o_ref):
        valid = _attn_mask(pl.program_id(0))
        qv = q_ref[...]
        kp, kc, vp, vc = kp_ref[...], kc_ref[...], vp_ref[...], vc_ref[...]
        outs = []
        for h in range(nkv):
            hs = slice(h * HEAD_DIM, (h + 1) * HEAD_DIM)
            k2 = jnp.concatenate([kp[:, hs], kc[:, hs]], axis=0)
            v2 = jnp.concatenate([vp[:, hs], vc[:, hs]], axis=0)
            pn, _ = _softmax_with_sink(_stack_heads(qv, h), k2, sink_ref, h, valid)
            o4 = lax.dot_general(pn.astype(BF16), v2, TN, preferred_element_type=F32)
            outs += [o4[g * BLOCK:(g + 1) * BLOCK] for g in range(GROUP)]
        o_ref[...] = jnp.concatenate(outs, axis=-1).astype(BF16)

    cur = lambda n: (n, 0)
    prev = lambda n: (jnp.maximum(n - 1, 0), 0)
    return pl.pallas_call(
        body, name="attn_fwd", grid=(t // BLOCK,),
        in_specs=[pl.BlockSpec((BLOCK, cw), cur),
                  pl.BlockSpec((BLOCK, kw), prev), pl.BlockSpec((BLOCK, kw), cur),
                  pl.BlockSpec((BLOCK, kw), prev), pl.BlockSpec((BLOCK, kw), cur),
                  pl.BlockSpec(sink_rows.shape, lambda n: (0, 0))],
        out_specs=pl.BlockSpec((BLOCK, cw), cur),
        out_shape=_sds((t, cw), BF16),
        compiler_params=_params(("parallel",), BLOCK * (cw + 4 * kw) * 2 + BLOCK * cw * 2, 8 << 20),
    )(qn, kn, kn, vb, vb, sink_rows)


def _attn_bwd(qn, kn, vb, sink_rows, do):
    t, cw = qn.shape
    kw = kn.shape[1]
    nkv = kw // HEAD_DIM
    nq = nkv * GROUP

    def body(q_ref, kp_ref, kc_ref, vp_ref, vc_ref, sink_ref, do_ref,
             dq_ref, dkc_ref, dkp_ref, dvc_ref, dvp_ref, dsink_ref):
        n = pl.program_id(0)
        valid = _attn_mask(n)
        qv, dov = q_ref[...], do_ref[...]
        kp, kc, vp, vc = kp_ref[...], kc_ref[...], vp_ref[...], vc_ref[...]
        dqs, dks, dvs, dsinks = [], [], [], []
        for h in range(nkv):
            hs = slice(h * HEAD_DIM, (h + 1) * HEAD_DIM)
            k2 = jnp.concatenate([kp[:, hs], kc[:, hs]], axis=0)
            v2 = jnp.concatenate([vp[:, hs], vc[:, hs]], axis=0)
            q4 = _stack_heads(qv, h)
            dob = _stack_heads(dov, h).astype(BF16)
            pn, psink = _softmax_with_sink(q4, k2, sink_ref, h, valid)
            dpn = lax.dot_general(v2, dob, NT, preferred_element_type=F32)
            dvs.append(jnp.dot(pn.astype(BF16), dob, preferred_element_type=F32))
            delta = jnp.sum(pn * dpn, axis=0, keepdims=True)
            ds = (pn * (dpn - delta) * ATTN_SCALE).astype(BF16)
            dks.append(jnp.dot(ds, q4, preferred_element_type=F32))
            dq4 = lax.dot_general(ds, k2, TN, preferred_element_type=F32)
            dsink4 = -psink * delta
            for g in range(GROUP):
                dqs.append(dq4[g * BLOCK:(g + 1) * BLOCK])
                dsinks.append(jnp.broadcast_to(jnp.sum(dsink4[:, g * BLOCK:(g + 1) * BLOCK], axis=1, keepdims=True), (1, LANES)))
        dq_ref[...] = jnp.concatenate(dqs, axis=-1)
        dkp_ref[...] = jnp.concatenate([d[:BLOCK] for d in dks], axis=-1)
        dkc_ref[...] = jnp.concatenate([d[BLOCK:] for d in dks], axis=-1)
        dvp_ref[...] = jnp.concatenate([d[:BLOCK] for d in dvs], axis=-1)
        dvc_ref[...] = jnp.concatenate([d[BLOCK:] for d in dvs], axis=-1)

        @pl.when(n == 0)
        def _():
            dsink_ref[...] = jnp.zeros_like(dsink_ref)

        dsink_ref[...] += jnp.concatenate(dsinks, axis=0)

    cur = lambda n: (n, 0)
    prev = lambda n: (jnp.maximum(n - 1, 0), 0)
    kspec = pl.BlockSpec((BLOCK, kw), cur)
    return pl.pallas_call(
        body, name="attn_bwd", grid=(t // BLOCK,),
        in_specs=[pl.BlockSpec((BLOCK, cw), cur),
                  pl.BlockSpec((BLOCK, kw), prev), kspec,
                  pl.BlockSpec((BLOCK, kw), prev), kspec,
                  pl.BlockSpec(sink_rows.shape, lambda n: (0, 0)),
                  pl.BlockSpec((BLOCK, cw), cur)],
        out_specs=[pl.BlockSpec((BLOCK, cw), cur), kspec, kspec, kspec, kspec,
                   pl.BlockSpec((nq, LANES), lambda n: (0, 0))],
        out_shape=[_sds((t, cw), F32)] + [_sds((t, kw), F32)] * 4 + [_sds((nq, LANES), F32)],
        compiler_params=_params(("arbitrary",), BLOCK * (cw + 4 * kw) * 2 + 2 * BLOCK * cw * 4 + 4 * BLOCK * kw * 4, 12 << 20),
    )(qn, kn, kn, vb, vb, sink_rows, do)


def _mix_out(ca, o, woc, woa, proj):
    t, cw = ca.shape
    nb = woc.shape[2]
    d = N_DEV * nb
    tm = min(t, 1024)
    ga0 = (3 * cw + cw + 2 * (cw // 4)) // nb

    def body(ca_ref, o_ref, woc_ref, woa_ref, ga_ref, gb_ref, m_ref, ya_ref, yb_ref):
        ya = jnp.dot(ca_ref[...], woc_ref[...], preferred_element_type=F32)
        yb = jnp.dot(o_ref[...], woa_ref[...], preferred_element_type=F32)
        ya_ref[...] = ya.astype(BF16)
        yb_ref[...] = yb.astype(BF16)
        m_ref[...] = (_sigmoid(ga_ref[...].astype(F32)) * ya + _sigmoid(gb_ref[...].astype(F32)) * yb).astype(BF16)

    act = pl.BlockSpec((tm, cw), lambda i, j: (i, 0))
    wsp = pl.BlockSpec((None, cw, nb), lambda i, j: (j, 0, 0))
    osp = pl.BlockSpec((tm, nb), lambda i, j: (i, j))
    blocks = 2 * tm * cw * 2 + 2 * cw * nb * 2 + 2 * tm * nb * 4 + 3 * tm * nb * 2
    return pl.pallas_call(
        body, name="mix_out", grid=(t // tm, N_DEV),
        in_specs=[act, act, wsp, wsp,
                  pl.BlockSpec((tm, nb), lambda i, j: (i, ga0 + j)),
                  pl.BlockSpec((tm, nb), lambda i, j: (i, ga0 + N_DEV + j))],
        out_specs=[osp, osp, osp],
        out_shape=[_sds((t, d), BF16)] * 3,
        compiler_params=_params(("parallel", "parallel"), blocks, 6 * tm * nb * 4),
    )(ca, o, woc, woa, proj, proj)


def _mix_residual(merged, wo, x):
    t, d = x.shape
    tm = min(t, 512)

    def epilogue(acc, ins, outs):
        outs[0][...] = ins[2][...] + acc

    row = pl.BlockSpec((tm, d), lambda i, j, k: (i, 0))
    return _fused("mix_residual", (t // tm, 1, 1),
                  [(merged, row), (wo, pl.BlockSpec((d, d), lambda i, j, k: (0, 0))), (x, row)],
                  [(_sds((t, d), F32), row)], [(0, 1, NN)], epilogue, temp_bytes=2 * tm * d * 4)[0]


def _mix_bwd_gates(dx, wo, ya, yb, proj, cw):
    t, d = dx.shape
    tm = min(t, 1024)
    tn = min(d, 512)
    ga0 = (4 * cw + 2 * (cw // 4)) // tn

    def epilogue(acc, ins, outs):
        sa, sb = _sigmoid(ins[4][...].astype(F32)), _sigmoid(ins[5][...].astype(F32))
        outs[0][...] = (acc * sa).astype(BF16)
        outs[1][...] = (acc * sb).astype(BF16)
        outs[2][0] = (acc * ins[2][...].astype(F32) * sa * (1.0 - sa)).astype(BF16)
        outs[2][1] = (acc * ins[3][...].astype(F32) * sb * (1.0 - sb)).astype(BF16)

    blk = pl.BlockSpec((tm, tn), lambda i, j, k: (i, j))
    return _fused("mix_bwd_gates", (t // tm, d // tn, 1),
                  [(dx, pl.BlockSpec((tm, d), lambda i, j, k: (i, 0))),
                   (wo, pl.BlockSpec((tn, d), lambda i, j, k: (j, 0))),
                   (ya, blk), (yb, blk),
                   (proj, pl.BlockSpec((tm, tn), lambda i, j, k: (i, ga0 + j))),
                   (proj, pl.BlockSpec((tm, tn), lambda i, j, k: (i, ga0 + d // tn + j)))],
                  [(_sds((t, d), BF16), blk), (_sds((t, d), BF16), blk),
                   (_sds((2, t, d), BF16), pl.BlockSpec((2, tm, tn), lambda i, j, k: (0, i, j)))],
                  [(0, 1, NT)], epilogue, temp_bytes=8 * tm * tn * 4)


def _tn_matmul(name, a, b, tm, out_dtype=BF16):
    t, m = a.shape
    n = b.shape[1]
    tk = min(t, 512)

    def epilogue(acc, ins, outs):
        outs[0][...] = acc.astype(out_dtype)

    return _fused(name, (m // tm, 1, t // tk),
                  [(a, pl.BlockSpec((tk, tm), lambda i, j, k: (k, i))),
                   (b, pl.BlockSpec((tk, n), lambda i, j, k: (k, 0)))],
                  [(_sds((m, n), out_dtype), pl.BlockSpec((tm, n), lambda i, j, k: (i, 0)))],
                  [(0, 1, TN)], epilogue, nk=t // tk, acc_shape=(tm, n), temp_bytes=tm * n * 4)[0]


def _out_proj_bwd_act(dya, dyb, woc, woa, deps=()):
    t, d = dya.shape
    kdim, nb = woc.shape[1], woc.shape[2]
    tm = min(t, 512)

    def body(dya_ref, dyb_ref, woc_ref, woa_ref, *rest):
        for dy_ref, w_ref, o_ref in ((dya_ref, woc_ref, rest[-2]), (dyb_ref, woa_ref, rest[-1])):
            total = None
            for j in range(N_DEV):
                part = lax.dot_general(dy_ref[:, j * nb:(j + 1) * nb], w_ref[j], NT, preferred_element_type=F32)
                total = part if total is None else total + part
            o_ref[...] = total

    row = pl.BlockSpec((tm, d), lambda i: (i, 0))
    wsp = pl.BlockSpec((N_DEV, kdim, nb), lambda i: (0, 0, 0))
    osp = pl.BlockSpec((tm, kdim), lambda i: (i, 0))
    blocks = 2 * tm * d * 2 + 2 * N_DEV * kdim * nb * 2 + 2 * tm * kdim * 4
    return pl.pallas_call(
        body, name="mix_bwd_dca_do", grid=(t // tm,),
        in_specs=[row, row, wsp, wsp] + [_ANY] * len(deps), out_specs=[osp, osp],
        out_shape=[_sds((t, kdim), F32)] * 2,
        compiler_params=_params(("parallel",), blocks, 4 * tm * kdim * 4),
    )(dya, dyb, woc, woa, *deps)


def _out_proj_bwd_w(ca, o, dya, dyb, nb):
    t, kdim = ca.shape

    def body(ca_ref, o_ref, dya_ref, dyb_ref, dwoc_ref, dwoa_ref):
        dwoc_ref[...] = lax.dot_general(ca_ref[...], dya_ref[...], TN, preferred_element_type=F32).astype(BF16)
        dwoa_ref[...] = lax.dot_general(o_ref[...], dyb_ref[...], TN, preferred_element_type=F32).astype(BF16)

    act = pl.BlockSpec((t, kdim), lambda j: (0, 0))
    col = pl.BlockSpec((t, nb), lambda j: (0, j))
    osp = pl.BlockSpec((None, kdim, nb), lambda j: (j, 0, 0))
    blocks = 2 * t * kdim * 2 + 2 * t * nb * 2 + 2 * kdim * nb * 2
    return pl.pallas_call(
        body, name="mix_bwd_dwoc_dwoa", grid=(N_DEV,),
        in_specs=[act, act, col, col], out_specs=[osp, osp],
        out_shape=[_sds((N_DEV, kdim, nb), BF16)] * 2,
        compiler_params=_params(("parallel",), blocks, 4 * kdim * nb * 4),
    )(ca, o, dya, dyb)


def _proj_bwd_act(dproj, w_in, deps=()):
    t, n = dproj.shape
    d, nb = w_in.shape[2], w_in.shape[3]
    tm = min(t, 512)

    def epilogue(acc, ins, outs):
        outs[0][...] = acc

    def products(ins):
        return (lax.dot_general(ins[0][:, 0:nb], ins[1][0], NT, preferred_element_type=F32)
                + lax.dot_general(ins[0][:, nb:2 * nb], ins[1][1], NT, preferred_element_type=F32))

    return _fused("mix_bwd_dh", (t // tm, 1, 4),
                  [(dproj, pl.BlockSpec((tm, 2 * nb), lambda i, j, k: (i, k))),
                   (w_in, pl.BlockSpec((None, 2, d, nb), lambda i, j, k: (k, 0, 0, 0)))],
                  [(_sds((t, d), F32), pl.BlockSpec((tm, d), lambda i, j, k: (i, 0)))],
                  products, epilogue, nk=4, acc_shape=(tm, d), temp_bytes=tm * d * 4, deps=deps)[0]


def _proj_bwd_w(h, dproj):
    t, d = h.shape
    nb = dproj.shape[1] // N_DEV
    tm = min(d, 512)

    def body(h_ref, dp_ref, o_ref):
        hv = h_ref[...]
        o_ref[0] = lax.dot_general(hv, dp_ref[:, 0:nb], TN, preferred_element_type=F32).astype(BF16)
        o_ref[1] = lax.dot_general(hv, dp_ref[:, nb:2 * nb], TN, preferred_element_type=F32).astype(BF16)

    blocks = t * tm * 2 + t * 2 * nb * 2 + 2 * tm * nb * 2
    return pl.pallas_call(
        body, name="mix_bwd_dwin", grid=(4, d // tm),
        in_specs=[pl.BlockSpec((t, tm), lambda j, i: (0, i)),
                  pl.BlockSpec((t, 2 * nb), lambda j, i: (0, j))],
        out_specs=pl.BlockSpec((None, 2, tm, nb), lambda j, i: (j, 0, i, 0)),
        out_shape=_sds((4, 2, d, nb), BF16),
        compiler_params=_params(("parallel", "parallel"), blocks, 4 * tm * nb * 4),
    )(h, dproj)


def _adamw_math(w, g, m, v):
    m = ADAM_B1 * m + (1.0 - ADAM_B1) * g
    v = ADAM_B2 * v + (1.0 - ADAM_B2) * (g * g)
    m_hat = m / (1.0 - ADAM_B1 ** ADAM_STEP)
    v_hat = v / (1.0 - ADAM_B2 ** ADAM_STEP)
    delta = -ADAM_LR * (m_hat / (jnp.sqrt(v_hat) + ADAM_EPS) + ADAM_WD * w)
    return delta, m, v


def _adamw(name, parts, w, m, v, tr):
    r, c = w.shape

    def body(p_ref, w_ref, m_ref, v_ref, g_out, d_out, m_out, v_out):
        g = p_ref[0].astype(F32)
        for s in range(1, N_DEV):
            g = g + p_ref[s].astype(F32)
        delta, mn, vn = _adamw_math(w_ref[...], g, m_ref[...], v_ref[...])
        g_out[...] = g
        d_out[...] = delta
        m_out[...] = mn
        v_out[...] = vn

    blk = pl.BlockSpec((tr, c), lambda i: (i, 0))
    blocks = N_DEV * tr * c * parts.dtype.itemsize + 7 * tr * c * 4
    return pl.pallas_call(
        body, name=name, grid=(r // tr,),
        in_specs=[pl.BlockSpec((N_DEV, tr, c), lambda i: (0, i, 0)), blk, blk, blk],
        out_specs=[blk] * 4, out_shape=[_sds((r, c), F32)] * 4,
        compiler_params=_params(("parallel",), blocks, 6 * tr * c * 4),
    )(parts, w, m, v)


def _chip_sum(sums_ref):
    g = sums_ref[0].astype(F32)
    for k in range(1, 4):
        g = g + sums_ref[k].astype(F32)
    return g


def _adamw_chips(name, sums, w, m, v, tr, deps=()):
    r, c = w.shape

    def body(sums_ref, w_ref, m_ref, v_ref, *rest):
        g_out, d_out, m_out, v_out = rest[len(deps):]
        g = _chip_sum(sums_ref)
        delta, mn, vn = _adamw_math(w_ref[...], g, m_ref[...], v_ref[...])
        g_out[...] = g
        d_out[...] = delta
        m_out[...] = mn
        v_out[...] = vn

    blk = pl.BlockSpec((tr, c), lambda i: (i, 0))
    blocks = 4 * tr * c * 2 + 7 * tr * c * 4
    return pl.pallas_call(
        body, name=name, grid=(r // tr,),
        in_specs=[pl.BlockSpec((4, tr, c), lambda i: (0, i, 0)), blk, blk, blk] + [_ANY] * len(deps),
        out_specs=[blk] * 4, out_shape=[_sds((r, c), F32)] * 4,
        compiler_params=_params(("parallel",), blocks, 6 * tr * c * 4),
    )(sums, w, m, v, *deps)


def _adamw_side(contrib, w, m, v, n_tiles, step_of):
    r, c = w.shape
    tr = r // n_tiles
    assert tr * n_tiles == r and tr % 16 == 0, (r, n_tiles)

    def tile(i, j, k):
        return jnp.minimum(step_of(i, j, k), n_tiles - 1)

    blk = pl.BlockSpec((tr, c), lambda i, j, k: (tile(i, j, k), 0))
    ins = [(contrib, pl.BlockSpec((4, tr, c), lambda i, j, k: (0, tile(i, j, k), 0))), (w, blk), (m, blk), (v, blk)]
    outs = [(_sds((r, c), F32), blk)] * 4

    def fn(in_refs, out_refs):
        @pl.when(step_of(pl.program_id(0), pl.program_id(1), pl.program_id(2)) < n_tiles)
        def _():
            g = _chip_sum(in_refs[0])
            delta, mn, vn = _adamw_math(in_refs[1][...], g, in_refs[2][...], in_refs[3][...])
            out_refs[0][...] = g
            out_refs[1][...] = delta
            out_refs[2][...] = mn
            out_refs[3][...] = vn

    return ins, outs, fn


def _rope_tables(t):
    half = ROT_DIM // 2
    inv_freq = 1.0 / (ROPE_THETA ** (jnp.arange(0, ROT_DIM, 2, dtype=F32) / ROT_DIM))
    ang = jnp.arange(t, dtype=F32)[:, None] * inv_freq[None, :]
    cos, sin = jnp.cos(ang), jnp.sin(ang)
    ones = jnp.ones((t, HEAD_DIM - ROT_DIM), F32)
    zeros = jnp.zeros((t, HEAD_DIM - half), F32)
    c = jnp.concatenate([cos, cos, ones], axis=1)
    sa = jnp.concatenate([-sin, zeros], axis=1)
    sb = jnp.concatenate([jnp.zeros((t, half), F32), sin, jnp.zeros((t, HEAD_DIM - ROT_DIM), F32)], axis=1)
    return tuple(jnp.tile(a, (1, LANES // HEAD_DIM)) for a in (c, sa, sb))


def _pad_rows(a, rows=8):
    return jnp.pad(a, ((0, rows - a.shape[0]), (0, 0)))


def kernel(x, g_ffn1, w_gu1, w_down1, g_mix, w_in, conv_w, q_norm_g, k_norm_g, sinks, w_out_conv, w_out_attn, w_o, g_ffn2, w_gu2, w_down2, loss_target, m_g_ffn1, m_w_gu1, m_w_down1, m_g_mix, m_w_in, m_conv_w, m_q_norm_g, m_k_norm_g, m_sinks, m_w_out_conv, m_w_out_attn, m_w_o, m_g_ffn2, m_w_gu2, m_w_down2, v_g_ffn1, v_w_gu1, v_w_down1, v_g_mix, v_w_in, v_conv_w, v_q_norm_g, v_k_norm_g, v_sinks, v_w_out_conv, v_w_out_attn, v_w_o, v_g_ffn2, v_w_gu2, v_w_down2):
    t, d = x.shape[1], x.shape[2]
    cw = d // 2
    kw = cw // GROUP
    nq = cw // HEAD_DIM
    xs, target = x.reshape(t, d), loss_target.reshape(t, d)
    me = 4 * lax.axis_index("x") + 2 * lax.axis_index("y") + lax.axis_index("c")

    big = {"w_gu1": w_gu1, "w_down1": w_down1, "w_in": w_in, "w_out_conv": w_out_conv,
           "w_out_attn": w_out_attn, "w_o": w_o, "w_gu2": w_gu2, "w_down2": w_down2}
    big_m = {"w_gu1": m_w_gu1, "w_down1": m_w_down1, "w_in": m_w_in, "w_out_conv": m_w_out_conv,
             "w_out_attn": m_w_out_attn, "w_o": m_w_o, "w_gu2": m_w_gu2, "w_down2": m_w_down2}
    big_v = {"w_gu1": v_w_gu1, "w_down1": v_w_down1, "w_in": v_w_in, "w_out_conv": v_w_out_conv,
             "w_out_attn": v_w_out_attn, "w_o": v_w_o, "w_gu2": v_w_gu2, "w_down2": v_w_down2}
    names = list(big)

    tiles = {"w_gu1": 256, "w_gu2": 256, "w_in": 256, "w_down1": 176, "w_down2": 176,
             "w_out_conv": 1024, "w_out_attn": 1024, "w_o": 128}

    def row_tile(n):
        r = big[n].shape[1]
        return tiles[n] if r % tiles[n] == 0 else r

    def add_tile(n):
        r, c = big[n].shape[1], big[n].shape[2]
        while r * c * 2 > (3 << 20) and r % 32 == 0:
            r //= 2
        return r

    me_arr = me.astype(jnp.int32).reshape(1)
    sources = [(n, big[n][0], BF16, row_tile(n)) for n in names] + [("conv_w", _pad_rows(conv_w[0]), F32, 8)]
    issue_order = [0, 1, 2, 8, 3, 4, 5, 6, 7]
    first = _place_shard("place_" + names[0], sources[0][1], BF16, me_arr, sources[0][3])
    started = [_gather_start("gather_start_first", [first])]
    early = {2: (big_m["w_in"][0], big_v["w_in"][0])}
    rest = [_place_shard("place_" + sources[i][0], sources[i][1], sources[i][2], me_arr, sources[i][3],
                         deps=(started[0][3],) + early.get(i, ())) for i in issue_order[1:]]
    started.append(_gather_start("gather_start_rest", rest))
    where = {0: (0, 0)}
    where.update({i: (1, p) for p, i in enumerate(issue_order[1:])})

    def fetch(tag, idxs, after):
        call = where[idxs[0]][0]
        send, recv, stacks, _ = started[call]
        positions = [where[i][1] for i in idxs]
        got = _gather_wait("gather_wait_" + tag, positions, send, recv, [stacks[p] for p in positions], after)
        return _forward_to_sibling("gather_forward_" + tag, got)

    rope_tabs = _rope_tables(t)
    gq = jnp.tile(q_norm_g, (1, nq))
    gk = jnp.tile(k_norm_g, (1, nq // GROUP))
    sink_rows = jnp.broadcast_to(sinks[0][:, None], (nq, LANES))

    wts = {}
    h1 = _rms_fwd("ffn1_norm", xs, g_ffn1)
    wts["w_gu1"], = fetch("gu1", [0], started[1][3])
    gu1, a1 = _ffn_up("ffn1_up", h1, wts["w_gu1"])
    wts["w_down1"], = fetch("down1", [1], a1)
    wd1 = wts["w_down1"].reshape(-1, d)
    x1 = _ffn_down("ffn1_down", a1, wd1, xs)
    h2 = _rms_fwd("mix_norm", x1, g_mix)
    wts["w_in"], conv_land = fetch("in", [2, 8], h2)
    w_in_full = wts["w_in"].reshape(4, 2, d, -1)
    conv_full = jnp.transpose(conv_land, (1, 0, 2)).reshape(8, cw)
    proj = _proj(h2, w_in_full)
    ca = _conv_fwd(proj, conv_full)
    qn, kn, vb = _qk_prep(proj, gq, gk, rope_tabs, cw, kw)
    o = _attn_fwd(qn, kn, vb, sink_rows)
    wts["w_out_conv"], wts["w_out_attn"] = fetch("out", [3, 4], o)
    merged, ya, yb = _mix_out(ca, o, wts["w_out_conv"], wts["w_out_attn"], proj)
    wts["w_o"], = fetch("o", [5], merged)
    wo = wts["w_o"].reshape(d, d)
    x2 = _mix_residual(merged, wo, x1)
    h3 = _rms_fwd("ffn2_norm", x2, g_ffn2)
    wts["w_gu2"], = fetch("gu2", [6], h3)
    gu2, a2 = _ffn_up("ffn2_up", h3, wts["w_gu2"])
    wts["w_down2"], = fetch("down2", [7], a2)
    wd2 = wts["w_down2"].reshape(-1, d)
    y = _ffn_down("ffn2_down", a2, wd2, x2)
    dy, sq = _loss_dy(y, target)
    loss = lax.psum(sq[0, 0] * (0.5 / d), ("x", "y", "c"))

    place = jnp.stack([lax.axis_index("c"), 2 * lax.axis_index("x") + lax.axis_index("y")]).astype(jnp.int32)
    def pair_start(tag, group, grads, deps=()):
        stacks = [grads[n].reshape((4, 2) + big[n].shape[1:]) for n in group]
        lands = [lax.empty((4,) + big[n].shape[1:], BF16) for n in group]
        return _pair_start("rs_pair_start_" + tag, stacks, lands, deps)

    def chip_start(tag, group, pending, after):
        send, recv, stacks, lands, _ = pending
        stacks, lands = _pair_wait("rs_pair_wait_" + tag, send, recv, stacks, lands, after)
        added = [_pair_add("rs_pair_add_" + n, st, ld, place, add_tile(n)) for n, st, ld in zip(group, stacks, lands)]
        return _chip_start("rs_chip_start_" + tag, [a[0] for a in added], [a[1] for a in added])

    group_a, group_b, group_c = ["w_down2", "w_gu2"], ["w_o", "w_out_conv", "w_out_attn"], ["w_in"]
    group_d, group_e = ["w_down1"], ["w_gu1"]
    g = {}
    dgu2, a2 = _ffn_bwd_act("ffn2_bwd_act", dy, wd2, gu2)
    g["w_down2"], = _ffn_bwd_dwd("ffn2_bwd_dwd", a2, dy)
    g["w_gu2"], = _ffn_bwd_dwgu("ffn2_bwd_dwgu", h3, dgu2)
    pend_a = pair_start("a", group_a, g)
    dh3, = _ffn_bwd_dh("ffn2_bwd_dh", dgu2, wts["w_gu2"], deps=(pend_a[4],))
    ring_a = chip_start("a", group_a, pend_a, dh3)
    dx2, dg_ffn2 = _rms_bwd("ffn2_bwd_rms", x2, g_ffn2, dh3, dy, deps=(ring_a[4],))

    dya, dyb, dgates = _mix_bwd_gates(dx2, wo, ya, yb, proj, cw)
    g["w_o"] = _tn_matmul("mix_bwd_dwo", merged, dx2, min(d, 1024))
    g["w_out_conv"], g["w_out_attn"] = _out_proj_bwd_w(ca, o, dya, dyb, d // N_DEV)
    pend_b = pair_start("b", group_b, g)
    dca, do = _out_proj_bwd_act(dya, dyb, wts["w_out_conv"], wts["w_out_attn"], deps=(pend_b[4],))
    ring_b = chip_start("b", group_b, pend_b, do)
    d3, dconv_w = _conv_bwd(proj, conv_full, dca, deps=(ring_b[4],))
    dq, dkc, dkp, dvc, dvp, dsink = _attn_bwd(qn, kn, vb, sink_rows, do)
    dqkv, dgq, dgk = _qk_prep_bwd(proj, gq, gk, rope_tabs, dq, dkc, dkp, dvc, dvp, cw, kw)
    dproj = jnp.concatenate([d3[0], d3[1], d3[2], dqkv, dgates[0], dgates[1]], axis=1)
    g["w_in"] = _proj_bwd_w(h2, dproj)
    pend_c = pair_start("c", group_c, g)
    dh2 = _proj_bwd_act(dproj, w_in_full, deps=(pend_c[4],))
    ring_c = chip_start("c", group_c, pend_c, dh2)
    dx1, dg_mix = _rms_bwd("mix_bwd_rms", x1, g_mix, dh2, dx2, deps=(ring_c[4],))

    big_out = {}
    arrived = {}

    def wait_group(tag, group, ring, after):
        send, recv, parts, lands2, _ = ring
        parts, lands2 = _chip_wait("rs_chip_wait_" + tag, send, recv, parts, lands2, after)
        arrived.update(dict(zip(group, lands2)))

    def update(n, after):
        res = _adamw_chips("adamw_" + n, arrived[n], big[n][0], big_m[n][0], big_v[n][0], row_tile(n), deps=(after,))
        big_out[n] = [a[None] for a in res]
        return res[0]

    def update_beside(n, n_tiles, step_of):
        return _adamw_side(arrived[n], big[n][0], big_m[n][0], big_v[n][0], n_tiles, step_of)

    def keep(n, res):
        big_out[n] = [a[None] for a in res]

    dgu1, a1 = _ffn_bwd_act("ffn1_bwd_act", dx1, wd1, gu1)
    wait_group("a", group_a, ring_a, a1)
    g["w_down1"], *res = _ffn_bwd_dwd("ffn1_bwd_dwd", a1, dx1,
                                       side=update_beside("w_down2", 11, lambda i, j, k: i * 4 + j))
    keep("w_down2", res)
    pend_d = pair_start("d", group_d, g)
    g["w_gu1"], *res = _ffn_bwd_dwgu("ffn1_bwd_dwgu", h1, dgu1, deps=(pend_d[4],),
                                      side=update_beside("w_gu2", 32, lambda i, j, k: i * 4 + j))
    keep("w_gu2", res)
    pend_e = pair_start("e", group_e, g)
    ring_d = chip_start("d", group_d, pend_d, pend_e[4])
    ring_e = chip_start("e", group_e, pend_e, ring_d[4])
    wait_group("c", group_c, ring_c, ring_e[4])
    dh1, *res = _ffn_bwd_dh("ffn1_bwd_dh", dgu1, wts["w_gu1"],
                             side=update_beside("w_in", 16, lambda i, j, k: i * 4 + k))
    keep("w_in", res)
    grad_x, dg_ffn1 = _rms_bwd("ffn1_bwd_rms", xs, g_ffn1, dh1, dx1)
    after = grad_x
    for tag, group, ring in (("b", group_b, ring_b), ("d", group_d, ring_d), ("e", group_e, ring_e)):
        wait_group(tag, group, ring, after)
        for n in group:
            after = update(n, after)

    small = {"g_ffn1": dg_ffn1[0:1], "g_mix": dg_mix[0:1], "g_ffn2": dg_ffn2[0:1],
             "q_norm_g": dgq[0:1, :HEAD_DIM], "k_norm_g": dgk[0:1, :HEAD_DIM], "sinks": dsink[:, 0][None],
             "conv_w": dconv_w[0:CONV_K].reshape(1, -1)}
    small_w = {"g_ffn1": g_ffn1, "g_mix": g_mix, "g_ffn2": g_ffn2, "q_norm_g": q_norm_g, "k_norm_g": k_norm_g,
               "sinks": sinks, "conv_w": None}
    small_m = {"g_ffn1": m_g_ffn1, "g_mix": m_g_mix, "g_ffn2": m_g_ffn2, "q_norm_g": m_q_norm_g,
               "k_norm_g": m_k_norm_g, "sinks": m_sinks, "conv_w": m_conv_w}
    small_v = {"g_ffn1": v_g_ffn1, "g_mix": v_g_mix, "g_ffn2": v_g_ffn2, "q_norm_g": v_q_norm_g,
               "k_norm_g": v_k_norm_g, "sinks": v_sinks, "conv_w": v_conv_w}
    snames = list(small)
    widths = [small[n].shape[1] for n in snames]
    total = sum(widths)
    rows = -(-total // LANES)
    rows = -(-rows // 8) * 8

    def pack(vals):
        flat = jnp.concatenate([v.reshape(1, -1) for v in vals], axis=1)
        return jnp.pad(flat, ((0, 0), (0, rows * LANES - total))).reshape(rows, LANES)

    csh = cw // N_DEV

    def place_conv(local, fill):
        full = jnp.full((CONV_K, cw), fill, F32)
        return lax.dynamic_update_slice(full, local, (0, me * csh)).reshape(1, -1)

    pw = pack([small_w[n] if n != "conv_w" else place_conv(conv_w[0], 0.0) for n in snames])
    pm = pack([small_m[n] if n != "conv_w" else place_conv(m_conv_w[0], 0.0) for n in snames])
    pv = pack([small_v[n] if n != "conv_w" else place_conv(v_conv_w[0], 1.0) for n in snames])
    parts = _exchange("gather_small_grads", [pack([small[n] for n in snames])], gather=True, deps=(after,))[0]
    sg, sd, sm, sv = [a.reshape(1, -1) for a in _adamw("adamw_small", parts, pw, pm, pv, rows)]

    def unpack(flat, n):
        off = sum(widths[:snames.index(n)])
        piece = flat[:, off:off + widths[snames.index(n)]]
        if n == "conv_w":
            piece = lax.dynamic_slice(piece.reshape(CONV_K, cw), (0, me * csh), (CONV_K, csh))[None]
        return piece

    order = ["g_ffn1", "w_gu1", "w_down1", "g_mix", "w_in", "conv_w", "q_norm_g", "k_norm_g", "sinks",
             "w_out_conv", "w_out_attn", "w_o", "g_ffn2", "w_gu2", "w_down2"]
    outs = [loss, grad_x[None]]
    for idx, flat in enumerate((sg, sd, sm, sv)):
        for n in order:
            outs.append(big_out[n][idx] if n in big_out else unpack(flat, n))
    return tuple(outs)
```

```python
import functools

import jax
import jax.numpy as jnp
from jax import lax
from jax.experimental import pallas as pl
from jax.experimental.pallas import tpu as pltpu

F32 = jnp.float32
BF16 = jnp.bfloat16

N_DEV = 8
HEAD_DIM = 64
GROUP = 4
BLOCK = 128
ROT_DIM = 16
ROPE_THETA = 500000.0
RMS_EPS = 1e-6
NEG_INF = -1e30
ATTN_SCALE = HEAD_DIM ** -0.5
CONV_K = 3
LANES = 128
MXU_COLS = 256
VMEM_BYTES_V7X = 64 * 1024 * 1024
VMEM_CAP = VMEM_BYTES_V7X - 6 * 1024 * 1024

ADAM_LR = 0.001
ADAM_B1 = 0.9
ADAM_B2 = 0.999
ADAM_EPS = 1e-08
ADAM_WD = 0.01
ADAM_STEP = 10

NN = (((1,), (0,)), ((), ()))
NT = (((1,), (1,)), ((), ()))
TN = (((0,), (0,)), ((), ()))

MESH = pl.DeviceIdType.MESH


def _nbytes(shape, dtype):
    n = 1
    for s in shape:
        if s is not None:
            n *= s
    return n * jnp.dtype(dtype).itemsize


def _params(semantics, block_bytes, temp_bytes):
    assert 2 * block_bytes + temp_bytes <= VMEM_CAP, (block_bytes, temp_bytes)
    return pltpu.CompilerParams(dimension_semantics=semantics, vmem_limit_bytes=VMEM_CAP)


def _fused(name, grid, ins, outs, dots, epilogue, *, nk=1, acc_shape=None, temp_bytes=0,
           semantics=("parallel", "parallel", "arbitrary"), deps=(), side=None):
    n_main_in, n_main_out = len(ins), len(outs)
    if side is not None:
        ins, outs = list(ins) + list(side[0]), list(outs) + list(side[1])
    n_in, n_out = len(ins), len(outs)
    n_dep = len(deps)

    def body(*refs):
        in_refs, out_refs = refs[:n_in], refs[n_in + n_dep:n_in + n_dep + n_out]
        scratch = refs[n_in + n_dep + n_out:]
        if side is not None:
            side[2](in_refs[n_main_in:], out_refs[n_main_out:])

        def products():
            if callable(dots):
                return dots(in_refs)
            total = None
            for ai, bi, contract in dots:
                a, b = in_refs[ai][...], in_refs[bi][...]
                a = a if a.dtype == BF16 else a.astype(BF16)
                b = b if b.dtype == BF16 else b.astype(BF16)
                p = lax.dot_general(a, b, contract, preferred_element_type=F32)
                total = p if total is None else total + p
            return total

        if nk == 1:
            epilogue(products() if dots else None, in_refs, out_refs)
        else:
            acc = scratch[0]
            k = pl.program_id(2)

            @pl.when(k == 0)
            def _():
                acc[...] = jnp.zeros_like(acc)

            acc[...] += products()

            @pl.when(k == nk - 1)
            def _():
                epilogue(acc[...], in_refs, out_refs)

    block_bytes = sum(_nbytes(spec.block_shape, a.dtype) for a, spec in ins)
    block_bytes += sum(_nbytes(spec.block_shape, s.dtype) for s, spec in outs)
    scratch_shapes = []
    if nk > 1:
        scratch_shapes.append(pltpu.VMEM(acc_shape, F32))
        temp_bytes += _nbytes(acc_shape, F32)
    res = pl.pallas_call(
        body, name=name, grid=grid,
        in_specs=[spec for _, spec in ins] + [pl.BlockSpec(memory_space=pl.ANY)] * n_dep,
        out_specs=[spec for _, spec in outs],
        out_shape=[s for s, _ in outs],
        scratch_shapes=scratch_shapes,
        compiler_params=_params(semantics, block_bytes, temp_bytes),
    )(*[a for a, _ in ins], *deps)
    return res


def _sds(shape, dtype):
    return jax.ShapeDtypeStruct(shape, dtype)


def _sigmoid(x):
    return jax.nn.sigmoid(x)


def _exchange(name, arrays, gather, deps=()):
    n = len(arrays)
    out_shapes = [((N_DEV,) + a.shape) if gather else a.shape for a in arrays]

    def body(*refs):
        srcs, dsts = refs[:n], refs[n + len(deps):2 * n + len(deps)]
        send_sems, recv_sems, local_sems = refs[2 * n + len(deps):]
        x, y, c = lax.axis_index("x"), lax.axis_index("y"), lax.axis_index("c")
        me = 4 * x + 2 * y + c
        copies = []
        for w in range(n):
            own = srcs[w] if gather else srcs[w].at[me]
            local = pltpu.make_async_copy(own, dsts[w].at[me], local_sems.at[w])
            local.start()
            copies.append(local)
            for k in range(1, N_DEV):
                px = (1 - x) if (k & 4) else x
                py = (1 - y) if (k & 2) else y
                pc = (1 - c) if (k & 1) else c
                peer = 4 * px + 2 * py + pc
                cp = pltpu.make_async_remote_copy(
                    src_ref=srcs[w] if gather else srcs[w].at[peer],
                    dst_ref=dsts[w].at[me],
                    send_sem=send_sems.at[w * (N_DEV - 1) + k - 1],
                    recv_sem=recv_sems.at[w * (N_DEV - 1) + k - 1],
                    device_id=(px, py, pc), device_id_type=MESH)
                cp.start()
                copies.append(cp)
        for cp in copies:
            cp.wait()

    hbm = pl.BlockSpec(memory_space=pltpu.HBM)
    return pl.pallas_call(
        body, name=name,
        in_specs=[hbm] * n + [pl.BlockSpec(memory_space=pl.ANY)] * len(deps), out_specs=[hbm] * n,
        out_shape=[_sds(s, a.dtype) for s, a in zip(out_shapes, arrays)],
        scratch_shapes=[pltpu.SemaphoreType.DMA((n * (N_DEV - 1),)),
                        pltpu.SemaphoreType.DMA((n * (N_DEV - 1),)),
                        pltpu.SemaphoreType.DMA((n,))],
    )(*arrays, *deps)


_HBM = pl.BlockSpec(memory_space=pltpu.HBM)
_SEM = pl.BlockSpec(memory_space=pltpu.SEMAPHORE)
_ANY = pl.BlockSpec(memory_space=pl.ANY)
_EFFECT = pltpu.SideEffectType.DATAFLOW_SIDE_EFFECTING
N_TARGETS = 4


def _mesh_pos():
    return lax.axis_index("x"), lax.axis_index("y"), lax.axis_index("c")


def _chip_peers(x, y, c):
    return [(1 - x, y, c), (x, 1 - y, c), (1 - x, 1 - y, c)]


def _dev_index(pos):
    return 4 * pos[0] + 2 * pos[1] + pos[2]


def _hbm_like(a):
    return pltpu.HBM(a.shape, a.dtype)


def _place_shard(name, w, out_dtype, me, tr, deps=()):
    r, c = w.shape
    n_dep = len(deps)

    def body(me_ref, w_ref, *rest):
        rest[n_dep][...] = w_ref[...].astype(out_dtype)

    grid_spec = pltpu.PrefetchScalarGridSpec(
        num_scalar_prefetch=1, grid=(r // tr,),
        in_specs=[pl.BlockSpec((tr, c), lambda i, me_ref: (i, 0))] + [_ANY] * n_dep,
        out_specs=pl.BlockSpec((None, tr, c), lambda i, me_ref: (me_ref[0], i, 0)))
    return pl.pallas_call(
        body, name=name, grid_spec=grid_spec, out_shape=_sds((N_DEV, r, c), out_dtype),
        compiler_params=_params(("parallel",), tr * c * 6, tr * c * 4),
    )(me, w, *deps)


def _gather_start(name, lands):
    n = len(lands)

    def body(*refs):
        bufs = refs[:n]
        send, recv = refs[n], refs[n + 1]
        token = refs[-1]
        x, y, c = _mesh_pos()
        me = _dev_index((x, y, c))
        targets = [(x, y, 1 - c)] + _chip_peers(x, y, c)
        for w in range(n):
            for k, to in enumerate(targets):
                pltpu.make_async_remote_copy(
                    src_ref=bufs[w].at[me], dst_ref=bufs[w].at[me],
                    send_sem=send.at[N_TARGETS * w + k], recv_sem=recv.at[N_TARGETS * w + k],
                    device_id=to, device_id_type=MESH).start()
        token[...] = jnp.zeros_like(token)

    sems = pltpu.SemaphoreType.DMA((N_TARGETS * n,))
    outs = pl.pallas_call(
        body, name=name,
        in_specs=[_HBM] * n, out_specs=[_SEM, _SEM] + [_HBM] * n + [_token_spec()],
        out_shape=[sems, sems] + [_hbm_like(a) for a in lands] + [_sds((8, LANES), F32)],
        input_output_aliases={i: 2 + i for i in range(n)},
        compiler_params=pltpu.CompilerParams(has_side_effects=_EFFECT),
    )(*lands)
    return outs[0], outs[1], list(outs[2:2 + n]), outs[-1]


def _gather_wait(name, positions, send, recv, lands, after):
    m = len(positions)

    def body(*refs):
        bufs = refs[:m]
        send_sems, recv_sems = refs[m], refs[m + 1]
        x, y, c = _mesh_pos()
        me = _dev_index((x, y, c))
        sources = [(x, y, 1 - c)] + _chip_peers(x, y, c)
        for j, w in enumerate(positions):
            for k, frm in enumerate(sources):
                cp = pltpu.make_async_remote_copy(
                    src_ref=bufs[j].at[me], dst_ref=bufs[j].at[_dev_index(frm)],
                    send_sem=send_sems.at[N_TARGETS * w + k], recv_sem=recv_sems.at[N_TARGETS * w + k],
                    device_id=frm, device_id_type=MESH)
                cp.wait_send()
                cp.wait_recv()

    outs = pl.pallas_call(
        body, name=name,
        in_specs=[_HBM] * m + [_SEM, _SEM, _ANY], out_specs=[_HBM] * m,
        out_shape=[_hbm_like(a) for a in lands],
        input_output_aliases={i: i for i in range(m)},
        compiler_params=pltpu.CompilerParams(has_side_effects=_EFFECT),
    )(*lands, send, recv, after)
    return list(outs)


def _forward_to_sibling(name, lands):
    m = len(lands)

    def body(*refs):
        bufs = refs[m:2 * m]
        send_sems, recv_sems = refs[2 * m], refs[2 * m + 1]
        x, y, c = _mesh_pos()
        copies = []
        for j in range(m):
            for k, chip in enumerate(_chip_peers(x, y, c)):
                block = bufs[j].at[_dev_index(chip)]
                cp = pltpu.make_async_remote_copy(
                    src_ref=block, dst_ref=block,
                    send_sem=send_sems.at[3 * j + k], recv_sem=recv_sems.at[3 * j + k],
                    device_id=(x, y, 1 - c), device_id_type=MESH)
                cp.start()
                copies.append(cp)
        for cp in copies:
            cp.wait()

    outs = pl.pallas_call(
        body, name=name,
        in_specs=[_HBM] * m, out_specs=[_HBM] * m,
        out_shape=[_sds(a.shape, a.dtype) for a in lands],
        input_output_aliases={i: i for i in range(m)},
        scratch_shapes=[pltpu.SemaphoreType.DMA((3 * m,)), pltpu.SemaphoreType.DMA((3 * m,))],
    )(*lands)
    return list(outs)


def _token_spec():
    return pl.BlockSpec(memory_space=pltpu.VMEM)


def _pair_start(name, stacks, lands, deps=()):
    n = len(stacks)
    n_dep = len(deps)

    def body(*refs):
        srcs, dsts = refs[:n], refs[n:2 * n]
        send, recv = refs[2 * n + n_dep], refs[2 * n + n_dep + 1]
        token = refs[-1]
        x, y, c = _mesh_pos()
        for w in range(n):
            for chip in range(4):
                pltpu.make_async_remote_copy(
                    src_ref=srcs[w].at[chip, 1 - c], dst_ref=dsts[w].at[chip],
                    send_sem=send.at[4 * w + chip], recv_sem=recv.at[4 * w + chip],
                    device_id=(x, y, 1 - c), device_id_type=MESH).start()
        token[...] = jnp.zeros_like(token)

    sems = pltpu.SemaphoreType.DMA((4 * n,))
    outs = pl.pallas_call(
        body, name=name,
        in_specs=[_HBM] * (2 * n) + [_ANY] * n_dep, out_specs=[_SEM, _SEM] + [_HBM] * (2 * n) + [_token_spec()],
        out_shape=[sems, sems] + [_hbm_like(a) for a in stacks] + [_hbm_like(a) for a in lands] + [_sds((8, LANES), F32)],
        input_output_aliases={i: 2 + i for i in range(2 * n)},
        compiler_params=pltpu.CompilerParams(has_side_effects=_EFFECT),
    )(*stacks, *lands, *deps)
    return outs[0], outs[1], list(outs[2:2 + n]), list(outs[2 + n:2 + 2 * n]), outs[-1]


def _pair_wait(name, send, recv, stacks, lands, after):
    n = len(stacks)

    def body(*refs):
        srcs, dsts = refs[:n], refs[n:2 * n]
        send_sems, recv_sems = refs[2 * n], refs[2 * n + 1]
        x, y, c = _mesh_pos()
        for w in range(n):
            for chip in range(4):
                cp = pltpu.make_async_remote_copy(
                    src_ref=srcs[w].at[chip, 1 - c], dst_ref=dsts[w].at[chip],
                    send_sem=send_sems.at[4 * w + chip], recv_sem=recv_sems.at[4 * w + chip],
                    device_id=(x, y, 1 - c), device_id_type=MESH)
                cp.wait_send()
                cp.wait_recv()

    outs = pl.pallas_call(
        body, name=name,
        in_specs=[_HBM] * (2 * n) + [_SEM, _SEM, _ANY], out_specs=[_HBM] * (2 * n),
        out_shape=[_hbm_like(a) for a in stacks] + [_hbm_like(a) for a in lands],
        input_output_aliases={i: i for i in range(2 * n)},
        compiler_params=pltpu.CompilerParams(has_side_effects=_EFFECT),
    )(*stacks, *lands, send, recv, after)
    return list(outs[:n]), list(outs[n:])


def _pair_add(name, stack, land, place, tr):
    _, _, r, c = stack.shape

    def body(place_ref, a_ref, b_ref, sums_ref, slots_ref):
        total = (a_ref[...].astype(F32) + b_ref[...].astype(F32)).astype(BF16)
        sums_ref[...] = total

        @pl.when(pl.program_id(1) == place_ref[1])
        def _():
            slots_ref[...] = total

    grid_spec = pltpu.PrefetchScalarGridSpec(
        num_scalar_prefetch=1, grid=(r // tr, 4),
        in_specs=[pl.BlockSpec((None, None, tr, c), lambda i, k, place_ref: (k, place_ref[0], i, 0)),
                  pl.BlockSpec((None, tr, c), lambda i, k, place_ref: (k, i, 0))],
        out_specs=[pl.BlockSpec((None, tr, c), lambda i, k, place_ref: (k, i, 0)),
                   pl.BlockSpec((None, tr, c), lambda i, k, place_ref: (place_ref[1], i, 0))])
    return pl.pallas_call(
        body, name=name, grid_spec=grid_spec, out_shape=[_sds((4, r, c), BF16)] * 2,
        compiler_params=_params(("parallel", "arbitrary"), 4 * tr * c * 2, 3 * tr * c * 4),
    )(place, stack, land)


def _chip_start(name, parts, lands):
    n = len(parts)

    def body(*refs):
        srcs, dsts = refs[:n], refs[n:2 * n]
        send, recv = refs[2 * n], refs[2 * n + 1]
        token = refs[-1]
        x, y, c = _mesh_pos()
        for w in range(n):
            for k, to in enumerate(_chip_peers(x, y, c)):
                pltpu.make_async_remote_copy(
                    src_ref=srcs[w].at[2 * to[0] + to[1]], dst_ref=dsts[w].at[2 * x + y],
                    send_sem=send.at[3 * w + k], recv_sem=recv.at[3 * w + k],
                    device_id=to, device_id_type=MESH).start()
        token[...] = jnp.zeros_like(token)

    sems = pltpu.SemaphoreType.DMA((3 * n,))
    outs = pl.pallas_call(
        body, name=name,
        in_specs=[_HBM] * (2 * n), out_specs=[_SEM, _SEM] + [_HBM] * (2 * n) + [_token_spec()],
        out_shape=[sems, sems] + [_hbm_like(a) for a in parts] + [_hbm_like(a) for a in lands] + [_sds((8, LANES), F32)],
        input_output_aliases={i: 2 + i for i in range(2 * n)},
        compiler_params=pltpu.CompilerParams(has_side_effects=_EFFECT),
    )(*parts, *lands)
    return outs[0], outs[1], list(outs[2:2 + n]), list(outs[2 + n:2 + 2 * n]), outs[-1]


def _chip_wait(name, send, recv, parts, lands, after):
    n = len(parts)

    def body(*refs):
        srcs, dsts = refs[:n], refs[n:2 * n]
        send_sems, recv_sems = refs[2 * n], refs[2 * n + 1]
        x, y, c = _mesh_pos()
        for w in range(n):
            for k, frm in enumerate(_chip_peers(x, y, c)):
                chip = 2 * frm[0] + frm[1]
                cp = pltpu.make_async_remote_copy(
                    src_ref=srcs[w].at[chip], dst_ref=dsts[w].at[chip],
                    send_sem=send_sems.at[3 * w + k], recv_sem=recv_sems.at[3 * w + k],
                    device_id=frm, device_id_type=MESH)
                cp.wait_send()
                cp.wait_recv()

    outs = pl.pallas_call(
        body, name=name,
        in_specs=[_HBM] * (2 * n) + [_SEM, _SEM, _ANY], out_specs=[_HBM] * (2 * n),
        out_shape=[_hbm_like(a) for a in parts] + [_hbm_like(a) for a in lands],
        input_output_aliases={i: i for i in range(2 * n)},
        compiler_params=pltpu.CompilerParams(has_side_effects=_EFFECT),
    )(*parts, *lands, send, recv, after)
    return list(outs[:n]), list(outs[n:])


def _row_tile(t):
    return min(t, 256)


def _rms_fwd(name, x, g):
    t, d = x.shape
    tm = _row_tile(t)

    def epilogue(_, ins, outs):
        xv = ins[0][...]
        r = lax.rsqrt(jnp.mean(xv * xv, axis=-1, keepdims=True) + RMS_EPS)
        outs[0][...] = (xv * r * ins[1][...]).astype(BF16)

    row = pl.BlockSpec((tm, d), lambda i, j, k: (i, 0))
    vec = pl.BlockSpec((1, d), lambda i, j, k: (0, 0))
    return _fused(name, (t // tm, 1, 1), [(x, row), (g, vec)], [(_sds((t, d), BF16), row)], [], epilogue,
                  temp_bytes=4 * tm * d * 4)[0]


def _rms_bwd(name, x, g, dh, resid, deps=()):
    t, d = x.shape
    tm = _row_tile(t)

    def epilogue(_, ins, outs):
        xv, gv, dhv = ins[0][...], ins[1][...], ins[2][...]
        r = lax.rsqrt(jnp.mean(xv * xv, axis=-1, keepdims=True) + RMS_EPS)
        xh = xv * r
        u = dhv * gv
        dot = jnp.mean(u * xh, axis=-1, keepdims=True)
        outs[0][...] = ins[3][...] + r * (u - xh * dot)

        @pl.when(pl.program_id(0) == 0)
        def _():
            outs[1][...] = jnp.zeros_like(outs[1])

        outs[1][0:1, :] += jnp.sum(dhv * xh, axis=0, keepdims=True)

    row = pl.BlockSpec((tm, d), lambda i, j, k: (i, 0))
    vec = pl.BlockSpec((1, d), lambda i, j, k: (0, 0))
    acc = pl.BlockSpec((8, d), lambda i, j, k: (0, 0))
    return _fused(name, (t // tm, 1, 1), [(x, row), (g, vec), (dh, row), (resid, row)],
                  [(_sds((t, d), F32), row), (_sds((8, d), F32), acc)], [], epilogue,
                  temp_bytes=6 * tm * d * 4, semantics=("arbitrary", "arbitrary", "arbitrary"), deps=deps)


def _loss_dy(y, target):
    t, d = y.shape
    tm = _row_tile(t)

    def epilogue(_, ins, outs):
        e = ins[0][...] - ins[1][...]
        outs[0][...] = e * (1.0 / d)

        @pl.when(pl.program_id(0) == 0)
        def _():
            outs[1][...] = jnp.zeros_like(outs[1])

        part = jnp.sum(jnp.sum(e * e, axis=1, keepdims=True), axis=0, keepdims=True)
        outs[1][...] += jnp.broadcast_to(part, outs[1].shape)

    row = pl.BlockSpec((tm, d), lambda i, j, k: (i, 0))
    acc = pl.BlockSpec((8, LANES), lambda i, j, k: (0, 0))
    return _fused("loss_dy", (t // tm, 1, 1), [(y, row), (target, row)],
                  [(_sds((t, d), F32), row), (_sds((8, LANES), F32), acc)], [], epilogue,
                  temp_bytes=3 * tm * d * 4, semantics=("arbitrary", "arbitrary", "arbitrary"))


def _ffn_up(name, h, wgu):
    t, d = h.shape
    nb = wgu.shape[2]
    f = 4 * nb
    tm = min(t, 512)

    def body(h_ref, wg_ref, wu_ref, gu_ref, a_ref):
        hv = h_ref[...]
        for c0 in range(0, nb, MXU_COLS):
            cs = slice(c0, min(c0 + MXU_COLS, nb))
            g = jnp.dot(hv, wg_ref[:, cs], preferred_element_type=F32)
            u = jnp.dot(hv, wu_ref[:, cs], preferred_element_type=F32)
            gu_ref[0, :, cs] = g.astype(BF16)
            gu_ref[1, :, cs] = u.astype(BF16)
            a_ref[:, cs] = (g * _sigmoid(g) * u).astype(BF16)

    blocks = tm * d * 2 + 2 * d * nb * 2 + 3 * tm * nb * 2
    return pl.pallas_call(
        body, name=name, grid=(4, t // tm),
        in_specs=[pl.BlockSpec((tm, d), lambda j, i: (i, 0)),
                  pl.BlockSpec((None, d, nb), lambda j, i: (j, 0, 0)),
                  pl.BlockSpec((None, d, nb), lambda j, i: (j + 4, 0, 0))],
        out_specs=[pl.BlockSpec((2, tm, nb), lambda j, i: (0, i, j)),
                   pl.BlockSpec((tm, nb), lambda j, i: (i, j))],
        out_shape=[_sds((2, t, f), BF16), _sds((t, f), BF16)],
        compiler_params=_params(("parallel", "parallel"), blocks, 8 * tm * MXU_COLS * 4),
    )(h, wgu, wgu)


def _ffn_down(name, a, wd, x):
    t, f = a.shape
    d = wd.shape[1]
    tm = min(t, 512)
    tn = min(d, 1024)

    def epilogue(acc, ins, outs):
        outs[0][...] = ins[2][...] + 0.5 * acc

    blk = pl.BlockSpec((tm, tn), lambda j, i, k: (i, j))
    return _fused(name, (d // tn, t // tm, 1),
                  [(a, pl.BlockSpec((tm, f), lambda j, i, k: (i, 0))),
                   (wd, pl.BlockSpec((f, tn), lambda j, i, k: (0, j))),
                   (x, blk)],
                  [(_sds((t, d), F32), blk)],
                  [(0, 1, NN)], epilogue, temp_bytes=2 * tm * tn * 4)[0]


def _ffn_bwd_act(name, dy, wd, gu, deps=()):
    t, d = dy.shape
    f = wd.shape[0]
    nb = f // 4
    tm = min(t, 512)

    def body(dy_ref, wd_ref, gu_ref, *rest):
        dgu_ref, a_ref = rest[-2], rest[-1]
        dyv = dy_ref[...].astype(BF16)
        for c0 in range(0, nb, MXU_COLS):
            cs = slice(c0, min(c0 + MXU_COLS, nb))
            da = 0.5 * lax.dot_general(dyv, wd_ref[cs, :], NT, preferred_element_type=F32)
            g = gu_ref[0, :, cs].astype(F32)
            u = gu_ref[1, :, cs].astype(F32)
            s = _sigmoid(g)
            silu = g * s
            dgu_ref[0, :, cs] = (da * u * (s * (1.0 + g * (1.0 - s)))).astype(BF16)
            dgu_ref[1, :, cs] = (da * silu).astype(BF16)
            a_ref[:, cs] = (silu * u).astype(BF16)

    blocks = tm * d * 4 + nb * d * 2 + 5 * tm * nb * 2
    return pl.pallas_call(
        body, name=name, grid=(4, t // tm),
        in_specs=[pl.BlockSpec((tm, d), lambda j, i: (i, 0)),
                  pl.BlockSpec((nb, d), lambda j, i: (j, 0)),
                  pl.BlockSpec((2, tm, nb), lambda j, i: (0, i, j))] + [_ANY] * len(deps),
        out_specs=[pl.BlockSpec((2, tm, nb), lambda j, i: (0, i, j)), pl.BlockSpec((tm, nb), lambda j, i: (i, j))],
        out_shape=[_sds((2, t, f), BF16), _sds((t, f), BF16)],
        compiler_params=_params(("parallel", "parallel"), blocks, tm * d * 2 + 8 * tm * MXU_COLS * 4),
    )(dy, wd, gu, *deps)


def _ffn_bwd_dwd(name, a, dy, deps=(), side=None):
    t, f = a.shape
    d = dy.shape[1]
    tm = f // 4
    tn = min(d, 512)

    def epilogue(acc, ins, outs):
        outs[0][...] = (0.5 * acc).astype(BF16)

    return _fused(name, (4, d // tn, 1),
                  [(a, pl.BlockSpec((t, tm), lambda i, j, k: (0, i))),
                   (dy, pl.BlockSpec((t, tn), lambda i, j, k: (0, j)))],
                  [(_sds((f, d), BF16), pl.BlockSpec((tm, tn), lambda i, j, k: (i, j)))],
                  [(0, 1, TN)], epilogue, temp_bytes=t * tn * 2 + 2 * tm * tn * 4, deps=deps, side=side)


def _ffn_bwd_dh(name, dgu, wgu, deps=(), side=None):
    _, t, f = dgu.shape
    d, nb = wgu.shape[1], wgu.shape[2]
    tm = min(t, 512)

    def products(ins):
        return (lax.dot_general(ins[0][:, 0:nb], ins[1][0], NT, preferred_element_type=F32)
                + lax.dot_general(ins[0][:, nb:2 * nb], ins[1][1], NT, preferred_element_type=F32))

    def epilogue(acc, ins, outs):
        outs[0][...] = acc

    return _fused(name, (t // tm, 1, 4),
                  [(dgu, pl.BlockSpec((None, tm, 2 * nb), lambda i, j, k: (k // 2, i, k % 2))),
                   (wgu, pl.BlockSpec((2, d, nb), lambda i, j, k: (k, 0, 0)))],
                  [(_sds((t, d), F32), pl.BlockSpec((tm, d), lambda i, j, k: (i, 0)))],
                  products, epilogue, nk=4, acc_shape=(tm, d), temp_bytes=tm * d * 4, deps=deps, side=side)


def _ffn_bwd_dwgu(name, h, dgu, deps=(), side=None, rows=None):
    t, d = h.shape
    nb = dgu.shape[2] // 4
    tm = min(d, 512)
    row0, nrows = rows if rows is not None else (0, d)
    j0 = row0 // tm

    def epilogue(acc, ins, outs):
        outs[0][...] = acc.astype(BF16)

    return _fused(name, (N_DEV, nrows // tm, 1),
                  [(h, pl.BlockSpec((t, tm), lambda i, j, k: (0, j0 + j))),
                   (dgu, pl.BlockSpec((None, t, nb), lambda i, j, k: (i // 4, 0, i % 4)))],
                  [(_sds((N_DEV, nrows, nb), BF16), pl.BlockSpec((None, tm, nb), lambda i, j, k: (i, j, 0)))],
                  [(0, 1, TN)], epilogue, temp_bytes=2 * tm * nb * 4, deps=deps, side=side)


def _proj(h, w_in):
    t, d = h.shape
    nb = w_in.shape[3]
    tm = min(t, 512)

    def body(h_ref, w_ref, o_ref):
        hv = h_ref[...]
        o_ref[:, 0:nb] = jnp.dot(hv, w_ref[0], preferred_element_type=F32).astype(BF16)
        o_ref[:, nb:2 * nb] = jnp.dot(hv, w_ref[1], preferred_element_type=F32).astype(BF16)

    blocks = tm * d * 2 + 2 * d * nb * 2 + tm * 2 * nb * 4
    return pl.pallas_call(
        body, name="mix_proj", grid=(4, t // tm),
        in_specs=[pl.BlockSpec((tm, d), lambda j, i: (i, 0)),
                  pl.BlockSpec((None, 2, d, nb), lambda j, i: (j, 0, 0, 0))],
        out_specs=pl.BlockSpec((tm, 2 * nb), lambda j, i: (i, j)),
        out_shape=_sds((t, N_DEV * nb), BF16),
        compiler_params=_params(("parallel", "parallel"), blocks, 2 * tm * nb * 4),
    )(h, w_in)


def _shift_rows(u, k):
    t = u.shape[0]
    rolled = pltpu.roll(u, k % t, axis=0)
    row = lax.broadcasted_iota(jnp.int32, u.shape, 0)
    keep = (row >= k) if k > 0 else (row < t + k)
    return jnp.where(keep, rolled, 0.0)


def _conv_fwd(proj, conv_w):
    t = proj.shape[0]
    cw = conv_w.shape[1]
    tc = min(cw, 256)
    nc = cw // tc

    def epilogue(_, ins, outs):
        u = ins[2][...].astype(F32) * ins[0][...].astype(F32)
        w = ins[3][...]
        y = u * w[2:3, :] + _shift_rows(u, 1) * w[1:2, :] + _shift_rows(u, 2) * w[0:1, :]
        outs[0][...] = (ins[1][...].astype(F32) * y).astype(BF16)

    def col(seg):
        return pl.BlockSpec((t, tc), lambda i, j, k: (0, seg * nc + i))

    return _fused("conv_fwd", (nc, 1, 1),
                  [(proj, col(0)), (proj, col(1)), (proj, col(2)),
                   (conv_w, pl.BlockSpec((8, tc), lambda i, j, k: (0, i)))],
                  [(_sds((t, cw), BF16), pl.BlockSpec((t, tc), lambda i, j, k: (0, i)))],
                  [], epilogue, temp_bytes=6 * t * tc * 4)[0]


def _conv_bwd(proj, conv_w, dca, deps=()):
    t = proj.shape[0]
    cw = conv_w.shape[1]
    tc = min(cw, 256)
    nc = cw // tc

    def epilogue(_, ins, outs):
        xc, bg, cg = ins[0][...].astype(F32), ins[1][...].astype(F32), ins[2][...].astype(F32)
        w, dc = ins[3][...], ins[4][...]
        u = cg * xc
        u1, u2 = _shift_rows(u, 1), _shift_rows(u, 2)
        y = u * w[2:3, :] + u1 * w[1:2, :] + u2 * w[0:1, :]
        dconv = dc * bg
        du = dconv * w[2:3, :] + _shift_rows(dconv, -1) * w[1:2, :] + _shift_rows(dconv, -2) * w[0:1, :]
        outs[0][0] = (du * cg).astype(BF16)
        outs[0][1] = (dc * y).astype(BF16)
        outs[0][2] = (du * xc).astype(BF16)
        outs[1][...] = jnp.zeros_like(outs[1])
        outs[1][0:1, :] = jnp.sum(dconv * u2, axis=0, keepdims=True)
        outs[1][1:2, :] = jnp.sum(dconv * u1, axis=0, keepdims=True)
        outs[1][2:3, :] = jnp.sum(dconv * u, axis=0, keepdims=True)

    def col(seg):
        return pl.BlockSpec((t, tc), lambda i, j, k: (0, seg * nc + i))

    own = pl.BlockSpec((t, tc), lambda i, j, k: (0, i))
    wspec = pl.BlockSpec((8, tc), lambda i, j, k: (0, i))
    return _fused("conv_bwd", (nc, 1, 1),
                  [(proj, col(0)), (proj, col(1)), (proj, col(2)), (conv_w, wspec), (dca, own)],
                  [(_sds((3, t, cw), BF16), pl.BlockSpec((3, t, tc), lambda i, j, k: (0, 0, i))),
                   (_sds((8, cw), F32), wspec)],
                  [], epilogue, temp_bytes=10 * t * tc * 4, deps=deps)


def _split3(x):
    hi = x.astype(BF16)
    r1 = x - hi.astype(F32)
    mid = r1.astype(BF16)
    lo = (r1 - mid.astype(F32)).astype(BF16)
    return hi, mid, lo


def _head_selector(width):
    r = lax.broadcasted_iota(jnp.int32, (width, LANES), 0)
    c = lax.broadcasted_iota(jnp.int32, (width, LANES), 1)
    return (lax.shift_right_logical(r, 6) == c).astype(BF16)


def _head_sum(x, sel):
    return sum(jnp.dot(p, sel, preferred_element_type=F32) for p in _split3(x))


def _head_bcast(r, sel):
    return sum(lax.dot_general(p, sel, NT, preferred_element_type=F32) for p in _split3(r))


def _rope(x, c, sa, sb):
    n = x.shape[1]
    return x * c + pltpu.roll(x, n - ROT_DIM // 2, axis=1) * sa + pltpu.roll(x, ROT_DIM // 2, axis=1) * sb


def _rope_t(d, c, sa, sb):
    n = d.shape[1]
    return d * c + pltpu.roll(d * sa, ROT_DIM // 2, axis=1) + pltpu.roll(d * sb, n - ROT_DIM // 2, axis=1)


def _tile_lanes(tab, width):
    return tab if width == tab.shape[1] else jnp.tile(tab, (1, width // tab.shape[1]))


def _qk_prep(proj, gq, gk, rope_tabs, cw, kw):
    t = proj.shape[0]
    tm = _row_tile(t)

    def epilogue(_, ins, outs):
        c, sa, sb = ins[5][...], ins[6][...], ins[7][...]
        for src, gain, dst, width in ((0, 3, 0, cw), (1, 4, 1, kw)):
            xv = ins[src][...].astype(F32)
            sel = _head_selector(width)
            r = lax.rsqrt(_head_sum(xv * xv, sel) * (1.0 / HEAD_DIM) + RMS_EPS)
            xn = xv * _head_bcast(r, sel) * ins[gain][...]
            outs[dst][...] = _rope(xn, _tile_lanes(c, width), _tile_lanes(sa, width), _tile_lanes(sb, width)).astype(BF16)
        outs[2][...] = ins[2][...].astype(BF16)

    kblk = cw // kw
    tab = pl.BlockSpec((tm, LANES), lambda i, j, k: (i, 0))
    kspec = pl.BlockSpec((tm, kw), lambda i, j, k: (i, 0))
    return _fused("qk_prep", (t // tm, 1, 1),
                  [(proj, pl.BlockSpec((tm, cw), lambda i, j, k: (i, 3))),
                   (proj, pl.BlockSpec((tm, kw), lambda i, j, k: (i, 4 * kblk))),
                   (proj, pl.BlockSpec((tm, kw), lambda i, j, k: (i, 4 * kblk + 1))),
                   (gq, pl.BlockSpec((1, cw), lambda i, j, k: (0, 0))),
                   (gk, pl.BlockSpec((1, kw), lambda i, j, k: (0, 0))),
                   (rope_tabs[0], tab), (rope_tabs[1], tab), (rope_tabs[2], tab)],
                  [(_sds((t, cw), BF16), pl.BlockSpec((tm, cw), lambda i, j, k: (i, 0))),
                   (_sds((t, kw), BF16), kspec), (_sds((t, kw), BF16), kspec)],
                  [], epilogue, temp_bytes=12 * tm * cw * 4)


def _qk_prep_bwd(proj, gq, gk, rope_tabs, dq, dkc, dkp, dvc, dvp, cw, kw):
    t = proj.shape[0]
    tm = BLOCK
    nblk = t // tm

    def epilogue(_, ins, outs):
        c, sa, sb = ins[5][...], ins[6][...], ins[7][...]
        has_next = (pl.program_id(0) < nblk - 1).astype(F32)
        dk = ins[9][...] + has_next * ins[10][...]
        dv = ins[11][...] + has_next * ins[12][...]
        pieces = []
        for src, gain, dval, dst, width in ((0, 3, ins[8][...], 1, cw), (1, 4, dk, 2, kw)):
            xv, gv = ins[src][...].astype(F32), ins[gain][...]
            sel = _head_selector(width)
            r = _head_bcast(lax.rsqrt(_head_sum(xv * xv, sel) * (1.0 / HEAD_DIM) + RMS_EPS), sel)
            xh = xv * r
            dxn = _rope_t(dval, _tile_lanes(c, width), _tile_lanes(sa, width), _tile_lanes(sb, width))
            u = dxn * gv
            dot = _head_bcast(_head_sum(u * xh, sel), sel) * (1.0 / HEAD_DIM)
            pieces.append((r * (u - xh * dot)).astype(BF16))
            ri = lax.broadcasted_iota(jnp.int32, (width, LANES), 0)
            ci = lax.broadcasted_iota(jnp.int32, (width, LANES), 1)
            fold = (lax.bitwise_and(ri, HEAD_DIM - 1) == ci).astype(BF16)
            colsum = jnp.broadcast_to(jnp.sum(dxn * xh, axis=0, keepdims=True), (8, width))
            part = sum(jnp.dot(p, fold, preferred_element_type=F32) for p in _split3(colsum))

            @pl.when(pl.program_id(0) == 0)
            def _():
                outs[dst][...] = jnp.zeros_like(outs[dst])

            outs[dst][0:1, :] += part[0:1, :]
        outs[0][:, 0:cw] = pieces[0]
        outs[0][:, cw:cw + kw] = pieces[1]
        outs[0][:, cw + kw:cw + 2 * kw] = dv.astype(BF16)

    kblk = cw // kw
    tab = pl.BlockSpec((tm, LANES), lambda i, j, k: (i, 0))
    kcur = pl.BlockSpec((tm, kw), lambda i, j, k: (i, 0))
    knext = pl.BlockSpec((tm, kw), lambda i, j, k: (jnp.minimum(i + 1, nblk - 1), 0))
    acc = pl.BlockSpec((8, LANES), lambda i, j, k: (0, 0))
    return _fused("qk_prep_bwd", (nblk, 1, 1),
                  [(proj, pl.BlockSpec((tm, cw), lambda i, j, k: (i, 3))),
                   (proj, pl.BlockSpec((tm, kw), lambda i, j, k: (i, 4 * kblk))),
                   (proj, pl.BlockSpec((tm, kw), lambda i, j, k: (i, 4 * kblk + 1))),
                   (gq, pl.BlockSpec((1, cw), lambda i, j, k: (0, 0))),
                   (gk, pl.BlockSpec((1, kw), lambda i, j, k: (0, 0))),
                   (rope_tabs[0], tab), (rope_tabs[1], tab), (rope_tabs[2], tab),
                   (dq, pl.BlockSpec((tm, cw), lambda i, j, k: (i, 0))),
                   (dkc, kcur), (dkp, knext), (dvc, kcur), (dvp, knext)],
                  [(_sds((t, cw + 2 * kw), BF16), pl.BlockSpec((tm, cw + 2 * kw), lambda i, j, k: (i, 0))),
                   (_sds((8, LANES), F32), acc), (_sds((8, LANES), F32), acc)],
                  [], epilogue, temp_bytes=16 * tm * cw * 4, semantics=("arbitrary", "arbitrary", "arbitrary"))


def _attn_mask(n):
    key = lax.broadcasted_iota(jnp.int32, (2 * BLOCK, GROUP * BLOCK), 0)
    qry = lax.bitwise_and(lax.broadcasted_iota(jnp.int32, (2 * BLOCK, GROUP * BLOCK), 1), BLOCK - 1)
    return (key > qry) & (key <= qry + BLOCK) & ((key >= BLOCK) | (n > 0))


def _stack_heads(x, h):
    return jnp.concatenate([x[:, (h * GROUP + g) * HEAD_DIM:(h * GROUP + g + 1) * HEAD_DIM] for g in range(GROUP)], axis=0)


def _softmax_with_sink(q4, k2, sink_ref, h, valid):
    sink = jnp.concatenate([sink_ref[h * GROUP + g:h * GROUP + g + 1, :] for g in range(GROUP)], axis=1)
    s = lax.dot_general(k2, q4, NT, preferred_element_type=F32) * ATTN_SCALE
    s = jnp.where(valid, s, NEG_INF)
    m = jnp.maximum(jnp.max(s, axis=0, keepdims=True), sink)
    p = jnp.exp(s - m)
    es = jnp.exp(sink - m)
    inv = 1.0 / (jnp.sum(p, axis=0, keepdims=True) + es)
    return p * inv, es * inv


def _attn_fwd(qn, kn, vb, sink_rows):
    t, cw = qn.shape
    kw = kn.shape[1]
    nkv = kw // HEAD_DIM

    def body(q_ref, kp_ref, kc_ref, vp_ref, vc_ref, sink_ref, o_ref):
        valid = _attn_mask(pl.program_id(0))
        qv = q_ref[...]
        kp, kc, vp, vc = kp_ref[...], kc_ref[...], vp_ref[...], vc_ref[...]
        outs = []
        for h in range(nkv):
            hs = slice(h * HEAD_DIM, (h + 1) * HEAD_DIM)
            k2 = jnp.concatenate([kp[:, hs], kc[:, hs]], axis=0)
            v2 = jnp.concatenate([vp[:, hs], vc[:, hs]], axis=0)
            pn, _ = _softmax_with_sink(_stack_heads(qv, h), k2, sink_ref, h, valid)
            o4 = lax.dot_general(pn.astype(BF16), v2, TN, preferred_element_type=F32)
            outs += [o4[g * BLOCK:(g + 1) * BLOCK] for g in range(GROUP)]
        o_ref[...] = jnp.concatenate(outs, axis=-1).astype(BF16)

    cur = lambda n: (n, 0)
    prev = lambda n: (jnp.maximum(n - 1, 0), 0)
    return pl.pallas_call(
        body, name="attn_fwd", grid=(t // BLOCK,),
        in_specs=[pl.BlockSpec((BLOCK, cw), cur),
                  pl.BlockSpec((BLOCK, kw), prev), pl.BlockSpec((BLOCK, kw), cur),
                  pl.BlockSpec((BLOCK, kw), prev), pl.BlockSpec((BLOCK, kw), cur),
                  pl.BlockSpec(sink_rows.shape, lambda n: (0, 0))],
        out_specs=pl.BlockSpec((BLOCK, cw), cur),
        out_shape=_sds((t, cw), BF16),
        compiler_params=_params(("parallel",), BLOCK * (cw + 4 * kw) * 2 + BLOCK * cw * 2, 8 << 20),
    )(qn, kn, kn, vb, vb, sink_rows)


def _attn_bwd(qn, kn, vb, sink_rows, do):
    t, cw = qn.shape
    kw = kn.shape[1]
    nkv = kw // HEAD_DIM
    nq = nkv * GROUP

    def body(q_ref, kp_ref, kc_ref, vp_ref, vc_ref, sink_ref, do_ref,
             dq_ref, dkc_ref, dkp_ref, dvc_ref, dvp_ref, dsink_ref):
        n = pl.program_id(0)
        valid = _attn_mask(n)
        qv, dov = q_ref[...], do_ref[...]
        kp, kc, vp, vc = kp_ref[...], kc_ref[...], vp_ref[...], vc_ref[...]
        dqs, dks, dvs, dsinks = [], [], [], []
        for h in range(nkv):
            hs = slice(h * HEAD_DIM, (h + 1) * HEAD_DIM)
            k2 = jnp.concatenate([kp[:, hs], kc[:, hs]], axis=0)
            v2 = jnp.concatenate([vp[:, hs], vc[:, hs]], axis=0)
            q4 = _stack_heads(qv, h)
            dob = _stack_heads(dov, h).astype(BF16)
            pn, psink = _softmax_with_sink(q4, k2, sink_ref, h, valid)
            dpn = lax.dot_general(v2, dob, NT, preferred_element_type=F32)
            dvs.append(jnp.dot(pn.astype(BF16), dob, preferred_element_type=F32))
            delta = jnp.sum(pn * dpn, axis=0, keepdims=True)
            ds = (pn * (dpn - delta) * ATTN_SCALE).astype(BF16)
            dks.append(jnp.dot(ds, q4, preferred_element_type=F32))
            dq4 = lax.dot_general(ds, k2, TN, preferred_element_type=F32)
            dsink4 = -psink * delta
            for g in range(GROUP):
                dqs.append(dq4[g * BLOCK:(g + 1) * BLOCK])
                dsinks.append(jnp.broadcast_to(jnp.sum(dsink4[:, g * BLOCK:(g + 1) * BLOCK], axis=1, keepdims=True), (1, LANES)))
        dq_ref[...] = jnp.concatenate(dqs, axis=-1)
        dkp_ref[...] = jnp.concatenate([d[:BLOCK] for d in dks], axis=-1)
        dkc_ref[...] = jnp.concatenate([d[BLOCK:] for d in dks], axis=-1)
        dvp_ref[...] = jnp.concatenate([d[:BLOCK] for d in dvs], axis=-1)
        dvc_ref[...] = jnp.concatenate([d[BLOCK:] for d in dvs], axis=-1)

        @pl.when(n == 0)
        def _():
            dsink_ref[...] = jnp.zeros_like(dsink_ref)

        dsink_ref[...] += jnp.concatenate(dsinks, axis=0)

    cur = lambda n: (n, 0)
    prev = lambda n: (jnp.maximum(n - 1, 0), 0)
    kspec = pl.BlockSpec((BLOCK, kw), cur)
    return pl.pallas_call(
        body, name="attn_bwd", grid=(t // BLOCK,),
        in_specs=[pl.BlockSpec((BLOCK, cw), cur),
                  pl.BlockSpec((BLOCK, kw), prev), kspec,
                  pl.BlockSpec((BLOCK, kw), prev), kspec,
                  pl.BlockSpec(sink_rows.shape, lambda n: (0, 0)),
                  pl.BlockSpec((BLOCK, cw), cur)],
        out_specs=[pl.BlockSpec((BLOCK, cw), cur), kspec, kspec, kspec, kspec,
                   pl.BlockSpec((nq, LANES), lambda n: (0, 0))],
        out_shape=[_sds((t, cw), F32)] + [_sds((t, kw), F32)] * 4 + [_sds((nq, LANES), F32)],
        compiler_params=_params(("arbitrary",), BLOCK * (cw + 4 * kw) * 2 + 2 * BLOCK * cw * 4 + 4 * BLOCK * kw * 4, 12 << 20),
    )(qn, kn, kn, vb, vb, sink_rows, do)


def _mix_out(ca, o, woc, woa, proj):
    t, cw = ca.shape
    nb = woc.shape[2]
    d = N_DEV * nb
    tm = min(t, 1024)
    ga0 = (3 * cw + cw + 2 * (cw // 4)) // nb

    def body(ca_ref, o_ref, woc_ref, woa_ref, ga_ref, gb_ref, m_ref, ya_ref, yb_ref):
        ya = jnp.dot(ca_ref[...], woc_ref[...], preferred_element_type=F32)
        yb = jnp.dot(o_ref[...], woa_ref[...], preferred_element_type=F32)
        ya_ref[...] = ya.astype(BF16)
        yb_ref[...] = yb.astype(BF16)
        m_ref[...] = (_sigmoid(ga_ref[...].astype(F32)) * ya + _sigmoid(gb_ref[...].astype(F32)) * yb).astype(BF16)

    act = pl.BlockSpec((tm, cw), lambda i, j: (i, 0))
    wsp = pl.BlockSpec((None, cw, nb), lambda i, j: (j, 0, 0))
    osp = pl.BlockSpec((tm, nb), lambda i, j: (i, j))
    blocks = 2 * tm * cw * 2 + 2 * cw * nb * 2 + 2 * tm * nb * 4 + 3 * tm * nb * 2
    return pl.pallas_call(
        body, name="mix_out", grid=(t // tm, N_DEV),
        in_specs=[act, act, wsp, wsp,
                  pl.BlockSpec((tm, nb), lambda i, j: (i, ga0 + j)),
                  pl.BlockSpec((tm, nb), lambda i, j: (i, ga0 + N_DEV + j))],
        out_specs=[osp, osp, osp],
        out_shape=[_sds((t, d), BF16)] * 3,
        compiler_params=_params(("parallel", "parallel"), blocks, 6 * tm * nb * 4),
    )(ca, o, woc, woa, proj, proj)


def _mix_residual(merged, wo, x):
    t, d = x.shape
    tm = min(t, 512)

    def epilogue(acc, ins, outs):
        outs[0][...] = ins[2][...] + acc

    row = pl.BlockSpec((tm, d), lambda i, j, k: (i, 0))
    return _fused("mix_residual", (t // tm, 1, 1),
                  [(merged, row), (wo, pl.BlockSpec((d, d), lambda i, j, k: (0, 0))), (x, row)],
                  [(_sds((t, d), F32), row)], [(0, 1, NN)], epilogue, temp_bytes=2 * tm * d * 4)[0]


def _mix_bwd_gates(dx, wo, ya, yb, proj, cw):
    t, d = dx.shape
    tm = min(t, 1024)
    tn = min(d, 512)
    ga0 = (4 * cw + 2 * (cw // 4)) // tn

    def epilogue(acc, ins, outs):
        sa, sb = _sigmoid(ins[4][...].astype(F32)), _sigmoid(ins[5][...].astype(F32))
        outs[0][...] = (acc * sa).astype(BF16)
        outs[1][...] = (acc * sb).astype(BF16)
        outs[2][0] = (acc * ins[2][...].astype(F32) * sa * (1.0 - sa)).astype(BF16)
        outs[2][1] = (acc * ins[3][...].astype(F32) * sb * (1.0 - sb)).astype(BF16)

    blk = pl.BlockSpec((tm, tn), lambda i, j, k: (i, j))
    return _fused("mix_bwd_gates", (t // tm, d // tn, 1),
                  [(dx, pl.BlockSpec((tm, d), lambda i, j, k: (i, 0))),
                   (wo, pl.BlockSpec((tn, d), lambda i, j, k: (j, 0))),
                   (ya, blk), (yb, blk),
                   (proj, pl.BlockSpec((tm, tn), lambda i, j, k: (i, ga0 + j))),
                   (proj, pl.BlockSpec((tm, tn), lambda i, j, k: (i, ga0 + d // tn + j)))],
                  [(_sds((t, d), BF16), blk), (_sds((t, d), BF16), blk),
                   (_sds((2, t, d), BF16), pl.BlockSpec((2, tm, tn), lambda i, j, k: (0, i, j)))],
                  [(0, 1, NT)], epilogue, temp_bytes=8 * tm * tn * 4)


def _tn_matmul(name, a, b, tm, out_dtype=BF16):
    t, m = a.shape
    n = b.shape[1]
    tk = min(t, 512)

    def epilogue(acc, ins, outs):
        outs[0][...] = acc.astype(out_dtype)

    return _fused(name, (m // tm, 1, t // tk),
                  [(a, pl.BlockSpec((tk, tm), lambda i, j, k: (k, i))),
                   (b, pl.BlockSpec((tk, n), lambda i, j, k: (k, 0)))],
                  [(_sds((m, n), out_dtype), pl.BlockSpec((tm, n), lambda i, j, k: (i, 0)))],
                  [(0, 1, TN)], epilogue, nk=t // tk, acc_shape=(tm, n), temp_bytes=tm * n * 4)[0]


def _out_proj_bwd_act(dya, dyb, woc, woa, deps=()):
    t, d = dya.shape
    kdim, nb = woc.shape[1], woc.shape[2]
    tm = min(t, 512)

    def body(dya_ref, dyb_ref, woc_ref, woa_ref, *rest):
        for dy_ref, w_ref, o_ref in ((dya_ref, woc_ref, rest[-2]), (dyb_ref, woa_ref, rest[-1])):
            total = None
            for j in range(N_DEV):
                part = lax.dot_general(dy_ref[:, j * nb:(j + 1) * nb], w_ref[j], NT, preferred_element_type=F32)
                total = part if total is None else total + part
            o_ref[...] = total

    row = pl.BlockSpec((tm, d), lambda i: (i, 0))
    wsp = pl.BlockSpec((N_DEV, kdim, nb), lambda i: (0, 0, 0))
    osp = pl.BlockSpec((tm, kdim), lambda i: (i, 0))
    blocks = 2 * tm * d * 2 + 2 * N_DEV * kdim * nb * 2 + 2 * tm * kdim * 4
    return pl.pallas_call(
        body, name="mix_bwd_dca_do", grid=(t // tm,),
        in_specs=[row, row, wsp, wsp] + [_ANY] * len(deps), out_specs=[osp, osp],
        out_shape=[_sds((t, kdim), F32)] * 2,
        compiler_params=_params(("parallel",), blocks, 4 * tm * kdim * 4),
    )(dya, dyb, woc, woa, *deps)


def _out_proj_bwd_w(ca, o, dya, dyb, nb):
    t, kdim = ca.shape

    def body(ca_ref, o_ref, dya_ref, dyb_ref, dwoc_ref, dwoa_ref):
        dwoc_ref[...] = lax.dot_general(ca_ref[...], dya_ref[...], TN, preferred_element_type=F32).astype(BF16)
        dwoa_ref[...] = lax.dot_general(o_ref[...], dyb_ref[...], TN, preferred_element_type=F32).astype(BF16)

    act = pl.BlockSpec((t, kdim), lambda j: (0, 0))
    col = pl.BlockSpec((t, nb), lambda j: (0, j))
    osp = pl.BlockSpec((None, kdim, nb), lambda j: (j, 0, 0))
    blocks = 2 * t * kdim * 2 + 2 * t * nb * 2 + 2 * kdim * nb * 2
    return pl.pallas_call(
        body, name="mix_bwd_dwoc_dwoa", grid=(N_DEV,),
        in_specs=[act, act, col, col], out_specs=[osp, osp],
        out_shape=[_sds((N_DEV, kdim, nb), BF16)] * 2,
        compiler_params=_params(("parallel",), blocks, 4 * kdim * nb * 4),
    )(ca, o, dya, dyb)


def _proj_bwd_act(dproj, w_in, deps=()):
    t, n = dproj.shape
    d, nb = w_in.shape[2], w_in.shape[3]
    tm = min(t, 512)

    def epilogue(acc, ins, outs):
        outs[0][...] = acc

    def products(ins):
        return (lax.dot_general(ins[0][:, 0:nb], ins[1][0], NT, preferred_element_type=F32)
                + lax.dot_general(ins[0][:, nb:2 * nb], ins[1][1], NT, preferred_element_type=F32))

    return _fused("mix_bwd_dh", (t // tm, 1, 4),
                  [(dproj, pl.BlockSpec((tm, 2 * nb), lambda i, j, k: (i, k))),
                   (w_in, pl.BlockSpec((None, 2, d, nb), lambda i, j, k: (k, 0, 0, 0)))],
                  [(_sds((t, d), F32), pl.BlockSpec((tm, d), lambda i, j, k: (i, 0)))],
                  products, epilogue, nk=4, acc_shape=(tm, d), temp_bytes=tm * d * 4, deps=deps)[0]


def _proj_bwd_w(h, dproj):
    t, d = h.shape
    nb = dproj.shape[1] // N_DEV
    tm = min(d, 512)

    def body(h_ref, dp_ref, o_ref):
        hv = h_ref[...]
        o_ref[0] = lax.dot_general(hv, dp_ref[:, 0:nb], TN, preferred_element_type=F32).astype(BF16)
        o_ref[1] = lax.dot_general(hv, dp_ref[:, nb:2 * nb], TN, preferred_element_type=F32).astype(BF16)

    blocks = t * tm * 2 + t * 2 * nb * 2 + 2 * tm * nb * 2
    return pl.pallas_call(
        body, name="mix_bwd_dwin", grid=(4, d // tm),
        in_specs=[pl.BlockSpec((t, tm), lambda j, i: (0, i)),
                  pl.BlockSpec((t, 2 * nb), lambda j, i: (0, j))],
        out_specs=pl.BlockSpec((None, 2, tm, nb), lambda j, i: (j, 0, i, 0)),
        out_shape=_sds((4, 2, d, nb), BF16),
        compiler_params=_params(("parallel", "parallel"), blocks, 4 * tm * nb * 4),
    )(h, dproj)


def _adamw_math(w, g, m, v):
    m = ADAM_B1 * m + (1.0 - ADAM_B1) * g
    v = ADAM_B2 * v + (1.0 - ADAM_B2) * (g * g)
    m_hat = m / (1.0 - ADAM_B1 ** ADAM_STEP)
    v_hat = v / (1.0 - ADAM_B2 ** ADAM_STEP)
    delta = -ADAM_LR * (m_hat / (jnp.sqrt(v_hat) + ADAM_EPS) + ADAM_WD * w)
    return delta, m, v


def _adamw(name, parts, w, m, v, tr):
    r, c = w.shape

    def body(p_ref, w_ref, m_ref, v_ref, g_out, d_out, m_out, v_out):
        g = p_ref[0].astype(F32)
        for s in range(1, N_DEV):
            g = g + p_ref[s].astype(F32)
        delta, mn, vn = _adamw_math(w_ref[...], g, m_ref[...], v_ref[...])
        g_out[...] = g
        d_out[...] = delta
        m_out[...] = mn
        v_out[...] = vn

    blk = pl.BlockSpec((tr, c), lambda i: (i, 0))
    blocks = N_DEV * tr * c * parts.dtype.itemsize + 7 * tr * c * 4
    return pl.pallas_call(
        body, name=name, grid=(r // tr,),
        in_specs=[pl.BlockSpec((N_DEV, tr, c), lambda i: (0, i, 0)), blk, blk, blk],
        out_specs=[blk] * 4, out_shape=[_sds((r, c), F32)] * 4,
        compiler_params=_params(("parallel",), blocks, 6 * tr * c * 4),
    )(parts, w, m, v)


def _chip_sum(sums_ref):
    g = sums_ref[0].astype(F32)
    for k in range(1, 4):
        g = g + sums_ref[k].astype(F32)
    return g


def _adamw_chips(name, sums, w, m, v, tr, deps=(), row0=0, into=None):
    r, c = w.shape
    rs = sums.shape[1]
    i0 = row0 // tr
    n_pass = len(deps) + (4 if into is not None else 0)

    def body(sums_ref, w_ref, m_ref, v_ref, *rest):
        g_out, d_out, m_out, v_out = rest[n_pass:]
        g = _chip_sum(sums_ref)
        delta, mn, vn = _adamw_math(w_ref[...], g, m_ref[...], v_ref[...])
        g_out[...] = g
        d_out[...] = delta
        m_out[...] = mn
        v_out[...] = vn

    blk = pl.BlockSpec((tr, c), lambda i: (i0 + i, 0))
    blocks = 4 * tr * c * 2 + 7 * tr * c * 4
    passed = list(deps) + (list(into) if into is not None else [])
    aliases = {4 + len(deps) + q: q for q in range(4)} if into is not None else {}
    return pl.pallas_call(
        body, name=name, grid=(rs // tr,),
        in_specs=[pl.BlockSpec((4, tr, c), lambda i: (0, i, 0)), blk, blk, blk] + [_ANY] * n_pass,
        out_specs=[blk] * 4, out_shape=[_sds((r, c), F32)] * 4,
        input_output_aliases=aliases,
        compiler_params=_params(("parallel",), blocks, 6 * tr * c * 4),
    )(sums, w, m, v, *passed)


def _adamw_side(contrib, w, m, v, n_tiles, step_of):
    r, c = w.shape
    tr = r // n_tiles
    assert tr * n_tiles == r and tr % 16 == 0, (r, n_tiles)

    def tile(i, j, k):
        return jnp.minimum(step_of(i, j, k), n_tiles - 1)

    blk = pl.BlockSpec((tr, c), lambda i, j, k: (tile(i, j, k), 0))
    ins = [(contrib, pl.BlockSpec((4, tr, c), lambda i, j, k: (0, tile(i, j, k), 0))), (w, blk), (m, blk), (v, blk)]
    outs = [(_sds((r, c), F32), blk)] * 4

    def fn(in_refs, out_refs):
        @pl.when(step_of(pl.program_id(0), pl.program_id(1), pl.program_id(2)) < n_tiles)
        def _():
            g = _chip_sum(in_refs[0])
            delta, mn, vn = _adamw_math(in_refs[1][...], g, in_refs[2][...], in_refs[3][...])
            out_refs[0][...] = g
            out_refs[1][...] = delta
            out_refs[2][...] = mn
            out_refs[3][...] = vn

    return ins, outs, fn


def _rope_tables(t):
    half = ROT_DIM // 2
    inv_freq = 1.0 / (ROPE_THETA ** (jnp.arange(0, ROT_DIM, 2, dtype=F32) / ROT_DIM))
    ang = jnp.arange(t, dtype=F32)[:, None] * inv_freq[None, :]
    cos, sin = jnp.cos(ang), jnp.sin(ang)
    ones = jnp.ones((t, HEAD_DIM - ROT_DIM), F32)
    zeros = jnp.zeros((t, HEAD_DIM - half), F32)
    c = jnp.concatenate([cos, cos, ones], axis=1)
    sa = jnp.concatenate([-sin, zeros], axis=1)
    sb = jnp.concatenate([jnp.zeros((t, half), F32), sin, jnp.zeros((t, HEAD_DIM - ROT_DIM), F32)], axis=1)
    return tuple(jnp.tile(a, (1, LANES // HEAD_DIM)) for a in (c, sa, sb))


def _pad_rows(a, rows=8):
    return jnp.pad(a, ((0, rows - a.shape[0]), (0, 0)))


def kernel(x, g_ffn1, w_gu1, w_down1, g_mix, w_in, conv_w, q_norm_g, k_norm_g, sinks, w_out_conv, w_out_attn, w_o, g_ffn2, w_gu2, w_down2, loss_target, m_g_ffn1, m_w_gu1, m_w_down1, m_g_mix, m_w_in, m_conv_w, m_q_norm_g, m_k_norm_g, m_sinks, m_w_out_conv, m_w_out_attn, m_w_o, m_g_ffn2, m_w_gu2, m_w_down2, v_g_ffn1, v_w_gu1, v_w_down1, v_g_mix, v_w_in, v_conv_w, v_q_norm_g, v_k_norm_g, v_sinks, v_w_out_conv, v_w_out_attn, v_w_o, v_g_ffn2, v_w_gu2, v_w_down2):
    t, d = x.shape[1], x.shape[2]
    cw = d // 2
    kw = cw // GROUP
    nq = cw // HEAD_DIM
    xs, target = x.reshape(t, d), loss_target.reshape(t, d)
    me = 4 * lax.axis_index("x") + 2 * lax.axis_index("y") + lax.axis_index("c")

    big = {"w_gu1": w_gu1, "w_down1": w_down1, "w_in": w_in, "w_out_conv": w_out_conv,
           "w_out_attn": w_out_attn, "w_o": w_o, "w_gu2": w_gu2, "w_down2": w_down2}
    big_m = {"w_gu1": m_w_gu1, "w_down1": m_w_down1, "w_in": m_w_in, "w_out_conv": m_w_out_conv,
             "w_out_attn": m_w_out_attn, "w_o": m_w_o, "w_gu2": m_w_gu2, "w_down2": m_w_down2}
    big_v = {"w_gu1": v_w_gu1, "w_down1": v_w_down1, "w_in": v_w_in, "w_out_conv": v_w_out_conv,
             "w_out_attn": v_w_out_attn, "w_o": v_w_o, "w_gu2": v_w_gu2, "w_down2": v_w_down2}
    names = list(big)

    tiles = {"w_gu1": 256, "w_gu2": 256, "w_in": 256, "w_down1": 176, "w_down2": 176,
             "w_out_conv": 1024, "w_out_attn": 1024, "w_o": 128}

    def row_tile(n):
        r = big[n].shape[1]
        return tiles[n] if r % tiles[n] == 0 else r

    rs_shape = {n: big[n].shape[1:] for n in names}
    half = rs_shape["w_gu1"][0] // 2
    rs_shape["w_gu1_lo"] = rs_shape["w_gu1_hi"] = (half, rs_shape["w_gu1"][1])

    def add_tile(n):
        r, c = rs_shape[n]
        while r * c * 2 > (3 << 20) and r % 32 == 0:
            r //= 2
        return r

    me_arr = me.astype(jnp.int32).reshape(1)
    sources = [(n, big[n][0], BF16, row_tile(n)) for n in names] + [("conv_w", _pad_rows(conv_w[0]), F32, 8)]
    issue_order = [0, 1, 2, 8, 3, 4, 5, 6, 7]
    first = _place_shard("place_" + names[0], sources[0][1], BF16, me_arr, sources[0][3])
    started = [_gather_start("gather_start_first", [first])]
    early = {2: (big_m["w_in"][0], big_v["w_in"][0])}
    rest = [_place_shard("place_" + sources[i][0], sources[i][1], sources[i][2], me_arr, sources[i][3],
                         deps=(started[0][3],) + early.get(i, ())) for i in issue_order[1:]]
    started.append(_gather_start("gather_start_rest", rest))
    where = {0: (0, 0)}
    where.update({i: (1, p) for p, i in enumerate(issue_order[1:])})

    def fetch(tag, idxs, after):
        call = where[idxs[0]][0]
        send, recv, stacks, _ = started[call]
        positions = [where[i][1] for i in idxs]
        got = _gather_wait("gather_wait_" + tag, positions, send, recv, [stacks[p] for p in positions], after)
        return _forward_to_sibling("gather_forward_" + tag, got)

    rope_tabs = _rope_tables(t)
    gq = jnp.tile(q_norm_g, (1, nq))
    gk = jnp.tile(k_norm_g, (1, nq // GROUP))
    sink_rows = jnp.broadcast_to(sinks[0][:, None], (nq, LANES))

    wts = {}
    h1 = _rms_fwd("ffn1_norm", xs, g_ffn1)
    wts["w_gu1"], = fetch("gu1", [0], started[1][3])
    gu1, a1 = _ffn_up("ffn1_up", h1, wts["w_gu1"])
    wts["w_down1"], = fetch("down1", [1], a1)
    wd1 = wts["w_down1"].reshape(-1, d)
    x1 = _ffn_down("ffn1_down", a1, wd1, xs)
    h2 = _rms_fwd("mix_norm", x1, g_mix)
    wts["w_in"], conv_land = fetch("in", [2, 8], h2)
    w_in_full = wts["w_in"].reshape(4, 2, d, -1)
    conv_full = jnp.transpose(conv_land, (1, 0, 2)).reshape(8, cw)
    proj = _proj(h2, w_in_full)
    ca = _conv_fwd(proj, conv_full)
    qn, kn, vb = _qk_prep(proj, gq, gk, rope_tabs, cw, kw)
    o = _attn_fwd(qn, kn, vb, sink_rows)
    wts["w_out_conv"], wts["w_out_attn"] = fetch("out", [3, 4], o)
    merged, ya, yb = _mix_out(ca, o, wts["w_out_conv"], wts["w_out_attn"], proj)
    wts["w_o"], = fetch("o", [5], merged)
    wo = wts["w_o"].reshape(d, d)
    x2 = _mix_residual(merged, wo, x1)
    h3 = _rms_fwd("ffn2_norm", x2, g_ffn2)
    wts["w_gu2"], = fetch("gu2", [6], h3)
    gu2, a2 = _ffn_up("ffn2_up", h3, wts["w_gu2"])
    wts["w_down2"], = fetch("down2", [7], a2)
    wd2 = wts["w_down2"].reshape(-1, d)
    y = _ffn_down("ffn2_down", a2, wd2, x2)
    dy, sq = _loss_dy(y, target)
    loss = lax.psum(sq[0, 0] * (0.5 / d), ("x", "y", "c"))

    place = jnp.stack([lax.axis_index("c"), 2 * lax.axis_index("x") + lax.axis_index("y")]).astype(jnp.int32)
    def pair_start(tag, group, grads, deps=()):
        stacks = [grads[n].reshape((4, 2) + rs_shape[n]) for n in group]
        lands = [lax.empty((4,) + rs_shape[n], BF16) for n in group]
        return _pair_start("rs_pair_start_" + tag, stacks, lands, deps)

    def chip_start(tag, group, pending, after):
        send, recv, stacks, lands, _ = pending
        stacks, lands = _pair_wait("rs_pair_wait_" + tag, send, recv, stacks, lands, after)
        added = [_pair_add("rs_pair_add_" + n, st, ld, place, add_tile(n)) for n, st, ld in zip(group, stacks, lands)]
        return _chip_start("rs_chip_start_" + tag, [a[0] for a in added], [a[1] for a in added])

    group_a, group_b, group_c = ["w_down2", "w_gu2"], ["w_o", "w_out_conv", "w_out_attn"], ["w_in"]
    group_d, group_e, group_f = ["w_down1"], ["w_gu1_lo"], ["w_gu1_hi"]
    g = {}
    dgu2, a2 = _ffn_bwd_act("ffn2_bwd_act", dy, wd2, gu2)
    g["w_down2"], = _ffn_bwd_dwd("ffn2_bwd_dwd", a2, dy)
    g["w_gu2"], = _ffn_bwd_dwgu("ffn2_bwd_dwgu", h3, dgu2)
    pend_a = pair_start("a", group_a, g)
    dh3, = _ffn_bwd_dh("ffn2_bwd_dh", dgu2, wts["w_gu2"], deps=(pend_a[4],))
    ring_a = chip_start("a", group_a, pend_a, dh3)
    dx2, dg_ffn2 = _rms_bwd("ffn2_bwd_rms", x2, g_ffn2, dh3, dy, deps=(ring_a[4],))

    dya, dyb, dgates = _mix_bwd_gates(dx2, wo, ya, yb, proj, cw)
    g["w_o"] = _tn_matmul("mix_bwd_dwo", merged, dx2, min(d, 1024))
    g["w_out_conv"], g["w_out_attn"] = _out_proj_bwd_w(ca, o, dya, dyb, d // N_DEV)
    pend_b = pair_start("b", group_b, g)
    dca, do = _out_proj_bwd_act(dya, dyb, wts["w_out_conv"], wts["w_out_attn"], deps=(pend_b[4],))
    ring_b = chip_start("b", group_b, pend_b, do)
    d3, dconv_w = _conv_bwd(proj, conv_full, dca, deps=(ring_b[4],))
    dq, dkc, dkp, dvc, dvp, dsink = _attn_bwd(qn, kn, vb, sink_rows, do)
    dqkv, dgq, dgk = _qk_prep_bwd(proj, gq, gk, rope_tabs, dq, dkc, dkp, dvc, dvp, cw, kw)
    dproj = jnp.concatenate([d3[0], d3[1], d3[2], dqkv, dgates[0], dgates[1]], axis=1)
    g["w_in"] = _proj_bwd_w(h2, dproj)
    pend_c = pair_start("c", group_c, g)
    dh2 = _proj_bwd_act(dproj, w_in_full, deps=(pend_c[4],))
    ring_c = chip_start("c", group_c, pend_c, dh2)
    dx1, dg_mix = _rms_bwd("mix_bwd_rms", x1, g_mix, dh2, dx2, deps=(ring_c[4],))

    big_out = {}
    arrived = {}

    def wait_group(tag, group, ring, after):
        send, recv, parts, lands2, _ = ring
        parts, lands2 = _chip_wait("rs_chip_wait_" + tag, send, recv, parts, lands2, after)
        arrived.update(dict(zip(group, lands2)))

    def update(n, after):
        res = _adamw_chips("adamw_" + n, arrived[n], big[n][0], big_m[n][0], big_v[n][0], row_tile(n), deps=(after,))
        big_out[n] = [a[None] for a in res]
        return res[0]

    def update_beside(n, n_tiles, step_of):
        return _adamw_side(arrived[n], big[n][0], big_m[n][0], big_v[n][0], n_tiles, step_of)

    def keep(n, res):
        big_out[n] = [a[None] for a in res]

    dgu1, a1 = _ffn_bwd_act("ffn1_bwd_act", dx1, wd1, gu1)
    wait_group("a", group_a, ring_a, a1)
    g["w_down1"], *res = _ffn_bwd_dwd("ffn1_bwd_dwd", a1, dx1,
                                       side=update_beside("w_down2", 11, lambda i, j, k: i * 4 + j))
    keep("w_down2", res)
    pend_d = pair_start("d", group_d, g)
    g["w_gu1_lo"], *res = _ffn_bwd_dwgu("ffn1_bwd_dwgu_lo", h1, dgu1, deps=(pend_d[4],), rows=(0, half),
                                         side=update_beside("w_gu2", 16, lambda i, j, k: i * 2 + j))
    keep("w_gu2", res)
    ring_d = chip_start("d", group_d, pend_d, g["w_gu1_lo"])
    pend_e = pair_start("e", group_e, g, deps=(ring_d[4],))
    g["w_gu1_hi"], = _ffn_bwd_dwgu("ffn1_bwd_dwgu_hi", h1, dgu1, deps=(pend_e[4],), rows=(half, half))
    ring_e = chip_start("e", group_e, pend_e, g["w_gu1_hi"])
    pend_f = pair_start("f", group_f, g, deps=(ring_e[4],))
    wait_group("b", group_b, ring_b, pend_f[4])
    after = pend_f[4]
    for n in group_b:
        after = update(n, after)
    ring_f = chip_start("f", group_f, pend_f, after)
    wait_group("c", group_c, ring_c, ring_f[4])
    dh1, *res = _ffn_bwd_dh("ffn1_bwd_dh", dgu1, wts["w_gu1"],
                             side=update_beside("w_in", 16, lambda i, j, k: i * 4 + k))
    keep("w_in", res)
    grad_x, dg_ffn1 = _rms_bwd("ffn1_bwd_rms", xs, g_ffn1, dh1, dx1)
    wait_group("d", group_d, ring_d, grad_x)
    after = update("w_down1", grad_x)
    n = "w_gu1"
    wait_group("e", group_e, ring_e, after)
    res = _adamw_chips("adamw_w_gu1_lo", arrived["w_gu1_lo"], big[n][0], big_m[n][0], big_v[n][0], row_tile(n), deps=(after,))
    wait_group("f", group_f, ring_f, res[0])
    res = _adamw_chips("adamw_w_gu1_hi", arrived["w_gu1_hi"], big[n][0], big_m[n][0], big_v[n][0], row_tile(n),
                       row0=half, into=res)
    keep(n, res)
    after = res[0]

    small = {"g_ffn1": dg_ffn1[0:1], "g_mix": dg_mix[0:1], "g_ffn2": dg_ffn2[0:1],
             "q_norm_g": dgq[0:1, :HEAD_DIM], "k_norm_g": dgk[0:1, :HEAD_DIM], "sinks": dsink[:, 0][None],
             "conv_w": dconv_w[0:CONV_K].reshape(1, -1)}
    small_w = {"g_ffn1": g_ffn1, "g_mix": g_mix, "g_ffn2": g_ffn2, "q_norm_g": q_norm_g, "k_norm_g": k_norm_g,
               "sinks": sinks, "conv_w": None}
    small_m = {"g_ffn1": m_g_ffn1, "g_mix": m_g_mix, "g_ffn2": m_g_ffn2, "q_norm_g": m_q_norm_g,
               "k_norm_g": m_k_norm_g, "sinks": m_sinks, "conv_w": m_conv_w}
    small_v = {"g_ffn1": v_g_ffn1, "g_mix": v_g_mix, "g_ffn2": v_g_ffn2, "q_norm_g": v_q_norm_g,
               "k_norm_g": v_k_norm_g, "sinks": v_sinks, "conv_w": v_conv_w}
    snames = list(small)
    widths = [small[n].shape[1] for n in snames]
    total = sum(widths)
    rows = -(-total // LANES)
    rows = -(-rows // 8) * 8

    def pack(vals):
        flat = jnp.concatenate([v.reshape(1, -1) for v in vals], axis=1)
        return jnp.pad(flat, ((0, 0), (0, rows * LANES - total))).reshape(rows, LANES)

    csh = cw // N_DEV

    def place_conv(local, fill):
        full = jnp.full((CONV_K, cw), fill, F32)
        return lax.dynamic_update_slice(full, local, (0, me * csh)).reshape(1, -1)

    pw = pack([small_w[n] if n != "conv_w" else place_conv(conv_w[0], 0.0) for n in snames])
    pm = pack([small_m[n] if n != "conv_w" else place_conv(m_conv_w[0], 0.0) for n in snames])
    pv = pack([small_v[n] if n != "conv_w" else place_conv(v_conv_w[0], 1.0) for n in snames])
    parts = _exchange("gather_small_grads", [pack([small[n] for n in snames])], gather=True, deps=(after,))[0]
    sg, sd, sm, sv = [a.reshape(1, -1) for a in _adamw("adamw_small", parts, pw, pm, pv, rows)]

    def unpack(flat, n):
        off = sum(widths[:snames.index(n)])
        piece = flat[:, off:off + widths[snames.index(n)]]
        if n == "conv_w":
            piece = lax.dynamic_slice(piece.reshape(CONV_K, cw), (0, me * csh), (CONV_K, csh))[None]
        return piece

    order = ["g_ffn1", "w_gu1", "w_down1", "g_mix", "w_in", "conv_w", "q_norm_g", "k_norm_g", "sinks",
             "w_out_conv", "w_out_attn", "w_o", "g_ffn2", "w_gu2", "w_down2"]
    outs = [loss, grad_x[None]]
    for idx, flat in enumerate((sg, sd, sm, sv)):
        for n in order:
            outs.append(big_out[n][idx] if n in big_out else unpack(flat, n))
    return tuple(outs)
```

```python
import functools

import jax
import jax.numpy as jnp
from jax import lax
from jax.experimental import pallas as pl
from jax.experimental.pallas import tpu as pltpu

F32 = jnp.float32
BF16 = jnp.bfloat16

N_DEV = 8
HEAD_DIM = 64
GROUP = 4
BLOCK = 128
ROT_DIM = 16
ROPE_THETA = 500000.0
RMS_EPS = 1e-6
NEG_INF = -1e30
ATTN_SCALE = HEAD_DIM ** -0.5
CONV_K = 3
LANES = 128
MXU_COLS = 256
VMEM_BYTES_V7X = 64 * 1024 * 1024
VMEM_CAP = VMEM_BYTES_V7X - 6 * 1024 * 1024

ADAM_LR = 0.001
ADAM_B1 = 0.9
ADAM_B2 = 0.999
ADAM_EPS = 1e-08
ADAM_WD = 0.01
ADAM_STEP = 10

NN = (((1,), (0,)), ((), ()))
NT = (((1,), (1,)), ((), ()))
TN = (((0,), (0,)), ((), ()))

MESH = pl.DeviceIdType.MESH


def _nbytes(shape, dtype):
    n = 1
    for s in shape:
        if s is not None:
            n *= s
    return n * jnp.dtype(dtype).itemsize


def _params(semantics, block_bytes, temp_bytes):
    assert 2 * block_bytes + temp_bytes <= VMEM_CAP, (block_bytes, temp_bytes)
    return pltpu.CompilerParams(dimension_semantics=semantics, vmem_limit_bytes=VMEM_CAP)


def _fused(name, grid, ins, outs, dots, epilogue, *, nk=1, acc_shape=None, temp_bytes=0,
           semantics=("parallel", "parallel", "arbitrary"), deps=(), side=None):
    n_main_in, n_main_out = len(ins), len(outs)
    if side is not None:
        ins, outs = list(ins) + list(side[0]), list(outs) + list(side[1])
    n_in, n_out = len(ins), len(outs)
    n_dep = len(deps)

    def body(*refs):
        in_refs, out_refs = refs[:n_in], refs[n_in + n_dep:n_in + n_dep + n_out]
        scratch = refs[n_in + n_dep + n_out:]
        if side is not None:
            side[2](in_refs[n_main_in:], out_refs[n_main_out:])

        def products():
            if callable(dots):
                return dots(in_refs)
            total = None
            for ai, bi, contract in dots:
                a, b = in_refs[ai][...], in_refs[bi][...]
                a = a if a.dtype == BF16 else a.astype(BF16)
                b = b if b.dtype == BF16 else b.astype(BF16)
                p = lax.dot_general(a, b, contract, preferred_element_type=F32)
                total = p if total is None else total + p
            return total

        if nk == 1:
            epilogue(products() if dots else None, in_refs, out_refs)
        else:
            acc = scratch[0]
            k = pl.program_id(2)

            @pl.when(k == 0)
            def _():
                acc[...] = jnp.zeros_like(acc)

            acc[...] += products()

            @pl.when(k == nk - 1)
            def _():
                epilogue(acc[...], in_refs, out_refs)

    block_bytes = sum(_nbytes(spec.block_shape, a.dtype) for a, spec in ins)
    block_bytes += sum(_nbytes(spec.block_shape, s.dtype) for s, spec in outs)
    scratch_shapes = []
    if nk > 1:
        scratch_shapes.append(pltpu.VMEM(acc_shape, F32))
        temp_bytes += _nbytes(acc_shape, F32)
    res = pl.pallas_call(
        body, name=name, grid=grid,
        in_specs=[spec for _, spec in ins] + [pl.BlockSpec(memory_space=pl.ANY)] * n_dep,
        out_specs=[spec for _, spec in outs],
        out_shape=[s for s, _ in outs],
        scratch_shapes=scratch_shapes,
        compiler_params=_params(semantics, block_bytes, temp_bytes),
    )(*[a for a, _ in ins], *deps)
    return res


def _sds(shape, dtype):
    return jax.ShapeDtypeStruct(shape, dtype)


def _sigmoid(x):
    return jax.nn.sigmoid(x)


def _exchange(name, arrays, gather, deps=()):
    n = len(arrays)
    out_shapes = [((N_DEV,) + a.shape) if gather else a.shape for a in arrays]

    def body(*refs):
        srcs, dsts = refs[:n], refs[n + len(deps):2 * n + len(deps)]
        send_sems, recv_sems, local_sems = refs[2 * n + len(deps):]
        x, y, c = lax.axis_index("x"), lax.axis_index("y"), lax.axis_index("c")
        me = 4 * x + 2 * y + c
        copies = []
        for w in range(n):
            own = srcs[w] if gather else srcs[w].at[me]
            local = pltpu.make_async_copy(own, dsts[w].at[me], local_sems.at[w])
            local.start()
            copies.append(local)
            for k in range(1, N_DEV):
                px = (1 - x) if (k & 4) else x
                py = (1 - y) if (k & 2) else y
                pc = (1 - c) if (k & 1) else c
                peer = 4 * px + 2 * py + pc
                cp = pltpu.make_async_remote_copy(
                    src_ref=srcs[w] if gather else srcs[w].at[peer],
                    dst_ref=dsts[w].at[me],
                    send_sem=send_sems.at[w * (N_DEV - 1) + k - 1],
                    recv_sem=recv_sems.at[w * (N_DEV - 1) + k - 1],
                    device_id=(px, py, pc), device_id_type=MESH)
                cp.start()
                copies.append(cp)
        for cp in copies:
            cp.wait()

    hbm = pl.BlockSpec(memory_space=pltpu.HBM)
    return pl.pallas_call(
        body, name=name,
        in_specs=[hbm] * n + [pl.BlockSpec(memory_space=pl.ANY)] * len(deps), out_specs=[hbm] * n,
        out_shape=[_sds(s, a.dtype) for s, a in zip(out_shapes, arrays)],
        scratch_shapes=[pltpu.SemaphoreType.DMA((n * (N_DEV - 1),)),
                        pltpu.SemaphoreType.DMA((n * (N_DEV - 1),)),
                        pltpu.SemaphoreType.DMA((n,))],
    )(*arrays, *deps)


_HBM = pl.BlockSpec(memory_space=pltpu.HBM)
_SEM = pl.BlockSpec(memory_space=pltpu.SEMAPHORE)
_ANY = pl.BlockSpec(memory_space=pl.ANY)
_EFFECT = pltpu.SideEffectType.DATAFLOW_SIDE_EFFECTING
N_TARGETS = 4


def _mesh_pos():
    return lax.axis_index("x"), lax.axis_index("y"), lax.axis_index("c")


def _chip_peers(x, y, c):
    return [(1 - x, y, c), (x, 1 - y, c), (1 - x, 1 - y, c)]


def _dev_index(pos):
    return 4 * pos[0] + 2 * pos[1] + pos[2]


def _hbm_like(a):
    return pltpu.HBM(a.shape, a.dtype)


def _place_shard(name, w, out_dtype, me, tr, deps=()):
    r, c = w.shape
    n_dep = len(deps)

    def body(me_ref, w_ref, *rest):
        rest[n_dep][...] = w_ref[...].astype(out_dtype)

    grid_spec = pltpu.PrefetchScalarGridSpec(
        num_scalar_prefetch=1, grid=(r // tr,),
        in_specs=[pl.BlockSpec((tr, c), lambda i, me_ref: (i, 0))] + [_ANY] * n_dep,
        out_specs=pl.BlockSpec((None, tr, c), lambda i, me_ref: (me_ref[0], i, 0)))
    return pl.pallas_call(
        body, name=name, grid_spec=grid_spec, out_shape=_sds((N_DEV, r, c), out_dtype),
        compiler_params=_params(("parallel",), tr * c * 6, tr * c * 4),
    )(me, w, *deps)


def _gather_start(name, lands):
    n = len(lands)

    def body(*refs):
        bufs = refs[:n]
        send, recv = refs[n], refs[n + 1]
        token = refs[-1]
        x, y, c = _mesh_pos()
        me = _dev_index((x, y, c))
        targets = [(x, y, 1 - c)] + _chip_peers(x, y, c)
        for w in range(n):
            for k, to in enumerate(targets):
                pltpu.make_async_remote_copy(
                    src_ref=bufs[w].at[me], dst_ref=bufs[w].at[me],
                    send_sem=send.at[N_TARGETS * w + k], recv_sem=recv.at[N_TARGETS * w + k],
                    device_id=to, device_id_type=MESH).start()
        token[...] = jnp.zeros_like(token)

    sems = pltpu.SemaphoreType.DMA((N_TARGETS * n,))
    outs = pl.pallas_call(
        body, name=name,
        in_specs=[_HBM] * n, out_specs=[_SEM, _SEM] + [_HBM] * n + [_token_spec()],
        out_shape=[sems, sems] + [_hbm_like(a) for a in lands] + [_sds((8, LANES), F32)],
        input_output_aliases={i: 2 + i for i in range(n)},
        compiler_params=pltpu.CompilerParams(has_side_effects=_EFFECT),
    )(*lands)
    return outs[0], outs[1], list(outs[2:2 + n]), outs[-1]


def _gather_wait(name, positions, send, recv, lands, after):
    m = len(positions)

    def body(*refs):
        bufs = refs[:m]
        send_sems, recv_sems = refs[m], refs[m + 1]
        x, y, c = _mesh_pos()
        me = _dev_index((x, y, c))
        sources = [(x, y, 1 - c)] + _chip_peers(x, y, c)
        for j, w in enumerate(positions):
            for k, frm in enumerate(sources):
                cp = pltpu.make_async_remote_copy(
                    src_ref=bufs[j].at[me], dst_ref=bufs[j].at[_dev_index(frm)],
                    send_sem=send_sems.at[N_TARGETS * w + k], recv_sem=recv_sems.at[N_TARGETS * w + k],
                    device_id=frm, device_id_type=MESH)
                cp.wait_send()
                cp.wait_recv()

    outs = pl.pallas_call(
        body, name=name,
        in_specs=[_HBM] * m + [_SEM, _SEM, _ANY], out_specs=[_HBM] * m,
        out_shape=[_hbm_like(a) for a in lands],
        input_output_aliases={i: i for i in range(m)},
        compiler_params=pltpu.CompilerParams(has_side_effects=_EFFECT),
    )(*lands, send, recv, after)
    return list(outs)


def _forward_to_sibling(name, lands):
    m = len(lands)

    def body(*refs):
        bufs = refs[m:2 * m]
        send_sems, recv_sems = refs[2 * m], refs[2 * m + 1]
        x, y, c = _mesh_pos()
        copies = []
        for j in range(m):
            for k, chip in enumerate(_chip_peers(x, y, c)):
                block = bufs[j].at[_dev_index(chip)]
                cp = pltpu.make_async_remote_copy(
                    src_ref=block, dst_ref=block,
                    send_sem=send_sems.at[3 * j + k], recv_sem=recv_sems.at[3 * j + k],
                    device_id=(x, y, 1 - c), device_id_type=MESH)
                cp.start()
                copies.append(cp)
        for cp in copies:
            cp.wait()

    outs = pl.pallas_call(
        body, name=name,
        in_specs=[_HBM] * m, out_specs=[_HBM] * m,
        out_shape=[_sds(a.shape, a.dtype) for a in lands],
        input_output_aliases={i: i for i in range(m)},
        scratch_shapes=[pltpu.SemaphoreType.DMA((3 * m,)), pltpu.SemaphoreType.DMA((3 * m,))],
    )(*lands)
    return list(outs)


def _token_spec():
    return pl.BlockSpec(memory_space=pltpu.VMEM)


def _pair_start(name, stacks, lands, deps=()):
    n = len(stacks)
    n_dep = len(deps)

    def body(*refs):
        srcs, dsts = refs[:n], refs[n:2 * n]
        send, recv = refs[2 * n + n_dep], refs[2 * n + n_dep + 1]
        token = refs[-1]
        x, y, c = _mesh_pos()
        for w in range(n):
            for chip in range(4):
                pltpu.make_async_remote_copy(
                    src_ref=srcs[w].at[chip, 1 - c], dst_ref=dsts[w].at[chip],
                    send_sem=send.at[4 * w + chip], recv_sem=recv.at[4 * w + chip],
                    device_id=(x, y, 1 - c), device_id_type=MESH).start()
        token[...] = jnp.zeros_like(token)

    sems = pltpu.SemaphoreType.DMA((4 * n,))
    outs = pl.pallas_call(
        body, name=name,
        in_specs=[_HBM] * (2 * n) + [_ANY] * n_dep, out_specs=[_SEM, _SEM] + [_HBM] * (2 * n) + [_token_spec()],
        out_shape=[sems, sems] + [_hbm_like(a) for a in stacks] + [_hbm_like(a) for a in lands] + [_sds((8, LANES), F32)],
        input_output_aliases={i: 2 + i for i in range(2 * n)},
        compiler_params=pltpu.CompilerParams(has_side_effects=_EFFECT),
    )(*stacks, *lands, *deps)
    return outs[0], outs[1], list(outs[2:2 + n]), list(outs[2 + n:2 + 2 * n]), outs[-1]


def _pair_wait(name, send, recv, stacks, lands, after):
    n = len(stacks)

    def body(*refs):
        srcs, dsts = refs[:n], refs[n:2 * n]
        send_sems, recv_sems = refs[2 * n], refs[2 * n + 1]
        x, y, c = _mesh_pos()
        for w in range(n):
            for chip in range(4):
                cp = pltpu.make_async_remote_copy(
                    src_ref=srcs[w].at[chip, 1 - c], dst_ref=dsts[w].at[chip],
                    send_sem=send_sems.at[4 * w + chip], recv_sem=recv_sems.at[4 * w + chip],
                    device_id=(x, y, 1 - c), device_id_type=MESH)
                cp.wait_send()
                cp.wait_recv()

    outs = pl.pallas_call(
        body, name=name,
        in_specs=[_HBM] * (2 * n) + [_SEM, _SEM, _ANY], out_specs=[_HBM] * (2 * n),
        out_shape=[_hbm_like(a) for a in stacks] + [_hbm_like(a) for a in lands],
        input_output_aliases={i: i for i in range(2 * n)},
        compiler_params=pltpu.CompilerParams(has_side_effects=_EFFECT),
    )(*stacks, *lands, send, recv, after)
    return list(outs[:n]), list(outs[n:])


def _pair_add(name, stack, land, place, tr):
    _, _, r, c = stack.shape

    def body(place_ref, a_ref, b_ref, sums_ref, slots_ref):
        total = (a_ref[...].astype(F32) + b_ref[...].astype(F32)).astype(BF16)
        sums_ref[...] = total

        @pl.when(pl.program_id(1) == place_ref[1])
        def _():
            slots_ref[...] = total

    grid_spec = pltpu.PrefetchScalarGridSpec(
        num_scalar_prefetch=1, grid=(r // tr, 4),
        in_specs=[pl.BlockSpec((None, None, tr, c), lambda i, k, place_ref: (k, place_ref[0], i, 0)),
                  pl.BlockSpec((None, tr, c), lambda i, k, place_ref: (k, i, 0))],
        out_specs=[pl.BlockSpec((None, tr, c), lambda i, k, place_ref: (k, i, 0)),
                   pl.BlockSpec((None, tr, c), lambda i, k, place_ref: (place_ref[1], i, 0))])
    return pl.pallas_call(
        body, name=name, grid_spec=grid_spec, out_shape=[_sds((4, r, c), BF16)] * 2,
        compiler_params=_params(("parallel", "arbitrary"), 4 * tr * c * 2, 3 * tr * c * 4),
    )(place, stack, land)


def _chip_start(name, parts, lands):
    n = len(parts)

    def body(*refs):
        srcs, dsts = refs[:n], refs[n:2 * n]
        send, recv = refs[2 * n], refs[2 * n + 1]
        token = refs[-1]
        x, y, c = _mesh_pos()
        for w in range(n):
            for k, to in enumerate(_chip_peers(x, y, c)):
                pltpu.make_async_remote_copy(
                    src_ref=srcs[w].at[2 * to[0] + to[1]], dst_ref=dsts[w].at[2 * x + y],
                    send_sem=send.at[3 * w + k], recv_sem=recv.at[3 * w + k],
                    device_id=to, device_id_type=MESH).start()
        token[...] = jnp.zeros_like(token)

    sems = pltpu.SemaphoreType.DMA((3 * n,))
    outs = pl.pallas_call(
        body, name=name,
        in_specs=[_HBM] * (2 * n), out_specs=[_SEM, _SEM] + [_HBM] * (2 * n) + [_token_spec()],
        out_shape=[sems, sems] + [_hbm_like(a) for a in parts] + [_hbm_like(a) for a in lands] + [_sds((8, LANES), F32)],
        input_output_aliases={i: 2 + i for i in range(2 * n)},
        compiler_params=pltpu.CompilerParams(has_side_effects=_EFFECT),
    )(*parts, *lands)
    return outs[0], outs[1], list(outs[2:2 + n]), list(outs[2 + n:2 + 2 * n]), outs[-1]


def _chip_wait(name, send, recv, parts, lands, after):
    n = len(parts)

    def body(*refs):
        srcs, dsts = refs[:n], refs[n:2 * n]
        send_sems, recv_sems = refs[2 * n], refs[2 * n + 1]
        x, y, c = _mesh_pos()
        for w in range(n):
            for k, frm in enumerate(_chip_peers(x, y, c)):
                chip = 2 * frm[0] + frm[1]
                cp = pltpu.make_async_remote_copy(
                    src_ref=srcs[w].at[chip], dst_ref=dsts[w].at[chip],
                    send_sem=send_sems.at[3 * w + k], recv_sem=recv_sems.at[3 * w + k],
                    device_id=frm, device_id_type=MESH)
                cp.wait_send()
                cp.wait_recv()

    outs = pl.pallas_call(
        body, name=name,
        in_specs=[_HBM] * (2 * n) + [_SEM, _SEM, _ANY], out_specs=[_HBM] * (2 * n),
        out_shape=[_hbm_like(a) for a in parts] + [_hbm_like(a) for a in lands],
        input_output_aliases={i: i for i in range(2 * n)},
        compiler_params=pltpu.CompilerParams(has_side_effects=_EFFECT),
    )(*parts, *lands, send, recv, after)
    return list(outs[:n]), list(outs[n:])


def _row_tile(t):
    return min(t, 256)


def _rms_fwd(name, x, g):
    t, d = x.shape
    tm = _row_tile(t)

    def epilogue(_, ins, outs):
        xv = ins[0][...]
        r = lax.rsqrt(jnp.mean(xv * xv, axis=-1, keepdims=True) + RMS_EPS)
        outs[0][...] = (xv * r * ins[1][...]).astype(BF16)

    row = pl.BlockSpec((tm, d), lambda i, j, k: (i, 0))
    vec = pl.BlockSpec((1, d), lambda i, j, k: (0, 0))
    return _fused(name, (t // tm, 1, 1), [(x, row), (g, vec)], [(_sds((t, d), BF16), row)], [], epilogue,
                  temp_bytes=4 * tm * d * 4)[0]


def _rms_bwd(name, x, g, dh, resid, deps=()):
    t, d = x.shape
    tm = _row_tile(t)

    def epilogue(_, ins, outs):
        xv, gv, dhv = ins[0][...], ins[1][...], ins[2][...]
        r = lax.rsqrt(jnp.mean(xv * xv, axis=-1, keepdims=True) + RMS_EPS)
        xh = xv * r
        u = dhv * gv
        dot = jnp.mean(u * xh, axis=-1, keepdims=True)
        outs[0][...] = ins[3][...] + r * (u - xh * dot)

        @pl.when(pl.program_id(0) == 0)
        def _():
            outs[1][...] = jnp.zeros_like(outs[1])

        outs[1][0:1, :] += jnp.sum(dhv * xh, axis=0, keepdims=True)

    row = pl.BlockSpec((tm, d), lambda i, j, k: (i, 0))
    vec = pl.BlockSpec((1, d), lambda i, j, k: (0, 0))
    acc = pl.BlockSpec((8, d), lambda i, j, k: (0, 0))
    return _fused(name, (t // tm, 1, 1), [(x, row), (g, vec), (dh, row), (resid, row)],
                  [(_sds((t, d), F32), row), (_sds((8, d), F32), acc)], [], epilogue,
                  temp_bytes=6 * tm * d * 4, semantics=("arbitrary", "arbitrary", "arbitrary"), deps=deps)


def _ffn_up(name, h, wgu):
    t, d = h.shape
    nb = wgu.shape[2]
    f = 4 * nb
    tm = min(t, 512)

    def body(h_ref, wg_ref, wu_ref, gu_ref, a_ref):
        hv = h_ref[...]
        for c0 in range(0, nb, MXU_COLS):
            cs = slice(c0, min(c0 + MXU_COLS, nb))
            g = jnp.dot(hv, wg_ref[:, cs], preferred_element_type=F32)
            u = jnp.dot(hv, wu_ref[:, cs], preferred_element_type=F32)
            gu_ref[0, :, cs] = g.astype(BF16)
            gu_ref[1, :, cs] = u.astype(BF16)
            a_ref[:, cs] = (g * _sigmoid(g) * u).astype(BF16)

    blocks = tm * d * 2 + 2 * d * nb * 2 + 3 * tm * nb * 2
    return pl.pallas_call(
        body, name=name, grid=(4, t // tm),
        in_specs=[pl.BlockSpec((tm, d), lambda j, i: (i, 0)),
                  pl.BlockSpec((None, d, nb), lambda j, i: (j, 0, 0)),
                  pl.BlockSpec((None, d, nb), lambda j, i: (j + 4, 0, 0))],
        out_specs=[pl.BlockSpec((2, tm, nb), lambda j, i: (0, i, j)),
                   pl.BlockSpec((tm, nb), lambda j, i: (i, j))],
        out_shape=[_sds((2, t, f), BF16), _sds((t, f), BF16)],
        compiler_params=_params(("parallel", "parallel"), blocks, 8 * tm * MXU_COLS * 4),
    )(h, wgu, wgu)


def _ffn_down(name, a, wd, x, target=None):
    t, f = a.shape
    d = wd.shape[1]
    tm = min(t, 512)
    tn = min(d, 1024)
    blk = pl.BlockSpec((tm, tn), lambda j, i, k: (i, j))
    ins = [(a, pl.BlockSpec((tm, f), lambda j, i, k: (i, 0))), (wd, pl.BlockSpec((f, tn), lambda j, i, k: (0, j))), (x, blk)]

    if target is None:
        def epilogue(acc, ins, outs):
            outs[0][...] = ins[2][...] + 0.5 * acc

        return _fused(name, (d // tn, t // tm, 1), ins, [(_sds((t, d), F32), blk)],
                      [(0, 1, NN)], epilogue, temp_bytes=2 * tm * tn * 4)[0]

    def epilogue(acc, ins, outs):
        e = ins[2][...] + 0.5 * acc - ins[3][...]
        outs[0][...] = e * (1.0 / d)

        @pl.when((pl.program_id(0) == 0) & (pl.program_id(1) == 0))
        def _():
            outs[1][...] = jnp.zeros_like(outs[1])

        part = jnp.sum(jnp.sum(e * e, axis=1, keepdims=True), axis=0, keepdims=True)
        outs[1][...] += jnp.broadcast_to(part, outs[1].shape)

    return _fused(name, (d // tn, t // tm, 1), ins + [(target, blk)],
                  [(_sds((t, d), F32), blk), (_sds((8, LANES), F32), pl.BlockSpec((8, LANES), lambda j, i, k: (0, 0)))],
                  [(0, 1, NN)], epilogue, temp_bytes=3 * tm * tn * 4,
                  semantics=("arbitrary", "arbitrary", "arbitrary"))


def _ffn_bwd_act(name, dy, wd, gu, deps=()):
    t, d = dy.shape
    f = wd.shape[0]
    nb = f // 4
    tm = min(t, 512)

    def body(dy_ref, wd_ref, gu_ref, *rest):
        dgu_ref, a_ref = rest[-2], rest[-1]
        dyv = dy_ref[...].astype(BF16)
        for c0 in range(0, nb, MXU_COLS):
            cs = slice(c0, min(c0 + MXU_COLS, nb))
            da = 0.5 * lax.dot_general(dyv, wd_ref[cs, :], NT, preferred_element_type=F32)
            g = gu_ref[0, :, cs].astype(F32)
            u = gu_ref[1, :, cs].astype(F32)
            s = _sigmoid(g)
            silu = g * s
            dgu_ref[0, :, cs] = (da * u * (s * (1.0 + g * (1.0 - s)))).astype(BF16)
            dgu_ref[1, :, cs] = (da * silu).astype(BF16)
            a_ref[:, cs] = (silu * u).astype(BF16)

    blocks = tm * d * 4 + nb * d * 2 + 5 * tm * nb * 2
    return pl.pallas_call(
        body, name=name, grid=(4, t // tm),
        in_specs=[pl.BlockSpec((tm, d), lambda j, i: (i, 0)),
                  pl.BlockSpec((nb, d), lambda j, i: (j, 0)),
                  pl.BlockSpec((2, tm, nb), lambda j, i: (0, i, j))] + [_ANY] * len(deps),
        out_specs=[pl.BlockSpec((2, tm, nb), lambda j, i: (0, i, j)), pl.BlockSpec((tm, nb), lambda j, i: (i, j))],
        out_shape=[_sds((2, t, f), BF16), _sds((t, f), BF16)],
        compiler_params=_params(("parallel", "parallel"), blocks, tm * d * 2 + 8 * tm * MXU_COLS * 4),
    )(dy, wd, gu, *deps)


def _ffn_bwd_dwd(name, a, dy, deps=(), side=None):
    t, f = a.shape
    d = dy.shape[1]
    tm = f // 4
    tn = min(d, 512)

    def epilogue(acc, ins, outs):
        outs[0][...] = (0.5 * acc).astype(BF16)

    return _fused(name, (4, d // tn, 1),
                  [(a, pl.BlockSpec((t, tm), lambda i, j, k: (0, i))),
                   (dy, pl.BlockSpec((t, tn), lambda i, j, k: (0, j)))],
                  [(_sds((f, d), BF16), pl.BlockSpec((tm, tn), lambda i, j, k: (i, j)))],
                  [(0, 1, TN)], epilogue, temp_bytes=t * tn * 2 + 2 * tm * tn * 4, deps=deps, side=side)


def _ffn_bwd_dh(name, dgu, wgu, deps=(), side=None):
    _, t, f = dgu.shape
    d, nb = wgu.shape[1], wgu.shape[2]
    tm = min(t, 512)

    def products(ins):
        return (lax.dot_general(ins[0][:, 0:nb], ins[1][0], NT, preferred_element_type=F32)
                + lax.dot_general(ins[0][:, nb:2 * nb], ins[1][1], NT, preferred_element_type=F32))

    def epilogue(acc, ins, outs):
        outs[0][...] = acc

    return _fused(name, (t // tm, 1, 4),
                  [(dgu, pl.BlockSpec((None, tm, 2 * nb), lambda i, j, k: (k // 2, i, k % 2))),
                   (wgu, pl.BlockSpec((2, d, nb), lambda i, j, k: (k, 0, 0)))],
                  [(_sds((t, d), F32), pl.BlockSpec((tm, d), lambda i, j, k: (i, 0)))],
                  products, epilogue, nk=4, acc_shape=(tm, d), temp_bytes=tm * d * 4, deps=deps, side=side)


def _ffn_bwd_dwgu(name, h, dgu, deps=(), side=None, rows=None):
    t, d = h.shape
    nb = dgu.shape[2] // 4
    tm = min(d, 512)
    row0, nrows = rows if rows is not None else (0, d)
    j0 = row0 // tm

    def epilogue(acc, ins, outs):
        outs[0][...] = acc.astype(BF16)

    return _fused(name, (N_DEV, nrows // tm, 1),
                  [(h, pl.BlockSpec((t, tm), lambda i, j, k: (0, j0 + j))),
                   (dgu, pl.BlockSpec((None, t, nb), lambda i, j, k: (i // 4, 0, i % 4)))],
                  [(_sds((N_DEV, nrows, nb), BF16), pl.BlockSpec((None, tm, nb), lambda i, j, k: (i, j, 0)))],
                  [(0, 1, TN)], epilogue, temp_bytes=2 * tm * nb * 4, deps=deps, side=side)


def _proj(h, w_in):
    t, d = h.shape
    nb = w_in.shape[3]
    tm = min(t, 512)

    def body(h_ref, w_ref, o_ref):
        hv = h_ref[...]
        o_ref[:, 0:nb] = jnp.dot(hv, w_ref[0], preferred_element_type=F32).astype(BF16)
        o_ref[:, nb:2 * nb] = jnp.dot(hv, w_ref[1], preferred_element_type=F32).astype(BF16)

    blocks = tm * d * 2 + 2 * d * nb * 2 + tm * 2 * nb * 4
    return pl.pallas_call(
        body, name="mix_proj", grid=(4, t // tm),
        in_specs=[pl.BlockSpec((tm, d), lambda j, i: (i, 0)),
                  pl.BlockSpec((None, 2, d, nb), lambda j, i: (j, 0, 0, 0))],
        out_specs=pl.BlockSpec((tm, 2 * nb), lambda j, i: (i, j)),
        out_shape=_sds((t, N_DEV * nb), BF16),
        compiler_params=_params(("parallel", "parallel"), blocks, 2 * tm * nb * 4),
    )(h, w_in)


def _shift_rows(u, k):
    t = u.shape[0]
    rolled = pltpu.roll(u, k % t, axis=0)
    row = lax.broadcasted_iota(jnp.int32, u.shape, 0)
    keep = (row >= k) if k > 0 else (row < t + k)
    return jnp.where(keep, rolled, 0.0)


def _conv_fwd(proj, conv_w):
    t = proj.shape[0]
    cw = conv_w.shape[1]
    tc = min(cw, 256)
    nc = cw // tc

    def epilogue(_, ins, outs):
        u = ins[2][...].astype(F32) * ins[0][...].astype(F32)
        w = ins[3][...]
        y = u * w[2:3, :] + _shift_rows(u, 1) * w[1:2, :] + _shift_rows(u, 2) * w[0:1, :]
        outs[0][...] = (ins[1][...].astype(F32) * y).astype(BF16)

    def col(seg):
        return pl.BlockSpec((t, tc), lambda i, j, k: (0, seg * nc + i))

    return _fused("conv_fwd", (nc, 1, 1),
                  [(proj, col(0)), (proj, col(1)), (proj, col(2)),
                   (conv_w, pl.BlockSpec((8, tc), lambda i, j, k: (0, i)))],
                  [(_sds((t, cw), BF16), pl.BlockSpec((t, tc), lambda i, j, k: (0, i)))],
                  [], epilogue, temp_bytes=6 * t * tc * 4)[0]


def _conv_bwd(proj, conv_w, dca, deps=()):
    t = proj.shape[0]
    cw = conv_w.shape[1]
    tc = min(cw, 256)
    nc = cw // tc

    def epilogue(_, ins, outs):
        xc, bg, cg = ins[0][...].astype(F32), ins[1][...].astype(F32), ins[2][...].astype(F32)
        w, dc = ins[3][...], ins[4][...]
        u = cg * xc
        u1, u2 = _shift_rows(u, 1), _shift_rows(u, 2)
        y = u * w[2:3, :] + u1 * w[1:2, :] + u2 * w[0:1, :]
        dconv = dc * bg
        du = dconv * w[2:3, :] + _shift_rows(dconv, -1) * w[1:2, :] + _shift_rows(dconv, -2) * w[0:1, :]
        outs[0][0] = (du * cg).astype(BF16)
        outs[0][1] = (dc * y).astype(BF16)
        outs[0][2] = (du * xc).astype(BF16)
        outs[1][...] = jnp.zeros_like(outs[1])
        outs[1][0:1, :] = jnp.sum(dconv * u2, axis=0, keepdims=True)
        outs[1][1:2, :] = jnp.sum(dconv * u1, axis=0, keepdims=True)
        outs[1][2:3, :] = jnp.sum(dconv * u, axis=0, keepdims=True)

    def col(seg):
        return pl.BlockSpec((t, tc), lambda i, j, k: (0, seg * nc + i))

    own = pl.BlockSpec((t, tc), lambda i, j, k: (0, i))
    wspec = pl.BlockSpec((8, tc), lambda i, j, k: (0, i))
    return _fused("conv_bwd", (nc, 1, 1),
                  [(proj, col(0)), (proj, col(1)), (proj, col(2)), (conv_w, wspec), (dca, own)],
                  [(_sds((3, t, cw), BF16), pl.BlockSpec((3, t, tc), lambda i, j, k: (0, 0, i))),
                   (_sds((8, cw), F32), wspec)],
                  [], epilogue, temp_bytes=10 * t * tc * 4, deps=deps)


def _split3(x):
    hi = x.astype(BF16)
    r1 = x - hi.astype(F32)
    mid = r1.astype(BF16)
    lo = (r1 - mid.astype(F32)).astype(BF16)
    return hi, mid, lo


def _head_selector(width):
    r = lax.broadcasted_iota(jnp.int32, (width, LANES), 0)
    c = lax.broadcasted_iota(jnp.int32, (width, LANES), 1)
    return (lax.shift_right_logical(r, 6) == c).astype(BF16)


def _head_sum(x, sel):
    return sum(jnp.dot(p, sel, preferred_element_type=F32) for p in _split3(x))


def _head_bcast(r, sel):
    return sum(lax.dot_general(p, sel, NT, preferred_element_type=F32) for p in _split3(r))


def _rope(x, c, sa, sb):
    n = x.shape[1]
    return x * c + pltpu.roll(x, n - ROT_DIM // 2, axis=1) * sa + pltpu.roll(x, ROT_DIM // 2, axis=1) * sb


def _rope_t(d, c, sa, sb):
    n = d.shape[1]
    return d * c + pltpu.roll(d * sa, ROT_DIM // 2, axis=1) + pltpu.roll(d * sb, n - ROT_DIM // 2, axis=1)


def _tile_lanes(tab, width):
    return tab if width == tab.shape[1] else jnp.tile(tab, (1, width // tab.shape[1]))


def _qk_prep(proj, gq, gk, rope_tabs, cw, kw):
    t = proj.shape[0]
    tm = _row_tile(t)

    def epilogue(_, ins, outs):
        c, sa, sb = ins[5][...], ins[6][...], ins[7][...]
        for src, gain, dst, width in ((0, 3, 0, cw), (1, 4, 1, kw)):
            xv = ins[src][...].astype(F32)
            sel = _head_selector(width)
            r = lax.rsqrt(_head_sum(xv * xv, sel) * (1.0 / HEAD_DIM) + RMS_EPS)
            xn = xv * _head_bcast(r, sel) * ins[gain][...]
            outs[dst][...] = _rope(xn, _tile_lanes(c, width), _tile_lanes(sa, width), _tile_lanes(sb, width)).astype(BF16)
        outs[2][...] = ins[2][...].astype(BF16)

    kblk = cw // kw
    tab = pl.BlockSpec((tm, LANES), lambda i, j, k: (i, 0))
    kspec = pl.BlockSpec((tm, kw), lambda i, j, k: (i, 0))
    return _fused("qk_prep", (t // tm, 1, 1),
                  [(proj, pl.BlockSpec((tm, cw), lambda i, j, k: (i, 3))),
                   (proj, pl.BlockSpec((tm, kw), lambda i, j, k: (i, 4 * kblk))),
                   (proj, pl.BlockSpec((tm, kw), lambda i, j, k: (i, 4 * kblk + 1))),
                   (gq, pl.BlockSpec((1, cw), lambda i, j, k: (0, 0))),
                   (gk, pl.BlockSpec((1, kw), lambda i, j, k: (0, 0))),
                   (rope_tabs[0], tab), (rope_tabs[1], tab), (rope_tabs[2], tab)],
                  [(_sds((t, cw), BF16), pl.BlockSpec((tm, cw), lambda i, j, k: (i, 0))),
                   (_sds((t, kw), BF16), kspec), (_sds((t, kw), BF16), kspec)],
                  [], epilogue, temp_bytes=12 * tm * cw * 4)


def _qk_prep_bwd(proj, gq, gk, rope_tabs, dq, dkc, dkp, dvc, dvp, cw, kw):
    t = proj.shape[0]
    tm = BLOCK
    nblk = t // tm

    def epilogue(_, ins, outs):
        c, sa, sb = ins[5][...], ins[6][...], ins[7][...]
        has_next = (pl.program_id(0) < nblk - 1).astype(F32)
        dk = ins[9][...] + has_next * ins[10][...]
        dv = ins[11][...] + has_next * ins[12][...]
        pieces = []
        for src, gain, dval, dst, width in ((0, 3, ins[8][...], 1, cw), (1, 4, dk, 2, kw)):
            xv, gv = ins[src][...].astype(F32), ins[gain][...]
            sel = _head_selector(width)
            r = _head_bcast(lax.rsqrt(_head_sum(xv * xv, sel) * (1.0 / HEAD_DIM) + RMS_EPS), sel)
            xh = xv * r
            dxn = _rope_t(dval, _tile_lanes(c, width), _tile_lanes(sa, width), _tile_lanes(sb, width))
            u = dxn * gv
            dot = _head_bcast(_head_sum(u * xh, sel), sel) * (1.0 / HEAD_DIM)
            pieces.append((r * (u - xh * dot)).astype(BF16))
            ri = lax.broadcasted_iota(jnp.int32, (width, LANES), 0)
            ci = lax.broadcasted_iota(jnp.int32, (width, LANES), 1)
            fold = (lax.bitwise_and(ri, HEAD_DIM - 1) == ci).astype(BF16)
            colsum = jnp.broadcast_to(jnp.sum(dxn * xh, axis=0, keepdims=True), (8, width))
            part = sum(jnp.dot(p, fold, preferred_element_type=F32) for p in _split3(colsum))

            @pl.when(pl.program_id(0) == 0)
            def _():
                outs[dst][...] = jnp.zeros_like(outs[dst])

            outs[dst][0:1, :] += part[0:1, :]
        outs[0][:, 0:cw] = pieces[0]
        outs[0][:, cw:cw + kw] = pieces[1]
        outs[0][:, cw + kw:cw + 2 * kw] = dv.astype(BF16)

    kblk = cw // kw
    tab = pl.BlockSpec((tm, LANES), lambda i, j, k: (i, 0))
    kcur = pl.BlockSpec((tm, kw), lambda i, j, k: (i, 0))
    knext = pl.BlockSpec((tm, kw), lambda i, j, k: (jnp.minimum(i + 1, nblk - 1), 0))
    acc = pl.BlockSpec((8, LANES), lambda i, j, k: (0, 0))
    return _fused("qk_prep_bwd", (nblk, 1, 1),
                  [(proj, pl.BlockSpec((tm, cw), lambda i, j, k: (i, 3))),
                   (proj, pl.BlockSpec((tm, kw), lambda i, j, k: (i, 4 * kblk))),
                   (proj, pl.BlockSpec((tm, kw), lambda i, j, k: (i, 4 * kblk + 1))),
                   (gq, pl.BlockSpec((1, cw), lambda i, j, k: (0, 0))),
                   (gk, pl.BlockSpec((1, kw), lambda i, j, k: (0, 0))),
                   (rope_tabs[0], tab), (rope_tabs[1], tab), (rope_tabs[2], tab),
                   (dq, pl.BlockSpec((tm, cw), lambda i, j, k: (i, 0))),
                   (dkc, kcur), (dkp, knext), (dvc, kcur), (dvp, knext)],
                  [(_sds((t, cw + 2 * kw), BF16), pl.BlockSpec((tm, cw + 2 * kw), lambda i, j, k: (i, 0))),
                   (_sds((8, LANES), F32), acc), (_sds((8, LANES), F32), acc)],
                  [], epilogue, temp_bytes=16 * tm * cw * 4, semantics=("arbitrary", "arbitrary", "arbitrary"))


def _attn_mask(n):
    key = lax.broadcasted_iota(jnp.int32, (2 * BLOCK, GROUP * BLOCK), 0)
    qry = lax.bitwise_and(lax.broadcasted_iota(jnp.int32, (2 * BLOCK, GROUP * BLOCK), 1), BLOCK - 1)
    return (key > qry) & (key <= qry + BLOCK) & ((key >= BLOCK) | (n > 0))


def _stack_heads(x, h):
    return jnp.concatenate([x[:, (h * GROUP + g) * HEAD_DIM:(h * GROUP + g + 1) * HEAD_DIM] for g in range(GROUP)], axis=0)


def _softmax_with_sink(q4, k2, sink_ref, h, valid):
    sink = jnp.concatenate([sink_ref[h * GROUP + g:h * GROUP + g + 1, :] for g in range(GROUP)], axis=1)
    s = lax.dot_general(k2, q4, NT, preferred_element_type=F32) * ATTN_SCALE
    s = jnp.where(valid, s, NEG_INF)
    m = jnp.maximum(jnp.max(s, axis=0, keepdims=True), sink)
    p = jnp.exp(s - m)
    es = jnp.exp(sink - m)
    inv = 1.0 / (jnp.sum(p, axis=0, keepdims=True) + es)
    return p * inv, es * inv


def _attn_fwd(qn, kn, vb, sink_rows):
    t, cw = qn.shape
    kw = kn.shape[1]
    nkv = kw // HEAD_DIM

    def body(q_ref, kp_ref, kc_ref, vp_ref, vc_ref, sink_ref, o_ref):
        valid = _attn_mask(pl.program_id(0))
        qv = q_ref[...]
        kp, kc, vp, vc = kp_ref[...], kc_ref[...], vp_ref[...], vc_ref[...]
        outs = []
        for h in range(nkv):
            hs = slice(h * HEAD_DIM, (h + 1) * HEAD_DIM)
            k2 = jnp.concatenate([kp[:, hs], kc[:, hs]], axis=0)
            v2 = jnp.concatenate([vp[:, hs], vc[:, hs]], axis=0)
            pn, _ = _softmax_with_sink(_stack_heads(qv, h), k2, sink_ref, h, valid)
            o4 = lax.dot_general(pn.astype(BF16), v2, TN, preferred_element_type=F32)
            outs += [o4[g * BLOCK:(g + 1) * BLOCK] for g in range(GROUP)]
        o_ref[...] = jnp.concatenate(outs, axis=-1).astype(BF16)

    cur = lambda n: (n, 0)
    prev = lambda n: (jnp.maximum(n - 1, 0), 0)
    return pl.pallas_call(
        body, name="attn_fwd", grid=(t // BLOCK,),
        in_specs=[pl.BlockSpec((BLOCK, cw), cur),
                  pl.BlockSpec((BLOCK, kw), prev), pl.BlockSpec((BLOCK, kw), cur),
                  pl.BlockSpec((BLOCK, kw), prev), pl.BlockSpec((BLOCK, kw), cur),
                  pl.BlockSpec(sink_rows.shape, lambda n: (0, 0))],
        out_specs=pl.BlockSpec((BLOCK, cw), cur),
        out_shape=_sds((t, cw), BF16),
        compiler_params=_params(("parallel",), BLOCK * (cw + 4 * kw) * 2 + BLOCK * cw * 2, 8 << 20),
    )(qn, kn, kn, vb, vb, sink_rows)


def _attn_bwd(qn, kn, vb, sink_rows, do):
    t, cw = qn.shape
    kw = kn.shape[1]
    nkv = kw // HEAD_DIM
    nq = nkv * GROUP

    def body(q_ref, kp_ref, kc_ref, vp_ref, vc_ref, sink_ref, do_ref,
             dq_ref, dkc_ref, dkp_ref, dvc_ref, dvp_ref, dsink_ref):
        n = pl.program_id(0)
        valid = _attn_mask(n)
        qv, dov = q_ref[...], do_ref[...]
        kp, kc, vp, vc = kp_ref[...], kc_ref[...], vp_ref[...], vc_ref[...]
        dqs, dks, dvs, dsinks = [], [], [], []
        for h in range(nkv):
            hs = slice(h * HEAD_DIM, (h + 1) * HEAD_DIM)
            k2 = jnp.concatenate([kp[:, hs], kc[:, hs]], axis=0)
            v2 = jnp.concatenate([vp[:, hs], vc[:, hs]], axis=0)
            q4 = _stack_heads(qv, h)
            dob = _stack_heads(dov, h).astype(BF16)
            pn, psink = _softmax_with_sink(q4, k2, sink_ref, h, valid)
            dpn = lax.dot_general(v2, dob, NT, preferred_element_type=F32)
            dvs.append(jnp.dot(pn.astype(BF16), dob, preferred_element_type=F32))
            delta = jnp.sum(pn * dpn, axis=0, keepdims=True)
            ds = (pn * (dpn - delta) * ATTN_SCALE).astype(BF16)
            dks.append(jnp.dot(ds, q4, preferred_element_type=F32))
            dq4 = lax.dot_general(ds, k2, TN, preferred_element_type=F32)
            dsink4 = -psink * delta
            for g in range(GROUP):
                dqs.append(dq4[g * BLOCK:(g + 1) * BLOCK])
                dsinks.append(jnp.broadcast_to(jnp.sum(dsink4[:, g * BLOCK:(g + 1) * BLOCK], axis=1, keepdims=True), (1, LANES)))
        dq_ref[...] = jnp.concatenate(dqs, axis=-1)
        dkp_ref[...] = jnp.concatenate([d[:BLOCK] for d in dks], axis=-1)
        dkc_ref[...] = jnp.concatenate([d[BLOCK:] for d in dks], axis=-1)
        dvp_ref[...] = jnp.concatenate([d[:BLOCK] for d in dvs], axis=-1)
        dvc_ref[...] = jnp.concatenate([d[BLOCK:] for d in dvs], axis=-1)

        @pl.when(n == 0)
        def _():
            dsink_ref[...] = jnp.zeros_like(dsink_ref)

        dsink_ref[...] += jnp.concatenate(dsinks, axis=0)

    cur = lambda n: (n, 0)
    prev = lambda n: (jnp.maximum(n - 1, 0), 0)
    kspec = pl.BlockSpec((BLOCK, kw), cur)
    return pl.pallas_call(
        body, name="attn_bwd", grid=(t // BLOCK,),
        in_specs=[pl.BlockSpec((BLOCK, cw), cur),
                  pl.BlockSpec((BLOCK, kw), prev), kspec,
                  pl.BlockSpec((BLOCK, kw), prev), kspec,
                  pl.BlockSpec(sink_rows.shape, lambda n: (0, 0)),
                  pl.BlockSpec((BLOCK, cw), cur)],
        out_specs=[pl.BlockSpec((BLOCK, cw), cur), kspec, kspec, kspec, kspec,
                   pl.BlockSpec((nq, LANES), lambda n: (0, 0))],
        out_shape=[_sds((t, cw), F32)] + [_sds((t, kw), F32)] * 4 + [_sds((nq, LANES), F32)],
        compiler_params=_params(("arbitrary",), BLOCK * (cw + 4 * kw) * 2 + 2 * BLOCK * cw * 4 + 4 * BLOCK * kw * 4, 12 << 20),
    )(qn, kn, kn, vb, vb, sink_rows, do)


def _mix_out(ca, o, woc, woa, proj):
    t, cw = ca.shape
    nb = woc.shape[2]
    d = N_DEV * nb
    tm = min(t, 1024)
    ga0 = (3 * cw + cw + 2 * (cw // 4)) // nb

    def body(ca_ref, o_ref, woc_ref, woa_ref, ga_ref, gb_ref, m_ref, ya_ref, yb_ref):
        ya = jnp.dot(ca_ref[...], woc_ref[...], preferred_element_type=F32)
        yb = jnp.dot(o_ref[...], woa_ref[...], preferred_element_type=F32)
        ya_ref[...] = ya.astype(BF16)
        yb_ref[...] = yb.astype(BF16)
        m_ref[...] = (_sigmoid(ga_ref[...].astype(F32)) * ya + _sigmoid(gb_ref[...].astype(F32)) * yb).astype(BF16)

    act = pl.BlockSpec((tm, cw), lambda i, j: (i, 0))
    wsp = pl.BlockSpec((None, cw, nb), lambda i, j: (j, 0, 0))
    osp = pl.BlockSpec((tm, nb), lambda i, j: (i, j))
    blocks = 2 * tm * cw * 2 + 2 * cw * nb * 2 + 2 * tm * nb * 4 + 3 * tm * nb * 2
    return pl.pallas_call(
        body, name="mix_out", grid=(t // tm, N_DEV),
        in_specs=[act, act, wsp, wsp,
                  pl.BlockSpec((tm, nb), lambda i, j: (i, ga0 + j)),
                  pl.BlockSpec((tm, nb), lambda i, j: (i, ga0 + N_DEV + j))],
        out_specs=[osp, osp, osp],
        out_shape=[_sds((t, d), BF16)] * 3,
        compiler_params=_params(("parallel", "parallel"), blocks, 6 * tm * nb * 4),
    )(ca, o, woc, woa, proj, proj)


def _mix_residual(merged, wo, x):
    t, d = x.shape
    tm = min(t, 512)

    def epilogue(acc, ins, outs):
        outs[0][...] = ins[2][...] + acc

    row = pl.BlockSpec((tm, d), lambda i, j, k: (i, 0))
    return _fused("mix_residual", (t // tm, 1, 1),
                  [(merged, row), (wo, pl.BlockSpec((d, d), lambda i, j, k: (0, 0))), (x, row)],
                  [(_sds((t, d), F32), row)], [(0, 1, NN)], epilogue, temp_bytes=2 * tm * d * 4)[0]


def _mix_bwd_gates(dx, wo, ya, yb, proj, cw):
    t, d = dx.shape
    tm = min(t, 1024)
    tn = min(d, 512)
    ga0 = (4 * cw + 2 * (cw // 4)) // tn

    def epilogue(acc, ins, outs):
        sa, sb = _sigmoid(ins[4][...].astype(F32)), _sigmoid(ins[5][...].astype(F32))
        outs[0][...] = (acc * sa).astype(BF16)
        outs[1][...] = (acc * sb).astype(BF16)
        outs[2][0] = (acc * ins[2][...].astype(F32) * sa * (1.0 - sa)).astype(BF16)
        outs[2][1] = (acc * ins[3][...].astype(F32) * sb * (1.0 - sb)).astype(BF16)

    blk = pl.BlockSpec((tm, tn), lambda i, j, k: (i, j))
    return _fused("mix_bwd_gates", (t // tm, d // tn, 1),
                  [(dx, pl.BlockSpec((tm, d), lambda i, j, k: (i, 0))),
                   (wo, pl.BlockSpec((tn, d), lambda i, j, k: (j, 0))),
                   (ya, blk), (yb, blk),
                   (proj, pl.BlockSpec((tm, tn), lambda i, j, k: (i, ga0 + j))),
                   (proj, pl.BlockSpec((tm, tn), lambda i, j, k: (i, ga0 + d // tn + j)))],
                  [(_sds((t, d), BF16), blk), (_sds((t, d), BF16), blk),
                   (_sds((2, t, d), BF16), pl.BlockSpec((2, tm, tn), lambda i, j, k: (0, i, j)))],
                  [(0, 1, NT)], epilogue, temp_bytes=8 * tm * tn * 4)


def _tn_matmul(name, a, b, tm, out_dtype=BF16):
    t, m = a.shape
    n = b.shape[1]
    tk = min(t, 512)

    def epilogue(acc, ins, outs):
        outs[0][...] = acc.astype(out_dtype)

    return _fused(name, (m // tm, 1, t // tk),
                  [(a, pl.BlockSpec((tk, tm), lambda i, j, k: (k, i))),
                   (b, pl.BlockSpec((tk, n), lambda i, j, k: (k, 0)))],
                  [(_sds((m, n), out_dtype), pl.BlockSpec((tm, n), lambda i, j, k: (i, 0)))],
                  [(0, 1, TN)], epilogue, nk=t // tk, acc_shape=(tm, n), temp_bytes=tm * n * 4)[0]


def _out_proj_bwd_act(dya, dyb, woc, woa, deps=()):
    t, d = dya.shape
    kdim, nb = woc.shape[1], woc.shape[2]
    tm = min(t, 512)

    def body(dya_ref, dyb_ref, woc_ref, woa_ref, *rest):
        for dy_ref, w_ref, o_ref in ((dya_ref, woc_ref, rest[-2]), (dyb_ref, woa_ref, rest[-1])):
            total = None
            for j in range(N_DEV):
                part = lax.dot_general(dy_ref[:, j * nb:(j + 1) * nb], w_ref[j], NT, preferred_element_type=F32)
                total = part if total is None else total + part
            o_ref[...] = total

    row = pl.BlockSpec((tm, d), lambda i: (i, 0))
    wsp = pl.BlockSpec((N_DEV, kdim, nb), lambda i: (0, 0, 0))
    osp = pl.BlockSpec((tm, kdim), lambda i: (i, 0))
    blocks = 2 * tm * d * 2 + 2 * N_DEV * kdim * nb * 2 + 2 * tm * kdim * 4
    return pl.pallas_call(
        body, name="mix_bwd_dca_do", grid=(t // tm,),
        in_specs=[row, row, wsp, wsp] + [_ANY] * len(deps), out_specs=[osp, osp],
        out_shape=[_sds((t, kdim), F32)] * 2,
        compiler_params=_params(("parallel",), blocks, 4 * tm * kdim * 4),
    )(dya, dyb, woc, woa, *deps)


def _out_proj_bwd_w(ca, o, dya, dyb, nb):
    t, kdim = ca.shape

    def body(ca_ref, o_ref, dya_ref, dyb_ref, dwoc_ref, dwoa_ref):
        dwoc_ref[...] = lax.dot_general(ca_ref[...], dya_ref[...], TN, preferred_element_type=F32).astype(BF16)
        dwoa_ref[...] = lax.dot_general(o_ref[...], dyb_ref[...], TN, preferred_element_type=F32).astype(BF16)

    act = pl.BlockSpec((t, kdim), lambda j: (0, 0))
    col = pl.BlockSpec((t, nb), lambda j: (0, j))
    osp = pl.BlockSpec((None, kdim, nb), lambda j: (j, 0, 0))
    blocks = 2 * t * kdim * 2 + 2 * t * nb * 2 + 2 * kdim * nb * 2
    return pl.pallas_call(
        body, name="mix_bwd_dwoc_dwoa", grid=(N_DEV,),
        in_specs=[act, act, col, col], out_specs=[osp, osp],
        out_shape=[_sds((N_DEV, kdim, nb), BF16)] * 2,
        compiler_params=_params(("parallel",), blocks, 4 * kdim * nb * 4),
    )(ca, o, dya, dyb)


def _proj_bwd_act(dproj, w_in, deps=()):
    t, n = dproj.shape
    d, nb = w_in.shape[2], w_in.shape[3]
    tm = min(t, 512)

    def epilogue(acc, ins, outs):
        outs[0][...] = acc

    def products(ins):
        return (lax.dot_general(ins[0][:, 0:nb], ins[1][0], NT, preferred_element_type=F32)
                + lax.dot_general(ins[0][:, nb:2 * nb], ins[1][1], NT, preferred_element_type=F32))

    return _fused("mix_bwd_dh", (t // tm, 1, 4),
                  [(dproj, pl.BlockSpec((tm, 2 * nb), lambda i, j, k: (i, k))),
                   (w_in, pl.BlockSpec((None, 2, d, nb), lambda i, j, k: (k, 0, 0, 0)))],
                  [(_sds((t, d), F32), pl.BlockSpec((tm, d), lambda i, j, k: (i, 0)))],
                  products, epilogue, nk=4, acc_shape=(tm, d), temp_bytes=tm * d * 4, deps=deps)[0]


def _proj_bwd_w(h, dproj):
    t, d = h.shape
    nb = dproj.shape[1] // N_DEV
    tm = min(d, 512)

    def body(h_ref, dp_ref, o_ref):
        hv = h_ref[...]
        o_ref[0] = lax.dot_general(hv, dp_ref[:, 0:nb], TN, preferred_element_type=F32).astype(BF16)
        o_ref[1] = lax.dot_general(hv, dp_ref[:, nb:2 * nb], TN, preferred_element_type=F32).astype(BF16)

    blocks = t * tm * 2 + t * 2 * nb * 2 + 2 * tm * nb * 2
    return pl.pallas_call(
        body, name="mix_bwd_dwin", grid=(4, d // tm),
        in_specs=[pl.BlockSpec((t, tm), lambda j, i: (0, i)),
                  pl.BlockSpec((t, 2 * nb), lambda j, i: (0, j))],
        out_specs=pl.BlockSpec((None, 2, tm, nb), lambda j, i: (j, 0, i, 0)),
        out_shape=_sds((4, 2, d, nb), BF16),
        compiler_params=_params(("parallel", "parallel"), blocks, 4 * tm * nb * 4),
    )(h, dproj)


def _adamw_math(w, g, m, v):
    m = ADAM_B1 * m + (1.0 - ADAM_B1) * g
    v = ADAM_B2 * v + (1.0 - ADAM_B2) * (g * g)
    m_hat = m / (1.0 - ADAM_B1 ** ADAM_STEP)
    v_hat = v / (1.0 - ADAM_B2 ** ADAM_STEP)
    delta = -ADAM_LR * (m_hat / (jnp.sqrt(v_hat) + ADAM_EPS) + ADAM_WD * w)
    return delta, m, v


def _adamw(name, parts, w, m, v, tr):
    r, c = w.shape

    def body(p_ref, w_ref, m_ref, v_ref, g_out, d_out, m_out, v_out):
        g = p_ref[0].astype(F32)
        for s in range(1, N_DEV):
            g = g + p_ref[s].astype(F32)
        delta, mn, vn = _adamw_math(w_ref[...], g, m_ref[...], v_ref[...])
        g_out[...] = g
        d_out[...] = delta
        m_out[...] = mn
        v_out[...] = vn

    blk = pl.BlockSpec((tr, c), lambda i: (i, 0))
    blocks = N_DEV * tr * c * parts.dtype.itemsize + 7 * tr * c * 4
    return pl.pallas_call(
        body, name=name, grid=(r // tr,),
        in_specs=[pl.BlockSpec((N_DEV, tr, c), lambda i: (0, i, 0)), blk, blk, blk],
        out_specs=[blk] * 4, out_shape=[_sds((r, c), F32)] * 4,
        compiler_params=_params(("parallel",), blocks, 6 * tr * c * 4),
    )(parts, w, m, v)


def _chip_sum(sums_ref):
    g = sums_ref[0].astype(F32)
    for k in range(1, 4):
        g = g + sums_ref[k].astype(F32)
    return g


def _adamw_chips(name, sums, w, m, v, tr, deps=(), row0=0, into=None):
    r, c = w.shape
    rs = sums.shape[1]
    i0 = row0 // tr
    n_pass = len(deps) + (4 if into is not None else 0)

    def body(sums_ref, w_ref, m_ref, v_ref, *rest):
        g_out, d_out, m_out, v_out = rest[n_pass:]
        g = _chip_sum(sums_ref)
        delta, mn, vn = _adamw_math(w_ref[...], g, m_ref[...], v_ref[...])
        g_out[...] = g
        d_out[...] = delta
        m_out[...] = mn
        v_out[...] = vn

    blk = pl.BlockSpec((tr, c), lambda i: (i0 + i, 0))
    blocks = 4 * tr * c * 2 + 7 * tr * c * 4
    passed = list(deps) + (list(into) if into is not None else [])
    aliases = {4 + len(deps) + q: q for q in range(4)} if into is not None else {}
    return pl.pallas_call(
        body, name=name, grid=(rs // tr,),
        in_specs=[pl.BlockSpec((4, tr, c), lambda i: (0, i, 0)), blk, blk, blk] + [_ANY] * n_pass,
        out_specs=[blk] * 4, out_shape=[_sds((r, c), F32)] * 4,
        input_output_aliases=aliases,
        compiler_params=_params(("parallel",), blocks, 6 * tr * c * 4),
    )(sums, w, m, v, *passed)


def _adamw_side(contrib, w, m, v, n_tiles, step_of):
    r, c = w.shape
    tr = r // n_tiles
    assert tr * n_tiles == r and tr % 16 == 0, (r, n_tiles)

    def tile(i, j, k):
        return jnp.minimum(step_of(i, j, k), n_tiles - 1)

    blk = pl.BlockSpec((tr, c), lambda i, j, k: (tile(i, j, k), 0))
    ins = [(contrib, pl.BlockSpec((4, tr, c), lambda i, j, k: (0, tile(i, j, k), 0))), (w, blk), (m, blk), (v, blk)]
    outs = [(_sds((r, c), F32), blk)] * 4

    def fn(in_refs, out_refs):
        @pl.when(step_of(pl.program_id(0), pl.program_id(1), pl.program_id(2)) < n_tiles)
        def _():
            g = _chip_sum(in_refs[0])
            delta, mn, vn = _adamw_math(in_refs[1][...], g, in_refs[2][...], in_refs[3][...])
            out_refs[0][...] = g
            out_refs[1][...] = delta
            out_refs[2][...] = mn
            out_refs[3][...] = vn

    return ins, outs, fn


def _rope_tables(t):
    half = ROT_DIM // 2
    inv_freq = 1.0 / (ROPE_THETA ** (jnp.arange(0, ROT_DIM, 2, dtype=F32) / ROT_DIM))
    ang = jnp.arange(t, dtype=F32)[:, None] * inv_freq[None, :]
    cos, sin = jnp.cos(ang), jnp.sin(ang)
    ones = jnp.ones((t, HEAD_DIM - ROT_DIM), F32)
    zeros = jnp.zeros((t, HEAD_DIM - half), F32)
    c = jnp.concatenate([cos, cos, ones], axis=1)
    sa = jnp.concatenate([-sin, zeros], axis=1)
    sb = jnp.concatenate([jnp.zeros((t, half), F32), sin, jnp.zeros((t, HEAD_DIM - ROT_DIM), F32)], axis=1)
    return tuple(jnp.tile(a, (1, LANES // HEAD_DIM)) for a in (c, sa, sb))


def _pad_rows(a, rows=8):
    return jnp.pad(a, ((0, rows - a.shape[0]), (0, 0)))


def kernel(x, g_ffn1, w_gu1, w_down1, g_mix, w_in, conv_w, q_norm_g, k_norm_g, sinks, w_out_conv, w_out_attn, w_o, g_ffn2, w_gu2, w_down2, loss_target, m_g_ffn1, m_w_gu1, m_w_down1, m_g_mix, m_w_in, m_conv_w, m_q_norm_g, m_k_norm_g, m_sinks, m_w_out_conv, m_w_out_attn, m_w_o, m_g_ffn2, m_w_gu2, m_w_down2, v_g_ffn1, v_w_gu1, v_w_down1, v_g_mix, v_w_in, v_conv_w, v_q_norm_g, v_k_norm_g, v_sinks, v_w_out_conv, v_w_out_attn, v_w_o, v_g_ffn2, v_w_gu2, v_w_down2):
    t, d = x.shape[1], x.shape[2]
    cw = d // 2
    kw = cw // GROUP
    nq = cw // HEAD_DIM
    xs, target = x.reshape(t, d), loss_target.reshape(t, d)
    me = 4 * lax.axis_index("x") + 2 * lax.axis_index("y") + lax.axis_index("c")

    big = {"w_gu1": w_gu1, "w_down1": w_down1, "w_in": w_in, "w_out_conv": w_out_conv,
           "w_out_attn": w_out_attn, "w_o": w_o, "w_gu2": w_gu2, "w_down2": w_down2}
    big_m = {"w_gu1": m_w_gu1, "w_down1": m_w_down1, "w_in": m_w_in, "w_out_conv": m_w_out_conv,
             "w_out_attn": m_w_out_attn, "w_o": m_w_o, "w_gu2": m_w_gu2, "w_down2": m_w_down2}
    big_v = {"w_gu1": v_w_gu1, "w_down1": v_w_down1, "w_in": v_w_in, "w_out_conv": v_w_out_conv,
             "w_out_attn": v_w_out_attn, "w_o": v_w_o, "w_gu2": v_w_gu2, "w_down2": v_w_down2}
    names = list(big)

    tiles = {"w_gu1": 256, "w_gu2": 256, "w_in": 256, "w_down1": 176, "w_down2": 176,
             "w_out_conv": 1024, "w_out_attn": 1024, "w_o": 128}

    def row_tile(n):
        r = big[n].shape[1]
        return tiles[n] if r % tiles[n] == 0 else r

    rs_shape = {n: big[n].shape[1:] for n in names}
    half = rs_shape["w_gu1"][0] // 2
    rs_shape["w_gu1_lo"] = rs_shape["w_gu1_hi"] = (half, rs_shape["w_gu1"][1])

    def add_tile(n):
        r, c = rs_shape[n]
        while r * c * 2 > (3 << 20) and r % 32 == 0:
            r //= 2
        return r

    me_arr = me.astype(jnp.int32).reshape(1)
    sources = [(n, big[n][0], BF16, row_tile(n)) for n in names] + [("conv_w", _pad_rows(conv_w[0]), F32, 8)]
    issue_order = [0, 1, 2, 8, 3, 4, 5, 6, 7]
    first = _place_shard("place_" + names[0], sources[0][1], BF16, me_arr, sources[0][3])
    started = [_gather_start("gather_start_first", [first])]
    early = {2: (big_m["w_in"][0], big_v["w_in"][0])}
    rest = [_place_shard("place_" + sources[i][0], sources[i][1], sources[i][2], me_arr, sources[i][3],
                         deps=(started[0][3],) + early.get(i, ())) for i in issue_order[1:]]
    started.append(_gather_start("gather_start_rest", rest))
    where = {0: (0, 0)}
    where.update({i: (1, p) for p, i in enumerate(issue_order[1:])})

    def fetch(tag, idxs, after):
        call = where[idxs[0]][0]
        send, recv, stacks, _ = started[call]
        positions = [where[i][1] for i in idxs]
        got = _gather_wait("gather_wait_" + tag, positions, send, recv, [stacks[p] for p in positions], after)
        return _forward_to_sibling("gather_forward_" + tag, got)

    rope_tabs = _rope_tables(t)
    gq = jnp.tile(q_norm_g, (1, nq))
    gk = jnp.tile(k_norm_g, (1, nq // GROUP))
    sink_rows = jnp.broadcast_to(sinks[0][:, None], (nq, LANES))

    wts = {}
    h1 = _rms_fwd("ffn1_norm", xs, g_ffn1)
    wts["w_gu1"], = fetch("gu1", [0], started[1][3])
    gu1, a1 = _ffn_up("ffn1_up", h1, wts["w_gu1"])
    wts["w_down1"], = fetch("down1", [1], a1)
    wd1 = wts["w_down1"].reshape(-1, d)
    x1 = _ffn_down("ffn1_down", a1, wd1, xs)
    h2 = _rms_fwd("mix_norm", x1, g_mix)
    wts["w_in"], conv_land = fetch("in", [2, 8], h2)
    w_in_full = wts["w_in"].reshape(4, 2, d, -1)
    conv_full = jnp.transpose(conv_land, (1, 0, 2)).reshape(8, cw)
    proj = _proj(h2, w_in_full)
    ca = _conv_fwd(proj, conv_full)
    qn, kn, vb = _qk_prep(proj, gq, gk, rope_tabs, cw, kw)
    o = _attn_fwd(qn, kn, vb, sink_rows)
    wts["w_out_conv"], wts["w_out_attn"] = fetch("out", [3, 4], o)
    merged, ya, yb = _mix_out(ca, o, wts["w_out_conv"], wts["w_out_attn"], proj)
    wts["w_o"], = fetch("o", [5], merged)
    wo = wts["w_o"].reshape(d, d)
    x2 = _mix_residual(merged, wo, x1)
    h3 = _rms_fwd("ffn2_norm", x2, g_ffn2)
    wts["w_gu2"], = fetch("gu2", [6], h3)
    gu2, a2 = _ffn_up("ffn2_up", h3, wts["w_gu2"])
    wts["w_down2"], = fetch("down2", [7], a2)
    wd2 = wts["w_down2"].reshape(-1, d)
    dy, sq = _ffn_down("ffn2_down", a2, wd2, x2, target=target)
    loss = lax.psum(sq[0, 0] * (0.5 / d), ("x", "y", "c"))

    place = jnp.stack([lax.axis_index("c"), 2 * lax.axis_index("x") + lax.axis_index("y")]).astype(jnp.int32)
    def pair_start(tag, group, grads, deps=()):
        stacks = [grads[n].reshape((4, 2) + rs_shape[n]) for n in group]
        lands = [lax.empty((4,) + rs_shape[n], BF16) for n in group]
        return _pair_start("rs_pair_start_" + tag, stacks, lands, deps)

    def chip_start(tag, group, pending, after):
        send, recv, stacks, lands, _ = pending
        stacks, lands = _pair_wait("rs_pair_wait_" + tag, send, recv, stacks, lands, after)
        added = [_pair_add("rs_pair_add_" + n, st, ld, place, add_tile(n)) for n, st, ld in zip(group, stacks, lands)]
        return _chip_start("rs_chip_start_" + tag, [a[0] for a in added], [a[1] for a in added])

    group_a, group_b, group_c = ["w_down2", "w_gu2"], ["w_o", "w_out_conv", "w_out_attn"], ["w_in"]
    group_d, group_e, group_f = ["w_down1"], ["w_gu1_lo"], ["w_gu1_hi"]
    g = {}
    dgu2, a2 = _ffn_bwd_act("ffn2_bwd_act", dy, wd2, gu2)
    g["w_down2"], = _ffn_bwd_dwd("ffn2_bwd_dwd", a2, dy)
    g["w_gu2"], = _ffn_bwd_dwgu("ffn2_bwd_dwgu", h3, dgu2)
    pend_a = pair_start("a", group_a, g)
    dh3, = _ffn_bwd_dh("ffn2_bwd_dh", dgu2, wts["w_gu2"], deps=(pend_a[4],))
    ring_a = chip_start("a", group_a, pend_a, dh3)
    dx2, dg_ffn2 = _rms_bwd("ffn2_bwd_rms", x2, g_ffn2, dh3, dy, deps=(ring_a[4],))

    dya, dyb, dgates = _mix_bwd_gates(dx2, wo, ya, yb, proj, cw)
    g["w_o"] = _tn_matmul("mix_bwd_dwo", merged, dx2, min(d, 1024))
    g["w_out_conv"], g["w_out_attn"] = _out_proj_bwd_w(ca, o, dya, dyb, d // N_DEV)
    pend_b = pair_start("b", group_b, g)
    dca, do = _out_proj_bwd_act(dya, dyb, wts["w_out_conv"], wts["w_out_attn"], deps=(pend_b[4],))
    ring_b = chip_start("b", group_b, pend_b, do)
    d3, dconv_w = _conv_bwd(proj, conv_full, dca, deps=(ring_b[4],))
    dq, dkc, dkp, dvc, dvp, dsink = _attn_bwd(qn, kn, vb, sink_rows, do)
    dqkv, dgq, dgk = _qk_prep_bwd(proj, gq, gk, rope_tabs, dq, dkc, dkp, dvc, dvp, cw, kw)
    dproj = jnp.concatenate([d3[0], d3[1], d3[2], dqkv, dgates[0], dgates[1]], axis=1)
    g["w_in"] = _proj_bwd_w(h2, dproj)
    pend_c = pair_start("c", group_c, g)
    dh2 = _proj_bwd_act(dproj, w_in_full, deps=(pend_c[4],))
    ring_c = chip_start("c", group_c, pend_c, dh2)
    dx1, dg_mix = _rms_bwd("mix_bwd_rms", x1, g_mix, dh2, dx2, deps=(ring_c[4],))

    big_out = {}
    arrived = {}

    def wait_group(tag, group, ring, after):
        send, recv, parts, lands2, _ = ring
        parts, lands2 = _chip_wait("rs_chip_wait_" + tag, send, recv, parts, lands2, after)
        arrived.update(dict(zip(group, lands2)))

    def update(n, after):
        res = _adamw_chips("adamw_" + n, arrived[n], big[n][0], big_m[n][0], big_v[n][0], row_tile(n), deps=(after,))
        big_out[n] = [a[None] for a in res]
        return res[0]

    def update_beside(n, n_tiles, step_of):
        return _adamw_side(arrived[n], big[n][0], big_m[n][0], big_v[n][0], n_tiles, step_of)

    def keep(n, res):
        big_out[n] = [a[None] for a in res]

    dgu1, a1 = _ffn_bwd_act("ffn1_bwd_act", dx1, wd1, gu1)
    wait_group("a", group_a, ring_a, a1)
    g["w_down1"], *res = _ffn_bwd_dwd("ffn1_bwd_dwd", a1, dx1,
                                       side=update_beside("w_down2", 11, lambda i, j, k: i * 4 + j))
    keep("w_down2", res)
    pend_d = pair_start("d", group_d, g)
    g["w_gu1_lo"], *res = _ffn_bwd_dwgu("ffn1_bwd_dwgu_lo", h1, dgu1, deps=(pend_d[4],), rows=(0, half),
                                         side=update_beside("w_gu2", 16, lambda i, j, k: i * 2 + j))
    keep("w_gu2", res)
    ring_d = chip_start("d", group_d, pend_d, g["w_gu1_lo"])
    pend_e = pair_start("e", group_e, g, deps=(ring_d[4],))
    g["w_gu1_hi"], = _ffn_bwd_dwgu("ffn1_bwd_dwgu_hi", h1, dgu1, deps=(pend_e[4],), rows=(half, half))
    ring_e = chip_start("e", group_e, pend_e, g["w_gu1_hi"])
    pend_f = pair_start("f", group_f, g, deps=(ring_e[4],))
    wait_group("b", group_b, ring_b, pend_f[4])
    after = pend_f[4]
    for n in group_b:
        after = update(n, after)
    ring_f = chip_start("f", group_f, pend_f, after)
    wait_group("c", group_c, ring_c, ring_f[4])
    dh1, *res = _ffn_bwd_dh("ffn1_bwd_dh", dgu1, wts["w_gu1"],
                             side=update_beside("w_in", 16, lambda i, j, k: i * 4 + k))
    keep("w_in", res)
    grad_x, dg_ffn1 = _rms_bwd("ffn1_bwd_rms", xs, g_ffn1, dh1, dx1)
    wait_group("d", group_d, ring_d, grad_x)
    after = update("w_down1", grad_x)
    n = "w_gu1"
    wait_group("e", group_e, ring_e, after)
    res = _adamw_chips("adamw_w_gu1_lo", arrived["w_gu1_lo"], big[n][0], big_m[n][0], big_v[n][0], row_tile(n), deps=(after,))
    wait_group("f", group_f, ring_f, res[0])
    res = _adamw_chips("adamw_w_gu1_hi", arrived["w_gu1_hi"], big[n][0], big_m[n][0], big_v[n][0], row_tile(n),
                       row0=half, into=res)
    keep(n, res)
    after = res[0]

    small = {"g_ffn1": dg_ffn1[0:1], "g_mix": dg_mix[0:1], "g_ffn2": dg_ffn2[0:1],
             "q_norm_g": dgq[0:1, :HEAD_DIM], "k_norm_g": dgk[0:1, :HEAD_DIM], "sinks": dsink[:, 0][None],
             "conv_w": dconv_w[0:CONV_K].reshape(1, -1)}
    small_w = {"g_ffn1": g_ffn1, "g_mix": g_mix, "g_ffn2": g_ffn2, "q_norm_g": q_norm_g, "k_norm_g": k_norm_g,
               "sinks": sinks, "conv_w": None}
    small_m = {"g_ffn1": m_g_ffn1, "g_mix": m_g_mix, "g_ffn2": m_g_ffn2, "q_norm_g": m_q_norm_g,
               "k_norm_g": m_k_norm_g, "sinks": m_sinks, "conv_w": m_conv_w}
    small_v = {"g_ffn1": v_g_ffn1, "g_mix": v_g_mix, "g_ffn2": v_g_ffn2, "q_norm_g": v_q_norm_g,
               "k_norm_g": v_k_norm_g, "sinks": v_sinks, "conv_w": v_conv_w}
    snames = list(small)
    widths = [small[n].shape[1] for n in snames]
    total = sum(widths)
    rows = -(-total // LANES)
    rows = -(-rows // 8) * 8

    def pack(vals):
        flat = jnp.concatenate([v.reshape(1, -1) for v in vals], axis=1)
        return jnp.pad(flat, ((0, 0), (0, rows * LANES - total))).reshape(rows, LANES)

    csh = cw // N_DEV

    def place_conv(local, fill):
        full = jnp.full((CONV_K, cw), fill, F32)
        return lax.dynamic_update_slice(full, local, (0, me * csh)).reshape(1, -1)

    pw = pack([small_w[n] if n != "conv_w" else place_conv(conv_w[0], 0.0) for n in snames])
    pm = pack([small_m[n] if n != "conv_w" else place_conv(m_conv_w[0], 0.0) for n in snames])
    pv = pack([small_v[n] if n != "conv_w" else place_conv(v_conv_w[0], 1.0) for n in snames])
    parts = _exchange("gather_small_grads", [pack([small[n] for n in snames])], gather=True, deps=(after,))[0]
    sg, sd, sm, sv = [a.reshape(1, -1) for a in _adamw("adamw_small", parts, pw, pm, pv, rows)]

    def unpack(flat, n):
        off = sum(widths[:snames.index(n)])
        piece = flat[:, off:off + widths[snames.index(n)]]
        if n == "conv_w":
            piece = lax.dynamic_slice(piece.reshape(CONV_K, cw), (0, me * csh), (CONV_K, csh))[None]
        return piece

    order = ["g_ffn1", "w_gu1", "w_down1", "g_mix", "w_in", "conv_w", "q_norm_g", "k_norm_g", "sinks",
             "w_out_conv", "w_out_attn", "w_o", "g_ffn2", "w_gu2", "w_down2"]
    outs = [loss, grad_x[None]]
    for idx, flat in enumerate((sg, sd, sm, sv)):
        for n in order:
            outs.append(big_out[n][idx] if n in big_out else unpack(flat, n))
    return tuple(outs)
```

```python
import jax
import jax.numpy as jnp
from jax import lax
from jax.experimental import pallas as pl
from jax.experimental.pallas import tpu as pltpu

F32 = jnp.float32
BF16 = jnp.bfloat16

N_DEV = 8
HEAD_DIM = 64
GROUP = 4
BLOCK = 128
ROT_DIM = 16
ROPE_THETA = 500000.0
RMS_EPS = 1e-6
NEG_INF = -1e30
ATTN_SCALE = HEAD_DIM ** -0.5
CONV_K = 3
LANES = 128
MXU_COLS = 256
VMEM_BYTES_V7X = 64 * 1024 * 1024
VMEM_CAP = VMEM_BYTES_V7X - 6 * 1024 * 1024

ADAM_LR = 0.001
ADAM_B1 = 0.9
ADAM_B2 = 0.999
ADAM_EPS = 1e-08
ADAM_WD = 0.01
ADAM_STEP = 10

NN = (((1,), (0,)), ((), ()))
NT = (((1,), (1,)), ((), ()))
TN = (((0,), (0,)), ((), ()))

MESH = pl.DeviceIdType.MESH


def _nbytes(shape, dtype):
    n = 1
    for s in shape:
        if s is not None:
            n *= s
    return n * jnp.dtype(dtype).itemsize


def _params(semantics, block_bytes, temp_bytes):
    assert 2 * block_bytes + temp_bytes <= VMEM_CAP, (block_bytes, temp_bytes)
    return pltpu.CompilerParams(dimension_semantics=semantics, vmem_limit_bytes=VMEM_CAP)


def _fused(name, grid, ins, outs, dots, epilogue, *, nk=1, acc_shape=None, temp_bytes=0,
           semantics=("parallel", "parallel", "arbitrary"), deps=(), side=None):
    n_main_in, n_main_out = len(ins), len(outs)
    if side is not None:
        ins, outs = list(ins) + list(side[0]), list(outs) + list(side[1])
    n_in, n_out = len(ins), len(outs)
    n_dep = len(deps)

    def body(*refs):
        in_refs, out_refs = refs[:n_in], refs[n_in + n_dep:n_in + n_dep + n_out]
        scratch = refs[n_in + n_dep + n_out:]
        if side is not None:
            side[2](in_refs[n_main_in:], out_refs[n_main_out:])

        def products():
            if callable(dots):
                return dots(in_refs)
            total = None
            for ai, bi, contract in dots:
                a, b = in_refs[ai][...], in_refs[bi][...]
                a = a if a.dtype == BF16 else a.astype(BF16)
                b = b if b.dtype == BF16 else b.astype(BF16)
                p = lax.dot_general(a, b, contract, preferred_element_type=F32)
                total = p if total is None else total + p
            return total

        if nk == 1:
            epilogue(products() if dots else None, in_refs, out_refs)
        else:
            acc = scratch[0]
            k = pl.program_id(2)

            @pl.when(k == 0)
            def _():
                acc[...] = jnp.zeros_like(acc)

            acc[...] += products()

            @pl.when(k == nk - 1)
            def _():
                epilogue(acc[...], in_refs, out_refs)

    block_bytes = sum(_nbytes(spec.block_shape, a.dtype) for a, spec in ins)
    block_bytes += sum(_nbytes(spec.block_shape, s.dtype) for s, spec in outs)
    scratch_shapes = []
    if nk > 1:
        scratch_shapes.append(pltpu.VMEM(acc_shape, F32))
        temp_bytes += _nbytes(acc_shape, F32)
    res = pl.pallas_call(
        body, name=name, grid=grid,
        in_specs=[spec for _, spec in ins] + [pl.BlockSpec(memory_space=pl.ANY)] * n_dep,
        out_specs=[spec for _, spec in outs],
        out_shape=[s for s, _ in outs],
        scratch_shapes=scratch_shapes,
        compiler_params=_params(semantics, block_bytes, temp_bytes),
    )(*[a for a, _ in ins], *deps)
    return res


def _sds(shape, dtype):
    return jax.ShapeDtypeStruct(shape, dtype)


def _sigmoid(x):
    return jax.nn.sigmoid(x)


def _all_gather_small(name, shard, deps=()):
    n_dep = len(deps)

    def body(src, *rest):
        dst, send_sems, recv_sems, local_sem = rest[n_dep:]
        x, y, c = lax.axis_index("x"), lax.axis_index("y"), lax.axis_index("c")
        me = 4 * x + 2 * y + c
        copies = [pltpu.make_async_copy(src, dst.at[me], local_sem)]
        for k in range(1, N_DEV):
            peer = ((1 - x) if (k & 4) else x, (1 - y) if (k & 2) else y, (1 - c) if (k & 1) else c)
            copies.append(pltpu.make_async_remote_copy(
                src_ref=src, dst_ref=dst.at[me], send_sem=send_sems.at[k - 1], recv_sem=recv_sems.at[k - 1],
                device_id=peer, device_id_type=MESH))
        for cp in copies:
            cp.start()
        for cp in copies:
            cp.wait()

    hbm = pl.BlockSpec(memory_space=pltpu.HBM)
    return pl.pallas_call(
        body, name=name,
        in_specs=[hbm] + [pl.BlockSpec(memory_space=pl.ANY)] * n_dep, out_specs=hbm,
        out_shape=_sds((N_DEV,) + shard.shape, shard.dtype),
        scratch_shapes=[pltpu.SemaphoreType.DMA((N_DEV - 1,)), pltpu.SemaphoreType.DMA((N_DEV - 1,)),
                        pltpu.SemaphoreType.DMA],
    )(shard, *deps)


_HBM = pl.BlockSpec(memory_space=pltpu.HBM)
_SEM = pl.BlockSpec(memory_space=pltpu.SEMAPHORE)
_ANY = pl.BlockSpec(memory_space=pl.ANY)
_EFFECT = pltpu.SideEffectType.DATAFLOW_SIDE_EFFECTING
N_TARGETS = 4


def _mesh_pos():
    return lax.axis_index("x"), lax.axis_index("y"), lax.axis_index("c")


def _chip_peers(x, y, c):
    return [(1 - x, y, c), (x, 1 - y, c), (1 - x, 1 - y, c)]


def _dev_index(pos):
    return 4 * pos[0] + 2 * pos[1] + pos[2]


def _hbm_like(a):
    return pltpu.HBM(a.shape, a.dtype)


def _place_shard(name, w, out_dtype, me, tr, deps=()):
    r, c = w.shape
    n_dep = len(deps)

    def body(me_ref, w_ref, *rest):
        rest[n_dep][...] = w_ref[...].astype(out_dtype)

    grid_spec = pltpu.PrefetchScalarGridSpec(
        num_scalar_prefetch=1, grid=(r // tr,),
        in_specs=[pl.BlockSpec((tr, c), lambda i, me_ref: (i, 0))] + [_ANY] * n_dep,
        out_specs=pl.BlockSpec((None, tr, c), lambda i, me_ref: (me_ref[0], i, 0)))
    return pl.pallas_call(
        body, name=name, grid_spec=grid_spec, out_shape=_sds((N_DEV, r, c), out_dtype),
        compiler_params=_params(("parallel",), tr * c * 6, tr * c * 4),
    )(me, w, *deps)


def _gather_start(name, lands):
    n = len(lands)

    def body(*refs):
        bufs = refs[:n]
        send, recv = refs[n], refs[n + 1]
        token = refs[-1]
        x, y, c = _mesh_pos()
        me = _dev_index((x, y, c))
        targets = [(x, y, 1 - c)] + _chip_peers(x, y, c)
        for w in range(n):
            for k, to in enumerate(targets):
                pltpu.make_async_remote_copy(
                    src_ref=bufs[w].at[me], dst_ref=bufs[w].at[me],
                    send_sem=send.at[N_TARGETS * w + k], recv_sem=recv.at[N_TARGETS * w + k],
                    device_id=to, device_id_type=MESH).start()
        token[...] = jnp.zeros_like(token)

    sems = pltpu.SemaphoreType.DMA((N_TARGETS * n,))
    outs = pl.pallas_call(
        body, name=name,
        in_specs=[_HBM] * n, out_specs=[_SEM, _SEM] + [_HBM] * n + [_token_spec()],
        out_shape=[sems, sems] + [_hbm_like(a) for a in lands] + [_sds((8, LANES), F32)],
        input_output_aliases={i: 2 + i for i in range(n)},
        compiler_params=pltpu.CompilerParams(has_side_effects=_EFFECT),
    )(*lands)
    return outs[0], outs[1], list(outs[2:2 + n]), outs[-1]


def _gather_wait(name, positions, send, recv, lands, after):
    m = len(positions)

    def body(*refs):
        bufs = refs[:m]
        send_sems, recv_sems = refs[m], refs[m + 1]
        x, y, c = _mesh_pos()
        me = _dev_index((x, y, c))
        sources = [(x, y, 1 - c)] + _chip_peers(x, y, c)
        for j, w in enumerate(positions):
            for k, frm in enumerate(sources):
                cp = pltpu.make_async_remote_copy(
                    src_ref=bufs[j].at[me], dst_ref=bufs[j].at[_dev_index(frm)],
                    send_sem=send_sems.at[N_TARGETS * w + k], recv_sem=recv_sems.at[N_TARGETS * w + k],
                    device_id=frm, device_id_type=MESH)
                cp.wait_send()
                cp.wait_recv()

    outs = pl.pallas_call(
        body, name=name,
        in_specs=[_HBM] * m + [_SEM, _SEM, _ANY], out_specs=[_HBM] * m,
        out_shape=[_hbm_like(a) for a in lands],
        input_output_aliases={i: i for i in range(m)},
        compiler_params=pltpu.CompilerParams(has_side_effects=_EFFECT),
    )(*lands, send, recv, after)
    return list(outs)


def _forward_to_sibling(name, lands):
    m = len(lands)

    def body(*refs):
        bufs = refs[m:2 * m]
        send_sems, recv_sems = refs[2 * m], refs[2 * m + 1]
        x, y, c = _mesh_pos()
        copies = []
        for j in range(m):
            for k, chip in enumerate(_chip_peers(x, y, c)):
                block = bufs[j].at[_dev_index(chip)]
                cp = pltpu.make_async_remote_copy(
                    src_ref=block, dst_ref=block,
                    send_sem=send_sems.at[3 * j + k], recv_sem=recv_sems.at[3 * j + k],
                    device_id=(x, y, 1 - c), device_id_type=MESH)
                cp.start()
                copies.append(cp)
        for cp in copies:
            cp.wait()

    outs = pl.pallas_call(
        body, name=name,
        in_specs=[_HBM] * m, out_specs=[_HBM] * m,
        out_shape=[_sds(a.shape, a.dtype) for a in lands],
        input_output_aliases={i: i for i in range(m)},
        scratch_shapes=[pltpu.SemaphoreType.DMA((3 * m,)), pltpu.SemaphoreType.DMA((3 * m,))],
    )(*lands)
    return list(outs)


def _token_spec():
    return pl.BlockSpec(memory_space=pltpu.VMEM)


def _pair_start(name, stacks, lands, deps=()):
    n = len(stacks)
    n_dep = len(deps)

    def body(*refs):
        srcs, dsts = refs[:n], refs[n:2 * n]
        send, recv = refs[2 * n + n_dep], refs[2 * n + n_dep + 1]
        token = refs[-1]
        x, y, c = _mesh_pos()
        for w in range(n):
            for chip in range(4):
                pltpu.make_async_remote_copy(
                    src_ref=srcs[w].at[chip, 1 - c], dst_ref=dsts[w].at[chip],
                    send_sem=send.at[4 * w + chip], recv_sem=recv.at[4 * w + chip],
                    device_id=(x, y, 1 - c), device_id_type=MESH).start()
        token[...] = jnp.zeros_like(token)

    sems = pltpu.SemaphoreType.DMA((4 * n,))
    outs = pl.pallas_call(
        body, name=name,
        in_specs=[_HBM] * (2 * n) + [_ANY] * n_dep, out_specs=[_SEM, _SEM] + [_HBM] * (2 * n) + [_token_spec()],
        out_shape=[sems, sems] + [_hbm_like(a) for a in stacks] + [_hbm_like(a) for a in lands] + [_sds((8, LANES), F32)],
        input_output_aliases={i: 2 + i for i in range(2 * n)},
        compiler_params=pltpu.CompilerParams(has_side_effects=_EFFECT),
    )(*stacks, *lands, *deps)
    return outs[0], outs[1], list(outs[2:2 + n]), list(outs[2 + n:2 + 2 * n]), outs[-1]


def _pair_wait(name, send, recv, stacks, lands, after):
    n = len(stacks)

    def body(*refs):
        srcs, dsts = refs[:n], refs[n:2 * n]
        send_sems, recv_sems = refs[2 * n], refs[2 * n + 1]
        x, y, c = _mesh_pos()
        for w in range(n):
            for chip in range(4):
                cp = pltpu.make_async_remote_copy(
                    src_ref=srcs[w].at[chip, 1 - c], dst_ref=dsts[w].at[chip],
                    send_sem=send_sems.at[4 * w + chip], recv_sem=recv_sems.at[4 * w + chip],
                    device_id=(x, y, 1 - c), device_id_type=MESH)
                cp.wait_send()
                cp.wait_recv()

    outs = pl.pallas_call(
        body, name=name,
        in_specs=[_HBM] * (2 * n) + [_SEM, _SEM, _ANY], out_specs=[_HBM] * (2 * n),
        out_shape=[_hbm_like(a) for a in stacks] + [_hbm_like(a) for a in lands],
        input_output_aliases={i: i for i in range(2 * n)},
        compiler_params=pltpu.CompilerParams(has_side_effects=_EFFECT),
    )(*stacks, *lands, send, recv, after)
    return list(outs[:n]), list(outs[n:])


def _pair_add(name, stack, land, place, tr):
    _, _, r, c = stack.shape

    def body(place_ref, a_ref, b_ref, sums_ref, slots_ref):
        total = (a_ref[...].astype(F32) + b_ref[...].astype(F32)).astype(BF16)
        sums_ref[...] = total

        @pl.when(pl.program_id(1) == place_ref[1])
        def _():
            slots_ref[...] = total

    grid_spec = pltpu.PrefetchScalarGridSpec(
        num_scalar_prefetch=1, grid=(r // tr, 4),
        in_specs=[pl.BlockSpec((None, None, tr, c), lambda i, k, place_ref: (k, place_ref[0], i, 0)),
                  pl.BlockSpec((None, tr, c), lambda i, k, place_ref: (k, i, 0))],
        out_specs=[pl.BlockSpec((None, tr, c), lambda i, k, place_ref: (k, i, 0)),
                   pl.BlockSpec((None, tr, c), lambda i, k, place_ref: (place_ref[1], i, 0))])
    return pl.pallas_call(
        body, name=name, grid_spec=grid_spec, out_shape=[_sds((4, r, c), BF16)] * 2,
        compiler_params=_params(("parallel", "arbitrary"), 4 * tr * c * 2, 3 * tr * c * 4),
    )(place, stack, land)


def _chip_start(name, parts, lands):
    n = len(parts)

    def body(*refs):
        srcs, dsts = refs[:n], refs[n:2 * n]
        send, recv = refs[2 * n], refs[2 * n + 1]
        token = refs[-1]
        x, y, c = _mesh_pos()
        for w in range(n):
            for k, to in enumerate(_chip_peers(x, y, c)):
                pltpu.make_async_remote_copy(
                    src_ref=srcs[w].at[2 * to[0] + to[1]], dst_ref=dsts[w].at[2 * x + y],
                    send_sem=send.at[3 * w + k], recv_sem=recv.at[3 * w + k],
                    device_id=to, device_id_type=MESH).start()
        token[...] = jnp.zeros_like(token)

    sems = pltpu.SemaphoreType.DMA((3 * n,))
    outs = pl.pallas_call(
        body, name=name,
        in_specs=[_HBM] * (2 * n), out_specs=[_SEM, _SEM] + [_HBM] * (2 * n) + [_token_spec()],
        out_shape=[sems, sems] + [_hbm_like(a) for a in parts] + [_hbm_like(a) for a in lands] + [_sds((8, LANES), F32)],
        input_output_aliases={i: 2 + i for i in range(2 * n)},
        compiler_params=pltpu.CompilerParams(has_side_effects=_EFFECT),
    )(*parts, *lands)
    return outs[0], outs[1], list(outs[2:2 + n]), list(outs[2 + n:2 + 2 * n]), outs[-1]


def _chip_wait(name, send, recv, parts, lands, after):
    n = len(parts)

    def body(*refs):
        srcs, dsts = refs[:n], refs[n:2 * n]
        send_sems, recv_sems = refs[2 * n], refs[2 * n + 1]
        x, y, c = _mesh_pos()
        for w in range(n):
            for k, frm in enumerate(_chip_peers(x, y, c)):
                chip = 2 * frm[0] + frm[1]
                cp = pltpu.make_async_remote_copy(
                    src_ref=srcs[w].at[chip], dst_ref=dsts[w].at[chip],
                    send_sem=send_sems.at[3 * w + k], recv_sem=recv_sems.at[3 * w + k],
                    device_id=frm, device_id_type=MESH)
                cp.wait_send()
                cp.wait_recv()

    outs = pl.pallas_call(
        body, name=name,
        in_specs=[_HBM] * (2 * n) + [_SEM, _SEM, _ANY], out_specs=[_HBM] * (2 * n),
        out_shape=[_hbm_like(a) for a in parts] + [_hbm_like(a) for a in lands],
        input_output_aliases={i: i for i in range(2 * n)},
        compiler_params=pltpu.CompilerParams(has_side_effects=_EFFECT),
    )(*parts, *lands, send, recv, after)
    return list(outs[:n]), list(outs[n:])


def _row_tile(t):
    return min(t, 256)


def _rms_fwd(name, x, g):
    t, d = x.shape
    tm = _row_tile(t)

    def epilogue(_, ins, outs):
        xv = ins[0][...]
        r = lax.rsqrt(jnp.mean(xv * xv, axis=-1, keepdims=True) + RMS_EPS)
        outs[0][...] = (xv * r * ins[1][...]).astype(BF16)

    row = pl.BlockSpec((tm, d), lambda i, j, k: (i, 0))
    vec = pl.BlockSpec((1, d), lambda i, j, k: (0, 0))
    return _fused(name, (t // tm, 1, 1), [(x, row), (g, vec)], [(_sds((t, d), BF16), row)], [], epilogue,
                  temp_bytes=4 * tm * d * 4)[0]


def _rms_bwd(name, x, g, dh, resid, deps=()):
    t, d = x.shape
    tm = _row_tile(t)

    def epilogue(_, ins, outs):
        xv, gv, dhv = ins[0][...], ins[1][...], ins[2][...]
        r = lax.rsqrt(jnp.mean(xv * xv, axis=-1, keepdims=True) + RMS_EPS)
        xh = xv * r
        u = dhv * gv
        dot = jnp.mean(u * xh, axis=-1, keepdims=True)
        outs[0][...] = ins[3][...] + r * (u - xh * dot)

        @pl.when(pl.program_id(0) == 0)
        def _():
            outs[1][...] = jnp.zeros_like(outs[1])

        outs[1][0:1, :] += jnp.sum(dhv * xh, axis=0, keepdims=True)

    row = pl.BlockSpec((tm, d), lambda i, j, k: (i, 0))
    vec = pl.BlockSpec((1, d), lambda i, j, k: (0, 0))
    acc = pl.BlockSpec((8, d), lambda i, j, k: (0, 0))
    return _fused(name, (t // tm, 1, 1), [(x, row), (g, vec), (dh, row), (resid, row)],
                  [(_sds((t, d), F32), row), (_sds((8, d), F32), acc)], [], epilogue,
                  temp_bytes=6 * tm * d * 4, semantics=("arbitrary", "arbitrary", "arbitrary"), deps=deps)


def _ffn_up(name, h, wgu):
    t, d = h.shape
    nb = wgu.shape[2]
    f = 4 * nb
    tm = min(t, 512)

    def body(h_ref, wg_ref, wu_ref, gu_ref, a_ref):
        hv = h_ref[...]
        for c0 in range(0, nb, MXU_COLS):
            cs = slice(c0, min(c0 + MXU_COLS, nb))
            g = jnp.dot(hv, wg_ref[:, cs], preferred_element_type=F32)
            u = jnp.dot(hv, wu_ref[:, cs], preferred_element_type=F32)
            gu_ref[0, :, cs] = g.astype(BF16)
            gu_ref[1, :, cs] = u.astype(BF16)
            a_ref[:, cs] = (g * _sigmoid(g) * u).astype(BF16)

    blocks = tm * d * 2 + 2 * d * nb * 2 + 3 * tm * nb * 2
    return pl.pallas_call(
        body, name=name, grid=(4, t // tm),
        in_specs=[pl.BlockSpec((tm, d), lambda j, i: (i, 0)),
                  pl.BlockSpec((None, d, nb), lambda j, i: (j, 0, 0)),
                  pl.BlockSpec((None, d, nb), lambda j, i: (j + 4, 0, 0))],
        out_specs=[pl.BlockSpec((2, tm, nb), lambda j, i: (0, i, j)),
                   pl.BlockSpec((tm, nb), lambda j, i: (i, j))],
        out_shape=[_sds((2, t, f), BF16), _sds((t, f), BF16)],
        compiler_params=_params(("parallel", "parallel"), blocks, 8 * tm * MXU_COLS * 4),
    )(h, wgu, wgu)


def _ffn_down(name, a, wd, x, target=None):
    t, f = a.shape
    d = wd.shape[1]
    tm = min(t, 512)
    tn = min(d, 1024)
    blk = pl.BlockSpec((tm, tn), lambda j, i, k: (i, j))
    ins = [(a, pl.BlockSpec((tm, f), lambda j, i, k: (i, 0))), (wd, pl.BlockSpec((f, tn), lambda j, i, k: (0, j))), (x, blk)]

    if target is None:
        def epilogue(acc, ins, outs):
            outs[0][...] = ins[2][...] + 0.5 * acc

        return _fused(name, (d // tn, t // tm, 1), ins, [(_sds((t, d), F32), blk)],
                      [(0, 1, NN)], epilogue, temp_bytes=2 * tm * tn * 4)[0]

    def epilogue(acc, ins, outs):
        e = ins[2][...] + 0.5 * acc - ins[3][...]
        outs[0][...] = e * (1.0 / d)

        @pl.when((pl.program_id(0) == 0) & (pl.program_id(1) == 0))
        def _():
            outs[1][...] = jnp.zeros_like(outs[1])

        part = jnp.sum(jnp.sum(e * e, axis=1, keepdims=True), axis=0, keepdims=True)
        outs[1][...] += jnp.broadcast_to(part, outs[1].shape)

    return _fused(name, (d // tn, t // tm, 1), ins + [(target, blk)],
                  [(_sds((t, d), F32), blk), (_sds((8, LANES), F32), pl.BlockSpec((8, LANES), lambda j, i, k: (0, 0)))],
                  [(0, 1, NN)], epilogue, temp_bytes=3 * tm * tn * 4,
                  semantics=("arbitrary", "arbitrary", "arbitrary"))


def _ffn_bwd_act(name, dy, wd, gu, deps=()):
    t, d = dy.shape
    f = wd.shape[0]
    nb = f // 4
    tm = min(t, 512)

    def body(dy_ref, wd_ref, gu_ref, *rest):
        dgu_ref, a_ref = rest[-2], rest[-1]
        dyv = dy_ref[...].astype(BF16)
        for c0 in range(0, nb, MXU_COLS):
            cs = slice(c0, min(c0 + MXU_COLS, nb))
            da = 0.5 * lax.dot_general(dyv, wd_ref[cs, :], NT, preferred_element_type=F32)
            g = gu_ref[0, :, cs].astype(F32)
            u = gu_ref[1, :, cs].astype(F32)
            s = _sigmoid(g)
            silu = g * s
            dgu_ref[0, :, cs] = (da * u * (s * (1.0 + g * (1.0 - s)))).astype(BF16)
            dgu_ref[1, :, cs] = (da * silu).astype(BF16)
            a_ref[:, cs] = (silu * u).astype(BF16)

    blocks = tm * d * 4 + nb * d * 2 + 5 * tm * nb * 2
    return pl.pallas_call(
        body, name=name, grid=(4, t // tm),
        in_specs=[pl.BlockSpec((tm, d), lambda j, i: (i, 0)),
                  pl.BlockSpec((nb, d), lambda j, i: (j, 0)),
                  pl.BlockSpec((2, tm, nb), lambda j, i: (0, i, j))] + [_ANY] * len(deps),
        out_specs=[pl.BlockSpec((2, tm, nb), lambda j, i: (0, i, j)), pl.BlockSpec((tm, nb), lambda j, i: (i, j))],
        out_shape=[_sds((2, t, f), BF16), _sds((t, f), BF16)],
        compiler_params=_params(("parallel", "parallel"), blocks, tm * d * 2 + 8 * tm * MXU_COLS * 4),
    )(dy, wd, gu, *deps)


def _ffn_bwd_dwd(name, a, dy, deps=(), side=None):
    t, f = a.shape
    d = dy.shape[1]
    tm = f // 4
    tn = min(d, 512)

    def epilogue(acc, ins, outs):
        outs[0][...] = (0.5 * acc).astype(BF16)

    return _fused(name, (4, d // tn, 1),
                  [(a, pl.BlockSpec((t, tm), lambda i, j, k: (0, i))),
                   (dy, pl.BlockSpec((t, tn), lambda i, j, k: (0, j)))],
                  [(_sds((f, d), BF16), pl.BlockSpec((tm, tn), lambda i, j, k: (i, j)))],
                  [(0, 1, TN)], epilogue, temp_bytes=t * tn * 2 + 2 * tm * tn * 4, deps=deps, side=side)


def _ffn_bwd_dh(name, dgu, wgu, deps=(), side=None):
    _, t, f = dgu.shape
    d, nb = wgu.shape[1], wgu.shape[2]
    tm = min(t, 512)

    def products(ins):
        return (lax.dot_general(ins[0][:, 0:nb], ins[1][0], NT, preferred_element_type=F32)
                + lax.dot_general(ins[0][:, nb:2 * nb], ins[1][1], NT, preferred_element_type=F32))

    def epilogue(acc, ins, outs):
        outs[0][...] = acc

    return _fused(name, (t // tm, 1, 4),
                  [(dgu, pl.BlockSpec((None, tm, 2 * nb), lambda i, j, k: (k // 2, i, k % 2))),
                   (wgu, pl.BlockSpec((2, d, nb), lambda i, j, k: (k, 0, 0)))],
                  [(_sds((t, d), F32), pl.BlockSpec((tm, d), lambda i, j, k: (i, 0)))],
                  products, epilogue, nk=4, acc_shape=(tm, d), temp_bytes=tm * d * 4, deps=deps, side=side)


def _ffn_bwd_dwgu(name, h, dgu, deps=(), side=None, rows=None):
    t, d = h.shape
    nb = dgu.shape[2] // 4
    tm = min(d, 512)
    row0, nrows = rows if rows is not None else (0, d)
    j0 = row0 // tm

    def epilogue(acc, ins, outs):
        outs[0][...] = acc.astype(BF16)

    return _fused(name, (N_DEV, nrows // tm, 1),
                  [(h, pl.BlockSpec((t, tm), lambda i, j, k: (0, j0 + j))),
                   (dgu, pl.BlockSpec((None, t, nb), lambda i, j, k: (i // 4, 0, i % 4)))],
                  [(_sds((N_DEV, nrows, nb), BF16), pl.BlockSpec((None, tm, nb), lambda i, j, k: (i, j, 0)))],
                  [(0, 1, TN)], epilogue, temp_bytes=2 * tm * nb * 4, deps=deps, side=side)


def _proj(h, w_in):
    t, d = h.shape
    nb = w_in.shape[3]
    tm = min(t, 512)

    def body(h_ref, w_ref, o_ref):
        hv = h_ref[...]
        o_ref[:, 0:nb] = jnp.dot(hv, w_ref[0], preferred_element_type=F32).astype(BF16)
        o_ref[:, nb:2 * nb] = jnp.dot(hv, w_ref[1], preferred_element_type=F32).astype(BF16)

    blocks = tm * d * 2 + 2 * d * nb * 2 + tm * 2 * nb * 4
    return pl.pallas_call(
        body, name="mix_proj", grid=(4, t // tm),
        in_specs=[pl.BlockSpec((tm, d), lambda j, i: (i, 0)),
                  pl.BlockSpec((None, 2, d, nb), lambda j, i: (j, 0, 0, 0))],
        out_specs=pl.BlockSpec((tm, 2 * nb), lambda j, i: (i, j)),
        out_shape=_sds((t, N_DEV * nb), BF16),
        compiler_params=_params(("parallel", "parallel"), blocks, 2 * tm * nb * 4),
    )(h, w_in)


def _shift_rows(u, k):
    t = u.shape[0]
    rolled = pltpu.roll(u, k % t, axis=0)
    row = lax.broadcasted_iota(jnp.int32, u.shape, 0)
    keep = (row >= k) if k > 0 else (row < t + k)
    return jnp.where(keep, rolled, 0.0)


def _conv_fwd(proj, conv_w):
    t = proj.shape[0]
    cw = conv_w.shape[1]
    tc = min(cw, 256)
    nc = cw // tc

    def epilogue(_, ins, outs):
        u = ins[2][...].astype(F32) * ins[0][...].astype(F32)
        w = ins[3][...]
        y = u * w[2:3, :] + _shift_rows(u, 1) * w[1:2, :] + _shift_rows(u, 2) * w[0:1, :]
        outs[0][...] = (ins[1][...].astype(F32) * y).astype(BF16)

    def col(seg):
        return pl.BlockSpec((t, tc), lambda i, j, k: (0, seg * nc + i))

    return _fused("conv_fwd", (nc, 1, 1),
                  [(proj, col(0)), (proj, col(1)), (proj, col(2)),
                   (conv_w, pl.BlockSpec((8, tc), lambda i, j, k: (0, i)))],
                  [(_sds((t, cw), BF16), pl.BlockSpec((t, tc), lambda i, j, k: (0, i)))],
                  [], epilogue, temp_bytes=6 * t * tc * 4)[0]


def _conv_bwd(proj, conv_w, dca, deps=()):
    t = proj.shape[0]
    cw = conv_w.shape[1]
    tc = min(cw, 256)
    nc = cw // tc

    def epilogue(_, ins, outs):
        xc, bg, cg = ins[0][...].astype(F32), ins[1][...].astype(F32), ins[2][...].astype(F32)
        w, dc = ins[3][...], ins[4][...]
        u = cg * xc
        u1, u2 = _shift_rows(u, 1), _shift_rows(u, 2)
        y = u * w[2:3, :] + u1 * w[1:2, :] + u2 * w[0:1, :]
        dconv = dc * bg
        du = dconv * w[2:3, :] + _shift_rows(dconv, -1) * w[1:2, :] + _shift_rows(dconv, -2) * w[0:1, :]
        outs[0][0] = (du * cg).astype(BF16)
        outs[0][1] = (dc * y).astype(BF16)
        outs[0][2] = (du * xc).astype(BF16)
        outs[1][...] = jnp.zeros_like(outs[1])
        outs[1][0:1, :] = jnp.sum(dconv * u2, axis=0, keepdims=True)
        outs[1][1:2, :] = jnp.sum(dconv * u1, axis=0, keepdims=True)
        outs[1][2:3, :] = jnp.sum(dconv * u, axis=0, keepdims=True)

    def col(seg):
        return pl.BlockSpec((t, tc), lambda i, j, k: (0, seg * nc + i))

    own = pl.BlockSpec((t, tc), lambda i, j, k: (0, i))
    wspec = pl.BlockSpec((8, tc), lambda i, j, k: (0, i))
    return _fused("conv_bwd", (nc, 1, 1),
                  [(proj, col(0)), (proj, col(1)), (proj, col(2)), (conv_w, wspec), (dca, own)],
                  [(_sds((3, t, cw), BF16), pl.BlockSpec((3, t, tc), lambda i, j, k: (0, 0, i))),
                   (_sds((8, cw), F32), wspec)],
                  [], epilogue, temp_bytes=10 * t * tc * 4, deps=deps)


def _split3(x):
    hi = x.astype(BF16)
    r1 = x - hi.astype(F32)
    mid = r1.astype(BF16)
    lo = (r1 - mid.astype(F32)).astype(BF16)
    return hi, mid, lo


def _head_selector(width):
    r = lax.broadcasted_iota(jnp.int32, (width, LANES), 0)
    c = lax.broadcasted_iota(jnp.int32, (width, LANES), 1)
    return (lax.shift_right_logical(r, 6) == c).astype(BF16)


def _head_sum(x, sel):
    return sum(jnp.dot(p, sel, preferred_element_type=F32) for p in _split3(x))


def _head_bcast(r, sel):
    return sum(lax.dot_general(p, sel, NT, preferred_element_type=F32) for p in _split3(r))


def _rope(x, c, sa, sb):
    n = x.shape[1]
    return x * c + pltpu.roll(x, n - ROT_DIM // 2, axis=1) * sa + pltpu.roll(x, ROT_DIM // 2, axis=1) * sb


def _rope_t(d, c, sa, sb):
    n = d.shape[1]
    return d * c + pltpu.roll(d * sa, ROT_DIM // 2, axis=1) + pltpu.roll(d * sb, n - ROT_DIM // 2, axis=1)


def _tile_lanes(tab, width):
    return tab if width == tab.shape[1] else jnp.tile(tab, (1, width // tab.shape[1]))


def _qk_prep(proj, gq, gk, rope_tabs, cw, kw):
    t = proj.shape[0]
    tm = _row_tile(t)

    def epilogue(_, ins, outs):
        c, sa, sb = ins[5][...], ins[6][...], ins[7][...]
        for src, gain, dst, width in ((0, 3, 0, cw), (1, 4, 1, kw)):
            xv = ins[src][...].astype(F32)
            sel = _head_selector(width)
            r = lax.rsqrt(_head_sum(xv * xv, sel) * (1.0 / HEAD_DIM) + RMS_EPS)
            xn = xv * _head_bcast(r, sel) * ins[gain][...]
            outs[dst][...] = _rope(xn, _tile_lanes(c, width), _tile_lanes(sa, width), _tile_lanes(sb, width)).astype(BF16)
        outs[2][...] = ins[2][...].astype(BF16)

    kblk = cw // kw
    tab = pl.BlockSpec((tm, LANES), lambda i, j, k: (i, 0))
    kspec = pl.BlockSpec((tm, kw), lambda i, j, k: (i, 0))
    return _fused("qk_prep", (t // tm, 1, 1),
                  [(proj, pl.BlockSpec((tm, cw), lambda i, j, k: (i, 3))),
                   (proj, pl.BlockSpec((tm, kw), lambda i, j, k: (i, 4 * kblk))),
                   (proj, pl.BlockSpec((tm, kw), lambda i, j, k: (i, 4 * kblk + 1))),
                   (gq, pl.BlockSpec((1, cw), lambda i, j, k: (0, 0))),
                   (gk, pl.BlockSpec((1, kw), lambda i, j, k: (0, 0))),
                   (rope_tabs[0], tab), (rope_tabs[1], tab), (rope_tabs[2], tab)],
                  [(_sds((t, cw), BF16), pl.BlockSpec((tm, cw), lambda i, j, k: (i, 0))),
                   (_sds((t, kw), BF16), kspec), (_sds((t, kw), BF16), kspec)],
                  [], epilogue, temp_bytes=12 * tm * cw * 4)


def _qk_prep_bwd(proj, gq, gk, rope_tabs, dq, dkc, dkp, dvc, dvp, cw, kw):
    t = proj.shape[0]
    tm = BLOCK
    nblk = t // tm

    def epilogue(_, ins, outs):
        c, sa, sb = ins[5][...], ins[6][...], ins[7][...]
        has_next = (pl.program_id(0) < nblk - 1).astype(F32)
        dk = ins[9][...] + has_next * ins[10][...]
        dv = ins[11][...] + has_next * ins[12][...]
        pieces = []
        for src, gain, dval, dst, width in ((0, 3, ins[8][...], 1, cw), (1, 4, dk, 2, kw)):
            xv, gv = ins[src][...].astype(F32), ins[gain][...]
            sel = _head_selector(width)
            r = _head_bcast(lax.rsqrt(_head_sum(xv * xv, sel) * (1.0 / HEAD_DIM) + RMS_EPS), sel)
            xh = xv * r
            dxn = _rope_t(dval, _tile_lanes(c, width), _tile_lanes(sa, width), _tile_lanes(sb, width))
            u = dxn * gv
            dot = _head_bcast(_head_sum(u * xh, sel), sel) * (1.0 / HEAD_DIM)
            pieces.append((r * (u - xh * dot)).astype(BF16))
            ri = lax.broadcasted_iota(jnp.int32, (width, LANES), 0)
            ci = lax.broadcasted_iota(jnp.int32, (width, LANES), 1)
            fold = (lax.bitwise_and(ri, HEAD_DIM - 1) == ci).astype(BF16)
            colsum = jnp.broadcast_to(jnp.sum(dxn * xh, axis=0, keepdims=True), (8, width))
            part = sum(jnp.dot(p, fold, preferred_element_type=F32) for p in _split3(colsum))

            @pl.when(pl.program_id(0) == 0)
            def _():
                outs[dst][...] = jnp.zeros_like(outs[dst])

            outs[dst][0:1, :] += part[0:1, :]
        outs[0][:, 0:cw] = pieces[0]
        outs[0][:, cw:cw + kw] = pieces[1]
        outs[0][:, cw + kw:cw + 2 * kw] = dv.astype(BF16)

    kblk = cw // kw
    tab = pl.BlockSpec((tm, LANES), lambda i, j, k: (i, 0))
    kcur = pl.BlockSpec((tm, kw), lambda i, j, k: (i, 0))
    knext = pl.BlockSpec((tm, kw), lambda i, j, k: (jnp.minimum(i + 1, nblk - 1), 0))
    acc = pl.BlockSpec((8, LANES), lambda i, j, k: (0, 0))
    return _fused("qk_prep_bwd", (nblk, 1, 1),
                  [(proj, pl.BlockSpec((tm, cw), lambda i, j, k: (i, 3))),
                   (proj, pl.BlockSpec((tm, kw), lambda i, j, k: (i, 4 * kblk))),
                   (proj, pl.BlockSpec((tm, kw), lambda i, j, k: (i, 4 * kblk + 1))),
                   (gq, pl.BlockSpec((1, cw), lambda i, j, k: (0, 0))),
                   (gk, pl.BlockSpec((1, kw), lambda i, j, k: (0, 0))),
                   (rope_tabs[0], tab), (rope_tabs[1], tab), (rope_tabs[2], tab),
                   (dq, pl.BlockSpec((tm, cw), lambda i, j, k: (i, 0))),
                   (dkc, kcur), (dkp, knext), (dvc, kcur), (dvp, knext)],
                  [(_sds((t, cw + 2 * kw), BF16), pl.BlockSpec((tm, cw + 2 * kw), lambda i, j, k: (i, 0))),
                   (_sds((8, LANES), F32), acc), (_sds((8, LANES), F32), acc)],
                  [], epilogue, temp_bytes=16 * tm * cw * 4, semantics=("arbitrary", "arbitrary", "arbitrary"))


def _attn_mask(n):
    key = lax.broadcasted_iota(jnp.int32, (2 * BLOCK, GROUP * BLOCK), 0)
    qry = lax.bitwise_and(lax.broadcasted_iota(jnp.int32, (2 * BLOCK, GROUP * BLOCK), 1), BLOCK - 1)
    return (key > qry) & (key <= qry + BLOCK) & ((key >= BLOCK) | (n > 0))


def _stack_heads(x, h):
    return jnp.concatenate([x[:, (h * GROUP + g) * HEAD_DIM:(h * GROUP + g + 1) * HEAD_DIM] for g in range(GROUP)], axis=0)


def _softmax_with_sink(q4, k2, sink_ref, h, valid):
    sink = jnp.concatenate([sink_ref[h * GROUP + g:h * GROUP + g + 1, :] for g in range(GROUP)], axis=1)
    s = lax.dot_general(k2, q4, NT, preferred_element_type=F32) * ATTN_SCALE
    s = jnp.where(valid, s, NEG_INF)
    m = jnp.maximum(jnp.max(s, axis=0, keepdims=True), sink)
    p = jnp.exp(s - m)
    es = jnp.exp(sink - m)
    inv = 1.0 / (jnp.sum(p, axis=0, keepdims=True) + es)
    return p * inv, es * inv


def _attn_fwd(qn, kn, vb, sink_rows):
    t, cw = qn.shape
    kw = kn.shape[1]
    nkv = kw // HEAD_DIM

    def body(q_ref, kp_ref, kc_ref, vp_ref, vc_ref, sink_ref, o_ref):
        valid = _attn_mask(pl.program_id(0))
        qv = q_ref[...]
        kp, kc, vp, vc = kp_ref[...], kc_ref[...], vp_ref[...], vc_ref[...]
        outs = []
        for h in range(nkv):
            hs = slice(h * HEAD_DIM, (h + 1) * HEAD_DIM)
            k2 = jnp.concatenate([kp[:, hs], kc[:, hs]], axis=0)
            v2 = jnp.concatenate([vp[:, hs], vc[:, hs]], axis=0)
            pn, _ = _softmax_with_sink(_stack_heads(qv, h), k2, sink_ref, h, valid)
            o4 = lax.dot_general(pn.astype(BF16), v2, TN, preferred_element_type=F32)
            outs += [o4[g * BLOCK:(g + 1) * BLOCK] for g in range(GROUP)]
        o_ref[...] = jnp.concatenate(outs, axis=-1).astype(BF16)

    cur = lambda n: (n, 0)
    prev = lambda n: (jnp.maximum(n - 1, 0), 0)
    return pl.pallas_call(
        body, name="attn_fwd", grid=(t // BLOCK,),
        in_specs=[pl.BlockSpec((BLOCK, cw), cur),
                  pl.BlockSpec((BLOCK, kw), prev), pl.BlockSpec((BLOCK, kw), cur),
                  pl.BlockSpec((BLOCK, kw), prev), pl.BlockSpec((BLOCK, kw), cur),
                  pl.BlockSpec(sink_rows.shape, lambda n: (0, 0))],
        out_specs=pl.BlockSpec((BLOCK, cw), cur),
        out_shape=_sds((t, cw), BF16),
        compiler_params=_params(("parallel",), BLOCK * (cw + 4 * kw) * 2 + BLOCK * cw * 2, 8 << 20),
    )(qn, kn, kn, vb, vb, sink_rows)


def _attn_bwd(qn, kn, vb, sink_rows, do):
    t, cw = qn.shape
    kw = kn.shape[1]
    nkv = kw // HEAD_DIM
    nq = nkv * GROUP

    def body(q_ref, kp_ref, kc_ref, vp_ref, vc_ref, sink_ref, do_ref,
             dq_ref, dkc_ref, dkp_ref, dvc_ref, dvp_ref, dsink_ref):
        n = pl.program_id(0)
        valid = _attn_mask(n)
        qv, dov = q_ref[...], do_ref[...]
        kp, kc, vp, vc = kp_ref[...], kc_ref[...], vp_ref[...], vc_ref[...]
        dqs, dks, dvs, dsinks = [], [], [], []
        for h in range(nkv):
            hs = slice(h * HEAD_DIM, (h + 1) * HEAD_DIM)
            k2 = jnp.concatenate([kp[:, hs], kc[:, hs]], axis=0)
            v2 = jnp.concatenate([vp[:, hs], vc[:, hs]], axis=0)
            q4 = _stack_heads(qv, h)
            dob = _stack_heads(dov, h).astype(BF16)
            pn, psink = _softmax_with_sink(q4, k2, sink_ref, h, valid)
            dpn = lax.dot_general(v2, dob, NT, preferred_element_type=F32)
            dvs.append(jnp.dot(pn.astype(BF16), dob, preferred_element_type=F32))
            delta = jnp.sum(pn * dpn, axis=0, keepdims=True)
            ds = (pn * (dpn - delta) * ATTN_SCALE).astype(BF16)
            dks.append(jnp.dot(ds, q4, preferred_element_type=F32))
            dq4 = lax.dot_general(ds, k2, TN, preferred_element_type=F32)
            dsink4 = -psink * delta
            for g in range(GROUP):
                dqs.append(dq4[g * BLOCK:(g + 1) * BLOCK])
                dsinks.append(jnp.broadcast_to(jnp.sum(dsink4[:, g * BLOCK:(g + 1) * BLOCK], axis=1, keepdims=True), (1, LANES)))
        dq_ref[...] = jnp.concatenate(dqs, axis=-1)
        dkp_ref[...] = jnp.concatenate([d[:BLOCK] for d in dks], axis=-1)
        dkc_ref[...] = jnp.concatenate([d[BLOCK:] for d in dks], axis=-1)
        dvp_ref[...] = jnp.concatenate([d[:BLOCK] for d in dvs], axis=-1)
        dvc_ref[...] = jnp.concatenate([d[BLOCK:] for d in dvs], axis=-1)

        @pl.when(n == 0)
        def _():
            dsink_ref[...] = jnp.zeros_like(dsink_ref)

        dsink_ref[...] += jnp.concatenate(dsinks, axis=0)

    cur = lambda n: (n, 0)
    prev = lambda n: (jnp.maximum(n - 1, 0), 0)
    kspec = pl.BlockSpec((BLOCK, kw), cur)
    return pl.pallas_call(
        body, name="attn_bwd", grid=(t // BLOCK,),
        in_specs=[pl.BlockSpec((BLOCK, cw), cur),
                  pl.BlockSpec((BLOCK, kw), prev), kspec,
                  pl.BlockSpec((BLOCK, kw), prev), kspec,
                  pl.BlockSpec(sink_rows.shape, lambda n: (0, 0)),
                  pl.BlockSpec((BLOCK, cw), cur)],
        out_specs=[pl.BlockSpec((BLOCK, cw), cur), kspec, kspec, kspec, kspec,
                   pl.BlockSpec((nq, LANES), lambda n: (0, 0))],
        out_shape=[_sds((t, cw), F32)] + [_sds((t, kw), F32)] * 4 + [_sds((nq, LANES), F32)],
        compiler_params=_params(("arbitrary",), BLOCK * (cw + 4 * kw) * 2 + 2 * BLOCK * cw * 4 + 4 * BLOCK * kw * 4, 12 << 20),
    )(qn, kn, kn, vb, vb, sink_rows, do)


def _mix_out(ca, o, woc, woa, proj):
    t, cw = ca.shape
    nb = woc.shape[2]
    d = N_DEV * nb
    tm = min(t, 1024)
    ga0 = (3 * cw + cw + 2 * (cw // 4)) // nb

    def body(ca_ref, o_ref, woc_ref, woa_ref, ga_ref, gb_ref, m_ref, ya_ref, yb_ref):
        ya = jnp.dot(ca_ref[...], woc_ref[...], preferred_element_type=F32)
        yb = jnp.dot(o_ref[...], woa_ref[...], preferred_element_type=F32)
        ya_ref[...] = ya.astype(BF16)
        yb_ref[...] = yb.astype(BF16)
        m_ref[...] = (_sigmoid(ga_ref[...].astype(F32)) * ya + _sigmoid(gb_ref[...].astype(F32)) * yb).astype(BF16)

    act = pl.BlockSpec((tm, cw), lambda i, j: (i, 0))
    wsp = pl.BlockSpec((None, cw, nb), lambda i, j: (j, 0, 0))
    osp = pl.BlockSpec((tm, nb), lambda i, j: (i, j))
    blocks = 2 * tm * cw * 2 + 2 * cw * nb * 2 + 2 * tm * nb * 4 + 3 * tm * nb * 2
    return pl.pallas_call(
        body, name="mix_out", grid=(t // tm, N_DEV),
        in_specs=[act, act, wsp, wsp,
                  pl.BlockSpec((tm, nb), lambda i, j: (i, ga0 + j)),
                  pl.BlockSpec((tm, nb), lambda i, j: (i, ga0 + N_DEV + j))],
        out_specs=[osp, osp, osp],
        out_shape=[_sds((t, d), BF16)] * 3,
        compiler_params=_params(("parallel", "parallel"), blocks, 6 * tm * nb * 4),
    )(ca, o, woc, woa, proj, proj)


def _mix_residual(merged, wo, x):
    t, d = x.shape
    tm = min(t, 512)

    def epilogue(acc, ins, outs):
        outs[0][...] = ins[2][...] + acc

    row = pl.BlockSpec((tm, d), lambda i, j, k: (i, 0))
    return _fused("mix_residual", (t // tm, 1, 1),
                  [(merged, row), (wo, pl.BlockSpec((d, d), lambda i, j, k: (0, 0))), (x, row)],
                  [(_sds((t, d), F32), row)], [(0, 1, NN)], epilogue, temp_bytes=2 * tm * d * 4)[0]


def _mix_bwd_gates(dx, wo, ya, yb, proj, cw):
    t, d = dx.shape
    tm = min(t, 1024)
    tn = min(d, 512)
    ga0 = (4 * cw + 2 * (cw // 4)) // tn

    def epilogue(acc, ins, outs):
        sa, sb = _sigmoid(ins[4][...].astype(F32)), _sigmoid(ins[5][...].astype(F32))
        outs[0][...] = (acc * sa).astype(BF16)
        outs[1][...] = (acc * sb).astype(BF16)
        outs[2][0] = (acc * ins[2][...].astype(F32) * sa * (1.0 - sa)).astype(BF16)
        outs[2][1] = (acc * ins[3][...].astype(F32) * sb * (1.0 - sb)).astype(BF16)

    blk = pl.BlockSpec((tm, tn), lambda i, j, k: (i, j))
    return _fused("mix_bwd_gates", (t // tm, d // tn, 1),
                  [(dx, pl.BlockSpec((tm, d), lambda i, j, k: (i, 0))),
                   (wo, pl.BlockSpec((tn, d), lambda i, j, k: (j, 0))),
                   (ya, blk), (yb, blk),
                   (proj, pl.BlockSpec((tm, tn), lambda i, j, k: (i, ga0 + j))),
                   (proj, pl.BlockSpec((tm, tn), lambda i, j, k: (i, ga0 + d // tn + j)))],
                  [(_sds((t, d), BF16), blk), (_sds((t, d), BF16), blk),
                   (_sds((2, t, d), BF16), pl.BlockSpec((2, tm, tn), lambda i, j, k: (0, i, j)))],
                  [(0, 1, NT)], epilogue, temp_bytes=8 * tm * tn * 4)


def _tn_matmul(name, a, b, tm, out_dtype=BF16):
    t, m = a.shape
    n = b.shape[1]
    tk = min(t, 512)

    def epilogue(acc, ins, outs):
        outs[0][...] = acc.astype(out_dtype)

    return _fused(name, (m // tm, 1, t // tk),
                  [(a, pl.BlockSpec((tk, tm), lambda i, j, k: (k, i))),
                   (b, pl.BlockSpec((tk, n), lambda i, j, k: (k, 0)))],
                  [(_sds((m, n), out_dtype), pl.BlockSpec((tm, n), lambda i, j, k: (i, 0)))],
                  [(0, 1, TN)], epilogue, nk=t // tk, acc_shape=(tm, n), temp_bytes=tm * n * 4)[0]


def _out_proj_bwd_act(dya, dyb, woc, woa, deps=()):
    t, d = dya.shape
    kdim, nb = woc.shape[1], woc.shape[2]
    tm = min(t, 512)

    def body(dya_ref, dyb_ref, woc_ref, woa_ref, *rest):
        for dy_ref, w_ref, o_ref in ((dya_ref, woc_ref, rest[-2]), (dyb_ref, woa_ref, rest[-1])):
            total = None
            for j in range(N_DEV):
                part = lax.dot_general(dy_ref[:, j * nb:(j + 1) * nb], w_ref[j], NT, preferred_element_type=F32)
                total = part if total is None else total + part
            o_ref[...] = total

    row = pl.BlockSpec((tm, d), lambda i: (i, 0))
    wsp = pl.BlockSpec((N_DEV, kdim, nb), lambda i: (0, 0, 0))
    osp = pl.BlockSpec((tm, kdim), lambda i: (i, 0))
    blocks = 2 * tm * d * 2 + 2 * N_DEV * kdim * nb * 2 + 2 * tm * kdim * 4
    return pl.pallas_call(
        body, name="mix_bwd_dca_do", grid=(t // tm,),
        in_specs=[row, row, wsp, wsp] + [_ANY] * len(deps), out_specs=[osp, osp],
        out_shape=[_sds((t, kdim), F32)] * 2,
        compiler_params=_params(("parallel",), blocks, 4 * tm * kdim * 4),
    )(dya, dyb, woc, woa, *deps)


def _out_proj_bwd_w(ca, o, dya, dyb, nb):
    t, kdim = ca.shape

    def body(ca_ref, o_ref, dya_ref, dyb_ref, dwoc_ref, dwoa_ref):
        dwoc_ref[...] = lax.dot_general(ca_ref[...], dya_ref[...], TN, preferred_element_type=F32).astype(BF16)
        dwoa_ref[...] = lax.dot_general(o_ref[...], dyb_ref[...], TN, preferred_element_type=F32).astype(BF16)

    act = pl.BlockSpec((t, kdim), lambda j: (0, 0))
    col = pl.BlockSpec((t, nb), lambda j: (0, j))
    osp = pl.BlockSpec((None, kdim, nb), lambda j: (j, 0, 0))
    blocks = 2 * t * kdim * 2 + 2 * t * nb * 2 + 2 * kdim * nb * 2
    return pl.pallas_call(
        body, name="mix_bwd_dwoc_dwoa", grid=(N_DEV,),
        in_specs=[act, act, col, col], out_specs=[osp, osp],
        out_shape=[_sds((N_DEV, kdim, nb), BF16)] * 2,
        compiler_params=_params(("parallel",), blocks, 4 * kdim * nb * 4),
    )(ca, o, dya, dyb)


def _proj_bwd_act(dproj, w_in, deps=()):
    t, n = dproj.shape
    d, nb = w_in.shape[2], w_in.shape[3]
    tm = min(t, 512)

    def epilogue(acc, ins, outs):
        outs[0][...] = acc

    def products(ins):
        return (lax.dot_general(ins[0][:, 0:nb], ins[1][0], NT, preferred_element_type=F32)
                + lax.dot_general(ins[0][:, nb:2 * nb], ins[1][1], NT, preferred_element_type=F32))

    return _fused("mix_bwd_dh", (t // tm, 1, 4),
                  [(dproj, pl.BlockSpec((tm, 2 * nb), lambda i, j, k: (i, k))),
                   (w_in, pl.BlockSpec((None, 2, d, nb), lambda i, j, k: (k, 0, 0, 0)))],
                  [(_sds((t, d), F32), pl.BlockSpec((tm, d), lambda i, j, k: (i, 0)))],
                  products, epilogue, nk=4, acc_shape=(tm, d), temp_bytes=tm * d * 4, deps=deps)[0]


def _proj_bwd_w(h, dproj):
    t, d = h.shape
    nb = dproj.shape[1] // N_DEV
    tm = min(d, 512)

    def body(h_ref, dp_ref, o_ref):
        hv = h_ref[...]
        o_ref[0] = lax.dot_general(hv, dp_ref[:, 0:nb], TN, preferred_element_type=F32).astype(BF16)
        o_ref[1] = lax.dot_general(hv, dp_ref[:, nb:2 * nb], TN, preferred_element_type=F32).astype(BF16)

    blocks = t * tm * 2 + t * 2 * nb * 2 + 2 * tm * nb * 2
    return pl.pallas_call(
        body, name="mix_bwd_dwin", grid=(4, d // tm),
        in_specs=[pl.BlockSpec((t, tm), lambda j, i: (0, i)),
                  pl.BlockSpec((t, 2 * nb), lambda j, i: (0, j))],
        out_specs=pl.BlockSpec((None, 2, tm, nb), lambda j, i: (j, 0, i, 0)),
        out_shape=_sds((4, 2, d, nb), BF16),
        compiler_params=_params(("parallel", "parallel"), blocks, 4 * tm * nb * 4),
    )(h, dproj)


def _adamw_math(w, g, m, v):
    m = ADAM_B1 * m + (1.0 - ADAM_B1) * g
    v = ADAM_B2 * v + (1.0 - ADAM_B2) * (g * g)
    m_hat = m / (1.0 - ADAM_B1 ** ADAM_STEP)
    v_hat = v / (1.0 - ADAM_B2 ** ADAM_STEP)
    delta = -ADAM_LR * (m_hat / (jnp.sqrt(v_hat) + ADAM_EPS) + ADAM_WD * w)
    return delta, m, v


def _adamw(name, parts, w, m, v, tr):
    r, c = w.shape

    def body(p_ref, w_ref, m_ref, v_ref, g_out, d_out, m_out, v_out):
        g = p_ref[0].astype(F32)
        for s in range(1, N_DEV):
            g = g + p_ref[s].astype(F32)
        delta, mn, vn = _adamw_math(w_ref[...], g, m_ref[...], v_ref[...])
        g_out[...] = g
        d_out[...] = delta
        m_out[...] = mn
        v_out[...] = vn

    blk = pl.BlockSpec((tr, c), lambda i: (i, 0))
    blocks = N_DEV * tr * c * parts.dtype.itemsize + 7 * tr * c * 4
    return pl.pallas_call(
        body, name=name, grid=(r // tr,),
        in_specs=[pl.BlockSpec((N_DEV, tr, c), lambda i: (0, i, 0)), blk, blk, blk],
        out_specs=[blk] * 4, out_shape=[_sds((r, c), F32)] * 4,
        compiler_params=_params(("parallel",), blocks, 6 * tr * c * 4),
    )(parts, w, m, v)


def _chip_sum(sums_ref):
    g = sums_ref[0].astype(F32)
    for k in range(1, 4):
        g = g + sums_ref[k].astype(F32)
    return g


def _adamw_chips(name, sums, w, m, v, tr, deps=(), row0=0, into=None):
    r, c = w.shape
    rs = sums.shape[1]
    i0 = row0 // tr
    n_pass = len(deps) + (4 if into is not None else 0)

    def body(sums_ref, w_ref, m_ref, v_ref, *rest):
        g_out, d_out, m_out, v_out = rest[n_pass:]
        g = _chip_sum(sums_ref)
        delta, mn, vn = _adamw_math(w_ref[...], g, m_ref[...], v_ref[...])
        g_out[...] = g
        d_out[...] = delta
        m_out[...] = mn
        v_out[...] = vn

    blk = pl.BlockSpec((tr, c), lambda i: (i0 + i, 0))
    blocks = 4 * tr * c * 2 + 7 * tr * c * 4
    passed = list(deps) + (list(into) if into is not None else [])
    aliases = {4 + len(deps) + q: q for q in range(4)} if into is not None else {}
    return pl.pallas_call(
        body, name=name, grid=(rs // tr,),
        in_specs=[pl.BlockSpec((4, tr, c), lambda i: (0, i, 0)), blk, blk, blk] + [_ANY] * n_pass,
        out_specs=[blk] * 4, out_shape=[_sds((r, c), F32)] * 4,
        input_output_aliases=aliases,
        compiler_params=_params(("parallel",), blocks, 6 * tr * c * 4),
    )(sums, w, m, v, *passed)


def _adamw_side(contrib, w, m, v, n_tiles, step_of):
    r, c = w.shape
    tr = r // n_tiles
    assert tr * n_tiles == r and tr % 16 == 0, (r, n_tiles)

    def tile(i, j, k):
        return jnp.minimum(step_of(i, j, k), n_tiles - 1)

    blk = pl.BlockSpec((tr, c), lambda i, j, k: (tile(i, j, k), 0))
    ins = [(contrib, pl.BlockSpec((4, tr, c), lambda i, j, k: (0, tile(i, j, k), 0))), (w, blk), (m, blk), (v, blk)]
    outs = [(_sds((r, c), F32), blk)] * 4

    def fn(in_refs, out_refs):
        @pl.when(step_of(pl.program_id(0), pl.program_id(1), pl.program_id(2)) < n_tiles)
        def _():
            g = _chip_sum(in_refs[0])
            delta, mn, vn = _adamw_math(in_refs[1][...], g, in_refs[2][...], in_refs[3][...])
            out_refs[0][...] = g
            out_refs[1][...] = delta
            out_refs[2][...] = mn
            out_refs[3][...] = vn

    return ins, outs, fn


def _rope_tables(t):
    half = ROT_DIM // 2
    inv_freq = 1.0 / (ROPE_THETA ** (jnp.arange(0, ROT_DIM, 2, dtype=F32) / ROT_DIM))
    ang = jnp.arange(t, dtype=F32)[:, None] * inv_freq[None, :]
    cos, sin = jnp.cos(ang), jnp.sin(ang)
    ones = jnp.ones((t, HEAD_DIM - ROT_DIM), F32)
    zeros = jnp.zeros((t, HEAD_DIM - half), F32)
    c = jnp.concatenate([cos, cos, ones], axis=1)
    sa = jnp.concatenate([-sin, zeros], axis=1)
    sb = jnp.concatenate([jnp.zeros((t, half), F32), sin, jnp.zeros((t, HEAD_DIM - ROT_DIM), F32)], axis=1)
    return tuple(jnp.tile(a, (1, LANES // HEAD_DIM)) for a in (c, sa, sb))


def _pad_rows(a, rows=8):
    return jnp.pad(a, ((0, rows - a.shape[0]), (0, 0)))


def kernel(x, g_ffn1, w_gu1, w_down1, g_mix, w_in, conv_w, q_norm_g, k_norm_g, sinks, w_out_conv, w_out_attn, w_o, g_ffn2, w_gu2, w_down2, loss_target, m_g_ffn1, m_w_gu1, m_w_down1, m_g_mix, m_w_in, m_conv_w, m_q_norm_g, m_k_norm_g, m_sinks, m_w_out_conv, m_w_out_attn, m_w_o, m_g_ffn2, m_w_gu2, m_w_down2, v_g_ffn1, v_w_gu1, v_w_down1, v_g_mix, v_w_in, v_conv_w, v_q_norm_g, v_k_norm_g, v_sinks, v_w_out_conv, v_w_out_attn, v_w_o, v_g_ffn2, v_w_gu2, v_w_down2):
    t, d = x.shape[1], x.shape[2]
    cw = d // 2
    kw = cw // GROUP
    nq = cw // HEAD_DIM
    xs, target = x.reshape(t, d), loss_target.reshape(t, d)
    me = 4 * lax.axis_index("x") + 2 * lax.axis_index("y") + lax.axis_index("c")

    big = {"w_gu1": w_gu1, "w_down1": w_down1, "w_in": w_in, "w_out_conv": w_out_conv,
           "w_out_attn": w_out_attn, "w_o": w_o, "w_gu2": w_gu2, "w_down2": w_down2}
    big_m = {"w_gu1": m_w_gu1, "w_down1": m_w_down1, "w_in": m_w_in, "w_out_conv": m_w_out_conv,
             "w_out_attn": m_w_out_attn, "w_o": m_w_o, "w_gu2": m_w_gu2, "w_down2": m_w_down2}
    big_v = {"w_gu1": v_w_gu1, "w_down1": v_w_down1, "w_in": v_w_in, "w_out_conv": v_w_out_conv,
             "w_out_attn": v_w_out_attn, "w_o": v_w_o, "w_gu2": v_w_gu2, "w_down2": v_w_down2}
    names = list(big)

    tiles = {"w_gu1": 256, "w_gu2": 256, "w_in": 256, "w_down1": 176, "w_down2": 176,
             "w_out_conv": 1024, "w_out_attn": 1024, "w_o": 128}

    def row_tile(n):
        r = big[n].shape[1]
        return tiles[n] if r % tiles[n] == 0 else r

    rs_shape = {n: big[n].shape[1:] for n in names}
    half = rs_shape["w_gu1"][0] // 2
    rs_shape["w_gu1_lo"] = rs_shape["w_gu1_hi"] = (half, rs_shape["w_gu1"][1])

    def add_tile(n):
        r, c = rs_shape[n]
        while r * c * 2 > (3 << 20) and r % 32 == 0:
            r //= 2
        return r

    me_arr = me.astype(jnp.int32).reshape(1)
    sources = [(n, big[n][0], BF16, row_tile(n)) for n in names] + [("conv_w", _pad_rows(conv_w[0]), F32, 8)]
    issue_order = [0, 1, 2, 8, 3, 4, 5, 6, 7]
    first = _place_shard("place_" + names[0], sources[0][1], BF16, me_arr, sources[0][3])
    started = [_gather_start("gather_start_first", [first])]
    early = {2: (big_m["w_in"][0], big_v["w_in"][0])}
    rest = [_place_shard("place_" + sources[i][0], sources[i][1], sources[i][2], me_arr, sources[i][3],
                         deps=(started[0][3],) + early.get(i, ())) for i in issue_order[1:]]
    started.append(_gather_start("gather_start_rest", rest))
    where = {0: (0, 0)}
    where.update({i: (1, p) for p, i in enumerate(issue_order[1:])})

    def fetch(tag, idxs, after):
        call = where[idxs[0]][0]
        send, recv, stacks, _ = started[call]
        positions = [where[i][1] for i in idxs]
        got = _gather_wait("gather_wait_" + tag, positions, send, recv, [stacks[p] for p in positions], after)
        return _forward_to_sibling("gather_forward_" + tag, got)

    rope_tabs = _rope_tables(t)
    gq = jnp.tile(q_norm_g, (1, nq))
    gk = jnp.tile(k_norm_g, (1, nq // GROUP))
    sink_rows = jnp.broadcast_to(sinks[0][:, None], (nq, LANES))

    wts = {}
    h1 = _rms_fwd("ffn1_norm", xs, g_ffn1)
    wts["w_gu1"], = fetch("gu1", [0], started[1][3])
    gu1, a1 = _ffn_up("ffn1_up", h1, wts["w_gu1"])
    wts["w_down1"], = fetch("down1", [1], a1)
    wd1 = wts["w_down1"].reshape(-1, d)
    x1 = _ffn_down("ffn1_down", a1, wd1, xs)
    h2 = _rms_fwd("mix_norm", x1, g_mix)
    wts["w_in"], conv_land = fetch("in", [2, 8], h2)
    w_in_full = wts["w_in"].reshape(4, 2, d, -1)
    conv_full = jnp.transpose(conv_land, (1, 0, 2)).reshape(8, cw)
    proj = _proj(h2, w_in_full)
    ca = _conv_fwd(proj, conv_full)
    qn, kn, vb = _qk_prep(proj, gq, gk, rope_tabs, cw, kw)
    o = _attn_fwd(qn, kn, vb, sink_rows)
    wts["w_out_conv"], wts["w_out_attn"] = fetch("out", [3, 4], o)
    merged, ya, yb = _mix_out(ca, o, wts["w_out_conv"], wts["w_out_attn"], proj)
    wts["w_o"], = fetch("o", [5], merged)
    wo = wts["w_o"].reshape(d, d)
    x2 = _mix_residual(merged, wo, x1)
    h3 = _rms_fwd("ffn2_norm", x2, g_ffn2)
    wts["w_gu2"], = fetch("gu2", [6], h3)
    gu2, a2 = _ffn_up("ffn2_up", h3, wts["w_gu2"])
    wts["w_down2"], = fetch("down2", [7], a2)
    wd2 = wts["w_down2"].reshape(-1, d)
    dy, sq = _ffn_down("ffn2_down", a2, wd2, x2, target=target)
    loss = lax.psum(sq[0, 0] * (0.5 / d), ("x", "y", "c"))

    place = jnp.stack([lax.axis_index("c"), 2 * lax.axis_index("x") + lax.axis_index("y")]).astype(jnp.int32)
    def pair_start(tag, group, grads, deps=()):
        stacks = [grads[n].reshape((4, 2) + rs_shape[n]) for n in group]
        lands = [lax.empty((4,) + rs_shape[n], BF16) for n in group]
        return _pair_start("rs_pair_start_" + tag, stacks, lands, deps)

    def chip_start(tag, group, pending, after):
        send, recv, stacks, lands, _ = pending
        stacks, lands = _pair_wait("rs_pair_wait_" + tag, send, recv, stacks, lands, after)
        added = [_pair_add("rs_pair_add_" + n, st, ld, place, add_tile(n)) for n, st, ld in zip(group, stacks, lands)]
        return _chip_start("rs_chip_start_" + tag, [a[0] for a in added], [a[1] for a in added])

    group_a, group_b, group_c = ["w_down2", "w_gu2"], ["w_o", "w_out_conv", "w_out_attn"], ["w_in"]
    group_d, group_e, group_f = ["w_down1"], ["w_gu1_lo"], ["w_gu1_hi"]
    g = {}
    dgu2, a2 = _ffn_bwd_act("ffn2_bwd_act", dy, wd2, gu2)
    g["w_down2"], = _ffn_bwd_dwd("ffn2_bwd_dwd", a2, dy)
    g["w_gu2"], = _ffn_bwd_dwgu("ffn2_bwd_dwgu", h3, dgu2)
    pend_a = pair_start("a", group_a, g)
    dh3, = _ffn_bwd_dh("ffn2_bwd_dh", dgu2, wts["w_gu2"], deps=(pend_a[4],))
    ring_a = chip_start("a", group_a, pend_a, dh3)
    dx2, dg_ffn2 = _rms_bwd("ffn2_bwd_rms", x2, g_ffn2, dh3, dy, deps=(ring_a[4],))

    dya, dyb, dgates = _mix_bwd_gates(dx2, wo, ya, yb, proj, cw)
    g["w_o"] = _tn_matmul("mix_bwd_dwo", merged, dx2, min(d, 1024))
    g["w_out_conv"], g["w_out_attn"] = _out_proj_bwd_w(ca, o, dya, dyb, d // N_DEV)
    pend_b = pair_start("b", group_b, g)
    dca, do = _out_proj_bwd_act(dya, dyb, wts["w_out_conv"], wts["w_out_attn"], deps=(pend_b[4],))
    ring_b = chip_start("b", group_b, pend_b, do)
    d3, dconv_w = _conv_bwd(proj, conv_full, dca, deps=(ring_b[4],))
    dq, dkc, dkp, dvc, dvp, dsink = _attn_bwd(qn, kn, vb, sink_rows, do)
    dqkv, dgq, dgk = _qk_prep_bwd(proj, gq, gk, rope_tabs, dq, dkc, dkp, dvc, dvp, cw, kw)
    dproj = jnp.concatenate([d3[0], d3[1], d3[2], dqkv, dgates[0], dgates[1]], axis=1)
    g["w_in"] = _proj_bwd_w(h2, dproj)
    pend_c = pair_start("c", group_c, g)
    dh2 = _proj_bwd_act(dproj, w_in_full, deps=(pend_c[4],))
    ring_c = chip_start("c", group_c, pend_c, dh2)
    dx1, dg_mix = _rms_bwd("mix_bwd_rms", x1, g_mix, dh2, dx2, deps=(ring_c[4],))

    big_out = {}
    arrived = {}

    def wait_group(tag, group, ring, after):
        send, recv, parts, lands2, _ = ring
        parts, lands2 = _chip_wait("rs_chip_wait_" + tag, send, recv, parts, lands2, after)
        arrived.update(dict(zip(group, lands2)))

    def update(n, after):
        res = _adamw_chips("adamw_" + n, arrived[n], big[n][0], big_m[n][0], big_v[n][0], row_tile(n), deps=(after,))
        big_out[n] = [a[None] for a in res]
        return res[0]

    def update_beside(n, n_tiles, step_of):
        return _adamw_side(arrived[n], big[n][0], big_m[n][0], big_v[n][0], n_tiles, step_of)

    def keep(n, res):
        big_out[n] = [a[None] for a in res]

    dgu1, a1 = _ffn_bwd_act("ffn1_bwd_act", dx1, wd1, gu1)
    wait_group("a", group_a, ring_a, a1)
    g["w_down1"], *res = _ffn_bwd_dwd("ffn1_bwd_dwd", a1, dx1,
                                       side=update_beside("w_down2", 11, lambda i, j, k: i * 4 + j))
    keep("w_down2", res)
    pend_d = pair_start("d", group_d, g)
    g["w_gu1_lo"], *res = _ffn_bwd_dwgu("ffn1_bwd_dwgu_lo", h1, dgu1, deps=(pend_d[4],), rows=(0, half),
                                         side=update_beside("w_gu2", 16, lambda i, j, k: i * 2 + j))
    keep("w_gu2", res)
    ring_d = chip_start("d", group_d, pend_d, g["w_gu1_lo"])
    pend_e = pair_start("e", group_e, g, deps=(ring_d[4],))
    g["w_gu1_hi"], = _ffn_bwd_dwgu("ffn1_bwd_dwgu_hi", h1, dgu1, deps=(pend_e[4],), rows=(half, half))
    ring_e = chip_start("e", group_e, pend_e, g["w_gu1_hi"])
    pend_f = pair_start("f", group_f, g, deps=(ring_e[4],))
    wait_group("b", group_b, ring_b, pend_f[4])
    after = pend_f[4]
    for n in group_b:
        after = update(n, after)
    ring_f = chip_start("f", group_f, pend_f, after)
    wait_group("c", group_c, ring_c, ring_f[4])
    dh1, *res = _ffn_bwd_dh("ffn1_bwd_dh", dgu1, wts["w_gu1"],
                             side=update_beside("w_in", 16, lambda i, j, k: i * 4 + k))
    keep("w_in", res)
    grad_x, dg_ffn1 = _rms_bwd("ffn1_bwd_rms", xs, g_ffn1, dh1, dx1)
    wait_group("d", group_d, ring_d, grad_x)
    after = update("w_down1", grad_x)
    n = "w_gu1"
    wait_group("e", group_e, ring_e, after)
    res = _adamw_chips("adamw_w_gu1_lo", arrived["w_gu1_lo"], big[n][0], big_m[n][0], big_v[n][0], row_tile(n), deps=(after,))
    wait_group("f", group_f, ring_f, res[0])
    res = _adamw_chips("adamw_w_gu1_hi", arrived["w_gu1_hi"], big[n][0], big_m[n][0], big_v[n][0], row_tile(n),
                       row0=half, into=res)
    keep(n, res)
    after = res[0]

    small = {"g_ffn1": dg_ffn1[0:1], "g_mix": dg_mix[0:1], "g_ffn2": dg_ffn2[0:1],
             "q_norm_g": dgq[0:1, :HEAD_DIM], "k_norm_g": dgk[0:1, :HEAD_DIM], "sinks": dsink[:, 0][None],
             "conv_w": dconv_w[0:CONV_K].reshape(1, -1)}
    small_w = {"g_ffn1": g_ffn1, "g_mix": g_mix, "g_ffn2": g_ffn2, "q_norm_g": q_norm_g, "k_norm_g": k_norm_g,
               "sinks": sinks, "conv_w": None}
    small_m = {"g_ffn1": m_g_ffn1, "g_mix": m_g_mix, "g_ffn2": m_g_ffn2, "q_norm_g": m_q_norm_g,
               "k_norm_g": m_k_norm_g, "sinks": m_sinks, "conv_w": m_conv_w}
    small_v = {"g_ffn1": v_g_ffn1, "g_mix": v_g_mix, "g_ffn2": v_g_ffn2, "q_norm_g": v_q_norm_g,
               "k_norm_g": v_k_norm_g, "sinks": v_sinks, "conv_w": v_conv_w}
    snames = list(small)
    widths = [small[n].shape[1] for n in snames]
    total = sum(widths)
    rows = -(-total // LANES)
    rows = -(-rows // 8) * 8

    def pack(vals):
        flat = jnp.concatenate([v.reshape(1, -1) for v in vals], axis=1)
        return jnp.pad(flat, ((0, 0), (0, rows * LANES - total))).reshape(rows, LANES)

    csh = cw // N_DEV

    def place_conv(local, fill):
        full = jnp.full((CONV_K, cw), fill, F32)
        return lax.dynamic_update_slice(full, local, (0, me * csh)).reshape(1, -1)

    pw = pack([small_w[n] if n != "conv_w" else place_conv(conv_w[0], 0.0) for n in snames])
    pm = pack([small_m[n] if n != "conv_w" else place_conv(m_conv_w[0], 0.0) for n in snames])
    pv = pack([small_v[n] if n != "conv_w" else place_conv(v_conv_w[0], 1.0) for n in snames])
    parts = _all_gather_small("gather_small_grads", pack([small[n] for n in snames]), deps=(after,))
    sg, sd, sm, sv = [a.reshape(1, -1) for a in _adamw("adamw_small", parts, pw, pm, pv, rows)]

    def unpack(flat, n):
        off = sum(widths[:snames.index(n)])
        piece = flat[:, off:off + widths[snames.index(n)]]
        if n == "conv_w":
            piece = lax.dynamic_slice(piece.reshape(CONV_K, cw), (0, me * csh), (CONV_K, csh))[None]
        return piece

    order = ["g_ffn1", "w_gu1", "w_down1", "g_mix", "w_in", "conv_w", "q_norm_g", "k_norm_g", "sinks",
             "w_out_conv", "w_out_attn", "w_o", "g_ffn2", "w_gu2", "w_down2"]
    outs = [loss, grad_x[None]]
    for idx, flat in enumerate((sg, sd, sm, sv)):
        for n in order:
            outs.append(big_out[n][idx] if n in big_out else unpack(flat, n))
    return tuple(outs)
```

```python
import jax
import jax.numpy as jnp
from jax import lax
from jax.experimental import pallas as pl
from jax.experimental.pallas import tpu as pltpu

F32 = jnp.float32
BF16 = jnp.bfloat16

N_DEV = 8
HEAD_DIM = 64
GROUP = 4
BLOCK = 128
ROT_DIM = 16
ROPE_THETA = 500000.0
RMS_EPS = 1e-6
NEG_INF = -1e30
ATTN_SCALE = HEAD_DIM ** -0.5
CONV_K = 3
LANES = 128
MXU_COLS = 256
VMEM_BYTES_V7X = 64 * 1024 * 1024
VMEM_CAP = VMEM_BYTES_V7X - 6 * 1024 * 1024

ADAM_LR = 0.001
ADAM_B1 = 0.9
ADAM_B2 = 0.999
ADAM_EPS = 1e-08
ADAM_WD = 0.01
ADAM_STEP = 10

NN = (((1,), (0,)), ((), ()))
NT = (((1,), (1,)), ((), ()))
TN = (((0,), (0,)), ((), ()))

MESH = pl.DeviceIdType.MESH


def _nbytes(shape, dtype):
    n = 1
    for s in shape:
        if s is not None:
            n *= s
    return n * jnp.dtype(dtype).itemsize


def _params(semantics, block_bytes, temp_bytes):
    assert 2 * block_bytes + temp_bytes <= VMEM_CAP, (block_bytes, temp_bytes)
    return pltpu.CompilerParams(dimension_semantics=semantics, vmem_limit_bytes=VMEM_CAP)


def _fused(name, grid, ins, outs, dots, epilogue, *, nk=1, acc_shape=None, temp_bytes=0,
           semantics=("parallel", "parallel", "arbitrary"), deps=(), side=None):
    n_main_in, n_main_out = len(ins), len(outs)
    if side is not None:
        ins, outs = list(ins) + list(side[0]), list(outs) + list(side[1])
    n_in, n_out = len(ins), len(outs)
    n_dep = len(deps)

    def body(*refs):
        in_refs, out_refs = refs[:n_in], refs[n_in + n_dep:n_in + n_dep + n_out]
        scratch = refs[n_in + n_dep + n_out:]
        if side is not None:
            side[2](in_refs[n_main_in:], out_refs[n_main_out:])

        def products():
            if callable(dots):
                return dots(in_refs)
            total = None
            for ai, bi, contract in dots:
                a, b = in_refs[ai][...], in_refs[bi][...]
                a = a if a.dtype == BF16 else a.astype(BF16)
                b = b if b.dtype == BF16 else b.astype(BF16)
                p = lax.dot_general(a, b, contract, preferred_element_type=F32)
                total = p if total is None else total + p
            return total

        if nk == 1:
            epilogue(products() if dots else None, in_refs, out_refs)
        else:
            acc = scratch[0]
            k = pl.program_id(2)

            @pl.when(k == 0)
            def _():
                acc[...] = jnp.zeros_like(acc)

            acc[...] += products()

            @pl.when(k == nk - 1)
            def _():
                epilogue(acc[...], in_refs, out_refs)

    block_bytes = sum(_nbytes(spec.block_shape, a.dtype) for a, spec in ins)
    block_bytes += sum(_nbytes(spec.block_shape, s.dtype) for s, spec in outs)
    scratch_shapes = []
    if nk > 1:
        scratch_shapes.append(pltpu.VMEM(acc_shape, F32))
        temp_bytes += _nbytes(acc_shape, F32)
    res = pl.pallas_call(
        body, name=name, grid=grid,
        in_specs=[spec for _, spec in ins] + [pl.BlockSpec(memory_space=pl.ANY)] * n_dep,
        out_specs=[spec for _, spec in outs],
        out_shape=[s for s, _ in outs],
        scratch_shapes=scratch_shapes,
        compiler_params=_params(semantics, block_bytes, temp_bytes),
    )(*[a for a, _ in ins], *deps)
    return res


def _sds(shape, dtype):
    return jax.ShapeDtypeStruct(shape, dtype)


def _sigmoid(x):
    return jax.nn.sigmoid(x)


def _all_gather_small(name, shard, deps=()):
    n_dep = len(deps)

    def body(src, *rest):
        dst, send_sems, recv_sems, local_sem = rest[n_dep:]
        x, y, c = lax.axis_index("x"), lax.axis_index("y"), lax.axis_index("c")
        me = 4 * x + 2 * y + c
        copies = [pltpu.make_async_copy(src, dst.at[me], local_sem)]
        for k in range(1, N_DEV):
            peer = ((1 - x) if (k & 4) else x, (1 - y) if (k & 2) else y, (1 - c) if (k & 1) else c)
            copies.append(pltpu.make_async_remote_copy(
                src_ref=src, dst_ref=dst.at[me], send_sem=send_sems.at[k - 1], recv_sem=recv_sems.at[k - 1],
                device_id=peer, device_id_type=MESH))
        for cp in copies:
            cp.start()
        for cp in copies:
            cp.wait()

    hbm = pl.BlockSpec(memory_space=pltpu.HBM)
    return pl.pallas_call(
        body, name=name,
        in_specs=[hbm] + [pl.BlockSpec(memory_space=pl.ANY)] * n_dep, out_specs=hbm,
        out_shape=_sds((N_DEV,) + shard.shape, shard.dtype),
        scratch_shapes=[pltpu.SemaphoreType.DMA((N_DEV - 1,)), pltpu.SemaphoreType.DMA((N_DEV - 1,)),
                        pltpu.SemaphoreType.DMA],
    )(shard, *deps)


_HBM = pl.BlockSpec(memory_space=pltpu.HBM)
_SEM = pl.BlockSpec(memory_space=pltpu.SEMAPHORE)
_ANY = pl.BlockSpec(memory_space=pl.ANY)
_EFFECT = pltpu.SideEffectType.DATAFLOW_SIDE_EFFECTING
N_TARGETS = 4


def _mesh_pos():
    return lax.axis_index("x"), lax.axis_index("y"), lax.axis_index("c")


def _chip_peers(x, y, c):
    return [(1 - x, y, c), (x, 1 - y, c), (1 - x, 1 - y, c)]


def _dev_index(pos):
    return 4 * pos[0] + 2 * pos[1] + pos[2]


def _hbm_like(a):
    return pltpu.HBM(a.shape, a.dtype)


def _place_shard(name, w, out_dtype, me, tr, deps=()):
    r, c = w.shape
    n_dep = len(deps)

    def body(me_ref, w_ref, *rest):
        rest[n_dep][...] = w_ref[...].astype(out_dtype)

    grid_spec = pltpu.PrefetchScalarGridSpec(
        num_scalar_prefetch=1, grid=(r // tr,),
        in_specs=[pl.BlockSpec((tr, c), lambda i, me_ref: (i, 0))] + [_ANY] * n_dep,
        out_specs=pl.BlockSpec((None, tr, c), lambda i, me_ref: (me_ref[0], i, 0)))
    return pl.pallas_call(
        body, name=name, grid_spec=grid_spec, out_shape=_sds((N_DEV, r, c), out_dtype),
        compiler_params=_params(("parallel",), tr * c * 6, tr * c * 4),
    )(me, w, *deps)


def _gather_start(name, lands):
    n = len(lands)

    def body(*refs):
        bufs = refs[:n]
        send, recv = refs[n], refs[n + 1]
        token = refs[-1]
        x, y, c = _mesh_pos()
        me = _dev_index((x, y, c))
        targets = [(x, y, 1 - c)] + _chip_peers(x, y, c)
        for w in range(n):
            for k, to in enumerate(targets):
                pltpu.make_async_remote_copy(
                    src_ref=bufs[w].at[me], dst_ref=bufs[w].at[me],
                    send_sem=send.at[N_TARGETS * w + k], recv_sem=recv.at[N_TARGETS * w + k],
                    device_id=to, device_id_type=MESH).start()
        token[...] = jnp.zeros_like(token)

    sems = pltpu.SemaphoreType.DMA((N_TARGETS * n,))
    outs = pl.pallas_call(
        body, name=name,
        in_specs=[_HBM] * n, out_specs=[_SEM, _SEM] + [_HBM] * n + [_token_spec()],
        out_shape=[sems, sems] + [_hbm_like(a) for a in lands] + [_sds((8, LANES), F32)],
        input_output_aliases={i: 2 + i for i in range(n)},
        compiler_params=pltpu.CompilerParams(has_side_effects=_EFFECT),
    )(*lands)
    return outs[0], outs[1], list(outs[2:2 + n]), outs[-1]


def _gather_wait(name, positions, send, recv, lands, after):
    m = len(positions)

    def body(*refs):
        bufs = refs[:m]
        send_sems, recv_sems = refs[m], refs[m + 1]
        x, y, c = _mesh_pos()
        me = _dev_index((x, y, c))
        sources = [(x, y, 1 - c)] + _chip_peers(x, y, c)
        for j, w in enumerate(positions):
            for k, frm in enumerate(sources):
                cp = pltpu.make_async_remote_copy(
                    src_ref=bufs[j].at[me], dst_ref=bufs[j].at[_dev_index(frm)],
                    send_sem=send_sems.at[N_TARGETS * w + k], recv_sem=recv_sems.at[N_TARGETS * w + k],
                    device_id=frm, device_id_type=MESH)
                cp.wait_send()
                cp.wait_recv()

    outs = pl.pallas_call(
        body, name=name,
        in_specs=[_HBM] * m + [_SEM, _SEM, _ANY], out_specs=[_HBM] * m,
        out_shape=[_hbm_like(a) for a in lands],
        input_output_aliases={i: i for i in range(m)},
        compiler_params=pltpu.CompilerParams(has_side_effects=_EFFECT),
    )(*lands, send, recv, after)
    return list(outs)


def _forward_to_sibling(name, lands):
    m = len(lands)

    def body(*refs):
        bufs = refs[m:2 * m]
        send_sems, recv_sems = refs[2 * m], refs[2 * m + 1]
        x, y, c = _mesh_pos()
        copies = []
        for j in range(m):
            for k, chip in enumerate(_chip_peers(x, y, c)):
                block = bufs[j].at[_dev_index(chip)]
                cp = pltpu.make_async_remote_copy(
                    src_ref=block, dst_ref=block,
                    send_sem=send_sems.at[3 * j + k], recv_sem=recv_sems.at[3 * j + k],
                    device_id=(x, y, 1 - c), device_id_type=MESH)
                cp.start()
                copies.append(cp)
        for cp in copies:
            cp.wait()

    outs = pl.pallas_call(
        body, name=name,
        in_specs=[_HBM] * m, out_specs=[_HBM] * m,
        out_shape=[_sds(a.shape, a.dtype) for a in lands],
        input_output_aliases={i: i for i in range(m)},
        scratch_shapes=[pltpu.SemaphoreType.DMA((3 * m,)), pltpu.SemaphoreType.DMA((3 * m,))],
    )(*lands)
    return list(outs)


def _token_spec():
    return pl.BlockSpec(memory_space=pltpu.VMEM)


def _pair_start(name, stacks, lands, deps=()):
    n = len(stacks)
    n_dep = len(deps)

    def body(*refs):
        srcs, dsts = refs[:n], refs[n:2 * n]
        send, recv = refs[2 * n + n_dep], refs[2 * n + n_dep + 1]
        token = refs[-1]
        x, y, c = _mesh_pos()
        for w in range(n):
            for chip in range(4):
                pltpu.make_async_remote_copy(
                    src_ref=srcs[w].at[chip, 1 - c], dst_ref=dsts[w].at[chip],
                    send_sem=send.at[4 * w + chip], recv_sem=recv.at[4 * w + chip],
                    device_id=(x, y, 1 - c), device_id_type=MESH).start()
        token[...] = jnp.zeros_like(token)

    sems = pltpu.SemaphoreType.DMA((4 * n,))
    outs = pl.pallas_call(
        body, name=name,
        in_specs=[_HBM] * (2 * n) + [_ANY] * n_dep, out_specs=[_SEM, _SEM] + [_HBM] * (2 * n) + [_token_spec()],
        out_shape=[sems, sems] + [_hbm_like(a) for a in stacks] + [_hbm_like(a) for a in lands] + [_sds((8, LANES), F32)],
        input_output_aliases={i: 2 + i for i in range(2 * n)},
        compiler_params=pltpu.CompilerParams(has_side_effects=_EFFECT),
    )(*stacks, *lands, *deps)
    return outs[0], outs[1], list(outs[2:2 + n]), list(outs[2 + n:2 + 2 * n]), outs[-1]


def _pair_wait(name, send, recv, stacks, lands, after):
    n = len(stacks)

    def body(*refs):
        srcs, dsts = refs[:n], refs[n:2 * n]
        send_sems, recv_sems = refs[2 * n], refs[2 * n + 1]
        x, y, c = _mesh_pos()
        for w in range(n):
            for chip in range(4):
                cp = pltpu.make_async_remote_copy(
                    src_ref=srcs[w].at[chip, 1 - c], dst_ref=dsts[w].at[chip],
                    send_sem=send_sems.at[4 * w + chip], recv_sem=recv_sems.at[4 * w + chip],
                    device_id=(x, y, 1 - c), device_id_type=MESH)
                cp.wait_send()
                cp.wait_recv()

    outs = pl.pallas_call(
        body, name=name,
        in_specs=[_HBM] * (2 * n) + [_SEM, _SEM, _ANY], out_specs=[_HBM] * (2 * n),
        out_shape=[_hbm_like(a) for a in stacks] + [_hbm_like(a) for a in lands],
        input_output_aliases={i: i for i in range(2 * n)},
        compiler_params=pltpu.CompilerParams(has_side_effects=_EFFECT),
    )(*stacks, *lands, send, recv, after)
    return list(outs[:n]), list(outs[n:])


def _pair_add(name, stack, land, place, tr):
    _, _, r, c = stack.shape

    def body(place_ref, a_ref, b_ref, sums_ref, slots_ref):
        total = (a_ref[...].astype(F32) + b_ref[...].astype(F32)).astype(BF16)
        sums_ref[...] = total

        @pl.when(pl.program_id(1) == place_ref[1])
        def _():
            slots_ref[...] = total

    grid_spec = pltpu.PrefetchScalarGridSpec(
        num_scalar_prefetch=1, grid=(r // tr, 4),
        in_specs=[pl.BlockSpec((None, None, tr, c), lambda i, k, place_ref: (k, place_ref[0], i, 0)),
                  pl.BlockSpec((None, tr, c), lambda i, k, place_ref: (k, i, 0))],
        out_specs=[pl.BlockSpec((None, tr, c), lambda i, k, place_ref: (k, i, 0)),
                   pl.BlockSpec((None, tr, c), lambda i, k, place_ref: (place_ref[1], i, 0))])
    return pl.pallas_call(
        body, name=name, grid_spec=grid_spec, out_shape=[_sds((4, r, c), BF16)] * 2,
        compiler_params=_params(("parallel", "arbitrary"), 4 * tr * c * 2, 3 * tr * c * 4),
    )(place, stack, land)


def _chip_start(name, parts, lands):
    n = len(parts)

    def body(*refs):
        srcs, dsts = refs[:n], refs[n:2 * n]
        send, recv = refs[2 * n], refs[2 * n + 1]
        token = refs[-1]
        x, y, c = _mesh_pos()
        for w in range(n):
            for k, to in enumerate(_chip_peers(x, y, c)):
                pltpu.make_async_remote_copy(
                    src_ref=srcs[w].at[2 * to[0] + to[1]], dst_ref=dsts[w].at[2 * x + y],
                    send_sem=send.at[3 * w + k], recv_sem=recv.at[3 * w + k],
                    device_id=to, device_id_type=MESH).start()
        token[...] = jnp.zeros_like(token)

    sems = pltpu.SemaphoreType.DMA((3 * n,))
    outs = pl.pallas_call(
        body, name=name,
        in_specs=[_HBM] * (2 * n), out_specs=[_SEM, _SEM] + [_HBM] * (2 * n) + [_token_spec()],
        out_shape=[sems, sems] + [_hbm_like(a) for a in parts] + [_hbm_like(a) for a in lands] + [_sds((8, LANES), F32)],
        input_output_aliases={i: 2 + i for i in range(2 * n)},
        compiler_params=pltpu.CompilerParams(has_side_effects=_EFFECT),
    )(*parts, *lands)
    return outs[0], outs[1], list(outs[2:2 + n]), list(outs[2 + n:2 + 2 * n]), outs[-1]


def _chip_wait(name, send, recv, parts, lands, after):
    n = len(parts)

    def body(*refs):
        srcs, dsts = refs[:n], refs[n:2 * n]
        send_sems, recv_sems = refs[2 * n], refs[2 * n + 1]
        x, y, c = _mesh_pos()
        for w in range(n):
            for k, frm in enumerate(_chip_peers(x, y, c)):
                chip = 2 * frm[0] + frm[1]
                cp = pltpu.make_async_remote_copy(
                    src_ref=srcs[w].at[chip], dst_ref=dsts[w].at[chip],
                    send_sem=send_sems.at[3 * w + k], recv_sem=recv_sems.at[3 * w + k],
                    device_id=frm, device_id_type=MESH)
                cp.wait_send()
                cp.wait_recv()

    outs = pl.pallas_call(
        body, name=name,
        in_specs=[_HBM] * (2 * n) + [_SEM, _SEM, _ANY], out_specs=[_HBM] * (2 * n),
        out_shape=[_hbm_like(a) for a in parts] + [_hbm_like(a) for a in lands],
        input_output_aliases={i: i for i in range(2 * n)},
        compiler_params=pltpu.CompilerParams(has_side_effects=_EFFECT),
    )(*parts, *lands, send, recv, after)
    return list(outs[:n]), list(outs[n:])


def _row_tile(t):
    return min(t, 256)


def _rms_fwd(name, x, g):
    t, d = x.shape
    tm = _row_tile(t)

    def epilogue(_, ins, outs):
        xv = ins[0][...]
        r = lax.rsqrt(jnp.mean(xv * xv, axis=-1, keepdims=True) + RMS_EPS)
        outs[0][...] = (xv * r * ins[1][...]).astype(BF16)

    row = pl.BlockSpec((tm, d), lambda i, j, k: (i, 0))
    vec = pl.BlockSpec((1, d), lambda i, j, k: (0, 0))
    return _fused(name, (t // tm, 1, 1), [(x, row), (g, vec)], [(_sds((t, d), BF16), row)], [], epilogue,
                  temp_bytes=4 * tm * d * 4)[0]


def _rms_bwd(name, x, g, dh, resid, deps=(), with_bf16=False):
    t, d = x.shape
    tm = _row_tile(t)

    def epilogue(_, ins, outs):
        xv, gv, dhv = ins[0][...], ins[1][...], ins[2][...]
        r = lax.rsqrt(jnp.mean(xv * xv, axis=-1, keepdims=True) + RMS_EPS)
        xh = xv * r
        u = dhv * gv
        dot = jnp.mean(u * xh, axis=-1, keepdims=True)
        dx = ins[3][...] + r * (u - xh * dot)
        outs[0][...] = dx
        if with_bf16:
            outs[2][...] = dx.astype(BF16)

        @pl.when(pl.program_id(0) == 0)
        def _():
            outs[1][...] = jnp.zeros_like(outs[1])

        outs[1][0:1, :] += jnp.sum(dhv * xh, axis=0, keepdims=True)

    row = pl.BlockSpec((tm, d), lambda i, j, k: (i, 0))
    vec = pl.BlockSpec((1, d), lambda i, j, k: (0, 0))
    acc = pl.BlockSpec((8, d), lambda i, j, k: (0, 0))
    outs = [(_sds((t, d), F32), row), (_sds((8, d), F32), acc)] + ([(_sds((t, d), BF16), row)] if with_bf16 else [])
    return _fused(name, (t // tm, 1, 1), [(x, row), (g, vec), (dh, row), (resid, row)], outs, [], epilogue,
                  temp_bytes=6 * tm * d * 4, semantics=("arbitrary", "arbitrary", "arbitrary"), deps=deps)


def _ffn_up(name, h, wgu):
    t, d = h.shape
    nb = wgu.shape[2]
    f = 4 * nb
    tm = min(t, 512)

    def body(h_ref, wg_ref, wu_ref, gu_ref, a_ref):
        hv = h_ref[...]
        for c0 in range(0, nb, MXU_COLS):
            cs = slice(c0, min(c0 + MXU_COLS, nb))
            g = jnp.dot(hv, wg_ref[:, cs], preferred_element_type=F32)
            u = jnp.dot(hv, wu_ref[:, cs], preferred_element_type=F32)
            gu_ref[0, :, cs] = g.astype(BF16)
            gu_ref[1, :, cs] = u.astype(BF16)
            a_ref[:, cs] = (g * _sigmoid(g) * u).astype(BF16)

    blocks = tm * d * 2 + 2 * d * nb * 2 + 3 * tm * nb * 2
    return pl.pallas_call(
        body, name=name, grid=(4, t // tm),
        in_specs=[pl.BlockSpec((tm, d), lambda j, i: (i, 0)),
                  pl.BlockSpec((None, d, nb), lambda j, i: (j, 0, 0)),
                  pl.BlockSpec((None, d, nb), lambda j, i: (j + 4, 0, 0))],
        out_specs=[pl.BlockSpec((2, tm, nb), lambda j, i: (0, i, j)),
                   pl.BlockSpec((tm, nb), lambda j, i: (i, j))],
        out_shape=[_sds((2, t, f), BF16), _sds((t, f), BF16)],
        compiler_params=_params(("parallel", "parallel"), blocks, 8 * tm * MXU_COLS * 4),
    )(h, wgu, wgu)


def _ffn_down(name, a, wd, x, target=None):
    t, f = a.shape
    d = wd.shape[1]
    tm = min(t, 512)
    tn = min(d, 1024)
    blk = pl.BlockSpec((tm, tn), lambda j, i, k: (i, j))
    ins = [(a, pl.BlockSpec((tm, f), lambda j, i, k: (i, 0))), (wd, pl.BlockSpec((f, tn), lambda j, i, k: (0, j))), (x, blk)]

    if target is None:
        def epilogue(acc, ins, outs):
            outs[0][...] = ins[2][...] + 0.5 * acc

        return _fused(name, (d // tn, t // tm, 1), ins, [(_sds((t, d), F32), blk)],
                      [(0, 1, NN)], epilogue, temp_bytes=2 * tm * tn * 4)[0]

    def epilogue(acc, ins, outs):
        e = ins[2][...] + 0.5 * acc - ins[3][...]
        outs[0][...] = e * (1.0 / d)
        outs[2][...] = (e * (1.0 / d)).astype(BF16)

        @pl.when((pl.program_id(0) == 0) & (pl.program_id(1) == 0))
        def _():
            outs[1][...] = jnp.zeros_like(outs[1])

        part = jnp.sum(jnp.sum(e * e, axis=1, keepdims=True), axis=0, keepdims=True)
        outs[1][...] += jnp.broadcast_to(part, outs[1].shape)

    return _fused(name, (d // tn, t // tm, 1), ins + [(target, blk)],
                  [(_sds((t, d), F32), blk), (_sds((8, LANES), F32), pl.BlockSpec((8, LANES), lambda j, i, k: (0, 0))),
                   (_sds((t, d), BF16), blk)],
                  [(0, 1, NN)], epilogue, temp_bytes=3 * tm * tn * 4,
                  semantics=("arbitrary", "arbitrary", "arbitrary"))


def _ffn_bwd_act(name, dy, wd, gu, deps=()):
    t, d = dy.shape
    f = wd.shape[0]
    nb = f // 4
    tm = min(t, 512)

    def body(dy_ref, wd_ref, gu_ref, *rest):
        dgu_ref, a_ref = rest[-2], rest[-1]
        dyv = dy_ref[...].astype(BF16)
        for c0 in range(0, nb, MXU_COLS):
            cs = slice(c0, min(c0 + MXU_COLS, nb))
            da = 0.5 * lax.dot_general(dyv, wd_ref[cs, :], NT, preferred_element_type=F32)
            g = gu_ref[0, :, cs].astype(F32)
            u = gu_ref[1, :, cs].astype(F32)
            s = _sigmoid(g)
            silu = g * s
            dgu_ref[0, :, cs] = (da * u * (s * (1.0 + g * (1.0 - s)))).astype(BF16)
            dgu_ref[1, :, cs] = (da * silu).astype(BF16)
            a_ref[:, cs] = (silu * u).astype(BF16)

    blocks = tm * d * 4 + nb * d * 2 + 5 * tm * nb * 2
    return pl.pallas_call(
        body, name=name, grid=(4, t // tm),
        in_specs=[pl.BlockSpec((tm, d), lambda j, i: (i, 0)),
                  pl.BlockSpec((nb, d), lambda j, i: (j, 0)),
                  pl.BlockSpec((2, tm, nb), lambda j, i: (0, i, j))] + [_ANY] * len(deps),
        out_specs=[pl.BlockSpec((2, tm, nb), lambda j, i: (0, i, j)), pl.BlockSpec((tm, nb), lambda j, i: (i, j))],
        out_shape=[_sds((2, t, f), BF16), _sds((t, f), BF16)],
        compiler_params=_params(("parallel", "parallel"), blocks, tm * d * 2 + 8 * tm * MXU_COLS * 4),
    )(dy, wd, gu, *deps)


def _ffn_bwd_dwd(name, a, dy, deps=(), side=None):
    t, f = a.shape
    d = dy.shape[1]
    tm = f // 4
    tn = min(d, 512)

    def epilogue(acc, ins, outs):
        outs[0][...] = (0.5 * acc).astype(BF16)

    return _fused(name, (4, d // tn, 1),
                  [(a, pl.BlockSpec((t, tm), lambda i, j, k: (0, i))),
                   (dy, pl.BlockSpec((t, tn), lambda i, j, k: (0, j)))],
                  [(_sds((f, d), BF16), pl.BlockSpec((tm, tn), lambda i, j, k: (i, j)))],
                  [(0, 1, TN)], epilogue, temp_bytes=t * tn * 2 + 2 * tm * tn * 4, deps=deps, side=side)


def _ffn_bwd_dh(name, dgu, wgu, deps=(), side=None):
    _, t, f = dgu.shape
    d, nb = wgu.shape[1], wgu.shape[2]
    tm = min(t, 512)

    def products(ins):
        return (lax.dot_general(ins[0][:, 0:nb], ins[1][0], NT, preferred_element_type=F32)
                + lax.dot_general(ins[0][:, nb:2 * nb], ins[1][1], NT, preferred_element_type=F32))

    def epilogue(acc, ins, outs):
        outs[0][...] = acc

    return _fused(name, (t // tm, 1, 4),
                  [(dgu, pl.BlockSpec((None, tm, 2 * nb), lambda i, j, k: (k // 2, i, k % 2))),
                   (wgu, pl.BlockSpec((2, d, nb), lambda i, j, k: (k, 0, 0)))],
                  [(_sds((t, d), F32), pl.BlockSpec((tm, d), lambda i, j, k: (i, 0)))],
                  products, epilogue, nk=4, acc_shape=(tm, d), temp_bytes=tm * d * 4, deps=deps, side=side)


def _ffn_bwd_dwgu(name, h, dgu, deps=(), side=None, rows=None):
    t, d = h.shape
    nb = dgu.shape[2] // 4
    tm = min(d, 512)
    row0, nrows = rows if rows is not None else (0, d)
    j0 = row0 // tm

    def epilogue(acc, ins, outs):
        outs[0][...] = acc.astype(BF16)

    return _fused(name, (N_DEV, nrows // tm, 1),
                  [(h, pl.BlockSpec((t, tm), lambda i, j, k: (0, j0 + j))),
                   (dgu, pl.BlockSpec((None, t, nb), lambda i, j, k: (i // 4, 0, i % 4)))],
                  [(_sds((N_DEV, nrows, nb), BF16), pl.BlockSpec((None, tm, nb), lambda i, j, k: (i, j, 0)))],
                  [(0, 1, TN)], epilogue, temp_bytes=2 * tm * nb * 4, deps=deps, side=side)


def _proj(h, w_in):
    t, d = h.shape
    nb = w_in.shape[3]
    tm = min(t, 512)

    def body(h_ref, w_ref, o_ref):
        hv = h_ref[...]
        o_ref[:, 0:nb] = jnp.dot(hv, w_ref[0], preferred_element_type=F32).astype(BF16)
        o_ref[:, nb:2 * nb] = jnp.dot(hv, w_ref[1], preferred_element_type=F32).astype(BF16)

    blocks = tm * d * 2 + 2 * d * nb * 2 + tm * 2 * nb * 4
    return pl.pallas_call(
        body, name="mix_proj", grid=(4, t // tm),
        in_specs=[pl.BlockSpec((tm, d), lambda j, i: (i, 0)),
                  pl.BlockSpec((None, 2, d, nb), lambda j, i: (j, 0, 0, 0))],
        out_specs=pl.BlockSpec((tm, 2 * nb), lambda j, i: (i, j)),
        out_shape=_sds((t, N_DEV * nb), BF16),
        compiler_params=_params(("parallel", "parallel"), blocks, 2 * tm * nb * 4),
    )(h, w_in)


def _shift_rows(u, k):
    t = u.shape[0]
    rolled = pltpu.roll(u, k % t, axis=0)
    row = lax.broadcasted_iota(jnp.int32, u.shape, 0)
    keep = (row >= k) if k > 0 else (row < t + k)
    return jnp.where(keep, rolled, 0.0)


def _conv_fwd(proj, conv_w):
    t = proj.shape[0]
    cw = conv_w.shape[1]
    tc = min(cw, 256)
    nc = cw // tc

    def epilogue(_, ins, outs):
        u = ins[2][...].astype(F32) * ins[0][...].astype(F32)
        w = ins[3][...]
        y = u * w[2:3, :] + _shift_rows(u, 1) * w[1:2, :] + _shift_rows(u, 2) * w[0:1, :]
        outs[0][...] = (ins[1][...].astype(F32) * y).astype(BF16)

    def col(seg):
        return pl.BlockSpec((t, tc), lambda i, j, k: (0, seg * nc + i))

    return _fused("conv_fwd", (nc, 1, 1),
                  [(proj, col(0)), (proj, col(1)), (proj, col(2)),
                   (conv_w, pl.BlockSpec((8, tc), lambda i, j, k: (0, i)))],
                  [(_sds((t, cw), BF16), pl.BlockSpec((t, tc), lambda i, j, k: (0, i)))],
                  [], epilogue, temp_bytes=6 * t * tc * 4)[0]


def _conv_bwd(proj, conv_w, dca, deps=()):
    t = proj.shape[0]
    cw = conv_w.shape[1]
    tc = min(cw, 256)
    nc = cw // tc

    def epilogue(_, ins, outs):
        xc, bg, cg = ins[0][...].astype(F32), ins[1][...].astype(F32), ins[2][...].astype(F32)
        w, dc = ins[3][...], ins[4][...]
        u = cg * xc
        u1, u2 = _shift_rows(u, 1), _shift_rows(u, 2)
        y = u * w[2:3, :] + u1 * w[1:2, :] + u2 * w[0:1, :]
        dconv = dc * bg
        du = dconv * w[2:3, :] + _shift_rows(dconv, -1) * w[1:2, :] + _shift_rows(dconv, -2) * w[0:1, :]
        outs[0][0] = (du * cg).astype(BF16)
        outs[0][1] = (dc * y).astype(BF16)
        outs[0][2] = (du * xc).astype(BF16)
        outs[1][...] = jnp.zeros_like(outs[1])
        outs[1][0:1, :] = jnp.sum(dconv * u2, axis=0, keepdims=True)
        outs[1][1:2, :] = jnp.sum(dconv * u1, axis=0, keepdims=True)
        outs[1][2:3, :] = jnp.sum(dconv * u, axis=0, keepdims=True)

    def col(seg):
        return pl.BlockSpec((t, tc), lambda i, j, k: (0, seg * nc + i))

    own = pl.BlockSpec((t, tc), lambda i, j, k: (0, i))
    wspec = pl.BlockSpec((8, tc), lambda i, j, k: (0, i))
    return _fused("conv_bwd", (nc, 1, 1),
                  [(proj, col(0)), (proj, col(1)), (proj, col(2)), (conv_w, wspec), (dca, own)],
                  [(_sds((3, t, cw), BF16), pl.BlockSpec((3, t, tc), lambda i, j, k: (0, 0, i))),
                   (_sds((8, cw), F32), wspec)],
                  [], epilogue, temp_bytes=10 * t * tc * 4, deps=deps)


def _split3(x):
    hi = x.astype(BF16)
    r1 = x - hi.astype(F32)
    mid = r1.astype(BF16)
    lo = (r1 - mid.astype(F32)).astype(BF16)
    return hi, mid, lo


def _head_selector(width):
    r = lax.broadcasted_iota(jnp.int32, (width, LANES), 0)
    c = lax.broadcasted_iota(jnp.int32, (width, LANES), 1)
    return (lax.shift_right_logical(r, 6) == c).astype(BF16)


def _head_sum(x, sel):
    return sum(jnp.dot(p, sel, preferred_element_type=F32) for p in _split3(x))


def _head_bcast(r, sel):
    return sum(lax.dot_general(p, sel, NT, preferred_element_type=F32) for p in _split3(r))


def _rope(x, c, sa, sb):
    n = x.shape[1]
    return x * c + pltpu.roll(x, n - ROT_DIM // 2, axis=1) * sa + pltpu.roll(x, ROT_DIM // 2, axis=1) * sb


def _rope_t(d, c, sa, sb):
    n = d.shape[1]
    return d * c + pltpu.roll(d * sa, ROT_DIM // 2, axis=1) + pltpu.roll(d * sb, n - ROT_DIM // 2, axis=1)


def _tile_lanes(tab, width):
    return tab if width == tab.shape[1] else jnp.tile(tab, (1, width // tab.shape[1]))


def _qk_prep(proj, gq, gk, rope_tabs, cw, kw):
    t = proj.shape[0]
    tm = _row_tile(t)

    def epilogue(_, ins, outs):
        c, sa, sb = ins[5][...], ins[6][...], ins[7][...]
        for src, gain, dst, width in ((0, 3, 0, cw), (1, 4, 1, kw)):
            xv = ins[src][...].astype(F32)
            sel = _head_selector(width)
            r = lax.rsqrt(_head_sum(xv * xv, sel) * (1.0 / HEAD_DIM) + RMS_EPS)
            xn = xv * _head_bcast(r, sel) * ins[gain][...]
            outs[dst][...] = _rope(xn, _tile_lanes(c, width), _tile_lanes(sa, width), _tile_lanes(sb, width)).astype(BF16)
        outs[2][...] = ins[2][...].astype(BF16)

    kblk = cw // kw
    tab = pl.BlockSpec((tm, LANES), lambda i, j, k: (i, 0))
    kspec = pl.BlockSpec((tm, kw), lambda i, j, k: (i, 0))
    return _fused("qk_prep", (t // tm, 1, 1),
                  [(proj, pl.BlockSpec((tm, cw), lambda i, j, k: (i, 3))),
                   (proj, pl.BlockSpec((tm, kw), lambda i, j, k: (i, 4 * kblk))),
                   (proj, pl.BlockSpec((tm, kw), lambda i, j, k: (i, 4 * kblk + 1))),
                   (gq, pl.BlockSpec((1, cw), lambda i, j, k: (0, 0))),
                   (gk, pl.BlockSpec((1, kw), lambda i, j, k: (0, 0))),
                   (rope_tabs[0], tab), (rope_tabs[1], tab), (rope_tabs[2], tab)],
                  [(_sds((t, cw), BF16), pl.BlockSpec((tm, cw), lambda i, j, k: (i, 0))),
                   (_sds((t, kw), BF16), kspec), (_sds((t, kw), BF16), kspec)],
                  [], epilogue, temp_bytes=12 * tm * cw * 4)


def _qk_prep_bwd(proj, gq, gk, rope_tabs, dq, dkc, dkp, dvc, dvp, cw, kw):
    t = proj.shape[0]
    tm = BLOCK
    nblk = t // tm

    def epilogue(_, ins, outs):
        c, sa, sb = ins[5][...], ins[6][...], ins[7][...]
        has_next = (pl.program_id(0) < nblk - 1).astype(F32)
        dk = ins[9][...] + has_next * ins[10][...]
        dv = ins[11][...] + has_next * ins[12][...]
        pieces = []
        for src, gain, dval, dst, width in ((0, 3, ins[8][...], 1, cw), (1, 4, dk, 2, kw)):
            xv, gv = ins[src][...].astype(F32), ins[gain][...]
            sel = _head_selector(width)
            r = _head_bcast(lax.rsqrt(_head_sum(xv * xv, sel) * (1.0 / HEAD_DIM) + RMS_EPS), sel)
            xh = xv * r
            dxn = _rope_t(dval, _tile_lanes(c, width), _tile_lanes(sa, width), _tile_lanes(sb, width))
            u = dxn * gv
            dot = _head_bcast(_head_sum(u * xh, sel), sel) * (1.0 / HEAD_DIM)
            pieces.append((r * (u - xh * dot)).astype(BF16))
            ri = lax.broadcasted_iota(jnp.int32, (width, LANES), 0)
            ci = lax.broadcasted_iota(jnp.int32, (width, LANES), 1)
            fold = (lax.bitwise_and(ri, HEAD_DIM - 1) == ci).astype(BF16)
            colsum = jnp.broadcast_to(jnp.sum(dxn * xh, axis=0, keepdims=True), (8, width))
            part = sum(jnp.dot(p, fold, preferred_element_type=F32) for p in _split3(colsum))

            @pl.when(pl.program_id(0) == 0)
            def _():
                outs[dst][...] = jnp.zeros_like(outs[dst])

            outs[dst][0:1, :] += part[0:1, :]
        outs[0][:, 0:cw] = pieces[0]
        outs[0][:, cw:cw + kw] = pieces[1]
        outs[0][:, cw + kw:cw + 2 * kw] = dv.astype(BF16)

    kblk = cw // kw
    tab = pl.BlockSpec((tm, LANES), lambda i, j, k: (i, 0))
    kcur = pl.BlockSpec((tm, kw), lambda i, j, k: (i, 0))
    knext = pl.BlockSpec((tm, kw), lambda i, j, k: (jnp.minimum(i + 1, nblk - 1), 0))
    acc = pl.BlockSpec((8, LANES), lambda i, j, k: (0, 0))
    return _fused("qk_prep_bwd", (nblk, 1, 1),
                  [(proj, pl.BlockSpec((tm, cw), lambda i, j, k: (i, 3))),
                   (proj, pl.BlockSpec((tm, kw), lambda i, j, k: (i, 4 * kblk))),
                   (proj, pl.BlockSpec((tm, kw), lambda i, j, k: (i, 4 * kblk + 1))),
                   (gq, pl.BlockSpec((1, cw), lambda i, j, k: (0, 0))),
                   (gk, pl.BlockSpec((1, kw), lambda i, j, k: (0, 0))),
                   (rope_tabs[0], tab), (rope_tabs[1], tab), (rope_tabs[2], tab),
                   (dq, pl.BlockSpec((tm, cw), lambda i, j, k: (i, 0))),
                   (dkc, kcur), (dkp, knext), (dvc, kcur), (dvp, knext)],
                  [(_sds((t, cw + 2 * kw), BF16), pl.BlockSpec((tm, cw + 2 * kw), lambda i, j, k: (i, 0))),
                   (_sds((8, LANES), F32), acc), (_sds((8, LANES), F32), acc)],
                  [], epilogue, temp_bytes=16 * tm * cw * 4, semantics=("arbitrary", "arbitrary", "arbitrary"))


def _attn_mask(n):
    key = lax.broadcasted_iota(jnp.int32, (2 * BLOCK, GROUP * BLOCK), 0)
    qry = lax.bitwise_and(lax.broadcasted_iota(jnp.int32, (2 * BLOCK, GROUP * BLOCK), 1), BLOCK - 1)
    return (key > qry) & (key <= qry + BLOCK) & ((key >= BLOCK) | (n > 0))


def _stack_heads(x, h):
    return jnp.concatenate([x[:, (h * GROUP + g) * HEAD_DIM:(h * GROUP + g + 1) * HEAD_DIM] for g in range(GROUP)], axis=0)


def _softmax_with_sink(q4, k2, sink_ref, h, valid):
    sink = jnp.concatenate([sink_ref[h * GROUP + g:h * GROUP + g + 1, :] for g in range(GROUP)], axis=1)
    s = lax.dot_general(k2, q4, NT, preferred_element_type=F32) * ATTN_SCALE
    s = jnp.where(valid, s, NEG_INF)
    m = jnp.maximum(jnp.max(s, axis=0, keepdims=True), sink)
    p = jnp.exp(s - m)
    es = jnp.exp(sink - m)
    inv = 1.0 / (jnp.sum(p, axis=0, keepdims=True) + es)
    return p * inv, es * inv


def _attn_fwd(qn, kn, vb, sink_rows):
    t, cw = qn.shape
    kw = kn.shape[1]
    nkv = kw // HEAD_DIM

    def body(q_ref, kp_ref, kc_ref, vp_ref, vc_ref, sink_ref, o_ref):
        valid = _attn_mask(pl.program_id(0))
        qv = q_ref[...]
        kp, kc, vp, vc = kp_ref[...], kc_ref[...], vp_ref[...], vc_ref[...]
        outs = []
        for h in range(nkv):
            hs = slice(h * HEAD_DIM, (h + 1) * HEAD_DIM)
            k2 = jnp.concatenate([kp[:, hs], kc[:, hs]], axis=0)
            v2 = jnp.concatenate([vp[:, hs], vc[:, hs]], axis=0)
            pn, _ = _softmax_with_sink(_stack_heads(qv, h), k2, sink_ref, h, valid)
            o4 = lax.dot_general(pn.astype(BF16), v2, TN, preferred_element_type=F32)
            outs += [o4[g * BLOCK:(g + 1) * BLOCK] for g in range(GROUP)]
        o_ref[...] = jnp.concatenate(outs, axis=-1).astype(BF16)

    cur = lambda n: (n, 0)
    prev = lambda n: (jnp.maximum(n - 1, 0), 0)
    return pl.pallas_call(
        body, name="attn_fwd", grid=(t // BLOCK,),
        in_specs=[pl.BlockSpec((BLOCK, cw), cur),
                  pl.BlockSpec((BLOCK, kw), prev), pl.BlockSpec((BLOCK, kw), cur),
                  pl.BlockSpec((BLOCK, kw), prev), pl.BlockSpec((BLOCK, kw), cur),
                  pl.BlockSpec(sink_rows.shape, lambda n: (0, 0))],
        out_specs=pl.BlockSpec((BLOCK, cw), cur),
        out_shape=_sds((t, cw), BF16),
        compiler_params=_params(("parallel",), BLOCK * (cw + 4 * kw) * 2 + BLOCK * cw * 2, 8 << 20),
    )(qn, kn, kn, vb, vb, sink_rows)


def _attn_bwd(qn, kn, vb, sink_rows, do):
    t, cw = qn.shape
    kw = kn.shape[1]
    nkv = kw // HEAD_DIM
    nq = nkv * GROUP

    def body(q_ref, kp_ref, kc_ref, vp_ref, vc_ref, sink_ref, do_ref,
             dq_ref, dkc_ref, dkp_ref, dvc_ref, dvp_ref, dsink_ref):
        n = pl.program_id(0)
        valid = _attn_mask(n)
        qv, dov = q_ref[...], do_ref[...]
        kp, kc, vp, vc = kp_ref[...], kc_ref[...], vp_ref[...], vc_ref[...]
        dqs, dks, dvs, dsinks = [], [], [], []
        for h in range(nkv):
            hs = slice(h * HEAD_DIM, (h + 1) * HEAD_DIM)
            k2 = jnp.concatenate([kp[:, hs], kc[:, hs]], axis=0)
            v2 = jnp.concatenate([vp[:, hs], vc[:, hs]], axis=0)
            q4 = _stack_heads(qv, h)
            dob = _stack_heads(dov, h).astype(BF16)
            pn, psink = _softmax_with_sink(q4, k2, sink_ref, h, valid)
            dpn = lax.dot_general(v2, dob, NT, preferred_element_type=F32)
            dvs.append(jnp.dot(pn.astype(BF16), dob, preferred_element_type=F32))
            delta = jnp.sum(pn * dpn, axis=0, keepdims=True)
            ds = (pn * (dpn - delta) * ATTN_SCALE).astype(BF16)
            dks.append(jnp.dot(ds, q4, preferred_element_type=F32))
            dq4 = lax.dot_general(ds, k2, TN, preferred_element_type=F32)
            dsink4 = -psink * delta
            for g in range(GROUP):
                dqs.append(dq4[g * BLOCK:(g + 1) * BLOCK])
                dsinks.append(jnp.broadcast_to(jnp.sum(dsink4[:, g * BLOCK:(g + 1) * BLOCK], axis=1, keepdims=True), (1, LANES)))
        dq_ref[...] = jnp.concatenate(dqs, axis=-1)
        dkp_ref[...] = jnp.concatenate([d[:BLOCK] for d in dks], axis=-1)
        dkc_ref[...] = jnp.concatenate([d[BLOCK:] for d in dks], axis=-1)
        dvp_ref[...] = jnp.concatenate([d[:BLOCK] for d in dvs], axis=-1)
        dvc_ref[...] = jnp.concatenate([d[BLOCK:] for d in dvs], axis=-1)

        @pl.when(n == 0)
        def _():
            dsink_ref[...] = jnp.zeros_like(dsink_ref)

        dsink_ref[...] += jnp.concatenate(dsinks, axis=0)

    cur = lambda n: (n, 0)
    prev = lambda n: (jnp.maximum(n - 1, 0), 0)
    kspec = pl.BlockSpec((BLOCK, kw), cur)
    return pl.pallas_call(
        body, name="attn_bwd", grid=(t // BLOCK,),
        in_specs=[pl.BlockSpec((BLOCK, cw), cur),
                  pl.BlockSpec((BLOCK, kw), prev), kspec,
                  pl.BlockSpec((BLOCK, kw), prev), kspec,
                  pl.BlockSpec(sink_rows.shape, lambda n: (0, 0)),
                  pl.BlockSpec((BLOCK, cw), cur)],
        out_specs=[pl.BlockSpec((BLOCK, cw), cur), kspec, kspec, kspec, kspec,
                   pl.BlockSpec((nq, LANES), lambda n: (0, 0))],
        out_shape=[_sds((t, cw), F32)] + [_sds((t, kw), F32)] * 4 + [_sds((nq, LANES), F32)],
        compiler_params=_params(("arbitrary",), BLOCK * (cw + 4 * kw) * 2 + 2 * BLOCK * cw * 4 + 4 * BLOCK * kw * 4, 12 << 20),
    )(qn, kn, kn, vb, vb, sink_rows, do)


def _mix_out(ca, o, woc, woa, proj):
    t, cw = ca.shape
    nb = woc.shape[2]
    d = N_DEV * nb
    tm = min(t, 1024)
    ga0 = (3 * cw + cw + 2 * (cw // 4)) // nb

    def body(ca_ref, o_ref, woc_ref, woa_ref, ga_ref, gb_ref, m_ref, ya_ref, yb_ref):
        ya = jnp.dot(ca_ref[...], woc_ref[...], preferred_element_type=F32)
        yb = jnp.dot(o_ref[...], woa_ref[...], preferred_element_type=F32)
        ya_ref[...] = ya.astype(BF16)
        yb_ref[...] = yb.astype(BF16)
        m_ref[...] = (_sigmoid(ga_ref[...].astype(F32)) * ya + _sigmoid(gb_ref[...].astype(F32)) * yb).astype(BF16)

    act = pl.BlockSpec((tm, cw), lambda i, j: (i, 0))
    wsp = pl.BlockSpec((None, cw, nb), lambda i, j: (j, 0, 0))
    osp = pl.BlockSpec((tm, nb), lambda i, j: (i, j))
    blocks = 2 * tm * cw * 2 + 2 * cw * nb * 2 + 2 * tm * nb * 4 + 3 * tm * nb * 2
    return pl.pallas_call(
        body, name="mix_out", grid=(t // tm, N_DEV),
        in_specs=[act, act, wsp, wsp,
                  pl.BlockSpec((tm, nb), lambda i, j: (i, ga0 + j)),
                  pl.BlockSpec((tm, nb), lambda i, j: (i, ga0 + N_DEV + j))],
        out_specs=[osp, osp, osp],
        out_shape=[_sds((t, d), BF16)] * 3,
        compiler_params=_params(("parallel", "parallel"), blocks, 6 * tm * nb * 4),
    )(ca, o, woc, woa, proj, proj)


def _mix_residual(merged, wo, x):
    t, d = x.shape
    tm = min(t, 512)

    def epilogue(acc, ins, outs):
        outs[0][...] = ins[2][...] + acc

    row = pl.BlockSpec((tm, d), lambda i, j, k: (i, 0))
    return _fused("mix_residual", (t // tm, 1, 1),
                  [(merged, row), (wo, pl.BlockSpec((d, d), lambda i, j, k: (0, 0))), (x, row)],
                  [(_sds((t, d), F32), row)], [(0, 1, NN)], epilogue, temp_bytes=2 * tm * d * 4)[0]


def _mix_bwd_gates(dx, wo, ya, yb, proj, cw):
    t, d = dx.shape
    tm = min(t, 1024)
    tn = min(d, 512)
    ga0 = (4 * cw + 2 * (cw // 4)) // tn

    def epilogue(acc, ins, outs):
        sa, sb = _sigmoid(ins[4][...].astype(F32)), _sigmoid(ins[5][...].astype(F32))
        outs[0][...] = (acc * sa).astype(BF16)
        outs[1][...] = (acc * sb).astype(BF16)
        outs[2][0] = (acc * ins[2][...].astype(F32) * sa * (1.0 - sa)).astype(BF16)
        outs[2][1] = (acc * ins[3][...].astype(F32) * sb * (1.0 - sb)).astype(BF16)

    blk = pl.BlockSpec((tm, tn), lambda i, j, k: (i, j))
    return _fused("mix_bwd_gates", (t // tm, d // tn, 1),
                  [(dx, pl.BlockSpec((tm, d), lambda i, j, k: (i, 0))),
                   (wo, pl.BlockSpec((tn, d), lambda i, j, k: (j, 0))),
                   (ya, blk), (yb, blk),
                   (proj, pl.BlockSpec((tm, tn), lambda i, j, k: (i, ga0 + j))),
                   (proj, pl.BlockSpec((tm, tn), lambda i, j, k: (i, ga0 + d // tn + j)))],
                  [(_sds((t, d), BF16), blk), (_sds((t, d), BF16), blk),
                   (_sds((2, t, d), BF16), pl.BlockSpec((2, tm, tn), lambda i, j, k: (0, i, j)))],
                  [(0, 1, NT)], epilogue, temp_bytes=8 * tm * tn * 4)


def _tn_matmul(name, a, b, tm, out_dtype=BF16):
    t, m = a.shape
    n = b.shape[1]
    tk = min(t, 512)

    def epilogue(acc, ins, outs):
        outs[0][...] = acc.astype(out_dtype)

    return _fused(name, (m // tm, 1, t // tk),
                  [(a, pl.BlockSpec((tk, tm), lambda i, j, k: (k, i))),
                   (b, pl.BlockSpec((tk, n), lambda i, j, k: (k, 0)))],
                  [(_sds((m, n), out_dtype), pl.BlockSpec((tm, n), lambda i, j, k: (i, 0)))],
                  [(0, 1, TN)], epilogue, nk=t // tk, acc_shape=(tm, n), temp_bytes=tm * n * 4)[0]


def _out_proj_bwd_act(dya, dyb, woc, woa, deps=()):
    t, d = dya.shape
    kdim, nb = woc.shape[1], woc.shape[2]
    tm = min(t, 512)

    def body(dya_ref, dyb_ref, woc_ref, woa_ref, *rest):
        for dy_ref, w_ref, o_ref in ((dya_ref, woc_ref, rest[-2]), (dyb_ref, woa_ref, rest[-1])):
            total = None
            for j in range(N_DEV):
                part = lax.dot_general(dy_ref[:, j * nb:(j + 1) * nb], w_ref[j], NT, preferred_element_type=F32)
                total = part if total is None else total + part
            o_ref[...] = total

    row = pl.BlockSpec((tm, d), lambda i: (i, 0))
    wsp = pl.BlockSpec((N_DEV, kdim, nb), lambda i: (0, 0, 0))
    osp = pl.BlockSpec((tm, kdim), lambda i: (i, 0))
    blocks = 2 * tm * d * 2 + 2 * N_DEV * kdim * nb * 2 + 2 * tm * kdim * 4
    return pl.pallas_call(
        body, name="mix_bwd_dca_do", grid=(t // tm,),
        in_specs=[row, row, wsp, wsp] + [_ANY] * len(deps), out_specs=[osp, osp],
        out_shape=[_sds((t, kdim), F32)] * 2,
        compiler_params=_params(("parallel",), blocks, 4 * tm * kdim * 4),
    )(dya, dyb, woc, woa, *deps)


def _out_proj_bwd_w(ca, o, dya, dyb, nb):
    t, kdim = ca.shape

    def body(ca_ref, o_ref, dya_ref, dyb_ref, dwoc_ref, dwoa_ref):
        dwoc_ref[...] = lax.dot_general(ca_ref[...], dya_ref[...], TN, preferred_element_type=F32).astype(BF16)
        dwoa_ref[...] = lax.dot_general(o_ref[...], dyb_ref[...], TN, preferred_element_type=F32).astype(BF16)

    act = pl.BlockSpec((t, kdim), lambda j: (0, 0))
    col = pl.BlockSpec((t, nb), lambda j: (0, j))
    osp = pl.BlockSpec((None, kdim, nb), lambda j: (j, 0, 0))
    blocks = 2 * t * kdim * 2 + 2 * t * nb * 2 + 2 * kdim * nb * 2
    return pl.pallas_call(
        body, name="mix_bwd_dwoc_dwoa", grid=(N_DEV,),
        in_specs=[act, act, col, col], out_specs=[osp, osp],
        out_shape=[_sds((N_DEV, kdim, nb), BF16)] * 2,
        compiler_params=_params(("parallel",), blocks, 4 * kdim * nb * 4),
    )(ca, o, dya, dyb)


def _proj_bwd_act(dproj, w_in, deps=()):
    t, n = dproj.shape
    d, nb = w_in.shape[2], w_in.shape[3]
    tm = min(t, 512)

    def epilogue(acc, ins, outs):
        outs[0][...] = acc

    def products(ins):
        return (lax.dot_general(ins[0][:, 0:nb], ins[1][0], NT, preferred_element_type=F32)
                + lax.dot_general(ins[0][:, nb:2 * nb], ins[1][1], NT, preferred_element_type=F32))

    return _fused("mix_bwd_dh", (t // tm, 1, 4),
                  [(dproj, pl.BlockSpec((tm, 2 * nb), lambda i, j, k: (i, k))),
                   (w_in, pl.BlockSpec((None, 2, d, nb), lambda i, j, k: (k, 0, 0, 0)))],
                  [(_sds((t, d), F32), pl.BlockSpec((tm, d), lambda i, j, k: (i, 0)))],
                  products, epilogue, nk=4, acc_shape=(tm, d), temp_bytes=tm * d * 4, deps=deps)[0]


def _proj_bwd_w(h, dproj):
    t, d = h.shape
    nb = dproj.shape[1] // N_DEV
    tm = min(d, 512)

    def body(h_ref, dp_ref, o_ref):
        hv = h_ref[...]
        o_ref[0] = lax.dot_general(hv, dp_ref[:, 0:nb], TN, preferred_element_type=F32).astype(BF16)
        o_ref[1] = lax.dot_general(hv, dp_ref[:, nb:2 * nb], TN, preferred_element_type=F32).astype(BF16)

    blocks = t * tm * 2 + t * 2 * nb * 2 + 2 * tm * nb * 2
    return pl.pallas_call(
        body, name="mix_bwd_dwin", grid=(4, d // tm),
        in_specs=[pl.BlockSpec((t, tm), lambda j, i: (0, i)),
                  pl.BlockSpec((t, 2 * nb), lambda j, i: (0, j))],
        out_specs=pl.BlockSpec((None, 2, tm, nb), lambda j, i: (j, 0, i, 0)),
        out_shape=_sds((4, 2, d, nb), BF16),
        compiler_params=_params(("parallel", "parallel"), blocks, 4 * tm * nb * 4),
    )(h, dproj)


def _adamw_math(w, g, m, v):
    m = ADAM_B1 * m + (1.0 - ADAM_B1) * g
    v = ADAM_B2 * v + (1.0 - ADAM_B2) * (g * g)
    m_hat = m / (1.0 - ADAM_B1 ** ADAM_STEP)
    v_hat = v / (1.0 - ADAM_B2 ** ADAM_STEP)
    delta = -ADAM_LR * (m_hat / (jnp.sqrt(v_hat) + ADAM_EPS) + ADAM_WD * w)
    return delta, m, v


def _adamw(name, parts, w, m, v, tr):
    r, c = w.shape

    def body(p_ref, w_ref, m_ref, v_ref, g_out, d_out, m_out, v_out):
        g = p_ref[0].astype(F32)
        for s in range(1, N_DEV):
            g = g + p_ref[s].astype(F32)
        delta, mn, vn = _adamw_math(w_ref[...], g, m_ref[...], v_ref[...])
        g_out[...] = g
        d_out[...] = delta
        m_out[...] = mn
        v_out[...] = vn

    blk = pl.BlockSpec((tr, c), lambda i: (i, 0))
    blocks = N_DEV * tr * c * parts.dtype.itemsize + 7 * tr * c * 4
    return pl.pallas_call(
        body, name=name, grid=(r // tr,),
        in_specs=[pl.BlockSpec((N_DEV, tr, c), lambda i: (0, i, 0)), blk, blk, blk],
        out_specs=[blk] * 4, out_shape=[_sds((r, c), F32)] * 4,
        compiler_params=_params(("parallel",), blocks, 6 * tr * c * 4),
    )(parts, w, m, v)


def _chip_sum(sums_ref):
    g = sums_ref[0].astype(F32)
    for k in range(1, 4):
        g = g + sums_ref[k].astype(F32)
    return g


def _adamw_chips(name, sums, w, m, v, tr, deps=(), row0=0, into=None):
    r, c = w.shape
    rs = sums.shape[1]
    i0 = row0 // tr
    n_pass = len(deps) + (4 if into is not None else 0)

    def body(sums_ref, w_ref, m_ref, v_ref, *rest):
        g_out, d_out, m_out, v_out = rest[n_pass:]
        g = _chip_sum(sums_ref)
        delta, mn, vn = _adamw_math(w_ref[...], g, m_ref[...], v_ref[...])
        g_out[...] = g
        d_out[...] = delta
        m_out[...] = mn
        v_out[...] = vn

    blk = pl.BlockSpec((tr, c), lambda i: (i0 + i, 0))
    blocks = 4 * tr * c * 2 + 7 * tr * c * 4
    passed = list(deps) + (list(into) if into is not None else [])
    aliases = {4 + len(deps) + q: q for q in range(4)} if into is not None else {}
    return pl.pallas_call(
        body, name=name, grid=(rs // tr,),
        in_specs=[pl.BlockSpec((4, tr, c), lambda i: (0, i, 0)), blk, blk, blk] + [_ANY] * n_pass,
        out_specs=[blk] * 4, out_shape=[_sds((r, c), F32)] * 4,
        input_output_aliases=aliases,
        compiler_params=_params(("parallel",), blocks, 6 * tr * c * 4),
    )(sums, w, m, v, *passed)


def _adamw_side(contrib, w, m, v, n_tiles, step_of):
    r, c = w.shape
    tr = r // n_tiles
    assert tr * n_tiles == r and tr % 16 == 0, (r, n_tiles)

    def tile(i, j, k):
        return jnp.minimum(step_of(i, j, k), n_tiles - 1)

    blk = pl.BlockSpec((tr, c), lambda i, j, k: (tile(i, j, k), 0))
    ins = [(contrib, pl.BlockSpec((4, tr, c), lambda i, j, k: (0, tile(i, j, k), 0))), (w, blk), (m, blk), (v, blk)]
    outs = [(_sds((r, c), F32), blk)] * 4

    def fn(in_refs, out_refs):
        @pl.when(step_of(pl.program_id(0), pl.program_id(1), pl.program_id(2)) < n_tiles)
        def _():
            g = _chip_sum(in_refs[0])
            delta, mn, vn = _adamw_math(in_refs[1][...], g, in_refs[2][...], in_refs[3][...])
            out_refs[0][...] = g
            out_refs[1][...] = delta
            out_refs[2][...] = mn
            out_refs[3][...] = vn

    return ins, outs, fn


def _rope_tables(t):
    half = ROT_DIM // 2
    inv_freq = 1.0 / (ROPE_THETA ** (jnp.arange(0, ROT_DIM, 2, dtype=F32) / ROT_DIM))
    ang = jnp.arange(t, dtype=F32)[:, None] * inv_freq[None, :]
    cos, sin = jnp.cos(ang), jnp.sin(ang)
    ones = jnp.ones((t, HEAD_DIM - ROT_DIM), F32)
    zeros = jnp.zeros((t, HEAD_DIM - half), F32)
    c = jnp.concatenate([cos, cos, ones], axis=1)
    sa = jnp.concatenate([-sin, zeros], axis=1)
    sb = jnp.concatenate([jnp.zeros((t, half), F32), sin, jnp.zeros((t, HEAD_DIM - ROT_DIM), F32)], axis=1)
    return tuple(jnp.tile(a, (1, LANES // HEAD_DIM)) for a in (c, sa, sb))


def _pad_rows(a, rows=8):
    return jnp.pad(a, ((0, rows - a.shape[0]), (0, 0)))


def kernel(x, g_ffn1, w_gu1, w_down1, g_mix, w_in, conv_w, q_norm_g, k_norm_g, sinks, w_out_conv, w_out_attn, w_o, g_ffn2, w_gu2, w_down2, loss_target, m_g_ffn1, m_w_gu1, m_w_down1, m_g_mix, m_w_in, m_conv_w, m_q_norm_g, m_k_norm_g, m_sinks, m_w_out_conv, m_w_out_attn, m_w_o, m_g_ffn2, m_w_gu2, m_w_down2, v_g_ffn1, v_w_gu1, v_w_down1, v_g_mix, v_w_in, v_conv_w, v_q_norm_g, v_k_norm_g, v_sinks, v_w_out_conv, v_w_out_attn, v_w_o, v_g_ffn2, v_w_gu2, v_w_down2):
    t, d = x.shape[1], x.shape[2]
    cw = d // 2
    kw = cw // GROUP
    nq = cw // HEAD_DIM
    xs, target = x.reshape(t, d), loss_target.reshape(t, d)
    me = 4 * lax.axis_index("x") + 2 * lax.axis_index("y") + lax.axis_index("c")

    big = {"w_gu1": w_gu1, "w_down1": w_down1, "w_in": w_in, "w_out_conv": w_out_conv,
           "w_out_attn": w_out_attn, "w_o": w_o, "w_gu2": w_gu2, "w_down2": w_down2}
    big_m = {"w_gu1": m_w_gu1, "w_down1": m_w_down1, "w_in": m_w_in, "w_out_conv": m_w_out_conv,
             "w_out_attn": m_w_out_attn, "w_o": m_w_o, "w_gu2": m_w_gu2, "w_down2": m_w_down2}
    big_v = {"w_gu1": v_w_gu1, "w_down1": v_w_down1, "w_in": v_w_in, "w_out_conv": v_w_out_conv,
             "w_out_attn": v_w_out_attn, "w_o": v_w_o, "w_gu2": v_w_gu2, "w_down2": v_w_down2}
    names = list(big)

    tiles = {"w_gu1": 256, "w_gu2": 256, "w_in": 256, "w_down1": 176, "w_down2": 176,
             "w_out_conv": 1024, "w_out_attn": 1024, "w_o": 128}

    def row_tile(n):
        r = big[n].shape[1]
        return tiles[n] if r % tiles[n] == 0 else r

    rs_shape = {n: big[n].shape[1:] for n in names}
    half = rs_shape["w_gu1"][0] // 2
    rs_shape["w_gu1_lo"] = rs_shape["w_gu1_hi"] = (half, rs_shape["w_gu1"][1])

    def add_tile(n):
        r, c = rs_shape[n]
        while r * c * 2 > (3 << 20) and r % 32 == 0:
            r //= 2
        return r

    me_arr = me.astype(jnp.int32).reshape(1)
    sources = [(n, big[n][0], BF16, row_tile(n)) for n in names] + [("conv_w", _pad_rows(conv_w[0]), F32, 8)]
    issue_order = [0, 1, 2, 8, 3, 4, 5, 6, 7]
    first = _place_shard("place_" + names[0], sources[0][1], BF16, me_arr, sources[0][3])
    started = [_gather_start("gather_start_first", [first])]
    early = {2: (big_m["w_in"][0], big_v["w_in"][0])}
    rest = [_place_shard("place_" + sources[i][0], sources[i][1], sources[i][2], me_arr, sources[i][3],
                         deps=(started[0][3],) + early.get(i, ())) for i in issue_order[1:]]
    started.append(_gather_start("gather_start_rest", rest))
    where = {0: (0, 0)}
    where.update({i: (1, p) for p, i in enumerate(issue_order[1:])})

    def fetch(tag, idxs, after):
        call = where[idxs[0]][0]
        send, recv, stacks, _ = started[call]
        positions = [where[i][1] for i in idxs]
        got = _gather_wait("gather_wait_" + tag, positions, send, recv, [stacks[p] for p in positions], after)
        return _forward_to_sibling("gather_forward_" + tag, got)

    rope_tabs = _rope_tables(t)
    gq = jnp.tile(q_norm_g, (1, nq))
    gk = jnp.tile(k_norm_g, (1, nq // GROUP))
    sink_rows = jnp.broadcast_to(sinks[0][:, None], (nq, LANES))

    wts = {}
    h1 = _rms_fwd("ffn1_norm", xs, g_ffn1)
    wts["w_gu1"], = fetch("gu1", [0], started[1][3])
    gu1, a1 = _ffn_up("ffn1_up", h1, wts["w_gu1"])
    wts["w_down1"], = fetch("down1", [1], a1)
    wd1 = wts["w_down1"].reshape(-1, d)
    x1 = _ffn_down("ffn1_down", a1, wd1, xs)
    h2 = _rms_fwd("mix_norm", x1, g_mix)
    wts["w_in"], conv_land = fetch("in", [2, 8], h2)
    w_in_full = wts["w_in"].reshape(4, 2, d, -1)
    conv_full = jnp.transpose(conv_land, (1, 0, 2)).reshape(8, cw)
    proj = _proj(h2, w_in_full)
    ca = _conv_fwd(proj, conv_full)
    qn, kn, vb = _qk_prep(proj, gq, gk, rope_tabs, cw, kw)
    o = _attn_fwd(qn, kn, vb, sink_rows)
    wts["w_out_conv"], wts["w_out_attn"] = fetch("out", [3, 4], o)
    merged, ya, yb = _mix_out(ca, o, wts["w_out_conv"], wts["w_out_attn"], proj)
    wts["w_o"], = fetch("o", [5], merged)
    wo = wts["w_o"].reshape(d, d)
    x2 = _mix_residual(merged, wo, x1)
    h3 = _rms_fwd("ffn2_norm", x2, g_ffn2)
    wts["w_gu2"], = fetch("gu2", [6], h3)
    gu2, a2 = _ffn_up("ffn2_up", h3, wts["w_gu2"])
    wts["w_down2"], = fetch("down2", [7], a2)
    wd2 = wts["w_down2"].reshape(-1, d)
    dy, sq, dy_bf = _ffn_down("ffn2_down", a2, wd2, x2, target=target)
    loss = lax.psum(sq[0, 0] * (0.5 / d), ("x", "y", "c"))

    place = jnp.stack([lax.axis_index("c"), 2 * lax.axis_index("x") + lax.axis_index("y")]).astype(jnp.int32)
    def pair_start(tag, group, grads, deps=()):
        stacks = [grads[n].reshape((4, 2) + rs_shape[n]) for n in group]
        lands = [lax.empty((4,) + rs_shape[n], BF16) for n in group]
        return _pair_start("rs_pair_start_" + tag, stacks, lands, deps)

    def chip_start(tag, group, pending, after):
        send, recv, stacks, lands, _ = pending
        stacks, lands = _pair_wait("rs_pair_wait_" + tag, send, recv, stacks, lands, after)
        added = [_pair_add("rs_pair_add_" + n, st, ld, place, add_tile(n)) for n, st, ld in zip(group, stacks, lands)]
        return _chip_start("rs_chip_start_" + tag, [a[0] for a in added], [a[1] for a in added])

    group_a, group_b, group_c = ["w_down2", "w_gu2"], ["w_o", "w_out_conv", "w_out_attn"], ["w_in"]
    group_d, group_e, group_f = ["w_down1"], ["w_gu1_lo"], ["w_gu1_hi"]
    g = {}
    dgu2, a2 = _ffn_bwd_act("ffn2_bwd_act", dy_bf, wd2, gu2)
    g["w_down2"], = _ffn_bwd_dwd("ffn2_bwd_dwd", a2, dy_bf)
    g["w_gu2"], = _ffn_bwd_dwgu("ffn2_bwd_dwgu", h3, dgu2)
    pend_a = pair_start("a", group_a, g)
    dh3, = _ffn_bwd_dh("ffn2_bwd_dh", dgu2, wts["w_gu2"], deps=(pend_a[4],))
    ring_a = chip_start("a", group_a, pend_a, dh3)
    dx2, dg_ffn2, dx2_bf = _rms_bwd("ffn2_bwd_rms", x2, g_ffn2, dh3, dy, deps=(ring_a[4],), with_bf16=True)

    dya, dyb, dgates = _mix_bwd_gates(dx2_bf, wo, ya, yb, proj, cw)
    g["w_o"] = _tn_matmul("mix_bwd_dwo", merged, dx2_bf, min(d, 1024))
    g["w_out_conv"], g["w_out_attn"] = _out_proj_bwd_w(ca, o, dya, dyb, d // N_DEV)
    pend_b = pair_start("b", group_b, g)
    dca, do = _out_proj_bwd_act(dya, dyb, wts["w_out_conv"], wts["w_out_attn"], deps=(pend_b[4],))
    ring_b = chip_start("b", group_b, pend_b, do)
    d3, dconv_w = _conv_bwd(proj, conv_full, dca, deps=(ring_b[4],))
    dq, dkc, dkp, dvc, dvp, dsink = _attn_bwd(qn, kn, vb, sink_rows, do)
    dqkv, dgq, dgk = _qk_prep_bwd(proj, gq, gk, rope_tabs, dq, dkc, dkp, dvc, dvp, cw, kw)
    dproj = jnp.concatenate([d3[0], d3[1], d3[2], dqkv, dgates[0], dgates[1]], axis=1)
    g["w_in"] = _proj_bwd_w(h2, dproj)
    pend_c = pair_start("c", group_c, g)
    dh2 = _proj_bwd_act(dproj, w_in_full, deps=(pend_c[4],))
    ring_c = chip_start("c", group_c, pend_c, dh2)
    dx1, dg_mix, dx1_bf = _rms_bwd("mix_bwd_rms", x1, g_mix, dh2, dx2, deps=(ring_c[4],), with_bf16=True)

    big_out = {}
    arrived = {}

    def wait_group(tag, group, ring, after):
        send, recv, parts, lands2, _ = ring
        parts, lands2 = _chip_wait("rs_chip_wait_" + tag, send, recv, parts, lands2, after)
        arrived.update(dict(zip(group, lands2)))

    def update(n, after):
        res = _adamw_chips("adamw_" + n, arrived[n], big[n][0], big_m[n][0], big_v[n][0], row_tile(n), deps=(after,))
        big_out[n] = [a[None] for a in res]
        return res[0]

    def update_beside(n, n_tiles, step_of):
        return _adamw_side(arrived[n], big[n][0], big_m[n][0], big_v[n][0], n_tiles, step_of)

    def keep(n, res):
        big_out[n] = [a[None] for a in res]

    dgu1, a1 = _ffn_bwd_act("ffn1_bwd_act", dx1_bf, wd1, gu1)
    wait_group("a", group_a, ring_a, a1)
    g["w_down1"], *res = _ffn_bwd_dwd("ffn1_bwd_dwd", a1, dx1_bf,
                                       side=update_beside("w_down2", 11, lambda i, j, k: i * 4 + j))
    keep("w_down2", res)
    pend_d = pair_start("d", group_d, g)
    g["w_gu1_lo"], *res = _ffn_bwd_dwgu("ffn1_bwd_dwgu_lo", h1, dgu1, deps=(pend_d[4],), rows=(0, half),
                                         side=update_beside("w_gu2", 16, lambda i, j, k: i * 2 + j))
    keep("w_gu2", res)
    ring_d = chip_start("d", group_d, pend_d, g["w_gu1_lo"])
    pend_e = pair_start("e", group_e, g, deps=(ring_d[4],))
    g["w_gu1_hi"], = _ffn_bwd_dwgu("ffn1_bwd_dwgu_hi", h1, dgu1, deps=(pend_e[4],), rows=(half, half))
    ring_e = chip_start("e", group_e, pend_e, g["w_gu1_hi"])
    pend_f = pair_start("f", group_f, g, deps=(ring_e[4],))
    wait_group("b", group_b, ring_b, pend_f[4])
    after = pend_f[4]
    for n in group_b:
        after = update(n, after)
    ring_f = chip_start("f", group_f, pend_f, after)
    wait_group("c", group_c, ring_c, ring_f[4])
    dh1, *res = _ffn_bwd_dh("ffn1_bwd_dh", dgu1, wts["w_gu1"],
                             side=update_beside("w_in", 16, lambda i, j, k: i * 4 + k))
    keep("w_in", res)
    grad_x, dg_ffn1 = _rms_bwd("ffn1_bwd_rms", xs, g_ffn1, dh1, dx1)
    wait_group("d", group_d, ring_d, grad_x)
    after = update("w_down1", grad_x)
    n = "w_gu1"
    wait_group("e", group_e, ring_e, after)
    res = _adamw_chips("adamw_w_gu1_lo", arrived["w_gu1_lo"], big[n][0], big_m[n][0], big_v[n][0], row_tile(n), deps=(after,))
    wait_group("f", group_f, ring_f, res[0])
    res = _adamw_chips("adamw_w_gu1_hi", arrived["w_gu1_hi"], big[n][0], big_m[n][0], big_v[n][0], row_tile(n),
                       row0=half, into=res)
    keep(n, res)
    after = res[0]

    small = {"g_ffn1": dg_ffn1[0:1], "g_mix": dg_mix[0:1], "g_ffn2": dg_ffn2[0:1],
             "q_norm_g": dgq[0:1, :HEAD_DIM], "k_norm_g": dgk[0:1, :HEAD_DIM], "sinks": dsink[:, 0][None],
             "conv_w": dconv_w[0:CONV_K].reshape(1, -1)}
    small_w = {"g_ffn1": g_ffn1, "g_mix": g_mix, "g_ffn2": g_ffn2, "q_norm_g": q_norm_g, "k_norm_g": k_norm_g,
               "sinks": sinks, "conv_w": None}
    small_m = {"g_ffn1": m_g_ffn1, "g_mix": m_g_mix, "g_ffn2": m_g_ffn2, "q_norm_g": m_q_norm_g,
               "k_norm_g": m_k_norm_g, "sinks": m_sinks, "conv_w": m_conv_w}
    small_v = {"g_ffn1": v_g_ffn1, "g_mix": v_g_mix, "g_ffn2": v_g_ffn2, "q_norm_g": v_q_norm_g,
               "k_norm_g": v_k_norm_g, "sinks": v_sinks, "conv_w": v_conv_w}
    snames = list(small)
    widths = [small[n].shape[1] for n in snames]
    total = sum(widths)
    rows = -(-total // LANES)
    rows = -(-rows // 8) * 8

    def pack(vals):
        flat = jnp.concatenate([v.reshape(1, -1) for v in vals], axis=1)
        return jnp.pad(flat, ((0, 0), (0, rows * LANES - total))).reshape(rows, LANES)

    csh = cw // N_DEV

    def place_conv(local, fill):
        full = jnp.full((CONV_K, cw), fill, F32)
        return lax.dynamic_update_slice(full, local, (0, me * csh)).reshape(1, -1)

    pw = pack([small_w[n] if n != "conv_w" else place_conv(conv_w[0], 0.0) for n in snames])
    pm = pack([small_m[n] if n != "conv_w" else place_conv(m_conv_w[0], 0.0) for n in snames])
    pv = pack([small_v[n] if n != "conv_w" else place_conv(v_conv_w[0], 1.0) for n in snames])
    parts = _all_gather_small("gather_small_grads", pack([small[n] for n in snames]), deps=(after,))
    sg, sd, sm, sv = [a.reshape(1, -1) for a in _adamw("adamw_small", parts, pw, pm, pv, rows)]

    def unpack(flat, n):
        off = sum(widths[:snames.index(n)])
        piece = flat[:, off:off + widths[snames.index(n)]]
        if n == "conv_w":
            piece = lax.dynamic_slice(piece.reshape(CONV_K, cw), (0, me * csh), (CONV_K, csh))[None]
        return piece

    order = ["g_ffn1", "w_gu1", "w_down1", "g_mix", "w_in", "conv_w", "q_norm_g", "k_norm_g", "sinks",
             "w_out_conv", "w_out_attn", "w_o", "g_ffn2", "w_gu2", "w_down2"]
    outs = [loss, grad_x[None]]
    for idx, flat in enumerate((sg, sd, sm, sv)):
        for n in order:
            outs.append(big_out[n][idx] if n in big_out else unpack(flat, n))
    return tuple(outs)
```

```python
import jax
import jax.numpy as jnp
from jax import lax
from jax.experimental import pallas as pl
from jax.experimental.pallas import tpu as pltpu

F32 = jnp.float32
BF16 = jnp.bfloat16

N_DEV = 8
HEAD_DIM = 64
GROUP = 4
BLOCK = 128
ROT_DIM = 16
ROPE_THETA = 500000.0
RMS_EPS = 1e-6
NEG_INF = -1e30
ATTN_SCALE = HEAD_DIM ** -0.5
CONV_K = 3
LANES = 128
MXU_COLS = 256
VMEM_BYTES_V7X = 64 * 1024 * 1024
VMEM_CAP = VMEM_BYTES_V7X - 6 * 1024 * 1024

ADAM_LR = 0.001
ADAM_B1 = 0.9
ADAM_B2 = 0.999
ADAM_EPS = 1e-08
ADAM_WD = 0.01
ADAM_STEP = 10

NN = (((1,), (0,)), ((), ()))
NT = (((1,), (1,)), ((), ()))
TN = (((0,), (0,)), ((), ()))

MESH = pl.DeviceIdType.MESH


def _nbytes(shape, dtype):
    n = 1
    for s in shape:
        if s is not None:
            n *= s
    return n * jnp.dtype(dtype).itemsize


def _params(semantics, block_bytes, temp_bytes):
    assert 2 * block_bytes + temp_bytes <= VMEM_CAP, (block_bytes, temp_bytes)
    return pltpu.CompilerParams(dimension_semantics=semantics, vmem_limit_bytes=VMEM_CAP)


def _fused(name, grid, ins, outs, dots, epilogue, *, nk=1, acc_shape=None, temp_bytes=0,
           semantics=("parallel", "parallel", "arbitrary"), deps=(), side=None):
    n_main_in, n_main_out = len(ins), len(outs)
    if side is not None:
        ins, outs = list(ins) + list(side[0]), list(outs) + list(side[1])
    n_in, n_out = len(ins), len(outs)
    n_dep = len(deps)

    def body(*refs):
        in_refs, out_refs = refs[:n_in], refs[n_in + n_dep:n_in + n_dep + n_out]
        scratch = refs[n_in + n_dep + n_out:]
        if side is not None:
            side[2](in_refs[n_main_in:], out_refs[n_main_out:])

        def products():
            if callable(dots):
                return dots(in_refs)
            total = None
            for ai, bi, contract in dots:
                a, b = in_refs[ai][...], in_refs[bi][...]
                a = a if a.dtype == BF16 else a.astype(BF16)
                b = b if b.dtype == BF16 else b.astype(BF16)
                p = lax.dot_general(a, b, contract, preferred_element_type=F32)
                total = p if total is None else total + p
            return total

        if nk == 1:
            epilogue(products() if dots else None, in_refs, out_refs)
        else:
            acc = scratch[0]
            k = pl.program_id(2)

            @pl.when(k == 0)
            def _():
                acc[...] = jnp.zeros_like(acc)

            acc[...] += products()

            @pl.when(k == nk - 1)
            def _():
                epilogue(acc[...], in_refs, out_refs)

    block_bytes = sum(_nbytes(spec.block_shape, a.dtype) for a, spec in ins)
    block_bytes += sum(_nbytes(spec.block_shape, s.dtype) for s, spec in outs)
    scratch_shapes = []
    if nk > 1:
        scratch_shapes.append(pltpu.VMEM(acc_shape, F32))
        temp_bytes += _nbytes(acc_shape, F32)
    res = pl.pallas_call(
        body, name=name, grid=grid,
        in_specs=[spec for _, spec in ins] + [pl.BlockSpec(memory_space=pl.ANY)] * n_dep,
        out_specs=[spec for _, spec in outs],
        out_shape=[s for s, _ in outs],
        scratch_shapes=scratch_shapes,
        compiler_params=_params(semantics, block_bytes, temp_bytes),
    )(*[a for a, _ in ins], *deps)
    return res


def _sds(shape, dtype):
    return jax.ShapeDtypeStruct(shape, dtype)


def _sigmoid(x):
    return jax.nn.sigmoid(x)


def _all_gather_small(name, shard, deps=()):
    n_dep = len(deps)

    def body(src, *rest):
        dst, send_sems, recv_sems, local_sem = rest[n_dep:]
        x, y, c = lax.axis_index("x"), lax.axis_index("y"), lax.axis_index("c")
        me = 4 * x + 2 * y + c
        copies = [pltpu.make_async_copy(src, dst.at[me], local_sem)]
        for k in range(1, N_DEV):
            peer = ((1 - x) if (k & 4) else x, (1 - y) if (k & 2) else y, (1 - c) if (k & 1) else c)
            copies.append(pltpu.make_async_remote_copy(
                src_ref=src, dst_ref=dst.at[me], send_sem=send_sems.at[k - 1], recv_sem=recv_sems.at[k - 1],
                device_id=peer, device_id_type=MESH))
        for cp in copies:
            cp.start()
        for cp in copies:
            cp.wait()

    hbm = pl.BlockSpec(memory_space=pltpu.HBM)
    return pl.pallas_call(
        body, name=name,
        in_specs=[hbm] + [pl.BlockSpec(memory_space=pl.ANY)] * n_dep, out_specs=hbm,
        out_shape=_sds((N_DEV,) + shard.shape, shard.dtype),
        scratch_shapes=[pltpu.SemaphoreType.DMA((N_DEV - 1,)), pltpu.SemaphoreType.DMA((N_DEV - 1,)),
                        pltpu.SemaphoreType.DMA],
    )(shard, *deps)


_HBM = pl.BlockSpec(memory_space=pltpu.HBM)
_SEM = pl.BlockSpec(memory_space=pltpu.SEMAPHORE)
_ANY = pl.BlockSpec(memory_space=pl.ANY)
_EFFECT = pltpu.SideEffectType.DATAFLOW_SIDE_EFFECTING
N_TARGETS = 4


def _mesh_pos():
    return lax.axis_index("x"), lax.axis_index("y"), lax.axis_index("c")


def _chip_peers(x, y, c):
    return [(1 - x, y, c), (x, 1 - y, c), (1 - x, 1 - y, c)]


def _dev_index(pos):
    return 4 * pos[0] + 2 * pos[1] + pos[2]


def _hbm_like(a):
    return pltpu.HBM(a.shape, a.dtype)


def _place_shard(name, w, out_dtype, me, tr, deps=()):
    r, c = w.shape
    n_dep = len(deps)

    def body(me_ref, w_ref, *rest):
        rest[n_dep][...] = w_ref[...].astype(out_dtype)

    grid_spec = pltpu.PrefetchScalarGridSpec(
        num_scalar_prefetch=1, grid=(r // tr,),
        in_specs=[pl.BlockSpec((tr, c), lambda i, me_ref: (i, 0))] + [_ANY] * n_dep,
        out_specs=pl.BlockSpec((None, tr, c), lambda i, me_ref: (me_ref[0], i, 0)))
    return pl.pallas_call(
        body, name=name, grid_spec=grid_spec, out_shape=_sds((N_DEV, r, c), out_dtype),
        compiler_params=_params(("parallel",), tr * c * 6, tr * c * 4),
    )(me, w, *deps)


def _gather_start(name, lands):
    n = len(lands)

    def body(*refs):
        bufs = refs[:n]
        send, recv = refs[n], refs[n + 1]
        token = refs[-1]
        x, y, c = _mesh_pos()
        me = _dev_index((x, y, c))
        targets = [(x, y, 1 - c)] + _chip_peers(x, y, c)
        for w in range(n):
            for k, to in enumerate(targets):
                pltpu.make_async_remote_copy(
                    src_ref=bufs[w].at[me], dst_ref=bufs[w].at[me],
                    send_sem=send.at[N_TARGETS * w + k], recv_sem=recv.at[N_TARGETS * w + k],
                    device_id=to, device_id_type=MESH).start()
        token[...] = jnp.zeros_like(token)

    sems = pltpu.SemaphoreType.DMA((N_TARGETS * n,))
    outs = pl.pallas_call(
        body, name=name,
        in_specs=[_HBM] * n, out_specs=[_SEM, _SEM] + [_HBM] * n + [_token_spec()],
        out_shape=[sems, sems] + [_hbm_like(a) for a in lands] + [_sds((8, LANES), F32)],
        input_output_aliases={i: 2 + i for i in range(n)},
        compiler_params=pltpu.CompilerParams(has_side_effects=_EFFECT),
    )(*lands)
    return outs[0], outs[1], list(outs[2:2 + n]), outs[-1]


def _gather_wait(name, positions, send, recv, lands, after):
    m = len(positions)

    def body(*refs):
        bufs = refs[:m]
        send_sems, recv_sems = refs[m], refs[m + 1]
        x, y, c = _mesh_pos()
        me = _dev_index((x, y, c))
        sources = [(x, y, 1 - c)] + _chip_peers(x, y, c)
        for j, w in enumerate(positions):
            for k, frm in enumerate(sources):
                cp = pltpu.make_async_remote_copy(
                    src_ref=bufs[j].at[me], dst_ref=bufs[j].at[_dev_index(frm)],
                    send_sem=send_sems.at[N_TARGETS * w + k], recv_sem=recv_sems.at[N_TARGETS * w + k],
                    device_id=frm, device_id_type=MESH)
                cp.wait_send()
                cp.wait_recv()

    outs = pl.pallas_call(
        body, name=name,
        in_specs=[_HBM] * m + [_SEM, _SEM, _ANY], out_specs=[_HBM] * m,
        out_shape=[_hbm_like(a) for a in lands],
        input_output_aliases={i: i for i in range(m)},
        compiler_params=pltpu.CompilerParams(has_side_effects=_EFFECT),
    )(*lands, send, recv, after)
    return list(outs)


def _forward_to_sibling(name, lands):
    m = len(lands)

    def body(*refs):
        bufs = refs[m:2 * m]
        send_sems, recv_sems = refs[2 * m], refs[2 * m + 1]
        x, y, c = _mesh_pos()
        copies = []
        for j in range(m):
            for k, chip in enumerate(_chip_peers(x, y, c)):
                block = bufs[j].at[_dev_index(chip)]
                cp = pltpu.make_async_remote_copy(
                    src_ref=block, dst_ref=block,
                    send_sem=send_sems.at[3 * j + k], recv_sem=recv_sems.at[3 * j + k],
                    device_id=(x, y, 1 - c), device_id_type=MESH)
                cp.start()
                copies.append(cp)
        for cp in copies:
            cp.wait()

    outs = pl.pallas_call(
        body, name=name,
        in_specs=[_HBM] * m, out_specs=[_HBM] * m,
        out_shape=[_sds(a.shape, a.dtype) for a in lands],
        input_output_aliases={i: i for i in range(m)},
        scratch_shapes=[pltpu.SemaphoreType.DMA((3 * m,)), pltpu.SemaphoreType.DMA((3 * m,))],
    )(*lands)
    return list(outs)


def _token_spec():
    return pl.BlockSpec(memory_space=pltpu.VMEM)


def _pair_start(name, stacks, lands, deps=()):
    n = len(stacks)
    n_dep = len(deps)

    def body(*refs):
        srcs, dsts = refs[:n], refs[n:2 * n]
        send, recv = refs[2 * n + n_dep], refs[2 * n + n_dep + 1]
        token = refs[-1]
        x, y, c = _mesh_pos()
        for w in range(n):
            for chip in range(4):
                pltpu.make_async_remote_copy(
                    src_ref=srcs[w].at[chip, 1 - c], dst_ref=dsts[w].at[chip],
                    send_sem=send.at[4 * w + chip], recv_sem=recv.at[4 * w + chip],
                    device_id=(x, y, 1 - c), device_id_type=MESH).start()
        token[...] = jnp.zeros_like(token)

    sems = pltpu.SemaphoreType.DMA((4 * n,))
    outs = pl.pallas_call(
        body, name=name,
        in_specs=[_HBM] * (2 * n) + [_ANY] * n_dep, out_specs=[_SEM, _SEM] + [_HBM] * (2 * n) + [_token_spec()],
        out_shape=[sems, sems] + [_hbm_like(a) for a in stacks] + [_hbm_like(a) for a in lands] + [_sds((8, LANES), F32)],
        input_output_aliases={i: 2 + i for i in range(2 * n)},
        compiler_params=pltpu.CompilerParams(has_side_effects=_EFFECT),
    )(*stacks, *lands, *deps)
    return outs[0], outs[1], list(outs[2:2 + n]), list(outs[2 + n:2 + 2 * n]), outs[-1]


def _pair_wait(name, send, recv, stacks, lands, after):
    n = len(stacks)

    def body(*refs):
        srcs, dsts = refs[:n], refs[n:2 * n]
        send_sems, recv_sems = refs[2 * n], refs[2 * n + 1]
        x, y, c = _mesh_pos()
        for w in range(n):
            for chip in range(4):
                cp = pltpu.make_async_remote_copy(
                    src_ref=srcs[w].at[chip, 1 - c], dst_ref=dsts[w].at[chip],
                    send_sem=send_sems.at[4 * w + chip], recv_sem=recv_sems.at[4 * w + chip],
                    device_id=(x, y, 1 - c), device_id_type=MESH)
                cp.wait_send()
                cp.wait_recv()

    outs = pl.pallas_call(
        body, name=name,
        in_specs=[_HBM] * (2 * n) + [_SEM, _SEM, _ANY], out_specs=[_HBM] * (2 * n),
        out_shape=[_hbm_like(a) for a in stacks] + [_hbm_like(a) for a in lands],
        input_output_aliases={i: i for i in range(2 * n)},
        compiler_params=pltpu.CompilerParams(has_side_effects=_EFFECT),
    )(*stacks, *lands, send, recv, after)
    return list(outs[:n]), list(outs[n:])


def _pair_add(name, stack, land, place, tr):
    _, _, r, c = stack.shape

    def body(place_ref, a_ref, b_ref, sums_ref, slots_ref):
        total = (a_ref[...].astype(F32) + b_ref[...].astype(F32)).astype(BF16)
        sums_ref[...] = total

        @pl.when(pl.program_id(1) == place_ref[1])
        def _():
            slots_ref[...] = total

    grid_spec = pltpu.PrefetchScalarGridSpec(
        num_scalar_prefetch=1, grid=(r // tr, 4),
        in_specs=[pl.BlockSpec((None, None, tr, c), lambda i, k, place_ref: (k, place_ref[0], i, 0)),
                  pl.BlockSpec((None, tr, c), lambda i, k, place_ref: (k, i, 0))],
        out_specs=[pl.BlockSpec((None, tr, c), lambda i, k, place_ref: (k, i, 0)),
                   pl.BlockSpec((None, tr, c), lambda i, k, place_ref: (place_ref[1], i, 0))])
    return pl.pallas_call(
        body, name=name, grid_spec=grid_spec, out_shape=[_sds((4, r, c), BF16)] * 2,
        compiler_params=_params(("parallel", "arbitrary"), 4 * tr * c * 2, 3 * tr * c * 4),
    )(place, stack, land)


def _chip_start(name, parts, lands):
    n = len(parts)

    def body(*refs):
        srcs, dsts = refs[:n], refs[n:2 * n]
        send, recv = refs[2 * n], refs[2 * n + 1]
        token = refs[-1]
        x, y, c = _mesh_pos()
        for w in range(n):
            for k, to in enumerate(_chip_peers(x, y, c)):
                pltpu.make_async_remote_copy(
                    src_ref=srcs[w].at[2 * to[0] + to[1]], dst_ref=dsts[w].at[2 * x + y],
                    send_sem=send.at[3 * w + k], recv_sem=recv.at[3 * w + k],
                    device_id=to, device_id_type=MESH).start()
        token[...] = jnp.zeros_like(token)

    sems = pltpu.SemaphoreType.DMA((3 * n,))
    outs = pl.pallas_call(
        body, name=name,
        in_specs=[_HBM] * (2 * n), out_specs=[_SEM, _SEM] + [_HBM] * (2 * n) + [_token_spec()],
        out_shape=[sems, sems] + [_hbm_like(a) for a in parts] + [_hbm_like(a) for a in lands] + [_sds((8, LANES), F32)],
        input_output_aliases={i: 2 + i for i in range(2 * n)},
        compiler_params=pltpu.CompilerParams(has_side_effects=_EFFECT),
    )(*parts, *lands)
    return outs[0], outs[1], list(outs[2:2 + n]), list(outs[2 + n:2 + 2 * n]), outs[-1]


def _chip_wait(name, send, recv, parts, lands, after):
    n = len(parts)

    def body(*refs):
        srcs, dsts = refs[:n], refs[n:2 * n]
        send_sems, recv_sems = refs[2 * n], refs[2 * n + 1]
        x, y, c = _mesh_pos()
        for w in range(n):
            for k, frm in enumerate(_chip_peers(x, y, c)):
                chip = 2 * frm[0] + frm[1]
                cp = pltpu.make_async_remote_copy(
                    src_ref=srcs[w].at[chip], dst_ref=dsts[w].at[chip],
                    send_sem=send_sems.at[3 * w + k], recv_sem=recv_sems.at[3 * w + k],
                    device_id=frm, device_id_type=MESH)
                cp.wait_send()
                cp.wait_recv()

    outs = pl.pallas_call(
        body, name=name,
        in_specs=[_HBM] * (2 * n) + [_SEM, _SEM, _ANY], out_specs=[_HBM] * (2 * n),
        out_shape=[_hbm_like(a) for a in parts] + [_hbm_like(a) for a in lands],
        input_output_aliases={i: i for i in range(2 * n)},
        compiler_params=pltpu.CompilerParams(has_side_effects=_EFFECT),
    )(*parts, *lands, send, recv, after)
    return list(outs[:n]), list(outs[n:])


def _row_tile(t):
    return min(t, 256)


def _rms_fwd(name, x, g):
    t, d = x.shape
    tm = _row_tile(t)

    def epilogue(_, ins, outs):
        xv = ins[0][...]
        r = lax.rsqrt(jnp.mean(xv * xv, axis=-1, keepdims=True) + RMS_EPS)
        outs[0][...] = (xv * r * ins[1][...]).astype(BF16)

    row = pl.BlockSpec((tm, d), lambda i, j, k: (i, 0))
    vec = pl.BlockSpec((1, d), lambda i, j, k: (0, 0))
    return _fused(name, (t // tm, 1, 1), [(x, row), (g, vec)], [(_sds((t, d), BF16), row)], [], epilogue,
                  temp_bytes=4 * tm * d * 4)[0]


def _rms_bwd(name, x, g, dh, resid, deps=(), with_bf16=False):
    t, d = x.shape
    tm = _row_tile(t)

    def epilogue(_, ins, outs):
        xv, gv, dhv = ins[0][...], ins[1][...], ins[2][...]
        r = lax.rsqrt(jnp.mean(xv * xv, axis=-1, keepdims=True) + RMS_EPS)
        xh = xv * r
        u = dhv * gv
        dot = jnp.mean(u * xh, axis=-1, keepdims=True)
        dx = ins[3][...] + r * (u - xh * dot)
        outs[0][...] = dx
        if with_bf16:
            outs[2][...] = dx.astype(BF16)

        @pl.when(pl.program_id(0) == 0)
        def _():
            outs[1][...] = jnp.zeros_like(outs[1])

        outs[1][0:1, :] += jnp.sum(dhv * xh, axis=0, keepdims=True)

    row = pl.BlockSpec((tm, d), lambda i, j, k: (i, 0))
    vec = pl.BlockSpec((1, d), lambda i, j, k: (0, 0))
    acc = pl.BlockSpec((8, d), lambda i, j, k: (0, 0))
    outs = [(_sds((t, d), F32), row), (_sds((8, d), F32), acc)] + ([(_sds((t, d), BF16), row)] if with_bf16 else [])
    return _fused(name, (t // tm, 1, 1), [(x, row), (g, vec), (dh, row), (resid, row)], outs, [], epilogue,
                  temp_bytes=6 * tm * d * 4, semantics=("arbitrary", "arbitrary", "arbitrary"), deps=deps)


def _ffn_up(name, h, wgu):
    t, d = h.shape
    nb = wgu.shape[2]
    f = 4 * nb
    tm = min(t, 512)

    def body(h_ref, wg_ref, wu_ref, gu_ref, a_ref):
        hv = h_ref[...]
        for c0 in range(0, nb, MXU_COLS):
            cs = slice(c0, min(c0 + MXU_COLS, nb))
            g = jnp.dot(hv, wg_ref[:, cs], preferred_element_type=F32)
            u = jnp.dot(hv, wu_ref[:, cs], preferred_element_type=F32)
            gu_ref[0, :, cs] = g.astype(BF16)
            gu_ref[1, :, cs] = u.astype(BF16)
            a_ref[:, cs] = (g * _sigmoid(g) * u).astype(BF16)

    blocks = tm * d * 2 + 2 * d * nb * 2 + 3 * tm * nb * 2
    return pl.pallas_call(
        body, name=name, grid=(4, t // tm),
        in_specs=[pl.BlockSpec((tm, d), lambda j, i: (i, 0)),
                  pl.BlockSpec((None, d, nb), lambda j, i: (j, 0, 0)),
                  pl.BlockSpec((None, d, nb), lambda j, i: (j + 4, 0, 0))],
        out_specs=[pl.BlockSpec((2, tm, nb), lambda j, i: (0, i, j)),
                   pl.BlockSpec((tm, nb), lambda j, i: (i, j))],
        out_shape=[_sds((2, t, f), BF16), _sds((t, f), BF16)],
        compiler_params=_params(("parallel", "parallel"), blocks, 8 * tm * MXU_COLS * 4),
    )(h, wgu, wgu)


def _ffn_down(name, a, wd, x, target=None):
    t, f = a.shape
    d = wd.shape[1]
    tm = min(t, 512)
    tn = min(d, 1024)
    blk = pl.BlockSpec((tm, tn), lambda j, i, k: (i, j))
    ins = [(a, pl.BlockSpec((tm, f), lambda j, i, k: (i, 0))), (wd, pl.BlockSpec((f, tn), lambda j, i, k: (0, j))), (x, blk)]

    if target is None:
        def epilogue(acc, ins, outs):
            outs[0][...] = ins[2][...] + 0.5 * acc

        return _fused(name, (d // tn, t // tm, 1), ins, [(_sds((t, d), F32), blk)],
                      [(0, 1, NN)], epilogue, temp_bytes=2 * tm * tn * 4)[0]

    def epilogue(acc, ins, outs):
        e = ins[2][...] + 0.5 * acc - ins[3][...]
        outs[0][...] = e * (1.0 / d)
        outs[2][...] = (e * (1.0 / d)).astype(BF16)

        @pl.when((pl.program_id(0) == 0) & (pl.program_id(1) == 0))
        def _():
            outs[1][...] = jnp.zeros_like(outs[1])

        part = jnp.sum(jnp.sum(e * e, axis=1, keepdims=True), axis=0, keepdims=True)
        outs[1][...] += jnp.broadcast_to(part, outs[1].shape)

    return _fused(name, (d // tn, t // tm, 1), ins + [(target, blk)],
                  [(_sds((t, d), F32), blk), (_sds((8, LANES), F32), pl.BlockSpec((8, LANES), lambda j, i, k: (0, 0))),
                   (_sds((t, d), BF16), blk)],
                  [(0, 1, NN)], epilogue, temp_bytes=3 * tm * tn * 4,
                  semantics=("arbitrary", "arbitrary", "arbitrary"))


def _ffn_bwd_act(name, dy, wd, gu, deps=()):
    t, d = dy.shape
    f = wd.shape[0]
    nb = f // 4
    tm = min(t, 512)

    def body(dy_ref, wd_ref, gu_ref, *rest):
        dgu_ref, a_ref = rest[-2], rest[-1]
        dyv = dy_ref[...].astype(BF16)
        for c0 in range(0, nb, MXU_COLS):
            cs = slice(c0, min(c0 + MXU_COLS, nb))
            da = 0.5 * lax.dot_general(dyv, wd_ref[cs, :], NT, preferred_element_type=F32)
            g = gu_ref[0, :, cs].astype(F32)
            u = gu_ref[1, :, cs].astype(F32)
            s = _sigmoid(g)
            silu = g * s
            dgu_ref[0, :, cs] = (da * u * (s * (1.0 + g * (1.0 - s)))).astype(BF16)
            dgu_ref[1, :, cs] = (da * silu).astype(BF16)
            a_ref[:, cs] = (silu * u).astype(BF16)

    blocks = tm * d * 4 + nb * d * 2 + 5 * tm * nb * 2
    return pl.pallas_call(
        body, name=name, grid=(4, t // tm),
        in_specs=[pl.BlockSpec((tm, d), lambda j, i: (i, 0)),
                  pl.BlockSpec((nb, d), lambda j, i: (j, 0)),
                  pl.BlockSpec((2, tm, nb), lambda j, i: (0, i, j))] + [_ANY] * len(deps),
        out_specs=[pl.BlockSpec((2, tm, nb), lambda j, i: (0, i, j)), pl.BlockSpec((tm, nb), lambda j, i: (i, j))],
        out_shape=[_sds((2, t, f), BF16), _sds((t, f), BF16)],
        compiler_params=_params(("parallel", "parallel"), blocks, tm * d * 2 + 8 * tm * MXU_COLS * 4),
    )(dy, wd, gu, *deps)


def _ffn_bwd_dwd(name, a, dy, deps=(), side=None):
    t, f = a.shape
    d = dy.shape[1]
    tm = f // 4
    tn = min(d, 512)

    def epilogue(acc, ins, outs):
        outs[0][...] = (0.5 * acc).astype(BF16)

    return _fused(name, (4, d // tn, 1),
                  [(a, pl.BlockSpec((t, tm), lambda i, j, k: (0, i))),
                   (dy, pl.BlockSpec((t, tn), lambda i, j, k: (0, j)))],
                  [(_sds((f, d), BF16), pl.BlockSpec((tm, tn), lambda i, j, k: (i, j)))],
                  [(0, 1, TN)], epilogue, temp_bytes=t * tn * 2 + 2 * tm * tn * 4, deps=deps, side=side)


def _ffn_bwd_dh(name, dgu, wgu, deps=(), side=None):
    _, t, f = dgu.shape
    d, nb = wgu.shape[1], wgu.shape[2]
    tm = min(t, 512)

    def products(ins):
        return (lax.dot_general(ins[0][:, 0:nb], ins[1][0], NT, preferred_element_type=F32)
                + lax.dot_general(ins[0][:, nb:2 * nb], ins[1][1], NT, preferred_element_type=F32))

    def epilogue(acc, ins, outs):
        outs[0][...] = acc

    return _fused(name, (t // tm, 1, 4),
                  [(dgu, pl.BlockSpec((None, tm, 2 * nb), lambda i, j, k: (k // 2, i, k % 2))),
                   (wgu, pl.BlockSpec((2, d, nb), lambda i, j, k: (k, 0, 0)))],
                  [(_sds((t, d), F32), pl.BlockSpec((tm, d), lambda i, j, k: (i, 0)))],
                  products, epilogue, nk=4, acc_shape=(tm, d), temp_bytes=tm * d * 4, deps=deps, side=side)


def _ffn_bwd_dwgu(name, h, dgu, deps=(), side=None, rows=None):
    t, d = h.shape
    nb = dgu.shape[2] // 4
    tm = min(d, 512)
    row0, nrows = rows if rows is not None else (0, d)
    j0 = row0 // tm

    def epilogue(acc, ins, outs):
        outs[0][...] = acc.astype(BF16)

    return _fused(name, (N_DEV, nrows // tm, 1),
                  [(h, pl.BlockSpec((t, tm), lambda i, j, k: (0, j0 + j))),
                   (dgu, pl.BlockSpec((None, t, nb), lambda i, j, k: (i // 4, 0, i % 4)))],
                  [(_sds((N_DEV, nrows, nb), BF16), pl.BlockSpec((None, tm, nb), lambda i, j, k: (i, j, 0)))],
                  [(0, 1, TN)], epilogue, temp_bytes=2 * tm * nb * 4, deps=deps, side=side)


def _proj(h, w_in):
    t, d = h.shape
    nb = w_in.shape[3]
    tm = min(t, 512)

    def body(h_ref, w_ref, o_ref):
        hv = h_ref[...]
        o_ref[:, 0:nb] = jnp.dot(hv, w_ref[0], preferred_element_type=F32).astype(BF16)
        o_ref[:, nb:2 * nb] = jnp.dot(hv, w_ref[1], preferred_element_type=F32).astype(BF16)

    blocks = tm * d * 2 + 2 * d * nb * 2 + tm * 2 * nb * 4
    return pl.pallas_call(
        body, name="mix_proj", grid=(4, t // tm),
        in_specs=[pl.BlockSpec((tm, d), lambda j, i: (i, 0)),
                  pl.BlockSpec((None, 2, d, nb), lambda j, i: (j, 0, 0, 0))],
        out_specs=pl.BlockSpec((tm, 2 * nb), lambda j, i: (i, j)),
        out_shape=_sds((t, N_DEV * nb), BF16),
        compiler_params=_params(("parallel", "parallel"), blocks, 2 * tm * nb * 4),
    )(h, w_in)


def _shift_rows(u, k):
    t = u.shape[0]
    rolled = pltpu.roll(u, k % t, axis=0)
    row = lax.broadcasted_iota(jnp.int32, u.shape, 0)
    keep = (row >= k) if k > 0 else (row < t + k)
    return jnp.where(keep, rolled, 0.0)


def _conv_fwd(proj, conv_w):
    t = proj.shape[0]
    cw = conv_w.shape[1]
    tc = min(cw, 256)
    nc = cw // tc

    def epilogue(_, ins, outs):
        u = ins[2][...].astype(F32) * ins[0][...].astype(F32)
        w = ins[3][...]
        y = u * w[2:3, :] + _shift_rows(u, 1) * w[1:2, :] + _shift_rows(u, 2) * w[0:1, :]
        outs[0][...] = (ins[1][...].astype(F32) * y).astype(BF16)

    def col(seg):
        return pl.BlockSpec((t, tc), lambda i, j, k: (0, seg * nc + i))

    return _fused("conv_fwd", (nc, 1, 1),
                  [(proj, col(0)), (proj, col(1)), (proj, col(2)),
                   (conv_w, pl.BlockSpec((8, tc), lambda i, j, k: (0, i)))],
                  [(_sds((t, cw), BF16), pl.BlockSpec((t, tc), lambda i, j, k: (0, i)))],
                  [], epilogue, temp_bytes=6 * t * tc * 4)[0]


def _conv_bwd(proj, conv_w, dca, deps=()):
    t = proj.shape[0]
    cw = conv_w.shape[1]
    tc = min(cw, 256)
    nc = cw // tc

    def epilogue(_, ins, outs):
        xc, bg, cg = ins[0][...].astype(F32), ins[1][...].astype(F32), ins[2][...].astype(F32)
        w, dc = ins[3][...], ins[4][...]
        u = cg * xc
        u1, u2 = _shift_rows(u, 1), _shift_rows(u, 2)
        y = u * w[2:3, :] + u1 * w[1:2, :] + u2 * w[0:1, :]
        dconv = dc * bg
        du = dconv * w[2:3, :] + _shift_rows(dconv, -1) * w[1:2, :] + _shift_rows(dconv, -2) * w[0:1, :]
        outs[0][0] = (du * cg).astype(BF16)
        outs[0][1] = (dc * y).astype(BF16)
        outs[0][2] = (du * xc).astype(BF16)
        outs[1][...] = jnp.zeros_like(outs[1])
        outs[1][0:1, :] = jnp.sum(dconv * u2, axis=0, keepdims=True)
        outs[1][1:2, :] = jnp.sum(dconv * u1, axis=0, keepdims=True)
        outs[1][2:3, :] = jnp.sum(dconv * u, axis=0, keepdims=True)

    def col(seg):
        return pl.BlockSpec((t, tc), lambda i, j, k: (0, seg * nc + i))

    own = pl.BlockSpec((t, tc), lambda i, j, k: (0, i))
    wspec = pl.BlockSpec((8, tc), lambda i, j, k: (0, i))
    return _fused("conv_bwd", (nc, 1, 1),
                  [(proj, col(0)), (proj, col(1)), (proj, col(2)), (conv_w, wspec), (dca, own)],
                  [(_sds((3, t, cw), BF16), pl.BlockSpec((3, t, tc), lambda i, j, k: (0, 0, i))),
                   (_sds((8, cw), F32), wspec)],
                  [], epilogue, temp_bytes=10 * t * tc * 4, deps=deps)


def _split3(x):
    hi = x.astype(BF16)
    r1 = x - hi.astype(F32)
    mid = r1.astype(BF16)
    lo = (r1 - mid.astype(F32)).astype(BF16)
    return hi, mid, lo


def _head_selector(width):
    r = lax.broadcasted_iota(jnp.int32, (width, LANES), 0)
    c = lax.broadcasted_iota(jnp.int32, (width, LANES), 1)
    return (lax.shift_right_logical(r, 6) == c).astype(BF16)


def _head_sum(x, sel):
    return sum(jnp.dot(p, sel, preferred_element_type=F32) for p in _split3(x))


def _head_bcast(r, sel):
    return sum(lax.dot_general(p, sel, NT, preferred_element_type=F32) for p in _split3(r))


def _rope(x, c, sa, sb):
    n = x.shape[1]
    return x * c + pltpu.roll(x, n - ROT_DIM // 2, axis=1) * sa + pltpu.roll(x, ROT_DIM // 2, axis=1) * sb


def _rope_t(d, c, sa, sb):
    n = d.shape[1]
    return d * c + pltpu.roll(d * sa, ROT_DIM // 2, axis=1) + pltpu.roll(d * sb, n - ROT_DIM // 2, axis=1)


def _tile_lanes(tab, width):
    return tab if width == tab.shape[1] else jnp.tile(tab, (1, width // tab.shape[1]))


def _qk_prep(proj, gq, gk, rope_tabs, cw, kw):
    t = proj.shape[0]
    tm = _row_tile(t)

    def epilogue(_, ins, outs):
        c, sa, sb = ins[5][...], ins[6][...], ins[7][...]
        for src, gain, dst, width in ((0, 3, 0, cw), (1, 4, 1, kw)):
            xv = ins[src][...].astype(F32)
            sel = _head_selector(width)
            r = lax.rsqrt(_head_sum(xv * xv, sel) * (1.0 / HEAD_DIM) + RMS_EPS)
            xn = xv * _head_bcast(r, sel) * ins[gain][...]
            outs[dst][...] = _rope(xn, _tile_lanes(c, width), _tile_lanes(sa, width), _tile_lanes(sb, width)).astype(BF16)
        outs[2][...] = ins[2][...].astype(BF16)

    kblk = cw // kw
    tab = pl.BlockSpec((tm, LANES), lambda i, j, k: (i, 0))
    kspec = pl.BlockSpec((tm, kw), lambda i, j, k: (i, 0))
    return _fused("qk_prep", (t // tm, 1, 1),
                  [(proj, pl.BlockSpec((tm, cw), lambda i, j, k: (i, 3))),
                   (proj, pl.BlockSpec((tm, kw), lambda i, j, k: (i, 4 * kblk))),
                   (proj, pl.BlockSpec((tm, kw), lambda i, j, k: (i, 4 * kblk + 1))),
                   (gq, pl.BlockSpec((1, cw), lambda i, j, k: (0, 0))),
                   (gk, pl.BlockSpec((1, kw), lambda i, j, k: (0, 0))),
                   (rope_tabs[0], tab), (rope_tabs[1], tab), (rope_tabs[2], tab)],
                  [(_sds((t, cw), BF16), pl.BlockSpec((tm, cw), lambda i, j, k: (i, 0))),
                   (_sds((t, kw), BF16), kspec), (_sds((t, kw), BF16), kspec)],
                  [], epilogue, temp_bytes=12 * tm * cw * 4)


def _qk_prep_bwd(proj, gq, gk, rope_tabs, dq, dkc, dkp, dvc, dvp, cw, kw):
    t = proj.shape[0]
    tm = BLOCK
    nblk = t // tm

    def epilogue(_, ins, outs):
        c, sa, sb = ins[5][...], ins[6][...], ins[7][...]
        has_next = (pl.program_id(0) < nblk - 1).astype(F32)
        dk = ins[9][...] + has_next * ins[10][...]
        dv = ins[11][...] + has_next * ins[12][...]
        pieces = []
        for src, gain, dval, dst, width in ((0, 3, ins[8][...], 1, cw), (1, 4, dk, 2, kw)):
            xv, gv = ins[src][...].astype(F32), ins[gain][...]
            sel = _head_selector(width)
            r = _head_bcast(lax.rsqrt(_head_sum(xv * xv, sel) * (1.0 / HEAD_DIM) + RMS_EPS), sel)
            xh = xv * r
            dxn = _rope_t(dval, _tile_lanes(c, width), _tile_lanes(sa, width), _tile_lanes(sb, width))
            u = dxn * gv
            dot = _head_bcast(_head_sum(u * xh, sel), sel) * (1.0 / HEAD_DIM)
            pieces.append((r * (u - xh * dot)).astype(BF16))
            ri = lax.broadcasted_iota(jnp.int32, (width, LANES), 0)
            ci = lax.broadcasted_iota(jnp.int32, (width, LANES), 1)
            fold = (lax.bitwise_and(ri, HEAD_DIM - 1) == ci).astype(BF16)
            colsum = jnp.broadcast_to(jnp.sum(dxn * xh, axis=0, keepdims=True), (8, width))
            part = sum(jnp.dot(p, fold, preferred_element_type=F32) for p in _split3(colsum))

            @pl.when(pl.program_id(0) == 0)
            def _():
                outs[dst][...] = jnp.zeros_like(outs[dst])

            outs[dst][0:1, :] += part[0:1, :]
        outs[0][:, 0:cw] = pieces[0]
        outs[0][:, cw:cw + kw] = pieces[1]
        outs[0][:, cw + kw:cw + 2 * kw] = dv.astype(BF16)

    kblk = cw // kw
    tab = pl.BlockSpec((tm, LANES), lambda i, j, k: (i, 0))
    kcur = pl.BlockSpec((tm, kw), lambda i, j, k: (i, 0))
    knext = pl.BlockSpec((tm, kw), lambda i, j, k: (jnp.minimum(i + 1, nblk - 1), 0))
    acc = pl.BlockSpec((8, LANES), lambda i, j, k: (0, 0))
    return _fused("qk_prep_bwd", (nblk, 1, 1),
                  [(proj, pl.BlockSpec((tm, cw), lambda i, j, k: (i, 3))),
                   (proj, pl.BlockSpec((tm, kw), lambda i, j, k: (i, 4 * kblk))),
                   (proj, pl.BlockSpec((tm, kw), lambda i, j, k: (i, 4 * kblk + 1))),
                   (gq, pl.BlockSpec((1, cw), lambda i, j, k: (0, 0))),
                   (gk, pl.BlockSpec((1, kw), lambda i, j, k: (0, 0))),
                   (rope_tabs[0], tab), (rope_tabs[1], tab), (rope_tabs[2], tab),
                   (dq, pl.BlockSpec((tm, cw), lambda i, j, k: (i, 0))),
                   (dkc, kcur), (dkp, knext), (dvc, kcur), (dvp, knext)],
                  [(_sds((t, cw + 2 * kw), BF16), pl.BlockSpec((tm, cw + 2 * kw), lambda i, j, k: (i, 0))),
                   (_sds((8, LANES), F32), acc), (_sds((8, LANES), F32), acc)],
                  [], epilogue, temp_bytes=16 * tm * cw * 4, semantics=("arbitrary", "arbitrary", "arbitrary"))


def _attn_mask(n):
    key = lax.broadcasted_iota(jnp.int32, (2 * BLOCK, GROUP * BLOCK), 0)
    qry = lax.bitwise_and(lax.broadcasted_iota(jnp.int32, (2 * BLOCK, GROUP * BLOCK), 1), BLOCK - 1)
    return (key > qry) & (key <= qry + BLOCK) & ((key >= BLOCK) | (n > 0))


def _stack_heads(x, h):
    return jnp.concatenate([x[:, (h * GROUP + g) * HEAD_DIM:(h * GROUP + g + 1) * HEAD_DIM] for g in range(GROUP)], axis=0)


def _softmax_with_sink(q4, k2, sink_ref, h, valid):
    sink = jnp.concatenate([sink_ref[h * GROUP + g:h * GROUP + g + 1, :] for g in range(GROUP)], axis=1)
    s = lax.dot_general(k2, q4, NT, preferred_element_type=F32) * ATTN_SCALE
    s = jnp.where(valid, s, NEG_INF)
    m = jnp.maximum(jnp.max(s, axis=0, keepdims=True), sink)
    p = jnp.exp(s - m)
    es = jnp.exp(sink - m)
    inv = 1.0 / (jnp.sum(p, axis=0, keepdims=True) + es)
    return p * inv, es * inv


def _attn_fwd(qn, kn, vb, sink_rows):
    t, cw = qn.shape
    kw = kn.shape[1]
    nkv = kw // HEAD_DIM

    def body(q_ref, kp_ref, kc_ref, vp_ref, vc_ref, sink_ref, o_ref):
        valid = _attn_mask(pl.program_id(0))
        qv = q_ref[...]
        kp, kc, vp, vc = kp_ref[...], kc_ref[...], vp_ref[...], vc_ref[...]
        outs = []
        for h in range(nkv):
            hs = slice(h * HEAD_DIM, (h + 1) * HEAD_DIM)
            k2 = jnp.concatenate([kp[:, hs], kc[:, hs]], axis=0)
            v2 = jnp.concatenate([vp[:, hs], vc[:, hs]], axis=0)
            pn, _ = _softmax_with_sink(_stack_heads(qv, h), k2, sink_ref, h, valid)
            o4 = lax.dot_general(pn.astype(BF16), v2, TN, preferred_element_type=F32)
            outs += [o4[g * BLOCK:(g + 1) * BLOCK] for g in range(GROUP)]
        o_ref[...] = jnp.concatenate(outs, axis=-1).astype(BF16)

    cur = lambda n: (n, 0)
    prev = lambda n: (jnp.maximum(n - 1, 0), 0)
    return pl.pallas_call(
        body, name="attn_fwd", grid=(t // BLOCK,),
        in_specs=[pl.BlockSpec((BLOCK, cw), cur),
                  pl.BlockSpec((BLOCK, kw), prev), pl.BlockSpec((BLOCK, kw), cur),
                  pl.BlockSpec((BLOCK, kw), prev), pl.BlockSpec((BLOCK, kw), cur),
                  pl.BlockSpec(sink_rows.shape, lambda n: (0, 0))],
        out_specs=pl.BlockSpec((BLOCK, cw), cur),
        out_shape=_sds((t, cw), BF16),
        compiler_params=_params(("parallel",), BLOCK * (cw + 4 * kw) * 2 + BLOCK * cw * 2, 8 << 20),
    )(qn, kn, kn, vb, vb, sink_rows)


def _attn_bwd(qn, kn, vb, sink_rows, do):
    t, cw = qn.shape
    kw = kn.shape[1]
    nkv = kw // HEAD_DIM
    nq = nkv * GROUP

    def body(q_ref, kp_ref, kc_ref, vp_ref, vc_ref, sink_ref, do_ref,
             dq_ref, dkc_ref, dkp_ref, dvc_ref, dvp_ref, dsink_ref):
        n = pl.program_id(0)
        valid = _attn_mask(n)
        qv, dov = q_ref[...], do_ref[...]
        kp, kc, vp, vc = kp_ref[...], kc_ref[...], vp_ref[...], vc_ref[...]
        dqs, dks, dvs, dsinks = [], [], [], []
        for h in range(nkv):
            hs = slice(h * HEAD_DIM, (h + 1) * HEAD_DIM)
            k2 = jnp.concatenate([kp[:, hs], kc[:, hs]], axis=0)
            v2 = jnp.concatenate([vp[:, hs], vc[:, hs]], axis=0)
            q4 = _stack_heads(qv, h)
            dob = _stack_heads(dov, h).astype(BF16)
            pn, psink = _softmax_with_sink(q4, k2, sink_ref, h, valid)
            dpn = lax.dot_general(v2, dob, NT, preferred_element_type=F32)
            dvs.append(jnp.dot(pn.astype(BF16), dob, preferred_element_type=F32))
            delta = jnp.sum(pn * dpn, axis=0, keepdims=True)
            ds = (pn * (dpn - delta) * ATTN_SCALE).astype(BF16)
            dks.append(jnp.dot(ds, q4, preferred_element_type=F32))
            dq4 = lax.dot_general(ds, k2, TN, preferred_element_type=F32)
            dsink4 = -psink * delta
            for g in range(GROUP):
                dqs.append(dq4[g * BLOCK:(g + 1) * BLOCK])
                dsinks.append(jnp.broadcast_to(jnp.sum(dsink4[:, g * BLOCK:(g + 1) * BLOCK], axis=1, keepdims=True), (1, LANES)))
        dq_ref[...] = jnp.concatenate(dqs, axis=-1)
        dkp_ref[...] = jnp.concatenate([d[:BLOCK] for d in dks], axis=-1)
        dkc_ref[...] = jnp.concatenate([d[BLOCK:] for d in dks], axis=-1)
        dvp_ref[...] = jnp.concatenate([d[:BLOCK] for d in dvs], axis=-1)
        dvc_ref[...] = jnp.concatenate([d[BLOCK:] for d in dvs], axis=-1)

        @pl.when(n == 0)
        def _():
            dsink_ref[...] = jnp.zeros_like(dsink_ref)

        dsink_ref[...] += jnp.concatenate(dsinks, axis=0)

    cur = lambda n: (n, 0)
    prev = lambda n: (jnp.maximum(n - 1, 0), 0)
    kspec = pl.BlockSpec((BLOCK, kw), cur)
    return pl.pallas_call(
        body, name="attn_bwd", grid=(t // BLOCK,),
        in_specs=[pl.BlockSpec((BLOCK, cw), cur),
                  pl.BlockSpec((BLOCK, kw), prev), kspec,
                  pl.BlockSpec((BLOCK, kw), prev), kspec,
                  pl.BlockSpec(sink_rows.shape, lambda n: (0, 0)),
                  pl.BlockSpec((BLOCK, cw), cur)],
        out_specs=[pl.BlockSpec((BLOCK, cw), cur), kspec, kspec, kspec, kspec,
                   pl.BlockSpec((nq, LANES), lambda n: (0, 0))],
        out_shape=[_sds((t, cw), F32)] + [_sds((t, kw), F32)] * 4 + [_sds((nq, LANES), F32)],
        compiler_params=_params(("arbitrary",), BLOCK * (cw + 4 * kw) * 2 + 2 * BLOCK * cw * 4 + 4 * BLOCK * kw * 4, 12 << 20),
    )(qn, kn, kn, vb, vb, sink_rows, do)


def _mix_out(ca, o, woc, woa, proj):
    t, cw = ca.shape
    nb = woc.shape[2]
    d = N_DEV * nb
    tm = min(t, 1024)
    ga0 = (3 * cw + cw + 2 * (cw // 4)) // nb

    def body(ca_ref, o_ref, woc_ref, woa_ref, ga_ref, gb_ref, m_ref, ya_ref, yb_ref):
        ya = jnp.dot(ca_ref[...], woc_ref[...], preferred_element_type=F32)
        yb = jnp.dot(o_ref[...], woa_ref[...], preferred_element_type=F32)
        ya_ref[...] = ya.astype(BF16)
        yb_ref[...] = yb.astype(BF16)
        m_ref[...] = (_sigmoid(ga_ref[...].astype(F32)) * ya + _sigmoid(gb_ref[...].astype(F32)) * yb).astype(BF16)

    act = pl.BlockSpec((tm, cw), lambda i, j: (i, 0))
    wsp = pl.BlockSpec((None, cw, nb), lambda i, j: (j, 0, 0))
    osp = pl.BlockSpec((tm, nb), lambda i, j: (i, j))
    blocks = 2 * tm * cw * 2 + 2 * cw * nb * 2 + 2 * tm * nb * 4 + 3 * tm * nb * 2
    return pl.pallas_call(
        body, name="mix_out", grid=(t // tm, N_DEV),
        in_specs=[act, act, wsp, wsp,
                  pl.BlockSpec((tm, nb), lambda i, j: (i, ga0 + j)),
                  pl.BlockSpec((tm, nb), lambda i, j: (i, ga0 + N_DEV + j))],
        out_specs=[osp, osp, osp],
        out_shape=[_sds((t, d), BF16)] * 3,
        compiler_params=_params(("parallel", "parallel"), blocks, 6 * tm * nb * 4),
    )(ca, o, woc, woa, proj, proj)


def _mix_residual(merged, wo, x):
    t, d = x.shape
    tm = min(t, 512)

    def epilogue(acc, ins, outs):
        outs[0][...] = ins[2][...] + acc

    row = pl.BlockSpec((tm, d), lambda i, j, k: (i, 0))
    return _fused("mix_residual", (t // tm, 1, 1),
                  [(merged, row), (wo, pl.BlockSpec((d, d), lambda i, j, k: (0, 0))), (x, row)],
                  [(_sds((t, d), F32), row)], [(0, 1, NN)], epilogue, temp_bytes=2 * tm * d * 4)[0]


def _mix_bwd_gates(dx, wo, ya, yb, proj, cw):
    t, d = dx.shape
    tm = min(t, 1024)
    tn = min(d, 512)
    ga0 = (4 * cw + 2 * (cw // 4)) // tn

    def epilogue(acc, ins, outs):
        sa, sb = _sigmoid(ins[4][...].astype(F32)), _sigmoid(ins[5][...].astype(F32))
        outs[0][...] = (acc * sa).astype(BF16)
        outs[1][...] = (acc * sb).astype(BF16)
        outs[2][0] = (acc * ins[2][...].astype(F32) * sa * (1.0 - sa)).astype(BF16)
        outs[2][1] = (acc * ins[3][...].astype(F32) * sb * (1.0 - sb)).astype(BF16)

    blk = pl.BlockSpec((tm, tn), lambda i, j, k: (i, j))
    return _fused("mix_bwd_gates", (t // tm, d // tn, 1),
                  [(dx, pl.BlockSpec((tm, d), lambda i, j, k: (i, 0))),
                   (wo, pl.BlockSpec((tn, d), lambda i, j, k: (j, 0))),
                   (ya, blk), (yb, blk),
                   (proj, pl.BlockSpec((tm, tn), lambda i, j, k: (i, ga0 + j))),
                   (proj, pl.BlockSpec((tm, tn), lambda i, j, k: (i, ga0 + d // tn + j)))],
                  [(_sds((t, d), BF16), blk), (_sds((t, d), BF16), blk),
                   (_sds((2, t, d), BF16), pl.BlockSpec((2, tm, tn), lambda i, j, k: (0, i, j)))],
                  [(0, 1, NT)], epilogue, temp_bytes=8 * tm * tn * 4)


def _tn_matmul(name, a, b, tm, out_dtype=BF16):
    t, m = a.shape
    n = b.shape[1]

    def epilogue(acc, ins, outs):
        outs[0][...] = acc.astype(out_dtype)

    return _fused(name, (m // tm, 1, 1),
                  [(a, pl.BlockSpec((t, tm), lambda i, j, k: (0, i))),
                   (b, pl.BlockSpec((t, n), lambda i, j, k: (0, 0)))],
                  [(_sds((m, n), out_dtype), pl.BlockSpec((tm, n), lambda i, j, k: (i, 0)))],
                  [(0, 1, TN)], epilogue, temp_bytes=2 * tm * n * 4)[0]


def _out_proj_bwd_act(dya, dyb, woc, woa, deps=()):
    t, d = dya.shape
    kdim, nb = woc.shape[1], woc.shape[2]
    tm = min(t, 512)

    def body(dya_ref, dyb_ref, woc_ref, woa_ref, *rest):
        for dy_ref, w_ref, o_ref in ((dya_ref, woc_ref, rest[-2]), (dyb_ref, woa_ref, rest[-1])):
            total = None
            for j in range(N_DEV):
                part = lax.dot_general(dy_ref[:, j * nb:(j + 1) * nb], w_ref[j], NT, preferred_element_type=F32)
                total = part if total is None else total + part
            o_ref[...] = total

    row = pl.BlockSpec((tm, d), lambda i: (i, 0))
    wsp = pl.BlockSpec((N_DEV, kdim, nb), lambda i: (0, 0, 0))
    osp = pl.BlockSpec((tm, kdim), lambda i: (i, 0))
    blocks = 2 * tm * d * 2 + 2 * N_DEV * kdim * nb * 2 + 2 * tm * kdim * 4
    return pl.pallas_call(
        body, name="mix_bwd_dca_do", grid=(t // tm,),
        in_specs=[row, row, wsp, wsp] + [_ANY] * len(deps), out_specs=[osp, osp],
        out_shape=[_sds((t, kdim), F32)] * 2,
        compiler_params=_params(("parallel",), blocks, 4 * tm * kdim * 4),
    )(dya, dyb, woc, woa, *deps)


def _out_proj_bwd_w(ca, o, dya, dyb, nb):
    t, kdim = ca.shape

    def body(ca_ref, o_ref, dya_ref, dyb_ref, dwoc_ref, dwoa_ref):
        dwoc_ref[...] = lax.dot_general(ca_ref[...], dya_ref[...], TN, preferred_element_type=F32).astype(BF16)
        dwoa_ref[...] = lax.dot_general(o_ref[...], dyb_ref[...], TN, preferred_element_type=F32).astype(BF16)

    act = pl.BlockSpec((t, kdim), lambda j: (0, 0))
    col = pl.BlockSpec((t, nb), lambda j: (0, j))
    osp = pl.BlockSpec((None, kdim, nb), lambda j: (j, 0, 0))
    blocks = 2 * t * kdim * 2 + 2 * t * nb * 2 + 2 * kdim * nb * 2
    return pl.pallas_call(
        body, name="mix_bwd_dwoc_dwoa", grid=(N_DEV,),
        in_specs=[act, act, col, col], out_specs=[osp, osp],
        out_shape=[_sds((N_DEV, kdim, nb), BF16)] * 2,
        compiler_params=_params(("parallel",), blocks, 4 * kdim * nb * 4),
    )(ca, o, dya, dyb)


def _proj_bwd_act(dproj, w_in, deps=()):
    t, n = dproj.shape
    d, nb = w_in.shape[2], w_in.shape[3]
    tm = min(t, 512)

    def epilogue(acc, ins, outs):
        outs[0][...] = acc

    def products(ins):
        return (lax.dot_general(ins[0][:, 0:nb], ins[1][0], NT, preferred_element_type=F32)
                + lax.dot_general(ins[0][:, nb:2 * nb], ins[1][1], NT, preferred_element_type=F32))

    return _fused("mix_bwd_dh", (t // tm, 1, 4),
                  [(dproj, pl.BlockSpec((tm, 2 * nb), lambda i, j, k: (i, k))),
                   (w_in, pl.BlockSpec((None, 2, d, nb), lambda i, j, k: (k, 0, 0, 0)))],
                  [(_sds((t, d), F32), pl.BlockSpec((tm, d), lambda i, j, k: (i, 0)))],
                  products, epilogue, nk=4, acc_shape=(tm, d), temp_bytes=tm * d * 4, deps=deps)[0]


def _proj_bwd_w(h, dproj):
    t, d = h.shape
    nb = dproj.shape[1] // N_DEV
    tm = min(d, 512)

    def body(h_ref, dp_ref, o_ref):
        hv = h_ref[...]
        o_ref[0] = lax.dot_general(hv, dp_ref[:, 0:nb], TN, preferred_element_type=F32).astype(BF16)
        o_ref[1] = lax.dot_general(hv, dp_ref[:, nb:2 * nb], TN, preferred_element_type=F32).astype(BF16)

    blocks = t * tm * 2 + t * 2 * nb * 2 + 2 * tm * nb * 2
    return pl.pallas_call(
        body, name="mix_bwd_dwin", grid=(4, d // tm),
        in_specs=[pl.BlockSpec((t, tm), lambda j, i: (0, i)),
                  pl.BlockSpec((t, 2 * nb), lambda j, i: (0, j))],
        out_specs=pl.BlockSpec((None, 2, tm, nb), lambda j, i: (j, 0, i, 0)),
        out_shape=_sds((4, 2, d, nb), BF16),
        compiler_params=_params(("parallel", "parallel"), blocks, 4 * tm * nb * 4),
    )(h, dproj)


def _adamw_math(w, g, m, v):
    m = ADAM_B1 * m + (1.0 - ADAM_B1) * g
    v = ADAM_B2 * v + (1.0 - ADAM_B2) * (g * g)
    m_hat = m / (1.0 - ADAM_B1 ** ADAM_STEP)
    v_hat = v / (1.0 - ADAM_B2 ** ADAM_STEP)
    delta = -ADAM_LR * (m_hat / (jnp.sqrt(v_hat) + ADAM_EPS) + ADAM_WD * w)
    return delta, m, v


def _adamw(name, parts, w, m, v, tr):
    r, c = w.shape

    def body(p_ref, w_ref, m_ref, v_ref, g_out, d_out, m_out, v_out):
        g = p_ref[0].astype(F32)
        for s in range(1, N_DEV):
            g = g + p_ref[s].astype(F32)
        delta, mn, vn = _adamw_math(w_ref[...], g, m_ref[...], v_ref[...])
        g_out[...] = g
        d_out[...] = delta
        m_out[...] = mn
        v_out[...] = vn

    blk = pl.BlockSpec((tr, c), lambda i: (i, 0))
    blocks = N_DEV * tr * c * parts.dtype.itemsize + 7 * tr * c * 4
    return pl.pallas_call(
        body, name=name, grid=(r // tr,),
        in_specs=[pl.BlockSpec((N_DEV, tr, c), lambda i: (0, i, 0)), blk, blk, blk],
        out_specs=[blk] * 4, out_shape=[_sds((r, c), F32)] * 4,
        compiler_params=_params(("parallel",), blocks, 6 * tr * c * 4),
    )(parts, w, m, v)


def _chip_sum(sums_ref):
    g = sums_ref[0].astype(F32)
    for k in range(1, 4):
        g = g + sums_ref[k].astype(F32)
    return g


def _adamw_chips(name, sums, w, m, v, tr, deps=(), row0=0, into=None):
    r, c = w.shape
    rs = sums.shape[1]
    i0 = row0 // tr
    n_pass = len(deps) + (4 if into is not None else 0)

    def body(sums_ref, w_ref, m_ref, v_ref, *rest):
        g_out, d_out, m_out, v_out = rest[n_pass:]
        g = _chip_sum(sums_ref)
        delta, mn, vn = _adamw_math(w_ref[...], g, m_ref[...], v_ref[...])
        g_out[...] = g
        d_out[...] = delta
        m_out[...] = mn
        v_out[...] = vn

    blk = pl.BlockSpec((tr, c), lambda i: (i0 + i, 0))
    blocks = 4 * tr * c * 2 + 7 * tr * c * 4
    passed = list(deps) + (list(into) if into is not None else [])
    aliases = {4 + len(deps) + q: q for q in range(4)} if into is not None else {}
    return pl.pallas_call(
        body, name=name, grid=(rs // tr,),
        in_specs=[pl.BlockSpec((4, tr, c), lambda i: (0, i, 0)), blk, blk, blk] + [_ANY] * n_pass,
        out_specs=[blk] * 4, out_shape=[_sds((r, c), F32)] * 4,
        input_output_aliases=aliases,
        compiler_params=_params(("parallel",), blocks, 6 * tr * c * 4),
    )(sums, w, m, v, *passed)


def _adamw_side(contrib, w, m, v, n_tiles, step_of):
    r, c = w.shape
    tr = r // n_tiles
    assert tr * n_tiles == r and tr % 16 == 0, (r, n_tiles)

    def tile(i, j, k):
        return jnp.minimum(step_of(i, j, k), n_tiles - 1)

    blk = pl.BlockSpec((tr, c), lambda i, j, k: (tile(i, j, k), 0))
    ins = [(contrib, pl.BlockSpec((4, tr, c), lambda i, j, k: (0, tile(i, j, k), 0))), (w, blk), (m, blk), (v, blk)]
    outs = [(_sds((r, c), F32), blk)] * 4

    def fn(in_refs, out_refs):
        @pl.when(step_of(pl.program_id(0), pl.program_id(1), pl.program_id(2)) < n_tiles)
        def _():
            g = _chip_sum(in_refs[0])
            delta, mn, vn = _adamw_math(in_refs[1][...], g, in_refs[2][...], in_refs[3][...])
            out_refs[0][...] = g
            out_refs[1][...] = delta
            out_refs[2][...] = mn
            out_refs[3][...] = vn

    return ins, outs, fn


def _rope_tables(t):
    half = ROT_DIM // 2
    inv_freq = 1.0 / (ROPE_THETA ** (jnp.arange(0, ROT_DIM, 2, dtype=F32) / ROT_DIM))
    ang = jnp.arange(t, dtype=F32)[:, None] * inv_freq[None, :]
    cos, sin = jnp.cos(ang), jnp.sin(ang)
    ones = jnp.ones((t, HEAD_DIM - ROT_DIM), F32)
    zeros = jnp.zeros((t, HEAD_DIM - half), F32)
    c = jnp.concatenate([cos, cos, ones], axis=1)
    sa = jnp.concatenate([-sin, zeros], axis=1)
    sb = jnp.concatenate([jnp.zeros((t, half), F32), sin, jnp.zeros((t, HEAD_DIM - ROT_DIM), F32)], axis=1)
    return tuple(jnp.tile(a, (1, LANES // HEAD_DIM)) for a in (c, sa, sb))


def _pad_rows(a, rows=8):
    return jnp.pad(a, ((0, rows - a.shape[0]), (0, 0)))


def kernel(x, g_ffn1, w_gu1, w_down1, g_mix, w_in, conv_w, q_norm_g, k_norm_g, sinks, w_out_conv, w_out_attn, w_o, g_ffn2, w_gu2, w_down2, loss_target, m_g_ffn1, m_w_gu1, m_w_down1, m_g_mix, m_w_in, m_conv_w, m_q_norm_g, m_k_norm_g, m_sinks, m_w_out_conv, m_w_out_attn, m_w_o, m_g_ffn2, m_w_gu2, m_w_down2, v_g_ffn1, v_w_gu1, v_w_down1, v_g_mix, v_w_in, v_conv_w, v_q_norm_g, v_k_norm_g, v_sinks, v_w_out_conv, v_w_out_attn, v_w_o, v_g_ffn2, v_w_gu2, v_w_down2):
    t, d = x.shape[1], x.shape[2]
    cw = d // 2
    kw = cw // GROUP
    nq = cw // HEAD_DIM
    xs, target = x.reshape(t, d), loss_target.reshape(t, d)
    me = 4 * lax.axis_index("x") + 2 * lax.axis_index("y") + lax.axis_index("c")

    big = {"w_gu1": w_gu1, "w_down1": w_down1, "w_in": w_in, "w_out_conv": w_out_conv,
           "w_out_attn": w_out_attn, "w_o": w_o, "w_gu2": w_gu2, "w_down2": w_down2}
    big_m = {"w_gu1": m_w_gu1, "w_down1": m_w_down1, "w_in": m_w_in, "w_out_conv": m_w_out_conv,
             "w_out_attn": m_w_out_attn, "w_o": m_w_o, "w_gu2": m_w_gu2, "w_down2": m_w_down2}
    big_v = {"w_gu1": v_w_gu1, "w_down1": v_w_down1, "w_in": v_w_in, "w_out_conv": v_w_out_conv,
             "w_out_attn": v_w_out_attn, "w_o": v_w_o, "w_gu2": v_w_gu2, "w_down2": v_w_down2}
    names = list(big)

    tiles = {"w_gu1": 256, "w_gu2": 256, "w_in": 256, "w_down1": 176, "w_down2": 176,
             "w_out_conv": 1024, "w_out_attn": 1024, "w_o": 128}

    def row_tile(n):
        r = big[n].shape[1]
        return tiles[n] if r % tiles[n] == 0 else r

    rs_shape = {n: big[n].shape[1:] for n in names}
    half = rs_shape["w_gu1"][0] // 2
    rs_shape["w_gu1_lo"] = rs_shape["w_gu1_hi"] = (half, rs_shape["w_gu1"][1])

    def add_tile(n):
        r, c = rs_shape[n]
        while r * c * 2 > (3 << 20) and r % 32 == 0:
            r //= 2
        return r

    me_arr = me.astype(jnp.int32).reshape(1)
    sources = [(n, big[n][0], BF16, row_tile(n)) for n in names] + [("conv_w", _pad_rows(conv_w[0]), F32, 8)]
    issue_order = [0, 1, 2, 8, 3, 4, 5, 6, 7]
    first = _place_shard("place_" + names[0], sources[0][1], BF16, me_arr, sources[0][3])
    started = [_gather_start("gather_start_first", [first])]
    early = {2: (big_m["w_in"][0], big_v["w_in"][0])}
    rest = [_place_shard("place_" + sources[i][0], sources[i][1], sources[i][2], me_arr, sources[i][3],
                         deps=(started[0][3],) + early.get(i, ())) for i in issue_order[1:]]
    started.append(_gather_start("gather_start_rest", rest))
    where = {0: (0, 0)}
    where.update({i: (1, p) for p, i in enumerate(issue_order[1:])})

    def fetch(tag, idxs, after):
        call = where[idxs[0]][0]
        send, recv, stacks, _ = started[call]
        positions = [where[i][1] for i in idxs]
        got = _gather_wait("gather_wait_" + tag, positions, send, recv, [stacks[p] for p in positions], after)
        return _forward_to_sibling("gather_forward_" + tag, got)

    rope_tabs = _rope_tables(t)
    gq = jnp.tile(q_norm_g, (1, nq))
    gk = jnp.tile(k_norm_g, (1, nq // GROUP))
    sink_rows = jnp.broadcast_to(sinks[0][:, None], (nq, LANES))

    wts = {}
    h1 = _rms_fwd("ffn1_norm", xs, g_ffn1)
    wts["w_gu1"], = fetch("gu1", [0], started[1][3])
    gu1, a1 = _ffn_up("ffn1_up", h1, wts["w_gu1"])
    wts["w_down1"], = fetch("down1", [1], a1)
    wd1 = wts["w_down1"].reshape(-1, d)
    x1 = _ffn_down("ffn1_down", a1, wd1, xs)
    h2 = _rms_fwd("mix_norm", x1, g_mix)
    wts["w_in"], conv_land = fetch("in", [2, 8], h2)
    w_in_full = wts["w_in"].reshape(4, 2, d, -1)
    conv_full = jnp.transpose(conv_land, (1, 0, 2)).reshape(8, cw)
    proj = _proj(h2, w_in_full)
    ca = _conv_fwd(proj, conv_full)
    qn, kn, vb = _qk_prep(proj, gq, gk, rope_tabs, cw, kw)
    o = _attn_fwd(qn, kn, vb, sink_rows)
    wts["w_out_conv"], wts["w_out_attn"] = fetch("out", [3, 4], o)
    merged, ya, yb = _mix_out(ca, o, wts["w_out_conv"], wts["w_out_attn"], proj)
    wts["w_o"], = fetch("o", [5], merged)
    wo = wts["w_o"].reshape(d, d)
    x2 = _mix_residual(merged, wo, x1)
    h3 = _rms_fwd("ffn2_norm", x2, g_ffn2)
    wts["w_gu2"], = fetch("gu2", [6], h3)
    gu2, a2 = _ffn_up("ffn2_up", h3, wts["w_gu2"])
    wts["w_down2"], = fetch("down2", [7], a2)
    wd2 = wts["w_down2"].reshape(-1, d)
    dy, sq, dy_bf = _ffn_down("ffn2_down", a2, wd2, x2, target=target)
    loss = lax.psum(sq[0, 0] * (0.5 / d), ("x", "y", "c"))

    place = jnp.stack([lax.axis_index("c"), 2 * lax.axis_index("x") + lax.axis_index("y")]).astype(jnp.int32)
    def pair_start(tag, group, grads, deps=()):
        stacks = [grads[n].reshape((4, 2) + rs_shape[n]) for n in group]
        lands = [lax.empty((4,) + rs_shape[n], BF16) for n in group]
        return _pair_start("rs_pair_start_" + tag, stacks, lands, deps)

    def chip_start(tag, group, pending, after):
        send, recv, stacks, lands, _ = pending
        stacks, lands = _pair_wait("rs_pair_wait_" + tag, send, recv, stacks, lands, after)
        added = [_pair_add("rs_pair_add_" + n, st, ld, place, add_tile(n)) for n, st, ld in zip(group, stacks, lands)]
        return _chip_start("rs_chip_start_" + tag, [a[0] for a in added], [a[1] for a in added])

    group_a, group_b, group_c = ["w_down2", "w_gu2"], ["w_o", "w_out_conv", "w_out_attn"], ["w_in"]
    group_d, group_e, group_f = ["w_down1"], ["w_gu1_lo"], ["w_gu1_hi"]
    g = {}
    dgu2, a2 = _ffn_bwd_act("ffn2_bwd_act", dy_bf, wd2, gu2)
    g["w_down2"], = _ffn_bwd_dwd("ffn2_bwd_dwd", a2, dy_bf)
    g["w_gu2"], = _ffn_bwd_dwgu("ffn2_bwd_dwgu", h3, dgu2)
    pend_a = pair_start("a", group_a, g)
    dh3, = _ffn_bwd_dh("ffn2_bwd_dh", dgu2, wts["w_gu2"], deps=(pend_a[4],))
    ring_a = chip_start("a", group_a, pend_a, dh3)
    dx2, dg_ffn2, dx2_bf = _rms_bwd("ffn2_bwd_rms", x2, g_ffn2, dh3, dy, deps=(ring_a[4],), with_bf16=True)

    dya, dyb, dgates = _mix_bwd_gates(dx2_bf, wo, ya, yb, proj, cw)
    g["w_o"] = _tn_matmul("mix_bwd_dwo", merged, dx2_bf, min(d, 512))
    g["w_out_conv"], g["w_out_attn"] = _out_proj_bwd_w(ca, o, dya, dyb, d // N_DEV)
    pend_b = pair_start("b", group_b, g)
    dca, do = _out_proj_bwd_act(dya, dyb, wts["w_out_conv"], wts["w_out_attn"], deps=(pend_b[4],))
    ring_b = chip_start("b", group_b, pend_b, do)
    d3, dconv_w = _conv_bwd(proj, conv_full, dca, deps=(ring_b[4],))
    dq, dkc, dkp, dvc, dvp, dsink = _attn_bwd(qn, kn, vb, sink_rows, do)
    dqkv, dgq, dgk = _qk_prep_bwd(proj, gq, gk, rope_tabs, dq, dkc, dkp, dvc, dvp, cw, kw)
    dproj = jnp.concatenate([d3[0], d3[1], d3[2], dqkv, dgates[0], dgates[1]], axis=1)
    g["w_in"] = _proj_bwd_w(h2, dproj)
    pend_c = pair_start("c", group_c, g)
    dh2 = _proj_bwd_act(dproj, w_in_full, deps=(pend_c[4],))
    ring_c = chip_start("c", group_c, pend_c, dh2)
    dx1, dg_mix, dx1_bf = _rms_bwd("mix_bwd_rms", x1, g_mix, dh2, dx2, deps=(ring_c[4],), with_bf16=True)

    big_out = {}
    arrived = {}

    def wait_group(tag, group, ring, after):
        send, recv, parts, lands2, _ = ring
        parts, lands2 = _chip_wait("rs_chip_wait_" + tag, send, recv, parts, lands2, after)
        arrived.update(dict(zip(group, lands2)))

    def update(n, after):
        res = _adamw_chips("adamw_" + n, arrived[n], big[n][0], big_m[n][0], big_v[n][0], row_tile(n), deps=(after,))
        big_out[n] = [a[None] for a in res]
        return res[0]

    def update_beside(n, n_tiles, step_of):
        return _adamw_side(arrived[n], big[n][0], big_m[n][0], big_v[n][0], n_tiles, step_of)

    def keep(n, res):
        big_out[n] = [a[None] for a in res]

    dgu1, a1 = _ffn_bwd_act("ffn1_bwd_act", dx1_bf, wd1, gu1)
    wait_group("a", group_a, ring_a, a1)
    g["w_down1"], *res = _ffn_bwd_dwd("ffn1_bwd_dwd", a1, dx1_bf,
                                       side=update_beside("w_down2", 11, lambda i, j, k: i * 4 + j))
    keep("w_down2", res)
    pend_d = pair_start("d", group_d, g)
    g["w_gu1_lo"], *res = _ffn_bwd_dwgu("ffn1_bwd_dwgu_lo", h1, dgu1, deps=(pend_d[4],), rows=(0, half),
                                         side=update_beside("w_gu2", 16, lambda i, j, k: i * 2 + j))
    keep("w_gu2", res)
    ring_d = chip_start("d", group_d, pend_d, g["w_gu1_lo"])
    pend_e = pair_start("e", group_e, g, deps=(ring_d[4],))
    wait_group("c", group_c, ring_c, pend_e[4])
    g["w_gu1_hi"], *res = _ffn_bwd_dwgu("ffn1_bwd_dwgu_hi", h1, dgu1, rows=(half, half),
                                         side=update_beside("w_in", 16, lambda i, j, k: i * 2 + j))
    keep("w_in", res)
    ring_e = chip_start("e", group_e, pend_e, g["w_gu1_hi"])
    pend_f = pair_start("f", group_f, g, deps=(ring_e[4],))
    wait_group("b", group_b, ring_b, pend_f[4])
    after = pend_f[4]
    for n in group_b:
        after = update(n, after)
    ring_f = chip_start("f", group_f, pend_f, after)
    wait_group("d", group_d, ring_d, ring_f[4])
    dh1, *res = _ffn_bwd_dh("ffn1_bwd_dh", dgu1, wts["w_gu1"],
                             side=update_beside("w_down1", 11, lambda i, j, k: i * 4 + k))
    keep("w_down1", res)
    grad_x, dg_ffn1 = _rms_bwd("ffn1_bwd_rms", xs, g_ffn1, dh1, dx1)
    after = grad_x
    n = "w_gu1"
    wait_group("e", group_e, ring_e, after)
    res = _adamw_chips("adamw_w_gu1_lo", arrived["w_gu1_lo"], big[n][0], big_m[n][0], big_v[n][0], row_tile(n), deps=(after,))
    wait_group("f", group_f, ring_f, res[0])
    res = _adamw_chips("adamw_w_gu1_hi", arrived["w_gu1_hi"], big[n][0], big_m[n][0], big_v[n][0], row_tile(n),
                       row0=half, into=res)
    keep(n, res)
    after = res[0]

    small = {"g_ffn1": dg_ffn1[0:1], "g_mix": dg_mix[0:1], "g_ffn2": dg_ffn2[0:1],
             "q_norm_g": dgq[0:1, :HEAD_DIM], "k_norm_g": dgk[0:1, :HEAD_DIM], "sinks": dsink[:, 0][None],
             "conv_w": dconv_w[0:CONV_K].reshape(1, -1)}
    small_w = {"g_ffn1": g_ffn1, "g_mix": g_mix, "g_ffn2": g_ffn2, "q_norm_g": q_norm_g, "k_norm_g": k_norm_g,
               "sinks": sinks, "conv_w": None}
    small_m = {"g_ffn1": m_g_ffn1, "g_mix": m_g_mix, "g_ffn2": m_g_ffn2, "q_norm_g": m_q_norm_g,
               "k_norm_g": m_k_norm_g, "sinks": m_sinks, "conv_w": m_conv_w}
    small_v = {"g_ffn1": v_g_ffn1, "g_mix": v_g_mix, "g_ffn2": v_g_ffn2, "q_norm_g": v_q_norm_g,
               "k_norm_g": v_k_norm_g, "sinks": v_sinks, "conv_w": v_conv_w}
    snames = list(small)
    widths = [small[n].shape[1] for n in snames]
    total = sum(widths)
    rows = -(-total // LANES)
    rows = -(-rows // 8) * 8

    def pack(vals):
        flat = jnp.concatenate([v.reshape(1, -1) for v in vals], axis=1)
        return jnp.pad(flat, ((0, 0), (0, rows * LANES - total))).reshape(rows, LANES)

    csh = cw // N_DEV

    def place_conv(local, fill):
        full = jnp.full((CONV_K, cw), fill, F32)
        return lax.dynamic_update_slice(full, local, (0, me * csh)).reshape(1, -1)

    pw = pack([small_w[n] if n != "conv_w" else place_conv(conv_w[0], 0.0) for n in snames])
    pm = pack([small_m[n] if n != "conv_w" else place_conv(m_conv_w[0], 0.0) for n in snames])
    pv = pack([small_v[n] if n != "conv_w" else place_conv(v_conv_w[0], 1.0) for n in snames])
    parts = _all_gather_small("gather_small_grads", pack([small[n] for n in snames]), deps=(after,))
    sg, sd, sm, sv = [a.reshape(1, -1) for a in _adamw("adamw_small", parts, pw, pm, pv, rows)]

    def unpack(flat, n):
        off = sum(widths[:snames.index(n)])
        piece = flat[:, off:off + widths[snames.index(n)]]
        if n == "conv_w":
            piece = lax.dynamic_slice(piece.reshape(CONV_K, cw), (0, me * csh), (CONV_K, csh))[None]
        return piece

    order = ["g_ffn1", "w_gu1", "w_down1", "g_mix", "w_in", "conv_w", "q_norm_g", "k_norm_g", "sinks",
             "w_out_conv", "w_out_attn", "w_o", "g_ffn2", "w_gu2", "w_down2"]
    outs = [loss, grad_x[None]]
    for idx, flat in enumerate((sg, sd, sm, sv)):
        for n in order:
            outs.append(big_out[n][idx] if n in big_out else unpack(flat, n))
    return tuple(outs)
```

```python
import jax
import jax.numpy as jnp
from jax import lax
from jax.experimental import pallas as pl
from jax.experimental.pallas import tpu as pltpu

F32 = jnp.float32
BF16 = jnp.bfloat16

N_DEV = 8
HEAD_DIM = 64
GROUP = 4
BLOCK = 128
ROT_DIM = 16
ROPE_THETA = 500000.0
RMS_EPS = 1e-6
NEG_INF = -1e30
ATTN_SCALE = HEAD_DIM ** -0.5
CONV_K = 3
LANES = 128
MXU_COLS = 256
VMEM_BYTES_V7X = 64 * 1024 * 1024
VMEM_CAP = VMEM_BYTES_V7X - 6 * 1024 * 1024

ADAM_LR = 0.001
ADAM_B1 = 0.9
ADAM_B2 = 0.999
ADAM_EPS = 1e-08
ADAM_WD = 0.01
ADAM_STEP = 10

NN = (((1,), (0,)), ((), ()))
NT = (((1,), (1,)), ((), ()))
TN = (((0,), (0,)), ((), ()))

MESH = pl.DeviceIdType.MESH


def _nbytes(shape, dtype):
    n = 1
    for s in shape:
        if s is not None:
            n *= s
    return n * jnp.dtype(dtype).itemsize


def _params(semantics, block_bytes, temp_bytes):
    assert 2 * block_bytes + temp_bytes <= VMEM_CAP, (block_bytes, temp_bytes)
    return pltpu.CompilerParams(dimension_semantics=semantics, vmem_limit_bytes=VMEM_CAP)


def _fused(name, grid, ins, outs, dots, epilogue, *, nk=1, acc_shape=None, temp_bytes=0,
           semantics=("parallel", "parallel", "arbitrary"), deps=(), side=None):
    n_main_in, n_main_out = len(ins), len(outs)
    if side is not None:
        ins, outs = list(ins) + list(side[0]), list(outs) + list(side[1])
    n_in, n_out = len(ins), len(outs)
    n_dep = len(deps)

    def body(*refs):
        in_refs, out_refs = refs[:n_in], refs[n_in + n_dep:n_in + n_dep + n_out]
        scratch = refs[n_in + n_dep + n_out:]
        if side is not None:
            side[2](in_refs[n_main_in:], out_refs[n_main_out:])

        def products():
            if callable(dots):
                return dots(in_refs)
            total = None
            for ai, bi, contract in dots:
                a, b = in_refs[ai][...], in_refs[bi][...]
                a = a if a.dtype == BF16 else a.astype(BF16)
                b = b if b.dtype == BF16 else b.astype(BF16)
                p = lax.dot_general(a, b, contract, preferred_element_type=F32)
                total = p if total is None else total + p
            return total

        if nk == 1:
            epilogue(products() if dots else None, in_refs, out_refs)
        else:
            acc = scratch[0]
            k = pl.program_id(2)

            @pl.when(k == 0)
            def _():
                acc[...] = jnp.zeros_like(acc)

            acc[...] += products()

            @pl.when(k == nk - 1)
            def _():
                epilogue(acc[...], in_refs, out_refs)

    block_bytes = sum(_nbytes(spec.block_shape, a.dtype) for a, spec in ins)
    block_bytes += sum(_nbytes(spec.block_shape, s.dtype) for s, spec in outs)
    scratch_shapes = []
    if nk > 1:
        scratch_shapes.append(pltpu.VMEM(acc_shape, F32))
        temp_bytes += _nbytes(acc_shape, F32)
    res = pl.pallas_call(
        body, name=name, grid=grid,
        in_specs=[spec for _, spec in ins] + [pl.BlockSpec(memory_space=pl.ANY)] * n_dep,
        out_specs=[spec for _, spec in outs],
        out_shape=[s for s, _ in outs],
        scratch_shapes=scratch_shapes,
        compiler_params=_params(semantics, block_bytes, temp_bytes),
    )(*[a for a, _ in ins], *deps)
    return res


def _sds(shape, dtype):
    return jax.ShapeDtypeStruct(shape, dtype)


def _sigmoid(x):
    return jax.nn.sigmoid(x)


def _all_gather_small(name, shard, deps=()):
    n_dep = len(deps)

    def body(src, *rest):
        dst, send_sems, recv_sems, local_sem = rest[n_dep:]
        x, y, c = lax.axis_index("x"), lax.axis_index("y"), lax.axis_index("c")
        me = 4 * x + 2 * y + c
        copies = [pltpu.make_async_copy(src, dst.at[me], local_sem)]
        for k in range(1, N_DEV):
            peer = ((1 - x) if (k & 4) else x, (1 - y) if (k & 2) else y, (1 - c) if (k & 1) else c)
            copies.append(pltpu.make_async_remote_copy(
                src_ref=src, dst_ref=dst.at[me], send_sem=send_sems.at[k - 1], recv_sem=recv_sems.at[k - 1],
                device_id=peer, device_id_type=MESH))
        for cp in copies:
            cp.start()
        for cp in copies:
            cp.wait()

    hbm = pl.BlockSpec(memory_space=pltpu.HBM)
    return pl.pallas_call(
        body, name=name,
        in_specs=[hbm] + [pl.BlockSpec(memory_space=pl.ANY)] * n_dep, out_specs=hbm,
        out_shape=_sds((N_DEV,) + shard.shape, shard.dtype),
        scratch_shapes=[pltpu.SemaphoreType.DMA((N_DEV - 1,)), pltpu.SemaphoreType.DMA((N_DEV - 1,)),
                        pltpu.SemaphoreType.DMA],
    )(shard, *deps)


_HBM = pl.BlockSpec(memory_space=pltpu.HBM)
_SEM = pl.BlockSpec(memory_space=pltpu.SEMAPHORE)
_ANY = pl.BlockSpec(memory_space=pl.ANY)
_EFFECT = pltpu.SideEffectType.DATAFLOW_SIDE_EFFECTING
N_TARGETS = 4


def _mesh_pos():
    return lax.axis_index("x"), lax.axis_index("y"), lax.axis_index("c")


def _chip_peers(x, y, c):
    return [(1 - x, y, c), (x, 1 - y, c), (1 - x, 1 - y, c)]


def _dev_index(pos):
    return 4 * pos[0] + 2 * pos[1] + pos[2]


def _hbm_like(a):
    return pltpu.HBM(a.shape, a.dtype)


def _place_shard(name, w, out_dtype, me, tr, deps=()):
    r, c = w.shape
    n_dep = len(deps)

    def body(me_ref, w_ref, *rest):
        rest[n_dep][...] = w_ref[...].astype(out_dtype)

    grid_spec = pltpu.PrefetchScalarGridSpec(
        num_scalar_prefetch=1, grid=(r // tr,),
        in_specs=[pl.BlockSpec((tr, c), lambda i, me_ref: (i, 0))] + [_ANY] * n_dep,
        out_specs=pl.BlockSpec((None, tr, c), lambda i, me_ref: (me_ref[0], i, 0)))
    return pl.pallas_call(
        body, name=name, grid_spec=grid_spec, out_shape=_sds((N_DEV, r, c), out_dtype),
        compiler_params=_params(("parallel",), tr * c * 6, tr * c * 4),
    )(me, w, *deps)


def _gather_start(name, lands):
    n = len(lands)

    def body(*refs):
        bufs = refs[:n]
        send, recv = refs[n], refs[n + 1]
        token = refs[-1]
        x, y, c = _mesh_pos()
        me = _dev_index((x, y, c))
        targets = [(x, y, 1 - c)] + _chip_peers(x, y, c)
        for w in range(n):
            for k, to in enumerate(targets):
                pltpu.make_async_remote_copy(
                    src_ref=bufs[w].at[me], dst_ref=bufs[w].at[me],
                    send_sem=send.at[N_TARGETS * w + k], recv_sem=recv.at[N_TARGETS * w + k],
                    device_id=to, device_id_type=MESH).start()
        token[...] = jnp.zeros_like(token)

    sems = pltpu.SemaphoreType.DMA((N_TARGETS * n,))
    outs = pl.pallas_call(
        body, name=name,
        in_specs=[_HBM] * n, out_specs=[_SEM, _SEM] + [_HBM] * n + [_token_spec()],
        out_shape=[sems, sems] + [_hbm_like(a) for a in lands] + [_sds((8, LANES), F32)],
        input_output_aliases={i: 2 + i for i in range(n)},
        compiler_params=pltpu.CompilerParams(has_side_effects=_EFFECT),
    )(*lands)
    return outs[0], outs[1], list(outs[2:2 + n]), outs[-1]


def _gather_wait(name, positions, send, recv, lands, after):
    m = len(positions)

    def body(*refs):
        bufs = refs[:m]
        send_sems, recv_sems = refs[m], refs[m + 1]
        x, y, c = _mesh_pos()
        me = _dev_index((x, y, c))
        sources = [(x, y, 1 - c)] + _chip_peers(x, y, c)
        for j, w in enumerate(positions):
            for k, frm in enumerate(sources):
                cp = pltpu.make_async_remote_copy(
                    src_ref=bufs[j].at[me], dst_ref=bufs[j].at[_dev_index(frm)],
                    send_sem=send_sems.at[N_TARGETS * w + k], recv_sem=recv_sems.at[N_TARGETS * w + k],
                    device_id=frm, device_id_type=MESH)
                cp.wait_send()
                cp.wait_recv()

    outs = pl.pallas_call(
        body, name=name,
        in_specs=[_HBM] * m + [_SEM, _SEM, _ANY], out_specs=[_HBM] * m,
        out_shape=[_hbm_like(a) for a in lands],
        input_output_aliases={i: i for i in range(m)},
        compiler_params=pltpu.CompilerParams(has_side_effects=_EFFECT),
    )(*lands, send, recv, after)
    return list(outs)


def _forward_to_sibling(name, lands):
    m = len(lands)

    def body(*refs):
        bufs = refs[m:2 * m]
        send_sems, recv_sems = refs[2 * m], refs[2 * m + 1]
        x, y, c = _mesh_pos()
        copies = []
        for j in range(m):
            for k, chip in enumerate(_chip_peers(x, y, c)):
                block = bufs[j].at[_dev_index(chip)]
                cp = pltpu.make_async_remote_copy(
                    src_ref=block, dst_ref=block,
                    send_sem=send_sems.at[3 * j + k], recv_sem=recv_sems.at[3 * j + k],
                    device_id=(x, y, 1 - c), device_id_type=MESH)
                cp.start()
                copies.append(cp)
        for cp in copies:
            cp.wait()

    outs = pl.pallas_call(
        body, name=name,
        in_specs=[_HBM] * m, out_specs=[_HBM] * m,
        out_shape=[_sds(a.shape, a.dtype) for a in lands],
        input_output_aliases={i: i for i in range(m)},
        scratch_shapes=[pltpu.SemaphoreType.DMA((3 * m,)), pltpu.SemaphoreType.DMA((3 * m,))],
    )(*lands)
    return list(outs)


def _forward_copies(bufs, send, recv):
    x, y, c = _mesh_pos()
    copies = []
    for j, buf in enumerate(bufs):
        for k, chip in enumerate(_chip_peers(x, y, c)):
            block = buf.at[_dev_index(chip)]
            copies.append(pltpu.make_async_remote_copy(
                src_ref=block, dst_ref=block, send_sem=send.at[3 * j + k], recv_sem=recv.at[3 * j + k],
                device_id=(x, y, 1 - c), device_id_type=MESH))
    return copies


def _forward_start(name, lands):
    m = len(lands)

    def body(*refs):
        for cp in _forward_copies(refs[:m], refs[m], refs[m + 1]):
            cp.start()

    sems = pltpu.SemaphoreType.DMA((3 * m,))
    outs = pl.pallas_call(
        body, name=name,
        in_specs=[_HBM] * m, out_specs=[_SEM, _SEM] + [_HBM] * m,
        out_shape=[sems, sems] + [_hbm_like(a) for a in lands],
        input_output_aliases={i: 2 + i for i in range(m)},
        compiler_params=pltpu.CompilerParams(has_side_effects=_EFFECT),
    )(*lands)
    return outs[0], outs[1], list(outs[2:])


def _forward_wait(name, send, recv, lands, after):
    m = len(lands)

    def body(*refs):
        for cp in _forward_copies(refs[:m], refs[m], refs[m + 1]):
            cp.wait_send()
            cp.wait_recv()

    outs = pl.pallas_call(
        body, name=name,
        in_specs=[_HBM] * m + [_SEM, _SEM, _ANY], out_specs=[_HBM] * m,
        out_shape=[_hbm_like(a) for a in lands],
        input_output_aliases={i: i for i in range(m)},
        compiler_params=pltpu.CompilerParams(has_side_effects=_EFFECT),
    )(*lands, send, recv, after)
    return list(outs)


def _token_spec():
    return pl.BlockSpec(memory_space=pltpu.VMEM)


def _pair_start(name, stacks, lands, deps=()):
    n = len(stacks)
    n_dep = len(deps)

    def body(*refs):
        srcs, dsts = refs[:n], refs[n:2 * n]
        send, recv = refs[2 * n + n_dep], refs[2 * n + n_dep + 1]
        token = refs[-1]
        x, y, c = _mesh_pos()
        for w in range(n):
            for chip in range(4):
                pltpu.make_async_remote_copy(
                    src_ref=srcs[w].at[chip, 1 - c], dst_ref=dsts[w].at[chip],
                    send_sem=send.at[4 * w + chip], recv_sem=recv.at[4 * w + chip],
                    device_id=(x, y, 1 - c), device_id_type=MESH).start()
        token[...] = jnp.zeros_like(token)

    sems = pltpu.SemaphoreType.DMA((4 * n,))
    outs = pl.pallas_call(
        body, name=name,
        in_specs=[_HBM] * (2 * n) + [_ANY] * n_dep, out_specs=[_SEM, _SEM] + [_HBM] * (2 * n) + [_token_spec()],
        out_shape=[sems, sems] + [_hbm_like(a) for a in stacks] + [_hbm_like(a) for a in lands] + [_sds((8, LANES), F32)],
        input_output_aliases={i: 2 + i for i in range(2 * n)},
        compiler_params=pltpu.CompilerParams(has_side_effects=_EFFECT),
    )(*stacks, *lands, *deps)
    return outs[0], outs[1], list(outs[2:2 + n]), list(outs[2 + n:2 + 2 * n]), outs[-1]


def _pair_wait(name, send, recv, stacks, lands, after):
    n = len(stacks)

    def body(*refs):
        srcs, dsts = refs[:n], refs[n:2 * n]
        send_sems, recv_sems = refs[2 * n], refs[2 * n + 1]
        x, y, c = _mesh_pos()
        for w in range(n):
            for chip in range(4):
                cp = pltpu.make_async_remote_copy(
                    src_ref=srcs[w].at[chip, 1 - c], dst_ref=dsts[w].at[chip],
                    send_sem=send_sems.at[4 * w + chip], recv_sem=recv_sems.at[4 * w + chip],
                    device_id=(x, y, 1 - c), device_id_type=MESH)
                cp.wait_send()
                cp.wait_recv()

    outs = pl.pallas_call(
        body, name=name,
        in_specs=[_HBM] * (2 * n) + [_SEM, _SEM, _ANY], out_specs=[_HBM] * (2 * n),
        out_shape=[_hbm_like(a) for a in stacks] + [_hbm_like(a) for a in lands],
        input_output_aliases={i: i for i in range(2 * n)},
        compiler_params=pltpu.CompilerParams(has_side_effects=_EFFECT),
    )(*stacks, *lands, send, recv, after)
    return list(outs[:n]), list(outs[n:])


def _pair_add(name, stack, land, place, tr):
    _, _, r, c = stack.shape

    def body(place_ref, a_ref, b_ref, sums_ref, slots_ref):
        total = (a_ref[...].astype(F32) + b_ref[...].astype(F32)).astype(BF16)
        sums_ref[...] = total

        @pl.when(pl.program_id(1) == place_ref[1])
        def _():
            slots_ref[...] = total

    grid_spec = pltpu.PrefetchScalarGridSpec(
        num_scalar_prefetch=1, grid=(r // tr, 4),
        in_specs=[pl.BlockSpec((None, None, tr, c), lambda i, k, place_ref: (k, place_ref[0], i, 0)),
                  pl.BlockSpec((None, tr, c), lambda i, k, place_ref: (k, i, 0))],
        out_specs=[pl.BlockSpec((None, tr, c), lambda i, k, place_ref: (k, i, 0)),
                   pl.BlockSpec((None, tr, c), lambda i, k, place_ref: (place_ref[1], i, 0))])
    return pl.pallas_call(
        body, name=name, grid_spec=grid_spec, out_shape=[_sds((4, r, c), BF16)] * 2,
        compiler_params=_params(("parallel", "arbitrary"), 4 * tr * c * 2, 3 * tr * c * 4),
    )(place, stack, land)


def _chip_start(name, parts, lands):
    n = len(parts)

    def body(*refs):
        srcs, dsts = refs[:n], refs[n:2 * n]
        send, recv = refs[2 * n], refs[2 * n + 1]
        token = refs[-1]
        x, y, c = _mesh_pos()
        for w in range(n):
            for k, to in enumerate(_chip_peers(x, y, c)):
                pltpu.make_async_remote_copy(
                    src_ref=srcs[w].at[2 * to[0] + to[1]], dst_ref=dsts[w].at[2 * x + y],
                    send_sem=send.at[3 * w + k], recv_sem=recv.at[3 * w + k],
                    device_id=to, device_id_type=MESH).start()
        token[...] = jnp.zeros_like(token)

    sems = pltpu.SemaphoreType.DMA((3 * n,))
    outs = pl.pallas_call(
        body, name=name,
        in_specs=[_HBM] * (2 * n), out_specs=[_SEM, _SEM] + [_HBM] * (2 * n) + [_token_spec()],
        out_shape=[sems, sems] + [_hbm_like(a) for a in parts] + [_hbm_like(a) for a in lands] + [_sds((8, LANES), F32)],
        input_output_aliases={i: 2 + i for i in range(2 * n)},
        compiler_params=pltpu.CompilerParams(has_side_effects=_EFFECT),
    )(*parts, *lands)
    return outs[0], outs[1], list(outs[2:2 + n]), list(outs[2 + n:2 + 2 * n]), outs[-1]


def _chip_wait(name, send, recv, parts, lands, after):
    n = len(parts)

    def body(*refs):
        srcs, dsts = refs[:n], refs[n:2 * n]
        send_sems, recv_sems = refs[2 * n], refs[2 * n + 1]
        x, y, c = _mesh_pos()
        for w in range(n):
            for k, frm in enumerate(_chip_peers(x, y, c)):
                chip = 2 * frm[0] + frm[1]
                cp = pltpu.make_async_remote_copy(
                    src_ref=srcs[w].at[chip], dst_ref=dsts[w].at[chip],
                    send_sem=send_sems.at[3 * w + k], recv_sem=recv_sems.at[3 * w + k],
                    device_id=frm, device_id_type=MESH)
                cp.wait_send()
                cp.wait_recv()

    outs = pl.pallas_call(
        body, name=name,
        in_specs=[_HBM] * (2 * n) + [_SEM, _SEM, _ANY], out_specs=[_HBM] * (2 * n),
        out_shape=[_hbm_like(a) for a in parts] + [_hbm_like(a) for a in lands],
        input_output_aliases={i: i for i in range(2 * n)},
        compiler_params=pltpu.CompilerParams(has_side_effects=_EFFECT),
    )(*parts, *lands, send, recv, after)
    return list(outs[:n]), list(outs[n:])


def _row_tile(t):
    return min(t, 256)


def _rms_fwd(name, x, g):
    t, d = x.shape
    tm = _row_tile(t)

    def epilogue(_, ins, outs):
        xv = ins[0][...]
        r = lax.rsqrt(jnp.mean(xv * xv, axis=-1, keepdims=True) + RMS_EPS)
        outs[0][...] = (xv * r * ins[1][...]).astype(BF16)

    row = pl.BlockSpec((tm, d), lambda i, j, k: (i, 0))
    vec = pl.BlockSpec((1, d), lambda i, j, k: (0, 0))
    return _fused(name, (t // tm, 1, 1), [(x, row), (g, vec)], [(_sds((t, d), BF16), row)], [], epilogue,
                  temp_bytes=4 * tm * d * 4)[0]


def _rms_bwd(name, x, g, dh, resid, deps=(), with_bf16=False):
    t, d = x.shape
    tm = _row_tile(t)

    def epilogue(_, ins, outs):
        xv, gv, dhv = ins[0][...], ins[1][...], ins[2][...]
        r = lax.rsqrt(jnp.mean(xv * xv, axis=-1, keepdims=True) + RMS_EPS)
        xh = xv * r
        u = dhv * gv
        dot = jnp.mean(u * xh, axis=-1, keepdims=True)
        dx = ins[3][...] + r * (u - xh * dot)
        outs[0][...] = dx
        if with_bf16:
            outs[2][...] = dx.astype(BF16)

        @pl.when(pl.program_id(0) == 0)
        def _():
            outs[1][...] = jnp.zeros_like(outs[1])

        outs[1][0:1, :] += jnp.sum(dhv * xh, axis=0, keepdims=True)

    row = pl.BlockSpec((tm, d), lambda i, j, k: (i, 0))
    vec = pl.BlockSpec((1, d), lambda i, j, k: (0, 0))
    acc = pl.BlockSpec((8, d), lambda i, j, k: (0, 0))
    outs = [(_sds((t, d), F32), row), (_sds((8, d), F32), acc)] + ([(_sds((t, d), BF16), row)] if with_bf16 else [])
    return _fused(name, (t // tm, 1, 1), [(x, row), (g, vec), (dh, row), (resid, row)], outs, [], epilogue,
                  temp_bytes=6 * tm * d * 4, semantics=("arbitrary", "arbitrary", "arbitrary"), deps=deps)


def _ffn_up(name, h, wgu, parity=None, into=None):
    t, d = h.shape
    nb = wgu.shape[2]
    f = 4 * nb
    tm = min(t, 512)

    def body(h_ref, wg_ref, wu_ref, gu_ref, a_ref):
        hv = h_ref[...]
        for c0 in range(0, nb, MXU_COLS):
            cs = slice(c0, min(c0 + MXU_COLS, nb))
            g = jnp.dot(hv, wg_ref[:, cs], preferred_element_type=F32)
            u = jnp.dot(hv, wu_ref[:, cs], preferred_element_type=F32)
            gu_ref[0, :, cs] = g.astype(BF16)
            gu_ref[1, :, cs] = u.astype(BF16)
            a_ref[:, cs] = (g * _sigmoid(g) * u).astype(BF16)

    blocks = tm * d * 2 + 2 * d * nb * 2 + 3 * tm * nb * 2
    params = _params(("parallel", "parallel"), blocks, 8 * tm * MXU_COLS * 4)
    out_shape = [_sds((2, t, f), BF16), _sds((t, f), BF16)]
    if parity is None:
        return pl.pallas_call(
            body, name=name, grid=(4, t // tm),
            in_specs=[pl.BlockSpec((tm, d), lambda j, i: (i, 0)),
                      pl.BlockSpec((None, d, nb), lambda j, i: (j, 0, 0)),
                      pl.BlockSpec((None, d, nb), lambda j, i: (j + 4, 0, 0))],
            out_specs=[pl.BlockSpec((2, tm, nb), lambda j, i: (0, i, j)),
                       pl.BlockSpec((tm, nb), lambda j, i: (i, j))],
            out_shape=out_shape, compiler_params=params,
        )(h, wgu, wgu)

    def half_body(parity_ref, h_ref, wg_ref, wu_ref, *rest):
        body(h_ref, wg_ref, wu_ref, rest[-2], rest[-1])

    n_pass = 0 if into is None else 2
    grid_spec = pltpu.PrefetchScalarGridSpec(
        num_scalar_prefetch=1, grid=(2, t // tm),
        in_specs=[pl.BlockSpec((tm, d), lambda jj, i, p: (i, 0)),
                  pl.BlockSpec((None, d, nb), lambda jj, i, p: (2 * jj + p[0], 0, 0)),
                  pl.BlockSpec((None, d, nb), lambda jj, i, p: (2 * jj + p[0] + 4, 0, 0))] + [_ANY] * n_pass,
        out_specs=[pl.BlockSpec((2, tm, nb), lambda jj, i, p: (0, i, 2 * jj + p[0])),
                   pl.BlockSpec((tm, nb), lambda jj, i, p: (i, 2 * jj + p[0]))])
    return pl.pallas_call(
        half_body, name=name, grid_spec=grid_spec, out_shape=out_shape,
        input_output_aliases={} if into is None else {4: 0, 5: 1}, compiler_params=params,
    )(parity, h, wgu, wgu, *(into or ()))


def _ffn_down(name, a, wd, x, target=None):
    t, f = a.shape
    d = wd.shape[1]
    tm = min(t, 512)
    tn = min(d, 1024)
    blk = pl.BlockSpec((tm, tn), lambda j, i, k: (i, j))
    ins = [(a, pl.BlockSpec((tm, f), lambda j, i, k: (i, 0))), (wd, pl.BlockSpec((f, tn), lambda j, i, k: (0, j))), (x, blk)]

    if target is None:
        def epilogue(acc, ins, outs):
            outs[0][...] = ins[2][...] + 0.5 * acc

        return _fused(name, (d // tn, t // tm, 1), ins, [(_sds((t, d), F32), blk)],
                      [(0, 1, NN)], epilogue, temp_bytes=2 * tm * tn * 4)[0]

    def epilogue(acc, ins, outs):
        e = ins[2][...] + 0.5 * acc - ins[3][...]
        outs[0][...] = e * (1.0 / d)
        outs[2][...] = (e * (1.0 / d)).astype(BF16)

        @pl.when((pl.program_id(0) == 0) & (pl.program_id(1) == 0))
        def _():
            outs[1][...] = jnp.zeros_like(outs[1])

        part = jnp.sum(jnp.sum(e * e, axis=1, keepdims=True), axis=0, keepdims=True)
        outs[1][...] += jnp.broadcast_to(part, outs[1].shape)

    return _fused(name, (d // tn, t // tm, 1), ins + [(target, blk)],
                  [(_sds((t, d), F32), blk), (_sds((8, LANES), F32), pl.BlockSpec((8, LANES), lambda j, i, k: (0, 0))),
                   (_sds((t, d), BF16), blk)],
                  [(0, 1, NN)], epilogue, temp_bytes=3 * tm * tn * 4,
                  semantics=("arbitrary", "arbitrary", "arbitrary"))


def _ffn_bwd_act(name, dy, wd, gu, deps=()):
    t, d = dy.shape
    f = wd.shape[0]
    nb = f // 4
    tm = min(t, 512)

    def body(dy_ref, wd_ref, gu_ref, *rest):
        dgu_ref, a_ref = rest[-2], rest[-1]
        dyv = dy_ref[...].astype(BF16)
        for c0 in range(0, nb, MXU_COLS):
            cs = slice(c0, min(c0 + MXU_COLS, nb))
            da = 0.5 * lax.dot_general(dyv, wd_ref[cs, :], NT, preferred_element_type=F32)
            g = gu_ref[0, :, cs].astype(F32)
            u = gu_ref[1, :, cs].astype(F32)
            s = _sigmoid(g)
            silu = g * s
            dgu_ref[0, :, cs] = (da * u * (s * (1.0 + g * (1.0 - s)))).astype(BF16)
            dgu_ref[1, :, cs] = (da * silu).astype(BF16)
            a_ref[:, cs] = (silu * u).astype(BF16)

    blocks = tm * d * 4 + nb * d * 2 + 5 * tm * nb * 2
    return pl.pallas_call(
        body, name=name, grid=(4, t // tm),
        in_specs=[pl.BlockSpec((tm, d), lambda j, i: (i, 0)),
                  pl.BlockSpec((nb, d), lambda j, i: (j, 0)),
                  pl.BlockSpec((2, tm, nb), lambda j, i: (0, i, j))] + [_ANY] * len(deps),
        out_specs=[pl.BlockSpec((2, tm, nb), lambda j, i: (0, i, j)), pl.BlockSpec((tm, nb), lambda j, i: (i, j))],
        out_shape=[_sds((2, t, f), BF16), _sds((t, f), BF16)],
        compiler_params=_params(("parallel", "parallel"), blocks, tm * d * 2 + 8 * tm * MXU_COLS * 4),
    )(dy, wd, gu, *deps)


def _ffn_bwd_dwd(name, a, dy, deps=(), side=None):
    t, f = a.shape
    d = dy.shape[1]
    tm = f // 4
    tn = min(d, 512)

    def epilogue(acc, ins, outs):
        outs[0][...] = (0.5 * acc).astype(BF16)

    return _fused(name, (4, d // tn, 1),
                  [(a, pl.BlockSpec((t, tm), lambda i, j, k: (0, i))),
                   (dy, pl.BlockSpec((t, tn), lambda i, j, k: (0, j)))],
                  [(_sds((f, d), BF16), pl.BlockSpec((tm, tn), lambda i, j, k: (i, j)))],
                  [(0, 1, TN)], epilogue, temp_bytes=t * tn * 2 + 2 * tm * tn * 4, deps=deps, side=side)


def _ffn_bwd_dh(name, dgu, wgu, deps=(), side=None):
    _, t, f = dgu.shape
    d, nb = wgu.shape[1], wgu.shape[2]
    tm = min(t, 512)

    def products(ins):
        return (lax.dot_general(ins[0][:, 0:nb], ins[1][0], NT, preferred_element_type=F32)
                + lax.dot_general(ins[0][:, nb:2 * nb], ins[1][1], NT, preferred_element_type=F32))

    def epilogue(acc, ins, outs):
        outs[0][...] = acc

    return _fused(name, (t // tm, 1, 4),
                  [(dgu, pl.BlockSpec((None, tm, 2 * nb), lambda i, j, k: (k // 2, i, k % 2))),
                   (wgu, pl.BlockSpec((2, d, nb), lambda i, j, k: (k, 0, 0)))],
                  [(_sds((t, d), F32), pl.BlockSpec((tm, d), lambda i, j, k: (i, 0)))],
                  products, epilogue, nk=4, acc_shape=(tm, d), temp_bytes=tm * d * 4, deps=deps, side=side)


def _ffn_bwd_dwgu(name, h, dgu, deps=(), side=None, rows=None):
    t, d = h.shape
    nb = dgu.shape[2] // 4
    tm = min(d, 512)
    row0, nrows = rows if rows is not None else (0, d)
    j0 = row0 // tm

    def epilogue(acc, ins, outs):
        outs[0][...] = acc.astype(BF16)

    return _fused(name, (N_DEV, nrows // tm, 1),
                  [(h, pl.BlockSpec((t, tm), lambda i, j, k: (0, j0 + j))),
                   (dgu, pl.BlockSpec((None, t, nb), lambda i, j, k: (i // 4, 0, i % 4)))],
                  [(_sds((N_DEV, nrows, nb), BF16), pl.BlockSpec((None, tm, nb), lambda i, j, k: (i, j, 0)))],
                  [(0, 1, TN)], epilogue, temp_bytes=2 * tm * nb * 4, deps=deps, side=side)


def _proj(h, w_in):
    t, d = h.shape
    nb = w_in.shape[3]
    tm = min(t, 512)

    def body(h_ref, w_ref, o_ref):
        hv = h_ref[...]
        o_ref[:, 0:nb] = jnp.dot(hv, w_ref[0], preferred_element_type=F32).astype(BF16)
        o_ref[:, nb:2 * nb] = jnp.dot(hv, w_ref[1], preferred_element_type=F32).astype(BF16)

    blocks = tm * d * 2 + 2 * d * nb * 2 + tm * 2 * nb * 4
    return pl.pallas_call(
        body, name="mix_proj", grid=(4, t // tm),
        in_specs=[pl.BlockSpec((tm, d), lambda j, i: (i, 0)),
                  pl.BlockSpec((None, 2, d, nb), lambda j, i: (j, 0, 0, 0))],
        out_specs=pl.BlockSpec((tm, 2 * nb), lambda j, i: (i, j)),
        out_shape=_sds((t, N_DEV * nb), BF16),
        compiler_params=_params(("parallel", "parallel"), blocks, 2 * tm * nb * 4),
    )(h, w_in)


def _shift_rows(u, k):
    t = u.shape[0]
    rolled = pltpu.roll(u, k % t, axis=0)
    row = lax.broadcasted_iota(jnp.int32, u.shape, 0)
    keep = (row >= k) if k > 0 else (row < t + k)
    return jnp.where(keep, rolled, 0.0)


def _conv_fwd(proj, conv_w):
    t = proj.shape[0]
    cw = conv_w.shape[1]
    tc = min(cw, 256)
    nc = cw // tc

    def epilogue(_, ins, outs):
        u = ins[2][...].astype(F32) * ins[0][...].astype(F32)
        w = ins[3][...]
        y = u * w[2:3, :] + _shift_rows(u, 1) * w[1:2, :] + _shift_rows(u, 2) * w[0:1, :]
        outs[0][...] = (ins[1][...].astype(F32) * y).astype(BF16)

    def col(seg):
        return pl.BlockSpec((t, tc), lambda i, j, k: (0, seg * nc + i))

    return _fused("conv_fwd", (nc, 1, 1),
                  [(proj, col(0)), (proj, col(1)), (proj, col(2)),
                   (conv_w, pl.BlockSpec((8, tc), lambda i, j, k: (0, i)))],
                  [(_sds((t, cw), BF16), pl.BlockSpec((t, tc), lambda i, j, k: (0, i)))],
                  [], epilogue, temp_bytes=6 * t * tc * 4)[0]


def _conv_bwd(proj, conv_w, dca, deps=()):
    t = proj.shape[0]
    cw = conv_w.shape[1]
    tc = min(cw, 256)
    nc = cw // tc

    def epilogue(_, ins, outs):
        xc, bg, cg = ins[0][...].astype(F32), ins[1][...].astype(F32), ins[2][...].astype(F32)
        w, dc = ins[3][...], ins[4][...]
        u = cg * xc
        u1, u2 = _shift_rows(u, 1), _shift_rows(u, 2)
        y = u * w[2:3, :] + u1 * w[1:2, :] + u2 * w[0:1, :]
        dconv = dc * bg
        du = dconv * w[2:3, :] + _shift_rows(dconv, -1) * w[1:2, :] + _shift_rows(dconv, -2) * w[0:1, :]
        outs[0][0] = (du * cg).astype(BF16)
        outs[0][1] = (dc * y).astype(BF16)
        outs[0][2] = (du * xc).astype(BF16)
        outs[1][...] = jnp.zeros_like(outs[1])
        outs[1][0:1, :] = jnp.sum(dconv * u2, axis=0, keepdims=True)
        outs[1][1:2, :] = jnp.sum(dconv * u1, axis=0, keepdims=True)
        outs[1][2:3, :] = jnp.sum(dconv * u, axis=0, keepdims=True)

    def col(seg):
        return pl.BlockSpec((t, tc), lambda i, j, k: (0, seg * nc + i))

    own = pl.BlockSpec((t, tc), lambda i, j, k: (0, i))
    wspec = pl.BlockSpec((8, tc), lambda i, j, k: (0, i))
    return _fused("conv_bwd", (nc, 1, 1),
                  [(proj, col(0)), (proj, col(1)), (proj, col(2)), (conv_w, wspec), (dca, own)],
                  [(_sds((3, t, cw), BF16), pl.BlockSpec((3, t, tc), lambda i, j, k: (0, 0, i))),
                   (_sds((8, cw), F32), wspec)],
                  [], epilogue, temp_bytes=10 * t * tc * 4, deps=deps)


def _split3(x):
    hi = x.astype(BF16)
    r1 = x - hi.astype(F32)
    mid = r1.astype(BF16)
    lo = (r1 - mid.astype(F32)).astype(BF16)
    return hi, mid, lo


def _head_selector(width):
    r = lax.broadcasted_iota(jnp.int32, (width, LANES), 0)
    c = lax.broadcasted_iota(jnp.int32, (width, LANES), 1)
    return (lax.shift_right_logical(r, 6) == c).astype(BF16)


def _head_sum(x, sel):
    return sum(jnp.dot(p, sel, preferred_element_type=F32) for p in _split3(x))


def _head_bcast(r, sel):
    return sum(lax.dot_general(p, sel, NT, preferred_element_type=F32) for p in _split3(r))


def _rope(x, c, sa, sb):
    n = x.shape[1]
    return x * c + pltpu.roll(x, n - ROT_DIM // 2, axis=1) * sa + pltpu.roll(x, ROT_DIM // 2, axis=1) * sb


def _rope_t(d, c, sa, sb):
    n = d.shape[1]
    return d * c + pltpu.roll(d * sa, ROT_DIM // 2, axis=1) + pltpu.roll(d * sb, n - ROT_DIM // 2, axis=1)


def _tile_lanes(tab, width):
    return tab if width == tab.shape[1] else jnp.tile(tab, (1, width // tab.shape[1]))


def _qk_prep(proj, gq, gk, rope_tabs, cw, kw):
    t = proj.shape[0]
    tm = _row_tile(t)

    def epilogue(_, ins, outs):
        c, sa, sb = ins[5][...], ins[6][...], ins[7][...]
        for src, gain, dst, width in ((0, 3, 0, cw), (1, 4, 1, kw)):
            xv = ins[src][...].astype(F32)
            sel = _head_selector(width)
            r = lax.rsqrt(_head_sum(xv * xv, sel) * (1.0 / HEAD_DIM) + RMS_EPS)
            xn = xv * _head_bcast(r, sel) * ins[gain][...]
            outs[dst][...] = _rope(xn, _tile_lanes(c, width), _tile_lanes(sa, width), _tile_lanes(sb, width)).astype(BF16)
        outs[2][...] = ins[2][...].astype(BF16)

    kblk = cw // kw
    tab = pl.BlockSpec((tm, LANES), lambda i, j, k: (i, 0))
    kspec = pl.BlockSpec((tm, kw), lambda i, j, k: (i, 0))
    return _fused("qk_prep", (t // tm, 1, 1),
                  [(proj, pl.BlockSpec((tm, cw), lambda i, j, k: (i, 3))),
                   (proj, pl.BlockSpec((tm, kw), lambda i, j, k: (i, 4 * kblk))),
                   (proj, pl.BlockSpec((tm, kw), lambda i, j, k: (i, 4 * kblk + 1))),
                   (gq, pl.BlockSpec((1, cw), lambda i, j, k: (0, 0))),
                   (gk, pl.BlockSpec((1, kw), lambda i, j, k: (0, 0))),
                   (rope_tabs[0], tab), (rope_tabs[1], tab), (rope_tabs[2], tab)],
                  [(_sds((t, cw), BF16), pl.BlockSpec((tm, cw), lambda i, j, k: (i, 0))),
                   (_sds((t, kw), BF16), kspec), (_sds((t, kw), BF16), kspec)],
                  [], epilogue, temp_bytes=12 * tm * cw * 4)


def _qk_prep_bwd(proj, gq, gk, rope_tabs, dq, dkc, dkp, dvc, dvp, cw, kw):
    t = proj.shape[0]
    tm = BLOCK
    nblk = t // tm

    def epilogue(_, ins, outs):
        c, sa, sb = ins[5][...], ins[6][...], ins[7][...]
        has_next = (pl.program_id(0) < nblk - 1).astype(F32)
        dk = ins[9][...] + has_next * ins[10][...]
        dv = ins[11][...] + has_next * ins[12][...]
        pieces = []
        for src, gain, dval, dst, width in ((0, 3, ins[8][...], 1, cw), (1, 4, dk, 2, kw)):
            xv, gv = ins[src][...].astype(F32), ins[gain][...]
            sel = _head_selector(width)
            r = _head_bcast(lax.rsqrt(_head_sum(xv * xv, sel) * (1.0 / HEAD_DIM) + RMS_EPS), sel)
            xh = xv * r
            dxn = _rope_t(dval, _tile_lanes(c, width), _tile_lanes(sa, width), _tile_lanes(sb, width))
            u = dxn * gv
            dot = _head_bcast(_head_sum(u * xh, sel), sel) * (1.0 / HEAD_DIM)
            pieces.append((r * (u - xh * dot)).astype(BF16))
            ri = lax.broadcasted_iota(jnp.int32, (width, LANES), 0)
            ci = lax.broadcasted_iota(jnp.int32, (width, LANES), 1)
            fold = (lax.bitwise_and(ri, HEAD_DIM - 1) == ci).astype(BF16)
            colsum = jnp.broadcast_to(jnp.sum(dxn * xh, axis=0, keepdims=True), (8, width))
            part = sum(jnp.dot(p, fold, preferred_element_type=F32) for p in _split3(colsum))

            @pl.when(pl.program_id(0) == 0)
            def _():
                outs[dst][...] = jnp.zeros_like(outs[dst])

            outs[dst][0:1, :] += part[0:1, :]
        outs[0][:, 0:cw] = pieces[0]
        outs[0][:, cw:cw + kw] = pieces[1]
        outs[0][:, cw + kw:cw + 2 * kw] = dv.astype(BF16)

    kblk = cw // kw
    tab = pl.BlockSpec((tm, LANES), lambda i, j, k: (i, 0))
    kcur = pl.BlockSpec((tm, kw), lambda i, j, k: (i, 0))
    knext = pl.BlockSpec((tm, kw), lambda i, j, k: (jnp.minimum(i + 1, nblk - 1), 0))
    acc = pl.BlockSpec((8, LANES), lambda i, j, k: (0, 0))
    return _fused("qk_prep_bwd", (nblk, 1, 1),
                  [(proj, pl.BlockSpec((tm, cw), lambda i, j, k: (i, 3))),
                   (proj, pl.BlockSpec((tm, kw), lambda i, j, k: (i, 4 * kblk))),
                   (proj, pl.BlockSpec((tm, kw), lambda i, j, k: (i, 4 * kblk + 1))),
                   (gq, pl.BlockSpec((1, cw), lambda i, j, k: (0, 0))),
                   (gk, pl.BlockSpec((1, kw), lambda i, j, k: (0, 0))),
                   (rope_tabs[0], tab), (rope_tabs[1], tab), (rope_tabs[2], tab),
                   (dq, pl.BlockSpec((tm, cw), lambda i, j, k: (i, 0))),
                   (dkc, kcur), (dkp, knext), (dvc, kcur), (dvp, knext)],
                  [(_sds((t, cw + 2 * kw), BF16), pl.BlockSpec((tm, cw + 2 * kw), lambda i, j, k: (i, 0))),
                   (_sds((8, LANES), F32), acc), (_sds((8, LANES), F32), acc)],
                  [], epilogue, temp_bytes=16 * tm * cw * 4, semantics=("arbitrary", "arbitrary", "arbitrary"))


def _attn_mask(n):
    key = lax.broadcasted_iota(jnp.int32, (2 * BLOCK, GROUP * BLOCK), 0)
    qry = lax.bitwise_and(lax.broadcasted_iota(jnp.int32, (2 * BLOCK, GROUP * BLOCK), 1), BLOCK - 1)
    return (key > qry) & (key <= qry + BLOCK) & ((key >= BLOCK) | (n > 0))


def _stack_heads(x, h):
    return jnp.concatenate([x[:, (h * GROUP + g) * HEAD_DIM:(h * GROUP + g + 1) * HEAD_DIM] for g in range(GROUP)], axis=0)


def _softmax_with_sink(q4, k2, sink_ref, h, valid):
    sink = jnp.concatenate([sink_ref[h * GROUP + g:h * GROUP + g + 1, :] for g in range(GROUP)], axis=1)
    s = lax.dot_general(k2, q4, NT, preferred_element_type=F32) * ATTN_SCALE
    s = jnp.where(valid, s, NEG_INF)
    m = jnp.maximum(jnp.max(s, axis=0, keepdims=True), sink)
    p = jnp.exp(s - m)
    es = jnp.exp(sink - m)
    inv = 1.0 / (jnp.sum(p, axis=0, keepdims=True) + es)
    return p * inv, es * inv


def _attn_fwd(qn, kn, vb, sink_rows):
    t, cw = qn.shape
    kw = kn.shape[1]
    nkv = kw // HEAD_DIM

    def body(q_ref, kp_ref, kc_ref, vp_ref, vc_ref, sink_ref, o_ref):
        valid = _attn_mask(pl.program_id(0))
        qv = q_ref[...]
        kp, kc, vp, vc = kp_ref[...], kc_ref[...], vp_ref[...], vc_ref[...]
        outs = []
        for h in range(nkv):
            hs = slice(h * HEAD_DIM, (h + 1) * HEAD_DIM)
            k2 = jnp.concatenate([kp[:, hs], kc[:, hs]], axis=0)
            v2 = jnp.concatenate([vp[:, hs], vc[:, hs]], axis=0)
            pn, _ = _softmax_with_sink(_stack_heads(qv, h), k2, sink_ref, h, valid)
            o4 = lax.dot_general(pn.astype(BF16), v2, TN, preferred_element_type=F32)
            outs += [o4[g * BLOCK:(g + 1) * BLOCK] for g in range(GROUP)]
        o_ref[...] = jnp.concatenate(outs, axis=-1).astype(BF16)

    cur = lambda n: (n, 0)
    prev = lambda n: (jnp.maximum(n - 1, 0), 0)
    return pl.pallas_call(
        body, name="attn_fwd", grid=(t // BLOCK,),
        in_specs=[pl.BlockSpec((BLOCK, cw), cur),
                  pl.BlockSpec((BLOCK, kw), prev), pl.BlockSpec((BLOCK, kw), cur),
                  pl.BlockSpec((BLOCK, kw), prev), pl.BlockSpec((BLOCK, kw), cur),
                  pl.BlockSpec(sink_rows.shape, lambda n: (0, 0))],
        out_specs=pl.BlockSpec((BLOCK, cw), cur),
        out_shape=_sds((t, cw), BF16),
        compiler_params=_params(("parallel",), BLOCK * (cw + 4 * kw) * 2 + BLOCK * cw * 2, 8 << 20),
    )(qn, kn, kn, vb, vb, sink_rows)


def _attn_bwd(qn, kn, vb, sink_rows, do):
    t, cw = qn.shape
    kw = kn.shape[1]
    nkv = kw // HEAD_DIM
    nq = nkv * GROUP

    def body(q_ref, kp_ref, kc_ref, vp_ref, vc_ref, sink_ref, do_ref,
             dq_ref, dkc_ref, dkp_ref, dvc_ref, dvp_ref, dsink_ref):
        n = pl.program_id(0)
        valid = _attn_mask(n)
        qv, dov = q_ref[...], do_ref[...]
        kp, kc, vp, vc = kp_ref[...], kc_ref[...], vp_ref[...], vc_ref[...]
        dqs, dks, dvs, dsinks = [], [], [], []
        for h in range(nkv):
            hs = slice(h * HEAD_DIM, (h + 1) * HEAD_DIM)
            k2 = jnp.concatenate([kp[:, hs], kc[:, hs]], axis=0)
            v2 = jnp.concatenate([vp[:, hs], vc[:, hs]], axis=0)
            q4 = _stack_heads(qv, h)
            dob = _stack_heads(dov, h).astype(BF16)
            pn, psink = _softmax_with_sink(q4, k2, sink_ref, h, valid)
            dpn = lax.dot_general(v2, dob, NT, preferred_element_type=F32)
            dvs.append(jnp.dot(pn.astype(BF16), dob, preferred_element_type=F32))
            delta = jnp.sum(pn * dpn, axis=0, keepdims=True)
            ds = (pn * (dpn - delta) * ATTN_SCALE).astype(BF16)
            dks.append(jnp.dot(ds, q4, preferred_element_type=F32))
            dq4 = lax.dot_general(ds, k2, TN, preferred_element_type=F32)
            dsink4 = -psink * delta
            for g in range(GROUP):
                dqs.append(dq4[g * BLOCK:(g + 1) * BLOCK])
                dsinks.append(jnp.broadcast_to(jnp.sum(dsink4[:, g * BLOCK:(g + 1) * BLOCK], axis=1, keepdims=True), (1, LANES)))
        dq_ref[...] = jnp.concatenate(dqs, axis=-1)
        dkp_ref[...] = jnp.concatenate([d[:BLOCK] for d in dks], axis=-1)
        dkc_ref[...] = jnp.concatenate([d[BLOCK:] for d in dks], axis=-1)
        dvp_ref[...] = jnp.concatenate([d[:BLOCK] for d in dvs], axis=-1)
        dvc_ref[...] = jnp.concatenate([d[BLOCK:] for d in dvs], axis=-1)

        @pl.when(n == 0)
        def _():
            dsink_ref[...] = jnp.zeros_like(dsink_ref)

        dsink_ref[...] += jnp.concatenate(dsinks, axis=0)

    cur = lambda n: (n, 0)
    prev = lambda n: (jnp.maximum(n - 1, 0), 0)
    kspec = pl.BlockSpec((BLOCK, kw), cur)
    return pl.pallas_call(
        body, name="attn_bwd", grid=(t // BLOCK,),
        in_specs=[pl.BlockSpec((BLOCK, cw), cur),
                  pl.BlockSpec((BLOCK, kw), prev), kspec,
                  pl.BlockSpec((BLOCK, kw), prev), kspec,
                  pl.BlockSpec(sink_rows.shape, lambda n: (0, 0)),
                  pl.BlockSpec((BLOCK, cw), cur)],
        out_specs=[pl.BlockSpec((BLOCK, cw), cur), kspec, kspec, kspec, kspec,
                   pl.BlockSpec((nq, LANES), lambda n: (0, 0))],
        out_shape=[_sds((t, cw), F32)] + [_sds((t, kw), F32)] * 4 + [_sds((nq, LANES), F32)],
        compiler_params=_params(("arbitrary",), BLOCK * (cw + 4 * kw) * 2 + 2 * BLOCK * cw * 4 + 4 * BLOCK * kw * 4, 12 << 20),
    )(qn, kn, kn, vb, vb, sink_rows, do)


def _mix_out(ca, o, woc, woa, proj):
    t, cw = ca.shape
    nb = woc.shape[2]
    d = N_DEV * nb
    tm = min(t, 1024)
    ga0 = (3 * cw + cw + 2 * (cw // 4)) // nb

    def body(ca_ref, o_ref, woc_ref, woa_ref, ga_ref, gb_ref, m_ref, ya_ref, yb_ref):
        ya = jnp.dot(ca_ref[...], woc_ref[...], preferred_element_type=F32)
        yb = jnp.dot(o_ref[...], woa_ref[...], preferred_element_type=F32)
        ya_ref[...] = ya.astype(BF16)
        yb_ref[...] = yb.astype(BF16)
        m_ref[...] = (_sigmoid(ga_ref[...].astype(F32)) * ya + _sigmoid(gb_ref[...].astype(F32)) * yb).astype(BF16)

    act = pl.BlockSpec((tm, cw), lambda i, j: (i, 0))
    wsp = pl.BlockSpec((None, cw, nb), lambda i, j: (j, 0, 0))
    osp = pl.BlockSpec((tm, nb), lambda i, j: (i, j))
    blocks = 2 * tm * cw * 2 + 2 * cw * nb * 2 + 2 * tm * nb * 4 + 3 * tm * nb * 2
    return pl.pallas_call(
        body, name="mix_out", grid=(t // tm, N_DEV),
        in_specs=[act, act, wsp, wsp,
                  pl.BlockSpec((tm, nb), lambda i, j: (i, ga0 + j)),
                  pl.BlockSpec((tm, nb), lambda i, j: (i, ga0 + N_DEV + j))],
        out_specs=[osp, osp, osp],
        out_shape=[_sds((t, d), BF16)] * 3,
        compiler_params=_params(("parallel", "parallel"), blocks, 6 * tm * nb * 4),
    )(ca, o, woc, woa, proj, proj)


def _mix_residual(merged, wo, x):
    t, d = x.shape
    tm = min(t, 512)

    def epilogue(acc, ins, outs):
        outs[0][...] = ins[2][...] + acc

    row = pl.BlockSpec((tm, d), lambda i, j, k: (i, 0))
    return _fused("mix_residual", (t // tm, 1, 1),
                  [(merged, row), (wo, pl.BlockSpec((d, d), lambda i, j, k: (0, 0))), (x, row)],
                  [(_sds((t, d), F32), row)], [(0, 1, NN)], epilogue, temp_bytes=2 * tm * d * 4)[0]


def _mix_bwd_gates(dx, wo, ya, yb, proj, cw):
    t, d = dx.shape
    tm = min(t, 1024)
    tn = min(d, 512)
    ga0 = (4 * cw + 2 * (cw // 4)) // tn

    def epilogue(acc, ins, outs):
        sa, sb = _sigmoid(ins[4][...].astype(F32)), _sigmoid(ins[5][...].astype(F32))
        outs[0][...] = (acc * sa).astype(BF16)
        outs[1][...] = (acc * sb).astype(BF16)
        outs[2][0] = (acc * ins[2][...].astype(F32) * sa * (1.0 - sa)).astype(BF16)
        outs[2][1] = (acc * ins[3][...].astype(F32) * sb * (1.0 - sb)).astype(BF16)

    blk = pl.BlockSpec((tm, tn), lambda i, j, k: (i, j))
    return _fused("mix_bwd_gates", (t // tm, d // tn, 1),
                  [(dx, pl.BlockSpec((tm, d), lambda i, j, k: (i, 0))),
                   (wo, pl.BlockSpec((tn, d), lambda i, j, k: (j, 0))),
                   (ya, blk), (yb, blk),
                   (proj, pl.BlockSpec((tm, tn), lambda i, j, k: (i, ga0 + j))),
                   (proj, pl.BlockSpec((tm, tn), lambda i, j, k: (i, ga0 + d // tn + j)))],
                  [(_sds((t, d), BF16), blk), (_sds((t, d), BF16), blk),
                   (_sds((2, t, d), BF16), pl.BlockSpec((2, tm, tn), lambda i, j, k: (0, i, j)))],
                  [(0, 1, NT)], epilogue, temp_bytes=8 * tm * tn * 4)


def _tn_matmul(name, a, b, tm, out_dtype=BF16):
    t, m = a.shape
    n = b.shape[1]

    def epilogue(acc, ins, outs):
        outs[0][...] = acc.astype(out_dtype)

    return _fused(name, (m // tm, 1, 1),
                  [(a, pl.BlockSpec((t, tm), lambda i, j, k: (0, i))),
                   (b, pl.BlockSpec((t, n), lambda i, j, k: (0, 0)))],
                  [(_sds((m, n), out_dtype), pl.BlockSpec((tm, n), lambda i, j, k: (i, 0)))],
                  [(0, 1, TN)], epilogue, temp_bytes=2 * tm * n * 4)[0]


def _out_proj_bwd_act(dya, dyb, woc, woa, deps=()):
    t, d = dya.shape
    kdim, nb = woc.shape[1], woc.shape[2]
    tm = min(t, 512)

    def body(dya_ref, dyb_ref, woc_ref, woa_ref, *rest):
        for dy_ref, w_ref, o_ref in ((dya_ref, woc_ref, rest[-2]), (dyb_ref, woa_ref, rest[-1])):
            total = None
            for j in range(N_DEV):
                part = lax.dot_general(dy_ref[:, j * nb:(j + 1) * nb], w_ref[j], NT, preferred_element_type=F32)
                total = part if total is None else total + part
            o_ref[...] = total

    row = pl.BlockSpec((tm, d), lambda i: (i, 0))
    wsp = pl.BlockSpec((N_DEV, kdim, nb), lambda i: (0, 0, 0))
    osp = pl.BlockSpec((tm, kdim), lambda i: (i, 0))
    blocks = 2 * tm * d * 2 + 2 * N_DEV * kdim * nb * 2 + 2 * tm * kdim * 4
    return pl.pallas_call(
        body, name="mix_bwd_dca_do", grid=(t // tm,),
        in_specs=[row, row, wsp, wsp] + [_ANY] * len(deps), out_specs=[osp, osp],
        out_shape=[_sds((t, kdim), F32)] * 2,
        compiler_params=_params(("parallel",), blocks, 4 * tm * kdim * 4),
    )(dya, dyb, woc, woa, *deps)


def _out_proj_bwd_w(ca, o, dya, dyb, nb):
    t, kdim = ca.shape

    def body(ca_ref, o_ref, dya_ref, dyb_ref, dwoc_ref, dwoa_ref):
        dwoc_ref[...] = lax.dot_general(ca_ref[...], dya_ref[...], TN, preferred_element_type=F32).astype(BF16)
        dwoa_ref[...] = lax.dot_general(o_ref[...], dyb_ref[...], TN, preferred_element_type=F32).astype(BF16)

    act = pl.BlockSpec((t, kdim), lambda j: (0, 0))
    col = pl.BlockSpec((t, nb), lambda j: (0, j))
    osp = pl.BlockSpec((None, kdim, nb), lambda j: (j, 0, 0))
    blocks = 2 * t * kdim * 2 + 2 * t * nb * 2 + 2 * kdim * nb * 2
    return pl.pallas_call(
        body, name="mix_bwd_dwoc_dwoa", grid=(N_DEV,),
        in_specs=[act, act, col, col], out_specs=[osp, osp],
        out_shape=[_sds((N_DEV, kdim, nb), BF16)] * 2,
        compiler_params=_params(("parallel",), blocks, 4 * kdim * nb * 4),
    )(ca, o, dya, dyb)


def _proj_bwd_act(dproj, w_in, deps=()):
    t, n = dproj.shape
    d, nb = w_in.shape[2], w_in.shape[3]
    tm = min(t, 512)

    def epilogue(acc, ins, outs):
        outs[0][...] = acc

    def products(ins):
        return (lax.dot_general(ins[0][:, 0:nb], ins[1][0], NT, preferred_element_type=F32)
                + lax.dot_general(ins[0][:, nb:2 * nb], ins[1][1], NT, preferred_element_type=F32))

    return _fused("mix_bwd_dh", (t // tm, 1, 4),
                  [(dproj, pl.BlockSpec((tm, 2 * nb), lambda i, j, k: (i, k))),
                   (w_in, pl.BlockSpec((None, 2, d, nb), lambda i, j, k: (k, 0, 0, 0)))],
                  [(_sds((t, d), F32), pl.BlockSpec((tm, d), lambda i, j, k: (i, 0)))],
                  products, epilogue, nk=4, acc_shape=(tm, d), temp_bytes=tm * d * 4, deps=deps)[0]


def _proj_bwd_w(h, dproj):
    t, d = h.shape
    nb = dproj.shape[1] // N_DEV
    tm = min(d, 512)

    def body(h_ref, dp_ref, o_ref):
        hv = h_ref[...]
        o_ref[0] = lax.dot_general(hv, dp_ref[:, 0:nb], TN, preferred_element_type=F32).astype(BF16)
        o_ref[1] = lax.dot_general(hv, dp_ref[:, nb:2 * nb], TN, preferred_element_type=F32).astype(BF16)

    blocks = t * tm * 2 + t * 2 * nb * 2 + 2 * tm * nb * 2
    return pl.pallas_call(
        body, name="mix_bwd_dwin", grid=(4, d // tm),
        in_specs=[pl.BlockSpec((t, tm), lambda j, i: (0, i)),
                  pl.BlockSpec((t, 2 * nb), lambda j, i: (0, j))],
        out_specs=pl.BlockSpec((None, 2, tm, nb), lambda j, i: (j, 0, i, 0)),
        out_shape=_sds((4, 2, d, nb), BF16),
        compiler_params=_params(("parallel", "parallel"), blocks, 4 * tm * nb * 4),
    )(h, dproj)


def _adamw_math(w, g, m, v):
    m = ADAM_B1 * m + (1.0 - ADAM_B1) * g
    v = ADAM_B2 * v + (1.0 - ADAM_B2) * (g * g)
    m_hat = m / (1.0 - ADAM_B1 ** ADAM_STEP)
    v_hat = v / (1.0 - ADAM_B2 ** ADAM_STEP)
    delta = -ADAM_LR * (m_hat / (jnp.sqrt(v_hat) + ADAM_EPS) + ADAM_WD * w)
    return delta, m, v


def _adamw(name, parts, w, m, v, tr):
    r, c = w.shape

    def body(p_ref, w_ref, m_ref, v_ref, g_out, d_out, m_out, v_out):
        g = p_ref[0].astype(F32)
        for s in range(1, N_DEV):
            g = g + p_ref[s].astype(F32)
        delta, mn, vn = _adamw_math(w_ref[...], g, m_ref[...], v_ref[...])
        g_out[...] = g
        d_out[...] = delta
        m_out[...] = mn
        v_out[...] = vn

    blk = pl.BlockSpec((tr, c), lambda i: (i, 0))
    blocks = N_DEV * tr * c * parts.dtype.itemsize + 7 * tr * c * 4
    return pl.pallas_call(
        body, name=name, grid=(r // tr,),
        in_specs=[pl.BlockSpec((N_DEV, tr, c), lambda i: (0, i, 0)), blk, blk, blk],
        out_specs=[blk] * 4, out_shape=[_sds((r, c), F32)] * 4,
        compiler_params=_params(("parallel",), blocks, 6 * tr * c * 4),
    )(parts, w, m, v)


def _chip_sum(sums_ref):
    g = sums_ref[0].astype(F32)
    for k in range(1, 4):
        g = g + sums_ref[k].astype(F32)
    return g


def _adamw_chips(name, sums, w, m, v, tr, deps=(), row0=0, into=None):
    r, c = w.shape
    rs = sums.shape[1]
    i0 = row0 // tr
    n_pass = len(deps) + (4 if into is not None else 0)

    def body(sums_ref, w_ref, m_ref, v_ref, *rest):
        g_out, d_out, m_out, v_out = rest[n_pass:]
        g = _chip_sum(sums_ref)
        delta, mn, vn = _adamw_math(w_ref[...], g, m_ref[...], v_ref[...])
        g_out[...] = g
        d_out[...] = delta
        m_out[...] = mn
        v_out[...] = vn

    blk = pl.BlockSpec((tr, c), lambda i: (i0 + i, 0))
    blocks = 4 * tr * c * 2 + 7 * tr * c * 4
    passed = list(deps) + (list(into) if into is not None else [])
    aliases = {4 + len(deps) + q: q for q in range(4)} if into is not None else {}
    return pl.pallas_call(
        body, name=name, grid=(rs // tr,),
        in_specs=[pl.BlockSpec((4, tr, c), lambda i: (0, i, 0)), blk, blk, blk] + [_ANY] * n_pass,
        out_specs=[blk] * 4, out_shape=[_sds((r, c), F32)] * 4,
        input_output_aliases=aliases,
        compiler_params=_params(("parallel",), blocks, 6 * tr * c * 4),
    )(sums, w, m, v, *passed)


def _adamw_side(contrib, w, m, v, n_tiles, step_of):
    r, c = w.shape
    tr = r // n_tiles
    assert tr * n_tiles == r and tr % 16 == 0, (r, n_tiles)

    def tile(i, j, k):
        return jnp.minimum(step_of(i, j, k), n_tiles - 1)

    blk = pl.BlockSpec((tr, c), lambda i, j, k: (tile(i, j, k), 0))
    ins = [(contrib, pl.BlockSpec((4, tr, c), lambda i, j, k: (0, tile(i, j, k), 0))), (w, blk), (m, blk), (v, blk)]
    outs = [(_sds((r, c), F32), blk)] * 4

    def fn(in_refs, out_refs):
        @pl.when(step_of(pl.program_id(0), pl.program_id(1), pl.program_id(2)) < n_tiles)
        def _():
            g = _chip_sum(in_refs[0])
            delta, mn, vn = _adamw_math(in_refs[1][...], g, in_refs[2][...], in_refs[3][...])
            out_refs[0][...] = g
            out_refs[1][...] = delta
            out_refs[2][...] = mn
            out_refs[3][...] = vn

    return ins, outs, fn


def _rope_tables(t):
    half = ROT_DIM // 2
    inv_freq = 1.0 / (ROPE_THETA ** (jnp.arange(0, ROT_DIM, 2, dtype=F32) / ROT_DIM))
    ang = jnp.arange(t, dtype=F32)[:, None] * inv_freq[None, :]
    cos, sin = jnp.cos(ang), jnp.sin(ang)
    ones = jnp.ones((t, HEAD_DIM - ROT_DIM), F32)
    zeros = jnp.zeros((t, HEAD_DIM - half), F32)
    c = jnp.concatenate([cos, cos, ones], axis=1)
    sa = jnp.concatenate([-sin, zeros], axis=1)
    sb = jnp.concatenate([jnp.zeros((t, half), F32), sin, jnp.zeros((t, HEAD_DIM - ROT_DIM), F32)], axis=1)
    return tuple(jnp.tile(a, (1, LANES // HEAD_DIM)) for a in (c, sa, sb))


def _pad_rows(a, rows=8):
    return jnp.pad(a, ((0, rows - a.shape[0]), (0, 0)))


def kernel(x, g_ffn1, w_gu1, w_down1, g_mix, w_in, conv_w, q_norm_g, k_norm_g, sinks, w_out_conv, w_out_attn, w_o, g_ffn2, w_gu2, w_down2, loss_target, m_g_ffn1, m_w_gu1, m_w_down1, m_g_mix, m_w_in, m_conv_w, m_q_norm_g, m_k_norm_g, m_sinks, m_w_out_conv, m_w_out_attn, m_w_o, m_g_ffn2, m_w_gu2, m_w_down2, v_g_ffn1, v_w_gu1, v_w_down1, v_g_mix, v_w_in, v_conv_w, v_q_norm_g, v_k_norm_g, v_sinks, v_w_out_conv, v_w_out_attn, v_w_o, v_g_ffn2, v_w_gu2, v_w_down2):
    t, d = x.shape[1], x.shape[2]
    cw = d // 2
    kw = cw // GROUP
    nq = cw // HEAD_DIM
    xs, target = x.reshape(t, d), loss_target.reshape(t, d)
    me = 4 * lax.axis_index("x") + 2 * lax.axis_index("y") + lax.axis_index("c")

    big = {"w_gu1": w_gu1, "w_down1": w_down1, "w_in": w_in, "w_out_conv": w_out_conv,
           "w_out_attn": w_out_attn, "w_o": w_o, "w_gu2": w_gu2, "w_down2": w_down2}
    big_m = {"w_gu1": m_w_gu1, "w_down1": m_w_down1, "w_in": m_w_in, "w_out_conv": m_w_out_conv,
             "w_out_attn": m_w_out_attn, "w_o": m_w_o, "w_gu2": m_w_gu2, "w_down2": m_w_down2}
    big_v = {"w_gu1": v_w_gu1, "w_down1": v_w_down1, "w_in": v_w_in, "w_out_conv": v_w_out_conv,
             "w_out_attn": v_w_out_attn, "w_o": v_w_o, "w_gu2": v_w_gu2, "w_down2": v_w_down2}
    names = list(big)

    tiles = {"w_gu1": 256, "w_gu2": 256, "w_in": 256, "w_down1": 176, "w_down2": 176,
             "w_out_conv": 1024, "w_out_attn": 1024, "w_o": 128}

    def row_tile(n):
        r = big[n].shape[1]
        return tiles[n] if r % tiles[n] == 0 else r

    rs_shape = {n: big[n].shape[1:] for n in names}
    half = rs_shape["w_gu1"][0] // 2
    rs_shape["w_gu1_lo"] = rs_shape["w_gu1_hi"] = (half, rs_shape["w_gu1"][1])

    def add_tile(n):
        r, c = rs_shape[n]
        while r * c * 2 > (3 << 20) and r % 32 == 0:
            r //= 2
        return r

    me_arr = me.astype(jnp.int32).reshape(1)
    sources = [(n, big[n][0], BF16, row_tile(n)) for n in names] + [("conv_w", _pad_rows(conv_w[0]), F32, 8)]
    issue_order = [0, 1, 2, 8, 3, 4, 5, 6, 7]
    first = _place_shard("place_" + names[0], sources[0][1], BF16, me_arr, sources[0][3])
    started = [_gather_start("gather_start_first", [first])]
    early = {2: (big_m["w_in"][0], big_v["w_in"][0])}
    rest = [_place_shard("place_" + sources[i][0], sources[i][1], sources[i][2], me_arr, sources[i][3],
                         deps=(started[0][3],) + early.get(i, ())) for i in issue_order[1:]]
    started.append(_gather_start("gather_start_rest", rest))
    where = {0: (0, 0)}
    where.update({i: (1, p) for p, i in enumerate(issue_order[1:])})

    def fetch(tag, idxs, after, forward=True):
        call = where[idxs[0]][0]
        send, recv, stacks, _ = started[call]
        positions = [where[i][1] for i in idxs]
        got = _gather_wait("gather_wait_" + tag, positions, send, recv, [stacks[p] for p in positions], after)
        return _forward_to_sibling("gather_forward_" + tag, got) if forward else got

    rope_tabs = _rope_tables(t)
    gq = jnp.tile(q_norm_g, (1, nq))
    gk = jnp.tile(k_norm_g, (1, nq // GROUP))
    sink_rows = jnp.broadcast_to(sinks[0][:, None], (nq, LANES))

    wts = {}
    h1 = _rms_fwd("ffn1_norm", xs, g_ffn1)
    wts["w_gu1"], = fetch("gu1", [0], started[1][3])
    gu1, a1 = _ffn_up("ffn1_up", h1, wts["w_gu1"])
    wts["w_down1"], = fetch("down1", [1], a1)
    wd1 = wts["w_down1"].reshape(-1, d)
    x1 = _ffn_down("ffn1_down", a1, wd1, xs)
    h2 = _rms_fwd("mix_norm", x1, g_mix)
    wts["w_in"], conv_land = fetch("in", [2, 8], h2)
    w_in_full = wts["w_in"].reshape(4, 2, d, -1)
    conv_full = jnp.transpose(conv_land, (1, 0, 2)).reshape(8, cw)
    proj = _proj(h2, w_in_full)
    ca = _conv_fwd(proj, conv_full)
    qn, kn, vb = _qk_prep(proj, gq, gk, rope_tabs, cw, kw)
    o = _attn_fwd(qn, kn, vb, sink_rows)
    wts["w_out_conv"], wts["w_out_attn"] = fetch("out", [3, 4], o)
    merged, ya, yb = _mix_out(ca, o, wts["w_out_conv"], wts["w_out_attn"], proj)
    wts["w_o"], = fetch("o", [5], merged)
    wo = wts["w_o"].reshape(d, d)
    x2 = _mix_residual(merged, wo, x1)
    h3 = _rms_fwd("ffn2_norm", x2, g_ffn2)
    mine = lax.axis_index("c").astype(jnp.int32).reshape(1)
    got = fetch("gu2", [6], h3, forward=False)
    fsend, frecv, got = _forward_start("gather_forward_start_gu2", got)
    part = _ffn_up("ffn2_up_mine", h3, got[0], parity=mine)
    wts["w_gu2"], = _forward_wait("gather_forward_wait_gu2", fsend, frecv, got, part[1])
    gu2, a2 = _ffn_up("ffn2_up_sibling", h3, wts["w_gu2"], parity=1 - mine, into=part)
    wts["w_down2"], = fetch("down2", [7], a2)
    wd2 = wts["w_down2"].reshape(-1, d)
    dy, sq, dy_bf = _ffn_down("ffn2_down", a2, wd2, x2, target=target)
    loss = lax.psum(sq[0, 0] * (0.5 / d), ("x", "y", "c"))

    place = jnp.stack([lax.axis_index("c"), 2 * lax.axis_index("x") + lax.axis_index("y")]).astype(jnp.int32)
    def pair_start(tag, group, grads, deps=()):
        stacks = [grads[n].reshape((4, 2) + rs_shape[n]) for n in group]
        lands = [lax.empty((4,) + rs_shape[n], BF16) for n in group]
        return _pair_start("rs_pair_start_" + tag, stacks, lands, deps)

    def chip_start(tag, group, pending, after):
        send, recv, stacks, lands, _ = pending
        stacks, lands = _pair_wait("rs_pair_wait_" + tag, send, recv, stacks, lands, after)
        added = [_pair_add("rs_pair_add_" + n, st, ld, place, add_tile(n)) for n, st, ld in zip(group, stacks, lands)]
        return _chip_start("rs_chip_start_" + tag, [a[0] for a in added], [a[1] for a in added])

    group_a, group_b, group_c = ["w_down2", "w_gu2"], ["w_o", "w_out_conv", "w_out_attn"], ["w_in"]
    group_d, group_e, group_f = ["w_down1"], ["w_gu1_lo"], ["w_gu1_hi"]
    g = {}
    dgu2, a2 = _ffn_bwd_act("ffn2_bwd_act", dy_bf, wd2, gu2)
    g["w_down2"], = _ffn_bwd_dwd("ffn2_bwd_dwd", a2, dy_bf)
    g["w_gu2"], = _ffn_bwd_dwgu("ffn2_bwd_dwgu", h3, dgu2)
    pend_a = pair_start("a", group_a, g)
    dh3, = _ffn_bwd_dh("ffn2_bwd_dh", dgu2, wts["w_gu2"], deps=(pend_a[4],))
    ring_a = chip_start("a", group_a, pend_a, dh3)
    dx2, dg_ffn2, dx2_bf = _rms_bwd("ffn2_bwd_rms", x2, g_ffn2, dh3, dy, deps=(ring_a[4],), with_bf16=True)

    dya, dyb, dgates = _mix_bwd_gates(dx2_bf, wo, ya, yb, proj, cw)
    g["w_o"] = _tn_matmul("mix_bwd_dwo", merged, dx2_bf, min(d, 512))
    g["w_out_conv"], g["w_out_attn"] = _out_proj_bwd_w(ca, o, dya, dyb, d // N_DEV)
    pend_b = pair_start("b", group_b, g)
    dca, do = _out_proj_bwd_act(dya, dyb, wts["w_out_conv"], wts["w_out_attn"], deps=(pend_b[4],))
    ring_b = chip_start("b", group_b, pend_b, do)
    d3, dconv_w = _conv_bwd(proj, conv_full, dca, deps=(ring_b[4],))
    dq, dkc, dkp, dvc, dvp, dsink = _attn_bwd(qn, kn, vb, sink_rows, do)
    dqkv, dgq, dgk = _qk_prep_bwd(proj, gq, gk, rope_tabs, dq, dkc, dkp, dvc, dvp, cw, kw)
    dproj = jnp.concatenate([d3[0], d3[1], d3[2], dqkv, dgates[0], dgates[1]], axis=1)
    g["w_in"] = _proj_bwd_w(h2, dproj)
    pend_c = pair_start("c", group_c, g)
    dh2 = _proj_bwd_act(dproj, w_in_full, deps=(pend_c[4],))
    ring_c = chip_start("c", group_c, pend_c, dh2)
    dx1, dg_mix, dx1_bf = _rms_bwd("mix_bwd_rms", x1, g_mix, dh2, dx2, deps=(ring_c[4],), with_bf16=True)

    big_out = {}
    arrived = {}

    def wait_group(tag, group, ring, after):
        send, recv, parts, lands2, _ = ring
        parts, lands2 = _chip_wait("rs_chip_wait_" + tag, send, recv, parts, lands2, after)
        arrived.update(dict(zip(group, lands2)))

    def update(n, after):
        res = _adamw_chips("adamw_" + n, arrived[n], big[n][0], big_m[n][0], big_v[n][0], row_tile(n), deps=(after,))
        big_out[n] = [a[None] for a in res]
        return res[0]

    def update_beside(n, n_tiles, step_of):
        return _adamw_side(arrived[n], big[n][0], big_m[n][0], big_v[n][0], n_tiles, step_of)

    def keep(n, res):
        big_out[n] = [a[None] for a in res]

    dgu1, a1 = _ffn_bwd_act("ffn1_bwd_act", dx1_bf, wd1, gu1)
    wait_group("a", group_a, ring_a, a1)
    g["w_down1"], *res = _ffn_bwd_dwd("ffn1_bwd_dwd", a1, dx1_bf,
                                       side=update_beside("w_down2", 11, lambda i, j, k: i * 4 + j))
    keep("w_down2", res)
    pend_d = pair_start("d", group_d, g)
    g["w_gu1_lo"], *res = _ffn_bwd_dwgu("ffn1_bwd_dwgu_lo", h1, dgu1, deps=(pend_d[4],), rows=(0, half),
                                         side=update_beside("w_gu2", 16, lambda i, j, k: i * 2 + j))
    keep("w_gu2", res)
    ring_d = chip_start("d", group_d, pend_d, g["w_gu1_lo"])
    pend_e = pair_start("e", group_e, g, deps=(ring_d[4],))
    wait_group("c", group_c, ring_c, pend_e[4])
    g["w_gu1_hi"], *res = _ffn_bwd_dwgu("ffn1_bwd_dwgu_hi", h1, dgu1, rows=(half, half),
                                         side=update_beside("w_in", 16, lambda i, j, k: i * 2 + j))
    keep("w_in", res)
    ring_e = chip_start("e", group_e, pend_e, g["w_gu1_hi"])
    pend_f = pair_start("f", group_f, g, deps=(ring_e[4],))
    wait_group("b", group_b, ring_b, pend_f[4])
    after = pend_f[4]
    for n in group_b:
        after = update(n, after)
    ring_f = chip_start("f", group_f, pend_f, after)
    wait_group("d", group_d, ring_d, ring_f[4])
    dh1, *res = _ffn_bwd_dh("ffn1_bwd_dh", dgu1, wts["w_gu1"],
                             side=update_beside("w_down1", 11, lambda i, j, k: i * 4 + k))
    keep("w_down1", res)
    grad_x, dg_ffn1 = _rms_bwd("ffn1_bwd_rms", xs, g_ffn1, dh1, dx1)
    after = grad_x
    n = "w_gu1"
    wait_group("e", group_e, ring_e, after)
    res = _adamw_chips("adamw_w_gu1_lo", arrived["w_gu1_lo"], big[n][0], big_m[n][0], big_v[n][0], row_tile(n), deps=(after,))
    wait_group("f", group_f, ring_f, res[0])
    res = _adamw_chips("adamw_w_gu1_hi", arrived["w_gu1_hi"], big[n][0], big_m[n][0], big_v[n][0], row_tile(n),
                       row0=half, into=res)
    keep(n, res)
    after = res[0]

    small = {"g_ffn1": dg_ffn1[0:1], "g_mix": dg_mix[0:1], "g_ffn2": dg_ffn2[0:1],
             "q_norm_g": dgq[0:1, :HEAD_DIM], "k_norm_g": dgk[0:1, :HEAD_DIM], "sinks": dsink[:, 0][None],
             "conv_w": dconv_w[0:CONV_K].reshape(1, -1)}
    small_w = {"g_ffn1": g_ffn1, "g_mix": g_mix, "g_ffn2": g_ffn2, "q_norm_g": q_norm_g, "k_norm_g": k_norm_g,
               "sinks": sinks, "conv_w": None}
    small_m = {"g_ffn1": m_g_ffn1, "g_mix": m_g_mix, "g_ffn2": m_g_ffn2, "q_norm_g": m_q_norm_g,
               "k_norm_g": m_k_norm_g, "sinks": m_sinks, "conv_w": m_conv_w}
    small_v = {"g_ffn1": v_g_ffn1, "g_mix": v_g_mix, "g_ffn2": v_g_ffn2, "q_norm_g": v_q_norm_g,
               "k_norm_g": v_k_norm_g, "sinks": v_sinks, "conv_w": v_conv_w}
    snames = list(small)
    widths = [small[n].shape[1] for n in snames]
    total = sum(widths)
    rows = -(-total // LANES)
    rows = -(-rows // 8) * 8

    def pack(vals):
        flat = jnp.concatenate([v.reshape(1, -1) for v in vals], axis=1)
        return jnp.pad(flat, ((0, 0), (0, rows * LANES - total))).reshape(rows, LANES)

    csh = cw // N_DEV

    def place_conv(local, fill):
        full = jnp.full((CONV_K, cw), fill, F32)
        return lax.dynamic_update_slice(full, local, (0, me * csh)).reshape(1, -1)

    pw = pack([small_w[n] if n != "conv_w" else place_conv(conv_w[0], 0.0) for n in snames])
    pm = pack([small_m[n] if n != "conv_w" else place_conv(m_conv_w[0], 0.0) for n in snames])
    pv = pack([small_v[n] if n != "conv_w" else place_conv(v_conv_w[0], 1.0) for n in snames])
    parts = _all_gather_small("gather_small_grads", pack([small[n] for n in snames]), deps=(after,))
    sg, sd, sm, sv = [a.reshape(1, -1) for a in _adamw("adamw_small", parts, pw, pm, pv, rows)]

    def unpack(flat, n):
        off = sum(widths[:snames.index(n)])
        piece = flat[:, off:off + widths[snames.index(n)]]
        if n == "conv_w":
            piece = lax.dynamic_slice(piece.reshape(CONV_K, cw), (0, me * csh), (CONV_K, csh))[None]
        return piece

    order = ["g_ffn1", "w_gu1", "w_down1", "g_mix", "w_in", "conv_w", "q_norm_g", "k_norm_g", "sinks",
             "w_out_conv", "w_out_attn", "w_o", "g_ffn2", "w_gu2", "w_down2"]
    outs = [loss, grad_x[None]]
    for idx, flat in enumerate((sg, sd, sm, sv)):
        for n in order:
            outs.append(big_out[n][idx] if n in big_out else unpack(flat, n))
    return tuple(outs)
```

```python
import jax
import jax.numpy as jnp
from jax import lax
from jax.experimental import pallas as pl
from jax.experimental.pallas import tpu as pltpu

F32 = jnp.float32
BF16 = jnp.bfloat16

N_DEV = 8
HEAD_DIM = 64
GROUP = 4
BLOCK = 128
ROT_DIM = 16
ROPE_THETA = 500000.0
RMS_EPS = 1e-6
NEG_INF = -1e30
ATTN_SCALE = HEAD_DIM ** -0.5
CONV_K = 3
LANES = 128
MXU_COLS = 256
VMEM_BYTES_V7X = 64 * 1024 * 1024
VMEM_CAP = VMEM_BYTES_V7X - 6 * 1024 * 1024

ADAM_LR = 0.001
ADAM_B1 = 0.9
ADAM_B2 = 0.999
ADAM_EPS = 1e-08
ADAM_WD = 0.01
ADAM_STEP = 10

NN = (((1,), (0,)), ((), ()))
NT = (((1,), (1,)), ((), ()))
TN = (((0,), (0,)), ((), ()))

MESH = pl.DeviceIdType.MESH


def _nbytes(shape, dtype):
    n = 1
    for s in shape:
        if s is not None:
            n *= s
    return n * jnp.dtype(dtype).itemsize


def _params(semantics, block_bytes, temp_bytes):
    assert 2 * block_bytes + temp_bytes <= VMEM_CAP, (block_bytes, temp_bytes)
    return pltpu.CompilerParams(dimension_semantics=semantics, vmem_limit_bytes=VMEM_CAP)


def _fused(name, grid, ins, outs, dots, epilogue, *, nk=1, acc_shape=None, temp_bytes=0,
           semantics=("parallel", "parallel", "arbitrary"), deps=(), side=None):
    n_main_in, n_main_out = len(ins), len(outs)
    if side is not None:
        ins, outs = list(ins) + list(side[0]), list(outs) + list(side[1])
    n_in, n_out = len(ins), len(outs)
    n_dep = len(deps)

    def body(*refs):
        in_refs, out_refs = refs[:n_in], refs[n_in + n_dep:n_in + n_dep + n_out]
        scratch = refs[n_in + n_dep + n_out:]
        if side is not None:
            side[2](in_refs[n_main_in:], out_refs[n_main_out:])

        def products():
            if callable(dots):
                return dots(in_refs)
            total = None
            for ai, bi, contract in dots:
                a, b = in_refs[ai][...], in_refs[bi][...]
                a = a if a.dtype == BF16 else a.astype(BF16)
                b = b if b.dtype == BF16 else b.astype(BF16)
                p = lax.dot_general(a, b, contract, preferred_element_type=F32)
                total = p if total is None else total + p
            return total

        if nk == 1:
            epilogue(products() if dots else None, in_refs, out_refs)
        else:
            acc = scratch[0]
            k = pl.program_id(2)

            @pl.when(k == 0)
            def _():
                acc[...] = jnp.zeros_like(acc)

            acc[...] += products()

            @pl.when(k == nk - 1)
            def _():
                epilogue(acc[...], in_refs, out_refs)

    block_bytes = sum(_nbytes(spec.block_shape, a.dtype) for a, spec in ins)
    block_bytes += sum(_nbytes(spec.block_shape, s.dtype) for s, spec in outs)
    scratch_shapes = []
    if nk > 1:
        scratch_shapes.append(pltpu.VMEM(acc_shape, F32))
        temp_bytes += _nbytes(acc_shape, F32)
    res = pl.pallas_call(
        body, name=name, grid=grid,
        in_specs=[spec for _, spec in ins] + [pl.BlockSpec(memory_space=pl.ANY)] * n_dep,
        out_specs=[spec for _, spec in outs],
        out_shape=[s for s, _ in outs],
        scratch_shapes=scratch_shapes,
        compiler_params=_params(semantics, block_bytes, temp_bytes),
    )(*[a for a, _ in ins], *deps)
    return res


def _sds(shape, dtype):
    return jax.ShapeDtypeStruct(shape, dtype)


def _sigmoid(x):
    return jax.nn.sigmoid(x)


def _all_gather_small(name, shard, deps=()):
    n_dep = len(deps)

    def body(src, *rest):
        dst, send_sems, recv_sems, local_sem = rest[n_dep:]
        x, y, c = lax.axis_index("x"), lax.axis_index("y"), lax.axis_index("c")
        me = 4 * x + 2 * y + c
        copies = [pltpu.make_async_copy(src, dst.at[me], local_sem)]
        for k in range(1, N_DEV):
            peer = ((1 - x) if (k & 4) else x, (1 - y) if (k & 2) else y, (1 - c) if (k & 1) else c)
            copies.append(pltpu.make_async_remote_copy(
                src_ref=src, dst_ref=dst.at[me], send_sem=send_sems.at[k - 1], recv_sem=recv_sems.at[k - 1],
                device_id=peer, device_id_type=MESH))
        for cp in copies:
            cp.start()
        for cp in copies:
            cp.wait()

    hbm = pl.BlockSpec(memory_space=pltpu.HBM)
    return pl.pallas_call(
        body, name=name,
        in_specs=[hbm] + [pl.BlockSpec(memory_space=pl.ANY)] * n_dep, out_specs=hbm,
        out_shape=_sds((N_DEV,) + shard.shape, shard.dtype),
        scratch_shapes=[pltpu.SemaphoreType.DMA((N_DEV - 1,)), pltpu.SemaphoreType.DMA((N_DEV - 1,)),
                        pltpu.SemaphoreType.DMA],
    )(shard, *deps)


_HBM = pl.BlockSpec(memory_space=pltpu.HBM)
_SEM = pl.BlockSpec(memory_space=pltpu.SEMAPHORE)
_ANY = pl.BlockSpec(memory_space=pl.ANY)
_EFFECT = pltpu.SideEffectType.DATAFLOW_SIDE_EFFECTING
N_TARGETS = 4


def _mesh_pos():
    return lax.axis_index("x"), lax.axis_index("y"), lax.axis_index("c")


def _chip_peers(x, y, c):
    return [(1 - x, y, c), (x, 1 - y, c), (1 - x, 1 - y, c)]


def _dev_index(pos):
    return 4 * pos[0] + 2 * pos[1] + pos[2]


def _hbm_like(a):
    return pltpu.HBM(a.shape, a.dtype)


def _place_shard(name, w, out_dtype, me, tr, deps=()):
    r, c = w.shape
    n_dep = len(deps)

    def body(me_ref, w_ref, *rest):
        rest[n_dep][...] = w_ref[...].astype(out_dtype)

    grid_spec = pltpu.PrefetchScalarGridSpec(
        num_scalar_prefetch=1, grid=(r // tr,),
        in_specs=[pl.BlockSpec((tr, c), lambda i, me_ref: (i, 0))] + [_ANY] * n_dep,
        out_specs=pl.BlockSpec((None, tr, c), lambda i, me_ref: (me_ref[0], i, 0)))
    return pl.pallas_call(
        body, name=name, grid_spec=grid_spec, out_shape=_sds((N_DEV, r, c), out_dtype),
        compiler_params=_params(("parallel",), tr * c * 6, tr * c * 4),
    )(me, w, *deps)


def _gather_start(name, lands):
    n = len(lands)

    def body(*refs):
        bufs = refs[:n]
        send, recv = refs[n], refs[n + 1]
        token = refs[-1]
        x, y, c = _mesh_pos()
        me = _dev_index((x, y, c))
        targets = [(x, y, 1 - c)] + _chip_peers(x, y, c)
        for w in range(n):
            for k, to in enumerate(targets):
                pltpu.make_async_remote_copy(
                    src_ref=bufs[w].at[me], dst_ref=bufs[w].at[me],
                    send_sem=send.at[N_TARGETS * w + k], recv_sem=recv.at[N_TARGETS * w + k],
                    device_id=to, device_id_type=MESH).start()
        token[...] = jnp.zeros_like(token)

    sems = pltpu.SemaphoreType.DMA((N_TARGETS * n,))
    outs = pl.pallas_call(
        body, name=name,
        in_specs=[_HBM] * n, out_specs=[_SEM, _SEM] + [_HBM] * n + [_token_spec()],
        out_shape=[sems, sems] + [_hbm_like(a) for a in lands] + [_sds((8, LANES), F32)],
        input_output_aliases={i: 2 + i for i in range(n)},
        compiler_params=pltpu.CompilerParams(has_side_effects=_EFFECT),
    )(*lands)
    return outs[0], outs[1], list(outs[2:2 + n]), outs[-1]


def _gather_wait(name, positions, send, recv, lands, after):
    m = len(positions)

    def body(*refs):
        bufs = refs[:m]
        send_sems, recv_sems = refs[m], refs[m + 1]
        x, y, c = _mesh_pos()
        me = _dev_index((x, y, c))
        sources = [(x, y, 1 - c)] + _chip_peers(x, y, c)
        for j, w in enumerate(positions):
            for k, frm in enumerate(sources):
                cp = pltpu.make_async_remote_copy(
                    src_ref=bufs[j].at[me], dst_ref=bufs[j].at[_dev_index(frm)],
                    send_sem=send_sems.at[N_TARGETS * w + k], recv_sem=recv_sems.at[N_TARGETS * w + k],
                    device_id=frm, device_id_type=MESH)
                cp.wait_send()
                cp.wait_recv()

    outs = pl.pallas_call(
        body, name=name,
        in_specs=[_HBM] * m + [_SEM, _SEM, _ANY], out_specs=[_HBM] * m,
        out_shape=[_hbm_like(a) for a in lands],
        input_output_aliases={i: i for i in range(m)},
        compiler_params=pltpu.CompilerParams(has_side_effects=_EFFECT),
    )(*lands, send, recv, after)
    return list(outs)


def _forward_to_sibling(name, lands):
    m = len(lands)

    def body(*refs):
        bufs = refs[m:2 * m]
        send_sems, recv_sems = refs[2 * m], refs[2 * m + 1]
        x, y, c = _mesh_pos()
        copies = []
        for j in range(m):
            for k, chip in enumerate(_chip_peers(x, y, c)):
                block = bufs[j].at[_dev_index(chip)]
                cp = pltpu.make_async_remote_copy(
                    src_ref=block, dst_ref=block,
                    send_sem=send_sems.at[3 * j + k], recv_sem=recv_sems.at[3 * j + k],
                    device_id=(x, y, 1 - c), device_id_type=MESH)
                cp.start()
                copies.append(cp)
        for cp in copies:
            cp.wait()

    outs = pl.pallas_call(
        body, name=name,
        in_specs=[_HBM] * m, out_specs=[_HBM] * m,
        out_shape=[_sds(a.shape, a.dtype) for a in lands],
        input_output_aliases={i: i for i in range(m)},
        scratch_shapes=[pltpu.SemaphoreType.DMA((3 * m,)), pltpu.SemaphoreType.DMA((3 * m,))],
    )(*lands)
    return list(outs)


def _forward_copies(bufs, send, recv):
    x, y, c = _mesh_pos()
    copies = []
    for j, buf in enumerate(bufs):
        for k, chip in enumerate(_chip_peers(x, y, c)):
            block = buf.at[_dev_index(chip)]
            copies.append(pltpu.make_async_remote_copy(
                src_ref=block, dst_ref=block, send_sem=send.at[3 * j + k], recv_sem=recv.at[3 * j + k],
                device_id=(x, y, 1 - c), device_id_type=MESH))
    return copies


def _forward_start(name, lands):
    m = len(lands)

    def body(*refs):
        for cp in _forward_copies(refs[:m], refs[m], refs[m + 1]):
            cp.start()

    sems = pltpu.SemaphoreType.DMA((3 * m,))
    outs = pl.pallas_call(
        body, name=name,
        in_specs=[_HBM] * m, out_specs=[_SEM, _SEM] + [_HBM] * m,
        out_shape=[sems, sems] + [_hbm_like(a) for a in lands],
        input_output_aliases={i: 2 + i for i in range(m)},
        compiler_params=pltpu.CompilerParams(has_side_effects=_EFFECT),
    )(*lands)
    return outs[0], outs[1], list(outs[2:])


def _forward_wait(name, send, recv, lands, after):
    m = len(lands)

    def body(*refs):
        for cp in _forward_copies(refs[:m], refs[m], refs[m + 1]):
            cp.wait_send()
            cp.wait_recv()

    outs = pl.pallas_call(
        body, name=name,
        in_specs=[_HBM] * m + [_SEM, _SEM, _ANY], out_specs=[_HBM] * m,
        out_shape=[_hbm_like(a) for a in lands],
        input_output_aliases={i: i for i in range(m)},
        compiler_params=pltpu.CompilerParams(has_side_effects=_EFFECT),
    )(*lands, send, recv, after)
    return list(outs)


def _token_spec():
    return pl.BlockSpec(memory_space=pltpu.VMEM)


def _pair_start(name, stacks, lands, deps=()):
    n = len(stacks)
    n_dep = len(deps)

    def body(*refs):
        srcs, dsts = refs[:n], refs[n:2 * n]
        send, recv = refs[2 * n + n_dep], refs[2 * n + n_dep + 1]
        token = refs[-1]
        x, y, c = _mesh_pos()
        for w in range(n):
            for chip in range(4):
                pltpu.make_async_remote_copy(
                    src_ref=srcs[w].at[chip, 1 - c], dst_ref=dsts[w].at[chip],
                    send_sem=send.at[4 * w + chip], recv_sem=recv.at[4 * w + chip],
                    device_id=(x, y, 1 - c), device_id_type=MESH).start()
        token[...] = jnp.zeros_like(token)

    sems = pltpu.SemaphoreType.DMA((4 * n,))
    outs = pl.pallas_call(
        body, name=name,
        in_specs=[_HBM] * (2 * n) + [_ANY] * n_dep, out_specs=[_SEM, _SEM] + [_HBM] * (2 * n) + [_token_spec()],
        out_shape=[sems, sems] + [_hbm_like(a) for a in stacks] + [_hbm_like(a) for a in lands] + [_sds((8, LANES), F32)],
        input_output_aliases={i: 2 + i for i in range(2 * n)},
        compiler_params=pltpu.CompilerParams(has_side_effects=_EFFECT),
    )(*stacks, *lands, *deps)
    return outs[0], outs[1], list(outs[2:2 + n]), list(outs[2 + n:2 + 2 * n]), outs[-1]


def _pair_wait(name, send, recv, stacks, lands, after):
    n = len(stacks)

    def body(*refs):
        srcs, dsts = refs[:n], refs[n:2 * n]
        send_sems, recv_sems = refs[2 * n], refs[2 * n + 1]
        x, y, c = _mesh_pos()
        for w in range(n):
            for chip in range(4):
                cp = pltpu.make_async_remote_copy(
                    src_ref=srcs[w].at[chip, 1 - c], dst_ref=dsts[w].at[chip],
                    send_sem=send_sems.at[4 * w + chip], recv_sem=recv_sems.at[4 * w + chip],
                    device_id=(x, y, 1 - c), device_id_type=MESH)
                cp.wait_send()
                cp.wait_recv()

    outs = pl.pallas_call(
        body, name=name,
        in_specs=[_HBM] * (2 * n) + [_SEM, _SEM, _ANY], out_specs=[_HBM] * (2 * n),
        out_shape=[_hbm_like(a) for a in stacks] + [_hbm_like(a) for a in lands],
        input_output_aliases={i: i for i in range(2 * n)},
        compiler_params=pltpu.CompilerParams(has_side_effects=_EFFECT),
    )(*stacks, *lands, send, recv, after)
    return list(outs[:n]), list(outs[n:])


def _pair_add(name, stack, land, place, tr):
    _, _, r, c = stack.shape

    def body(place_ref, a_ref, b_ref, sums_ref, slots_ref):
        total = (a_ref[...].astype(F32) + b_ref[...].astype(F32)).astype(BF16)
        sums_ref[...] = total

        @pl.when(pl.program_id(1) == place_ref[1])
        def _():
            slots_ref[...] = total

    grid_spec = pltpu.PrefetchScalarGridSpec(
        num_scalar_prefetch=1, grid=(r // tr, 4),
        in_specs=[pl.BlockSpec((None, None, tr, c), lambda i, k, place_ref: (k, place_ref[0], i, 0)),
                  pl.BlockSpec((None, tr, c), lambda i, k, place_ref: (k, i, 0))],
        out_specs=[pl.BlockSpec((None, tr, c), lambda i, k, place_ref: (k, i, 0)),
                   pl.BlockSpec((None, tr, c), lambda i, k, place_ref: (place_ref[1], i, 0))])
    return pl.pallas_call(
        body, name=name, grid_spec=grid_spec, out_shape=[_sds((4, r, c), BF16)] * 2,
        compiler_params=_params(("parallel", "arbitrary"), 4 * tr * c * 2, 3 * tr * c * 4),
    )(place, stack, land)


def _chip_start(name, parts, lands):
    n = len(parts)

    def body(*refs):
        srcs, dsts = refs[:n], refs[n:2 * n]
        send, recv = refs[2 * n], refs[2 * n + 1]
        token = refs[-1]
        x, y, c = _mesh_pos()
        for w in range(n):
            for k, to in enumerate(_chip_peers(x, y, c)):
                pltpu.make_async_remote_copy(
                    src_ref=srcs[w].at[2 * to[0] + to[1]], dst_ref=dsts[w].at[2 * x + y],
                    send_sem=send.at[3 * w + k], recv_sem=recv.at[3 * w + k],
                    device_id=to, device_id_type=MESH).start()
        token[...] = jnp.zeros_like(token)

    sems = pltpu.SemaphoreType.DMA((3 * n,))
    outs = pl.pallas_call(
        body, name=name,
        in_specs=[_HBM] * (2 * n), out_specs=[_SEM, _SEM] + [_HBM] * (2 * n) + [_token_spec()],
        out_shape=[sems, sems] + [_hbm_like(a) for a in parts] + [_hbm_like(a) for a in lands] + [_sds((8, LANES), F32)],
        input_output_aliases={i: 2 + i for i in range(2 * n)},
        compiler_params=pltpu.CompilerParams(has_side_effects=_EFFECT),
    )(*parts, *lands)
    return outs[0], outs[1], list(outs[2:2 + n]), list(outs[2 + n:2 + 2 * n]), outs[-1]


def _chip_wait(name, send, recv, parts, lands, after):
    n = len(parts)

    def body(*refs):
        srcs, dsts = refs[:n], refs[n:2 * n]
        send_sems, recv_sems = refs[2 * n], refs[2 * n + 1]
        x, y, c = _mesh_pos()
        for w in range(n):
            for k, frm in enumerate(_chip_peers(x, y, c)):
                chip = 2 * frm[0] + frm[1]
                cp = pltpu.make_async_remote_copy(
                    src_ref=srcs[w].at[chip], dst_ref=dsts[w].at[chip],
                    send_sem=send_sems.at[3 * w + k], recv_sem=recv_sems.at[3 * w + k],
                    device_id=frm, device_id_type=MESH)
                cp.wait_send()
                cp.wait_recv()

    outs = pl.pallas_call(
        body, name=name,
        in_specs=[_HBM] * (2 * n) + [_SEM, _SEM, _ANY], out_specs=[_HBM] * (2 * n),
        out_shape=[_hbm_like(a) for a in parts] + [_hbm_like(a) for a in lands],
        input_output_aliases={i: i for i in range(2 * n)},
        compiler_params=pltpu.CompilerParams(has_side_effects=_EFFECT),
    )(*parts, *lands, send, recv, after)
    return list(outs[:n]), list(outs[n:])


def _row_tile(t):
    return min(t, 256)


def _rms_fwd(name, x, g):
    t, d = x.shape
    tm = _row_tile(t)

    def epilogue(_, ins, outs):
        xv = ins[0][...]
        r = lax.rsqrt(jnp.mean(xv * xv, axis=-1, keepdims=True) + RMS_EPS)
        outs[0][...] = (xv * r * ins[1][...]).astype(BF16)

    row = pl.BlockSpec((tm, d), lambda i, j, k: (i, 0))
    vec = pl.BlockSpec((1, d), lambda i, j, k: (0, 0))
    return _fused(name, (t // tm, 1, 1), [(x, row), (g, vec)], [(_sds((t, d), BF16), row)], [], epilogue,
                  temp_bytes=4 * tm * d * 4)[0]


def _rms_bwd(name, x, g, dh, resid, deps=(), with_bf16=False):
    t, d = x.shape
    tm = _row_tile(t)

    def epilogue(_, ins, outs):
        xv, gv, dhv = ins[0][...], ins[1][...], ins[2][...]
        r = lax.rsqrt(jnp.mean(xv * xv, axis=-1, keepdims=True) + RMS_EPS)
        xh = xv * r
        u = dhv * gv
        dot = jnp.mean(u * xh, axis=-1, keepdims=True)
        dx = ins[3][...] + r * (u - xh * dot)
        outs[0][...] = dx
        if with_bf16:
            outs[2][...] = dx.astype(BF16)

        @pl.when(pl.program_id(0) == 0)
        def _():
            outs[1][...] = jnp.zeros_like(outs[1])

        outs[1][0:1, :] += jnp.sum(dhv * xh, axis=0, keepdims=True)

    row = pl.BlockSpec((tm, d), lambda i, j, k: (i, 0))
    vec = pl.BlockSpec((1, d), lambda i, j, k: (0, 0))
    acc = pl.BlockSpec((8, d), lambda i, j, k: (0, 0))
    outs = [(_sds((t, d), F32), row), (_sds((8, d), F32), acc)] + ([(_sds((t, d), BF16), row)] if with_bf16 else [])
    return _fused(name, (t // tm, 1, 1), [(x, row), (g, vec), (dh, row), (resid, row)], outs, [], epilogue,
                  temp_bytes=6 * tm * d * 4, semantics=("arbitrary", "arbitrary", "arbitrary"), deps=deps)


def _ffn_up(name, h, wgu, parity=None, into=None, deps=()):
    t, d = h.shape
    nb = wgu.shape[2]
    f = 4 * nb
    tm = min(t, 512)

    def body(h_ref, wg_ref, wu_ref, gu_ref, a_ref):
        hv = h_ref[...]
        for c0 in range(0, nb, MXU_COLS):
            cs = slice(c0, min(c0 + MXU_COLS, nb))
            g = jnp.dot(hv, wg_ref[:, cs], preferred_element_type=F32)
            u = jnp.dot(hv, wu_ref[:, cs], preferred_element_type=F32)
            gu_ref[0, :, cs] = g.astype(BF16)
            gu_ref[1, :, cs] = u.astype(BF16)
            a_ref[:, cs] = (g * _sigmoid(g) * u).astype(BF16)

    blocks = tm * d * 2 + 2 * d * nb * 2 + 3 * tm * nb * 2
    params = _params(("parallel", "parallel"), blocks, 8 * tm * MXU_COLS * 4)
    out_shape = [_sds((2, t, f), BF16), _sds((t, f), BF16)]
    if parity is None:
        return pl.pallas_call(
            body, name=name, grid=(4, t // tm),
            in_specs=[pl.BlockSpec((tm, d), lambda j, i: (i, 0)),
                      pl.BlockSpec((None, d, nb), lambda j, i: (j, 0, 0)),
                      pl.BlockSpec((None, d, nb), lambda j, i: (j + 4, 0, 0))],
            out_specs=[pl.BlockSpec((2, tm, nb), lambda j, i: (0, i, j)),
                       pl.BlockSpec((tm, nb), lambda j, i: (i, j))],
            out_shape=out_shape, compiler_params=params,
        )(h, wgu, wgu)

    def half_body(parity_ref, h_ref, wg_ref, wu_ref, *rest):
        body(h_ref, wg_ref, wu_ref, rest[-2], rest[-1])

    n_pass = len(deps) + (0 if into is None else 2)
    grid_spec = pltpu.PrefetchScalarGridSpec(
        num_scalar_prefetch=1, grid=(2, t // tm),
        in_specs=[pl.BlockSpec((tm, d), lambda jj, i, p: (i, 0)),
                  pl.BlockSpec((None, d, nb), lambda jj, i, p: (2 * jj + p[0], 0, 0)),
                  pl.BlockSpec((None, d, nb), lambda jj, i, p: (2 * jj + p[0] + 4, 0, 0))] + [_ANY] * n_pass,
        out_specs=[pl.BlockSpec((2, tm, nb), lambda jj, i, p: (0, i, 2 * jj + p[0])),
                   pl.BlockSpec((tm, nb), lambda jj, i, p: (i, 2 * jj + p[0]))])
    return pl.pallas_call(
        half_body, name=name, grid_spec=grid_spec, out_shape=out_shape,
        input_output_aliases={} if into is None else {4: 0, 5: 1}, compiler_params=params,
    )(parity, h, wgu, wgu, *(into or ()), *deps)


def _ffn_down(name, a, wd, x, target=None):
    t, f = a.shape
    d = wd.shape[1]
    tm = min(t, 512)
    tn = min(d, 1024)
    blk = pl.BlockSpec((tm, tn), lambda j, i, k: (i, j))
    ins = [(a, pl.BlockSpec((tm, f), lambda j, i, k: (i, 0))), (wd, pl.BlockSpec((f, tn), lambda j, i, k: (0, j))), (x, blk)]

    if target is None:
        def epilogue(acc, ins, outs):
            outs[0][...] = ins[2][...] + 0.5 * acc

        return _fused(name, (d // tn, t // tm, 1), ins, [(_sds((t, d), F32), blk)],
                      [(0, 1, NN)], epilogue, temp_bytes=2 * tm * tn * 4)[0]

    def epilogue(acc, ins, outs):
        e = ins[2][...] + 0.5 * acc - ins[3][...]
        outs[0][...] = e * (1.0 / d)
        outs[2][...] = (e * (1.0 / d)).astype(BF16)

        @pl.when((pl.program_id(0) == 0) & (pl.program_id(1) == 0))
        def _():
            outs[1][...] = jnp.zeros_like(outs[1])

        part = jnp.sum(jnp.sum(e * e, axis=1, keepdims=True), axis=0, keepdims=True)
        outs[1][...] += jnp.broadcast_to(part, outs[1].shape)

    return _fused(name, (d // tn, t // tm, 1), ins + [(target, blk)],
                  [(_sds((t, d), F32), blk), (_sds((8, LANES), F32), pl.BlockSpec((8, LANES), lambda j, i, k: (0, 0))),
                   (_sds((t, d), BF16), blk)],
                  [(0, 1, NN)], epilogue, temp_bytes=3 * tm * tn * 4,
                  semantics=("arbitrary", "arbitrary", "arbitrary"))


def _ffn_bwd_act(name, dy, wd, gu, deps=()):
    t, d = dy.shape
    f = wd.shape[0]
    nb = f // 4
    tm = min(t, 512)

    def body(dy_ref, wd_ref, gu_ref, *rest):
        dgu_ref, a_ref = rest[-2], rest[-1]
        dyv = dy_ref[...].astype(BF16)
        for c0 in range(0, nb, MXU_COLS):
            cs = slice(c0, min(c0 + MXU_COLS, nb))
            da = 0.5 * lax.dot_general(dyv, wd_ref[cs, :], NT, preferred_element_type=F32)
            g = gu_ref[0, :, cs].astype(F32)
            u = gu_ref[1, :, cs].astype(F32)
            s = _sigmoid(g)
            silu = g * s
            dgu_ref[0, :, cs] = (da * u * (s * (1.0 + g * (1.0 - s)))).astype(BF16)
            dgu_ref[1, :, cs] = (da * silu).astype(BF16)
            a_ref[:, cs] = (silu * u).astype(BF16)

    blocks = tm * d * 4 + nb * d * 2 + 5 * tm * nb * 2
    return pl.pallas_call(
        body, name=name, grid=(4, t // tm),
        in_specs=[pl.BlockSpec((tm, d), lambda j, i: (i, 0)),
                  pl.BlockSpec((nb, d), lambda j, i: (j, 0)),
                  pl.BlockSpec((2, tm, nb), lambda j, i: (0, i, j))] + [_ANY] * len(deps),
        out_specs=[pl.BlockSpec((2, tm, nb), lambda j, i: (0, i, j)), pl.BlockSpec((tm, nb), lambda j, i: (i, j))],
        out_shape=[_sds((2, t, f), BF16), _sds((t, f), BF16)],
        compiler_params=_params(("parallel", "parallel"), blocks, tm * d * 2 + 8 * tm * MXU_COLS * 4),
    )(dy, wd, gu, *deps)


def _ffn_bwd_dwd(name, a, dy, deps=(), side=None):
    t, f = a.shape
    d = dy.shape[1]
    tm = f // 4
    tn = min(d, 512)

    def epilogue(acc, ins, outs):
        outs[0][...] = (0.5 * acc).astype(BF16)

    return _fused(name, (4, d // tn, 1),
                  [(a, pl.BlockSpec((t, tm), lambda i, j, k: (0, i))),
                   (dy, pl.BlockSpec((t, tn), lambda i, j, k: (0, j)))],
                  [(_sds((f, d), BF16), pl.BlockSpec((tm, tn), lambda i, j, k: (i, j)))],
                  [(0, 1, TN)], epilogue, temp_bytes=t * tn * 2 + 2 * tm * tn * 4, deps=deps, side=side)


def _ffn_bwd_dh(name, dgu, wgu, deps=(), side=None):
    _, t, f = dgu.shape
    d, nb = wgu.shape[1], wgu.shape[2]
    tm = min(t, 512)

    def products(ins):
        return (lax.dot_general(ins[0][:, 0:nb], ins[1][0], NT, preferred_element_type=F32)
                + lax.dot_general(ins[0][:, nb:2 * nb], ins[1][1], NT, preferred_element_type=F32))

    def epilogue(acc, ins, outs):
        outs[0][...] = acc

    return _fused(name, (t // tm, 1, 4),
                  [(dgu, pl.BlockSpec((None, tm, 2 * nb), lambda i, j, k: (k // 2, i, k % 2))),
                   (wgu, pl.BlockSpec((2, d, nb), lambda i, j, k: (k, 0, 0)))],
                  [(_sds((t, d), F32), pl.BlockSpec((tm, d), lambda i, j, k: (i, 0)))],
                  products, epilogue, nk=4, acc_shape=(tm, d), temp_bytes=tm * d * 4, deps=deps, side=side)


def _ffn_bwd_dwgu(name, h, dgu, deps=(), side=None, rows=None):
    t, d = h.shape
    nb = dgu.shape[2] // 4
    tm = min(d, 512)
    row0, nrows = rows if rows is not None else (0, d)
    j0 = row0 // tm

    def epilogue(acc, ins, outs):
        outs[0][...] = acc.astype(BF16)

    return _fused(name, (N_DEV, nrows // tm, 1),
                  [(h, pl.BlockSpec((t, tm), lambda i, j, k: (0, j0 + j))),
                   (dgu, pl.BlockSpec((None, t, nb), lambda i, j, k: (i // 4, 0, i % 4)))],
                  [(_sds((N_DEV, nrows, nb), BF16), pl.BlockSpec((None, tm, nb), lambda i, j, k: (i, j, 0)))],
                  [(0, 1, TN)], epilogue, temp_bytes=2 * tm * nb * 4, deps=deps, side=side)


def _proj(h, w_in):
    t, d = h.shape
    nb = w_in.shape[3]
    tm = min(t, 512)

    def body(h_ref, w_ref, o_ref):
        hv = h_ref[...]
        o_ref[:, 0:nb] = jnp.dot(hv, w_ref[0], preferred_element_type=F32).astype(BF16)
        o_ref[:, nb:2 * nb] = jnp.dot(hv, w_ref[1], preferred_element_type=F32).astype(BF16)

    blocks = tm * d * 2 + 2 * d * nb * 2 + tm * 2 * nb * 4
    return pl.pallas_call(
        body, name="mix_proj", grid=(4, t // tm),
        in_specs=[pl.BlockSpec((tm, d), lambda j, i: (i, 0)),
                  pl.BlockSpec((None, 2, d, nb), lambda j, i: (j, 0, 0, 0))],
        out_specs=pl.BlockSpec((tm, 2 * nb), lambda j, i: (i, j)),
        out_shape=_sds((t, N_DEV * nb), BF16),
        compiler_params=_params(("parallel", "parallel"), blocks, 2 * tm * nb * 4),
    )(h, w_in)


def _shift_rows(u, k):
    t = u.shape[0]
    rolled = pltpu.roll(u, k % t, axis=0)
    row = lax.broadcasted_iota(jnp.int32, u.shape, 0)
    keep = (row >= k) if k > 0 else (row < t + k)
    return jnp.where(keep, rolled, 0.0)


def _conv_fwd(proj, conv_w):
    t = proj.shape[0]
    cw = conv_w.shape[1]
    tc = min(cw, 256)
    nc = cw // tc

    def epilogue(_, ins, outs):
        u = ins[2][...].astype(F32) * ins[0][...].astype(F32)
        w = ins[3][...]
        y = u * w[2:3, :] + _shift_rows(u, 1) * w[1:2, :] + _shift_rows(u, 2) * w[0:1, :]
        outs[0][...] = (ins[1][...].astype(F32) * y).astype(BF16)

    def col(seg):
        return pl.BlockSpec((t, tc), lambda i, j, k: (0, seg * nc + i))

    return _fused("conv_fwd", (nc, 1, 1),
                  [(proj, col(0)), (proj, col(1)), (proj, col(2)),
                   (conv_w, pl.BlockSpec((8, tc), lambda i, j, k: (0, i)))],
                  [(_sds((t, cw), BF16), pl.BlockSpec((t, tc), lambda i, j, k: (0, i)))],
                  [], epilogue, temp_bytes=6 * t * tc * 4)[0]


def _conv_bwd(proj, conv_w, dca, deps=()):
    t = proj.shape[0]
    cw = conv_w.shape[1]
    tc = min(cw, 256)
    nc = cw // tc

    def epilogue(_, ins, outs):
        xc, bg, cg = ins[0][...].astype(F32), ins[1][...].astype(F32), ins[2][...].astype(F32)
        w, dc = ins[3][...], ins[4][...]
        u = cg * xc
        u1, u2 = _shift_rows(u, 1), _shift_rows(u, 2)
        y = u * w[2:3, :] + u1 * w[1:2, :] + u2 * w[0:1, :]
        dconv = dc * bg
        du = dconv * w[2:3, :] + _shift_rows(dconv, -1) * w[1:2, :] + _shift_rows(dconv, -2) * w[0:1, :]
        outs[0][0] = (du * cg).astype(BF16)
        outs[0][1] = (dc * y).astype(BF16)
        outs[0][2] = (du * xc).astype(BF16)
        outs[1][...] = jnp.zeros_like(outs[1])
        outs[1][0:1, :] = jnp.sum(dconv * u2, axis=0, keepdims=True)
        outs[1][1:2, :] = jnp.sum(dconv * u1, axis=0, keepdims=True)
        outs[1][2:3, :] = jnp.sum(dconv * u, axis=0, keepdims=True)

    def col(seg):
        return pl.BlockSpec((t, tc), lambda i, j, k: (0, seg * nc + i))

    own = pl.BlockSpec((t, tc), lambda i, j, k: (0, i))
    wspec = pl.BlockSpec((8, tc), lambda i, j, k: (0, i))
    return _fused("conv_bwd", (nc, 1, 1),
                  [(proj, col(0)), (proj, col(1)), (proj, col(2)), (conv_w, wspec), (dca, own)],
                  [(_sds((3, t, cw), BF16), pl.BlockSpec((3, t, tc), lambda i, j, k: (0, 0, i))),
                   (_sds((8, cw), F32), wspec)],
                  [], epilogue, temp_bytes=10 * t * tc * 4, deps=deps)


def _split3(x):
    hi = x.astype(BF16)
    r1 = x - hi.astype(F32)
    mid = r1.astype(BF16)
    lo = (r1 - mid.astype(F32)).astype(BF16)
    return hi, mid, lo


def _head_selector(width):
    r = lax.broadcasted_iota(jnp.int32, (width, LANES), 0)
    c = lax.broadcasted_iota(jnp.int32, (width, LANES), 1)
    return (lax.shift_right_logical(r, 6) == c).astype(BF16)


def _head_sum(x, sel):
    return sum(jnp.dot(p, sel, preferred_element_type=F32) for p in _split3(x))


def _head_bcast(r, sel):
    return sum(lax.dot_general(p, sel, NT, preferred_element_type=F32) for p in _split3(r))


def _rope(x, c, sa, sb):
    n = x.shape[1]
    return x * c + pltpu.roll(x, n - ROT_DIM // 2, axis=1) * sa + pltpu.roll(x, ROT_DIM // 2, axis=1) * sb


def _rope_t(d, c, sa, sb):
    n = d.shape[1]
    return d * c + pltpu.roll(d * sa, ROT_DIM // 2, axis=1) + pltpu.roll(d * sb, n - ROT_DIM // 2, axis=1)


def _tile_lanes(tab, width):
    return tab if width == tab.shape[1] else jnp.tile(tab, (1, width // tab.shape[1]))


def _qk_prep(proj, gq, gk, rope_tabs, cw, kw):
    t = proj.shape[0]
    tm = _row_tile(t)

    def epilogue(_, ins, outs):
        c, sa, sb = ins[5][...], ins[6][...], ins[7][...]
        for src, gain, dst, width in ((0, 3, 0, cw), (1, 4, 1, kw)):
            xv = ins[src][...].astype(F32)
            sel = _head_selector(width)
            r = lax.rsqrt(_head_sum(xv * xv, sel) * (1.0 / HEAD_DIM) + RMS_EPS)
            xn = xv * _head_bcast(r, sel) * ins[gain][...]
            outs[dst][...] = _rope(xn, _tile_lanes(c, width), _tile_lanes(sa, width), _tile_lanes(sb, width)).astype(BF16)
        outs[2][...] = ins[2][...].astype(BF16)

    kblk = cw // kw
    tab = pl.BlockSpec((tm, LANES), lambda i, j, k: (i, 0))
    kspec = pl.BlockSpec((tm, kw), lambda i, j, k: (i, 0))
    return _fused("qk_prep", (t // tm, 1, 1),
                  [(proj, pl.BlockSpec((tm, cw), lambda i, j, k: (i, 3))),
                   (proj, pl.BlockSpec((tm, kw), lambda i, j, k: (i, 4 * kblk))),
                   (proj, pl.BlockSpec((tm, kw), lambda i, j, k: (i, 4 * kblk + 1))),
                   (gq, pl.BlockSpec((1, cw), lambda i, j, k: (0, 0))),
                   (gk, pl.BlockSpec((1, kw), lambda i, j, k: (0, 0))),
                   (rope_tabs[0], tab), (rope_tabs[1], tab), (rope_tabs[2], tab)],
                  [(_sds((t, cw), BF16), pl.BlockSpec((tm, cw), lambda i, j, k: (i, 0))),
                   (_sds((t, kw), BF16), kspec), (_sds((t, kw), BF16), kspec)],
                  [], epilogue, temp_bytes=12 * tm * cw * 4)


def _qk_prep_bwd(proj, gq, gk, rope_tabs, dq, dkc, dkp, dvc, dvp, cw, kw):
    t = proj.shape[0]
    tm = BLOCK
    nblk = t // tm

    def epilogue(_, ins, outs):
        c, sa, sb = ins[5][...], ins[6][...], ins[7][...]
        has_next = (pl.program_id(0) < nblk - 1).astype(F32)
        dk = ins[9][...] + has_next * ins[10][...]
        dv = ins[11][...] + has_next * ins[12][...]
        pieces = []
        for src, gain, dval, dst, width in ((0, 3, ins[8][...], 1, cw), (1, 4, dk, 2, kw)):
            xv, gv = ins[src][...].astype(F32), ins[gain][...]
            sel = _head_selector(width)
            r = _head_bcast(lax.rsqrt(_head_sum(xv * xv, sel) * (1.0 / HEAD_DIM) + RMS_EPS), sel)
            xh = xv * r
            dxn = _rope_t(dval, _tile_lanes(c, width), _tile_lanes(sa, width), _tile_lanes(sb, width))
            u = dxn * gv
            dot = _head_bcast(_head_sum(u * xh, sel), sel) * (1.0 / HEAD_DIM)
            pieces.append((r * (u - xh * dot)).astype(BF16))
            ri = lax.broadcasted_iota(jnp.int32, (width, LANES), 0)
            ci = lax.broadcasted_iota(jnp.int32, (width, LANES), 1)
            fold = (lax.bitwise_and(ri, HEAD_DIM - 1) == ci).astype(BF16)
            colsum = jnp.broadcast_to(jnp.sum(dxn * xh, axis=0, keepdims=True), (8, width))
            part = sum(jnp.dot(p, fold, preferred_element_type=F32) for p in _split3(colsum))

            @pl.when(pl.program_id(0) == 0)
            def _():
                outs[dst][...] = jnp.zeros_like(outs[dst])

            outs[dst][0:1, :] += part[0:1, :]
        outs[0][:, 0:cw] = pieces[0]
        outs[0][:, cw:cw + kw] = pieces[1]
        outs[0][:, cw + kw:cw + 2 * kw] = dv.astype(BF16)

    kblk = cw // kw
    tab = pl.BlockSpec((tm, LANES), lambda i, j, k: (i, 0))
    kcur = pl.BlockSpec((tm, kw), lambda i, j, k: (i, 0))
    knext = pl.BlockSpec((tm, kw), lambda i, j, k: (jnp.minimum(i + 1, nblk - 1), 0))
    acc = pl.BlockSpec((8, LANES), lambda i, j, k: (0, 0))
    return _fused("qk_prep_bwd", (nblk, 1, 1),
                  [(proj, pl.BlockSpec((tm, cw), lambda i, j, k: (i, 3))),
                   (proj, pl.BlockSpec((tm, kw), lambda i, j, k: (i, 4 * kblk))),
                   (proj, pl.BlockSpec((tm, kw), lambda i, j, k: (i, 4 * kblk + 1))),
                   (gq, pl.BlockSpec((1, cw), lambda i, j, k: (0, 0))),
                   (gk, pl.BlockSpec((1, kw), lambda i, j, k: (0, 0))),
                   (rope_tabs[0], tab), (rope_tabs[1], tab), (rope_tabs[2], tab),
                   (dq, pl.BlockSpec((tm, cw), lambda i, j, k: (i, 0))),
                   (dkc, kcur), (dkp, knext), (dvc, kcur), (dvp, knext)],
                  [(_sds((t, cw + 2 * kw), BF16), pl.BlockSpec((tm, cw + 2 * kw), lambda i, j, k: (i, 0))),
                   (_sds((8, LANES), F32), acc), (_sds((8, LANES), F32), acc)],
                  [], epilogue, temp_bytes=16 * tm * cw * 4, semantics=("arbitrary", "arbitrary", "arbitrary"))


def _attn_mask(n):
    key = lax.broadcasted_iota(jnp.int32, (2 * BLOCK, GROUP * BLOCK), 0)
    qry = lax.bitwise_and(lax.broadcasted_iota(jnp.int32, (2 * BLOCK, GROUP * BLOCK), 1), BLOCK - 1)
    return (key > qry) & (key <= qry + BLOCK) & ((key >= BLOCK) | (n > 0))


def _stack_heads(x, h):
    return jnp.concatenate([x[:, (h * GROUP + g) * HEAD_DIM:(h * GROUP + g + 1) * HEAD_DIM] for g in range(GROUP)], axis=0)


def _softmax_with_sink(q4, k2, sink_ref, h, valid):
    sink = jnp.concatenate([sink_ref[h * GROUP + g:h * GROUP + g + 1, :] for g in range(GROUP)], axis=1)
    s = lax.dot_general(k2, q4, NT, preferred_element_type=F32) * ATTN_SCALE
    s = jnp.where(valid, s, NEG_INF)
    m = jnp.maximum(jnp.max(s, axis=0, keepdims=True), sink)
    p = jnp.exp(s - m)
    es = jnp.exp(sink - m)
    inv = 1.0 / (jnp.sum(p, axis=0, keepdims=True) + es)
    return p * inv, es * inv


def _attn_fwd(qn, kn, vb, sink_rows):
    t, cw = qn.shape
    kw = kn.shape[1]
    nkv = kw // HEAD_DIM

    def body(q_ref, kp_ref, kc_ref, vp_ref, vc_ref, sink_ref, o_ref):
        valid = _attn_mask(pl.program_id(0))
        qv = q_ref[...]
        kp, kc, vp, vc = kp_ref[...], kc_ref[...], vp_ref[...], vc_ref[...]
        outs = []
        for h in range(nkv):
            hs = slice(h * HEAD_DIM, (h + 1) * HEAD_DIM)
            k2 = jnp.concatenate([kp[:, hs], kc[:, hs]], axis=0)
            v2 = jnp.concatenate([vp[:, hs], vc[:, hs]], axis=0)
            pn, _ = _softmax_with_sink(_stack_heads(qv, h), k2, sink_ref, h, valid)
            o4 = lax.dot_general(pn.astype(BF16), v2, TN, preferred_element_type=F32)
            outs += [o4[g * BLOCK:(g + 1) * BLOCK] for g in range(GROUP)]
        o_ref[...] = jnp.concatenate(outs, axis=-1).astype(BF16)

    cur = lambda n: (n, 0)
    prev = lambda n: (jnp.maximum(n - 1, 0), 0)
    return pl.pallas_call(
        body, name="attn_fwd", grid=(t // BLOCK,),
        in_specs=[pl.BlockSpec((BLOCK, cw), cur),
                  pl.BlockSpec((BLOCK, kw), prev), pl.BlockSpec((BLOCK, kw), cur),
                  pl.BlockSpec((BLOCK, kw), prev), pl.BlockSpec((BLOCK, kw), cur),
                  pl.BlockSpec(sink_rows.shape, lambda n: (0, 0))],
        out_specs=pl.BlockSpec((BLOCK, cw), cur),
        out_shape=_sds((t, cw), BF16),
        compiler_params=_params(("parallel",), BLOCK * (cw + 4 * kw) * 2 + BLOCK * cw * 2, 8 << 20),
    )(qn, kn, kn, vb, vb, sink_rows)


def _attn_bwd(qn, kn, vb, sink_rows, do):
    t, cw = qn.shape
    kw = kn.shape[1]
    nkv = kw // HEAD_DIM
    nq = nkv * GROUP

    def body(q_ref, kp_ref, kc_ref, vp_ref, vc_ref, sink_ref, do_ref,
             dq_ref, dkc_ref, dkp_ref, dvc_ref, dvp_ref, dsink_ref):
        n = pl.program_id(0)
        valid = _attn_mask(n)
        qv, dov = q_ref[...], do_ref[...]
        kp, kc, vp, vc = kp_ref[...], kc_ref[...], vp_ref[...], vc_ref[...]
        dqs, dks, dvs, dsinks = [], [], [], []
        for h in range(nkv):
            hs = slice(h * HEAD_DIM, (h + 1) * HEAD_DIM)
            k2 = jnp.concatenate([kp[:, hs], kc[:, hs]], axis=0)
            v2 = jnp.concatenate([vp[:, hs], vc[:, hs]], axis=0)
            q4 = _stack_heads(qv, h)
            dob = _stack_heads(dov, h).astype(BF16)
            pn, psink = _softmax_with_sink(q4, k2, sink_ref, h, valid)
            dpn = lax.dot_general(v2, dob, NT, preferred_element_type=F32)
            dvs.append(jnp.dot(pn.astype(BF16), dob, preferred_element_type=F32))
            delta = jnp.sum(pn * dpn, axis=0, keepdims=True)
            ds = (pn * (dpn - delta) * ATTN_SCALE).astype(BF16)
            dks.append(jnp.dot(ds, q4, preferred_element_type=F32))
            dq4 = lax.dot_general(ds, k2, TN, preferred_element_type=F32)
            dsink4 = -psink * delta
            for g in range(GROUP):
                dqs.append(dq4[g * BLOCK:(g + 1) * BLOCK])
                dsinks.append(jnp.broadcast_to(jnp.sum(dsink4[:, g * BLOCK:(g + 1) * BLOCK], axis=1, keepdims=True), (1, LANES)))
        dq_ref[...] = jnp.concatenate(dqs, axis=-1)
        dkp_ref[...] = jnp.concatenate([d[:BLOCK] for d in dks], axis=-1)
        dkc_ref[...] = jnp.concatenate([d[BLOCK:] for d in dks], axis=-1)
        dvp_ref[...] = jnp.concatenate([d[:BLOCK] for d in dvs], axis=-1)
        dvc_ref[...] = jnp.concatenate([d[BLOCK:] for d in dvs], axis=-1)

        @pl.when(n == 0)
        def _():
            dsink_ref[...] = jnp.zeros_like(dsink_ref)

        dsink_ref[...] += jnp.concatenate(dsinks, axis=0)

    cur = lambda n: (n, 0)
    prev = lambda n: (jnp.maximum(n - 1, 0), 0)
    kspec = pl.BlockSpec((BLOCK, kw), cur)
    return pl.pallas_call(
        body, name="attn_bwd", grid=(t // BLOCK,),
        in_specs=[pl.BlockSpec((BLOCK, cw), cur),
                  pl.BlockSpec((BLOCK, kw), prev), kspec,
                  pl.BlockSpec((BLOCK, kw), prev), kspec,
                  pl.BlockSpec(sink_rows.shape, lambda n: (0, 0)),
                  pl.BlockSpec((BLOCK, cw), cur)],
        out_specs=[pl.BlockSpec((BLOCK, cw), cur), kspec, kspec, kspec, kspec,
                   pl.BlockSpec((nq, LANES), lambda n: (0, 0))],
        out_shape=[_sds((t, cw), F32)] + [_sds((t, kw), F32)] * 4 + [_sds((nq, LANES), F32)],
        compiler_params=_params(("arbitrary",), BLOCK * (cw + 4 * kw) * 2 + 2 * BLOCK * cw * 4 + 4 * BLOCK * kw * 4, 12 << 20),
    )(qn, kn, kn, vb, vb, sink_rows, do)


def _mix_out(ca, o, woc, woa, proj):
    t, cw = ca.shape
    nb = woc.shape[2]
    d = N_DEV * nb
    tm = min(t, 1024)
    ga0 = (3 * cw + cw + 2 * (cw // 4)) // nb

    def body(ca_ref, o_ref, woc_ref, woa_ref, ga_ref, gb_ref, m_ref, ya_ref, yb_ref):
        ya = jnp.dot(ca_ref[...], woc_ref[...], preferred_element_type=F32)
        yb = jnp.dot(o_ref[...], woa_ref[...], preferred_element_type=F32)
        ya_ref[...] = ya.astype(BF16)
        yb_ref[...] = yb.astype(BF16)
        m_ref[...] = (_sigmoid(ga_ref[...].astype(F32)) * ya + _sigmoid(gb_ref[...].astype(F32)) * yb).astype(BF16)

    act = pl.BlockSpec((tm, cw), lambda i, j: (i, 0))
    wsp = pl.BlockSpec((None, cw, nb), lambda i, j: (j, 0, 0))
    osp = pl.BlockSpec((tm, nb), lambda i, j: (i, j))
    blocks = 2 * tm * cw * 2 + 2 * cw * nb * 2 + 2 * tm * nb * 4 + 3 * tm * nb * 2
    return pl.pallas_call(
        body, name="mix_out", grid=(t // tm, N_DEV),
        in_specs=[act, act, wsp, wsp,
                  pl.BlockSpec((tm, nb), lambda i, j: (i, ga0 + j)),
                  pl.BlockSpec((tm, nb), lambda i, j: (i, ga0 + N_DEV + j))],
        out_specs=[osp, osp, osp],
        out_shape=[_sds((t, d), BF16)] * 3,
        compiler_params=_params(("parallel", "parallel"), blocks, 6 * tm * nb * 4),
    )(ca, o, woc, woa, proj, proj)


def _mix_residual(merged, wo, x):
    t, d = x.shape
    tm = min(t, 512)

    def epilogue(acc, ins, outs):
        outs[0][...] = ins[2][...] + acc

    row = pl.BlockSpec((tm, d), lambda i, j, k: (i, 0))
    return _fused("mix_residual", (t // tm, 1, 1),
                  [(merged, row), (wo, pl.BlockSpec((d, d), lambda i, j, k: (0, 0))), (x, row)],
                  [(_sds((t, d), F32), row)], [(0, 1, NN)], epilogue, temp_bytes=2 * tm * d * 4)[0]


def _mix_bwd_gates(dx, wo, ya, yb, proj, cw):
    t, d = dx.shape
    tm = min(t, 1024)
    tn = min(d, 512)
    ga0 = (4 * cw + 2 * (cw // 4)) // tn

    def epilogue(acc, ins, outs):
        sa, sb = _sigmoid(ins[4][...].astype(F32)), _sigmoid(ins[5][...].astype(F32))
        outs[0][...] = (acc * sa).astype(BF16)
        outs[1][...] = (acc * sb).astype(BF16)
        outs[2][0] = (acc * ins[2][...].astype(F32) * sa * (1.0 - sa)).astype(BF16)
        outs[2][1] = (acc * ins[3][...].astype(F32) * sb * (1.0 - sb)).astype(BF16)

    blk = pl.BlockSpec((tm, tn), lambda i, j, k: (i, j))
    return _fused("mix_bwd_gates", (t // tm, d // tn, 1),
                  [(dx, pl.BlockSpec((tm, d), lambda i, j, k: (i, 0))),
                   (wo, pl.BlockSpec((tn, d), lambda i, j, k: (j, 0))),
                   (ya, blk), (yb, blk),
                   (proj, pl.BlockSpec((tm, tn), lambda i, j, k: (i, ga0 + j))),
                   (proj, pl.BlockSpec((tm, tn), lambda i, j, k: (i, ga0 + d // tn + j)))],
                  [(_sds((t, d), BF16), blk), (_sds((t, d), BF16), blk),
                   (_sds((2, t, d), BF16), pl.BlockSpec((2, tm, tn), lambda i, j, k: (0, i, j)))],
                  [(0, 1, NT)], epilogue, temp_bytes=8 * tm * tn * 4)


def _tn_matmul(name, a, b, tm, out_dtype=BF16):
    t, m = a.shape
    n = b.shape[1]

    def epilogue(acc, ins, outs):
        outs[0][...] = acc.astype(out_dtype)

    return _fused(name, (m // tm, 1, 1),
                  [(a, pl.BlockSpec((t, tm), lambda i, j, k: (0, i))),
                   (b, pl.BlockSpec((t, n), lambda i, j, k: (0, 0)))],
                  [(_sds((m, n), out_dtype), pl.BlockSpec((tm, n), lambda i, j, k: (i, 0)))],
                  [(0, 1, TN)], epilogue, temp_bytes=2 * tm * n * 4)[0]


def _out_proj_bwd_act(dya, dyb, woc, woa, deps=()):
    t, d = dya.shape
    kdim, nb = woc.shape[1], woc.shape[2]
    tm = min(t, 512)

    def body(dya_ref, dyb_ref, woc_ref, woa_ref, *rest):
        for dy_ref, w_ref, o_ref in ((dya_ref, woc_ref, rest[-2]), (dyb_ref, woa_ref, rest[-1])):
            total = None
            for j in range(N_DEV):
                part = lax.dot_general(dy_ref[:, j * nb:(j + 1) * nb], w_ref[j], NT, preferred_element_type=F32)
                total = part if total is None else total + part
            o_ref[...] = total

    row = pl.BlockSpec((tm, d), lambda i: (i, 0))
    wsp = pl.BlockSpec((N_DEV, kdim, nb), lambda i: (0, 0, 0))
    osp = pl.BlockSpec((tm, kdim), lambda i: (i, 0))
    blocks = 2 * tm * d * 2 + 2 * N_DEV * kdim * nb * 2 + 2 * tm * kdim * 4
    return pl.pallas_call(
        body, name="mix_bwd_dca_do", grid=(t // tm,),
        in_specs=[row, row, wsp, wsp] + [_ANY] * len(deps), out_specs=[osp, osp],
        out_shape=[_sds((t, kdim), F32)] * 2,
        compiler_params=_params(("parallel",), blocks, 4 * tm * kdim * 4),
    )(dya, dyb, woc, woa, *deps)


def _out_proj_bwd_w(ca, o, dya, dyb, nb):
    t, kdim = ca.shape

    def body(ca_ref, o_ref, dya_ref, dyb_ref, dwoc_ref, dwoa_ref):
        dwoc_ref[...] = lax.dot_general(ca_ref[...], dya_ref[...], TN, preferred_element_type=F32).astype(BF16)
        dwoa_ref[...] = lax.dot_general(o_ref[...], dyb_ref[...], TN, preferred_element_type=F32).astype(BF16)

    act = pl.BlockSpec((t, kdim), lambda j: (0, 0))
    col = pl.BlockSpec((t, nb), lambda j: (0, j))
    osp = pl.BlockSpec((None, kdim, nb), lambda j: (j, 0, 0))
    blocks = 2 * t * kdim * 2 + 2 * t * nb * 2 + 2 * kdim * nb * 2
    return pl.pallas_call(
        body, name="mix_bwd_dwoc_dwoa", grid=(N_DEV,),
        in_specs=[act, act, col, col], out_specs=[osp, osp],
        out_shape=[_sds((N_DEV, kdim, nb), BF16)] * 2,
        compiler_params=_params(("parallel",), blocks, 4 * kdim * nb * 4),
    )(ca, o, dya, dyb)


def _proj_bwd_act(dproj, w_in, deps=()):
    t, n = dproj.shape
    d, nb = w_in.shape[2], w_in.shape[3]
    tm = min(t, 512)

    def epilogue(acc, ins, outs):
        outs[0][...] = acc

    def products(ins):
        return (lax.dot_general(ins[0][:, 0:nb], ins[1][0], NT, preferred_element_type=F32)
                + lax.dot_general(ins[0][:, nb:2 * nb], ins[1][1], NT, preferred_element_type=F32))

    return _fused("mix_bwd_dh", (t // tm, 1, 4),
                  [(dproj, pl.BlockSpec((tm, 2 * nb), lambda i, j, k: (i, k))),
                   (w_in, pl.BlockSpec((None, 2, d, nb), lambda i, j, k: (k, 0, 0, 0)))],
                  [(_sds((t, d), F32), pl.BlockSpec((tm, d), lambda i, j, k: (i, 0)))],
                  products, epilogue, nk=4, acc_shape=(tm, d), temp_bytes=tm * d * 4, deps=deps)[0]


def _proj_bwd_w(h, dproj):
    t, d = h.shape
    nb = dproj.shape[1] // N_DEV
    tm = min(d, 512)

    def body(h_ref, dp_ref, o_ref):
        hv = h_ref[...]
        o_ref[0] = lax.dot_general(hv, dp_ref[:, 0:nb], TN, preferred_element_type=F32).astype(BF16)
        o_ref[1] = lax.dot_general(hv, dp_ref[:, nb:2 * nb], TN, preferred_element_type=F32).astype(BF16)

    blocks = t * tm * 2 + t * 2 * nb * 2 + 2 * tm * nb * 2
    return pl.pallas_call(
        body, name="mix_bwd_dwin", grid=(4, d // tm),
        in_specs=[pl.BlockSpec((t, tm), lambda j, i: (0, i)),
                  pl.BlockSpec((t, 2 * nb), lambda j, i: (0, j))],
        out_specs=pl.BlockSpec((None, 2, tm, nb), lambda j, i: (j, 0, i, 0)),
        out_shape=_sds((4, 2, d, nb), BF16),
        compiler_params=_params(("parallel", "parallel"), blocks, 4 * tm * nb * 4),
    )(h, dproj)


def _adamw_math(w, g, m, v):
    m = ADAM_B1 * m + (1.0 - ADAM_B1) * g
    v = ADAM_B2 * v + (1.0 - ADAM_B2) * (g * g)
    m_hat = m / (1.0 - ADAM_B1 ** ADAM_STEP)
    v_hat = v / (1.0 - ADAM_B2 ** ADAM_STEP)
    delta = -ADAM_LR * (m_hat / (jnp.sqrt(v_hat) + ADAM_EPS) + ADAM_WD * w)
    return delta, m, v


def _adamw(name, parts, w, m, v, tr):
    r, c = w.shape

    def body(p_ref, w_ref, m_ref, v_ref, g_out, d_out, m_out, v_out):
        g = p_ref[0].astype(F32)
        for s in range(1, N_DEV):
            g = g + p_ref[s].astype(F32)
        delta, mn, vn = _adamw_math(w_ref[...], g, m_ref[...], v_ref[...])
        g_out[...] = g
        d_out[...] = delta
        m_out[...] = mn
        v_out[...] = vn

    blk = pl.BlockSpec((tr, c), lambda i: (i, 0))
    blocks = N_DEV * tr * c * parts.dtype.itemsize + 7 * tr * c * 4
    return pl.pallas_call(
        body, name=name, grid=(r // tr,),
        in_specs=[pl.BlockSpec((N_DEV, tr, c), lambda i: (0, i, 0)), blk, blk, blk],
        out_specs=[blk] * 4, out_shape=[_sds((r, c), F32)] * 4,
        compiler_params=_params(("parallel",), blocks, 6 * tr * c * 4),
    )(parts, w, m, v)


def _chip_sum(sums_ref):
    g = sums_ref[0].astype(F32)
    for k in range(1, 4):
        g = g + sums_ref[k].astype(F32)
    return g


def _adamw_chips(name, sums, w, m, v, tr, deps=(), row0=0, into=None):
    r, c = w.shape
    rs = sums.shape[1]
    i0 = row0 // tr
    n_pass = len(deps) + (4 if into is not None else 0)

    def body(sums_ref, w_ref, m_ref, v_ref, *rest):
        g_out, d_out, m_out, v_out = rest[n_pass:]
        g = _chip_sum(sums_ref)
        delta, mn, vn = _adamw_math(w_ref[...], g, m_ref[...], v_ref[...])
        g_out[...] = g
        d_out[...] = delta
        m_out[...] = mn
        v_out[...] = vn

    blk = pl.BlockSpec((tr, c), lambda i: (i0 + i, 0))
    blocks = 4 * tr * c * 2 + 7 * tr * c * 4
    passed = list(deps) + (list(into) if into is not None else [])
    aliases = {4 + len(deps) + q: q for q in range(4)} if into is not None else {}
    return pl.pallas_call(
        body, name=name, grid=(rs // tr,),
        in_specs=[pl.BlockSpec((4, tr, c), lambda i: (0, i, 0)), blk, blk, blk] + [_ANY] * n_pass,
        out_specs=[blk] * 4, out_shape=[_sds((r, c), F32)] * 4,
        input_output_aliases=aliases,
        compiler_params=_params(("parallel",), blocks, 6 * tr * c * 4),
    )(sums, w, m, v, *passed)


def _adamw_side(contrib, w, m, v, n_tiles, step_of):
    r, c = w.shape
    tr = r // n_tiles
    assert tr * n_tiles == r and tr % 16 == 0, (r, n_tiles)

    def tile(i, j, k):
        return jnp.minimum(step_of(i, j, k), n_tiles - 1)

    blk = pl.BlockSpec((tr, c), lambda i, j, k: (tile(i, j, k), 0))
    ins = [(contrib, pl.BlockSpec((4, tr, c), lambda i, j, k: (0, tile(i, j, k), 0))), (w, blk), (m, blk), (v, blk)]
    outs = [(_sds((r, c), F32), blk)] * 4

    def fn(in_refs, out_refs):
        @pl.when(step_of(pl.program_id(0), pl.program_id(1), pl.program_id(2)) < n_tiles)
        def _():
            g = _chip_sum(in_refs[0])
            delta, mn, vn = _adamw_math(in_refs[1][...], g, in_refs[2][...], in_refs[3][...])
            out_refs[0][...] = g
            out_refs[1][...] = delta
            out_refs[2][...] = mn
            out_refs[3][...] = vn

    return ins, outs, fn


def _rope_tables(t):
    half = ROT_DIM // 2
    inv_freq = 1.0 / (ROPE_THETA ** (jnp.arange(0, ROT_DIM, 2, dtype=F32) / ROT_DIM))
    ang = jnp.arange(t, dtype=F32)[:, None] * inv_freq[None, :]
    cos, sin = jnp.cos(ang), jnp.sin(ang)
    ones = jnp.ones((t, HEAD_DIM - ROT_DIM), F32)
    zeros = jnp.zeros((t, HEAD_DIM - half), F32)
    c = jnp.concatenate([cos, cos, ones], axis=1)
    sa = jnp.concatenate([-sin, zeros], axis=1)
    sb = jnp.concatenate([jnp.zeros((t, half), F32), sin, jnp.zeros((t, HEAD_DIM - ROT_DIM), F32)], axis=1)
    return tuple(jnp.tile(a, (1, LANES // HEAD_DIM)) for a in (c, sa, sb))


def _pad_rows(a, rows=8):
    return jnp.pad(a, ((0, rows - a.shape[0]), (0, 0)))


def kernel(x, g_ffn1, w_gu1, w_down1, g_mix, w_in, conv_w, q_norm_g, k_norm_g, sinks, w_out_conv, w_out_attn, w_o, g_ffn2, w_gu2, w_down2, loss_target, m_g_ffn1, m_w_gu1, m_w_down1, m_g_mix, m_w_in, m_conv_w, m_q_norm_g, m_k_norm_g, m_sinks, m_w_out_conv, m_w_out_attn, m_w_o, m_g_ffn2, m_w_gu2, m_w_down2, v_g_ffn1, v_w_gu1, v_w_down1, v_g_mix, v_w_in, v_conv_w, v_q_norm_g, v_k_norm_g, v_sinks, v_w_out_conv, v_w_out_attn, v_w_o, v_g_ffn2, v_w_gu2, v_w_down2):
    t, d = x.shape[1], x.shape[2]
    cw = d // 2
    kw = cw // GROUP
    nq = cw // HEAD_DIM
    xs, target = x.reshape(t, d), loss_target.reshape(t, d)
    me = 4 * lax.axis_index("x") + 2 * lax.axis_index("y") + lax.axis_index("c")

    big = {"w_gu1": w_gu1, "w_down1": w_down1, "w_in": w_in, "w_out_conv": w_out_conv,
           "w_out_attn": w_out_attn, "w_o": w_o, "w_gu2": w_gu2, "w_down2": w_down2}
    big_m = {"w_gu1": m_w_gu1, "w_down1": m_w_down1, "w_in": m_w_in, "w_out_conv": m_w_out_conv,
             "w_out_attn": m_w_out_attn, "w_o": m_w_o, "w_gu2": m_w_gu2, "w_down2": m_w_down2}
    big_v = {"w_gu1": v_w_gu1, "w_down1": v_w_down1, "w_in": v_w_in, "w_out_conv": v_w_out_conv,
             "w_out_attn": v_w_out_attn, "w_o": v_w_o, "w_gu2": v_w_gu2, "w_down2": v_w_down2}
    names = list(big)

    tiles = {"w_gu1": 256, "w_gu2": 256, "w_in": 256, "w_down1": 176, "w_down2": 176,
             "w_out_conv": 1024, "w_out_attn": 1024, "w_o": 128}

    def row_tile(n):
        r = big[n].shape[1]
        return tiles[n] if r % tiles[n] == 0 else r

    rs_shape = {n: big[n].shape[1:] for n in names}
    half = rs_shape["w_gu1"][0] // 2
    rs_shape["w_gu1_lo"] = rs_shape["w_gu1_hi"] = (half, rs_shape["w_gu1"][1])

    def add_tile(n):
        r, c = rs_shape[n]
        while r * c * 2 > (3 << 20) and r % 32 == 0:
            r //= 2
        return r

    me_arr = me.astype(jnp.int32).reshape(1)
    sources = [(n, big[n][0], BF16, row_tile(n)) for n in names] + [("conv_w", _pad_rows(conv_w[0]), F32, 8)]
    issue_order = [0, 1, 2, 8, 3, 4, 5, 6, 7]
    first = _place_shard("place_" + names[0], sources[0][1], BF16, me_arr, sources[0][3])
    started = [_gather_start("gather_start_first", [first])]
    early = {2: (big_m["w_in"][0], big_v["w_in"][0])}
    rest = [_place_shard("place_" + sources[i][0], sources[i][1], sources[i][2], me_arr, sources[i][3],
                         deps=(started[0][3],) + early.get(i, ())) for i in issue_order[1:]]
    started.append(_gather_start("gather_start_rest", rest))
    where = {0: (0, 0)}
    where.update({i: (1, p) for p, i in enumerate(issue_order[1:])})

    def fetch(tag, idxs, after, forward=True):
        call = where[idxs[0]][0]
        send, recv, stacks, _ = started[call]
        positions = [where[i][1] for i in idxs]
        got = _gather_wait("gather_wait_" + tag, positions, send, recv, [stacks[p] for p in positions], after)
        return _forward_to_sibling("gather_forward_" + tag, got) if forward else got

    rope_tabs = _rope_tables(t)
    gq = jnp.tile(q_norm_g, (1, nq))
    gk = jnp.tile(k_norm_g, (1, nq // GROUP))
    sink_rows = jnp.broadcast_to(sinks[0][:, None], (nq, LANES))

    wts = {}
    h1 = _rms_fwd("ffn1_norm", xs, g_ffn1)
    wts["w_gu1"], = fetch("gu1", [0], started[1][3])
    gu1, a1 = _ffn_up("ffn1_up", h1, wts["w_gu1"])
    wts["w_down1"], = fetch("down1", [1], a1)
    wd1 = wts["w_down1"].reshape(-1, d)
    x1 = _ffn_down("ffn1_down", a1, wd1, xs)
    h2 = _rms_fwd("mix_norm", x1, g_mix)
    wts["w_in"], conv_land = fetch("in", [2, 8], h2)
    w_in_full = wts["w_in"].reshape(4, 2, d, -1)
    conv_full = jnp.transpose(conv_land, (1, 0, 2)).reshape(8, cw)
    proj = _proj(h2, w_in_full)
    ca = _conv_fwd(proj, conv_full)
    qn, kn, vb = _qk_prep(proj, gq, gk, rope_tabs, cw, kw)
    o = _attn_fwd(qn, kn, vb, sink_rows)
    wts["w_out_conv"], wts["w_out_attn"] = fetch("out", [3, 4], o)
    merged, ya, yb = _mix_out(ca, o, wts["w_out_conv"], wts["w_out_attn"], proj)
    wts["w_o"], = fetch("o", [5], merged)
    wo = wts["w_o"].reshape(d, d)
    x2 = _mix_residual(merged, wo, x1)
    h3 = _rms_fwd("ffn2_norm", x2, g_ffn2)
    mine = lax.axis_index("c").astype(jnp.int32).reshape(1)
    got = fetch("gu2", [6], h3, forward=False)
    fsend, frecv, got = _forward_start("gather_forward_start_gu2", got)
    part = _ffn_up("ffn2_up_mine", h3, got[0], parity=mine)
    wts["w_gu2"], = _forward_wait("gather_forward_wait_gu2", fsend, frecv, got, part[1])
    got = fetch("down2", [7], wts["w_gu2"], forward=False)
    fsend, frecv, got = _forward_start("gather_forward_start_down2", got)
    gu2, a2 = _ffn_up("ffn2_up_sibling", h3, wts["w_gu2"], parity=1 - mine, into=part, deps=(got[0],))
    wts["w_down2"], = _forward_wait("gather_forward_wait_down2", fsend, frecv, got, a2)
    wd2 = wts["w_down2"].reshape(-1, d)
    dy, sq, dy_bf = _ffn_down("ffn2_down", a2, wd2, x2, target=target)
    loss = lax.psum(sq[0, 0] * (0.5 / d), ("x", "y", "c"))

    place = jnp.stack([lax.axis_index("c"), 2 * lax.axis_index("x") + lax.axis_index("y")]).astype(jnp.int32)
    def pair_start(tag, group, grads, deps=()):
        stacks = [grads[n].reshape((4, 2) + rs_shape[n]) for n in group]
        lands = [lax.empty((4,) + rs_shape[n], BF16) for n in group]
        return _pair_start("rs_pair_start_" + tag, stacks, lands, deps)

    def chip_start(tag, group, pending, after):
        send, recv, stacks, lands, _ = pending
        stacks, lands = _pair_wait("rs_pair_wait_" + tag, send, recv, stacks, lands, after)
        added = [_pair_add("rs_pair_add_" + n, st, ld, place, add_tile(n)) for n, st, ld in zip(group, stacks, lands)]
        return _chip_start("rs_chip_start_" + tag, [a[0] for a in added], [a[1] for a in added])

    group_a, group_b, group_c = ["w_down2", "w_gu2"], ["w_o", "w_out_conv", "w_out_attn"], ["w_in"]
    group_d, group_e, group_f = ["w_down1"], ["w_gu1_lo"], ["w_gu1_hi"]
    g = {}
    dgu2, a2 = _ffn_bwd_act("ffn2_bwd_act", dy_bf, wd2, gu2)
    g["w_down2"], = _ffn_bwd_dwd("ffn2_bwd_dwd", a2, dy_bf)
    g["w_gu2"], = _ffn_bwd_dwgu("ffn2_bwd_dwgu", h3, dgu2)
    pend_a = pair_start("a", group_a, g)
    dh3, = _ffn_bwd_dh("ffn2_bwd_dh", dgu2, wts["w_gu2"], deps=(pend_a[4],))
    ring_a = chip_start("a", group_a, pend_a, dh3)
    dx2, dg_ffn2, dx2_bf = _rms_bwd("ffn2_bwd_rms", x2, g_ffn2, dh3, dy, deps=(ring_a[4],), with_bf16=True)

    dya, dyb, dgates = _mix_bwd_gates(dx2_bf, wo, ya, yb, proj, cw)
    g["w_o"] = _tn_matmul("mix_bwd_dwo", merged, dx2_bf, min(d, 512))
    g["w_out_conv"], g["w_out_attn"] = _out_proj_bwd_w(ca, o, dya, dyb, d // N_DEV)
    pend_b = pair_start("b", group_b, g)
    dca, do = _out_proj_bwd_act(dya, dyb, wts["w_out_conv"], wts["w_out_attn"], deps=(pend_b[4],))
    ring_b = chip_start("b", group_b, pend_b, do)
    d3, dconv_w = _conv_bwd(proj, conv_full, dca, deps=(ring_b[4],))
    dq, dkc, dkp, dvc, dvp, dsink = _attn_bwd(qn, kn, vb, sink_rows, do)
    dqkv, dgq, dgk = _qk_prep_bwd(proj, gq, gk, rope_tabs, dq, dkc, dkp, dvc, dvp, cw, kw)
    dproj = jnp.concatenate([d3[0], d3[1], d3[2], dqkv, dgates[0], dgates[1]], axis=1)
    g["w_in"] = _proj_bwd_w(h2, dproj)
    pend_c = pair_start("c", group_c, g)
    dh2 = _proj_bwd_act(dproj, w_in_full, deps=(pend_c[4],))
    ring_c = chip_start("c", group_c, pend_c, dh2)
    dx1, dg_mix, dx1_bf = _rms_bwd("mix_bwd_rms", x1, g_mix, dh2, dx2, deps=(ring_c[4],), with_bf16=True)

    big_out = {}
    arrived = {}

    def wait_group(tag, group, ring, after):
        send, recv, parts, lands2, _ = ring
        parts, lands2 = _chip_wait("rs_chip_wait_" + tag, send, recv, parts, lands2, after)
        arrived.update(dict(zip(group, lands2)))

    def update(n, after):
        res = _adamw_chips("adamw_" + n, arrived[n], big[n][0], big_m[n][0], big_v[n][0], row_tile(n), deps=(after,))
        big_out[n] = [a[None] for a in res]
        return res[0]

    def update_beside(n, n_tiles, step_of):
        return _adamw_side(arrived[n], big[n][0], big_m[n][0], big_v[n][0], n_tiles, step_of)

    def keep(n, res):
        big_out[n] = [a[None] for a in res]

    dgu1, a1 = _ffn_bwd_act("ffn1_bwd_act", dx1_bf, wd1, gu1)
    wait_group("a", group_a, ring_a, a1)
    g["w_down1"], *res = _ffn_bwd_dwd("ffn1_bwd_dwd", a1, dx1_bf,
                                       side=update_beside("w_down2", 11, lambda i, j, k: i * 4 + j))
    keep("w_down2", res)
    pend_d = pair_start("d", group_d, g)
    g["w_gu1_lo"], *res = _ffn_bwd_dwgu("ffn1_bwd_dwgu_lo", h1, dgu1, deps=(pend_d[4],), rows=(0, half),
                                         side=update_beside("w_gu2", 16, lambda i, j, k: i * 2 + j))
    keep("w_gu2", res)
    ring_d = chip_start("d", group_d, pend_d, g["w_gu1_lo"])
    pend_e = pair_start("e", group_e, g, deps=(ring_d[4],))
    wait_group("c", group_c, ring_c, pend_e[4])
    g["w_gu1_hi"], *res = _ffn_bwd_dwgu("ffn1_bwd_dwgu_hi", h1, dgu1, rows=(half, half),
                                         side=update_beside("w_in", 16, lambda i, j, k: i * 2 + j))
    keep("w_in", res)
    ring_e = chip_start("e", group_e, pend_e, g["w_gu1_hi"])
    pend_f = pair_start("f", group_f, g, deps=(ring_e[4],))
    wait_group("b", group_b, ring_b, pend_f[4])
    after = pend_f[4]
    for n in group_b:
        after = update(n, after)
    ring_f = chip_start("f", group_f, pend_f, after)
    wait_group("d", group_d, ring_d, ring_f[4])
    dh1, *res = _ffn_bwd_dh("ffn1_bwd_dh", dgu1, wts["w_gu1"],
                             side=update_beside("w_down1", 11, lambda i, j, k: i * 4 + k))
    keep("w_down1", res)
    grad_x, dg_ffn1 = _rms_bwd("ffn1_bwd_rms", xs, g_ffn1, dh1, dx1)
    after = grad_x
    n = "w_gu1"
    wait_group("e", group_e, ring_e, after)
    res = _adamw_chips("adamw_w_gu1_lo", arrived["w_gu1_lo"], big[n][0], big_m[n][0], big_v[n][0], row_tile(n), deps=(after,))
    wait_group("f", group_f, ring_f, res[0])
    res = _adamw_chips("adamw_w_gu1_hi", arrived["w_gu1_hi"], big[n][0], big_m[n][0], big_v[n][0], row_tile(n),
                       row0=half, into=res)
    keep(n, res)
    after = res[0]

    small = {"g_ffn1": dg_ffn1[0:1], "g_mix": dg_mix[0:1], "g_ffn2": dg_ffn2[0:1],
             "q_norm_g": dgq[0:1, :HEAD_DIM], "k_norm_g": dgk[0:1, :HEAD_DIM], "sinks": dsink[:, 0][None],
             "conv_w": dconv_w[0:CONV_K].reshape(1, -1)}
    small_w = {"g_ffn1": g_ffn1, "g_mix": g_mix, "g_ffn2": g_ffn2, "q_norm_g": q_norm_g, "k_norm_g": k_norm_g,
               "sinks": sinks, "conv_w": None}
    small_m = {"g_ffn1": m_g_ffn1, "g_mix": m_g_mix, "g_ffn2": m_g_ffn2, "q_norm_g": m_q_norm_g,
               "k_norm_g": m_k_norm_g, "sinks": m_sinks, "conv_w": m_conv_w}
    small_v = {"g_ffn1": v_g_ffn1, "g_mix": v_g_mix, "g_ffn2": v_g_ffn2, "q_norm_g": v_q_norm_g,
               "k_norm_g": v_k_norm_g, "sinks": v_sinks, "conv_w": v_conv_w}
    snames = list(small)
    widths = [small[n].shape[1] for n in snames]
    total = sum(widths)
    rows = -(-total // LANES)
    rows = -(-rows // 8) * 8

    def pack(vals):
        flat = jnp.concatenate([v.reshape(1, -1) for v in vals], axis=1)
        return jnp.pad(flat, ((0, 0), (0, rows * LANES - total))).reshape(rows, LANES)

    csh = cw // N_DEV

    def place_conv(local, fill):
        full = jnp.full((CONV_K, cw), fill, F32)
        return lax.dynamic_update_slice(full, local, (0, me * csh)).reshape(1, -1)

    pw = pack([small_w[n] if n != "conv_w" else place_conv(conv_w[0], 0.0) for n in snames])
    pm = pack([small_m[n] if n != "conv_w" else place_conv(m_conv_w[0], 0.0) for n in snames])
    pv = pack([small_v[n] if n != "conv_w" else place_conv(v_conv_w[0], 1.0) for n in snames])
    parts = _all_gather_small("gather_small_grads", pack([small[n] for n in snames]), deps=(after,))
    sg, sd, sm, sv = [a.reshape(1, -1) for a in _adamw("adamw_small", parts, pw, pm, pv, rows)]

    def unpack(flat, n):
        off = sum(widths[:snames.index(n)])
        piece = flat[:, off:off + widths[snames.index(n)]]
        if n == "conv_w":
            piece = lax.dynamic_slice(piece.reshape(CONV_K, cw), (0, me * csh), (CONV_K, csh))[None]
        return piece

    order = ["g_ffn1", "w_gu1", "w_down1", "g_mix", "w_in", "conv_w", "q_norm_g", "k_norm_g", "sinks",
             "w_out_conv", "w_out_attn", "w_o", "g_ffn2", "w_gu2", "w_down2"]
    outs = [loss, grad_x[None]]
    for idx, flat in enumerate((sg, sd, sm, sv)):
        for n in order:
            outs.append(big_out[n][idx] if n in big_out else unpack(flat, n))
    return tuple(outs)
```

```python
import jax
import jax.numpy as jnp
from jax import lax
from jax.experimental import pallas as pl
from jax.experimental.pallas import tpu as pltpu

F32 = jnp.float32
BF16 = jnp.bfloat16

N_DEV = 8
HEAD_DIM = 64
GROUP = 4
BLOCK = 128
ROT_DIM = 16
ROPE_THETA = 500000.0
RMS_EPS = 1e-6
NEG_INF = -1e30
ATTN_SCALE = HEAD_DIM ** -0.5
CONV_K = 3
LANES = 128
MXU_COLS = 256
VMEM_BYTES_V7X = 64 * 1024 * 1024
VMEM_CAP = VMEM_BYTES_V7X - 6 * 1024 * 1024

ADAM_LR = 0.001
ADAM_B1 = 0.9
ADAM_B2 = 0.999
ADAM_EPS = 1e-08
ADAM_WD = 0.01
ADAM_STEP = 10

NN = (((1,), (0,)), ((), ()))
NT = (((1,), (1,)), ((), ()))
TN = (((0,), (0,)), ((), ()))

MESH = pl.DeviceIdType.MESH


def _nbytes(shape, dtype):
    n = 1
    for s in shape:
        if s is not None:
            n *= s
    return n * jnp.dtype(dtype).itemsize


def _params(semantics, block_bytes, temp_bytes):
    assert 2 * block_bytes + temp_bytes <= VMEM_CAP, (block_bytes, temp_bytes)
    return pltpu.CompilerParams(dimension_semantics=semantics, vmem_limit_bytes=VMEM_CAP)


def _fused(name, grid, ins, outs, dots, epilogue, *, nk=1, acc_shape=None, temp_bytes=0,
           semantics=("parallel", "parallel", "arbitrary"), deps=(), side=None):
    n_main_in, n_main_out = len(ins), len(outs)
    if side is not None:
        ins, outs = list(ins) + list(side[0]), list(outs) + list(side[1])
    n_in, n_out = len(ins), len(outs)
    n_dep = len(deps)

    def body(*refs):
        in_refs, out_refs = refs[:n_in], refs[n_in + n_dep:n_in + n_dep + n_out]
        scratch = refs[n_in + n_dep + n_out:]
        if side is not None:
            side[2](in_refs[n_main_in:], out_refs[n_main_out:])

        def products():
            if callable(dots):
                return dots(in_refs)
            total = None
            for ai, bi, contract in dots:
                a, b = in_refs[ai][...], in_refs[bi][...]
                a = a if a.dtype == BF16 else a.astype(BF16)
                b = b if b.dtype == BF16 else b.astype(BF16)
                p = lax.dot_general(a, b, contract, preferred_element_type=F32)
                total = p if total is None else total + p
            return total

        if nk == 1:
            epilogue(products() if dots else None, in_refs, out_refs)
        else:
            acc = scratch[0]
            k = pl.program_id(2)

            @pl.when(k == 0)
            def _():
                acc[...] = jnp.zeros_like(acc)

            acc[...] += products()

            @pl.when(k == nk - 1)
            def _():
                epilogue(acc[...], in_refs, out_refs)

    block_bytes = sum(_nbytes(spec.block_shape, a.dtype) for a, spec in ins)
    block_bytes += sum(_nbytes(spec.block_shape, s.dtype) for s, spec in outs)
    scratch_shapes = []
    if nk > 1:
        scratch_shapes.append(pltpu.VMEM(acc_shape, F32))
        temp_bytes += _nbytes(acc_shape, F32)
    res = pl.pallas_call(
        body, name=name, grid=grid,
        in_specs=[spec for _, spec in ins] + [pl.BlockSpec(memory_space=pl.ANY)] * n_dep,
        out_specs=[spec for _, spec in outs],
        out_shape=[s for s, _ in outs],
        scratch_shapes=scratch_shapes,
        compiler_params=_params(semantics, block_bytes, temp_bytes),
    )(*[a for a, _ in ins], *deps)
    return res


def _sds(shape, dtype):
    return jax.ShapeDtypeStruct(shape, dtype)


def _sigmoid(x):
    return jax.nn.sigmoid(x)


def _all_gather_small(name, shard, deps=()):
    n_dep = len(deps)

    def body(src, *rest):
        dst, send_sems, recv_sems, local_sem = rest[n_dep:]
        x, y, c = lax.axis_index("x"), lax.axis_index("y"), lax.axis_index("c")
        me = 4 * x + 2 * y + c
        copies = [pltpu.make_async_copy(src, dst.at[me], local_sem)]
        for k in range(1, N_DEV):
            peer = ((1 - x) if (k & 4) else x, (1 - y) if (k & 2) else y, (1 - c) if (k & 1) else c)
            copies.append(pltpu.make_async_remote_copy(
                src_ref=src, dst_ref=dst.at[me], send_sem=send_sems.at[k - 1], recv_sem=recv_sems.at[k - 1],
                device_id=peer, device_id_type=MESH))
        for cp in copies:
            cp.start()
        for cp in copies:
            cp.wait()

    hbm = pl.BlockSpec(memory_space=pltpu.HBM)
    return pl.pallas_call(
        body, name=name,
        in_specs=[hbm] + [pl.BlockSpec(memory_space=pl.ANY)] * n_dep, out_specs=hbm,
        out_shape=_sds((N_DEV,) + shard.shape, shard.dtype),
        scratch_shapes=[pltpu.SemaphoreType.DMA((N_DEV - 1,)), pltpu.SemaphoreType.DMA((N_DEV - 1,)),
                        pltpu.SemaphoreType.DMA],
    )(shard, *deps)


_HBM = pl.BlockSpec(memory_space=pltpu.HBM)
_SEM = pl.BlockSpec(memory_space=pltpu.SEMAPHORE)
_ANY = pl.BlockSpec(memory_space=pl.ANY)
_EFFECT = pltpu.SideEffectType.DATAFLOW_SIDE_EFFECTING
N_TARGETS = 4


def _mesh_pos():
    return lax.axis_index("x"), lax.axis_index("y"), lax.axis_index("c")


def _chip_peers(x, y, c):
    return [(1 - x, y, c), (x, 1 - y, c), (1 - x, 1 - y, c)]


def _dev_index(pos):
    return 4 * pos[0] + 2 * pos[1] + pos[2]


def _hbm_like(a):
    return pltpu.HBM(a.shape, a.dtype)


def _place_shard(name, w, out_dtype, me, tr, deps=()):
    r, c = w.shape
    n_dep = len(deps)

    def body(me_ref, w_ref, *rest):
        rest[n_dep][...] = w_ref[...].astype(out_dtype)

    grid_spec = pltpu.PrefetchScalarGridSpec(
        num_scalar_prefetch=1, grid=(r // tr,),
        in_specs=[pl.BlockSpec((tr, c), lambda i, me_ref: (i, 0))] + [_ANY] * n_dep,
        out_specs=pl.BlockSpec((None, tr, c), lambda i, me_ref: (me_ref[0], i, 0)))
    return pl.pallas_call(
        body, name=name, grid_spec=grid_spec, out_shape=_sds((N_DEV, r, c), out_dtype),
        compiler_params=_params(("parallel",), tr * c * 6, tr * c * 4),
    )(me, w, *deps)


def _gather_start(name, lands):
    n = len(lands)

    def body(*refs):
        bufs = refs[:n]
        send, recv = refs[n], refs[n + 1]
        token = refs[-1]
        x, y, c = _mesh_pos()
        me = _dev_index((x, y, c))
        targets = [(x, y, 1 - c)] + _chip_peers(x, y, c)
        for w in range(n):
            for k, to in enumerate(targets):
                pltpu.make_async_remote_copy(
                    src_ref=bufs[w].at[me], dst_ref=bufs[w].at[me],
                    send_sem=send.at[N_TARGETS * w + k], recv_sem=recv.at[N_TARGETS * w + k],
                    device_id=to, device_id_type=MESH).start()
        token[...] = jnp.zeros_like(token)

    sems = pltpu.SemaphoreType.DMA((N_TARGETS * n,))
    outs = pl.pallas_call(
        body, name=name,
        in_specs=[_HBM] * n, out_specs=[_SEM, _SEM] + [_HBM] * n + [_token_spec()],
        out_shape=[sems, sems] + [_hbm_like(a) for a in lands] + [_sds((8, LANES), F32)],
        input_output_aliases={i: 2 + i for i in range(n)},
        compiler_params=pltpu.CompilerParams(has_side_effects=_EFFECT),
    )(*lands)
    return outs[0], outs[1], list(outs[2:2 + n]), outs[-1]


def _gather_wait(name, positions, send, recv, lands, after):
    m = len(positions)

    def body(*refs):
        bufs = refs[:m]
        send_sems, recv_sems = refs[m], refs[m + 1]
        x, y, c = _mesh_pos()
        me = _dev_index((x, y, c))
        sources = [(x, y, 1 - c)] + _chip_peers(x, y, c)
        for j, w in enumerate(positions):
            for k, frm in enumerate(sources):
                cp = pltpu.make_async_remote_copy(
                    src_ref=bufs[j].at[me], dst_ref=bufs[j].at[_dev_index(frm)],
                    send_sem=send_sems.at[N_TARGETS * w + k], recv_sem=recv_sems.at[N_TARGETS * w + k],
                    device_id=frm, device_id_type=MESH)
                cp.wait_send()
                cp.wait_recv()

    outs = pl.pallas_call(
        body, name=name,
        in_specs=[_HBM] * m + [_SEM, _SEM, _ANY], out_specs=[_HBM] * m,
        out_shape=[_hbm_like(a) for a in lands],
        input_output_aliases={i: i for i in range(m)},
        compiler_params=pltpu.CompilerParams(has_side_effects=_EFFECT),
    )(*lands, send, recv, after)
    return list(outs)


def _forward_to_sibling(name, lands):
    m = len(lands)

    def body(*refs):
        copies = _forward_copies(refs[m:2 * m], refs[2 * m], refs[2 * m + 1])
        for cp in copies:
            cp.start()
        for cp in copies:
            cp.wait()

    outs = pl.pallas_call(
        body, name=name,
        in_specs=[_HBM] * m, out_specs=[_HBM] * m,
        out_shape=[_sds(a.shape, a.dtype) for a in lands],
        input_output_aliases={i: i for i in range(m)},
        scratch_shapes=[pltpu.SemaphoreType.DMA((3 * m,)), pltpu.SemaphoreType.DMA((3 * m,))],
    )(*lands)
    return list(outs)


def _forward_copies(bufs, send, recv):
    x, y, c = _mesh_pos()
    copies = []
    for j, buf in enumerate(bufs):
        for k, chip in enumerate(_chip_peers(x, y, c)):
            block = buf.at[_dev_index(chip)]
            copies.append(pltpu.make_async_remote_copy(
                src_ref=block, dst_ref=block, send_sem=send.at[3 * j + k], recv_sem=recv.at[3 * j + k],
                device_id=(x, y, 1 - c), device_id_type=MESH))
    return copies


def _forward_start(name, lands):
    m = len(lands)

    def body(*refs):
        for cp in _forward_copies(refs[:m], refs[m], refs[m + 1]):
            cp.start()

    sems = pltpu.SemaphoreType.DMA((3 * m,))
    outs = pl.pallas_call(
        body, name=name,
        in_specs=[_HBM] * m, out_specs=[_SEM, _SEM] + [_HBM] * m,
        out_shape=[sems, sems] + [_hbm_like(a) for a in lands],
        input_output_aliases={i: 2 + i for i in range(m)},
        compiler_params=pltpu.CompilerParams(has_side_effects=_EFFECT),
    )(*lands)
    return outs[0], outs[1], list(outs[2:])


def _forward_wait(name, send, recv, lands, after):
    m = len(lands)

    def body(*refs):
        for cp in _forward_copies(refs[:m], refs[m], refs[m + 1]):
            cp.wait_send()
            cp.wait_recv()

    outs = pl.pallas_call(
        body, name=name,
        in_specs=[_HBM] * m + [_SEM, _SEM, _ANY], out_specs=[_HBM] * m,
        out_shape=[_hbm_like(a) for a in lands],
        input_output_aliases={i: i for i in range(m)},
        compiler_params=pltpu.CompilerParams(has_side_effects=_EFFECT),
    )(*lands, send, recv, after)
    return list(outs)


def _token_spec():
    return pl.BlockSpec(memory_space=pltpu.VMEM)


def _pair_start(name, stacks, lands, deps=()):
    n = len(stacks)
    n_dep = len(deps)

    def body(*refs):
        srcs, dsts = refs[:n], refs[n:2 * n]
        send, recv = refs[2 * n + n_dep], refs[2 * n + n_dep + 1]
        token = refs[-1]
        x, y, c = _mesh_pos()
        for w in range(n):
            for chip in range(4):
                pltpu.make_async_remote_copy(
                    src_ref=srcs[w].at[chip, 1 - c], dst_ref=dsts[w].at[chip],
                    send_sem=send.at[4 * w + chip], recv_sem=recv.at[4 * w + chip],
                    device_id=(x, y, 1 - c), device_id_type=MESH).start()
        token[...] = jnp.zeros_like(token)

    sems = pltpu.SemaphoreType.DMA((4 * n,))
    outs = pl.pallas_call(
        body, name=name,
        in_specs=[_HBM] * (2 * n) + [_ANY] * n_dep, out_specs=[_SEM, _SEM] + [_HBM] * (2 * n) + [_token_spec()],
        out_shape=[sems, sems] + [_hbm_like(a) for a in stacks] + [_hbm_like(a) for a in lands] + [_sds((8, LANES), F32)],
        input_output_aliases={i: 2 + i for i in range(2 * n)},
        compiler_params=pltpu.CompilerParams(has_side_effects=_EFFECT),
    )(*stacks, *lands, *deps)
    return outs[0], outs[1], list(outs[2:2 + n]), list(outs[2 + n:2 + 2 * n]), outs[-1]


def _pair_wait(name, send, recv, stacks, lands, after):
    n = len(stacks)

    def body(*refs):
        srcs, dsts = refs[:n], refs[n:2 * n]
        send_sems, recv_sems = refs[2 * n], refs[2 * n + 1]
        x, y, c = _mesh_pos()
        for w in range(n):
            for chip in range(4):
                cp = pltpu.make_async_remote_copy(
                    src_ref=srcs[w].at[chip, 1 - c], dst_ref=dsts[w].at[chip],
                    send_sem=send_sems.at[4 * w + chip], recv_sem=recv_sems.at[4 * w + chip],
                    device_id=(x, y, 1 - c), device_id_type=MESH)
                cp.wait_send()
                cp.wait_recv()

    outs = pl.pallas_call(
        body, name=name,
        in_specs=[_HBM] * (2 * n) + [_SEM, _SEM, _ANY], out_specs=[_HBM] * (2 * n),
        out_shape=[_hbm_like(a) for a in stacks] + [_hbm_like(a) for a in lands],
        input_output_aliases={i: i for i in range(2 * n)},
        compiler_params=pltpu.CompilerParams(has_side_effects=_EFFECT),
    )(*stacks, *lands, send, recv, after)
    return list(outs[:n]), list(outs[n:])


def _pair_add(name, stack, land, place, tr):
    _, _, r, c = stack.shape

    def body(place_ref, a_ref, b_ref, sums_ref, slots_ref):
        total = (a_ref[...].astype(F32) + b_ref[...].astype(F32)).astype(BF16)
        sums_ref[...] = total

        @pl.when(pl.program_id(1) == place_ref[1])
        def _():
            slots_ref[...] = total

    grid_spec = pltpu.PrefetchScalarGridSpec(
        num_scalar_prefetch=1, grid=(r // tr, 4),
        in_specs=[pl.BlockSpec((None, None, tr, c), lambda i, k, place_ref: (k, place_ref[0], i, 0)),
                  pl.BlockSpec((None, tr, c), lambda i, k, place_ref: (k, i, 0))],
        out_specs=[pl.BlockSpec((None, tr, c), lambda i, k, place_ref: (k, i, 0)),
                   pl.BlockSpec((None, tr, c), lambda i, k, place_ref: (place_ref[1], i, 0))])
    return pl.pallas_call(
        body, name=name, grid_spec=grid_spec, out_shape=[_sds((4, r, c), BF16)] * 2,
        compiler_params=_params(("parallel", "arbitrary"), 4 * tr * c * 2, 3 * tr * c * 4),
    )(place, stack, land)


def _chip_start(name, parts, lands):
    n = len(parts)

    def body(*refs):
        srcs, dsts = refs[:n], refs[n:2 * n]
        send, recv = refs[2 * n], refs[2 * n + 1]
        token = refs[-1]
        x, y, c = _mesh_pos()
        for w in range(n):
            for k, to in enumerate(_chip_peers(x, y, c)):
                pltpu.make_async_remote_copy(
                    src_ref=srcs[w].at[2 * to[0] + to[1]], dst_ref=dsts[w].at[2 * x + y],
                    send_sem=send.at[3 * w + k], recv_sem=recv.at[3 * w + k],
                    device_id=to, device_id_type=MESH).start()
        token[...] = jnp.zeros_like(token)

    sems = pltpu.SemaphoreType.DMA((3 * n,))
    outs = pl.pallas_call(
        body, name=name,
        in_specs=[_HBM] * (2 * n), out_specs=[_SEM, _SEM] + [_HBM] * (2 * n) + [_token_spec()],
        out_shape=[sems, sems] + [_hbm_like(a) for a in parts] + [_hbm_like(a) for a in lands] + [_sds((8, LANES), F32)],
        input_output_aliases={i: 2 + i for i in range(2 * n)},
        compiler_params=pltpu.CompilerParams(has_side_effects=_EFFECT),
    )(*parts, *lands)
    return outs[0], outs[1], list(outs[2:2 + n]), list(outs[2 + n:2 + 2 * n]), outs[-1]


def _chip_wait(name, send, recv, parts, lands, after):
    n = len(parts)

    def body(*refs):
        srcs, dsts = refs[:n], refs[n:2 * n]
        send_sems, recv_sems = refs[2 * n], refs[2 * n + 1]
        x, y, c = _mesh_pos()
        for w in range(n):
            for k, frm in enumerate(_chip_peers(x, y, c)):
                chip = 2 * frm[0] + frm[1]
                cp = pltpu.make_async_remote_copy(
                    src_ref=srcs[w].at[chip], dst_ref=dsts[w].at[chip],
                    send_sem=send_sems.at[3 * w + k], recv_sem=recv_sems.at[3 * w + k],
                    device_id=frm, device_id_type=MESH)
                cp.wait_send()
                cp.wait_recv()

    outs = pl.pallas_call(
        body, name=name,
        in_specs=[_HBM] * (2 * n) + [_SEM, _SEM, _ANY], out_specs=[_HBM] * (2 * n),
        out_shape=[_hbm_like(a) for a in parts] + [_hbm_like(a) for a in lands],
        input_output_aliases={i: i for i in range(2 * n)},
        compiler_params=pltpu.CompilerParams(has_side_effects=_EFFECT),
    )(*parts, *lands, send, recv, after)
    return list(outs[:n]), list(outs[n:])


def _row_tile(t):
    return min(t, 256)


def _rms_fwd(name, x, g):
    t, d = x.shape
    tm = _row_tile(t)

    def epilogue(_, ins, outs):
        xv = ins[0][...]
        r = lax.rsqrt(jnp.mean(xv * xv, axis=-1, keepdims=True) + RMS_EPS)
        outs[0][...] = (xv * r * ins[1][...]).astype(BF16)

    row = pl.BlockSpec((tm, d), lambda i, j, k: (i, 0))
    vec = pl.BlockSpec((1, d), lambda i, j, k: (0, 0))
    return _fused(name, (t // tm, 1, 1), [(x, row), (g, vec)], [(_sds((t, d), BF16), row)], [], epilogue,
                  temp_bytes=4 * tm * d * 4)[0]


def _rms_bwd(name, x, g, dh, resid, deps=(), with_bf16=False):
    t, d = x.shape
    tm = _row_tile(t)

    def epilogue(_, ins, outs):
        xv, gv, dhv = ins[0][...], ins[1][...], ins[2][...]
        r = lax.rsqrt(jnp.mean(xv * xv, axis=-1, keepdims=True) + RMS_EPS)
        xh = xv * r
        u = dhv * gv
        dot = jnp.mean(u * xh, axis=-1, keepdims=True)
        dx = ins[3][...] + r * (u - xh * dot)
        outs[0][...] = dx
        if with_bf16:
            outs[2][...] = dx.astype(BF16)

        @pl.when(pl.program_id(0) == 0)
        def _():
            outs[1][...] = jnp.zeros_like(outs[1])

        outs[1][0:1, :] += jnp.sum(dhv * xh, axis=0, keepdims=True)

    row = pl.BlockSpec((tm, d), lambda i, j, k: (i, 0))
    vec = pl.BlockSpec((1, d), lambda i, j, k: (0, 0))
    acc = pl.BlockSpec((8, d), lambda i, j, k: (0, 0))
    outs = [(_sds((t, d), F32), row), (_sds((8, d), F32), acc)] + ([(_sds((t, d), BF16), row)] if with_bf16 else [])
    return _fused(name, (t // tm, 1, 1), [(x, row), (g, vec), (dh, row), (resid, row)], outs, [], epilogue,
                  temp_bytes=6 * tm * d * 4, semantics=("arbitrary", "arbitrary", "arbitrary"), deps=deps)


def _ffn_up(name, h, wgu, parity=None, into=None):
    t, d = h.shape
    nb = wgu.shape[2]
    f = 4 * nb
    tm = min(t, 512)

    def body(h_ref, wg_ref, wu_ref, gu_ref, a_ref):
        hv = h_ref[...]
        for c0 in range(0, nb, MXU_COLS):
            cs = slice(c0, min(c0 + MXU_COLS, nb))
            g = jnp.dot(hv, wg_ref[:, cs], preferred_element_type=F32)
            u = jnp.dot(hv, wu_ref[:, cs], preferred_element_type=F32)
            gu_ref[0, :, cs] = g.astype(BF16)
            gu_ref[1, :, cs] = u.astype(BF16)
            a_ref[:, cs] = (g * _sigmoid(g) * u).astype(BF16)

    blocks = tm * d * 2 + 2 * d * nb * 2 + 3 * tm * nb * 2
    params = _params(("parallel", "parallel"), blocks, 8 * tm * MXU_COLS * 4)
    out_shape = [_sds((2, t, f), BF16), _sds((t, f), BF16)]
    if parity is None:
        return pl.pallas_call(
            body, name=name, grid=(4, t // tm),
            in_specs=[pl.BlockSpec((tm, d), lambda j, i: (i, 0)),
                      pl.BlockSpec((None, d, nb), lambda j, i: (j, 0, 0)),
                      pl.BlockSpec((None, d, nb), lambda j, i: (j + 4, 0, 0))],
            out_specs=[pl.BlockSpec((2, tm, nb), lambda j, i: (0, i, j)),
                       pl.BlockSpec((tm, nb), lambda j, i: (i, j))],
            out_shape=out_shape, compiler_params=params,
        )(h, wgu, wgu)

    def half_body(parity_ref, h_ref, wg_ref, wu_ref, *rest):
        body(h_ref, wg_ref, wu_ref, rest[-2], rest[-1])

    n_pass = 0 if into is None else 2
    grid_spec = pltpu.PrefetchScalarGridSpec(
        num_scalar_prefetch=1, grid=(2, t // tm),
        in_specs=[pl.BlockSpec((tm, d), lambda jj, i, p: (i, 0)),
                  pl.BlockSpec((None, d, nb), lambda jj, i, p: (2 * jj + p[0], 0, 0)),
                  pl.BlockSpec((None, d, nb), lambda jj, i, p: (2 * jj + p[0] + 4, 0, 0))] + [_ANY] * n_pass,
        out_specs=[pl.BlockSpec((2, tm, nb), lambda jj, i, p: (0, i, 2 * jj + p[0])),
                   pl.BlockSpec((tm, nb), lambda jj, i, p: (i, 2 * jj + p[0]))])
    return pl.pallas_call(
        half_body, name=name, grid_spec=grid_spec, out_shape=out_shape,
        input_output_aliases={} if into is None else {4: 0, 5: 1}, compiler_params=params,
    )(parity, h, wgu, wgu, *(into or ()))


def _ffn_down(name, a, wd, x, target=None):
    t, f = a.shape
    d = wd.shape[1]
    tm = min(t, 512)
    tn = min(d, 1024)
    blk = pl.BlockSpec((tm, tn), lambda j, i, k: (i, j))
    ins = [(a, pl.BlockSpec((tm, f), lambda j, i, k: (i, 0))), (wd, pl.BlockSpec((f, tn), lambda j, i, k: (0, j))), (x, blk)]

    if target is None:
        def epilogue(acc, ins, outs):
            outs[0][...] = ins[2][...] + 0.5 * acc

        return _fused(name, (d // tn, t // tm, 1), ins, [(_sds((t, d), F32), blk)],
                      [(0, 1, NN)], epilogue, temp_bytes=2 * tm * tn * 4)[0]

    def epilogue(acc, ins, outs):
        e = ins[2][...] + 0.5 * acc - ins[3][...]
        outs[0][...] = e * (1.0 / d)
        outs[2][...] = (e * (1.0 / d)).astype(BF16)

        @pl.when((pl.program_id(0) == 0) & (pl.program_id(1) == 0))
        def _():
            outs[1][...] = jnp.zeros_like(outs[1])

        part = jnp.sum(jnp.sum(e * e, axis=1, keepdims=True), axis=0, keepdims=True)
        outs[1][...] += jnp.broadcast_to(part, outs[1].shape)

    return _fused(name, (d // tn, t // tm, 1), ins + [(target, blk)],
                  [(_sds((t, d), F32), blk), (_sds((8, LANES), F32), pl.BlockSpec((8, LANES), lambda j, i, k: (0, 0))),
                   (_sds((t, d), BF16), blk)],
                  [(0, 1, NN)], epilogue, temp_bytes=3 * tm * tn * 4,
                  semantics=("arbitrary", "arbitrary", "arbitrary"))


def _ffn_bwd_act(name, dy, wd, gu, deps=()):
    t, d = dy.shape
    f = wd.shape[0]
    nb = f // 4
    tm = min(t, 512)

    def body(dy_ref, wd_ref, gu_ref, *rest):
        dgu_ref, a_ref = rest[-2], rest[-1]
        dyv = dy_ref[...].astype(BF16)
        for c0 in range(0, nb, MXU_COLS):
            cs = slice(c0, min(c0 + MXU_COLS, nb))
            da = 0.5 * lax.dot_general(dyv, wd_ref[cs, :], NT, preferred_element_type=F32)
            g = gu_ref[0, :, cs].astype(F32)
            u = gu_ref[1, :, cs].astype(F32)
            s = _sigmoid(g)
            silu = g * s
            dgu_ref[0, :, cs] = (da * u * (s * (1.0 + g * (1.0 - s)))).astype(BF16)
            dgu_ref[1, :, cs] = (da * silu).astype(BF16)
            a_ref[:, cs] = (silu * u).astype(BF16)

    blocks = tm * d * 4 + nb * d * 2 + 5 * tm * nb * 2
    return pl.pallas_call(
        body, name=name, grid=(4, t // tm),
        in_specs=[pl.BlockSpec((tm, d), lambda j, i: (i, 0)),
                  pl.BlockSpec((nb, d), lambda j, i: (j, 0)),
                  pl.BlockSpec((2, tm, nb), lambda j, i: (0, i, j))] + [_ANY] * len(deps),
        out_specs=[pl.BlockSpec((2, tm, nb), lambda j, i: (0, i, j)), pl.BlockSpec((tm, nb), lambda j, i: (i, j))],
        out_shape=[_sds((2, t, f), BF16), _sds((t, f), BF16)],
        compiler_params=_params(("parallel", "parallel"), blocks, tm * d * 2 + 8 * tm * MXU_COLS * 4),
    )(dy, wd, gu, *deps)


def _ffn_bwd_dwd(name, a, dy, deps=(), side=None):
    t, f = a.shape
    d = dy.shape[1]
    tm = f // 4
    tn = min(d, 512)

    def epilogue(acc, ins, outs):
        outs[0][...] = (0.5 * acc).astype(BF16)

    return _fused(name, (4, d // tn, 1),
                  [(a, pl.BlockSpec((t, tm), lambda i, j, k: (0, i))),
                   (dy, pl.BlockSpec((t, tn), lambda i, j, k: (0, j)))],
                  [(_sds((f, d), BF16), pl.BlockSpec((tm, tn), lambda i, j, k: (i, j)))],
                  [(0, 1, TN)], epilogue, temp_bytes=t * tn * 2 + 2 * tm * tn * 4, deps=deps, side=side)


def _ffn_bwd_dh(name, dgu, wgu, deps=(), side=None):
    _, t, f = dgu.shape
    d, nb = wgu.shape[1], wgu.shape[2]
    tm = min(t, 512)

    def products(ins):
        return (lax.dot_general(ins[0][:, 0:nb], ins[1][0], NT, preferred_element_type=F32)
                + lax.dot_general(ins[0][:, nb:2 * nb], ins[1][1], NT, preferred_element_type=F32))

    def epilogue(acc, ins, outs):
        outs[0][...] = acc

    return _fused(name, (t // tm, 1, 4),
                  [(dgu, pl.BlockSpec((None, tm, 2 * nb), lambda i, j, k: (k // 2, i, k % 2))),
                   (wgu, pl.BlockSpec((2, d, nb), lambda i, j, k: (k, 0, 0)))],
                  [(_sds((t, d), F32), pl.BlockSpec((tm, d), lambda i, j, k: (i, 0)))],
                  products, epilogue, nk=4, acc_shape=(tm, d), temp_bytes=tm * d * 4, deps=deps, side=side)


def _ffn_bwd_dwgu(name, h, dgu, deps=(), side=None, rows=None):
    t, d = h.shape
    nb = dgu.shape[2] // 4
    tm = min(d, 512)
    row0, nrows = rows if rows is not None else (0, d)
    j0 = row0 // tm

    def epilogue(acc, ins, outs):
        outs[0][...] = acc.astype(BF16)

    return _fused(name, (N_DEV, nrows // tm, 1),
                  [(h, pl.BlockSpec((t, tm), lambda i, j, k: (0, j0 + j))),
                   (dgu, pl.BlockSpec((None, t, nb), lambda i, j, k: (i // 4, 0, i % 4)))],
                  [(_sds((N_DEV, nrows, nb), BF16), pl.BlockSpec((None, tm, nb), lambda i, j, k: (i, j, 0)))],
                  [(0, 1, TN)], epilogue, temp_bytes=2 * tm * nb * 4, deps=deps, side=side)


def _proj(h, w_in):
    t, d = h.shape
    nb = w_in.shape[3]
    tm = min(t, 512)

    def body(h_ref, w_ref, o_ref):
        hv = h_ref[...]
        o_ref[:, 0:nb] = jnp.dot(hv, w_ref[0], preferred_element_type=F32).astype(BF16)
        o_ref[:, nb:2 * nb] = jnp.dot(hv, w_ref[1], preferred_element_type=F32).astype(BF16)

    blocks = tm * d * 2 + 2 * d * nb * 2 + tm * 2 * nb * 4
    return pl.pallas_call(
        body, name="mix_proj", grid=(4, t // tm),
        in_specs=[pl.BlockSpec((tm, d), lambda j, i: (i, 0)),
                  pl.BlockSpec((None, 2, d, nb), lambda j, i: (j, 0, 0, 0))],
        out_specs=pl.BlockSpec((tm, 2 * nb), lambda j, i: (i, j)),
        out_shape=_sds((t, N_DEV * nb), BF16),
        compiler_params=_params(("parallel", "parallel"), blocks, 2 * tm * nb * 4),
    )(h, w_in)


def _shift_rows(u, k):
    t = u.shape[0]
    rolled = pltpu.roll(u, k % t, axis=0)
    row = lax.broadcasted_iota(jnp.int32, u.shape, 0)
    keep = (row >= k) if k > 0 else (row < t + k)
    return jnp.where(keep, rolled, 0.0)


def _conv_fwd(proj, conv_w):
    t = proj.shape[0]
    cw = conv_w.shape[1]
    tc = min(cw, 256)
    nc = cw // tc

    def epilogue(_, ins, outs):
        u = ins[2][...].astype(F32) * ins[0][...].astype(F32)
        w = ins[3][...]
        y = u * w[2:3, :] + _shift_rows(u, 1) * w[1:2, :] + _shift_rows(u, 2) * w[0:1, :]
        outs[0][...] = (ins[1][...].astype(F32) * y).astype(BF16)

    def col(seg):
        return pl.BlockSpec((t, tc), lambda i, j, k: (0, seg * nc + i))

    return _fused("conv_fwd", (nc, 1, 1),
                  [(proj, col(0)), (proj, col(1)), (proj, col(2)),
                   (conv_w, pl.BlockSpec((8, tc), lambda i, j, k: (0, i)))],
                  [(_sds((t, cw), BF16), pl.BlockSpec((t, tc), lambda i, j, k: (0, i)))],
                  [], epilogue, temp_bytes=6 * t * tc * 4)[0]


def _conv_bwd(proj, conv_w, dca, deps=()):
    t = proj.shape[0]
    cw = conv_w.shape[1]
    tc = min(cw, 256)
    nc = cw // tc

    def epilogue(_, ins, outs):
        xc, bg, cg = ins[0][...].astype(F32), ins[1][...].astype(F32), ins[2][...].astype(F32)
        w, dc = ins[3][...], ins[4][...]
        u = cg * xc
        u1, u2 = _shift_rows(u, 1), _shift_rows(u, 2)
        y = u * w[2:3, :] + u1 * w[1:2, :] + u2 * w[0:1, :]
        dconv = dc * bg
        du = dconv * w[2:3, :] + _shift_rows(dconv, -1) * w[1:2, :] + _shift_rows(dconv, -2) * w[0:1, :]
        outs[0][0] = (du * cg).astype(BF16)
        outs[0][1] = (dc * y).astype(BF16)
        outs[0][2] = (du * xc).astype(BF16)
        outs[1][...] = jnp.zeros_like(outs[1])
        outs[1][0:1, :] = jnp.sum(dconv * u2, axis=0, keepdims=True)
        outs[1][1:2, :] = jnp.sum(dconv * u1, axis=0, keepdims=True)
        outs[1][2:3, :] = jnp.sum(dconv * u, axis=0, keepdims=True)

    def col(seg):
        return pl.BlockSpec((t, tc), lambda i, j, k: (0, seg * nc + i))

    own = pl.BlockSpec((t, tc), lambda i, j, k: (0, i))
    wspec = pl.BlockSpec((8, tc), lambda i, j, k: (0, i))
    return _fused("conv_bwd", (nc, 1, 1),
                  [(proj, col(0)), (proj, col(1)), (proj, col(2)), (conv_w, wspec), (dca, own)],
                  [(_sds((3, t, cw), BF16), pl.BlockSpec((3, t, tc), lambda i, j, k: (0, 0, i))),
                   (_sds((8, cw), F32), wspec)],
                  [], epilogue, temp_bytes=10 * t * tc * 4, deps=deps)


def _split3(x):
    hi = x.astype(BF16)
    r1 = x - hi.astype(F32)
    mid = r1.astype(BF16)
    lo = (r1 - mid.astype(F32)).astype(BF16)
    return hi, mid, lo


def _head_selector(width):
    r = lax.broadcasted_iota(jnp.int32, (width, LANES), 0)
    c = lax.broadcasted_iota(jnp.int32, (width, LANES), 1)
    return (lax.shift_right_logical(r, 6) == c).astype(BF16)


def _head_sum(x, sel):
    return sum(jnp.dot(p, sel, preferred_element_type=F32) for p in _split3(x))


def _head_bcast(r, sel):
    return sum(lax.dot_general(p, sel, NT, preferred_element_type=F32) for p in _split3(r))


def _rope(x, c, sa, sb):
    n = x.shape[1]
    return x * c + pltpu.roll(x, n - ROT_DIM // 2, axis=1) * sa + pltpu.roll(x, ROT_DIM // 2, axis=1) * sb


def _rope_t(d, c, sa, sb):
    n = d.shape[1]
    return d * c + pltpu.roll(d * sa, ROT_DIM // 2, axis=1) + pltpu.roll(d * sb, n - ROT_DIM // 2, axis=1)


def _tile_lanes(tab, width):
    return tab if width == tab.shape[1] else jnp.tile(tab, (1, width // tab.shape[1]))


def _qk_prep(proj, gq, gk, rope_tabs, cw, kw):
    t = proj.shape[0]
    tm = _row_tile(t)

    def epilogue(_, ins, outs):
        c, sa, sb = ins[5][...], ins[6][...], ins[7][...]
        for src, gain, dst, width in ((0, 3, 0, cw), (1, 4, 1, kw)):
            xv = ins[src][...].astype(F32)
            sel = _head_selector(width)
            r = lax.rsqrt(_head_sum(xv * xv, sel) * (1.0 / HEAD_DIM) + RMS_EPS)
            xn = xv * _head_bcast(r, sel) * ins[gain][...]
            outs[dst][...] = _rope(xn, _tile_lanes(c, width), _tile_lanes(sa, width), _tile_lanes(sb, width)).astype(BF16)
        outs[2][...] = ins[2][...].astype(BF16)

    kblk = cw // kw
    tab = pl.BlockSpec((tm, LANES), lambda i, j, k: (i, 0))
    kspec = pl.BlockSpec((tm, kw), lambda i, j, k: (i, 0))
    return _fused("qk_prep", (t // tm, 1, 1),
                  [(proj, pl.BlockSpec((tm, cw), lambda i, j, k: (i, 3))),
                   (proj, pl.BlockSpec((tm, kw), lambda i, j, k: (i, 4 * kblk))),
                   (proj, pl.BlockSpec((tm, kw), lambda i, j, k: (i, 4 * kblk + 1))),
                   (gq, pl.BlockSpec((1, cw), lambda i, j, k: (0, 0))),
                   (gk, pl.BlockSpec((1, kw), lambda i, j, k: (0, 0))),
                   (rope_tabs[0], tab), (rope_tabs[1], tab), (rope_tabs[2], tab)],
                  [(_sds((t, cw), BF16), pl.BlockSpec((tm, cw), lambda i, j, k: (i, 0))),
                   (_sds((t, kw), BF16), kspec), (_sds((t, kw), BF16), kspec)],
                  [], epilogue, temp_bytes=12 * tm * cw * 4)


def _qk_prep_bwd(proj, gq, gk, rope_tabs, dq, dkc, dkp, dvc, dvp, cw, kw):
    t = proj.shape[0]
    tm = BLOCK
    nblk = t // tm

    def epilogue(_, ins, outs):
        c, sa, sb = ins[5][...], ins[6][...], ins[7][...]
        has_next = (pl.program_id(0) < nblk - 1).astype(F32)
        dk = ins[9][...] + has_next * ins[10][...]
        dv = ins[11][...] + has_next * ins[12][...]
        pieces = []
        for src, gain, dval, dst, width in ((0, 3, ins[8][...], 1, cw), (1, 4, dk, 2, kw)):
            xv, gv = ins[src][...].astype(F32), ins[gain][...]
            sel = _head_selector(width)
            r = _head_bcast(lax.rsqrt(_head_sum(xv * xv, sel) * (1.0 / HEAD_DIM) + RMS_EPS), sel)
            xh = xv * r
            dxn = _rope_t(dval, _tile_lanes(c, width), _tile_lanes(sa, width), _tile_lanes(sb, width))
            u = dxn * gv
            dot = _head_bcast(_head_sum(u * xh, sel), sel) * (1.0 / HEAD_DIM)
            pieces.append((r * (u - xh * dot)).astype(BF16))
            ri = lax.broadcasted_iota(jnp.int32, (width, LANES), 0)
            ci = lax.broadcasted_iota(jnp.int32, (width, LANES), 1)
            fold = (lax.bitwise_and(ri, HEAD_DIM - 1) == ci).astype(BF16)
            colsum = jnp.broadcast_to(jnp.sum(dxn * xh, axis=0, keepdims=True), (8, width))
            part = sum(jnp.dot(p, fold, preferred_element_type=F32) for p in _split3(colsum))

            @pl.when(pl.program_id(0) == 0)
            def _():
                outs[dst][...] = jnp.zeros_like(outs[dst])

            outs[dst][0:1, :] += part[0:1, :]
        outs[0][:, 0:cw] = pieces[0]
        outs[0][:, cw:cw + kw] = pieces[1]
        outs[0][:, cw + kw:cw + 2 * kw] = dv.astype(BF16)

    kblk = cw // kw
    tab = pl.BlockSpec((tm, LANES), lambda i, j, k: (i, 0))
    kcur = pl.BlockSpec((tm, kw), lambda i, j, k: (i, 0))
    knext = pl.BlockSpec((tm, kw), lambda i, j, k: (jnp.minimum(i + 1, nblk - 1), 0))
    acc = pl.BlockSpec((8, LANES), lambda i, j, k: (0, 0))
    return _fused("qk_prep_bwd", (nblk, 1, 1),
                  [(proj, pl.BlockSpec((tm, cw), lambda i, j, k: (i, 3))),
                   (proj, pl.BlockSpec((tm, kw), lambda i, j, k: (i, 4 * kblk))),
                   (proj, pl.BlockSpec((tm, kw), lambda i, j, k: (i, 4 * kblk + 1))),
                   (gq, pl.BlockSpec((1, cw), lambda i, j, k: (0, 0))),
                   (gk, pl.BlockSpec((1, kw), lambda i, j, k: (0, 0))),
                   (rope_tabs[0], tab), (rope_tabs[1], tab), (rope_tabs[2], tab),
                   (dq, pl.BlockSpec((tm, cw), lambda i, j, k: (i, 0))),
                   (dkc, kcur), (dkp, knext), (dvc, kcur), (dvp, knext)],
                  [(_sds((t, cw + 2 * kw), BF16), pl.BlockSpec((tm, cw + 2 * kw), lambda i, j, k: (i, 0))),
                   (_sds((8, LANES), F32), acc), (_sds((8, LANES), F32), acc)],
                  [], epilogue, temp_bytes=16 * tm * cw * 4, semantics=("arbitrary", "arbitrary", "arbitrary"))


def _attn_mask(n):
    key = lax.broadcasted_iota(jnp.int32, (2 * BLOCK, GROUP * BLOCK), 0)
    qry = lax.bitwise_and(lax.broadcasted_iota(jnp.int32, (2 * BLOCK, GROUP * BLOCK), 1), BLOCK - 1)
    return (key > qry) & (key <= qry + BLOCK) & ((key >= BLOCK) | (n > 0))


def _stack_heads(x, h):
    return jnp.concatenate([x[:, (h * GROUP + g) * HEAD_DIM:(h * GROUP + g + 1) * HEAD_DIM] for g in range(GROUP)], axis=0)


def _softmax_with_sink(q4, k2, sink_ref, h, valid):
    sink = jnp.concatenate([sink_ref[h * GROUP + g:h * GROUP + g + 1, :] for g in range(GROUP)], axis=1)
    s = lax.dot_general(k2, q4, NT, preferred_element_type=F32) * ATTN_SCALE
    s = jnp.where(valid, s, NEG_INF)
    m = jnp.maximum(jnp.max(s, axis=0, keepdims=True), sink)
    p = jnp.exp(s - m)
    es = jnp.exp(sink - m)
    inv = 1.0 / (jnp.sum(p, axis=0, keepdims=True) + es)
    return p * inv, es * inv


def _attn_fwd(qn, kn, vb, sink_rows):
    t, cw = qn.shape
    kw = kn.shape[1]
    nkv = kw // HEAD_DIM

    def body(q_ref, kp_ref, kc_ref, vp_ref, vc_ref, sink_ref, o_ref):
        valid = _attn_mask(pl.program_id(0))
        qv = q_ref[...]
        kp, kc, vp, vc = kp_ref[...], kc_ref[...], vp_ref[...], vc_ref[...]
        outs = []
        for h in range(nkv):
            hs = slice(h * HEAD_DIM, (h + 1) * HEAD_DIM)
            k2 = jnp.concatenate([kp[:, hs], kc[:, hs]], axis=0)
            v2 = jnp.concatenate([vp[:, hs], vc[:, hs]], axis=0)
            pn, _ = _softmax_with_sink(_stack_heads(qv, h), k2, sink_ref, h, valid)
            o4 = lax.dot_general(pn.astype(BF16), v2, TN, preferred_element_type=F32)
            outs += [o4[g * BLOCK:(g + 1) * BLOCK] for g in range(GROUP)]
        o_ref[...] = jnp.concatenate(outs, axis=-1).astype(BF16)

    cur = lambda n: (n, 0)
    prev = lambda n: (jnp.maximum(n - 1, 0), 0)
    return pl.pallas_call(
        body, name="attn_fwd", grid=(t // BLOCK,),
        in_specs=[pl.BlockSpec((BLOCK, cw), cur),
                  pl.BlockSpec((BLOCK, kw), prev), pl.BlockSpec((BLOCK, kw), cur),
                  pl.BlockSpec((BLOCK, kw), prev), pl.BlockSpec((BLOCK, kw), cur),
                  pl.BlockSpec(sink_rows.shape, lambda n: (0, 0))],
        out_specs=pl.BlockSpec((BLOCK, cw), cur),
        out_shape=_sds((t, cw), BF16),
        compiler_params=_params(("parallel",), BLOCK * (cw + 4 * kw) * 2 + BLOCK * cw * 2, 8 << 20),
    )(qn, kn, kn, vb, vb, sink_rows)


def _attn_bwd(qn, kn, vb, sink_rows, do):
    t, cw = qn.shape
    kw = kn.shape[1]
    nkv = kw // HEAD_DIM
    nq = nkv * GROUP

    def body(q_ref, kp_ref, kc_ref, vp_ref, vc_ref, sink_ref, do_ref,
             dq_ref, dkc_ref, dkp_ref, dvc_ref, dvp_ref, dsink_ref):
        n = pl.program_id(0)
        valid = _attn_mask(n)
        qv, dov = q_ref[...], do_ref[...]
        kp, kc, vp, vc = kp_ref[...], kc_ref[...], vp_ref[...], vc_ref[...]
        dqs, dks, dvs, dsinks = [], [], [], []
        for h in range(nkv):
            hs = slice(h * HEAD_DIM, (h + 1) * HEAD_DIM)
            k2 = jnp.concatenate([kp[:, hs], kc[:, hs]], axis=0)
            v2 = jnp.concatenate([vp[:, hs], vc[:, hs]], axis=0)
            q4 = _stack_heads(qv, h)
            dob = _stack_heads(dov, h).astype(BF16)
            pn, psink = _softmax_with_sink(q4, k2, sink_ref, h, valid)
            dpn = lax.dot_general(v2, dob, NT, preferred_element_type=F32)
            dvs.append(jnp.dot(pn.astype(BF16), dob, preferred_element_type=F32))
            delta = jnp.sum(pn * dpn, axis=0, keepdims=True)
            ds = (pn * (dpn - delta) * ATTN_SCALE).astype(BF16)
            dks.append(jnp.dot(ds, q4, preferred_element_type=F32))
            dq4 = lax.dot_general(ds, k2, TN, preferred_element_type=F32)
            dsink4 = -psink * delta
            for g in range(GROUP):
                dqs.append(dq4[g * BLOCK:(g + 1) * BLOCK])
                dsinks.append(jnp.broadcast_to(jnp.sum(dsink4[:, g * BLOCK:(g + 1) * BLOCK], axis=1, keepdims=True), (1, LANES)))
        dq_ref[...] = jnp.concatenate(dqs, axis=-1)
        dkp_ref[...] = jnp.concatenate([d[:BLOCK] for d in dks], axis=-1)
        dkc_ref[...] = jnp.concatenate([d[BLOCK:] for d in dks], axis=-1)
        dvp_ref[...] = jnp.concatenate([d[:BLOCK] for d in dvs], axis=-1)
        dvc_ref[...] = jnp.concatenate([d[BLOCK:] for d in dvs], axis=-1)

        @pl.when(n == 0)
        def _():
            dsink_ref[...] = jnp.zeros_like(dsink_ref)

        dsink_ref[...] += jnp.concatenate(dsinks, axis=0)

    cur = lambda n: (n, 0)
    prev = lambda n: (jnp.maximum(n - 1, 0), 0)
    kspec = pl.BlockSpec((BLOCK, kw), cur)
    return pl.pallas_call(
        body, name="attn_bwd", grid=(t // BLOCK,),
        in_specs=[pl.BlockSpec((BLOCK, cw), cur),
                  pl.BlockSpec((BLOCK, kw), prev), kspec,
                  pl.BlockSpec((BLOCK, kw), prev), kspec,
                  pl.BlockSpec(sink_rows.shape, lambda n: (0, 0)),
                  pl.BlockSpec((BLOCK, cw), cur)],
        out_specs=[pl.BlockSpec((BLOCK, cw), cur), kspec, kspec, kspec, kspec,
                   pl.BlockSpec((nq, LANES), lambda n: (0, 0))],
        out_shape=[_sds((t, cw), F32)] + [_sds((t, kw), F32)] * 4 + [_sds((nq, LANES), F32)],
        compiler_params=_params(("arbitrary",), BLOCK * (cw + 4 * kw) * 2 + 2 * BLOCK * cw * 4 + 4 * BLOCK * kw * 4, 12 << 20),
    )(qn, kn, kn, vb, vb, sink_rows, do)


def _mix_out(ca, o, woc, woa, proj):
    t, cw = ca.shape
    nb = woc.shape[2]
    d = N_DEV * nb
    tm = min(t, 1024)
    ga0 = (3 * cw + cw + 2 * (cw // 4)) // nb

    def body(ca_ref, o_ref, woc_ref, woa_ref, ga_ref, gb_ref, m_ref, ya_ref, yb_ref):
        ya = jnp.dot(ca_ref[...], woc_ref[...], preferred_element_type=F32)
        yb = jnp.dot(o_ref[...], woa_ref[...], preferred_element_type=F32)
        ya_ref[...] = ya.astype(BF16)
        yb_ref[...] = yb.astype(BF16)
        m_ref[...] = (_sigmoid(ga_ref[...].astype(F32)) * ya + _sigmoid(gb_ref[...].astype(F32)) * yb).astype(BF16)

    act = pl.BlockSpec((tm, cw), lambda i, j: (i, 0))
    wsp = pl.BlockSpec((None, cw, nb), lambda i, j: (j, 0, 0))
    osp = pl.BlockSpec((tm, nb), lambda i, j: (i, j))
    blocks = 2 * tm * cw * 2 + 2 * cw * nb * 2 + 2 * tm * nb * 4 + 3 * tm * nb * 2
    return pl.pallas_call(
        body, name="mix_out", grid=(t // tm, N_DEV),
        in_specs=[act, act, wsp, wsp,
                  pl.BlockSpec((tm, nb), lambda i, j: (i, ga0 + j)),
                  pl.BlockSpec((tm, nb), lambda i, j: (i, ga0 + N_DEV + j))],
        out_specs=[osp, osp, osp],
        out_shape=[_sds((t, d), BF16)] * 3,
        compiler_params=_params(("parallel", "parallel"), blocks, 6 * tm * nb * 4),
    )(ca, o, woc, woa, proj, proj)


def _mix_residual(merged, wo, x):
    t, d = x.shape
    tm = min(t, 512)

    def epilogue(acc, ins, outs):
        outs[0][...] = ins[2][...] + acc

    row = pl.BlockSpec((tm, d), lambda i, j, k: (i, 0))
    return _fused("mix_residual", (t // tm, 1, 1),
                  [(merged, row), (wo, pl.BlockSpec((d, d), lambda i, j, k: (0, 0))), (x, row)],
                  [(_sds((t, d), F32), row)], [(0, 1, NN)], epilogue, temp_bytes=2 * tm * d * 4)[0]


def _mix_bwd_gates(dx, wo, ya, yb, proj, cw):
    t, d = dx.shape
    tm = min(t, 1024)
    tn = min(d, 512)
    ga0 = (4 * cw + 2 * (cw // 4)) // tn

    def epilogue(acc, ins, outs):
        sa, sb = _sigmoid(ins[4][...].astype(F32)), _sigmoid(ins[5][...].astype(F32))
        outs[0][...] = (acc * sa).astype(BF16)
        outs[1][...] = (acc * sb).astype(BF16)
        outs[2][0] = (acc * ins[2][...].astype(F32) * sa * (1.0 - sa)).astype(BF16)
        outs[2][1] = (acc * ins[3][...].astype(F32) * sb * (1.0 - sb)).astype(BF16)

    blk = pl.BlockSpec((tm, tn), lambda i, j, k: (i, j))
    return _fused("mix_bwd_gates", (t // tm, d // tn, 1),
                  [(dx, pl.BlockSpec((tm, d), lambda i, j, k: (i, 0))),
                   (wo, pl.BlockSpec((tn, d), lambda i, j, k: (j, 0))),
                   (ya, blk), (yb, blk),
                   (proj, pl.BlockSpec((tm, tn), lambda i, j, k: (i, ga0 + j))),
                   (proj, pl.BlockSpec((tm, tn), lambda i, j, k: (i, ga0 + d // tn + j)))],
                  [(_sds((t, d), BF16), blk), (_sds((t, d), BF16), blk),
                   (_sds((2, t, d), BF16), pl.BlockSpec((2, tm, tn), lambda i, j, k: (0, i, j)))],
                  [(0, 1, NT)], epilogue, temp_bytes=8 * tm * tn * 4)


def _tn_matmul(name, a, b, tm, out_dtype=BF16):
    t, m = a.shape
    n = b.shape[1]

    def epilogue(acc, ins, outs):
        outs[0][...] = acc.astype(out_dtype)

    return _fused(name, (m // tm, 1, 1),
                  [(a, pl.BlockSpec((t, tm), lambda i, j, k: (0, i))),
                   (b, pl.BlockSpec((t, n), lambda i, j, k: (0, 0)))],
                  [(_sds((m, n), out_dtype), pl.BlockSpec((tm, n), lambda i, j, k: (i, 0)))],
                  [(0, 1, TN)], epilogue, temp_bytes=2 * tm * n * 4)[0]


def _out_proj_bwd_act(dya, dyb, woc, woa, deps=()):
    t, d = dya.shape
    kdim, nb = woc.shape[1], woc.shape[2]
    tm = min(t, 512)

    def body(dya_ref, dyb_ref, woc_ref, woa_ref, *rest):
        for dy_ref, w_ref, o_ref in ((dya_ref, woc_ref, rest[-2]), (dyb_ref, woa_ref, rest[-1])):
            total = None
            for j in range(N_DEV):
                part = lax.dot_general(dy_ref[:, j * nb:(j + 1) * nb], w_ref[j], NT, preferred_element_type=F32)
                total = part if total is None else total + part
            o_ref[...] = total

    row = pl.BlockSpec((tm, d), lambda i: (i, 0))
    wsp = pl.BlockSpec((N_DEV, kdim, nb), lambda i: (0, 0, 0))
    osp = pl.BlockSpec((tm, kdim), lambda i: (i, 0))
    blocks = 2 * tm * d * 2 + 2 * N_DEV * kdim * nb * 2 + 2 * tm * kdim * 4
    return pl.pallas_call(
        body, name="mix_bwd_dca_do", grid=(t // tm,),
        in_specs=[row, row, wsp, wsp] + [_ANY] * len(deps), out_specs=[osp, osp],
        out_shape=[_sds((t, kdim), F32)] * 2,
        compiler_params=_params(("parallel",), blocks, 4 * tm * kdim * 4),
    )(dya, dyb, woc, woa, *deps)


def _out_proj_bwd_w(ca, o, dya, dyb, nb):
    t, kdim = ca.shape

    def body(ca_ref, o_ref, dya_ref, dyb_ref, dwoc_ref, dwoa_ref):
        dwoc_ref[...] = lax.dot_general(ca_ref[...], dya_ref[...], TN, preferred_element_type=F32).astype(BF16)
        dwoa_ref[...] = lax.dot_general(o_ref[...], dyb_ref[...], TN, preferred_element_type=F32).astype(BF16)

    act = pl.BlockSpec((t, kdim), lambda j: (0, 0))
    col = pl.BlockSpec((t, nb), lambda j: (0, j))
    osp = pl.BlockSpec((None, kdim, nb), lambda j: (j, 0, 0))
    blocks = 2 * t * kdim * 2 + 2 * t * nb * 2 + 2 * kdim * nb * 2
    return pl.pallas_call(
        body, name="mix_bwd_dwoc_dwoa", grid=(N_DEV,),
        in_specs=[act, act, col, col], out_specs=[osp, osp],
        out_shape=[_sds((N_DEV, kdim, nb), BF16)] * 2,
        compiler_params=_params(("parallel",), blocks, 4 * kdim * nb * 4),
    )(ca, o, dya, dyb)


def _proj_bwd_act(dproj, w_in, deps=()):
    t, n = dproj.shape
    d, nb = w_in.shape[2], w_in.shape[3]
    tm = min(t, 512)

    def epilogue(acc, ins, outs):
        outs[0][...] = acc

    def products(ins):
        return (lax.dot_general(ins[0][:, 0:nb], ins[1][0], NT, preferred_element_type=F32)
                + lax.dot_general(ins[0][:, nb:2 * nb], ins[1][1], NT, preferred_element_type=F32))

    return _fused("mix_bwd_dh", (t // tm, 1, 4),
                  [(dproj, pl.BlockSpec((tm, 2 * nb), lambda i, j, k: (i, k))),
                   (w_in, pl.BlockSpec((None, 2, d, nb), lambda i, j, k: (k, 0, 0, 0)))],
                  [(_sds((t, d), F32), pl.BlockSpec((tm, d), lambda i, j, k: (i, 0)))],
                  products, epilogue, nk=4, acc_shape=(tm, d), temp_bytes=tm * d * 4, deps=deps)[0]


def _proj_bwd_w(h, dproj):
    t, d = h.shape
    nb = dproj.shape[1] // N_DEV
    tm = min(d, 512)

    def body(h_ref, dp_ref, o_ref):
        hv = h_ref[...]
        o_ref[0] = lax.dot_general(hv, dp_ref[:, 0:nb], TN, preferred_element_type=F32).astype(BF16)
        o_ref[1] = lax.dot_general(hv, dp_ref[:, nb:2 * nb], TN, preferred_element_type=F32).astype(BF16)

    blocks = t * tm * 2 + t * 2 * nb * 2 + 2 * tm * nb * 2
    return pl.pallas_call(
        body, name="mix_bwd_dwin", grid=(4, d // tm),
        in_specs=[pl.BlockSpec((t, tm), lambda j, i: (0, i)),
                  pl.BlockSpec((t, 2 * nb), lambda j, i: (0, j))],
        out_specs=pl.BlockSpec((None, 2, tm, nb), lambda j, i: (j, 0, i, 0)),
        out_shape=_sds((4, 2, d, nb), BF16),
        compiler_params=_params(("parallel", "parallel"), blocks, 4 * tm * nb * 4),
    )(h, dproj)


def _adamw_math(w, g, m, v):
    m = ADAM_B1 * m + (1.0 - ADAM_B1) * g
    v = ADAM_B2 * v + (1.0 - ADAM_B2) * (g * g)
    m_hat = m / (1.0 - ADAM_B1 ** ADAM_STEP)
    v_hat = v / (1.0 - ADAM_B2 ** ADAM_STEP)
    delta = -ADAM_LR * (m_hat / (jnp.sqrt(v_hat) + ADAM_EPS) + ADAM_WD * w)
    return delta, m, v


def _adamw(name, parts, w, m, v, tr):
    r, c = w.shape

    def body(p_ref, w_ref, m_ref, v_ref, g_out, d_out, m_out, v_out):
        g = p_ref[0].astype(F32)
        for s in range(1, N_DEV):
            g = g + p_ref[s].astype(F32)
        delta, mn, vn = _adamw_math(w_ref[...], g, m_ref[...], v_ref[...])
        g_out[...] = g
        d_out[...] = delta
        m_out[...] = mn
        v_out[...] = vn

    blk = pl.BlockSpec((tr, c), lambda i: (i, 0))
    blocks = N_DEV * tr * c * parts.dtype.itemsize + 7 * tr * c * 4
    return pl.pallas_call(
        body, name=name, grid=(r // tr,),
        in_specs=[pl.BlockSpec((N_DEV, tr, c), lambda i: (0, i, 0)), blk, blk, blk],
        out_specs=[blk] * 4, out_shape=[_sds((r, c), F32)] * 4,
        compiler_params=_params(("parallel",), blocks, 6 * tr * c * 4),
    )(parts, w, m, v)


def _chip_sum(sums_ref):
    g = sums_ref[0].astype(F32)
    for k in range(1, 4):
        g = g + sums_ref[k].astype(F32)
    return g


def _adamw_chips(name, sums, w, m, v, tr, deps=(), row0=0, into=None):
    r, c = w.shape
    rs = sums.shape[1]
    i0 = row0 // tr
    n_pass = len(deps) + (4 if into is not None else 0)

    def body(sums_ref, w_ref, m_ref, v_ref, *rest):
        g_out, d_out, m_out, v_out = rest[n_pass:]
        g = _chip_sum(sums_ref)
        delta, mn, vn = _adamw_math(w_ref[...], g, m_ref[...], v_ref[...])
        g_out[...] = g
        d_out[...] = delta
        m_out[...] = mn
        v_out[...] = vn

    blk = pl.BlockSpec((tr, c), lambda i: (i0 + i, 0))
    blocks = 4 * tr * c * 2 + 7 * tr * c * 4
    passed = list(deps) + (list(into) if into is not None else [])
    aliases = {4 + len(deps) + q: q for q in range(4)} if into is not None else {}
    return pl.pallas_call(
        body, name=name, grid=(rs // tr,),
        in_specs=[pl.BlockSpec((4, tr, c), lambda i: (0, i, 0)), blk, blk, blk] + [_ANY] * n_pass,
        out_specs=[blk] * 4, out_shape=[_sds((r, c), F32)] * 4,
        input_output_aliases=aliases,
        compiler_params=_params(("parallel",), blocks, 6 * tr * c * 4),
    )(sums, w, m, v, *passed)


def _adamw_side(contrib, w, m, v, n_tiles, step_of):
    r, c = w.shape
    tr = r // n_tiles
    assert tr * n_tiles == r and tr % 16 == 0, (r, n_tiles)

    def tile(i, j, k):
        return jnp.minimum(step_of(i, j, k), n_tiles - 1)

    blk = pl.BlockSpec((tr, c), lambda i, j, k: (tile(i, j, k), 0))
    ins = [(contrib, pl.BlockSpec((4, tr, c), lambda i, j, k: (0, tile(i, j, k), 0))), (w, blk), (m, blk), (v, blk)]
    outs = [(_sds((r, c), F32), blk)] * 4

    def fn(in_refs, out_refs):
        @pl.when(step_of(pl.program_id(0), pl.program_id(1), pl.program_id(2)) < n_tiles)
        def _():
            g = _chip_sum(in_refs[0])
            delta, mn, vn = _adamw_math(in_refs[1][...], g, in_refs[2][...], in_refs[3][...])
            out_refs[0][...] = g
            out_refs[1][...] = delta
            out_refs[2][...] = mn
            out_refs[3][...] = vn

    return ins, outs, fn


def _rope_tables(t):
    half = ROT_DIM // 2
    inv_freq = 1.0 / (ROPE_THETA ** (jnp.arange(0, ROT_DIM, 2, dtype=F32) / ROT_DIM))
    ang = jnp.arange(t, dtype=F32)[:, None] * inv_freq[None, :]
    cos, sin = jnp.cos(ang), jnp.sin(ang)
    ones = jnp.ones((t, HEAD_DIM - ROT_DIM), F32)
    zeros = jnp.zeros((t, HEAD_DIM - half), F32)
    c = jnp.concatenate([cos, cos, ones], axis=1)
    sa = jnp.concatenate([-sin, zeros], axis=1)
    sb = jnp.concatenate([jnp.zeros((t, half), F32), sin, jnp.zeros((t, HEAD_DIM - ROT_DIM), F32)], axis=1)
    return tuple(jnp.tile(a, (1, LANES // HEAD_DIM)) for a in (c, sa, sb))


def _pad_rows(a, rows=8):
    return jnp.pad(a, ((0, rows - a.shape[0]), (0, 0)))


def kernel(x, g_ffn1, w_gu1, w_down1, g_mix, w_in, conv_w, q_norm_g, k_norm_g, sinks, w_out_conv, w_out_attn, w_o, g_ffn2, w_gu2, w_down2, loss_target, m_g_ffn1, m_w_gu1, m_w_down1, m_g_mix, m_w_in, m_conv_w, m_q_norm_g, m_k_norm_g, m_sinks, m_w_out_conv, m_w_out_attn, m_w_o, m_g_ffn2, m_w_gu2, m_w_down2, v_g_ffn1, v_w_gu1, v_w_down1, v_g_mix, v_w_in, v_conv_w, v_q_norm_g, v_k_norm_g, v_sinks, v_w_out_conv, v_w_out_attn, v_w_o, v_g_ffn2, v_w_gu2, v_w_down2):
    t, d = x.shape[1], x.shape[2]
    cw = d // 2
    kw = cw // GROUP
    nq = cw // HEAD_DIM
    xs, target = x.reshape(t, d), loss_target.reshape(t, d)
    me = 4 * lax.axis_index("x") + 2 * lax.axis_index("y") + lax.axis_index("c")

    big = {"w_gu1": w_gu1, "w_down1": w_down1, "w_in": w_in, "w_out_conv": w_out_conv,
           "w_out_attn": w_out_attn, "w_o": w_o, "w_gu2": w_gu2, "w_down2": w_down2}
    big_m = {"w_gu1": m_w_gu1, "w_down1": m_w_down1, "w_in": m_w_in, "w_out_conv": m_w_out_conv,
             "w_out_attn": m_w_out_attn, "w_o": m_w_o, "w_gu2": m_w_gu2, "w_down2": m_w_down2}
    big_v = {"w_gu1": v_w_gu1, "w_down1": v_w_down1, "w_in": v_w_in, "w_out_conv": v_w_out_conv,
             "w_out_attn": v_w_out_attn, "w_o": v_w_o, "w_gu2": v_w_gu2, "w_down2": v_w_down2}
    names = list(big)

    tiles = {"w_gu1": 256, "w_gu2": 256, "w_in": 256, "w_down1": 176, "w_down2": 176,
             "w_out_conv": 1024, "w_out_attn": 1024, "w_o": 128}

    def row_tile(n):
        r = big[n].shape[1]
        return tiles[n] if r % tiles[n] == 0 else r

    rs_shape = {n: big[n].shape[1:] for n in names}
    half = rs_shape["w_gu1"][0] // 2
    rs_shape["w_gu1_lo"] = rs_shape["w_gu1_hi"] = (half, rs_shape["w_gu1"][1])

    def add_tile(n):
        r, c = rs_shape[n]
        while r * c * 2 > (3 << 20) and r % 32 == 0:
            r //= 2
        return r

    me_arr = me.astype(jnp.int32).reshape(1)
    sources = [(n, big[n][0], BF16, row_tile(n)) for n in names] + [("conv_w", _pad_rows(conv_w[0]), F32, 8)]
    issue_order = [0, 1, 2, 8, 3, 4, 5, 6, 7]
    first = _place_shard("place_" + names[0], sources[0][1], BF16, me_arr, sources[0][3])
    started = [_gather_start("gather_start_first", [first])]
    early = {2: (big_m["w_in"][0], big_v["w_in"][0])}
    rest = [_place_shard("place_" + sources[i][0], sources[i][1], sources[i][2], me_arr, sources[i][3],
                         deps=(started[0][3],) + early.get(i, ())) for i in issue_order[1:]]
    started.append(_gather_start("gather_start_rest", rest))
    where = {0: (0, 0)}
    where.update({i: (1, p) for p, i in enumerate(issue_order[1:])})

    def fetch(tag, idxs, after, forward=True):
        call = where[idxs[0]][0]
        send, recv, stacks, _ = started[call]
        positions = [where[i][1] for i in idxs]
        got = _gather_wait("gather_wait_" + tag, positions, send, recv, [stacks[p] for p in positions], after)
        return _forward_to_sibling("gather_forward_" + tag, got) if forward else got

    rope_tabs = _rope_tables(t)
    gq = jnp.tile(q_norm_g, (1, nq))
    gk = jnp.tile(k_norm_g, (1, nq // GROUP))
    sink_rows = jnp.broadcast_to(sinks[0][:, None], (nq, LANES))

    wts = {}
    h1 = _rms_fwd("ffn1_norm", xs, g_ffn1)
    wts["w_gu1"], = fetch("gu1", [0], started[1][3])
    gu1, a1 = _ffn_up("ffn1_up", h1, wts["w_gu1"])
    wts["w_down1"], = fetch("down1", [1], a1)
    wd1 = wts["w_down1"].reshape(-1, d)
    x1 = _ffn_down("ffn1_down", a1, wd1, xs)
    h2 = _rms_fwd("mix_norm", x1, g_mix)
    wts["w_in"], conv_land = fetch("in", [2, 8], h2)
    w_in_full = wts["w_in"].reshape(4, 2, d, -1)
    conv_full = jnp.transpose(conv_land, (1, 0, 2)).reshape(8, cw)
    proj = _proj(h2, w_in_full)
    ca = _conv_fwd(proj, conv_full)
    qn, kn, vb = _qk_prep(proj, gq, gk, rope_tabs, cw, kw)
    o = _attn_fwd(qn, kn, vb, sink_rows)
    wts["w_out_conv"], wts["w_out_attn"] = fetch("out", [3, 4], o)
    merged, ya, yb = _mix_out(ca, o, wts["w_out_conv"], wts["w_out_attn"], proj)
    wts["w_o"], = fetch("o", [5], merged)
    wo = wts["w_o"].reshape(d, d)
    x2 = _mix_residual(merged, wo, x1)
    h3 = _rms_fwd("ffn2_norm", x2, g_ffn2)
    mine = lax.axis_index("c").astype(jnp.int32).reshape(1)
    got = fetch("gu2", [6], h3, forward=False)
    fsend, frecv, got = _forward_start("gather_forward_start_gu2", got)
    part = _ffn_up("ffn2_up_mine", h3, got[0], parity=mine)
    wts["w_gu2"], = _forward_wait("gather_forward_wait_gu2", fsend, frecv, got, part[1])
    gu2, a2 = _ffn_up("ffn2_up_sibling", h3, wts["w_gu2"], parity=1 - mine, into=part)
    wts["w_down2"], = fetch("down2", [7], a2)
    wd2 = wts["w_down2"].reshape(-1, d)
    dy, sq, dy_bf = _ffn_down("ffn2_down", a2, wd2, x2, target=target)
    loss = lax.psum(sq[0, 0] * (0.5 / d), ("x", "y", "c"))

    place = jnp.stack([lax.axis_index("c"), 2 * lax.axis_index("x") + lax.axis_index("y")]).astype(jnp.int32)
    def pair_start(tag, group, grads, deps=()):
        stacks = [grads[n].reshape((4, 2) + rs_shape[n]) for n in group]
        lands = [lax.empty((4,) + rs_shape[n], BF16) for n in group]
        return _pair_start("rs_pair_start_" + tag, stacks, lands, deps)

    def chip_start(tag, group, pending, after):
        send, recv, stacks, lands, _ = pending
        stacks, lands = _pair_wait("rs_pair_wait_" + tag, send, recv, stacks, lands, after)
        added = [_pair_add("rs_pair_add_" + n, st, ld, place, add_tile(n)) for n, st, ld in zip(group, stacks, lands)]
        return _chip_start("rs_chip_start_" + tag, [a[0] for a in added], [a[1] for a in added])

    group_a, group_b, group_c = ["w_down2", "w_gu2"], ["w_o", "w_out_conv", "w_out_attn"], ["w_in"]
    group_d, group_e, group_f = ["w_down1"], ["w_gu1_lo"], ["w_gu1_hi"]
    g = {}
    dgu2, a2 = _ffn_bwd_act("ffn2_bwd_act", dy_bf, wd2, gu2)
    g["w_down2"], = _ffn_bwd_dwd("ffn2_bwd_dwd", a2, dy_bf)
    g["w_gu2"], = _ffn_bwd_dwgu("ffn2_bwd_dwgu", h3, dgu2)
    pend_a = pair_start("a", group_a, g)
    dh3, = _ffn_bwd_dh("ffn2_bwd_dh", dgu2, wts["w_gu2"], deps=(pend_a[4],))
    ring_a = chip_start("a", group_a, pend_a, dh3)
    dx2, dg_ffn2, dx2_bf = _rms_bwd("ffn2_bwd_rms", x2, g_ffn2, dh3, dy, deps=(ring_a[4],), with_bf16=True)

    dya, dyb, dgates = _mix_bwd_gates(dx2_bf, wo, ya, yb, proj, cw)
    g["w_o"] = _tn_matmul("mix_bwd_dwo", merged, dx2_bf, min(d, 512))
    g["w_out_conv"], g["w_out_attn"] = _out_proj_bwd_w(ca, o, dya, dyb, d // N_DEV)
    pend_b = pair_start("b", group_b, g)
    dca, do = _out_proj_bwd_act(dya, dyb, wts["w_out_conv"], wts["w_out_attn"], deps=(pend_b[4],))
    ring_b = chip_start("b", group_b, pend_b, do)
    d3, dconv_w = _conv_bwd(proj, conv_full, dca, deps=(ring_b[4],))
    dq, dkc, dkp, dvc, dvp, dsink = _attn_bwd(qn, kn, vb, sink_rows, do)
    dqkv, dgq, dgk = _qk_prep_bwd(proj, gq, gk, rope_tabs, dq, dkc, dkp, dvc, dvp, cw, kw)
    dproj = jnp.concatenate([d3[0], d3[1], d3[2], dqkv, dgates[0], dgates[1]], axis=1)
    g["w_in"] = _proj_bwd_w(h2, dproj)
    pend_c = pair_start("c", group_c, g)
    dh2 = _proj_bwd_act(dproj, w_in_full, deps=(pend_c[4],))
    ring_c = chip_start("c", group_c, pend_c, dh2)
    dx1, dg_mix, dx1_bf = _rms_bwd("mix_bwd_rms", x1, g_mix, dh2, dx2, deps=(ring_c[4],), with_bf16=True)

    big_out = {}
    arrived = {}

    def wait_group(tag, group, ring, after):
        send, recv, parts, lands2, _ = ring
        parts, lands2 = _chip_wait("rs_chip_wait_" + tag, send, recv, parts, lands2, after)
        arrived.update(dict(zip(group, lands2)))

    def update(n, after):
        res = _adamw_chips("adamw_" + n, arrived[n], big[n][0], big_m[n][0], big_v[n][0], row_tile(n), deps=(after,))
        big_out[n] = [a[None] for a in res]
        return res[0]

    def update_beside(n, n_tiles, step_of):
        return _adamw_side(arrived[n], big[n][0], big_m[n][0], big_v[n][0], n_tiles, step_of)

    def keep(n, res):
        big_out[n] = [a[None] for a in res]

    dgu1, a1 = _ffn_bwd_act("ffn1_bwd_act", dx1_bf, wd1, gu1)
    wait_group("a", group_a, ring_a, a1)
    g["w_down1"], *res = _ffn_bwd_dwd("ffn1_bwd_dwd", a1, dx1_bf,
                                       side=update_beside("w_down2", 11, lambda i, j, k: i * 4 + j))
    keep("w_down2", res)
    pend_d = pair_start("d", group_d, g)
    g["w_gu1_lo"], *res = _ffn_bwd_dwgu("ffn1_bwd_dwgu_lo", h1, dgu1, deps=(pend_d[4],), rows=(0, half),
                                         side=update_beside("w_gu2", 16, lambda i, j, k: i * 2 + j))
    keep("w_gu2", res)
    ring_d = chip_start("d", group_d, pend_d, g["w_gu1_lo"])
    pend_e = pair_start("e", group_e, g, deps=(ring_d[4],))
    wait_group("c", group_c, ring_c, pend_e[4])
    g["w_gu1_hi"], *res = _ffn_bwd_dwgu("ffn1_bwd_dwgu_hi", h1, dgu1, rows=(half, half),
                                         side=update_beside("w_in", 16, lambda i, j, k: i * 2 + j))
    keep("w_in", res)
    ring_e = chip_start("e", group_e, pend_e, g["w_gu1_hi"])
    pend_f = pair_start("f", group_f, g, deps=(ring_e[4],))
    wait_group("b", group_b, ring_b, pend_f[4])
    after = pend_f[4]
    for n in group_b:
        after = update(n, after)
    ring_f = chip_start("f", group_f, pend_f, after)
    wait_group("d", group_d, ring_d, ring_f[4])
    dh1, *res = _ffn_bwd_dh("ffn1_bwd_dh", dgu1, wts["w_gu1"],
                             side=update_beside("w_down1", 11, lambda i, j, k: i * 4 + k))
    keep("w_down1", res)
    grad_x, dg_ffn1 = _rms_bwd("ffn1_bwd_rms", xs, g_ffn1, dh1, dx1)
    after = grad_x
    n = "w_gu1"
    wait_group("e", group_e, ring_e, after)
    res = _adamw_chips("adamw_w_gu1_lo", arrived["w_gu1_lo"], big[n][0], big_m[n][0], big_v[n][0], row_tile(n), deps=(after,))
    wait_group("f", group_f, ring_f, res[0])
    res = _adamw_chips("adamw_w_gu1_hi", arrived["w_gu1_hi"], big[n][0], big_m[n][0], big_v[n][0], row_tile(n),
                       row0=half, into=res)
    keep(n, res)
    after = res[0]

    small = {"g_ffn1": dg_ffn1[0:1], "g_mix": dg_mix[0:1], "g_ffn2": dg_ffn2[0:1],
             "q_norm_g": dgq[0:1, :HEAD_DIM], "k_norm_g": dgk[0:1, :HEAD_DIM], "sinks": dsink[:, 0][None],
             "conv_w": dconv_w[0:CONV_K].reshape(1, -1)}
    small_w = {"g_ffn1": g_ffn1, "g_mix": g_mix, "g_ffn2": g_ffn2, "q_norm_g": q_norm_g, "k_norm_g": k_norm_g,
               "sinks": sinks, "conv_w": None}
    small_m = {"g_ffn1": m_g_ffn1, "g_mix": m_g_mix, "g_ffn2": m_g_ffn2, "q_norm_g": m_q_norm_g,
               "k_norm_g": m_k_norm_g, "sinks": m_sinks, "conv_w": m_conv_w}
    small_v = {"g_ffn1": v_g_ffn1, "g_mix": v_g_mix, "g_ffn2": v_g_ffn2, "q_norm_g": v_q_norm_g,
               "k_norm_g": v_k_norm_g, "sinks": v_sinks, "conv_w": v_conv_w}
    snames = list(small)
    widths = [small[n].shape[1] for n in snames]
    total = sum(widths)
    rows = -(-total // LANES)
    rows = -(-rows // 8) * 8

    def pack(vals):
        flat = jnp.concatenate([v.reshape(1, -1) for v in vals], axis=1)
        return jnp.pad(flat, ((0, 0), (0, rows * LANES - total))).reshape(rows, LANES)

    csh = cw // N_DEV

    def place_conv(local, fill):
        full = jnp.full((CONV_K, cw), fill, F32)
        return lax.dynamic_update_slice(full, local, (0, me * csh)).reshape(1, -1)

    pw = pack([small_w[n] if n != "conv_w" else place_conv(conv_w[0], 0.0) for n in snames])
    pm = pack([small_m[n] if n != "conv_w" else place_conv(m_conv_w[0], 0.0) for n in snames])
    pv = pack([small_v[n] if n != "conv_w" else place_conv(v_conv_w[0], 1.0) for n in snames])
    parts = _all_gather_small("gather_small_grads", pack([small[n] for n in snames]), deps=(after,))
    sg, sd, sm, sv = [a.reshape(1, -1) for a in _adamw("adamw_small", parts, pw, pm, pv, rows)]

    def unpack(flat, n):
        off = sum(widths[:snames.index(n)])
        piece = flat[:, off:off + widths[snames.index(n)]]
        if n == "conv_w":
            piece = lax.dynamic_slice(piece.reshape(CONV_K, cw), (0, me * csh), (CONV_K, csh))[None]
        return piece

    order = ["g_ffn1", "w_gu1", "w_down1", "g_mix", "w_in", "conv_w", "q_norm_g", "k_norm_g", "sinks",
             "w_out_conv", "w_out_attn", "w_o", "g_ffn2", "w_gu2", "w_down2"]
    outs = [loss, grad_x[None]]
    for idx, flat in enumerate((sg, sd, sm, sv)):
        for n in order:
            outs.append(big_out[n][idx] if n in big_out else unpack(flat, n))
    return tuple(outs)
```

```python
import jax
import jax.numpy as jnp
from jax import lax
from jax.experimental import pallas as pl
from jax.experimental.pallas import tpu as pltpu

F32 = jnp.float32
BF16 = jnp.bfloat16

N_DEV = 8
HEAD_DIM = 64
GROUP = 4
BLOCK = 128
ROT_DIM = 16
ROPE_THETA = 500000.0
RMS_EPS = 1e-6
NEG_INF = -1e30
ATTN_SCALE = HEAD_DIM ** -0.5
CONV_K = 3
LANES = 128
MXU_COLS = 256
VMEM_BYTES_V7X = 64 * 1024 * 1024
VMEM_CAP = VMEM_BYTES_V7X - 6 * 1024 * 1024

ADAM_LR = 0.001
ADAM_B1 = 0.9
ADAM_B2 = 0.999
ADAM_EPS = 1e-08
ADAM_WD = 0.01
ADAM_STEP = 10

NN = (((1,), (0,)), ((), ()))
NT = (((1,), (1,)), ((), ()))
TN = (((0,), (0,)), ((), ()))

MESH = pl.DeviceIdType.MESH


def _nbytes(shape, dtype):
    n = 1
    for s in shape:
        if s is not None:
            n *= s
    return n * jnp.dtype(dtype).itemsize


def _params(semantics, block_bytes, temp_bytes):
    assert 2 * block_bytes + temp_bytes <= VMEM_CAP, (block_bytes, temp_bytes)
    return pltpu.CompilerParams(dimension_semantics=semantics, vmem_limit_bytes=VMEM_CAP)


def _fused(name, grid, ins, outs, dots, epilogue, *, nk=1, acc_shape=None, temp_bytes=0,
           semantics=("parallel", "parallel", "arbitrary"), deps=(), side=None, alias_last_dep_to=None):
    n_main_in, n_main_out = len(ins), len(outs)
    if side is not None:
        ins, outs = list(ins) + list(side[0]), list(outs) + list(side[1])
    n_in, n_out = len(ins), len(outs)
    n_dep = len(deps)

    def body(*refs):
        in_refs, out_refs = refs[:n_in], refs[n_in + n_dep:n_in + n_dep + n_out]
        scratch = refs[n_in + n_dep + n_out:]
        if side is not None:
            side[2](in_refs[n_main_in:], out_refs[n_main_out:])

        def products():
            if callable(dots):
                return dots(in_refs)
            total = None
            for ai, bi, contract in dots:
                a, b = in_refs[ai][...], in_refs[bi][...]
                a = a if a.dtype == BF16 else a.astype(BF16)
                b = b if b.dtype == BF16 else b.astype(BF16)
                p = lax.dot_general(a, b, contract, preferred_element_type=F32)
                total = p if total is None else total + p
            return total

        if nk == 1:
            epilogue(products() if dots else None, in_refs, out_refs)
        else:
            acc = scratch[0]
            k = pl.program_id(2)

            @pl.when(k == 0)
            def _():
                acc[...] = jnp.zeros_like(acc)

            acc[...] += products()

            @pl.when(k == nk - 1)
            def _():
                epilogue(acc[...], in_refs, out_refs)

    block_bytes = sum(_nbytes(spec.block_shape, a.dtype) for a, spec in ins)
    block_bytes += sum(_nbytes(spec.block_shape, s.dtype) for s, spec in outs)
    scratch_shapes = []
    if nk > 1:
        scratch_shapes.append(pltpu.VMEM(acc_shape, F32))
        temp_bytes += _nbytes(acc_shape, F32)
    res = pl.pallas_call(
        body, name=name, grid=grid,
        in_specs=[spec for _, spec in ins] + [pl.BlockSpec(memory_space=pl.ANY)] * n_dep,
        out_specs=[spec for _, spec in outs],
        out_shape=[s for s, _ in outs],
        scratch_shapes=scratch_shapes,
        input_output_aliases={} if alias_last_dep_to is None else {n_in + n_dep - 1: alias_last_dep_to},
        compiler_params=_params(semantics, block_bytes, temp_bytes),
    )(*[a for a, _ in ins], *deps)
    return res


def _sds(shape, dtype):
    return jax.ShapeDtypeStruct(shape, dtype)


def _sigmoid(x):
    return jax.nn.sigmoid(x)


def _all_gather_small(name, shard, deps=()):
    n_dep = len(deps)

    def body(src, *rest):
        dst, send_sems, recv_sems, local_sem = rest[n_dep:]
        x, y, c = lax.axis_index("x"), lax.axis_index("y"), lax.axis_index("c")
        me = 4 * x + 2 * y + c
        copies = [pltpu.make_async_copy(src, dst.at[me], local_sem)]
        for k in range(1, N_DEV):
            peer = ((1 - x) if (k & 4) else x, (1 - y) if (k & 2) else y, (1 - c) if (k & 1) else c)
            copies.append(pltpu.make_async_remote_copy(
                src_ref=src, dst_ref=dst.at[me], send_sem=send_sems.at[k - 1], recv_sem=recv_sems.at[k - 1],
                device_id=peer, device_id_type=MESH))
        for cp in copies:
            cp.start()
        for cp in copies:
            cp.wait()

    hbm = pl.BlockSpec(memory_space=pltpu.HBM)
    return pl.pallas_call(
        body, name=name,
        in_specs=[hbm] + [pl.BlockSpec(memory_space=pl.ANY)] * n_dep, out_specs=hbm,
        out_shape=_sds((N_DEV,) + shard.shape, shard.dtype),
        scratch_shapes=[pltpu.SemaphoreType.DMA((N_DEV - 1,)), pltpu.SemaphoreType.DMA((N_DEV - 1,)),
                        pltpu.SemaphoreType.DMA],
    )(shard, *deps)


_HBM = pl.BlockSpec(memory_space=pltpu.HBM)
_SEM = pl.BlockSpec(memory_space=pltpu.SEMAPHORE)
_ANY = pl.BlockSpec(memory_space=pl.ANY)
_EFFECT = pltpu.SideEffectType.DATAFLOW_SIDE_EFFECTING
N_TARGETS = 4


def _mesh_pos():
    return lax.axis_index("x"), lax.axis_index("y"), lax.axis_index("c")


def _chip_peers(x, y, c):
    return [(1 - x, y, c), (x, 1 - y, c), (1 - x, 1 - y, c)]


def _dev_index(pos):
    return 4 * pos[0] + 2 * pos[1] + pos[2]


def _hbm_like(a):
    return pltpu.HBM(a.shape, a.dtype)


def _place_shard(name, w, out_dtype, me, tr, deps=()):
    r, c = w.shape
    n_dep = len(deps)

    def body(me_ref, w_ref, *rest):
        rest[n_dep][...] = w_ref[...].astype(out_dtype)

    grid_spec = pltpu.PrefetchScalarGridSpec(
        num_scalar_prefetch=1, grid=(r // tr,),
        in_specs=[pl.BlockSpec((tr, c), lambda i, me_ref: (i, 0))] + [_ANY] * n_dep,
        out_specs=pl.BlockSpec((None, tr, c), lambda i, me_ref: (me_ref[0], i, 0)))
    return pl.pallas_call(
        body, name=name, grid_spec=grid_spec, out_shape=_sds((N_DEV, r, c), out_dtype),
        compiler_params=_params(("parallel",), tr * c * 6, tr * c * 4),
    )(me, w, *deps)


def _gather_start(name, lands):
    n = len(lands)

    def body(*refs):
        bufs = refs[:n]
        send, recv = refs[n], refs[n + 1]
        token = refs[-1]
        x, y, c = _mesh_pos()
        me = _dev_index((x, y, c))
        targets = [(x, y, 1 - c)] + _chip_peers(x, y, c)
        for w in range(n):
            for k, to in enumerate(targets):
                pltpu.make_async_remote_copy(
                    src_ref=bufs[w].at[me], dst_ref=bufs[w].at[me],
                    send_sem=send.at[N_TARGETS * w + k], recv_sem=recv.at[N_TARGETS * w + k],
                    device_id=to, device_id_type=MESH).start()
        token[...] = jnp.zeros_like(token)

    sems = pltpu.SemaphoreType.DMA((N_TARGETS * n,))
    outs = pl.pallas_call(
        body, name=name,
        in_specs=[_HBM] * n, out_specs=[_SEM, _SEM] + [_HBM] * n + [_token_spec()],
        out_shape=[sems, sems] + [_hbm_like(a) for a in lands] + [_sds((8, LANES), F32)],
        input_output_aliases={i: 2 + i for i in range(n)},
        compiler_params=pltpu.CompilerParams(has_side_effects=_EFFECT),
    )(*lands)
    return outs[0], outs[1], list(outs[2:2 + n]), outs[-1]


def _gather_wait(name, positions, send, recv, lands, after):
    m = len(positions)

    def body(*refs):
        bufs = refs[:m]
        send_sems, recv_sems = refs[m], refs[m + 1]
        x, y, c = _mesh_pos()
        me = _dev_index((x, y, c))
        sources = [(x, y, 1 - c)] + _chip_peers(x, y, c)
        for j, w in enumerate(positions):
            for k, frm in enumerate(sources):
                cp = pltpu.make_async_remote_copy(
                    src_ref=bufs[j].at[me], dst_ref=bufs[j].at[_dev_index(frm)],
                    send_sem=send_sems.at[N_TARGETS * w + k], recv_sem=recv_sems.at[N_TARGETS * w + k],
                    device_id=frm, device_id_type=MESH)
                cp.wait_send()
                cp.wait_recv()

    outs = pl.pallas_call(
        body, name=name,
        in_specs=[_HBM] * m + [_SEM, _SEM, _ANY], out_specs=[_HBM] * m,
        out_shape=[_hbm_like(a) for a in lands],
        input_output_aliases={i: i for i in range(m)},
        compiler_params=pltpu.CompilerParams(has_side_effects=_EFFECT),
    )(*lands, send, recv, after)
    return list(outs)


def _forward_to_sibling(name, lands):
    m = len(lands)

    def body(*refs):
        copies = _forward_copies(refs[m:2 * m], refs[2 * m], refs[2 * m + 1])
        for cp in copies:
            cp.start()
        for cp in copies:
            cp.wait()

    outs = pl.pallas_call(
        body, name=name,
        in_specs=[_HBM] * m, out_specs=[_HBM] * m,
        out_shape=[_sds(a.shape, a.dtype) for a in lands],
        input_output_aliases={i: i for i in range(m)},
        scratch_shapes=[pltpu.SemaphoreType.DMA((3 * m,)), pltpu.SemaphoreType.DMA((3 * m,))],
    )(*lands)
    return list(outs)


def _forward_copies(bufs, send, recv):
    x, y, c = _mesh_pos()
    copies = []
    for j, buf in enumerate(bufs):
        for k, chip in enumerate(_chip_peers(x, y, c)):
            block = buf.at[_dev_index(chip)]
            copies.append(pltpu.make_async_remote_copy(
                src_ref=block, dst_ref=block, send_sem=send.at[3 * j + k], recv_sem=recv.at[3 * j + k],
                device_id=(x, y, 1 - c), device_id_type=MESH))
    return copies


def _forward_start(name, lands):
    m = len(lands)

    def body(*refs):
        for cp in _forward_copies(refs[:m], refs[m], refs[m + 1]):
            cp.start()

    sems = pltpu.SemaphoreType.DMA((3 * m,))
    outs = pl.pallas_call(
        body, name=name,
        in_specs=[_HBM] * m, out_specs=[_SEM, _SEM] + [_HBM] * m,
        out_shape=[sems, sems] + [_hbm_like(a) for a in lands],
        input_output_aliases={i: 2 + i for i in range(m)},
        compiler_params=pltpu.CompilerParams(has_side_effects=_EFFECT),
    )(*lands)
    return outs[0], outs[1], list(outs[2:])


def _forward_wait(name, send, recv, lands, after):
    m = len(lands)

    def body(*refs):
        for cp in _forward_copies(refs[:m], refs[m], refs[m + 1]):
            cp.wait_send()
            cp.wait_recv()

    outs = pl.pallas_call(
        body, name=name,
        in_specs=[_HBM] * m + [_SEM, _SEM, _ANY], out_specs=[_HBM] * m,
        out_shape=[_hbm_like(a) for a in lands],
        input_output_aliases={i: i for i in range(m)},
        compiler_params=pltpu.CompilerParams(has_side_effects=_EFFECT),
    )(*lands, send, recv, after)
    return list(outs)


def _token_spec():
    return pl.BlockSpec(memory_space=pltpu.VMEM)


def _pair_start(name, stacks, lands, deps=()):
    n = len(stacks)
    n_dep = len(deps)

    def body(*refs):
        srcs, dsts = refs[:n], refs[n:2 * n]
        send, recv = refs[2 * n + n_dep], refs[2 * n + n_dep + 1]
        token = refs[-1]
        x, y, c = _mesh_pos()
        for w in range(n):
            for chip in range(4):
                pltpu.make_async_remote_copy(
                    src_ref=srcs[w].at[chip, 1 - c], dst_ref=dsts[w].at[chip],
                    send_sem=send.at[4 * w + chip], recv_sem=recv.at[4 * w + chip],
                    device_id=(x, y, 1 - c), device_id_type=MESH).start()
        token[...] = jnp.zeros_like(token)

    sems = pltpu.SemaphoreType.DMA((4 * n,))
    outs = pl.pallas_call(
        body, name=name,
        in_specs=[_HBM] * (2 * n) + [_ANY] * n_dep, out_specs=[_SEM, _SEM] + [_HBM] * (2 * n) + [_token_spec()],
        out_shape=[sems, sems] + [_hbm_like(a) for a in stacks] + [_hbm_like(a) for a in lands] + [_sds((8, LANES), F32)],
        input_output_aliases={i: 2 + i for i in range(2 * n)},
        compiler_params=pltpu.CompilerParams(has_side_effects=_EFFECT),
    )(*stacks, *lands, *deps)
    return outs[0], outs[1], list(outs[2:2 + n]), list(outs[2 + n:2 + 2 * n]), outs[-1]


def _pair_wait(name, send, recv, stacks, lands, after):
    n = len(stacks)

    def body(*refs):
        srcs, dsts = refs[:n], refs[n:2 * n]
        send_sems, recv_sems = refs[2 * n], refs[2 * n + 1]
        x, y, c = _mesh_pos()
        for w in range(n):
            for chip in range(4):
                cp = pltpu.make_async_remote_copy(
                    src_ref=srcs[w].at[chip, 1 - c], dst_ref=dsts[w].at[chip],
                    send_sem=send_sems.at[4 * w + chip], recv_sem=recv_sems.at[4 * w + chip],
                    device_id=(x, y, 1 - c), device_id_type=MESH)
                cp.wait_send()
                cp.wait_recv()

    outs = pl.pallas_call(
        body, name=name,
        in_specs=[_HBM] * (2 * n) + [_SEM, _SEM, _ANY], out_specs=[_HBM] * (2 * n),
        out_shape=[_hbm_like(a) for a in stacks] + [_hbm_like(a) for a in lands],
        input_output_aliases={i: i for i in range(2 * n)},
        compiler_params=pltpu.CompilerParams(has_side_effects=_EFFECT),
    )(*stacks, *lands, send, recv, after)
    return list(outs[:n]), list(outs[n:])


def _pair_add(name, stack, land, place, tr):
    _, _, r, c = stack.shape

    def body(place_ref, a_ref, b_ref, sums_ref, slots_ref):
        total = (a_ref[...].astype(F32) + b_ref[...].astype(F32)).astype(BF16)
        sums_ref[...] = total

        @pl.when(pl.program_id(1) == place_ref[1])
        def _():
            slots_ref[...] = total

    grid_spec = pltpu.PrefetchScalarGridSpec(
        num_scalar_prefetch=1, grid=(r // tr, 4),
        in_specs=[pl.BlockSpec((None, None, tr, c), lambda i, k, place_ref: (k, place_ref[0], i, 0)),
                  pl.BlockSpec((None, tr, c), lambda i, k, place_ref: (k, i, 0))],
        out_specs=[pl.BlockSpec((None, tr, c), lambda i, k, place_ref: (k, i, 0)),
                   pl.BlockSpec((None, tr, c), lambda i, k, place_ref: (place_ref[1], i, 0))])
    return pl.pallas_call(
        body, name=name, grid_spec=grid_spec, out_shape=[_sds((4, r, c), BF16)] * 2,
        compiler_params=_params(("parallel", "arbitrary"), 4 * tr * c * 2, 3 * tr * c * 4),
    )(place, stack, land)


def _chip_start(name, parts, lands):
    n = len(parts)

    def body(*refs):
        srcs, dsts = refs[:n], refs[n:2 * n]
        send, recv = refs[2 * n], refs[2 * n + 1]
        token = refs[-1]
        x, y, c = _mesh_pos()
        for w in range(n):
            for k, to in enumerate(_chip_peers(x, y, c)):
                pltpu.make_async_remote_copy(
                    src_ref=srcs[w].at[2 * to[0] + to[1]], dst_ref=dsts[w].at[2 * x + y],
                    send_sem=send.at[3 * w + k], recv_sem=recv.at[3 * w + k],
                    device_id=to, device_id_type=MESH).start()
        token[...] = jnp.zeros_like(token)

    sems = pltpu.SemaphoreType.DMA((3 * n,))
    outs = pl.pallas_call(
        body, name=name,
        in_specs=[_HBM] * (2 * n), out_specs=[_SEM, _SEM] + [_HBM] * (2 * n) + [_token_spec()],
        out_shape=[sems, sems] + [_hbm_like(a) for a in parts] + [_hbm_like(a) for a in lands] + [_sds((8, LANES), F32)],
        input_output_aliases={i: 2 + i for i in range(2 * n)},
        compiler_params=pltpu.CompilerParams(has_side_effects=_EFFECT),
    )(*parts, *lands)
    return outs[0], outs[1], list(outs[2:2 + n]), list(outs[2 + n:2 + 2 * n]), outs[-1]


def _chip_wait(name, send, recv, parts, lands, after):
    n = len(parts)

    def body(*refs):
        srcs, dsts = refs[:n], refs[n:2 * n]
        send_sems, recv_sems = refs[2 * n], refs[2 * n + 1]
        x, y, c = _mesh_pos()
        for w in range(n):
            for k, frm in enumerate(_chip_peers(x, y, c)):
                chip = 2 * frm[0] + frm[1]
                cp = pltpu.make_async_remote_copy(
                    src_ref=srcs[w].at[chip], dst_ref=dsts[w].at[chip],
                    send_sem=send_sems.at[3 * w + k], recv_sem=recv_sems.at[3 * w + k],
                    device_id=frm, device_id_type=MESH)
                cp.wait_send()
                cp.wait_recv()

    outs = pl.pallas_call(
        body, name=name,
        in_specs=[_HBM] * (2 * n) + [_SEM, _SEM, _ANY], out_specs=[_HBM] * (2 * n),
        out_shape=[_hbm_like(a) for a in parts] + [_hbm_like(a) for a in lands],
        input_output_aliases={i: i for i in range(2 * n)},
        compiler_params=pltpu.CompilerParams(has_side_effects=_EFFECT),
    )(*parts, *lands, send, recv, after)
    return list(outs[:n]), list(outs[n:])


def _row_tile(t):
    return min(t, 256)


def _rms_fwd(name, x, g):
    t, d = x.shape
    tm = _row_tile(t)

    def epilogue(_, ins, outs):
        xv = ins[0][...]
        r = lax.rsqrt(jnp.mean(xv * xv, axis=-1, keepdims=True) + RMS_EPS)
        outs[0][...] = (xv * r * ins[1][...]).astype(BF16)

    row = pl.BlockSpec((tm, d), lambda i, j, k: (i, 0))
    vec = pl.BlockSpec((1, d), lambda i, j, k: (0, 0))
    return _fused(name, (t // tm, 1, 1), [(x, row), (g, vec)], [(_sds((t, d), BF16), row)], [], epilogue,
                  temp_bytes=4 * tm * d * 4)[0]


def _rms_bwd(name, x, g, dh, resid, deps=(), with_bf16=False):
    t, d = x.shape
    tm = _row_tile(t)

    def epilogue(_, ins, outs):
        xv, gv, dhv = ins[0][...], ins[1][...], ins[2][...]
        r = lax.rsqrt(jnp.mean(xv * xv, axis=-1, keepdims=True) + RMS_EPS)
        xh = xv * r
        u = dhv * gv
        dot = jnp.mean(u * xh, axis=-1, keepdims=True)
        dx = ins[3][...] + r * (u - xh * dot)
        outs[0][...] = dx
        if with_bf16:
            outs[2][...] = dx.astype(BF16)

        @pl.when(pl.program_id(0) == 0)
        def _():
            outs[1][...] = jnp.zeros_like(outs[1])

        outs[1][0:1, :] += jnp.sum(dhv * xh, axis=0, keepdims=True)

    row = pl.BlockSpec((tm, d), lambda i, j, k: (i, 0))
    vec = pl.BlockSpec((1, d), lambda i, j, k: (0, 0))
    acc = pl.BlockSpec((8, d), lambda i, j, k: (0, 0))
    outs = [(_sds((t, d), F32), row), (_sds((8, d), F32), acc)] + ([(_sds((t, d), BF16), row)] if with_bf16 else [])
    return _fused(name, (t // tm, 1, 1), [(x, row), (g, vec), (dh, row), (resid, row)], outs, [], epilogue,
                  temp_bytes=6 * tm * d * 4, semantics=("arbitrary", "arbitrary", "arbitrary"), deps=deps)


def _ffn_up(name, h, wgu, parity=None, into=None):
    t, d = h.shape
    nb = wgu.shape[2]
    f = 4 * nb
    tm = min(t, 512)

    def body(h_ref, wg_ref, wu_ref, gu_ref, a_ref):
        hv = h_ref[...]
        for c0 in range(0, nb, MXU_COLS):
            cs = slice(c0, min(c0 + MXU_COLS, nb))
            g = jnp.dot(hv, wg_ref[:, cs], preferred_element_type=F32)
            u = jnp.dot(hv, wu_ref[:, cs], preferred_element_type=F32)
            gu_ref[0, :, cs] = g.astype(BF16)
            gu_ref[1, :, cs] = u.astype(BF16)
            a_ref[:, cs] = (g * _sigmoid(g) * u).astype(BF16)

    blocks = tm * d * 2 + 2 * d * nb * 2 + 3 * tm * nb * 2
    params = _params(("parallel", "parallel"), blocks, 8 * tm * MXU_COLS * 4)
    out_shape = [_sds((2, t, f), BF16), _sds((t, f), BF16)]
    if parity is None:
        return pl.pallas_call(
            body, name=name, grid=(4, t // tm),
            in_specs=[pl.BlockSpec((tm, d), lambda j, i: (i, 0)),
                      pl.BlockSpec((None, d, nb), lambda j, i: (j, 0, 0)),
                      pl.BlockSpec((None, d, nb), lambda j, i: (j + 4, 0, 0))],
            out_specs=[pl.BlockSpec((2, tm, nb), lambda j, i: (0, i, j)),
                       pl.BlockSpec((tm, nb), lambda j, i: (i, j))],
            out_shape=out_shape, compiler_params=params,
        )(h, wgu, wgu)

    def half_body(parity_ref, h_ref, wg_ref, wu_ref, *rest):
        body(h_ref, wg_ref, wu_ref, rest[-2], rest[-1])

    n_pass = 0 if into is None else 2
    grid_spec = pltpu.PrefetchScalarGridSpec(
        num_scalar_prefetch=1, grid=(2, t // tm),
        in_specs=[pl.BlockSpec((tm, d), lambda jj, i, p: (i, 0)),
                  pl.BlockSpec((None, d, nb), lambda jj, i, p: (2 * jj + p[0], 0, 0)),
                  pl.BlockSpec((None, d, nb), lambda jj, i, p: (2 * jj + p[0] + 4, 0, 0))] + [_ANY] * n_pass,
        out_specs=[pl.BlockSpec((2, tm, nb), lambda jj, i, p: (0, i, 2 * jj + p[0])),
                   pl.BlockSpec((tm, nb), lambda jj, i, p: (i, 2 * jj + p[0]))])
    return pl.pallas_call(
        half_body, name=name, grid_spec=grid_spec, out_shape=out_shape,
        input_output_aliases={} if into is None else {4: 0, 5: 1}, compiler_params=params,
    )(parity, h, wgu, wgu, *(into or ()))


def _ffn_down(name, a, wd, x, target=None):
    t, f = a.shape
    d = wd.shape[1]
    tm = min(t, 512)
    tn = min(d, 1024)
    blk = pl.BlockSpec((tm, tn), lambda j, i, k: (i, j))
    ins = [(a, pl.BlockSpec((tm, f), lambda j, i, k: (i, 0))), (wd, pl.BlockSpec((f, tn), lambda j, i, k: (0, j))), (x, blk)]

    if target is None:
        def epilogue(acc, ins, outs):
            outs[0][...] = ins[2][...] + 0.5 * acc

        return _fused(name, (d // tn, t // tm, 1), ins, [(_sds((t, d), F32), blk)],
                      [(0, 1, NN)], epilogue, temp_bytes=2 * tm * tn * 4)[0]

    def epilogue(acc, ins, outs):
        e = ins[2][...] + 0.5 * acc - ins[3][...]
        outs[0][...] = e * (1.0 / d)
        outs[2][...] = (e * (1.0 / d)).astype(BF16)

        @pl.when((pl.program_id(0) == 0) & (pl.program_id(1) == 0))
        def _():
            outs[1][...] = jnp.zeros_like(outs[1])

        part = jnp.sum(jnp.sum(e * e, axis=1, keepdims=True), axis=0, keepdims=True)
        outs[1][...] += jnp.broadcast_to(part, outs[1].shape)

    return _fused(name, (d // tn, t // tm, 1), ins + [(target, blk)],
                  [(_sds((t, d), F32), blk), (_sds((8, LANES), F32), pl.BlockSpec((8, LANES), lambda j, i, k: (0, 0))),
                   (_sds((t, d), BF16), blk)],
                  [(0, 1, NN)], epilogue, temp_bytes=3 * tm * tn * 4,
                  semantics=("arbitrary", "arbitrary", "arbitrary"))


def _ffn_bwd_act(name, dy, wd, gu, deps=()):
    t, d = dy.shape
    f = wd.shape[0]
    nb = f // 4
    tm = min(t, 512)

    def body(dy_ref, wd_ref, gu_ref, *rest):
        dgu_ref, a_ref = rest[-2], rest[-1]
        dyv = dy_ref[...].astype(BF16)
        for c0 in range(0, nb, MXU_COLS):
            cs = slice(c0, min(c0 + MXU_COLS, nb))
            da = 0.5 * lax.dot_general(dyv, wd_ref[cs, :], NT, preferred_element_type=F32)
            g = gu_ref[0, :, cs].astype(F32)
            u = gu_ref[1, :, cs].astype(F32)
            s = _sigmoid(g)
            silu = g * s
            dgu_ref[0, :, cs] = (da * u * (s * (1.0 + g * (1.0 - s)))).astype(BF16)
            dgu_ref[1, :, cs] = (da * silu).astype(BF16)
            a_ref[:, cs] = (silu * u).astype(BF16)

    blocks = tm * d * 4 + nb * d * 2 + 5 * tm * nb * 2
    return pl.pallas_call(
        body, name=name, grid=(4, t // tm),
        in_specs=[pl.BlockSpec((tm, d), lambda j, i: (i, 0)),
                  pl.BlockSpec((nb, d), lambda j, i: (j, 0)),
                  pl.BlockSpec((2, tm, nb), lambda j, i: (0, i, j))] + [_ANY] * len(deps),
        out_specs=[pl.BlockSpec((2, tm, nb), lambda j, i: (0, i, j)), pl.BlockSpec((tm, nb), lambda j, i: (i, j))],
        out_shape=[_sds((2, t, f), BF16), _sds((t, f), BF16)],
        compiler_params=_params(("parallel", "parallel"), blocks, tm * d * 2 + 8 * tm * MXU_COLS * 4),
    )(dy, wd, gu, *deps)


def _ffn_bwd_dwd(name, a, dy, deps=(), side=None):
    t, f = a.shape
    d = dy.shape[1]
    tm = f // 4
    tn = min(d, 512)

    def epilogue(acc, ins, outs):
        outs[0][...] = (0.5 * acc).astype(BF16)

    return _fused(name, (4, d // tn, 1),
                  [(a, pl.BlockSpec((t, tm), lambda i, j, k: (0, i))),
                   (dy, pl.BlockSpec((t, tn), lambda i, j, k: (0, j)))],
                  [(_sds((f, d), BF16), pl.BlockSpec((tm, tn), lambda i, j, k: (i, j)))],
                  [(0, 1, TN)], epilogue, temp_bytes=t * tn * 2 + 2 * tm * tn * 4, deps=deps, side=side)


def _ffn_bwd_dh(name, dgu, wgu, deps=(), side=None):
    _, t, f = dgu.shape
    d, nb = wgu.shape[1], wgu.shape[2]
    tm = min(t, 512)

    def products(ins):
        return (lax.dot_general(ins[0][:, 0:nb], ins[1][0], NT, preferred_element_type=F32)
                + lax.dot_general(ins[0][:, nb:2 * nb], ins[1][1], NT, preferred_element_type=F32))

    def epilogue(acc, ins, outs):
        outs[0][...] = acc

    return _fused(name, (t // tm, 1, 4),
                  [(dgu, pl.BlockSpec((None, tm, 2 * nb), lambda i, j, k: (k // 2, i, k % 2))),
                   (wgu, pl.BlockSpec((2, d, nb), lambda i, j, k: (k, 0, 0)))],
                  [(_sds((t, d), F32), pl.BlockSpec((tm, d), lambda i, j, k: (i, 0)))],
                  products, epilogue, nk=4, acc_shape=(tm, d), temp_bytes=tm * d * 4, deps=deps, side=side)


def _ffn_bwd_dwgu(name, h, dgu, deps=(), side=None, rows=None):
    t, d = h.shape
    nb = dgu.shape[2] // 4
    tm = min(d, 512)
    row0, nrows = rows if rows is not None else (0, d)
    j0 = row0 // tm

    def epilogue(acc, ins, outs):
        outs[0][...] = acc.astype(BF16)

    return _fused(name, (N_DEV, nrows // tm, 1),
                  [(h, pl.BlockSpec((t, tm), lambda i, j, k: (0, j0 + j))),
                   (dgu, pl.BlockSpec((None, t, nb), lambda i, j, k: (i // 4, 0, i % 4)))],
                  [(_sds((N_DEV, nrows, nb), BF16), pl.BlockSpec((None, tm, nb), lambda i, j, k: (i, j, 0)))],
                  [(0, 1, TN)], epilogue, temp_bytes=2 * tm * nb * 4, deps=deps, side=side)


def _proj(h, w_in):
    t, d = h.shape
    nb = w_in.shape[3]
    tm = min(t, 512)

    def body(h_ref, w_ref, o_ref):
        hv = h_ref[...]
        o_ref[:, 0:nb] = jnp.dot(hv, w_ref[0], preferred_element_type=F32).astype(BF16)
        o_ref[:, nb:2 * nb] = jnp.dot(hv, w_ref[1], preferred_element_type=F32).astype(BF16)

    blocks = tm * d * 2 + 2 * d * nb * 2 + tm * 2 * nb * 4
    return pl.pallas_call(
        body, name="mix_proj", grid=(4, t // tm),
        in_specs=[pl.BlockSpec((tm, d), lambda j, i: (i, 0)),
                  pl.BlockSpec((None, 2, d, nb), lambda j, i: (j, 0, 0, 0))],
        out_specs=pl.BlockSpec((tm, 2 * nb), lambda j, i: (i, j)),
        out_shape=_sds((t, N_DEV * nb), BF16),
        compiler_params=_params(("parallel", "parallel"), blocks, 2 * tm * nb * 4),
    )(h, w_in)


def _shift_rows(u, k):
    t = u.shape[0]
    rolled = pltpu.roll(u, k % t, axis=0)
    row = lax.broadcasted_iota(jnp.int32, u.shape, 0)
    keep = (row >= k) if k > 0 else (row < t + k)
    return jnp.where(keep, rolled, 0.0)


def _conv_fwd(proj, conv_w):
    t = proj.shape[0]
    cw = conv_w.shape[1]
    tc = min(cw, 256)
    nc = cw // tc

    def epilogue(_, ins, outs):
        u = ins[2][...].astype(F32) * ins[0][...].astype(F32)
        w = ins[3][...]
        y = u * w[2:3, :] + _shift_rows(u, 1) * w[1:2, :] + _shift_rows(u, 2) * w[0:1, :]
        outs[0][...] = (ins[1][...].astype(F32) * y).astype(BF16)

    def col(seg):
        return pl.BlockSpec((t, tc), lambda i, j, k: (0, seg * nc + i))

    return _fused("conv_fwd", (nc, 1, 1),
                  [(proj, col(0)), (proj, col(1)), (proj, col(2)),
                   (conv_w, pl.BlockSpec((8, tc), lambda i, j, k: (0, i)))],
                  [(_sds((t, cw), BF16), pl.BlockSpec((t, tc), lambda i, j, k: (0, i)))],
                  [], epilogue, temp_bytes=6 * t * tc * 4)[0]


def _conv_bwd(proj, conv_w, dca, dproj, deps=()):
    t = proj.shape[0]
    cw = conv_w.shape[1]
    tc = min(cw, 256)
    nc = cw // tc
    n_pass = len(deps) + 1

    def body(xc_ref, bg_ref, cg_ref, w_ref, dc_ref, *rest):
        dp_ref, dw_ref, stash = rest[n_pass:]
        s = pl.program_id(1)

        @pl.when(s == 0)
        def _():
            xc, bg, cg = xc_ref[...].astype(F32), bg_ref[...].astype(F32), cg_ref[...].astype(F32)
            w, dc = w_ref[...], dc_ref[...]
            u = cg * xc
            u1, u2 = _shift_rows(u, 1), _shift_rows(u, 2)
            y = u * w[2:3, :] + u1 * w[1:2, :] + u2 * w[0:1, :]
            dconv = dc * bg
            du = dconv * w[2:3, :] + _shift_rows(dconv, -1) * w[1:2, :] + _shift_rows(dconv, -2) * w[0:1, :]
            stash[0] = (du * cg).astype(BF16)
            stash[1] = (dc * y).astype(BF16)
            stash[2] = (du * xc).astype(BF16)
            dw_ref[...] = jnp.zeros_like(dw_ref)
            dw_ref[0:1, :] = jnp.sum(dconv * u2, axis=0, keepdims=True)
            dw_ref[1:2, :] = jnp.sum(dconv * u1, axis=0, keepdims=True)
            dw_ref[2:3, :] = jnp.sum(dconv * u, axis=0, keepdims=True)

        dp_ref[...] = stash[s]

    def col(seg):
        return pl.BlockSpec((t, tc), lambda i, s: (0, seg * nc + i))

    own = pl.BlockSpec((t, tc), lambda i, s: (0, i))
    wspec = pl.BlockSpec((8, tc), lambda i, s: (0, i))
    blocks = 3 * t * tc * 2 + 2 * 8 * tc * 4 + t * tc * 4 + t * tc * 2
    return pl.pallas_call(
        body, name="conv_bwd", grid=(nc, 3),
        in_specs=[col(0), col(1), col(2), wspec, own] + [_ANY] * n_pass,
        out_specs=[pl.BlockSpec((t, tc), lambda i, s: (0, s * nc + i)), wspec],
        out_shape=[_sds(dproj.shape, BF16), _sds((8, cw), F32)],
        scratch_shapes=[pltpu.VMEM((3, t, tc), BF16)],
        input_output_aliases={5 + len(deps): 0},
        compiler_params=_params(("parallel", "arbitrary"), blocks, 3 * t * tc * 2 + 10 * t * tc * 4),
    )(proj, proj, proj, conv_w, dca, *deps, dproj)


def _split3(x):
    hi = x.astype(BF16)
    r1 = x - hi.astype(F32)
    mid = r1.astype(BF16)
    lo = (r1 - mid.astype(F32)).astype(BF16)
    return hi, mid, lo


def _head_selector(width):
    r = lax.broadcasted_iota(jnp.int32, (width, LANES), 0)
    c = lax.broadcasted_iota(jnp.int32, (width, LANES), 1)
    return (lax.shift_right_logical(r, 6) == c).astype(BF16)


def _head_sum(x, sel):
    return sum(jnp.dot(p, sel, preferred_element_type=F32) for p in _split3(x))


def _head_bcast(r, sel):
    return sum(lax.dot_general(p, sel, NT, preferred_element_type=F32) for p in _split3(r))


def _rope(x, c, sa, sb):
    n = x.shape[1]
    return x * c + pltpu.roll(x, n - ROT_DIM // 2, axis=1) * sa + pltpu.roll(x, ROT_DIM // 2, axis=1) * sb


def _rope_t(d, c, sa, sb):
    n = d.shape[1]
    return d * c + pltpu.roll(d * sa, ROT_DIM // 2, axis=1) + pltpu.roll(d * sb, n - ROT_DIM // 2, axis=1)


def _tile_lanes(tab, width):
    return tab if width == tab.shape[1] else jnp.tile(tab, (1, width // tab.shape[1]))


def _qk_prep(proj, gq, gk, rope_tabs, cw, kw):
    t = proj.shape[0]
    tm = _row_tile(t)

    def epilogue(_, ins, outs):
        c, sa, sb = ins[5][...], ins[6][...], ins[7][...]
        for src, gain, dst, width in ((0, 3, 0, cw), (1, 4, 1, kw)):
            xv = ins[src][...].astype(F32)
            sel = _head_selector(width)
            r = lax.rsqrt(_head_sum(xv * xv, sel) * (1.0 / HEAD_DIM) + RMS_EPS)
            xn = xv * _head_bcast(r, sel) * ins[gain][...]
            outs[dst][...] = _rope(xn, _tile_lanes(c, width), _tile_lanes(sa, width), _tile_lanes(sb, width)).astype(BF16)
        outs[2][...] = ins[2][...].astype(BF16)

    kblk = cw // kw
    tab = pl.BlockSpec((tm, LANES), lambda i, j, k: (i, 0))
    kspec = pl.BlockSpec((tm, kw), lambda i, j, k: (i, 0))
    return _fused("qk_prep", (t // tm, 1, 1),
                  [(proj, pl.BlockSpec((tm, cw), lambda i, j, k: (i, 3))),
                   (proj, pl.BlockSpec((tm, kw), lambda i, j, k: (i, 4 * kblk))),
                   (proj, pl.BlockSpec((tm, kw), lambda i, j, k: (i, 4 * kblk + 1))),
                   (gq, pl.BlockSpec((1, cw), lambda i, j, k: (0, 0))),
                   (gk, pl.BlockSpec((1, kw), lambda i, j, k: (0, 0))),
                   (rope_tabs[0], tab), (rope_tabs[1], tab), (rope_tabs[2], tab)],
                  [(_sds((t, cw), BF16), pl.BlockSpec((tm, cw), lambda i, j, k: (i, 0))),
                   (_sds((t, kw), BF16), kspec), (_sds((t, kw), BF16), kspec)],
                  [], epilogue, temp_bytes=12 * tm * cw * 4)


def _qk_prep_bwd(proj, gq, gk, rope_tabs, dq, dkc, dkp, dvc, dvp, dproj, cw, kw):
    t = proj.shape[0]
    tm = BLOCK
    nblk = t // tm

    def epilogue(_, ins, outs):
        c, sa, sb = ins[5][...], ins[6][...], ins[7][...]
        has_next = (pl.program_id(0) < nblk - 1).astype(F32)
        dk = ins[9][...] + has_next * ins[10][...]
        dv = ins[11][...] + has_next * ins[12][...]
        pieces = []
        for src, gain, dval, dst, width in ((0, 3, ins[8][...], 1, cw), (1, 4, dk, 2, kw)):
            xv, gv = ins[src][...].astype(F32), ins[gain][...]
            sel = _head_selector(width)
            r = _head_bcast(lax.rsqrt(_head_sum(xv * xv, sel) * (1.0 / HEAD_DIM) + RMS_EPS), sel)
            xh = xv * r
            dxn = _rope_t(dval, _tile_lanes(c, width), _tile_lanes(sa, width), _tile_lanes(sb, width))
            u = dxn * gv
            dot = _head_bcast(_head_sum(u * xh, sel), sel) * (1.0 / HEAD_DIM)
            pieces.append((r * (u - xh * dot)).astype(BF16))
            ri = lax.broadcasted_iota(jnp.int32, (width, LANES), 0)
            ci = lax.broadcasted_iota(jnp.int32, (width, LANES), 1)
            fold = (lax.bitwise_and(ri, HEAD_DIM - 1) == ci).astype(BF16)
            colsum = jnp.broadcast_to(jnp.sum(dxn * xh, axis=0, keepdims=True), (8, width))
            part = sum(jnp.dot(p, fold, preferred_element_type=F32) for p in _split3(colsum))

            @pl.when(pl.program_id(0) == 0)
            def _():
                outs[dst][...] = jnp.zeros_like(outs[dst])

            outs[dst][0:1, :] += part[0:1, :]
        outs[0][:, 0:cw] = pieces[0]
        outs[0][:, cw:cw + kw] = pieces[1]
        outs[0][:, cw + kw:cw + 2 * kw] = dv.astype(BF16)

    kblk = cw // kw
    tab = pl.BlockSpec((tm, LANES), lambda i, j, k: (i, 0))
    kcur = pl.BlockSpec((tm, kw), lambda i, j, k: (i, 0))
    knext = pl.BlockSpec((tm, kw), lambda i, j, k: (jnp.minimum(i + 1, nblk - 1), 0))
    acc = pl.BlockSpec((8, LANES), lambda i, j, k: (0, 0))
    return _fused("qk_prep_bwd", (nblk, 1, 1),
                  [(proj, pl.BlockSpec((tm, cw), lambda i, j, k: (i, 3))),
                   (proj, pl.BlockSpec((tm, kw), lambda i, j, k: (i, 4 * kblk))),
                   (proj, pl.BlockSpec((tm, kw), lambda i, j, k: (i, 4 * kblk + 1))),
                   (gq, pl.BlockSpec((1, cw), lambda i, j, k: (0, 0))),
                   (gk, pl.BlockSpec((1, kw), lambda i, j, k: (0, 0))),
                   (rope_tabs[0], tab), (rope_tabs[1], tab), (rope_tabs[2], tab),
                   (dq, pl.BlockSpec((tm, cw), lambda i, j, k: (i, 0))),
                   (dkc, kcur), (dkp, knext), (dvc, kcur), (dvp, knext)],
                  [(_sds(dproj.shape, BF16), pl.BlockSpec((tm, cw + 2 * kw), lambda i, j, k: (i, 3 * cw // (cw + 2 * kw)))),
                   (_sds((8, LANES), F32), acc), (_sds((8, LANES), F32), acc)],
                  [], epilogue, temp_bytes=16 * tm * cw * 4, semantics=("arbitrary", "arbitrary", "arbitrary"),
                  deps=(dproj,), alias_last_dep_to=0)


def _attn_mask(n):
    key = lax.broadcasted_iota(jnp.int32, (2 * BLOCK, GROUP * BLOCK), 0)
    qry = lax.bitwise_and(lax.broadcasted_iota(jnp.int32, (2 * BLOCK, GROUP * BLOCK), 1), BLOCK - 1)
    return (key > qry) & (key <= qry + BLOCK) & ((key >= BLOCK) | (n > 0))


def _stack_heads(x, h):
    return jnp.concatenate([x[:, (h * GROUP + g) * HEAD_DIM:(h * GROUP + g + 1) * HEAD_DIM] for g in range(GROUP)], axis=0)


def _softmax_with_sink(q4, k2, sink_ref, h, valid):
    sink = jnp.concatenate([sink_ref[h * GROUP + g:h * GROUP + g + 1, :] for g in range(GROUP)], axis=1)
    s = lax.dot_general(k2, q4, NT, preferred_element_type=F32) * ATTN_SCALE
    s = jnp.where(valid, s, NEG_INF)
    m = jnp.maximum(jnp.max(s, axis=0, keepdims=True), sink)
    p = jnp.exp(s - m)
    es = jnp.exp(sink - m)
    inv = 1.0 / (jnp.sum(p, axis=0, keepdims=True) + es)
    return p * inv, es * inv


def _attn_fwd(qn, kn, vb, sink_rows):
    t, cw = qn.shape
    kw = kn.shape[1]
    nkv = kw // HEAD_DIM

    def body(q_ref, kp_ref, kc_ref, vp_ref, vc_ref, sink_ref, o_ref):
        valid = _attn_mask(pl.program_id(0))
        qv = q_ref[...]
        kp, kc, vp, vc = kp_ref[...], kc_ref[...], vp_ref[...], vc_ref[...]
        outs = []
        for h in range(nkv):
            hs = slice(h * HEAD_DIM, (h + 1) * HEAD_DIM)
            k2 = jnp.concatenate([kp[:, hs], kc[:, hs]], axis=0)
            v2 = jnp.concatenate([vp[:, hs], vc[:, hs]], axis=0)
            pn, _ = _softmax_with_sink(_stack_heads(qv, h), k2, sink_ref, h, valid)
            o4 = lax.dot_general(pn.astype(BF16), v2, TN, preferred_element_type=F32)
            outs += [o4[g * BLOCK:(g + 1) * BLOCK] for g in range(GROUP)]
        o_ref[...] = jnp.concatenate(outs, axis=-1).astype(BF16)

    cur = lambda n: (n, 0)
    prev = lambda n: (jnp.maximum(n - 1, 0), 0)
    return pl.pallas_call(
        body, name="attn_fwd", grid=(t // BLOCK,),
        in_specs=[pl.BlockSpec((BLOCK, cw), cur),
                  pl.BlockSpec((BLOCK, kw), prev), pl.BlockSpec((BLOCK, kw), cur),
                  pl.BlockSpec((BLOCK, kw), prev), pl.BlockSpec((BLOCK, kw), cur),
                  pl.BlockSpec(sink_rows.shape, lambda n: (0, 0))],
        out_specs=pl.BlockSpec((BLOCK, cw), cur),
        out_shape=_sds((t, cw), BF16),
        compiler_params=_params(("parallel",), BLOCK * (cw + 4 * kw) * 2 + BLOCK * cw * 2, 8 << 20),
    )(qn, kn, kn, vb, vb, sink_rows)


def _attn_bwd(qn, kn, vb, sink_rows, do):
    t, cw = qn.shape
    kw = kn.shape[1]
    nkv = kw // HEAD_DIM
    nq = nkv * GROUP

    def body(q_ref, kp_ref, kc_ref, vp_ref, vc_ref, sink_ref, do_ref,
             dq_ref, dkc_ref, dkp_ref, dvc_ref, dvp_ref, dsink_ref):
        n = pl.program_id(0)
        valid = _attn_mask(n)
        qv, dov = q_ref[...], do_ref[...]
        kp, kc, vp, vc = kp_ref[...], kc_ref[...], vp_ref[...], vc_ref[...]
        dqs, dks, dvs, dsinks = [], [], [], []
        for h in range(nkv):
            hs = slice(h * HEAD_DIM, (h + 1) * HEAD_DIM)
            k2 = jnp.concatenate([kp[:, hs], kc[:, hs]], axis=0)
            v2 = jnp.concatenate([vp[:, hs], vc[:, hs]], axis=0)
            q4 = _stack_heads(qv, h)
            dob = _stack_heads(dov, h).astype(BF16)
            pn, psink = _softmax_with_sink(q4, k2, sink_ref, h, valid)
            dpn = lax.dot_general(v2, dob, NT, preferred_element_type=F32)
            dvs.append(jnp.dot(pn.astype(BF16), dob, preferred_element_type=F32))
            delta = jnp.sum(pn * dpn, axis=0, keepdims=True)
            ds = (pn * (dpn - delta) * ATTN_SCALE).astype(BF16)
            dks.append(jnp.dot(ds, q4, preferred_element_type=F32))
            dq4 = lax.dot_general(ds, k2, TN, preferred_element_type=F32)
            dsink4 = -psink * delta
            for g in range(GROUP):
                dqs.append(dq4[g * BLOCK:(g + 1) * BLOCK])
                dsinks.append(jnp.broadcast_to(jnp.sum(dsink4[:, g * BLOCK:(g + 1) * BLOCK], axis=1, keepdims=True), (1, LANES)))
        dq_ref[...] = jnp.concatenate(dqs, axis=-1)
        dkp_ref[...] = jnp.concatenate([d[:BLOCK] for d in dks], axis=-1)
        dkc_ref[...] = jnp.concatenate([d[BLOCK:] for d in dks], axis=-1)
        dvp_ref[...] = jnp.concatenate([d[:BLOCK] for d in dvs], axis=-1)
        dvc_ref[...] = jnp.concatenate([d[BLOCK:] for d in dvs], axis=-1)

        @pl.when(n == 0)
        def _():
            dsink_ref[...] = jnp.zeros_like(dsink_ref)

        dsink_ref[...] += jnp.concatenate(dsinks, axis=0)

    cur = lambda n: (n, 0)
    prev = lambda n: (jnp.maximum(n - 1, 0), 0)
    kspec = pl.BlockSpec((BLOCK, kw), cur)
    return pl.pallas_call(
        body, name="attn_bwd", grid=(t // BLOCK,),
        in_specs=[pl.BlockSpec((BLOCK, cw), cur),
                  pl.BlockSpec((BLOCK, kw), prev), kspec,
                  pl.BlockSpec((BLOCK, kw), prev), kspec,
                  pl.BlockSpec(sink_rows.shape, lambda n: (0, 0)),
                  pl.BlockSpec((BLOCK, cw), cur)],
        out_specs=[pl.BlockSpec((BLOCK, cw), cur), kspec, kspec, kspec, kspec,
                   pl.BlockSpec((nq, LANES), lambda n: (0, 0))],
        out_shape=[_sds((t, cw), F32)] + [_sds((t, kw), F32)] * 4 + [_sds((nq, LANES), F32)],
        compiler_params=_params(("arbitrary",), BLOCK * (cw + 4 * kw) * 2 + 2 * BLOCK * cw * 4 + 4 * BLOCK * kw * 4, 12 << 20),
    )(qn, kn, kn, vb, vb, sink_rows, do)


def _mix_out(ca, o, woc, woa, proj):
    t, cw = ca.shape
    nb = woc.shape[2]
    d = N_DEV * nb
    tm = min(t, 1024)
    ga0 = (3 * cw + cw + 2 * (cw // 4)) // nb

    def body(ca_ref, o_ref, woc_ref, woa_ref, ga_ref, gb_ref, m_ref, ya_ref, yb_ref):
        ya = jnp.dot(ca_ref[...], woc_ref[...], preferred_element_type=F32)
        yb = jnp.dot(o_ref[...], woa_ref[...], preferred_element_type=F32)
        ya_ref[...] = ya.astype(BF16)
        yb_ref[...] = yb.astype(BF16)
        m_ref[...] = (_sigmoid(ga_ref[...].astype(F32)) * ya + _sigmoid(gb_ref[...].astype(F32)) * yb).astype(BF16)

    act = pl.BlockSpec((tm, cw), lambda i, j: (i, 0))
    wsp = pl.BlockSpec((None, cw, nb), lambda i, j: (j, 0, 0))
    osp = pl.BlockSpec((tm, nb), lambda i, j: (i, j))
    blocks = 2 * tm * cw * 2 + 2 * cw * nb * 2 + 2 * tm * nb * 4 + 3 * tm * nb * 2
    return pl.pallas_call(
        body, name="mix_out", grid=(t // tm, N_DEV),
        in_specs=[act, act, wsp, wsp,
                  pl.BlockSpec((tm, nb), lambda i, j: (i, ga0 + j)),
                  pl.BlockSpec((tm, nb), lambda i, j: (i, ga0 + N_DEV + j))],
        out_specs=[osp, osp, osp],
        out_shape=[_sds((t, d), BF16)] * 3,
        compiler_params=_params(("parallel", "parallel"), blocks, 6 * tm * nb * 4),
    )(ca, o, woc, woa, proj, proj)


def _mix_residual(merged, wo, x):
    t, d = x.shape
    tm = min(t, 512)

    def epilogue(acc, ins, outs):
        outs[0][...] = ins[2][...] + acc

    row = pl.BlockSpec((tm, d), lambda i, j, k: (i, 0))
    return _fused("mix_residual", (t // tm, 1, 1),
                  [(merged, row), (wo, pl.BlockSpec((d, d), lambda i, j, k: (0, 0))), (x, row)],
                  [(_sds((t, d), F32), row)], [(0, 1, NN)], epilogue, temp_bytes=2 * tm * d * 4)[0]


def _mix_bwd_gates(dx, wo, ya, yb, proj, cw):
    t, d = dx.shape
    tm = min(t, 1024)
    tn = min(d, 512)
    nj = d // tn
    ga0 = (4 * cw + 2 * (cw // 4)) // tn

    def body(dx_ref, wo_ref, ya_ref, yb_ref, ga_ref, gb_ref, dya_ref, dyb_ref, dp_ref, stash):
        s = pl.program_id(2)

        @pl.when(s == 0)
        def _():
            acc = lax.dot_general(dx_ref[...].astype(BF16), wo_ref[...], NT, preferred_element_type=F32)
            sa, sb = _sigmoid(ga_ref[...].astype(F32)), _sigmoid(gb_ref[...].astype(F32))
            dya_ref[...] = (acc * sa).astype(BF16)
            dyb_ref[...] = (acc * sb).astype(BF16)
            stash[0] = (acc * ya_ref[...].astype(F32) * sa * (1.0 - sa)).astype(BF16)
            stash[1] = (acc * yb_ref[...].astype(F32) * sb * (1.0 - sb)).astype(BF16)

        dp_ref[...] = stash[s]

    blk = pl.BlockSpec((tm, tn), lambda i, j, s: (i, j))
    blocks = tm * d * dx.dtype.itemsize + tn * d * 2 + 7 * tm * tn * 2
    return pl.pallas_call(
        body, name="mix_bwd_gates", grid=(t // tm, nj, 2),
        in_specs=[pl.BlockSpec((tm, d), lambda i, j, s: (i, 0)),
                  pl.BlockSpec((tn, d), lambda i, j, s: (j, 0)),
                  blk, blk,
                  pl.BlockSpec((tm, tn), lambda i, j, s: (i, ga0 + j)),
                  pl.BlockSpec((tm, tn), lambda i, j, s: (i, ga0 + nj + j))],
        out_specs=[blk, blk, pl.BlockSpec((tm, tn), lambda i, j, s: (i, ga0 + s * nj + j))],
        out_shape=[_sds((t, d), BF16), _sds((t, d), BF16), _sds(proj.shape, BF16)],
        scratch_shapes=[pltpu.VMEM((2, tm, tn), BF16)],
        compiler_params=_params(("parallel", "parallel", "arbitrary"), blocks, 2 * tm * tn * 2 + 8 * tm * tn * 4),
    )(dx, wo, ya, yb, proj, proj)


def _tn_matmul(name, a, b, tm, out_dtype=BF16):
    t, m = a.shape
    n = b.shape[1]

    def epilogue(acc, ins, outs):
        outs[0][...] = acc.astype(out_dtype)

    return _fused(name, (m // tm, 1, 1),
                  [(a, pl.BlockSpec((t, tm), lambda i, j, k: (0, i))),
                   (b, pl.BlockSpec((t, n), lambda i, j, k: (0, 0)))],
                  [(_sds((m, n), out_dtype), pl.BlockSpec((tm, n), lambda i, j, k: (i, 0)))],
                  [(0, 1, TN)], epilogue, temp_bytes=2 * tm * n * 4)[0]


def _out_proj_bwd_act(dya, dyb, woc, woa, deps=()):
    t, d = dya.shape
    kdim, nb = woc.shape[1], woc.shape[2]
    tm = min(t, 512)

    def body(dya_ref, dyb_ref, woc_ref, woa_ref, *rest):
        for dy_ref, w_ref, o_ref in ((dya_ref, woc_ref, rest[-2]), (dyb_ref, woa_ref, rest[-1])):
            total = None
            for j in range(N_DEV):
                part = lax.dot_general(dy_ref[:, j * nb:(j + 1) * nb], w_ref[j], NT, preferred_element_type=F32)
                total = part if total is None else total + part
            o_ref[...] = total

    row = pl.BlockSpec((tm, d), lambda i: (i, 0))
    wsp = pl.BlockSpec((N_DEV, kdim, nb), lambda i: (0, 0, 0))
    osp = pl.BlockSpec((tm, kdim), lambda i: (i, 0))
    blocks = 2 * tm * d * 2 + 2 * N_DEV * kdim * nb * 2 + 2 * tm * kdim * 4
    return pl.pallas_call(
        body, name="mix_bwd_dca_do", grid=(t // tm,),
        in_specs=[row, row, wsp, wsp] + [_ANY] * len(deps), out_specs=[osp, osp],
        out_shape=[_sds((t, kdim), F32)] * 2,
        compiler_params=_params(("parallel",), blocks, 4 * tm * kdim * 4),
    )(dya, dyb, woc, woa, *deps)


def _out_proj_bwd_w(ca, o, dya, dyb, nb):
    t, kdim = ca.shape

    def body(ca_ref, o_ref, dya_ref, dyb_ref, dwoc_ref, dwoa_ref):
        dwoc_ref[...] = lax.dot_general(ca_ref[...], dya_ref[...], TN, preferred_element_type=F32).astype(BF16)
        dwoa_ref[...] = lax.dot_general(o_ref[...], dyb_ref[...], TN, preferred_element_type=F32).astype(BF16)

    act = pl.BlockSpec((t, kdim), lambda j: (0, 0))
    col = pl.BlockSpec((t, nb), lambda j: (0, j))
    osp = pl.BlockSpec((None, kdim, nb), lambda j: (j, 0, 0))
    blocks = 2 * t * kdim * 2 + 2 * t * nb * 2 + 2 * kdim * nb * 2
    return pl.pallas_call(
        body, name="mix_bwd_dwoc_dwoa", grid=(N_DEV,),
        in_specs=[act, act, col, col], out_specs=[osp, osp],
        out_shape=[_sds((N_DEV, kdim, nb), BF16)] * 2,
        compiler_params=_params(("parallel",), blocks, 4 * kdim * nb * 4),
    )(ca, o, dya, dyb)


def _proj_bwd_act(dproj, w_in, deps=()):
    t, n = dproj.shape
    d, nb = w_in.shape[2], w_in.shape[3]
    tm = min(t, 512)

    def epilogue(acc, ins, outs):
        outs[0][...] = acc

    def products(ins):
        return (lax.dot_general(ins[0][:, 0:nb], ins[1][0], NT, preferred_element_type=F32)
                + lax.dot_general(ins[0][:, nb:2 * nb], ins[1][1], NT, preferred_element_type=F32))

    return _fused("mix_bwd_dh", (t // tm, 1, 4),
                  [(dproj, pl.BlockSpec((tm, 2 * nb), lambda i, j, k: (i, k))),
                   (w_in, pl.BlockSpec((None, 2, d, nb), lambda i, j, k: (k, 0, 0, 0)))],
                  [(_sds((t, d), F32), pl.BlockSpec((tm, d), lambda i, j, k: (i, 0)))],
                  products, epilogue, nk=4, acc_shape=(tm, d), temp_bytes=tm * d * 4, deps=deps)[0]


def _proj_bwd_w(h, dproj):
    t, d = h.shape
    nb = dproj.shape[1] // N_DEV
    tm = min(d, 512)

    def body(h_ref, dp_ref, o_ref):
        hv = h_ref[...]
        o_ref[0] = lax.dot_general(hv, dp_ref[:, 0:nb], TN, preferred_element_type=F32).astype(BF16)
        o_ref[1] = lax.dot_general(hv, dp_ref[:, nb:2 * nb], TN, preferred_element_type=F32).astype(BF16)

    blocks = t * tm * 2 + t * 2 * nb * 2 + 2 * tm * nb * 2
    return pl.pallas_call(
        body, name="mix_bwd_dwin", grid=(4, d // tm),
        in_specs=[pl.BlockSpec((t, tm), lambda j, i: (0, i)),
                  pl.BlockSpec((t, 2 * nb), lambda j, i: (0, j))],
        out_specs=pl.BlockSpec((None, 2, tm, nb), lambda j, i: (j, 0, i, 0)),
        out_shape=_sds((4, 2, d, nb), BF16),
        compiler_params=_params(("parallel", "parallel"), blocks, 4 * tm * nb * 4),
    )(h, dproj)


def _adamw_math(w, g, m, v):
    m = ADAM_B1 * m + (1.0 - ADAM_B1) * g
    v = ADAM_B2 * v + (1.0 - ADAM_B2) * (g * g)
    m_hat = m / (1.0 - ADAM_B1 ** ADAM_STEP)
    v_hat = v / (1.0 - ADAM_B2 ** ADAM_STEP)
    delta = -ADAM_LR * (m_hat / (jnp.sqrt(v_hat) + ADAM_EPS) + ADAM_WD * w)
    return delta, m, v


def _adamw(name, parts, w, m, v, tr):
    r, c = w.shape

    def body(p_ref, w_ref, m_ref, v_ref, g_out, d_out, m_out, v_out):
        g = p_ref[0].astype(F32)
        for s in range(1, N_DEV):
            g = g + p_ref[s].astype(F32)
        delta, mn, vn = _adamw_math(w_ref[...], g, m_ref[...], v_ref[...])
        g_out[...] = g
        d_out[...] = delta
        m_out[...] = mn
        v_out[...] = vn

    blk = pl.BlockSpec((tr, c), lambda i: (i, 0))
    blocks = N_DEV * tr * c * parts.dtype.itemsize + 7 * tr * c * 4
    return pl.pallas_call(
        body, name=name, grid=(r // tr,),
        in_specs=[pl.BlockSpec((N_DEV, tr, c), lambda i: (0, i, 0)), blk, blk, blk],
        out_specs=[blk] * 4, out_shape=[_sds((r, c), F32)] * 4,
        compiler_params=_params(("parallel",), blocks, 6 * tr * c * 4),
    )(parts, w, m, v)


def _chip_sum(sums_ref):
    g = sums_ref[0].astype(F32)
    for k in range(1, 4):
        g = g + sums_ref[k].astype(F32)
    return g


def _adamw_chips(name, sums, w, m, v, tr, deps=(), row0=0, into=None):
    r, c = w.shape
    rs = sums.shape[1]
    i0 = row0 // tr
    n_pass = len(deps) + (4 if into is not None else 0)

    def body(sums_ref, w_ref, m_ref, v_ref, *rest):
        g_out, d_out, m_out, v_out = rest[n_pass:]
        g = _chip_sum(sums_ref)
        delta, mn, vn = _adamw_math(w_ref[...], g, m_ref[...], v_ref[...])
        g_out[...] = g
        d_out[...] = delta
        m_out[...] = mn
        v_out[...] = vn

    blk = pl.BlockSpec((tr, c), lambda i: (i0 + i, 0))
    blocks = 4 * tr * c * 2 + 7 * tr * c * 4
    passed = list(deps) + (list(into) if into is not None else [])
    aliases = {4 + len(deps) + q: q for q in range(4)} if into is not None else {}
    return pl.pallas_call(
        body, name=name, grid=(rs // tr,),
        in_specs=[pl.BlockSpec((4, tr, c), lambda i: (0, i, 0)), blk, blk, blk] + [_ANY] * n_pass,
        out_specs=[blk] * 4, out_shape=[_sds((r, c), F32)] * 4,
        input_output_aliases=aliases,
        compiler_params=_params(("parallel",), blocks, 6 * tr * c * 4),
    )(sums, w, m, v, *passed)


def _adamw_side(contrib, w, m, v, n_tiles, step_of):
    r, c = w.shape
    tr = r // n_tiles
    assert tr * n_tiles == r and tr % 16 == 0, (r, n_tiles)

    def tile(i, j, k):
        return jnp.minimum(step_of(i, j, k), n_tiles - 1)

    blk = pl.BlockSpec((tr, c), lambda i, j, k: (tile(i, j, k), 0))
    ins = [(contrib, pl.BlockSpec((4, tr, c), lambda i, j, k: (0, tile(i, j, k), 0))), (w, blk), (m, blk), (v, blk)]
    outs = [(_sds((r, c), F32), blk)] * 4

    def fn(in_refs, out_refs):
        @pl.when(step_of(pl.program_id(0), pl.program_id(1), pl.program_id(2)) < n_tiles)
        def _():
            g = _chip_sum(in_refs[0])
            delta, mn, vn = _adamw_math(in_refs[1][...], g, in_refs[2][...], in_refs[3][...])
            out_refs[0][...] = g
            out_refs[1][...] = delta
            out_refs[2][...] = mn
            out_refs[3][...] = vn

    return ins, outs, fn


def _rope_tables(t):
    half = ROT_DIM // 2
    inv_freq = 1.0 / (ROPE_THETA ** (jnp.arange(0, ROT_DIM, 2, dtype=F32) / ROT_DIM))
    ang = jnp.arange(t, dtype=F32)[:, None] * inv_freq[None, :]
    cos, sin = jnp.cos(ang), jnp.sin(ang)
    ones = jnp.ones((t, HEAD_DIM - ROT_DIM), F32)
    zeros = jnp.zeros((t, HEAD_DIM - half), F32)
    c = jnp.concatenate([cos, cos, ones], axis=1)
    sa = jnp.concatenate([-sin, zeros], axis=1)
    sb = jnp.concatenate([jnp.zeros((t, half), F32), sin, jnp.zeros((t, HEAD_DIM - ROT_DIM), F32)], axis=1)
    return tuple(jnp.tile(a, (1, LANES // HEAD_DIM)) for a in (c, sa, sb))


def _pad_rows(a, rows=8):
    return jnp.pad(a, ((0, rows - a.shape[0]), (0, 0)))


def kernel(x, g_ffn1, w_gu1, w_down1, g_mix, w_in, conv_w, q_norm_g, k_norm_g, sinks, w_out_conv, w_out_attn, w_o, g_ffn2, w_gu2, w_down2, loss_target, m_g_ffn1, m_w_gu1, m_w_down1, m_g_mix, m_w_in, m_conv_w, m_q_norm_g, m_k_norm_g, m_sinks, m_w_out_conv, m_w_out_attn, m_w_o, m_g_ffn2, m_w_gu2, m_w_down2, v_g_ffn1, v_w_gu1, v_w_down1, v_g_mix, v_w_in, v_conv_w, v_q_norm_g, v_k_norm_g, v_sinks, v_w_out_conv, v_w_out_attn, v_w_o, v_g_ffn2, v_w_gu2, v_w_down2):
    t, d = x.shape[1], x.shape[2]
    cw = d // 2
    kw = cw // GROUP
    nq = cw // HEAD_DIM
    xs, target = x.reshape(t, d), loss_target.reshape(t, d)
    me = 4 * lax.axis_index("x") + 2 * lax.axis_index("y") + lax.axis_index("c")

    big = {"w_gu1": w_gu1, "w_down1": w_down1, "w_in": w_in, "w_out_conv": w_out_conv,
           "w_out_attn": w_out_attn, "w_o": w_o, "w_gu2": w_gu2, "w_down2": w_down2}
    big_m = {"w_gu1": m_w_gu1, "w_down1": m_w_down1, "w_in": m_w_in, "w_out_conv": m_w_out_conv,
             "w_out_attn": m_w_out_attn, "w_o": m_w_o, "w_gu2": m_w_gu2, "w_down2": m_w_down2}
    big_v = {"w_gu1": v_w_gu1, "w_down1": v_w_down1, "w_in": v_w_in, "w_out_conv": v_w_out_conv,
             "w_out_attn": v_w_out_attn, "w_o": v_w_o, "w_gu2": v_w_gu2, "w_down2": v_w_down2}
    names = list(big)

    tiles = {"w_gu1": 256, "w_gu2": 256, "w_in": 256, "w_down1": 176, "w_down2": 176,
             "w_out_conv": 1024, "w_out_attn": 1024, "w_o": 128}

    def row_tile(n):
        r = big[n].shape[1]
        return tiles[n] if r % tiles[n] == 0 else r

    rs_shape = {n: big[n].shape[1:] for n in names}
    half = rs_shape["w_gu1"][0] // 2
    rs_shape["w_gu1_lo"] = rs_shape["w_gu1_hi"] = (half, rs_shape["w_gu1"][1])

    def add_tile(n):
        r, c = rs_shape[n]
        while r * c * 2 > (3 << 20) and r % 32 == 0:
            r //= 2
        return r

    me_arr = me.astype(jnp.int32).reshape(1)
    sources = [(n, big[n][0], BF16, row_tile(n)) for n in names] + [("conv_w", _pad_rows(conv_w[0]), F32, 8)]
    issue_order = [0, 1, 2, 8, 3, 4, 5, 6, 7]
    first = _place_shard("place_" + names[0], sources[0][1], BF16, me_arr, sources[0][3])
    started = [_gather_start("gather_start_first", [first])]
    early = {2: (big_m["w_in"][0], big_v["w_in"][0])}
    rest = [_place_shard("place_" + sources[i][0], sources[i][1], sources[i][2], me_arr, sources[i][3],
                         deps=(started[0][3],) + early.get(i, ())) for i in issue_order[1:]]
    started.append(_gather_start("gather_start_rest", rest))
    where = {0: (0, 0)}
    where.update({i: (1, p) for p, i in enumerate(issue_order[1:])})

    def fetch(tag, idxs, after, forward=True):
        call = where[idxs[0]][0]
        send, recv, stacks, _ = started[call]
        positions = [where[i][1] for i in idxs]
        got = _gather_wait("gather_wait_" + tag, positions, send, recv, [stacks[p] for p in positions], after)
        return _forward_to_sibling("gather_forward_" + tag, got) if forward else got

    rope_tabs = _rope_tables(t)
    gq = jnp.tile(q_norm_g, (1, nq))
    gk = jnp.tile(k_norm_g, (1, nq // GROUP))
    sink_rows = jnp.broadcast_to(sinks[0][:, None], (nq, LANES))

    wts = {}
    h1 = _rms_fwd("ffn1_norm", xs, g_ffn1)
    wts["w_gu1"], = fetch("gu1", [0], started[1][3])
    gu1, a1 = _ffn_up("ffn1_up", h1, wts["w_gu1"])
    wts["w_down1"], = fetch("down1", [1], a1)
    wd1 = wts["w_down1"].reshape(-1, d)
    x1 = _ffn_down("ffn1_down", a1, wd1, xs)
    h2 = _rms_fwd("mix_norm", x1, g_mix)
    wts["w_in"], conv_land = fetch("in", [2, 8], h2)
    w_in_full = wts["w_in"].reshape(4, 2, d, -1)
    conv_full = jnp.transpose(conv_land, (1, 0, 2)).reshape(8, cw)
    proj = _proj(h2, w_in_full)
    ca = _conv_fwd(proj, conv_full)
    qn, kn, vb = _qk_prep(proj, gq, gk, rope_tabs, cw, kw)
    o = _attn_fwd(qn, kn, vb, sink_rows)
    wts["w_out_conv"], wts["w_out_attn"] = fetch("out", [3, 4], o)
    merged, ya, yb = _mix_out(ca, o, wts["w_out_conv"], wts["w_out_attn"], proj)
    wts["w_o"], = fetch("o", [5], merged)
    wo = wts["w_o"].reshape(d, d)
    x2 = _mix_residual(merged, wo, x1)
    h3 = _rms_fwd("ffn2_norm", x2, g_ffn2)
    mine = lax.axis_index("c").astype(jnp.int32).reshape(1)
    got = fetch("gu2", [6], h3, forward=False)
    fsend, frecv, got = _forward_start("gather_forward_start_gu2", got)
    part = _ffn_up("ffn2_up_mine", h3, got[0], parity=mine)
    wts["w_gu2"], = _forward_wait("gather_forward_wait_gu2", fsend, frecv, got, part[1])
    gu2, a2 = _ffn_up("ffn2_up_sibling", h3, wts["w_gu2"], parity=1 - mine, into=part)
    wts["w_down2"], = fetch("down2", [7], a2)
    wd2 = wts["w_down2"].reshape(-1, d)
    dy, sq, dy_bf = _ffn_down("ffn2_down", a2, wd2, x2, target=target)
    loss = lax.psum(sq[0, 0] * (0.5 / d), ("x", "y", "c"))

    place = jnp.stack([lax.axis_index("c"), 2 * lax.axis_index("x") + lax.axis_index("y")]).astype(jnp.int32)
    def pair_start(tag, group, grads, deps=()):
        stacks = [grads[n].reshape((4, 2) + rs_shape[n]) for n in group]
        lands = [lax.empty((4,) + rs_shape[n], BF16) for n in group]
        return _pair_start("rs_pair_start_" + tag, stacks, lands, deps)

    def chip_start(tag, group, pending, after):
        send, recv, stacks, lands, _ = pending
        stacks, lands = _pair_wait("rs_pair_wait_" + tag, send, recv, stacks, lands, after)
        added = [_pair_add("rs_pair_add_" + n, st, ld, place, add_tile(n)) for n, st, ld in zip(group, stacks, lands)]
        return _chip_start("rs_chip_start_" + tag, [a[0] for a in added], [a[1] for a in added])

    group_a, group_b, group_c = ["w_down2", "w_gu2"], ["w_o", "w_out_conv", "w_out_attn"], ["w_in"]
    group_d, group_e, group_f = ["w_down1"], ["w_gu1_lo"], ["w_gu1_hi"]
    g = {}
    dgu2, a2 = _ffn_bwd_act("ffn2_bwd_act", dy_bf, wd2, gu2)
    g["w_down2"], = _ffn_bwd_dwd("ffn2_bwd_dwd", a2, dy_bf)
    g["w_gu2"], = _ffn_bwd_dwgu("ffn2_bwd_dwgu", h3, dgu2)
    pend_a = pair_start("a", group_a, g)
    dh3, = _ffn_bwd_dh("ffn2_bwd_dh", dgu2, wts["w_gu2"], deps=(pend_a[4],))
    ring_a = chip_start("a", group_a, pend_a, dh3)
    dx2, dg_ffn2, dx2_bf = _rms_bwd("ffn2_bwd_rms", x2, g_ffn2, dh3, dy, deps=(ring_a[4],), with_bf16=True)

    dya, dyb, dproj = _mix_bwd_gates(dx2_bf, wo, ya, yb, proj, cw)
    g["w_o"] = _tn_matmul("mix_bwd_dwo", merged, dx2_bf, min(d, 512))
    g["w_out_conv"], g["w_out_attn"] = _out_proj_bwd_w(ca, o, dya, dyb, d // N_DEV)
    pend_b = pair_start("b", group_b, g)
    dca, do = _out_proj_bwd_act(dya, dyb, wts["w_out_conv"], wts["w_out_attn"], deps=(pend_b[4],))
    ring_b = chip_start("b", group_b, pend_b, do)
    dproj, dconv_w = _conv_bwd(proj, conv_full, dca, dproj, deps=(ring_b[4],))
    dq, dkc, dkp, dvc, dvp, dsink = _attn_bwd(qn, kn, vb, sink_rows, do)
    dproj, dgq, dgk = _qk_prep_bwd(proj, gq, gk, rope_tabs, dq, dkc, dkp, dvc, dvp, dproj, cw, kw)
    g["w_in"] = _proj_bwd_w(h2, dproj)
    pend_c = pair_start("c", group_c, g)
    dh2 = _proj_bwd_act(dproj, w_in_full, deps=(pend_c[4],))
    ring_c = chip_start("c", group_c, pend_c, dh2)
    dx1, dg_mix, dx1_bf = _rms_bwd("mix_bwd_rms", x1, g_mix, dh2, dx2, deps=(ring_c[4],), with_bf16=True)

    big_out = {}
    arrived = {}

    def wait_group(tag, group, ring, after):
        send, recv, parts, lands2, _ = ring
        parts, lands2 = _chip_wait("rs_chip_wait_" + tag, send, recv, parts, lands2, after)
        arrived.update(dict(zip(group, lands2)))

    def update(n, after):
        res = _adamw_chips("adamw_" + n, arrived[n], big[n][0], big_m[n][0], big_v[n][0], row_tile(n), deps=(after,))
        big_out[n] = [a[None] for a in res]
        return res[0]

    def update_beside(n, n_tiles, step_of):
        return _adamw_side(arrived[n], big[n][0], big_m[n][0], big_v[n][0], n_tiles, step_of)

    def keep(n, res):
        big_out[n] = [a[None] for a in res]

    dgu1, a1 = _ffn_bwd_act("ffn1_bwd_act", dx1_bf, wd1, gu1)
    wait_group("a", group_a, ring_a, a1)
    g["w_down1"], *res = _ffn_bwd_dwd("ffn1_bwd_dwd", a1, dx1_bf,
                                       side=update_beside("w_down2", 11, lambda i, j, k: i * 4 + j))
    keep("w_down2", res)
    pend_d = pair_start("d", group_d, g)
    g["w_gu1_lo"], *res = _ffn_bwd_dwgu("ffn1_bwd_dwgu_lo", h1, dgu1, deps=(pend_d[4],), rows=(0, half),
                                         side=update_beside("w_gu2", 16, lambda i, j, k: i * 2 + j))
    keep("w_gu2", res)
    ring_d = chip_start("d", group_d, pend_d, g["w_gu1_lo"])
    pend_e = pair_start("e", group_e, g, deps=(ring_d[4],))
    wait_group("c", group_c, ring_c, pend_e[4])
    g["w_gu1_hi"], *res = _ffn_bwd_dwgu("ffn1_bwd_dwgu_hi", h1, dgu1, rows=(half, half),
                                         side=update_beside("w_in", 16, lambda i, j, k: i * 2 + j))
    keep("w_in", res)
    ring_e = chip_start("e", group_e, pend_e, g["w_gu1_hi"])
    pend_f = pair_start("f", group_f, g, deps=(ring_e[4],))
    wait_group("b", group_b, ring_b, pend_f[4])
    after = pend_f[4]
    for n in group_b:
        after = update(n, after)
    ring_f = chip_start("f", group_f, pend_f, after)
    wait_group("d", group_d, ring_d, ring_f[4])
    dh1, *res = _ffn_bwd_dh("ffn1_bwd_dh", dgu1, wts["w_gu1"],
                             side=update_beside("w_down1", 11, lambda i, j, k: i * 4 + k))
    keep("w_down1", res)
    grad_x, dg_ffn1 = _rms_bwd("ffn1_bwd_rms", xs, g_ffn1, dh1, dx1)
    after = grad_x
    n = "w_gu1"
    wait_group("e", group_e, ring_e, after)
    res = _adamw_chips("adamw_w_gu1_lo", arrived["w_gu1_lo"], big[n][0], big_m[n][0], big_v[n][0], row_tile(n), deps=(after,))
    wait_group("f", group_f, ring_f, res[0])
    res = _adamw_chips("adamw_w_gu1_hi", arrived["w_gu1_hi"], big[n][0], big_m[n][0], big_v[n][0], row_tile(n),
                       row0=half, into=res)
    keep(n, res)
    after = res[0]

    small = {"g_ffn1": dg_ffn1[0:1], "g_mix": dg_mix[0:1], "g_ffn2": dg_ffn2[0:1],
             "q_norm_g": dgq[0:1, :HEAD_DIM], "k_norm_g": dgk[0:1, :HEAD_DIM], "sinks": dsink[:, 0][None],
             "conv_w": dconv_w[0:CONV_K].reshape(1, -1)}
    small_w = {"g_ffn1": g_ffn1, "g_mix": g_mix, "g_ffn2": g_ffn2, "q_norm_g": q_norm_g, "k_norm_g": k_norm_g,
               "sinks": sinks, "conv_w": None}
    small_m = {"g_ffn1": m_g_ffn1, "g_mix": m_g_mix, "g_ffn2": m_g_ffn2, "q_norm_g": m_q_norm_g,
               "k_norm_g": m_k_norm_g, "sinks": m_sinks, "conv_w": m_conv_w}
    small_v = {"g_ffn1": v_g_ffn1, "g_mix": v_g_mix, "g_ffn2": v_g_ffn2, "q_norm_g": v_q_norm_g,
               "k_norm_g": v_k_norm_g, "sinks": v_sinks, "conv_w": v_conv_w}
    snames = list(small)
    widths = [small[n].shape[1] for n in snames]
    total = sum(widths)
    rows = -(-total // LANES)
    rows = -(-rows // 8) * 8

    def pack(vals):
        flat = jnp.concatenate([v.reshape(1, -1) for v in vals], axis=1)
        return jnp.pad(flat, ((0, 0), (0, rows * LANES - total))).reshape(rows, LANES)

    csh = cw // N_DEV

    def place_conv(local, fill):
        full = jnp.full((CONV_K, cw), fill, F32)
        return lax.dynamic_update_slice(full, local, (0, me * csh)).reshape(1, -1)

    pw = pack([small_w[n] if n != "conv_w" else place_conv(conv_w[0], 0.0) for n in snames])
    pm = pack([small_m[n] if n != "conv_w" else place_conv(m_conv_w[0], 0.0) for n in snames])
    pv = pack([small_v[n] if n != "conv_w" else place_conv(v_conv_w[0], 1.0) for n in snames])
    parts = _all_gather_small("gather_small_grads", pack([small[n] for n in snames]), deps=(after,))
    sg, sd, sm, sv = [a.reshape(1, -1) for a in _adamw("adamw_small", parts, pw, pm, pv, rows)]

    def unpack(flat, n):
        off = sum(widths[:snames.index(n)])
        piece = flat[:, off:off + widths[snames.index(n)]]
        if n == "conv_w":
            piece = lax.dynamic_slice(piece.reshape(CONV_K, cw), (0, me * csh), (CONV_K, csh))[None]
        return piece

    order = ["g_ffn1", "w_gu1", "w_down1", "g_mix", "w_in", "conv_w", "q_norm_g", "k_norm_g", "sinks",
             "w_out_conv", "w_out_attn", "w_o", "g_ffn2", "w_gu2", "w_down2"]
    outs = [loss, grad_x[None]]
    for idx, flat in enumerate((sg, sd, sm, sv)):
        for n in order:
            outs.append(big_out[n][idx] if n in big_out else unpack(flat, n))
    return tuple(outs)
```

```python
import jax
import jax.numpy as jnp
from jax import lax
from jax.experimental import pallas as pl
from jax.experimental.pallas import tpu as pltpu

F32 = jnp.float32
BF16 = jnp.bfloat16

N_DEV = 8
HEAD_DIM = 64
GROUP = 4
BLOCK = 128
ROT_DIM = 16
ROPE_THETA = 500000.0
RMS_EPS = 1e-6
NEG_INF = -1e30
ATTN_SCALE = HEAD_DIM ** -0.5
CONV_K = 3
LANES = 128
MXU_COLS = 256
VMEM_BYTES_V7X = 64 * 1024 * 1024
VMEM_CAP = VMEM_BYTES_V7X - 6 * 1024 * 1024

ADAM_LR = 0.001
ADAM_B1 = 0.9
ADAM_B2 = 0.999
ADAM_EPS = 1e-08
ADAM_WD = 0.01
ADAM_STEP = 10

NN = (((1,), (0,)), ((), ()))
NT = (((1,), (1,)), ((), ()))
TN = (((0,), (0,)), ((), ()))

MESH = pl.DeviceIdType.MESH


def _nbytes(shape, dtype):
    n = 1
    for s in shape:
        if s is not None:
            n *= s
    return n * jnp.dtype(dtype).itemsize


def _params(semantics, block_bytes, temp_bytes):
    assert 2 * block_bytes + temp_bytes <= VMEM_CAP, (block_bytes, temp_bytes)
    return pltpu.CompilerParams(dimension_semantics=semantics, vmem_limit_bytes=VMEM_CAP)


def _fused(name, grid, ins, outs, dots, epilogue, *, nk=1, acc_shape=None, temp_bytes=0,
           semantics=("parallel", "parallel", "arbitrary"), deps=(), side=None):
    n_main_in, n_main_out = len(ins), len(outs)
    if side is not None:
        ins, outs = list(ins) + list(side[0]), list(outs) + list(side[1])
    n_in, n_out = len(ins), len(outs)
    n_dep = len(deps)

    def body(*refs):
        in_refs, out_refs = refs[:n_in], refs[n_in + n_dep:n_in + n_dep + n_out]
        scratch = refs[n_in + n_dep + n_out:]
        if side is not None:
            side[2](in_refs[n_main_in:], out_refs[n_main_out:])

        def products():
            if callable(dots):
                return dots(in_refs)
            total = None
            for ai, bi, contract in dots:
                a, b = in_refs[ai][...], in_refs[bi][...]
                a = a if a.dtype == BF16 else a.astype(BF16)
                b = b if b.dtype == BF16 else b.astype(BF16)
                p = lax.dot_general(a, b, contract, preferred_element_type=F32)
                total = p if total is None else total + p
            return total

        if nk == 1:
            epilogue(products() if dots else None, in_refs, out_refs)
        else:
            acc = scratch[0]
            k = pl.program_id(2)

            @pl.when(k == 0)
            def _():
                acc[...] = jnp.zeros_like(acc)

            acc[...] += products()

            @pl.when(k == nk - 1)
            def _():
                epilogue(acc[...], in_refs, out_refs)

    block_bytes = sum(_nbytes(spec.block_shape, a.dtype) for a, spec in ins)
    block_bytes += sum(_nbytes(spec.block_shape, s.dtype) for s, spec in outs)
    scratch_shapes = []
    if nk > 1:
        scratch_shapes.append(pltpu.VMEM(acc_shape, F32))
        temp_bytes += _nbytes(acc_shape, F32)
    res = pl.pallas_call(
        body, name=name, grid=grid,
        in_specs=[spec for _, spec in ins] + [pl.BlockSpec(memory_space=pl.ANY)] * n_dep,
        out_specs=[spec for _, spec in outs],
        out_shape=[s for s, _ in outs],
        scratch_shapes=scratch_shapes,
        compiler_params=_params(semantics, block_bytes, temp_bytes),
    )(*[a for a, _ in ins], *deps)
    return res


def _sds(shape, dtype):
    return jax.ShapeDtypeStruct(shape, dtype)


def _sigmoid(x):
    return jax.nn.sigmoid(x)


def _all_gather_small(name, shard, deps=()):
    n_dep = len(deps)

    def body(src, *rest):
        dst, send_sems, recv_sems, local_sem = rest[n_dep:]
        x, y, c = lax.axis_index("x"), lax.axis_index("y"), lax.axis_index("c")
        me = 4 * x + 2 * y + c
        copies = [pltpu.make_async_copy(src, dst.at[me], local_sem)]
        for k in range(1, N_DEV):
            peer = ((1 - x) if (k & 4) else x, (1 - y) if (k & 2) else y, (1 - c) if (k & 1) else c)
            copies.append(pltpu.make_async_remote_copy(
                src_ref=src, dst_ref=dst.at[me], send_sem=send_sems.at[k - 1], recv_sem=recv_sems.at[k - 1],
                device_id=peer, device_id_type=MESH))
        for cp in copies:
            cp.start()
        for cp in copies:
            cp.wait()

    hbm = pl.BlockSpec(memory_space=pltpu.HBM)
    return pl.pallas_call(
        body, name=name,
        in_specs=[hbm] + [pl.BlockSpec(memory_space=pl.ANY)] * n_dep, out_specs=hbm,
        out_shape=_sds((N_DEV,) + shard.shape, shard.dtype),
        scratch_shapes=[pltpu.SemaphoreType.DMA((N_DEV - 1,)), pltpu.SemaphoreType.DMA((N_DEV - 1,)),
                        pltpu.SemaphoreType.DMA],
    )(shard, *deps)


_HBM = pl.BlockSpec(memory_space=pltpu.HBM)
_SEM = pl.BlockSpec(memory_space=pltpu.SEMAPHORE)
_ANY = pl.BlockSpec(memory_space=pl.ANY)
_EFFECT = pltpu.SideEffectType.DATAFLOW_SIDE_EFFECTING
N_TARGETS = 4


def _mesh_pos():
    return lax.axis_index("x"), lax.axis_index("y"), lax.axis_index("c")


def _chip_peers(x, y, c):
    return [(1 - x, y, c), (x, 1 - y, c), (1 - x, 1 - y, c)]


def _dev_index(pos):
    return 4 * pos[0] + 2 * pos[1] + pos[2]


def _hbm_like(a):
    return pltpu.HBM(a.shape, a.dtype)


def _place_shard(name, w, out_dtype, me, tr, deps=()):
    r, c = w.shape
    n_dep = len(deps)

    def body(me_ref, w_ref, *rest):
        rest[n_dep][...] = w_ref[...].astype(out_dtype)

    grid_spec = pltpu.PrefetchScalarGridSpec(
        num_scalar_prefetch=1, grid=(r // tr,),
        in_specs=[pl.BlockSpec((tr, c), lambda i, me_ref: (i, 0))] + [_ANY] * n_dep,
        out_specs=pl.BlockSpec((None, tr, c), lambda i, me_ref: (me_ref[0], i, 0)))
    return pl.pallas_call(
        body, name=name, grid_spec=grid_spec, out_shape=_sds((N_DEV, r, c), out_dtype),
        compiler_params=_params(("parallel",), tr * c * 6, tr * c * 4),
    )(me, w, *deps)


def _gather_start(name, lands):
    n = len(lands)

    def body(*refs):
        bufs = refs[:n]
        send, recv = refs[n], refs[n + 1]
        token = refs[-1]
        x, y, c = _mesh_pos()
        me = _dev_index((x, y, c))
        targets = [(x, y, 1 - c)] + _chip_peers(x, y, c)
        for w in range(n):
            for k, to in enumerate(targets):
                pltpu.make_async_remote_copy(
                    src_ref=bufs[w].at[me], dst_ref=bufs[w].at[me],
                    send_sem=send.at[N_TARGETS * w + k], recv_sem=recv.at[N_TARGETS * w + k],
                    device_id=to, device_id_type=MESH).start()
        token[...] = jnp.zeros_like(token)

    sems = pltpu.SemaphoreType.DMA((N_TARGETS * n,))
    outs = pl.pallas_call(
        body, name=name,
        in_specs=[_HBM] * n, out_specs=[_SEM, _SEM] + [_HBM] * n + [_token_spec()],
        out_shape=[sems, sems] + [_hbm_like(a) for a in lands] + [_sds((8, LANES), F32)],
        input_output_aliases={i: 2 + i for i in range(n)},
        compiler_params=pltpu.CompilerParams(has_side_effects=_EFFECT),
    )(*lands)
    return outs[0], outs[1], list(outs[2:2 + n]), outs[-1]


def _gather_wait(name, positions, send, recv, lands, after):
    m = len(positions)

    def body(*refs):
        bufs = refs[:m]
        send_sems, recv_sems = refs[m], refs[m + 1]
        x, y, c = _mesh_pos()
        me = _dev_index((x, y, c))
        sources = [(x, y, 1 - c)] + _chip_peers(x, y, c)
        for j, w in enumerate(positions):
            for k, frm in enumerate(sources):
                cp = pltpu.make_async_remote_copy(
                    src_ref=bufs[j].at[me], dst_ref=bufs[j].at[_dev_index(frm)],
                    send_sem=send_sems.at[N_TARGETS * w + k], recv_sem=recv_sems.at[N_TARGETS * w + k],
                    device_id=frm, device_id_type=MESH)
                cp.wait_send()
                cp.wait_recv()

    outs = pl.pallas_call(
        body, name=name,
        in_specs=[_HBM] * m + [_SEM, _SEM, _ANY], out_specs=[_HBM] * m,
        out_shape=[_hbm_like(a) for a in lands],
        input_output_aliases={i: i for i in range(m)},
        compiler_params=pltpu.CompilerParams(has_side_effects=_EFFECT),
    )(*lands, send, recv, after)
    return list(outs)


def _forward_to_sibling(name, lands):
    m = len(lands)

    def body(*refs):
        copies = _forward_copies(refs[m:2 * m], refs[2 * m], refs[2 * m + 1])
        for cp in copies:
            cp.start()
        for cp in copies:
            cp.wait()

    outs = pl.pallas_call(
        body, name=name,
        in_specs=[_HBM] * m, out_specs=[_HBM] * m,
        out_shape=[_sds(a.shape, a.dtype) for a in lands],
        input_output_aliases={i: i for i in range(m)},
        scratch_shapes=[pltpu.SemaphoreType.DMA((3 * m,)), pltpu.SemaphoreType.DMA((3 * m,))],
    )(*lands)
    return list(outs)


def _forward_copies(bufs, send, recv):
    x, y, c = _mesh_pos()
    copies = []
    for j, buf in enumerate(bufs):
        for k, chip in enumerate(_chip_peers(x, y, c)):
            block = buf.at[_dev_index(chip)]
            copies.append(pltpu.make_async_remote_copy(
                src_ref=block, dst_ref=block, send_sem=send.at[3 * j + k], recv_sem=recv.at[3 * j + k],
                device_id=(x, y, 1 - c), device_id_type=MESH))
    return copies


def _forward_start(name, lands):
    m = len(lands)

    def body(*refs):
        for cp in _forward_copies(refs[:m], refs[m], refs[m + 1]):
            cp.start()

    sems = pltpu.SemaphoreType.DMA((3 * m,))
    outs = pl.pallas_call(
        body, name=name,
        in_specs=[_HBM] * m, out_specs=[_SEM, _SEM] + [_HBM] * m,
        out_shape=[sems, sems] + [_hbm_like(a) for a in lands],
        input_output_aliases={i: 2 + i for i in range(m)},
        compiler_params=pltpu.CompilerParams(has_side_effects=_EFFECT),
    )(*lands)
    return outs[0], outs[1], list(outs[2:])


def _forward_wait(name, send, recv, lands, after):
    m = len(lands)

    def body(*refs):
        for cp in _forward_copies(refs[:m], refs[m], refs[m + 1]):
            cp.wait_send()
            cp.wait_recv()

    outs = pl.pallas_call(
        body, name=name,
        in_specs=[_HBM] * m + [_SEM, _SEM, _ANY], out_specs=[_HBM] * m,
        out_shape=[_hbm_like(a) for a in lands],
        input_output_aliases={i: i for i in range(m)},
        compiler_params=pltpu.CompilerParams(has_side_effects=_EFFECT),
    )(*lands, send, recv, after)
    return list(outs)


def _token_spec():
    return pl.BlockSpec(memory_space=pltpu.VMEM)


def _pair_start(name, stacks, lands, deps=()):
    n = len(stacks)
    n_dep = len(deps)

    def body(*refs):
        srcs, dsts = refs[:n], refs[n:2 * n]
        send, recv = refs[2 * n + n_dep], refs[2 * n + n_dep + 1]
        token = refs[-1]
        x, y, c = _mesh_pos()
        for w in range(n):
            for chip in range(4):
                pltpu.make_async_remote_copy(
                    src_ref=srcs[w].at[chip, 1 - c], dst_ref=dsts[w].at[chip],
                    send_sem=send.at[4 * w + chip], recv_sem=recv.at[4 * w + chip],
                    device_id=(x, y, 1 - c), device_id_type=MESH).start()
        token[...] = jnp.zeros_like(token)

    sems = pltpu.SemaphoreType.DMA((4 * n,))
    outs = pl.pallas_call(
        body, name=name,
        in_specs=[_HBM] * (2 * n) + [_ANY] * n_dep, out_specs=[_SEM, _SEM] + [_HBM] * (2 * n) + [_token_spec()],
        out_shape=[sems, sems] + [_hbm_like(a) for a in stacks] + [_hbm_like(a) for a in lands] + [_sds((8, LANES), F32)],
        input_output_aliases={i: 2 + i for i in range(2 * n)},
        compiler_params=pltpu.CompilerParams(has_side_effects=_EFFECT),
    )(*stacks, *lands, *deps)
    return outs[0], outs[1], list(outs[2:2 + n]), list(outs[2 + n:2 + 2 * n]), outs[-1]


def _pair_wait(name, send, recv, stacks, lands, after):
    n = len(stacks)

    def body(*refs):
        srcs, dsts = refs[:n], refs[n:2 * n]
        send_sems, recv_sems = refs[2 * n], refs[2 * n + 1]
        x, y, c = _mesh_pos()
        for w in range(n):
            for chip in range(4):
                cp = pltpu.make_async_remote_copy(
                    src_ref=srcs[w].at[chip, 1 - c], dst_ref=dsts[w].at[chip],
                    send_sem=send_sems.at[4 * w + chip], recv_sem=recv_sems.at[4 * w + chip],
                    device_id=(x, y, 1 - c), device_id_type=MESH)
                cp.wait_send()
                cp.wait_recv()

    outs = pl.pallas_call(
        body, name=name,
        in_specs=[_HBM] * (2 * n) + [_SEM, _SEM, _ANY], out_specs=[_HBM] * (2 * n),
        out_shape=[_hbm_like(a) for a in stacks] + [_hbm_like(a) for a in lands],
        input_output_aliases={i: i for i in range(2 * n)},
        compiler_params=pltpu.CompilerParams(has_side_effects=_EFFECT),
    )(*stacks, *lands, send, recv, after)
    return list(outs[:n]), list(outs[n:])


def _pair_add(name, stack, land, place, tr):
    _, _, r, c = stack.shape

    def body(place_ref, a_ref, b_ref, sums_ref, slots_ref):
        total = (a_ref[...].astype(F32) + b_ref[...].astype(F32)).astype(BF16)
        sums_ref[...] = total

        @pl.when(pl.program_id(1) == place_ref[1])
        def _():
            slots_ref[...] = total

    grid_spec = pltpu.PrefetchScalarGridSpec(
        num_scalar_prefetch=1, grid=(r // tr, 4),
        in_specs=[pl.BlockSpec((None, None, tr, c), lambda i, k, place_ref: (k, place_ref[0], i, 0)),
                  pl.BlockSpec((None, tr, c), lambda i, k, place_ref: (k, i, 0))],
        out_specs=[pl.BlockSpec((None, tr, c), lambda i, k, place_ref: (k, i, 0)),
                   pl.BlockSpec((None, tr, c), lambda i, k, place_ref: (place_ref[1], i, 0))])
    return pl.pallas_call(
        body, name=name, grid_spec=grid_spec, out_shape=[_sds((4, r, c), BF16)] * 2,
        compiler_params=_params(("parallel", "arbitrary"), 4 * tr * c * 2, 3 * tr * c * 4),
    )(place, stack, land)


def _chip_start(name, parts, lands):
    n = len(parts)

    def body(*refs):
        srcs, dsts = refs[:n], refs[n:2 * n]
        send, recv = refs[2 * n], refs[2 * n + 1]
        token = refs[-1]
        x, y, c = _mesh_pos()
        for w in range(n):
            for k, to in enumerate(_chip_peers(x, y, c)):
                pltpu.make_async_remote_copy(
                    src_ref=srcs[w].at[2 * to[0] + to[1]], dst_ref=dsts[w].at[2 * x + y],
                    send_sem=send.at[3 * w + k], recv_sem=recv.at[3 * w + k],
                    device_id=to, device_id_type=MESH).start()
        token[...] = jnp.zeros_like(token)

    sems = pltpu.SemaphoreType.DMA((3 * n,))
    outs = pl.pallas_call(
        body, name=name,
        in_specs=[_HBM] * (2 * n), out_specs=[_SEM, _SEM] + [_HBM] * (2 * n) + [_token_spec()],
        out_shape=[sems, sems] + [_hbm_like(a) for a in parts] + [_hbm_like(a) for a in lands] + [_sds((8, LANES), F32)],
        input_output_aliases={i: 2 + i for i in range(2 * n)},
        compiler_params=pltpu.CompilerParams(has_side_effects=_EFFECT),
    )(*parts, *lands)
    return outs[0], outs[1], list(outs[2:2 + n]), list(outs[2 + n:2 + 2 * n]), outs[-1]


def _chip_wait(name, send, recv, parts, lands, after):
    n = len(parts)

    def body(*refs):
        srcs, dsts = refs[:n], refs[n:2 * n]
        send_sems, recv_sems = refs[2 * n], refs[2 * n + 1]
        x, y, c = _mesh_pos()
        for w in range(n):
            for k, frm in enumerate(_chip_peers(x, y, c)):
                chip = 2 * frm[0] + frm[1]
                cp = pltpu.make_async_remote_copy(
                    src_ref=srcs[w].at[chip], dst_ref=dsts[w].at[chip],
                    send_sem=send_sems.at[3 * w + k], recv_sem=recv_sems.at[3 * w + k],
                    device_id=frm, device_id_type=MESH)
                cp.wait_send()
                cp.wait_recv()

    outs = pl.pallas_call(
        body, name=name,
        in_specs=[_HBM] * (2 * n) + [_SEM, _SEM, _ANY], out_specs=[_HBM] * (2 * n),
        out_shape=[_hbm_like(a) for a in parts] + [_hbm_like(a) for a in lands],
        input_output_aliases={i: i for i in range(2 * n)},
        compiler_params=pltpu.CompilerParams(has_side_effects=_EFFECT),
    )(*parts, *lands, send, recv, after)
    return list(outs[:n]), list(outs[n:])


def _row_tile(t):
    return min(t, 256)


def _rms_fwd(name, x, g):
    t, d = x.shape
    tm = _row_tile(t)

    def epilogue(_, ins, outs):
        xv = ins[0][...]
        r = lax.rsqrt(jnp.mean(xv * xv, axis=-1, keepdims=True) + RMS_EPS)
        outs[0][...] = (xv * r * ins[1][...]).astype(BF16)

    row = pl.BlockSpec((tm, d), lambda i, j, k: (i, 0))
    vec = pl.BlockSpec((1, d), lambda i, j, k: (0, 0))
    return _fused(name, (t // tm, 1, 1), [(x, row), (g, vec)], [(_sds((t, d), BF16), row)], [], epilogue,
                  temp_bytes=4 * tm * d * 4)[0]


def _rms_bwd(name, x, g, dh, resid, deps=(), with_bf16=False):
    t, d = x.shape
    tm = _row_tile(t)

    def epilogue(_, ins, outs):
        xv, gv, dhv = ins[0][...], ins[1][...], ins[2][...]
        r = lax.rsqrt(jnp.mean(xv * xv, axis=-1, keepdims=True) + RMS_EPS)
        xh = xv * r
        u = dhv * gv
        dot = jnp.mean(u * xh, axis=-1, keepdims=True)
        dx = ins[3][...] + r * (u - xh * dot)
        outs[0][...] = dx
        if with_bf16:
            outs[2][...] = dx.astype(BF16)

        @pl.when(pl.program_id(0) == 0)
        def _():
            outs[1][...] = jnp.zeros_like(outs[1])

        outs[1][0:1, :] += jnp.sum(dhv * xh, axis=0, keepdims=True)

    row = pl.BlockSpec((tm, d), lambda i, j, k: (i, 0))
    vec = pl.BlockSpec((1, d), lambda i, j, k: (0, 0))
    acc = pl.BlockSpec((8, d), lambda i, j, k: (0, 0))
    outs = [(_sds((t, d), F32), row), (_sds((8, d), F32), acc)] + ([(_sds((t, d), BF16), row)] if with_bf16 else [])
    return _fused(name, (t // tm, 1, 1), [(x, row), (g, vec), (dh, row), (resid, row)], outs, [], epilogue,
                  temp_bytes=6 * tm * d * 4, semantics=("arbitrary", "arbitrary", "arbitrary"), deps=deps)


def _ffn_up(name, h, wgu, parity=None, into=None):
    t, d = h.shape
    nb = wgu.shape[2]
    f = 4 * nb
    tm = min(t, 512)

    def body(h_ref, wg_ref, wu_ref, gu_ref, a_ref):
        hv = h_ref[...]
        for c0 in range(0, nb, MXU_COLS):
            cs = slice(c0, min(c0 + MXU_COLS, nb))
            g = jnp.dot(hv, wg_ref[:, cs], preferred_element_type=F32)
            u = jnp.dot(hv, wu_ref[:, cs], preferred_element_type=F32)
            gu_ref[0, :, cs] = g.astype(BF16)
            gu_ref[1, :, cs] = u.astype(BF16)
            a_ref[:, cs] = (g * _sigmoid(g) * u).astype(BF16)

    blocks = tm * d * 2 + 2 * d * nb * 2 + 3 * tm * nb * 2
    params = _params(("parallel", "parallel"), blocks, 8 * tm * MXU_COLS * 4)
    out_shape = [_sds((2, t, f), BF16), _sds((t, f), BF16)]
    if parity is None:
        return pl.pallas_call(
            body, name=name, grid=(4, t // tm),
            in_specs=[pl.BlockSpec((tm, d), lambda j, i: (i, 0)),
                      pl.BlockSpec((None, d, nb), lambda j, i: (j, 0, 0)),
                      pl.BlockSpec((None, d, nb), lambda j, i: (j + 4, 0, 0))],
            out_specs=[pl.BlockSpec((2, tm, nb), lambda j, i: (0, i, j)),
                       pl.BlockSpec((tm, nb), lambda j, i: (i, j))],
            out_shape=out_shape, compiler_params=params,
        )(h, wgu, wgu)

    def half_body(parity_ref, h_ref, wg_ref, wu_ref, *rest):
        body(h_ref, wg_ref, wu_ref, rest[-2], rest[-1])

    n_pass = 0 if into is None else 2
    grid_spec = pltpu.PrefetchScalarGridSpec(
        num_scalar_prefetch=1, grid=(2, t // tm),
        in_specs=[pl.BlockSpec((tm, d), lambda jj, i, p: (i, 0)),
                  pl.BlockSpec((None, d, nb), lambda jj, i, p: (2 * jj + p[0], 0, 0)),
                  pl.BlockSpec((None, d, nb), lambda jj, i, p: (2 * jj + p[0] + 4, 0, 0))] + [_ANY] * n_pass,
        out_specs=[pl.BlockSpec((2, tm, nb), lambda jj, i, p: (0, i, 2 * jj + p[0])),
                   pl.BlockSpec((tm, nb), lambda jj, i, p: (i, 2 * jj + p[0]))])
    return pl.pallas_call(
        half_body, name=name, grid_spec=grid_spec, out_shape=out_shape,
        input_output_aliases={} if into is None else {4: 0, 5: 1}, compiler_params=params,
    )(parity, h, wgu, wgu, *(into or ()))


def _ffn_down(name, a, wd, x, target=None):
    t, f = a.shape
    d = wd.shape[1]
    tm = min(t, 512)
    tn = min(d, 1024)
    blk = pl.BlockSpec((tm, tn), lambda j, i, k: (i, j))
    ins = [(a, pl.BlockSpec((tm, f), lambda j, i, k: (i, 0))), (wd, pl.BlockSpec((f, tn), lambda j, i, k: (0, j))), (x, blk)]

    if target is None:
        def epilogue(acc, ins, outs):
            outs[0][...] = ins[2][...] + 0.5 * acc

        return _fused(name, (d // tn, t // tm, 1), ins, [(_sds((t, d), F32), blk)],
                      [(0, 1, NN)], epilogue, temp_bytes=2 * tm * tn * 4)[0]

    def epilogue(acc, ins, outs):
        e = ins[2][...] + 0.5 * acc - ins[3][...]
        outs[0][...] = e * (1.0 / d)
        outs[2][...] = (e * (1.0 / d)).astype(BF16)

        @pl.when((pl.program_id(0) == 0) & (pl.program_id(1) == 0))
        def _():
            outs[1][...] = jnp.zeros_like(outs[1])

        part = jnp.sum(jnp.sum(e * e, axis=1, keepdims=True), axis=0, keepdims=True)
        outs[1][...] += jnp.broadcast_to(part, outs[1].shape)

    return _fused(name, (d // tn, t // tm, 1), ins + [(target, blk)],
                  [(_sds((t, d), F32), blk), (_sds((8, LANES), F32), pl.BlockSpec((8, LANES), lambda j, i, k: (0, 0))),
                   (_sds((t, d), BF16), blk)],
                  [(0, 1, NN)], epilogue, temp_bytes=3 * tm * tn * 4,
                  semantics=("arbitrary", "arbitrary", "arbitrary"))


def _ffn_bwd_act(name, dy, wd, gu, deps=()):
    t, d = dy.shape
    f = wd.shape[0]
    nb = f // 4
    tm = min(t, 512)

    def body(dy_ref, wd_ref, gu_ref, *rest):
        dgu_ref, a_ref = rest[-2], rest[-1]
        dyv = dy_ref[...].astype(BF16)
        for c0 in range(0, nb, MXU_COLS):
            cs = slice(c0, min(c0 + MXU_COLS, nb))
            da = 0.5 * lax.dot_general(dyv, wd_ref[cs, :], NT, preferred_element_type=F32)
            g = gu_ref[0, :, cs].astype(F32)
            u = gu_ref[1, :, cs].astype(F32)
            s = _sigmoid(g)
            silu = g * s
            dgu_ref[0, :, cs] = (da * u * (s * (1.0 + g * (1.0 - s)))).astype(BF16)
            dgu_ref[1, :, cs] = (da * silu).astype(BF16)
            a_ref[:, cs] = (silu * u).astype(BF16)

    blocks = tm * d * 4 + nb * d * 2 + 5 * tm * nb * 2
    return pl.pallas_call(
        body, name=name, grid=(4, t // tm),
        in_specs=[pl.BlockSpec((tm, d), lambda j, i: (i, 0)),
                  pl.BlockSpec((nb, d), lambda j, i: (j, 0)),
                  pl.BlockSpec((2, tm, nb), lambda j, i: (0, i, j))] + [_ANY] * len(deps),
        out_specs=[pl.BlockSpec((2, tm, nb), lambda j, i: (0, i, j)), pl.BlockSpec((tm, nb), lambda j, i: (i, j))],
        out_shape=[_sds((2, t, f), BF16), _sds((t, f), BF16)],
        compiler_params=_params(("parallel", "parallel"), blocks, tm * d * 2 + 8 * tm * MXU_COLS * 4),
    )(dy, wd, gu, *deps)


def _ffn_bwd_dwd(name, a, dy, deps=(), side=None):
    t, f = a.shape
    d = dy.shape[1]
    tm = f // 4
    tn = min(d, 512)

    def epilogue(acc, ins, outs):
        outs[0][...] = (0.5 * acc).astype(BF16)

    return _fused(name, (4, d // tn, 1),
                  [(a, pl.BlockSpec((t, tm), lambda i, j, k: (0, i))),
                   (dy, pl.BlockSpec((t, tn), lambda i, j, k: (0, j)))],
                  [(_sds((f, d), BF16), pl.BlockSpec((tm, tn), lambda i, j, k: (i, j)))],
                  [(0, 1, TN)], epilogue, temp_bytes=t * tn * 2 + 2 * tm * tn * 4, deps=deps, side=side)


def _ffn_bwd_dh(name, dgu, wgu, deps=(), side=None):
    _, t, f = dgu.shape
    d, nb = wgu.shape[1], wgu.shape[2]
    tm = min(t, 512)

    def products(ins):
        return (lax.dot_general(ins[0][:, 0:nb], ins[1][0], NT, preferred_element_type=F32)
                + lax.dot_general(ins[0][:, nb:2 * nb], ins[1][1], NT, preferred_element_type=F32))

    def epilogue(acc, ins, outs):
        outs[0][...] = acc

    return _fused(name, (t // tm, 1, 4),
                  [(dgu, pl.BlockSpec((None, tm, 2 * nb), lambda i, j, k: (k // 2, i, k % 2))),
                   (wgu, pl.BlockSpec((2, d, nb), lambda i, j, k: (k, 0, 0)))],
                  [(_sds((t, d), F32), pl.BlockSpec((tm, d), lambda i, j, k: (i, 0)))],
                  products, epilogue, nk=4, acc_shape=(tm, d), temp_bytes=tm * d * 4, deps=deps, side=side)


def _ffn_bwd_dwgu(name, h, dgu, deps=(), side=None, rows=None):
    t, d = h.shape
    nb = dgu.shape[2] // 4
    tm = min(d, 512)
    row0, nrows = rows if rows is not None else (0, d)
    j0 = row0 // tm

    def epilogue(acc, ins, outs):
        outs[0][...] = acc.astype(BF16)

    return _fused(name, (N_DEV, nrows // tm, 1),
                  [(h, pl.BlockSpec((t, tm), lambda i, j, k: (0, j0 + j))),
                   (dgu, pl.BlockSpec((None, t, nb), lambda i, j, k: (i // 4, 0, i % 4)))],
                  [(_sds((N_DEV, nrows, nb), BF16), pl.BlockSpec((None, tm, nb), lambda i, j, k: (i, j, 0)))],
                  [(0, 1, TN)], epilogue, temp_bytes=2 * tm * nb * 4, deps=deps, side=side)


def _proj(h, w_in):
    t, d = h.shape
    nb = w_in.shape[3]
    tm = min(t, 512)

    def body(h_ref, w_ref, o_ref):
        hv = h_ref[...]
        o_ref[:, 0:nb] = jnp.dot(hv, w_ref[0], preferred_element_type=F32).astype(BF16)
        o_ref[:, nb:2 * nb] = jnp.dot(hv, w_ref[1], preferred_element_type=F32).astype(BF16)

    blocks = tm * d * 2 + 2 * d * nb * 2 + tm * 2 * nb * 4
    return pl.pallas_call(
        body, name="mix_proj", grid=(4, t // tm),
        in_specs=[pl.BlockSpec((tm, d), lambda j, i: (i, 0)),
                  pl.BlockSpec((None, 2, d, nb), lambda j, i: (j, 0, 0, 0))],
        out_specs=pl.BlockSpec((tm, 2 * nb), lambda j, i: (i, j)),
        out_shape=_sds((t, N_DEV * nb), BF16),
        compiler_params=_params(("parallel", "parallel"), blocks, 2 * tm * nb * 4),
    )(h, w_in)


def _shift_rows(u, k):
    t = u.shape[0]
    rolled = pltpu.roll(u, k % t, axis=0)
    row = lax.broadcasted_iota(jnp.int32, u.shape, 0)
    keep = (row >= k) if k > 0 else (row < t + k)
    return jnp.where(keep, rolled, 0.0)


def _conv_fwd(proj, conv_w):
    t = proj.shape[0]
    cw = conv_w.shape[1]
    tc = min(cw, 256)
    nc = cw // tc

    def epilogue(_, ins, outs):
        u = ins[2][...].astype(F32) * ins[0][...].astype(F32)
        w = ins[3][...]
        y = u * w[2:3, :] + _shift_rows(u, 1) * w[1:2, :] + _shift_rows(u, 2) * w[0:1, :]
        outs[0][...] = (ins[1][...].astype(F32) * y).astype(BF16)

    def col(seg):
        return pl.BlockSpec((t, tc), lambda i, j, k: (0, seg * nc + i))

    return _fused("conv_fwd", (nc, 1, 1),
                  [(proj, col(0)), (proj, col(1)), (proj, col(2)),
                   (conv_w, pl.BlockSpec((8, tc), lambda i, j, k: (0, i)))],
                  [(_sds((t, cw), BF16), pl.BlockSpec((t, tc), lambda i, j, k: (0, i)))],
                  [], epilogue, temp_bytes=6 * t * tc * 4)[0]


def _conv_bwd(proj, conv_w, dca, deps=()):
    t = proj.shape[0]
    cw = conv_w.shape[1]
    tc = min(cw, 256)
    nc = cw // tc

    def epilogue(_, ins, outs):
        xc, bg, cg = ins[0][...].astype(F32), ins[1][...].astype(F32), ins[2][...].astype(F32)
        w, dc = ins[3][...], ins[4][...]
        u = cg * xc
        u1, u2 = _shift_rows(u, 1), _shift_rows(u, 2)
        y = u * w[2:3, :] + u1 * w[1:2, :] + u2 * w[0:1, :]
        dconv = dc * bg
        du = dconv * w[2:3, :] + _shift_rows(dconv, -1) * w[1:2, :] + _shift_rows(dconv, -2) * w[0:1, :]
        outs[0][0] = (du * cg).astype(BF16)
        outs[0][1] = (dc * y).astype(BF16)
        outs[0][2] = (du * xc).astype(BF16)
        outs[1][...] = jnp.zeros_like(outs[1])
        outs[1][0:1, :] = jnp.sum(dconv * u2, axis=0, keepdims=True)
        outs[1][1:2, :] = jnp.sum(dconv * u1, axis=0, keepdims=True)
        outs[1][2:3, :] = jnp.sum(dconv * u, axis=0, keepdims=True)

    def col(seg):
        return pl.BlockSpec((t, tc), lambda i, j, k: (0, seg * nc + i))

    own = pl.BlockSpec((t, tc), lambda i, j, k: (0, i))
    wspec = pl.BlockSpec((8, tc), lambda i, j, k: (0, i))
    return _fused("conv_bwd", (nc, 1, 1),
                  [(proj, col(0)), (proj, col(1)), (proj, col(2)), (conv_w, wspec), (dca, own)],
                  [(_sds((3, t, cw), BF16), pl.BlockSpec((3, t, tc), lambda i, j, k: (0, 0, i))),
                   (_sds((8, cw), F32), wspec)],
                  [], epilogue, temp_bytes=10 * t * tc * 4, deps=deps)


def _split3(x):
    hi = x.astype(BF16)
    r1 = x - hi.astype(F32)
    mid = r1.astype(BF16)
    lo = (r1 - mid.astype(F32)).astype(BF16)
    return hi, mid, lo


def _head_selector(width):
    r = lax.broadcasted_iota(jnp.int32, (width, LANES), 0)
    c = lax.broadcasted_iota(jnp.int32, (width, LANES), 1)
    return (lax.shift_right_logical(r, 6) == c).astype(BF16)


def _head_sum(x, sel):
    return sum(jnp.dot(p, sel, preferred_element_type=F32) for p in _split3(x))


def _head_bcast(r, sel):
    return sum(lax.dot_general(p, sel, NT, preferred_element_type=F32) for p in _split3(r))


def _rope(x, c, sa, sb):
    n = x.shape[1]
    return x * c + pltpu.roll(x, n - ROT_DIM // 2, axis=1) * sa + pltpu.roll(x, ROT_DIM // 2, axis=1) * sb


def _rope_t(d, c, sa, sb):
    n = d.shape[1]
    return d * c + pltpu.roll(d * sa, ROT_DIM // 2, axis=1) + pltpu.roll(d * sb, n - ROT_DIM // 2, axis=1)


def _tile_lanes(tab, width):
    return tab if width == tab.shape[1] else jnp.tile(tab, (1, width // tab.shape[1]))


def _qk_prep(proj, gq, gk, rope_tabs, cw, kw):
    t = proj.shape[0]
    tm = _row_tile(t)

    def epilogue(_, ins, outs):
        c, sa, sb = ins[5][...], ins[6][...], ins[7][...]
        for src, gain, dst, width in ((0, 3, 0, cw), (1, 4, 1, kw)):
            xv = ins[src][...].astype(F32)
            sel = _head_selector(width)
            r = lax.rsqrt(_head_sum(xv * xv, sel) * (1.0 / HEAD_DIM) + RMS_EPS)
            xn = xv * _head_bcast(r, sel) * ins[gain][...]
            outs[dst][...] = _rope(xn, _tile_lanes(c, width), _tile_lanes(sa, width), _tile_lanes(sb, width)).astype(BF16)
        outs[2][...] = ins[2][...].astype(BF16)

    kblk = cw // kw
    tab = pl.BlockSpec((tm, LANES), lambda i, j, k: (i, 0))
    kspec = pl.BlockSpec((tm, kw), lambda i, j, k: (i, 0))
    return _fused("qk_prep", (t // tm, 1, 1),
                  [(proj, pl.BlockSpec((tm, cw), lambda i, j, k: (i, 3))),
                   (proj, pl.BlockSpec((tm, kw), lambda i, j, k: (i, 4 * kblk))),
                   (proj, pl.BlockSpec((tm, kw), lambda i, j, k: (i, 4 * kblk + 1))),
                   (gq, pl.BlockSpec((1, cw), lambda i, j, k: (0, 0))),
                   (gk, pl.BlockSpec((1, kw), lambda i, j, k: (0, 0))),
                   (rope_tabs[0], tab), (rope_tabs[1], tab), (rope_tabs[2], tab)],
                  [(_sds((t, cw), BF16), pl.BlockSpec((tm, cw), lambda i, j, k: (i, 0))),
                   (_sds((t, kw), BF16), kspec), (_sds((t, kw), BF16), kspec)],
                  [], epilogue, temp_bytes=12 * tm * cw * 4)


def _qk_prep_bwd(proj, gq, gk, rope_tabs, dq, dkc, dkp, dvc, dvp, cw, kw):
    t = proj.shape[0]
    tm = BLOCK
    nblk = t // tm

    def epilogue(_, ins, outs):
        c, sa, sb = ins[5][...], ins[6][...], ins[7][...]
        has_next = (pl.program_id(0) < nblk - 1).astype(F32)
        dk = ins[9][...] + has_next * ins[10][...]
        dv = ins[11][...] + has_next * ins[12][...]
        pieces = []
        for src, gain, dval, dst, width in ((0, 3, ins[8][...], 1, cw), (1, 4, dk, 2, kw)):
            xv, gv = ins[src][...].astype(F32), ins[gain][...]
            sel = _head_selector(width)
            r = _head_bcast(lax.rsqrt(_head_sum(xv * xv, sel) * (1.0 / HEAD_DIM) + RMS_EPS), sel)
            xh = xv * r
            dxn = _rope_t(dval, _tile_lanes(c, width), _tile_lanes(sa, width), _tile_lanes(sb, width))
            u = dxn * gv
            dot = _head_bcast(_head_sum(u * xh, sel), sel) * (1.0 / HEAD_DIM)
            pieces.append((r * (u - xh * dot)).astype(BF16))
            ri = lax.broadcasted_iota(jnp.int32, (width, LANES), 0)
            ci = lax.broadcasted_iota(jnp.int32, (width, LANES), 1)
            fold = (lax.bitwise_and(ri, HEAD_DIM - 1) == ci).astype(BF16)
            colsum = jnp.broadcast_to(jnp.sum(dxn * xh, axis=0, keepdims=True), (8, width))
            part = sum(jnp.dot(p, fold, preferred_element_type=F32) for p in _split3(colsum))

            @pl.when(pl.program_id(0) == 0)
            def _():
                outs[dst][...] = jnp.zeros_like(outs[dst])

            outs[dst][0:1, :] += part[0:1, :]
        outs[0][:, 0:cw] = pieces[0]
        outs[0][:, cw:cw + kw] = pieces[1]
        outs[0][:, cw + kw:cw + 2 * kw] = dv.astype(BF16)

    kblk = cw // kw
    tab = pl.BlockSpec((tm, LANES), lambda i, j, k: (i, 0))
    kcur = pl.BlockSpec((tm, kw), lambda i, j, k: (i, 0))
    knext = pl.BlockSpec((tm, kw), lambda i, j, k: (jnp.minimum(i + 1, nblk - 1), 0))
    acc = pl.BlockSpec((8, LANES), lambda i, j, k: (0, 0))
    return _fused("qk_prep_bwd", (nblk, 1, 1),
                  [(proj, pl.BlockSpec((tm, cw), lambda i, j, k: (i, 3))),
                   (proj, pl.BlockSpec((tm, kw), lambda i, j, k: (i, 4 * kblk))),
                   (proj, pl.BlockSpec((tm, kw), lambda i, j, k: (i, 4 * kblk + 1))),
                   (gq, pl.BlockSpec((1, cw), lambda i, j, k: (0, 0))),
                   (gk, pl.BlockSpec((1, kw), lambda i, j, k: (0, 0))),
                   (rope_tabs[0], tab), (rope_tabs[1], tab), (rope_tabs[2], tab),
                   (dq, pl.BlockSpec((tm, cw), lambda i, j, k: (i, 0))),
                   (dkc, kcur), (dkp, knext), (dvc, kcur), (dvp, knext)],
                  [(_sds((t, cw + 2 * kw), BF16), pl.BlockSpec((tm, cw + 2 * kw), lambda i, j, k: (i, 0))),
                   (_sds((8, LANES), F32), acc), (_sds((8, LANES), F32), acc)],
                  [], epilogue, temp_bytes=16 * tm * cw * 4, semantics=("arbitrary", "arbitrary", "arbitrary"))


def _attn_mask(n):
    key = lax.broadcasted_iota(jnp.int32, (2 * BLOCK, GROUP * BLOCK), 0)
    qry = lax.bitwise_and(lax.broadcasted_iota(jnp.int32, (2 * BLOCK, GROUP * BLOCK), 1), BLOCK - 1)
    return (key > qry) & (key <= qry + BLOCK) & ((key >= BLOCK) | (n > 0))


def _stack_heads(x, h):
    return jnp.concatenate([x[:, (h * GROUP + g) * HEAD_DIM:(h * GROUP + g + 1) * HEAD_DIM] for g in range(GROUP)], axis=0)


def _softmax_with_sink(q4, k2, sink_ref, h, valid):
    sink = jnp.concatenate([sink_ref[h * GROUP + g:h * GROUP + g + 1, :] for g in range(GROUP)], axis=1)
    s = lax.dot_general(k2, q4, NT, preferred_element_type=F32) * ATTN_SCALE
    s = jnp.where(valid, s, NEG_INF)
    m = jnp.maximum(jnp.max(s, axis=0, keepdims=True), sink)
    p = jnp.exp(s - m)
    es = jnp.exp(sink - m)
    inv = 1.0 / (jnp.sum(p, axis=0, keepdims=True) + es)
    return p * inv, es * inv


def _attn_fwd(qn, kn, vb, sink_rows):
    t, cw = qn.shape
    kw = kn.shape[1]
    nkv = kw // HEAD_DIM

    def body(q_ref, kp_ref, kc_ref, vp_ref, vc_ref, sink_ref, o_ref):
        valid = _attn_mask(pl.program_id(0))
        qv = q_ref[...]
        kp, kc, vp, vc = kp_ref[...], kc_ref[...], vp_ref[...], vc_ref[...]
        outs = []
        for h in range(nkv):
            hs = slice(h * HEAD_DIM, (h + 1) * HEAD_DIM)
            k2 = jnp.concatenate([kp[:, hs], kc[:, hs]], axis=0)
            v2 = jnp.concatenate([vp[:, hs], vc[:, hs]], axis=0)
            pn, _ = _softmax_with_sink(_stack_heads(qv, h), k2, sink_ref, h, valid)
            o4 = lax.dot_general(pn.astype(BF16), v2, TN, preferred_element_type=F32)
            outs += [o4[g * BLOCK:(g + 1) * BLOCK] for g in range(GROUP)]
        o_ref[...] = jnp.concatenate(outs, axis=-1).astype(BF16)

    cur = lambda n: (n, 0)
    prev = lambda n: (jnp.maximum(n - 1, 0), 0)
    return pl.pallas_call(
        body, name="attn_fwd", grid=(t // BLOCK,),
        in_specs=[pl.BlockSpec((BLOCK, cw), cur),
                  pl.BlockSpec((BLOCK, kw), prev), pl.BlockSpec((BLOCK, kw), cur),
                  pl.BlockSpec((BLOCK, kw), prev), pl.BlockSpec((BLOCK, kw), cur),
                  pl.BlockSpec(sink_rows.shape, lambda n: (0, 0))],
        out_specs=pl.BlockSpec((BLOCK, cw), cur),
        out_shape=_sds((t, cw), BF16),
        compiler_params=_params(("parallel",), BLOCK * (cw + 4 * kw) * 2 + BLOCK * cw * 2, 8 << 20),
    )(qn, kn, kn, vb, vb, sink_rows)


def _attn_bwd(qn, kn, vb, sink_rows, do):
    t, cw = qn.shape
    kw = kn.shape[1]
    nkv = kw // HEAD_DIM
    nq = nkv * GROUP

    def body(q_ref, kp_ref, kc_ref, vp_ref, vc_ref, sink_ref, do_ref,
             dq_ref, dkc_ref, dkp_ref, dvc_ref, dvp_ref, dsink_ref):
        n = pl.program_id(0)
        valid = _attn_mask(n)
        qv, dov = q_ref[...], do_ref[...]
        kp, kc, vp, vc = kp_ref[...], kc_ref[...], vp_ref[...], vc_ref[...]
        dqs, dks, dvs, dsinks = [], [], [], []
        for h in range(nkv):
            hs = slice(h * HEAD_DIM, (h + 1) * HEAD_DIM)
            k2 = jnp.concatenate([kp[:, hs], kc[:, hs]], axis=0)
            v2 = jnp.concatenate([vp[:, hs], vc[:, hs]], axis=0)
            q4 = _stack_heads(qv, h)
            dob = _stack_heads(dov, h).astype(BF16)
            pn, psink = _softmax_with_sink(q4, k2, sink_ref, h, valid)
            dpn = lax.dot_general(v2, dob, NT, preferred_element_type=F32)
            dvs.append(jnp.dot(pn.astype(BF16), dob, preferred_element_type=F32))
            delta = jnp.sum(pn * dpn, axis=0, keepdims=True)
            ds = (pn * (dpn - delta) * ATTN_SCALE).astype(BF16)
            dks.append(jnp.dot(ds, q4, preferred_element_type=F32))
            dq4 = lax.dot_general(ds, k2, TN, preferred_element_type=F32)
            dsink4 = -psink * delta
            for g in range(GROUP):
                dqs.append(dq4[g * BLOCK:(g + 1) * BLOCK])
                dsinks.append(jnp.broadcast_to(jnp.sum(dsink4[:, g * BLOCK:(g + 1) * BLOCK], axis=1, keepdims=True), (1, LANES)))
        dq_ref[...] = jnp.concatenate(dqs, axis=-1)
        dkp_ref[...] = jnp.concatenate([d[:BLOCK] for d in dks], axis=-1)
        dkc_ref[...] = jnp.concatenate([d[BLOCK:] for d in dks], axis=-1)
        dvp_ref[...] = jnp.concatenate([d[:BLOCK] for d in dvs], axis=-1)
        dvc_ref[...] = jnp.concatenate([d[BLOCK:] for d in dvs], axis=-1)

        @pl.when(n == 0)
        def _():
            dsink_ref[...] = jnp.zeros_like(dsink_ref)

        dsink_ref[...] += jnp.concatenate(dsinks, axis=0)

    cur = lambda n: (n, 0)
    prev = lambda n: (jnp.maximum(n - 1, 0), 0)
    kspec = pl.BlockSpec((BLOCK, kw), cur)
    return pl.pallas_call(
        body, name="attn_bwd", grid=(t // BLOCK,),
        in_specs=[pl.BlockSpec((BLOCK, cw), cur),
                  pl.BlockSpec((BLOCK, kw), prev), kspec,
                  pl.BlockSpec((BLOCK, kw), prev), kspec,
                  pl.BlockSpec(sink_rows.shape, lambda n: (0, 0)),
                  pl.BlockSpec((BLOCK, cw), cur)],
        out_specs=[pl.BlockSpec((BLOCK, cw), cur), kspec, kspec, kspec, kspec,
                   pl.BlockSpec((nq, LANES), lambda n: (0, 0))],
        out_shape=[_sds((t, cw), F32)] + [_sds((t, kw), F32)] * 4 + [_sds((nq, LANES), F32)],
        compiler_params=_params(("arbitrary",), BLOCK * (cw + 4 * kw) * 2 + 2 * BLOCK * cw * 4 + 4 * BLOCK * kw * 4, 12 << 20),
    )(qn, kn, kn, vb, vb, sink_rows, do)


def _mix_out(ca, o, woc, woa, proj):
    t, cw = ca.shape
    nb = woc.shape[2]
    d = N_DEV * nb
    tm = min(t, 1024)
    ga0 = (3 * cw + cw + 2 * (cw // 4)) // nb

    def body(ca_ref, o_ref, woc_ref, woa_ref, ga_ref, gb_ref, m_ref, ya_ref, yb_ref):
        ya = jnp.dot(ca_ref[...], woc_ref[...], preferred_element_type=F32)
        yb = jnp.dot(o_ref[...], woa_ref[...], preferred_element_type=F32)
        ya_ref[...] = ya.astype(BF16)
        yb_ref[...] = yb.astype(BF16)
        m_ref[...] = (_sigmoid(ga_ref[...].astype(F32)) * ya + _sigmoid(gb_ref[...].astype(F32)) * yb).astype(BF16)

    act = pl.BlockSpec((tm, cw), lambda i, j: (i, 0))
    wsp = pl.BlockSpec((None, cw, nb), lambda i, j: (j, 0, 0))
    osp = pl.BlockSpec((tm, nb), lambda i, j: (i, j))
    blocks = 2 * tm * cw * 2 + 2 * cw * nb * 2 + 2 * tm * nb * 4 + 3 * tm * nb * 2
    return pl.pallas_call(
        body, name="mix_out", grid=(t // tm, N_DEV),
        in_specs=[act, act, wsp, wsp,
                  pl.BlockSpec((tm, nb), lambda i, j: (i, ga0 + j)),
                  pl.BlockSpec((tm, nb), lambda i, j: (i, ga0 + N_DEV + j))],
        out_specs=[osp, osp, osp],
        out_shape=[_sds((t, d), BF16)] * 3,
        compiler_params=_params(("parallel", "parallel"), blocks, 6 * tm * nb * 4),
    )(ca, o, woc, woa, proj, proj)


def _mix_residual(merged, wo, x):
    t, d = x.shape
    tm = min(t, 512)

    def epilogue(acc, ins, outs):
        outs[0][...] = ins[2][...] + acc

    row = pl.BlockSpec((tm, d), lambda i, j, k: (i, 0))
    return _fused("mix_residual", (t // tm, 1, 1),
                  [(merged, row), (wo, pl.BlockSpec((d, d), lambda i, j, k: (0, 0))), (x, row)],
                  [(_sds((t, d), F32), row)], [(0, 1, NN)], epilogue, temp_bytes=2 * tm * d * 4)[0]


def _mix_bwd_gates(dx, wo, ya, yb, proj, cw):
    t, d = dx.shape
    tm = min(t, 1024)
    tn = min(d, 512)
    ga0 = (4 * cw + 2 * (cw // 4)) // tn

    def epilogue(acc, ins, outs):
        sa, sb = _sigmoid(ins[4][...].astype(F32)), _sigmoid(ins[5][...].astype(F32))
        outs[0][...] = (acc * sa).astype(BF16)
        outs[1][...] = (acc * sb).astype(BF16)
        outs[2][0] = (acc * ins[2][...].astype(F32) * sa * (1.0 - sa)).astype(BF16)
        outs[2][1] = (acc * ins[3][...].astype(F32) * sb * (1.0 - sb)).astype(BF16)

    blk = pl.BlockSpec((tm, tn), lambda i, j, k: (i, j))
    return _fused("mix_bwd_gates", (t // tm, d // tn, 1),
                  [(dx, pl.BlockSpec((tm, d), lambda i, j, k: (i, 0))),
                   (wo, pl.BlockSpec((tn, d), lambda i, j, k: (j, 0))),
                   (ya, blk), (yb, blk),
                   (proj, pl.BlockSpec((tm, tn), lambda i, j, k: (i, ga0 + j))),
                   (proj, pl.BlockSpec((tm, tn), lambda i, j, k: (i, ga0 + d // tn + j)))],
                  [(_sds((t, d), BF16), blk), (_sds((t, d), BF16), blk),
                   (_sds((2, t, d), BF16), pl.BlockSpec((2, tm, tn), lambda i, j, k: (0, i, j)))],
                  [(0, 1, NT)], epilogue, temp_bytes=8 * tm * tn * 4)


def _tn_matmul(name, a, b, tm, out_dtype=BF16):
    t, m = a.shape
    n = b.shape[1]

    def epilogue(acc, ins, outs):
        outs[0][...] = acc.astype(out_dtype)

    return _fused(name, (m // tm, 1, 1),
                  [(a, pl.BlockSpec((t, tm), lambda i, j, k: (0, i))),
                   (b, pl.BlockSpec((t, n), lambda i, j, k: (0, 0)))],
                  [(_sds((m, n), out_dtype), pl.BlockSpec((tm, n), lambda i, j, k: (i, 0)))],
                  [(0, 1, TN)], epilogue, temp_bytes=2 * tm * n * 4)[0]


def _out_proj_bwd_act(dya, dyb, woc, woa, deps=()):
    t, d = dya.shape
    kdim, nb = woc.shape[1], woc.shape[2]
    tm = min(t, 512)

    def body(dya_ref, dyb_ref, woc_ref, woa_ref, *rest):
        for dy_ref, w_ref, o_ref in ((dya_ref, woc_ref, rest[-2]), (dyb_ref, woa_ref, rest[-1])):
            total = None
            for j in range(N_DEV):
                part = lax.dot_general(dy_ref[:, j * nb:(j + 1) * nb], w_ref[j], NT, preferred_element_type=F32)
                total = part if total is None else total + part
            o_ref[...] = total

    row = pl.BlockSpec((tm, d), lambda i: (i, 0))
    wsp = pl.BlockSpec((N_DEV, kdim, nb), lambda i: (0, 0, 0))
    osp = pl.BlockSpec((tm, kdim), lambda i: (i, 0))
    blocks = 2 * tm * d * 2 + 2 * N_DEV * kdim * nb * 2 + 2 * tm * kdim * 4
    return pl.pallas_call(
        body, name="mix_bwd_dca_do", grid=(t // tm,),
        in_specs=[row, row, wsp, wsp] + [_ANY] * len(deps), out_specs=[osp, osp],
        out_shape=[_sds((t, kdim), F32)] * 2,
        compiler_params=_params(("parallel",), blocks, 4 * tm * kdim * 4),
    )(dya, dyb, woc, woa, *deps)


def _out_proj_bwd_w(ca, o, dya, dyb, nb):
    t, kdim = ca.shape

    def body(ca_ref, o_ref, dya_ref, dyb_ref, dwoc_ref, dwoa_ref):
        dwoc_ref[...] = lax.dot_general(ca_ref[...], dya_ref[...], TN, preferred_element_type=F32).astype(BF16)
        dwoa_ref[...] = lax.dot_general(o_ref[...], dyb_ref[...], TN, preferred_element_type=F32).astype(BF16)

    act = pl.BlockSpec((t, kdim), lambda j: (0, 0))
    col = pl.BlockSpec((t, nb), lambda j: (0, j))
    osp = pl.BlockSpec((None, kdim, nb), lambda j: (j, 0, 0))
    blocks = 2 * t * kdim * 2 + 2 * t * nb * 2 + 2 * kdim * nb * 2
    return pl.pallas_call(
        body, name="mix_bwd_dwoc_dwoa", grid=(N_DEV,),
        in_specs=[act, act, col, col], out_specs=[osp, osp],
        out_shape=[_sds((N_DEV, kdim, nb), BF16)] * 2,
        compiler_params=_params(("parallel",), blocks, 4 * kdim * nb * 4),
    )(ca, o, dya, dyb)


def _proj_bwd_act(dproj, w_in, deps=()):
    t, n = dproj.shape
    d, nb = w_in.shape[2], w_in.shape[3]
    tm = min(t, 512)

    def epilogue(acc, ins, outs):
        outs[0][...] = acc

    def products(ins):
        return (lax.dot_general(ins[0][:, 0:nb], ins[1][0], NT, preferred_element_type=F32)
                + lax.dot_general(ins[0][:, nb:2 * nb], ins[1][1], NT, preferred_element_type=F32))

    return _fused("mix_bwd_dh", (t // tm, 1, 4),
                  [(dproj, pl.BlockSpec((tm, 2 * nb), lambda i, j, k: (i, k))),
                   (w_in, pl.BlockSpec((None, 2, d, nb), lambda i, j, k: (k, 0, 0, 0)))],
                  [(_sds((t, d), F32), pl.BlockSpec((tm, d), lambda i, j, k: (i, 0)))],
                  products, epilogue, nk=4, acc_shape=(tm, d), temp_bytes=tm * d * 4, deps=deps)[0]


def _proj_bwd_w(h, dproj):
    t, d = h.shape
    nb = dproj.shape[1] // N_DEV
    tm = min(d, 512)

    def body(h_ref, dp_ref, o_ref):
        hv = h_ref[...]
        o_ref[0] = lax.dot_general(hv, dp_ref[:, 0:nb], TN, preferred_element_type=F32).astype(BF16)
        o_ref[1] = lax.dot_general(hv, dp_ref[:, nb:2 * nb], TN, preferred_element_type=F32).astype(BF16)

    blocks = t * tm * 2 + t * 2 * nb * 2 + 2 * tm * nb * 2
    return pl.pallas_call(
        body, name="mix_bwd_dwin", grid=(4, d // tm),
        in_specs=[pl.BlockSpec((t, tm), lambda j, i: (0, i)),
                  pl.BlockSpec((t, 2 * nb), lambda j, i: (0, j))],
        out_specs=pl.BlockSpec((None, 2, tm, nb), lambda j, i: (j, 0, i, 0)),
        out_shape=_sds((4, 2, d, nb), BF16),
        compiler_params=_params(("parallel", "parallel"), blocks, 4 * tm * nb * 4),
    )(h, dproj)


def _adamw_math(w, g, m, v):
    m = ADAM_B1 * m + (1.0 - ADAM_B1) * g
    v = ADAM_B2 * v + (1.0 - ADAM_B2) * (g * g)
    m_hat = m / (1.0 - ADAM_B1 ** ADAM_STEP)
    v_hat = v / (1.0 - ADAM_B2 ** ADAM_STEP)
    delta = -ADAM_LR * (m_hat / (jnp.sqrt(v_hat) + ADAM_EPS) + ADAM_WD * w)
    return delta, m, v


def _adamw_small(parts, w, m, v, spans):
    n = len(spans)

    def body(p_ref, w_ref, m_ref, v_ref, *outs):
        g = p_ref[0]
        for s in range(1, N_DEV):
            g = g + p_ref[s]
        delta, mn, vn = _adamw_math(w_ref[...], g, m_ref[...], v_ref[...])
        for q, res in enumerate((g, delta, mn, vn)):
            for i, (off, width) in enumerate(spans):
                outs[q * n + i][...] = res[:, off:off + width]

    return pl.pallas_call(
        body, name="adamw_small", out_shape=[_sds((1, width), F32) for _ in range(4) for _, width in spans],
        compiler_params=pltpu.CompilerParams(vmem_limit_bytes=VMEM_CAP),
    )(parts, w, m, v)


def _chip_sum(sums_ref):
    g = sums_ref[0].astype(F32)
    for k in range(1, 4):
        g = g + sums_ref[k].astype(F32)
    return g


def _adamw_chips(name, sums, w, m, v, tr, deps=(), row0=0, into=None):
    r, c = w.shape
    rs = sums.shape[1]
    i0 = row0 // tr
    n_pass = len(deps) + (4 if into is not None else 0)

    def body(sums_ref, w_ref, m_ref, v_ref, *rest):
        g_out, d_out, m_out, v_out = rest[n_pass:]
        g = _chip_sum(sums_ref)
        delta, mn, vn = _adamw_math(w_ref[...], g, m_ref[...], v_ref[...])
        g_out[...] = g
        d_out[...] = delta
        m_out[...] = mn
        v_out[...] = vn

    blk = pl.BlockSpec((tr, c), lambda i: (i0 + i, 0))
    blocks = 4 * tr * c * 2 + 7 * tr * c * 4
    passed = list(deps) + (list(into) if into is not None else [])
    aliases = {4 + len(deps) + q: q for q in range(4)} if into is not None else {}
    return pl.pallas_call(
        body, name=name, grid=(rs // tr,),
        in_specs=[pl.BlockSpec((4, tr, c), lambda i: (0, i, 0)), blk, blk, blk] + [_ANY] * n_pass,
        out_specs=[blk] * 4, out_shape=[_sds((r, c), F32)] * 4,
        input_output_aliases=aliases,
        compiler_params=_params(("parallel",), blocks, 6 * tr * c * 4),
    )(sums, w, m, v, *passed)


def _adamw_side(contrib, w, m, v, n_tiles, step_of):
    r, c = w.shape
    tr = r // n_tiles
    assert tr * n_tiles == r and tr % 16 == 0, (r, n_tiles)

    def tile(i, j, k):
        return jnp.minimum(step_of(i, j, k), n_tiles - 1)

    blk = pl.BlockSpec((tr, c), lambda i, j, k: (tile(i, j, k), 0))
    ins = [(contrib, pl.BlockSpec((4, tr, c), lambda i, j, k: (0, tile(i, j, k), 0))), (w, blk), (m, blk), (v, blk)]
    outs = [(_sds((r, c), F32), blk)] * 4

    def fn(in_refs, out_refs):
        @pl.when(step_of(pl.program_id(0), pl.program_id(1), pl.program_id(2)) < n_tiles)
        def _():
            g = _chip_sum(in_refs[0])
            delta, mn, vn = _adamw_math(in_refs[1][...], g, in_refs[2][...], in_refs[3][...])
            out_refs[0][...] = g
            out_refs[1][...] = delta
            out_refs[2][...] = mn
            out_refs[3][...] = vn

    return ins, outs, fn


def _rope_tables(t):
    half = ROT_DIM // 2
    inv_freq = 1.0 / (ROPE_THETA ** (jnp.arange(0, ROT_DIM, 2, dtype=F32) / ROT_DIM))
    ang = jnp.arange(t, dtype=F32)[:, None] * inv_freq[None, :]
    cos, sin = jnp.cos(ang), jnp.sin(ang)
    ones = jnp.ones((t, HEAD_DIM - ROT_DIM), F32)
    zeros = jnp.zeros((t, HEAD_DIM - half), F32)
    c = jnp.concatenate([cos, cos, ones], axis=1)
    sa = jnp.concatenate([-sin, zeros], axis=1)
    sb = jnp.concatenate([jnp.zeros((t, half), F32), sin, jnp.zeros((t, HEAD_DIM - ROT_DIM), F32)], axis=1)
    return tuple(jnp.tile(a, (1, LANES // HEAD_DIM)) for a in (c, sa, sb))


def _pad_rows(a, rows=8):
    return jnp.pad(a, ((0, rows - a.shape[0]), (0, 0)))


def kernel(x, g_ffn1, w_gu1, w_down1, g_mix, w_in, conv_w, q_norm_g, k_norm_g, sinks, w_out_conv, w_out_attn, w_o, g_ffn2, w_gu2, w_down2, loss_target, m_g_ffn1, m_w_gu1, m_w_down1, m_g_mix, m_w_in, m_conv_w, m_q_norm_g, m_k_norm_g, m_sinks, m_w_out_conv, m_w_out_attn, m_w_o, m_g_ffn2, m_w_gu2, m_w_down2, v_g_ffn1, v_w_gu1, v_w_down1, v_g_mix, v_w_in, v_conv_w, v_q_norm_g, v_k_norm_g, v_sinks, v_w_out_conv, v_w_out_attn, v_w_o, v_g_ffn2, v_w_gu2, v_w_down2):
    t, d = x.shape[1], x.shape[2]
    cw = d // 2
    kw = cw // GROUP
    nq = cw // HEAD_DIM
    xs, target = x.reshape(t, d), loss_target.reshape(t, d)
    me = 4 * lax.axis_index("x") + 2 * lax.axis_index("y") + lax.axis_index("c")

    big = {"w_gu1": w_gu1, "w_down1": w_down1, "w_in": w_in, "w_out_conv": w_out_conv,
           "w_out_attn": w_out_attn, "w_o": w_o, "w_gu2": w_gu2, "w_down2": w_down2}
    big_m = {"w_gu1": m_w_gu1, "w_down1": m_w_down1, "w_in": m_w_in, "w_out_conv": m_w_out_conv,
             "w_out_attn": m_w_out_attn, "w_o": m_w_o, "w_gu2": m_w_gu2, "w_down2": m_w_down2}
    big_v = {"w_gu1": v_w_gu1, "w_down1": v_w_down1, "w_in": v_w_in, "w_out_conv": v_w_out_conv,
             "w_out_attn": v_w_out_attn, "w_o": v_w_o, "w_gu2": v_w_gu2, "w_down2": v_w_down2}
    names = list(big)

    tiles = {"w_gu1": 256, "w_gu2": 256, "w_in": 256, "w_down1": 176, "w_down2": 176,
             "w_out_conv": 1024, "w_out_attn": 1024, "w_o": 128}

    def row_tile(n):
        r = big[n].shape[1]
        return tiles[n] if r % tiles[n] == 0 else r

    rs_shape = {n: big[n].shape[1:] for n in names}
    half = rs_shape["w_gu1"][0] // 2
    rs_shape["w_gu1_lo"] = rs_shape["w_gu1_hi"] = (half, rs_shape["w_gu1"][1])

    def add_tile(n):
        r, c = rs_shape[n]
        while r * c * 2 > (3 << 20) and r % 32 == 0:
            r //= 2
        return r

    me_arr = me.astype(jnp.int32).reshape(1)
    sources = [(n, big[n][0], BF16, row_tile(n)) for n in names] + [("conv_w", _pad_rows(conv_w[0]), F32, 8)]
    issue_order = [0, 1, 2, 8, 3, 4, 5, 6, 7]
    first = _place_shard("place_" + names[0], sources[0][1], BF16, me_arr, sources[0][3])
    started = [_gather_start("gather_start_first", [first])]
    early = {2: (big_m["w_in"][0], big_v["w_in"][0])}
    rest = [_place_shard("place_" + sources[i][0], sources[i][1], sources[i][2], me_arr, sources[i][3],
                         deps=(started[0][3],) + early.get(i, ())) for i in issue_order[1:]]
    started.append(_gather_start("gather_start_rest", rest))
    where = {0: (0, 0)}
    where.update({i: (1, p) for p, i in enumerate(issue_order[1:])})

    def fetch(tag, idxs, after, forward=True):
        call = where[idxs[0]][0]
        send, recv, stacks, _ = started[call]
        positions = [where[i][1] for i in idxs]
        got = _gather_wait("gather_wait_" + tag, positions, send, recv, [stacks[p] for p in positions], after)
        return _forward_to_sibling("gather_forward_" + tag, got) if forward else got

    rope_tabs = _rope_tables(t)
    gq = jnp.tile(q_norm_g, (1, nq))
    gk = jnp.tile(k_norm_g, (1, nq // GROUP))
    sink_rows = jnp.broadcast_to(sinks[0][:, None], (nq, LANES))

    wts = {}
    h1 = _rms_fwd("ffn1_norm", xs, g_ffn1)
    wts["w_gu1"], = fetch("gu1", [0], started[1][3])
    gu1, a1 = _ffn_up("ffn1_up", h1, wts["w_gu1"])
    wts["w_down1"], = fetch("down1", [1], a1)
    wd1 = wts["w_down1"].reshape(-1, d)
    x1 = _ffn_down("ffn1_down", a1, wd1, xs)
    h2 = _rms_fwd("mix_norm", x1, g_mix)
    wts["w_in"], conv_land = fetch("in", [2, 8], h2)
    w_in_full = wts["w_in"].reshape(4, 2, d, -1)
    conv_full = jnp.transpose(conv_land, (1, 0, 2)).reshape(8, cw)
    proj = _proj(h2, w_in_full)
    ca = _conv_fwd(proj, conv_full)
    qn, kn, vb = _qk_prep(proj, gq, gk, rope_tabs, cw, kw)
    o = _attn_fwd(qn, kn, vb, sink_rows)
    wts["w_out_conv"], wts["w_out_attn"] = fetch("out", [3, 4], o)
    merged, ya, yb = _mix_out(ca, o, wts["w_out_conv"], wts["w_out_attn"], proj)
    wts["w_o"], = fetch("o", [5], merged)
    wo = wts["w_o"].reshape(d, d)
    x2 = _mix_residual(merged, wo, x1)
    h3 = _rms_fwd("ffn2_norm", x2, g_ffn2)
    mine = lax.axis_index("c").astype(jnp.int32).reshape(1)
    got = fetch("gu2", [6], h3, forward=False)
    fsend, frecv, got = _forward_start("gather_forward_start_gu2", got)
    part = _ffn_up("ffn2_up_mine", h3, got[0], parity=mine)
    wts["w_gu2"], = _forward_wait("gather_forward_wait_gu2", fsend, frecv, got, part[1])
    gu2, a2 = _ffn_up("ffn2_up_sibling", h3, wts["w_gu2"], parity=1 - mine, into=part)
    wts["w_down2"], = fetch("down2", [7], a2)
    wd2 = wts["w_down2"].reshape(-1, d)
    dy, sq, dy_bf = _ffn_down("ffn2_down", a2, wd2, x2, target=target)
    loss = lax.psum(sq[0, 0] * (0.5 / d), ("x", "y", "c"))

    place = jnp.stack([lax.axis_index("c"), 2 * lax.axis_index("x") + lax.axis_index("y")]).astype(jnp.int32)
    def pair_start(tag, group, grads, deps=()):
        stacks = [grads[n].reshape((4, 2) + rs_shape[n]) for n in group]
        lands = [lax.empty((4,) + rs_shape[n], BF16) for n in group]
        return _pair_start("rs_pair_start_" + tag, stacks, lands, deps)

    def chip_start(tag, group, pending, after):
        send, recv, stacks, lands, _ = pending
        stacks, lands = _pair_wait("rs_pair_wait_" + tag, send, recv, stacks, lands, after)
        added = [_pair_add("rs_pair_add_" + n, st, ld, place, add_tile(n)) for n, st, ld in zip(group, stacks, lands)]
        return _chip_start("rs_chip_start_" + tag, [a[0] for a in added], [a[1] for a in added])

    group_a, group_b, group_c = ["w_down2", "w_gu2"], ["w_o", "w_out_conv", "w_out_attn"], ["w_in"]
    group_d, group_e, group_f = ["w_down1"], ["w_gu1_lo"], ["w_gu1_hi"]
    g = {}
    dgu2, a2 = _ffn_bwd_act("ffn2_bwd_act", dy_bf, wd2, gu2)
    g["w_down2"], = _ffn_bwd_dwd("ffn2_bwd_dwd", a2, dy_bf)
    g["w_gu2"], = _ffn_bwd_dwgu("ffn2_bwd_dwgu", h3, dgu2)
    pend_a = pair_start("a", group_a, g)
    dh3, = _ffn_bwd_dh("ffn2_bwd_dh", dgu2, wts["w_gu2"], deps=(pend_a[4],))
    ring_a = chip_start("a", group_a, pend_a, dh3)
    dx2, dg_ffn2, dx2_bf = _rms_bwd("ffn2_bwd_rms", x2, g_ffn2, dh3, dy, deps=(ring_a[4],), with_bf16=True)

    dya, dyb, dgates = _mix_bwd_gates(dx2_bf, wo, ya, yb, proj, cw)
    g["w_o"] = _tn_matmul("mix_bwd_dwo", merged, dx2_bf, min(d, 512))
    g["w_out_conv"], g["w_out_attn"] = _out_proj_bwd_w(ca, o, dya, dyb, d // N_DEV)
    pend_b = pair_start("b", group_b, g)
    dca, do = _out_proj_bwd_act(dya, dyb, wts["w_out_conv"], wts["w_out_attn"], deps=(pend_b[4],))
    ring_b = chip_start("b", group_b, pend_b, do)
    d3, dconv_w = _conv_bwd(proj, conv_full, dca, deps=(ring_b[4],))
    dq, dkc, dkp, dvc, dvp, dsink = _attn_bwd(qn, kn, vb, sink_rows, do)
    dqkv, dgq, dgk = _qk_prep_bwd(proj, gq, gk, rope_tabs, dq, dkc, dkp, dvc, dvp, cw, kw)
    dproj = jnp.concatenate([d3[0], d3[1], d3[2], dqkv, dgates[0], dgates[1]], axis=1)
    g["w_in"] = _proj_bwd_w(h2, dproj)
    pend_c = pair_start("c", group_c, g)
    dh2 = _proj_bwd_act(dproj, w_in_full, deps=(pend_c[4],))
    ring_c = chip_start("c", group_c, pend_c, dh2)
    dx1, dg_mix, dx1_bf = _rms_bwd("mix_bwd_rms", x1, g_mix, dh2, dx2, deps=(ring_c[4],), with_bf16=True)

    big_out = {}
    arrived = {}

    def wait_group(tag, group, ring, after):
        send, recv, parts, lands2, _ = ring
        parts, lands2 = _chip_wait("rs_chip_wait_" + tag, send, recv, parts, lands2, after)
        arrived.update(dict(zip(group, lands2)))

    def update(n, after):
        res = _adamw_chips("adamw_" + n, arrived[n], big[n][0], big_m[n][0], big_v[n][0], row_tile(n), deps=(after,))
        big_out[n] = [a[None] for a in res]
        return res[0]

    def update_beside(n, n_tiles, step_of):
        return _adamw_side(arrived[n], big[n][0], big_m[n][0], big_v[n][0], n_tiles, step_of)

    def keep(n, res):
        big_out[n] = [a[None] for a in res]

    dgu1, a1 = _ffn_bwd_act("ffn1_bwd_act", dx1_bf, wd1, gu1)
    wait_group("a", group_a, ring_a, a1)
    g["w_down1"], *res = _ffn_bwd_dwd("ffn1_bwd_dwd", a1, dx1_bf,
                                       side=update_beside("w_down2", 11, lambda i, j, k: i * 4 + j))
    keep("w_down2", res)
    pend_d = pair_start("d", group_d, g)
    g["w_gu1_lo"], *res = _ffn_bwd_dwgu("ffn1_bwd_dwgu_lo", h1, dgu1, deps=(pend_d[4],), rows=(0, half),
                                         side=update_beside("w_gu2", 16, lambda i, j, k: i * 2 + j))
    keep("w_gu2", res)
    ring_d = chip_start("d", group_d, pend_d, g["w_gu1_lo"])
    pend_e = pair_start("e", group_e, g, deps=(ring_d[4],))
    wait_group("c", group_c, ring_c, pend_e[4])
    g["w_gu1_hi"], *res = _ffn_bwd_dwgu("ffn1_bwd_dwgu_hi", h1, dgu1, rows=(half, half),
                                         side=update_beside("w_in", 16, lambda i, j, k: i * 2 + j))
    keep("w_in", res)
    ring_e = chip_start("e", group_e, pend_e, g["w_gu1_hi"])
    pend_f = pair_start("f", group_f, g, deps=(ring_e[4],))
    wait_group("b", group_b, ring_b, pend_f[4])
    after = pend_f[4]
    for n in group_b:
        after = update(n, after)
    ring_f = chip_start("f", group_f, pend_f, after)
    wait_group("d", group_d, ring_d, ring_f[4])
    dh1, *res = _ffn_bwd_dh("ffn1_bwd_dh", dgu1, wts["w_gu1"],
                             side=update_beside("w_down1", 11, lambda i, j, k: i * 4 + k))
    keep("w_down1", res)
    grad_x, dg_ffn1 = _rms_bwd("ffn1_bwd_rms", xs, g_ffn1, dh1, dx1)
    after = grad_x
    n = "w_gu1"
    wait_group("e", group_e, ring_e, after)
    res = _adamw_chips("adamw_w_gu1_lo", arrived["w_gu1_lo"], big[n][0], big_m[n][0], big_v[n][0], row_tile(n), deps=(after,))
    wait_group("f", group_f, ring_f, res[0])
    res = _adamw_chips("adamw_w_gu1_hi", arrived["w_gu1_hi"], big[n][0], big_m[n][0], big_v[n][0], row_tile(n),
                       row0=half, into=res)
    keep(n, res)
    after = res[0]

    small = {"g_ffn1": dg_ffn1[0:1], "g_mix": dg_mix[0:1], "g_ffn2": dg_ffn2[0:1],
             "q_norm_g": dgq[0:1, :HEAD_DIM], "k_norm_g": dgk[0:1, :HEAD_DIM], "sinks": dsink[:, 0][None],
             "conv_w": dconv_w[0:CONV_K].reshape(1, -1)}
    small_w = {"g_ffn1": g_ffn1, "g_mix": g_mix, "g_ffn2": g_ffn2, "q_norm_g": q_norm_g, "k_norm_g": k_norm_g,
               "sinks": sinks, "conv_w": None}
    small_m = {"g_ffn1": m_g_ffn1, "g_mix": m_g_mix, "g_ffn2": m_g_ffn2, "q_norm_g": m_q_norm_g,
               "k_norm_g": m_k_norm_g, "sinks": m_sinks, "conv_w": m_conv_w}
    small_v = {"g_ffn1": v_g_ffn1, "g_mix": v_g_mix, "g_ffn2": v_g_ffn2, "q_norm_g": v_q_norm_g,
               "k_norm_g": v_k_norm_g, "sinks": v_sinks, "conv_w": v_conv_w}
    snames = list(small)
    widths = [small[n].shape[1] for n in snames]
    slots = [-(-w // LANES) * LANES for w in widths]
    spans = [(sum(slots[:i]), w) for i, w in enumerate(widths)]

    def pack(vals):
        return jnp.concatenate([jnp.pad(v.reshape(1, -1), ((0, 0), (0, s - v.size))) for v, s in zip(vals, slots)], axis=1)

    csh = cw // N_DEV

    def place_conv(local, fill):
        full = jnp.full((CONV_K, cw), fill, F32)
        return lax.dynamic_update_slice(full, local, (0, me * csh)).reshape(1, -1)

    pw = pack([small_w[n] if n != "conv_w" else place_conv(conv_w[0], 0.0) for n in snames])
    pm = pack([small_m[n] if n != "conv_w" else place_conv(m_conv_w[0], 0.0) for n in snames])
    pv = pack([small_v[n] if n != "conv_w" else place_conv(v_conv_w[0], 1.0) for n in snames])
    parts = _all_gather_small("gather_small_grads", pack([small[n] for n in snames]), deps=(after,))
    small_out = _adamw_small(parts, pw, pm, pv, spans)

    def unpack(idx, n):
        piece = small_out[idx * len(snames) + snames.index(n)]
        if n == "conv_w":
            piece = lax.dynamic_slice(piece.reshape(CONV_K, cw), (0, me * csh), (CONV_K, csh))[None]
        return piece

    order = ["g_ffn1", "w_gu1", "w_down1", "g_mix", "w_in", "conv_w", "q_norm_g", "k_norm_g", "sinks",
             "w_out_conv", "w_out_attn", "w_o", "g_ffn2", "w_gu2", "w_down2"]
    outs = [loss, grad_x[None]]
    for idx in range(4):
        for n in order:
            outs.append(big_out[n][idx] if n in big_out else unpack(idx, n))
    return tuple(outs)
```

```python
import jax
import jax.numpy as jnp
from jax import lax
from jax.experimental import pallas as pl
from jax.experimental.pallas import tpu as pltpu

F32 = jnp.float32
BF16 = jnp.bfloat16

N_DEV = 8
HEAD_DIM = 64
GROUP = 4
BLOCK = 128
ROT_DIM = 16
ROPE_THETA = 500000.0
RMS_EPS = 1e-6
NEG_INF = -1e30
ATTN_SCALE = HEAD_DIM ** -0.5
CONV_K = 3
LANES = 128
MXU_COLS = 256
VMEM_BYTES_V7X = 64 * 1024 * 1024
VMEM_CAP = VMEM_BYTES_V7X - 6 * 1024 * 1024

ADAM_LR = 0.001
ADAM_B1 = 0.9
ADAM_B2 = 0.999
ADAM_EPS = 1e-08
ADAM_WD = 0.01
ADAM_STEP = 10

NN = (((1,), (0,)), ((), ()))
NT = (((1,), (1,)), ((), ()))
TN = (((0,), (0,)), ((), ()))

MESH = pl.DeviceIdType.MESH


def _nbytes(shape, dtype):
    n = 1
    for s in shape:
        if s is not None:
            n *= s
    return n * jnp.dtype(dtype).itemsize


def _params(semantics, block_bytes, temp_bytes):
    assert 2 * block_bytes + temp_bytes <= VMEM_CAP, (block_bytes, temp_bytes)
    return pltpu.CompilerParams(dimension_semantics=semantics, vmem_limit_bytes=VMEM_CAP)


def _fused(name, grid, ins, outs, dots, epilogue, *, nk=1, acc_shape=None, temp_bytes=0,
           semantics=("parallel", "parallel", "arbitrary"), deps=(), side=None):
    n_main_in, n_main_out = len(ins), len(outs)
    if side is not None:
        ins, outs = list(ins) + list(side[0]), list(outs) + list(side[1])
    n_in, n_out = len(ins), len(outs)
    n_dep = len(deps)

    def body(*refs):
        in_refs, out_refs = refs[:n_in], refs[n_in + n_dep:n_in + n_dep + n_out]
        scratch = refs[n_in + n_dep + n_out:]
        if side is not None:
            side[2](in_refs[n_main_in:], out_refs[n_main_out:])

        def products():
            if callable(dots):
                return dots(in_refs)
            total = None
            for ai, bi, contract in dots:
                a, b = in_refs[ai][...], in_refs[bi][...]
                a = a if a.dtype == BF16 else a.astype(BF16)
                b = b if b.dtype == BF16 else b.astype(BF16)
                p = lax.dot_general(a, b, contract, preferred_element_type=F32)
                total = p if total is None else total + p
            return total

        if nk == 1:
            epilogue(products() if dots else None, in_refs, out_refs)
        else:
            acc = scratch[0]
            k = pl.program_id(2)

            @pl.when(k == 0)
            def _():
                acc[...] = jnp.zeros_like(acc)

            acc[...] += products()

            @pl.when(k == nk - 1)
            def _():
                epilogue(acc[...], in_refs, out_refs)

    block_bytes = sum(_nbytes(spec.block_shape, a.dtype) for a, spec in ins)
    block_bytes += sum(_nbytes(spec.block_shape, s.dtype) for s, spec in outs)
    scratch_shapes = []
    if nk > 1:
        scratch_shapes.append(pltpu.VMEM(acc_shape, F32))
        temp_bytes += _nbytes(acc_shape, F32)
    res = pl.pallas_call(
        body, name=name, grid=grid,
        in_specs=[spec for _, spec in ins] + [pl.BlockSpec(memory_space=pl.ANY)] * n_dep,
        out_specs=[spec for _, spec in outs],
        out_shape=[s for s, _ in outs],
        scratch_shapes=scratch_shapes,
        compiler_params=_params(semantics, block_bytes, temp_bytes),
    )(*[a for a, _ in ins], *deps)
    return res


def _sds(shape, dtype):
    return jax.ShapeDtypeStruct(shape, dtype)


def _sigmoid(x):
    return jax.nn.sigmoid(x)


def _all_gather_small(name, shard, deps=()):
    n_dep = len(deps)

    def body(src, *rest):
        dst, send_sems, recv_sems, local_sem = rest[n_dep:]
        x, y, c = lax.axis_index("x"), lax.axis_index("y"), lax.axis_index("c")
        me = 4 * x + 2 * y + c
        copies = [pltpu.make_async_copy(src, dst.at[me], local_sem)]
        for k in range(1, N_DEV):
            peer = ((1 - x) if (k & 4) else x, (1 - y) if (k & 2) else y, (1 - c) if (k & 1) else c)
            copies.append(pltpu.make_async_remote_copy(
                src_ref=src, dst_ref=dst.at[me], send_sem=send_sems.at[k - 1], recv_sem=recv_sems.at[k - 1],
                device_id=peer, device_id_type=MESH))
        for cp in copies:
            cp.start()
        for cp in copies:
            cp.wait()

    hbm = pl.BlockSpec(memory_space=pltpu.HBM)
    return pl.pallas_call(
        body, name=name,
        in_specs=[hbm] + [pl.BlockSpec(memory_space=pl.ANY)] * n_dep, out_specs=hbm,
        out_shape=_sds((N_DEV,) + shard.shape, shard.dtype),
        scratch_shapes=[pltpu.SemaphoreType.DMA((N_DEV - 1,)), pltpu.SemaphoreType.DMA((N_DEV - 1,)),
                        pltpu.SemaphoreType.DMA],
    )(shard, *deps)


_HBM = pl.BlockSpec(memory_space=pltpu.HBM)
_SEM = pl.BlockSpec(memory_space=pltpu.SEMAPHORE)
_ANY = pl.BlockSpec(memory_space=pl.ANY)
_EFFECT = pltpu.SideEffectType.DATAFLOW_SIDE_EFFECTING
N_TARGETS = 4


def _mesh_pos():
    return lax.axis_index("x"), lax.axis_index("y"), lax.axis_index("c")


def _chip_peers(x, y, c):
    return [(1 - x, y, c), (x, 1 - y, c), (1 - x, 1 - y, c)]


def _dev_index(pos):
    return 4 * pos[0] + 2 * pos[1] + pos[2]


def _hbm_like(a):
    return pltpu.HBM(a.shape, a.dtype)


def _place_shard(name, w, out_dtype, me, tr, deps=()):
    r, c = w.shape
    n_dep = len(deps)

    def body(me_ref, w_ref, *rest):
        rest[n_dep][...] = w_ref[...].astype(out_dtype)

    grid_spec = pltpu.PrefetchScalarGridSpec(
        num_scalar_prefetch=1, grid=(r // tr,),
        in_specs=[pl.BlockSpec((tr, c), lambda i, me_ref: (i, 0))] + [_ANY] * n_dep,
        out_specs=pl.BlockSpec((None, tr, c), lambda i, me_ref: (me_ref[0], i, 0)))
    return pl.pallas_call(
        body, name=name, grid_spec=grid_spec, out_shape=_sds((N_DEV, r, c), out_dtype),
        compiler_params=_params(("parallel",), tr * c * 6, tr * c * 4),
    )(me, w, *deps)


def _gather_start(name, lands):
    n = len(lands)

    def body(*refs):
        bufs = refs[:n]
        send, recv = refs[n], refs[n + 1]
        token = refs[-1]
        x, y, c = _mesh_pos()
        me = _dev_index((x, y, c))
        targets = [(x, y, 1 - c)] + _chip_peers(x, y, c)
        for w in range(n):
            for k, to in enumerate(targets):
                pltpu.make_async_remote_copy(
                    src_ref=bufs[w].at[me], dst_ref=bufs[w].at[me],
                    send_sem=send.at[N_TARGETS * w + k], recv_sem=recv.at[N_TARGETS * w + k],
                    device_id=to, device_id_type=MESH).start()
        token[...] = jnp.zeros_like(token)

    sems = pltpu.SemaphoreType.DMA((N_TARGETS * n,))
    outs = pl.pallas_call(
        body, name=name,
        in_specs=[_HBM] * n, out_specs=[_SEM, _SEM] + [_HBM] * n + [_token_spec()],
        out_shape=[sems, sems] + [_hbm_like(a) for a in lands] + [_sds((8, LANES), F32)],
        input_output_aliases={i: 2 + i for i in range(n)},
        compiler_params=pltpu.CompilerParams(has_side_effects=_EFFECT),
    )(*lands)
    return outs[0], outs[1], list(outs[2:2 + n]), outs[-1]


def _gather_wait(name, positions, send, recv, lands, after):
    m = len(positions)

    def body(*refs):
        bufs = refs[:m]
        send_sems, recv_sems = refs[m], refs[m + 1]
        x, y, c = _mesh_pos()
        me = _dev_index((x, y, c))
        sources = [(x, y, 1 - c)] + _chip_peers(x, y, c)
        for j, w in enumerate(positions):
            for k, frm in enumerate(sources):
                cp = pltpu.make_async_remote_copy(
                    src_ref=bufs[j].at[me], dst_ref=bufs[j].at[_dev_index(frm)],
                    send_sem=send_sems.at[N_TARGETS * w + k], recv_sem=recv_sems.at[N_TARGETS * w + k],
                    device_id=frm, device_id_type=MESH)
                cp.wait_send()
                cp.wait_recv()

    outs = pl.pallas_call(
        body, name=name,
        in_specs=[_HBM] * m + [_SEM, _SEM, _ANY], out_specs=[_HBM] * m,
        out_shape=[_hbm_like(a) for a in lands],
        input_output_aliases={i: i for i in range(m)},
        compiler_params=pltpu.CompilerParams(has_side_effects=_EFFECT),
    )(*lands, send, recv, after)
    return list(outs)


def _forward_to_sibling(name, lands):
    m = len(lands)

    def body(*refs):
        copies = _forward_copies(refs[m:2 * m], refs[2 * m], refs[2 * m + 1])
        for cp in copies:
            cp.start()
        for cp in copies:
            cp.wait()

    outs = pl.pallas_call(
        body, name=name,
        in_specs=[_HBM] * m, out_specs=[_HBM] * m,
        out_shape=[_sds(a.shape, a.dtype) for a in lands],
        input_output_aliases={i: i for i in range(m)},
        scratch_shapes=[pltpu.SemaphoreType.DMA((3 * m,)), pltpu.SemaphoreType.DMA((3 * m,))],
    )(*lands)
    return list(outs)


def _forward_copies(bufs, send, recv):
    x, y, c = _mesh_pos()
    copies = []
    for j, buf in enumerate(bufs):
        for k, chip in enumerate(_chip_peers(x, y, c)):
            block = buf.at[_dev_index(chip)]
            copies.append(pltpu.make_async_remote_copy(
                src_ref=block, dst_ref=block, send_sem=send.at[3 * j + k], recv_sem=recv.at[3 * j + k],
                device_id=(x, y, 1 - c), device_id_type=MESH))
    return copies


def _forward_start(name, lands):
    m = len(lands)

    def body(*refs):
        for cp in _forward_copies(refs[:m], refs[m], refs[m + 1]):
            cp.start()

    sems = pltpu.SemaphoreType.DMA((3 * m,))
    outs = pl.pallas_call(
        body, name=name,
        in_specs=[_HBM] * m, out_specs=[_SEM, _SEM] + [_HBM] * m,
        out_shape=[sems, sems] + [_hbm_like(a) for a in lands],
        input_output_aliases={i: 2 + i for i in range(m)},
        compiler_params=pltpu.CompilerParams(has_side_effects=_EFFECT),
    )(*lands)
    return outs[0], outs[1], list(outs[2:])


def _forward_wait(name, send, recv, lands, after):
    m = len(lands)

    def body(*refs):
        for cp in _forward_copies(refs[:m], refs[m], refs[m + 1]):
            cp.wait_send()
            cp.wait_recv()

    outs = pl.pallas_call(
        body, name=name,
        in_specs=[_HBM] * m + [_SEM, _SEM, _ANY], out_specs=[_HBM] * m,
        out_shape=[_hbm_like(a) for a in lands],
        input_output_aliases={i: i for i in range(m)},
        compiler_params=pltpu.CompilerParams(has_side_effects=_EFFECT),
    )(*lands, send, recv, after)
    return list(outs)


def _token_spec():
    return pl.BlockSpec(memory_space=pltpu.VMEM)


def _pair_start(name, stacks, lands, deps=()):
    n = len(stacks)
    n_dep = len(deps)

    def body(*refs):
        srcs, dsts = refs[:n], refs[n:2 * n]
        send, recv = refs[2 * n + n_dep], refs[2 * n + n_dep + 1]
        token = refs[-1]
        x, y, c = _mesh_pos()
        for w in range(n):
            for chip in range(4):
                pltpu.make_async_remote_copy(
                    src_ref=srcs[w].at[chip, 1 - c], dst_ref=dsts[w].at[chip],
                    send_sem=send.at[4 * w + chip], recv_sem=recv.at[4 * w + chip],
                    device_id=(x, y, 1 - c), device_id_type=MESH).start()
        token[...] = jnp.zeros_like(token)

    sems = pltpu.SemaphoreType.DMA((4 * n,))
    outs = pl.pallas_call(
        body, name=name,
        in_specs=[_HBM] * (2 * n) + [_ANY] * n_dep, out_specs=[_SEM, _SEM] + [_HBM] * (2 * n) + [_token_spec()],
        out_shape=[sems, sems] + [_hbm_like(a) for a in stacks] + [_hbm_like(a) for a in lands] + [_sds((8, LANES), F32)],
        input_output_aliases={i: 2 + i for i in range(2 * n)},
        compiler_params=pltpu.CompilerParams(has_side_effects=_EFFECT),
    )(*stacks, *lands, *deps)
    return outs[0], outs[1], list(outs[2:2 + n]), list(outs[2 + n:2 + 2 * n]), outs[-1]


def _pair_wait(name, send, recv, stacks, lands, after):
    n = len(stacks)

    def body(*refs):
        srcs, dsts = refs[:n], refs[n:2 * n]
        send_sems, recv_sems = refs[2 * n], refs[2 * n + 1]
        x, y, c = _mesh_pos()
        for w in range(n):
            for chip in range(4):
                cp = pltpu.make_async_remote_copy(
                    src_ref=srcs[w].at[chip, 1 - c], dst_ref=dsts[w].at[chip],
                    send_sem=send_sems.at[4 * w + chip], recv_sem=recv_sems.at[4 * w + chip],
                    device_id=(x, y, 1 - c), device_id_type=MESH)
                cp.wait_send()
                cp.wait_recv()

    outs = pl.pallas_call(
        body, name=name,
        in_specs=[_HBM] * (2 * n) + [_SEM, _SEM, _ANY], out_specs=[_HBM] * (2 * n),
        out_shape=[_hbm_like(a) for a in stacks] + [_hbm_like(a) for a in lands],
        input_output_aliases={i: i for i in range(2 * n)},
        compiler_params=pltpu.CompilerParams(has_side_effects=_EFFECT),
    )(*stacks, *lands, send, recv, after)
    return list(outs[:n]), list(outs[n:])


def _pair_add(name, stack, land, place, tr):
    _, _, r, c = stack.shape

    def body(place_ref, a_ref, b_ref, sums_ref, slots_ref):
        total = (a_ref[...].astype(F32) + b_ref[...].astype(F32)).astype(BF16)
        sums_ref[...] = total

        @pl.when(pl.program_id(1) == place_ref[1])
        def _():
            slots_ref[...] = total

    grid_spec = pltpu.PrefetchScalarGridSpec(
        num_scalar_prefetch=1, grid=(r // tr, 4),
        in_specs=[pl.BlockSpec((None, None, tr, c), lambda i, k, place_ref: (k, place_ref[0], i, 0)),
                  pl.BlockSpec((None, tr, c), lambda i, k, place_ref: (k, i, 0))],
        out_specs=[pl.BlockSpec((None, tr, c), lambda i, k, place_ref: (k, i, 0)),
                   pl.BlockSpec((None, tr, c), lambda i, k, place_ref: (place_ref[1], i, 0))])
    return pl.pallas_call(
        body, name=name, grid_spec=grid_spec, out_shape=[_sds((4, r, c), BF16)] * 2,
        compiler_params=_params(("parallel", "arbitrary"), 4 * tr * c * 2, 3 * tr * c * 4),
    )(place, stack, land)


def _sibling_copies(srcs, dsts, send, recv):
    x, y, c = _mesh_pos()
    return [pltpu.make_async_remote_copy(
        src_ref=srcs[w], dst_ref=dsts[w], send_sem=send.at[w], recv_sem=recv.at[w],
        device_id=(x, y, 1 - c), device_id_type=MESH) for w in range(len(srcs))]


def _sibling_start(name, srcs):
    n = len(srcs)
    lands = [lax.empty(a.shape, a.dtype) for a in srcs]

    def body(*refs):
        for cp in _sibling_copies(refs[:n], refs[n:2 * n], refs[2 * n], refs[2 * n + 1]):
            cp.start()
        token = refs[-1]
        token[...] = jnp.zeros_like(token)

    sems = pltpu.SemaphoreType.DMA((n,))
    outs = pl.pallas_call(
        body, name=name,
        in_specs=[_HBM] * (2 * n), out_specs=[_SEM, _SEM] + [_HBM] * (2 * n) + [_token_spec()],
        out_shape=[sems, sems] + [_hbm_like(a) for a in srcs] + [_hbm_like(a) for a in lands] + [_sds((8, LANES), F32)],
        input_output_aliases={i: 2 + i for i in range(2 * n)},
        compiler_params=pltpu.CompilerParams(has_side_effects=_EFFECT),
    )(*srcs, *lands)
    return outs[0], outs[1], list(outs[2:2 + n]), list(outs[2 + n:2 + 2 * n]), outs[-1]


def _sibling_wait(name, send, recv, srcs, lands, after):
    n = len(srcs)

    def body(*refs):
        for cp in _sibling_copies(refs[:n], refs[n:2 * n], refs[2 * n], refs[2 * n + 1]):
            cp.wait_send()
            cp.wait_recv()

    outs = pl.pallas_call(
        body, name=name,
        in_specs=[_HBM] * (2 * n) + [_SEM, _SEM, _ANY], out_specs=[_HBM] * (2 * n),
        out_shape=[_hbm_like(a) for a in srcs] + [_hbm_like(a) for a in lands],
        input_output_aliases={i: i for i in range(2 * n)},
        compiler_params=pltpu.CompilerParams(has_side_effects=_EFFECT),
    )(*srcs, *lands, send, recv, after)
    return list(outs[:n]), list(outs[n:])


def _chip_start(name, parts, lands):
    n = len(parts)

    def body(*refs):
        srcs, dsts = refs[:n], refs[n:2 * n]
        send, recv = refs[2 * n], refs[2 * n + 1]
        token = refs[-1]
        x, y, c = _mesh_pos()
        for w in range(n):
            for k, to in enumerate(_chip_peers(x, y, c)):
                pltpu.make_async_remote_copy(
                    src_ref=srcs[w].at[2 * to[0] + to[1]], dst_ref=dsts[w].at[2 * x + y],
                    send_sem=send.at[3 * w + k], recv_sem=recv.at[3 * w + k],
                    device_id=to, device_id_type=MESH).start()
        token[...] = jnp.zeros_like(token)

    sems = pltpu.SemaphoreType.DMA((3 * n,))
    outs = pl.pallas_call(
        body, name=name,
        in_specs=[_HBM] * (2 * n), out_specs=[_SEM, _SEM] + [_HBM] * (2 * n) + [_token_spec()],
        out_shape=[sems, sems] + [_hbm_like(a) for a in parts] + [_hbm_like(a) for a in lands] + [_sds((8, LANES), F32)],
        input_output_aliases={i: 2 + i for i in range(2 * n)},
        compiler_params=pltpu.CompilerParams(has_side_effects=_EFFECT),
    )(*parts, *lands)
    return outs[0], outs[1], list(outs[2:2 + n]), list(outs[2 + n:2 + 2 * n]), outs[-1]


def _chip_wait(name, send, recv, parts, lands, after):
    n = len(parts)

    def body(*refs):
        srcs, dsts = refs[:n], refs[n:2 * n]
        send_sems, recv_sems = refs[2 * n], refs[2 * n + 1]
        x, y, c = _mesh_pos()
        for w in range(n):
            for k, frm in enumerate(_chip_peers(x, y, c)):
                chip = 2 * frm[0] + frm[1]
                cp = pltpu.make_async_remote_copy(
                    src_ref=srcs[w].at[chip], dst_ref=dsts[w].at[chip],
                    send_sem=send_sems.at[3 * w + k], recv_sem=recv_sems.at[3 * w + k],
                    device_id=frm, device_id_type=MESH)
                cp.wait_send()
                cp.wait_recv()

    outs = pl.pallas_call(
        body, name=name,
        in_specs=[_HBM] * (2 * n) + [_SEM, _SEM, _ANY], out_specs=[_HBM] * (2 * n),
        out_shape=[_hbm_like(a) for a in parts] + [_hbm_like(a) for a in lands],
        input_output_aliases={i: i for i in range(2 * n)},
        compiler_params=pltpu.CompilerParams(has_side_effects=_EFFECT),
    )(*parts, *lands, send, recv, after)
    return list(outs[:n]), list(outs[n:])


def _row_tile(t):
    return min(t, 256)


def _rms_fwd(name, x, g):
    t, d = x.shape
    tm = _row_tile(t)

    def epilogue(_, ins, outs):
        xv = ins[0][...]
        r = lax.rsqrt(jnp.mean(xv * xv, axis=-1, keepdims=True) + RMS_EPS)
        outs[0][...] = (xv * r * ins[1][...]).astype(BF16)

    row = pl.BlockSpec((tm, d), lambda i, j, k: (i, 0))
    vec = pl.BlockSpec((1, d), lambda i, j, k: (0, 0))
    return _fused(name, (t // tm, 1, 1), [(x, row), (g, vec)], [(_sds((t, d), BF16), row)], [], epilogue,
                  temp_bytes=4 * tm * d * 4)[0]


def _rms_bwd(name, x, g, dh, resid, deps=(), with_bf16=False):
    t, d = x.shape
    tm = _row_tile(t)

    def epilogue(_, ins, outs):
        xv, gv, dhv = ins[0][...], ins[1][...], ins[2][...]
        r = lax.rsqrt(jnp.mean(xv * xv, axis=-1, keepdims=True) + RMS_EPS)
        xh = xv * r
        u = dhv * gv
        dot = jnp.mean(u * xh, axis=-1, keepdims=True)
        dx = ins[3][...] + r * (u - xh * dot)
        outs[0][...] = dx
        if with_bf16:
            outs[2][...] = dx.astype(BF16)

        @pl.when(pl.program_id(0) == 0)
        def _():
            outs[1][...] = jnp.zeros_like(outs[1])

        outs[1][0:1, :] += jnp.sum(dhv * xh, axis=0, keepdims=True)

    row = pl.BlockSpec((tm, d), lambda i, j, k: (i, 0))
    vec = pl.BlockSpec((1, d), lambda i, j, k: (0, 0))
    acc = pl.BlockSpec((8, d), lambda i, j, k: (0, 0))
    outs = [(_sds((t, d), F32), row), (_sds((8, d), F32), acc)] + ([(_sds((t, d), BF16), row)] if with_bf16 else [])
    return _fused(name, (t // tm, 1, 1), [(x, row), (g, vec), (dh, row), (resid, row)], outs, [], epilogue,
                  temp_bytes=6 * tm * d * 4, semantics=("arbitrary", "arbitrary", "arbitrary"), deps=deps)


def _ffn_up(name, h, wgu, parity=None, into=None):
    t, d = h.shape
    nb = wgu.shape[2]
    f = 4 * nb
    tm = min(t, 512)

    def body(h_ref, wg_ref, wu_ref, gu_ref, a_ref):
        hv = h_ref[...]
        for c0 in range(0, nb, MXU_COLS):
            cs = slice(c0, min(c0 + MXU_COLS, nb))
            g = jnp.dot(hv, wg_ref[:, cs], preferred_element_type=F32)
            u = jnp.dot(hv, wu_ref[:, cs], preferred_element_type=F32)
            gu_ref[0, :, cs] = g.astype(BF16)
            gu_ref[1, :, cs] = u.astype(BF16)
            a_ref[:, cs] = (g * _sigmoid(g) * u).astype(BF16)

    blocks = tm * d * 2 + 2 * d * nb * 2 + 3 * tm * nb * 2
    params = _params(("parallel", "parallel"), blocks, 8 * tm * MXU_COLS * 4)
    out_shape = [_sds((2, t, f), BF16), _sds((t, f), BF16)]
    if parity is None:
        return pl.pallas_call(
            body, name=name, grid=(4, t // tm),
            in_specs=[pl.BlockSpec((tm, d), lambda j, i: (i, 0)),
                      pl.BlockSpec((None, d, nb), lambda j, i: (j, 0, 0)),
                      pl.BlockSpec((None, d, nb), lambda j, i: (j + 4, 0, 0))],
            out_specs=[pl.BlockSpec((2, tm, nb), lambda j, i: (0, i, j)),
                       pl.BlockSpec((tm, nb), lambda j, i: (i, j))],
            out_shape=out_shape, compiler_params=params,
        )(h, wgu, wgu)

    def half_body(parity_ref, h_ref, wg_ref, wu_ref, *rest):
        body(h_ref, wg_ref, wu_ref, rest[-2], rest[-1])

    n_pass = 0 if into is None else 2
    grid_spec = pltpu.PrefetchScalarGridSpec(
        num_scalar_prefetch=1, grid=(2, t // tm),
        in_specs=[pl.BlockSpec((tm, d), lambda jj, i, p: (i, 0)),
                  pl.BlockSpec((None, d, nb), lambda jj, i, p: (2 * jj + p[0], 0, 0)),
                  pl.BlockSpec((None, d, nb), lambda jj, i, p: (2 * jj + p[0] + 4, 0, 0))] + [_ANY] * n_pass,
        out_specs=[pl.BlockSpec((2, tm, nb), lambda jj, i, p: (0, i, 2 * jj + p[0])),
                   pl.BlockSpec((tm, nb), lambda jj, i, p: (i, 2 * jj + p[0]))])
    return pl.pallas_call(
        half_body, name=name, grid_spec=grid_spec, out_shape=out_shape,
        input_output_aliases={} if into is None else {4: 0, 5: 1}, compiler_params=params,
    )(parity, h, wgu, wgu, *(into or ()))


def _ffn_down(name, a, wd, x, target=None):
    t, f = a.shape
    d = wd.shape[1]
    tm = min(t, 512)
    tn = min(d, 1024)
    blk = pl.BlockSpec((tm, tn), lambda j, i, k: (i, j))
    ins = [(a, pl.BlockSpec((tm, f), lambda j, i, k: (i, 0))), (wd, pl.BlockSpec((f, tn), lambda j, i, k: (0, j))), (x, blk)]

    if target is None:
        def epilogue(acc, ins, outs):
            outs[0][...] = ins[2][...] + 0.5 * acc

        return _fused(name, (d // tn, t // tm, 1), ins, [(_sds((t, d), F32), blk)],
                      [(0, 1, NN)], epilogue, temp_bytes=2 * tm * tn * 4)[0]

    def epilogue(acc, ins, outs):
        e = ins[2][...] + 0.5 * acc - ins[3][...]
        outs[0][...] = e * (1.0 / d)
        outs[2][...] = (e * (1.0 / d)).astype(BF16)

        @pl.when((pl.program_id(0) == 0) & (pl.program_id(1) == 0))
        def _():
            outs[1][...] = jnp.zeros_like(outs[1])

        part = jnp.sum(jnp.sum(e * e, axis=1, keepdims=True), axis=0, keepdims=True)
        outs[1][...] += jnp.broadcast_to(part, outs[1].shape)

    return _fused(name, (d // tn, t // tm, 1), ins + [(target, blk)],
                  [(_sds((t, d), F32), blk), (_sds((8, LANES), F32), pl.BlockSpec((8, LANES), lambda j, i, k: (0, 0))),
                   (_sds((t, d), BF16), blk)],
                  [(0, 1, NN)], epilogue, temp_bytes=3 * tm * tn * 4,
                  semantics=("arbitrary", "arbitrary", "arbitrary"))


def _ffn_bwd_act(name, dy, wd, gu, deps=()):
    t, d = dy.shape
    f = wd.shape[0]
    nb = f // 4
    tm = min(t, 512)

    def body(dy_ref, wd_ref, gu_ref, *rest):
        dgu_ref, a_ref = rest[-2], rest[-1]
        dyv = dy_ref[...].astype(BF16)
        for c0 in range(0, nb, MXU_COLS):
            cs = slice(c0, min(c0 + MXU_COLS, nb))
            da = 0.5 * lax.dot_general(dyv, wd_ref[cs, :], NT, preferred_element_type=F32)
            g = gu_ref[0, :, cs].astype(F32)
            u = gu_ref[1, :, cs].astype(F32)
            s = _sigmoid(g)
            silu = g * s
            dgu_ref[0, :, cs] = (da * u * (s * (1.0 + g * (1.0 - s)))).astype(BF16)
            dgu_ref[1, :, cs] = (da * silu).astype(BF16)
            a_ref[:, cs] = (silu * u).astype(BF16)

    blocks = tm * d * 4 + nb * d * 2 + 5 * tm * nb * 2
    return pl.pallas_call(
        body, name=name, grid=(4, t // tm),
        in_specs=[pl.BlockSpec((tm, d), lambda j, i: (i, 0)),
                  pl.BlockSpec((nb, d), lambda j, i: (j, 0)),
                  pl.BlockSpec((2, tm, nb), lambda j, i: (0, i, j))] + [_ANY] * len(deps),
        out_specs=[pl.BlockSpec((2, tm, nb), lambda j, i: (0, i, j)), pl.BlockSpec((tm, nb), lambda j, i: (i, j))],
        out_shape=[_sds((2, t, f), BF16), _sds((t, f), BF16)],
        compiler_params=_params(("parallel", "parallel"), blocks, tm * d * 2 + 8 * tm * MXU_COLS * 4),
    )(dy, wd, gu, *deps)


def _ffn_bwd_dwd(name, a, dy, deps=(), side=None):
    t, f = a.shape
    d = dy.shape[1]
    tm = f // 4
    tn = min(d, 512)

    def epilogue(acc, ins, outs):
        outs[0][...] = (0.5 * acc).astype(BF16)

    return _fused(name, (4, d // tn, 1),
                  [(a, pl.BlockSpec((t, tm), lambda i, j, k: (0, i))),
                   (dy, pl.BlockSpec((t, tn), lambda i, j, k: (0, j)))],
                  [(_sds((f, d), BF16), pl.BlockSpec((tm, tn), lambda i, j, k: (i, j)))],
                  [(0, 1, TN)], epilogue, temp_bytes=t * tn * 2 + 2 * tm * tn * 4, deps=deps, side=side)


def _ffn_bwd_dh(name, dgu, wgu, deps=(), side=None):
    _, t, f = dgu.shape
    d, nb = wgu.shape[1], wgu.shape[2]
    tm = min(t, 512)

    def products(ins):
        return (lax.dot_general(ins[0][:, 0:nb], ins[1][0], NT, preferred_element_type=F32)
                + lax.dot_general(ins[0][:, nb:2 * nb], ins[1][1], NT, preferred_element_type=F32))

    def epilogue(acc, ins, outs):
        outs[0][...] = acc

    return _fused(name, (t // tm, 1, 4),
                  [(dgu, pl.BlockSpec((None, tm, 2 * nb), lambda i, j, k: (k // 2, i, k % 2))),
                   (wgu, pl.BlockSpec((2, d, nb), lambda i, j, k: (k, 0, 0)))],
                  [(_sds((t, d), F32), pl.BlockSpec((tm, d), lambda i, j, k: (i, 0)))],
                  products, epilogue, nk=4, acc_shape=(tm, d), temp_bytes=tm * d * 4, deps=deps, side=side)


def _ffn_bwd_dwgu(name, h, dgu, deps=(), side=None, rows=None):
    t, d = h.shape
    nb = dgu.shape[2] // 4
    tm = min(d, 512)
    row0, nrows = rows if rows is not None else (0, d)
    j0 = row0 // tm

    def epilogue(acc, ins, outs):
        outs[0][...] = acc.astype(BF16)

    return _fused(name, (N_DEV, nrows // tm, 1),
                  [(h, pl.BlockSpec((t, tm), lambda i, j, k: (0, j0 + j))),
                   (dgu, pl.BlockSpec((None, t, nb), lambda i, j, k: (i // 4, 0, i % 4)))],
                  [(_sds((N_DEV, nrows, nb), BF16), pl.BlockSpec((None, tm, nb), lambda i, j, k: (i, j, 0)))],
                  [(0, 1, TN)], epilogue, temp_bytes=2 * tm * nb * 4, deps=deps, side=side)


def _ffn_bwd_dwgu_pair(name, h, h_sib, dgu, dgu_sib, place):
    t, d = h.shape
    nb = dgu.shape[2] // 4
    tm = min(d, 512)

    def body(place_ref, h_ref, hs_ref, g_ref, gs_ref, sums_ref, slots_ref):
        acc = lax.dot_general(h_ref[...], g_ref[...], TN, preferred_element_type=F32)
        acc += lax.dot_general(hs_ref[...], gs_ref[...], TN, preferred_element_type=F32)
        total = acc.astype(BF16)
        sums_ref[...] = total

        @pl.when(pl.program_id(1) == place_ref[1])
        def _():
            slots_ref[...] = total

    def act_map(i, k, place_ref):
        return (0, i)

    def grad_map(i, k, place_ref):
        dev = 2 * k + place_ref[0]
        return (dev // 4, 0, dev % 4)

    grid_spec = pltpu.PrefetchScalarGridSpec(
        num_scalar_prefetch=1, grid=(d // tm, 4),
        in_specs=[pl.BlockSpec((t, tm), act_map), pl.BlockSpec((t, tm), act_map),
                  pl.BlockSpec((None, t, nb), grad_map), pl.BlockSpec((None, t, nb), grad_map)],
        out_specs=[pl.BlockSpec((None, tm, nb), lambda i, k, place_ref: (k, i, 0)),
                   pl.BlockSpec((None, tm, nb), lambda i, k, place_ref: (place_ref[1], i, 0))])
    blocks = 2 * t * tm * 2 + 2 * t * nb * 2 + 2 * tm * nb * 2
    return pl.pallas_call(
        body, name=name, grid_spec=grid_spec, out_shape=[_sds((4, d, nb), BF16)] * 2,
        compiler_params=_params(("parallel", "arbitrary"), blocks, 2 * tm * nb * 4),
    )(place, h, h_sib, dgu, dgu_sib)


def _proj(h, w_in):
    t, d = h.shape
    nb = w_in.shape[3]
    tm = min(t, 512)

    def body(h_ref, w_ref, o_ref):
        hv = h_ref[...]
        o_ref[:, 0:nb] = jnp.dot(hv, w_ref[0], preferred_element_type=F32).astype(BF16)
        o_ref[:, nb:2 * nb] = jnp.dot(hv, w_ref[1], preferred_element_type=F32).astype(BF16)

    blocks = tm * d * 2 + 2 * d * nb * 2 + tm * 2 * nb * 4
    return pl.pallas_call(
        body, name="mix_proj", grid=(4, t // tm),
        in_specs=[pl.BlockSpec((tm, d), lambda j, i: (i, 0)),
                  pl.BlockSpec((None, 2, d, nb), lambda j, i: (j, 0, 0, 0))],
        out_specs=pl.BlockSpec((tm, 2 * nb), lambda j, i: (i, j)),
        out_shape=_sds((t, N_DEV * nb), BF16),
        compiler_params=_params(("parallel", "parallel"), blocks, 2 * tm * nb * 4),
    )(h, w_in)


def _shift_rows(u, k):
    t = u.shape[0]
    rolled = pltpu.roll(u, k % t, axis=0)
    row = lax.broadcasted_iota(jnp.int32, u.shape, 0)
    keep = (row >= k) if k > 0 else (row < t + k)
    return jnp.where(keep, rolled, 0.0)


def _conv_fwd(proj, conv_w):
    t = proj.shape[0]
    cw = conv_w.shape[1]
    tc = min(cw, 256)
    nc = cw // tc

    def epilogue(_, ins, outs):
        u = ins[2][...].astype(F32) * ins[0][...].astype(F32)
        w = ins[3][...]
        y = u * w[2:3, :] + _shift_rows(u, 1) * w[1:2, :] + _shift_rows(u, 2) * w[0:1, :]
        outs[0][...] = (ins[1][...].astype(F32) * y).astype(BF16)

    def col(seg):
        return pl.BlockSpec((t, tc), lambda i, j, k: (0, seg * nc + i))

    return _fused("conv_fwd", (nc, 1, 1),
                  [(proj, col(0)), (proj, col(1)), (proj, col(2)),
                   (conv_w, pl.BlockSpec((8, tc), lambda i, j, k: (0, i)))],
                  [(_sds((t, cw), BF16), pl.BlockSpec((t, tc), lambda i, j, k: (0, i)))],
                  [], epilogue, temp_bytes=6 * t * tc * 4)[0]


def _conv_bwd(proj, conv_w, dca, deps=()):
    t = proj.shape[0]
    cw = conv_w.shape[1]
    tc = min(cw, 256)
    nc = cw // tc

    def epilogue(_, ins, outs):
        xc, bg, cg = ins[0][...].astype(F32), ins[1][...].astype(F32), ins[2][...].astype(F32)
        w, dc = ins[3][...], ins[4][...]
        u = cg * xc
        u1, u2 = _shift_rows(u, 1), _shift_rows(u, 2)
        y = u * w[2:3, :] + u1 * w[1:2, :] + u2 * w[0:1, :]
        dconv = dc * bg
        du = dconv * w[2:3, :] + _shift_rows(dconv, -1) * w[1:2, :] + _shift_rows(dconv, -2) * w[0:1, :]
        outs[0][0] = (du * cg).astype(BF16)
        outs[0][1] = (dc * y).astype(BF16)
        outs[0][2] = (du * xc).astype(BF16)
        outs[1][...] = jnp.zeros_like(outs[1])
        outs[1][0:1, :] = jnp.sum(dconv * u2, axis=0, keepdims=True)
        outs[1][1:2, :] = jnp.sum(dconv * u1, axis=0, keepdims=True)
        outs[1][2:3, :] = jnp.sum(dconv * u, axis=0, keepdims=True)

    def col(seg):
        return pl.BlockSpec((t, tc), lambda i, j, k: (0, seg * nc + i))

    own = pl.BlockSpec((t, tc), lambda i, j, k: (0, i))
    wspec = pl.BlockSpec((8, tc), lambda i, j, k: (0, i))
    return _fused("conv_bwd", (nc, 1, 1),
                  [(proj, col(0)), (proj, col(1)), (proj, col(2)), (conv_w, wspec), (dca, own)],
                  [(_sds((3, t, cw), BF16), pl.BlockSpec((3, t, tc), lambda i, j, k: (0, 0, i))),
                   (_sds((8, cw), F32), wspec)],
                  [], epilogue, temp_bytes=10 * t * tc * 4, deps=deps)


def _split3(x):
    hi = x.astype(BF16)
    r1 = x - hi.astype(F32)
    mid = r1.astype(BF16)
    lo = (r1 - mid.astype(F32)).astype(BF16)
    return hi, mid, lo


def _head_selector(width):
    r = lax.broadcasted_iota(jnp.int32, (width, LANES), 0)
    c = lax.broadcasted_iota(jnp.int32, (width, LANES), 1)
    return (lax.shift_right_logical(r, 6) == c).astype(BF16)


def _head_sum(x, sel):
    return sum(jnp.dot(p, sel, preferred_element_type=F32) for p in _split3(x))


def _head_bcast(r, sel):
    return sum(lax.dot_general(p, sel, NT, preferred_element_type=F32) for p in _split3(r))


def _rope(x, c, sa, sb):
    n = x.shape[1]
    return x * c + pltpu.roll(x, n - ROT_DIM // 2, axis=1) * sa + pltpu.roll(x, ROT_DIM // 2, axis=1) * sb


def _rope_t(d, c, sa, sb):
    n = d.shape[1]
    return d * c + pltpu.roll(d * sa, ROT_DIM // 2, axis=1) + pltpu.roll(d * sb, n - ROT_DIM // 2, axis=1)


def _tile_lanes(tab, width):
    return tab if width == tab.shape[1] else jnp.tile(tab, (1, width // tab.shape[1]))


def _qk_prep(proj, gq, gk, rope_tabs, cw, kw):
    t = proj.shape[0]
    tm = _row_tile(t)

    def epilogue(_, ins, outs):
        c, sa, sb = ins[5][...], ins[6][...], ins[7][...]
        for src, gain, dst, width in ((0, 3, 0, cw), (1, 4, 1, kw)):
            xv = ins[src][...].astype(F32)
            sel = _head_selector(width)
            r = lax.rsqrt(_head_sum(xv * xv, sel) * (1.0 / HEAD_DIM) + RMS_EPS)
            xn = xv * _head_bcast(r, sel) * ins[gain][...]
            outs[dst][...] = _rope(xn, _tile_lanes(c, width), _tile_lanes(sa, width), _tile_lanes(sb, width)).astype(BF16)
        outs[2][...] = ins[2][...].astype(BF16)

    kblk = cw // kw
    tab = pl.BlockSpec((tm, LANES), lambda i, j, k: (i, 0))
    kspec = pl.BlockSpec((tm, kw), lambda i, j, k: (i, 0))
    return _fused("qk_prep", (t // tm, 1, 1),
                  [(proj, pl.BlockSpec((tm, cw), lambda i, j, k: (i, 3))),
                   (proj, pl.BlockSpec((tm, kw), lambda i, j, k: (i, 4 * kblk))),
                   (proj, pl.BlockSpec((tm, kw), lambda i, j, k: (i, 4 * kblk + 1))),
                   (gq, pl.BlockSpec((1, cw), lambda i, j, k: (0, 0))),
                   (gk, pl.BlockSpec((1, kw), lambda i, j, k: (0, 0))),
                   (rope_tabs[0], tab), (rope_tabs[1], tab), (rope_tabs[2], tab)],
                  [(_sds((t, cw), BF16), pl.BlockSpec((tm, cw), lambda i, j, k: (i, 0))),
                   (_sds((t, kw), BF16), kspec), (_sds((t, kw), BF16), kspec)],
                  [], epilogue, temp_bytes=12 * tm * cw * 4)


def _qk_prep_bwd(proj, gq, gk, rope_tabs, dq, dkc, dkp, dvc, dvp, cw, kw):
    t = proj.shape[0]
    tm = BLOCK
    nblk = t // tm

    def epilogue(_, ins, outs):
        c, sa, sb = ins[5][...], ins[6][...], ins[7][...]
        has_next = (pl.program_id(0) < nblk - 1).astype(F32)
        dk = ins[9][...] + has_next * ins[10][...]
        dv = ins[11][...] + has_next * ins[12][...]
        pieces = []
        for src, gain, dval, dst, width in ((0, 3, ins[8][...], 1, cw), (1, 4, dk, 2, kw)):
            xv, gv = ins[src][...].astype(F32), ins[gain][...]
            sel = _head_selector(width)
            r = _head_bcast(lax.rsqrt(_head_sum(xv * xv, sel) * (1.0 / HEAD_DIM) + RMS_EPS), sel)
            xh = xv * r
            dxn = _rope_t(dval, _tile_lanes(c, width), _tile_lanes(sa, width), _tile_lanes(sb, width))
            u = dxn * gv
            dot = _head_bcast(_head_sum(u * xh, sel), sel) * (1.0 / HEAD_DIM)
            pieces.append((r * (u - xh * dot)).astype(BF16))
            ri = lax.broadcasted_iota(jnp.int32, (width, LANES), 0)
            ci = lax.broadcasted_iota(jnp.int32, (width, LANES), 1)
            fold = (lax.bitwise_and(ri, HEAD_DIM - 1) == ci).astype(BF16)
            colsum = jnp.broadcast_to(jnp.sum(dxn * xh, axis=0, keepdims=True), (8, width))
            part = sum(jnp.dot(p, fold, preferred_element_type=F32) for p in _split3(colsum))

            @pl.when(pl.program_id(0) == 0)
            def _():
                outs[dst][...] = jnp.zeros_like(outs[dst])

            outs[dst][0:1, :] += part[0:1, :]
        outs[0][:, 0:cw] = pieces[0]
        outs[0][:, cw:cw + kw] = pieces[1]
        outs[0][:, cw + kw:cw + 2 * kw] = dv.astype(BF16)

    kblk = cw // kw
    tab = pl.BlockSpec((tm, LANES), lambda i, j, k: (i, 0))
    kcur = pl.BlockSpec((tm, kw), lambda i, j, k: (i, 0))
    knext = pl.BlockSpec((tm, kw), lambda i, j, k: (jnp.minimum(i + 1, nblk - 1), 0))
    acc = pl.BlockSpec((8, LANES), lambda i, j, k: (0, 0))
    return _fused("qk_prep_bwd", (nblk, 1, 1),
                  [(proj, pl.BlockSpec((tm, cw), lambda i, j, k: (i, 3))),
                   (proj, pl.BlockSpec((tm, kw), lambda i, j, k: (i, 4 * kblk))),
                   (proj, pl.BlockSpec((tm, kw), lambda i, j, k: (i, 4 * kblk + 1))),
                   (gq, pl.BlockSpec((1, cw), lambda i, j, k: (0, 0))),
                   (gk, pl.BlockSpec((1, kw), lambda i, j, k: (0, 0))),
                   (rope_tabs[0], tab), (rope_tabs[1], tab), (rope_tabs[2], tab),
                   (dq, pl.BlockSpec((tm, cw), lambda i, j, k: (i, 0))),
                   (dkc, kcur), (dkp, knext), (dvc, kcur), (dvp, knext)],
                  [(_sds((t, cw + 2 * kw), BF16), pl.BlockSpec((tm, cw + 2 * kw), lambda i, j, k: (i, 0))),
                   (_sds((8, LANES), F32), acc), (_sds((8, LANES), F32), acc)],
                  [], epilogue, temp_bytes=16 * tm * cw * 4, semantics=("arbitrary", "arbitrary", "arbitrary"))


def _attn_mask(n):
    key = lax.broadcasted_iota(jnp.int32, (2 * BLOCK, GROUP * BLOCK), 0)
    qry = lax.bitwise_and(lax.broadcasted_iota(jnp.int32, (2 * BLOCK, GROUP * BLOCK), 1), BLOCK - 1)
    return (key > qry) & (key <= qry + BLOCK) & ((key >= BLOCK) | (n > 0))


def _stack_heads(x, h):
    return jnp.concatenate([x[:, (h * GROUP + g) * HEAD_DIM:(h * GROUP + g + 1) * HEAD_DIM] for g in range(GROUP)], axis=0)


def _softmax_with_sink(q4, k2, sink_ref, h, valid):
    sink = jnp.concatenate([sink_ref[h * GROUP + g:h * GROUP + g + 1, :] for g in range(GROUP)], axis=1)
    s = lax.dot_general(k2, q4, NT, preferred_element_type=F32) * ATTN_SCALE
    s = jnp.where(valid, s, NEG_INF)
    m = jnp.maximum(jnp.max(s, axis=0, keepdims=True), sink)
    p = jnp.exp(s - m)
    es = jnp.exp(sink - m)
    inv = 1.0 / (jnp.sum(p, axis=0, keepdims=True) + es)
    return p * inv, es * inv


def _attn_fwd(qn, kn, vb, sink_rows):
    t, cw = qn.shape
    kw = kn.shape[1]
    nkv = kw // HEAD_DIM

    def body(q_ref, kp_ref, kc_ref, vp_ref, vc_ref, sink_ref, o_ref):
        valid = _attn_mask(pl.program_id(0))
        qv = q_ref[...]
        kp, kc, vp, vc = kp_ref[...], kc_ref[...], vp_ref[...], vc_ref[...]
        outs = []
        for h in range(nkv):
            hs = slice(h * HEAD_DIM, (h + 1) * HEAD_DIM)
            k2 = jnp.concatenate([kp[:, hs], kc[:, hs]], axis=0)
            v2 = jnp.concatenate([vp[:, hs], vc[:, hs]], axis=0)
            pn, _ = _softmax_with_sink(_stack_heads(qv, h), k2, sink_ref, h, valid)
            o4 = lax.dot_general(pn.astype(BF16), v2, TN, preferred_element_type=F32)
            outs += [o4[g * BLOCK:(g + 1) * BLOCK] for g in range(GROUP)]
        o_ref[...] = jnp.concatenate(outs, axis=-1).astype(BF16)

    cur = lambda n: (n, 0)
    prev = lambda n: (jnp.maximum(n - 1, 0), 0)
    return pl.pallas_call(
        body, name="attn_fwd", grid=(t // BLOCK,),
        in_specs=[pl.BlockSpec((BLOCK, cw), cur),
                  pl.BlockSpec((BLOCK, kw), prev), pl.BlockSpec((BLOCK, kw), cur),
                  pl.BlockSpec((BLOCK, kw), prev), pl.BlockSpec((BLOCK, kw), cur),
                  pl.BlockSpec(sink_rows.shape, lambda n: (0, 0))],
        out_specs=pl.BlockSpec((BLOCK, cw), cur),
        out_shape=_sds((t, cw), BF16),
        compiler_params=_params(("parallel",), BLOCK * (cw + 4 * kw) * 2 + BLOCK * cw * 2, 8 << 20),
    )(qn, kn, kn, vb, vb, sink_rows)


def _attn_bwd(qn, kn, vb, sink_rows, do):
    t, cw = qn.shape
    kw = kn.shape[1]
    nkv = kw // HEAD_DIM
    nq = nkv * GROUP

    def body(q_ref, kp_ref, kc_ref, vp_ref, vc_ref, sink_ref, do_ref,
             dq_ref, dkc_ref, dkp_ref, dvc_ref, dvp_ref, dsink_ref):
        n = pl.program_id(0)
        valid = _attn_mask(n)
        qv, dov = q_ref[...], do_ref[...]
        kp, kc, vp, vc = kp_ref[...], kc_ref[...], vp_ref[...], vc_ref[...]
        dqs, dks, dvs, dsinks = [], [], [], []
        for h in range(nkv):
            hs = slice(h * HEAD_DIM, (h + 1) * HEAD_DIM)
            k2 = jnp.concatenate([kp[:, hs], kc[:, hs]], axis=0)
            v2 = jnp.concatenate([vp[:, hs], vc[:, hs]], axis=0)
            q4 = _stack_heads(qv, h)
            dob = _stack_heads(dov, h).astype(BF16)
            pn, psink = _softmax_with_sink(q4, k2, sink_ref, h, valid)
            dpn = lax.dot_general(v2, dob, NT, preferred_element_type=F32)
            dvs.append(jnp.dot(pn.astype(BF16), dob, preferred_element_type=F32))
            delta = jnp.sum(pn * dpn, axis=0, keepdims=True)
            ds = (pn * (dpn - delta) * ATTN_SCALE).astype(BF16)
            dks.append(jnp.dot(ds, q4, preferred_element_type=F32))
            dq4 = lax.dot_general(ds, k2, TN, preferred_element_type=F32)
            dsink4 = -psink * delta
            for g in range(GROUP):
                dqs.append(dq4[g * BLOCK:(g + 1) * BLOCK])
                dsinks.append(jnp.broadcast_to(jnp.sum(dsink4[:, g * BLOCK:(g + 1) * BLOCK], axis=1, keepdims=True), (1, LANES)))
        dq_ref[...] = jnp.concatenate(dqs, axis=-1)
        dkp_ref[...] = jnp.concatenate([d[:BLOCK] for d in dks], axis=-1)
        dkc_ref[...] = jnp.concatenate([d[BLOCK:] for d in dks], axis=-1)
        dvp_ref[...] = jnp.concatenate([d[:BLOCK] for d in dvs], axis=-1)
        dvc_ref[...] = jnp.concatenate([d[BLOCK:] for d in dvs], axis=-1)

        @pl.when(n == 0)
        def _():
            dsink_ref[...] = jnp.zeros_like(dsink_ref)

        dsink_ref[...] += jnp.concatenate(dsinks, axis=0)

    cur = lambda n: (n, 0)
    prev = lambda n: (jnp.maximum(n - 1, 0), 0)
    kspec = pl.BlockSpec((BLOCK, kw), cur)
    return pl.pallas_call(
        body, name="attn_bwd", grid=(t // BLOCK,),
        in_specs=[pl.BlockSpec((BLOCK, cw), cur),
                  pl.BlockSpec((BLOCK, kw), prev), kspec,
                  pl.BlockSpec((BLOCK, kw), prev), kspec,
                  pl.BlockSpec(sink_rows.shape, lambda n: (0, 0)),
                  pl.BlockSpec((BLOCK, cw), cur)],
        out_specs=[pl.BlockSpec((BLOCK, cw), cur), kspec, kspec, kspec, kspec,
                   pl.BlockSpec((nq, LANES), lambda n: (0, 0))],
        out_shape=[_sds((t, cw), F32)] + [_sds((t, kw), F32)] * 4 + [_sds((nq, LANES), F32)],
        compiler_params=_params(("arbitrary",), BLOCK * (cw + 4 * kw) * 2 + 2 * BLOCK * cw * 4 + 4 * BLOCK * kw * 4, 12 << 20),
    )(qn, kn, kn, vb, vb, sink_rows, do)


def _mix_out(ca, o, woc, woa, proj):
    t, cw = ca.shape
    nb = woc.shape[2]
    d = N_DEV * nb
    tm = min(t, 1024)
    ga0 = (3 * cw + cw + 2 * (cw // 4)) // nb

    def body(ca_ref, o_ref, woc_ref, woa_ref, ga_ref, gb_ref, m_ref, ya_ref, yb_ref):
        ya = jnp.dot(ca_ref[...], woc_ref[...], preferred_element_type=F32)
        yb = jnp.dot(o_ref[...], woa_ref[...], preferred_element_type=F32)
        ya_ref[...] = ya.astype(BF16)
        yb_ref[...] = yb.astype(BF16)
        m_ref[...] = (_sigmoid(ga_ref[...].astype(F32)) * ya + _sigmoid(gb_ref[...].astype(F32)) * yb).astype(BF16)

    act = pl.BlockSpec((tm, cw), lambda i, j: (i, 0))
    wsp = pl.BlockSpec((None, cw, nb), lambda i, j: (j, 0, 0))
    osp = pl.BlockSpec((tm, nb), lambda i, j: (i, j))
    blocks = 2 * tm * cw * 2 + 2 * cw * nb * 2 + 2 * tm * nb * 4 + 3 * tm * nb * 2
    return pl.pallas_call(
        body, name="mix_out", grid=(t // tm, N_DEV),
        in_specs=[act, act, wsp, wsp,
                  pl.BlockSpec((tm, nb), lambda i, j: (i, ga0 + j)),
                  pl.BlockSpec((tm, nb), lambda i, j: (i, ga0 + N_DEV + j))],
        out_specs=[osp, osp, osp],
        out_shape=[_sds((t, d), BF16)] * 3,
        compiler_params=_params(("parallel", "parallel"), blocks, 6 * tm * nb * 4),
    )(ca, o, woc, woa, proj, proj)


def _mix_residual(merged, wo, x):
    t, d = x.shape
    tm = min(t, 512)

    def epilogue(acc, ins, outs):
        outs[0][...] = ins[2][...] + acc

    row = pl.BlockSpec((tm, d), lambda i, j, k: (i, 0))
    return _fused("mix_residual", (t // tm, 1, 1),
                  [(merged, row), (wo, pl.BlockSpec((d, d), lambda i, j, k: (0, 0))), (x, row)],
                  [(_sds((t, d), F32), row)], [(0, 1, NN)], epilogue, temp_bytes=2 * tm * d * 4)[0]


def _mix_bwd_gates(dx, wo, ya, yb, proj, cw):
    t, d = dx.shape
    tm = min(t, 1024)
    tn = min(d, 512)
    ga0 = (4 * cw + 2 * (cw // 4)) // tn

    def epilogue(acc, ins, outs):
        sa, sb = _sigmoid(ins[4][...].astype(F32)), _sigmoid(ins[5][...].astype(F32))
        outs[0][...] = (acc * sa).astype(BF16)
        outs[1][...] = (acc * sb).astype(BF16)
        outs[2][0] = (acc * ins[2][...].astype(F32) * sa * (1.0 - sa)).astype(BF16)
        outs[2][1] = (acc * ins[3][...].astype(F32) * sb * (1.0 - sb)).astype(BF16)

    blk = pl.BlockSpec((tm, tn), lambda i, j, k: (i, j))
    return _fused("mix_bwd_gates", (t // tm, d // tn, 1),
                  [(dx, pl.BlockSpec((tm, d), lambda i, j, k: (i, 0))),
                   (wo, pl.BlockSpec((tn, d), lambda i, j, k: (j, 0))),
                   (ya, blk), (yb, blk),
                   (proj, pl.BlockSpec((tm, tn), lambda i, j, k: (i, ga0 + j))),
                   (proj, pl.BlockSpec((tm, tn), lambda i, j, k: (i, ga0 + d // tn + j)))],
                  [(_sds((t, d), BF16), blk), (_sds((t, d), BF16), blk),
                   (_sds((2, t, d), BF16), pl.BlockSpec((2, tm, tn), lambda i, j, k: (0, i, j)))],
                  [(0, 1, NT)], epilogue, temp_bytes=8 * tm * tn * 4)


def _tn_matmul(name, a, b, tm, out_dtype=BF16):
    t, m = a.shape
    n = b.shape[1]

    def epilogue(acc, ins, outs):
        outs[0][...] = acc.astype(out_dtype)

    return _fused(name, (m // tm, 1, 1),
                  [(a, pl.BlockSpec((t, tm), lambda i, j, k: (0, i))),
                   (b, pl.BlockSpec((t, n), lambda i, j, k: (0, 0)))],
                  [(_sds((m, n), out_dtype), pl.BlockSpec((tm, n), lambda i, j, k: (i, 0)))],
                  [(0, 1, TN)], epilogue, temp_bytes=2 * tm * n * 4)[0]


def _out_proj_bwd_act(dya, dyb, woc, woa, deps=()):
    t, d = dya.shape
    kdim, nb = woc.shape[1], woc.shape[2]
    tm = min(t, 512)

    def body(dya_ref, dyb_ref, woc_ref, woa_ref, *rest):
        for dy_ref, w_ref, o_ref in ((dya_ref, woc_ref, rest[-2]), (dyb_ref, woa_ref, rest[-1])):
            total = None
            for j in range(N_DEV):
                part = lax.dot_general(dy_ref[:, j * nb:(j + 1) * nb], w_ref[j], NT, preferred_element_type=F32)
                total = part if total is None else total + part
            o_ref[...] = total

    row = pl.BlockSpec((tm, d), lambda i: (i, 0))
    wsp = pl.BlockSpec((N_DEV, kdim, nb), lambda i: (0, 0, 0))
    osp = pl.BlockSpec((tm, kdim), lambda i: (i, 0))
    blocks = 2 * tm * d * 2 + 2 * N_DEV * kdim * nb * 2 + 2 * tm * kdim * 4
    return pl.pallas_call(
        body, name="mix_bwd_dca_do", grid=(t // tm,),
        in_specs=[row, row, wsp, wsp] + [_ANY] * len(deps), out_specs=[osp, osp],
        out_shape=[_sds((t, kdim), F32)] * 2,
        compiler_params=_params(("parallel",), blocks, 4 * tm * kdim * 4),
    )(dya, dyb, woc, woa, *deps)


def _out_proj_bwd_w(ca, o, dya, dyb, nb):
    t, kdim = ca.shape

    def body(ca_ref, o_ref, dya_ref, dyb_ref, dwoc_ref, dwoa_ref):
        dwoc_ref[...] = lax.dot_general(ca_ref[...], dya_ref[...], TN, preferred_element_type=F32).astype(BF16)
        dwoa_ref[...] = lax.dot_general(o_ref[...], dyb_ref[...], TN, preferred_element_type=F32).astype(BF16)

    act = pl.BlockSpec((t, kdim), lambda j: (0, 0))
    col = pl.BlockSpec((t, nb), lambda j: (0, j))
    osp = pl.BlockSpec((None, kdim, nb), lambda j: (j, 0, 0))
    blocks = 2 * t * kdim * 2 + 2 * t * nb * 2 + 2 * kdim * nb * 2
    return pl.pallas_call(
        body, name="mix_bwd_dwoc_dwoa", grid=(N_DEV,),
        in_specs=[act, act, col, col], out_specs=[osp, osp],
        out_shape=[_sds((N_DEV, kdim, nb), BF16)] * 2,
        compiler_params=_params(("parallel",), blocks, 4 * kdim * nb * 4),
    )(ca, o, dya, dyb)


def _proj_bwd_act(dproj, w_in, deps=()):
    t, n = dproj.shape
    d, nb = w_in.shape[2], w_in.shape[3]
    tm = min(t, 512)

    def epilogue(acc, ins, outs):
        outs[0][...] = acc

    def products(ins):
        return (lax.dot_general(ins[0][:, 0:nb], ins[1][0], NT, preferred_element_type=F32)
                + lax.dot_general(ins[0][:, nb:2 * nb], ins[1][1], NT, preferred_element_type=F32))

    return _fused("mix_bwd_dh", (t // tm, 1, 4),
                  [(dproj, pl.BlockSpec((tm, 2 * nb), lambda i, j, k: (i, k))),
                   (w_in, pl.BlockSpec((None, 2, d, nb), lambda i, j, k: (k, 0, 0, 0)))],
                  [(_sds((t, d), F32), pl.BlockSpec((tm, d), lambda i, j, k: (i, 0)))],
                  products, epilogue, nk=4, acc_shape=(tm, d), temp_bytes=tm * d * 4, deps=deps)[0]


def _proj_bwd_w(h, dproj):
    t, d = h.shape
    nb = dproj.shape[1] // N_DEV
    tm = min(d, 512)

    def body(h_ref, dp_ref, o_ref):
        hv = h_ref[...]
        o_ref[0] = lax.dot_general(hv, dp_ref[:, 0:nb], TN, preferred_element_type=F32).astype(BF16)
        o_ref[1] = lax.dot_general(hv, dp_ref[:, nb:2 * nb], TN, preferred_element_type=F32).astype(BF16)

    blocks = t * tm * 2 + t * 2 * nb * 2 + 2 * tm * nb * 2
    return pl.pallas_call(
        body, name="mix_bwd_dwin", grid=(4, d // tm),
        in_specs=[pl.BlockSpec((t, tm), lambda j, i: (0, i)),
                  pl.BlockSpec((t, 2 * nb), lambda j, i: (0, j))],
        out_specs=pl.BlockSpec((None, 2, tm, nb), lambda j, i: (j, 0, i, 0)),
        out_shape=_sds((4, 2, d, nb), BF16),
        compiler_params=_params(("parallel", "parallel"), blocks, 4 * tm * nb * 4),
    )(h, dproj)


def _adamw_math(w, g, m, v):
    m = ADAM_B1 * m + (1.0 - ADAM_B1) * g
    v = ADAM_B2 * v + (1.0 - ADAM_B2) * (g * g)
    m_hat = m / (1.0 - ADAM_B1 ** ADAM_STEP)
    v_hat = v / (1.0 - ADAM_B2 ** ADAM_STEP)
    delta = -ADAM_LR * (m_hat / (jnp.sqrt(v_hat) + ADAM_EPS) + ADAM_WD * w)
    return delta, m, v


def _adamw(name, parts, w, m, v, tr):
    r, c = w.shape

    def body(p_ref, w_ref, m_ref, v_ref, g_out, d_out, m_out, v_out):
        g = p_ref[0].astype(F32)
        for s in range(1, N_DEV):
            g = g + p_ref[s].astype(F32)
        delta, mn, vn = _adamw_math(w_ref[...], g, m_ref[...], v_ref[...])
        g_out[...] = g
        d_out[...] = delta
        m_out[...] = mn
        v_out[...] = vn

    blk = pl.BlockSpec((tr, c), lambda i: (i, 0))
    blocks = N_DEV * tr * c * parts.dtype.itemsize + 7 * tr * c * 4
    return pl.pallas_call(
        body, name=name, grid=(r // tr,),
        in_specs=[pl.BlockSpec((N_DEV, tr, c), lambda i: (0, i, 0)), blk, blk, blk],
        out_specs=[blk] * 4, out_shape=[_sds((r, c), F32)] * 4,
        compiler_params=_params(("parallel",), blocks, 6 * tr * c * 4),
    )(parts, w, m, v)


def _chip_sum(sums_ref):
    g = sums_ref[0].astype(F32)
    for k in range(1, 4):
        g = g + sums_ref[k].astype(F32)
    return g


def _adamw_chips(name, sums, w, m, v, tr, deps=(), row0=0, into=None):
    r, c = w.shape
    rs = sums.shape[1]
    i0 = row0 // tr
    n_pass = len(deps) + (4 if into is not None else 0)

    def body(sums_ref, w_ref, m_ref, v_ref, *rest):
        g_out, d_out, m_out, v_out = rest[n_pass:]
        g = _chip_sum(sums_ref)
        delta, mn, vn = _adamw_math(w_ref[...], g, m_ref[...], v_ref[...])
        g_out[...] = g
        d_out[...] = delta
        m_out[...] = mn
        v_out[...] = vn

    blk = pl.BlockSpec((tr, c), lambda i: (i0 + i, 0))
    blocks = 4 * tr * c * 2 + 7 * tr * c * 4
    passed = list(deps) + (list(into) if into is not None else [])
    aliases = {4 + len(deps) + q: q for q in range(4)} if into is not None else {}
    return pl.pallas_call(
        body, name=name, grid=(rs // tr,),
        in_specs=[pl.BlockSpec((4, tr, c), lambda i: (0, i, 0)), blk, blk, blk] + [_ANY] * n_pass,
        out_specs=[blk] * 4, out_shape=[_sds((r, c), F32)] * 4,
        input_output_aliases=aliases,
        compiler_params=_params(("parallel",), blocks, 6 * tr * c * 4),
    )(sums, w, m, v, *passed)


def _adamw_side(contrib, w, m, v, n_tiles, step_of):
    r, c = w.shape
    tr = r // n_tiles
    assert tr * n_tiles == r and tr % 16 == 0, (r, n_tiles)

    def tile(i, j, k):
        return jnp.minimum(step_of(i, j, k), n_tiles - 1)

    blk = pl.BlockSpec((tr, c), lambda i, j, k: (tile(i, j, k), 0))
    ins = [(contrib, pl.BlockSpec((4, tr, c), lambda i, j, k: (0, tile(i, j, k), 0))), (w, blk), (m, blk), (v, blk)]
    outs = [(_sds((r, c), F32), blk)] * 4

    def fn(in_refs, out_refs):
        @pl.when(step_of(pl.program_id(0), pl.program_id(1), pl.program_id(2)) < n_tiles)
        def _():
            g = _chip_sum(in_refs[0])
            delta, mn, vn = _adamw_math(in_refs[1][...], g, in_refs[2][...], in_refs[3][...])
            out_refs[0][...] = g
            out_refs[1][...] = delta
            out_refs[2][...] = mn
            out_refs[3][...] = vn

    return ins, outs, fn


def _rope_tables(t):
    half = ROT_DIM // 2
    inv_freq = 1.0 / (ROPE_THETA ** (jnp.arange(0, ROT_DIM, 2, dtype=F32) / ROT_DIM))
    ang = jnp.arange(t, dtype=F32)[:, None] * inv_freq[None, :]
    cos, sin = jnp.cos(ang), jnp.sin(ang)
    ones = jnp.ones((t, HEAD_DIM - ROT_DIM), F32)
    zeros = jnp.zeros((t, HEAD_DIM - half), F32)
    c = jnp.concatenate([cos, cos, ones], axis=1)
    sa = jnp.concatenate([-sin, zeros], axis=1)
    sb = jnp.concatenate([jnp.zeros((t, half), F32), sin, jnp.zeros((t, HEAD_DIM - ROT_DIM), F32)], axis=1)
    return tuple(jnp.tile(a, (1, LANES // HEAD_DIM)) for a in (c, sa, sb))


def _pad_rows(a, rows=8):
    return jnp.pad(a, ((0, rows - a.shape[0]), (0, 0)))


def kernel(x, g_ffn1, w_gu1, w_down1, g_mix, w_in, conv_w, q_norm_g, k_norm_g, sinks, w_out_conv, w_out_attn, w_o, g_ffn2, w_gu2, w_down2, loss_target, m_g_ffn1, m_w_gu1, m_w_down1, m_g_mix, m_w_in, m_conv_w, m_q_norm_g, m_k_norm_g, m_sinks, m_w_out_conv, m_w_out_attn, m_w_o, m_g_ffn2, m_w_gu2, m_w_down2, v_g_ffn1, v_w_gu1, v_w_down1, v_g_mix, v_w_in, v_conv_w, v_q_norm_g, v_k_norm_g, v_sinks, v_w_out_conv, v_w_out_attn, v_w_o, v_g_ffn2, v_w_gu2, v_w_down2):
    t, d = x.shape[1], x.shape[2]
    cw = d // 2
    kw = cw // GROUP
    nq = cw // HEAD_DIM
    xs, target = x.reshape(t, d), loss_target.reshape(t, d)
    me = 4 * lax.axis_index("x") + 2 * lax.axis_index("y") + lax.axis_index("c")

    big = {"w_gu1": w_gu1, "w_down1": w_down1, "w_in": w_in, "w_out_conv": w_out_conv,
           "w_out_attn": w_out_attn, "w_o": w_o, "w_gu2": w_gu2, "w_down2": w_down2}
    big_m = {"w_gu1": m_w_gu1, "w_down1": m_w_down1, "w_in": m_w_in, "w_out_conv": m_w_out_conv,
             "w_out_attn": m_w_out_attn, "w_o": m_w_o, "w_gu2": m_w_gu2, "w_down2": m_w_down2}
    big_v = {"w_gu1": v_w_gu1, "w_down1": v_w_down1, "w_in": v_w_in, "w_out_conv": v_w_out_conv,
             "w_out_attn": v_w_out_attn, "w_o": v_w_o, "w_gu2": v_w_gu2, "w_down2": v_w_down2}
    names = list(big)

    tiles = {"w_gu1": 256, "w_gu2": 256, "w_in": 256, "w_down1": 176, "w_down2": 176,
             "w_out_conv": 1024, "w_out_attn": 1024, "w_o": 128}

    def row_tile(n):
        r = big[n].shape[1]
        return tiles[n] if r % tiles[n] == 0 else r

    rs_shape = {n: big[n].shape[1:] for n in names}
    half = rs_shape["w_gu1"][0] // 2
    rs_shape["w_gu1_lo"] = rs_shape["w_gu1_hi"] = (half, rs_shape["w_gu1"][1])

    def add_tile(n):
        r, c = rs_shape[n]
        while r * c * 2 > (3 << 20) and r % 32 == 0:
            r //= 2
        return r

    me_arr = me.astype(jnp.int32).reshape(1)
    sources = [(n, big[n][0], BF16, row_tile(n)) for n in names] + [("conv_w", _pad_rows(conv_w[0]), F32, 8)]
    issue_order = [0, 1, 2, 8, 3, 4, 5, 6, 7]
    first = _place_shard("place_" + names[0], sources[0][1], BF16, me_arr, sources[0][3])
    started = [_gather_start("gather_start_first", [first])]
    early = {2: (big_m["w_in"][0], big_v["w_in"][0])}
    rest = [_place_shard("place_" + sources[i][0], sources[i][1], sources[i][2], me_arr, sources[i][3],
                         deps=(started[0][3],) + early.get(i, ())) for i in issue_order[1:]]
    started.append(_gather_start("gather_start_rest", rest))
    where = {0: (0, 0)}
    where.update({i: (1, p) for p, i in enumerate(issue_order[1:])})

    def fetch(tag, idxs, after, forward=True):
        call = where[idxs[0]][0]
        send, recv, stacks, _ = started[call]
        positions = [where[i][1] for i in idxs]
        got = _gather_wait("gather_wait_" + tag, positions, send, recv, [stacks[p] for p in positions], after)
        return _forward_to_sibling("gather_forward_" + tag, got) if forward else got

    rope_tabs = _rope_tables(t)
    gq = jnp.tile(q_norm_g, (1, nq))
    gk = jnp.tile(k_norm_g, (1, nq // GROUP))
    sink_rows = jnp.broadcast_to(sinks[0][:, None], (nq, LANES))

    wts = {}
    h1 = _rms_fwd("ffn1_norm", xs, g_ffn1)
    wts["w_gu1"], = fetch("gu1", [0], started[1][3])
    gu1, a1 = _ffn_up("ffn1_up", h1, wts["w_gu1"])
    wts["w_down1"], = fetch("down1", [1], a1)
    wd1 = wts["w_down1"].reshape(-1, d)
    x1 = _ffn_down("ffn1_down", a1, wd1, xs)
    h2 = _rms_fwd("mix_norm", x1, g_mix)
    wts["w_in"], conv_land = fetch("in", [2, 8], h2)
    w_in_full = wts["w_in"].reshape(4, 2, d, -1)
    conv_full = jnp.transpose(conv_land, (1, 0, 2)).reshape(8, cw)
    proj = _proj(h2, w_in_full)
    ca = _conv_fwd(proj, conv_full)
    qn, kn, vb = _qk_prep(proj, gq, gk, rope_tabs, cw, kw)
    o = _attn_fwd(qn, kn, vb, sink_rows)
    wts["w_out_conv"], wts["w_out_attn"] = fetch("out", [3, 4], o)
    merged, ya, yb = _mix_out(ca, o, wts["w_out_conv"], wts["w_out_attn"], proj)
    wts["w_o"], = fetch("o", [5], merged)
    wo = wts["w_o"].reshape(d, d)
    x2 = _mix_residual(merged, wo, x1)
    h3 = _rms_fwd("ffn2_norm", x2, g_ffn2)
    mine = lax.axis_index("c").astype(jnp.int32).reshape(1)
    got = fetch("gu2", [6], h3, forward=False)
    fsend, frecv, got = _forward_start("gather_forward_start_gu2", got)
    part = _ffn_up("ffn2_up_mine", h3, got[0], parity=mine)
    wts["w_gu2"], = _forward_wait("gather_forward_wait_gu2", fsend, frecv, got, part[1])
    gu2, a2 = _ffn_up("ffn2_up_sibling", h3, wts["w_gu2"], parity=1 - mine, into=part)
    wts["w_down2"], = fetch("down2", [7], a2)
    wd2 = wts["w_down2"].reshape(-1, d)
    dy, sq, dy_bf = _ffn_down("ffn2_down", a2, wd2, x2, target=target)
    loss = lax.psum(sq[0, 0] * (0.5 / d), ("x", "y", "c"))

    place = jnp.stack([lax.axis_index("c"), 2 * lax.axis_index("x") + lax.axis_index("y")]).astype(jnp.int32)
    def pair_start(tag, group, grads, deps=()):
        stacks = [grads[n].reshape((4, 2) + rs_shape[n]) for n in group]
        lands = [lax.empty((4,) + rs_shape[n], BF16) for n in group]
        return _pair_start("rs_pair_start_" + tag, stacks, lands, deps)

    def chip_start(tag, group, pending, after):
        send, recv, stacks, lands, _ = pending
        stacks, lands = _pair_wait("rs_pair_wait_" + tag, send, recv, stacks, lands, after)
        added = [_pair_add("rs_pair_add_" + n, st, ld, place, add_tile(n)) for n, st, ld in zip(group, stacks, lands)]
        return _chip_start("rs_chip_start_" + tag, [a[0] for a in added], [a[1] for a in added])

    group_a, group_b, group_c = ["w_down2", "w_gu2"], ["w_o", "w_out_conv", "w_out_attn"], ["w_in"]
    group_d, group_e, group_f = ["w_down1"], ["w_gu1_lo"], ["w_gu1_hi"]
    g = {}
    dgu2, a2 = _ffn_bwd_act("ffn2_bwd_act", dy_bf, wd2, gu2)
    pend_s = _sibling_start("rs_act_start_gu2", [dgu2, h3])
    g["w_down2"], = _ffn_bwd_dwd("ffn2_bwd_dwd", a2, dy_bf, deps=(pend_s[4],))
    pend_a = pair_start("a", ["w_down2"], g)
    dh3, = _ffn_bwd_dh("ffn2_bwd_dh", pend_s[2][0], wts["w_gu2"], deps=(pend_a[4],))
    (dgu2, h3), (dgu2_sib, h3_sib) = _sibling_wait("rs_act_wait_gu2", pend_s[0], pend_s[1], pend_s[2], pend_s[3], dh3)
    sums_gu2, slots_gu2 = _ffn_bwd_dwgu_pair("ffn2_bwd_dwgu", h3, h3_sib, dgu2, dgu2_sib, place)
    stacks_a, lands_a = _pair_wait("rs_pair_wait_a", pend_a[0], pend_a[1], pend_a[2], pend_a[3], sums_gu2)
    added_a = _pair_add("rs_pair_add_w_down2", stacks_a[0], lands_a[0], place, add_tile("w_down2"))
    ring_a = _chip_start("rs_chip_start_a", [added_a[0], sums_gu2], [added_a[1], slots_gu2])
    dx2, dg_ffn2, dx2_bf = _rms_bwd("ffn2_bwd_rms", x2, g_ffn2, dh3, dy, deps=(ring_a[4],), with_bf16=True)

    dya, dyb, dgates = _mix_bwd_gates(dx2_bf, wo, ya, yb, proj, cw)
    g["w_o"] = _tn_matmul("mix_bwd_dwo", merged, dx2_bf, min(d, 512))
    g["w_out_conv"], g["w_out_attn"] = _out_proj_bwd_w(ca, o, dya, dyb, d // N_DEV)
    pend_b = pair_start("b", group_b, g)
    dca, do = _out_proj_bwd_act(dya, dyb, wts["w_out_conv"], wts["w_out_attn"], deps=(pend_b[4],))
    ring_b = chip_start("b", group_b, pend_b, do)
    d3, dconv_w = _conv_bwd(proj, conv_full, dca, deps=(ring_b[4],))
    dq, dkc, dkp, dvc, dvp, dsink = _attn_bwd(qn, kn, vb, sink_rows, do)
    dqkv, dgq, dgk = _qk_prep_bwd(proj, gq, gk, rope_tabs, dq, dkc, dkp, dvc, dvp, cw, kw)
    dproj = jnp.concatenate([d3[0], d3[1], d3[2], dqkv, dgates[0], dgates[1]], axis=1)
    g["w_in"] = _proj_bwd_w(h2, dproj)
    pend_c = pair_start("c", group_c, g)
    dh2 = _proj_bwd_act(dproj, w_in_full, deps=(pend_c[4],))
    ring_c = chip_start("c", group_c, pend_c, dh2)
    dx1, dg_mix, dx1_bf = _rms_bwd("mix_bwd_rms", x1, g_mix, dh2, dx2, deps=(ring_c[4],), with_bf16=True)

    big_out = {}
    arrived = {}

    def wait_group(tag, group, ring, after):
        send, recv, parts, lands2, _ = ring
        parts, lands2 = _chip_wait("rs_chip_wait_" + tag, send, recv, parts, lands2, after)
        arrived.update(dict(zip(group, lands2)))

    def update(n, after):
        res = _adamw_chips("adamw_" + n, arrived[n], big[n][0], big_m[n][0], big_v[n][0], row_tile(n), deps=(after,))
        big_out[n] = [a[None] for a in res]
        return res[0]

    def update_beside(n, n_tiles, step_of):
        return _adamw_side(arrived[n], big[n][0], big_m[n][0], big_v[n][0], n_tiles, step_of)

    def keep(n, res):
        big_out[n] = [a[None] for a in res]

    dgu1, a1 = _ffn_bwd_act("ffn1_bwd_act", dx1_bf, wd1, gu1)
    wait_group("a", group_a, ring_a, a1)
    g["w_down1"], *res = _ffn_bwd_dwd("ffn1_bwd_dwd", a1, dx1_bf,
                                       side=update_beside("w_down2", 11, lambda i, j, k: i * 4 + j))
    keep("w_down2", res)
    pend_d = pair_start("d", group_d, g)
    g["w_gu1_lo"], *res = _ffn_bwd_dwgu("ffn1_bwd_dwgu_lo", h1, dgu1, deps=(pend_d[4],), rows=(0, half),
                                         side=update_beside("w_gu2", 16, lambda i, j, k: i * 2 + j))
    keep("w_gu2", res)
    ring_d = chip_start("d", group_d, pend_d, g["w_gu1_lo"])
    pend_e = pair_start("e", group_e, g, deps=(ring_d[4],))
    wait_group("c", group_c, ring_c, pend_e[4])
    g["w_gu1_hi"], *res = _ffn_bwd_dwgu("ffn1_bwd_dwgu_hi", h1, dgu1, rows=(half, half),
                                         side=update_beside("w_in", 16, lambda i, j, k: i * 2 + j))
    keep("w_in", res)
    ring_e = chip_start("e", group_e, pend_e, g["w_gu1_hi"])
    pend_f = pair_start("f", group_f, g, deps=(ring_e[4],))
    wait_group("b", group_b, ring_b, pend_f[4])
    after = pend_f[4]
    for n in group_b:
        after = update(n, after)
    ring_f = chip_start("f", group_f, pend_f, after)
    wait_group("d", group_d, ring_d, ring_f[4])
    dh1, *res = _ffn_bwd_dh("ffn1_bwd_dh", dgu1, wts["w_gu1"],
                             side=update_beside("w_down1", 11, lambda i, j, k: i * 4 + k))
    keep("w_down1", res)
    grad_x, dg_ffn1 = _rms_bwd("ffn1_bwd_rms", xs, g_ffn1, dh1, dx1)
    after = grad_x
    n = "w_gu1"
    wait_group("e", group_e, ring_e, after)
    res = _adamw_chips("adamw_w_gu1_lo", arrived["w_gu1_lo"], big[n][0], big_m[n][0], big_v[n][0], row_tile(n), deps=(after,))
    wait_group("f", group_f, ring_f, res[0])
    res = _adamw_chips("adamw_w_gu1_hi", arrived["w_gu1_hi"], big[n][0], big_m[n][0], big_v[n][0], row_tile(n),
                       row0=half, into=res)
    keep(n, res)
    after = res[0]

    small = {"g_ffn1": dg_ffn1[0:1], "g_mix": dg_mix[0:1], "g_ffn2": dg_ffn2[0:1],
             "q_norm_g": dgq[0:1, :HEAD_DIM], "k_norm_g": dgk[0:1, :HEAD_DIM], "sinks": dsink[:, 0][None],
             "conv_w": dconv_w[0:CONV_K].reshape(1, -1)}
    small_w = {"g_ffn1": g_ffn1, "g_mix": g_mix, "g_ffn2": g_ffn2, "q_norm_g": q_norm_g, "k_norm_g": k_norm_g,
               "sinks": sinks, "conv_w": None}
    small_m = {"g_ffn1": m_g_ffn1, "g_mix": m_g_mix, "g_ffn2": m_g_ffn2, "q_norm_g": m_q_norm_g,
               "k_norm_g": m_k_norm_g, "sinks": m_sinks, "conv_w": m_conv_w}
    small_v = {"g_ffn1": v_g_ffn1, "g_mix": v_g_mix, "g_ffn2": v_g_ffn2, "q_norm_g": v_q_norm_g,
               "k_norm_g": v_k_norm_g, "sinks": v_sinks, "conv_w": v_conv_w}
    snames = list(small)
    widths = [small[n].shape[1] for n in snames]
    total = sum(widths)
    rows = -(-total // LANES)
    rows = -(-rows // 8) * 8

    def pack(vals):
        flat = jnp.concatenate([v.reshape(1, -1) for v in vals], axis=1)
        return jnp.pad(flat, ((0, 0), (0, rows * LANES - total))).reshape(rows, LANES)

    csh = cw // N_DEV

    def place_conv(local, fill):
        full = jnp.full((CONV_K, cw), fill, F32)
        return lax.dynamic_update_slice(full, local, (0, me * csh)).reshape(1, -1)

    pw = pack([small_w[n] if n != "conv_w" else place_conv(conv_w[0], 0.0) for n in snames])
    pm = pack([small_m[n] if n != "conv_w" else place_conv(m_conv_w[0], 0.0) for n in snames])
    pv = pack([small_v[n] if n != "conv_w" else place_conv(v_conv_w[0], 1.0) for n in snames])
    parts = _all_gather_small("gather_small_grads", pack([small[n] for n in snames]), deps=(after,))
    sg, sd, sm, sv = [a.reshape(1, -1) for a in _adamw("adamw_small", parts, pw, pm, pv, rows)]

    def unpack(flat, n):
        off = sum(widths[:snames.index(n)])
        piece = flat[:, off:off + widths[snames.index(n)]]
        if n == "conv_w":
            piece = lax.dynamic_slice(piece.reshape(CONV_K, cw), (0, me * csh), (CONV_K, csh))[None]
        return piece

    order = ["g_ffn1", "w_gu1", "w_down1", "g_mix", "w_in", "conv_w", "q_norm_g", "k_norm_g", "sinks",
             "w_out_conv", "w_out_attn", "w_o", "g_ffn2", "w_gu2", "w_down2"]
    outs = [loss, grad_x[None]]
    for idx, flat in enumerate((sg, sd, sm, sv)):
        for n in order:
            outs.append(big_out[n][idx] if n in big_out else unpack(flat, n))
    return tuple(outs)
```

```python
import jax
import jax.numpy as jnp
from jax import lax
from jax.experimental import pallas as pl
from jax.experimental.pallas import tpu as pltpu

F32 = jnp.float32
BF16 = jnp.bfloat16

N_DEV = 8
HEAD_DIM = 64
GROUP = 4
BLOCK = 128
ROT_DIM = 16
ROPE_THETA = 500000.0
RMS_EPS = 1e-6
NEG_INF = -1e30
ATTN_SCALE = HEAD_DIM ** -0.5
CONV_K = 3
LANES = 128
MXU_COLS = 256
VMEM_BYTES_V7X = 64 * 1024 * 1024
VMEM_CAP = VMEM_BYTES_V7X - 6 * 1024 * 1024

ADAM_LR = 0.001
ADAM_B1 = 0.9
ADAM_B2 = 0.999
ADAM_EPS = 1e-08
ADAM_WD = 0.01
ADAM_STEP = 10

NN = (((1,), (0,)), ((), ()))
NT = (((1,), (1,)), ((), ()))
TN = (((0,), (0,)), ((), ()))

MESH = pl.DeviceIdType.MESH


def _nbytes(shape, dtype):
    n = 1
    for s in shape:
        if s is not None:
            n *= s
    return n * jnp.dtype(dtype).itemsize


def _params(semantics, block_bytes, temp_bytes):
    assert 2 * block_bytes + temp_bytes <= VMEM_CAP, (block_bytes, temp_bytes)
    return pltpu.CompilerParams(dimension_semantics=semantics, vmem_limit_bytes=VMEM_CAP)


def _fused(name, grid, ins, outs, dots, epilogue, *, nk=1, acc_shape=None, temp_bytes=0,
           semantics=("parallel", "parallel", "arbitrary"), deps=(), side=None, prefetch=None):
    n_main_in, n_main_out = len(ins), len(outs)
    n_pre = 0 if prefetch is None else 1
    if side is not None:
        side_ins, side_outs = list(side[0]), list(side[1])
        if n_pre:
            side_ins = [(a, _blind(spec)) for a, spec in side_ins]
            side_outs = [(a, _blind(spec)) for a, spec in side_outs]
        ins, outs = list(ins) + side_ins, list(outs) + side_outs
    n_in, n_out = len(ins), len(outs)
    n_dep = len(deps)

    def body(*refs):
        pre, refs = refs[:n_pre], refs[n_pre:]
        in_refs, out_refs = refs[:n_in], refs[n_in + n_dep:n_in + n_dep + n_out]
        scratch = refs[n_in + n_dep + n_out:]
        if side is not None:
            side[2](in_refs[n_main_in:], out_refs[n_main_out:])

        def products():
            if callable(dots):
                return dots(in_refs)
            total = None
            for ai, bi, contract in dots:
                a, b = in_refs[ai][...], in_refs[bi][...]
                a = a if a.dtype == BF16 else a.astype(BF16)
                b = b if b.dtype == BF16 else b.astype(BF16)
                p = lax.dot_general(a, b, contract, preferred_element_type=F32)
                total = p if total is None else total + p
            return total

        if nk == 1:
            epilogue(products() if dots else None, in_refs, out_refs, *pre)
        else:
            acc = scratch[0]
            k = pl.program_id(2)

            @pl.when(k == 0)
            def _():
                acc[...] = jnp.zeros_like(acc)

            acc[...] += products()

            @pl.when(k == nk - 1)
            def _():
                epilogue(acc[...], in_refs, out_refs, *pre)

    block_bytes = sum(_nbytes(spec.block_shape, a.dtype) for a, spec in ins)
    block_bytes += sum(_nbytes(spec.block_shape, s.dtype) for s, spec in outs)
    scratch_shapes = []
    if nk > 1:
        scratch_shapes.append(pltpu.VMEM(acc_shape, F32))
        temp_bytes += _nbytes(acc_shape, F32)
    in_specs = [spec for _, spec in ins] + [pl.BlockSpec(memory_space=pl.ANY)] * n_dep
    out_specs = [spec for _, spec in outs]
    if n_pre:
        grid_spec = pltpu.PrefetchScalarGridSpec(num_scalar_prefetch=1, grid=grid, in_specs=in_specs,
                                                 out_specs=out_specs, scratch_shapes=scratch_shapes)
        return pl.pallas_call(
            body, name=name, grid_spec=grid_spec, out_shape=[s for s, _ in outs],
            compiler_params=_params(semantics, block_bytes, temp_bytes),
        )(prefetch, *[a for a, _ in ins], *deps)
    res = pl.pallas_call(
        body, name=name, grid=grid,
        in_specs=in_specs,
        out_specs=out_specs,
        out_shape=[s for s, _ in outs],
        scratch_shapes=scratch_shapes,
        compiler_params=_params(semantics, block_bytes, temp_bytes),
    )(*[a for a, _ in ins], *deps)
    return res


def _blind(spec):
    index_map = spec.index_map
    return pl.BlockSpec(spec.block_shape, lambda *a: index_map(*a[:-1]))


def _sds(shape, dtype):
    return jax.ShapeDtypeStruct(shape, dtype)


def _sigmoid(x):
    return jax.nn.sigmoid(x)


def _all_gather_small(name, shard, deps=()):
    n_dep = len(deps)

    def body(src, *rest):
        dst, send_sems, recv_sems, local_sem = rest[n_dep:]
        x, y, c = lax.axis_index("x"), lax.axis_index("y"), lax.axis_index("c")
        me = 4 * x + 2 * y + c
        copies = [pltpu.make_async_copy(src, dst.at[me], local_sem)]
        for k in range(1, N_DEV):
            peer = ((1 - x) if (k & 4) else x, (1 - y) if (k & 2) else y, (1 - c) if (k & 1) else c)
            copies.append(pltpu.make_async_remote_copy(
                src_ref=src, dst_ref=dst.at[me], send_sem=send_sems.at[k - 1], recv_sem=recv_sems.at[k - 1],
                device_id=peer, device_id_type=MESH))
        for cp in copies:
            cp.start()
        for cp in copies:
            cp.wait()

    hbm = pl.BlockSpec(memory_space=pltpu.HBM)
    return pl.pallas_call(
        body, name=name,
        in_specs=[hbm] + [pl.BlockSpec(memory_space=pl.ANY)] * n_dep, out_specs=hbm,
        out_shape=_sds((N_DEV,) + shard.shape, shard.dtype),
        scratch_shapes=[pltpu.SemaphoreType.DMA((N_DEV - 1,)), pltpu.SemaphoreType.DMA((N_DEV - 1,)),
                        pltpu.SemaphoreType.DMA],
    )(shard, *deps)


_HBM = pl.BlockSpec(memory_space=pltpu.HBM)
_SEM = pl.BlockSpec(memory_space=pltpu.SEMAPHORE)
_ANY = pl.BlockSpec(memory_space=pl.ANY)
_EFFECT = pltpu.SideEffectType.DATAFLOW_SIDE_EFFECTING
N_TARGETS = 4


def _mesh_pos():
    return lax.axis_index("x"), lax.axis_index("y"), lax.axis_index("c")


def _chip_peers(x, y, c):
    return [(1 - x, y, c), (x, 1 - y, c), (1 - x, 1 - y, c)]


def _dev_index(pos):
    return 4 * pos[0] + 2 * pos[1] + pos[2]


def _hbm_like(a):
    return pltpu.HBM(a.shape, a.dtype)


def _place_shard(name, w, out_dtype, me, tr, deps=()):
    r, c = w.shape
    n_dep = len(deps)

    def body(me_ref, w_ref, *rest):
        rest[n_dep][...] = w_ref[...].astype(out_dtype)

    grid_spec = pltpu.PrefetchScalarGridSpec(
        num_scalar_prefetch=1, grid=(r // tr,),
        in_specs=[pl.BlockSpec((tr, c), lambda i, me_ref: (i, 0))] + [_ANY] * n_dep,
        out_specs=pl.BlockSpec((None, tr, c), lambda i, me_ref: (me_ref[0], i, 0)))
    return pl.pallas_call(
        body, name=name, grid_spec=grid_spec, out_shape=_sds((N_DEV, r, c), out_dtype),
        compiler_params=_params(("parallel",), tr * c * 6, tr * c * 4),
    )(me, w, *deps)


def _gather_start(name, lands):
    n = len(lands)

    def body(*refs):
        bufs = refs[:n]
        send, recv = refs[n], refs[n + 1]
        token = refs[-1]
        x, y, c = _mesh_pos()
        me = _dev_index((x, y, c))
        targets = [(x, y, 1 - c)] + _chip_peers(x, y, c)
        for w in range(n):
            for k, to in enumerate(targets):
                pltpu.make_async_remote_copy(
                    src_ref=bufs[w].at[me], dst_ref=bufs[w].at[me],
                    send_sem=send.at[N_TARGETS * w + k], recv_sem=recv.at[N_TARGETS * w + k],
                    device_id=to, device_id_type=MESH).start()
        token[...] = jnp.zeros_like(token)

    sems = pltpu.SemaphoreType.DMA((N_TARGETS * n,))
    outs = pl.pallas_call(
        body, name=name,
        in_specs=[_HBM] * n, out_specs=[_SEM, _SEM] + [_HBM] * n + [_token_spec()],
        out_shape=[sems, sems] + [_hbm_like(a) for a in lands] + [_sds((8, LANES), F32)],
        input_output_aliases={i: 2 + i for i in range(n)},
        compiler_params=pltpu.CompilerParams(has_side_effects=_EFFECT),
    )(*lands)
    return outs[0], outs[1], list(outs[2:2 + n]), outs[-1]


def _gather_wait(name, positions, send, recv, lands, after):
    m = len(positions)

    def body(*refs):
        bufs = refs[:m]
        send_sems, recv_sems = refs[m], refs[m + 1]
        x, y, c = _mesh_pos()
        me = _dev_index((x, y, c))
        sources = [(x, y, 1 - c)] + _chip_peers(x, y, c)
        for j, w in enumerate(positions):
            for k, frm in enumerate(sources):
                cp = pltpu.make_async_remote_copy(
                    src_ref=bufs[j].at[me], dst_ref=bufs[j].at[_dev_index(frm)],
                    send_sem=send_sems.at[N_TARGETS * w + k], recv_sem=recv_sems.at[N_TARGETS * w + k],
                    device_id=frm, device_id_type=MESH)
                cp.wait_send()
                cp.wait_recv()

    outs = pl.pallas_call(
        body, name=name,
        in_specs=[_HBM] * m + [_SEM, _SEM, _ANY], out_specs=[_HBM] * m,
        out_shape=[_hbm_like(a) for a in lands],
        input_output_aliases={i: i for i in range(m)},
        compiler_params=pltpu.CompilerParams(has_side_effects=_EFFECT),
    )(*lands, send, recv, after)
    return list(outs)


def _forward_to_sibling(name, lands):
    m = len(lands)

    def body(*refs):
        copies = _forward_copies(refs[m:2 * m], refs[2 * m], refs[2 * m + 1])
        for cp in copies:
            cp.start()
        for cp in copies:
            cp.wait()

    outs = pl.pallas_call(
        body, name=name,
        in_specs=[_HBM] * m, out_specs=[_HBM] * m,
        out_shape=[_sds(a.shape, a.dtype) for a in lands],
        input_output_aliases={i: i for i in range(m)},
        scratch_shapes=[pltpu.SemaphoreType.DMA((3 * m,)), pltpu.SemaphoreType.DMA((3 * m,))],
    )(*lands)
    return list(outs)


def _forward_copies(bufs, send, recv):
    x, y, c = _mesh_pos()
    copies = []
    for j, buf in enumerate(bufs):
        for k, chip in enumerate(_chip_peers(x, y, c)):
            block = buf.at[_dev_index(chip)]
            copies.append(pltpu.make_async_remote_copy(
                src_ref=block, dst_ref=block, send_sem=send.at[3 * j + k], recv_sem=recv.at[3 * j + k],
                device_id=(x, y, 1 - c), device_id_type=MESH))
    return copies


def _forward_start(name, lands):
    m = len(lands)

    def body(*refs):
        for cp in _forward_copies(refs[:m], refs[m], refs[m + 1]):
            cp.start()

    sems = pltpu.SemaphoreType.DMA((3 * m,))
    outs = pl.pallas_call(
        body, name=name,
        in_specs=[_HBM] * m, out_specs=[_SEM, _SEM] + [_HBM] * m,
        out_shape=[sems, sems] + [_hbm_like(a) for a in lands],
        input_output_aliases={i: 2 + i for i in range(m)},
        compiler_params=pltpu.CompilerParams(has_side_effects=_EFFECT),
    )(*lands)
    return outs[0], outs[1], list(outs[2:])


def _forward_wait(name, send, recv, lands, after):
    m = len(lands)

    def body(*refs):
        for cp in _forward_copies(refs[:m], refs[m], refs[m + 1]):
            cp.wait_send()
            cp.wait_recv()

    outs = pl.pallas_call(
        body, name=name,
        in_specs=[_HBM] * m + [_SEM, _SEM, _ANY], out_specs=[_HBM] * m,
        out_shape=[_hbm_like(a) for a in lands],
        input_output_aliases={i: i for i in range(m)},
        compiler_params=pltpu.CompilerParams(has_side_effects=_EFFECT),
    )(*lands, send, recv, after)
    return list(outs)


def _token_spec():
    return pl.BlockSpec(memory_space=pltpu.VMEM)


def _pair_start(name, stacks, lands, deps=()):
    n = len(stacks)
    n_dep = len(deps)

    def body(*refs):
        srcs, dsts = refs[:n], refs[n:2 * n]
        send, recv = refs[2 * n + n_dep], refs[2 * n + n_dep + 1]
        token = refs[-1]
        x, y, c = _mesh_pos()
        for w in range(n):
            for chip in range(4):
                pltpu.make_async_remote_copy(
                    src_ref=srcs[w].at[chip, 1 - c], dst_ref=dsts[w].at[chip],
                    send_sem=send.at[4 * w + chip], recv_sem=recv.at[4 * w + chip],
                    device_id=(x, y, 1 - c), device_id_type=MESH).start()
        token[...] = jnp.zeros_like(token)

    sems = pltpu.SemaphoreType.DMA((4 * n,))
    outs = pl.pallas_call(
        body, name=name,
        in_specs=[_HBM] * (2 * n) + [_ANY] * n_dep, out_specs=[_SEM, _SEM] + [_HBM] * (2 * n) + [_token_spec()],
        out_shape=[sems, sems] + [_hbm_like(a) for a in stacks] + [_hbm_like(a) for a in lands] + [_sds((8, LANES), F32)],
        input_output_aliases={i: 2 + i for i in range(2 * n)},
        compiler_params=pltpu.CompilerParams(has_side_effects=_EFFECT),
    )(*stacks, *lands, *deps)
    return outs[0], outs[1], list(outs[2:2 + n]), list(outs[2 + n:2 + 2 * n]), outs[-1]


def _pair_wait(name, send, recv, stacks, lands, after):
    n = len(stacks)

    def body(*refs):
        srcs, dsts = refs[:n], refs[n:2 * n]
        send_sems, recv_sems = refs[2 * n], refs[2 * n + 1]
        x, y, c = _mesh_pos()
        for w in range(n):
            for chip in range(4):
                cp = pltpu.make_async_remote_copy(
                    src_ref=srcs[w].at[chip, 1 - c], dst_ref=dsts[w].at[chip],
                    send_sem=send_sems.at[4 * w + chip], recv_sem=recv_sems.at[4 * w + chip],
                    device_id=(x, y, 1 - c), device_id_type=MESH)
                cp.wait_send()
                cp.wait_recv()

    outs = pl.pallas_call(
        body, name=name,
        in_specs=[_HBM] * (2 * n) + [_SEM, _SEM, _ANY], out_specs=[_HBM] * (2 * n),
        out_shape=[_hbm_like(a) for a in stacks] + [_hbm_like(a) for a in lands],
        input_output_aliases={i: i for i in range(2 * n)},
        compiler_params=pltpu.CompilerParams(has_side_effects=_EFFECT),
    )(*stacks, *lands, send, recv, after)
    return list(outs[:n]), list(outs[n:])


def _pair_add(name, stack, land, place, tr):
    _, _, r, c = stack.shape

    def body(place_ref, a_ref, b_ref, sums_ref, slots_ref):
        total = (a_ref[...].astype(F32) + b_ref[...].astype(F32)).astype(BF16)
        sums_ref[...] = total

        @pl.when(pl.program_id(1) == place_ref[1])
        def _():
            slots_ref[...] = total

    grid_spec = pltpu.PrefetchScalarGridSpec(
        num_scalar_prefetch=1, grid=(r // tr, 4),
        in_specs=[pl.BlockSpec((None, None, tr, c), lambda i, k, place_ref: (k, place_ref[0], i, 0)),
                  pl.BlockSpec((None, tr, c), lambda i, k, place_ref: (k, i, 0))],
        out_specs=[pl.BlockSpec((None, tr, c), lambda i, k, place_ref: (k, i, 0)),
                   pl.BlockSpec((None, tr, c), lambda i, k, place_ref: (place_ref[1], i, 0))])
    return pl.pallas_call(
        body, name=name, grid_spec=grid_spec, out_shape=[_sds((4, r, c), BF16)] * 2,
        compiler_params=_params(("parallel", "arbitrary"), 4 * tr * c * 2, 3 * tr * c * 4),
    )(place, stack, land)


def _sibling_copies(srcs, dsts, send, recv):
    x, y, c = _mesh_pos()
    copies = []
    for w in range(len(srcs)):
        if len(srcs[w].shape) == 2:
            pairs = [(srcs[w], dsts[w])]
        else:
            nb = srcs[w].shape[2] // 4
            pairs = []
            for k in range(4):
                dev = 2 * k + 1 - c
                col = pl.multiple_of((dev % 4) * nb, LANES)
                pairs.append((srcs[w].at[dev // 4, :, pl.ds(col, nb)], dsts[w].at[k]))
        for src, dst in pairs:
            q = len(copies)
            copies.append(pltpu.make_async_remote_copy(
                src_ref=src, dst_ref=dst, send_sem=send.at[q], recv_sem=recv.at[q],
                device_id=(x, y, 1 - c), device_id_type=MESH))
    return copies


def _sibling_start(name, srcs):
    n = len(srcs)
    lands = [lax.empty(a.shape if a.ndim == 2 else (4, a.shape[1], a.shape[2] // 4), a.dtype) for a in srcs]

    def body(*refs):
        for cp in _sibling_copies(refs[:n], refs[n:2 * n], refs[2 * n], refs[2 * n + 1]):
            cp.start()
        token = refs[-1]
        token[...] = jnp.zeros_like(token)

    sems = pltpu.SemaphoreType.DMA((sum(1 if a.ndim == 2 else 4 for a in srcs),))
    outs = pl.pallas_call(
        body, name=name,
        in_specs=[_HBM] * (2 * n), out_specs=[_SEM, _SEM] + [_HBM] * (2 * n) + [_token_spec()],
        out_shape=[sems, sems] + [_hbm_like(a) for a in srcs] + [_hbm_like(a) for a in lands] + [_sds((8, LANES), F32)],
        input_output_aliases={i: 2 + i for i in range(2 * n)},
        compiler_params=pltpu.CompilerParams(has_side_effects=_EFFECT),
    )(*srcs, *lands)
    return outs[0], outs[1], list(outs[2:2 + n]), list(outs[2 + n:2 + 2 * n]), outs[-1]


def _sibling_wait(name, send, recv, srcs, lands, after):
    n = len(srcs)

    def body(*refs):
        for cp in _sibling_copies(refs[:n], refs[n:2 * n], refs[2 * n], refs[2 * n + 1]):
            cp.wait_send()
            cp.wait_recv()

    outs = pl.pallas_call(
        body, name=name,
        in_specs=[_HBM] * (2 * n) + [_SEM, _SEM, _ANY], out_specs=[_HBM] * (2 * n),
        out_shape=[_hbm_like(a) for a in srcs] + [_hbm_like(a) for a in lands],
        input_output_aliases={i: i for i in range(2 * n)},
        compiler_params=pltpu.CompilerParams(has_side_effects=_EFFECT),
    )(*srcs, *lands, send, recv, after)
    return list(outs[:n]), list(outs[n:])


def _chip_start(name, parts, lands, deps=(), own=False):
    n = len(parts)
    n_dep = len(deps)

    def body(*refs):
        srcs, dsts = refs[:n], refs[n:2 * n]
        send, recv = refs[2 * n + n_dep], refs[2 * n + n_dep + 1]
        token = refs[2 * n + n_dep + 2 + 2 * n]
        x, y, c = _mesh_pos()
        if own:
            local = refs[-1]
            for w in range(n):
                cp = pltpu.make_async_copy(srcs[w].at[2 * x + y], dsts[w].at[2 * x + y], local.at[w])
                cp.start()
                cp.wait()
        for w in range(n):
            for k, to in enumerate(_chip_peers(x, y, c)):
                pltpu.make_async_remote_copy(
                    src_ref=srcs[w].at[2 * to[0] + to[1]], dst_ref=dsts[w].at[2 * x + y],
                    send_sem=send.at[3 * w + k], recv_sem=recv.at[3 * w + k],
                    device_id=to, device_id_type=MESH).start()
        token[...] = jnp.zeros_like(token)

    sems = pltpu.SemaphoreType.DMA((3 * n,))
    outs = pl.pallas_call(
        body, name=name,
        in_specs=[_HBM] * (2 * n) + [_ANY] * n_dep, out_specs=[_SEM, _SEM] + [_HBM] * (2 * n) + [_token_spec()],
        out_shape=[sems, sems] + [_hbm_like(a) for a in parts] + [_hbm_like(a) for a in lands] + [_sds((8, LANES), F32)],
        input_output_aliases={i: 2 + i for i in range(2 * n)},
        scratch_shapes=[pltpu.SemaphoreType.DMA((n,))] if own else [],
        compiler_params=pltpu.CompilerParams(has_side_effects=_EFFECT),
    )(*parts, *lands, *deps)
    return outs[0], outs[1], list(outs[2:2 + n]), list(outs[2 + n:2 + 2 * n]), outs[-1]


def _chip_wait(name, send, recv, parts, lands, after):
    n = len(parts)

    def body(*refs):
        srcs, dsts = refs[:n], refs[n:2 * n]
        send_sems, recv_sems = refs[2 * n], refs[2 * n + 1]
        x, y, c = _mesh_pos()
        for w in range(n):
            for k, frm in enumerate(_chip_peers(x, y, c)):
                chip = 2 * frm[0] + frm[1]
                cp = pltpu.make_async_remote_copy(
                    src_ref=srcs[w].at[chip], dst_ref=dsts[w].at[chip],
                    send_sem=send_sems.at[3 * w + k], recv_sem=recv_sems.at[3 * w + k],
                    device_id=frm, device_id_type=MESH)
                cp.wait_send()
                cp.wait_recv()

    outs = pl.pallas_call(
        body, name=name,
        in_specs=[_HBM] * (2 * n) + [_SEM, _SEM, _ANY], out_specs=[_HBM] * (2 * n),
        out_shape=[_hbm_like(a) for a in parts] + [_hbm_like(a) for a in lands],
        input_output_aliases={i: i for i in range(2 * n)},
        compiler_params=pltpu.CompilerParams(has_side_effects=_EFFECT),
    )(*parts, *lands, send, recv, after)
    return list(outs[:n]), list(outs[n:])


def _row_tile(t):
    return min(t, 256)


def _rms_fwd(name, x, g):
    t, d = x.shape
    tm = _row_tile(t)

    def epilogue(_, ins, outs):
        xv = ins[0][...]
        r = lax.rsqrt(jnp.mean(xv * xv, axis=-1, keepdims=True) + RMS_EPS)
        outs[0][...] = (xv * r * ins[1][...]).astype(BF16)

    row = pl.BlockSpec((tm, d), lambda i, j, k: (i, 0))
    vec = pl.BlockSpec((1, d), lambda i, j, k: (0, 0))
    return _fused(name, (t // tm, 1, 1), [(x, row), (g, vec)], [(_sds((t, d), BF16), row)], [], epilogue,
                  temp_bytes=4 * tm * d * 4)[0]


def _rms_bwd(name, x, g, dh, resid, deps=(), with_bf16=False):
    t, d = x.shape
    tm = _row_tile(t)

    def epilogue(_, ins, outs):
        xv, gv, dhv = ins[0][...], ins[1][...], ins[2][...]
        r = lax.rsqrt(jnp.mean(xv * xv, axis=-1, keepdims=True) + RMS_EPS)
        xh = xv * r
        u = dhv * gv
        dot = jnp.mean(u * xh, axis=-1, keepdims=True)
        dx = ins[3][...] + r * (u - xh * dot)
        outs[0][...] = dx
        if with_bf16:
            outs[2][...] = dx.astype(BF16)

        @pl.when(pl.program_id(0) == 0)
        def _():
            outs[1][...] = jnp.zeros_like(outs[1])

        outs[1][0:1, :] += jnp.sum(dhv * xh, axis=0, keepdims=True)

    row = pl.BlockSpec((tm, d), lambda i, j, k: (i, 0))
    vec = pl.BlockSpec((1, d), lambda i, j, k: (0, 0))
    acc = pl.BlockSpec((8, d), lambda i, j, k: (0, 0))
    outs = [(_sds((t, d), F32), row), (_sds((8, d), F32), acc)] + ([(_sds((t, d), BF16), row)] if with_bf16 else [])
    return _fused(name, (t // tm, 1, 1), [(x, row), (g, vec), (dh, row), (resid, row)], outs, [], epilogue,
                  temp_bytes=6 * tm * d * 4, semantics=("arbitrary", "arbitrary", "arbitrary"), deps=deps)


def _ffn_up(name, h, wgu, parity=None, into=None):
    t, d = h.shape
    nb = wgu.shape[2]
    f = 4 * nb
    tm = min(t, 512)

    def body(h_ref, wg_ref, wu_ref, gu_ref, a_ref):
        hv = h_ref[...]
        for c0 in range(0, nb, MXU_COLS):
            cs = slice(c0, min(c0 + MXU_COLS, nb))
            g = jnp.dot(hv, wg_ref[:, cs], preferred_element_type=F32)
            u = jnp.dot(hv, wu_ref[:, cs], preferred_element_type=F32)
            gu_ref[0, :, cs] = g.astype(BF16)
            gu_ref[1, :, cs] = u.astype(BF16)
            a_ref[:, cs] = (g * _sigmoid(g) * u).astype(BF16)

    blocks = tm * d * 2 + 2 * d * nb * 2 + 3 * tm * nb * 2
    params = _params(("parallel", "parallel"), blocks, 8 * tm * MXU_COLS * 4)
    out_shape = [_sds((2, t, f), BF16), _sds((t, f), BF16)]
    if parity is None:
        return pl.pallas_call(
            body, name=name, grid=(4, t // tm),
            in_specs=[pl.BlockSpec((tm, d), lambda j, i: (i, 0)),
                      pl.BlockSpec((None, d, nb), lambda j, i: (j, 0, 0)),
                      pl.BlockSpec((None, d, nb), lambda j, i: (j + 4, 0, 0))],
            out_specs=[pl.BlockSpec((2, tm, nb), lambda j, i: (0, i, j)),
                       pl.BlockSpec((tm, nb), lambda j, i: (i, j))],
            out_shape=out_shape, compiler_params=params,
        )(h, wgu, wgu)

    def half_body(parity_ref, h_ref, wg_ref, wu_ref, *rest):
        body(h_ref, wg_ref, wu_ref, rest[-2], rest[-1])

    n_pass = 0 if into is None else 2
    grid_spec = pltpu.PrefetchScalarGridSpec(
        num_scalar_prefetch=1, grid=(2, t // tm),
        in_specs=[pl.BlockSpec((tm, d), lambda jj, i, p: (i, 0)),
                  pl.BlockSpec((None, d, nb), lambda jj, i, p: (2 * jj + p[0], 0, 0)),
                  pl.BlockSpec((None, d, nb), lambda jj, i, p: (2 * jj + p[0] + 4, 0, 0))] + [_ANY] * n_pass,
        out_specs=[pl.BlockSpec((2, tm, nb), lambda jj, i, p: (0, i, 2 * jj + p[0])),
                   pl.BlockSpec((tm, nb), lambda jj, i, p: (i, 2 * jj + p[0]))])
    return pl.pallas_call(
        half_body, name=name, grid_spec=grid_spec, out_shape=out_shape,
        input_output_aliases={} if into is None else {4: 0, 5: 1}, compiler_params=params,
    )(parity, h, wgu, wgu, *(into or ()))


def _ffn_down(name, a, wd, x, target=None):
    t, f = a.shape
    d = wd.shape[1]
    tm = min(t, 512)
    tn = min(d, 1024)
    blk = pl.BlockSpec((tm, tn), lambda j, i, k: (i, j))
    ins = [(a, pl.BlockSpec((tm, f), lambda j, i, k: (i, 0))), (wd, pl.BlockSpec((f, tn), lambda j, i, k: (0, j))), (x, blk)]

    if target is None:
        def epilogue(acc, ins, outs):
            outs[0][...] = ins[2][...] + 0.5 * acc

        return _fused(name, (d // tn, t // tm, 1), ins, [(_sds((t, d), F32), blk)],
                      [(0, 1, NN)], epilogue, temp_bytes=2 * tm * tn * 4)[0]

    def epilogue(acc, ins, outs):
        e = ins[2][...] + 0.5 * acc - ins[3][...]
        outs[0][...] = e * (1.0 / d)
        outs[2][...] = (e * (1.0 / d)).astype(BF16)

        @pl.when((pl.program_id(0) == 0) & (pl.program_id(1) == 0))
        def _():
            outs[1][...] = jnp.zeros_like(outs[1])

        part = jnp.sum(jnp.sum(e * e, axis=1, keepdims=True), axis=0, keepdims=True)
        outs[1][...] += jnp.broadcast_to(part, outs[1].shape)

    return _fused(name, (d // tn, t // tm, 1), ins + [(target, blk)],
                  [(_sds((t, d), F32), blk), (_sds((8, LANES), F32), pl.BlockSpec((8, LANES), lambda j, i, k: (0, 0))),
                   (_sds((t, d), BF16), blk)],
                  [(0, 1, NN)], epilogue, temp_bytes=3 * tm * tn * 4,
                  semantics=("arbitrary", "arbitrary", "arbitrary"))


def _ffn_bwd_act(name, dy, wd, gu, deps=()):
    t, d = dy.shape
    f = wd.shape[0]
    nb = f // 4
    tm = min(t, 512)

    def body(dy_ref, wd_ref, gu_ref, *rest):
        dgu_ref, a_ref = rest[-2], rest[-1]
        dyv = dy_ref[...].astype(BF16)
        for c0 in range(0, nb, MXU_COLS):
            cs = slice(c0, min(c0 + MXU_COLS, nb))
            da = 0.5 * lax.dot_general(dyv, wd_ref[cs, :], NT, preferred_element_type=F32)
            g = gu_ref[0, :, cs].astype(F32)
            u = gu_ref[1, :, cs].astype(F32)
            s = _sigmoid(g)
            silu = g * s
            dgu_ref[0, :, cs] = (da * u * (s * (1.0 + g * (1.0 - s)))).astype(BF16)
            dgu_ref[1, :, cs] = (da * silu).astype(BF16)
            a_ref[:, cs] = (silu * u).astype(BF16)

    blocks = tm * d * 4 + nb * d * 2 + 5 * tm * nb * 2
    return pl.pallas_call(
        body, name=name, grid=(4, t // tm),
        in_specs=[pl.BlockSpec((tm, d), lambda j, i: (i, 0)),
                  pl.BlockSpec((nb, d), lambda j, i: (j, 0)),
                  pl.BlockSpec((2, tm, nb), lambda j, i: (0, i, j))] + [_ANY] * len(deps),
        out_specs=[pl.BlockSpec((2, tm, nb), lambda j, i: (0, i, j)), pl.BlockSpec((tm, nb), lambda j, i: (i, j))],
        out_shape=[_sds((2, t, f), BF16), _sds((t, f), BF16)],
        compiler_params=_params(("parallel", "parallel"), blocks, tm * d * 2 + 8 * tm * MXU_COLS * 4),
    )(dy, wd, gu, *deps)


def _ffn_bwd_dwd(name, a, dy, deps=(), side=None):
    t, f = a.shape
    d = dy.shape[1]
    tm = f // 4
    tn = min(d, 512)

    def epilogue(acc, ins, outs):
        outs[0][...] = (0.5 * acc).astype(BF16)

    return _fused(name, (4, d // tn, 1),
                  [(a, pl.BlockSpec((t, tm), lambda i, j, k: (0, i))),
                   (dy, pl.BlockSpec((t, tn), lambda i, j, k: (0, j)))],
                  [(_sds((f, d), BF16), pl.BlockSpec((tm, tn), lambda i, j, k: (i, j)))],
                  [(0, 1, TN)], epilogue, temp_bytes=t * tn * 2 + 2 * tm * tn * 4, deps=deps, side=side)


def _ffn_bwd_dh(name, dgu, wgu, deps=(), side=None):
    _, t, f = dgu.shape
    d, nb = wgu.shape[1], wgu.shape[2]
    tm = min(t, 512)

    def products(ins):
        return (lax.dot_general(ins[0][:, 0:nb], ins[1][0], NT, preferred_element_type=F32)
                + lax.dot_general(ins[0][:, nb:2 * nb], ins[1][1], NT, preferred_element_type=F32))

    def epilogue(acc, ins, outs):
        outs[0][...] = acc

    return _fused(name, (t // tm, 1, 4),
                  [(dgu, pl.BlockSpec((None, tm, 2 * nb), lambda i, j, k: (k // 2, i, k % 2))),
                   (wgu, pl.BlockSpec((2, d, nb), lambda i, j, k: (k, 0, 0)))],
                  [(_sds((t, d), F32), pl.BlockSpec((tm, d), lambda i, j, k: (i, 0)))],
                  products, epilogue, nk=4, acc_shape=(tm, d), temp_bytes=tm * d * 4, deps=deps, side=side)


def _ffn_bwd_dwgu(name, h, dgu, deps=(), side=None, rows=None):
    t, d = h.shape
    nb = dgu.shape[2] // 4
    tm = min(d, 512)
    row0, nrows = rows if rows is not None else (0, d)
    j0 = row0 // tm

    def epilogue(acc, ins, outs):
        outs[0][...] = acc.astype(BF16)

    return _fused(name, (N_DEV, nrows // tm, 1),
                  [(h, pl.BlockSpec((t, tm), lambda i, j, k: (0, j0 + j))),
                   (dgu, pl.BlockSpec((None, t, nb), lambda i, j, k: (i // 4, 0, i % 4)))],
                  [(_sds((N_DEV, nrows, nb), BF16), pl.BlockSpec((None, tm, nb), lambda i, j, k: (i, j, 0)))],
                  [(0, 1, TN)], epilogue, temp_bytes=2 * tm * nb * 4, deps=deps, side=side)


def _ffn_bwd_dwgu_pair_rows(name, h, h_sib, dgu, dgu_sib, place, rows, deps=(), side=None):
    t, d = h.shape
    nb = dgu.shape[2] // 4
    tm = min(d, 512)
    row0, nrows = rows
    j0 = row0 // tm

    def products(ins):
        return lax.cond(pl.program_id(2) == 0,
                        lambda: lax.dot_general(ins[0][...], ins[2][...], TN, preferred_element_type=F32),
                        lambda: lax.dot_general(ins[1][...], ins[3][...], TN, preferred_element_type=F32))

    def epilogue(acc, ins, outs, place_ref):
        outs[0][...] = acc.astype(BF16)

    def act_map(i, j, k, place_ref):
        return (0, j0 + i)

    def grad_map(i, j, k, place_ref):
        dev = 2 * j + place_ref[0]
        return (dev // 4, 0, dev % 4)

    return _fused(name, (nrows // tm, 4, 2),
                  [(h, pl.BlockSpec((t, tm), act_map)), (h_sib, pl.BlockSpec((t, tm), act_map)),
                   (dgu, pl.BlockSpec((None, t, nb), grad_map)),
                   (dgu_sib, pl.BlockSpec((None, t, nb), lambda i, j, k, place_ref: (j, 0, 0)))],
                  [(_sds((4, nrows, nb), BF16), pl.BlockSpec((None, tm, nb), lambda i, j, k, place_ref: (j, i, 0)))],
                  products, epilogue, nk=2, acc_shape=(tm, nb), temp_bytes=tm * nb * 4,
                  semantics=("parallel", "arbitrary", "arbitrary"), deps=deps, side=side, prefetch=place)


def _ffn_bwd_dwgu_pair(name, h, h_sib, dgu, dgu_sib, place):
    t, d = h.shape
    nb = dgu.shape[2] // 4
    tm = min(d, 512)

    def body(place_ref, h_ref, hs_ref, g_ref, gs_ref, sums_ref, slots_ref):
        acc = lax.dot_general(h_ref[...], g_ref[...], TN, preferred_element_type=F32)
        acc += lax.dot_general(hs_ref[...], gs_ref[...], TN, preferred_element_type=F32)
        total = acc.astype(BF16)
        sums_ref[...] = total

        @pl.when(pl.program_id(1) == place_ref[1])
        def _():
            slots_ref[...] = total

    def act_map(i, k, place_ref):
        return (0, i)

    def grad_map(i, k, place_ref):
        dev = 2 * k + place_ref[0]
        return (dev // 4, 0, dev % 4)

    grid_spec = pltpu.PrefetchScalarGridSpec(
        num_scalar_prefetch=1, grid=(d // tm, 4),
        in_specs=[pl.BlockSpec((t, tm), act_map), pl.BlockSpec((t, tm), act_map),
                  pl.BlockSpec((None, t, nb), grad_map),
                  pl.BlockSpec((None, t, nb), lambda i, k, place_ref: (k, 0, 0))],
        out_specs=[pl.BlockSpec((None, tm, nb), lambda i, k, place_ref: (k, i, 0)),
                   pl.BlockSpec((None, tm, nb), lambda i, k, place_ref: (place_ref[1], i, 0))])
    blocks = 2 * t * tm * 2 + 2 * t * nb * 2 + 2 * tm * nb * 2
    return pl.pallas_call(
        body, name=name, grid_spec=grid_spec, out_shape=[_sds((4, d, nb), BF16)] * 2,
        compiler_params=_params(("parallel", "arbitrary"), blocks, 2 * tm * nb * 4),
    )(place, h, h_sib, dgu, dgu_sib)


def _proj(h, w_in):
    t, d = h.shape
    nb = w_in.shape[3]
    tm = min(t, 512)

    def body(h_ref, w_ref, o_ref):
        hv = h_ref[...]
        o_ref[:, 0:nb] = jnp.dot(hv, w_ref[0], preferred_element_type=F32).astype(BF16)
        o_ref[:, nb:2 * nb] = jnp.dot(hv, w_ref[1], preferred_element_type=F32).astype(BF16)

    blocks = tm * d * 2 + 2 * d * nb * 2 + tm * 2 * nb * 4
    return pl.pallas_call(
        body, name="mix_proj", grid=(4, t // tm),
        in_specs=[pl.BlockSpec((tm, d), lambda j, i: (i, 0)),
                  pl.BlockSpec((None, 2, d, nb), lambda j, i: (j, 0, 0, 0))],
        out_specs=pl.BlockSpec((tm, 2 * nb), lambda j, i: (i, j)),
        out_shape=_sds((t, N_DEV * nb), BF16),
        compiler_params=_params(("parallel", "parallel"), blocks, 2 * tm * nb * 4),
    )(h, w_in)


def _shift_rows(u, k):
    t = u.shape[0]
    rolled = pltpu.roll(u, k % t, axis=0)
    row = lax.broadcasted_iota(jnp.int32, u.shape, 0)
    keep = (row >= k) if k > 0 else (row < t + k)
    return jnp.where(keep, rolled, 0.0)


def _conv_fwd(proj, conv_w):
    t = proj.shape[0]
    cw = conv_w.shape[1]
    tc = min(cw, 256)
    nc = cw // tc

    def epilogue(_, ins, outs):
        u = ins[2][...].astype(F32) * ins[0][...].astype(F32)
        w = ins[3][...]
        y = u * w[2:3, :] + _shift_rows(u, 1) * w[1:2, :] + _shift_rows(u, 2) * w[0:1, :]
        outs[0][...] = (ins[1][...].astype(F32) * y).astype(BF16)

    def col(seg):
        return pl.BlockSpec((t, tc), lambda i, j, k: (0, seg * nc + i))

    return _fused("conv_fwd", (nc, 1, 1),
                  [(proj, col(0)), (proj, col(1)), (proj, col(2)),
                   (conv_w, pl.BlockSpec((8, tc), lambda i, j, k: (0, i)))],
                  [(_sds((t, cw), BF16), pl.BlockSpec((t, tc), lambda i, j, k: (0, i)))],
                  [], epilogue, temp_bytes=6 * t * tc * 4)[0]


def _conv_bwd(proj, conv_w, dca, deps=()):
    t = proj.shape[0]
    cw = conv_w.shape[1]
    tc = min(cw, 256)
    nc = cw // tc

    def epilogue(_, ins, outs):
        xc, bg, cg = ins[0][...].astype(F32), ins[1][...].astype(F32), ins[2][...].astype(F32)
        w, dc = ins[3][...], ins[4][...]
        u = cg * xc
        u1, u2 = _shift_rows(u, 1), _shift_rows(u, 2)
        y = u * w[2:3, :] + u1 * w[1:2, :] + u2 * w[0:1, :]
        dconv = dc * bg
        du = dconv * w[2:3, :] + _shift_rows(dconv, -1) * w[1:2, :] + _shift_rows(dconv, -2) * w[0:1, :]
        outs[0][0] = (du * cg).astype(BF16)
        outs[0][1] = (dc * y).astype(BF16)
        outs[0][2] = (du * xc).astype(BF16)
        outs[1][...] = jnp.zeros_like(outs[1])
        outs[1][0:1, :] = jnp.sum(dconv * u2, axis=0, keepdims=True)
        outs[1][1:2, :] = jnp.sum(dconv * u1, axis=0, keepdims=True)
        outs[1][2:3, :] = jnp.sum(dconv * u, axis=0, keepdims=True)

    def col(seg):
        return pl.BlockSpec((t, tc), lambda i, j, k: (0, seg * nc + i))

    own = pl.BlockSpec((t, tc), lambda i, j, k: (0, i))
    wspec = pl.BlockSpec((8, tc), lambda i, j, k: (0, i))
    return _fused("conv_bwd", (nc, 1, 1),
                  [(proj, col(0)), (proj, col(1)), (proj, col(2)), (conv_w, wspec), (dca, own)],
                  [(_sds((3, t, cw), BF16), pl.BlockSpec((3, t, tc), lambda i, j, k: (0, 0, i))),
                   (_sds((8, cw), F32), wspec)],
                  [], epilogue, temp_bytes=10 * t * tc * 4, deps=deps)


def _split3(x):
    hi = x.astype(BF16)
    r1 = x - hi.astype(F32)
    mid = r1.astype(BF16)
    lo = (r1 - mid.astype(F32)).astype(BF16)
    return hi, mid, lo


def _head_selector(width):
    r = lax.broadcasted_iota(jnp.int32, (width, LANES), 0)
    c = lax.broadcasted_iota(jnp.int32, (width, LANES), 1)
    return (lax.shift_right_logical(r, 6) == c).astype(BF16)


def _head_sum(x, sel):
    return sum(jnp.dot(p, sel, preferred_element_type=F32) for p in _split3(x))


def _head_bcast(r, sel):
    return sum(lax.dot_general(p, sel, NT, preferred_element_type=F32) for p in _split3(r))


def _rope(x, c, sa, sb):
    n = x.shape[1]
    return x * c + pltpu.roll(x, n - ROT_DIM // 2, axis=1) * sa + pltpu.roll(x, ROT_DIM // 2, axis=1) * sb


def _rope_t(d, c, sa, sb):
    n = d.shape[1]
    return d * c + pltpu.roll(d * sa, ROT_DIM // 2, axis=1) + pltpu.roll(d * sb, n - ROT_DIM // 2, axis=1)


def _tile_lanes(tab, width):
    return tab if width == tab.shape[1] else jnp.tile(tab, (1, width // tab.shape[1]))


def _qk_prep(proj, gq, gk, rope_tabs, cw, kw):
    t = proj.shape[0]
    tm = _row_tile(t)

    def epilogue(_, ins, outs):
        c, sa, sb = ins[5][...], ins[6][...], ins[7][...]
        for src, gain, dst, width in ((0, 3, 0, cw), (1, 4, 1, kw)):
            xv = ins[src][...].astype(F32)
            sel = _head_selector(width)
            r = lax.rsqrt(_head_sum(xv * xv, sel) * (1.0 / HEAD_DIM) + RMS_EPS)
            xn = xv * _head_bcast(r, sel) * ins[gain][...]
            outs[dst][...] = _rope(xn, _tile_lanes(c, width), _tile_lanes(sa, width), _tile_lanes(sb, width)).astype(BF16)
        outs[2][...] = ins[2][...].astype(BF16)

    kblk = cw // kw
    tab = pl.BlockSpec((tm, LANES), lambda i, j, k: (i, 0))
    kspec = pl.BlockSpec((tm, kw), lambda i, j, k: (i, 0))
    return _fused("qk_prep", (t // tm, 1, 1),
                  [(proj, pl.BlockSpec((tm, cw), lambda i, j, k: (i, 3))),
                   (proj, pl.BlockSpec((tm, kw), lambda i, j, k: (i, 4 * kblk))),
                   (proj, pl.BlockSpec((tm, kw), lambda i, j, k: (i, 4 * kblk + 1))),
                   (gq, pl.BlockSpec((1, cw), lambda i, j, k: (0, 0))),
                   (gk, pl.BlockSpec((1, kw), lambda i, j, k: (0, 0))),
                   (rope_tabs[0], tab), (rope_tabs[1], tab), (rope_tabs[2], tab)],
                  [(_sds((t, cw), BF16), pl.BlockSpec((tm, cw), lambda i, j, k: (i, 0))),
                   (_sds((t, kw), BF16), kspec), (_sds((t, kw), BF16), kspec)],
                  [], epilogue, temp_bytes=12 * tm * cw * 4)


def _qk_prep_bwd(proj, gq, gk, rope_tabs, dq, dkc, dkp, dvc, dvp, cw, kw):
    t = proj.shape[0]
    tm = BLOCK
    nblk = t // tm

    def epilogue(_, ins, outs):
        c, sa, sb = ins[5][...], ins[6][...], ins[7][...]
        has_next = (pl.program_id(0) < nblk - 1).astype(F32)
        dk = ins[9][...] + has_next * ins[10][...]
        dv = ins[11][...] + has_next * ins[12][...]
        pieces = []
        for src, gain, dval, dst, width in ((0, 3, ins[8][...], 1, cw), (1, 4, dk, 2, kw)):
            xv, gv = ins[src][...].astype(F32), ins[gain][...]
            sel = _head_selector(width)
            r = _head_bcast(lax.rsqrt(_head_sum(xv * xv, sel) * (1.0 / HEAD_DIM) + RMS_EPS), sel)
            xh = xv * r
            dxn = _rope_t(dval, _tile_lanes(c, width), _tile_lanes(sa, width), _tile_lanes(sb, width))
            u = dxn * gv
            dot = _head_bcast(_head_sum(u * xh, sel), sel) * (1.0 / HEAD_DIM)
            pieces.append((r * (u - xh * dot)).astype(BF16))
            ri = lax.broadcasted_iota(jnp.int32, (width, LANES), 0)
            ci = lax.broadcasted_iota(jnp.int32, (width, LANES), 1)
            fold = (lax.bitwise_and(ri, HEAD_DIM - 1) == ci).astype(BF16)
            colsum = jnp.broadcast_to(jnp.sum(dxn * xh, axis=0, keepdims=True), (8, width))
            part = sum(jnp.dot(p, fold, preferred_element_type=F32) for p in _split3(colsum))

            @pl.when(pl.program_id(0) == 0)
            def _():
                outs[dst][...] = jnp.zeros_like(outs[dst])

            outs[dst][0:1, :] += part[0:1, :]
        outs[0][:, 0:cw] = pieces[0]
        outs[0][:, cw:cw + kw] = pieces[1]
        outs[0][:, cw + kw:cw + 2 * kw] = dv.astype(BF16)

    kblk = cw // kw
    tab = pl.BlockSpec((tm, LANES), lambda i, j, k: (i, 0))
    kcur = pl.BlockSpec((tm, kw), lambda i, j, k: (i, 0))
    knext = pl.BlockSpec((tm, kw), lambda i, j, k: (jnp.minimum(i + 1, nblk - 1), 0))
    acc = pl.BlockSpec((8, LANES), lambda i, j, k: (0, 0))
    return _fused("qk_prep_bwd", (nblk, 1, 1),
                  [(proj, pl.BlockSpec((tm, cw), lambda i, j, k: (i, 3))),
                   (proj, pl.BlockSpec((tm, kw), lambda i, j, k: (i, 4 * kblk))),
                   (proj, pl.BlockSpec((tm, kw), lambda i, j, k: (i, 4 * kblk + 1))),
                   (gq, pl.BlockSpec((1, cw), lambda i, j, k: (0, 0))),
                   (gk, pl.BlockSpec((1, kw), lambda i, j, k: (0, 0))),
                   (rope_tabs[0], tab), (rope_tabs[1], tab), (rope_tabs[2], tab),
                   (dq, pl.BlockSpec((tm, cw), lambda i, j, k: (i, 0))),
                   (dkc, kcur), (dkp, knext), (dvc, kcur), (dvp, knext)],
                  [(_sds((t, cw + 2 * kw), BF16), pl.BlockSpec((tm, cw + 2 * kw), lambda i, j, k: (i, 0))),
                   (_sds((8, LANES), F32), acc), (_sds((8, LANES), F32), acc)],
                  [], epilogue, temp_bytes=16 * tm * cw * 4, semantics=("arbitrary", "arbitrary", "arbitrary"))


def _attn_mask(n):
    key = lax.broadcasted_iota(jnp.int32, (2 * BLOCK, GROUP * BLOCK), 0)
    qry = lax.bitwise_and(lax.broadcasted_iota(jnp.int32, (2 * BLOCK, GROUP * BLOCK), 1), BLOCK - 1)
    return (key > qry) & (key <= qry + BLOCK) & ((key >= BLOCK) | (n > 0))


def _stack_heads(x, h):
    return jnp.concatenate([x[:, (h * GROUP + g) * HEAD_DIM:(h * GROUP + g + 1) * HEAD_DIM] for g in range(GROUP)], axis=0)


def _softmax_with_sink(q4, k2, sink_ref, h, valid):
    sink = jnp.concatenate([sink_ref[h * GROUP + g:h * GROUP + g + 1, :] for g in range(GROUP)], axis=1)
    s = lax.dot_general(k2, q4, NT, preferred_element_type=F32) * ATTN_SCALE
    s = jnp.where(valid, s, NEG_INF)
    m = jnp.maximum(jnp.max(s, axis=0, keepdims=True), sink)
    p = jnp.exp(s - m)
    es = jnp.exp(sink - m)
    inv = 1.0 / (jnp.sum(p, axis=0, keepdims=True) + es)
    return p * inv, es * inv


def _attn_fwd(qn, kn, vb, sink_rows):
    t, cw = qn.shape
    kw = kn.shape[1]
    nkv = kw // HEAD_DIM

    def body(q_ref, kp_ref, kc_ref, vp_ref, vc_ref, sink_ref, o_ref):
        valid = _attn_mask(pl.program_id(0))
        qv = q_ref[...]
        kp, kc, vp, vc = kp_ref[...], kc_ref[...], vp_ref[...], vc_ref[...]
        outs = []
        for h in range(nkv):
            hs = slice(h * HEAD_DIM, (h + 1) * HEAD_DIM)
            k2 = jnp.concatenate([kp[:, hs], kc[:, hs]], axis=0)
            v2 = jnp.concatenate([vp[:, hs], vc[:, hs]], axis=0)
            pn, _ = _softmax_with_sink(_stack_heads(qv, h), k2, sink_ref, h, valid)
            o4 = lax.dot_general(pn.astype(BF16), v2, TN, preferred_element_type=F32)
            outs += [o4[g * BLOCK:(g + 1) * BLOCK] for g in range(GROUP)]
        o_ref[...] = jnp.concatenate(outs, axis=-1).astype(BF16)

    cur = lambda n: (n, 0)
    prev = lambda n: (jnp.maximum(n - 1, 0), 0)
    return pl.pallas_call(
        body, name="attn_fwd", grid=(t // BLOCK,),
        in_specs=[pl.BlockSpec((BLOCK, cw), cur),
                  pl.BlockSpec((BLOCK, kw), prev), pl.BlockSpec((BLOCK, kw), cur),
                  pl.BlockSpec((BLOCK, kw), prev), pl.BlockSpec((BLOCK, kw), cur),
                  pl.BlockSpec(sink_rows.shape, lambda n: (0, 0))],
        out_specs=pl.BlockSpec((BLOCK, cw), cur),
        out_shape=_sds((t, cw), BF16),
        compiler_params=_params(("parallel",), BLOCK * (cw + 4 * kw) * 2 + BLOCK * cw * 2, 8 << 20),
    )(qn, kn, kn, vb, vb, sink_rows)


def _attn_bwd(qn, kn, vb, sink_rows, do):
    t, cw = qn.shape
    kw = kn.shape[1]
    nkv = kw // HEAD_DIM
    nq = nkv * GROUP

    def body(q_ref, kp_ref, kc_ref, vp_ref, vc_ref, sink_ref, do_ref,
             dq_ref, dkc_ref, dkp_ref, dvc_ref, dvp_ref, dsink_ref):
        n = pl.program_id(0)
        valid = _attn_mask(n)
        qv, dov = q_ref[...], do_ref[...]
        kp, kc, vp, vc = kp_ref[...], kc_ref[...], vp_ref[...], vc_ref[...]
        dqs, dks, dvs, dsinks = [], [], [], []
        for h in range(nkv):
            hs = slice(h * HEAD_DIM, (h + 1) * HEAD_DIM)
            k2 = jnp.concatenate([kp[:, hs], kc[:, hs]], axis=0)
            v2 = jnp.concatenate([vp[:, hs], vc[:, hs]], axis=0)
            q4 = _stack_heads(qv, h)
            dob = _stack_heads(dov, h).astype(BF16)
            pn, psink = _softmax_with_sink(q4, k2, sink_ref, h, valid)
            dpn = lax.dot_general(v2, dob, NT, preferred_element_type=F32)
            dvs.append(jnp.dot(pn.astype(BF16), dob, preferred_element_type=F32))
            delta = jnp.sum(pn * dpn, axis=0, keepdims=True)
            ds = (pn * (dpn - delta) * ATTN_SCALE).astype(BF16)
            dks.append(jnp.dot(ds, q4, preferred_element_type=F32))
            dq4 = lax.dot_general(ds, k2, TN, preferred_element_type=F32)
            dsink4 = -psink * delta
            for g in range(GROUP):
                dqs.append(dq4[g * BLOCK:(g + 1) * BLOCK])
                dsinks.append(jnp.broadcast_to(jnp.sum(dsink4[:, g * BLOCK:(g + 1) * BLOCK], axis=1, keepdims=True), (1, LANES)))
        dq_ref[...] = jnp.concatenate(dqs, axis=-1)
        dkp_ref[...] = jnp.concatenate([d[:BLOCK] for d in dks], axis=-1)
        dkc_ref[...] = jnp.concatenate([d[BLOCK:] for d in dks], axis=-1)
        dvp_ref[...] = jnp.concatenate([d[:BLOCK] for d in dvs], axis=-1)
        dvc_ref[...] = jnp.concatenate([d[BLOCK:] for d in dvs], axis=-1)

        @pl.when(n == 0)
        def _():
            dsink_ref[...] = jnp.zeros_like(dsink_ref)

        dsink_ref[...] += jnp.concatenate(dsinks, axis=0)

    cur = lambda n: (n, 0)
    prev = lambda n: (jnp.maximum(n - 1, 0), 0)
    kspec = pl.BlockSpec((BLOCK, kw), cur)
    return pl.pallas_call(
        body, name="attn_bwd", grid=(t // BLOCK,),
        in_specs=[pl.BlockSpec((BLOCK, cw), cur),
                  pl.BlockSpec((BLOCK, kw), prev), kspec,
                  pl.BlockSpec((BLOCK, kw), prev), kspec,
                  pl.BlockSpec(sink_rows.shape, lambda n: (0, 0)),
                  pl.BlockSpec((BLOCK, cw), cur)],
        out_specs=[pl.BlockSpec((BLOCK, cw), cur), kspec, kspec, kspec, kspec,
                   pl.BlockSpec((nq, LANES), lambda n: (0, 0))],
        out_shape=[_sds((t, cw), F32)] + [_sds((t, kw), F32)] * 4 + [_sds((nq, LANES), F32)],
        compiler_params=_params(("arbitrary",), BLOCK * (cw + 4 * kw) * 2 + 2 * BLOCK * cw * 4 + 4 * BLOCK * kw * 4, 12 << 20),
    )(qn, kn, kn, vb, vb, sink_rows, do)


def _mix_out(ca, o, woc, woa, proj):
    t, cw = ca.shape
    nb = woc.shape[2]
    d = N_DEV * nb
    tm = min(t, 1024)
    ga0 = (3 * cw + cw + 2 * (cw // 4)) // nb

    def body(ca_ref, o_ref, woc_ref, woa_ref, ga_ref, gb_ref, m_ref, ya_ref, yb_ref):
        ya = jnp.dot(ca_ref[...], woc_ref[...], preferred_element_type=F32)
        yb = jnp.dot(o_ref[...], woa_ref[...], preferred_element_type=F32)
        ya_ref[...] = ya.astype(BF16)
        yb_ref[...] = yb.astype(BF16)
        m_ref[...] = (_sigmoid(ga_ref[...].astype(F32)) * ya + _sigmoid(gb_ref[...].astype(F32)) * yb).astype(BF16)

    act = pl.BlockSpec((tm, cw), lambda i, j: (i, 0))
    wsp = pl.BlockSpec((None, cw, nb), lambda i, j: (j, 0, 0))
    osp = pl.BlockSpec((tm, nb), lambda i, j: (i, j))
    blocks = 2 * tm * cw * 2 + 2 * cw * nb * 2 + 2 * tm * nb * 4 + 3 * tm * nb * 2
    return pl.pallas_call(
        body, name="mix_out", grid=(t // tm, N_DEV),
        in_specs=[act, act, wsp, wsp,
                  pl.BlockSpec((tm, nb), lambda i, j: (i, ga0 + j)),
                  pl.BlockSpec((tm, nb), lambda i, j: (i, ga0 + N_DEV + j))],
        out_specs=[osp, osp, osp],
        out_shape=[_sds((t, d), BF16)] * 3,
        compiler_params=_params(("parallel", "parallel"), blocks, 6 * tm * nb * 4),
    )(ca, o, woc, woa, proj, proj)


def _mix_residual(merged, wo, x):
    t, d = x.shape
    tm = min(t, 512)

    def epilogue(acc, ins, outs):
        outs[0][...] = ins[2][...] + acc

    row = pl.BlockSpec((tm, d), lambda i, j, k: (i, 0))
    return _fused("mix_residual", (t // tm, 1, 1),
                  [(merged, row), (wo, pl.BlockSpec((d, d), lambda i, j, k: (0, 0))), (x, row)],
                  [(_sds((t, d), F32), row)], [(0, 1, NN)], epilogue, temp_bytes=2 * tm * d * 4)[0]


def _mix_bwd_gates(dx, wo, ya, yb, proj, cw):
    t, d = dx.shape
    tm = min(t, 1024)
    tn = min(d, 512)
    ga0 = (4 * cw + 2 * (cw // 4)) // tn

    def epilogue(acc, ins, outs):
        sa, sb = _sigmoid(ins[4][...].astype(F32)), _sigmoid(ins[5][...].astype(F32))
        outs[0][...] = (acc * sa).astype(BF16)
        outs[1][...] = (acc * sb).astype(BF16)
        outs[2][0] = (acc * ins[2][...].astype(F32) * sa * (1.0 - sa)).astype(BF16)
        outs[2][1] = (acc * ins[3][...].astype(F32) * sb * (1.0 - sb)).astype(BF16)

    blk = pl.BlockSpec((tm, tn), lambda i, j, k: (i, j))
    return _fused("mix_bwd_gates", (t // tm, d // tn, 1),
                  [(dx, pl.BlockSpec((tm, d), lambda i, j, k: (i, 0))),
                   (wo, pl.BlockSpec((tn, d), lambda i, j, k: (j, 0))),
                   (ya, blk), (yb, blk),
                   (proj, pl.BlockSpec((tm, tn), lambda i, j, k: (i, ga0 + j))),
                   (proj, pl.BlockSpec((tm, tn), lambda i, j, k: (i, ga0 + d // tn + j)))],
                  [(_sds((t, d), BF16), blk), (_sds((t, d), BF16), blk),
                   (_sds((2, t, d), BF16), pl.BlockSpec((2, tm, tn), lambda i, j, k: (0, i, j)))],
                  [(0, 1, NT)], epilogue, temp_bytes=8 * tm * tn * 4)


def _tn_matmul(name, a, b, tm, out_dtype=BF16):
    t, m = a.shape
    n = b.shape[1]

    def epilogue(acc, ins, outs):
        outs[0][...] = acc.astype(out_dtype)

    return _fused(name, (m // tm, 1, 1),
                  [(a, pl.BlockSpec((t, tm), lambda i, j, k: (0, i))),
                   (b, pl.BlockSpec((t, n), lambda i, j, k: (0, 0)))],
                  [(_sds((m, n), out_dtype), pl.BlockSpec((tm, n), lambda i, j, k: (i, 0)))],
                  [(0, 1, TN)], epilogue, temp_bytes=2 * tm * n * 4)[0]


def _out_proj_bwd_act(dya, dyb, woc, woa, deps=()):
    t, d = dya.shape
    kdim, nb = woc.shape[1], woc.shape[2]
    tm = min(t, 512)

    def body(dya_ref, dyb_ref, woc_ref, woa_ref, *rest):
        for dy_ref, w_ref, o_ref in ((dya_ref, woc_ref, rest[-2]), (dyb_ref, woa_ref, rest[-1])):
            total = None
            for j in range(N_DEV):
                part = lax.dot_general(dy_ref[:, j * nb:(j + 1) * nb], w_ref[j], NT, preferred_element_type=F32)
                total = part if total is None else total + part
            o_ref[...] = total

    row = pl.BlockSpec((tm, d), lambda i: (i, 0))
    wsp = pl.BlockSpec((N_DEV, kdim, nb), lambda i: (0, 0, 0))
    osp = pl.BlockSpec((tm, kdim), lambda i: (i, 0))
    blocks = 2 * tm * d * 2 + 2 * N_DEV * kdim * nb * 2 + 2 * tm * kdim * 4
    return pl.pallas_call(
        body, name="mix_bwd_dca_do", grid=(t // tm,),
        in_specs=[row, row, wsp, wsp] + [_ANY] * len(deps), out_specs=[osp, osp],
        out_shape=[_sds((t, kdim), F32)] * 2,
        compiler_params=_params(("parallel",), blocks, 4 * tm * kdim * 4),
    )(dya, dyb, woc, woa, *deps)


def _out_proj_bwd_w(ca, o, dya, dyb, nb):
    t, kdim = ca.shape

    def body(ca_ref, o_ref, dya_ref, dyb_ref, dwoc_ref, dwoa_ref):
        dwoc_ref[...] = lax.dot_general(ca_ref[...], dya_ref[...], TN, preferred_element_type=F32).astype(BF16)
        dwoa_ref[...] = lax.dot_general(o_ref[...], dyb_ref[...], TN, preferred_element_type=F32).astype(BF16)

    act = pl.BlockSpec((t, kdim), lambda j: (0, 0))
    col = pl.BlockSpec((t, nb), lambda j: (0, j))
    osp = pl.BlockSpec((None, kdim, nb), lambda j: (j, 0, 0))
    blocks = 2 * t * kdim * 2 + 2 * t * nb * 2 + 2 * kdim * nb * 2
    return pl.pallas_call(
        body, name="mix_bwd_dwoc_dwoa", grid=(N_DEV,),
        in_specs=[act, act, col, col], out_specs=[osp, osp],
        out_shape=[_sds((N_DEV, kdim, nb), BF16)] * 2,
        compiler_params=_params(("parallel",), blocks, 4 * kdim * nb * 4),
    )(ca, o, dya, dyb)


def _proj_bwd_act(dproj, w_in, deps=()):
    t, n = dproj.shape
    d, nb = w_in.shape[2], w_in.shape[3]
    tm = min(t, 512)

    def epilogue(acc, ins, outs):
        outs[0][...] = acc

    def products(ins):
        return (lax.dot_general(ins[0][:, 0:nb], ins[1][0], NT, preferred_element_type=F32)
                + lax.dot_general(ins[0][:, nb:2 * nb], ins[1][1], NT, preferred_element_type=F32))

    return _fused("mix_bwd_dh", (t // tm, 1, 4),
                  [(dproj, pl.BlockSpec((tm, 2 * nb), lambda i, j, k: (i, k))),
                   (w_in, pl.BlockSpec((None, 2, d, nb), lambda i, j, k: (k, 0, 0, 0)))],
                  [(_sds((t, d), F32), pl.BlockSpec((tm, d), lambda i, j, k: (i, 0)))],
                  products, epilogue, nk=4, acc_shape=(tm, d), temp_bytes=tm * d * 4, deps=deps)[0]


def _proj_bwd_w(h, dproj):
    t, d = h.shape
    nb = dproj.shape[1] // N_DEV
    tm = min(d, 512)

    def body(h_ref, dp_ref, o_ref):
        hv = h_ref[...]
        o_ref[0] = lax.dot_general(hv, dp_ref[:, 0:nb], TN, preferred_element_type=F32).astype(BF16)
        o_ref[1] = lax.dot_general(hv, dp_ref[:, nb:2 * nb], TN, preferred_element_type=F32).astype(BF16)

    blocks = t * tm * 2 + t * 2 * nb * 2 + 2 * tm * nb * 2
    return pl.pallas_call(
        body, name="mix_bwd_dwin", grid=(4, d // tm),
        in_specs=[pl.BlockSpec((t, tm), lambda j, i: (0, i)),
                  pl.BlockSpec((t, 2 * nb), lambda j, i: (0, j))],
        out_specs=pl.BlockSpec((None, 2, tm, nb), lambda j, i: (j, 0, i, 0)),
        out_shape=_sds((4, 2, d, nb), BF16),
        compiler_params=_params(("parallel", "parallel"), blocks, 4 * tm * nb * 4),
    )(h, dproj)


def _adamw_math(w, g, m, v):
    m = ADAM_B1 * m + (1.0 - ADAM_B1) * g
    v = ADAM_B2 * v + (1.0 - ADAM_B2) * (g * g)
    m_hat = m / (1.0 - ADAM_B1 ** ADAM_STEP)
    v_hat = v / (1.0 - ADAM_B2 ** ADAM_STEP)
    delta = -ADAM_LR * (m_hat / (jnp.sqrt(v_hat) + ADAM_EPS) + ADAM_WD * w)
    return delta, m, v


def _adamw(name, parts, w, m, v, tr):
    r, c = w.shape

    def body(p_ref, w_ref, m_ref, v_ref, g_out, d_out, m_out, v_out):
        g = p_ref[0].astype(F32)
        for s in range(1, N_DEV):
            g = g + p_ref[s].astype(F32)
        delta, mn, vn = _adamw_math(w_ref[...], g, m_ref[...], v_ref[...])
        g_out[...] = g
        d_out[...] = delta
        m_out[...] = mn
        v_out[...] = vn

    blk = pl.BlockSpec((tr, c), lambda i: (i, 0))
    blocks = N_DEV * tr * c * parts.dtype.itemsize + 7 * tr * c * 4
    return pl.pallas_call(
        body, name=name, grid=(r // tr,),
        in_specs=[pl.BlockSpec((N_DEV, tr, c), lambda i: (0, i, 0)), blk, blk, blk],
        out_specs=[blk] * 4, out_shape=[_sds((r, c), F32)] * 4,
        compiler_params=_params(("parallel",), blocks, 6 * tr * c * 4),
    )(parts, w, m, v)


def _chip_sum(sums_ref):
    g = sums_ref[0].astype(F32)
    for k in range(1, 4):
        g = g + sums_ref[k].astype(F32)
    return g


def _adamw_chips(name, sums, w, m, v, tr, deps=(), row0=0, into=None):
    r, c = w.shape
    rs = sums.shape[1]
    i0 = row0 // tr
    n_pass = len(deps) + (4 if into is not None else 0)

    def body(sums_ref, w_ref, m_ref, v_ref, *rest):
        g_out, d_out, m_out, v_out = rest[n_pass:]
        g = _chip_sum(sums_ref)
        delta, mn, vn = _adamw_math(w_ref[...], g, m_ref[...], v_ref[...])
        g_out[...] = g
        d_out[...] = delta
        m_out[...] = mn
        v_out[...] = vn

    blk = pl.BlockSpec((tr, c), lambda i: (i0 + i, 0))
    blocks = 4 * tr * c * 2 + 7 * tr * c * 4
    passed = list(deps) + (list(into) if into is not None else [])
    aliases = {4 + len(deps) + q: q for q in range(4)} if into is not None else {}
    return pl.pallas_call(
        body, name=name, grid=(rs // tr,),
        in_specs=[pl.BlockSpec((4, tr, c), lambda i: (0, i, 0)), blk, blk, blk] + [_ANY] * n_pass,
        out_specs=[blk] * 4, out_shape=[_sds((r, c), F32)] * 4,
        input_output_aliases=aliases,
        compiler_params=_params(("parallel",), blocks, 6 * tr * c * 4),
    )(sums, w, m, v, *passed)


def _adamw_side(contrib, w, m, v, n_tiles, step_of):
    r, c = w.shape
    tr = r // n_tiles
    assert tr * n_tiles == r and tr % 16 == 0, (r, n_tiles)

    def tile(i, j, k):
        return jnp.minimum(step_of(i, j, k), n_tiles - 1)

    blk = pl.BlockSpec((tr, c), lambda i, j, k: (tile(i, j, k), 0))
    ins = [(contrib, pl.BlockSpec((4, tr, c), lambda i, j, k: (0, tile(i, j, k), 0))), (w, blk), (m, blk), (v, blk)]
    outs = [(_sds((r, c), F32), blk)] * 4

    def fn(in_refs, out_refs):
        @pl.when(step_of(pl.program_id(0), pl.program_id(1), pl.program_id(2)) < n_tiles)
        def _():
            g = _chip_sum(in_refs[0])
            delta, mn, vn = _adamw_math(in_refs[1][...], g, in_refs[2][...], in_refs[3][...])
            out_refs[0][...] = g
            out_refs[1][...] = delta
            out_refs[2][...] = mn
            out_refs[3][...] = vn

    return ins, outs, fn


def _rope_tables(t):
    half = ROT_DIM // 2
    inv_freq = 1.0 / (ROPE_THETA ** (jnp.arange(0, ROT_DIM, 2, dtype=F32) / ROT_DIM))
    ang = jnp.arange(t, dtype=F32)[:, None] * inv_freq[None, :]
    cos, sin = jnp.cos(ang), jnp.sin(ang)
    ones = jnp.ones((t, HEAD_DIM - ROT_DIM), F32)
    zeros = jnp.zeros((t, HEAD_DIM - half), F32)
    c = jnp.concatenate([cos, cos, ones], axis=1)
    sa = jnp.concatenate([-sin, zeros], axis=1)
    sb = jnp.concatenate([jnp.zeros((t, half), F32), sin, jnp.zeros((t, HEAD_DIM - ROT_DIM), F32)], axis=1)
    return tuple(jnp.tile(a, (1, LANES // HEAD_DIM)) for a in (c, sa, sb))


def _pad_rows(a, rows=8):
    return jnp.pad(a, ((0, rows - a.shape[0]), (0, 0)))


def kernel(x, g_ffn1, w_gu1, w_down1, g_mix, w_in, conv_w, q_norm_g, k_norm_g, sinks, w_out_conv, w_out_attn, w_o, g_ffn2, w_gu2, w_down2, loss_target, m_g_ffn1, m_w_gu1, m_w_down1, m_g_mix, m_w_in, m_conv_w, m_q_norm_g, m_k_norm_g, m_sinks, m_w_out_conv, m_w_out_attn, m_w_o, m_g_ffn2, m_w_gu2, m_w_down2, v_g_ffn1, v_w_gu1, v_w_down1, v_g_mix, v_w_in, v_conv_w, v_q_norm_g, v_k_norm_g, v_sinks, v_w_out_conv, v_w_out_attn, v_w_o, v_g_ffn2, v_w_gu2, v_w_down2):
    t, d = x.shape[1], x.shape[2]
    cw = d // 2
    kw = cw // GROUP
    nq = cw // HEAD_DIM
    xs, target = x.reshape(t, d), loss_target.reshape(t, d)
    me = 4 * lax.axis_index("x") + 2 * lax.axis_index("y") + lax.axis_index("c")

    big = {"w_gu1": w_gu1, "w_down1": w_down1, "w_in": w_in, "w_out_conv": w_out_conv,
           "w_out_attn": w_out_attn, "w_o": w_o, "w_gu2": w_gu2, "w_down2": w_down2}
    big_m = {"w_gu1": m_w_gu1, "w_down1": m_w_down1, "w_in": m_w_in, "w_out_conv": m_w_out_conv,
             "w_out_attn": m_w_out_attn, "w_o": m_w_o, "w_gu2": m_w_gu2, "w_down2": m_w_down2}
    big_v = {"w_gu1": v_w_gu1, "w_down1": v_w_down1, "w_in": v_w_in, "w_out_conv": v_w_out_conv,
             "w_out_attn": v_w_out_attn, "w_o": v_w_o, "w_gu2": v_w_gu2, "w_down2": v_w_down2}
    names = list(big)

    tiles = {"w_gu1": 256, "w_gu2": 256, "w_in": 256, "w_down1": 176, "w_down2": 176,
             "w_out_conv": 1024, "w_out_attn": 1024, "w_o": 128}

    def row_tile(n):
        r = big[n].shape[1]
        return tiles[n] if r % tiles[n] == 0 else r

    rs_shape = {n: big[n].shape[1:] for n in names}
    half = rs_shape["w_gu1"][0] // 2
    rs_shape["w_gu1_lo"] = rs_shape["w_gu1_hi"] = (half, rs_shape["w_gu1"][1])

    def add_tile(n):
        r, c = rs_shape[n]
        while r * c * 2 > (3 << 20) and r % 32 == 0:
            r //= 2
        return r

    me_arr = me.astype(jnp.int32).reshape(1)
    sources = [(n, big[n][0], BF16, row_tile(n)) for n in names] + [("conv_w", _pad_rows(conv_w[0]), F32, 8)]
    issue_order = [0, 1, 2, 8, 3, 4, 5, 6, 7]
    first = _place_shard("place_" + names[0], sources[0][1], BF16, me_arr, sources[0][3])
    started = [_gather_start("gather_start_first", [first])]
    early = {2: (big_m["w_in"][0], big_v["w_in"][0])}
    rest = [_place_shard("place_" + sources[i][0], sources[i][1], sources[i][2], me_arr, sources[i][3],
                         deps=(started[0][3],) + early.get(i, ())) for i in issue_order[1:]]
    started.append(_gather_start("gather_start_rest", rest))
    where = {0: (0, 0)}
    where.update({i: (1, p) for p, i in enumerate(issue_order[1:])})

    def fetch(tag, idxs, after, forward=True):
        call = where[idxs[0]][0]
        send, recv, stacks, _ = started[call]
        positions = [where[i][1] for i in idxs]
        got = _gather_wait("gather_wait_" + tag, positions, send, recv, [stacks[p] for p in positions], after)
        return _forward_to_sibling("gather_forward_" + tag, got) if forward else got

    rope_tabs = _rope_tables(t)
    gq = jnp.tile(q_norm_g, (1, nq))
    gk = jnp.tile(k_norm_g, (1, nq // GROUP))
    sink_rows = jnp.broadcast_to(sinks[0][:, None], (nq, LANES))

    wts = {}
    h1 = _rms_fwd("ffn1_norm", xs, g_ffn1)
    wts["w_gu1"], = fetch("gu1", [0], started[1][3])
    gu1, a1 = _ffn_up("ffn1_up", h1, wts["w_gu1"])
    wts["w_down1"], = fetch("down1", [1], a1)
    wd1 = wts["w_down1"].reshape(-1, d)
    x1 = _ffn_down("ffn1_down", a1, wd1, xs)
    h2 = _rms_fwd("mix_norm", x1, g_mix)
    wts["w_in"], conv_land = fetch("in", [2, 8], h2)
    w_in_full = wts["w_in"].reshape(4, 2, d, -1)
    conv_full = jnp.transpose(conv_land, (1, 0, 2)).reshape(8, cw)
    proj = _proj(h2, w_in_full)
    ca = _conv_fwd(proj, conv_full)
    qn, kn, vb = _qk_prep(proj, gq, gk, rope_tabs, cw, kw)
    o = _attn_fwd(qn, kn, vb, sink_rows)
    wts["w_out_conv"], wts["w_out_attn"] = fetch("out", [3, 4], o)
    merged, ya, yb = _mix_out(ca, o, wts["w_out_conv"], wts["w_out_attn"], proj)
    wts["w_o"], = fetch("o", [5], merged)
    wo = wts["w_o"].reshape(d, d)
    x2 = _mix_residual(merged, wo, x1)
    h3 = _rms_fwd("ffn2_norm", x2, g_ffn2)
    mine = lax.axis_index("c").astype(jnp.int32).reshape(1)
    got = fetch("gu2", [6], h3, forward=False)
    fsend, frecv, got = _forward_start("gather_forward_start_gu2", got)
    part = _ffn_up("ffn2_up_mine", h3, got[0], parity=mine)
    wts["w_gu2"], = _forward_wait("gather_forward_wait_gu2", fsend, frecv, got, part[1])
    gu2, a2 = _ffn_up("ffn2_up_sibling", h3, wts["w_gu2"], parity=1 - mine, into=part)
    wts["w_down2"], = fetch("down2", [7], a2)
    wd2 = wts["w_down2"].reshape(-1, d)
    dy, sq, dy_bf = _ffn_down("ffn2_down", a2, wd2, x2, target=target)
    loss = lax.psum(sq[0, 0] * (0.5 / d), ("x", "y", "c"))

    place = jnp.stack([lax.axis_index("c"), 2 * lax.axis_index("x") + lax.axis_index("y")]).astype(jnp.int32)
    def pair_start(tag, group, grads, deps=()):
        stacks = [grads[n].reshape((4, 2) + rs_shape[n]) for n in group]
        lands = [lax.empty((4,) + rs_shape[n], BF16) for n in group]
        return _pair_start("rs_pair_start_" + tag, stacks, lands, deps)

    def chip_start(tag, group, pending, after):
        send, recv, stacks, lands, _ = pending
        stacks, lands = _pair_wait("rs_pair_wait_" + tag, send, recv, stacks, lands, after)
        added = [_pair_add("rs_pair_add_" + n, st, ld, place, add_tile(n)) for n, st, ld in zip(group, stacks, lands)]
        return _chip_start("rs_chip_start_" + tag, [a[0] for a in added], [a[1] for a in added])

    group_a, group_b, group_c = ["w_down2", "w_gu2"], ["w_o", "w_out_conv", "w_out_attn"], ["w_in"]
    group_d, group_e, group_f = ["w_down1"], ["w_gu1_lo"], ["w_gu1_hi"]
    g = {}
    dgu2, a2 = _ffn_bwd_act("ffn2_bwd_act", dy_bf, wd2, gu2)
    pend_s = _sibling_start("rs_act_start_gu2", [dgu2, h3])
    g["w_down2"], = _ffn_bwd_dwd("ffn2_bwd_dwd", a2, dy_bf, deps=(pend_s[4],))
    pend_a = pair_start("a", ["w_down2"], g)
    dh3, = _ffn_bwd_dh("ffn2_bwd_dh", pend_s[2][0], wts["w_gu2"], deps=(pend_a[4],))
    (dgu2, h3), (dgu2_sib, h3_sib) = _sibling_wait("rs_act_wait_gu2", pend_s[0], pend_s[1], pend_s[2], pend_s[3], dh3)
    sums_gu2, slots_gu2 = _ffn_bwd_dwgu_pair("ffn2_bwd_dwgu", h3, h3_sib, dgu2, dgu2_sib, place)
    stacks_a, lands_a = _pair_wait("rs_pair_wait_a", pend_a[0], pend_a[1], pend_a[2], pend_a[3], sums_gu2)
    added_a = _pair_add("rs_pair_add_w_down2", stacks_a[0], lands_a[0], place, add_tile("w_down2"))
    ring_a = _chip_start("rs_chip_start_a", [added_a[0], sums_gu2], [added_a[1], slots_gu2])
    dx2, dg_ffn2, dx2_bf = _rms_bwd("ffn2_bwd_rms", x2, g_ffn2, dh3, dy, deps=(ring_a[4],), with_bf16=True)

    dya, dyb, dgates = _mix_bwd_gates(dx2_bf, wo, ya, yb, proj, cw)
    g["w_o"] = _tn_matmul("mix_bwd_dwo", merged, dx2_bf, min(d, 512))
    g["w_out_conv"], g["w_out_attn"] = _out_proj_bwd_w(ca, o, dya, dyb, d // N_DEV)
    pend_b = pair_start("b", group_b, g)
    dca, do = _out_proj_bwd_act(dya, dyb, wts["w_out_conv"], wts["w_out_attn"], deps=(pend_b[4],))
    ring_b = chip_start("b", group_b, pend_b, do)
    d3, dconv_w = _conv_bwd(proj, conv_full, dca, deps=(ring_b[4],))
    dq, dkc, dkp, dvc, dvp, dsink = _attn_bwd(qn, kn, vb, sink_rows, do)
    dqkv, dgq, dgk = _qk_prep_bwd(proj, gq, gk, rope_tabs, dq, dkc, dkp, dvc, dvp, cw, kw)
    dproj = jnp.concatenate([d3[0], d3[1], d3[2], dqkv, dgates[0], dgates[1]], axis=1)
    g["w_in"] = _proj_bwd_w(h2, dproj)
    pend_c = pair_start("c", group_c, g)
    dh2 = _proj_bwd_act(dproj, w_in_full, deps=(pend_c[4],))
    ring_c = chip_start("c", group_c, pend_c, dh2)
    dx1, dg_mix, dx1_bf = _rms_bwd("mix_bwd_rms", x1, g_mix, dh2, dx2, deps=(ring_c[4],), with_bf16=True)

    big_out = {}
    arrived = {}

    def wait_group(tag, group, ring, after):
        send, recv, parts, lands2, _ = ring
        parts, lands2 = _chip_wait("rs_chip_wait_" + tag, send, recv, parts, lands2, after)
        arrived.update(dict(zip(group, lands2)))

    def update(n, after):
        res = _adamw_chips("adamw_" + n, arrived[n], big[n][0], big_m[n][0], big_v[n][0], row_tile(n), deps=(after,))
        big_out[n] = [a[None] for a in res]
        return res[0]

    def update_beside(n, n_tiles, step_of):
        return _adamw_side(arrived[n], big[n][0], big_m[n][0], big_v[n][0], n_tiles, step_of)

    def keep(n, res):
        big_out[n] = [a[None] for a in res]

    dgu1, a1 = _ffn_bwd_act("ffn1_bwd_act", dx1_bf, wd1, gu1)
    pend_s = _sibling_start("rs_act_start_gu1", [dgu1, h1])
    wait_group("a", group_a, ring_a, pend_s[4])
    g["w_down1"], *res = _ffn_bwd_dwd("ffn1_bwd_dwd", a1, dx1_bf,
                                       side=update_beside("w_down2", 11, lambda i, j, k: i * 4 + j))
    keep("w_down2", res)
    pend_d = pair_start("d", group_d, g)
    (dgu1, h1), (dgu1_sib, h1_sib) = _sibling_wait("rs_act_wait_gu1", pend_s[0], pend_s[1], pend_s[2], pend_s[3],
                                                   pend_d[4])
    sums_lo, *res = _ffn_bwd_dwgu_pair_rows("ffn1_bwd_dwgu_lo", h1, h1_sib, dgu1, dgu1_sib, place, (0, half),
                                            side=update_beside("w_gu2", 16, lambda i, j, k: i * 8 + j * 2 + k))
    keep("w_gu2", res)
    ring_d = chip_start("d", group_d, pend_d, sums_lo)
    slots_lo = lax.empty(sums_lo.shape, BF16)
    ring_e = _chip_start("rs_chip_start_e", [sums_lo], [slots_lo], deps=(ring_d[4],), own=True)
    wait_group("c", group_c, ring_c, ring_e[4])
    sums_hi, *res = _ffn_bwd_dwgu_pair_rows("ffn1_bwd_dwgu_hi", h1, h1_sib, dgu1, dgu1_sib, place, (half, half),
                                            side=update_beside("w_in", 16, lambda i, j, k: i * 8 + j * 2 + k))
    keep("w_in", res)
    slots_hi = lax.empty(sums_hi.shape, BF16)
    ring_f = _chip_start("rs_chip_start_f", [sums_hi], [slots_hi], own=True)
    wait_group("b", group_b, ring_b, ring_f[4])
    after = ring_f[4]
    for n in group_b:
        after = update(n, after)
    wait_group("d", group_d, ring_d, after)
    dh1, *res = _ffn_bwd_dh("ffn1_bwd_dh", dgu1, wts["w_gu1"],
                             side=update_beside("w_down1", 11, lambda i, j, k: i * 4 + k))
    keep("w_down1", res)
    grad_x, dg_ffn1 = _rms_bwd("ffn1_bwd_rms", xs, g_ffn1, dh1, dx1)
    after = grad_x
    n = "w_gu1"
    wait_group("e", group_e, ring_e, after)
    res = _adamw_chips("adamw_w_gu1_lo", arrived["w_gu1_lo"], big[n][0], big_m[n][0], big_v[n][0], row_tile(n), deps=(after,))
    wait_group("f", group_f, ring_f, res[0])
    res = _adamw_chips("adamw_w_gu1_hi", arrived["w_gu1_hi"], big[n][0], big_m[n][0], big_v[n][0], row_tile(n),
                       row0=half, into=res)
    keep(n, res)
    after = res[0]

    small = {"g_ffn1": dg_ffn1[0:1], "g_mix": dg_mix[0:1], "g_ffn2": dg_ffn2[0:1],
             "q_norm_g": dgq[0:1, :HEAD_DIM], "k_norm_g": dgk[0:1, :HEAD_DIM], "sinks": dsink[:, 0][None],
             "conv_w": dconv_w[0:CONV_K].reshape(1, -1)}
    small_w = {"g_ffn1": g_ffn1, "g_mix": g_mix, "g_ffn2": g_ffn2, "q_norm_g": q_norm_g, "k_norm_g": k_norm_g,
               "sinks": sinks, "conv_w": None}
    small_m = {"g_ffn1": m_g_ffn1, "g_mix": m_g_mix, "g_ffn2": m_g_ffn2, "q_norm_g": m_q_norm_g,
               "k_norm_g": m_k_norm_g, "sinks": m_sinks, "conv_w": m_conv_w}
    small_v = {"g_ffn1": v_g_ffn1, "g_mix": v_g_mix, "g_ffn2": v_g_ffn2, "q_norm_g": v_q_norm_g,
               "k_norm_g": v_k_norm_g, "sinks": v_sinks, "conv_w": v_conv_w}
    snames = list(small)
    widths = [small[n].shape[1] for n in snames]
    total = sum(widths)
    rows = -(-total // LANES)
    rows = -(-rows // 8) * 8

    def pack(vals):
        flat = jnp.concatenate([v.reshape(1, -1) for v in vals], axis=1)
        return jnp.pad(flat, ((0, 0), (0, rows * LANES - total))).reshape(rows, LANES)

    csh = cw // N_DEV

    def place_conv(local, fill):
        full = jnp.full((CONV_K, cw), fill, F32)
        return lax.dynamic_update_slice(full, local, (0, me * csh)).reshape(1, -1)

    pw = pack([small_w[n] if n != "conv_w" else place_conv(conv_w[0], 0.0) for n in snames])
    pm = pack([small_m[n] if n != "conv_w" else place_conv(m_conv_w[0], 0.0) for n in snames])
    pv = pack([small_v[n] if n != "conv_w" else place_conv(v_conv_w[0], 1.0) for n in snames])
    parts = _all_gather_small("gather_small_grads", pack([small[n] for n in snames]), deps=(after,))
    sg, sd, sm, sv = [a.reshape(1, -1) for a in _adamw("adamw_small", parts, pw, pm, pv, rows)]

    def unpack(flat, n):
        off = sum(widths[:snames.index(n)])
        piece = flat[:, off:off + widths[snames.index(n)]]
        if n == "conv_w":
            piece = lax.dynamic_slice(piece.reshape(CONV_K, cw), (0, me * csh), (CONV_K, csh))[None]
        return piece

    order = ["g_ffn1", "w_gu1", "w_down1", "g_mix", "w_in", "conv_w", "q_norm_g", "k_norm_g", "sinks",
             "w_out_conv", "w_out_attn", "w_o", "g_ffn2", "w_gu2", "w_down2"]
    outs = [loss, grad_x[None]]
    for idx, flat in enumerate((sg, sd, sm, sv)):
        for n in order:
            outs.append(big_out[n][idx] if n in big_out else unpack(flat, n))
    return tuple(outs)
```

```python
import jax
import jax.numpy as jnp
from jax import lax
from jax.experimental import pallas as pl
from jax.experimental.pallas import tpu as pltpu

F32 = jnp.float32
BF16 = jnp.bfloat16

N_DEV = 8
HEAD_DIM = 64
GROUP = 4
BLOCK = 128
ROT_DIM = 16
ROPE_THETA = 500000.0
RMS_EPS = 1e-6
NEG_INF = -1e30
ATTN_SCALE = HEAD_DIM ** -0.5
CONV_K = 3
LANES = 128
MXU_COLS = 256
VMEM_BYTES_V7X = 64 * 1024 * 1024
VMEM_CAP = VMEM_BYTES_V7X - 6 * 1024 * 1024

ADAM_LR = 0.001
ADAM_B1 = 0.9
ADAM_B2 = 0.999
ADAM_EPS = 1e-08
ADAM_WD = 0.01
ADAM_STEP = 10

NN = (((1,), (0,)), ((), ()))
NT = (((1,), (1,)), ((), ()))
TN = (((0,), (0,)), ((), ()))

MESH = pl.DeviceIdType.MESH


def _nbytes(shape, dtype):
    n = 1
    for s in shape:
        if s is not None:
            n *= s
    return n * jnp.dtype(dtype).itemsize


def _params(semantics, block_bytes, temp_bytes):
    assert 2 * block_bytes + temp_bytes <= VMEM_CAP, (block_bytes, temp_bytes)
    return pltpu.CompilerParams(dimension_semantics=semantics, vmem_limit_bytes=VMEM_CAP)


def _fused(name, grid, ins, outs, dots, epilogue, *, nk=1, acc_shape=None, temp_bytes=0,
           semantics=("parallel", "parallel", "arbitrary"), deps=(), side=None, prefetch=None):
    n_main_in, n_main_out = len(ins), len(outs)
    n_pre = 0 if prefetch is None else 1
    if side is not None:
        side_ins, side_outs = list(side[0]), list(side[1])
        if n_pre:
            side_ins = [(a, _blind(spec)) for a, spec in side_ins]
            side_outs = [(a, _blind(spec)) for a, spec in side_outs]
        ins, outs = list(ins) + side_ins, list(outs) + side_outs
    n_in, n_out = len(ins), len(outs)
    n_dep = len(deps)

    def body(*refs):
        pre, refs = refs[:n_pre], refs[n_pre:]
        in_refs, out_refs = refs[:n_in], refs[n_in + n_dep:n_in + n_dep + n_out]
        scratch = refs[n_in + n_dep + n_out:]
        if side is not None:
            side[2](in_refs[n_main_in:], out_refs[n_main_out:])

        def products():
            if callable(dots):
                return dots(in_refs)
            total = None
            for ai, bi, contract in dots:
                a, b = in_refs[ai][...], in_refs[bi][...]
                a = a if a.dtype == BF16 else a.astype(BF16)
                b = b if b.dtype == BF16 else b.astype(BF16)
                p = lax.dot_general(a, b, contract, preferred_element_type=F32)
                total = p if total is None else total + p
            return total

        if nk == 1:
            epilogue(products() if dots else None, in_refs, out_refs, *pre)
        else:
            acc = scratch[0]
            k = pl.program_id(2)

            @pl.when(k == 0)
            def _():
                acc[...] = jnp.zeros_like(acc)

            acc[...] += products()

            @pl.when(k == nk - 1)
            def _():
                epilogue(acc[...], in_refs, out_refs, *pre)

    block_bytes = sum(_nbytes(spec.block_shape, a.dtype) for a, spec in ins)
    block_bytes += sum(_nbytes(spec.block_shape, s.dtype) for s, spec in outs)
    scratch_shapes = []
    if nk > 1:
        scratch_shapes.append(pltpu.VMEM(acc_shape, F32))
        temp_bytes += _nbytes(acc_shape, F32)
    in_specs = [spec for _, spec in ins] + [pl.BlockSpec(memory_space=pl.ANY)] * n_dep
    out_specs = [spec for _, spec in outs]
    if n_pre:
        grid_spec = pltpu.PrefetchScalarGridSpec(num_scalar_prefetch=1, grid=grid, in_specs=in_specs,
                                                 out_specs=out_specs, scratch_shapes=scratch_shapes)
        return pl.pallas_call(
            body, name=name, grid_spec=grid_spec, out_shape=[s for s, _ in outs],
            compiler_params=_params(semantics, block_bytes, temp_bytes),
        )(prefetch, *[a for a, _ in ins], *deps)
    res = pl.pallas_call(
        body, name=name, grid=grid,
        in_specs=in_specs,
        out_specs=out_specs,
        out_shape=[s for s, _ in outs],
        scratch_shapes=scratch_shapes,
        compiler_params=_params(semantics, block_bytes, temp_bytes),
    )(*[a for a, _ in ins], *deps)
    return res


def _blind(spec):
    index_map = spec.index_map
    return pl.BlockSpec(spec.block_shape, lambda *a: index_map(*a[:-1]))


def _sds(shape, dtype):
    return jax.ShapeDtypeStruct(shape, dtype)


def _sigmoid(x):
    return jax.nn.sigmoid(x)


def _all_gather_small(name, shard, deps=()):
    n_dep = len(deps)

    def body(src, *rest):
        dst, send_sems, recv_sems, local_sem = rest[n_dep:]
        x, y, c = lax.axis_index("x"), lax.axis_index("y"), lax.axis_index("c")
        me = 4 * x + 2 * y + c
        copies = [pltpu.make_async_copy(src, dst.at[me], local_sem)]
        for k in range(1, N_DEV):
            peer = ((1 - x) if (k & 4) else x, (1 - y) if (k & 2) else y, (1 - c) if (k & 1) else c)
            copies.append(pltpu.make_async_remote_copy(
                src_ref=src, dst_ref=dst.at[me], send_sem=send_sems.at[k - 1], recv_sem=recv_sems.at[k - 1],
                device_id=peer, device_id_type=MESH))
        for cp in copies:
            cp.start()
        for cp in copies:
            cp.wait()

    hbm = pl.BlockSpec(memory_space=pltpu.HBM)
    return pl.pallas_call(
        body, name=name,
        in_specs=[hbm] + [pl.BlockSpec(memory_space=pl.ANY)] * n_dep, out_specs=hbm,
        out_shape=_sds((N_DEV,) + shard.shape, shard.dtype),
        scratch_shapes=[pltpu.SemaphoreType.DMA((N_DEV - 1,)), pltpu.SemaphoreType.DMA((N_DEV - 1,)),
                        pltpu.SemaphoreType.DMA],
    )(shard, *deps)


_HBM = pl.BlockSpec(memory_space=pltpu.HBM)
_SEM = pl.BlockSpec(memory_space=pltpu.SEMAPHORE)
_ANY = pl.BlockSpec(memory_space=pl.ANY)
_EFFECT = pltpu.SideEffectType.DATAFLOW_SIDE_EFFECTING
N_TARGETS = 4


def _mesh_pos():
    return lax.axis_index("x"), lax.axis_index("y"), lax.axis_index("c")


def _chip_peers(x, y, c):
    return [(1 - x, y, c), (x, 1 - y, c), (1 - x, 1 - y, c)]


def _dev_index(pos):
    return 4 * pos[0] + 2 * pos[1] + pos[2]


def _hbm_like(a):
    return pltpu.HBM(a.shape, a.dtype)


def _place_shard(name, w, out_dtype, me, tr, deps=()):
    r, c = w.shape
    n_dep = len(deps)

    def body(me_ref, w_ref, *rest):
        rest[n_dep][...] = w_ref[...].astype(out_dtype)

    grid_spec = pltpu.PrefetchScalarGridSpec(
        num_scalar_prefetch=1, grid=(r // tr,),
        in_specs=[pl.BlockSpec((tr, c), lambda i, me_ref: (i, 0))] + [_ANY] * n_dep,
        out_specs=pl.BlockSpec((None, tr, c), lambda i, me_ref: (me_ref[0], i, 0)))
    return pl.pallas_call(
        body, name=name, grid_spec=grid_spec, out_shape=_sds((N_DEV, r, c), out_dtype),
        compiler_params=_params(("parallel",), tr * c * 6, tr * c * 4),
    )(me, w, *deps)


def _gather_start(name, lands):
    n = len(lands)

    def body(*refs):
        bufs = refs[:n]
        send, recv = refs[n], refs[n + 1]
        token = refs[-1]
        x, y, c = _mesh_pos()
        me = _dev_index((x, y, c))
        targets = [(x, y, 1 - c)] + _chip_peers(x, y, c)
        for w in range(n):
            for k, to in enumerate(targets):
                pltpu.make_async_remote_copy(
                    src_ref=bufs[w].at[me], dst_ref=bufs[w].at[me],
                    send_sem=send.at[N_TARGETS * w + k], recv_sem=recv.at[N_TARGETS * w + k],
                    device_id=to, device_id_type=MESH).start()
        token[...] = jnp.zeros_like(token)

    sems = pltpu.SemaphoreType.DMA((N_TARGETS * n,))
    outs = pl.pallas_call(
        body, name=name,
        in_specs=[_HBM] * n, out_specs=[_SEM, _SEM] + [_HBM] * n + [_token_spec()],
        out_shape=[sems, sems] + [_hbm_like(a) for a in lands] + [_sds((8, LANES), F32)],
        input_output_aliases={i: 2 + i for i in range(n)},
        compiler_params=pltpu.CompilerParams(has_side_effects=_EFFECT),
    )(*lands)
    return outs[0], outs[1], list(outs[2:2 + n]), outs[-1]


def _gather_wait(name, positions, send, recv, lands, after):
    m = len(positions)

    def body(*refs):
        bufs = refs[:m]
        send_sems, recv_sems = refs[m], refs[m + 1]
        x, y, c = _mesh_pos()
        me = _dev_index((x, y, c))
        sources = [(x, y, 1 - c)] + _chip_peers(x, y, c)
        for j, w in enumerate(positions):
            for k, frm in enumerate(sources):
                cp = pltpu.make_async_remote_copy(
                    src_ref=bufs[j].at[me], dst_ref=bufs[j].at[_dev_index(frm)],
                    send_sem=send_sems.at[N_TARGETS * w + k], recv_sem=recv_sems.at[N_TARGETS * w + k],
                    device_id=frm, device_id_type=MESH)
                cp.wait_send()
                cp.wait_recv()

    outs = pl.pallas_call(
        body, name=name,
        in_specs=[_HBM] * m + [_SEM, _SEM, _ANY], out_specs=[_HBM] * m,
        out_shape=[_hbm_like(a) for a in lands],
        input_output_aliases={i: i for i in range(m)},
        compiler_params=pltpu.CompilerParams(has_side_effects=_EFFECT),
    )(*lands, send, recv, after)
    return list(outs)


def _forward_to_sibling(name, lands):
    m = len(lands)

    def body(*refs):
        copies = _forward_copies(refs[m:2 * m], refs[2 * m], refs[2 * m + 1])
        for cp in copies:
            cp.start()
        for cp in copies:
            cp.wait()

    outs = pl.pallas_call(
        body, name=name,
        in_specs=[_HBM] * m, out_specs=[_HBM] * m,
        out_shape=[_sds(a.shape, a.dtype) for a in lands],
        input_output_aliases={i: i for i in range(m)},
        scratch_shapes=[pltpu.SemaphoreType.DMA((3 * m,)), pltpu.SemaphoreType.DMA((3 * m,))],
    )(*lands)
    return list(outs)


def _forward_copies(bufs, send, recv):
    x, y, c = _mesh_pos()
    copies = []
    for j, buf in enumerate(bufs):
        for k, chip in enumerate(_chip_peers(x, y, c)):
            block = buf.at[_dev_index(chip)]
            copies.append(pltpu.make_async_remote_copy(
                src_ref=block, dst_ref=block, send_sem=send.at[3 * j + k], recv_sem=recv.at[3 * j + k],
                device_id=(x, y, 1 - c), device_id_type=MESH))
    return copies


def _forward_start(name, lands):
    m = len(lands)

    def body(*refs):
        for cp in _forward_copies(refs[:m], refs[m], refs[m + 1]):
            cp.start()

    sems = pltpu.SemaphoreType.DMA((3 * m,))
    outs = pl.pallas_call(
        body, name=name,
        in_specs=[_HBM] * m, out_specs=[_SEM, _SEM] + [_HBM] * m,
        out_shape=[sems, sems] + [_hbm_like(a) for a in lands],
        input_output_aliases={i: 2 + i for i in range(m)},
        compiler_params=pltpu.CompilerParams(has_side_effects=_EFFECT),
    )(*lands)
    return outs[0], outs[1], list(outs[2:])


def _forward_wait(name, send, recv, lands, after):
    m = len(lands)

    def body(*refs):
        for cp in _forward_copies(refs[:m], refs[m], refs[m + 1]):
            cp.wait_send()
            cp.wait_recv()

    outs = pl.pallas_call(
        body, name=name,
        in_specs=[_HBM] * m + [_SEM, _SEM, _ANY], out_specs=[_HBM] * m,
        out_shape=[_hbm_like(a) for a in lands],
        input_output_aliases={i: i for i in range(m)},
        compiler_params=pltpu.CompilerParams(has_side_effects=_EFFECT),
    )(*lands, send, recv, after)
    return list(outs)


def _token_spec():
    return pl.BlockSpec(memory_space=pltpu.VMEM)


def _pair_start(name, stacks, lands, deps=()):
    n = len(stacks)
    n_dep = len(deps)

    def body(*refs):
        srcs, dsts = refs[:n], refs[n:2 * n]
        send, recv = refs[2 * n + n_dep], refs[2 * n + n_dep + 1]
        token = refs[-1]
        x, y, c = _mesh_pos()
        for w in range(n):
            for chip in range(4):
                pltpu.make_async_remote_copy(
                    src_ref=srcs[w].at[chip, 1 - c], dst_ref=dsts[w].at[chip],
                    send_sem=send.at[4 * w + chip], recv_sem=recv.at[4 * w + chip],
                    device_id=(x, y, 1 - c), device_id_type=MESH).start()
        token[...] = jnp.zeros_like(token)

    sems = pltpu.SemaphoreType.DMA((4 * n,))
    outs = pl.pallas_call(
        body, name=name,
        in_specs=[_HBM] * (2 * n) + [_ANY] * n_dep, out_specs=[_SEM, _SEM] + [_HBM] * (2 * n) + [_token_spec()],
        out_shape=[sems, sems] + [_hbm_like(a) for a in stacks] + [_hbm_like(a) for a in lands] + [_sds((8, LANES), F32)],
        input_output_aliases={i: 2 + i for i in range(2 * n)},
        compiler_params=pltpu.CompilerParams(has_side_effects=_EFFECT),
    )(*stacks, *lands, *deps)
    return outs[0], outs[1], list(outs[2:2 + n]), list(outs[2 + n:2 + 2 * n]), outs[-1]


def _pair_wait(name, send, recv, stacks, lands, after):
    n = len(stacks)

    def body(*refs):
        srcs, dsts = refs[:n], refs[n:2 * n]
        send_sems, recv_sems = refs[2 * n], refs[2 * n + 1]
        x, y, c = _mesh_pos()
        for w in range(n):
            for chip in range(4):
                cp = pltpu.make_async_remote_copy(
                    src_ref=srcs[w].at[chip, 1 - c], dst_ref=dsts[w].at[chip],
                    send_sem=send_sems.at[4 * w + chip], recv_sem=recv_sems.at[4 * w + chip],
                    device_id=(x, y, 1 - c), device_id_type=MESH)
                cp.wait_send()
                cp.wait_recv()

    outs = pl.pallas_call(
        body, name=name,
        in_specs=[_HBM] * (2 * n) + [_SEM, _SEM, _ANY], out_specs=[_HBM] * (2 * n),
        out_shape=[_hbm_like(a) for a in stacks] + [_hbm_like(a) for a in lands],
        input_output_aliases={i: i for i in range(2 * n)},
        compiler_params=pltpu.CompilerParams(has_side_effects=_EFFECT),
    )(*stacks, *lands, send, recv, after)
    return list(outs[:n]), list(outs[n:])


def _pair_add(name, stack, land, place, tr):
    _, _, r, c = stack.shape

    def body(place_ref, a_ref, b_ref, sums_ref, slots_ref):
        total = (a_ref[...].astype(F32) + b_ref[...].astype(F32)).astype(BF16)
        sums_ref[...] = total

        @pl.when(pl.program_id(1) == place_ref[1])
        def _():
            slots_ref[...] = total

    grid_spec = pltpu.PrefetchScalarGridSpec(
        num_scalar_prefetch=1, grid=(r // tr, 4),
        in_specs=[pl.BlockSpec((None, None, tr, c), lambda i, k, place_ref: (k, place_ref[0], i, 0)),
                  pl.BlockSpec((None, tr, c), lambda i, k, place_ref: (k, i, 0))],
        out_specs=[pl.BlockSpec((None, tr, c), lambda i, k, place_ref: (k, i, 0)),
                   pl.BlockSpec((None, tr, c), lambda i, k, place_ref: (place_ref[1], i, 0))])
    return pl.pallas_call(
        body, name=name, grid_spec=grid_spec, out_shape=[_sds((4, r, c), BF16)] * 2,
        compiler_params=_params(("parallel", "arbitrary"), 4 * tr * c * 2, 3 * tr * c * 4),
    )(place, stack, land)


def _sibling_copies(srcs, dsts, send, recv):
    x, y, c = _mesh_pos()
    copies = []
    for w in range(len(srcs)):
        if len(srcs[w].shape) == 2:
            pairs = [(srcs[w], dsts[w])]
        else:
            nb = srcs[w].shape[2] // 4
            pairs = []
            for k in range(4):
                dev = 2 * k + 1 - c
                col = pl.multiple_of((dev % 4) * nb, LANES)
                pairs.append((srcs[w].at[dev // 4, :, pl.ds(col, nb)], dsts[w].at[k]))
        for src, dst in pairs:
            q = len(copies)
            copies.append(pltpu.make_async_remote_copy(
                src_ref=src, dst_ref=dst, send_sem=send.at[q], recv_sem=recv.at[q],
                device_id=(x, y, 1 - c), device_id_type=MESH))
    return copies


def _sibling_start(name, srcs):
    n = len(srcs)
    lands = [lax.empty(a.shape if a.ndim == 2 else (4, a.shape[1], a.shape[2] // 4), a.dtype) for a in srcs]

    def body(*refs):
        for cp in _sibling_copies(refs[:n], refs[n:2 * n], refs[2 * n], refs[2 * n + 1]):
            cp.start()
        token = refs[-1]
        token[...] = jnp.zeros_like(token)

    sems = pltpu.SemaphoreType.DMA((sum(1 if a.ndim == 2 else 4 for a in srcs),))
    outs = pl.pallas_call(
        body, name=name,
        in_specs=[_HBM] * (2 * n), out_specs=[_SEM, _SEM] + [_HBM] * (2 * n) + [_token_spec()],
        out_shape=[sems, sems] + [_hbm_like(a) for a in srcs] + [_hbm_like(a) for a in lands] + [_sds((8, LANES), F32)],
        input_output_aliases={i: 2 + i for i in range(2 * n)},
        compiler_params=pltpu.CompilerParams(has_side_effects=_EFFECT),
    )(*srcs, *lands)
    return outs[0], outs[1], list(outs[2:2 + n]), list(outs[2 + n:2 + 2 * n]), outs[-1]


def _sibling_wait(name, send, recv, srcs, lands, after):
    n = len(srcs)

    def body(*refs):
        for cp in _sibling_copies(refs[:n], refs[n:2 * n], refs[2 * n], refs[2 * n + 1]):
            cp.wait_send()
            cp.wait_recv()

    outs = pl.pallas_call(
        body, name=name,
        in_specs=[_HBM] * (2 * n) + [_SEM, _SEM, _ANY], out_specs=[_HBM] * (2 * n),
        out_shape=[_hbm_like(a) for a in srcs] + [_hbm_like(a) for a in lands],
        input_output_aliases={i: i for i in range(2 * n)},
        compiler_params=pltpu.CompilerParams(has_side_effects=_EFFECT),
    )(*srcs, *lands, send, recv, after)
    return list(outs[:n]), list(outs[n:])


def _chip_start(name, parts, lands, deps=(), own=False):
    n = len(parts)
    n_dep = len(deps)

    def body(*refs):
        srcs, dsts = refs[:n], refs[n:2 * n]
        send, recv = refs[2 * n + n_dep], refs[2 * n + n_dep + 1]
        token = refs[2 * n + n_dep + 2 + 2 * n]
        x, y, c = _mesh_pos()
        if own:
            local = refs[-1]
            for w in range(n):
                cp = pltpu.make_async_copy(srcs[w].at[2 * x + y], dsts[w].at[2 * x + y], local.at[w])
                cp.start()
                cp.wait()
        for w in range(n):
            for k, to in enumerate(_chip_peers(x, y, c)):
                pltpu.make_async_remote_copy(
                    src_ref=srcs[w].at[2 * to[0] + to[1]], dst_ref=dsts[w].at[2 * x + y],
                    send_sem=send.at[3 * w + k], recv_sem=recv.at[3 * w + k],
                    device_id=to, device_id_type=MESH).start()
        token[...] = jnp.zeros_like(token)

    sems = pltpu.SemaphoreType.DMA((3 * n,))
    outs = pl.pallas_call(
        body, name=name,
        in_specs=[_HBM] * (2 * n) + [_ANY] * n_dep, out_specs=[_SEM, _SEM] + [_HBM] * (2 * n) + [_token_spec()],
        out_shape=[sems, sems] + [_hbm_like(a) for a in parts] + [_hbm_like(a) for a in lands] + [_sds((8, LANES), F32)],
        input_output_aliases={i: 2 + i for i in range(2 * n)},
        scratch_shapes=[pltpu.SemaphoreType.DMA((n,))] if own else [],
        compiler_params=pltpu.CompilerParams(has_side_effects=_EFFECT),
    )(*parts, *lands, *deps)
    return outs[0], outs[1], list(outs[2:2 + n]), list(outs[2 + n:2 + 2 * n]), outs[-1]


def _chip_wait(name, send, recv, parts, lands, after):
    n = len(parts)

    def body(*refs):
        srcs, dsts = refs[:n], refs[n:2 * n]
        send_sems, recv_sems = refs[2 * n], refs[2 * n + 1]
        x, y, c = _mesh_pos()
        for w in range(n):
            for k, frm in enumerate(_chip_peers(x, y, c)):
                chip = 2 * frm[0] + frm[1]
                cp = pltpu.make_async_remote_copy(
                    src_ref=srcs[w].at[chip], dst_ref=dsts[w].at[chip],
                    send_sem=send_sems.at[3 * w + k], recv_sem=recv_sems.at[3 * w + k],
                    device_id=frm, device_id_type=MESH)
                cp.wait_send()
                cp.wait_recv()

    outs = pl.pallas_call(
        body, name=name,
        in_specs=[_HBM] * (2 * n) + [_SEM, _SEM, _ANY], out_specs=[_HBM] * (2 * n),
        out_shape=[_hbm_like(a) for a in parts] + [_hbm_like(a) for a in lands],
        input_output_aliases={i: i for i in range(2 * n)},
        compiler_params=pltpu.CompilerParams(has_side_effects=_EFFECT),
    )(*parts, *lands, send, recv, after)
    return list(outs[:n]), list(outs[n:])


def _row_tile(t):
    return min(t, 256)


def _rms_fwd(name, x, g):
    t, d = x.shape
    tm = _row_tile(t)

    def epilogue(_, ins, outs):
        xv = ins[0][...]
        r = lax.rsqrt(jnp.mean(xv * xv, axis=-1, keepdims=True) + RMS_EPS)
        outs[0][...] = (xv * r * ins[1][...]).astype(BF16)

    row = pl.BlockSpec((tm, d), lambda i, j, k: (i, 0))
    vec = pl.BlockSpec((1, d), lambda i, j, k: (0, 0))
    return _fused(name, (t // tm, 1, 1), [(x, row), (g, vec)], [(_sds((t, d), BF16), row)], [], epilogue,
                  temp_bytes=4 * tm * d * 4)[0]


def _rms_bwd(name, x, g, dh, resid, deps=(), with_bf16=False):
    t, d = x.shape
    tm = _row_tile(t)

    def epilogue(_, ins, outs):
        xv, gv, dhv = ins[0][...], ins[1][...], ins[2][...]
        r = lax.rsqrt(jnp.mean(xv * xv, axis=-1, keepdims=True) + RMS_EPS)
        xh = xv * r
        u = dhv * gv
        dot = jnp.mean(u * xh, axis=-1, keepdims=True)
        dx = ins[3][...] + r * (u - xh * dot)
        outs[0][...] = dx
        if with_bf16:
            outs[2][...] = dx.astype(BF16)

        @pl.when(pl.program_id(0) == 0)
        def _():
            outs[1][...] = jnp.zeros_like(outs[1])

        outs[1][0:1, :] += jnp.sum(dhv * xh, axis=0, keepdims=True)

    row = pl.BlockSpec((tm, d), lambda i, j, k: (i, 0))
    vec = pl.BlockSpec((1, d), lambda i, j, k: (0, 0))
    acc = pl.BlockSpec((8, d), lambda i, j, k: (0, 0))
    outs = [(_sds((t, d), F32), row), (_sds((8, d), F32), acc)] + ([(_sds((t, d), BF16), row)] if with_bf16 else [])
    return _fused(name, (t // tm, 1, 1), [(x, row), (g, vec), (dh, row), (resid, row)], outs, [], epilogue,
                  temp_bytes=6 * tm * d * 4, semantics=("arbitrary", "arbitrary", "arbitrary"), deps=deps)


def _ffn_up(name, h, wgu, parity=None, into=None):
    t, d = h.shape
    nb = wgu.shape[2]
    f = 4 * nb
    tm = min(t, 512)

    def body(h_ref, wg_ref, wu_ref, gu_ref, a_ref):
        hv = h_ref[...]
        for c0 in range(0, nb, MXU_COLS):
            cs = slice(c0, min(c0 + MXU_COLS, nb))
            g = jnp.dot(hv, wg_ref[:, cs], preferred_element_type=F32)
            u = jnp.dot(hv, wu_ref[:, cs], preferred_element_type=F32)
            gu_ref[0, :, cs] = g.astype(BF16)
            gu_ref[1, :, cs] = u.astype(BF16)
            a_ref[:, cs] = (g * _sigmoid(g) * u).astype(BF16)

    blocks = tm * d * 2 + 2 * d * nb * 2 + 3 * tm * nb * 2
    params = _params(("parallel", "parallel"), blocks, 8 * tm * MXU_COLS * 4)
    out_shape = [_sds((2, t, f), BF16), _sds((t, f), BF16)]
    if parity is None:
        return pl.pallas_call(
            body, name=name, grid=(4, t // tm),
            in_specs=[pl.BlockSpec((tm, d), lambda j, i: (i, 0)),
                      pl.BlockSpec((None, d, nb), lambda j, i: (j, 0, 0)),
                      pl.BlockSpec((None, d, nb), lambda j, i: (j + 4, 0, 0))],
            out_specs=[pl.BlockSpec((2, tm, nb), lambda j, i: (0, i, j)),
                       pl.BlockSpec((tm, nb), lambda j, i: (i, j))],
            out_shape=out_shape, compiler_params=params,
        )(h, wgu, wgu)

    def half_body(parity_ref, h_ref, wg_ref, wu_ref, *rest):
        body(h_ref, wg_ref, wu_ref, rest[-2], rest[-1])

    n_pass = 0 if into is None else 2
    grid_spec = pltpu.PrefetchScalarGridSpec(
        num_scalar_prefetch=1, grid=(2, t // tm),
        in_specs=[pl.BlockSpec((tm, d), lambda jj, i, p: (i, 0)),
                  pl.BlockSpec((None, d, nb), lambda jj, i, p: (2 * jj + p[0], 0, 0)),
                  pl.BlockSpec((None, d, nb), lambda jj, i, p: (2 * jj + p[0] + 4, 0, 0))] + [_ANY] * n_pass,
        out_specs=[pl.BlockSpec((2, tm, nb), lambda jj, i, p: (0, i, 2 * jj + p[0])),
                   pl.BlockSpec((tm, nb), lambda jj, i, p: (i, 2 * jj + p[0]))])
    return pl.pallas_call(
        half_body, name=name, grid_spec=grid_spec, out_shape=out_shape,
        input_output_aliases={} if into is None else {4: 0, 5: 1}, compiler_params=params,
    )(parity, h, wgu, wgu, *(into or ()))


def _ffn_down(name, a, wd, x, target=None):
    t, f = a.shape
    d = wd.shape[1]
    tm = min(t, 512)
    tn = min(d, 1024)
    blk = pl.BlockSpec((tm, tn), lambda j, i, k: (i, j))
    ins = [(a, pl.BlockSpec((tm, f), lambda j, i, k: (i, 0))), (wd, pl.BlockSpec((f, tn), lambda j, i, k: (0, j))), (x, blk)]

    if target is None:
        def epilogue(acc, ins, outs):
            outs[0][...] = ins[2][...] + 0.5 * acc

        return _fused(name, (d // tn, t // tm, 1), ins, [(_sds((t, d), F32), blk)],
                      [(0, 1, NN)], epilogue, temp_bytes=2 * tm * tn * 4)[0]

    def epilogue(acc, ins, outs):
        e = ins[2][...] + 0.5 * acc - ins[3][...]
        outs[0][...] = e * (1.0 / d)
        outs[2][...] = (e * (1.0 / d)).astype(BF16)

        @pl.when((pl.program_id(0) == 0) & (pl.program_id(1) == 0))
        def _():
            outs[1][...] = jnp.zeros_like(outs[1])

        part = jnp.sum(jnp.sum(e * e, axis=1, keepdims=True), axis=0, keepdims=True)
        outs[1][...] += jnp.broadcast_to(part, outs[1].shape)

    return _fused(name, (d // tn, t // tm, 1), ins + [(target, blk)],
                  [(_sds((t, d), F32), blk), (_sds((8, LANES), F32), pl.BlockSpec((8, LANES), lambda j, i, k: (0, 0))),
                   (_sds((t, d), BF16), blk)],
                  [(0, 1, NN)], epilogue, temp_bytes=3 * tm * tn * 4,
                  semantics=("arbitrary", "arbitrary", "arbitrary"))


def _ffn_bwd_act(name, dy, wd, gu, deps=()):
    t, d = dy.shape
    f = wd.shape[0]
    nb = f // 4
    tm = min(t, 512)

    def body(dy_ref, wd_ref, gu_ref, *rest):
        dgu_ref, a_ref = rest[-2], rest[-1]
        dyv = dy_ref[...].astype(BF16)
        for c0 in range(0, nb, MXU_COLS):
            cs = slice(c0, min(c0 + MXU_COLS, nb))
            da = 0.5 * lax.dot_general(dyv, wd_ref[cs, :], NT, preferred_element_type=F32)
            g = gu_ref[0, :, cs].astype(F32)
            u = gu_ref[1, :, cs].astype(F32)
            s = _sigmoid(g)
            silu = g * s
            dgu_ref[0, :, cs] = (da * u * (s * (1.0 + g * (1.0 - s)))).astype(BF16)
            dgu_ref[1, :, cs] = (da * silu).astype(BF16)
            a_ref[:, cs] = (silu * u).astype(BF16)

    blocks = tm * d * 4 + nb * d * 2 + 5 * tm * nb * 2
    return pl.pallas_call(
        body, name=name, grid=(4, t // tm),
        in_specs=[pl.BlockSpec((tm, d), lambda j, i: (i, 0)),
                  pl.BlockSpec((nb, d), lambda j, i: (j, 0)),
                  pl.BlockSpec((2, tm, nb), lambda j, i: (0, i, j))] + [_ANY] * len(deps),
        out_specs=[pl.BlockSpec((2, tm, nb), lambda j, i: (0, i, j)), pl.BlockSpec((tm, nb), lambda j, i: (i, j))],
        out_shape=[_sds((2, t, f), BF16), _sds((t, f), BF16)],
        compiler_params=_params(("parallel", "parallel"), blocks, tm * d * 2 + 8 * tm * MXU_COLS * 4),
    )(dy, wd, gu, *deps)


def _ffn_bwd_dwd(name, a, dy, deps=(), side=None):
    t, f = a.shape
    d = dy.shape[1]
    tm = f // 4
    tn = min(d, 512)

    def epilogue(acc, ins, outs):
        outs[0][...] = (0.5 * acc).astype(BF16)

    return _fused(name, (4, d // tn, 1),
                  [(a, pl.BlockSpec((t, tm), lambda i, j, k: (0, i))),
                   (dy, pl.BlockSpec((t, tn), lambda i, j, k: (0, j)))],
                  [(_sds((f, d), BF16), pl.BlockSpec((tm, tn), lambda i, j, k: (i, j)))],
                  [(0, 1, TN)], epilogue, temp_bytes=t * tn * 2 + 2 * tm * tn * 4, deps=deps, side=side)


def _ffn_bwd_dh(name, dgu, wgu, deps=(), side=None):
    _, t, f = dgu.shape
    d, nb = wgu.shape[1], wgu.shape[2]
    tm = min(t, 512)

    def products(ins):
        return (lax.dot_general(ins[0][:, 0:nb], ins[1][0], NT, preferred_element_type=F32)
                + lax.dot_general(ins[0][:, nb:2 * nb], ins[1][1], NT, preferred_element_type=F32))

    def epilogue(acc, ins, outs):
        outs[0][...] = acc

    return _fused(name, (t // tm, 1, 4),
                  [(dgu, pl.BlockSpec((None, tm, 2 * nb), lambda i, j, k: (k // 2, i, k % 2))),
                   (wgu, pl.BlockSpec((2, d, nb), lambda i, j, k: (k, 0, 0)))],
                  [(_sds((t, d), F32), pl.BlockSpec((tm, d), lambda i, j, k: (i, 0)))],
                  products, epilogue, nk=4, acc_shape=(tm, d), temp_bytes=tm * d * 4, deps=deps, side=side)


def _ffn_bwd_dwgu(name, h, dgu, deps=(), side=None, rows=None):
    t, d = h.shape
    nb = dgu.shape[2] // 4
    tm = min(d, 512)
    row0, nrows = rows if rows is not None else (0, d)
    j0 = row0 // tm

    def epilogue(acc, ins, outs):
        outs[0][...] = acc.astype(BF16)

    return _fused(name, (N_DEV, nrows // tm, 1),
                  [(h, pl.BlockSpec((t, tm), lambda i, j, k: (0, j0 + j))),
                   (dgu, pl.BlockSpec((None, t, nb), lambda i, j, k: (i // 4, 0, i % 4)))],
                  [(_sds((N_DEV, nrows, nb), BF16), pl.BlockSpec((None, tm, nb), lambda i, j, k: (i, j, 0)))],
                  [(0, 1, TN)], epilogue, temp_bytes=2 * tm * nb * 4, deps=deps, side=side)


def _ffn_bwd_dwgu_pair_rows(name, h, h_sib, dgu, dgu_sib, place, rows, deps=(), side=None):
    t, d = h.shape
    nb = dgu.shape[2] // 4
    tm = min(d, 512)
    row0, nrows = rows
    j0 = row0 // tm

    def products(ins):
        return lax.cond(pl.program_id(2) == 0,
                        lambda: lax.dot_general(ins[0][...], ins[2][...], TN, preferred_element_type=F32),
                        lambda: lax.dot_general(ins[1][...], ins[3][...], TN, preferred_element_type=F32))

    def epilogue(acc, ins, outs, place_ref):
        total = acc.astype(BF16)
        outs[0][...] = total

        @pl.when(pl.program_id(1) == place_ref[1])
        def _():
            outs[1][...] = total

    def act_map(i, j, k, place_ref):
        return (0, j0 + i)

    def sib_map(i, j, k, place_ref):
        return (jnp.maximum(j + k - 1, 0), 0, 0)

    def grad_map(i, j, k, place_ref):
        dev = 2 * j + place_ref[0]
        return (dev // 4, 0, dev % 4)

    return _fused(name, (nrows // tm, 4, 2),
                  [(h, pl.BlockSpec((t, tm), act_map)), (h_sib, pl.BlockSpec((t, tm), act_map)),
                   (dgu, pl.BlockSpec((None, t, nb), grad_map)),
                   (dgu_sib, pl.BlockSpec((None, t, nb), sib_map))],
                  [(_sds((4, nrows, nb), BF16), pl.BlockSpec((None, tm, nb), lambda i, j, k, place_ref: (j, i, 0))),
                   (_sds((4, nrows, nb), BF16),
                    pl.BlockSpec((None, tm, nb), lambda i, j, k, place_ref: (place_ref[1], i, 0)))],
                  products, epilogue, nk=2, acc_shape=(tm, nb), temp_bytes=tm * nb * 4,
                  semantics=("parallel", "arbitrary", "arbitrary"), deps=deps, side=side, prefetch=place)


def _ffn_bwd_dwgu_pair(name, h, h_sib, dgu, dgu_sib, place):
    t, d = h.shape
    nb = dgu.shape[2] // 4
    tm = min(d, 512)

    def body(place_ref, h_ref, hs_ref, g_ref, gs_ref, sums_ref, slots_ref):
        acc = lax.dot_general(h_ref[...], g_ref[...], TN, preferred_element_type=F32)
        acc += lax.dot_general(hs_ref[...], gs_ref[...], TN, preferred_element_type=F32)
        total = acc.astype(BF16)
        sums_ref[...] = total

        @pl.when(pl.program_id(1) == place_ref[1])
        def _():
            slots_ref[...] = total

    def act_map(i, k, place_ref):
        return (0, i)

    def grad_map(i, k, place_ref):
        dev = 2 * k + place_ref[0]
        return (dev // 4, 0, dev % 4)

    grid_spec = pltpu.PrefetchScalarGridSpec(
        num_scalar_prefetch=1, grid=(d // tm, 4),
        in_specs=[pl.BlockSpec((t, tm), act_map), pl.BlockSpec((t, tm), act_map),
                  pl.BlockSpec((None, t, nb), grad_map),
                  pl.BlockSpec((None, t, nb), lambda i, k, place_ref: (k, 0, 0))],
        out_specs=[pl.BlockSpec((None, tm, nb), lambda i, k, place_ref: (k, i, 0)),
                   pl.BlockSpec((None, tm, nb), lambda i, k, place_ref: (place_ref[1], i, 0))])
    blocks = 2 * t * tm * 2 + 2 * t * nb * 2 + 2 * tm * nb * 2
    return pl.pallas_call(
        body, name=name, grid_spec=grid_spec, out_shape=[_sds((4, d, nb), BF16)] * 2,
        compiler_params=_params(("parallel", "arbitrary"), blocks, 2 * tm * nb * 4),
    )(place, h, h_sib, dgu, dgu_sib)


def _proj(h, w_in):
    t, d = h.shape
    nb = w_in.shape[3]
    tm = min(t, 512)

    def body(h_ref, w_ref, o_ref):
        hv = h_ref[...]
        o_ref[:, 0:nb] = jnp.dot(hv, w_ref[0], preferred_element_type=F32).astype(BF16)
        o_ref[:, nb:2 * nb] = jnp.dot(hv, w_ref[1], preferred_element_type=F32).astype(BF16)

    blocks = tm * d * 2 + 2 * d * nb * 2 + tm * 2 * nb * 4
    return pl.pallas_call(
        body, name="mix_proj", grid=(4, t // tm),
        in_specs=[pl.BlockSpec((tm, d), lambda j, i: (i, 0)),
                  pl.BlockSpec((None, 2, d, nb), lambda j, i: (j, 0, 0, 0))],
        out_specs=pl.BlockSpec((tm, 2 * nb), lambda j, i: (i, j)),
        out_shape=_sds((t, N_DEV * nb), BF16),
        compiler_params=_params(("parallel", "parallel"), blocks, 2 * tm * nb * 4),
    )(h, w_in)


def _shift_rows(u, k):
    t = u.shape[0]
    rolled = pltpu.roll(u, k % t, axis=0)
    row = lax.broadcasted_iota(jnp.int32, u.shape, 0)
    keep = (row >= k) if k > 0 else (row < t + k)
    return jnp.where(keep, rolled, 0.0)


def _conv_fwd(proj, conv_w):
    t = proj.shape[0]
    cw = conv_w.shape[1]
    tc = min(cw, 256)
    nc = cw // tc

    def epilogue(_, ins, outs):
        u = ins[2][...].astype(F32) * ins[0][...].astype(F32)
        w = ins[3][...]
        y = u * w[2:3, :] + _shift_rows(u, 1) * w[1:2, :] + _shift_rows(u, 2) * w[0:1, :]
        outs[0][...] = (ins[1][...].astype(F32) * y).astype(BF16)

    def col(seg):
        return pl.BlockSpec((t, tc), lambda i, j, k: (0, seg * nc + i))

    return _fused("conv_fwd", (nc, 1, 1),
                  [(proj, col(0)), (proj, col(1)), (proj, col(2)),
                   (conv_w, pl.BlockSpec((8, tc), lambda i, j, k: (0, i)))],
                  [(_sds((t, cw), BF16), pl.BlockSpec((t, tc), lambda i, j, k: (0, i)))],
                  [], epilogue, temp_bytes=6 * t * tc * 4)[0]


def _conv_bwd(proj, conv_w, dca, deps=()):
    t = proj.shape[0]
    cw = conv_w.shape[1]
    tc = min(cw, 256)
    nc = cw // tc

    def epilogue(_, ins, outs):
        xc, bg, cg = ins[0][...].astype(F32), ins[1][...].astype(F32), ins[2][...].astype(F32)
        w, dc = ins[3][...], ins[4][...]
        u = cg * xc
        u1, u2 = _shift_rows(u, 1), _shift_rows(u, 2)
        y = u * w[2:3, :] + u1 * w[1:2, :] + u2 * w[0:1, :]
        dconv = dc * bg
        du = dconv * w[2:3, :] + _shift_rows(dconv, -1) * w[1:2, :] + _shift_rows(dconv, -2) * w[0:1, :]
        outs[0][0] = (du * cg).astype(BF16)
        outs[0][1] = (dc * y).astype(BF16)
        outs[0][2] = (du * xc).astype(BF16)
        outs[1][...] = jnp.zeros_like(outs[1])
        outs[1][0:1, :] = jnp.sum(dconv * u2, axis=0, keepdims=True)
        outs[1][1:2, :] = jnp.sum(dconv * u1, axis=0, keepdims=True)
        outs[1][2:3, :] = jnp.sum(dconv * u, axis=0, keepdims=True)

    def col(seg):
        return pl.BlockSpec((t, tc), lambda i, j, k: (0, seg * nc + i))

    own = pl.BlockSpec((t, tc), lambda i, j, k: (0, i))
    wspec = pl.BlockSpec((8, tc), lambda i, j, k: (0, i))
    return _fused("conv_bwd", (nc, 1, 1),
                  [(proj, col(0)), (proj, col(1)), (proj, col(2)), (conv_w, wspec), (dca, own)],
                  [(_sds((3, t, cw), BF16), pl.BlockSpec((3, t, tc), lambda i, j, k: (0, 0, i))),
                   (_sds((8, cw), F32), wspec)],
                  [], epilogue, temp_bytes=10 * t * tc * 4, deps=deps)


def _split3(x):
    hi = x.astype(BF16)
    r1 = x - hi.astype(F32)
    mid = r1.astype(BF16)
    lo = (r1 - mid.astype(F32)).astype(BF16)
    return hi, mid, lo


def _head_selector(width):
    r = lax.broadcasted_iota(jnp.int32, (width, LANES), 0)
    c = lax.broadcasted_iota(jnp.int32, (width, LANES), 1)
    return (lax.shift_right_logical(r, 6) == c).astype(BF16)


def _head_sum(x, sel):
    return sum(jnp.dot(p, sel, preferred_element_type=F32) for p in _split3(x))


def _head_bcast(r, sel):
    return sum(lax.dot_general(p, sel, NT, preferred_element_type=F32) for p in _split3(r))


def _rope(x, c, sa, sb):
    n = x.shape[1]
    return x * c + pltpu.roll(x, n - ROT_DIM // 2, axis=1) * sa + pltpu.roll(x, ROT_DIM // 2, axis=1) * sb


def _rope_t(d, c, sa, sb):
    n = d.shape[1]
    return d * c + pltpu.roll(d * sa, ROT_DIM // 2, axis=1) + pltpu.roll(d * sb, n - ROT_DIM // 2, axis=1)


def _tile_lanes(tab, width):
    return tab if width == tab.shape[1] else jnp.tile(tab, (1, width // tab.shape[1]))


def _qk_prep(proj, gq, gk, rope_tabs, cw, kw):
    t = proj.shape[0]
    tm = _row_tile(t)

    def epilogue(_, ins, outs):
        c, sa, sb = ins[5][...], ins[6][...], ins[7][...]
        for src, gain, dst, width in ((0, 3, 0, cw), (1, 4, 1, kw)):
            xv = ins[src][...].astype(F32)
            sel = _head_selector(width)
            r = lax.rsqrt(_head_sum(xv * xv, sel) * (1.0 / HEAD_DIM) + RMS_EPS)
            xn = xv * _head_bcast(r, sel) * ins[gain][...]
            outs[dst][...] = _rope(xn, _tile_lanes(c, width), _tile_lanes(sa, width), _tile_lanes(sb, width)).astype(BF16)
        outs[2][...] = ins[2][...].astype(BF16)

    kblk = cw // kw
    tab = pl.BlockSpec((tm, LANES), lambda i, j, k: (i, 0))
    kspec = pl.BlockSpec((tm, kw), lambda i, j, k: (i, 0))
    return _fused("qk_prep", (t // tm, 1, 1),
                  [(proj, pl.BlockSpec((tm, cw), lambda i, j, k: (i, 3))),
                   (proj, pl.BlockSpec((tm, kw), lambda i, j, k: (i, 4 * kblk))),
                   (proj, pl.BlockSpec((tm, kw), lambda i, j, k: (i, 4 * kblk + 1))),
                   (gq, pl.BlockSpec((1, cw), lambda i, j, k: (0, 0))),
                   (gk, pl.BlockSpec((1, kw), lambda i, j, k: (0, 0))),
                   (rope_tabs[0], tab), (rope_tabs[1], tab), (rope_tabs[2], tab)],
                  [(_sds((t, cw), BF16), pl.BlockSpec((tm, cw), lambda i, j, k: (i, 0))),
                   (_sds((t, kw), BF16), kspec), (_sds((t, kw), BF16), kspec)],
                  [], epilogue, temp_bytes=12 * tm * cw * 4)


def _qk_prep_bwd(proj, gq, gk, rope_tabs, dq, dkc, dkp, dvc, dvp, cw, kw):
    t = proj.shape[0]
    tm = BLOCK
    nblk = t // tm

    def epilogue(_, ins, outs):
        c, sa, sb = ins[5][...], ins[6][...], ins[7][...]
        has_next = (pl.program_id(0) < nblk - 1).astype(F32)
        dk = ins[9][...] + has_next * ins[10][...]
        dv = ins[11][...] + has_next * ins[12][...]
        pieces = []
        for src, gain, dval, dst, width in ((0, 3, ins[8][...], 1, cw), (1, 4, dk, 2, kw)):
            xv, gv = ins[src][...].astype(F32), ins[gain][...]
            sel = _head_selector(width)
            r = _head_bcast(lax.rsqrt(_head_sum(xv * xv, sel) * (1.0 / HEAD_DIM) + RMS_EPS), sel)
            xh = xv * r
            dxn = _rope_t(dval, _tile_lanes(c, width), _tile_lanes(sa, width), _tile_lanes(sb, width))
            u = dxn * gv
            dot = _head_bcast(_head_sum(u * xh, sel), sel) * (1.0 / HEAD_DIM)
            pieces.append((r * (u - xh * dot)).astype(BF16))
            ri = lax.broadcasted_iota(jnp.int32, (width, LANES), 0)
            ci = lax.broadcasted_iota(jnp.int32, (width, LANES), 1)
            fold = (lax.bitwise_and(ri, HEAD_DIM - 1) == ci).astype(BF16)
            colsum = jnp.broadcast_to(jnp.sum(dxn * xh, axis=0, keepdims=True), (8, width))
            part = sum(jnp.dot(p, fold, preferred_element_type=F32) for p in _split3(colsum))

            @pl.when(pl.program_id(0) == 0)
            def _():
                outs[dst][...] = jnp.zeros_like(outs[dst])

            outs[dst][0:1, :] += part[0:1, :]
        outs[0][:, 0:cw] = pieces[0]
        outs[0][:, cw:cw + kw] = pieces[1]
        outs[0][:, cw + kw:cw + 2 * kw] = dv.astype(BF16)

    kblk = cw // kw
    tab = pl.BlockSpec((tm, LANES), lambda i, j, k: (i, 0))
    kcur = pl.BlockSpec((tm, kw), lambda i, j, k: (i, 0))
    knext = pl.BlockSpec((tm, kw), lambda i, j, k: (jnp.minimum(i + 1, nblk - 1), 0))
    acc = pl.BlockSpec((8, LANES), lambda i, j, k: (0, 0))
    return _fused("qk_prep_bwd", (nblk, 1, 1),
                  [(proj, pl.BlockSpec((tm, cw), lambda i, j, k: (i, 3))),
                   (proj, pl.BlockSpec((tm, kw), lambda i, j, k: (i, 4 * kblk))),
                   (proj, pl.BlockSpec((tm, kw), lambda i, j, k: (i, 4 * kblk + 1))),
                   (gq, pl.BlockSpec((1, cw), lambda i, j, k: (0, 0))),
                   (gk, pl.BlockSpec((1, kw), lambda i, j, k: (0, 0))),
                   (rope_tabs[0], tab), (rope_tabs[1], tab), (rope_tabs[2], tab),
                   (dq, pl.BlockSpec((tm, cw), lambda i, j, k: (i, 0))),
                   (dkc, kcur), (dkp, knext), (dvc, kcur), (dvp, knext)],
                  [(_sds((t, cw + 2 * kw), BF16), pl.BlockSpec((tm, cw + 2 * kw), lambda i, j, k: (i, 0))),
                   (_sds((8, LANES), F32), acc), (_sds((8, LANES), F32), acc)],
                  [], epilogue, temp_bytes=16 * tm * cw * 4, semantics=("arbitrary", "arbitrary", "arbitrary"))


def _attn_mask(n):
    key = lax.broadcasted_iota(jnp.int32, (2 * BLOCK, GROUP * BLOCK), 0)
    qry = lax.bitwise_and(lax.broadcasted_iota(jnp.int32, (2 * BLOCK, GROUP * BLOCK), 1), BLOCK - 1)
    return (key > qry) & (key <= qry + BLOCK) & ((key >= BLOCK) | (n > 0))


def _stack_heads(x, h):
    return jnp.concatenate([x[:, (h * GROUP + g) * HEAD_DIM:(h * GROUP + g + 1) * HEAD_DIM] for g in range(GROUP)], axis=0)


def _softmax_with_sink(q4, k2, sink_ref, h, valid):
    sink = jnp.concatenate([sink_ref[h * GROUP + g:h * GROUP + g + 1, :] for g in range(GROUP)], axis=1)
    s = lax.dot_general(k2, q4, NT, preferred_element_type=F32) * ATTN_SCALE
    s = jnp.where(valid, s, NEG_INF)
    m = jnp.maximum(jnp.max(s, axis=0, keepdims=True), sink)
    p = jnp.exp(s - m)
    es = jnp.exp(sink - m)
    inv = 1.0 / (jnp.sum(p, axis=0, keepdims=True) + es)
    return p * inv, es * inv


def _attn_fwd(qn, kn, vb, sink_rows):
    t, cw = qn.shape
    kw = kn.shape[1]
    nkv = kw // HEAD_DIM

    def body(q_ref, kp_ref, kc_ref, vp_ref, vc_ref, sink_ref, o_ref):
        valid = _attn_mask(pl.program_id(0))
        qv = q_ref[...]
        kp, kc, vp, vc = kp_ref[...], kc_ref[...], vp_ref[...], vc_ref[...]
        outs = []
        for h in range(nkv):
            hs = slice(h * HEAD_DIM, (h + 1) * HEAD_DIM)
            k2 = jnp.concatenate([kp[:, hs], kc[:, hs]], axis=0)
            v2 = jnp.concatenate([vp[:, hs], vc[:, hs]], axis=0)
            pn, _ = _softmax_with_sink(_stack_heads(qv, h), k2, sink_ref, h, valid)
            o4 = lax.dot_general(pn.astype(BF16), v2, TN, preferred_element_type=F32)
            outs += [o4[g * BLOCK:(g + 1) * BLOCK] for g in range(GROUP)]
        o_ref[...] = jnp.concatenate(outs, axis=-1).astype(BF16)

    cur = lambda n: (n, 0)
    prev = lambda n: (jnp.maximum(n - 1, 0), 0)
    return pl.pallas_call(
        body, name="attn_fwd", grid=(t // BLOCK,),
        in_specs=[pl.BlockSpec((BLOCK, cw), cur),
                  pl.BlockSpec((BLOCK, kw), prev), pl.BlockSpec((BLOCK, kw), cur),
                  pl.BlockSpec((BLOCK, kw), prev), pl.BlockSpec((BLOCK, kw), cur),
                  pl.BlockSpec(sink_rows.shape, lambda n: (0, 0))],
        out_specs=pl.BlockSpec((BLOCK, cw), cur),
        out_shape=_sds((t, cw), BF16),
        compiler_params=_params(("parallel",), BLOCK * (cw + 4 * kw) * 2 + BLOCK * cw * 2, 8 << 20),
    )(qn, kn, kn, vb, vb, sink_rows)


def _attn_bwd(qn, kn, vb, sink_rows, do):
    t, cw = qn.shape
    kw = kn.shape[1]
    nkv = kw // HEAD_DIM
    nq = nkv * GROUP

    def body(q_ref, kp_ref, kc_ref, vp_ref, vc_ref, sink_ref, do_ref,
             dq_ref, dkc_ref, dkp_ref, dvc_ref, dvp_ref, dsink_ref):
        n = pl.program_id(0)
        valid = _attn_mask(n)
        qv, dov = q_ref[...], do_ref[...]
        kp, kc, vp, vc = kp_ref[...], kc_ref[...], vp_ref[...], vc_ref[...]
        dqs, dks, dvs, dsinks = [], [], [], []
        for h in range(nkv):
            hs = slice(h * HEAD_DIM, (h + 1) * HEAD_DIM)
            k2 = jnp.concatenate([kp[:, hs], kc[:, hs]], axis=0)
            v2 = jnp.concatenate([vp[:, hs], vc[:, hs]], axis=0)
            q4 = _stack_heads(qv, h)
            dob = _stack_heads(dov, h).astype(BF16)
            pn, psink = _softmax_with_sink(q4, k2, sink_ref, h, valid)
            dpn = lax.dot_general(v2, dob, NT, preferred_element_type=F32)
            dvs.append(jnp.dot(pn.astype(BF16), dob, preferred_element_type=F32))
            delta = jnp.sum(pn * dpn, axis=0, keepdims=True)
            ds = (pn * (dpn - delta) * ATTN_SCALE).astype(BF16)
            dks.append(jnp.dot(ds, q4, preferred_element_type=F32))
            dq4 = lax.dot_general(ds, k2, TN, preferred_element_type=F32)
            dsink4 = -psink * delta
            for g in range(GROUP):
                dqs.append(dq4[g * BLOCK:(g + 1) * BLOCK])
                dsinks.append(jnp.broadcast_to(jnp.sum(dsink4[:, g * BLOCK:(g + 1) * BLOCK], axis=1, keepdims=True), (1, LANES)))
        dq_ref[...] = jnp.concatenate(dqs, axis=-1)
        dkp_ref[...] = jnp.concatenate([d[:BLOCK] for d in dks], axis=-1)
        dkc_ref[...] = jnp.concatenate([d[BLOCK:] for d in dks], axis=-1)
        dvp_ref[...] = jnp.concatenate([d[:BLOCK] for d in dvs], axis=-1)
        dvc_ref[...] = jnp.concatenate([d[BLOCK:] for d in dvs], axis=-1)

        @pl.when(n == 0)
        def _():
            dsink_ref[...] = jnp.zeros_like(dsink_ref)

        dsink_ref[...] += jnp.concatenate(dsinks, axis=0)

    cur = lambda n: (n, 0)
    prev = lambda n: (jnp.maximum(n - 1, 0), 0)
    kspec = pl.BlockSpec((BLOCK, kw), cur)
    return pl.pallas_call(
        body, name="attn_bwd", grid=(t // BLOCK,),
        in_specs=[pl.BlockSpec((BLOCK, cw), cur),
                  pl.BlockSpec((BLOCK, kw), prev), kspec,
                  pl.BlockSpec((BLOCK, kw), prev), kspec,
                  pl.BlockSpec(sink_rows.shape, lambda n: (0, 0)),
                  pl.BlockSpec((BLOCK, cw), cur)],
        out_specs=[pl.BlockSpec((BLOCK, cw), cur), kspec, kspec, kspec, kspec,
                   pl.BlockSpec((nq, LANES), lambda n: (0, 0))],
        out_shape=[_sds((t, cw), F32)] + [_sds((t, kw), F32)] * 4 + [_sds((nq, LANES), F32)],
        compiler_params=_params(("arbitrary",), BLOCK * (cw + 4 * kw) * 2 + 2 * BLOCK * cw * 4 + 4 * BLOCK * kw * 4, 12 << 20),
    )(qn, kn, kn, vb, vb, sink_rows, do)


def _mix_out(ca, o, woc, woa, proj):
    t, cw = ca.shape
    nb = woc.shape[2]
    d = N_DEV * nb
    tm = min(t, 1024)
    ga0 = (3 * cw + cw + 2 * (cw // 4)) // nb

    def body(ca_ref, o_ref, woc_ref, woa_ref, ga_ref, gb_ref, m_ref, ya_ref, yb_ref):
        ya = jnp.dot(ca_ref[...], woc_ref[...], preferred_element_type=F32)
        yb = jnp.dot(o_ref[...], woa_ref[...], preferred_element_type=F32)
        ya_ref[...] = ya.astype(BF16)
        yb_ref[...] = yb.astype(BF16)
        m_ref[...] = (_sigmoid(ga_ref[...].astype(F32)) * ya + _sigmoid(gb_ref[...].astype(F32)) * yb).astype(BF16)

    act = pl.BlockSpec((tm, cw), lambda i, j: (i, 0))
    wsp = pl.BlockSpec((None, cw, nb), lambda i, j: (j, 0, 0))
    osp = pl.BlockSpec((tm, nb), lambda i, j: (i, j))
    blocks = 2 * tm * cw * 2 + 2 * cw * nb * 2 + 2 * tm * nb * 4 + 3 * tm * nb * 2
    return pl.pallas_call(
        body, name="mix_out", grid=(t // tm, N_DEV),
        in_specs=[act, act, wsp, wsp,
                  pl.BlockSpec((tm, nb), lambda i, j: (i, ga0 + j)),
                  pl.BlockSpec((tm, nb), lambda i, j: (i, ga0 + N_DEV + j))],
        out_specs=[osp, osp, osp],
        out_shape=[_sds((t, d), BF16)] * 3,
        compiler_params=_params(("parallel", "parallel"), blocks, 6 * tm * nb * 4),
    )(ca, o, woc, woa, proj, proj)


def _mix_residual(merged, wo, x):
    t, d = x.shape
    tm = min(t, 512)

    def epilogue(acc, ins, outs):
        outs[0][...] = ins[2][...] + acc

    row = pl.BlockSpec((tm, d), lambda i, j, k: (i, 0))
    return _fused("mix_residual", (t // tm, 1, 1),
                  [(merged, row), (wo, pl.BlockSpec((d, d), lambda i, j, k: (0, 0))), (x, row)],
                  [(_sds((t, d), F32), row)], [(0, 1, NN)], epilogue, temp_bytes=2 * tm * d * 4)[0]


def _mix_bwd_gates(dx, wo, ya, yb, proj, cw):
    t, d = dx.shape
    tm = min(t, 1024)
    tn = min(d, 512)
    ga0 = (4 * cw + 2 * (cw // 4)) // tn

    def epilogue(acc, ins, outs):
        sa, sb = _sigmoid(ins[4][...].astype(F32)), _sigmoid(ins[5][...].astype(F32))
        outs[0][...] = (acc * sa).astype(BF16)
        outs[1][...] = (acc * sb).astype(BF16)
        outs[2][0] = (acc * ins[2][...].astype(F32) * sa * (1.0 - sa)).astype(BF16)
        outs[2][1] = (acc * ins[3][...].astype(F32) * sb * (1.0 - sb)).astype(BF16)

    blk = pl.BlockSpec((tm, tn), lambda i, j, k: (i, j))
    return _fused("mix_bwd_gates", (t // tm, d // tn, 1),
                  [(dx, pl.BlockSpec((tm, d), lambda i, j, k: (i, 0))),
                   (wo, pl.BlockSpec((tn, d), lambda i, j, k: (j, 0))),
                   (ya, blk), (yb, blk),
                   (proj, pl.BlockSpec((tm, tn), lambda i, j, k: (i, ga0 + j))),
                   (proj, pl.BlockSpec((tm, tn), lambda i, j, k: (i, ga0 + d // tn + j)))],
                  [(_sds((t, d), BF16), blk), (_sds((t, d), BF16), blk),
                   (_sds((2, t, d), BF16), pl.BlockSpec((2, tm, tn), lambda i, j, k: (0, i, j)))],
                  [(0, 1, NT)], epilogue, temp_bytes=8 * tm * tn * 4)


def _tn_matmul(name, a, b, tm, out_dtype=BF16):
    t, m = a.shape
    n = b.shape[1]

    def epilogue(acc, ins, outs):
        outs[0][...] = acc.astype(out_dtype)

    return _fused(name, (m // tm, 1, 1),
                  [(a, pl.BlockSpec((t, tm), lambda i, j, k: (0, i))),
                   (b, pl.BlockSpec((t, n), lambda i, j, k: (0, 0)))],
                  [(_sds((m, n), out_dtype), pl.BlockSpec((tm, n), lambda i, j, k: (i, 0)))],
                  [(0, 1, TN)], epilogue, temp_bytes=2 * tm * n * 4)[0]


def _out_proj_bwd_act(dya, dyb, woc, woa, deps=()):
    t, d = dya.shape
    kdim, nb = woc.shape[1], woc.shape[2]
    tm = min(t, 512)

    def body(dya_ref, dyb_ref, woc_ref, woa_ref, *rest):
        for dy_ref, w_ref, o_ref in ((dya_ref, woc_ref, rest[-2]), (dyb_ref, woa_ref, rest[-1])):
            total = None
            for j in range(N_DEV):
                part = lax.dot_general(dy_ref[:, j * nb:(j + 1) * nb], w_ref[j], NT, preferred_element_type=F32)
                total = part if total is None else total + part
            o_ref[...] = total

    row = pl.BlockSpec((tm, d), lambda i: (i, 0))
    wsp = pl.BlockSpec((N_DEV, kdim, nb), lambda i: (0, 0, 0))
    osp = pl.BlockSpec((tm, kdim), lambda i: (i, 0))
    blocks = 2 * tm * d * 2 + 2 * N_DEV * kdim * nb * 2 + 2 * tm * kdim * 4
    return pl.pallas_call(
        body, name="mix_bwd_dca_do", grid=(t // tm,),
        in_specs=[row, row, wsp, wsp] + [_ANY] * len(deps), out_specs=[osp, osp],
        out_shape=[_sds((t, kdim), F32)] * 2,
        compiler_params=_params(("parallel",), blocks, 4 * tm * kdim * 4),
    )(dya, dyb, woc, woa, *deps)


def _out_proj_bwd_w(ca, o, dya, dyb, nb):
    t, kdim = ca.shape

    def body(ca_ref, o_ref, dya_ref, dyb_ref, dwoc_ref, dwoa_ref):
        dwoc_ref[...] = lax.dot_general(ca_ref[...], dya_ref[...], TN, preferred_element_type=F32).astype(BF16)
        dwoa_ref[...] = lax.dot_general(o_ref[...], dyb_ref[...], TN, preferred_element_type=F32).astype(BF16)

    act = pl.BlockSpec((t, kdim), lambda j: (0, 0))
    col = pl.BlockSpec((t, nb), lambda j: (0, j))
    osp = pl.BlockSpec((None, kdim, nb), lambda j: (j, 0, 0))
    blocks = 2 * t * kdim * 2 + 2 * t * nb * 2 + 2 * kdim * nb * 2
    return pl.pallas_call(
        body, name="mix_bwd_dwoc_dwoa", grid=(N_DEV,),
        in_specs=[act, act, col, col], out_specs=[osp, osp],
        out_shape=[_sds((N_DEV, kdim, nb), BF16)] * 2,
        compiler_params=_params(("parallel",), blocks, 4 * kdim * nb * 4),
    )(ca, o, dya, dyb)


def _proj_bwd_act(dproj, w_in, deps=()):
    t, n = dproj.shape
    d, nb = w_in.shape[2], w_in.shape[3]
    tm = min(t, 512)

    def epilogue(acc, ins, outs):
        outs[0][...] = acc

    def products(ins):
        return (lax.dot_general(ins[0][:, 0:nb], ins[1][0], NT, preferred_element_type=F32)
                + lax.dot_general(ins[0][:, nb:2 * nb], ins[1][1], NT, preferred_element_type=F32))

    return _fused("mix_bwd_dh", (t // tm, 1, 4),
                  [(dproj, pl.BlockSpec((tm, 2 * nb), lambda i, j, k: (i, k))),
                   (w_in, pl.BlockSpec((None, 2, d, nb), lambda i, j, k: (k, 0, 0, 0)))],
                  [(_sds((t, d), F32), pl.BlockSpec((tm, d), lambda i, j, k: (i, 0)))],
                  products, epilogue, nk=4, acc_shape=(tm, d), temp_bytes=tm * d * 4, deps=deps)[0]


def _proj_bwd_w(h, dproj):
    t, d = h.shape
    nb = dproj.shape[1] // N_DEV
    tm = min(d, 512)

    def body(h_ref, dp_ref, o_ref):
        hv = h_ref[...]
        o_ref[0] = lax.dot_general(hv, dp_ref[:, 0:nb], TN, preferred_element_type=F32).astype(BF16)
        o_ref[1] = lax.dot_general(hv, dp_ref[:, nb:2 * nb], TN, preferred_element_type=F32).astype(BF16)

    blocks = t * tm * 2 + t * 2 * nb * 2 + 2 * tm * nb * 2
    return pl.pallas_call(
        body, name="mix_bwd_dwin", grid=(4, d // tm),
        in_specs=[pl.BlockSpec((t, tm), lambda j, i: (0, i)),
                  pl.BlockSpec((t, 2 * nb), lambda j, i: (0, j))],
        out_specs=pl.BlockSpec((None, 2, tm, nb), lambda j, i: (j, 0, i, 0)),
        out_shape=_sds((4, 2, d, nb), BF16),
        compiler_params=_params(("parallel", "parallel"), blocks, 4 * tm * nb * 4),
    )(h, dproj)


def _adamw_math(w, g, m, v):
    m = ADAM_B1 * m + (1.0 - ADAM_B1) * g
    v = ADAM_B2 * v + (1.0 - ADAM_B2) * (g * g)
    m_hat = m / (1.0 - ADAM_B1 ** ADAM_STEP)
    v_hat = v / (1.0 - ADAM_B2 ** ADAM_STEP)
    delta = -ADAM_LR * (m_hat / (jnp.sqrt(v_hat) + ADAM_EPS) + ADAM_WD * w)
    return delta, m, v


def _adamw(name, parts, w, m, v, tr):
    r, c = w.shape

    def body(p_ref, w_ref, m_ref, v_ref, g_out, d_out, m_out, v_out):
        g = p_ref[0].astype(F32)
        for s in range(1, N_DEV):
            g = g + p_ref[s].astype(F32)
        delta, mn, vn = _adamw_math(w_ref[...], g, m_ref[...], v_ref[...])
        g_out[...] = g
        d_out[...] = delta
        m_out[...] = mn
        v_out[...] = vn

    blk = pl.BlockSpec((tr, c), lambda i: (i, 0))
    blocks = N_DEV * tr * c * parts.dtype.itemsize + 7 * tr * c * 4
    return pl.pallas_call(
        body, name=name, grid=(r // tr,),
        in_specs=[pl.BlockSpec((N_DEV, tr, c), lambda i: (0, i, 0)), blk, blk, blk],
        out_specs=[blk] * 4, out_shape=[_sds((r, c), F32)] * 4,
        compiler_params=_params(("parallel",), blocks, 6 * tr * c * 4),
    )(parts, w, m, v)


def _chip_sum(sums_ref):
    g = sums_ref[0].astype(F32)
    for k in range(1, 4):
        g = g + sums_ref[k].astype(F32)
    return g


def _adamw_chips(name, sums, w, m, v, tr, deps=(), row0=0, into=None):
    r, c = w.shape
    rs = sums.shape[1]
    i0 = row0 // tr
    n_pass = len(deps) + (4 if into is not None else 0)

    def body(sums_ref, w_ref, m_ref, v_ref, *rest):
        g_out, d_out, m_out, v_out = rest[n_pass:]
        g = _chip_sum(sums_ref)
        delta, mn, vn = _adamw_math(w_ref[...], g, m_ref[...], v_ref[...])
        g_out[...] = g
        d_out[...] = delta
        m_out[...] = mn
        v_out[...] = vn

    blk = pl.BlockSpec((tr, c), lambda i: (i0 + i, 0))
    blocks = 4 * tr * c * 2 + 7 * tr * c * 4
    passed = list(deps) + (list(into) if into is not None else [])
    aliases = {4 + len(deps) + q: q for q in range(4)} if into is not None else {}
    return pl.pallas_call(
        body, name=name, grid=(rs // tr,),
        in_specs=[pl.BlockSpec((4, tr, c), lambda i: (0, i, 0)), blk, blk, blk] + [_ANY] * n_pass,
        out_specs=[blk] * 4, out_shape=[_sds((r, c), F32)] * 4,
        input_output_aliases=aliases,
        compiler_params=_params(("parallel",), blocks, 6 * tr * c * 4),
    )(sums, w, m, v, *passed)


def _adamw_side(contrib, w, m, v, n_tiles, step_of):
    r, c = w.shape
    tr = r // n_tiles
    assert tr * n_tiles == r and tr % 16 == 0, (r, n_tiles)

    def tile(i, j, k):
        return jnp.minimum(step_of(i, j, k), n_tiles - 1)

    blk = pl.BlockSpec((tr, c), lambda i, j, k: (tile(i, j, k), 0))
    ins = [(contrib, pl.BlockSpec((4, tr, c), lambda i, j, k: (0, tile(i, j, k), 0))), (w, blk), (m, blk), (v, blk)]
    outs = [(_sds((r, c), F32), blk)] * 4

    def fn(in_refs, out_refs):
        @pl.when(step_of(pl.program_id(0), pl.program_id(1), pl.program_id(2)) < n_tiles)
        def _():
            g = _chip_sum(in_refs[0])
            delta, mn, vn = _adamw_math(in_refs[1][...], g, in_refs[2][...], in_refs[3][...])
            out_refs[0][...] = g
            out_refs[1][...] = delta
            out_refs[2][...] = mn
            out_refs[3][...] = vn

    return ins, outs, fn


def _rope_tables(t):
    half = ROT_DIM // 2
    inv_freq = 1.0 / (ROPE_THETA ** (jnp.arange(0, ROT_DIM, 2, dtype=F32) / ROT_DIM))
    ang = jnp.arange(t, dtype=F32)[:, None] * inv_freq[None, :]
    cos, sin = jnp.cos(ang), jnp.sin(ang)
    ones = jnp.ones((t, HEAD_DIM - ROT_DIM), F32)
    zeros = jnp.zeros((t, HEAD_DIM - half), F32)
    c = jnp.concatenate([cos, cos, ones], axis=1)
    sa = jnp.concatenate([-sin, zeros], axis=1)
    sb = jnp.concatenate([jnp.zeros((t, half), F32), sin, jnp.zeros((t, HEAD_DIM - ROT_DIM), F32)], axis=1)
    return tuple(jnp.tile(a, (1, LANES // HEAD_DIM)) for a in (c, sa, sb))


def _pad_rows(a, rows=8):
    return jnp.pad(a, ((0, rows - a.shape[0]), (0, 0)))


def kernel(x, g_ffn1, w_gu1, w_down1, g_mix, w_in, conv_w, q_norm_g, k_norm_g, sinks, w_out_conv, w_out_attn, w_o, g_ffn2, w_gu2, w_down2, loss_target, m_g_ffn1, m_w_gu1, m_w_down1, m_g_mix, m_w_in, m_conv_w, m_q_norm_g, m_k_norm_g, m_sinks, m_w_out_conv, m_w_out_attn, m_w_o, m_g_ffn2, m_w_gu2, m_w_down2, v_g_ffn1, v_w_gu1, v_w_down1, v_g_mix, v_w_in, v_conv_w, v_q_norm_g, v_k_norm_g, v_sinks, v_w_out_conv, v_w_out_attn, v_w_o, v_g_ffn2, v_w_gu2, v_w_down2):
    t, d = x.shape[1], x.shape[2]
    cw = d // 2
    kw = cw // GROUP
    nq = cw // HEAD_DIM
    xs, target = x.reshape(t, d), loss_target.reshape(t, d)
    me = 4 * lax.axis_index("x") + 2 * lax.axis_index("y") + lax.axis_index("c")

    big = {"w_gu1": w_gu1, "w_down1": w_down1, "w_in": w_in, "w_out_conv": w_out_conv,
           "w_out_attn": w_out_attn, "w_o": w_o, "w_gu2": w_gu2, "w_down2": w_down2}
    big_m = {"w_gu1": m_w_gu1, "w_down1": m_w_down1, "w_in": m_w_in, "w_out_conv": m_w_out_conv,
             "w_out_attn": m_w_out_attn, "w_o": m_w_o, "w_gu2": m_w_gu2, "w_down2": m_w_down2}
    big_v = {"w_gu1": v_w_gu1, "w_down1": v_w_down1, "w_in": v_w_in, "w_out_conv": v_w_out_conv,
             "w_out_attn": v_w_out_attn, "w_o": v_w_o, "w_gu2": v_w_gu2, "w_down2": v_w_down2}
    names = list(big)

    tiles = {"w_gu1": 256, "w_gu2": 256, "w_in": 256, "w_down1": 176, "w_down2": 176,
             "w_out_conv": 1024, "w_out_attn": 1024, "w_o": 128}

    def row_tile(n):
        r = big[n].shape[1]
        return tiles[n] if r % tiles[n] == 0 else r

    rs_shape = {n: big[n].shape[1:] for n in names}
    half = rs_shape["w_gu1"][0] // 2
    rs_shape["w_gu1_lo"] = rs_shape["w_gu1_hi"] = (half, rs_shape["w_gu1"][1])

    def add_tile(n):
        r, c = rs_shape[n]
        while r * c * 2 > (3 << 20) and r % 32 == 0:
            r //= 2
        return r

    me_arr = me.astype(jnp.int32).reshape(1)
    sources = [(n, big[n][0], BF16, row_tile(n)) for n in names] + [("conv_w", _pad_rows(conv_w[0]), F32, 8)]
    issue_order = [0, 1, 2, 8, 3, 4, 5, 6, 7]
    first = _place_shard("place_" + names[0], sources[0][1], BF16, me_arr, sources[0][3])
    started = [_gather_start("gather_start_first", [first])]
    early = {2: (big_m["w_in"][0], big_v["w_in"][0])}
    rest = [_place_shard("place_" + sources[i][0], sources[i][1], sources[i][2], me_arr, sources[i][3],
                         deps=(started[0][3],) + early.get(i, ())) for i in issue_order[1:]]
    started.append(_gather_start("gather_start_rest", rest))
    where = {0: (0, 0)}
    where.update({i: (1, p) for p, i in enumerate(issue_order[1:])})

    def fetch(tag, idxs, after, forward=True):
        call = where[idxs[0]][0]
        send, recv, stacks, _ = started[call]
        positions = [where[i][1] for i in idxs]
        got = _gather_wait("gather_wait_" + tag, positions, send, recv, [stacks[p] for p in positions], after)
        return _forward_to_sibling("gather_forward_" + tag, got) if forward else got

    rope_tabs = _rope_tables(t)
    gq = jnp.tile(q_norm_g, (1, nq))
    gk = jnp.tile(k_norm_g, (1, nq // GROUP))
    sink_rows = jnp.broadcast_to(sinks[0][:, None], (nq, LANES))

    wts = {}
    h1 = _rms_fwd("ffn1_norm", xs, g_ffn1)
    wts["w_gu1"], = fetch("gu1", [0], started[1][3])
    gu1, a1 = _ffn_up("ffn1_up", h1, wts["w_gu1"])
    wts["w_down1"], = fetch("down1", [1], a1)
    wd1 = wts["w_down1"].reshape(-1, d)
    x1 = _ffn_down("ffn1_down", a1, wd1, xs)
    h2 = _rms_fwd("mix_norm", x1, g_mix)
    wts["w_in"], conv_land = fetch("in", [2, 8], h2)
    w_in_full = wts["w_in"].reshape(4, 2, d, -1)
    conv_full = jnp.transpose(conv_land, (1, 0, 2)).reshape(8, cw)
    proj = _proj(h2, w_in_full)
    ca = _conv_fwd(proj, conv_full)
    qn, kn, vb = _qk_prep(proj, gq, gk, rope_tabs, cw, kw)
    o = _attn_fwd(qn, kn, vb, sink_rows)
    wts["w_out_conv"], wts["w_out_attn"] = fetch("out", [3, 4], o)
    merged, ya, yb = _mix_out(ca, o, wts["w_out_conv"], wts["w_out_attn"], proj)
    wts["w_o"], = fetch("o", [5], merged)
    wo = wts["w_o"].reshape(d, d)
    x2 = _mix_residual(merged, wo, x1)
    h3 = _rms_fwd("ffn2_norm", x2, g_ffn2)
    mine = lax.axis_index("c").astype(jnp.int32).reshape(1)
    got = fetch("gu2", [6], h3, forward=False)
    fsend, frecv, got = _forward_start("gather_forward_start_gu2", got)
    part = _ffn_up("ffn2_up_mine", h3, got[0], parity=mine)
    wts["w_gu2"], = _forward_wait("gather_forward_wait_gu2", fsend, frecv, got, part[1])
    gu2, a2 = _ffn_up("ffn2_up_sibling", h3, wts["w_gu2"], parity=1 - mine, into=part)
    wts["w_down2"], = fetch("down2", [7], a2)
    wd2 = wts["w_down2"].reshape(-1, d)
    dy, sq, dy_bf = _ffn_down("ffn2_down", a2, wd2, x2, target=target)
    loss = lax.psum(sq[0, 0] * (0.5 / d), ("x", "y", "c"))

    place = jnp.stack([lax.axis_index("c"), 2 * lax.axis_index("x") + lax.axis_index("y")]).astype(jnp.int32)
    def pair_start(tag, group, grads, deps=()):
        stacks = [grads[n].reshape((4, 2) + rs_shape[n]) for n in group]
        lands = [lax.empty((4,) + rs_shape[n], BF16) for n in group]
        return _pair_start("rs_pair_start_" + tag, stacks, lands, deps)

    def chip_start(tag, group, pending, after):
        send, recv, stacks, lands, _ = pending
        stacks, lands = _pair_wait("rs_pair_wait_" + tag, send, recv, stacks, lands, after)
        added = [_pair_add("rs_pair_add_" + n, st, ld, place, add_tile(n)) for n, st, ld in zip(group, stacks, lands)]
        return _chip_start("rs_chip_start_" + tag, [a[0] for a in added], [a[1] for a in added])

    group_a, group_b, group_c = ["w_down2", "w_gu2"], ["w_o", "w_out_conv", "w_out_attn"], ["w_in"]
    group_d, group_e, group_f = ["w_down1"], ["w_gu1_lo"], ["w_gu1_hi"]
    g = {}
    dgu2, a2 = _ffn_bwd_act("ffn2_bwd_act", dy_bf, wd2, gu2)
    pend_s = _sibling_start("rs_act_start_gu2", [dgu2, h3])
    g["w_down2"], = _ffn_bwd_dwd("ffn2_bwd_dwd", a2, dy_bf, deps=(pend_s[4],))
    pend_a = pair_start("a", ["w_down2"], g)
    dh3, = _ffn_bwd_dh("ffn2_bwd_dh", pend_s[2][0], wts["w_gu2"], deps=(pend_a[4],))
    (dgu2, h3), (dgu2_sib, h3_sib) = _sibling_wait("rs_act_wait_gu2", pend_s[0], pend_s[1], pend_s[2], pend_s[3], dh3)
    sums_gu2, slots_gu2 = _ffn_bwd_dwgu_pair("ffn2_bwd_dwgu", h3, h3_sib, dgu2, dgu2_sib, place)
    stacks_a, lands_a = _pair_wait("rs_pair_wait_a", pend_a[0], pend_a[1], pend_a[2], pend_a[3], sums_gu2)
    added_a = _pair_add("rs_pair_add_w_down2", stacks_a[0], lands_a[0], place, add_tile("w_down2"))
    ring_a = _chip_start("rs_chip_start_a", [added_a[0], sums_gu2], [added_a[1], slots_gu2])
    dx2, dg_ffn2, dx2_bf = _rms_bwd("ffn2_bwd_rms", x2, g_ffn2, dh3, dy, deps=(ring_a[4],), with_bf16=True)

    dya, dyb, dgates = _mix_bwd_gates(dx2_bf, wo, ya, yb, proj, cw)
    g["w_o"] = _tn_matmul("mix_bwd_dwo", merged, dx2_bf, min(d, 512))
    g["w_out_conv"], g["w_out_attn"] = _out_proj_bwd_w(ca, o, dya, dyb, d // N_DEV)
    pend_b = pair_start("b", group_b, g)
    dca, do = _out_proj_bwd_act(dya, dyb, wts["w_out_conv"], wts["w_out_attn"], deps=(pend_b[4],))
    ring_b = chip_start("b", group_b, pend_b, do)
    d3, dconv_w = _conv_bwd(proj, conv_full, dca, deps=(ring_b[4],))
    dq, dkc, dkp, dvc, dvp, dsink = _attn_bwd(qn, kn, vb, sink_rows, do)
    dqkv, dgq, dgk = _qk_prep_bwd(proj, gq, gk, rope_tabs, dq, dkc, dkp, dvc, dvp, cw, kw)
    dproj = jnp.concatenate([d3[0], d3[1], d3[2], dqkv, dgates[0], dgates[1]], axis=1)
    g["w_in"] = _proj_bwd_w(h2, dproj)
    pend_c = pair_start("c", group_c, g)
    dh2 = _proj_bwd_act(dproj, w_in_full, deps=(pend_c[4],))
    ring_c = chip_start("c", group_c, pend_c, dh2)
    dx1, dg_mix, dx1_bf = _rms_bwd("mix_bwd_rms", x1, g_mix, dh2, dx2, deps=(ring_c[4],), with_bf16=True)

    big_out = {}
    arrived = {}

    def wait_group(tag, group, ring, after):
        send, recv, parts, lands2, _ = ring
        parts, lands2 = _chip_wait("rs_chip_wait_" + tag, send, recv, parts, lands2, after)
        arrived.update(dict(zip(group, lands2)))

    def update(n, after):
        res = _adamw_chips("adamw_" + n, arrived[n], big[n][0], big_m[n][0], big_v[n][0], row_tile(n), deps=(after,))
        big_out[n] = [a[None] for a in res]
        return res[0]

    def update_beside(n, n_tiles, step_of):
        return _adamw_side(arrived[n], big[n][0], big_m[n][0], big_v[n][0], n_tiles, step_of)

    def keep(n, res):
        big_out[n] = [a[None] for a in res]

    dgu1, a1 = _ffn_bwd_act("ffn1_bwd_act", dx1_bf, wd1, gu1)
    pend_s = _sibling_start("rs_act_start_gu1", [dgu1, h1])
    wait_group("a", group_a, ring_a, pend_s[4])
    g["w_down1"], *res = _ffn_bwd_dwd("ffn1_bwd_dwd", a1, dx1_bf,
                                       side=update_beside("w_down2", 11, lambda i, j, k: i * 4 + j))
    keep("w_down2", res)
    pend_d = pair_start("d", group_d, g)
    (dgu1, h1), (dgu1_sib, h1_sib) = _sibling_wait("rs_act_wait_gu1", pend_s[0], pend_s[1], pend_s[2], pend_s[3],
                                                   pend_d[4])
    sums_lo, slots_lo, *res = _ffn_bwd_dwgu_pair_rows(
        "ffn1_bwd_dwgu_lo", h1, h1_sib, dgu1, dgu1_sib, place, (0, half),
        side=update_beside("w_gu2", 16, lambda i, j, k: i * 8 + j * 2 + k))
    keep("w_gu2", res)
    ring_d = chip_start("d", group_d, pend_d, sums_lo)
    ring_e = _chip_start("rs_chip_start_e", [sums_lo], [slots_lo], deps=(ring_d[4],))
    wait_group("c", group_c, ring_c, ring_e[4])
    sums_hi, slots_hi, *res = _ffn_bwd_dwgu_pair_rows(
        "ffn1_bwd_dwgu_hi", h1, h1_sib, dgu1, dgu1_sib, place, (half, half),
        side=update_beside("w_in", 16, lambda i, j, k: i * 8 + j * 2 + k))
    keep("w_in", res)
    ring_f = _chip_start("rs_chip_start_f", [sums_hi], [slots_hi])
    wait_group("b", group_b, ring_b, ring_f[4])
    after = ring_f[4]
    for n in group_b:
        after = update(n, after)
    wait_group("d", group_d, ring_d, after)
    dh1, *res = _ffn_bwd_dh("ffn1_bwd_dh", dgu1, wts["w_gu1"],
                             side=update_beside("w_down1", 11, lambda i, j, k: i * 4 + k))
    keep("w_down1", res)
    grad_x, dg_ffn1 = _rms_bwd("ffn1_bwd_rms", xs, g_ffn1, dh1, dx1)
    after = grad_x
    n = "w_gu1"
    wait_group("e", group_e, ring_e, after)
    res = _adamw_chips("adamw_w_gu1_lo", arrived["w_gu1_lo"], big[n][0], big_m[n][0], big_v[n][0], row_tile(n), deps=(after,))
    wait_group("f", group_f, ring_f, res[0])
    res = _adamw_chips("adamw_w_gu1_hi", arrived["w_gu1_hi"], big[n][0], big_m[n][0], big_v[n][0], row_tile(n),
                       row0=half, into=res)
    keep(n, res)
    after = res[0]

    small = {"g_ffn1": dg_ffn1[0:1], "g_mix": dg_mix[0:1], "g_ffn2": dg_ffn2[0:1],
             "q_norm_g": dgq[0:1, :HEAD_DIM], "k_norm_g": dgk[0:1, :HEAD_DIM], "sinks": dsink[:, 0][None],
             "conv_w": dconv_w[0:CONV_K].reshape(1, -1)}
    small_w = {"g_ffn1": g_ffn1, "g_mix": g_mix, "g_ffn2": g_ffn2, "q_norm_g": q_norm_g, "k_norm_g": k_norm_g,
               "sinks": sinks, "conv_w": None}
    small_m = {"g_ffn1": m_g_ffn1, "g_mix": m_g_mix, "g_ffn2": m_g_ffn2, "q_norm_g": m_q_norm_g,
               "k_norm_g": m_k_norm_g, "sinks": m_sinks, "conv_w": m_conv_w}
    small_v = {"g_ffn1": v_g_ffn1, "g_mix": v_g_mix, "g_ffn2": v_g_ffn2, "q_norm_g": v_q_norm_g,
               "k_norm_g": v_k_norm_g, "sinks": v_sinks, "conv_w": v_conv_w}
    snames = list(small)
    widths = [small[n].shape[1] for n in snames]
    total = sum(widths)
    rows = -(-total // LANES)
    rows = -(-rows // 8) * 8

    def pack(vals):
        flat = jnp.concatenate([v.reshape(1, -1) for v in vals], axis=1)
        return jnp.pad(flat, ((0, 0), (0, rows * LANES - total))).reshape(rows, LANES)

    csh = cw // N_DEV

    def place_conv(local, fill):
        full = jnp.full((CONV_K, cw), fill, F32)
        return lax.dynamic_update_slice(full, local, (0, me * csh)).reshape(1, -1)

    pw = pack([small_w[n] if n != "conv_w" else place_conv(conv_w[0], 0.0) for n in snames])
    pm = pack([small_m[n] if n != "conv_w" else place_conv(m_conv_w[0], 0.0) for n in snames])
    pv = pack([small_v[n] if n != "conv_w" else place_conv(v_conv_w[0], 1.0) for n in snames])
    parts = _all_gather_small("gather_small_grads", pack([small[n] for n in snames]), deps=(after,))
    sg, sd, sm, sv = [a.reshape(1, -1) for a in _adamw("adamw_small", parts, pw, pm, pv, rows)]

    def unpack(flat, n):
        off = sum(widths[:snames.index(n)])
        piece = flat[:, off:off + widths[snames.index(n)]]
        if n == "conv_w":
            piece = lax.dynamic_slice(piece.reshape(CONV_K, cw), (0, me * csh), (CONV_K, csh))[None]
        return piece

    order = ["g_ffn1", "w_gu1", "w_down1", "g_mix", "w_in", "conv_w", "q_norm_g", "k_norm_g", "sinks",
             "w_out_conv", "w_out_attn", "w_o", "g_ffn2", "w_gu2", "w_down2"]
    outs = [loss, grad_x[None]]
    for idx, flat in enumerate((sg, sd, sm, sv)):
        for n in order:
            outs.append(big_out[n][idx] if n in big_out else unpack(flat, n))
    return tuple(outs)
```

```python
import jax
import jax.numpy as jnp
from jax import lax
from jax.experimental import pallas as pl
from jax.experimental.pallas import tpu as pltpu

F32 = jnp.float32
BF16 = jnp.bfloat16

N_DEV = 8
HEAD_DIM = 64
GROUP = 4
BLOCK = 128
ROT_DIM = 16
ROPE_THETA = 500000.0
RMS_EPS = 1e-6
NEG_INF = -1e30
ATTN_SCALE = HEAD_DIM ** -0.5
CONV_K = 3
LANES = 128
MXU_COLS = 256
VMEM_BYTES_V7X = 64 * 1024 * 1024
VMEM_CAP = VMEM_BYTES_V7X - 6 * 1024 * 1024

ADAM_LR = 0.001
ADAM_B1 = 0.9
ADAM_B2 = 0.999
ADAM_EPS = 1e-08
ADAM_WD = 0.01
ADAM_STEP = 10

NN = (((1,), (0,)), ((), ()))
NT = (((1,), (1,)), ((), ()))
TN = (((0,), (0,)), ((), ()))

MESH = pl.DeviceIdType.MESH


def _nbytes(shape, dtype):
    n = 1
    for s in shape:
        if s is not None:
            n *= s
    return n * jnp.dtype(dtype).itemsize


def _params(semantics, block_bytes, temp_bytes):
    assert 2 * block_bytes + temp_bytes <= VMEM_CAP, (block_bytes, temp_bytes)
    return pltpu.CompilerParams(dimension_semantics=semantics, vmem_limit_bytes=VMEM_CAP)


def _fused(name, grid, ins, outs, dots, epilogue, *, nk=1, acc_shape=None, temp_bytes=0,
           semantics=("parallel", "parallel", "arbitrary"), deps=(), side=None, prefetch=None):
    n_main_in, n_main_out = len(ins), len(outs)
    n_pre = 0 if prefetch is None else 1
    if side is not None:
        side_ins, side_outs = list(side[0]), list(side[1])
        if n_pre:
            side_ins = [(a, _blind(spec)) for a, spec in side_ins]
            side_outs = [(a, _blind(spec)) for a, spec in side_outs]
        ins, outs = list(ins) + side_ins, list(outs) + side_outs
    n_in, n_out = len(ins), len(outs)
    n_dep = len(deps)

    def body(*refs):
        pre, refs = refs[:n_pre], refs[n_pre:]
        in_refs, out_refs = refs[:n_in], refs[n_in + n_dep:n_in + n_dep + n_out]
        scratch = refs[n_in + n_dep + n_out:]
        if side is not None:
            side[2](in_refs[n_main_in:], out_refs[n_main_out:])

        def products():
            if callable(dots):
                return dots(in_refs)
            total = None
            for ai, bi, contract in dots:
                a, b = in_refs[ai][...], in_refs[bi][...]
                a = a if a.dtype == BF16 else a.astype(BF16)
                b = b if b.dtype == BF16 else b.astype(BF16)
                p = lax.dot_general(a, b, contract, preferred_element_type=F32)
                total = p if total is None else total + p
            return total

        if nk == 1:
            epilogue(products() if dots else None, in_refs, out_refs, *pre)
        else:
            acc = scratch[0]
            k = pl.program_id(2)

            @pl.when(k == 0)
            def _():
                acc[...] = jnp.zeros_like(acc)

            acc[...] += products()

            @pl.when(k == nk - 1)
            def _():
                epilogue(acc[...], in_refs, out_refs, *pre)

    block_bytes = sum(_nbytes(spec.block_shape, a.dtype) for a, spec in ins)
    block_bytes += sum(_nbytes(spec.block_shape, s.dtype) for s, spec in outs)
    scratch_shapes = []
    if nk > 1:
        scratch_shapes.append(pltpu.VMEM(acc_shape, F32))
        temp_bytes += _nbytes(acc_shape, F32)
    in_specs = [spec for _, spec in ins] + [pl.BlockSpec(memory_space=pl.ANY)] * n_dep
    out_specs = [spec for _, spec in outs]
    if n_pre:
        grid_spec = pltpu.PrefetchScalarGridSpec(num_scalar_prefetch=1, grid=grid, in_specs=in_specs,
                                                 out_specs=out_specs, scratch_shapes=scratch_shapes)
        return pl.pallas_call(
            body, name=name, grid_spec=grid_spec, out_shape=[s for s, _ in outs],
            compiler_params=_params(semantics, block_bytes, temp_bytes),
        )(prefetch, *[a for a, _ in ins], *deps)
    res = pl.pallas_call(
        body, name=name, grid=grid,
        in_specs=in_specs,
        out_specs=out_specs,
        out_shape=[s for s, _ in outs],
        scratch_shapes=scratch_shapes,
        compiler_params=_params(semantics, block_bytes, temp_bytes),
    )(*[a for a, _ in ins], *deps)
    return res


def _blind(spec):
    index_map = spec.index_map
    return pl.BlockSpec(spec.block_shape, lambda *a: index_map(*a[:-1]))


def _sds(shape, dtype):
    return jax.ShapeDtypeStruct(shape, dtype)


def _sigmoid(x):
    return jax.nn.sigmoid(x)


def _all_gather_small(name, shard, deps=()):
    n_dep = len(deps)

    def body(src, *rest):
        dst, send_sems, recv_sems, local_sem = rest[n_dep:]
        x, y, c = lax.axis_index("x"), lax.axis_index("y"), lax.axis_index("c")
        me = 4 * x + 2 * y + c
        copies = [pltpu.make_async_copy(src, dst.at[me], local_sem)]
        for k in range(1, N_DEV):
            peer = ((1 - x) if (k & 4) else x, (1 - y) if (k & 2) else y, (1 - c) if (k & 1) else c)
            copies.append(pltpu.make_async_remote_copy(
                src_ref=src, dst_ref=dst.at[me], send_sem=send_sems.at[k - 1], recv_sem=recv_sems.at[k - 1],
                device_id=peer, device_id_type=MESH))
        for cp in copies:
            cp.start()
        for cp in copies:
            cp.wait()

    hbm = pl.BlockSpec(memory_space=pltpu.HBM)
    return pl.pallas_call(
        body, name=name,
        in_specs=[hbm] + [pl.BlockSpec(memory_space=pl.ANY)] * n_dep, out_specs=hbm,
        out_shape=_sds((N_DEV,) + shard.shape, shard.dtype),
        scratch_shapes=[pltpu.SemaphoreType.DMA((N_DEV - 1,)), pltpu.SemaphoreType.DMA((N_DEV - 1,)),
                        pltpu.SemaphoreType.DMA],
    )(shard, *deps)


_HBM = pl.BlockSpec(memory_space=pltpu.HBM)
_SEM = pl.BlockSpec(memory_space=pltpu.SEMAPHORE)
_ANY = pl.BlockSpec(memory_space=pl.ANY)
_EFFECT = pltpu.SideEffectType.DATAFLOW_SIDE_EFFECTING
N_TARGETS = 4


def _mesh_pos():
    return lax.axis_index("x"), lax.axis_index("y"), lax.axis_index("c")


def _chip_peers(x, y, c):
    return [(1 - x, y, c), (x, 1 - y, c), (1 - x, 1 - y, c)]


def _dev_index(pos):
    return 4 * pos[0] + 2 * pos[1] + pos[2]


def _hbm_like(a):
    return pltpu.HBM(a.shape, a.dtype)


def _place_shard(name, w, out_dtype, me, tr, deps=()):
    r, c = w.shape
    n_dep = len(deps)

    def body(me_ref, w_ref, *rest):
        rest[n_dep][...] = w_ref[...].astype(out_dtype)

    grid_spec = pltpu.PrefetchScalarGridSpec(
        num_scalar_prefetch=1, grid=(r // tr,),
        in_specs=[pl.BlockSpec((tr, c), lambda i, me_ref: (i, 0))] + [_ANY] * n_dep,
        out_specs=pl.BlockSpec((None, tr, c), lambda i, me_ref: (me_ref[0], i, 0)))
    return pl.pallas_call(
        body, name=name, grid_spec=grid_spec, out_shape=_sds((N_DEV, r, c), out_dtype),
        compiler_params=_params(("parallel",), tr * c * 6, tr * c * 4),
    )(me, w, *deps)


def _gather_start(name, lands):
    n = len(lands)

    def body(*refs):
        bufs = refs[:n]
        send, recv = refs[n], refs[n + 1]
        token = refs[-1]
        x, y, c = _mesh_pos()
        me = _dev_index((x, y, c))
        targets = [(x, y, 1 - c)] + _chip_peers(x, y, c)
        for w in range(n):
            for k, to in enumerate(targets):
                pltpu.make_async_remote_copy(
                    src_ref=bufs[w].at[me], dst_ref=bufs[w].at[me],
                    send_sem=send.at[N_TARGETS * w + k], recv_sem=recv.at[N_TARGETS * w + k],
                    device_id=to, device_id_type=MESH).start()
        token[...] = jnp.zeros_like(token)

    sems = pltpu.SemaphoreType.DMA((N_TARGETS * n,))
    outs = pl.pallas_call(
        body, name=name,
        in_specs=[_HBM] * n, out_specs=[_SEM, _SEM] + [_HBM] * n + [_token_spec()],
        out_shape=[sems, sems] + [_hbm_like(a) for a in lands] + [_sds((8, LANES), F32)],
        input_output_aliases={i: 2 + i for i in range(n)},
        compiler_params=pltpu.CompilerParams(has_side_effects=_EFFECT),
    )(*lands)
    return outs[0], outs[1], list(outs[2:2 + n]), outs[-1]


def _gather_wait(name, positions, send, recv, lands, after):
    m = len(positions)

    def body(*refs):
        bufs = refs[:m]
        send_sems, recv_sems = refs[m], refs[m + 1]
        x, y, c = _mesh_pos()
        me = _dev_index((x, y, c))
        sources = [(x, y, 1 - c)] + _chip_peers(x, y, c)
        for j, w in enumerate(positions):
            for k, frm in enumerate(sources):
                cp = pltpu.make_async_remote_copy(
                    src_ref=bufs[j].at[me], dst_ref=bufs[j].at[_dev_index(frm)],
                    send_sem=send_sems.at[N_TARGETS * w + k], recv_sem=recv_sems.at[N_TARGETS * w + k],
                    device_id=frm, device_id_type=MESH)
                cp.wait_send()
                cp.wait_recv()

    outs = pl.pallas_call(
        body, name=name,
        in_specs=[_HBM] * m + [_SEM, _SEM, _ANY], out_specs=[_HBM] * m,
        out_shape=[_hbm_like(a) for a in lands],
        input_output_aliases={i: i for i in range(m)},
        compiler_params=pltpu.CompilerParams(has_side_effects=_EFFECT),
    )(*lands, send, recv, after)
    return list(outs)


def _forward_to_sibling(name, lands):
    m = len(lands)

    def body(*refs):
        copies = _forward_copies(refs[m:2 * m], refs[2 * m], refs[2 * m + 1])
        for cp in copies:
            cp.start()
        for cp in copies:
            cp.wait()

    outs = pl.pallas_call(
        body, name=name,
        in_specs=[_HBM] * m, out_specs=[_HBM] * m,
        out_shape=[_sds(a.shape, a.dtype) for a in lands],
        input_output_aliases={i: i for i in range(m)},
        scratch_shapes=[pltpu.SemaphoreType.DMA((3 * m,)), pltpu.SemaphoreType.DMA((3 * m,))],
    )(*lands)
    return list(outs)


def _forward_copies(bufs, send, recv):
    x, y, c = _mesh_pos()
    copies = []
    for j, buf in enumerate(bufs):
        for k, chip in enumerate(_chip_peers(x, y, c)):
            block = buf.at[_dev_index(chip)]
            copies.append(pltpu.make_async_remote_copy(
                src_ref=block, dst_ref=block, send_sem=send.at[3 * j + k], recv_sem=recv.at[3 * j + k],
                device_id=(x, y, 1 - c), device_id_type=MESH))
    return copies


def _forward_start(name, lands):
    m = len(lands)

    def body(*refs):
        for cp in _forward_copies(refs[:m], refs[m], refs[m + 1]):
            cp.start()

    sems = pltpu.SemaphoreType.DMA((3 * m,))
    outs = pl.pallas_call(
        body, name=name,
        in_specs=[_HBM] * m, out_specs=[_SEM, _SEM] + [_HBM] * m,
        out_shape=[sems, sems] + [_hbm_like(a) for a in lands],
        input_output_aliases={i: 2 + i for i in range(m)},
        compiler_params=pltpu.CompilerParams(has_side_effects=_EFFECT),
    )(*lands)
    return outs[0], outs[1], list(outs[2:])


def _forward_wait(name, send, recv, lands, after):
    m = len(lands)

    def body(*refs):
        for cp in _forward_copies(refs[:m], refs[m], refs[m + 1]):
            cp.wait_send()
            cp.wait_recv()

    outs = pl.pallas_call(
        body, name=name,
        in_specs=[_HBM] * m + [_SEM, _SEM, _ANY], out_specs=[_HBM] * m,
        out_shape=[_hbm_like(a) for a in lands],
        input_output_aliases={i: i for i in range(m)},
        compiler_params=pltpu.CompilerParams(has_side_effects=_EFFECT),
    )(*lands, send, recv, after)
    return list(outs)


def _token_spec():
    return pl.BlockSpec(memory_space=pltpu.VMEM)


def _pair_start(name, stacks, lands, deps=()):
    n = len(stacks)
    n_dep = len(deps)

    def body(*refs):
        srcs, dsts = refs[:n], refs[n:2 * n]
        send, recv = refs[2 * n + n_dep], refs[2 * n + n_dep + 1]
        token = refs[-1]
        x, y, c = _mesh_pos()
        for w in range(n):
            for chip in range(4):
                pltpu.make_async_remote_copy(
                    src_ref=srcs[w].at[chip, 1 - c], dst_ref=dsts[w].at[chip],
                    send_sem=send.at[4 * w + chip], recv_sem=recv.at[4 * w + chip],
                    device_id=(x, y, 1 - c), device_id_type=MESH).start()
        token[...] = jnp.zeros_like(token)

    sems = pltpu.SemaphoreType.DMA((4 * n,))
    outs = pl.pallas_call(
        body, name=name,
        in_specs=[_HBM] * (2 * n) + [_ANY] * n_dep, out_specs=[_SEM, _SEM] + [_HBM] * (2 * n) + [_token_spec()],
        out_shape=[sems, sems] + [_hbm_like(a) for a in stacks] + [_hbm_like(a) for a in lands] + [_sds((8, LANES), F32)],
        input_output_aliases={i: 2 + i for i in range(2 * n)},
        compiler_params=pltpu.CompilerParams(has_side_effects=_EFFECT),
    )(*stacks, *lands, *deps)
    return outs[0], outs[1], list(outs[2:2 + n]), list(outs[2 + n:2 + 2 * n]), outs[-1]


def _pair_wait(name, send, recv, stacks, lands, after):
    n = len(stacks)

    def body(*refs):
        srcs, dsts = refs[:n], refs[n:2 * n]
        send_sems, recv_sems = refs[2 * n], refs[2 * n + 1]
        x, y, c = _mesh_pos()
        for w in range(n):
            for chip in range(4):
                cp = pltpu.make_async_remote_copy(
                    src_ref=srcs[w].at[chip, 1 - c], dst_ref=dsts[w].at[chip],
                    send_sem=send_sems.at[4 * w + chip], recv_sem=recv_sems.at[4 * w + chip],
                    device_id=(x, y, 1 - c), device_id_type=MESH)
                cp.wait_send()
                cp.wait_recv()

    outs = pl.pallas_call(
        body, name=name,
        in_specs=[_HBM] * (2 * n) + [_SEM, _SEM, _ANY], out_specs=[_HBM] * (2 * n),
        out_shape=[_hbm_like(a) for a in stacks] + [_hbm_like(a) for a in lands],
        input_output_aliases={i: i for i in range(2 * n)},
        compiler_params=pltpu.CompilerParams(has_side_effects=_EFFECT),
    )(*stacks, *lands, send, recv, after)
    return list(outs[:n]), list(outs[n:])


def _pair_add(name, stack, land, place, tr):
    _, _, r, c = stack.shape

    def body(place_ref, a_ref, b_ref, sums_ref, slots_ref):
        total = (a_ref[...].astype(F32) + b_ref[...].astype(F32)).astype(BF16)
        sums_ref[...] = total

        @pl.when(pl.program_id(1) == place_ref[1])
        def _():
            slots_ref[...] = total

    grid_spec = pltpu.PrefetchScalarGridSpec(
        num_scalar_prefetch=1, grid=(r // tr, 4),
        in_specs=[pl.BlockSpec((None, None, tr, c), lambda i, k, place_ref: (k, place_ref[0], i, 0)),
                  pl.BlockSpec((None, tr, c), lambda i, k, place_ref: (k, i, 0))],
        out_specs=[pl.BlockSpec((None, tr, c), lambda i, k, place_ref: (k, i, 0)),
                   pl.BlockSpec((None, tr, c), lambda i, k, place_ref: (place_ref[1], i, 0))])
    return pl.pallas_call(
        body, name=name, grid_spec=grid_spec, out_shape=[_sds((4, r, c), BF16)] * 2,
        compiler_params=_params(("parallel", "arbitrary"), 4 * tr * c * 2, 3 * tr * c * 4),
    )(place, stack, land)


def _sibling_copies(srcs, dsts, send, recv):
    x, y, c = _mesh_pos()
    copies = []
    for w in range(len(srcs)):
        if len(srcs[w].shape) == 2:
            pairs = [(srcs[w], dsts[w])]
        else:
            nb = srcs[w].shape[2] // 4
            pairs = []
            for k in range(4):
                dev = 2 * k + 1 - c
                col = pl.multiple_of((dev % 4) * nb, LANES)
                pairs.append((srcs[w].at[dev // 4, :, pl.ds(col, nb)], dsts[w].at[k]))
        for src, dst in pairs:
            q = len(copies)
            copies.append(pltpu.make_async_remote_copy(
                src_ref=src, dst_ref=dst, send_sem=send.at[q], recv_sem=recv.at[q],
                device_id=(x, y, 1 - c), device_id_type=MESH))
    return copies


def _sibling_start(name, srcs):
    n = len(srcs)
    lands = [lax.empty(a.shape if a.ndim == 2 else (4, a.shape[1], a.shape[2] // 4), a.dtype) for a in srcs]

    def body(*refs):
        for cp in _sibling_copies(refs[:n], refs[n:2 * n], refs[2 * n], refs[2 * n + 1]):
            cp.start()
        token = refs[-1]
        token[...] = jnp.zeros_like(token)

    sems = pltpu.SemaphoreType.DMA((sum(1 if a.ndim == 2 else 4 for a in srcs),))
    outs = pl.pallas_call(
        body, name=name,
        in_specs=[_HBM] * (2 * n), out_specs=[_SEM, _SEM] + [_HBM] * (2 * n) + [_token_spec()],
        out_shape=[sems, sems] + [_hbm_like(a) for a in srcs] + [_hbm_like(a) for a in lands] + [_sds((8, LANES), F32)],
        input_output_aliases={i: 2 + i for i in range(2 * n)},
        compiler_params=pltpu.CompilerParams(has_side_effects=_EFFECT),
    )(*srcs, *lands)
    return outs[0], outs[1], list(outs[2:2 + n]), list(outs[2 + n:2 + 2 * n]), outs[-1]


def _sibling_wait(name, send, recv, srcs, lands, after):
    n = len(srcs)

    def body(*refs):
        for cp in _sibling_copies(refs[:n], refs[n:2 * n], refs[2 * n], refs[2 * n + 1]):
            cp.wait_send()
            cp.wait_recv()

    outs = pl.pallas_call(
        body, name=name,
        in_specs=[_HBM] * (2 * n) + [_SEM, _SEM, _ANY], out_specs=[_HBM] * (2 * n),
        out_shape=[_hbm_like(a) for a in srcs] + [_hbm_like(a) for a in lands],
        input_output_aliases={i: i for i in range(2 * n)},
        compiler_params=pltpu.CompilerParams(has_side_effects=_EFFECT),
    )(*srcs, *lands, send, recv, after)
    return list(outs[:n]), list(outs[n:])


def _chip_start(name, parts, lands, deps=(), own=False):
    n = len(parts)
    n_dep = len(deps)

    def body(*refs):
        srcs, dsts = refs[:n], refs[n:2 * n]
        send, recv = refs[2 * n + n_dep], refs[2 * n + n_dep + 1]
        token = refs[2 * n + n_dep + 2 + 2 * n]
        x, y, c = _mesh_pos()
        if own:
            local = refs[-1]
            for w in range(n):
                cp = pltpu.make_async_copy(srcs[w].at[2 * x + y], dsts[w].at[2 * x + y], local.at[w])
                cp.start()
                cp.wait()
        for w in range(n):
            for k, to in enumerate(_chip_peers(x, y, c)):
                pltpu.make_async_remote_copy(
                    src_ref=srcs[w].at[2 * to[0] + to[1]], dst_ref=dsts[w].at[2 * x + y],
                    send_sem=send.at[3 * w + k], recv_sem=recv.at[3 * w + k],
                    device_id=to, device_id_type=MESH).start()
        token[...] = jnp.zeros_like(token)

    sems = pltpu.SemaphoreType.DMA((3 * n,))
    outs = pl.pallas_call(
        body, name=name,
        in_specs=[_HBM] * (2 * n) + [_ANY] * n_dep, out_specs=[_SEM, _SEM] + [_HBM] * (2 * n) + [_token_spec()],
        out_shape=[sems, sems] + [_hbm_like(a) for a in parts] + [_hbm_like(a) for a in lands] + [_sds((8, LANES), F32)],
        input_output_aliases={i: 2 + i for i in range(2 * n)},
        scratch_shapes=[pltpu.SemaphoreType.DMA((n,))] if own else [],
        compiler_params=pltpu.CompilerParams(has_side_effects=_EFFECT),
    )(*parts, *lands, *deps)
    return outs[0], outs[1], list(outs[2:2 + n]), list(outs[2 + n:2 + 2 * n]), outs[-1]


def _chip_wait(name, send, recv, parts, lands, after):
    n = len(parts)

    def body(*refs):
        srcs, dsts = refs[:n], refs[n:2 * n]
        send_sems, recv_sems = refs[2 * n], refs[2 * n + 1]
        x, y, c = _mesh_pos()
        for w in range(n):
            for k, frm in enumerate(_chip_peers(x, y, c)):
                chip = 2 * frm[0] + frm[1]
                cp = pltpu.make_async_remote_copy(
                    src_ref=srcs[w].at[chip], dst_ref=dsts[w].at[chip],
                    send_sem=send_sems.at[3 * w + k], recv_sem=recv_sems.at[3 * w + k],
                    device_id=frm, device_id_type=MESH)
                cp.wait_send()
                cp.wait_recv()

    outs = pl.pallas_call(
        body, name=name,
        in_specs=[_HBM] * (2 * n) + [_SEM, _SEM, _ANY], out_specs=[_HBM] * (2 * n),
        out_shape=[_hbm_like(a) for a in parts] + [_hbm_like(a) for a in lands],
        input_output_aliases={i: i for i in range(2 * n)},
        compiler_params=pltpu.CompilerParams(has_side_effects=_EFFECT),
    )(*parts, *lands, send, recv, after)
    return list(outs[:n]), list(outs[n:])


def _row_tile(t):
    return min(t, 256)


def _rms_fwd(name, x, g):
    t, d = x.shape
    tm = _row_tile(t)

    def epilogue(_, ins, outs):
        xv = ins[0][...]
        r = lax.rsqrt(jnp.mean(xv * xv, axis=-1, keepdims=True) + RMS_EPS)
        outs[0][...] = (xv * r * ins[1][...]).astype(BF16)

    row = pl.BlockSpec((tm, d), lambda i, j, k: (i, 0))
    vec = pl.BlockSpec((1, d), lambda i, j, k: (0, 0))
    return _fused(name, (t // tm, 1, 1), [(x, row), (g, vec)], [(_sds((t, d), BF16), row)], [], epilogue,
                  temp_bytes=4 * tm * d * 4)[0]


def _rms_bwd(name, x, g, dh, resid, deps=(), with_bf16=False):
    t, d = x.shape
    tm = _row_tile(t)

    def epilogue(_, ins, outs):
        xv, gv, dhv = ins[0][...], ins[1][...], ins[2][...]
        r = lax.rsqrt(jnp.mean(xv * xv, axis=-1, keepdims=True) + RMS_EPS)
        xh = xv * r
        u = dhv * gv
        dot = jnp.mean(u * xh, axis=-1, keepdims=True)
        dx = ins[3][...] + r * (u - xh * dot)
        outs[0][...] = dx
        if with_bf16:
            outs[2][...] = dx.astype(BF16)

        @pl.when(pl.program_id(0) == 0)
        def _():
            outs[1][...] = jnp.zeros_like(outs[1])

        outs[1][0:1, :] += jnp.sum(dhv * xh, axis=0, keepdims=True)

    row = pl.BlockSpec((tm, d), lambda i, j, k: (i, 0))
    vec = pl.BlockSpec((1, d), lambda i, j, k: (0, 0))
    acc = pl.BlockSpec((8, d), lambda i, j, k: (0, 0))
    outs = [(_sds((t, d), F32), row), (_sds((8, d), F32), acc)] + ([(_sds((t, d), BF16), row)] if with_bf16 else [])
    return _fused(name, (t // tm, 1, 1), [(x, row), (g, vec), (dh, row), (resid, row)], outs, [], epilogue,
                  temp_bytes=6 * tm * d * 4, semantics=("arbitrary", "arbitrary", "arbitrary"), deps=deps)


def _ffn_up(name, h, wgu, parity=None, into=None):
    t, d = h.shape
    nb = wgu.shape[2]
    f = 4 * nb
    tm = min(t, 512)

    def body(h_ref, wg_ref, wu_ref, gu_ref, a_ref):
        hv = h_ref[...]
        for c0 in range(0, nb, MXU_COLS):
            cs = slice(c0, min(c0 + MXU_COLS, nb))
            g = jnp.dot(hv, wg_ref[:, cs], preferred_element_type=F32)
            u = jnp.dot(hv, wu_ref[:, cs], preferred_element_type=F32)
            gu_ref[0, :, cs] = g.astype(BF16)
            gu_ref[1, :, cs] = u.astype(BF16)
            a_ref[:, cs] = (g * _sigmoid(g) * u).astype(BF16)

    blocks = tm * d * 2 + 2 * d * nb * 2 + 3 * tm * nb * 2
    params = _params(("parallel", "parallel"), blocks, 8 * tm * MXU_COLS * 4)
    out_shape = [_sds((2, t, f), BF16), _sds((t, f), BF16)]
    if parity is None:
        return pl.pallas_call(
            body, name=name, grid=(4, t // tm),
            in_specs=[pl.BlockSpec((tm, d), lambda j, i: (i, 0)),
                      pl.BlockSpec((None, d, nb), lambda j, i: (j, 0, 0)),
                      pl.BlockSpec((None, d, nb), lambda j, i: (j + 4, 0, 0))],
            out_specs=[pl.BlockSpec((2, tm, nb), lambda j, i: (0, i, j)),
                       pl.BlockSpec((tm, nb), lambda j, i: (i, j))],
            out_shape=out_shape, compiler_params=params,
        )(h, wgu, wgu)

    def half_body(parity_ref, h_ref, wg_ref, wu_ref, *rest):
        body(h_ref, wg_ref, wu_ref, rest[-2], rest[-1])

    n_pass = 0 if into is None else 2
    grid_spec = pltpu.PrefetchScalarGridSpec(
        num_scalar_prefetch=1, grid=(2, t // tm),
        in_specs=[pl.BlockSpec((tm, d), lambda jj, i, p: (i, 0)),
                  pl.BlockSpec((None, d, nb), lambda jj, i, p: (2 * jj + p[0], 0, 0)),
                  pl.BlockSpec((None, d, nb), lambda jj, i, p: (2 * jj + p[0] + 4, 0, 0))] + [_ANY] * n_pass,
        out_specs=[pl.BlockSpec((2, tm, nb), lambda jj, i, p: (0, i, 2 * jj + p[0])),
                   pl.BlockSpec((tm, nb), lambda jj, i, p: (i, 2 * jj + p[0]))])
    return pl.pallas_call(
        half_body, name=name, grid_spec=grid_spec, out_shape=out_shape,
        input_output_aliases={} if into is None else {4: 0, 5: 1}, compiler_params=params,
    )(parity, h, wgu, wgu, *(into or ()))


def _ffn_down(name, a, wd, x, target=None):
    t, f = a.shape
    d = wd.shape[1]
    tm = min(t, 512)
    tn = min(d, 1024)
    blk = pl.BlockSpec((tm, tn), lambda j, i, k: (i, j))
    ins = [(a, pl.BlockSpec((tm, f), lambda j, i, k: (i, 0))), (wd, pl.BlockSpec((f, tn), lambda j, i, k: (0, j))), (x, blk)]

    if target is None:
        def epilogue(acc, ins, outs):
            outs[0][...] = ins[2][...] + 0.5 * acc

        return _fused(name, (d // tn, t // tm, 1), ins, [(_sds((t, d), F32), blk)],
                      [(0, 1, NN)], epilogue, temp_bytes=2 * tm * tn * 4)[0]

    def epilogue(acc, ins, outs):
        e = ins[2][...] + 0.5 * acc - ins[3][...]
        outs[0][...] = e * (1.0 / d)
        outs[2][...] = (e * (1.0 / d)).astype(BF16)

        @pl.when((pl.program_id(0) == 0) & (pl.program_id(1) == 0))
        def _():
            outs[1][...] = jnp.zeros_like(outs[1])

        part = jnp.sum(jnp.sum(e * e, axis=1, keepdims=True), axis=0, keepdims=True)
        outs[1][...] += jnp.broadcast_to(part, outs[1].shape)

    return _fused(name, (d // tn, t // tm, 1), ins + [(target, blk)],
                  [(_sds((t, d), F32), blk), (_sds((8, LANES), F32), pl.BlockSpec((8, LANES), lambda j, i, k: (0, 0))),
                   (_sds((t, d), BF16), blk)],
                  [(0, 1, NN)], epilogue, temp_bytes=3 * tm * tn * 4,
                  semantics=("arbitrary", "arbitrary", "arbitrary"))


def _ffn_bwd_act(name, dy, wd, gu, deps=()):
    t, d = dy.shape
    f = wd.shape[0]
    nb = f // 4
    tm = min(t, 512)

    def body(dy_ref, wd_ref, gu_ref, *rest):
        dgu_ref, a_ref = rest[-2], rest[-1]
        dyv = dy_ref[...].astype(BF16)
        for c0 in range(0, nb, MXU_COLS):
            cs = slice(c0, min(c0 + MXU_COLS, nb))
            da = 0.5 * lax.dot_general(dyv, wd_ref[cs, :], NT, preferred_element_type=F32)
            g = gu_ref[0, :, cs].astype(F32)
            u = gu_ref[1, :, cs].astype(F32)
            s = _sigmoid(g)
            silu = g * s
            dgu_ref[0, :, cs] = (da * u * (s * (1.0 + g * (1.0 - s)))).astype(BF16)
            dgu_ref[1, :, cs] = (da * silu).astype(BF16)
            a_ref[:, cs] = (silu * u).astype(BF16)

    blocks = tm * d * 4 + nb * d * 2 + 5 * tm * nb * 2
    return pl.pallas_call(
        body, name=name, grid=(4, t // tm),
        in_specs=[pl.BlockSpec((tm, d), lambda j, i: (i, 0)),
                  pl.BlockSpec((nb, d), lambda j, i: (j, 0)),
                  pl.BlockSpec((2, tm, nb), lambda j, i: (0, i, j))] + [_ANY] * len(deps),
        out_specs=[pl.BlockSpec((2, tm, nb), lambda j, i: (0, i, j)), pl.BlockSpec((tm, nb), lambda j, i: (i, j))],
        out_shape=[_sds((2, t, f), BF16), _sds((t, f), BF16)],
        compiler_params=_params(("parallel", "parallel"), blocks, tm * d * 2 + 8 * tm * MXU_COLS * 4),
    )(dy, wd, gu, *deps)


def _ffn_bwd_dwd(name, a, dy, deps=(), side=None):
    t, f = a.shape
    d = dy.shape[1]
    tm = f // 4
    tn = min(d, 512)

    def epilogue(acc, ins, outs):
        outs[0][...] = (0.5 * acc).astype(BF16)

    return _fused(name, (4, d // tn, 1),
                  [(a, pl.BlockSpec((t, tm), lambda i, j, k: (0, i))),
                   (dy, pl.BlockSpec((t, tn), lambda i, j, k: (0, j)))],
                  [(_sds((f, d), BF16), pl.BlockSpec((tm, tn), lambda i, j, k: (i, j)))],
                  [(0, 1, TN)], epilogue, temp_bytes=t * tn * 2 + 2 * tm * tn * 4, deps=deps, side=side)


def _ffn_bwd_dh(name, dgu, wgu, deps=(), side=None):
    _, t, f = dgu.shape
    d, nb = wgu.shape[1], wgu.shape[2]
    tm = min(t, 512)

    def products(ins):
        return (lax.dot_general(ins[0][:, 0:nb], ins[1][0], NT, preferred_element_type=F32)
                + lax.dot_general(ins[0][:, nb:2 * nb], ins[1][1], NT, preferred_element_type=F32))

    def epilogue(acc, ins, outs):
        outs[0][...] = acc

    return _fused(name, (t // tm, 1, 4),
                  [(dgu, pl.BlockSpec((None, tm, 2 * nb), lambda i, j, k: (k // 2, i, k % 2))),
                   (wgu, pl.BlockSpec((2, d, nb), lambda i, j, k: (k, 0, 0)))],
                  [(_sds((t, d), F32), pl.BlockSpec((tm, d), lambda i, j, k: (i, 0)))],
                  products, epilogue, nk=4, acc_shape=(tm, d), temp_bytes=tm * d * 4, deps=deps, side=side)


def _ffn_bwd_dwgu(name, h, dgu, deps=(), side=None, rows=None):
    t, d = h.shape
    nb = dgu.shape[2] // 4
    tm = min(d, 512)
    row0, nrows = rows if rows is not None else (0, d)
    j0 = row0 // tm

    def epilogue(acc, ins, outs):
        outs[0][...] = acc.astype(BF16)

    return _fused(name, (N_DEV, nrows // tm, 1),
                  [(h, pl.BlockSpec((t, tm), lambda i, j, k: (0, j0 + j))),
                   (dgu, pl.BlockSpec((None, t, nb), lambda i, j, k: (i // 4, 0, i % 4)))],
                  [(_sds((N_DEV, nrows, nb), BF16), pl.BlockSpec((None, tm, nb), lambda i, j, k: (i, j, 0)))],
                  [(0, 1, TN)], epilogue, temp_bytes=2 * tm * nb * 4, deps=deps, side=side)


def _ffn_bwd_dwgu_pair_rows(name, h, h_sib, dgu, dgu_sib, place, rows, deps=(), side=None):
    t, d = h.shape
    nb = dgu.shape[2] // 4
    tm = min(d, 256)
    row0, nrows = rows
    j0 = row0 // tm
    n_row = nrows // tm

    def products(ins):
        return (lax.dot_general(ins[0][...], ins[2][...], TN, preferred_element_type=F32)
                + lax.dot_general(ins[1][...], ins[3][...], TN, preferred_element_type=F32))

    def epilogue(acc, ins, outs, place_ref):
        total = acc.astype(BF16)
        outs[0][...] = total

        @pl.when(pl.program_id(0) == place_ref[1])
        def _():
            outs[1][...] = total

    def act_map(i, j, k, place_ref):
        return (0, j0 + j)

    def grad_map(i, j, k, place_ref):
        dev = 2 * i + place_ref[0]
        return (dev // 4, 0, dev % 4)

    def slot_map(i, j, k, place_ref):
        chip = place_ref[1]
        return (chip, jnp.where(i < chip, 0, jnp.where(i == chip, j, n_row - 1)), 0)

    return _fused(name, (4, n_row, 1),
                  [(h, pl.BlockSpec((t, tm), act_map)), (h_sib, pl.BlockSpec((t, tm), act_map)),
                   (dgu, pl.BlockSpec((None, t, nb), grad_map)),
                   (dgu_sib, pl.BlockSpec((None, t, nb), lambda i, j, k, place_ref: (i, 0, 0)))],
                  [(_sds((4, nrows, nb), BF16), pl.BlockSpec((None, tm, nb), lambda i, j, k, place_ref: (i, j, 0))),
                   (_sds((4, nrows, nb), BF16), pl.BlockSpec((None, tm, nb), slot_map))],
                  products, epilogue, temp_bytes=2 * tm * nb * 4,
                  semantics=("arbitrary", "arbitrary", "arbitrary"), deps=deps, side=side, prefetch=place)


def _ffn_bwd_dwgu_pair(name, h, h_sib, dgu, dgu_sib, place):
    t, d = h.shape
    nb = dgu.shape[2] // 4
    tm = min(d, 512)

    def body(place_ref, h_ref, hs_ref, g_ref, gs_ref, sums_ref, slots_ref):
        acc = lax.dot_general(h_ref[...], g_ref[...], TN, preferred_element_type=F32)
        acc += lax.dot_general(hs_ref[...], gs_ref[...], TN, preferred_element_type=F32)
        total = acc.astype(BF16)
        sums_ref[...] = total

        @pl.when(pl.program_id(1) == place_ref[1])
        def _():
            slots_ref[...] = total

    def act_map(i, k, place_ref):
        return (0, i)

    def grad_map(i, k, place_ref):
        dev = 2 * k + place_ref[0]
        return (dev // 4, 0, dev % 4)

    grid_spec = pltpu.PrefetchScalarGridSpec(
        num_scalar_prefetch=1, grid=(d // tm, 4),
        in_specs=[pl.BlockSpec((t, tm), act_map), pl.BlockSpec((t, tm), act_map),
                  pl.BlockSpec((None, t, nb), grad_map),
                  pl.BlockSpec((None, t, nb), lambda i, k, place_ref: (k, 0, 0))],
        out_specs=[pl.BlockSpec((None, tm, nb), lambda i, k, place_ref: (k, i, 0)),
                   pl.BlockSpec((None, tm, nb), lambda i, k, place_ref: (place_ref[1], i, 0))])
    blocks = 2 * t * tm * 2 + 2 * t * nb * 2 + 2 * tm * nb * 2
    return pl.pallas_call(
        body, name=name, grid_spec=grid_spec, out_shape=[_sds((4, d, nb), BF16)] * 2,
        compiler_params=_params(("parallel", "arbitrary"), blocks, 2 * tm * nb * 4),
    )(place, h, h_sib, dgu, dgu_sib)


def _proj(h, w_in):
    t, d = h.shape
    nb = w_in.shape[3]
    tm = min(t, 512)

    def body(h_ref, w_ref, o_ref):
        hv = h_ref[...]
        o_ref[:, 0:nb] = jnp.dot(hv, w_ref[0], preferred_element_type=F32).astype(BF16)
        o_ref[:, nb:2 * nb] = jnp.dot(hv, w_ref[1], preferred_element_type=F32).astype(BF16)

    blocks = tm * d * 2 + 2 * d * nb * 2 + tm * 2 * nb * 4
    return pl.pallas_call(
        body, name="mix_proj", grid=(4, t // tm),
        in_specs=[pl.BlockSpec((tm, d), lambda j, i: (i, 0)),
                  pl.BlockSpec((None, 2, d, nb), lambda j, i: (j, 0, 0, 0))],
        out_specs=pl.BlockSpec((tm, 2 * nb), lambda j, i: (i, j)),
        out_shape=_sds((t, N_DEV * nb), BF16),
        compiler_params=_params(("parallel", "parallel"), blocks, 2 * tm * nb * 4),
    )(h, w_in)


def _shift_rows(u, k):
    t = u.shape[0]
    rolled = pltpu.roll(u, k % t, axis=0)
    row = lax.broadcasted_iota(jnp.int32, u.shape, 0)
    keep = (row >= k) if k > 0 else (row < t + k)
    return jnp.where(keep, rolled, 0.0)


def _conv_fwd(proj, conv_w):
    t = proj.shape[0]
    cw = conv_w.shape[1]
    tc = min(cw, 256)
    nc = cw // tc

    def epilogue(_, ins, outs):
        u = ins[2][...].astype(F32) * ins[0][...].astype(F32)
        w = ins[3][...]
        y = u * w[2:3, :] + _shift_rows(u, 1) * w[1:2, :] + _shift_rows(u, 2) * w[0:1, :]
        outs[0][...] = (ins[1][...].astype(F32) * y).astype(BF16)

    def col(seg):
        return pl.BlockSpec((t, tc), lambda i, j, k: (0, seg * nc + i))

    return _fused("conv_fwd", (nc, 1, 1),
                  [(proj, col(0)), (proj, col(1)), (proj, col(2)),
                   (conv_w, pl.BlockSpec((8, tc), lambda i, j, k: (0, i)))],
                  [(_sds((t, cw), BF16), pl.BlockSpec((t, tc), lambda i, j, k: (0, i)))],
                  [], epilogue, temp_bytes=6 * t * tc * 4)[0]


def _conv_bwd(proj, conv_w, dca, deps=()):
    t = proj.shape[0]
    cw = conv_w.shape[1]
    tc = min(cw, 256)
    nc = cw // tc

    def epilogue(_, ins, outs):
        xc, bg, cg = ins[0][...].astype(F32), ins[1][...].astype(F32), ins[2][...].astype(F32)
        w, dc = ins[3][...], ins[4][...]
        u = cg * xc
        u1, u2 = _shift_rows(u, 1), _shift_rows(u, 2)
        y = u * w[2:3, :] + u1 * w[1:2, :] + u2 * w[0:1, :]
        dconv = dc * bg
        du = dconv * w[2:3, :] + _shift_rows(dconv, -1) * w[1:2, :] + _shift_rows(dconv, -2) * w[0:1, :]
        outs[0][0] = (du * cg).astype(BF16)
        outs[0][1] = (dc * y).astype(BF16)
        outs[0][2] = (du * xc).astype(BF16)
        outs[1][...] = jnp.zeros_like(outs[1])
        outs[1][0:1, :] = jnp.sum(dconv * u2, axis=0, keepdims=True)
        outs[1][1:2, :] = jnp.sum(dconv * u1, axis=0, keepdims=True)
        outs[1][2:3, :] = jnp.sum(dconv * u, axis=0, keepdims=True)

    def col(seg):
        return pl.BlockSpec((t, tc), lambda i, j, k: (0, seg * nc + i))

    own = pl.BlockSpec((t, tc), lambda i, j, k: (0, i))
    wspec = pl.BlockSpec((8, tc), lambda i, j, k: (0, i))
    return _fused("conv_bwd", (nc, 1, 1),
                  [(proj, col(0)), (proj, col(1)), (proj, col(2)), (conv_w, wspec), (dca, own)],
                  [(_sds((3, t, cw), BF16), pl.BlockSpec((3, t, tc), lambda i, j, k: (0, 0, i))),
                   (_sds((8, cw), F32), wspec)],
                  [], epilogue, temp_bytes=10 * t * tc * 4, deps=deps)


def _split3(x):
    hi = x.astype(BF16)
    r1 = x - hi.astype(F32)
    mid = r1.astype(BF16)
    lo = (r1 - mid.astype(F32)).astype(BF16)
    return hi, mid, lo


def _head_selector(width):
    r = lax.broadcasted_iota(jnp.int32, (width, LANES), 0)
    c = lax.broadcasted_iota(jnp.int32, (width, LANES), 1)
    return (lax.shift_right_logical(r, 6) == c).astype(BF16)


def _head_sum(x, sel):
    return sum(jnp.dot(p, sel, preferred_element_type=F32) for p in _split3(x))


def _head_bcast(r, sel):
    return sum(lax.dot_general(p, sel, NT, preferred_element_type=F32) for p in _split3(r))


def _rope(x, c, sa, sb):
    n = x.shape[1]
    return x * c + pltpu.roll(x, n - ROT_DIM // 2, axis=1) * sa + pltpu.roll(x, ROT_DIM // 2, axis=1) * sb


def _rope_t(d, c, sa, sb):
    n = d.shape[1]
    return d * c + pltpu.roll(d * sa, ROT_DIM // 2, axis=1) + pltpu.roll(d * sb, n - ROT_DIM // 2, axis=1)


def _tile_lanes(tab, width):
    return tab if width == tab.shape[1] else jnp.tile(tab, (1, width // tab.shape[1]))


def _qk_prep(proj, gq, gk, rope_tabs, cw, kw):
    t = proj.shape[0]
    tm = _row_tile(t)

    def epilogue(_, ins, outs):
        c, sa, sb = ins[5][...], ins[6][...], ins[7][...]
        for src, gain, dst, width in ((0, 3, 0, cw), (1, 4, 1, kw)):
            xv = ins[src][...].astype(F32)
            sel = _head_selector(width)
            r = lax.rsqrt(_head_sum(xv * xv, sel) * (1.0 / HEAD_DIM) + RMS_EPS)
            xn = xv * _head_bcast(r, sel) * ins[gain][...]
            outs[dst][...] = _rope(xn, _tile_lanes(c, width), _tile_lanes(sa, width), _tile_lanes(sb, width)).astype(BF16)
        outs[2][...] = ins[2][...].astype(BF16)

    kblk = cw // kw
    tab = pl.BlockSpec((tm, LANES), lambda i, j, k: (i, 0))
    kspec = pl.BlockSpec((tm, kw), lambda i, j, k: (i, 0))
    return _fused("qk_prep", (t // tm, 1, 1),
                  [(proj, pl.BlockSpec((tm, cw), lambda i, j, k: (i, 3))),
                   (proj, pl.BlockSpec((tm, kw), lambda i, j, k: (i, 4 * kblk))),
                   (proj, pl.BlockSpec((tm, kw), lambda i, j, k: (i, 4 * kblk + 1))),
                   (gq, pl.BlockSpec((1, cw), lambda i, j, k: (0, 0))),
                   (gk, pl.BlockSpec((1, kw), lambda i, j, k: (0, 0))),
                   (rope_tabs[0], tab), (rope_tabs[1], tab), (rope_tabs[2], tab)],
                  [(_sds((t, cw), BF16), pl.BlockSpec((tm, cw), lambda i, j, k: (i, 0))),
                   (_sds((t, kw), BF16), kspec), (_sds((t, kw), BF16), kspec)],
                  [], epilogue, temp_bytes=12 * tm * cw * 4)


def _qk_prep_bwd(proj, gq, gk, rope_tabs, dq, dkc, dkp, dvc, dvp, cw, kw):
    t = proj.shape[0]
    tm = BLOCK
    nblk = t // tm

    def epilogue(_, ins, outs):
        c, sa, sb = ins[5][...], ins[6][...], ins[7][...]
        has_next = (pl.program_id(0) < nblk - 1).astype(F32)
        dk = ins[9][...] + has_next * ins[10][...]
        dv = ins[11][...] + has_next * ins[12][...]
        pieces = []
        for src, gain, dval, dst, width in ((0, 3, ins[8][...], 1, cw), (1, 4, dk, 2, kw)):
            xv, gv = ins[src][...].astype(F32), ins[gain][...]
            sel = _head_selector(width)
            r = _head_bcast(lax.rsqrt(_head_sum(xv * xv, sel) * (1.0 / HEAD_DIM) + RMS_EPS), sel)
            xh = xv * r
            dxn = _rope_t(dval, _tile_lanes(c, width), _tile_lanes(sa, width), _tile_lanes(sb, width))
            u = dxn * gv
            dot = _head_bcast(_head_sum(u * xh, sel), sel) * (1.0 / HEAD_DIM)
            pieces.append((r * (u - xh * dot)).astype(BF16))
            ri = lax.broadcasted_iota(jnp.int32, (width, LANES), 0)
            ci = lax.broadcasted_iota(jnp.int32, (width, LANES), 1)
            fold = (lax.bitwise_and(ri, HEAD_DIM - 1) == ci).astype(BF16)
            colsum = jnp.broadcast_to(jnp.sum(dxn * xh, axis=0, keepdims=True), (8, width))
            part = sum(jnp.dot(p, fold, preferred_element_type=F32) for p in _split3(colsum))

            @pl.when(pl.program_id(0) == 0)
            def _():
                outs[dst][...] = jnp.zeros_like(outs[dst])

            outs[dst][0:1, :] += part[0:1, :]
        outs[0][:, 0:cw] = pieces[0]
        outs[0][:, cw:cw + kw] = pieces[1]
        outs[0][:, cw + kw:cw + 2 * kw] = dv.astype(BF16)

    kblk = cw // kw
    tab = pl.BlockSpec((tm, LANES), lambda i, j, k: (i, 0))
    kcur = pl.BlockSpec((tm, kw), lambda i, j, k: (i, 0))
    knext = pl.BlockSpec((tm, kw), lambda i, j, k: (jnp.minimum(i + 1, nblk - 1), 0))
    acc = pl.BlockSpec((8, LANES), lambda i, j, k: (0, 0))
    return _fused("qk_prep_bwd", (nblk, 1, 1),
                  [(proj, pl.BlockSpec((tm, cw), lambda i, j, k: (i, 3))),
                   (proj, pl.BlockSpec((tm, kw), lambda i, j, k: (i, 4 * kblk))),
                   (proj, pl.BlockSpec((tm, kw), lambda i, j, k: (i, 4 * kblk + 1))),
                   (gq, pl.BlockSpec((1, cw), lambda i, j, k: (0, 0))),
                   (gk, pl.BlockSpec((1, kw), lambda i, j, k: (0, 0))),
                   (rope_tabs[0], tab), (rope_tabs[1], tab), (rope_tabs[2], tab),
                   (dq, pl.BlockSpec((tm, cw), lambda i, j, k: (i, 0))),
                   (dkc, kcur), (dkp, knext), (dvc, kcur), (dvp, knext)],
                  [(_sds((t, cw + 2 * kw), BF16), pl.BlockSpec((tm, cw + 2 * kw), lambda i, j, k: (i, 0))),
                   (_sds((8, LANES), F32), acc), (_sds((8, LANES), F32), acc)],
                  [], epilogue, temp_bytes=16 * tm * cw * 4, semantics=("arbitrary", "arbitrary", "arbitrary"))


def _attn_mask(n):
    key = lax.broadcasted_iota(jnp.int32, (2 * BLOCK, GROUP * BLOCK), 0)
    qry = lax.bitwise_and(lax.broadcasted_iota(jnp.int32, (2 * BLOCK, GROUP * BLOCK), 1), BLOCK - 1)
    return (key > qry) & (key <= qry + BLOCK) & ((key >= BLOCK) | (n > 0))


def _stack_heads(x, h):
    return jnp.concatenate([x[:, (h * GROUP + g) * HEAD_DIM:(h * GROUP + g + 1) * HEAD_DIM] for g in range(GROUP)], axis=0)


def _softmax_with_sink(q4, k2, sink_ref, h, valid):
    sink = jnp.concatenate([sink_ref[h * GROUP + g:h * GROUP + g + 1, :] for g in range(GROUP)], axis=1)
    s = lax.dot_general(k2, q4, NT, preferred_element_type=F32) * ATTN_SCALE
    s = jnp.where(valid, s, NEG_INF)
    m = jnp.maximum(jnp.max(s, axis=0, keepdims=True), sink)
    p = jnp.exp(s - m)
    es = jnp.exp(sink - m)
    inv = 1.0 / (jnp.sum(p, axis=0, keepdims=True) + es)
    return p * inv, es * inv


def _attn_fwd(qn, kn, vb, sink_rows):
    t, cw = qn.shape
    kw = kn.shape[1]
    nkv = kw // HEAD_DIM

    def body(q_ref, kp_ref, kc_ref, vp_ref, vc_ref, sink_ref, o_ref):
        valid = _attn_mask(pl.program_id(0))
        qv = q_ref[...]
        kp, kc, vp, vc = kp_ref[...], kc_ref[...], vp_ref[...], vc_ref[...]
        outs = []
        for h in range(nkv):
            hs = slice(h * HEAD_DIM, (h + 1) * HEAD_DIM)
            k2 = jnp.concatenate([kp[:, hs], kc[:, hs]], axis=0)
            v2 = jnp.concatenate([vp[:, hs], vc[:, hs]], axis=0)
            pn, _ = _softmax_with_sink(_stack_heads(qv, h), k2, sink_ref, h, valid)
            o4 = lax.dot_general(pn.astype(BF16), v2, TN, preferred_element_type=F32)
            outs += [o4[g * BLOCK:(g + 1) * BLOCK] for g in range(GROUP)]
        o_ref[...] = jnp.concatenate(outs, axis=-1).astype(BF16)

    cur = lambda n: (n, 0)
    prev = lambda n: (jnp.maximum(n - 1, 0), 0)
    return pl.pallas_call(
        body, name="attn_fwd", grid=(t // BLOCK,),
        in_specs=[pl.BlockSpec((BLOCK, cw), cur),
                  pl.BlockSpec((BLOCK, kw), prev), pl.BlockSpec((BLOCK, kw), cur),
                  pl.BlockSpec((BLOCK, kw), prev), pl.BlockSpec((BLOCK, kw), cur),
                  pl.BlockSpec(sink_rows.shape, lambda n: (0, 0))],
        out_specs=pl.BlockSpec((BLOCK, cw), cur),
        out_shape=_sds((t, cw), BF16),
        compiler_params=_params(("parallel",), BLOCK * (cw + 4 * kw) * 2 + BLOCK * cw * 2, 8 << 20),
    )(qn, kn, kn, vb, vb, sink_rows)


def _attn_bwd(qn, kn, vb, sink_rows, do):
    t, cw = qn.shape
    kw = kn.shape[1]
    nkv = kw // HEAD_DIM
    nq = nkv * GROUP

    def body(q_ref, kp_ref, kc_ref, vp_ref, vc_ref, sink_ref, do_ref,
             dq_ref, dkc_ref, dkp_ref, dvc_ref, dvp_ref, dsink_ref):
        n = pl.program_id(0)
        valid = _attn_mask(n)
        qv, dov = q_ref[...], do_ref[...]
        kp, kc, vp, vc = kp_ref[...], kc_ref[...], vp_ref[...], vc_ref[...]
        dqs, dks, dvs, dsinks = [], [], [], []
        for h in range(nkv):
            hs = slice(h * HEAD_DIM, (h + 1) * HEAD_DIM)
            k2 = jnp.concatenate([kp[:, hs], kc[:, hs]], axis=0)
            v2 = jnp.concatenate([vp[:, hs], vc[:, hs]], axis=0)
            q4 = _stack_heads(qv, h)
            dob = _stack_heads(dov, h).astype(BF16)
            pn, psink = _softmax_with_sink(q4, k2, sink_ref, h, valid)
            dpn = lax.dot_general(v2, dob, NT, preferred_element_type=F32)
            dvs.append(jnp.dot(pn.astype(BF16), dob, preferred_element_type=F32))
            delta = jnp.sum(pn * dpn, axis=0, keepdims=True)
            ds = (pn * (dpn - delta) * ATTN_SCALE).astype(BF16)
            dks.append(jnp.dot(ds, q4, preferred_element_type=F32))
            dq4 = lax.dot_general(ds, k2, TN, preferred_element_type=F32)
            dsink4 = -psink * delta
            for g in range(GROUP):
                dqs.append(dq4[g * BLOCK:(g + 1) * BLOCK])
                dsinks.append(jnp.broadcast_to(jnp.sum(dsink4[:, g * BLOCK:(g + 1) * BLOCK], axis=1, keepdims=True), (1, LANES)))
        dq_ref[...] = jnp.concatenate(dqs, axis=-1)
        dkp_ref[...] = jnp.concatenate([d[:BLOCK] for d in dks], axis=-1)
        dkc_ref[...] = jnp.concatenate([d[BLOCK:] for d in dks], axis=-1)
        dvp_ref[...] = jnp.concatenate([d[:BLOCK] for d in dvs], axis=-1)
        dvc_ref[...] = jnp.concatenate([d[BLOCK:] for d in dvs], axis=-1)

        @pl.when(n == 0)
        def _():
            dsink_ref[...] = jnp.zeros_like(dsink_ref)

        dsink_ref[...] += jnp.concatenate(dsinks, axis=0)

    cur = lambda n: (n, 0)
    prev = lambda n: (jnp.maximum(n - 1, 0), 0)
    kspec = pl.BlockSpec((BLOCK, kw), cur)
    return pl.pallas_call(
        body, name="attn_bwd", grid=(t // BLOCK,),
        in_specs=[pl.BlockSpec((BLOCK, cw), cur),
                  pl.BlockSpec((BLOCK, kw), prev), kspec,
                  pl.BlockSpec((BLOCK, kw), prev), kspec,
                  pl.BlockSpec(sink_rows.shape, lambda n: (0, 0)),
                  pl.BlockSpec((BLOCK, cw), cur)],
        out_specs=[pl.BlockSpec((BLOCK, cw), cur), kspec, kspec, kspec, kspec,
                   pl.BlockSpec((nq, LANES), lambda n: (0, 0))],
        out_shape=[_sds((t, cw), F32)] + [_sds((t, kw), F32)] * 4 + [_sds((nq, LANES), F32)],
        compiler_params=_params(("arbitrary",), BLOCK * (cw + 4 * kw) * 2 + 2 * BLOCK * cw * 4 + 4 * BLOCK * kw * 4, 12 << 20),
    )(qn, kn, kn, vb, vb, sink_rows, do)


def _mix_out(ca, o, woc, woa, proj):
    t, cw = ca.shape
    nb = woc.shape[2]
    d = N_DEV * nb
    tm = min(t, 1024)
    ga0 = (3 * cw + cw + 2 * (cw // 4)) // nb

    def body(ca_ref, o_ref, woc_ref, woa_ref, ga_ref, gb_ref, m_ref, ya_ref, yb_ref):
        ya = jnp.dot(ca_ref[...], woc_ref[...], preferred_element_type=F32)
        yb = jnp.dot(o_ref[...], woa_ref[...], preferred_element_type=F32)
        ya_ref[...] = ya.astype(BF16)
        yb_ref[...] = yb.astype(BF16)
        m_ref[...] = (_sigmoid(ga_ref[...].astype(F32)) * ya + _sigmoid(gb_ref[...].astype(F32)) * yb).astype(BF16)

    act = pl.BlockSpec((tm, cw), lambda i, j: (i, 0))
    wsp = pl.BlockSpec((None, cw, nb), lambda i, j: (j, 0, 0))
    osp = pl.BlockSpec((tm, nb), lambda i, j: (i, j))
    blocks = 2 * tm * cw * 2 + 2 * cw * nb * 2 + 2 * tm * nb * 4 + 3 * tm * nb * 2
    return pl.pallas_call(
        body, name="mix_out", grid=(t // tm, N_DEV),
        in_specs=[act, act, wsp, wsp,
                  pl.BlockSpec((tm, nb), lambda i, j: (i, ga0 + j)),
                  pl.BlockSpec((tm, nb), lambda i, j: (i, ga0 + N_DEV + j))],
        out_specs=[osp, osp, osp],
        out_shape=[_sds((t, d), BF16)] * 3,
        compiler_params=_params(("parallel", "parallel"), blocks, 6 * tm * nb * 4),
    )(ca, o, woc, woa, proj, proj)


def _mix_residual(merged, wo, x):
    t, d = x.shape
    tm = min(t, 512)

    def epilogue(acc, ins, outs):
        outs[0][...] = ins[2][...] + acc

    row = pl.BlockSpec((tm, d), lambda i, j, k: (i, 0))
    return _fused("mix_residual", (t // tm, 1, 1),
                  [(merged, row), (wo, pl.BlockSpec((d, d), lambda i, j, k: (0, 0))), (x, row)],
                  [(_sds((t, d), F32), row)], [(0, 1, NN)], epilogue, temp_bytes=2 * tm * d * 4)[0]


def _mix_bwd_gates(dx, wo, ya, yb, proj, cw):
    t, d = dx.shape
    tm = min(t, 1024)
    tn = min(d, 512)
    ga0 = (4 * cw + 2 * (cw // 4)) // tn

    def epilogue(acc, ins, outs):
        sa, sb = _sigmoid(ins[4][...].astype(F32)), _sigmoid(ins[5][...].astype(F32))
        outs[0][...] = (acc * sa).astype(BF16)
        outs[1][...] = (acc * sb).astype(BF16)
        outs[2][0] = (acc * ins[2][...].astype(F32) * sa * (1.0 - sa)).astype(BF16)
        outs[2][1] = (acc * ins[3][...].astype(F32) * sb * (1.0 - sb)).astype(BF16)

    blk = pl.BlockSpec((tm, tn), lambda i, j, k: (i, j))
    return _fused("mix_bwd_gates", (t // tm, d // tn, 1),
                  [(dx, pl.BlockSpec((tm, d), lambda i, j, k: (i, 0))),
                   (wo, pl.BlockSpec((tn, d), lambda i, j, k: (j, 0))),
                   (ya, blk), (yb, blk),
                   (proj, pl.BlockSpec((tm, tn), lambda i, j, k: (i, ga0 + j))),
                   (proj, pl.BlockSpec((tm, tn), lambda i, j, k: (i, ga0 + d // tn + j)))],
                  [(_sds((t, d), BF16), blk), (_sds((t, d), BF16), blk),
                   (_sds((2, t, d), BF16), pl.BlockSpec((2, tm, tn), lambda i, j, k: (0, i, j)))],
                  [(0, 1, NT)], epilogue, temp_bytes=8 * tm * tn * 4)


def _tn_matmul(name, a, b, tm, out_dtype=BF16):
    t, m = a.shape
    n = b.shape[1]

    def epilogue(acc, ins, outs):
        outs[0][...] = acc.astype(out_dtype)

    return _fused(name, (m // tm, 1, 1),
                  [(a, pl.BlockSpec((t, tm), lambda i, j, k: (0, i))),
                   (b, pl.BlockSpec((t, n), lambda i, j, k: (0, 0)))],
                  [(_sds((m, n), out_dtype), pl.BlockSpec((tm, n), lambda i, j, k: (i, 0)))],
                  [(0, 1, TN)], epilogue, temp_bytes=2 * tm * n * 4)[0]


def _out_proj_bwd_act(dya, dyb, woc, woa, deps=()):
    t, d = dya.shape
    kdim, nb = woc.shape[1], woc.shape[2]
    tm = min(t, 512)

    def body(dya_ref, dyb_ref, woc_ref, woa_ref, *rest):
        for dy_ref, w_ref, o_ref in ((dya_ref, woc_ref, rest[-2]), (dyb_ref, woa_ref, rest[-1])):
            total = None
            for j in range(N_DEV):
                part = lax.dot_general(dy_ref[:, j * nb:(j + 1) * nb], w_ref[j], NT, preferred_element_type=F32)
                total = part if total is None else total + part
            o_ref[...] = total

    row = pl.BlockSpec((tm, d), lambda i: (i, 0))
    wsp = pl.BlockSpec((N_DEV, kdim, nb), lambda i: (0, 0, 0))
    osp = pl.BlockSpec((tm, kdim), lambda i: (i, 0))
    blocks = 2 * tm * d * 2 + 2 * N_DEV * kdim * nb * 2 + 2 * tm * kdim * 4
    return pl.pallas_call(
        body, name="mix_bwd_dca_do", grid=(t // tm,),
        in_specs=[row, row, wsp, wsp] + [_ANY] * len(deps), out_specs=[osp, osp],
        out_shape=[_sds((t, kdim), F32)] * 2,
        compiler_params=_params(("parallel",), blocks, 4 * tm * kdim * 4),
    )(dya, dyb, woc, woa, *deps)


def _out_proj_bwd_w(ca, o, dya, dyb, nb):
    t, kdim = ca.shape

    def body(ca_ref, o_ref, dya_ref, dyb_ref, dwoc_ref, dwoa_ref):
        dwoc_ref[...] = lax.dot_general(ca_ref[...], dya_ref[...], TN, preferred_element_type=F32).astype(BF16)
        dwoa_ref[...] = lax.dot_general(o_ref[...], dyb_ref[...], TN, preferred_element_type=F32).astype(BF16)

    act = pl.BlockSpec((t, kdim), lambda j: (0, 0))
    col = pl.BlockSpec((t, nb), lambda j: (0, j))
    osp = pl.BlockSpec((None, kdim, nb), lambda j: (j, 0, 0))
    blocks = 2 * t * kdim * 2 + 2 * t * nb * 2 + 2 * kdim * nb * 2
    return pl.pallas_call(
        body, name="mix_bwd_dwoc_dwoa", grid=(N_DEV,),
        in_specs=[act, act, col, col], out_specs=[osp, osp],
        out_shape=[_sds((N_DEV, kdim, nb), BF16)] * 2,
        compiler_params=_params(("parallel",), blocks, 4 * kdim * nb * 4),
    )(ca, o, dya, dyb)


def _proj_bwd_act(dproj, w_in, deps=()):
    t, n = dproj.shape
    d, nb = w_in.shape[2], w_in.shape[3]
    tm = min(t, 512)

    def epilogue(acc, ins, outs):
        outs[0][...] = acc

    def products(ins):
        return (lax.dot_general(ins[0][:, 0:nb], ins[1][0], NT, preferred_element_type=F32)
                + lax.dot_general(ins[0][:, nb:2 * nb], ins[1][1], NT, preferred_element_type=F32))

    return _fused("mix_bwd_dh", (t // tm, 1, 4),
                  [(dproj, pl.BlockSpec((tm, 2 * nb), lambda i, j, k: (i, k))),
                   (w_in, pl.BlockSpec((None, 2, d, nb), lambda i, j, k: (k, 0, 0, 0)))],
                  [(_sds((t, d), F32), pl.BlockSpec((tm, d), lambda i, j, k: (i, 0)))],
                  products, epilogue, nk=4, acc_shape=(tm, d), temp_bytes=tm * d * 4, deps=deps)[0]


def _proj_bwd_w(h, dproj):
    t, d = h.shape
    nb = dproj.shape[1] // N_DEV
    tm = min(d, 512)

    def body(h_ref, dp_ref, o_ref):
        hv = h_ref[...]
        o_ref[0] = lax.dot_general(hv, dp_ref[:, 0:nb], TN, preferred_element_type=F32).astype(BF16)
        o_ref[1] = lax.dot_general(hv, dp_ref[:, nb:2 * nb], TN, preferred_element_type=F32).astype(BF16)

    blocks = t * tm * 2 + t * 2 * nb * 2 + 2 * tm * nb * 2
    return pl.pallas_call(
        body, name="mix_bwd_dwin", grid=(4, d // tm),
        in_specs=[pl.BlockSpec((t, tm), lambda j, i: (0, i)),
                  pl.BlockSpec((t, 2 * nb), lambda j, i: (0, j))],
        out_specs=pl.BlockSpec((None, 2, tm, nb), lambda j, i: (j, 0, i, 0)),
        out_shape=_sds((4, 2, d, nb), BF16),
        compiler_params=_params(("parallel", "parallel"), blocks, 4 * tm * nb * 4),
    )(h, dproj)


def _adamw_math(w, g, m, v):
    m = ADAM_B1 * m + (1.0 - ADAM_B1) * g
    v = ADAM_B2 * v + (1.0 - ADAM_B2) * (g * g)
    m_hat = m / (1.0 - ADAM_B1 ** ADAM_STEP)
    v_hat = v / (1.0 - ADAM_B2 ** ADAM_STEP)
    delta = -ADAM_LR * (m_hat / (jnp.sqrt(v_hat) + ADAM_EPS) + ADAM_WD * w)
    return delta, m, v


def _adamw(name, parts, w, m, v, tr):
    r, c = w.shape

    def body(p_ref, w_ref, m_ref, v_ref, g_out, d_out, m_out, v_out):
        g = p_ref[0].astype(F32)
        for s in range(1, N_DEV):
            g = g + p_ref[s].astype(F32)
        delta, mn, vn = _adamw_math(w_ref[...], g, m_ref[...], v_ref[...])
        g_out[...] = g
        d_out[...] = delta
        m_out[...] = mn
        v_out[...] = vn

    blk = pl.BlockSpec((tr, c), lambda i: (i, 0))
    blocks = N_DEV * tr * c * parts.dtype.itemsize + 7 * tr * c * 4
    return pl.pallas_call(
        body, name=name, grid=(r // tr,),
        in_specs=[pl.BlockSpec((N_DEV, tr, c), lambda i: (0, i, 0)), blk, blk, blk],
        out_specs=[blk] * 4, out_shape=[_sds((r, c), F32)] * 4,
        compiler_params=_params(("parallel",), blocks, 6 * tr * c * 4),
    )(parts, w, m, v)


def _chip_sum(sums_ref):
    g = sums_ref[0].astype(F32)
    for k in range(1, 4):
        g = g + sums_ref[k].astype(F32)
    return g


def _adamw_chips(name, sums, w, m, v, tr, deps=(), row0=0, into=None):
    r, c = w.shape
    rs = sums.shape[1]
    i0 = row0 // tr
    n_pass = len(deps) + (4 if into is not None else 0)

    def body(sums_ref, w_ref, m_ref, v_ref, *rest):
        g_out, d_out, m_out, v_out = rest[n_pass:]
        g = _chip_sum(sums_ref)
        delta, mn, vn = _adamw_math(w_ref[...], g, m_ref[...], v_ref[...])
        g_out[...] = g
        d_out[...] = delta
        m_out[...] = mn
        v_out[...] = vn

    blk = pl.BlockSpec((tr, c), lambda i: (i0 + i, 0))
    blocks = 4 * tr * c * 2 + 7 * tr * c * 4
    passed = list(deps) + (list(into) if into is not None else [])
    aliases = {4 + len(deps) + q: q for q in range(4)} if into is not None else {}
    return pl.pallas_call(
        body, name=name, grid=(rs // tr,),
        in_specs=[pl.BlockSpec((4, tr, c), lambda i: (0, i, 0)), blk, blk, blk] + [_ANY] * n_pass,
        out_specs=[blk] * 4, out_shape=[_sds((r, c), F32)] * 4,
        input_output_aliases=aliases,
        compiler_params=_params(("parallel",), blocks, 6 * tr * c * 4),
    )(sums, w, m, v, *passed)


def _adamw_side(contrib, w, m, v, n_tiles, step_of):
    r, c = w.shape
    tr = r // n_tiles
    assert tr * n_tiles == r and tr % 16 == 0, (r, n_tiles)

    def tile(i, j, k):
        return jnp.minimum(step_of(i, j, k), n_tiles - 1)

    blk = pl.BlockSpec((tr, c), lambda i, j, k: (tile(i, j, k), 0))
    ins = [(contrib, pl.BlockSpec((4, tr, c), lambda i, j, k: (0, tile(i, j, k), 0))), (w, blk), (m, blk), (v, blk)]
    outs = [(_sds((r, c), F32), blk)] * 4

    def fn(in_refs, out_refs):
        @pl.when(step_of(pl.program_id(0), pl.program_id(1), pl.program_id(2)) < n_tiles)
        def _():
            g = _chip_sum(in_refs[0])
            delta, mn, vn = _adamw_math(in_refs[1][...], g, in_refs[2][...], in_refs[3][...])
            out_refs[0][...] = g
            out_refs[1][...] = delta
            out_refs[2][...] = mn
            out_refs[3][...] = vn

    return ins, outs, fn


def _rope_tables(t):
    half = ROT_DIM // 2
    inv_freq = 1.0 / (ROPE_THETA ** (jnp.arange(0, ROT_DIM, 2, dtype=F32) / ROT_DIM))
    ang = jnp.arange(t, dtype=F32)[:, None] * inv_freq[None, :]
    cos, sin = jnp.cos(ang), jnp.sin(ang)
    ones = jnp.ones((t, HEAD_DIM - ROT_DIM), F32)
    zeros = jnp.zeros((t, HEAD_DIM - half), F32)
    c = jnp.concatenate([cos, cos, ones], axis=1)
    sa = jnp.concatenate([-sin, zeros], axis=1)
    sb = jnp.concatenate([jnp.zeros((t, half), F32), sin, jnp.zeros((t, HEAD_DIM - ROT_DIM), F32)], axis=1)
    return tuple(jnp.tile(a, (1, LANES // HEAD_DIM)) for a in (c, sa, sb))


def _pad_rows(a, rows=8):
    return jnp.pad(a, ((0, rows - a.shape[0]), (0, 0)))


def kernel(x, g_ffn1, w_gu1, w_down1, g_mix, w_in, conv_w, q_norm_g, k_norm_g, sinks, w_out_conv, w_out_attn, w_o, g_ffn2, w_gu2, w_down2, loss_target, m_g_ffn1, m_w_gu1, m_w_down1, m_g_mix, m_w_in, m_conv_w, m_q_norm_g, m_k_norm_g, m_sinks, m_w_out_conv, m_w_out_attn, m_w_o, m_g_ffn2, m_w_gu2, m_w_down2, v_g_ffn1, v_w_gu1, v_w_down1, v_g_mix, v_w_in, v_conv_w, v_q_norm_g, v_k_norm_g, v_sinks, v_w_out_conv, v_w_out_attn, v_w_o, v_g_ffn2, v_w_gu2, v_w_down2):
    t, d = x.shape[1], x.shape[2]
    cw = d // 2
    kw = cw // GROUP
    nq = cw // HEAD_DIM
    xs, target = x.reshape(t, d), loss_target.reshape(t, d)
    me = 4 * lax.axis_index("x") + 2 * lax.axis_index("y") + lax.axis_index("c")

    big = {"w_gu1": w_gu1, "w_down1": w_down1, "w_in": w_in, "w_out_conv": w_out_conv,
           "w_out_attn": w_out_attn, "w_o": w_o, "w_gu2": w_gu2, "w_down2": w_down2}
    big_m = {"w_gu1": m_w_gu1, "w_down1": m_w_down1, "w_in": m_w_in, "w_out_conv": m_w_out_conv,
             "w_out_attn": m_w_out_attn, "w_o": m_w_o, "w_gu2": m_w_gu2, "w_down2": m_w_down2}
    big_v = {"w_gu1": v_w_gu1, "w_down1": v_w_down1, "w_in": v_w_in, "w_out_conv": v_w_out_conv,
             "w_out_attn": v_w_out_attn, "w_o": v_w_o, "w_gu2": v_w_gu2, "w_down2": v_w_down2}
    names = list(big)

    tiles = {"w_gu1": 256, "w_gu2": 256, "w_in": 256, "w_down1": 176, "w_down2": 176,
             "w_out_conv": 1024, "w_out_attn": 1024, "w_o": 128}

    def row_tile(n):
        r = big[n].shape[1]
        return tiles[n] if r % tiles[n] == 0 else r

    rs_shape = {n: big[n].shape[1:] for n in names}
    half = rs_shape["w_gu1"][0] // 2
    rs_shape["w_gu1_lo"] = rs_shape["w_gu1_hi"] = (half, rs_shape["w_gu1"][1])

    def add_tile(n):
        r, c = rs_shape[n]
        while r * c * 2 > (3 << 20) and r % 32 == 0:
            r //= 2
        return r

    me_arr = me.astype(jnp.int32).reshape(1)
    sources = [(n, big[n][0], BF16, row_tile(n)) for n in names] + [("conv_w", _pad_rows(conv_w[0]), F32, 8)]
    issue_order = [0, 1, 2, 8, 3, 4, 5, 6, 7]
    first = _place_shard("place_" + names[0], sources[0][1], BF16, me_arr, sources[0][3])
    started = [_gather_start("gather_start_first", [first])]
    early = {2: (big_m["w_in"][0], big_v["w_in"][0])}
    rest = [_place_shard("place_" + sources[i][0], sources[i][1], sources[i][2], me_arr, sources[i][3],
                         deps=(started[0][3],) + early.get(i, ())) for i in issue_order[1:]]
    started.append(_gather_start("gather_start_rest", rest))
    where = {0: (0, 0)}
    where.update({i: (1, p) for p, i in enumerate(issue_order[1:])})

    def fetch(tag, idxs, after, forward=True):
        call = where[idxs[0]][0]
        send, recv, stacks, _ = started[call]
        positions = [where[i][1] for i in idxs]
        got = _gather_wait("gather_wait_" + tag, positions, send, recv, [stacks[p] for p in positions], after)
        return _forward_to_sibling("gather_forward_" + tag, got) if forward else got

    rope_tabs = _rope_tables(t)
    gq = jnp.tile(q_norm_g, (1, nq))
    gk = jnp.tile(k_norm_g, (1, nq // GROUP))
    sink_rows = jnp.broadcast_to(sinks[0][:, None], (nq, LANES))

    wts = {}
    h1 = _rms_fwd("ffn1_norm", xs, g_ffn1)
    wts["w_gu1"], = fetch("gu1", [0], started[1][3])
    gu1, a1 = _ffn_up("ffn1_up", h1, wts["w_gu1"])
    wts["w_down1"], = fetch("down1", [1], a1)
    wd1 = wts["w_down1"].reshape(-1, d)
    x1 = _ffn_down("ffn1_down", a1, wd1, xs)
    h2 = _rms_fwd("mix_norm", x1, g_mix)
    wts["w_in"], conv_land = fetch("in", [2, 8], h2)
    w_in_full = wts["w_in"].reshape(4, 2, d, -1)
    conv_full = jnp.transpose(conv_land, (1, 0, 2)).reshape(8, cw)
    proj = _proj(h2, w_in_full)
    ca = _conv_fwd(proj, conv_full)
    qn, kn, vb = _qk_prep(proj, gq, gk, rope_tabs, cw, kw)
    o = _attn_fwd(qn, kn, vb, sink_rows)
    wts["w_out_conv"], wts["w_out_attn"] = fetch("out", [3, 4], o)
    merged, ya, yb = _mix_out(ca, o, wts["w_out_conv"], wts["w_out_attn"], proj)
    wts["w_o"], = fetch("o", [5], merged)
    wo = wts["w_o"].reshape(d, d)
    x2 = _mix_residual(merged, wo, x1)
    h3 = _rms_fwd("ffn2_norm", x2, g_ffn2)
    mine = lax.axis_index("c").astype(jnp.int32).reshape(1)
    got = fetch("gu2", [6], h3, forward=False)
    fsend, frecv, got = _forward_start("gather_forward_start_gu2", got)
    part = _ffn_up("ffn2_up_mine", h3, got[0], parity=mine)
    wts["w_gu2"], = _forward_wait("gather_forward_wait_gu2", fsend, frecv, got, part[1])
    gu2, a2 = _ffn_up("ffn2_up_sibling", h3, wts["w_gu2"], parity=1 - mine, into=part)
    wts["w_down2"], = fetch("down2", [7], a2)
    wd2 = wts["w_down2"].reshape(-1, d)
    dy, sq, dy_bf = _ffn_down("ffn2_down", a2, wd2, x2, target=target)
    loss = lax.psum(sq[0, 0] * (0.5 / d), ("x", "y", "c"))

    place = jnp.stack([lax.axis_index("c"), 2 * lax.axis_index("x") + lax.axis_index("y")]).astype(jnp.int32)
    def pair_start(tag, group, grads, deps=()):
        stacks = [grads[n].reshape((4, 2) + rs_shape[n]) for n in group]
        lands = [lax.empty((4,) + rs_shape[n], BF16) for n in group]
        return _pair_start("rs_pair_start_" + tag, stacks, lands, deps)

    def chip_start(tag, group, pending, after):
        send, recv, stacks, lands, _ = pending
        stacks, lands = _pair_wait("rs_pair_wait_" + tag, send, recv, stacks, lands, after)
        added = [_pair_add("rs_pair_add_" + n, st, ld, place, add_tile(n)) for n, st, ld in zip(group, stacks, lands)]
        return _chip_start("rs_chip_start_" + tag, [a[0] for a in added], [a[1] for a in added])

    group_a, group_b, group_c = ["w_down2", "w_gu2"], ["w_o", "w_out_conv", "w_out_attn"], ["w_in"]
    group_d, group_e, group_f = ["w_down1"], ["w_gu1_lo"], ["w_gu1_hi"]
    g = {}
    dgu2, a2 = _ffn_bwd_act("ffn2_bwd_act", dy_bf, wd2, gu2)
    pend_s = _sibling_start("rs_act_start_gu2", [dgu2, h3])
    g["w_down2"], = _ffn_bwd_dwd("ffn2_bwd_dwd", a2, dy_bf, deps=(pend_s[4],))
    pend_a = pair_start("a", ["w_down2"], g)
    dh3, = _ffn_bwd_dh("ffn2_bwd_dh", pend_s[2][0], wts["w_gu2"], deps=(pend_a[4],))
    (dgu2, h3), (dgu2_sib, h3_sib) = _sibling_wait("rs_act_wait_gu2", pend_s[0], pend_s[1], pend_s[2], pend_s[3], dh3)
    sums_gu2, slots_gu2 = _ffn_bwd_dwgu_pair("ffn2_bwd_dwgu", h3, h3_sib, dgu2, dgu2_sib, place)
    stacks_a, lands_a = _pair_wait("rs_pair_wait_a", pend_a[0], pend_a[1], pend_a[2], pend_a[3], sums_gu2)
    added_a = _pair_add("rs_pair_add_w_down2", stacks_a[0], lands_a[0], place, add_tile("w_down2"))
    ring_a = _chip_start("rs_chip_start_a", [added_a[0], sums_gu2], [added_a[1], slots_gu2])
    dx2, dg_ffn2, dx2_bf = _rms_bwd("ffn2_bwd_rms", x2, g_ffn2, dh3, dy, deps=(ring_a[4],), with_bf16=True)

    dya, dyb, dgates = _mix_bwd_gates(dx2_bf, wo, ya, yb, proj, cw)
    g["w_o"] = _tn_matmul("mix_bwd_dwo", merged, dx2_bf, min(d, 512))
    g["w_out_conv"], g["w_out_attn"] = _out_proj_bwd_w(ca, o, dya, dyb, d // N_DEV)
    pend_b = pair_start("b", group_b, g)
    dca, do = _out_proj_bwd_act(dya, dyb, wts["w_out_conv"], wts["w_out_attn"], deps=(pend_b[4],))
    ring_b = chip_start("b", group_b, pend_b, do)
    d3, dconv_w = _conv_bwd(proj, conv_full, dca, deps=(ring_b[4],))
    dq, dkc, dkp, dvc, dvp, dsink = _attn_bwd(qn, kn, vb, sink_rows, do)
    dqkv, dgq, dgk = _qk_prep_bwd(proj, gq, gk, rope_tabs, dq, dkc, dkp, dvc, dvp, cw, kw)
    dproj = jnp.concatenate([d3[0], d3[1], d3[2], dqkv, dgates[0], dgates[1]], axis=1)
    g["w_in"] = _proj_bwd_w(h2, dproj)
    pend_c = pair_start("c", group_c, g)
    dh2 = _proj_bwd_act(dproj, w_in_full, deps=(pend_c[4],))
    ring_c = chip_start("c", group_c, pend_c, dh2)
    dx1, dg_mix, dx1_bf = _rms_bwd("mix_bwd_rms", x1, g_mix, dh2, dx2, deps=(ring_c[4],), with_bf16=True)

    big_out = {}
    arrived = {}

    def wait_group(tag, group, ring, after):
        send, recv, parts, lands2, _ = ring
        parts, lands2 = _chip_wait("rs_chip_wait_" + tag, send, recv, parts, lands2, after)
        arrived.update(dict(zip(group, lands2)))

    def update(n, after):
        res = _adamw_chips("adamw_" + n, arrived[n], big[n][0], big_m[n][0], big_v[n][0], row_tile(n), deps=(after,))
        big_out[n] = [a[None] for a in res]
        return res[0]

    def update_beside(n, n_tiles, step_of):
        return _adamw_side(arrived[n], big[n][0], big_m[n][0], big_v[n][0], n_tiles, step_of)

    def keep(n, res):
        big_out[n] = [a[None] for a in res]

    dgu1, a1 = _ffn_bwd_act("ffn1_bwd_act", dx1_bf, wd1, gu1)
    pend_s = _sibling_start("rs_act_start_gu1", [dgu1, h1])
    wait_group("a", group_a, ring_a, pend_s[4])
    g["w_down1"], *res = _ffn_bwd_dwd("ffn1_bwd_dwd", a1, dx1_bf,
                                       side=update_beside("w_down2", 11, lambda i, j, k: i * 4 + j))
    keep("w_down2", res)
    pend_d = pair_start("d", group_d, g)
    (dgu1, h1), (dgu1_sib, h1_sib) = _sibling_wait("rs_act_wait_gu1", pend_s[0], pend_s[1], pend_s[2], pend_s[3],
                                                   pend_d[4])
    sums_lo, slots_lo, *res = _ffn_bwd_dwgu_pair_rows(
        "ffn1_bwd_dwgu_lo", h1, h1_sib, dgu1, dgu1_sib, place, (0, half),
        side=update_beside("w_gu2", 16, lambda i, j, k: i * 4 + j))
    keep("w_gu2", res)
    ring_d = chip_start("d", group_d, pend_d, sums_lo)
    ring_e = _chip_start("rs_chip_start_e", [sums_lo], [slots_lo], deps=(ring_d[4],))
    wait_group("c", group_c, ring_c, ring_e[4])
    sums_hi, slots_hi, *res = _ffn_bwd_dwgu_pair_rows(
        "ffn1_bwd_dwgu_hi", h1, h1_sib, dgu1, dgu1_sib, place, (half, half),
        side=update_beside("w_in", 16, lambda i, j, k: i * 4 + j))
    keep("w_in", res)
    ring_f = _chip_start("rs_chip_start_f", [sums_hi], [slots_hi])
    wait_group("b", group_b, ring_b, ring_f[4])
    after = ring_f[4]
    for n in group_b:
        after = update(n, after)
    wait_group("d", group_d, ring_d, after)
    dh1, *res = _ffn_bwd_dh("ffn1_bwd_dh", dgu1, wts["w_gu1"],
                             side=update_beside("w_down1", 11, lambda i, j, k: i * 4 + k))
    keep("w_down1", res)
    grad_x, dg_ffn1 = _rms_bwd("ffn1_bwd_rms", xs, g_ffn1, dh1, dx1)
    after = grad_x
    n = "w_gu1"
    wait_group("e", group_e, ring_e, after)
    res = _adamw_chips("adamw_w_gu1_lo", arrived["w_gu1_lo"], big[n][0], big_m[n][0], big_v[n][0], row_tile(n), deps=(after,))
    wait_group("f", group_f, ring_f, res[0])
    res = _adamw_chips("adamw_w_gu1_hi", arrived["w_gu1_hi"], big[n][0], big_m[n][0], big_v[n][0], row_tile(n),
                       row0=half, into=res)
    keep(n, res)
    after = res[0]

    small = {"g_ffn1": dg_ffn1[0:1], "g_mix": dg_mix[0:1], "g_ffn2": dg_ffn2[0:1],
             "q_norm_g": dgq[0:1, :HEAD_DIM], "k_norm_g": dgk[0:1, :HEAD_DIM], "sinks": dsink[:, 0][None],
             "conv_w": dconv_w[0:CONV_K].reshape(1, -1)}
    small_w = {"g_ffn1": g_ffn1, "g_mix": g_mix, "g_ffn2": g_ffn2, "q_norm_g": q_norm_g, "k_norm_g": k_norm_g,
               "sinks": sinks, "conv_w": None}
    small_m = {"g_ffn1": m_g_ffn1, "g_mix": m_g_mix, "g_ffn2": m_g_ffn2, "q_norm_g": m_q_norm_g,
               "k_norm_g": m_k_norm_g, "sinks": m_sinks, "conv_w": m_conv_w}
    small_v = {"g_ffn1": v_g_ffn1, "g_mix": v_g_mix, "g_ffn2": v_g_ffn2, "q_norm_g": v_q_norm_g,
               "k_norm_g": v_k_norm_g, "sinks": v_sinks, "conv_w": v_conv_w}
    snames = list(small)
    widths = [small[n].shape[1] for n in snames]
    total = sum(widths)
    rows = -(-total // LANES)
    rows = -(-rows // 8) * 8

    def pack(vals):
        flat = jnp.concatenate([v.reshape(1, -1) for v in vals], axis=1)
        return jnp.pad(flat, ((0, 0), (0, rows * LANES - total))).reshape(rows, LANES)

    csh = cw // N_DEV

    def place_conv(local, fill):
        full = jnp.full((CONV_K, cw), fill, F32)
        return lax.dynamic_update_slice(full, local, (0, me * csh)).reshape(1, -1)

    pw = pack([small_w[n] if n != "conv_w" else place_conv(conv_w[0], 0.0) for n in snames])
    pm = pack([small_m[n] if n != "conv_w" else place_conv(m_conv_w[0], 0.0) for n in snames])
    pv = pack([small_v[n] if n != "conv_w" else place_conv(v_conv_w[0], 1.0) for n in snames])
    parts = _all_gather_small("gather_small_grads", pack([small[n] for n in snames]), deps=(after,))
    sg, sd, sm, sv = [a.reshape(1, -1) for a in _adamw("adamw_small", parts, pw, pm, pv, rows)]

    def unpack(flat, n):
        off = sum(widths[:snames.index(n)])
        piece = flat[:, off:off + widths[snames.index(n)]]
        if n == "conv_w":
            piece = lax.dynamic_slice(piece.reshape(CONV_K, cw), (0, me * csh), (CONV_K, csh))[None]
        return piece

    order = ["g_ffn1", "w_gu1", "w_down1", "g_mix", "w_in", "conv_w", "q_norm_g", "k_norm_g", "sinks",
             "w_out_conv", "w_out_attn", "w_o", "g_ffn2", "w_gu2", "w_down2"]
    outs = [loss, grad_x[None]]
    for idx, flat in enumerate((sg, sd, sm, sv)):
        for n in order:
            outs.append(big_out[n][idx] if n in big_out else unpack(flat, n))
    return tuple(outs)
```

```python
import jax
import jax.numpy as jnp
from jax import lax
from jax.experimental import pallas as pl
from jax.experimental.pallas import tpu as pltpu

F32 = jnp.float32
BF16 = jnp.bfloat16

N_DEV = 8
HEAD_DIM = 64
GROUP = 4
BLOCK = 128
ROT_DIM = 16
ROPE_THETA = 500000.0
RMS_EPS = 1e-6
NEG_INF = -1e30
ATTN_SCALE = HEAD_DIM ** -0.5
CONV_K = 3
LANES = 128
MXU_COLS = 256
VMEM_BYTES_V7X = 64 * 1024 * 1024
VMEM_CAP = VMEM_BYTES_V7X - 6 * 1024 * 1024

ADAM_LR = 0.001
ADAM_B1 = 0.9
ADAM_B2 = 0.999
ADAM_EPS = 1e-08
ADAM_WD = 0.01
ADAM_STEP = 10

NN = (((1,), (0,)), ((), ()))
NT = (((1,), (1,)), ((), ()))
TN = (((0,), (0,)), ((), ()))

MESH = pl.DeviceIdType.MESH


def _nbytes(shape, dtype):
    n = 1
    for s in shape:
        if s is not None:
            n *= s
    return n * jnp.dtype(dtype).itemsize


def _params(semantics, block_bytes, temp_bytes):
    assert 2 * block_bytes + temp_bytes <= VMEM_CAP, (block_bytes, temp_bytes)
    return pltpu.CompilerParams(dimension_semantics=semantics, vmem_limit_bytes=VMEM_CAP)


def _fused(name, grid, ins, outs, dots, epilogue, *, nk=1, acc_shape=None, temp_bytes=0,
           semantics=("parallel", "parallel", "arbitrary"), deps=(), side=None):
    n_main_in, n_main_out = len(ins), len(outs)
    if side is not None:
        ins, outs = list(ins) + list(side[0]), list(outs) + list(side[1])
    n_in, n_out = len(ins), len(outs)
    n_dep = len(deps)

    def body(*refs):
        in_refs, out_refs = refs[:n_in], refs[n_in + n_dep:n_in + n_dep + n_out]
        scratch = refs[n_in + n_dep + n_out:]
        if side is not None:
            side[2](in_refs[n_main_in:], out_refs[n_main_out:])

        def products():
            if callable(dots):
                return dots(in_refs)
            total = None
            for ai, bi, contract in dots:
                a, b = in_refs[ai][...], in_refs[bi][...]
                a = a if a.dtype == BF16 else a.astype(BF16)
                b = b if b.dtype == BF16 else b.astype(BF16)
                p = lax.dot_general(a, b, contract, preferred_element_type=F32)
                total = p if total is None else total + p
            return total

        if nk == 1:
            epilogue(products() if dots else None, in_refs, out_refs)
        else:
            acc = scratch[0]
            k = pl.program_id(2)

            @pl.when(k == 0)
            def _():
                acc[...] = jnp.zeros_like(acc)

            acc[...] += products()

            @pl.when(k == nk - 1)
            def _():
                epilogue(acc[...], in_refs, out_refs)

    block_bytes = sum(_nbytes(spec.block_shape, a.dtype) for a, spec in ins)
    block_bytes += sum(_nbytes(spec.block_shape, s.dtype) for s, spec in outs)
    scratch_shapes = []
    if nk > 1:
        scratch_shapes.append(pltpu.VMEM(acc_shape, F32))
        temp_bytes += _nbytes(acc_shape, F32)
    res = pl.pallas_call(
        body, name=name, grid=grid,
        in_specs=[spec for _, spec in ins] + [pl.BlockSpec(memory_space=pl.ANY)] * n_dep,
        out_specs=[spec for _, spec in outs],
        out_shape=[s for s, _ in outs],
        scratch_shapes=scratch_shapes,
        compiler_params=_params(semantics, block_bytes, temp_bytes),
    )(*[a for a, _ in ins], *deps)
    return res


def _sds(shape, dtype):
    return jax.ShapeDtypeStruct(shape, dtype)


def _sigmoid(x):
    return jax.nn.sigmoid(x)


def _all_gather_small(name, shard, deps=()):
    n_dep = len(deps)

    def body(src, *rest):
        dst, send_sems, recv_sems, local_sem = rest[n_dep:]
        x, y, c = lax.axis_index("x"), lax.axis_index("y"), lax.axis_index("c")
        me = 4 * x + 2 * y + c
        copies = [pltpu.make_async_copy(src, dst.at[me], local_sem)]
        for k in range(1, N_DEV):
            peer = ((1 - x) if (k & 4) else x, (1 - y) if (k & 2) else y, (1 - c) if (k & 1) else c)
            copies.append(pltpu.make_async_remote_copy(
                src_ref=src, dst_ref=dst.at[me], send_sem=send_sems.at[k - 1], recv_sem=recv_sems.at[k - 1],
                device_id=peer, device_id_type=MESH))
        for cp in copies:
            cp.start()
        for cp in copies:
            cp.wait()

    hbm = pl.BlockSpec(memory_space=pltpu.HBM)
    return pl.pallas_call(
        body, name=name,
        in_specs=[hbm] + [pl.BlockSpec(memory_space=pl.ANY)] * n_dep, out_specs=hbm,
        out_shape=_sds((N_DEV,) + shard.shape, shard.dtype),
        scratch_shapes=[pltpu.SemaphoreType.DMA((N_DEV - 1,)), pltpu.SemaphoreType.DMA((N_DEV - 1,)),
                        pltpu.SemaphoreType.DMA],
    )(shard, *deps)


_HBM = pl.BlockSpec(memory_space=pltpu.HBM)
_SEM = pl.BlockSpec(memory_space=pltpu.SEMAPHORE)
_ANY = pl.BlockSpec(memory_space=pl.ANY)
_EFFECT = pltpu.SideEffectType.DATAFLOW_SIDE_EFFECTING
N_TARGETS = 4


def _mesh_pos():
    return lax.axis_index("x"), lax.axis_index("y"), lax.axis_index("c")


def _chip_peers(x, y, c):
    return [(1 - x, y, c), (x, 1 - y, c), (1 - x, 1 - y, c)]


def _dev_index(pos):
    return 4 * pos[0] + 2 * pos[1] + pos[2]


def _hbm_like(a):
    return pltpu.HBM(a.shape, a.dtype)


def _place_shard(name, w, out_dtype, me, tr, deps=()):
    r, c = w.shape
    n_dep = len(deps)

    def body(me_ref, w_ref, *rest):
        rest[n_dep][...] = w_ref[...].astype(out_dtype)

    grid_spec = pltpu.PrefetchScalarGridSpec(
        num_scalar_prefetch=1, grid=(r // tr,),
        in_specs=[pl.BlockSpec((tr, c), lambda i, me_ref: (i, 0))] + [_ANY] * n_dep,
        out_specs=pl.BlockSpec((None, tr, c), lambda i, me_ref: (me_ref[0], i, 0)))
    return pl.pallas_call(
        body, name=name, grid_spec=grid_spec, out_shape=_sds((N_DEV, r, c), out_dtype),
        compiler_params=_params(("parallel",), tr * c * 6, tr * c * 4),
    )(me, w, *deps)


def _gather_start(name, lands):
    n = len(lands)

    def body(*refs):
        bufs = refs[:n]
        send, recv = refs[n], refs[n + 1]
        token = refs[-1]
        x, y, c = _mesh_pos()
        me = _dev_index((x, y, c))
        targets = [(x, y, 1 - c)] + _chip_peers(x, y, c)
        for w in range(n):
            for k, to in enumerate(targets):
                pltpu.make_async_remote_copy(
                    src_ref=bufs[w].at[me], dst_ref=bufs[w].at[me],
                    send_sem=send.at[N_TARGETS * w + k], recv_sem=recv.at[N_TARGETS * w + k],
                    device_id=to, device_id_type=MESH).start()
        token[...] = jnp.zeros_like(token)

    sems = pltpu.SemaphoreType.DMA((N_TARGETS * n,))
    outs = pl.pallas_call(
        body, name=name,
        in_specs=[_HBM] * n, out_specs=[_SEM, _SEM] + [_HBM] * n + [_token_spec()],
        out_shape=[sems, sems] + [_hbm_like(a) for a in lands] + [_sds((8, LANES), F32)],
        input_output_aliases={i: 2 + i for i in range(n)},
        compiler_params=pltpu.CompilerParams(has_side_effects=_EFFECT),
    )(*lands)
    return outs[0], outs[1], list(outs[2:2 + n]), outs[-1]


def _gather_wait(name, positions, send, recv, lands, after):
    m = len(positions)

    def body(*refs):
        bufs = refs[:m]
        send_sems, recv_sems = refs[m], refs[m + 1]
        x, y, c = _mesh_pos()
        me = _dev_index((x, y, c))
        sources = [(x, y, 1 - c)] + _chip_peers(x, y, c)
        for j, w in enumerate(positions):
            for k, frm in enumerate(sources):
                cp = pltpu.make_async_remote_copy(
                    src_ref=bufs[j].at[me], dst_ref=bufs[j].at[_dev_index(frm)],
                    send_sem=send_sems.at[N_TARGETS * w + k], recv_sem=recv_sems.at[N_TARGETS * w + k],
                    device_id=frm, device_id_type=MESH)
                cp.wait_send()
                cp.wait_recv()

    outs = pl.pallas_call(
        body, name=name,
        in_specs=[_HBM] * m + [_SEM, _SEM, _ANY], out_specs=[_HBM] * m,
        out_shape=[_hbm_like(a) for a in lands],
        input_output_aliases={i: i for i in range(m)},
        compiler_params=pltpu.CompilerParams(has_side_effects=_EFFECT),
    )(*lands, send, recv, after)
    return list(outs)


def _forward_to_sibling(name, lands):
    m = len(lands)

    def body(*refs):
        copies = _forward_copies(refs[m:2 * m], refs[2 * m], refs[2 * m + 1])
        for cp in copies:
            cp.start()
        for cp in copies:
            cp.wait()

    outs = pl.pallas_call(
        body, name=name,
        in_specs=[_HBM] * m, out_specs=[_HBM] * m,
        out_shape=[_sds(a.shape, a.dtype) for a in lands],
        input_output_aliases={i: i for i in range(m)},
        scratch_shapes=[pltpu.SemaphoreType.DMA((3 * m,)), pltpu.SemaphoreType.DMA((3 * m,))],
    )(*lands)
    return list(outs)


def _forward_copies(bufs, send, recv):
    x, y, c = _mesh_pos()
    copies = []
    for j, buf in enumerate(bufs):
        for k, chip in enumerate(_chip_peers(x, y, c)):
            block = buf.at[_dev_index(chip)]
            copies.append(pltpu.make_async_remote_copy(
                src_ref=block, dst_ref=block, send_sem=send.at[3 * j + k], recv_sem=recv.at[3 * j + k],
                device_id=(x, y, 1 - c), device_id_type=MESH))
    return copies


def _forward_start(name, lands):
    m = len(lands)

    def body(*refs):
        for cp in _forward_copies(refs[:m], refs[m], refs[m + 1]):
            cp.start()

    sems = pltpu.SemaphoreType.DMA((3 * m,))
    outs = pl.pallas_call(
        body, name=name,
        in_specs=[_HBM] * m, out_specs=[_SEM, _SEM] + [_HBM] * m,
        out_shape=[sems, sems] + [_hbm_like(a) for a in lands],
        input_output_aliases={i: 2 + i for i in range(m)},
        compiler_params=pltpu.CompilerParams(has_side_effects=_EFFECT),
    )(*lands)
    return outs[0], outs[1], list(outs[2:])


def _forward_wait(name, send, recv, lands, after):
    m = len(lands)

    def body(*refs):
        for cp in _forward_copies(refs[:m], refs[m], refs[m + 1]):
            cp.wait_send()
            cp.wait_recv()

    outs = pl.pallas_call(
        body, name=name,
        in_specs=[_HBM] * m + [_SEM, _SEM, _ANY], out_specs=[_HBM] * m,
        out_shape=[_hbm_like(a) for a in lands],
        input_output_aliases={i: i for i in range(m)},
        compiler_params=pltpu.CompilerParams(has_side_effects=_EFFECT),
    )(*lands, send, recv, after)
    return list(outs)


def _token_spec():
    return pl.BlockSpec(memory_space=pltpu.VMEM)


def _pair_start(name, stacks, lands, deps=()):
    n = len(stacks)
    n_dep = len(deps)

    def body(*refs):
        srcs, dsts = refs[:n], refs[n:2 * n]
        send, recv = refs[2 * n + n_dep], refs[2 * n + n_dep + 1]
        token = refs[-1]
        x, y, c = _mesh_pos()
        for w in range(n):
            for chip in range(4):
                pltpu.make_async_remote_copy(
                    src_ref=srcs[w].at[chip, 1 - c], dst_ref=dsts[w].at[chip],
                    send_sem=send.at[4 * w + chip], recv_sem=recv.at[4 * w + chip],
                    device_id=(x, y, 1 - c), device_id_type=MESH).start()
        token[...] = jnp.zeros_like(token)

    sems = pltpu.SemaphoreType.DMA((4 * n,))
    outs = pl.pallas_call(
        body, name=name,
        in_specs=[_HBM] * (2 * n) + [_ANY] * n_dep, out_specs=[_SEM, _SEM] + [_HBM] * (2 * n) + [_token_spec()],
        out_shape=[sems, sems] + [_hbm_like(a) for a in stacks] + [_hbm_like(a) for a in lands] + [_sds((8, LANES), F32)],
        input_output_aliases={i: 2 + i for i in range(2 * n)},
        compiler_params=pltpu.CompilerParams(has_side_effects=_EFFECT),
    )(*stacks, *lands, *deps)
    return outs[0], outs[1], list(outs[2:2 + n]), list(outs[2 + n:2 + 2 * n]), outs[-1]


def _pair_wait(name, send, recv, stacks, lands, after):
    n = len(stacks)

    def body(*refs):
        srcs, dsts = refs[:n], refs[n:2 * n]
        send_sems, recv_sems = refs[2 * n], refs[2 * n + 1]
        x, y, c = _mesh_pos()
        for w in range(n):
            for chip in range(4):
                cp = pltpu.make_async_remote_copy(
                    src_ref=srcs[w].at[chip, 1 - c], dst_ref=dsts[w].at[chip],
                    send_sem=send_sems.at[4 * w + chip], recv_sem=recv_sems.at[4 * w + chip],
                    device_id=(x, y, 1 - c), device_id_type=MESH)
                cp.wait_send()
                cp.wait_recv()

    outs = pl.pallas_call(
        body, name=name,
        in_specs=[_HBM] * (2 * n) + [_SEM, _SEM, _ANY], out_specs=[_HBM] * (2 * n),
        out_shape=[_hbm_like(a) for a in stacks] + [_hbm_like(a) for a in lands],
        input_output_aliases={i: i for i in range(2 * n)},
        compiler_params=pltpu.CompilerParams(has_side_effects=_EFFECT),
    )(*stacks, *lands, send, recv, after)
    return list(outs[:n]), list(outs[n:])


def _pair_add(name, stack, land, place, tr):
    _, _, r, c = stack.shape

    def body(place_ref, a_ref, b_ref, sums_ref, slots_ref):
        total = (a_ref[...].astype(F32) + b_ref[...].astype(F32)).astype(BF16)
        sums_ref[...] = total

        @pl.when(pl.program_id(1) == place_ref[1])
        def _():
            slots_ref[...] = total

    grid_spec = pltpu.PrefetchScalarGridSpec(
        num_scalar_prefetch=1, grid=(r // tr, 4),
        in_specs=[pl.BlockSpec((None, None, tr, c), lambda i, k, place_ref: (k, place_ref[0], i, 0)),
                  pl.BlockSpec((None, tr, c), lambda i, k, place_ref: (k, i, 0))],
        out_specs=[pl.BlockSpec((None, tr, c), lambda i, k, place_ref: (k, i, 0)),
                   pl.BlockSpec((None, tr, c), lambda i, k, place_ref: (place_ref[1], i, 0))])
    return pl.pallas_call(
        body, name=name, grid_spec=grid_spec, out_shape=[_sds((4, r, c), BF16)] * 2,
        compiler_params=_params(("parallel", "arbitrary"), 4 * tr * c * 2, 3 * tr * c * 4),
    )(place, stack, land)


def _sibling_copies(srcs, dsts, send, recv):
    x, y, c = _mesh_pos()
    return [pltpu.make_async_remote_copy(
        src_ref=srcs[w], dst_ref=dsts[w], send_sem=send.at[w], recv_sem=recv.at[w],
        device_id=(x, y, 1 - c), device_id_type=MESH) for w in range(len(srcs))]


def _sibling_start(name, srcs):
    n = len(srcs)
    lands = [lax.empty(a.shape, a.dtype) for a in srcs]

    def body(*refs):
        for cp in _sibling_copies(refs[:n], refs[n:2 * n], refs[2 * n], refs[2 * n + 1]):
            cp.start()
        token = refs[-1]
        token[...] = jnp.zeros_like(token)

    sems = pltpu.SemaphoreType.DMA((n,))
    outs = pl.pallas_call(
        body, name=name,
        in_specs=[_HBM] * (2 * n), out_specs=[_SEM, _SEM] + [_HBM] * (2 * n) + [_token_spec()],
        out_shape=[sems, sems] + [_hbm_like(a) for a in srcs] + [_hbm_like(a) for a in lands] + [_sds((8, LANES), F32)],
        input_output_aliases={i: 2 + i for i in range(2 * n)},
        compiler_params=pltpu.CompilerParams(has_side_effects=_EFFECT),
    )(*srcs, *lands)
    return outs[0], outs[1], list(outs[2:2 + n]), list(outs[2 + n:2 + 2 * n]), outs[-1]


def _sibling_wait(name, send, recv, srcs, lands, after):
    n = len(srcs)

    def body(*refs):
        for cp in _sibling_copies(refs[:n], refs[n:2 * n], refs[2 * n], refs[2 * n + 1]):
            cp.wait_send()
            cp.wait_recv()

    outs = pl.pallas_call(
        body, name=name,
        in_specs=[_HBM] * (2 * n) + [_SEM, _SEM, _ANY], out_specs=[_HBM] * (2 * n),
        out_shape=[_hbm_like(a) for a in srcs] + [_hbm_like(a) for a in lands],
        input_output_aliases={i: i for i in range(2 * n)},
        compiler_params=pltpu.CompilerParams(has_side_effects=_EFFECT),
    )(*srcs, *lands, send, recv, after)
    return list(outs[:n]), list(outs[n:])


def _chip_start(name, parts, lands):
    n = len(parts)

    def body(*refs):
        srcs, dsts = refs[:n], refs[n:2 * n]
        send, recv = refs[2 * n], refs[2 * n + 1]
        token = refs[-1]
        x, y, c = _mesh_pos()
        for w in range(n):
            for k, to in enumerate(_chip_peers(x, y, c)):
                pltpu.make_async_remote_copy(
                    src_ref=srcs[w].at[2 * to[0] + to[1]], dst_ref=dsts[w].at[2 * x + y],
                    send_sem=send.at[3 * w + k], recv_sem=recv.at[3 * w + k],
                    device_id=to, device_id_type=MESH).start()
        token[...] = jnp.zeros_like(token)

    sems = pltpu.SemaphoreType.DMA((3 * n,))
    outs = pl.pallas_call(
        body, name=name,
        in_specs=[_HBM] * (2 * n), out_specs=[_SEM, _SEM] + [_HBM] * (2 * n) + [_token_spec()],
        out_shape=[sems, sems] + [_hbm_like(a) for a in parts] + [_hbm_like(a) for a in lands] + [_sds((8, LANES), F32)],
        input_output_aliases={i: 2 + i for i in range(2 * n)},
        compiler_params=pltpu.CompilerParams(has_side_effects=_EFFECT),
    )(*parts, *lands)
    return outs[0], outs[1], list(outs[2:2 + n]), list(outs[2 + n:2 + 2 * n]), outs[-1]


def _chip_wait(name, send, recv, parts, lands, after):
    n = len(parts)

    def body(*refs):
        srcs, dsts = refs[:n], refs[n:2 * n]
        send_sems, recv_sems = refs[2 * n], refs[2 * n + 1]
        x, y, c = _mesh_pos()
        for w in range(n):
            for k, frm in enumerate(_chip_peers(x, y, c)):
                chip = 2 * frm[0] + frm[1]
                cp = pltpu.make_async_remote_copy(
                    src_ref=srcs[w].at[chip], dst_ref=dsts[w].at[chip],
                    send_sem=send_sems.at[3 * w + k], recv_sem=recv_sems.at[3 * w + k],
                    device_id=frm, device_id_type=MESH)
                cp.wait_send()
                cp.wait_recv()

    outs = pl.pallas_call(
        body, name=name,
        in_specs=[_HBM] * (2 * n) + [_SEM, _SEM, _ANY], out_specs=[_HBM] * (2 * n),
        out_shape=[_hbm_like(a) for a in parts] + [_hbm_like(a) for a in lands],
        input_output_aliases={i: i for i in range(2 * n)},
        compiler_params=pltpu.CompilerParams(has_side_effects=_EFFECT),
    )(*parts, *lands, send, recv, after)
    return list(outs[:n]), list(outs[n:])


def _row_tile(t):
    return min(t, 256)


def _rms_fwd(name, x, g):
    t, d = x.shape
    tm = _row_tile(t)

    def epilogue(_, ins, outs):
        xv = ins[0][...]
        r = lax.rsqrt(jnp.mean(xv * xv, axis=-1, keepdims=True) + RMS_EPS)
        outs[0][...] = (xv * r * ins[1][...]).astype(BF16)

    row = pl.BlockSpec((tm, d), lambda i, j, k: (i, 0))
    vec = pl.BlockSpec((1, d), lambda i, j, k: (0, 0))
    return _fused(name, (t // tm, 1, 1), [(x, row), (g, vec)], [(_sds((t, d), BF16), row)], [], epilogue,
                  temp_bytes=4 * tm * d * 4)[0]


def _rms_bwd(name, x, g, dh, resid, deps=(), with_bf16=False):
    t, d = x.shape
    tm = _row_tile(t)

    def epilogue(_, ins, outs):
        xv, gv, dhv = ins[0][...], ins[1][...], ins[2][...]
        r = lax.rsqrt(jnp.mean(xv * xv, axis=-1, keepdims=True) + RMS_EPS)
        xh = xv * r
        u = dhv * gv
        dot = jnp.mean(u * xh, axis=-1, keepdims=True)
        dx = ins[3][...] + r * (u - xh * dot)
        outs[0][...] = dx
        if with_bf16:
            outs[2][...] = dx.astype(BF16)

        @pl.when(pl.program_id(0) == 0)
        def _():
            outs[1][...] = jnp.zeros_like(outs[1])

        outs[1][0:1, :] += jnp.sum(dhv * xh, axis=0, keepdims=True)

    row = pl.BlockSpec((tm, d), lambda i, j, k: (i, 0))
    vec = pl.BlockSpec((1, d), lambda i, j, k: (0, 0))
    acc = pl.BlockSpec((8, d), lambda i, j, k: (0, 0))
    outs = [(_sds((t, d), F32), row), (_sds((8, d), F32), acc)] + ([(_sds((t, d), BF16), row)] if with_bf16 else [])
    return _fused(name, (t // tm, 1, 1), [(x, row), (g, vec), (dh, row), (resid, row)], outs, [], epilogue,
                  temp_bytes=6 * tm * d * 4, semantics=("arbitrary", "arbitrary", "arbitrary"), deps=deps)


def _ffn_up(name, h, wgu, parity=None, into=None):
    t, d = h.shape
    nb = wgu.shape[2]
    f = 4 * nb
    tm = min(t, 512)

    def body(h_ref, wg_ref, wu_ref, gu_ref, a_ref):
        hv = h_ref[...]
        for c0 in range(0, nb, MXU_COLS):
            cs = slice(c0, min(c0 + MXU_COLS, nb))
            g = jnp.dot(hv, wg_ref[:, cs], preferred_element_type=F32)
            u = jnp.dot(hv, wu_ref[:, cs], preferred_element_type=F32)
            gu_ref[0, :, cs] = g.astype(BF16)
            gu_ref[1, :, cs] = u.astype(BF16)
            a_ref[:, cs] = (g * _sigmoid(g) * u).astype(BF16)

    blocks = tm * d * 2 + 2 * d * nb * 2 + 3 * tm * nb * 2
    params = _params(("parallel", "parallel"), blocks, 8 * tm * MXU_COLS * 4)
    out_shape = [_sds((2, t, f), BF16), _sds((t, f), BF16)]
    if parity is None:
        return pl.pallas_call(
            body, name=name, grid=(4, t // tm),
            in_specs=[pl.BlockSpec((tm, d), lambda j, i: (i, 0)),
                      pl.BlockSpec((None, d, nb), lambda j, i: (j, 0, 0)),
                      pl.BlockSpec((None, d, nb), lambda j, i: (j + 4, 0, 0))],
            out_specs=[pl.BlockSpec((2, tm, nb), lambda j, i: (0, i, j)),
                       pl.BlockSpec((tm, nb), lambda j, i: (i, j))],
            out_shape=out_shape, compiler_params=params,
        )(h, wgu, wgu)

    def half_body(parity_ref, h_ref, wg_ref, wu_ref, *rest):
        body(h_ref, wg_ref, wu_ref, rest[-2], rest[-1])

    n_pass = 0 if into is None else 2
    grid_spec = pltpu.PrefetchScalarGridSpec(
        num_scalar_prefetch=1, grid=(2, t // tm),
        in_specs=[pl.BlockSpec((tm, d), lambda jj, i, p: (i, 0)),
                  pl.BlockSpec((None, d, nb), lambda jj, i, p: (2 * jj + p[0], 0, 0)),
                  pl.BlockSpec((None, d, nb), lambda jj, i, p: (2 * jj + p[0] + 4, 0, 0))] + [_ANY] * n_pass,
        out_specs=[pl.BlockSpec((2, tm, nb), lambda jj, i, p: (0, i, 2 * jj + p[0])),
                   pl.BlockSpec((tm, nb), lambda jj, i, p: (i, 2 * jj + p[0]))])
    return pl.pallas_call(
        half_body, name=name, grid_spec=grid_spec, out_shape=out_shape,
        input_output_aliases={} if into is None else {4: 0, 5: 1}, compiler_params=params,
    )(parity, h, wgu, wgu, *(into or ()))


def _ffn_down(name, a, wd, x, target=None):
    t, f = a.shape
    d = wd.shape[1]
    tm = min(t, 512)
    tn = min(d, 1024)
    blk = pl.BlockSpec((tm, tn), lambda j, i, k: (i, j))
    ins = [(a, pl.BlockSpec((tm, f), lambda j, i, k: (i, 0))), (wd, pl.BlockSpec((f, tn), lambda j, i, k: (0, j))), (x, blk)]

    if target is None:
        def epilogue(acc, ins, outs):
            outs[0][...] = ins[2][...] + 0.5 * acc

        return _fused(name, (d // tn, t // tm, 1), ins, [(_sds((t, d), F32), blk)],
                      [(0, 1, NN)], epilogue, temp_bytes=2 * tm * tn * 4)[0]

    def epilogue(acc, ins, outs):
        e = ins[2][...] + 0.5 * acc - ins[3][...]
        outs[0][...] = e * (1.0 / d)
        outs[2][...] = (e * (1.0 / d)).astype(BF16)

        @pl.when((pl.program_id(0) == 0) & (pl.program_id(1) == 0))
        def _():
            outs[1][...] = jnp.zeros_like(outs[1])

        part = jnp.sum(jnp.sum(e * e, axis=1, keepdims=True), axis=0, keepdims=True)
        outs[1][...] += jnp.broadcast_to(part, outs[1].shape)

    return _fused(name, (d // tn, t // tm, 1), ins + [(target, blk)],
                  [(_sds((t, d), F32), blk), (_sds((8, LANES), F32), pl.BlockSpec((8, LANES), lambda j, i, k: (0, 0))),
                   (_sds((t, d), BF16), blk)],
                  [(0, 1, NN)], epilogue, temp_bytes=3 * tm * tn * 4,
                  semantics=("arbitrary", "arbitrary", "arbitrary"))


def _ffn_bwd_act(name, dy, wd, gu, deps=()):
    t, d = dy.shape
    f = wd.shape[0]
    nb = f // 4
    tm = min(t, 512)

    def body(dy_ref, wd_ref, gu_ref, *rest):
        dgu_ref, a_ref = rest[-2], rest[-1]
        dyv = dy_ref[...].astype(BF16)
        for c0 in range(0, nb, MXU_COLS):
            cs = slice(c0, min(c0 + MXU_COLS, nb))
            da = 0.5 * lax.dot_general(dyv, wd_ref[cs, :], NT, preferred_element_type=F32)
            g = gu_ref[0, :, cs].astype(F32)
            u = gu_ref[1, :, cs].astype(F32)
            s = _sigmoid(g)
            silu = g * s
            dgu_ref[0, :, cs] = (da * u * (s * (1.0 + g * (1.0 - s)))).astype(BF16)
            dgu_ref[1, :, cs] = (da * silu).astype(BF16)
            a_ref[:, cs] = (silu * u).astype(BF16)

    blocks = tm * d * 4 + nb * d * 2 + 5 * tm * nb * 2
    return pl.pallas_call(
        body, name=name, grid=(4, t // tm),
        in_specs=[pl.BlockSpec((tm, d), lambda j, i: (i, 0)),
                  pl.BlockSpec((nb, d), lambda j, i: (j, 0)),
                  pl.BlockSpec((2, tm, nb), lambda j, i: (0, i, j))] + [_ANY] * len(deps),
        out_specs=[pl.BlockSpec((2, tm, nb), lambda j, i: (0, i, j)), pl.BlockSpec((tm, nb), lambda j, i: (i, j))],
        out_shape=[_sds((2, t, f), BF16), _sds((t, f), BF16)],
        compiler_params=_params(("parallel", "parallel"), blocks, tm * d * 2 + 8 * tm * MXU_COLS * 4),
    )(dy, wd, gu, *deps)


def _ffn_bwd_dwd(name, a, dy, deps=(), side=None):
    t, f = a.shape
    d = dy.shape[1]
    tm = f // 4
    tn = min(d, 512)

    def epilogue(acc, ins, outs):
        outs[0][...] = (0.5 * acc).astype(BF16)

    return _fused(name, (4, d // tn, 1),
                  [(a, pl.BlockSpec((t, tm), lambda i, j, k: (0, i))),
                   (dy, pl.BlockSpec((t, tn), lambda i, j, k: (0, j)))],
                  [(_sds((f, d), BF16), pl.BlockSpec((tm, tn), lambda i, j, k: (i, j)))],
                  [(0, 1, TN)], epilogue, temp_bytes=t * tn * 2 + 2 * tm * tn * 4, deps=deps, side=side)


def _ffn_bwd_dh(name, dgu, wgu, deps=(), side=None):
    _, t, f = dgu.shape
    d, nb = wgu.shape[1], wgu.shape[2]
    tm = min(t, 512)

    def products(ins):
        return (lax.dot_general(ins[0][:, 0:nb], ins[1][0], NT, preferred_element_type=F32)
                + lax.dot_general(ins[0][:, nb:2 * nb], ins[1][1], NT, preferred_element_type=F32))

    def epilogue(acc, ins, outs):
        outs[0][...] = acc

    return _fused(name, (t // tm, 1, 4),
                  [(dgu, pl.BlockSpec((None, tm, 2 * nb), lambda i, j, k: (k // 2, i, k % 2))),
                   (wgu, pl.BlockSpec((2, d, nb), lambda i, j, k: (k, 0, 0)))],
                  [(_sds((t, d), F32), pl.BlockSpec((tm, d), lambda i, j, k: (i, 0)))],
                  products, epilogue, nk=4, acc_shape=(tm, d), temp_bytes=tm * d * 4, deps=deps, side=side)


def _ffn_bwd_dwgu(name, h, dgu, deps=(), side=None, rows=None):
    t, d = h.shape
    nb = dgu.shape[2] // 4
    tm = min(d, 512)
    row0, nrows = rows if rows is not None else (0, d)
    j0 = row0 // tm

    def epilogue(acc, ins, outs):
        outs[0][...] = acc.astype(BF16)

    return _fused(name, (N_DEV, nrows // tm, 1),
                  [(h, pl.BlockSpec((t, tm), lambda i, j, k: (0, j0 + j))),
                   (dgu, pl.BlockSpec((None, t, nb), lambda i, j, k: (i // 4, 0, i % 4)))],
                  [(_sds((N_DEV, nrows, nb), BF16), pl.BlockSpec((None, tm, nb), lambda i, j, k: (i, j, 0)))],
                  [(0, 1, TN)], epilogue, temp_bytes=2 * tm * nb * 4, deps=deps, side=side)


def _ffn_bwd_dwgu_pair(name, h, h_sib, dgu, dgu_sib, place):
    t, d = h.shape
    nb = dgu.shape[2] // 4
    tm = min(d, 512)

    def body(place_ref, h_ref, hs_ref, g_ref, gs_ref, sums_ref, slots_ref):
        acc = lax.dot_general(h_ref[...], g_ref[...], TN, preferred_element_type=F32)
        acc += lax.dot_general(hs_ref[...], gs_ref[...], TN, preferred_element_type=F32)
        total = acc.astype(BF16)
        sums_ref[...] = total

        @pl.when(pl.program_id(1) == place_ref[1])
        def _():
            slots_ref[...] = total

    def act_map(i, k, place_ref):
        return (0, i)

    def grad_map(i, k, place_ref):
        dev = 2 * k + place_ref[0]
        return (dev // 4, 0, dev % 4)

    grid_spec = pltpu.PrefetchScalarGridSpec(
        num_scalar_prefetch=1, grid=(d // tm, 4),
        in_specs=[pl.BlockSpec((t, tm), act_map), pl.BlockSpec((t, tm), act_map),
                  pl.BlockSpec((None, t, nb), grad_map), pl.BlockSpec((None, t, nb), grad_map)],
        out_specs=[pl.BlockSpec((None, tm, nb), lambda i, k, place_ref: (k, i, 0)),
                   pl.BlockSpec((None, tm, nb), lambda i, k, place_ref: (place_ref[1], i, 0))])
    blocks = 2 * t * tm * 2 + 2 * t * nb * 2 + 2 * tm * nb * 2
    return pl.pallas_call(
        body, name=name, grid_spec=grid_spec, out_shape=[_sds((4, d, nb), BF16)] * 2,
        compiler_params=_params(("parallel", "arbitrary"), blocks, 2 * tm * nb * 4),
    )(place, h, h_sib, dgu, dgu_sib)


def _proj(h, w_in):
    t, d = h.shape
    nb = w_in.shape[3]
    tm = min(t, 512)

    def body(h_ref, w_ref, o_ref):
        hv = h_ref[...]
        o_ref[:, 0:nb] = jnp.dot(hv, w_ref[0], preferred_element_type=F32).astype(BF16)
        o_ref[:, nb:2 * nb] = jnp.dot(hv, w_ref[1], preferred_element_type=F32).astype(BF16)

    blocks = tm * d * 2 + 2 * d * nb * 2 + tm * 2 * nb * 4
    return pl.pallas_call(
        body, name="mix_proj", grid=(4, t // tm),
        in_specs=[pl.BlockSpec((tm, d), lambda j, i: (i, 0)),
                  pl.BlockSpec((None, 2, d, nb), lambda j, i: (j, 0, 0, 0))],
        out_specs=pl.BlockSpec((tm, 2 * nb), lambda j, i: (i, j)),
        out_shape=_sds((t, N_DEV * nb), BF16),
        compiler_params=_params(("parallel", "parallel"), blocks, 2 * tm * nb * 4),
    )(h, w_in)


def _shift_rows(u, k):
    t = u.shape[0]
    rolled = pltpu.roll(u, k % t, axis=0)
    row = lax.broadcasted_iota(jnp.int32, u.shape, 0)
    keep = (row >= k) if k > 0 else (row < t + k)
    return jnp.where(keep, rolled, 0.0)


def _conv_fwd(proj, conv_w):
    t = proj.shape[0]
    cw = conv_w.shape[1]
    tc = min(cw, 256)
    nc = cw // tc

    def epilogue(_, ins, outs):
        u = ins[2][...].astype(F32) * ins[0][...].astype(F32)
        w = ins[3][...]
        y = u * w[2:3, :] + _shift_rows(u, 1) * w[1:2, :] + _shift_rows(u, 2) * w[0:1, :]
        outs[0][...] = (ins[1][...].astype(F32) * y).astype(BF16)

    def col(seg):
        return pl.BlockSpec((t, tc), lambda i, j, k: (0, seg * nc + i))

    return _fused("conv_fwd", (nc, 1, 1),
                  [(proj, col(0)), (proj, col(1)), (proj, col(2)),
                   (conv_w, pl.BlockSpec((8, tc), lambda i, j, k: (0, i)))],
                  [(_sds((t, cw), BF16), pl.BlockSpec((t, tc), lambda i, j, k: (0, i)))],
                  [], epilogue, temp_bytes=6 * t * tc * 4)[0]


def _conv_bwd(proj, conv_w, dca, deps=()):
    t = proj.shape[0]
    cw = conv_w.shape[1]
    tc = min(cw, 256)
    nc = cw // tc

    def epilogue(_, ins, outs):
        xc, bg, cg = ins[0][...].astype(F32), ins[1][...].astype(F32), ins[2][...].astype(F32)
        w, dc = ins[3][...], ins[4][...]
        u = cg * xc
        u1, u2 = _shift_rows(u, 1), _shift_rows(u, 2)
        y = u * w[2:3, :] + u1 * w[1:2, :] + u2 * w[0:1, :]
        dconv = dc * bg
        du = dconv * w[2:3, :] + _shift_rows(dconv, -1) * w[1:2, :] + _shift_rows(dconv, -2) * w[0:1, :]
        outs[0][0] = (du * cg).astype(BF16)
        outs[0][1] = (dc * y).astype(BF16)
        outs[0][2] = (du * xc).astype(BF16)
        outs[1][...] = jnp.zeros_like(outs[1])
        outs[1][0:1, :] = jnp.sum(dconv * u2, axis=0, keepdims=True)
        outs[1][1:2, :] = jnp.sum(dconv * u1, axis=0, keepdims=True)
        outs[1][2:3, :] = jnp.sum(dconv * u, axis=0, keepdims=True)

    def col(seg):
        return pl.BlockSpec((t, tc), lambda i, j, k: (0, seg * nc + i))

    own = pl.BlockSpec((t, tc), lambda i, j, k: (0, i))
    wspec = pl.BlockSpec((8, tc), lambda i, j, k: (0, i))
    return _fused("conv_bwd", (nc, 1, 1),
                  [(proj, col(0)), (proj, col(1)), (proj, col(2)), (conv_w, wspec), (dca, own)],
                  [(_sds((3, t, cw), BF16), pl.BlockSpec((3, t, tc), lambda i, j, k: (0, 0, i))),
                   (_sds((8, cw), F32), wspec)],
                  [], epilogue, temp_bytes=10 * t * tc * 4, deps=deps)


def _split3(x):
    hi = x.astype(BF16)
    r1 = x - hi.astype(F32)
    mid = r1.astype(BF16)
    lo = (r1 - mid.astype(F32)).astype(BF16)
    return hi, mid, lo


def _head_selector(width):
    r = lax.broadcasted_iota(jnp.int32, (width, LANES), 0)
    c = lax.broadcasted_iota(jnp.int32, (width, LANES), 1)
    return (lax.shift_right_logical(r, 6) == c).astype(BF16)


def _head_sum(x, sel):
    return sum(jnp.dot(p, sel, preferred_element_type=F32) for p in _split3(x))


def _head_bcast(r, sel):
    return sum(lax.dot_general(p, sel, NT, preferred_element_type=F32) for p in _split3(r))


def _rope(x, c, sa, sb):
    n = x.shape[1]
    return x * c + pltpu.roll(x, n - ROT_DIM // 2, axis=1) * sa + pltpu.roll(x, ROT_DIM // 2, axis=1) * sb


def _rope_t(d, c, sa, sb):
    n = d.shape[1]
    return d * c + pltpu.roll(d * sa, ROT_DIM // 2, axis=1) + pltpu.roll(d * sb, n - ROT_DIM // 2, axis=1)


def _tile_lanes(tab, width):
    return tab if width == tab.shape[1] else jnp.tile(tab, (1, width // tab.shape[1]))


def _qk_prep(proj, gq, gk, rope_tabs, cw, kw):
    t = proj.shape[0]
    tm = _row_tile(t)

    def epilogue(_, ins, outs):
        c, sa, sb = ins[5][...], ins[6][...], ins[7][...]
        for src, gain, dst, width in ((0, 3, 0, cw), (1, 4, 1, kw)):
            xv = ins[src][...].astype(F32)
            sel = _head_selector(width)
            r = lax.rsqrt(_head_sum(xv * xv, sel) * (1.0 / HEAD_DIM) + RMS_EPS)
            xn = xv * _head_bcast(r, sel) * ins[gain][...]
            outs[dst][...] = _rope(xn, _tile_lanes(c, width), _tile_lanes(sa, width), _tile_lanes(sb, width)).astype(BF16)
        outs[2][...] = ins[2][...].astype(BF16)

    kblk = cw // kw
    tab = pl.BlockSpec((tm, LANES), lambda i, j, k: (i, 0))
    kspec = pl.BlockSpec((tm, kw), lambda i, j, k: (i, 0))
    return _fused("qk_prep", (t // tm, 1, 1),
                  [(proj, pl.BlockSpec((tm, cw), lambda i, j, k: (i, 3))),
                   (proj, pl.BlockSpec((tm, kw), lambda i, j, k: (i, 4 * kblk))),
                   (proj, pl.BlockSpec((tm, kw), lambda i, j, k: (i, 4 * kblk + 1))),
                   (gq, pl.BlockSpec((1, cw), lambda i, j, k: (0, 0))),
                   (gk, pl.BlockSpec((1, kw), lambda i, j, k: (0, 0))),
                   (rope_tabs[0], tab), (rope_tabs[1], tab), (rope_tabs[2], tab)],
                  [(_sds((t, cw), BF16), pl.BlockSpec((tm, cw), lambda i, j, k: (i, 0))),
                   (_sds((t, kw), BF16), kspec), (_sds((t, kw), BF16), kspec)],
                  [], epilogue, temp_bytes=12 * tm * cw * 4)


def _qk_prep_bwd(proj, gq, gk, rope_tabs, dq, dkc, dkp, dvc, dvp, cw, kw):
    t = proj.shape[0]
    tm = BLOCK
    nblk = t // tm

    def epilogue(_, ins, outs):
        c, sa, sb = ins[5][...], ins[6][...], ins[7][...]
        has_next = (pl.program_id(0) < nblk - 1).astype(F32)
        dk = ins[9][...] + has_next * ins[10][...]
        dv = ins[11][...] + has_next * ins[12][...]
        pieces = []
        for src, gain, dval, dst, width in ((0, 3, ins[8][...], 1, cw), (1, 4, dk, 2, kw)):
            xv, gv = ins[src][...].astype(F32), ins[gain][...]
            sel = _head_selector(width)
            r = _head_bcast(lax.rsqrt(_head_sum(xv * xv, sel) * (1.0 / HEAD_DIM) + RMS_EPS), sel)
            xh = xv * r
            dxn = _rope_t(dval, _tile_lanes(c, width), _tile_lanes(sa, width), _tile_lanes(sb, width))
            u = dxn * gv
            dot = _head_bcast(_head_sum(u * xh, sel), sel) * (1.0 / HEAD_DIM)
            pieces.append((r * (u - xh * dot)).astype(BF16))
            ri = lax.broadcasted_iota(jnp.int32, (width, LANES), 0)
            ci = lax.broadcasted_iota(jnp.int32, (width, LANES), 1)
            fold = (lax.bitwise_and(ri, HEAD_DIM - 1) == ci).astype(BF16)
            colsum = jnp.broadcast_to(jnp.sum(dxn * xh, axis=0, keepdims=True), (8, width))
            part = sum(jnp.dot(p, fold, preferred_element_type=F32) for p in _split3(colsum))

            @pl.when(pl.program_id(0) == 0)
            def _():
                outs[dst][...] = jnp.zeros_like(outs[dst])

            outs[dst][0:1, :] += part[0:1, :]
        outs[0][:, 0:cw] = pieces[0]
        outs[0][:, cw:cw + kw] = pieces[1]
        outs[0][:, cw + kw:cw + 2 * kw] = dv.astype(BF16)

    kblk = cw // kw
    tab = pl.BlockSpec((tm, LANES), lambda i, j, k: (i, 0))
    kcur = pl.BlockSpec((tm, kw), lambda i, j, k: (i, 0))
    knext = pl.BlockSpec((tm, kw), lambda i, j, k: (jnp.minimum(i + 1, nblk - 1), 0))
    acc = pl.BlockSpec((8, LANES), lambda i, j, k: (0, 0))
    return _fused("qk_prep_bwd", (nblk, 1, 1),
                  [(proj, pl.BlockSpec((tm, cw), lambda i, j, k: (i, 3))),
                   (proj, pl.BlockSpec((tm, kw), lambda i, j, k: (i, 4 * kblk))),
                   (proj, pl.BlockSpec((tm, kw), lambda i, j, k: (i, 4 * kblk + 1))),
                   (gq, pl.BlockSpec((1, cw), lambda i, j, k: (0, 0))),
                   (gk, pl.BlockSpec((1, kw), lambda i, j, k: (0, 0))),
                   (rope_tabs[0], tab), (rope_tabs[1], tab), (rope_tabs[2], tab),
                   (dq, pl.BlockSpec((tm, cw), lambda i, j, k: (i, 0))),
                   (dkc, kcur), (dkp, knext), (dvc, kcur), (dvp, knext)],
                  [(_sds((t, cw + 2 * kw), BF16), pl.BlockSpec((tm, cw + 2 * kw), lambda i, j, k: (i, 0))),
                   (_sds((8, LANES), F32), acc), (_sds((8, LANES), F32), acc)],
                  [], epilogue, temp_bytes=16 * tm * cw * 4, semantics=("arbitrary", "arbitrary", "arbitrary"))


def _attn_mask(n):
    key = lax.broadcasted_iota(jnp.int32, (2 * BLOCK, GROUP * BLOCK), 0)
    qry = lax.bitwise_and(lax.broadcasted_iota(jnp.int32, (2 * BLOCK, GROUP * BLOCK), 1), BLOCK - 1)
    return (key > qry) & (key <= qry + BLOCK) & ((key >= BLOCK) | (n > 0))


def _stack_heads(x, h):
    return jnp.concatenate([x[:, (h * GROUP + g) * HEAD_DIM:(h * GROUP + g + 1) * HEAD_DIM] for g in range(GROUP)], axis=0)


def _softmax_with_sink(q4, k2, sink_ref, h, valid):
    sink = jnp.concatenate([sink_ref[h * GROUP + g:h * GROUP + g + 1, :] for g in range(GROUP)], axis=1)
    s = lax.dot_general(k2, q4, NT, preferred_element_type=F32) * ATTN_SCALE
    s = jnp.where(valid, s, NEG_INF)
    m = jnp.maximum(jnp.max(s, axis=0, keepdims=True), sink)
    p = jnp.exp(s - m)
    es = jnp.exp(sink - m)
    inv = 1.0 / (jnp.sum(p, axis=0, keepdims=True) + es)
    return p * inv, es * inv


def _attn_fwd(qn, kn, vb, sink_rows):
    t, cw = qn.shape
    kw = kn.shape[1]
    nkv = kw // HEAD_DIM

    def body(q_ref, kp_ref, kc_ref, vp_ref, vc_ref, sink_ref, o_ref):
        valid = _attn_mask(pl.program_id(0))
        qv = q_ref[...]
        kp, kc, vp, vc = kp_ref[...], kc_ref[...], vp_ref[...], vc_ref[...]
        outs = []
        for h in range(nkv):
            hs = slice(h * HEAD_DIM, (h + 1) * HEAD_DIM)
            k2 = jnp.concatenate([kp[:, hs], kc[:, hs]], axis=0)
            v2 = jnp.concatenate([vp[:, hs], vc[:, hs]], axis=0)
            pn, _ = _softmax_with_sink(_stack_heads(qv, h), k2, sink_ref, h, valid)
            o4 = lax.dot_general(pn.astype(BF16), v2, TN, preferred_element_type=F32)
            outs += [o4[g * BLOCK:(g + 1) * BLOCK] for g in range(GROUP)]
        o_ref[...] = jnp.concatenate(outs, axis=-1).astype(BF16)

    cur = lambda n: (n, 0)
    prev = lambda n: (jnp.maximum(n - 1, 0), 0)
    return pl.pallas_call(
        body, name="attn_fwd", grid=(t // BLOCK,),
        in_specs=[pl.BlockSpec((BLOCK, cw), cur),
                  pl.BlockSpec((BLOCK, kw), prev), pl.BlockSpec((BLOCK, kw), cur),
                  pl.BlockSpec((BLOCK, kw), prev), pl.BlockSpec((BLOCK, kw), cur),
                  pl.BlockSpec(sink_rows.shape, lambda n: (0, 0))],
        out_specs=pl.BlockSpec((BLOCK, cw), cur),
        out_shape=_sds((t, cw), BF16),
        compiler_params=_params(("parallel",), BLOCK * (cw + 4 * kw) * 2 + BLOCK * cw * 2, 8 << 20),
    )(qn, kn, kn, vb, vb, sink_rows)


def _attn_bwd(qn, kn, vb, sink_rows, do):
    t, cw = qn.shape
    kw = kn.shape[1]
    nkv = kw // HEAD_DIM
    nq = nkv * GROUP

    def body(q_ref, kp_ref, kc_ref, vp_ref, vc_ref, sink_ref, do_ref,
             dq_ref, dkc_ref, dkp_ref, dvc_ref, dvp_ref, dsink_ref):
        n = pl.program_id(0)
        valid = _attn_mask(n)
        qv, dov = q_ref[...], do_ref[...]
        kp, kc, vp, vc = kp_ref[...], kc_ref[...], vp_ref[...], vc_ref[...]
        dqs, dks, dvs, dsinks = [], [], [], []
        for h in range(nkv):
            hs = slice(h * HEAD_DIM, (h + 1) * HEAD_DIM)
            k2 = jnp.concatenate([kp[:, hs], kc[:, hs]], axis=0)
            v2 = jnp.concatenate([vp[:, hs], vc[:, hs]], axis=0)
            q4 = _stack_heads(qv, h)
            dob = _stack_heads(dov, h).astype(BF16)
            pn, psink = _softmax_with_sink(q4, k2, sink_ref, h, valid)
            dpn = lax.dot_general(v2, dob, NT, preferred_element_type=F32)
            dvs.append(jnp.dot(pn.astype(BF16), dob, preferred_element_type=F32))
            delta = jnp.sum(pn * dpn, axis=0, keepdims=True)
            ds = (pn * (dpn - delta) * ATTN_SCALE).astype(BF16)
            dks.append(jnp.dot(ds, q4, preferred_element_type=F32))
            dq4 = lax.dot_general(ds, k2, TN, preferred_element_type=F32)
            dsink4 = -psink * delta
            for g in range(GROUP):
                dqs.append(dq4[g * BLOCK:(g + 1) * BLOCK])
                dsinks.append(jnp.broadcast_to(jnp.sum(dsink4[:, g * BLOCK:(g + 1) * BLOCK], axis=1, keepdims=True), (1, LANES)))
        dq_ref[...] = jnp.concatenate(dqs, axis=-1)
        dkp_ref[...] = jnp.concatenate([d[:BLOCK] for d in dks], axis=-1)
        dkc_ref[...] = jnp.concatenate([d[BLOCK:] for d in dks], axis=-1)
        dvp_ref[...] = jnp.concatenate([d[:BLOCK] for d in dvs], axis=-1)
        dvc_ref[...] = jnp.concatenate([d[BLOCK:] for d in dvs], axis=-1)

        @pl.when(n == 0)
        def _():
            dsink_ref[...] = jnp.zeros_like(dsink_ref)

        dsink_ref[...] += jnp.concatenate(dsinks, axis=0)

    cur = lambda n: (n, 0)
    prev = lambda n: (jnp.maximum(n - 1, 0), 0)
    kspec = pl.BlockSpec((BLOCK, kw), cur)
    return pl.pallas_call(
        body, name="attn_bwd", grid=(t // BLOCK,),
        in_specs=[pl.BlockSpec((BLOCK, cw), cur),
                  pl.BlockSpec((BLOCK, kw), prev), kspec,
                  pl.BlockSpec((BLOCK, kw), prev), kspec,
                  pl.BlockSpec(sink_rows.shape, lambda n: (0, 0)),
                  pl.BlockSpec((BLOCK, cw), cur)],
        out_specs=[pl.BlockSpec((BLOCK, cw), cur), kspec, kspec, kspec, kspec,
                   pl.BlockSpec((nq, LANES), lambda n: (0, 0))],
        out_shape=[_sds((t, cw), F32)] + [_sds((t, kw), F32)] * 4 + [_sds((nq, LANES), F32)],
        compiler_params=_params(("arbitrary",), BLOCK * (cw + 4 * kw) * 2 + 2 * BLOCK * cw * 4 + 4 * BLOCK * kw * 4, 12 << 20),
    )(qn, kn, kn, vb, vb, sink_rows, do)


def _mix_out(ca, o, woc, woa, proj):
    t, cw = ca.shape
    nb = woc.shape[2]
    d = N_DEV * nb
    tm = min(t, 1024)
    ga0 = (3 * cw + cw + 2 * (cw // 4)) // nb

    def body(ca_ref, o_ref, woc_ref, woa_ref, ga_ref, gb_ref, m_ref, ya_ref, yb_ref):
        ya = jnp.dot(ca_ref[...], woc_ref[...], preferred_element_type=F32)
        yb = jnp.dot(o_ref[...], woa_ref[...], preferred_element_type=F32)
        ya_ref[...] = ya.astype(BF16)
        yb_ref[...] = yb.astype(BF16)
        m_ref[...] = (_sigmoid(ga_ref[...].astype(F32)) * ya + _sigmoid(gb_ref[...].astype(F32)) * yb).astype(BF16)

    act = pl.BlockSpec((tm, cw), lambda i, j: (i, 0))
    wsp = pl.BlockSpec((None, cw, nb), lambda i, j: (j, 0, 0))
    osp = pl.BlockSpec((tm, nb), lambda i, j: (i, j))
    blocks = 2 * tm * cw * 2 + 2 * cw * nb * 2 + 2 * tm * nb * 4 + 3 * tm * nb * 2
    return pl.pallas_call(
        body, name="mix_out", grid=(t // tm, N_DEV),
        in_specs=[act, act, wsp, wsp,
                  pl.BlockSpec((tm, nb), lambda i, j: (i, ga0 + j)),
                  pl.BlockSpec((tm, nb), lambda i, j: (i, ga0 + N_DEV + j))],
        out_specs=[osp, osp, osp],
        out_shape=[_sds((t, d), BF16)] * 3,
        compiler_params=_params(("parallel", "parallel"), blocks, 6 * tm * nb * 4),
    )(ca, o, woc, woa, proj, proj)


def _mix_residual(merged, wo, x):
    t, d = x.shape
    tm = min(t, 512)

    def epilogue(acc, ins, outs):
        outs[0][...] = ins[2][...] + acc

    row = pl.BlockSpec((tm, d), lambda i, j, k: (i, 0))
    return _fused("mix_residual", (t // tm, 1, 1),
                  [(merged, row), (wo, pl.BlockSpec((d, d), lambda i, j, k: (0, 0))), (x, row)],
                  [(_sds((t, d), F32), row)], [(0, 1, NN)], epilogue, temp_bytes=2 * tm * d * 4)[0]


def _mix_bwd_gates(dx, wo, ya, yb, proj, cw):
    t, d = dx.shape
    tm = min(t, 1024)
    tn = min(d, 512)
    ga0 = (4 * cw + 2 * (cw // 4)) // tn

    def epilogue(acc, ins, outs):
        sa, sb = _sigmoid(ins[4][...].astype(F32)), _sigmoid(ins[5][...].astype(F32))
        outs[0][...] = (acc * sa).astype(BF16)
        outs[1][...] = (acc * sb).astype(BF16)
        outs[2][0] = (acc * ins[2][...].astype(F32) * sa * (1.0 - sa)).astype(BF16)
        outs[2][1] = (acc * ins[3][...].astype(F32) * sb * (1.0 - sb)).astype(BF16)

    blk = pl.BlockSpec((tm, tn), lambda i, j, k: (i, j))
    return _fused("mix_bwd_gates", (t // tm, d // tn, 1),
                  [(dx, pl.BlockSpec((tm, d), lambda i, j, k: (i, 0))),
                   (wo, pl.BlockSpec((tn, d), lambda i, j, k: (j, 0))),
                   (ya, blk), (yb, blk),
                   (proj, pl.BlockSpec((tm, tn), lambda i, j, k: (i, ga0 + j))),
                   (proj, pl.BlockSpec((tm, tn), lambda i, j, k: (i, ga0 + d // tn + j)))],
                  [(_sds((t, d), BF16), blk), (_sds((t, d), BF16), blk),
                   (_sds((2, t, d), BF16), pl.BlockSpec((2, tm, tn), lambda i, j, k: (0, i, j)))],
                  [(0, 1, NT)], epilogue, temp_bytes=8 * tm * tn * 4)


def _tn_matmul(name, a, b, tm, out_dtype=BF16):
    t, m = a.shape
    n = b.shape[1]

    def epilogue(acc, ins, outs):
        outs[0][...] = acc.astype(out_dtype)

    return _fused(name, (m // tm, 1, 1),
                  [(a, pl.BlockSpec((t, tm), lambda i, j, k: (0, i))),
                   (b, pl.BlockSpec((t, n), lambda i, j, k: (0, 0)))],
                  [(_sds((m, n), out_dtype), pl.BlockSpec((tm, n), lambda i, j, k: (i, 0)))],
                  [(0, 1, TN)], epilogue, temp_bytes=2 * tm * n * 4)[0]


def _out_proj_bwd_act(dya, dyb, woc, woa, deps=()):
    t, d = dya.shape
    kdim, nb = woc.shape[1], woc.shape[2]
    tm = min(t, 512)

    def body(dya_ref, dyb_ref, woc_ref, woa_ref, *rest):
        for dy_ref, w_ref, o_ref in ((dya_ref, woc_ref, rest[-2]), (dyb_ref, woa_ref, rest[-1])):
            total = None
            for j in range(N_DEV):
                part = lax.dot_general(dy_ref[:, j * nb:(j + 1) * nb], w_ref[j], NT, preferred_element_type=F32)
                total = part if total is None else total + part
            o_ref[...] = total

    row = pl.BlockSpec((tm, d), lambda i: (i, 0))
    wsp = pl.BlockSpec((N_DEV, kdim, nb), lambda i: (0, 0, 0))
    osp = pl.BlockSpec((tm, kdim), lambda i: (i, 0))
    blocks = 2 * tm * d * 2 + 2 * N_DEV * kdim * nb * 2 + 2 * tm * kdim * 4
    return pl.pallas_call(
        body, name="mix_bwd_dca_do", grid=(t // tm,),
        in_specs=[row, row, wsp, wsp] + [_ANY] * len(deps), out_specs=[osp, osp],
        out_shape=[_sds((t, kdim), F32)] * 2,
        compiler_params=_params(("parallel",), blocks, 4 * tm * kdim * 4),
    )(dya, dyb, woc, woa, *deps)


def _out_proj_bwd_w(ca, o, dya, dyb, nb):
    t, kdim = ca.shape

    def body(ca_ref, o_ref, dya_ref, dyb_ref, dwoc_ref, dwoa_ref):
        dwoc_ref[...] = lax.dot_general(ca_ref[...], dya_ref[...], TN, preferred_element_type=F32).astype(BF16)
        dwoa_ref[...] = lax.dot_general(o_ref[...], dyb_ref[...], TN, preferred_element_type=F32).astype(BF16)

    act = pl.BlockSpec((t, kdim), lambda j: (0, 0))
    col = pl.BlockSpec((t, nb), lambda j: (0, j))
    osp = pl.BlockSpec((None, kdim, nb), lambda j: (j, 0, 0))
    blocks = 2 * t * kdim * 2 + 2 * t * nb * 2 + 2 * kdim * nb * 2
    return pl.pallas_call(
        body, name="mix_bwd_dwoc_dwoa", grid=(N_DEV,),
        in_specs=[act, act, col, col], out_specs=[osp, osp],
        out_shape=[_sds((N_DEV, kdim, nb), BF16)] * 2,
        compiler_params=_params(("parallel",), blocks, 4 * kdim * nb * 4),
    )(ca, o, dya, dyb)


def _proj_bwd_act(dproj, w_in, deps=()):
    t, n = dproj.shape
    d, nb = w_in.shape[2], w_in.shape[3]
    tm = min(t, 512)

    def epilogue(acc, ins, outs):
        outs[0][...] = acc

    def products(ins):
        return (lax.dot_general(ins[0][:, 0:nb], ins[1][0], NT, preferred_element_type=F32)
                + lax.dot_general(ins[0][:, nb:2 * nb], ins[1][1], NT, preferred_element_type=F32))

    return _fused("mix_bwd_dh", (t // tm, 1, 4),
                  [(dproj, pl.BlockSpec((tm, 2 * nb), lambda i, j, k: (i, k))),
                   (w_in, pl.BlockSpec((None, 2, d, nb), lambda i, j, k: (k, 0, 0, 0)))],
                  [(_sds((t, d), F32), pl.BlockSpec((tm, d), lambda i, j, k: (i, 0)))],
                  products, epilogue, nk=4, acc_shape=(tm, d), temp_bytes=tm * d * 4, deps=deps)[0]


def _proj_bwd_w(h, dproj):
    t, d = h.shape
    nb = dproj.shape[1] // N_DEV
    tm = min(d, 512)

    def body(h_ref, dp_ref, o_ref):
        hv = h_ref[...]
        o_ref[0] = lax.dot_general(hv, dp_ref[:, 0:nb], TN, preferred_element_type=F32).astype(BF16)
        o_ref[1] = lax.dot_general(hv, dp_ref[:, nb:2 * nb], TN, preferred_element_type=F32).astype(BF16)

    blocks = t * tm * 2 + t * 2 * nb * 2 + 2 * tm * nb * 2
    return pl.pallas_call(
        body, name="mix_bwd_dwin", grid=(4, d // tm),
        in_specs=[pl.BlockSpec((t, tm), lambda j, i: (0, i)),
                  pl.BlockSpec((t, 2 * nb), lambda j, i: (0, j))],
        out_specs=pl.BlockSpec((None, 2, tm, nb), lambda j, i: (j, 0, i, 0)),
        out_shape=_sds((4, 2, d, nb), BF16),
        compiler_params=_params(("parallel", "parallel"), blocks, 4 * tm * nb * 4),
    )(h, dproj)


def _adamw_math(w, g, m, v):
    m = ADAM_B1 * m + (1.0 - ADAM_B1) * g
    v = ADAM_B2 * v + (1.0 - ADAM_B2) * (g * g)
    m_hat = m / (1.0 - ADAM_B1 ** ADAM_STEP)
    v_hat = v / (1.0 - ADAM_B2 ** ADAM_STEP)
    delta = -ADAM_LR * (m_hat / (jnp.sqrt(v_hat) + ADAM_EPS) + ADAM_WD * w)
    return delta, m, v


def _adamw(name, parts, w, m, v, tr):
    r, c = w.shape

    def body(p_ref, w_ref, m_ref, v_ref, g_out, d_out, m_out, v_out):
        g = p_ref[0].astype(F32)
        for s in range(1, N_DEV):
            g = g + p_ref[s].astype(F32)
        delta, mn, vn = _adamw_math(w_ref[...], g, m_ref[...], v_ref[...])
        g_out[...] = g
        d_out[...] = delta
        m_out[...] = mn
        v_out[...] = vn

    blk = pl.BlockSpec((tr, c), lambda i: (i, 0))
    blocks = N_DEV * tr * c * parts.dtype.itemsize + 7 * tr * c * 4
    return pl.pallas_call(
        body, name=name, grid=(r // tr,),
        in_specs=[pl.BlockSpec((N_DEV, tr, c), lambda i: (0, i, 0)), blk, blk, blk],
        out_specs=[blk] * 4, out_shape=[_sds((r, c), F32)] * 4,
        compiler_params=_params(("parallel",), blocks, 6 * tr * c * 4),
    )(parts, w, m, v)


def _chip_sum(sums_ref):
    g = sums_ref[0].astype(F32)
    for k in range(1, 4):
        g = g + sums_ref[k].astype(F32)
    return g


def _adamw_chips(name, sums, w, m, v, tr, deps=(), row0=0, into=None):
    r, c = w.shape
    rs = sums.shape[1]
    i0 = row0 // tr
    n_pass = len(deps) + (4 if into is not None else 0)

    def body(sums_ref, w_ref, m_ref, v_ref, *rest):
        g_out, d_out, m_out, v_out = rest[n_pass:]
        g = _chip_sum(sums_ref)
        delta, mn, vn = _adamw_math(w_ref[...], g, m_ref[...], v_ref[...])
        g_out[...] = g
        d_out[...] = delta
        m_out[...] = mn
        v_out[...] = vn

    blk = pl.BlockSpec((tr, c), lambda i: (i0 + i, 0))
    blocks = 4 * tr * c * 2 + 7 * tr * c * 4
    passed = list(deps) + (list(into) if into is not None else [])
    aliases = {4 + len(deps) + q: q for q in range(4)} if into is not None else {}
    return pl.pallas_call(
        body, name=name, grid=(rs // tr,),
        in_specs=[pl.BlockSpec((4, tr, c), lambda i: (0, i, 0)), blk, blk, blk] + [_ANY] * n_pass,
        out_specs=[blk] * 4, out_shape=[_sds((r, c), F32)] * 4,
        input_output_aliases=aliases,
        compiler_params=_params(("parallel",), blocks, 6 * tr * c * 4),
    )(sums, w, m, v, *passed)


def _adamw_side(contrib, w, m, v, n_tiles, step_of):
    r, c = w.shape
    tr = r // n_tiles
    assert tr * n_tiles == r and tr % 16 == 0, (r, n_tiles)

    def tile(i, j, k):
        return jnp.minimum(step_of(i, j, k), n_tiles - 1)

    blk = pl.BlockSpec((tr, c), lambda i, j, k: (tile(i, j, k), 0))
    ins = [(contrib, pl.BlockSpec((4, tr, c), lambda i, j, k: (0, tile(i, j, k), 0))), (w, blk), (m, blk), (v, blk)]
    outs = [(_sds((r, c), F32), blk)] * 4

    def fn(in_refs, out_refs):
        @pl.when(step_of(pl.program_id(0), pl.program_id(1), pl.program_id(2)) < n_tiles)
        def _():
            g = _chip_sum(in_refs[0])
            delta, mn, vn = _adamw_math(in_refs[1][...], g, in_refs[2][...], in_refs[3][...])
            out_refs[0][...] = g
            out_refs[1][...] = delta
            out_refs[2][...] = mn
            out_refs[3][...] = vn

    return ins, outs, fn


def _rope_tables(t):
    half = ROT_DIM // 2
    inv_freq = 1.0 / (ROPE_THETA ** (jnp.arange(0, ROT_DIM, 2, dtype=F32) / ROT_DIM))
    ang = jnp.arange(t, dtype=F32)[:, None] * inv_freq[None, :]
    cos, sin = jnp.cos(ang), jnp.sin(ang)
    ones = jnp.ones((t, HEAD_DIM - ROT_DIM), F32)
    zeros = jnp.zeros((t, HEAD_DIM - half), F32)
    c = jnp.concatenate([cos, cos, ones], axis=1)
    sa = jnp.concatenate([-sin, zeros], axis=1)
    sb = jnp.concatenate([jnp.zeros((t, half), F32), sin, jnp.zeros((t, HEAD_DIM - ROT_DIM), F32)], axis=1)
    return tuple(jnp.tile(a, (1, LANES // HEAD_DIM)) for a in (c, sa, sb))


def _pad_rows(a, rows=8):
    return jnp.pad(a, ((0, rows - a.shape[0]), (0, 0)))


def kernel(x, g_ffn1, w_gu1, w_down1, g_mix, w_in, conv_w, q_norm_g, k_norm_g, sinks, w_out_conv, w_out_attn, w_o, g_ffn2, w_gu2, w_down2, loss_target, m_g_ffn1, m_w_gu1, m_w_down1, m_g_mix, m_w_in, m_conv_w, m_q_norm_g, m_k_norm_g, m_sinks, m_w_out_conv, m_w_out_attn, m_w_o, m_g_ffn2, m_w_gu2, m_w_down2, v_g_ffn1, v_w_gu1, v_w_down1, v_g_mix, v_w_in, v_conv_w, v_q_norm_g, v_k_norm_g, v_sinks, v_w_out_conv, v_w_out_attn, v_w_o, v_g_ffn2, v_w_gu2, v_w_down2):
    t, d = x.shape[1], x.shape[2]
    cw = d // 2
    kw = cw // GROUP
    nq = cw // HEAD_DIM
    xs, target = x.reshape(t, d), loss_target.reshape(t, d)
    me = 4 * lax.axis_index("x") + 2 * lax.axis_index("y") + lax.axis_index("c")

    big = {"w_gu1": w_gu1, "w_down1": w_down1, "w_in": w_in, "w_out_conv": w_out_conv,
           "w_out_attn": w_out_attn, "w_o": w_o, "w_gu2": w_gu2, "w_down2": w_down2}
    big_m = {"w_gu1": m_w_gu1, "w_down1": m_w_down1, "w_in": m_w_in, "w_out_conv": m_w_out_conv,
             "w_out_attn": m_w_out_attn, "w_o": m_w_o, "w_gu2": m_w_gu2, "w_down2": m_w_down2}
    big_v = {"w_gu1": v_w_gu1, "w_down1": v_w_down1, "w_in": v_w_in, "w_out_conv": v_w_out_conv,
             "w_out_attn": v_w_out_attn, "w_o": v_w_o, "w_gu2": v_w_gu2, "w_down2": v_w_down2}
    names = list(big)

    tiles = {"w_gu1": 256, "w_gu2": 256, "w_in": 256, "w_down1": 176, "w_down2": 176,
             "w_out_conv": 1024, "w_out_attn": 1024, "w_o": 128}

    def row_tile(n):
        r = big[n].shape[1]
        return tiles[n] if r % tiles[n] == 0 else r

    rs_shape = {n: big[n].shape[1:] for n in names}
    half = rs_shape["w_gu1"][0] // 2
    rs_shape["w_gu1_lo"] = rs_shape["w_gu1_hi"] = (half, rs_shape["w_gu1"][1])

    def add_tile(n):
        r, c = rs_shape[n]
        while r * c * 2 > (3 << 20) and r % 32 == 0:
            r //= 2
        return r

    me_arr = me.astype(jnp.int32).reshape(1)
    sources = [(n, big[n][0], BF16, row_tile(n)) for n in names] + [("conv_w", _pad_rows(conv_w[0]), F32, 8)]
    issue_order = [0, 1, 2, 8, 3, 4, 5, 6, 7]
    first = _place_shard("place_" + names[0], sources[0][1], BF16, me_arr, sources[0][3])
    started = [_gather_start("gather_start_first", [first])]
    early = {2: (big_m["w_in"][0], big_v["w_in"][0])}
    rest = [_place_shard("place_" + sources[i][0], sources[i][1], sources[i][2], me_arr, sources[i][3],
                         deps=(started[0][3],) + early.get(i, ())) for i in issue_order[1:]]
    started.append(_gather_start("gather_start_rest", rest))
    where = {0: (0, 0)}
    where.update({i: (1, p) for p, i in enumerate(issue_order[1:])})

    def fetch(tag, idxs, after, forward=True):
        call = where[idxs[0]][0]
        send, recv, stacks, _ = started[call]
        positions = [where[i][1] for i in idxs]
        got = _gather_wait("gather_wait_" + tag, positions, send, recv, [stacks[p] for p in positions], after)
        return _forward_to_sibling("gather_forward_" + tag, got) if forward else got

    rope_tabs = _rope_tables(t)
    gq = jnp.tile(q_norm_g, (1, nq))
    gk = jnp.tile(k_norm_g, (1, nq // GROUP))
    sink_rows = jnp.broadcast_to(sinks[0][:, None], (nq, LANES))

    wts = {}
    h1 = _rms_fwd("ffn1_norm", xs, g_ffn1)
    wts["w_gu1"], = fetch("gu1", [0], started[1][3])
    gu1, a1 = _ffn_up("ffn1_up", h1, wts["w_gu1"])
    wts["w_down1"], = fetch("down1", [1], a1)
    wd1 = wts["w_down1"].reshape(-1, d)
    x1 = _ffn_down("ffn1_down", a1, wd1, xs)
    h2 = _rms_fwd("mix_norm", x1, g_mix)
    wts["w_in"], conv_land = fetch("in", [2, 8], h2)
    w_in_full = wts["w_in"].reshape(4, 2, d, -1)
    conv_full = jnp.transpose(conv_land, (1, 0, 2)).reshape(8, cw)
    proj = _proj(h2, w_in_full)
    ca = _conv_fwd(proj, conv_full)
    qn, kn, vb = _qk_prep(proj, gq, gk, rope_tabs, cw, kw)
    o = _attn_fwd(qn, kn, vb, sink_rows)
    wts["w_out_conv"], wts["w_out_attn"] = fetch("out", [3, 4], o)
    merged, ya, yb = _mix_out(ca, o, wts["w_out_conv"], wts["w_out_attn"], proj)
    wts["w_o"], = fetch("o", [5], merged)
    wo = wts["w_o"].reshape(d, d)
    x2 = _mix_residual(merged, wo, x1)
    h3 = _rms_fwd("ffn2_norm", x2, g_ffn2)
    mine = lax.axis_index("c").astype(jnp.int32).reshape(1)
    got = fetch("gu2", [6], h3, forward=False)
    fsend, frecv, got = _forward_start("gather_forward_start_gu2", got)
    part = _ffn_up("ffn2_up_mine", h3, got[0], parity=mine)
    wts["w_gu2"], = _forward_wait("gather_forward_wait_gu2", fsend, frecv, got, part[1])
    gu2, a2 = _ffn_up("ffn2_up_sibling", h3, wts["w_gu2"], parity=1 - mine, into=part)
    wts["w_down2"], = fetch("down2", [7], a2)
    wd2 = wts["w_down2"].reshape(-1, d)
    dy, sq, dy_bf = _ffn_down("ffn2_down", a2, wd2, x2, target=target)
    loss_part = sq[0:1, 0:1] * (0.5 / d)

    place = jnp.stack([lax.axis_index("c"), 2 * lax.axis_index("x") + lax.axis_index("y")]).astype(jnp.int32)
    def pair_start(tag, group, grads, deps=()):
        stacks = [grads[n].reshape((4, 2) + rs_shape[n]) for n in group]
        lands = [lax.empty((4,) + rs_shape[n], BF16) for n in group]
        return _pair_start("rs_pair_start_" + tag, stacks, lands, deps)

    def chip_start(tag, group, pending, after):
        send, recv, stacks, lands, _ = pending
        stacks, lands = _pair_wait("rs_pair_wait_" + tag, send, recv, stacks, lands, after)
        added = [_pair_add("rs_pair_add_" + n, st, ld, place, add_tile(n)) for n, st, ld in zip(group, stacks, lands)]
        return _chip_start("rs_chip_start_" + tag, [a[0] for a in added], [a[1] for a in added])

    group_a, group_b, group_c = ["w_down2", "w_gu2"], ["w_o", "w_out_conv", "w_out_attn"], ["w_in"]
    group_d, group_e, group_f = ["w_down1"], ["w_gu1_lo"], ["w_gu1_hi"]
    g = {}
    dgu2, a2 = _ffn_bwd_act("ffn2_bwd_act", dy_bf, wd2, gu2)
    pend_s = _sibling_start("rs_act_start_gu2", [dgu2, h3])
    g["w_down2"], = _ffn_bwd_dwd("ffn2_bwd_dwd", a2, dy_bf, deps=(pend_s[4],))
    pend_a = pair_start("a", ["w_down2"], g)
    dh3, = _ffn_bwd_dh("ffn2_bwd_dh", pend_s[2][0], wts["w_gu2"], deps=(pend_a[4],))
    (dgu2, h3), (dgu2_sib, h3_sib) = _sibling_wait("rs_act_wait_gu2", pend_s[0], pend_s[1], pend_s[2], pend_s[3], dh3)
    sums_gu2, slots_gu2 = _ffn_bwd_dwgu_pair("ffn2_bwd_dwgu", h3, h3_sib, dgu2, dgu2_sib, place)
    stacks_a, lands_a = _pair_wait("rs_pair_wait_a", pend_a[0], pend_a[1], pend_a[2], pend_a[3], sums_gu2)
    added_a = _pair_add("rs_pair_add_w_down2", stacks_a[0], lands_a[0], place, add_tile("w_down2"))
    ring_a = _chip_start("rs_chip_start_a", [added_a[0], sums_gu2], [added_a[1], slots_gu2])
    dx2, dg_ffn2, dx2_bf = _rms_bwd("ffn2_bwd_rms", x2, g_ffn2, dh3, dy, deps=(ring_a[4],), with_bf16=True)

    dya, dyb, dgates = _mix_bwd_gates(dx2_bf, wo, ya, yb, proj, cw)
    g["w_o"] = _tn_matmul("mix_bwd_dwo", merged, dx2_bf, min(d, 512))
    g["w_out_conv"], g["w_out_attn"] = _out_proj_bwd_w(ca, o, dya, dyb, d // N_DEV)
    pend_b = pair_start("b", group_b, g)
    dca, do = _out_proj_bwd_act(dya, dyb, wts["w_out_conv"], wts["w_out_attn"], deps=(pend_b[4],))
    ring_b = chip_start("b", group_b, pend_b, do)
    d3, dconv_w = _conv_bwd(proj, conv_full, dca, deps=(ring_b[4],))
    dq, dkc, dkp, dvc, dvp, dsink = _attn_bwd(qn, kn, vb, sink_rows, do)
    dqkv, dgq, dgk = _qk_prep_bwd(proj, gq, gk, rope_tabs, dq, dkc, dkp, dvc, dvp, cw, kw)
    dproj = jnp.concatenate([d3[0], d3[1], d3[2], dqkv, dgates[0], dgates[1]], axis=1)
    g["w_in"] = _proj_bwd_w(h2, dproj)
    pend_c = pair_start("c", group_c, g)
    dh2 = _proj_bwd_act(dproj, w_in_full, deps=(pend_c[4],))
    ring_c = chip_start("c", group_c, pend_c, dh2)
    dx1, dg_mix, dx1_bf = _rms_bwd("mix_bwd_rms", x1, g_mix, dh2, dx2, deps=(ring_c[4],), with_bf16=True)

    big_out = {}
    arrived = {}

    def wait_group(tag, group, ring, after):
        send, recv, parts, lands2, _ = ring
        parts, lands2 = _chip_wait("rs_chip_wait_" + tag, send, recv, parts, lands2, after)
        arrived.update(dict(zip(group, lands2)))

    def update(n, after):
        res = _adamw_chips("adamw_" + n, arrived[n], big[n][0], big_m[n][0], big_v[n][0], row_tile(n), deps=(after,))
        big_out[n] = [a[None] for a in res]
        return res[0]

    def update_beside(n, n_tiles, step_of):
        return _adamw_side(arrived[n], big[n][0], big_m[n][0], big_v[n][0], n_tiles, step_of)

    def keep(n, res):
        big_out[n] = [a[None] for a in res]

    dgu1, a1 = _ffn_bwd_act("ffn1_bwd_act", dx1_bf, wd1, gu1)
    wait_group("a", group_a, ring_a, a1)
    g["w_down1"], *res = _ffn_bwd_dwd("ffn1_bwd_dwd", a1, dx1_bf,
                                       side=update_beside("w_down2", 11, lambda i, j, k: i * 4 + j))
    keep("w_down2", res)
    pend_d = pair_start("d", group_d, g)
    g["w_gu1_lo"], *res = _ffn_bwd_dwgu("ffn1_bwd_dwgu_lo", h1, dgu1, deps=(pend_d[4],), rows=(0, half),
                                         side=update_beside("w_gu2", 16, lambda i, j, k: i * 2 + j))
    keep("w_gu2", res)
    ring_d = chip_start("d", group_d, pend_d, g["w_gu1_lo"])
    pend_e = pair_start("e", group_e, g, deps=(ring_d[4],))
    wait_group("c", group_c, ring_c, pend_e[4])
    g["w_gu1_hi"], *res = _ffn_bwd_dwgu("ffn1_bwd_dwgu_hi", h1, dgu1, rows=(half, half),
                                         side=update_beside("w_in", 16, lambda i, j, k: i * 2 + j))
    keep("w_in", res)
    ring_e = chip_start("e", group_e, pend_e, g["w_gu1_hi"])
    pend_f = pair_start("f", group_f, g, deps=(ring_e[4],))
    wait_group("b", group_b, ring_b, pend_f[4])
    after = pend_f[4]
    for n in group_b:
        after = update(n, after)
    ring_f = chip_start("f", group_f, pend_f, after)
    wait_group("d", group_d, ring_d, ring_f[4])
    dh1, *res = _ffn_bwd_dh("ffn1_bwd_dh", dgu1, wts["w_gu1"],
                             side=update_beside("w_down1", 11, lambda i, j, k: i * 4 + k))
    keep("w_down1", res)
    grad_x, dg_ffn1 = _rms_bwd("ffn1_bwd_rms", xs, g_ffn1, dh1, dx1)
    after = grad_x
    n = "w_gu1"
    wait_group("e", group_e, ring_e, after)
    res = _adamw_chips("adamw_w_gu1_lo", arrived["w_gu1_lo"], big[n][0], big_m[n][0], big_v[n][0], row_tile(n), deps=(after,))
    wait_group("f", group_f, ring_f, res[0])
    res = _adamw_chips("adamw_w_gu1_hi", arrived["w_gu1_hi"], big[n][0], big_m[n][0], big_v[n][0], row_tile(n),
                       row0=half, into=res)
    keep(n, res)
    after = res[0]

    small = {"g_ffn1": dg_ffn1[0:1], "g_mix": dg_mix[0:1], "g_ffn2": dg_ffn2[0:1],
             "q_norm_g": dgq[0:1, :HEAD_DIM], "k_norm_g": dgk[0:1, :HEAD_DIM], "sinks": dsink[:, 0][None],
             "conv_w": dconv_w[0:CONV_K].reshape(1, -1)}
    small_w = {"g_ffn1": g_ffn1, "g_mix": g_mix, "g_ffn2": g_ffn2, "q_norm_g": q_norm_g, "k_norm_g": k_norm_g,
               "sinks": sinks, "conv_w": None}
    small_m = {"g_ffn1": m_g_ffn1, "g_mix": m_g_mix, "g_ffn2": m_g_ffn2, "q_norm_g": m_q_norm_g,
               "k_norm_g": m_k_norm_g, "sinks": m_sinks, "conv_w": m_conv_w}
    small_v = {"g_ffn1": v_g_ffn1, "g_mix": v_g_mix, "g_ffn2": v_g_ffn2, "q_norm_g": v_q_norm_g,
               "k_norm_g": v_k_norm_g, "sinks": v_sinks, "conv_w": v_conv_w}
    small["loss"] = loss_part
    small_w["loss"] = small_m["loss"] = jnp.zeros((1, 1), F32)
    small_v["loss"] = jnp.ones((1, 1), F32)
    snames = list(small)
    widths = [small[n].shape[1] for n in snames]
    total = sum(widths)
    rows = -(-total // LANES)
    rows = -(-rows // 8) * 8

    def pack(vals):
        flat = jnp.concatenate([v.reshape(1, -1) for v in vals], axis=1)
        return jnp.pad(flat, ((0, 0), (0, rows * LANES - total))).reshape(rows, LANES)

    csh = cw // N_DEV

    def place_conv(local, fill):
        full = jnp.full((CONV_K, cw), fill, F32)
        return lax.dynamic_update_slice(full, local, (0, me * csh)).reshape(1, -1)

    pw = pack([small_w[n] if n != "conv_w" else place_conv(conv_w[0], 0.0) for n in snames])
    pm = pack([small_m[n] if n != "conv_w" else place_conv(m_conv_w[0], 0.0) for n in snames])
    pv = pack([small_v[n] if n != "conv_w" else place_conv(v_conv_w[0], 1.0) for n in snames])
    parts = _all_gather_small("gather_small_grads", pack([small[n] for n in snames]), deps=(after,))
    sg, sd, sm, sv = [a.reshape(1, -1) for a in _adamw("adamw_small", parts, pw, pm, pv, rows)]

    def unpack(flat, n):
        off = sum(widths[:snames.index(n)])
        piece = flat[:, off:off + widths[snames.index(n)]]
        if n == "conv_w":
            piece = lax.dynamic_slice(piece.reshape(CONV_K, cw), (0, me * csh), (CONV_K, csh))[None]
        return piece

    order = ["g_ffn1", "w_gu1", "w_down1", "g_mix", "w_in", "conv_w", "q_norm_g", "k_norm_g", "sinks",
             "w_out_conv", "w_out_attn", "w_o", "g_ffn2", "w_gu2", "w_down2"]
    outs = [unpack(sg, "loss")[0, 0], grad_x[None]]
    for idx, flat in enumerate((sg, sd, sm, sv)):
        for n in order:
            outs.append(big_out[n][idx] if n in big_out else unpack(flat, n))
    return tuple(outs)
```
